```python
import math
import jax, jax.numpy as jnp
from jax import lax
import numpy as np


D_MODEL = 1024
BATCH = 8
SEQ = 4096
DEPTH = 2

SSM_GROUP = 16
N_GROUPS = D_MODEL // SSM_GROUP
SSM_STATE = 64
DT_MIN = 1e-3
DT_MAX = 1e-1
N_HEADS = 8
QK_NOPE = 128
QK_ROPE = 64
V_HEAD = 128
Q_LORA = 384
KV_LORA = 256
ROPE_THETA = 10000.0
Q_BLOCK = 128
SM_SCALE = (QK_NOPE + QK_ROPE) ** -0.5
NEG_INF = -1e30
D_FF = 4 * D_MODEL
N_A_LAYERS = DEPTH // 2
N_B_LAYERS = DEPTH - N_A_LAYERS
DN_ALPHA = (2 * DEPTH) ** 0.25
DN_BETA = (8 * DEPTH) ** -0.25
LN_EPS = 1e-5
RMS_EPS = 1e-6

kernel_name = 'yoco_s5_mla_sqrelu_deepnorm'


def layer_norm(x, g, b):
    xf = x.astype(jnp.float32)
    mu = jnp.mean(xf, axis=-1, keepdims=True)
    var = jnp.mean(jnp.square(xf - mu), axis=-1, keepdims=True)
    y = (xf - mu) * lax.rsqrt(var + LN_EPS) * g.astype(jnp.float32) + b.astype(jnp.float32)
    return y.astype(x.dtype)


def rms_norm(x, g):
    xf = x.astype(jnp.float32)
    y = xf * lax.rsqrt(jnp.mean(jnp.square(xf), axis=-1, keepdims=True) + RMS_EPS) * g.astype(jnp.float32)
    return y.astype(x.dtype)


def rope_tables(positions):
    half = QK_ROPE // 2
    inv_freq = ROPE_THETA ** (-jnp.arange(half, dtype=jnp.float32) / half)
    ang = positions.astype(jnp.float32)[..., None] * inv_freq
    return jnp.cos(ang), jnp.sin(ang)


def apply_rope(x, cos, sin):
    x1, x2 = jnp.split(x.astype(jnp.float32), 2, axis=-1)
    return jnp.concatenate([x1 * cos - x2 * sin, x1 * sin + x2 * cos], axis=-1).astype(x.dtype)


def _complex_linear_combine(left, right):
    ar_l, ai_l, hr_l, hi_l = left
    ar_r, ai_r, hr_r, hi_r = right
    return (ar_r * ar_l - ai_r * ai_l,
            ar_r * ai_l + ai_r * ar_l,
            ar_r * hr_l - ai_r * hi_l + hr_r,
            ar_r * hi_l + ai_r * hr_l + hi_r)


def s5_mixer(x, lam_re, lam_im, log_dt, b_re, b_im, c_re, c_im, d_skip, w_glu, w_out):
    f32 = jnp.float32
    bsz, seq, _ = x.shape
    u = x.astype(f32).reshape(bsz, seq, N_GROUPS, SSM_GROUP)
    lr = lam_re.astype(f32)
    li = lam_im.astype(f32)
    dt = jnp.exp(log_dt.astype(f32))[:, None]
    mag = jnp.exp(lr * dt)
    a_re = mag * jnp.cos(li * dt)
    a_im = mag * jnp.sin(li * dt)
    inv_den = 1.0 / (lr * lr + li * li)
    coef_re = ((a_re - 1.0) * lr + a_im * li) * inv_den
    coef_im = (a_im * lr - (a_re - 1.0) * li) * inv_den
    br = b_re.astype(f32)
    bi = b_im.astype(f32)
    bb_re = coef_re[..., None] * br - coef_im[..., None] * bi
    bb_im = coef_re[..., None] * bi + coef_im[..., None] * br
    bu_re = jnp.einsum('bsgc,gpc->bsgp', u, bb_re)
    bu_im = jnp.einsum('bsgc,gpc->bsgp', u, bb_im)
    shape_a = (1, seq, N_GROUPS, SSM_STATE)
    a_re_t = jnp.broadcast_to(a_re, shape_a)
    a_im_t = jnp.broadcast_to(a_im, shape_a)
    _, _, h_re, h_im = lax.associative_scan(
        _complex_linear_combine, (a_re_t, a_im_t, bu_re, bu_im), axis=1)
    y = (jnp.einsum('bsgp,gcp->bsgc', h_re, c_re.astype(f32))
         - jnp.einsum('bsgp,gcp->bsgc', h_im, c_im.astype(f32)))
    y = y + d_skip.astype(f32).reshape(N_GROUPS, SSM_GROUP) * u
    y = jax.nn.gelu(y.reshape(bsz, seq, D_MODEL)).astype(x.dtype)
    val, gate = jnp.split(y @ w_glu, 2, axis=-1)
    return (val * jax.nn.sigmoid(gate)) @ w_out


def mla_shared_kv(h, kv_w_a, kv_norm_g, kv_w_b, cos, sin):
    bsz, seq, _ = h.shape
    c_kv, k_rope = jnp.split(h @ kv_w_a, [KV_LORA], axis=-1)
    c_kv = rms_norm(c_kv, kv_norm_g)
    k_rope = apply_rope(k_rope, cos, sin)
    kv = (c_kv @ kv_w_b).reshape(bsz, seq, N_HEADS, QK_NOPE + V_HEAD)
    k_nope, v = jnp.split(kv, [QK_NOPE], axis=-1)
    return k_nope, k_rope, v


def mla_mixer(h, q_w_a, q_norm_g, q_w_b, w_o, k_nope, k_rope, v, cos, sin):
    bsz, seq, _ = h.shape
    c_q = rms_norm(h @ q_w_a, q_norm_g)
    q = (c_q @ q_w_b).reshape(bsz, seq, N_HEADS, QK_NOPE + QK_ROPE)
    q_nope, q_rope = jnp.split(q, [QK_NOPE], axis=-1)
    q_rope = apply_rope(q_rope, cos[:, :, None, :], sin[:, :, None, :])
    n_blocks = seq // Q_BLOCK

    def to_blocks(t):
        return t.reshape(bsz, n_blocks, Q_BLOCK, *t.shape[2:]).swapaxes(0, 1)

    key_pos = jnp.arange(seq)

    def attend_block(args):
        blk, qn, qr = args
        s = (jnp.einsum('bqhd,bkhd->bhqk', qn, k_nope, preferred_element_type=jnp.float32)
             + jnp.einsum('bqhr,bkr->bhqk', qr, k_rope, preferred_element_type=jnp.float32)) * SM_SCALE
        q_pos = blk * Q_BLOCK + jnp.arange(Q_BLOCK)
        s = jnp.where(key_pos[None, :] <= q_pos[:, None], s, NEG_INF)
        p = jax.nn.softmax(s, axis=-1).astype(v.dtype)
        return jnp.einsum('bhqk,bkhd->bqhd', p, v)

    o = lax.map(attend_block, (jnp.arange(n_blocks), to_blocks(q_nope), to_blocks(q_rope)))
    o = o.swapaxes(0, 1).reshape(bsz, seq, N_HEADS * V_HEAD)
    return o @ w_o


def sq_relu_mlp(h, w1, w2):
    return jnp.square(jax.nn.relu(h @ w1)) @ w2


def _fwd_setup_inputs(seed: int = 0) -> dict:
    key = jax.random.key(seed)
    k = jax.random.split(key, 24)
    f32 = jnp.float32

    def nrm(i, shape, scale):
        return jax.random.normal(k[i], shape, f32) * scale

    n_idx = jnp.arange(SSM_STATE, dtype=f32)
    return {
        'x': nrm(0, (BATCH, SEQ, D_MODEL), 1.0),
        'positions': jnp.broadcast_to(jnp.arange(SEQ, dtype=jnp.int32), (BATCH, SEQ)),
        'ln_mix_g': 1.0 + nrm(1, (DEPTH, D_MODEL), 0.02),
        'ln_mix_b': nrm(2, (DEPTH, D_MODEL), 0.02),
        'ln_ffn_g': 1.0 + nrm(3, (DEPTH, D_MODEL), 0.02),
        'ln_ffn_b': nrm(4, (DEPTH, D_MODEL), 0.02),
        'w_ff1': nrm(5, (DEPTH, D_MODEL, D_FF), D_MODEL ** -0.5),
        'w_ff2': nrm(6, (DEPTH, D_FF, D_MODEL), D_FF ** -0.5 * DN_BETA),
        'ssm_lam_re': -0.5 + nrm(7, (N_A_LAYERS, N_GROUPS, SSM_STATE), 0.01),
        'ssm_lam_im': math.pi * n_idx + nrm(8, (N_A_LAYERS, N_GROUPS, SSM_STATE), 0.01),
        'ssm_log_dt': jax.random.uniform(k[9], (N_A_LAYERS, N_GROUPS), f32, math.log(DT_MIN), math.log(DT_MAX)),
        'ssm_b_re': nrm(10, (N_A_LAYERS, N_GROUPS, SSM_STATE, SSM_GROUP), (2 * SSM_GROUP) ** -0.5),
        'ssm_b_im': nrm(11, (N_A_LAYERS, N_GROUPS, SSM_STATE, SSM_GROUP), (2 * SSM_GROUP) ** -0.5),
        'ssm_c_re': nrm(12, (N_A_LAYERS, N_GROUPS, SSM_GROUP, SSM_STATE), SSM_STATE ** -0.5),
        'ssm_c_im': nrm(13, (N_A_LAYERS, N_GROUPS, SSM_GROUP, SSM_STATE), SSM_STATE ** -0.5),
        'ssm_d': nrm(14, (N_A_LAYERS, D_MODEL), 1.0),
        'ssm_w_glu': nrm(15, (N_A_LAYERS, D_MODEL, 2 * D_MODEL), D_MODEL ** -0.5),
        'ssm_w_out': nrm(16, (N_A_LAYERS, D_MODEL, D_MODEL), D_MODEL ** -0.5 * DN_BETA),
        'kv_w_a': nrm(17, (D_MODEL, KV_LORA + QK_ROPE), D_MODEL ** -0.5),
        'kv_norm_g': 1.0 + nrm(18, (KV_LORA,), 0.02),
        'kv_w_b': nrm(19, (KV_LORA, N_HEADS * (QK_NOPE + V_HEAD)), KV_LORA ** -0.5),
        'q_w_a': nrm(20, (N_B_LAYERS, D_MODEL, Q_LORA), D_MODEL ** -0.5),
        'q_norm_g': 1.0 + nrm(21, (N_B_LAYERS, Q_LORA), 0.02),
        'q_w_b': nrm(22, (N_B_LAYERS, Q_LORA, N_HEADS * (QK_NOPE + QK_ROPE)), Q_LORA ** -0.5),
        'attn_w_o': nrm(23, (N_B_LAYERS, N_HEADS * V_HEAD, D_MODEL), (N_HEADS * V_HEAD) ** -0.5 * DN_BETA),
    }


def _fwd_reference(x, positions, ln_mix_g, ln_mix_b, ln_ffn_g, ln_ffn_b, w_ff1, w_ff2,
              ssm_lam_re, ssm_lam_im, ssm_log_dt, ssm_b_re, ssm_b_im, ssm_c_re, ssm_c_im,
              ssm_d, ssm_w_glu, ssm_w_out, kv_w_a, kv_norm_g, kv_w_b,
              q_w_a, q_norm_g, q_w_b, attn_w_o):
    cos, sin = rope_tables(positions)
    h = x
    k_nope = k_rope = v = None
    for layer in range(DEPTH):
        if layer < N_A_LAYERS:
            i = layer
            mix = s5_mixer(h, ssm_lam_re[i], ssm_lam_im[i], ssm_log_dt[i], ssm_b_re[i], ssm_b_im[i],
                           ssm_c_re[i], ssm_c_im[i], ssm_d[i], ssm_w_glu[i], ssm_w_out[i])
        else:
            if layer == N_A_LAYERS:
                k_nope, k_rope, v = mla_shared_kv(h, kv_w_a, kv_norm_g, kv_w_b, cos, sin)
            j = layer - N_A_LAYERS
            mix = mla_mixer(h, q_w_a[j], q_norm_g[j], q_w_b[j], attn_w_o[j], k_nope, k_rope, v, cos, sin)
        h = layer_norm(DN_ALPHA * h + mix, ln_mix_g[layer], ln_mix_b[layer])
        h = layer_norm(DN_ALPHA * h + sq_relu_mlp(h, w_ff1[layer], w_ff2[layer]), ln_ffn_g[layer], ln_ffn_b[layer])
    return h


import jax as _jax
import jax.numpy as _jnp

TWIN_FORMAT = 'train_step'
FWD_PARAMS = ['x', 'positions', 'ln_mix_g', 'ln_mix_b', 'ln_ffn_g', 'ln_ffn_b', 'w_ff1', 'w_ff2', 'ssm_lam_re', 'ssm_lam_im', 'ssm_log_dt', 'ssm_b_re', 'ssm_b_im', 'ssm_c_re', 'ssm_c_im', 'ssm_d', 'ssm_w_glu', 'ssm_w_out', 'kv_w_a', 'kv_norm_g', 'kv_w_b', 'q_w_a', 'q_norm_g', 'q_w_b', 'attn_w_o']
TWIN_WEIGHTS = ['ln_mix_g', 'ln_mix_b', 'ln_ffn_g', 'ln_ffn_b', 'w_ff1', 'w_ff2', 'ssm_lam_re', 'ssm_lam_im', 'ssm_log_dt', 'ssm_b_re', 'ssm_b_im', 'ssm_c_re', 'ssm_c_im', 'ssm_d', 'ssm_w_glu', 'ssm_w_out', 'kv_w_a', 'kv_norm_g', 'kv_w_b', 'q_w_a', 'q_norm_g', 'q_w_b', 'attn_w_o']
TWIN_DIFF_INPUT = 'x'
TWIN_INPUTS = ['x', 'positions', 'ln_mix_g', 'ln_mix_b', 'ln_ffn_g', 'ln_ffn_b', 'w_ff1', 'w_ff2', 'ssm_lam_re', 'ssm_lam_im', 'ssm_log_dt', 'ssm_b_re', 'ssm_b_im', 'ssm_c_re', 'ssm_c_im', 'ssm_d', 'ssm_w_glu', 'ssm_w_out', 'kv_w_a', 'kv_norm_g', 'kv_w_b', 'q_w_a', 'q_norm_g', 'q_w_b', 'attn_w_o', 'loss_target', 'm_ln_mix_g', 'm_ln_mix_b', 'm_ln_ffn_g', 'm_ln_ffn_b', 'm_w_ff1', 'm_w_ff2', 'm_ssm_lam_re', 'm_ssm_lam_im', 'm_ssm_log_dt', 'm_ssm_b_re', 'm_ssm_b_im', 'm_ssm_c_re', 'm_ssm_c_im', 'm_ssm_d', 'm_ssm_w_glu', 'm_ssm_w_out', 'm_kv_w_a', 'm_kv_norm_g', 'm_kv_w_b', 'm_q_w_a', 'm_q_norm_g', 'm_q_w_b', 'm_attn_w_o', 'v_ln_mix_g', 'v_ln_mix_b', 'v_ln_ffn_g', 'v_ln_ffn_b', 'v_w_ff1', 'v_w_ff2', 'v_ssm_lam_re', 'v_ssm_lam_im', 'v_ssm_log_dt', 'v_ssm_b_re', 'v_ssm_b_im', 'v_ssm_c_re', 'v_ssm_c_im', 'v_ssm_d', 'v_ssm_w_glu', 'v_ssm_w_out', 'v_kv_w_a', 'v_kv_norm_g', 'v_kv_w_b', 'v_q_w_a', 'v_q_norm_g', 'v_q_w_b', 'v_attn_w_o']
TWIN_OUTPUTS = ['loss', 'grad_x', 'grad_ln_mix_g', 'grad_ln_mix_b', 'grad_ln_ffn_g', 'grad_ln_ffn_b', 'grad_w_ff1', 'grad_w_ff2', 'grad_ssm_lam_re', 'grad_ssm_lam_im', 'grad_ssm_log_dt', 'grad_ssm_b_re', 'grad_ssm_b_im', 'grad_ssm_c_re', 'grad_ssm_c_im', 'grad_ssm_d', 'grad_ssm_w_glu', 'grad_ssm_w_out', 'grad_kv_w_a', 'grad_kv_norm_g', 'grad_kv_w_b', 'grad_q_w_a', 'grad_q_norm_g', 'grad_q_w_b', 'grad_attn_w_o', 'delta_ln_mix_g', 'delta_ln_mix_b', 'delta_ln_ffn_g', 'delta_ln_ffn_b', 'delta_w_ff1', 'delta_w_ff2', 'delta_ssm_lam_re', 'delta_ssm_lam_im', 'delta_ssm_log_dt', 'delta_ssm_b_re', 'delta_ssm_b_im', 'delta_ssm_c_re', 'delta_ssm_c_im', 'delta_ssm_d', 'delta_ssm_w_glu', 'delta_ssm_w_out', 'delta_kv_w_a', 'delta_kv_norm_g', 'delta_kv_w_b', 'delta_q_w_a', 'delta_q_norm_g', 'delta_q_w_b', 'delta_attn_w_o', 'new_m_ln_mix_g', 'new_m_ln_mix_b', 'new_m_ln_ffn_g', 'new_m_ln_ffn_b', 'new_m_w_ff1', 'new_m_w_ff2', 'new_m_ssm_lam_re', 'new_m_ssm_lam_im', 'new_m_ssm_log_dt', 'new_m_ssm_b_re', 'new_m_ssm_b_im', 'new_m_ssm_c_re', 'new_m_ssm_c_im', 'new_m_ssm_d', 'new_m_ssm_w_glu', 'new_m_ssm_w_out', 'new_m_kv_w_a', 'new_m_kv_norm_g', 'new_m_kv_w_b', 'new_m_q_w_a', 'new_m_q_norm_g', 'new_m_q_w_b', 'new_m_attn_w_o', 'new_v_ln_mix_g', 'new_v_ln_mix_b', 'new_v_ln_ffn_g', 'new_v_ln_ffn_b', 'new_v_w_ff1', 'new_v_w_ff2', 'new_v_ssm_lam_re', 'new_v_ssm_lam_im', 'new_v_ssm_log_dt', 'new_v_ssm_b_re', 'new_v_ssm_b_im', 'new_v_ssm_c_re', 'new_v_ssm_c_im', 'new_v_ssm_d', 'new_v_ssm_w_glu', 'new_v_ssm_w_out', 'new_v_kv_w_a', 'new_v_kv_norm_g', 'new_v_kv_w_b', 'new_v_q_w_a', 'new_v_q_norm_g', 'new_v_q_w_b', 'new_v_attn_w_o']
TWIN_LEAF_KINDS = {'loss': 'loss', 'grad_x': 'grad_x', 'grad_ln_mix_g': 'grad_w', 'grad_ln_mix_b': 'grad_w', 'grad_ln_ffn_g': 'grad_w', 'grad_ln_ffn_b': 'grad_w', 'grad_w_ff1': 'grad_w', 'grad_w_ff2': 'grad_w', 'grad_ssm_lam_re': 'grad_w', 'grad_ssm_lam_im': 'grad_w', 'grad_ssm_log_dt': 'grad_w', 'grad_ssm_b_re': 'grad_w', 'grad_ssm_b_im': 'grad_w', 'grad_ssm_c_re': 'grad_w', 'grad_ssm_c_im': 'grad_w', 'grad_ssm_d': 'grad_w', 'grad_ssm_w_glu': 'grad_w', 'grad_ssm_w_out': 'grad_w', 'grad_kv_w_a': 'grad_w', 'grad_kv_norm_g': 'grad_w', 'grad_kv_w_b': 'grad_w', 'grad_q_w_a': 'grad_w', 'grad_q_norm_g': 'grad_w', 'grad_q_w_b': 'grad_w', 'grad_attn_w_o': 'grad_w', 'delta_ln_mix_g': 'delta_w', 'delta_ln_mix_b': 'delta_w', 'delta_ln_ffn_g': 'delta_w', 'delta_ln_ffn_b': 'delta_w', 'delta_w_ff1': 'delta_w', 'delta_w_ff2': 'delta_w', 'delta_ssm_lam_re': 'delta_w', 'delta_ssm_lam_im': 'delta_w', 'delta_ssm_log_dt': 'delta_w', 'delta_ssm_b_re': 'delta_w', 'delta_ssm_b_im': 'delta_w', 'delta_ssm_c_re': 'delta_w', 'delta_ssm_c_im': 'delta_w', 'delta_ssm_d': 'delta_w', 'delta_ssm_w_glu': 'delta_w', 'delta_ssm_w_out': 'delta_w', 'delta_kv_w_a': 'delta_w', 'delta_kv_norm_g': 'delta_w', 'delta_kv_w_b': 'delta_w', 'delta_q_w_a': 'delta_w', 'delta_q_norm_g': 'delta_w', 'delta_q_w_b': 'delta_w', 'delta_attn_w_o': 'delta_w', 'new_m_ln_mix_g': 'new_m', 'new_m_ln_mix_b': 'new_m', 'new_m_ln_ffn_g': 'new_m', 'new_m_ln_ffn_b': 'new_m', 'new_m_w_ff1': 'new_m', 'new_m_w_ff2': 'new_m', 'new_m_ssm_lam_re': 'new_m', 'new_m_ssm_lam_im': 'new_m', 'new_m_ssm_log_dt': 'new_m', 'new_m_ssm_b_re': 'new_m', 'new_m_ssm_b_im': 'new_m', 'new_m_ssm_c_re': 'new_m', 'new_m_ssm_c_im': 'new_m', 'new_m_ssm_d': 'new_m', 'new_m_ssm_w_glu': 'new_m', 'new_m_ssm_w_out': 'new_m', 'new_m_kv_w_a': 'new_m', 'new_m_kv_norm_g': 'new_m', 'new_m_kv_w_b': 'new_m', 'new_m_q_w_a': 'new_m', 'new_m_q_norm_g': 'new_m', 'new_m_q_w_b': 'new_m', 'new_m_attn_w_o': 'new_m', 'new_v_ln_mix_g': 'new_v', 'new_v_ln_mix_b': 'new_v', 'new_v_ln_ffn_g': 'new_v', 'new_v_ln_ffn_b': 'new_v', 'new_v_w_ff1': 'new_v', 'new_v_w_ff2': 'new_v', 'new_v_ssm_lam_re': 'new_v', 'new_v_ssm_lam_im': 'new_v', 'new_v_ssm_log_dt': 'new_v', 'new_v_ssm_b_re': 'new_v', 'new_v_ssm_b_im': 'new_v', 'new_v_ssm_c_re': 'new_v', 'new_v_ssm_c_im': 'new_v', 'new_v_ssm_d': 'new_v', 'new_v_ssm_w_glu': 'new_v', 'new_v_ssm_w_out': 'new_v', 'new_v_kv_w_a': 'new_v', 'new_v_kv_norm_g': 'new_v', 'new_v_kv_w_b': 'new_v', 'new_v_q_w_a': 'new_v', 'new_v_q_norm_g': 'new_v', 'new_v_q_w_b': 'new_v', 'new_v_attn_w_o': 'new_v'}


def _forward(args):
    return _fwd_reference(*[args[k] for k in FWD_PARAMS])


def _output_shape():
    out = _jax.eval_shape(lambda: _forward(_fwd_setup_inputs(0)))
    return out.shape, out.dtype

N_MICROBATCH = 1
ADAM_LR = 0.001
ADAM_B1 = 0.9
ADAM_B2 = 0.999
ADAM_EPS = 1e-08
ADAM_WD = 0.01
ADAM_STEP = 10
PER_EXAMPLE_BATCH_AXIS = {'x': 0, 'positions': 0, 'loss_target': 0}
SHARED_INPUTS = []
_WEIGHT_DTYPES = {'ln_mix_g': _jnp.float32, 'ln_mix_b': _jnp.float32, 'ln_ffn_g': _jnp.float32, 'ln_ffn_b': _jnp.float32, 'w_ff1': _jnp.float32, 'w_ff2': _jnp.float32, 'ssm_lam_re': _jnp.float32, 'ssm_lam_im': _jnp.float32, 'ssm_log_dt': _jnp.float32, 'ssm_b_re': _jnp.float32, 'ssm_b_im': _jnp.float32, 'ssm_c_re': _jnp.float32, 'ssm_c_im': _jnp.float32, 'ssm_d': _jnp.float32, 'ssm_w_glu': _jnp.float32, 'ssm_w_out': _jnp.float32, 'kv_w_a': _jnp.float32, 'kv_norm_g': _jnp.float32, 'kv_w_b': _jnp.float32, 'q_w_a': _jnp.float32, 'q_norm_g': _jnp.float32, 'q_w_b': _jnp.float32, 'attn_w_o': _jnp.float32}
MOMENT_SCALE = {'ln_mix_g': 7.550236e-01, 'ln_mix_b': 4.432944e-01, 'ln_ffn_g': 2.268828e+01, 'ln_ffn_b': 5.282327e+00, 'w_ff1': 4.155938e-02, 'w_ff2': 1.868685e-01, 'ssm_lam_re': 1.857788e-03, 'ssm_lam_im': 1.667705e-03, 'ssm_log_dt': 2.038481e+00, 'ssm_b_re': 1.171239e-03, 'ssm_b_im': 1.126544e-03, 'ssm_c_re': 1.606444e-03, 'ssm_c_im': 1.582810e-03, 'ssm_d': 3.657355e-02, 'ssm_w_glu': 2.240890e-02, 'ssm_w_out': 6.142502e-02, 'kv_w_a': 3.878631e-02, 'kv_norm_g': 4.876051e-02, 'kv_w_b': 1.558644e-02, 'q_w_a': 1.819313e-02, 'q_norm_g': 1.801645e-02, 'q_w_b': 9.123827e-03, 'attn_w_o': 4.498461e-02}


def _to_microbatches(a, axis):
    t = _jnp.moveaxis(a, axis, 0)
    t = t.reshape((N_MICROBATCH, t.shape[0] // N_MICROBATCH) + t.shape[1:])
    return _jnp.moveaxis(t, 1, axis + 1)


def setup_inputs(seed: int = 0) -> dict:
    inp = _fwd_setup_inputs(seed)
    key = _jax.random.fold_in(_jax.random.key(seed), 7919)
    shape, _ = _output_shape()
    out = dict(inp)
    out["loss_target"] = _jax.random.normal(_jax.random.fold_in(key, 0), shape, _jnp.float32)
    for i, name in enumerate(TWIN_WEIGHTS):
        w = inp[name].astype(_jnp.float32)
        if MOMENT_SCALE is None:
            s = _jnp.sqrt(_jnp.mean(_jnp.square(w)) + 1e-30)
        else:
            s = MOMENT_SCALE[name]
        km, kv = _jax.random.split(_jax.random.fold_in(key, i + 1))
        out[name] = w
        out["m_" + name] = s * _jax.random.normal(km, w.shape, _jnp.float32)
        out["v_" + name] = (s * s) * _jax.random.uniform(kv, w.shape, _jnp.float32, 0.5, 1.5)
    if N_MICROBATCH > 1:
        for name, axis in PER_EXAMPLE_BATCH_AXIS.items():
            out[name] = _to_microbatches(out[name], axis)
    return {'x': out['x'], 'positions': out['positions'], 'ln_mix_g': out['ln_mix_g'], 'ln_mix_b': out['ln_mix_b'], 'ln_ffn_g': out['ln_ffn_g'], 'ln_ffn_b': out['ln_ffn_b'], 'w_ff1': out['w_ff1'], 'w_ff2': out['w_ff2'], 'ssm_lam_re': out['ssm_lam_re'], 'ssm_lam_im': out['ssm_lam_im'], 'ssm_log_dt': out['ssm_log_dt'], 'ssm_b_re': out['ssm_b_re'], 'ssm_b_im': out['ssm_b_im'], 'ssm_c_re': out['ssm_c_re'], 'ssm_c_im': out['ssm_c_im'], 'ssm_d': out['ssm_d'], 'ssm_w_glu': out['ssm_w_glu'], 'ssm_w_out': out['ssm_w_out'], 'kv_w_a': out['kv_w_a'], 'kv_norm_g': out['kv_norm_g'], 'kv_w_b': out['kv_w_b'], 'q_w_a': out['q_w_a'], 'q_norm_g': out['q_norm_g'], 'q_w_b': out['q_w_b'], 'attn_w_o': out['attn_w_o'], 'loss_target': out['loss_target'], 'm_ln_mix_g': out['m_ln_mix_g'], 'm_ln_mix_b': out['m_ln_mix_b'], 'm_ln_ffn_g': out['m_ln_ffn_g'], 'm_ln_ffn_b': out['m_ln_ffn_b'], 'm_w_ff1': out['m_w_ff1'], 'm_w_ff2': out['m_w_ff2'], 'm_ssm_lam_re': out['m_ssm_lam_re'], 'm_ssm_lam_im': out['m_ssm_lam_im'], 'm_ssm_log_dt': out['m_ssm_log_dt'], 'm_ssm_b_re': out['m_ssm_b_re'], 'm_ssm_b_im': out['m_ssm_b_im'], 'm_ssm_c_re': out['m_ssm_c_re'], 'm_ssm_c_im': out['m_ssm_c_im'], 'm_ssm_d': out['m_ssm_d'], 'm_ssm_w_glu': out['m_ssm_w_glu'], 'm_ssm_w_out': out['m_ssm_w_out'], 'm_kv_w_a': out['m_kv_w_a'], 'm_kv_norm_g': out['m_kv_norm_g'], 'm_kv_w_b': out['m_kv_w_b'], 'm_q_w_a': out['m_q_w_a'], 'm_q_norm_g': out['m_q_norm_g'], 'm_q_w_b': out['m_q_w_b'], 'm_attn_w_o': out['m_attn_w_o'], 'v_ln_mix_g': out['v_ln_mix_g'], 'v_ln_mix_b': out['v_ln_mix_b'], 'v_ln_ffn_g': out['v_ln_ffn_g'], 'v_ln_ffn_b': out['v_ln_ffn_b'], 'v_w_ff1': out['v_w_ff1'], 'v_w_ff2': out['v_w_ff2'], 'v_ssm_lam_re': out['v_ssm_lam_re'], 'v_ssm_lam_im': out['v_ssm_lam_im'], 'v_ssm_log_dt': out['v_ssm_log_dt'], 'v_ssm_b_re': out['v_ssm_b_re'], 'v_ssm_b_im': out['v_ssm_b_im'], 'v_ssm_c_re': out['v_ssm_c_re'], 'v_ssm_c_im': out['v_ssm_c_im'], 'v_ssm_d': out['v_ssm_d'], 'v_ssm_w_glu': out['v_ssm_w_glu'], 'v_ssm_w_out': out['v_ssm_w_out'], 'v_kv_w_a': out['v_kv_w_a'], 'v_kv_norm_g': out['v_kv_norm_g'], 'v_kv_w_b': out['v_kv_w_b'], 'v_q_w_a': out['v_q_w_a'], 'v_q_norm_g': out['v_q_norm_g'], 'v_q_w_b': out['v_q_w_b'], 'v_attn_w_o': out['v_attn_w_o']}


def _loss(weights, diff, rest, loss_target):
    with _jax.named_scope("forward"):
        args = {**rest, TWIN_DIFF_INPUT: diff, **{k: w.astype(_WEIGHT_DTYPES[k]) for k, w in weights.items()}}
        y = _forward(args)
    with _jax.named_scope("loss_head"):
        err = _jnp.square(y.astype(_jnp.float32) - loss_target)
        return 0.5 * _jnp.sum(_jnp.mean(err, axis=-1)) if err.ndim else 0.5 * err


def _adamw(w, g, m, v):
    m = ADAM_B1 * m + (1.0 - ADAM_B1) * g
    v = ADAM_B2 * v + (1.0 - ADAM_B2) * _jnp.square(g)
    m_hat = m / (1.0 - ADAM_B1 ** ADAM_STEP)
    v_hat = v / (1.0 - ADAM_B2 ** ADAM_STEP)
    delta = -ADAM_LR * (m_hat / (_jnp.sqrt(v_hat) + ADAM_EPS) + ADAM_WD * w)
    return delta, m, v


def reference(x, positions, ln_mix_g, ln_mix_b, ln_ffn_g, ln_ffn_b, w_ff1, w_ff2, ssm_lam_re, ssm_lam_im, ssm_log_dt, ssm_b_re, ssm_b_im, ssm_c_re, ssm_c_im, ssm_d, ssm_w_glu, ssm_w_out, kv_w_a, kv_norm_g, kv_w_b, q_w_a, q_norm_g, q_w_b, attn_w_o, loss_target, m_ln_mix_g, m_ln_mix_b, m_ln_ffn_g, m_ln_ffn_b, m_w_ff1, m_w_ff2, m_ssm_lam_re, m_ssm_lam_im, m_ssm_log_dt, m_ssm_b_re, m_ssm_b_im, m_ssm_c_re, m_ssm_c_im, m_ssm_d, m_ssm_w_glu, m_ssm_w_out, m_kv_w_a, m_kv_norm_g, m_kv_w_b, m_q_w_a, m_q_norm_g, m_q_w_b, m_attn_w_o, v_ln_mix_g, v_ln_mix_b, v_ln_ffn_g, v_ln_ffn_b, v_w_ff1, v_w_ff2, v_ssm_lam_re, v_ssm_lam_im, v_ssm_log_dt, v_ssm_b_re, v_ssm_b_im, v_ssm_c_re, v_ssm_c_im, v_ssm_d, v_ssm_w_glu, v_ssm_w_out, v_kv_w_a, v_kv_norm_g, v_kv_w_b, v_q_w_a, v_q_norm_g, v_q_w_b, v_attn_w_o):
    given = dict(x=x, positions=positions, ln_mix_g=ln_mix_g, ln_mix_b=ln_mix_b, ln_ffn_g=ln_ffn_g, ln_ffn_b=ln_ffn_b, w_ff1=w_ff1, w_ff2=w_ff2, ssm_lam_re=ssm_lam_re, ssm_lam_im=ssm_lam_im, ssm_log_dt=ssm_log_dt, ssm_b_re=ssm_b_re, ssm_b_im=ssm_b_im, ssm_c_re=ssm_c_re, ssm_c_im=ssm_c_im, ssm_d=ssm_d, ssm_w_glu=ssm_w_glu, ssm_w_out=ssm_w_out, kv_w_a=kv_w_a, kv_norm_g=kv_norm_g, kv_w_b=kv_w_b, q_w_a=q_w_a, q_norm_g=q_norm_g, q_w_b=q_w_b, attn_w_o=attn_w_o, loss_target=loss_target, m_ln_mix_g=m_ln_mix_g, m_ln_mix_b=m_ln_mix_b, m_ln_ffn_g=m_ln_ffn_g, m_ln_ffn_b=m_ln_ffn_b, m_w_ff1=m_w_ff1, m_w_ff2=m_w_ff2, m_ssm_lam_re=m_ssm_lam_re, m_ssm_lam_im=m_ssm_lam_im, m_ssm_log_dt=m_ssm_log_dt, m_ssm_b_re=m_ssm_b_re, m_ssm_b_im=m_ssm_b_im, m_ssm_c_re=m_ssm_c_re, m_ssm_c_im=m_ssm_c_im, m_ssm_d=m_ssm_d, m_ssm_w_glu=m_ssm_w_glu, m_ssm_w_out=m_ssm_w_out, m_kv_w_a=m_kv_w_a, m_kv_norm_g=m_kv_norm_g, m_kv_w_b=m_kv_w_b, m_q_w_a=m_q_w_a, m_q_norm_g=m_q_norm_g, m_q_w_b=m_q_w_b, m_attn_w_o=m_attn_w_o, v_ln_mix_g=v_ln_mix_g, v_ln_mix_b=v_ln_mix_b, v_ln_ffn_g=v_ln_ffn_g, v_ln_ffn_b=v_ln_ffn_b, v_w_ff1=v_w_ff1, v_w_ff2=v_w_ff2, v_ssm_lam_re=v_ssm_lam_re, v_ssm_lam_im=v_ssm_lam_im, v_ssm_log_dt=v_ssm_log_dt, v_ssm_b_re=v_ssm_b_re, v_ssm_b_im=v_ssm_b_im, v_ssm_c_re=v_ssm_c_re, v_ssm_c_im=v_ssm_c_im, v_ssm_d=v_ssm_d, v_ssm_w_glu=v_ssm_w_glu, v_ssm_w_out=v_ssm_w_out, v_kv_w_a=v_kv_w_a, v_kv_norm_g=v_kv_norm_g, v_kv_w_b=v_kv_w_b, v_q_w_a=v_q_w_a, v_q_norm_g=v_q_norm_g, v_q_w_b=v_q_w_b, v_attn_w_o=v_attn_w_o)
    weights = {n: given[n] for n in TWIN_WEIGHTS}
    shared = {n: given[n] for n in SHARED_INPUTS}
    per_example = {n: given[n] for n in ['x', 'positions']}
    grad_fn = _jax.value_and_grad(_loss, argnums=(0, 1))

    def one_microbatch(ex, loss_target):
        ex = dict(ex)
        diff = ex.pop(TWIN_DIFF_INPUT)
        return grad_fn(weights, diff, {**shared, **ex}, loss_target)

    if N_MICROBATCH == 1:
        loss, (grad_w, grad_x) = one_microbatch(per_example, given["loss_target"])
    else:
        def body(carry, xs):
            loss_sum, grad_sum = carry
            l_k, (gw_k, gx_k) = one_microbatch(xs[0], xs[1])
            with _jax.named_scope("update"):
                return (loss_sum + l_k, _jax.tree.map(_jnp.add, grad_sum, gw_k)), gx_k

        init = (_jnp.zeros((), _jnp.float32), _jax.tree.map(_jnp.zeros_like, weights))
        (loss, grad_w), grad_x = _jax.lax.scan(body, init, (per_example, given["loss_target"]))
    with _jax.named_scope("update"):
        delta_w, new_m, new_v = {}, {}, {}
        for n in TWIN_WEIGHTS:
            delta_w[n], new_m[n], new_v[n] = _adamw(weights[n], grad_w[n], given["m_" + n], given["v_" + n])
    return (loss, grad_x, *[grad_w[n] for n in TWIN_WEIGHTS], *[delta_w[n] for n in TWIN_WEIGHTS],
            *[new_m[n] for n in TWIN_WEIGHTS], *[new_v[n] for n in TWIN_WEIGHTS])
```

```python
import functools
import math

import jax
import jax.numpy as jnp
from jax import lax
from jax.experimental import pallas as pl
from jax.experimental.pallas import tpu as pltpu

F32 = jnp.float32
BF16 = jnp.bfloat16
MESH = pl.DeviceIdType.MESH

D_MODEL = 1024
DEPTH = 2
SSM_GROUP = 16
N_GROUPS = D_MODEL // SSM_GROUP
SSM_STATE = 64
N_STATES = N_GROUPS * SSM_STATE
N_HEADS = 8
QK_NOPE = 128
QK_ROPE = 64
HALF_ROPE = QK_ROPE // 2
V_HEAD = 128
QK_DIM = QK_NOPE + QK_ROPE
Q_LORA = 384
KV_LORA = 256
ROPE_THETA = 10000.0
SM_SCALE = QK_DIM ** -0.5
NEG_INF = -1e30
D_FF = 4 * D_MODEL
DN_ALPHA = (2 * DEPTH) ** 0.25
LN_EPS = 1e-5
RMS_EPS = 1e-6
ADAM_LR = 0.001
ADAM_B1 = 0.9
ADAM_B2 = 0.999
ADAM_EPS = 1e-08
ADAM_WD = 0.01
ADAM_STEP = 10

N_CHIPS = 4
LANES = 128
STATE_TILES = N_STATES // LANES
VMEM_LIMIT = 56 * 1024 * 1024
PACK_W = 1024
KVA_PAD = 512

SHARDED = ("w_ff1", "w_ff2", "ssm_w_glu", "ssm_w_out", "kv_w_a", "kv_w_b", "q_w_a", "q_w_b", "attn_w_o", "ssm_d")
SHARD_ROWS = {"w_ff1": 2048, "w_ff2": 2048, "ssm_w_glu": 512, "ssm_w_out": 256, "kv_w_a": 80, "kv_w_b": 128,
              "q_w_a": 96, "q_w_b": 144, "attn_w_o": 256, "ssm_d": 16}
SHARD_OFF = {}
_o = 0
for _n in SHARDED:
    SHARD_OFF[_n] = _o
    _o += SHARD_ROWS[_n]
SMALL_OFF = _o
SMALL_Q_ROWS = 80
SMALL_ROWS = N_CHIPS * SMALL_Q_ROWS
W_PACK_ROWS = 5600
G_PACK_ROWS = 5760
G_BLOCK_ROWS = 960
REPLICATED = ("ln_mix_g", "ln_mix_b", "ln_ffn_g", "ln_ffn_b", "ssm_lam_re", "ssm_lam_im", "ssm_log_dt",
              "ssm_b_re", "ssm_b_im", "ssm_c_re", "ssm_c_im", "kv_norm_g", "q_norm_g")
WEIGHTS = ("ln_mix_g", "ln_mix_b", "ln_ffn_g", "ln_ffn_b", "w_ff1", "w_ff2", "ssm_lam_re", "ssm_lam_im",
           "ssm_log_dt", "ssm_b_re", "ssm_b_im", "ssm_c_re", "ssm_c_im", "ssm_d", "ssm_w_glu", "ssm_w_out",
           "kv_w_a", "kv_norm_g", "kv_w_b", "q_w_a", "q_norm_g", "q_w_b", "attn_w_o")


def _pcall(body, **kw):
    return pl.pallas_call(body, **kw)


def _params(sem=None):
    return pltpu.CompilerParams(dimension_semantics=sem, vmem_limit_bytes=VMEM_LIMIT)


def _tile(dim, prefs):
    for p in prefs:
        if dim % p == 0:
            return p
    return dim


def mm(a, b, *, name, ta=False, tb=False, pro_a=None, epi=None, extras=(), out_dtypes=(F32,)):
    if ta:
        k_dim, m_dim = a.shape
    else:
        m_dim, k_dim = a.shape
    n_dim = b.shape[0] if tb else b.shape[1]
    assert (b.shape[1] if tb else b.shape[0]) == k_dim, (a.shape, b.shape, ta, tb)
    tm = _tile(m_dim, (1024, 512, 256, 128))
    tn = _tile(n_dim, (1024, 512, 256, 128))
    tk = _tile(k_dim, (512, 256, 128))
    nk = k_dim // tk
    n_ex, n_out = len(extras), len(out_dtypes)
    dims = (((0 if ta else 1,), (1 if tb else 0,)), ((), ()))

    def body(a_ref, b_ref, *rest):
        ex_refs, out_refs, acc = rest[:n_ex], rest[n_ex:n_ex + n_out], rest[-1]
        k = pl.program_id(2)

        @pl.when(k == 0)
        def _():
            acc[...] = jnp.zeros_like(acc)

        av = a_ref[...]
        if pro_a is not None:
            av = pro_a(av)
        acc[...] += lax.dot_general(av.astype(BF16), b_ref[...].astype(BF16), dims, preferred_element_type=F32)

        @pl.when(k == nk - 1)
        def _():
            r = acc[...]
            res = epi(r, *[e[...] for e in ex_refs]) if epi is not None else (r,)
            for o_ref, v in zip(out_refs, res):
                o_ref[...] = v.astype(o_ref.dtype)

    a_spec = pl.BlockSpec((tk, tm), lambda i, j, k: (k, i)) if ta else pl.BlockSpec((tm, tk), lambda i, j, k: (i, k))
    b_spec = pl.BlockSpec((tn, tk), lambda i, j, k: (j, k)) if tb else pl.BlockSpec((tk, tn), lambda i, j, k: (k, j))
    o_spec = pl.BlockSpec((tm, tn), lambda i, j, k: (i, j))
    outs = _pcall(
        body, name=name, grid=(m_dim // tm, n_dim // tn, nk),
        in_specs=[a_spec, b_spec] + [o_spec] * n_ex,
        out_specs=[o_spec] * n_out,
        out_shape=[jax.ShapeDtypeStruct((m_dim, n_dim), dt) for dt in out_dtypes],
        scratch_shapes=[pltpu.VMEM((tm, tn), F32)],
        compiler_params=_params(("parallel", "parallel", "arbitrary")),
    )(a, b, *extras)
    return outs[0] if n_out == 1 else outs


def rowwise(fn, ins, outs, *, name, accs=(), tm=256):
    rows = ins[0].shape[0]
    tm = min(tm, rows)
    n_in, n_out, n_acc = len(ins), len(outs), len(accs)

    def body(*refs):
        in_refs, out_refs, acc_refs = refs[:n_in], refs[n_in:n_in + n_out], refs[n_in + n_out:]
        res, sums = fn(*[r[...] for r in in_refs])
        for o_ref, v in zip(out_refs, res):
            o_ref[...] = v.astype(o_ref.dtype)
        if n_acc:
            @pl.when(pl.program_id(0) == 0)
            def _():
                for a_ref in acc_refs:
                    a_ref[...] = jnp.zeros_like(a_ref)

            for a_ref, s in zip(acc_refs, sums):
                a_ref[...] += s

    def spec(arr):
        if arr.shape[0] == rows:
            return pl.BlockSpec((tm, arr.shape[1]), lambda i: (i, 0))
        return pl.BlockSpec(arr.shape, lambda i: (0, 0))

    res = _pcall(
        body, name=name, grid=(rows // tm,),
        in_specs=[spec(a) for a in ins],
        out_specs=[pl.BlockSpec((tm, w), lambda i: (i, 0)) for w, _ in outs]
        + [pl.BlockSpec((1, w), lambda i: (0, 0)) for w in accs],
        out_shape=[jax.ShapeDtypeStruct((rows, w), dt) for w, dt in outs]
        + [jax.ShapeDtypeStruct((1, w), F32) for w in accs],
        compiler_params=_params(("arbitrary",) if n_acc else ("parallel",)),
    )(*ins)
    return res


def _relu2(v):
    r = jnp.maximum(v, 0.0)
    return r * r


def _gelu(x):
    c = math.sqrt(2.0 / math.pi)
    return 0.5 * x * (1.0 + jnp.tanh(c * (x + 0.044715 * x * x * x)))


def _gelu_grad(x):
    c = math.sqrt(2.0 / math.pi)
    t = jnp.tanh(c * (x + 0.044715 * x * x * x))
    return 0.5 * (1.0 + t) + 0.5 * x * (1.0 - t * t) * c * (1.0 + 3 * 0.044715 * x * x)


def _sigmoid(x):
    return 1.0 / (1.0 + jnp.exp(-x))


def ln_fwd(h, mix, g, b, name):
    def fn(h, mix, g, b):
        r = DN_ALPHA * h + mix
        mu = jnp.mean(r, axis=-1, keepdims=True)
        xc = r - mu
        var = jnp.mean(xc * xc, axis=-1, keepdims=True)
        y = xc * lax.rsqrt(var + LN_EPS) * g + b
        return (y, y), ()
    return rowwise(fn, (h, mix, g, b), ((D_MODEL, F32), (D_MODEL, BF16)), name=name)


def ln_bwd(h, mix, g, dy, name):
    def fn(h, mix, g, dy):
        r = DN_ALPHA * h + mix
        mu = jnp.mean(r, axis=-1, keepdims=True)
        xc = r - mu
        var = jnp.mean(xc * xc, axis=-1, keepdims=True)
        rstd = lax.rsqrt(var + LN_EPS)
        xhat = xc * rstd
        dxh = dy * g
        m1 = jnp.mean(dxh, axis=-1, keepdims=True)
        m2 = jnp.mean(dxh * xhat, axis=-1, keepdims=True)
        dr = rstd * (dxh - m1 - xhat * m2)
        return (dr, dr), (jnp.sum(dy * xhat, axis=0, keepdims=True), jnp.sum(dy, axis=0, keepdims=True))
    return rowwise(fn, (h, mix, g, dy), ((D_MODEL, F32), (D_MODEL, BF16)), accs=(D_MODEL, D_MODEL), name=name)


def _rms(x, g):
    r = lax.rsqrt(jnp.mean(x * x, axis=-1, keepdims=True) + RMS_EPS)
    return x * r * g


def _rms_bwd(x, g, dy):
    r = lax.rsqrt(jnp.mean(x * x, axis=-1, keepdims=True) + RMS_EPS)
    xn = x * r
    dyg = dy * g
    dx = r * (dyg - xn * jnp.mean(dyg * xn, axis=-1, keepdims=True))
    return dx, jnp.sum(dy * xn, axis=0, keepdims=True)


def _s5_disc(lr, li, ldt):
    dt = jnp.exp(ldt)
    mag = jnp.exp(lr * dt)
    cs, sn = jnp.cos(li * dt), jnp.sin(li * dt)
    ar, ai = mag * cs, mag * sn
    inv = 1.0 / (lr * lr + li * li)
    n_re = (ar - 1.0) * lr + ai * li
    n_im = ai * lr - (ar - 1.0) * li
    return dt, mag, cs, sn, ar, ai, inv, n_re, n_im


def s5_prep(lr, li, ldt, b_re, b_im):
    def fn(lr, li, ldt, b_re, b_im):
        _, _, _, _, ar, ai, inv, n_re, n_im = _s5_disc(lr, li, ldt)
        cr, ci = n_re * inv, n_im * inv
        return (ar, ai, cr * b_re - ci * b_im, cr * b_im + ci * b_re), ()
    return rowwise(fn, (lr, li, ldt, b_re, b_im), ((1, F32), (1, F32), (SSM_GROUP, F32), (SSM_GROUP, F32)),
                   name="s5_prep", tm=512)


def s5_prep_bwd(lr, li, ldt, b_re, b_im, dar, dai, dbb_re, dbb_im):
    def fn(lr, li, ldt, b_re, b_im, dar, dai, dbb_re, dbb_im):
        dt, mag, cs, sn, ar, ai, inv, n_re, n_im = _s5_disc(lr, li, ldt)
        cr, ci = n_re * inv, n_im * inv
        db_re = cr * dbb_re + ci * dbb_im
        db_im = cr * dbb_im - ci * dbb_re
        dcr = jnp.sum(dbb_re * b_re + dbb_im * b_im, axis=-1, keepdims=True)
        dci = jnp.sum(dbb_im * b_re - dbb_re * b_im, axis=-1, keepdims=True)
        dar = dar + (dcr * lr - dci * li) * inv
        dai = dai + (dcr * li + dci * lr) * inv
        dinv = dcr * n_re + dci * n_im
        dlr = (dcr * (ar - 1.0) + dci * ai) * inv - 2.0 * lr * inv * inv * dinv
        dli = (dcr * ai - dci * (ar - 1.0)) * inv - 2.0 * li * inv * inv * dinv
        dmag = dar * cs + dai * sn
        dth = dai * ar - dar * ai
        dlr = dlr + dmag * mag * dt
        dli = dli + dth * dt
        ddt = dmag * mag * lr + dth * li
        return (dlr, dli, ddt * dt, db_re, db_im), ()
    return rowwise(fn, (lr, li, ldt, b_re, b_im, dar, dai, dbb_re, dbb_im),
                   ((1, F32), (1, F32), (1, F32), (SSM_GROUP, F32), (SSM_GROUP, F32)), name="s5_prep_bwd", tm=512)


def group_sum(x):
    def body(x_ref, o_ref):
        o_ref[...] = jnp.sum(x_ref[...], axis=1)
    return _pcall(body, name="s5_group_sum", out_shape=jax.ShapeDtypeStruct((N_GROUPS, 1), F32))(
        x.reshape(N_GROUPS, SSM_STATE, 1))


GROUPS_PER_TILE = LANES // SSM_GROUP
TILE_STATES = GROUPS_PER_TILE * SSM_STATE
N_UTILES = D_MODEL // LANES
TILES_PER_UTILE = TILE_STATES // LANES


def _store_states(ref, j, val, t_rows):
    for q in range(TILES_PER_UTILE):
        ref[pl.ds(TILES_PER_UTILE * j + q, t_rows, stride=STATE_TILES), :] = val[:, LANES * q:LANES * (q + 1)]


def _load_states(ref, j, t_rows):
    return jnp.concatenate(
        [ref[pl.ds(TILES_PER_UTILE * j + q, t_rows, stride=STATE_TILES), :] for q in range(TILES_PER_UTILE)], axis=1)


def _tok(t):
    return pl.ds(pl.multiple_of(t * STATE_TILES, STATE_TILES), STATE_TILES)


_NT = (((1,), (1,)), ((), ()))
_TN = (((0,), (0,)), ((), ()))


def s5_fwd(u, bbd_re, bbd_im, cbd_re, cbd_imn, a_re, a_im, dskip, t_rows=128):
    seq = u.shape[0]
    t_rows = min(t_rows, seq)

    def body(u_ref, bre, bim, cre, cimn, are, aim, d_ref, y_ref, hre_ref, him_ref, car_re, car_im):
        @pl.when(pl.program_id(0) == 0)
        def _():
            car_re[...] = jnp.zeros_like(car_re)
            car_im[...] = jnp.zeros_like(car_im)

        uf = u_ref[...]
        ub = uf.astype(BF16)
        for j in range(N_UTILES):
            uj = ub[:, LANES * j:LANES * (j + 1)]
            _store_states(hre_ref, j, jnp.dot(uj, bre[j], preferred_element_type=F32), t_rows)
            _store_states(him_ref, j, jnp.dot(uj, bim[j], preferred_element_type=F32), t_rows)
        ar, ai = are[...], aim[...]

        def step(t, carry):
            hr, hi = carry
            rows = _tok(t)
            nr = ar * hr - ai * hi + hre_ref[rows, :]
            ni = ar * hi + ai * hr + him_ref[rows, :]
            hre_ref[rows, :] = nr
            him_ref[rows, :] = ni
            return nr, ni

        hr, hi = lax.fori_loop(0, t_rows, step, (car_re[...], car_im[...]))
        car_re[...] = hr
        car_im[...] = hi
        dv = d_ref[...]
        for j in range(N_UTILES):
            hrj = _load_states(hre_ref, j, t_rows).astype(BF16)
            hij = _load_states(him_ref, j, t_rows).astype(BF16)
            yj = jnp.dot(hrj, cre[j], preferred_element_type=F32) + jnp.dot(hij, cimn[j], preferred_element_type=F32)
            sl = slice(LANES * j, LANES * (j + 1))
            y_ref[:, sl] = yj + dv[:, sl] * uf[:, sl]

    full3 = lambda a: pl.BlockSpec(a.shape, lambda i: (0, 0, 0))
    full2 = lambda a: pl.BlockSpec(a.shape, lambda i: (0, 0))
    return _pcall(
        body, name="s5_fwd", grid=(seq // t_rows,),
        in_specs=[pl.BlockSpec((t_rows, D_MODEL), lambda i: (i, 0)), full3(bbd_re), full3(bbd_im), full3(cbd_re),
                  full3(cbd_imn), full2(a_re), full2(a_im), full2(dskip)],
        out_specs=[pl.BlockSpec((t_rows, D_MODEL), lambda i: (i, 0)),
                   pl.BlockSpec((t_rows * STATE_TILES, LANES), lambda i: (i, 0)),
                   pl.BlockSpec((t_rows * STATE_TILES, LANES), lambda i: (i, 0))],
        out_shape=[jax.ShapeDtypeStruct((seq, D_MODEL), F32),
                   jax.ShapeDtypeStruct((seq * STATE_TILES, LANES), F32),
                   jax.ShapeDtypeStruct((seq * STATE_TILES, LANES), F32)],
        scratch_shapes=[pltpu.VMEM((STATE_TILES, LANES), F32), pltpu.VMEM((STATE_TILES, LANES), F32)],
        compiler_params=_params(("arbitrary",)),
    )(u, bbd_re, bbd_im, cbd_re, cbd_imn, a_re, a_im, dskip)


def s5_bwd(dy, u, dres, h_re, h_im, bbd_re, bbd_im, cbd_re, cbd_imn, a_re, a_im, dskip, t_rows=128):
    seq = u.shape[0]
    t_rows = min(t_rows, seq)
    n_chunks = seq // t_rows

    def body(dy_ref, u_ref, dres_ref, hre_ref, him_ref, hpre_ref, hpim_ref, bre, bim, cre, cimn, are, aim, d_ref,
             dx_ref, dbre, dbim, dcre, dcimn, dar_ref, dai_ref, dd_ref, lre, lim, car_re, car_im):
        i = pl.program_id(0)

        @pl.when(i == 0)
        def _():
            for r in (car_re, car_im, dbre, dbim, dcre, dcimn, dar_ref, dai_ref, dd_ref):
                r[...] = jnp.zeros_like(r)

        dyf = dy_ref[...]
        dyb = dyf.astype(BF16)
        uf = u_ref[...]
        ub = uf.astype(BF16)
        for j in range(N_UTILES):
            dyj = dyb[:, LANES * j:LANES * (j + 1)]
            _store_states(lre, j, lax.dot_general(dyj, cre[j], _NT, preferred_element_type=F32), t_rows)
            _store_states(lim, j, lax.dot_general(dyj, cimn[j], _NT, preferred_element_type=F32), t_rows)
        ar, ai = are[...], aim[...]

        def adjoint(t, lr, li):
            rows = _tok(t)
            nr = ar * lr + ai * li + lre[rows, :]
            ni = ar * li - ai * lr + lim[rows, :]
            lre[rows, :] = nr
            lim[rows, :] = ni
            return nr, ni

        def step(k, carry):
            lr, li, dar, dai = carry
            t = t_rows - 1 - k
            nr, ni = adjoint(t, lr, li)
            prev = _tok(t - 1)
            hpr, hpi = hre_ref[prev, :], him_ref[prev, :]
            return nr, ni, dar + nr * hpr + ni * hpi, dai + ni * hpr - nr * hpi

        zero = jnp.zeros((STATE_TILES, LANES), F32)
        lr, li, dar, dai = lax.fori_loop(0, t_rows - 1, step, (car_re[...], car_im[...], zero, zero))
        nr, ni = adjoint(0, lr, li)
        first = (i == n_chunks - 1).astype(F32)
        hpr = hpre_ref[...] * (1.0 - first)
        hpi = hpim_ref[...] * (1.0 - first)
        car_re[...] = nr
        car_im[...] = ni
        dar_ref[...] += dar + nr * hpr + ni * hpi
        dai_ref[...] += dai + ni * hpr - nr * hpi

        dv = d_ref[...]
        for j in range(N_UTILES):
            sl = slice(LANES * j, LANES * (j + 1))
            lrj = _load_states(lre, j, t_rows).astype(BF16)
            lij = _load_states(lim, j, t_rows).astype(BF16)
            du = (lax.dot_general(lrj, bre[j], _NT, preferred_element_type=F32)
                  + lax.dot_general(lij, bim[j], _NT, preferred_element_type=F32))
            dx_ref[:, sl] = du + dv[:, sl] * dyf[:, sl] + DN_ALPHA * dres_ref[:, sl]
            uj = ub[:, sl]
            dbre[j] += lax.dot_general(uj, lrj, _TN, preferred_element_type=F32)
            dbim[j] += lax.dot_general(uj, lij, _TN, preferred_element_type=F32)
            hrj = _load_states(hre_ref, j, t_rows).astype(BF16)
            hij = _load_states(him_ref, j, t_rows).astype(BF16)
            dyj = dyb[:, sl]
            dcre[j] += lax.dot_general(hrj, dyj, _TN, preferred_element_type=F32)
            dcimn[j] += lax.dot_general(hij, dyj, _TN, preferred_element_type=F32)
        dd_ref[...] += jnp.sum(dyf * uf, axis=0, keepdims=True)

    rev = lambda i: (n_chunks - 1 - i, 0)
    prev_tok = lambda i: (jnp.maximum((n_chunks - 1 - i) * t_rows - 1, 0), 0)
    full3 = lambda a: pl.BlockSpec(a.shape, lambda i: (0, 0, 0))
    full2 = lambda a: pl.BlockSpec(a.shape, lambda i: (0, 0))
    acc3 = lambda shape: pl.BlockSpec(shape, lambda i: (0, 0, 0))
    acc2 = lambda shape: pl.BlockSpec(shape, lambda i: (0, 0))
    st = (STATE_TILES, LANES)
    return _pcall(
        body, name="s5_bwd", grid=(n_chunks,),
        in_specs=[pl.BlockSpec((t_rows, D_MODEL), rev), pl.BlockSpec((t_rows, D_MODEL), rev),
                  pl.BlockSpec((t_rows, D_MODEL), rev),
                  pl.BlockSpec((t_rows * STATE_TILES, LANES), rev), pl.BlockSpec((t_rows * STATE_TILES, LANES), rev),
                  pl.BlockSpec(st, prev_tok), pl.BlockSpec(st, prev_tok),
                  full3(bbd_re), full3(bbd_im), full3(cbd_re), full3(cbd_imn), full2(a_re), full2(a_im), full2(dskip)],
        out_specs=[pl.BlockSpec((t_rows, D_MODEL), rev), acc3(bbd_re.shape), acc3(bbd_im.shape), acc3(cbd_re.shape),
                   acc3(cbd_imn.shape), acc2(st), acc2(st), acc2((1, D_MODEL))],
        out_shape=[jax.ShapeDtypeStruct((seq, D_MODEL), F32), jax.ShapeDtypeStruct(bbd_re.shape, F32),
                   jax.ShapeDtypeStruct(bbd_im.shape, F32), jax.ShapeDtypeStruct(cbd_re.shape, F32),
                   jax.ShapeDtypeStruct(cbd_imn.shape, F32), jax.ShapeDtypeStruct(st, F32),
                   jax.ShapeDtypeStruct(st, F32), jax.ShapeDtypeStruct((1, D_MODEL), F32)],
        scratch_shapes=[pltpu.VMEM((t_rows * STATE_TILES, LANES), F32), pltpu.VMEM((t_rows * STATE_TILES, LANES), F32),
                        pltpu.VMEM(st, F32), pltpu.VMEM(st, F32)],
        compiler_params=_params(("arbitrary",)),
    )(dy, u, dres, h_re, h_im, h_re, h_im, bbd_re, bbd_im, cbd_re, cbd_imn, a_re, a_im, dskip)


def _eye_groups():
    return jnp.eye(GROUPS_PER_TILE, dtype=F32)


def _blockdiag_in(bb):
    t = bb.transpose(0, 2, 1).reshape(N_UTILES, GROUPS_PER_TILE, SSM_GROUP, SSM_STATE)
    bd = jnp.einsum("jgcp,gh->jgchp", t, _eye_groups())
    return bd.reshape(N_UTILES, LANES, TILE_STATES)


def _blockdiag_in_t(d):
    t = jnp.einsum("jgchp,gh->jgcp", d.reshape(N_UTILES, GROUPS_PER_TILE, SSM_GROUP, GROUPS_PER_TILE, SSM_STATE),
                   _eye_groups())
    return t.reshape(N_GROUPS, SSM_GROUP, SSM_STATE).transpose(0, 2, 1)


def _blockdiag_out(c):
    t = c.transpose(0, 2, 1).reshape(N_UTILES, GROUPS_PER_TILE, SSM_STATE, SSM_GROUP)
    bd = jnp.einsum("jhpc,hg->jhpgc", t, _eye_groups())
    return bd.reshape(N_UTILES, TILE_STATES, LANES)


def _blockdiag_out_t(d):
    t = jnp.einsum("jhpgc,hg->jhpc", d.reshape(N_UTILES, GROUPS_PER_TILE, SSM_STATE, GROUPS_PER_TILE, SSM_GROUP),
                   _eye_groups())
    return t.reshape(N_GROUPS, SSM_STATE, SSM_GROUP).transpose(0, 2, 1)


def _causal(s, transposed=False):
    r = lax.broadcasted_iota(jnp.int32, s.shape, 0)
    c = lax.broadcasted_iota(jnp.int32, s.shape, 1)
    keep = (r <= c) if transposed else (c <= r)
    return jnp.where(keep, s, NEG_INF)


def attn_fwd(q, k, v, tq=512):
    n_heads, seq, _ = q.shape
    tq = min(tq, seq)

    def body(q_ref, k_ref, v_ref, o_ref, lse_ref):
        qi = pl.program_id(1)
        qv = q_ref[0]

        def block(j, carry, diag):
            m, l, acc = carry
            rows = pl.ds(pl.multiple_of(j * tq, tq), tq)
            s = lax.dot_general(qv, k_ref[0, rows, :], _NT, preferred_element_type=F32) * SM_SCALE
            if diag:
                s = _causal(s)
            m_new = jnp.maximum(m, jnp.max(s, axis=-1, keepdims=True))
            p = jnp.exp(s - m_new)
            corr = jnp.exp(m - m_new)
            l = l * corr + jnp.sum(p, axis=-1, keepdims=True)
            acc = acc * corr + jnp.dot(p.astype(BF16), v_ref[rows, :], preferred_element_type=F32)
            return m_new, l, acc

        init = (jnp.full((tq, 1), NEG_INF, F32), jnp.zeros((tq, 1), F32), jnp.zeros((tq, V_HEAD), F32))
        carry = lax.fori_loop(0, qi, lambda j, c: block(j, c, False), init)
        m, l, acc = block(qi, carry, True)
        o_ref[...] = acc / l
        lse_ref[0] = jnp.broadcast_to(m + jnp.log(l), (tq, LANES))

    return _pcall(
        body, name="attn_fwd", grid=(n_heads, seq // tq),
        in_specs=[pl.BlockSpec((1, tq, QK_DIM), lambda h, i: (h, i, 0)),
                  pl.BlockSpec((1, seq, QK_DIM), lambda h, i: (h, 0, 0)),
                  pl.BlockSpec((seq, V_HEAD), lambda h, i: (0, h))],
        out_specs=[pl.BlockSpec((tq, V_HEAD), lambda h, i: (i, h)),
                   pl.BlockSpec((1, tq, LANES), lambda h, i: (h, i, 0))],
        out_shape=[jax.ShapeDtypeStruct((seq, n_heads * V_HEAD), F32),
                   jax.ShapeDtypeStruct((n_heads, seq, LANES), F32)],
        compiler_params=_params(("parallel", "parallel")),
    )(q, k, v)


def attn_bwd_dq(q, k, v, do, o, lse, tq=512):
    n_heads, seq, _ = q.shape
    tq = min(tq, seq)

    def body(q_ref, k_ref, v_ref, do_ref, o_ref, lse_ref, dqn_ref, dqr_ref, delta_ref):
        qi = pl.program_id(1)
        qv = q_ref[0]
        dof = do_ref[...]
        dob = dof.astype(BF16)
        delta = jnp.sum(dof * o_ref[...], axis=-1, keepdims=True)
        lse = lse_ref[0][:, :1]

        def block(j, dq, diag):
            rows = pl.ds(pl.multiple_of(j * tq, tq), tq)
            kv = k_ref[0, rows, :]
            s = lax.dot_general(qv, kv, _NT, preferred_element_type=F32) * SM_SCALE
            if diag:
                s = _causal(s)
            p = jnp.exp(s - lse)
            dp = lax.dot_general(dob, v_ref[rows, :], _NT, preferred_element_type=F32)
            ds = p * (dp - delta) * SM_SCALE
            return dq + jnp.dot(ds.astype(BF16), kv, preferred_element_type=F32)

        dq = lax.fori_loop(0, qi, lambda j, c: block(j, c, False), jnp.zeros((tq, QK_DIM), F32))
        dq = block(qi, dq, True)
        dqn_ref[...] = dq[:, :QK_NOPE]
        dqr_ref[0] = dq[:, QK_NOPE:]
        delta_ref[0] = jnp.broadcast_to(delta, (tq, LANES))

    return _pcall(
        body, name="attn_bwd_dq", grid=(n_heads, seq // tq),
        in_specs=[pl.BlockSpec((1, tq, QK_DIM), lambda h, i: (h, i, 0)),
                  pl.BlockSpec((1, seq, QK_DIM), lambda h, i: (h, 0, 0)),
                  pl.BlockSpec((seq, V_HEAD), lambda h, i: (0, h)),
                  pl.BlockSpec((tq, V_HEAD), lambda h, i: (i, h)),
                  pl.BlockSpec((tq, V_HEAD), lambda h, i: (i, h)),
                  pl.BlockSpec((1, tq, LANES), lambda h, i: (h, i, 0))],
        out_specs=[pl.BlockSpec((tq, QK_NOPE), lambda h, i: (i, h)),
                   pl.BlockSpec((1, tq, QK_ROPE), lambda h, i: (h, i, 0)),
                   pl.BlockSpec((1, tq, LANES), lambda h, i: (h, i, 0))],
        out_shape=[jax.ShapeDtypeStruct((seq, n_heads * QK_NOPE), F32),
                   jax.ShapeDtypeStruct((n_heads, seq, QK_ROPE), F32),
                   jax.ShapeDtypeStruct((n_heads, seq, LANES), F32)],
        compiler_params=_params(("parallel", "parallel")),
    )(q, k, v, do, o, lse)


def attn_bwd_dkv(q, k, v, do, lse_row, delta_row, tq=512):
    n_heads, seq, _ = q.shape
    tq = min(tq, seq)
    n_blk = seq // tq

    def body(q_ref, k_ref, v_ref, do_ref, lse_ref, delta_ref, dkn_ref, dkr_ref, dv_ref):
        kj = pl.program_id(1)
        kv = k_ref[0]
        vv = v_ref[...]

        def block(i, carry, diag):
            dk, dv = carry
            rows = pl.ds(pl.multiple_of(i * tq, tq), tq)
            qv = q_ref[0, rows, :]
            st = lax.dot_general(kv, qv, _NT, preferred_element_type=F32) * SM_SCALE
            if diag:
                st = _causal(st, transposed=True)
            pt = jnp.exp(st - lse_ref[0, pl.ds(i, 1), :])
            dob = do_ref[rows, :].astype(BF16)
            dv = dv + jnp.dot(pt.astype(BF16), dob, preferred_element_type=F32)
            dpt = lax.dot_general(vv, dob, _NT, preferred_element_type=F32)
            dst = pt * (dpt - delta_ref[0, pl.ds(i, 1), :]) * SM_SCALE
            dk = dk + jnp.dot(dst.astype(BF16), qv, preferred_element_type=F32)
            return dk, dv

        carry = block(kj, (jnp.zeros((tq, QK_DIM), F32), jnp.zeros((tq, V_HEAD), F32)), True)
        dk, dv = lax.fori_loop(kj + 1, n_blk, lambda i, c: block(i, c, False), carry)
        dkn_ref[...] = dk[:, :QK_NOPE]
        dkr_ref[0] = dk[:, QK_NOPE:]
        dv_ref[...] = dv

    return _pcall(
        body, name="attn_bwd_dkv", grid=(n_heads, n_blk),
        in_specs=[pl.BlockSpec((1, seq, QK_DIM), lambda h, j: (h, 0, 0)),
                  pl.BlockSpec((1, tq, QK_DIM), lambda h, j: (h, j, 0)),
                  pl.BlockSpec((tq, V_HEAD), lambda h, j: (j, h)),
                  pl.BlockSpec((seq, V_HEAD), lambda h, j: (0, h)),
                  pl.BlockSpec((1, n_blk, tq), lambda h, j: (h, 0, 0)),
                  pl.BlockSpec((1, n_blk, tq), lambda h, j: (h, 0, 0))],
        out_specs=[pl.BlockSpec((tq, QK_NOPE), lambda h, j: (j, h)),
                   pl.BlockSpec((1, tq, QK_ROPE), lambda h, j: (h, j, 0)),
                   pl.BlockSpec((tq, V_HEAD), lambda h, j: (j, h))],
        out_shape=[jax.ShapeDtypeStruct((seq, n_heads * QK_NOPE), F32),
                   jax.ShapeDtypeStruct((n_heads, seq, QK_ROPE), F32),
                   jax.ShapeDtypeStruct((seq, n_heads * V_HEAD), F32)],
        compiler_params=_params(("parallel", "parallel")),
    )(q, k, v, do, lse_row, delta_row)


def head_sum(x, ts=512):
    n_heads, seq, w = x.shape
    ts = min(ts, seq)

    def body(x_ref, o_ref):
        o_ref[...] = jnp.sum(x_ref[...], axis=0)

    return _pcall(body, name="head_sum", grid=(seq // ts,),
                  in_specs=[pl.BlockSpec((n_heads, ts, w), lambda i: (0, i, 0))],
                  out_specs=pl.BlockSpec((ts, w), lambda i: (i, 0)),
                  out_shape=jax.ShapeDtypeStruct((seq, w), F32),
                  compiler_params=_params(("parallel",)))(x)


def _perm_q_cols(w):
    r = w.shape[0]
    t = w.reshape(r, N_HEADS, QK_DIM)
    return jnp.concatenate([t[:, :, :QK_NOPE].reshape(r, -1), t[:, :, QK_NOPE:QK_NOPE + HALF_ROPE].reshape(r, -1),
                            t[:, :, QK_NOPE + HALF_ROPE:].reshape(r, -1)], axis=1)


def _unperm_q_cols(w):
    r = w.shape[0]
    nope = w[:, :N_HEADS * QK_NOPE].reshape(r, N_HEADS, QK_NOPE)
    r1 = w[:, N_HEADS * QK_NOPE:N_HEADS * (QK_NOPE + HALF_ROPE)].reshape(r, N_HEADS, HALF_ROPE)
    r2 = w[:, N_HEADS * (QK_NOPE + HALF_ROPE):].reshape(r, N_HEADS, HALF_ROPE)
    return jnp.concatenate([nope, r1, r2], axis=2).reshape(r, N_HEADS * QK_DIM)


def _pad_kva_cols(w):
    z = jnp.zeros((w.shape[0], LANES - HALF_ROPE), w.dtype)
    return jnp.concatenate([w[:, :KV_LORA], w[:, KV_LORA:KV_LORA + HALF_ROPE], z, w[:, KV_LORA + HALF_ROPE:], z], axis=1)


def _unpad_kva_cols(w):
    return jnp.concatenate([w[:, :KV_LORA], w[:, KV_LORA:KV_LORA + HALF_ROPE],
                            w[:, KV_LORA + LANES:KV_LORA + LANES + HALF_ROPE]], axis=1)


def device_step(x, positions, target, w):
    seq = x.shape[0]
    nq = N_HEADS * QK_NOPE
    nr = N_HEADS * HALF_ROPE

    inv_freq = ROPE_THETA ** (-jnp.arange(HALF_ROPE, dtype=F32) / HALF_ROPE)
    ang = positions.astype(F32)[:, None] * inv_freq
    cos, sin = jnp.cos(ang), jnp.sin(ang)
    cos_h, sin_h = jnp.tile(cos, (1, N_HEADS)), jnp.tile(sin, (1, N_HEADS))
    zpad = jnp.zeros((seq, LANES - HALF_ROPE), F32)
    cos_p, sin_p = jnp.concatenate([cos, zpad], 1), jnp.concatenate([sin, zpad], 1)

    lr = w["ssm_lam_re"].reshape(N_STATES, 1)
    li = w["ssm_lam_im"].reshape(N_STATES, 1)
    ldt = jnp.repeat(w["ssm_log_dt"].reshape(N_GROUPS), SSM_STATE).reshape(N_STATES, 1)
    b_re = w["ssm_b_re"].reshape(N_STATES, SSM_GROUP)
    b_im = w["ssm_b_im"].reshape(N_STATES, SSM_GROUP)
    a_re, a_im, bb_re, bb_im = s5_prep(lr, li, ldt, b_re, b_im)
    a_re, a_im = a_re.reshape(STATE_TILES, LANES), a_im.reshape(STATE_TILES, LANES)
    bbd_re = _blockdiag_in(bb_re.reshape(N_GROUPS, SSM_STATE, SSM_GROUP)).astype(BF16)
    bbd_im = _blockdiag_in(bb_im.reshape(N_GROUPS, SSM_STATE, SSM_GROUP)).astype(BF16)
    cbd_re = _blockdiag_out(w["ssm_c_re"].reshape(N_GROUPS, SSM_GROUP, SSM_STATE)).astype(BF16)
    cbd_imn = _blockdiag_out(-w["ssm_c_im"].reshape(N_GROUPS, SSM_GROUP, SSM_STATE)).astype(BF16)
    dskip = w["ssm_d"].reshape(1, D_MODEL)
    ypre, h_re, h_im = s5_fwd(x, bbd_re, bbd_im, cbd_re, cbd_imn, a_re, a_im, dskip)
    (yg,) = rowwise(lambda y: ((_gelu(y),), ()), (ypre,), ((D_MODEL, BF16),), name="gelu")
    vg = mm(yg, w["ssm_w_glu"], name="glu_proj")

    def glu(v):
        return (v[:, :D_MODEL] * _sigmoid(v[:, D_MODEL:]),), ()
    (z,) = rowwise(glu, (vg,), ((D_MODEL, BF16),), name="glu")
    mix0 = mm(z, w["ssm_w_out"], name="ssm_out")

    def mlp_fwd(h, hb, mix_unused, layer):
        pre = mm(hb, w["w_ff1"][layer], name=f"ff1_{layer}")
        f = mm(pre, w["w_ff2"][layer], pro_a=_relu2, name=f"ff2_{layer}")
        return pre, f

    ln = lambda name, l: w[name][l].reshape(1, D_MODEL)
    h1, h1b = ln_fwd(x, mix0, ln("ln_mix_g", 0), ln("ln_mix_b", 0), "ln_mix_0")
    f1pre, f1 = mlp_fwd(h1, h1b, None, 0)
    h2, h2b = ln_fwd(h1, f1, ln("ln_ffn_g", 0), ln("ln_ffn_b", 0), "ln_ffn_0")

    kv_w_a = _pad_kva_cols(w["kv_w_a"])
    kvn_g = w["kv_norm_g"].reshape(1, KV_LORA)
    qn_g = w["q_norm_g"].reshape(1, Q_LORA)
    q_w_b = _perm_q_cols(w["q_w_b"])
    kva = mm(h2b, kv_w_a, name="kv_a")

    def kv_post(kva, g, cs, sn):
        r1, r2 = kva[:, KV_LORA:KV_LORA + LANES], kva[:, KV_LORA + LANES:]
        return (_rms(kva[:, :KV_LORA], g), jnp.concatenate([r1 * cs - r2 * sn, r1 * sn + r2 * cs], axis=1)), ()
    ckv, krope = rowwise(kv_post, (kva, kvn_g, cos_p, sin_p), ((KV_LORA, BF16), (2 * LANES, BF16)), name="kv_post")
    kvb = mm(ckv, w["kv_w_b"], name="kv_b", out_dtypes=(BF16,))
    cq_raw = mm(h2b, w["q_w_a"], name="q_a")
    (cq,) = rowwise(lambda c, g: ((_rms(c, g),), ()), (cq_raw, qn_g), ((Q_LORA, BF16),), name="q_norm")
    qlin = mm(cq, q_w_b, name="q_b")

    def q_rope(q, cs, sn):
        r1, r2 = q[:, nq:nq + nr], q[:, nq + nr:]
        return (jnp.concatenate([q[:, :nq], r1 * cs - r2 * sn, r1 * sn + r2 * cs], axis=1),), ()
    (qro,) = rowwise(q_rope, (qlin, cos_h, sin_h), ((nq + 2 * nr, BF16),), name="q_rope")
    q_h = jnp.concatenate([qro[:, :nq].reshape(seq, N_HEADS, QK_NOPE), qro[:, nq:nq + nr].reshape(seq, N_HEADS, HALF_ROPE),
                           qro[:, nq + nr:].reshape(seq, N_HEADS, HALF_ROPE)], axis=2).transpose(1, 0, 2)
    kvb3 = kvb.reshape(seq, N_HEADS, QK_NOPE + V_HEAD)
    kr = jnp.concatenate([krope[:, :HALF_ROPE], krope[:, LANES:LANES + HALF_ROPE]], axis=1)
    k_h = jnp.concatenate([kvb3[:, :, :QK_NOPE], jnp.broadcast_to(kr[:, None, :], (seq, N_HEADS, QK_ROPE))],
                          axis=2).transpose(1, 0, 2)
    v2 = kvb3[:, :, QK_NOPE:].reshape(seq, N_HEADS * V_HEAD)
    o, lse = attn_fwd(q_h, k_h, v2)
    mix1 = mm(o, w["attn_w_o"], name="attn_out")
    h3, h3b = ln_fwd(h2, mix1, ln("ln_mix_g", 1), ln("ln_mix_b", 1), "ln_mix_1")
    f2pre, f2 = mlp_fwd(h3, h3b, None, 1)
    h4, _ = ln_fwd(h3, f2, ln("ln_ffn_g", 1), ln("ln_ffn_b", 1), "ln_ffn_1")

    def loss_fn(y, t):
        e = y - t
        return (e * (1.0 / D_MODEL),), (jnp.broadcast_to(jnp.sum(e * e), (1, LANES)),)
    dh4, loss_acc = rowwise(loss_fn, (h4, target), ((D_MODEL, F32),), accs=(LANES,), name="loss")
    loss = loss_acc[0, 0] * (0.5 / D_MODEL)

    g = {}

    def mlp_bwd(dr, drb, hb, pre, layer):
        dpre = mm(drb, w["w_ff2"][layer], tb=True, epi=lambda r, p: (r * 2.0 * jnp.maximum(p, 0.0),), extras=(pre,),
                  out_dtypes=(BF16,), name=f"ff2_dx_{layer}")
        dw2 = mm(pre, drb, ta=True, pro_a=_relu2, name=f"ff2_dw_{layer}")
        dw1 = mm(hb, dpre, ta=True, name=f"ff1_dw_{layer}")
        dh = mm(dpre, w["w_ff1"][layer], tb=True, epi=lambda r, d: (r + DN_ALPHA * d,), extras=(dr,),
                name=f"ff1_dx_{layer}")
        return dw1, dw2, dh

    dr4, dr4b, dg_f1, db_f1 = ln_bwd(h3, f2, ln("ln_ffn_g", 1), dh4, "ln_ffn_bwd_1")
    dw1_1, dw2_1, dh3 = mlp_bwd(dr4, dr4b, h3b, f2pre, 1)
    dr3, dr3b, dg_m1, db_m1 = ln_bwd(h2, mix1, ln("ln_mix_g", 1), dh3, "ln_mix_bwd_1")
    g["attn_w_o"] = mm(o, dr3b, ta=True, name="attn_out_dw")
    do = mm(dr3b, w["attn_w_o"], tb=True, name="attn_out_dx")
    dqn, dqr, delta = attn_bwd_dq(q_h, k_h, v2, do, o, lse)
    tq = min(512, seq)
    lse_row = lse[:, :, 0].reshape(N_HEADS, seq // tq, tq)
    delta_row = delta[:, :, 0].reshape(N_HEADS, seq // tq, tq)
    dkn, dkr, dv = attn_bwd_dkv(q_h, k_h, v2, do, lse_row, delta_row)
    dqr_t = dqr.transpose(1, 0, 2)
    dq_cat = jnp.concatenate([dqn, dqr_t[:, :, :HALF_ROPE].reshape(seq, nr), dqr_t[:, :, HALF_ROPE:].reshape(seq, nr)], 1)

    def q_rope_bwd(dq, cs, sn):
        d1, d2 = dq[:, nq:nq + nr], dq[:, nq + nr:]
        return (jnp.concatenate([dq[:, :nq], d1 * cs + d2 * sn, d2 * cs - d1 * sn], axis=1),), ()
    (dqlin,) = rowwise(q_rope_bwd, (dq_cat, cos_h, sin_h), ((nq + 2 * nr, BF16),), name="q_rope_bwd")
    g["q_w_b"] = _unperm_q_cols(mm(cq, dqlin, ta=True, name="q_b_dw"))
    dcq = mm(dqlin, q_w_b, tb=True, name="q_b_dx")

    def q_norm_bwd(c, gq, d):
        dx, dgq = _rms_bwd(c, gq, d)
        return (dx,), (dgq,)
    dcq_raw, dqn_g = rowwise(q_norm_bwd, (cq_raw, qn_g, dcq), ((Q_LORA, BF16),), accs=(Q_LORA,), name="q_norm_bwd")
    g["q_w_a"] = mm(h2b, dcq_raw, ta=True, name="q_a_dw")
    dkvb = jnp.concatenate([dkn.reshape(seq, N_HEADS, QK_NOPE), dv.reshape(seq, N_HEADS, V_HEAD)], 2).reshape(
        seq, N_HEADS * (QK_NOPE + V_HEAD)).astype(BF16)
    g["kv_w_b"] = mm(ckv, dkvb, ta=True, name="kv_b_dw")
    dckv = mm(dkvb, w["kv_w_b"], tb=True, name="kv_b_dx")
    dkr_sum = head_sum(dkr)
    dkr1 = jnp.concatenate([dkr_sum[:, :HALF_ROPE], zpad], 1)
    dkr2 = jnp.concatenate([dkr_sum[:, HALF_ROPE:], zpad], 1)

    def kv_post_bwd(kva, gk, dc, d1, d2, cs, sn):
        dx, dgk = _rms_bwd(kva[:, :KV_LORA], gk, dc)
        return (jnp.concatenate([dx, d1 * cs + d2 * sn, d2 * cs - d1 * sn], axis=1),), (dgk,)
    dkva, dkvn_g = rowwise(kv_post_bwd, (kva, kvn_g, dckv, dkr1, dkr2, cos_p, sin_p), ((KVA_PAD, BF16),),
                           accs=(KV_LORA,), name="kv_post_bwd")
    g["kv_w_a"] = _unpad_kva_cols(mm(h2b, dkva, ta=True, name="kv_a_dw"))
    dh2 = mm(dcq_raw, w["q_w_a"], tb=True, epi=lambda r, d: (r + DN_ALPHA * d,), extras=(dr3,), name="q_a_dx")
    dh2 = mm(dkva, kv_w_a, tb=True, epi=lambda r, d: (r + d,), extras=(dh2,), name="kv_a_dx")

    dr2, dr2b, dg_f0, db_f0 = ln_bwd(h1, f1, ln("ln_ffn_g", 0), dh2, "ln_ffn_bwd_0")
    dw1_0, dw2_0, dh1 = mlp_bwd(dr2, dr2b, h1b, f1pre, 0)
    dr1, dr1b, dg_m0, db_m0 = ln_bwd(x, mix0, ln("ln_mix_g", 0), dh1, "ln_mix_bwd_0")
    g["ssm_w_out"] = mm(z, dr1b, ta=True, name="ssm_out_dw")
    dz = mm(dr1b, w["ssm_w_out"], tb=True, name="ssm_out_dx")

    def glu_bwd(v, dz):
        val, sg = v[:, :D_MODEL], _sigmoid(v[:, D_MODEL:])
        return (jnp.concatenate([dz * sg, dz * val * sg * (1.0 - sg)], axis=1),), ()
    (dvg,) = rowwise(glu_bwd, (vg, dz), ((2 * D_MODEL, BF16),), name="glu_bwd")
    g["ssm_w_glu"] = mm(yg, dvg, ta=True, name="glu_proj_dw")
    dypre = mm(dvg, w["ssm_w_glu"], tb=True, epi=lambda r, y: (r * _gelu_grad(y),), extras=(ypre,), name="glu_proj_dx")
    dx, dbbd_re, dbbd_im, dcbd_re, dcbd_imn, dar, dai, dd = s5_bwd(
        dypre, x, dr1, h_re, h_im, bbd_re, bbd_im, cbd_re, cbd_imn, a_re, a_im, dskip)
    dbb_re = _blockdiag_in_t(dbbd_re).reshape(N_STATES, SSM_GROUP)
    dbb_im = _blockdiag_in_t(dbbd_im).reshape(N_STATES, SSM_GROUP)
    dlr, dli, dldt, db_re, db_im = s5_prep_bwd(lr, li, ldt, b_re, b_im, dar.reshape(N_STATES, 1),
                                               dai.reshape(N_STATES, 1), dbb_re, dbb_im)
    g["ssm_lam_re"] = dlr.reshape(1, N_GROUPS, SSM_STATE)
    g["ssm_lam_im"] = dli.reshape(1, N_GROUPS, SSM_STATE)
    g["ssm_log_dt"] = group_sum(dldt).reshape(1, N_GROUPS)
    g["ssm_b_re"] = db_re.reshape(1, N_GROUPS, SSM_STATE, SSM_GROUP)
    g["ssm_b_im"] = db_im.reshape(1, N_GROUPS, SSM_STATE, SSM_GROUP)
    g["ssm_c_re"] = _blockdiag_out_t(dcbd_re).reshape(1, N_GROUPS, SSM_GROUP, SSM_STATE)
    g["ssm_c_im"] = -_blockdiag_out_t(dcbd_imn).reshape(1, N_GROUPS, SSM_GROUP, SSM_STATE)
    g["ssm_d"] = dd
    g["w_ff1"] = jnp.stack([dw1_0, dw1_1])
    g["w_ff2"] = jnp.stack([dw2_0, dw2_1])
    g["ln_mix_g"] = jnp.concatenate([dg_m0, dg_m1], 0)
    g["ln_mix_b"] = jnp.concatenate([db_m0, db_m1], 0)
    g["ln_ffn_g"] = jnp.concatenate([dg_f0, dg_f1], 0)
    g["ln_ffn_b"] = jnp.concatenate([db_f0, db_f1], 0)
    g["kv_norm_g"] = dkvn_g.reshape(KV_LORA)
    g["q_norm_g"] = dqn_g
    return loss, dx, g


def _place():
    x, y, c = lax.axis_index("x"), lax.axis_index("y"), lax.axis_index("c")
    return x, y, c, [(1 - x, y), (x, 1 - y), (1 - x, 1 - y)]


_ANY = pl.BlockSpec(memory_space=pl.ANY)


def all_gather_chips(shard, name):
    rows, cols = shard.shape
    rh = rows // 2

    def body(x_ref, out_ref, send_sems, recv_sems, local_sem):
        x, y, c, chips = _place()
        sibling = (x, y, 1 - c)
        me = 2 * x + y
        half, other = pl.ds(c * rh, rh), pl.ds((1 - c) * rh, rh)

        def copy(k, src, dst, to):
            return pltpu.make_async_remote_copy(src_ref=src, dst_ref=dst, send_sem=send_sems.at[k],
                                                recv_sem=recv_sems.at[k], device_id=to, device_id_type=MESH)

        mine = pltpu.make_async_copy(x_ref, out_ref.at[me], local_sem)
        mine.start()
        sends = [copy(j, x_ref.at[half], out_ref.at[me, half], (px, py, c)) for j, (px, py) in enumerate(chips)]
        for cp in sends:
            cp.start()
        passed = []
        for j, (px, py) in enumerate(chips):
            blk = out_ref.at[2 * px + py, half]
            copy(j, blk, blk, (px, py, c)).wait_recv()
            cp = copy(3 + j, blk, blk, sibling)
            cp.start()
            passed.append(cp)
        for j, (px, py) in enumerate(chips):
            blk = out_ref.at[2 * px + py, other]
            copy(3 + j, blk, blk, sibling).wait_recv()
        for cp in sends + passed:
            cp.wait_send()
        mine.wait()

    return _pcall(body, name=name, in_specs=[_ANY], out_specs=_ANY,
                  out_shape=jax.ShapeDtypeStruct((N_CHIPS, rows, cols), shard.dtype),
                  scratch_shapes=[pltpu.SemaphoreType.DMA((6,)), pltpu.SemaphoreType.DMA((6,)), pltpu.SemaphoreType.DMA])(shard)


def swap_halves(gpack):
    _, rows, cols = gpack.shape
    rh = rows // 2

    def body(g_ref, got_ref, send_sem, recv_sem):
        x, y, c, _ = _place()
        cp = pltpu.make_async_remote_copy(src_ref=g_ref.at[:, pl.ds((1 - c) * rh, rh), :], dst_ref=got_ref,
                                          send_sem=send_sem, recv_sem=recv_sem, device_id=(x, y, 1 - c),
                                          device_id_type=MESH)
        cp.start()
        cp.wait()

    return _pcall(body, name="grad_swap_halves", in_specs=[_ANY], out_specs=_ANY,
                  out_shape=jax.ShapeDtypeStruct((N_CHIPS, rh, cols), gpack.dtype),
                  scratch_shapes=[pltpu.SemaphoreType.DMA, pltpu.SemaphoreType.DMA])(gpack)


def add_halves(gpack, got, c_idx):
    _, rows, cols = gpack.shape
    rh = rows // 2
    nb = rh // G_BLOCK_ROWS

    def body(c_ref, g_ref, r_ref, o_ref):
        o_ref[...] = g_ref[...] + r_ref[...]

    blk = (1, G_BLOCK_ROWS, cols)
    return _pcall(
        body, name="grad_add_halves",
        grid_spec=pltpu.PrefetchScalarGridSpec(
            num_scalar_prefetch=1, grid=(N_CHIPS, nb),
            in_specs=[pl.BlockSpec(blk, lambda k, i, c: (k, c[0] * nb + i, 0)), pl.BlockSpec(blk, lambda k, i, c: (k, i, 0))],
            out_specs=pl.BlockSpec(blk, lambda k, i, c: (k, i, 0))),
        out_shape=jax.ShapeDtypeStruct((N_CHIPS, rh, cols), F32),
        compiler_params=_params(("parallel", "parallel")),
    )(c_idx, gpack, got)


def send_to_owners(part):
    _, rh, cols = part.shape

    def body(p_ref, got_ref, send_sems, recv_sems):
        x, y, c, chips = _place()
        cps = [pltpu.make_async_remote_copy(src_ref=p_ref.at[2 * px + py], dst_ref=got_ref.at[j],
                                            send_sem=send_sems.at[j], recv_sem=recv_sems.at[j],
                                            device_id=(px, py, c), device_id_type=MESH)
               for j, (px, py) in enumerate(chips)]
        for cp in cps:
            cp.start()
        for cp in cps:
            cp.wait()

    return _pcall(body, name="grad_send_to_owners", in_specs=[_ANY], out_specs=_ANY,
                  out_shape=jax.ShapeDtypeStruct((3, rh, cols), part.dtype),
                  scratch_shapes=[pltpu.SemaphoreType.DMA((3,)), pltpu.SemaphoreType.DMA((3,))])(part)


def sum_owner(part, got, me_idx):
    _, rh, cols = part.shape
    tr = G_BLOCK_ROWS // 2

    def body(m_ref, p_ref, g_ref, o_ref):
        o_ref[...] = ((p_ref[0] + g_ref[0]) + g_ref[1]) + g_ref[2]

    return _pcall(
        body, name="grad_sum_owner",
        grid_spec=pltpu.PrefetchScalarGridSpec(
            num_scalar_prefetch=1, grid=(rh // tr,),
            in_specs=[pl.BlockSpec((1, tr, cols), lambda i, m: (m[0], i, 0)),
                      pl.BlockSpec((3, tr, cols), lambda i, m: (0, i, 0))],
            out_specs=pl.BlockSpec((tr, cols), lambda i, m: (i, 0))),
        out_shape=jax.ShapeDtypeStruct((rh, cols), F32),
        compiler_params=_params(("parallel",)),
    )(me_idx, part, got)


def join_halves(mine):
    rh, cols = mine.shape

    def body(m_ref, out_ref, send_sem, recv_sem, local_sem):
        x, y, c, _ = _place()
        half = pl.ds(c * rh, rh)
        keep = pltpu.make_async_copy(m_ref, out_ref.at[half], local_sem)
        keep.start()
        cp = pltpu.make_async_remote_copy(src_ref=m_ref, dst_ref=out_ref.at[half], send_sem=send_sem, recv_sem=recv_sem,
                                          device_id=(x, y, 1 - c), device_id_type=MESH)
        cp.start()
        cp.wait_send()
        other = out_ref.at[pl.ds((1 - c) * rh, rh)]
        pltpu.make_async_remote_copy(src_ref=m_ref, dst_ref=other, send_sem=send_sem, recv_sem=recv_sem,
                                     device_id=(x, y, 1 - c), device_id_type=MESH).wait_recv()
        keep.wait()

    return _pcall(body, name="grad_join_halves", in_specs=[_ANY], out_specs=_ANY,
                  out_shape=jax.ShapeDtypeStruct((2 * rh, cols), mine.dtype),
                  scratch_shapes=[pltpu.SemaphoreType.DMA, pltpu.SemaphoreType.DMA, pltpu.SemaphoreType.DMA])(mine)


def adamw(gsrc, g_off, wt, m, v, name):
    n, cols = wt.shape
    tr = math.gcd(math.gcd(g_off, n), 256) if g_off else math.gcd(n, 256)
    off_blk = g_off // tr
    c1 = 1.0 / (1.0 - ADAM_B1 ** ADAM_STEP)
    c2 = 1.0 / (1.0 - ADAM_B2 ** ADAM_STEP)

    def body(g_ref, w_ref, m_ref, v_ref, go_ref, d_ref, mo_ref, vo_ref):
        gv = g_ref[...]
        mn = ADAM_B1 * m_ref[...] + (1.0 - ADAM_B1) * gv
        vn = ADAM_B2 * v_ref[...] + (1.0 - ADAM_B2) * gv * gv
        go_ref[...] = gv
        mo_ref[...] = mn
        vo_ref[...] = vn
        d_ref[...] = -ADAM_LR * ((mn * c1) / (jnp.sqrt(vn * c2) + ADAM_EPS) + ADAM_WD * w_ref[...])

    blk = pl.BlockSpec((tr, cols), lambda i: (i, 0))
    return _pcall(body, name=name, grid=(n // tr,),
                  in_specs=[pl.BlockSpec((tr, cols), lambda i: (off_blk + i, 0)), blk, blk, blk],
                  out_specs=[blk] * 4, out_shape=[jax.ShapeDtypeStruct((n, cols), F32)] * 4,
                  compiler_params=_params(("parallel",)))(gsrc, wt, m, v)


def _as_rows(a, rows=None):
    flat = a.reshape(-1)
    n = -(-flat.shape[0] // PACK_W) if rows is None else rows
    return jnp.pad(flat, (0, n * PACK_W - flat.shape[0])).reshape(n, PACK_W)


def _shard_of(name, full, k):
    if name == "w_ff1":
        return full[:, :, 1024 * k:1024 * (k + 1)]
    if name == "w_ff2":
        return full[:, 1024 * k:1024 * (k + 1), :]
    if name in ("ssm_w_glu", "kv_w_b"):
        return full[:, 512 * k:512 * (k + 1)]
    if name == "q_w_b":
        return full[:, 384 * k:384 * (k + 1)]
    if name == "ssm_d":
        return full[:, 256 * k:256 * (k + 1)]
    return full[256 * k:256 * (k + 1)]


def _join_shards(name, st):
    if name in ("w_ff1",):
        return jnp.concatenate(list(st), axis=2)
    if name == "w_ff2":
        return jnp.concatenate(list(st), axis=1)
    if name in ("ssm_w_glu", "kv_w_b", "q_w_b", "ssm_d"):
        return jnp.concatenate(list(st), axis=1)
    return jnp.concatenate(list(st), axis=0)


LOCAL_SHAPE = {"w_ff1": (2, 1024, 1024), "w_ff2": (2, 1024, 1024), "ssm_w_glu": (1024, 512), "ssm_w_out": (256, 1024),
               "kv_w_a": (256, 320), "kv_w_b": (256, 512), "q_w_a": (256, 384), "q_w_b": (384, 384),
               "attn_w_o": (256, 1024), "ssm_d": (1, 256)}


def kernel(x, positions, ln_mix_g, ln_mix_b, ln_ffn_g, ln_ffn_b, w_ff1, w_ff2, ssm_lam_re, ssm_lam_im, ssm_log_dt, ssm_b_re, ssm_b_im, ssm_c_re, ssm_c_im, ssm_d, ssm_w_glu, ssm_w_out, kv_w_a, kv_norm_g, kv_w_b, q_w_a, q_norm_g, q_w_b, attn_w_o, loss_target, m_ln_mix_g, m_ln_mix_b, m_ln_ffn_g, m_ln_ffn_b, m_w_ff1, m_w_ff2, m_ssm_lam_re, m_ssm_lam_im, m_ssm_log_dt, m_ssm_b_re, m_ssm_b_im, m_ssm_c_re, m_ssm_c_im, m_ssm_d, m_ssm_w_glu, m_ssm_w_out, m_kv_w_a, m_kv_norm_g, m_kv_w_b, m_q_w_a, m_q_norm_g, m_q_w_b, m_attn_w_o, v_ln_mix_g, v_ln_mix_b, v_ln_ffn_g, v_ln_ffn_b, v_w_ff1, v_w_ff2, v_ssm_lam_re, v_ssm_lam_im, v_ssm_log_dt, v_ssm_b_re, v_ssm_b_im, v_ssm_c_re, v_ssm_c_im, v_ssm_d, v_ssm_w_glu, v_ssm_w_out, v_kv_w_a, v_kv_norm_g, v_kv_w_b, v_q_w_a, v_q_norm_g, v_q_w_b, v_attn_w_o):
    env = dict(locals())
    wl = {n: env[n] for n in WEIGHTS}
    ml = {n: env["m_" + n] for n in WEIGHTS}
    vl = {n: env["v_" + n] for n in WEIGHTS}
    for n in ("ssm_w_glu", "ssm_w_out", "q_w_a", "q_w_b", "attn_w_o"):
        wl[n], ml[n], vl[n] = wl[n][0], ml[n][0], vl[n][0]

    parts = []
    for n in SHARDED:
        if n == "ssm_d":
            parts.append(lax.bitcast_convert_type(wl[n].reshape(-1), BF16).reshape(-1))
        else:
            parts.append(wl[n].astype(BF16).reshape(-1))
    flat = jnp.concatenate(parts)
    wpack = jnp.pad(flat, (0, W_PACK_ROWS * PACK_W - flat.shape[0])).reshape(W_PACK_ROWS, PACK_W)
    gathered = all_gather_chips(wpack, "weight_all_gather").reshape(N_CHIPS, -1)
    full = {}
    pos = 0
    for n in SHARDED:
        shp = LOCAL_SHAPE[n]
        cnt = math.prod(shp) * (2 if n == "ssm_d" else 1)
        seg = gathered[:, pos:pos + cnt]
        pos += cnt
        if n == "ssm_d":
            st = lax.bitcast_convert_type(seg.reshape(N_CHIPS, -1, 2), F32).reshape((N_CHIPS,) + shp)
        else:
            st = seg.reshape((N_CHIPS,) + shp)
        full[n] = _join_shards(n, st)
    for n in REPLICATED:
        full[n] = wl[n]

    loss_part, dx, g = device_step(x[0], positions[0], loss_target[0], full)
    loss = lax.psum(loss_part, ("x", "y", "c"))

    small = jnp.concatenate([_as_rows(g[n]) for n in REPLICATED], axis=0)
    small = jnp.pad(small, ((0, SMALL_ROWS - small.shape[0]), (0, 0)))
    blocks = []
    for k in range(N_CHIPS):
        rows = [_as_rows(_shard_of(n, g[n], k), SHARD_ROWS[n]) for n in SHARDED]
        rows.append(small[SMALL_Q_ROWS * k:SMALL_Q_ROWS * (k + 1)])
        blk = jnp.concatenate(rows, axis=0)
        blocks.append(jnp.pad(blk, ((0, G_PACK_ROWS - blk.shape[0]), (0, 0))))
    gpack = jnp.stack(blocks)
    c_idx = lax.axis_index("c").astype(jnp.int32).reshape(1)
    me_idx = (2 * lax.axis_index("x") + lax.axis_index("y")).astype(jnp.int32).reshape(1)
    chip_part = add_halves(gpack, swap_halves(gpack), c_idx)
    reduced = join_halves(sum_owner(chip_part, send_to_owners(chip_part), me_idx))
    small_tot = all_gather_chips(reduced[SMALL_OFF:SMALL_OFF + SMALL_Q_ROWS], "small_grad_all_gather").reshape(
        SMALL_ROWS, PACK_W)

    out_g, out_d, out_m, out_v = {}, {}, {}, {}
    for n in SHARDED:
        shp = LOCAL_SHAPE[n]
        r = SHARD_ROWS[n]
        res = adamw(reduced, SHARD_OFF[n], _as_rows(wl[n], r), _as_rows(ml[n], r), _as_rows(vl[n], r), "adamw_" + n)
        cnt = math.prod(shp)
        out_g[n], out_d[n], out_m[n], out_v[n] = [a.reshape(-1)[:cnt].reshape(env[n].shape) for a in res]
    ws = jnp.concatenate([_as_rows(wl[n]) for n in REPLICATED], axis=0)
    ms = jnp.concatenate([_as_rows(ml[n]) for n in REPLICATED], axis=0)
    vs = jnp.concatenate([_as_rows(vl[n]) for n in REPLICATED], axis=0)
    pad = ((0, SMALL_ROWS - ws.shape[0]), (0, 0))
    res = adamw(small_tot, 0, jnp.pad(ws, pad), jnp.pad(ms, pad), jnp.pad(vs, pad), "adamw_replicated")
    row = 0
    for n in REPLICATED:
        cnt = math.prod(env[n].shape)
        nrows = -(-cnt // PACK_W)
        out_g[n], out_d[n], out_m[n], out_v[n] = [a[row:row + nrows].reshape(-1)[:cnt].reshape(env[n].shape) for a in res]
        row += nrows

    return (loss, dx[None], *[out_g[n] for n in WEIGHTS], *[out_d[n] for n in WEIGHTS],
            *[out_m[n] for n in WEIGHTS], *[out_v[n] for n in WEIGHTS])
```

```python
import functools
import math

import jax
import jax.numpy as jnp
from jax import lax
from jax.experimental import pallas as pl
from jax.experimental.pallas import tpu as pltpu

F32 = jnp.float32
BF16 = jnp.bfloat16
MESH = pl.DeviceIdType.MESH

D_MODEL = 1024
DEPTH = 2
SSM_GROUP = 16
N_GROUPS = D_MODEL // SSM_GROUP
SSM_STATE = 64
N_STATES = N_GROUPS * SSM_STATE
N_HEADS = 8
QK_NOPE = 128
QK_ROPE = 64
HALF_ROPE = QK_ROPE // 2
V_HEAD = 128
QK_DIM = QK_NOPE + QK_ROPE
Q_LORA = 384
KV_LORA = 256
ROPE_THETA = 10000.0
SM_SCALE = QK_DIM ** -0.5
NEG_INF = -1e30
D_FF = 4 * D_MODEL
DN_ALPHA = (2 * DEPTH) ** 0.25
LN_EPS = 1e-5
RMS_EPS = 1e-6
ADAM_LR = 0.001
ADAM_B1 = 0.9
ADAM_B2 = 0.999
ADAM_EPS = 1e-08
ADAM_WD = 0.01
ADAM_STEP = 10

N_CHIPS = 4
LANES = 128
STATE_TILES = N_STATES // LANES
VMEM_LIMIT = 56 * 1024 * 1024
PACK_W = 1024
KVA_PAD = 384
HALF_W = PACK_W // 2

SHARDED = ("w_ff1", "w_ff2", "ssm_w_glu", "ssm_w_out", "kv_w_a", "kv_w_b", "q_w_a", "q_w_b", "attn_w_o", "ssm_d")
DIRECT_OFF = {"w_ff1": 0, "w_ff2": 2048, "ssm_w_out": 4096, "attn_w_o": 4352}
DIRECT_ROWS = {"w_ff1": 2048, "w_ff2": 2048, "ssm_w_out": 256, "attn_w_o": 256}
MISC_OFF = 4608
SMALL_Q_ROWS = 96
SMALL_ROWS = N_CHIPS * SMALL_Q_ROWS
MISC_SHARDED = ("ssm_d", "ssm_w_glu", "kv_w_b", "kv_w_a", "q_w_a", "q_w_b")
MISC_SHARD_ROWS = {"ssm_d": 16, "ssm_w_glu": 512, "kv_w_b": 128, "kv_w_a": 80, "q_w_a": 96, "q_w_b": 144}
MISC_SHARD_OFF = {}
_o = SMALL_Q_ROWS
for _n in MISC_SHARDED:
    MISC_SHARD_OFF[_n] = _o
    _o += MISC_SHARD_ROWS[_n]
MISC_USED = _o
G_PACK_ROWS = 5760
MISC_ROWS = G_PACK_ROWS - MISC_OFF
G_BLOCK_ROWS = 960
REPLICATED = ("ln_mix_g", "ln_mix_b", "ln_ffn_g", "ln_ffn_b", "ssm_lam_re", "ssm_lam_im", "ssm_log_dt",
              "ssm_b_re", "ssm_b_im", "ssm_c_re", "ssm_c_im", "kv_norm_g", "q_norm_g")
WEIGHTS = ("ln_mix_g", "ln_mix_b", "ln_ffn_g", "ln_ffn_b", "w_ff1", "w_ff2", "ssm_lam_re", "ssm_lam_im",
           "ssm_log_dt", "ssm_b_re", "ssm_b_im", "ssm_c_re", "ssm_c_im", "ssm_d", "ssm_w_glu", "ssm_w_out",
           "kv_w_a", "kv_norm_g", "kv_w_b", "q_w_a", "q_norm_g", "q_w_b", "attn_w_o")


def _pcall(body, **kw):
    return pl.pallas_call(body, **kw)


def _params(sem=None):
    return pltpu.CompilerParams(dimension_semantics=sem, vmem_limit_bytes=VMEM_LIMIT)


_ANY = pl.BlockSpec(memory_space=pl.ANY)


def _tile(dim, prefs):
    for p in prefs:
        if dim % p == 0:
            return p
    return dim


def mm(a, b, *, name, ta=False, tb=False, pro_a=None, epi=None, extras=(), out_dtypes=(F32,), n_dim=None,
       tiles=(None, None, None), b_view=None, out_view=None, into=None):
    if ta:
        k_dim, m_dim = a.shape
    else:
        m_dim, k_dim = a.shape
    if n_dim is None:
        n_dim = b.shape[0] if tb else b.shape[1]
    tm = tiles[0] or _tile(m_dim, (1024, 512, 256, 128))
    tn = tiles[1] or _tile(n_dim, (1024, 512, 256, 128))
    tk = tiles[2] or _tile(k_dim, (512, 256, 128))
    assert m_dim % tm == 0 and n_dim % tn == 0 and k_dim % tk == 0, (name, m_dim, n_dim, k_dim, tm, tn, tk)
    nk = k_dim // tk
    n_ex, n_out = len(extras), len(out_dtypes)
    n_into = 0 if into is None else 1
    dims = (((0 if ta else 1,), (1 if tb else 0,)), ((), ()))

    def body(a_ref, b_ref, *rest):
        ex_refs, out_refs, acc = rest[:n_ex], rest[n_ex + n_into:n_ex + n_into + n_out], rest[-1]
        k = pl.program_id(2)

        @pl.when(k == 0)
        def _():
            acc[...] = jnp.zeros_like(acc)

        av = a_ref[...]
        if pro_a is not None:
            av = pro_a(av)
        acc[...] += lax.dot_general(av.astype(BF16), b_ref[...].astype(BF16), dims, preferred_element_type=F32)

        @pl.when(k == nk - 1)
        def _():
            r = acc[...]
            res = epi(r, *[e[...] for e in ex_refs]) if epi is not None else (r,)
            for o_ref, v in zip(out_refs, res):
                o_ref[...] = v.astype(o_ref.dtype)

    a_spec = pl.BlockSpec((tk, tm), lambda i, j, k: (k, i)) if ta else pl.BlockSpec((tm, tk), lambda i, j, k: (i, k))
    if b_view is not None:
        b_spec = b_view(tk, tn)
    else:
        b_spec = pl.BlockSpec((tn, tk), lambda i, j, k: (j, k)) if tb else pl.BlockSpec((tk, tn), lambda i, j, k: (k, j))
    o_spec = pl.BlockSpec((tm, tn), lambda i, j, k: (i, j))
    if out_view is None:
        out_specs = [o_spec] * n_out
        out_shape = [jax.ShapeDtypeStruct((m_dim, n_dim), dt) for dt in out_dtypes]
    else:
        assert n_out == 1
        out_specs = [out_view[1](tm, tn)]
        out_shape = [jax.ShapeDtypeStruct(out_view[0], out_dtypes[0])]
    outs = _pcall(
        body, name=name, grid=(m_dim // tm, n_dim // tn, nk),
        in_specs=[a_spec, b_spec] + [o_spec] * n_ex + [_ANY] * n_into,
        out_specs=out_specs, out_shape=out_shape,
        input_output_aliases={2 + n_ex: 0} if n_into else {},
        scratch_shapes=[pltpu.VMEM((tm, tn), F32)],
        compiler_params=_params(("parallel", "parallel", "arbitrary")),
    )(a, b, *extras, *([into] if n_into else []))
    return outs[0] if n_out == 1 else outs


def rowwise(fn, ins, outs, *, name, accs=(), tm=256):
    rows = ins[0].shape[0]
    tm = min(tm, rows)
    n_in, n_out, n_acc = len(ins), len(outs), len(accs)

    def body(*refs):
        in_refs, out_refs, acc_refs = refs[:n_in], refs[n_in:n_in + n_out], refs[n_in + n_out:]
        res, sums = fn(*[r[...] for r in in_refs])
        for o_ref, v in zip(out_refs, res):
            o_ref[...] = v.astype(o_ref.dtype)
        if n_acc:
            @pl.when(pl.program_id(0) == 0)
            def _():
                for a_ref in acc_refs:
                    a_ref[...] = jnp.zeros_like(a_ref)

            for a_ref, s in zip(acc_refs, sums):
                a_ref[...] += s

    def spec(arr):
        if arr.shape[0] == rows:
            return pl.BlockSpec((tm, arr.shape[1]), lambda i: (i, 0))
        return pl.BlockSpec(arr.shape, lambda i: (0, 0))

    res = _pcall(
        body, name=name, grid=(rows // tm,),
        in_specs=[spec(a) for a in ins],
        out_specs=[pl.BlockSpec((tm, w), lambda i: (i, 0)) for w, _ in outs]
        + [pl.BlockSpec((1, w), lambda i: (0, 0)) for w in accs],
        out_shape=[jax.ShapeDtypeStruct((rows, w), dt) for w, dt in outs]
        + [jax.ShapeDtypeStruct((1, w), F32) for w in accs],
        compiler_params=_params(("arbitrary",) if n_acc else ("parallel",)),
    )(*ins)
    return res


def _relu2(v):
    r = jnp.maximum(v, 0.0)
    return r * r


def _gelu(x):
    c = math.sqrt(2.0 / math.pi)
    return 0.5 * x * (1.0 + jnp.tanh(c * (x + 0.044715 * x * x * x)))


def _gelu_grad(x):
    c = math.sqrt(2.0 / math.pi)
    t = jnp.tanh(c * (x + 0.044715 * x * x * x))
    return 0.5 * (1.0 + t) + 0.5 * x * (1.0 - t * t) * c * (1.0 + 3 * 0.044715 * x * x)


def _sigmoid(x):
    return 1.0 / (1.0 + jnp.exp(-x))


def ln_fwd(h, mix, g, b, name):
    def fn(h, mix, g, b):
        r = DN_ALPHA * h + mix
        mu = jnp.mean(r, axis=-1, keepdims=True)
        xc = r - mu
        var = jnp.mean(xc * xc, axis=-1, keepdims=True)
        y = xc * lax.rsqrt(var + LN_EPS) * g + b
        return (y, y), ()
    return rowwise(fn, (h, mix, g, b), ((D_MODEL, F32), (D_MODEL, BF16)), name=name)


def ln_bwd(h, mix, g, dy, name):
    def fn(h, mix, g, dy):
        r = DN_ALPHA * h + mix
        mu = jnp.mean(r, axis=-1, keepdims=True)
        xc = r - mu
        var = jnp.mean(xc * xc, axis=-1, keepdims=True)
        rstd = lax.rsqrt(var + LN_EPS)
        xhat = xc * rstd
        dxh = dy * g
        m1 = jnp.mean(dxh, axis=-1, keepdims=True)
        m2 = jnp.mean(dxh * xhat, axis=-1, keepdims=True)
        dr = rstd * (dxh - m1 - xhat * m2)
        return (dr, dr), (jnp.sum(dy * xhat, axis=0, keepdims=True), jnp.sum(dy, axis=0, keepdims=True))
    return rowwise(fn, (h, mix, g, dy), ((D_MODEL, F32), (D_MODEL, BF16)), accs=(D_MODEL, D_MODEL), name=name)


def _rms(x, g):
    r = lax.rsqrt(jnp.mean(x * x, axis=-1, keepdims=True) + RMS_EPS)
    return x * r * g


def _rms_bwd(x, g, dy):
    r = lax.rsqrt(jnp.mean(x * x, axis=-1, keepdims=True) + RMS_EPS)
    xn = x * r
    dyg = dy * g
    dx = r * (dyg - xn * jnp.mean(dyg * xn, axis=-1, keepdims=True))
    return dx, jnp.sum(dy * xn, axis=0, keepdims=True)


def _s5_disc(lr, li, ldt):
    dt = jnp.exp(ldt)
    mag = jnp.exp(lr * dt)
    cs, sn = jnp.cos(li * dt), jnp.sin(li * dt)
    ar, ai = mag * cs, mag * sn
    inv = 1.0 / (lr * lr + li * li)
    n_re = (ar - 1.0) * lr + ai * li
    n_im = ai * lr - (ar - 1.0) * li
    return dt, mag, cs, sn, ar, ai, inv, n_re, n_im


def s5_prep(lr, li, ldt, b_re, b_im):
    def fn(lr, li, ldt, b_re, b_im):
        _, _, _, _, ar, ai, inv, n_re, n_im = _s5_disc(lr, li, ldt)
        cr, ci = n_re * inv, n_im * inv
        return (ar, ai, cr * b_re - ci * b_im, cr * b_im + ci * b_re), ()
    return rowwise(fn, (lr, li, ldt, b_re, b_im), ((1, F32), (1, F32), (SSM_GROUP, F32), (SSM_GROUP, F32)),
                   name="s5_prep", tm=512)


def s5_prep_bwd(lr, li, ldt, b_re, b_im, dar, dai, dbb_re, dbb_im):
    def fn(lr, li, ldt, b_re, b_im, dar, dai, dbb_re, dbb_im):
        dt, mag, cs, sn, ar, ai, inv, n_re, n_im = _s5_disc(lr, li, ldt)
        cr, ci = n_re * inv, n_im * inv
        db_re = cr * dbb_re + ci * dbb_im
        db_im = cr * dbb_im - ci * dbb_re
        dcr = jnp.sum(dbb_re * b_re + dbb_im * b_im, axis=-1, keepdims=True)
        dci = jnp.sum(dbb_im * b_re - dbb_re * b_im, axis=-1, keepdims=True)
        dar = dar + (dcr * lr - dci * li) * inv
        dai = dai + (dcr * li + dci * lr) * inv
        dinv = dcr * n_re + dci * n_im
        dlr = (dcr * (ar - 1.0) + dci * ai) * inv - 2.0 * lr * inv * inv * dinv
        dli = (dcr * ai - dci * (ar - 1.0)) * inv - 2.0 * li * inv * inv * dinv
        dmag = dar * cs + dai * sn
        dth = dai * ar - dar * ai
        dlr = dlr + dmag * mag * dt
        dli = dli + dth * dt
        ddt = dmag * mag * lr + dth * li
        return (dlr, dli, ddt * dt, db_re, db_im), ()
    return rowwise(fn, (lr, li, ldt, b_re, b_im, dar, dai, dbb_re, dbb_im),
                   ((1, F32), (1, F32), (1, F32), (SSM_GROUP, F32), (SSM_GROUP, F32)), name="s5_prep_bwd", tm=512)


def group_sum(x):
    def body(x_ref, o_ref):
        o_ref[...] = jnp.sum(x_ref[...], axis=1)
    return _pcall(body, name="s5_group_sum", out_shape=jax.ShapeDtypeStruct((N_GROUPS, 1), F32))(
        x.reshape(N_GROUPS, SSM_STATE, 1))


GROUPS_PER_TILE = LANES // SSM_GROUP
TILE_STATES = GROUPS_PER_TILE * SSM_STATE
N_UTILES = D_MODEL // LANES
TILES_PER_UTILE = TILE_STATES // LANES


def _store_states(ref, j, val, t_rows):
    for q in range(TILES_PER_UTILE):
        ref[pl.ds(TILES_PER_UTILE * j + q, t_rows, stride=STATE_TILES), :] = val[:, LANES * q:LANES * (q + 1)]


def _load_states(ref, j, t_rows):
    return jnp.concatenate(
        [ref[pl.ds(TILES_PER_UTILE * j + q, t_rows, stride=STATE_TILES), :] for q in range(TILES_PER_UTILE)], axis=1)


def _tok(t):
    return pl.ds(pl.multiple_of(t * STATE_TILES, STATE_TILES), STATE_TILES)


_NT = (((1,), (1,)), ((), ()))
_TN = (((0,), (0,)), ((), ()))


def s5_fwd(u, bbd_re, bbd_im, cbd_re, cbd_imn, a_re, a_im, dskip, t_rows=128):
    seq = u.shape[0]
    t_rows = min(t_rows, seq)

    def body(u_ref, bre, bim, cre, cimn, are, aim, d_ref, y_ref, hre_ref, him_ref, car_re, car_im):
        @pl.when(pl.program_id(0) == 0)
        def _():
            car_re[...] = jnp.zeros_like(car_re)
            car_im[...] = jnp.zeros_like(car_im)

        uf = u_ref[...]
        ub = uf.astype(BF16)
        for j in range(N_UTILES):
            uj = ub[:, LANES * j:LANES * (j + 1)]
            _store_states(hre_ref, j, jnp.dot(uj, bre[j], preferred_element_type=F32), t_rows)
            _store_states(him_ref, j, jnp.dot(uj, bim[j], preferred_element_type=F32), t_rows)
        ar, ai = are[...], aim[...]

        def step(t, carry):
            hr, hi = carry
            rows = _tok(t)
            nr = ar * hr - ai * hi + hre_ref[rows, :]
            ni = ar * hi + ai * hr + him_ref[rows, :]
            hre_ref[rows, :] = nr
            him_ref[rows, :] = ni
            return nr, ni

        hr, hi = lax.fori_loop(0, t_rows, step, (car_re[...], car_im[...]))
        car_re[...] = hr
        car_im[...] = hi
        dv = d_ref[...]
        for j in range(N_UTILES):
            hrj = _load_states(hre_ref, j, t_rows).astype(BF16)
            hij = _load_states(him_ref, j, t_rows).astype(BF16)
            yj = jnp.dot(hrj, cre[j], preferred_element_type=F32) + jnp.dot(hij, cimn[j], preferred_element_type=F32)
            sl = slice(LANES * j, LANES * (j + 1))
            y_ref[:, sl] = yj + dv[:, sl] * uf[:, sl]

    full3 = lambda a: pl.BlockSpec(a.shape, lambda i: (0, 0, 0))
    full2 = lambda a: pl.BlockSpec(a.shape, lambda i: (0, 0))
    return _pcall(
        body, name="s5_fwd", grid=(seq // t_rows,),
        in_specs=[pl.BlockSpec((t_rows, D_MODEL), lambda i: (i, 0)), full3(bbd_re), full3(bbd_im), full3(cbd_re),
                  full3(cbd_imn), full2(a_re), full2(a_im), full2(dskip)],
        out_specs=[pl.BlockSpec((t_rows, D_MODEL), lambda i: (i, 0)),
                   pl.BlockSpec((t_rows * STATE_TILES, LANES), lambda i: (i, 0)),
                   pl.BlockSpec((t_rows * STATE_TILES, LANES), lambda i: (i, 0))],
        out_shape=[jax.ShapeDtypeStruct((seq, D_MODEL), F32),
                   jax.ShapeDtypeStruct((seq * STATE_TILES, LANES), F32),
                   jax.ShapeDtypeStruct((seq * STATE_TILES, LANES), F32)],
        scratch_shapes=[pltpu.VMEM((STATE_TILES, LANES), F32), pltpu.VMEM((STATE_TILES, LANES), F32)],
        compiler_params=_params(("arbitrary",)),
    )(u, bbd_re, bbd_im, cbd_re, cbd_imn, a_re, a_im, dskip)


def s5_bwd(dy, u, dres, h_re, h_im, bbd_re, bbd_im, cbd_re, cbd_imn, a_re, a_im, dskip, t_rows=128):
    seq = u.shape[0]
    t_rows = min(t_rows, seq)
    n_chunks = seq // t_rows

    def body(dy_ref, u_ref, dres_ref, hre_ref, him_ref, hpre_ref, hpim_ref, bre, bim, cre, cimn, are, aim, d_ref,
             dx_ref, dbre, dbim, dcre, dcimn, dar_ref, dai_ref, dd_ref, lre, lim, car_re, car_im):
        i = pl.program_id(0)

        @pl.when(i == 0)
        def _():
            for r in (car_re, car_im, dbre, dbim, dcre, dcimn, dar_ref, dai_ref, dd_ref):
                r[...] = jnp.zeros_like(r)

        dyf = dy_ref[...]
        dyb = dyf.astype(BF16)
        uf = u_ref[...]
        ub = uf.astype(BF16)
        for j in range(N_UTILES):
            dyj = dyb[:, LANES * j:LANES * (j + 1)]
            _store_states(lre, j, lax.dot_general(dyj, cre[j], _NT, preferred_element_type=F32), t_rows)
            _store_states(lim, j, lax.dot_general(dyj, cimn[j], _NT, preferred_element_type=F32), t_rows)
        ar, ai = are[...], aim[...]

        def adjoint(t, lr, li):
            rows = _tok(t)
            nr = ar * lr + ai * li + lre[rows, :]
            ni = ar * li - ai * lr + lim[rows, :]
            lre[rows, :] = nr
            lim[rows, :] = ni
            return nr, ni

        def step(k, carry):
            lr, li, dar, dai = carry
            t = t_rows - 1 - k
            nr, ni = adjoint(t, lr, li)
            prev = _tok(t - 1)
            hpr, hpi = hre_ref[prev, :], him_ref[prev, :]
            return nr, ni, dar + nr * hpr + ni * hpi, dai + ni * hpr - nr * hpi

        zero = jnp.zeros((STATE_TILES, LANES), F32)
        lr, li, dar, dai = lax.fori_loop(0, t_rows - 1, step, (car_re[...], car_im[...], zero, zero))
        nr, ni = adjoint(0, lr, li)
        first = (i == n_chunks - 1).astype(F32)
        hpr = hpre_ref[...] * (1.0 - first)
        hpi = hpim_ref[...] * (1.0 - first)
        car_re[...] = nr
        car_im[...] = ni
        dar_ref[...] += dar + nr * hpr + ni * hpi
        dai_ref[...] += dai + ni * hpr - nr * hpi

        dv = d_ref[...]
        for j in range(N_UTILES):
            sl = slice(LANES * j, LANES * (j + 1))
            lrj = _load_states(lre, j, t_rows).astype(BF16)
            lij = _load_states(lim, j, t_rows).astype(BF16)
            du = (lax.dot_general(lrj, bre[j], _NT, preferred_element_type=F32)
                  + lax.dot_general(lij, bim[j], _NT, preferred_element_type=F32))
            dx_ref[:, sl] = du + dv[:, sl] * dyf[:, sl] + DN_ALPHA * dres_ref[:, sl]
            uj = ub[:, sl]
            dbre[j] += lax.dot_general(uj, lrj, _TN, preferred_element_type=F32)
            dbim[j] += lax.dot_general(uj, lij, _TN, preferred_element_type=F32)
            hrj = _load_states(hre_ref, j, t_rows).astype(BF16)
            hij = _load_states(him_ref, j, t_rows).astype(BF16)
            dyj = dyb[:, sl]
            dcre[j] += lax.dot_general(hrj, dyj, _TN, preferred_element_type=F32)
            dcimn[j] += lax.dot_general(hij, dyj, _TN, preferred_element_type=F32)
        dd_ref[...] += jnp.sum(dyf * uf, axis=0, keepdims=True)

    rev = lambda i: (n_chunks - 1 - i, 0)
    prev_tok = lambda i: (jnp.maximum((n_chunks - 1 - i) * t_rows - 1, 0), 0)
    full3 = lambda a: pl.BlockSpec(a.shape, lambda i: (0, 0, 0))
    full2 = lambda a: pl.BlockSpec(a.shape, lambda i: (0, 0))
    acc3 = lambda shape: pl.BlockSpec(shape, lambda i: (0, 0, 0))
    acc2 = lambda shape: pl.BlockSpec(shape, lambda i: (0, 0))
    st = (STATE_TILES, LANES)
    return _pcall(
        body, name="s5_bwd", grid=(n_chunks,),
        in_specs=[pl.BlockSpec((t_rows, D_MODEL), rev), pl.BlockSpec((t_rows, D_MODEL), rev),
                  pl.BlockSpec((t_rows, D_MODEL), rev),
                  pl.BlockSpec((t_rows * STATE_TILES, LANES), rev), pl.BlockSpec((t_rows * STATE_TILES, LANES), rev),
                  pl.BlockSpec(st, prev_tok), pl.BlockSpec(st, prev_tok),
                  full3(bbd_re), full3(bbd_im), full3(cbd_re), full3(cbd_imn), full2(a_re), full2(a_im), full2(dskip)],
        out_specs=[pl.BlockSpec((t_rows, D_MODEL), rev), acc3(bbd_re.shape), acc3(bbd_im.shape), acc3(cbd_re.shape),
                   acc3(cbd_imn.shape), acc2(st), acc2(st), acc2((1, D_MODEL))],
        out_shape=[jax.ShapeDtypeStruct((seq, D_MODEL), F32), jax.ShapeDtypeStruct(bbd_re.shape, F32),
                   jax.ShapeDtypeStruct(bbd_im.shape, F32), jax.ShapeDtypeStruct(cbd_re.shape, F32),
                   jax.ShapeDtypeStruct(cbd_imn.shape, F32), jax.ShapeDtypeStruct(st, F32),
                   jax.ShapeDtypeStruct(st, F32), jax.ShapeDtypeStruct((1, D_MODEL), F32)],
        scratch_shapes=[pltpu.VMEM((t_rows * STATE_TILES, LANES), F32), pltpu.VMEM((t_rows * STATE_TILES, LANES), F32),
                        pltpu.VMEM(st, F32), pltpu.VMEM(st, F32)],
        compiler_params=_params(("arbitrary",)),
    )(dy, u, dres, h_re, h_im, h_re, h_im, bbd_re, bbd_im, cbd_re, cbd_imn, a_re, a_im, dskip)


def _eye_groups():
    return jnp.eye(GROUPS_PER_TILE, dtype=F32)


def _blockdiag_in(bb):
    t = bb.transpose(0, 2, 1).reshape(N_UTILES, GROUPS_PER_TILE, SSM_GROUP, SSM_STATE)
    bd = jnp.einsum("jgcp,gh->jgchp", t, _eye_groups())
    return bd.reshape(N_UTILES, LANES, TILE_STATES)


def _blockdiag_in_t(d):
    t = jnp.einsum("jgchp,gh->jgcp", d.reshape(N_UTILES, GROUPS_PER_TILE, SSM_GROUP, GROUPS_PER_TILE, SSM_STATE),
                   _eye_groups())
    return t.reshape(N_GROUPS, SSM_GROUP, SSM_STATE).transpose(0, 2, 1)


def _blockdiag_out(c):
    t = c.transpose(0, 2, 1).reshape(N_UTILES, GROUPS_PER_TILE, SSM_STATE, SSM_GROUP)
    bd = jnp.einsum("jhpc,hg->jhpgc", t, _eye_groups())
    return bd.reshape(N_UTILES, TILE_STATES, LANES)


def _blockdiag_out_t(d):
    t = jnp.einsum("jhpgc,hg->jhpc", d.reshape(N_UTILES, GROUPS_PER_TILE, SSM_STATE, GROUPS_PER_TILE, SSM_GROUP),
                   _eye_groups())
    return t.reshape(N_GROUPS, SSM_STATE, SSM_GROUP).transpose(0, 2, 1)


def _causal(s, transposed=False):
    r = lax.broadcasted_iota(jnp.int32, s.shape, 0)
    c = lax.broadcasted_iota(jnp.int32, s.shape, 1)
    keep = (r <= c) if transposed else (c <= r)
    return jnp.where(keep, s, NEG_INF)


def attn_fwd(q, k, v, tq=512):
    n_heads, seq, _ = q.shape
    tq = min(tq, seq)

    def body(q_ref, k_ref, v_ref, o_ref, lse_ref):
        qi = pl.program_id(1)
        qv = q_ref[0]

        def block(j, carry, diag):
            m, l, acc = carry
            rows = pl.ds(pl.multiple_of(j * tq, tq), tq)
            s = lax.dot_general(qv, k_ref[0, rows, :], _NT, preferred_element_type=F32) * SM_SCALE
            if diag:
                s = _causal(s)
            m_new = jnp.maximum(m, jnp.max(s, axis=-1, keepdims=True))
            p = jnp.exp(s - m_new)
            corr = jnp.exp(m - m_new)
            l = l * corr + jnp.sum(p, axis=-1, keepdims=True)
            acc = acc * corr + jnp.dot(p.astype(BF16), v_ref[rows, :], preferred_element_type=F32)
            return m_new, l, acc

        init = (jnp.full((tq, 1), NEG_INF, F32), jnp.zeros((tq, 1), F32), jnp.zeros((tq, V_HEAD), F32))
        carry = lax.fori_loop(0, qi, lambda j, c: block(j, c, False), init)
        m, l, acc = block(qi, carry, True)
        o_ref[...] = acc / l
        lse_ref[0] = jnp.broadcast_to(m + jnp.log(l), (tq, LANES))

    return _pcall(
        body, name="attn_fwd", grid=(n_heads, seq // tq),
        in_specs=[pl.BlockSpec((1, tq, QK_DIM), lambda h, i: (h, i, 0)),
                  pl.BlockSpec((1, seq, QK_DIM), lambda h, i: (h, 0, 0)),
                  pl.BlockSpec((seq, V_HEAD), lambda h, i: (0, h))],
        out_specs=[pl.BlockSpec((tq, V_HEAD), lambda h, i: (i, h)),
                   pl.BlockSpec((1, tq, LANES), lambda h, i: (h, i, 0))],
        out_shape=[jax.ShapeDtypeStruct((seq, n_heads * V_HEAD), F32),
                   jax.ShapeDtypeStruct((n_heads, seq, LANES), F32)],
        compiler_params=_params(("parallel", "parallel")),
    )(q, k, v)


def attn_bwd_dq(q, k, v, do, o, lse, tq=512):
    n_heads, seq, _ = q.shape
    tq = min(tq, seq)

    def body(q_ref, k_ref, v_ref, do_ref, o_ref, lse_ref, dqn_ref, dqr_ref, delta_ref):
        qi = pl.program_id(1)
        qv = q_ref[0]
        dof = do_ref[...]
        dob = dof.astype(BF16)
        delta = jnp.sum(dof * o_ref[...], axis=-1, keepdims=True)
        lse = lse_ref[0][:, :1]

        def block(j, dq, diag):
            rows = pl.ds(pl.multiple_of(j * tq, tq), tq)
            kv = k_ref[0, rows, :]
            s = lax.dot_general(qv, kv, _NT, preferred_element_type=F32) * SM_SCALE
            if diag:
                s = _causal(s)
            p = jnp.exp(s - lse)
            dp = lax.dot_general(dob, v_ref[rows, :], _NT, preferred_element_type=F32)
            ds = p * (dp - delta) * SM_SCALE
            return dq + jnp.dot(ds.astype(BF16), kv, preferred_element_type=F32)

        dq = lax.fori_loop(0, qi, lambda j, c: block(j, c, False), jnp.zeros((tq, QK_DIM), F32))
        dq = block(qi, dq, True)
        dqn_ref[...] = dq[:, :QK_NOPE]
        dqr_ref[0] = dq[:, QK_NOPE:]
        delta_ref[0] = jnp.broadcast_to(delta, (tq, LANES))

    return _pcall(
        body, name="attn_bwd_dq", grid=(n_heads, seq // tq),
        in_specs=[pl.BlockSpec((1, tq, QK_DIM), lambda h, i: (h, i, 0)),
                  pl.BlockSpec((1, seq, QK_DIM), lambda h, i: (h, 0, 0)),
                  pl.BlockSpec((seq, V_HEAD), lambda h, i: (0, h)),
                  pl.BlockSpec((tq, V_HEAD), lambda h, i: (i, h)),
                  pl.BlockSpec((tq, V_HEAD), lambda h, i: (i, h)),
                  pl.BlockSpec((1, tq, LANES), lambda h, i: (h, i, 0))],
        out_specs=[pl.BlockSpec((tq, QK_NOPE), lambda h, i: (i, h)),
                   pl.BlockSpec((1, tq, QK_ROPE), lambda h, i: (h, i, 0)),
                   pl.BlockSpec((1, tq, LANES), lambda h, i: (h, i, 0))],
        out_shape=[jax.ShapeDtypeStruct((seq, n_heads * QK_NOPE), F32),
                   jax.ShapeDtypeStruct((n_heads, seq, QK_ROPE), F32),
                   jax.ShapeDtypeStruct((n_heads, seq, LANES), F32)],
        compiler_params=_params(("parallel", "parallel")),
    )(q, k, v, do, o, lse)


def attn_bwd_dkv(q, k, v, do, lse_row, delta_row, tq=512):
    n_heads, seq, _ = q.shape
    tq = min(tq, seq)
    n_blk = seq // tq

    def body(q_ref, k_ref, v_ref, do_ref, lse_ref, delta_ref, dkn_ref, dkr_ref, dv_ref):
        kj = pl.program_id(1)
        kv = k_ref[0]
        vv = v_ref[...]

        def block(i, carry, diag):
            dk, dv = carry
            rows = pl.ds(pl.multiple_of(i * tq, tq), tq)
            qv = q_ref[0, rows, :]
            st = lax.dot_general(kv, qv, _NT, preferred_element_type=F32) * SM_SCALE
            if diag:
                st = _causal(st, transposed=True)
            pt = jnp.exp(st - lse_ref[0, pl.ds(i, 1), :])
            dob = do_ref[rows, :].astype(BF16)
            dv = dv + jnp.dot(pt.astype(BF16), dob, preferred_element_type=F32)
            dpt = lax.dot_general(vv, dob, _NT, preferred_element_type=F32)
            dst = pt * (dpt - delta_ref[0, pl.ds(i, 1), :]) * SM_SCALE
            dk = dk + jnp.dot(dst.astype(BF16), qv, preferred_element_type=F32)
            return dk, dv

        carry = block(kj, (jnp.zeros((tq, QK_DIM), F32), jnp.zeros((tq, V_HEAD), F32)), True)
        dk, dv = lax.fori_loop(kj + 1, n_blk, lambda i, c: block(i, c, False), carry)
        dkn_ref[...] = dk[:, :QK_NOPE]
        dkr_ref[0] = dk[:, QK_NOPE:]
        dv_ref[...] = dv

    return _pcall(
        body, name="attn_bwd_dkv", grid=(n_heads, n_blk),
        in_specs=[pl.BlockSpec((1, seq, QK_DIM), lambda h, j: (h, 0, 0)),
                  pl.BlockSpec((1, tq, QK_DIM), lambda h, j: (h, j, 0)),
                  pl.BlockSpec((tq, V_HEAD), lambda h, j: (j, h)),
                  pl.BlockSpec((seq, V_HEAD), lambda h, j: (0, h)),
                  pl.BlockSpec((1, n_blk, tq), lambda h, j: (h, 0, 0)),
                  pl.BlockSpec((1, n_blk, tq), lambda h, j: (h, 0, 0))],
        out_specs=[pl.BlockSpec((tq, QK_NOPE), lambda h, j: (j, h)),
                   pl.BlockSpec((1, tq, QK_ROPE), lambda h, j: (h, j, 0)),
                   pl.BlockSpec((tq, V_HEAD), lambda h, j: (j, h))],
        out_shape=[jax.ShapeDtypeStruct((seq, n_heads * QK_NOPE), F32),
                   jax.ShapeDtypeStruct((n_heads, seq, QK_ROPE), F32),
                   jax.ShapeDtypeStruct((seq, n_heads * V_HEAD), F32)],
        compiler_params=_params(("parallel", "parallel")),
    )(q, k, v, do, lse_row, delta_row)


def head_sum(x, ts=512):
    n_heads, seq, w = x.shape
    ts = min(ts, seq)

    def body(x_ref, o_ref):
        o_ref[...] = jnp.sum(x_ref[...], axis=0)

    return _pcall(body, name="head_sum", grid=(seq // ts,),
                  in_specs=[pl.BlockSpec((n_heads, ts, w), lambda i: (0, i, 0))],
                  out_specs=pl.BlockSpec((ts, w), lambda i: (i, 0)),
                  out_shape=jax.ShapeDtypeStruct((seq, w), F32),
                  compiler_params=_params(("parallel",)))(x)


HEADS_PER_CHIP = N_HEADS // N_CHIPS
Q_CHIP = HEADS_PER_CHIP * QK_DIM
Q_CHIP_NOPE = HEADS_PER_CHIP * QK_NOPE


def _perm_q_cols(w):
    t = w.reshape(w.shape[0], HEADS_PER_CHIP, QK_DIM)
    return jnp.concatenate([t[:, :, :QK_NOPE].reshape(w.shape[0], -1),
                            t[:, :, QK_NOPE:QK_NOPE + HALF_ROPE].reshape(w.shape[0], -1),
                            t[:, :, QK_NOPE + HALF_ROPE:].reshape(w.shape[0], -1)], axis=1)


def _unperm_q_cols(w):
    r = w.shape[0]
    nope = w[:, :Q_CHIP_NOPE].reshape(r, HEADS_PER_CHIP, QK_NOPE)
    r1 = w[:, Q_CHIP_NOPE:Q_CHIP_NOPE + QK_ROPE].reshape(r, HEADS_PER_CHIP, HALF_ROPE)
    r2 = w[:, Q_CHIP_NOPE + QK_ROPE:].reshape(r, HEADS_PER_CHIP, HALF_ROPE)
    return jnp.concatenate([nope, r1, r2], axis=2).reshape(r, Q_CHIP)


def _pad_kva_cols(w):
    z = jnp.zeros((w.shape[0], HALF_ROPE), w.dtype)
    return jnp.concatenate([w[:, :KV_LORA], w[:, KV_LORA:KV_LORA + HALF_ROPE], z, w[:, KV_LORA + HALF_ROPE:], z], axis=1)


def _unpad_kva_cols(w):
    return jnp.concatenate([w[:, :KV_LORA], w[:, KV_LORA:KV_LORA + HALF_ROPE],
                            w[:, KV_LORA + QK_ROPE:KV_LORA + QK_ROPE + HALF_ROPE]], axis=1)


def _rope_tile(t, cs, sn):
    return t * cs + pltpu.roll(t, LANES // 2, 1) * sn


def _rope_tile_bwd(d, cs, sn):
    return d * cs + pltpu.roll(d * sn, LANES // 2, 1)


def _b_cols(tk, tn):
    return pl.BlockSpec((None, tk, tn), lambda i, j, k: (j, k, 0))


def _b_cols_t(tk, tn):
    return pl.BlockSpec((None, tn, tk), lambda i, j, k: (k, j, 0))


def _out_cols(shape):
    return shape, lambda tm, tn: pl.BlockSpec((None, tm, tn), lambda i, j, k: (j, i, 0))


def device_step(x, positions, target, w):
    seq = x.shape[0]

    inv_freq = ROPE_THETA ** (-jnp.arange(HALF_ROPE, dtype=F32) / HALF_ROPE)
    ang = positions.astype(F32)[:, None] * inv_freq
    cos, sin = jnp.cos(ang), jnp.sin(ang)
    zero = jnp.zeros_like(cos)
    cos_q, sin_q = jnp.concatenate([cos] * 4, 1), jnp.concatenate([-sin, -sin, sin, sin], 1)
    cos_k, sin_k = jnp.concatenate([cos, zero, cos, zero], 1), jnp.concatenate([-sin, zero, sin, zero], 1)
    w1, w2 = w["w_ff1"], w["w_ff2"]
    ff_tile = D_FF // N_CHIPS
    pack_shape = (N_CHIPS, G_PACK_ROWS, PACK_W)

    lr = w["ssm_lam_re"].reshape(N_STATES, 1)
    li = w["ssm_lam_im"].reshape(N_STATES, 1)
    ldt = jnp.repeat(w["ssm_log_dt"].reshape(N_GROUPS), SSM_STATE).reshape(N_STATES, 1)
    b_re = w["ssm_b_re"].reshape(N_STATES, SSM_GROUP)
    b_im = w["ssm_b_im"].reshape(N_STATES, SSM_GROUP)
    a_re, a_im, bb_re, bb_im = s5_prep(lr, li, ldt, b_re, b_im)
    a_re, a_im = a_re.reshape(STATE_TILES, LANES), a_im.reshape(STATE_TILES, LANES)
    bbd_re = _blockdiag_in(bb_re.reshape(N_GROUPS, SSM_STATE, SSM_GROUP)).astype(BF16)
    bbd_im = _blockdiag_in(bb_im.reshape(N_GROUPS, SSM_STATE, SSM_GROUP)).astype(BF16)
    cbd_re = _blockdiag_out(w["ssm_c_re"].reshape(N_GROUPS, SSM_GROUP, SSM_STATE)).astype(BF16)
    cbd_imn = _blockdiag_out(-w["ssm_c_im"].reshape(N_GROUPS, SSM_GROUP, SSM_STATE)).astype(BF16)
    dskip = w["ssm_d"].reshape(1, D_MODEL)
    ypre, h_re, h_im = s5_fwd(x, bbd_re, bbd_im, cbd_re, cbd_imn, a_re, a_im, dskip)
    (yg,) = rowwise(lambda y: ((_gelu(y),), ()), (ypre,), ((D_MODEL, BF16),), name="gelu")
    w_glu = w["ssm_w_glu"]
    glu_tile = w_glu.shape[2]
    vg = mm(yg, w_glu, n_dim=2 * D_MODEL, tiles=(None, glu_tile, None), b_view=_b_cols, name="glu_proj")

    def glu(v):
        return (v[:, :D_MODEL] * _sigmoid(v[:, D_MODEL:]),), ()
    (z,) = rowwise(glu, (vg,), ((D_MODEL, BF16),), name="glu")
    w_out = w["ssm_w_out"].reshape(D_MODEL, D_MODEL)
    mix0 = mm(z, w_out, name="ssm_out")

    def mlp_fwd(hb, layer):
        pre = mm(hb, w1, n_dim=D_FF, tiles=(None, ff_tile, None), name=f"ff1_{layer}",
                 b_view=lambda tk, tn: pl.BlockSpec((None, None, tk, tn), lambda i, j, k: (j, layer, k, 0)))
        f = mm(pre, w2, pro_a=_relu2, n_dim=D_MODEL, tiles=(None, D_MODEL, None), name=f"ff2_{layer}",
               b_view=lambda tk, tn: pl.BlockSpec((None, None, tk, tn),
                                                  lambda i, j, k: (k // (ff_tile // tk), layer, k % (ff_tile // tk), j)))
        return pre, f

    ln = lambda name, l: w[name][l].reshape(1, D_MODEL)
    h1, h1b = ln_fwd(x, mix0, ln("ln_mix_g", 0), ln("ln_mix_b", 0), "ln_mix_0")
    f1pre, f1 = mlp_fwd(h1b, 0)
    h2, h2b = ln_fwd(h1, f1, ln("ln_ffn_g", 0), ln("ln_ffn_b", 0), "ln_ffn_0")

    kv_w_a = w["kv_w_a"].reshape(D_MODEL, KVA_PAD)
    kv_w_b = w["kv_w_b"]
    q_w_a = w["q_w_a"].reshape(D_MODEL, Q_LORA)
    q_w_b = w["q_w_b"]
    w_o = w["attn_w_o"].reshape(D_MODEL, D_MODEL)
    kvb_tile = kv_w_b.shape[2]
    kvn_g = w["kv_norm_g"].reshape(1, KV_LORA)
    qn_g = w["q_norm_g"].reshape(1, Q_LORA)
    kva = mm(h2b, kv_w_a, name="kv_a")

    def kv_post(kva, g, cs, sn):
        return (_rms(kva[:, :KV_LORA], g), _rope_tile(kva[:, KV_LORA:], cs, sn)), ()
    ckv, krope = rowwise(kv_post, (kva, kvn_g, cos_k, sin_k), ((KV_LORA, BF16), (LANES, BF16)), name="kv_post")
    kvb = mm(ckv, kv_w_b, n_dim=N_CHIPS * kvb_tile, tiles=(None, kvb_tile, KV_LORA), b_view=_b_cols, name="kv_b",
             out_dtypes=(BF16,))
    cq_raw = mm(h2b, q_w_a, name="q_a")
    (cq,) = rowwise(lambda c, g: ((_rms(c, g),), ()), (cq_raw, qn_g), ((Q_LORA, BF16),), name="q_norm")
    qlin = mm(cq, q_w_b, n_dim=N_CHIPS * Q_CHIP, tiles=(None, Q_CHIP, Q_LORA), b_view=_b_cols, name="q_b")

    def on_rope_tiles(fn):
        def apply(q, cs, sn):
            parts = []
            for k in range(N_CHIPS):
                parts.append(q[:, Q_CHIP * k:Q_CHIP * k + Q_CHIP_NOPE])
                parts.append(fn(q[:, Q_CHIP * k + Q_CHIP_NOPE:Q_CHIP * (k + 1)], cs, sn))
            return (jnp.concatenate(parts, axis=1),), ()
        return apply
    (qro,) = rowwise(on_rope_tiles(_rope_tile), (qlin, cos_q, sin_q), ((N_CHIPS * Q_CHIP, BF16),), name="q_rope")
    qro3 = qro.reshape(seq, N_CHIPS, Q_CHIP)
    q_h = jnp.concatenate([qro3[:, :, :Q_CHIP_NOPE].reshape(seq, N_HEADS, QK_NOPE),
                           qro3[:, :, Q_CHIP_NOPE:Q_CHIP_NOPE + QK_ROPE].reshape(seq, N_HEADS, HALF_ROPE),
                           qro3[:, :, Q_CHIP_NOPE + QK_ROPE:].reshape(seq, N_HEADS, HALF_ROPE)], axis=2).transpose(1, 0, 2)
    kvb3 = kvb.reshape(seq, N_HEADS, QK_NOPE + V_HEAD)
    kr = jnp.concatenate([krope[:, :HALF_ROPE], krope[:, QK_ROPE:QK_ROPE + HALF_ROPE]], axis=1)
    k_h = jnp.concatenate([kvb3[:, :, :QK_NOPE], jnp.broadcast_to(kr[:, None, :], (seq, N_HEADS, QK_ROPE))],
                          axis=2).transpose(1, 0, 2)
    v2 = kvb3[:, :, QK_NOPE:].reshape(seq, N_HEADS * V_HEAD)
    o, lse = attn_fwd(q_h, k_h, v2)
    mix1 = mm(o, w_o, name="attn_out")
    h3, h3b = ln_fwd(h2, mix1, ln("ln_mix_g", 1), ln("ln_mix_b", 1), "ln_mix_1")
    f2pre, f2 = mlp_fwd(h3b, 1)
    h4, _ = ln_fwd(h3, f2, ln("ln_ffn_g", 1), ln("ln_ffn_b", 1), "ln_ffn_1")

    def loss_fn(y, t):
        e = y - t
        return (e * (1.0 / D_MODEL),), (jnp.broadcast_to(jnp.sum(e * e), (1, LANES)),)
    dh4, loss_acc = rowwise(loss_fn, (h4, target), ((D_MODEL, F32),), accs=(LANES,), name="loss")
    loss = loss_acc[0, 0] * (0.5 / D_MODEL)

    g = {}

    def into_rows(off, rows_per_chip):
        def view(tm, tn):
            nb = rows_per_chip // tm
            return pl.BlockSpec((None, tm, tn), lambda i, j, k: (i // nb, off // tm + i % nb, 0))
        return pack_shape, view

    def into_cols(off):
        return pack_shape, lambda tm, tn: pl.BlockSpec((None, tm, tn), lambda i, j, k: (j, off // tm + i, 0))

    def mlp_bwd(pack, dr, drb, hb, pre, layer):
        nb = lambda tk: ff_tile // tk
        dpre = mm(drb, w2, tb=True, epi=lambda r, p: (r * 2.0 * jnp.maximum(p, 0.0),), extras=(pre,),
                  out_dtypes=(BF16,), n_dim=D_FF, tiles=(None, ff_tile, None), name=f"ff2_dx_{layer}",
                  b_view=lambda tk, tn: pl.BlockSpec((None, None, tn, tk), lambda i, j, k: (j, layer, 0, k)))
        pack = mm(pre, drb, ta=True, pro_a=_relu2, name=f"ff2_dw_{layer}", tiles=(None, PACK_W, None), into=pack,
                  out_view=into_rows(DIRECT_OFF["w_ff2"] + layer * ff_tile, ff_tile))
        pack = mm(hb, dpre, ta=True, name=f"ff1_dw_{layer}", tiles=(None, PACK_W, None), into=pack,
                  out_view=into_cols(DIRECT_OFF["w_ff1"] + layer * D_MODEL))
        dh = mm(dpre, w1, tb=True, epi=lambda r, d: (r + DN_ALPHA * d,), extras=(dr,), n_dim=D_MODEL,
                tiles=(None, D_MODEL, None), name=f"ff1_dx_{layer}",
                b_view=lambda tk, tn: pl.BlockSpec((None, None, tn, tk), lambda i, j, k: (k // nb(tk), layer, 0, k % nb(tk))))
        return pack, dh

    dr4, dr4b, dg_f1, db_f1 = ln_bwd(h3, f2, ln("ln_ffn_g", 1), dh4, "ln_ffn_bwd_1")
    pack, dh3 = mlp_bwd(None, dr4, dr4b, h3b, f2pre, 1)
    dr3, dr3b, dg_m1, db_m1 = ln_bwd(h2, mix1, ln("ln_mix_g", 1), dh3, "ln_mix_bwd_1")
    shard_rows = D_MODEL // N_CHIPS
    pack = mm(o, dr3b, ta=True, name="attn_out_dw", tiles=(shard_rows, PACK_W, None), into=pack,
              out_view=into_rows(DIRECT_OFF["attn_w_o"], shard_rows))
    do = mm(dr3b, w_o, tb=True, name="attn_out_dx")
    dqn, dqr, delta = attn_bwd_dq(q_h, k_h, v2, do, o, lse)
    tq = min(512, seq)
    lse_row = lse[:, :, 0].reshape(N_HEADS, seq // tq, tq)
    delta_row = delta[:, :, 0].reshape(N_HEADS, seq // tq, tq)
    dkn, dkr, dv = attn_bwd_dkv(q_h, k_h, v2, do, lse_row, delta_row)
    dqr_t = dqr.transpose(1, 0, 2)
    dq_cat = jnp.concatenate([dqn.reshape(seq, N_CHIPS, Q_CHIP_NOPE), dqr_t[:, :, :HALF_ROPE].reshape(seq, N_CHIPS, QK_ROPE),
                              dqr_t[:, :, HALF_ROPE:].reshape(seq, N_CHIPS, QK_ROPE)], 2).reshape(seq, N_CHIPS * Q_CHIP)
    (dqlin,) = rowwise(on_rope_tiles(_rope_tile_bwd), (dq_cat, cos_q, sin_q), ((N_CHIPS * Q_CHIP, BF16),), name="q_rope_bwd")
    g["q_w_b"] = mm(cq, dqlin, ta=True, name="q_b_dw", tiles=(Q_LORA, Q_CHIP, None), out_view=_out_cols(q_w_b.shape))
    dcq = mm(dqlin, q_w_b, tb=True, n_dim=Q_LORA, tiles=(None, Q_LORA, Q_CHIP), b_view=_b_cols_t, name="q_b_dx")

    def q_norm_bwd(c, gq, d):
        dx, dgq = _rms_bwd(c, gq, d)
        return (dx,), (dgq,)
    dcq_raw, dqn_g = rowwise(q_norm_bwd, (cq_raw, qn_g, dcq), ((Q_LORA, BF16),), accs=(Q_LORA,), name="q_norm_bwd")
    g["q_w_a"] = mm(h2b, dcq_raw, ta=True, name="q_a_dw")
    dkvb = jnp.concatenate([dkn.reshape(seq, N_HEADS, QK_NOPE), dv.reshape(seq, N_HEADS, V_HEAD)], 2).reshape(
        seq, N_HEADS * (QK_NOPE + V_HEAD)).astype(BF16)
    g["kv_w_b"] = mm(ckv, dkvb, ta=True, name="kv_b_dw", tiles=(KV_LORA, kvb_tile, None), out_view=_out_cols(kv_w_b.shape))
    dckv = mm(dkvb, kv_w_b, tb=True, n_dim=KV_LORA, tiles=(None, KV_LORA, kvb_tile), b_view=_b_cols_t, name="kv_b_dx")
    dkr_sum = head_sum(dkr)
    zpad = jnp.zeros((seq, HALF_ROPE), F32)
    dkr_tile = jnp.concatenate([dkr_sum[:, :HALF_ROPE], zpad, dkr_sum[:, HALF_ROPE:], zpad], 1)

    def kv_post_bwd(kva, gk, dc, dk, cs, sn):
        dx, dgk = _rms_bwd(kva[:, :KV_LORA], gk, dc)
        return (jnp.concatenate([dx, _rope_tile_bwd(dk, cs, sn)], axis=1),), (dgk,)
    dkva, dkvn_g = rowwise(kv_post_bwd, (kva, kvn_g, dckv, dkr_tile, cos_k, sin_k), ((KVA_PAD, BF16),),
                           accs=(KV_LORA,), name="kv_post_bwd")
    g["kv_w_a"] = mm(h2b, dkva, ta=True, name="kv_a_dw")
    dh2 = mm(dcq_raw, q_w_a, tb=True, epi=lambda r, d: (r + DN_ALPHA * d,), extras=(dr3,), name="q_a_dx")
    dh2 = mm(dkva, kv_w_a, tb=True, epi=lambda r, d: (r + d,), extras=(dh2,), name="kv_a_dx")

    dr2, dr2b, dg_f0, db_f0 = ln_bwd(h1, f1, ln("ln_ffn_g", 0), dh2, "ln_ffn_bwd_0")
    pack, dh1 = mlp_bwd(pack, dr2, dr2b, h1b, f1pre, 0)
    dr1, dr1b, dg_m0, db_m0 = ln_bwd(x, mix0, ln("ln_mix_g", 0), dh1, "ln_mix_bwd_0")
    pack = mm(z, dr1b, ta=True, name="ssm_out_dw", tiles=(shard_rows, PACK_W, None), into=pack,
              out_view=into_rows(DIRECT_OFF["ssm_w_out"], shard_rows))
    dz = mm(dr1b, w_out, tb=True, name="ssm_out_dx")

    def glu_bwd(v, dz):
        val, sg = v[:, :D_MODEL], _sigmoid(v[:, D_MODEL:])
        return (jnp.concatenate([dz * sg, dz * val * sg * (1.0 - sg)], axis=1),), ()
    (dvg,) = rowwise(glu_bwd, (vg, dz), ((2 * D_MODEL, BF16),), name="glu_bwd")
    g["ssm_w_glu"] = mm(yg, dvg, ta=True, name="glu_proj_dw", tiles=(None, glu_tile, None), out_view=_out_cols(w_glu.shape))
    dypre = mm(dvg, w_glu, tb=True, epi=lambda r, y: (r * _gelu_grad(y),), extras=(ypre,), n_dim=D_MODEL,
               tiles=(None, D_MODEL, glu_tile), b_view=_b_cols_t, name="glu_proj_dx")
    dx, dbbd_re, dbbd_im, dcbd_re, dcbd_imn, dar, dai, dd = s5_bwd(
        dypre, x, dr1, h_re, h_im, bbd_re, bbd_im, cbd_re, cbd_imn, a_re, a_im, dskip)
    dbb_re = _blockdiag_in_t(dbbd_re).reshape(N_STATES, SSM_GROUP)
    dbb_im = _blockdiag_in_t(dbbd_im).reshape(N_STATES, SSM_GROUP)
    dlr, dli, dldt, db_re, db_im = s5_prep_bwd(lr, li, ldt, b_re, b_im, dar.reshape(N_STATES, 1),
                                               dai.reshape(N_STATES, 1), dbb_re, dbb_im)
    g["ssm_lam_re"] = dlr.reshape(1, N_GROUPS, SSM_STATE)
    g["ssm_lam_im"] = dli.reshape(1, N_GROUPS, SSM_STATE)
    g["ssm_log_dt"] = group_sum(dldt).reshape(1, N_GROUPS)
    g["ssm_b_re"] = db_re.reshape(1, N_GROUPS, SSM_STATE, SSM_GROUP)
    g["ssm_b_im"] = db_im.reshape(1, N_GROUPS, SSM_STATE, SSM_GROUP)
    g["ssm_c_re"] = _blockdiag_out_t(dcbd_re).reshape(1, N_GROUPS, SSM_GROUP, SSM_STATE)
    g["ssm_c_im"] = -_blockdiag_out_t(dcbd_imn).reshape(1, N_GROUPS, SSM_GROUP, SSM_STATE)
    g["ssm_d"] = dd
    g["ln_mix_g"] = jnp.concatenate([dg_m0, dg_m1], 0)
    g["ln_mix_b"] = jnp.concatenate([db_m0, db_m1], 0)
    g["ln_ffn_g"] = jnp.concatenate([dg_f0, dg_f1], 0)
    g["ln_ffn_b"] = jnp.concatenate([db_f0, db_f1], 0)
    g["kv_norm_g"] = dkvn_g.reshape(KV_LORA)
    g["q_norm_g"] = dqn_g
    return loss, dx, pack, g


def _place():
    x, y, c = lax.axis_index("x"), lax.axis_index("y"), lax.axis_index("c")
    return x, y, c, [(1 - x, y), (x, 1 - y), (1 - x, 1 - y)]


def place(shard, me_idx, dtype, name):
    rows, cols = shard.shape
    tr = _tile(rows, (512, 256, 128))

    def body(m_ref, x_ref, o_ref):
        o_ref[...] = x_ref[...].astype(o_ref.dtype)

    return _pcall(
        body, name=name,
        grid_spec=pltpu.PrefetchScalarGridSpec(
            num_scalar_prefetch=1, grid=(rows // tr,),
            in_specs=[pl.BlockSpec((tr, cols), lambda i, m: (i, 0))],
            out_specs=pl.BlockSpec((None, tr, cols), lambda i, m: (m[0], i, 0))),
        out_shape=jax.ShapeDtypeStruct((N_CHIPS, rows, cols), dtype),
        compiler_params=_params(("parallel",)),
    )(me_idx, shard)


def gather_stacked(arrs, name):
    n = len(arrs)

    def body(*refs):
        outs, send_sems, recv_sems = refs[n:2 * n], refs[2 * n], refs[2 * n + 1]
        x, y, c, chips = _place()
        sibling = (x, y, 1 - c)
        me = 2 * x + y

        def copy(k, blk, to):
            return pltpu.make_async_remote_copy(src_ref=blk, dst_ref=blk, send_sem=send_sems.at[k],
                                                recv_sem=recv_sems.at[k], device_id=to, device_id_type=MESH)

        started = []
        for a, o in enumerate(outs):
            for j, (px, py) in enumerate(chips):
                cp = copy(6 * a + j, o.at[me, c], (px, py, c))
                cp.start()
                started.append(cp)
        for a, o in enumerate(outs):
            for j, (px, py) in enumerate(chips):
                blk = o.at[2 * px + py, c]
                copy(6 * a + j, blk, (px, py, c)).wait_recv()
                cp = copy(6 * a + 3 + j, blk, sibling)
                cp.start()
                started.append(cp)
        for a, o in enumerate(outs):
            for j, (px, py) in enumerate(chips):
                copy(6 * a + 3 + j, o.at[2 * px + py, 1 - c], sibling).wait_recv()
        for cp in started:
            cp.wait_send()

    return _pcall(body, name=name, in_specs=[_ANY] * n, out_specs=[_ANY] * n,
                  out_shape=[jax.ShapeDtypeStruct(a.shape, a.dtype) for a in arrs],
                  input_output_aliases={i: i for i in range(n)},
                  scratch_shapes=[pltpu.SemaphoreType.DMA((6 * n,)), pltpu.SemaphoreType.DMA((6 * n,))])(*arrs)


def put_rows(pack, rows, off):
    _, n, cols = rows.shape
    tr = math.gcd(math.gcd(off, n), 512)

    def body(r_ref, p_ref, o_ref):
        o_ref[...] = r_ref[...]

    return _pcall(body, name="grad_put_rows", grid=(N_CHIPS, n // tr),
                  in_specs=[pl.BlockSpec((None, tr, cols), lambda k, i: (k, i, 0)), _ANY],
                  out_specs=pl.BlockSpec((None, tr, cols), lambda k, i: (k, off // tr + i, 0)),
                  out_shape=jax.ShapeDtypeStruct(pack.shape, pack.dtype), input_output_aliases={1: 0},
                  compiler_params=_params(("parallel", "parallel")))(rows, pack)


def _my_cols(c, mine=True):
    start = (c if mine else 1 - c) * HALF_W
    return pl.ds(pl.multiple_of(start, HALF_W), HALF_W)


def swap_halves(gpack):
    n, rows, _ = gpack.shape

    def body(g_ref, got_ref, send_sem, recv_sem):
        x, y, c, _ = _place()
        cp = pltpu.make_async_remote_copy(src_ref=g_ref.at[:, :, _my_cols(c, mine=False)], dst_ref=got_ref,
                                          send_sem=send_sem, recv_sem=recv_sem, device_id=(x, y, 1 - c),
                                          device_id_type=MESH)
        cp.start()
        cp.wait()

    return _pcall(body, name="grad_swap_halves", in_specs=[_ANY], out_specs=_ANY,
                  out_shape=jax.ShapeDtypeStruct((n, rows, HALF_W), gpack.dtype),
                  scratch_shapes=[pltpu.SemaphoreType.DMA, pltpu.SemaphoreType.DMA])(gpack)


def add_halves(gpack, got, c_idx):
    n, rows, _ = gpack.shape
    blk = (None, G_BLOCK_ROWS, HALF_W)

    def body(c_ref, g_ref, r_ref, o_ref):
        o_ref[...] = (g_ref[...] + r_ref[...]).astype(o_ref.dtype)

    return _pcall(
        body, name="grad_add_halves",
        grid_spec=pltpu.PrefetchScalarGridSpec(
            num_scalar_prefetch=1, grid=(n, rows // G_BLOCK_ROWS),
            in_specs=[pl.BlockSpec(blk, lambda k, i, c: (k, i, c[0])), pl.BlockSpec(blk, lambda k, i, c: (k, i, 0))],
            out_specs=pl.BlockSpec(blk, lambda k, i, c: (k, i, 0))),
        out_shape=jax.ShapeDtypeStruct((n, rows, HALF_W), BF16),
        compiler_params=_params(("parallel", "parallel")),
    )(c_idx, gpack, got)


def send_to_owners(part):
    _, rh, cols = part.shape

    def body(p_ref, got_ref, send_sems, recv_sems):
        x, y, c, chips = _place()
        cps = [pltpu.make_async_remote_copy(src_ref=p_ref.at[2 * px + py], dst_ref=got_ref.at[j],
                                            send_sem=send_sems.at[j], recv_sem=recv_sems.at[j],
                                            device_id=(px, py, c), device_id_type=MESH)
               for j, (px, py) in enumerate(chips)]
        for cp in cps:
            cp.start()
        for cp in cps:
            cp.wait()

    return _pcall(body, name="grad_send_to_owners", in_specs=[_ANY], out_specs=_ANY,
                  out_shape=jax.ShapeDtypeStruct((3, rh, cols), part.dtype),
                  scratch_shapes=[pltpu.SemaphoreType.DMA((3,)), pltpu.SemaphoreType.DMA((3,))])(part)


def sum_owner(part, got, idx):
    _, rows, _ = part.shape
    tr = G_BLOCK_ROWS

    def body(m_ref, p_ref, g_ref, o_ref):
        up = lambda v: v.astype(F32)
        o_ref[...] = ((up(p_ref[...]) + up(g_ref[0])) + up(g_ref[1])) + up(g_ref[2])

    return _pcall(
        body, name="grad_sum_owner",
        grid_spec=pltpu.PrefetchScalarGridSpec(
            num_scalar_prefetch=1, grid=(rows // tr,),
            in_specs=[pl.BlockSpec((None, tr, HALF_W), lambda i, m: (m[0], i, 0)),
                      pl.BlockSpec((3, tr, HALF_W), lambda i, m: (0, i, 0))],
            out_specs=pl.BlockSpec((tr, HALF_W), lambda i, m: (i, m[1]))),
        out_shape=jax.ShapeDtypeStruct((rows, PACK_W), F32),
        compiler_params=_params(("parallel",)),
    )(idx, part, got)


def join_halves(red):
    def body(in_ref, out_ref, send_sem, recv_sem):
        x, y, c, _ = _place()
        sibling = (x, y, 1 - c)
        mine = out_ref.at[:, _my_cols(c)]
        cp = pltpu.make_async_remote_copy(src_ref=mine, dst_ref=mine, send_sem=send_sem, recv_sem=recv_sem,
                                          device_id=sibling, device_id_type=MESH)
        cp.start()
        cp.wait_send()
        other = out_ref.at[:, _my_cols(c, mine=False)]
        pltpu.make_async_remote_copy(src_ref=other, dst_ref=other, send_sem=send_sem, recv_sem=recv_sem,
                                     device_id=sibling, device_id_type=MESH).wait_recv()

    return _pcall(body, name="grad_join_halves", in_specs=[_ANY], out_specs=_ANY,
                  out_shape=jax.ShapeDtypeStruct(red.shape, red.dtype), input_output_aliases={0: 0},
                  scratch_shapes=[pltpu.SemaphoreType.DMA, pltpu.SemaphoreType.DMA])(red)


def adamw(gsrc, g_off, wt, m, v, name):
    n, cols = wt.shape
    tr = math.gcd(math.gcd(g_off, n), 256) if g_off else math.gcd(n, 256)
    off_blk = g_off // tr
    c1 = 1.0 / (1.0 - ADAM_B1 ** ADAM_STEP)
    c2 = 1.0 / (1.0 - ADAM_B2 ** ADAM_STEP)

    def body(g_ref, w_ref, m_ref, v_ref, go_ref, d_ref, mo_ref, vo_ref):
        gv = g_ref[...]
        mn = ADAM_B1 * m_ref[...] + (1.0 - ADAM_B1) * gv
        vn = ADAM_B2 * v_ref[...] + (1.0 - ADAM_B2) * gv * gv
        go_ref[...] = gv
        mo_ref[...] = mn
        vo_ref[...] = vn
        d_ref[...] = -ADAM_LR * ((mn * c1) / (jnp.sqrt(vn * c2) + ADAM_EPS) + ADAM_WD * w_ref[...])

    blk = pl.BlockSpec((tr, cols), lambda i: (i, 0))
    return _pcall(body, name=name, grid=(n // tr,),
                  in_specs=[pl.BlockSpec((tr, cols), lambda i: (off_blk + i, 0)), blk, blk, blk],
                  out_specs=[blk] * 4, out_shape=[jax.ShapeDtypeStruct((n, cols), F32)] * 4,
                  compiler_params=_params(("parallel",)))(gsrc, wt, m, v)


def _rows8(a):
    return -(-a.size // (8 * PACK_W)) * 8


def _as_rows(a, rows=None):
    flat = a.reshape(-1)
    n = _rows8(a) if rows is None else rows
    return jnp.pad(flat, (0, n * PACK_W - flat.shape[0])).reshape(n, PACK_W)


def local_shards_2d(wl):
    return {"w_ff1": wl["w_ff1"].reshape(2 * D_MODEL, D_FF // N_CHIPS), "w_ff2": wl["w_ff2"].reshape(2 * D_FF // N_CHIPS, D_MODEL),
            "ssm_w_glu": wl["ssm_w_glu"], "ssm_w_out": wl["ssm_w_out"], "kv_w_a": _pad_kva_cols(wl["kv_w_a"]),
            "kv_w_b": wl["kv_w_b"], "q_w_a": wl["q_w_a"], "q_w_b": _perm_q_cols(wl["q_w_b"]),
            "attn_w_o": wl["attn_w_o"], "ssm_d": wl["ssm_d"].reshape(2, -1)}


def misc_grad_shard(name, g, k):
    if name == "ssm_d":
        w = D_MODEL // N_CHIPS
        return g[:, w * k:w * (k + 1)]
    if name in ("ssm_w_glu", "kv_w_b"):
        return g[k]
    if name == "q_w_b":
        return _unperm_q_cols(g[k])
    rows = D_MODEL // N_CHIPS
    shard = g[rows * k:rows * (k + 1)]
    return _unpad_kva_cols(shard) if name == "kv_w_a" else shard


def kernel(x, positions, ln_mix_g, ln_mix_b, ln_ffn_g, ln_ffn_b, w_ff1, w_ff2, ssm_lam_re, ssm_lam_im, ssm_log_dt, ssm_b_re, ssm_b_im, ssm_c_re, ssm_c_im, ssm_d, ssm_w_glu, ssm_w_out, kv_w_a, kv_norm_g, kv_w_b, q_w_a, q_norm_g, q_w_b, attn_w_o, loss_target, m_ln_mix_g, m_ln_mix_b, m_ln_ffn_g, m_ln_ffn_b, m_w_ff1, m_w_ff2, m_ssm_lam_re, m_ssm_lam_im, m_ssm_log_dt, m_ssm_b_re, m_ssm_b_im, m_ssm_c_re, m_ssm_c_im, m_ssm_d, m_ssm_w_glu, m_ssm_w_out, m_kv_w_a, m_kv_norm_g, m_kv_w_b, m_q_w_a, m_q_norm_g, m_q_w_b, m_attn_w_o, v_ln_mix_g, v_ln_mix_b, v_ln_ffn_g, v_ln_ffn_b, v_w_ff1, v_w_ff2, v_ssm_lam_re, v_ssm_lam_im, v_ssm_log_dt, v_ssm_b_re, v_ssm_b_im, v_ssm_c_re, v_ssm_c_im, v_ssm_d, v_ssm_w_glu, v_ssm_w_out, v_kv_w_a, v_kv_norm_g, v_kv_w_b, v_q_w_a, v_q_norm_g, v_q_w_b, v_attn_w_o):
    env = dict(locals())
    wl = {n: env[n] for n in WEIGHTS}
    ml = {n: env["m_" + n] for n in WEIGHTS}
    vl = {n: env["v_" + n] for n in WEIGHTS}
    for n in ("ssm_w_glu", "ssm_w_out", "q_w_a", "q_w_b", "attn_w_o"):
        wl[n], ml[n], vl[n] = wl[n][0], ml[n][0], vl[n][0]

    c_idx = lax.axis_index("c").astype(jnp.int32).reshape(1)
    me_idx = (2 * lax.axis_index("x") + lax.axis_index("y")).astype(jnp.int32).reshape(1)

    local = local_shards_2d(wl)
    placed = [place(local[n], me_idx, F32 if n == "ssm_d" else BF16, "place_" + n) for n in SHARDED]
    halves = [p.reshape(N_CHIPS, 2, p.shape[1] // 2, p.shape[2]) for p in placed]
    gathered = gather_stacked(halves, "weight_all_gather")
    full = {n: a.reshape(p.shape) for n, a, p in zip(SHARDED, gathered, placed)}
    for n in ("w_ff1", "w_ff2"):
        full[n] = full[n].reshape(N_CHIPS, 2, D_MODEL, D_MODEL)
    full["ssm_d"] = full["ssm_d"].reshape(1, D_MODEL)
    for n in REPLICATED:
        full[n] = wl[n]

    loss_part, dx, gpack, g = device_step(x[0], positions[0], loss_target[0], full)
    loss = lax.psum(loss_part, ("x", "y", "c"))

    small = jnp.concatenate([_as_rows(g[n]) for n in REPLICATED], axis=0)
    small = jnp.pad(small, ((0, SMALL_ROWS - small.shape[0]), (0, 0)))
    blocks = []
    for k in range(N_CHIPS):
        rows = [small[SMALL_Q_ROWS * k:SMALL_Q_ROWS * (k + 1)]]
        rows += [_as_rows(misc_grad_shard(n, g[n], k), MISC_SHARD_ROWS[n]) for n in MISC_SHARDED]
        blk = jnp.concatenate(rows, axis=0)
        blocks.append(jnp.pad(blk, ((0, MISC_ROWS - blk.shape[0]), (0, 0))))
    gpack = put_rows(gpack, jnp.stack(blocks), MISC_OFF)
    chip_part = add_halves(gpack, swap_halves(gpack), c_idx)
    reduced = join_halves(sum_owner(chip_part, send_to_owners(chip_part), jnp.concatenate([me_idx, c_idx])))
    quarter = reduced[MISC_OFF:MISC_OFF + SMALL_Q_ROWS]
    small_tot = gather_stacked([place(quarter, me_idx, F32, "place_small_grads").reshape(
        N_CHIPS, 2, SMALL_Q_ROWS // 2, PACK_W)], "small_grad_all_gather")[0].reshape(SMALL_ROWS, PACK_W)

    out_g, out_d, out_m, out_v = {}, {}, {}, {}
    for n in DIRECT_OFF:
        res = adamw(reduced, DIRECT_OFF[n], wl[n].reshape(-1, PACK_W), ml[n].reshape(-1, PACK_W),
                    vl[n].reshape(-1, PACK_W), "adamw_" + n)
        out_g[n], out_d[n], out_m[n], out_v[n] = [a.reshape(env[n].shape) for a in res]
    pack3 = lambda d: jnp.concatenate([_as_rows(d[n], MISC_SHARD_ROWS[n]) for n in MISC_SHARDED], axis=0)
    res = adamw(reduced, MISC_OFF + SMALL_Q_ROWS, pack3(wl), pack3(ml), pack3(vl), "adamw_row_packed")
    for n in MISC_SHARDED:
        cnt = math.prod(env[n].shape)
        r0 = MISC_SHARD_OFF[n] - SMALL_Q_ROWS
        out_g[n], out_d[n], out_m[n], out_v[n] = [
            a[r0:r0 + MISC_SHARD_ROWS[n]].reshape(-1)[:cnt].reshape(env[n].shape) for a in res]
    ws = jnp.concatenate([_as_rows(wl[n]) for n in REPLICATED], axis=0)
    ms = jnp.concatenate([_as_rows(ml[n]) for n in REPLICATED], axis=0)
    vs = jnp.concatenate([_as_rows(vl[n]) for n in REPLICATED], axis=0)
    pad = ((0, SMALL_ROWS - ws.shape[0]), (0, 0))
    res = adamw(small_tot, 0, jnp.pad(ws, pad), jnp.pad(ms, pad), jnp.pad(vs, pad), "adamw_replicated")
    row = 0
    for n in REPLICATED:
        cnt = math.prod(env[n].shape)
        nrows = _rows8(env[n])
        out_g[n], out_d[n], out_m[n], out_v[n] = [a[row:row + nrows].reshape(-1)[:cnt].reshape(env[n].shape) for a in res]
        row += nrows

    return (loss, dx[None], *[out_g[n] for n in WEIGHTS], *[out_d[n] for n in WEIGHTS],
            *[out_m[n] for n in WEIGHTS], *[out_v[n] for n in WEIGHTS])
```

```python
import functools
import math

import jax
import jax.numpy as jnp
from jax import lax
from jax.experimental import pallas as pl
from jax.experimental.pallas import tpu as pltpu

F32 = jnp.float32
BF16 = jnp.bfloat16
MESH = pl.DeviceIdType.MESH

D_MODEL = 1024
DEPTH = 2
SSM_GROUP = 16
N_GROUPS = D_MODEL // SSM_GROUP
SSM_STATE = 64
N_STATES = N_GROUPS * SSM_STATE
N_HEADS = 8
QK_NOPE = 128
QK_ROPE = 64
HALF_ROPE = QK_ROPE // 2
V_HEAD = 128
QK_DIM = QK_NOPE + QK_ROPE
Q_LORA = 384
KV_LORA = 256
ROPE_THETA = 10000.0
SM_SCALE = QK_DIM ** -0.5
NEG_INF = -1e30
D_FF = 4 * D_MODEL
DN_ALPHA = (2 * DEPTH) ** 0.25
LN_EPS = 1e-5
RMS_EPS = 1e-6
ADAM_LR = 0.001
ADAM_B1 = 0.9
ADAM_B2 = 0.999
ADAM_EPS = 1e-08
ADAM_WD = 0.01
ADAM_STEP = 10

N_CHIPS = 4
LANES = 128
STATE_TILES = N_STATES // LANES
VMEM_LIMIT = 56 * 1024 * 1024
PACK_W = 1024
KVA_PAD = 384
HALF_W = PACK_W // 2

SHARDED = ("w_ff1", "w_ff2", "ssm_w_glu", "ssm_w_out", "kv_w_a", "kv_w_b", "q_w_a", "q_w_b", "attn_w_o", "ssm_d")
DIRECT_OFF = {"w_ff1": 0, "w_ff2": 2048, "ssm_w_out": 4096, "attn_w_o": 4352}
DIRECT_ROWS = {"w_ff1": 2048, "w_ff2": 2048, "ssm_w_out": 256, "attn_w_o": 256}
MISC_OFF = 4608
SMALL_Q_ROWS = 96
SMALL_ROWS = N_CHIPS * SMALL_Q_ROWS
MISC_SHARDED = ("ssm_d", "ssm_w_glu", "kv_w_b", "kv_w_a", "q_w_a", "q_w_b")
MISC_SHARD_ROWS = {"ssm_d": 16, "ssm_w_glu": 512, "kv_w_b": 128, "kv_w_a": 80, "q_w_a": 96, "q_w_b": 144}
MISC_SHARD_OFF = {}
_o = SMALL_Q_ROWS
for _n in MISC_SHARDED:
    MISC_SHARD_OFF[_n] = _o
    _o += MISC_SHARD_ROWS[_n]
MISC_USED = _o
G_PACK_ROWS = 5760
MISC_ROWS = G_PACK_ROWS - MISC_OFF
G_BLOCK_ROWS = 960
REPLICATED = ("ln_mix_g", "ln_mix_b", "ln_ffn_g", "ln_ffn_b", "ssm_lam_re", "ssm_lam_im", "ssm_log_dt",
              "ssm_b_re", "ssm_b_im", "ssm_c_re", "ssm_c_im", "kv_norm_g", "q_norm_g")
WEIGHTS = ("ln_mix_g", "ln_mix_b", "ln_ffn_g", "ln_ffn_b", "w_ff1", "w_ff2", "ssm_lam_re", "ssm_lam_im",
           "ssm_log_dt", "ssm_b_re", "ssm_b_im", "ssm_c_re", "ssm_c_im", "ssm_d", "ssm_w_glu", "ssm_w_out",
           "kv_w_a", "kv_norm_g", "kv_w_b", "q_w_a", "q_norm_g", "q_w_b", "attn_w_o")


def _pcall(body, **kw):
    return pl.pallas_call(body, **kw)


def _params(sem=None):
    return pltpu.CompilerParams(dimension_semantics=sem, vmem_limit_bytes=VMEM_LIMIT)


_ANY = pl.BlockSpec(memory_space=pl.ANY)


def _tile(dim, prefs):
    for p in prefs:
        if dim % p == 0:
            return p
    return dim


def mm(a, b, *, name, ta=False, tb=False, pro_a=None, epi=None, extras=(), out_dtypes=(F32,), n_dim=None,
       tiles=(None, None, None), b_view=None, out_view=None, into=None):
    if ta:
        k_dim, m_dim = a.shape
    else:
        m_dim, k_dim = a.shape
    if n_dim is None:
        n_dim = b.shape[0] if tb else b.shape[1]
    tm = tiles[0] or _tile(m_dim, (1024, 512, 256, 128))
    tn = tiles[1] or _tile(n_dim, (1024, 512, 256, 128))
    tk = tiles[2] or (k_dim if k_dim <= 1024 else _tile(k_dim, (1024, 512, 256, 128)))
    assert m_dim % tm == 0 and n_dim % tn == 0 and k_dim % tk == 0, (name, m_dim, n_dim, k_dim, tm, tn, tk)
    nk = k_dim // tk
    n_ex, n_out = len(extras), len(out_dtypes)
    n_into = 0 if into is None else 1
    dims = (((0 if ta else 1,), (1 if tb else 0,)), ((), ()))

    def body(a_ref, b_ref, *rest):
        ex_refs, out_refs = rest[:n_ex], rest[n_ex + n_into:n_ex + n_into + n_out]

        def partial():
            av = a_ref[...]
            if pro_a is not None:
                av = pro_a(av)
            return lax.dot_general(av.astype(BF16), b_ref[...].astype(BF16), dims, preferred_element_type=F32)

        def finish(r):
            res = epi(r, *[e[...] for e in ex_refs]) if epi is not None else (r,)
            for o_ref, v in zip(out_refs, res):
                o_ref[...] = v.astype(o_ref.dtype)

        if nk == 1:
            finish(partial())
            return
        acc = rest[-1]
        k = pl.program_id(2)

        @pl.when(k == 0)
        def _():
            acc[...] = partial()

        @pl.when(k > 0)
        def _():
            acc[...] += partial()

        @pl.when(k == nk - 1)
        def _():
            finish(acc[...])

    a_spec = pl.BlockSpec((tk, tm), lambda i, j, k: (k, i)) if ta else pl.BlockSpec((tm, tk), lambda i, j, k: (i, k))
    if b_view is not None:
        b_spec = b_view(tk, tn)
    else:
        b_spec = pl.BlockSpec((tn, tk), lambda i, j, k: (j, k)) if tb else pl.BlockSpec((tk, tn), lambda i, j, k: (k, j))
    o_spec = pl.BlockSpec((tm, tn), lambda i, j, k: (i, j))
    if out_view is None:
        out_specs = [o_spec] * n_out
        out_shape = [jax.ShapeDtypeStruct((m_dim, n_dim), dt) for dt in out_dtypes]
    else:
        assert n_out == 1
        out_specs = [out_view[1](tm, tn)]
        out_shape = [jax.ShapeDtypeStruct(out_view[0], out_dtypes[0])]
    outs = _pcall(
        body, name=name, grid=(m_dim // tm, n_dim // tn, nk),
        in_specs=[a_spec, b_spec] + [o_spec] * n_ex + [_ANY] * n_into,
        out_specs=out_specs, out_shape=out_shape,
        input_output_aliases={2 + n_ex: 0} if n_into else {},
        scratch_shapes=[pltpu.VMEM((tm, tn), F32)] if nk > 1 else [],
        compiler_params=_params(("parallel", "parallel", "arbitrary")),
    )(a, b, *extras, *([into] if n_into else []))
    return outs[0] if n_out == 1 else outs


def rowwise(fn, ins, outs, *, name, accs=(), tm=256):
    rows = ins[0].shape[0]
    tm = min(tm, rows)
    n_in, n_out, n_acc = len(ins), len(outs), len(accs)

    def body(*refs):
        in_refs, out_refs, acc_refs = refs[:n_in], refs[n_in:n_in + n_out], refs[n_in + n_out:]
        res, sums = fn(*[r[...] for r in in_refs])
        for o_ref, v in zip(out_refs, res):
            o_ref[...] = v.astype(o_ref.dtype)
        if n_acc:
            @pl.when(pl.program_id(0) == 0)
            def _():
                for a_ref in acc_refs:
                    a_ref[...] = jnp.zeros_like(a_ref)

            for a_ref, s in zip(acc_refs, sums):
                a_ref[...] += s

    def spec(arr):
        if arr.shape[0] == rows:
            return pl.BlockSpec((tm, arr.shape[1]), lambda i: (i, 0))
        return pl.BlockSpec(arr.shape, lambda i: (0, 0))

    res = _pcall(
        body, name=name, grid=(rows // tm,),
        in_specs=[spec(a) for a in ins],
        out_specs=[pl.BlockSpec((tm, w), lambda i: (i, 0)) for w, _ in outs]
        + [pl.BlockSpec((1, w), lambda i: (0, 0)) for w in accs],
        out_shape=[jax.ShapeDtypeStruct((rows, w), dt) for w, dt in outs]
        + [jax.ShapeDtypeStruct((1, w), F32) for w in accs],
        compiler_params=_params(("arbitrary",) if n_acc else ("parallel",)),
    )(*ins)
    return res


def _relu2(v):
    r = jnp.maximum(v, 0.0)
    return r * r


def _gelu(x):
    c = math.sqrt(2.0 / math.pi)
    return 0.5 * x * (1.0 + jnp.tanh(c * (x + 0.044715 * x * x * x)))


def _gelu_grad(x):
    c = math.sqrt(2.0 / math.pi)
    t = jnp.tanh(c * (x + 0.044715 * x * x * x))
    return 0.5 * (1.0 + t) + 0.5 * x * (1.0 - t * t) * c * (1.0 + 3 * 0.044715 * x * x)


def _sigmoid(x):
    return 1.0 / (1.0 + jnp.exp(-x))


def ln_fwd(h, mix, g, b, name):
    def fn(h, mix, g, b):
        r = DN_ALPHA * h + mix
        mu = jnp.mean(r, axis=-1, keepdims=True)
        xc = r - mu
        var = jnp.mean(xc * xc, axis=-1, keepdims=True)
        y = xc * lax.rsqrt(var + LN_EPS) * g + b
        return (y, y), ()
    return rowwise(fn, (h, mix, g, b), ((D_MODEL, F32), (D_MODEL, BF16)), name=name)


def ln_bwd(h, mix, g, dy, name):
    def fn(h, mix, g, dy):
        r = DN_ALPHA * h + mix
        mu = jnp.mean(r, axis=-1, keepdims=True)
        xc = r - mu
        var = jnp.mean(xc * xc, axis=-1, keepdims=True)
        rstd = lax.rsqrt(var + LN_EPS)
        xhat = xc * rstd
        dxh = dy * g
        m1 = jnp.mean(dxh, axis=-1, keepdims=True)
        m2 = jnp.mean(dxh * xhat, axis=-1, keepdims=True)
        dr = rstd * (dxh - m1 - xhat * m2)
        return (dr, dr), (jnp.sum(dy * xhat, axis=0, keepdims=True), jnp.sum(dy, axis=0, keepdims=True))
    return rowwise(fn, (h, mix, g, dy), ((D_MODEL, F32), (D_MODEL, BF16)), accs=(D_MODEL, D_MODEL), name=name)


def _rms(x, g):
    r = lax.rsqrt(jnp.mean(x * x, axis=-1, keepdims=True) + RMS_EPS)
    return x * r * g


def _rms_bwd(x, g, dy):
    r = lax.rsqrt(jnp.mean(x * x, axis=-1, keepdims=True) + RMS_EPS)
    xn = x * r
    dyg = dy * g
    dx = r * (dyg - xn * jnp.mean(dyg * xn, axis=-1, keepdims=True))
    return dx, jnp.sum(dy * xn, axis=0, keepdims=True)


def _s5_disc(lr, li, ldt):
    dt = jnp.exp(ldt)
    mag = jnp.exp(lr * dt)
    cs, sn = jnp.cos(li * dt), jnp.sin(li * dt)
    ar, ai = mag * cs, mag * sn
    inv = 1.0 / (lr * lr + li * li)
    n_re = (ar - 1.0) * lr + ai * li
    n_im = ai * lr - (ar - 1.0) * li
    return dt, mag, cs, sn, ar, ai, inv, n_re, n_im


def s5_prep(lr, li, ldt, b_re, b_im):
    def fn(lr, li, ldt, b_re, b_im):
        _, _, _, _, ar, ai, inv, n_re, n_im = _s5_disc(lr, li, ldt)
        cr, ci = n_re * inv, n_im * inv
        return (ar, ai, cr * b_re - ci * b_im, cr * b_im + ci * b_re), ()
    return rowwise(fn, (lr, li, ldt, b_re, b_im), ((1, F32), (1, F32), (SSM_GROUP, F32), (SSM_GROUP, F32)),
                   name="s5_prep", tm=512)


def s5_prep_bwd(lr, li, ldt, b_re, b_im, dar, dai, dbb_re, dbb_im):
    def fn(lr, li, ldt, b_re, b_im, dar, dai, dbb_re, dbb_im):
        dt, mag, cs, sn, ar, ai, inv, n_re, n_im = _s5_disc(lr, li, ldt)
        cr, ci = n_re * inv, n_im * inv
        db_re = cr * dbb_re + ci * dbb_im
        db_im = cr * dbb_im - ci * dbb_re
        dcr = jnp.sum(dbb_re * b_re + dbb_im * b_im, axis=-1, keepdims=True)
        dci = jnp.sum(dbb_im * b_re - dbb_re * b_im, axis=-1, keepdims=True)
        dar = dar + (dcr * lr - dci * li) * inv
        dai = dai + (dcr * li + dci * lr) * inv
        dinv = dcr * n_re + dci * n_im
        dlr = (dcr * (ar - 1.0) + dci * ai) * inv - 2.0 * lr * inv * inv * dinv
        dli = (dcr * ai - dci * (ar - 1.0)) * inv - 2.0 * li * inv * inv * dinv
        dmag = dar * cs + dai * sn
        dth = dai * ar - dar * ai
        dlr = dlr + dmag * mag * dt
        dli = dli + dth * dt
        ddt = dmag * mag * lr + dth * li
        return (dlr, dli, ddt * dt, db_re, db_im), ()
    return rowwise(fn, (lr, li, ldt, b_re, b_im, dar, dai, dbb_re, dbb_im),
                   ((1, F32), (1, F32), (1, F32), (SSM_GROUP, F32), (SSM_GROUP, F32)), name="s5_prep_bwd", tm=512)


def group_sum(x):
    def body(x_ref, o_ref):
        o_ref[...] = jnp.sum(x_ref[...], axis=1)
    return _pcall(body, name="s5_group_sum", out_shape=jax.ShapeDtypeStruct((N_GROUPS, 1), F32))(
        x.reshape(N_GROUPS, SSM_STATE, 1))


GROUPS_PER_TILE = LANES // SSM_GROUP
TILE_STATES = GROUPS_PER_TILE * SSM_STATE
N_UTILES = D_MODEL // LANES
TILES_PER_UTILE = TILE_STATES // LANES


def _store_states(ref, j, val, t_rows):
    for q in range(TILES_PER_UTILE):
        ref[pl.ds(TILES_PER_UTILE * j + q, t_rows, stride=STATE_TILES), :] = val[:, LANES * q:LANES * (q + 1)]


def _load_states(ref, j, t_rows):
    return jnp.concatenate(
        [ref[pl.ds(TILES_PER_UTILE * j + q, t_rows, stride=STATE_TILES), :] for q in range(TILES_PER_UTILE)], axis=1)


def _tok(t):
    return pl.ds(pl.multiple_of(t * STATE_TILES, STATE_TILES), STATE_TILES)


_NT = (((1,), (1,)), ((), ()))
_TN = (((0,), (0,)), ((), ()))


def s5_fwd(u, bbd_re, bbd_im, cbd_re, cbd_imn, a_re, a_im, dskip, t_rows=128):
    seq = u.shape[0]
    t_rows = min(t_rows, seq)

    def body(u_ref, bre, bim, cre, cimn, are, aim, d_ref, y_ref, hre_ref, him_ref, car_re, car_im):
        @pl.when(pl.program_id(0) == 0)
        def _():
            car_re[...] = jnp.zeros_like(car_re)
            car_im[...] = jnp.zeros_like(car_im)

        uf = u_ref[...]
        ub = uf.astype(BF16)
        for j in range(N_UTILES):
            uj = ub[:, LANES * j:LANES * (j + 1)]
            _store_states(hre_ref, j, jnp.dot(uj, bre[j], preferred_element_type=F32), t_rows)
            _store_states(him_ref, j, jnp.dot(uj, bim[j], preferred_element_type=F32), t_rows)
        ar, ai = are[...], aim[...]

        def step(t, carry):
            hr, hi = carry
            rows = _tok(t)
            nr = ar * hr - ai * hi + hre_ref[rows, :]
            ni = ar * hi + ai * hr + him_ref[rows, :]
            hre_ref[rows, :] = nr
            him_ref[rows, :] = ni
            return nr, ni

        hr, hi = lax.fori_loop(0, t_rows, step, (car_re[...], car_im[...]))
        car_re[...] = hr
        car_im[...] = hi
        dv = d_ref[...]
        for j in range(N_UTILES):
            hrj = _load_states(hre_ref, j, t_rows).astype(BF16)
            hij = _load_states(him_ref, j, t_rows).astype(BF16)
            yj = jnp.dot(hrj, cre[j], preferred_element_type=F32) + jnp.dot(hij, cimn[j], preferred_element_type=F32)
            sl = slice(LANES * j, LANES * (j + 1))
            y_ref[:, sl] = yj + dv[:, sl] * uf[:, sl]

    full3 = lambda a: pl.BlockSpec(a.shape, lambda i: (0, 0, 0))
    full2 = lambda a: pl.BlockSpec(a.shape, lambda i: (0, 0))
    return _pcall(
        body, name="s5_fwd", grid=(seq // t_rows,),
        in_specs=[pl.BlockSpec((t_rows, D_MODEL), lambda i: (i, 0)), full3(bbd_re), full3(bbd_im), full3(cbd_re),
                  full3(cbd_imn), full2(a_re), full2(a_im), full2(dskip)],
        out_specs=[pl.BlockSpec((t_rows, D_MODEL), lambda i: (i, 0)),
                   pl.BlockSpec((t_rows * STATE_TILES, LANES), lambda i: (i, 0)),
                   pl.BlockSpec((t_rows * STATE_TILES, LANES), lambda i: (i, 0))],
        out_shape=[jax.ShapeDtypeStruct((seq, D_MODEL), F32),
                   jax.ShapeDtypeStruct((seq * STATE_TILES, LANES), F32),
                   jax.ShapeDtypeStruct((seq * STATE_TILES, LANES), F32)],
        scratch_shapes=[pltpu.VMEM((STATE_TILES, LANES), F32), pltpu.VMEM((STATE_TILES, LANES), F32)],
        compiler_params=_params(("arbitrary",)),
    )(u, bbd_re, bbd_im, cbd_re, cbd_imn, a_re, a_im, dskip)


def s5_bwd(dy, u, dres, h_re, h_im, bbd_re, bbd_im, cbd_re, cbd_imn, a_re, a_im, dskip, t_rows=128):
    seq = u.shape[0]
    t_rows = min(t_rows, seq)
    n_chunks = seq // t_rows

    def body(dy_ref, u_ref, dres_ref, hre_ref, him_ref, hpre_ref, hpim_ref, bre, bim, cre, cimn, are, aim, d_ref,
             dx_ref, dbre, dbim, dcre, dcimn, dar_ref, dai_ref, dd_ref, lre, lim, car_re, car_im):
        i = pl.program_id(0)

        @pl.when(i == 0)
        def _():
            for r in (car_re, car_im, dbre, dbim, dcre, dcimn, dar_ref, dai_ref, dd_ref):
                r[...] = jnp.zeros_like(r)

        dyf = dy_ref[...]
        dyb = dyf.astype(BF16)
        uf = u_ref[...]
        ub = uf.astype(BF16)
        for j in range(N_UTILES):
            dyj = dyb[:, LANES * j:LANES * (j + 1)]
            _store_states(lre, j, lax.dot_general(dyj, cre[j], _NT, preferred_element_type=F32), t_rows)
            _store_states(lim, j, lax.dot_general(dyj, cimn[j], _NT, preferred_element_type=F32), t_rows)
        ar, ai = are[...], aim[...]

        def adjoint(t, lr, li):
            rows = _tok(t)
            nr = ar * lr + ai * li + lre[rows, :]
            ni = ar * li - ai * lr + lim[rows, :]
            lre[rows, :] = nr
            lim[rows, :] = ni
            return nr, ni

        def step(k, carry):
            lr, li, dar, dai = carry
            t = t_rows - 1 - k
            nr, ni = adjoint(t, lr, li)
            prev = _tok(t - 1)
            hpr, hpi = hre_ref[prev, :], him_ref[prev, :]
            return nr, ni, dar + nr * hpr + ni * hpi, dai + ni * hpr - nr * hpi

        zero = jnp.zeros((STATE_TILES, LANES), F32)
        lr, li, dar, dai = lax.fori_loop(0, t_rows - 1, step, (car_re[...], car_im[...], zero, zero))
        nr, ni = adjoint(0, lr, li)
        first = (i == n_chunks - 1).astype(F32)
        hpr = hpre_ref[...] * (1.0 - first)
        hpi = hpim_ref[...] * (1.0 - first)
        car_re[...] = nr
        car_im[...] = ni
        dar_ref[...] += dar + nr * hpr + ni * hpi
        dai_ref[...] += dai + ni * hpr - nr * hpi

        dv = d_ref[...]
        for j in range(N_UTILES):
            sl = slice(LANES * j, LANES * (j + 1))
            lrj = _load_states(lre, j, t_rows).astype(BF16)
            lij = _load_states(lim, j, t_rows).astype(BF16)
            du = (lax.dot_general(lrj, bre[j], _NT, preferred_element_type=F32)
                  + lax.dot_general(lij, bim[j], _NT, preferred_element_type=F32))
            dx_ref[:, sl] = du + dv[:, sl] * dyf[:, sl] + DN_ALPHA * dres_ref[:, sl]
            uj = ub[:, sl]
            dbre[j] += lax.dot_general(uj, lrj, _TN, preferred_element_type=F32)
            dbim[j] += lax.dot_general(uj, lij, _TN, preferred_element_type=F32)
            hrj = _load_states(hre_ref, j, t_rows).astype(BF16)
            hij = _load_states(him_ref, j, t_rows).astype(BF16)
            dyj = dyb[:, sl]
            dcre[j] += lax.dot_general(hrj, dyj, _TN, preferred_element_type=F32)
            dcimn[j] += lax.dot_general(hij, dyj, _TN, preferred_element_type=F32)
        dd_ref[...] += jnp.sum(dyf * uf, axis=0, keepdims=True)

    rev = lambda i: (n_chunks - 1 - i, 0)
    prev_tok = lambda i: (jnp.maximum((n_chunks - 1 - i) * t_rows - 1, 0), 0)
    full3 = lambda a: pl.BlockSpec(a.shape, lambda i: (0, 0, 0))
    full2 = lambda a: pl.BlockSpec(a.shape, lambda i: (0, 0))
    acc3 = lambda shape: pl.BlockSpec(shape, lambda i: (0, 0, 0))
    acc2 = lambda shape: pl.BlockSpec(shape, lambda i: (0, 0))
    st = (STATE_TILES, LANES)
    return _pcall(
        body, name="s5_bwd", grid=(n_chunks,),
        in_specs=[pl.BlockSpec((t_rows, D_MODEL), rev), pl.BlockSpec((t_rows, D_MODEL), rev),
                  pl.BlockSpec((t_rows, D_MODEL), rev),
                  pl.BlockSpec((t_rows * STATE_TILES, LANES), rev), pl.BlockSpec((t_rows * STATE_TILES, LANES), rev),
                  pl.BlockSpec(st, prev_tok), pl.BlockSpec(st, prev_tok),
                  full3(bbd_re), full3(bbd_im), full3(cbd_re), full3(cbd_imn), full2(a_re), full2(a_im), full2(dskip)],
        out_specs=[pl.BlockSpec((t_rows, D_MODEL), rev), acc3(bbd_re.shape), acc3(bbd_im.shape), acc3(cbd_re.shape),
                   acc3(cbd_imn.shape), acc2(st), acc2(st), acc2((1, D_MODEL))],
        out_shape=[jax.ShapeDtypeStruct((seq, D_MODEL), F32), jax.ShapeDtypeStruct(bbd_re.shape, F32),
                   jax.ShapeDtypeStruct(bbd_im.shape, F32), jax.ShapeDtypeStruct(cbd_re.shape, F32),
                   jax.ShapeDtypeStruct(cbd_imn.shape, F32), jax.ShapeDtypeStruct(st, F32),
                   jax.ShapeDtypeStruct(st, F32), jax.ShapeDtypeStruct((1, D_MODEL), F32)],
        scratch_shapes=[pltpu.VMEM((t_rows * STATE_TILES, LANES), F32), pltpu.VMEM((t_rows * STATE_TILES, LANES), F32),
                        pltpu.VMEM(st, F32), pltpu.VMEM(st, F32)],
        compiler_params=_params(("arbitrary",)),
    )(dy, u, dres, h_re, h_im, h_re, h_im, bbd_re, bbd_im, cbd_re, cbd_imn, a_re, a_im, dskip)


def _eye_groups():
    return jnp.eye(GROUPS_PER_TILE, dtype=F32)


def _blockdiag_in(bb):
    t = bb.transpose(0, 2, 1).reshape(N_UTILES, GROUPS_PER_TILE, SSM_GROUP, SSM_STATE)
    bd = jnp.einsum("jgcp,gh->jgchp", t, _eye_groups())
    return bd.reshape(N_UTILES, LANES, TILE_STATES)


def _blockdiag_in_t(d):
    t = jnp.einsum("jgchp,gh->jgcp", d.reshape(N_UTILES, GROUPS_PER_TILE, SSM_GROUP, GROUPS_PER_TILE, SSM_STATE),
                   _eye_groups())
    return t.reshape(N_GROUPS, SSM_GROUP, SSM_STATE).transpose(0, 2, 1)


def _blockdiag_out(c):
    t = c.transpose(0, 2, 1).reshape(N_UTILES, GROUPS_PER_TILE, SSM_STATE, SSM_GROUP)
    bd = jnp.einsum("jhpc,hg->jhpgc", t, _eye_groups())
    return bd.reshape(N_UTILES, TILE_STATES, LANES)


def _blockdiag_out_t(d):
    t = jnp.einsum("jhpgc,hg->jhpc", d.reshape(N_UTILES, GROUPS_PER_TILE, SSM_STATE, GROUPS_PER_TILE, SSM_GROUP),
                   _eye_groups())
    return t.reshape(N_GROUPS, SSM_STATE, SSM_GROUP).transpose(0, 2, 1)


ATT_TQ = 512
ATT_TK = 512
LOG2E = math.log2(math.e)
LN2 = math.log(2.0)
Q_PRESCALE = SM_SCALE * LOG2E


def _loop_in_pairs(n, step, carry, start=0):
    pairs = (n - start) // 2

    def two(t, c):
        return step(start + 2 * t + 1, step(start + 2 * t, c))

    carry = lax.fori_loop(0, pairs, two, carry)
    return lax.fori_loop(start + 2 * pairs, n, step, carry)


def _causal(s, off=0, transposed=False):
    r = lax.broadcasted_iota(jnp.int32, s.shape, 0)
    c = lax.broadcasted_iota(jnp.int32, s.shape, 1)
    keep = (r <= c + off) if transposed else (c <= r + off)
    return jnp.where(keep, s, NEG_INF)


def attn_fwd(q, k, v, tq=ATT_TQ, tk=ATT_TK):
    n_heads, seq, _ = q.shape
    tq, tk = min(tq, seq), min(tk, seq)

    def body(q_ref, k_ref, v_ref, o_ref, lse_ref):
        qi = pl.program_id(1)
        qv = q_ref[0]
        jd = (qi * tq) // tk

        def block(j, carry, diag):
            m, l, acc = carry
            rows = pl.ds(pl.multiple_of(j * tk, tk), tk)
            s = lax.dot_general(qv, k_ref[0, rows, :], _NT, preferred_element_type=F32)
            if diag:
                s = _causal(s, qi * tq - jd * tk)
            m_new = jnp.maximum(m, jnp.max(s, axis=-1, keepdims=True))
            p = jnp.exp2(s - m_new)
            corr = jnp.exp2(m - m_new)
            l = l * corr + jnp.sum(p, axis=-1, keepdims=True)
            acc = acc * corr + jnp.dot(p.astype(BF16), v_ref[rows, :], preferred_element_type=F32)
            return m_new, l, acc

        init = (jnp.full((tq, 1), NEG_INF, F32), jnp.zeros((tq, 1), F32), jnp.zeros((tq, V_HEAD), F32))
        carry = _loop_in_pairs(jd, lambda j, c: block(j, c, False), init)
        m, l, acc = block(jd, carry, True)
        o_ref[...] = acc / l
        lse_ref[0] = jnp.broadcast_to(m + jnp.log2(l), (tq, LANES))

    return _pcall(
        body, name="attn_fwd", grid=(n_heads, seq // tq),
        in_specs=[pl.BlockSpec((1, tq, QK_DIM), lambda h, i: (h, i, 0)),
                  pl.BlockSpec((1, seq, QK_DIM), lambda h, i: (h, 0, 0)),
                  pl.BlockSpec((seq, V_HEAD), lambda h, i: (0, h))],
        out_specs=[pl.BlockSpec((tq, V_HEAD), lambda h, i: (i, h)),
                   pl.BlockSpec((1, tq, LANES), lambda h, i: (h, i, 0))],
        out_shape=[jax.ShapeDtypeStruct((seq, n_heads * V_HEAD), F32),
                   jax.ShapeDtypeStruct((n_heads, seq, LANES), F32)],
        compiler_params=_params(("parallel", "parallel")),
    )(q, k, v)


def attn_bwd_dq(q, k, v, do, o, lse, tq=ATT_TQ, tk=ATT_TK):
    n_heads, seq, _ = q.shape
    tq, tk = min(tq, seq), min(tk, seq)

    def body(q_ref, k_ref, v_ref, do_ref, o_ref, lse_ref, dqn_ref, dqr_ref, delta_ref):
        qi = pl.program_id(1)
        qv = q_ref[0]
        dof = do_ref[...]
        dob = dof.astype(BF16)
        delta = jnp.sum(dof * o_ref[...], axis=-1, keepdims=True)
        lse = lse_ref[0][:, :1]
        jd = (qi * tq) // tk

        def block(j, dq, diag):
            rows = pl.ds(pl.multiple_of(j * tk, tk), tk)
            kv = k_ref[0, rows, :]
            s = lax.dot_general(qv, kv, _NT, preferred_element_type=F32)
            if diag:
                s = _causal(s, qi * tq - jd * tk)
            p = jnp.exp2(s - lse)
            dp = lax.dot_general(dob, v_ref[rows, :], _NT, preferred_element_type=F32)
            ds = p * (dp - delta)
            return dq + jnp.dot(ds.astype(BF16), kv, preferred_element_type=F32)

        dq = _loop_in_pairs(jd, lambda j, c: block(j, c, False), jnp.zeros((tq, QK_DIM), F32))
        dq = block(jd, dq, True) * SM_SCALE
        dqn_ref[...] = dq[:, :QK_NOPE]
        dqr_ref[0] = dq[:, QK_NOPE:]
        delta_ref[0] = jnp.broadcast_to(delta, (tq, LANES))

    return _pcall(
        body, name="attn_bwd_dq", grid=(n_heads, seq // tq),
        in_specs=[pl.BlockSpec((1, tq, QK_DIM), lambda h, i: (h, i, 0)),
                  pl.BlockSpec((1, seq, QK_DIM), lambda h, i: (h, 0, 0)),
                  pl.BlockSpec((seq, V_HEAD), lambda h, i: (0, h)),
                  pl.BlockSpec((tq, V_HEAD), lambda h, i: (i, h)),
                  pl.BlockSpec((tq, V_HEAD), lambda h, i: (i, h)),
                  pl.BlockSpec((1, tq, LANES), lambda h, i: (h, i, 0))],
        out_specs=[pl.BlockSpec((tq, QK_NOPE), lambda h, i: (i, h)),
                   pl.BlockSpec((1, tq, QK_ROPE), lambda h, i: (h, i, 0)),
                   pl.BlockSpec((1, tq, LANES), lambda h, i: (h, i, 0))],
        out_shape=[jax.ShapeDtypeStruct((seq, n_heads * QK_NOPE), F32),
                   jax.ShapeDtypeStruct((n_heads, seq, QK_ROPE), F32),
                   jax.ShapeDtypeStruct((n_heads, seq, LANES), F32)],
        compiler_params=_params(("parallel", "parallel")),
    )(q, k, v, do, o, lse)


def attn_bwd_dkv(q, k, v, do, lse_row, delta_row, tq=ATT_TK):
    n_heads, seq, _ = q.shape
    tq = min(tq, seq)
    n_blk = seq // tq

    def body(q_ref, k_ref, v_ref, do_ref, lse_ref, delta_ref, dkn_ref, dkr_ref, dv_ref):
        kj = pl.program_id(1)
        kv = k_ref[0]
        vv = v_ref[...]

        def block(i, carry, diag):
            dk, dv = carry
            rows = pl.ds(pl.multiple_of(i * tq, tq), tq)
            qv = q_ref[0, rows, :]
            st = lax.dot_general(kv, qv, _NT, preferred_element_type=F32)
            if diag:
                st = _causal(st, transposed=True)
            pt = jnp.exp2(st - lse_ref[0, pl.ds(i, 1), :])
            dob = do_ref[rows, :].astype(BF16)
            dv = dv + jnp.dot(pt.astype(BF16), dob, preferred_element_type=F32)
            dpt = lax.dot_general(vv, dob, _NT, preferred_element_type=F32)
            dst = pt * (dpt - delta_ref[0, pl.ds(i, 1), :])
            dk = dk + jnp.dot(dst.astype(BF16), qv, preferred_element_type=F32)
            return dk, dv

        carry = block(kj, (jnp.zeros((tq, QK_DIM), F32), jnp.zeros((tq, V_HEAD), F32)), True)
        dk, dv = _loop_in_pairs(n_blk, lambda i, c: block(i, c, False), carry, start=kj + 1)
        dk = dk * LN2
        dkn_ref[...] = dk[:, :QK_NOPE]
        dkr_ref[0] = dk[:, QK_NOPE:]
        dv_ref[...] = dv

    return _pcall(
        body, name="attn_bwd_dkv", grid=(n_heads, n_blk),
        in_specs=[pl.BlockSpec((1, seq, QK_DIM), lambda h, j: (h, 0, 0)),
                  pl.BlockSpec((1, tq, QK_DIM), lambda h, j: (h, j, 0)),
                  pl.BlockSpec((tq, V_HEAD), lambda h, j: (j, h)),
                  pl.BlockSpec((seq, V_HEAD), lambda h, j: (0, h)),
                  pl.BlockSpec((1, n_blk, tq), lambda h, j: (h, 0, 0)),
                  pl.BlockSpec((1, n_blk, tq), lambda h, j: (h, 0, 0))],
        out_specs=[pl.BlockSpec((tq, QK_NOPE), lambda h, j: (j, h)),
                   pl.BlockSpec((1, tq, QK_ROPE), lambda h, j: (h, j, 0)),
                   pl.BlockSpec((tq, V_HEAD), lambda h, j: (j, h))],
        out_shape=[jax.ShapeDtypeStruct((seq, n_heads * QK_NOPE), F32),
                   jax.ShapeDtypeStruct((n_heads, seq, QK_ROPE), F32),
                   jax.ShapeDtypeStruct((seq, n_heads * V_HEAD), F32)],
        compiler_params=_params(("parallel", "parallel")),
    )(q, k, v, do, lse_row, delta_row)


def head_sum(x, ts=512):
    n_heads, seq, w = x.shape
    ts = min(ts, seq)

    def body(x_ref, o_ref):
        o_ref[...] = jnp.sum(x_ref[...], axis=0)

    return _pcall(body, name="head_sum", grid=(seq // ts,),
                  in_specs=[pl.BlockSpec((n_heads, ts, w), lambda i: (0, i, 0))],
                  out_specs=pl.BlockSpec((ts, w), lambda i: (i, 0)),
                  out_shape=jax.ShapeDtypeStruct((seq, w), F32),
                  compiler_params=_params(("parallel",)))(x)


HEADS_PER_CHIP = N_HEADS // N_CHIPS
Q_CHIP = HEADS_PER_CHIP * QK_DIM
Q_CHIP_NOPE = HEADS_PER_CHIP * QK_NOPE


def _perm_q_cols(w):
    t = w.reshape(w.shape[0], HEADS_PER_CHIP, QK_DIM)
    return jnp.concatenate([t[:, :, :QK_NOPE].reshape(w.shape[0], -1),
                            t[:, :, QK_NOPE:QK_NOPE + HALF_ROPE].reshape(w.shape[0], -1),
                            t[:, :, QK_NOPE + HALF_ROPE:].reshape(w.shape[0], -1)], axis=1)


def _unperm_q_cols(w):
    r = w.shape[0]
    nope = w[:, :Q_CHIP_NOPE].reshape(r, HEADS_PER_CHIP, QK_NOPE)
    r1 = w[:, Q_CHIP_NOPE:Q_CHIP_NOPE + QK_ROPE].reshape(r, HEADS_PER_CHIP, HALF_ROPE)
    r2 = w[:, Q_CHIP_NOPE + QK_ROPE:].reshape(r, HEADS_PER_CHIP, HALF_ROPE)
    return jnp.concatenate([nope, r1, r2], axis=2).reshape(r, Q_CHIP)


def _pad_kva_cols(w):
    z = jnp.zeros((w.shape[0], HALF_ROPE), w.dtype)
    return jnp.concatenate([w[:, :KV_LORA], w[:, KV_LORA:KV_LORA + HALF_ROPE], z, w[:, KV_LORA + HALF_ROPE:], z], axis=1)


def _unpad_kva_cols(w):
    return jnp.concatenate([w[:, :KV_LORA], w[:, KV_LORA:KV_LORA + HALF_ROPE],
                            w[:, KV_LORA + QK_ROPE:KV_LORA + QK_ROPE + HALF_ROPE]], axis=1)


def _rope_tile(t, cs, sn):
    return t * cs + pltpu.roll(t, LANES // 2, 1) * sn


def _rope_tile_bwd(d, cs, sn):
    return d * cs + pltpu.roll(d * sn, LANES // 2, 1)


def _b_cols(tk, tn):
    return pl.BlockSpec((None, tk, tn), lambda i, j, k: (j, k, 0))


def _b_cols_t(tk, tn):
    return pl.BlockSpec((None, tn, tk), lambda i, j, k: (k, j, 0))


def _out_cols(shape):
    return shape, lambda tm, tn: pl.BlockSpec((None, tm, tn), lambda i, j, k: (j, i, 0))


def device_step(x, positions, target, w):
    seq = x.shape[0]

    inv_freq = ROPE_THETA ** (-jnp.arange(HALF_ROPE, dtype=F32) / HALF_ROPE)
    ang = positions.astype(F32)[:, None] * inv_freq
    cos, sin = jnp.cos(ang), jnp.sin(ang)
    zero = jnp.zeros_like(cos)
    cos_q, sin_q = jnp.concatenate([cos] * 4, 1), jnp.concatenate([-sin, -sin, sin, sin], 1)
    cos_k, sin_k = jnp.concatenate([cos, zero, cos, zero], 1), jnp.concatenate([-sin, zero, sin, zero], 1)
    w1, w2 = w["w_ff1"], w["w_ff2"]
    ff_tile = D_FF // N_CHIPS
    pack_shape = (N_CHIPS, G_PACK_ROWS, PACK_W)

    lr = w["ssm_lam_re"].reshape(N_STATES, 1)
    li = w["ssm_lam_im"].reshape(N_STATES, 1)
    ldt = jnp.repeat(w["ssm_log_dt"].reshape(N_GROUPS), SSM_STATE).reshape(N_STATES, 1)
    b_re = w["ssm_b_re"].reshape(N_STATES, SSM_GROUP)
    b_im = w["ssm_b_im"].reshape(N_STATES, SSM_GROUP)
    a_re, a_im, bb_re, bb_im = s5_prep(lr, li, ldt, b_re, b_im)
    a_re, a_im = a_re.reshape(STATE_TILES, LANES), a_im.reshape(STATE_TILES, LANES)
    bbd_re = _blockdiag_in(bb_re.reshape(N_GROUPS, SSM_STATE, SSM_GROUP)).astype(BF16)
    bbd_im = _blockdiag_in(bb_im.reshape(N_GROUPS, SSM_STATE, SSM_GROUP)).astype(BF16)
    cbd_re = _blockdiag_out(w["ssm_c_re"].reshape(N_GROUPS, SSM_GROUP, SSM_STATE)).astype(BF16)
    cbd_imn = _blockdiag_out(-w["ssm_c_im"].reshape(N_GROUPS, SSM_GROUP, SSM_STATE)).astype(BF16)
    dskip = w["ssm_d"].reshape(1, D_MODEL)
    ypre, h_re, h_im = s5_fwd(x, bbd_re, bbd_im, cbd_re, cbd_imn, a_re, a_im, dskip)
    (yg,) = rowwise(lambda y: ((_gelu(y),), ()), (ypre,), ((D_MODEL, BF16),), name="gelu")
    w_glu = w["ssm_w_glu"]
    glu_tile = w_glu.shape[2]
    vg = mm(yg, w_glu, n_dim=2 * D_MODEL, tiles=(None, glu_tile, None), b_view=_b_cols, name="glu_proj")

    def glu(v):
        return (v[:, :D_MODEL] * _sigmoid(v[:, D_MODEL:]),), ()
    (z,) = rowwise(glu, (vg,), ((D_MODEL, BF16),), name="glu")
    w_out = w["ssm_w_out"].reshape(D_MODEL, D_MODEL)
    mix0 = mm(z, w_out, name="ssm_out")

    def mlp_fwd(hb, layer):
        pre = mm(hb, w1, n_dim=D_FF, tiles=(None, ff_tile, None), name=f"ff1_{layer}",
                 b_view=lambda tk, tn: pl.BlockSpec((None, None, tk, tn), lambda i, j, k: (j, layer, k, 0)))
        f = mm(pre, w2, pro_a=_relu2, n_dim=D_MODEL, tiles=(None, D_MODEL, None), name=f"ff2_{layer}",
               b_view=lambda tk, tn: pl.BlockSpec((None, None, tk, tn),
                                                  lambda i, j, k: (k // (ff_tile // tk), layer, k % (ff_tile // tk), j)))
        return pre, f

    ln = lambda name, l: w[name][l].reshape(1, D_MODEL)
    h1, h1b = ln_fwd(x, mix0, ln("ln_mix_g", 0), ln("ln_mix_b", 0), "ln_mix_0")
    f1pre, f1 = mlp_fwd(h1b, 0)
    h2, h2b = ln_fwd(h1, f1, ln("ln_ffn_g", 0), ln("ln_ffn_b", 0), "ln_ffn_0")

    kv_w_a = w["kv_w_a"].reshape(D_MODEL, KVA_PAD)
    kv_w_b = w["kv_w_b"]
    q_w_a = w["q_w_a"].reshape(D_MODEL, Q_LORA)
    q_w_b = w["q_w_b"]
    w_o = w["attn_w_o"].reshape(D_MODEL, D_MODEL)
    kvb_tile = kv_w_b.shape[2]
    kvn_g = w["kv_norm_g"].reshape(1, KV_LORA)
    qn_g = w["q_norm_g"].reshape(1, Q_LORA)
    kva = mm(h2b, kv_w_a, name="kv_a")

    def kv_post(kva, g, cs, sn):
        return (_rms(kva[:, :KV_LORA], g), _rope_tile(kva[:, KV_LORA:], cs, sn)), ()
    ckv, krope = rowwise(kv_post, (kva, kvn_g, cos_k, sin_k), ((KV_LORA, BF16), (LANES, BF16)), name="kv_post")
    kvb = mm(ckv, kv_w_b, n_dim=N_CHIPS * kvb_tile, tiles=(None, kvb_tile, KV_LORA), b_view=_b_cols, name="kv_b",
             out_dtypes=(BF16,))
    cq_raw = mm(h2b, q_w_a, name="q_a")
    (cq,) = rowwise(lambda c, g: ((_rms(c, g),), ()), (cq_raw, qn_g), ((Q_LORA, BF16),), name="q_norm")
    qlin = mm(cq, q_w_b, n_dim=N_CHIPS * Q_CHIP, tiles=(None, Q_CHIP, Q_LORA), b_view=_b_cols, name="q_b")

    def on_rope_tiles(fn, scale=None):
        def apply(q, cs, sn):
            parts = []
            for k in range(N_CHIPS):
                parts.append(q[:, Q_CHIP * k:Q_CHIP * k + Q_CHIP_NOPE])
                parts.append(fn(q[:, Q_CHIP * k + Q_CHIP_NOPE:Q_CHIP * (k + 1)], cs, sn))
            out = jnp.concatenate(parts, axis=1)
            return (out if scale is None else out * scale,), ()
        return apply
    (qro,) = rowwise(on_rope_tiles(_rope_tile, Q_PRESCALE), (qlin, cos_q, sin_q), ((N_CHIPS * Q_CHIP, BF16),),
                     name="q_rope")
    qro3 = qro.reshape(seq, N_CHIPS, Q_CHIP)
    q_h = jnp.concatenate([qro3[:, :, :Q_CHIP_NOPE].reshape(seq, N_HEADS, QK_NOPE),
                           qro3[:, :, Q_CHIP_NOPE:Q_CHIP_NOPE + QK_ROPE].reshape(seq, N_HEADS, HALF_ROPE),
                           qro3[:, :, Q_CHIP_NOPE + QK_ROPE:].reshape(seq, N_HEADS, HALF_ROPE)], axis=2).transpose(1, 0, 2)
    kvb3 = kvb.reshape(seq, N_HEADS, QK_NOPE + V_HEAD)
    kr = jnp.concatenate([krope[:, :HALF_ROPE], krope[:, QK_ROPE:QK_ROPE + HALF_ROPE]], axis=1)
    k_h = jnp.concatenate([kvb3[:, :, :QK_NOPE], jnp.broadcast_to(kr[:, None, :], (seq, N_HEADS, QK_ROPE))],
                          axis=2).transpose(1, 0, 2)
    v2 = kvb3[:, :, QK_NOPE:].reshape(seq, N_HEADS * V_HEAD)
    o, lse = attn_fwd(q_h, k_h, v2)
    mix1 = mm(o, w_o, name="attn_out")
    h3, h3b = ln_fwd(h2, mix1, ln("ln_mix_g", 1), ln("ln_mix_b", 1), "ln_mix_1")
    f2pre, f2 = mlp_fwd(h3b, 1)
    h4, _ = ln_fwd(h3, f2, ln("ln_ffn_g", 1), ln("ln_ffn_b", 1), "ln_ffn_1")

    def loss_fn(y, t):
        e = y - t
        return (e * (1.0 / D_MODEL),), (jnp.broadcast_to(jnp.sum(e * e), (1, LANES)),)
    dh4, loss_acc = rowwise(loss_fn, (h4, target), ((D_MODEL, F32),), accs=(LANES,), name="loss")
    loss = loss_acc[0, 0] * (0.5 / D_MODEL)

    g = {}

    def into_rows(off, rows_per_chip):
        def view(tm, tn):
            nb = rows_per_chip // tm
            return pl.BlockSpec((None, tm, tn), lambda i, j, k: (i // nb, off // tm + i % nb, 0))
        return pack_shape, view

    def into_cols(off):
        return pack_shape, lambda tm, tn: pl.BlockSpec((None, tm, tn), lambda i, j, k: (j, off // tm + i, 0))

    def mlp_bwd(pack, dr, drb, hb, pre, layer):
        nb = lambda tk: ff_tile // tk
        dpre = mm(drb, w2, tb=True, epi=lambda r, p: (r * 2.0 * jnp.maximum(p, 0.0),), extras=(pre,),
                  out_dtypes=(BF16,), n_dim=D_FF, tiles=(None, ff_tile, None), name=f"ff2_dx_{layer}",
                  b_view=lambda tk, tn: pl.BlockSpec((None, None, tn, tk), lambda i, j, k: (j, layer, 0, k)))
        pack = mm(pre, drb, ta=True, pro_a=_relu2, name=f"ff2_dw_{layer}", tiles=(None, PACK_W, None), into=pack,
                  out_view=into_rows(DIRECT_OFF["w_ff2"] + layer * ff_tile, ff_tile))
        pack = mm(hb, dpre, ta=True, name=f"ff1_dw_{layer}", tiles=(None, PACK_W, None), into=pack,
                  out_view=into_cols(DIRECT_OFF["w_ff1"] + layer * D_MODEL))
        dh = mm(dpre, w1, tb=True, epi=lambda r, d: (r + DN_ALPHA * d,), extras=(dr,), n_dim=D_MODEL,
                tiles=(None, D_MODEL, None), name=f"ff1_dx_{layer}",
                b_view=lambda tk, tn: pl.BlockSpec((None, None, tn, tk), lambda i, j, k: (k // nb(tk), layer, 0, k % nb(tk))))
        return pack, dh

    dr4, dr4b, dg_f1, db_f1 = ln_bwd(h3, f2, ln("ln_ffn_g", 1), dh4, "ln_ffn_bwd_1")
    pack, dh3 = mlp_bwd(None, dr4, dr4b, h3b, f2pre, 1)
    dr3, dr3b, dg_m1, db_m1 = ln_bwd(h2, mix1, ln("ln_mix_g", 1), dh3, "ln_mix_bwd_1")
    shard_rows = D_MODEL // N_CHIPS
    pack = mm(o, dr3b, ta=True, name="attn_out_dw", tiles=(shard_rows, PACK_W, None), into=pack,
              out_view=into_rows(DIRECT_OFF["attn_w_o"], shard_rows))
    do = mm(dr3b, w_o, tb=True, name="attn_out_dx")
    dqn, dqr, delta = attn_bwd_dq(q_h, k_h, v2, do, o, lse)
    tb = min(ATT_TK, seq)
    lse_row = lse[:, :, 0].reshape(N_HEADS, seq // tb, tb)
    delta_row = delta[:, :, 0].reshape(N_HEADS, seq // tb, tb)
    dkn, dkr, dv = attn_bwd_dkv(q_h, k_h, v2, do, lse_row, delta_row)
    dqr_t = dqr.transpose(1, 0, 2)
    dq_cat = jnp.concatenate([dqn.reshape(seq, N_CHIPS, Q_CHIP_NOPE), dqr_t[:, :, :HALF_ROPE].reshape(seq, N_CHIPS, QK_ROPE),
                              dqr_t[:, :, HALF_ROPE:].reshape(seq, N_CHIPS, QK_ROPE)], 2).reshape(seq, N_CHIPS * Q_CHIP)
    (dqlin,) = rowwise(on_rope_tiles(_rope_tile_bwd), (dq_cat, cos_q, sin_q), ((N_CHIPS * Q_CHIP, BF16),), name="q_rope_bwd")
    g["q_w_b"] = mm(cq, dqlin, ta=True, name="q_b_dw", tiles=(Q_LORA, Q_CHIP, None), out_view=_out_cols(q_w_b.shape))
    dcq = mm(dqlin, q_w_b, tb=True, n_dim=Q_LORA, tiles=(None, Q_LORA, Q_CHIP), b_view=_b_cols_t, name="q_b_dx")

    def q_norm_bwd(c, gq, d):
        dx, dgq = _rms_bwd(c, gq, d)
        return (dx,), (dgq,)
    dcq_raw, dqn_g = rowwise(q_norm_bwd, (cq_raw, qn_g, dcq), ((Q_LORA, BF16),), accs=(Q_LORA,), name="q_norm_bwd")
    g["q_w_a"] = mm(h2b, dcq_raw, ta=True, name="q_a_dw")
    dkvb = jnp.concatenate([dkn.reshape(seq, N_HEADS, QK_NOPE), dv.reshape(seq, N_HEADS, V_HEAD)], 2).reshape(
        seq, N_HEADS * (QK_NOPE + V_HEAD)).astype(BF16)
    g["kv_w_b"] = mm(ckv, dkvb, ta=True, name="kv_b_dw", tiles=(KV_LORA, kvb_tile, None), out_view=_out_cols(kv_w_b.shape))
    dckv = mm(dkvb, kv_w_b, tb=True, n_dim=KV_LORA, tiles=(None, KV_LORA, kvb_tile), b_view=_b_cols_t, name="kv_b_dx")
    dkr_sum = head_sum(dkr)
    zpad = jnp.zeros((seq, HALF_ROPE), F32)
    dkr_tile = jnp.concatenate([dkr_sum[:, :HALF_ROPE], zpad, dkr_sum[:, HALF_ROPE:], zpad], 1)

    def kv_post_bwd(kva, gk, dc, dk, cs, sn):
        dx, dgk = _rms_bwd(kva[:, :KV_LORA], gk, dc)
        return (jnp.concatenate([dx, _rope_tile_bwd(dk, cs, sn)], axis=1),), (dgk,)
    dkva, dkvn_g = rowwise(kv_post_bwd, (kva, kvn_g, dckv, dkr_tile, cos_k, sin_k), ((KVA_PAD, BF16),),
                           accs=(KV_LORA,), name="kv_post_bwd")
    g["kv_w_a"] = mm(h2b, dkva, ta=True, name="kv_a_dw")
    dh2 = mm(dcq_raw, q_w_a, tb=True, epi=lambda r, d: (r + DN_ALPHA * d,), extras=(dr3,), name="q_a_dx")
    dh2 = mm(dkva, kv_w_a, tb=True, epi=lambda r, d: (r + d,), extras=(dh2,), name="kv_a_dx")

    dr2, dr2b, dg_f0, db_f0 = ln_bwd(h1, f1, ln("ln_ffn_g", 0), dh2, "ln_ffn_bwd_0")
    pack, dh1 = mlp_bwd(pack, dr2, dr2b, h1b, f1pre, 0)
    dr1, dr1b, dg_m0, db_m0 = ln_bwd(x, mix0, ln("ln_mix_g", 0), dh1, "ln_mix_bwd_0")
    pack = mm(z, dr1b, ta=True, name="ssm_out_dw", tiles=(shard_rows, PACK_W, None), into=pack,
              out_view=into_rows(DIRECT_OFF["ssm_w_out"], shard_rows))
    dz = mm(dr1b, w_out, tb=True, name="ssm_out_dx")

    def glu_bwd(v, dz):
        val, sg = v[:, :D_MODEL], _sigmoid(v[:, D_MODEL:])
        return (jnp.concatenate([dz * sg, dz * val * sg * (1.0 - sg)], axis=1),), ()
    (dvg,) = rowwise(glu_bwd, (vg, dz), ((2 * D_MODEL, BF16),), name="glu_bwd")
    g["ssm_w_glu"] = mm(yg, dvg, ta=True, name="glu_proj_dw", tiles=(None, glu_tile, None), out_view=_out_cols(w_glu.shape))
    dypre = mm(dvg, w_glu, tb=True, epi=lambda r, y: (r * _gelu_grad(y),), extras=(ypre,), n_dim=D_MODEL,
               tiles=(None, D_MODEL, glu_tile), b_view=_b_cols_t, name="glu_proj_dx")
    dx, dbbd_re, dbbd_im, dcbd_re, dcbd_imn, dar, dai, dd = s5_bwd(
        dypre, x, dr1, h_re, h_im, bbd_re, bbd_im, cbd_re, cbd_imn, a_re, a_im, dskip)
    dbb_re = _blockdiag_in_t(dbbd_re).reshape(N_STATES, SSM_GROUP)
    dbb_im = _blockdiag_in_t(dbbd_im).reshape(N_STATES, SSM_GROUP)
    dlr, dli, dldt, db_re, db_im = s5_prep_bwd(lr, li, ldt, b_re, b_im, dar.reshape(N_STATES, 1),
                                               dai.reshape(N_STATES, 1), dbb_re, dbb_im)
    g["ssm_lam_re"] = dlr.reshape(1, N_GROUPS, SSM_STATE)
    g["ssm_lam_im"] = dli.reshape(1, N_GROUPS, SSM_STATE)
    g["ssm_log_dt"] = group_sum(dldt).reshape(1, N_GROUPS)
    g["ssm_b_re"] = db_re.reshape(1, N_GROUPS, SSM_STATE, SSM_GROUP)
    g["ssm_b_im"] = db_im.reshape(1, N_GROUPS, SSM_STATE, SSM_GROUP)
    g["ssm_c_re"] = _blockdiag_out_t(dcbd_re).reshape(1, N_GROUPS, SSM_GROUP, SSM_STATE)
    g["ssm_c_im"] = -_blockdiag_out_t(dcbd_imn).reshape(1, N_GROUPS, SSM_GROUP, SSM_STATE)
    g["ssm_d"] = dd
    g["ln_mix_g"] = jnp.concatenate([dg_m0, dg_m1], 0)
    g["ln_mix_b"] = jnp.concatenate([db_m0, db_m1], 0)
    g["ln_ffn_g"] = jnp.concatenate([dg_f0, dg_f1], 0)
    g["ln_ffn_b"] = jnp.concatenate([db_f0, db_f1], 0)
    g["kv_norm_g"] = dkvn_g.reshape(KV_LORA)
    g["q_norm_g"] = dqn_g
    return loss, dx, pack, g


def _place():
    x, y, c = lax.axis_index("x"), lax.axis_index("y"), lax.axis_index("c")
    return x, y, c, [(1 - x, y), (x, 1 - y), (1 - x, 1 - y)]


def place(shard, me_idx, dtype, name):
    rows, cols = shard.shape
    tr = _tile(rows, (512, 256, 128))

    def body(m_ref, x_ref, o_ref):
        o_ref[...] = x_ref[...].astype(o_ref.dtype)

    return _pcall(
        body, name=name,
        grid_spec=pltpu.PrefetchScalarGridSpec(
            num_scalar_prefetch=1, grid=(rows // tr,),
            in_specs=[pl.BlockSpec((tr, cols), lambda i, m: (i, 0))],
            out_specs=pl.BlockSpec((None, tr, cols), lambda i, m: (m[0], i, 0))),
        out_shape=jax.ShapeDtypeStruct((N_CHIPS, rows, cols), dtype),
        compiler_params=_params(("parallel",)),
    )(me_idx, shard)


def gather_stacked(arrs, name):
    n = len(arrs)

    def body(*refs):
        outs, send_sems, recv_sems = refs[n:2 * n], refs[2 * n], refs[2 * n + 1]
        x, y, c, chips = _place()
        sibling = (x, y, 1 - c)
        me = 2 * x + y

        def copy(k, blk, to):
            return pltpu.make_async_remote_copy(src_ref=blk, dst_ref=blk, send_sem=send_sems.at[k],
                                                recv_sem=recv_sems.at[k], device_id=to, device_id_type=MESH)

        started = []
        for a, o in enumerate(outs):
            for j, (px, py) in enumerate(chips):
                cp = copy(6 * a + j, o.at[me, c], (px, py, c))
                cp.start()
                started.append(cp)
        for a, o in enumerate(outs):
            for j, (px, py) in enumerate(chips):
                blk = o.at[2 * px + py, c]
                copy(6 * a + j, blk, (px, py, c)).wait_recv()
                cp = copy(6 * a + 3 + j, blk, sibling)
                cp.start()
                started.append(cp)
        for a, o in enumerate(outs):
            for j, (px, py) in enumerate(chips):
                copy(6 * a + 3 + j, o.at[2 * px + py, 1 - c], sibling).wait_recv()
        for cp in started:
            cp.wait_send()

    return _pcall(body, name=name, in_specs=[_ANY] * n, out_specs=[_ANY] * n,
                  out_shape=[jax.ShapeDtypeStruct(a.shape, a.dtype) for a in arrs],
                  input_output_aliases={i: i for i in range(n)},
                  scratch_shapes=[pltpu.SemaphoreType.DMA((6 * n,)), pltpu.SemaphoreType.DMA((6 * n,))])(*arrs)


def put_rows(pack, rows, off):
    _, n, cols = rows.shape
    tr = math.gcd(math.gcd(off, n), 512)

    def body(r_ref, p_ref, o_ref):
        o_ref[...] = r_ref[...]

    return _pcall(body, name="grad_put_rows", grid=(N_CHIPS, n // tr),
                  in_specs=[pl.BlockSpec((None, tr, cols), lambda k, i: (k, i, 0)), _ANY],
                  out_specs=pl.BlockSpec((None, tr, cols), lambda k, i: (k, off // tr + i, 0)),
                  out_shape=jax.ShapeDtypeStruct(pack.shape, pack.dtype), input_output_aliases={1: 0},
                  compiler_params=_params(("parallel", "parallel")))(rows, pack)


def _my_cols(c, mine=True):
    start = (c if mine else 1 - c) * HALF_W
    return pl.ds(pl.multiple_of(start, HALF_W), HALF_W)


def swap_halves(gpack):
    n, rows, _ = gpack.shape

    def body(g_ref, got_ref, send_sem, recv_sem):
        x, y, c, _ = _place()
        cp = pltpu.make_async_remote_copy(src_ref=g_ref.at[:, :, _my_cols(c, mine=False)], dst_ref=got_ref,
                                          send_sem=send_sem, recv_sem=recv_sem, device_id=(x, y, 1 - c),
                                          device_id_type=MESH)
        cp.start()
        cp.wait()

    return _pcall(body, name="grad_swap_halves", in_specs=[_ANY], out_specs=_ANY,
                  out_shape=jax.ShapeDtypeStruct((n, rows, HALF_W), gpack.dtype),
                  scratch_shapes=[pltpu.SemaphoreType.DMA, pltpu.SemaphoreType.DMA])(gpack)


def add_halves(gpack, got, c_idx):
    n, rows, _ = gpack.shape
    blk = (None, G_BLOCK_ROWS, HALF_W)

    def body(c_ref, g_ref, r_ref, o_ref):
        o_ref[...] = (g_ref[...] + r_ref[...]).astype(o_ref.dtype)

    return _pcall(
        body, name="grad_add_halves",
        grid_spec=pltpu.PrefetchScalarGridSpec(
            num_scalar_prefetch=1, grid=(n, rows // G_BLOCK_ROWS),
            in_specs=[pl.BlockSpec(blk, lambda k, i, c: (k, i, c[0])), pl.BlockSpec(blk, lambda k, i, c: (k, i, 0))],
            out_specs=pl.BlockSpec(blk, lambda k, i, c: (k, i, 0))),
        out_shape=jax.ShapeDtypeStruct((n, rows, HALF_W), BF16),
        compiler_params=_params(("parallel", "parallel")),
    )(c_idx, gpack, got)


def send_to_owners(part):
    _, rh, cols = part.shape

    def body(p_ref, got_ref, send_sems, recv_sems):
        x, y, c, chips = _place()
        cps = [pltpu.make_async_remote_copy(src_ref=p_ref.at[2 * px + py], dst_ref=got_ref.at[j],
                                            send_sem=send_sems.at[j], recv_sem=recv_sems.at[j],
                                            device_id=(px, py, c), device_id_type=MESH)
               for j, (px, py) in enumerate(chips)]
        for cp in cps:
            cp.start()
        for cp in cps:
            cp.wait()

    return _pcall(body, name="grad_send_to_owners", in_specs=[_ANY], out_specs=_ANY,
                  out_shape=jax.ShapeDtypeStruct((3, rh, cols), part.dtype),
                  scratch_shapes=[pltpu.SemaphoreType.DMA((3,)), pltpu.SemaphoreType.DMA((3,))])(part)


def sum_owner(part, got, idx):
    _, rows, _ = part.shape
    tr = G_BLOCK_ROWS

    def body(m_ref, p_ref, g_ref, o_ref):
        up = lambda v: v.astype(F32)
        o_ref[...] = ((up(p_ref[...]) + up(g_ref[0])) + up(g_ref[1])) + up(g_ref[2])

    return _pcall(
        body, name="grad_sum_owner",
        grid_spec=pltpu.PrefetchScalarGridSpec(
            num_scalar_prefetch=1, grid=(rows // tr,),
            in_specs=[pl.BlockSpec((None, tr, HALF_W), lambda i, m: (m[0], i, 0)),
                      pl.BlockSpec((3, tr, HALF_W), lambda i, m: (0, i, 0))],
            out_specs=pl.BlockSpec((tr, HALF_W), lambda i, m: (i, m[1]))),
        out_shape=jax.ShapeDtypeStruct((rows, PACK_W), F32),
        compiler_params=_params(("parallel",)),
    )(idx, part, got)


def join_halves(red):
    def body(in_ref, out_ref, send_sem, recv_sem):
        x, y, c, _ = _place()
        sibling = (x, y, 1 - c)
        mine = out_ref.at[:, _my_cols(c)]
        cp = pltpu.make_async_remote_copy(src_ref=mine, dst_ref=mine, send_sem=send_sem, recv_sem=recv_sem,
                                          device_id=sibling, device_id_type=MESH)
        cp.start()
        cp.wait_send()
        other = out_ref.at[:, _my_cols(c, mine=False)]
        pltpu.make_async_remote_copy(src_ref=other, dst_ref=other, send_sem=send_sem, recv_sem=recv_sem,
                                     device_id=sibling, device_id_type=MESH).wait_recv()

    return _pcall(body, name="grad_join_halves", in_specs=[_ANY], out_specs=_ANY,
                  out_shape=jax.ShapeDtypeStruct(red.shape, red.dtype), input_output_aliases={0: 0},
                  scratch_shapes=[pltpu.SemaphoreType.DMA, pltpu.SemaphoreType.DMA])(red)


def adamw(gsrc, g_off, wt, m, v, name):
    n, cols = wt.shape
    tr = math.gcd(math.gcd(g_off, n), 256) if g_off else math.gcd(n, 256)
    off_blk = g_off // tr
    c1 = 1.0 / (1.0 - ADAM_B1 ** ADAM_STEP)
    c2 = 1.0 / (1.0 - ADAM_B2 ** ADAM_STEP)

    def body(g_ref, w_ref, m_ref, v_ref, go_ref, d_ref, mo_ref, vo_ref):
        gv = g_ref[...]
        mn = ADAM_B1 * m_ref[...] + (1.0 - ADAM_B1) * gv
        vn = ADAM_B2 * v_ref[...] + (1.0 - ADAM_B2) * gv * gv
        go_ref[...] = gv
        mo_ref[...] = mn
        vo_ref[...] = vn
        d_ref[...] = -ADAM_LR * ((mn * c1) / (jnp.sqrt(vn * c2) + ADAM_EPS) + ADAM_WD * w_ref[...])

    blk = pl.BlockSpec((tr, cols), lambda i: (i, 0))
    return _pcall(body, name=name, grid=(n // tr,),
                  in_specs=[pl.BlockSpec((tr, cols), lambda i: (off_blk + i, 0)), blk, blk, blk],
                  out_specs=[blk] * 4, out_shape=[jax.ShapeDtypeStruct((n, cols), F32)] * 4,
                  compiler_params=_params(("parallel",)))(gsrc, wt, m, v)


def _rows8(a):
    return -(-a.size // (8 * PACK_W)) * 8


def _as_rows(a, rows=None):
    flat = a.reshape(-1)
    n = _rows8(a) if rows is None else rows
    return jnp.pad(flat, (0, n * PACK_W - flat.shape[0])).reshape(n, PACK_W)


def local_shards_2d(wl):
    return {"w_ff1": wl["w_ff1"].reshape(2 * D_MODEL, D_FF // N_CHIPS), "w_ff2": wl["w_ff2"].reshape(2 * D_FF // N_CHIPS, D_MODEL),
            "ssm_w_glu": wl["ssm_w_glu"], "ssm_w_out": wl["ssm_w_out"], "kv_w_a": _pad_kva_cols(wl["kv_w_a"]),
            "kv_w_b": wl["kv_w_b"], "q_w_a": wl["q_w_a"], "q_w_b": _perm_q_cols(wl["q_w_b"]),
            "attn_w_o": wl["attn_w_o"], "ssm_d": wl["ssm_d"].reshape(2, -1)}


def misc_grad_shard(name, g, k):
    if name == "ssm_d":
        w = D_MODEL // N_CHIPS
        return g[:, w * k:w * (k + 1)]
    if name in ("ssm_w_glu", "kv_w_b"):
        return g[k]
    if name == "q_w_b":
        return _unperm_q_cols(g[k])
    rows = D_MODEL // N_CHIPS
    shard = g[rows * k:rows * (k + 1)]
    return _unpad_kva_cols(shard) if name == "kv_w_a" else shard


def kernel(x, positions, ln_mix_g, ln_mix_b, ln_ffn_g, ln_ffn_b, w_ff1, w_ff2, ssm_lam_re, ssm_lam_im, ssm_log_dt, ssm_b_re, ssm_b_im, ssm_c_re, ssm_c_im, ssm_d, ssm_w_glu, ssm_w_out, kv_w_a, kv_norm_g, kv_w_b, q_w_a, q_norm_g, q_w_b, attn_w_o, loss_target, m_ln_mix_g, m_ln_mix_b, m_ln_ffn_g, m_ln_ffn_b, m_w_ff1, m_w_ff2, m_ssm_lam_re, m_ssm_lam_im, m_ssm_log_dt, m_ssm_b_re, m_ssm_b_im, m_ssm_c_re, m_ssm_c_im, m_ssm_d, m_ssm_w_glu, m_ssm_w_out, m_kv_w_a, m_kv_norm_g, m_kv_w_b, m_q_w_a, m_q_norm_g, m_q_w_b, m_attn_w_o, v_ln_mix_g, v_ln_mix_b, v_ln_ffn_g, v_ln_ffn_b, v_w_ff1, v_w_ff2, v_ssm_lam_re, v_ssm_lam_im, v_ssm_log_dt, v_ssm_b_re, v_ssm_b_im, v_ssm_c_re, v_ssm_c_im, v_ssm_d, v_ssm_w_glu, v_ssm_w_out, v_kv_w_a, v_kv_norm_g, v_kv_w_b, v_q_w_a, v_q_norm_g, v_q_w_b, v_attn_w_o):
    env = dict(locals())
    wl = {n: env[n] for n in WEIGHTS}
    ml = {n: env["m_" + n] for n in WEIGHTS}
    vl = {n: env["v_" + n] for n in WEIGHTS}
    for n in ("ssm_w_glu", "ssm_w_out", "q_w_a", "q_w_b", "attn_w_o"):
        wl[n], ml[n], vl[n] = wl[n][0], ml[n][0], vl[n][0]

    c_idx = lax.axis_index("c").astype(jnp.int32).reshape(1)
    me_idx = (2 * lax.axis_index("x") + lax.axis_index("y")).astype(jnp.int32).reshape(1)

    local = local_shards_2d(wl)
    placed = [place(local[n], me_idx, F32 if n == "ssm_d" else BF16, "place_" + n) for n in SHARDED]
    halves = [p.reshape(N_CHIPS, 2, p.shape[1] // 2, p.shape[2]) for p in placed]
    gathered = gather_stacked(halves, "weight_all_gather")
    full = {n: a.reshape(p.shape) for n, a, p in zip(SHARDED, gathered, placed)}
    for n in ("w_ff1", "w_ff2"):
        full[n] = full[n].reshape(N_CHIPS, 2, D_MODEL, D_MODEL)
    full["ssm_d"] = full["ssm_d"].reshape(1, D_MODEL)
    for n in REPLICATED:
        full[n] = wl[n]

    loss_part, dx, gpack, g = device_step(x[0], positions[0], loss_target[0], full)
    loss = lax.psum(loss_part, ("x", "y", "c"))

    small = jnp.concatenate([_as_rows(g[n]) for n in REPLICATED], axis=0)
    small = jnp.pad(small, ((0, SMALL_ROWS - small.shape[0]), (0, 0)))
    blocks = []
    for k in range(N_CHIPS):
        rows = [small[SMALL_Q_ROWS * k:SMALL_Q_ROWS * (k + 1)]]
        rows += [_as_rows(misc_grad_shard(n, g[n], k), MISC_SHARD_ROWS[n]) for n in MISC_SHARDED]
        blk = jnp.concatenate(rows, axis=0)
        blocks.append(jnp.pad(blk, ((0, MISC_ROWS - blk.shape[0]), (0, 0))))
    gpack = put_rows(gpack, jnp.stack(blocks), MISC_OFF)
    chip_part = add_halves(gpack, swap_halves(gpack), c_idx)
    reduced = join_halves(sum_owner(chip_part, send_to_owners(chip_part), jnp.concatenate([me_idx, c_idx])))
    quarter = reduced[MISC_OFF:MISC_OFF + SMALL_Q_ROWS]
    small_tot = gather_stacked([place(quarter, me_idx, F32, "place_small_grads").reshape(
        N_CHIPS, 2, SMALL_Q_ROWS // 2, PACK_W)], "small_grad_all_gather")[0].reshape(SMALL_ROWS, PACK_W)

    out_g, out_d, out_m, out_v = {}, {}, {}, {}
    for n in DIRECT_OFF:
        res = adamw(reduced, DIRECT_OFF[n], wl[n].reshape(-1, PACK_W), ml[n].reshape(-1, PACK_W),
                    vl[n].reshape(-1, PACK_W), "adamw_" + n)
        out_g[n], out_d[n], out_m[n], out_v[n] = [a.reshape(env[n].shape) for a in res]
    pack3 = lambda d: jnp.concatenate([_as_rows(d[n], MISC_SHARD_ROWS[n]) for n in MISC_SHARDED], axis=0)
    res = adamw(reduced, MISC_OFF + SMALL_Q_ROWS, pack3(wl), pack3(ml), pack3(vl), "adamw_row_packed")
    for n in MISC_SHARDED:
        cnt = math.prod(env[n].shape)
        r0 = MISC_SHARD_OFF[n] - SMALL_Q_ROWS
        out_g[n], out_d[n], out_m[n], out_v[n] = [
            a[r0:r0 + MISC_SHARD_ROWS[n]].reshape(-1)[:cnt].reshape(env[n].shape) for a in res]
    ws = jnp.concatenate([_as_rows(wl[n]) for n in REPLICATED], axis=0)
    ms = jnp.concatenate([_as_rows(ml[n]) for n in REPLICATED], axis=0)
    vs = jnp.concatenate([_as_rows(vl[n]) for n in REPLICATED], axis=0)
    pad = ((0, SMALL_ROWS - ws.shape[0]), (0, 0))
    res = adamw(small_tot, 0, jnp.pad(ws, pad), jnp.pad(ms, pad), jnp.pad(vs, pad), "adamw_replicated")
    row = 0
    for n in REPLICATED:
        cnt = math.prod(env[n].shape)
        nrows = _rows8(env[n])
        out_g[n], out_d[n], out_m[n], out_v[n] = [a[row:row + nrows].reshape(-1)[:cnt].reshape(env[n].shape) for a in res]
        row += nrows

    return (loss, dx[None], *[out_g[n] for n in WEIGHTS], *[out_d[n] for n in WEIGHTS],
            *[out_m[n] for n in WEIGHTS], *[out_v[n] for n in WEIGHTS])
```

```python
import functools
import math

import jax
import jax.numpy as jnp
from jax import lax
from jax.experimental import pallas as pl
from jax.experimental.pallas import tpu as pltpu

F32 = jnp.float32
BF16 = jnp.bfloat16
MESH = pl.DeviceIdType.MESH

D_MODEL = 1024
DEPTH = 2
SSM_GROUP = 16
N_GROUPS = D_MODEL // SSM_GROUP
SSM_STATE = 64
N_STATES = N_GROUPS * SSM_STATE
N_HEADS = 8
QK_NOPE = 128
QK_ROPE = 64
HALF_ROPE = QK_ROPE // 2
V_HEAD = 128
QK_DIM = QK_NOPE + QK_ROPE
Q_LORA = 384
KV_LORA = 256
ROPE_THETA = 10000.0
SM_SCALE = QK_DIM ** -0.5
NEG_INF = -1e30
D_FF = 4 * D_MODEL
DN_ALPHA = (2 * DEPTH) ** 0.25
LN_EPS = 1e-5
RMS_EPS = 1e-6
ADAM_LR = 0.001
ADAM_B1 = 0.9
ADAM_B2 = 0.999
ADAM_EPS = 1e-08
ADAM_WD = 0.01
ADAM_STEP = 10

N_CHIPS = 4
LANES = 128
VMEM_LIMIT = 56 * 1024 * 1024
PACK_W = 1024
KVA_PAD = 384
HALF_W = PACK_W // 2

SHARDED = ("w_ff1", "w_ff2", "ssm_w_glu", "ssm_w_out", "kv_w_a", "kv_w_b", "q_w_a", "q_w_b", "attn_w_o", "ssm_d")
DIRECT_OFF = {"w_ff1": 0, "w_ff2": 2048, "ssm_w_out": 4096, "attn_w_o": 4352}
DIRECT_ROWS = {"w_ff1": 2048, "w_ff2": 2048, "ssm_w_out": 256, "attn_w_o": 256}
MISC_OFF = 4608
SMALL_Q_ROWS = 96
SMALL_ROWS = N_CHIPS * SMALL_Q_ROWS
MISC_SHARDED = ("ssm_d", "ssm_w_glu", "kv_w_b", "kv_w_a", "q_w_a", "q_w_b")
MISC_SHARD_ROWS = {"ssm_d": 16, "ssm_w_glu": 512, "kv_w_b": 128, "kv_w_a": 80, "q_w_a": 96, "q_w_b": 144}
MISC_SHARD_OFF = {}
_o = SMALL_Q_ROWS
for _n in MISC_SHARDED:
    MISC_SHARD_OFF[_n] = _o
    _o += MISC_SHARD_ROWS[_n]
MISC_USED = _o
G_PACK_ROWS = 5760
MISC_ROWS = G_PACK_ROWS - MISC_OFF
G_BLOCK_ROWS = 960
REPLICATED = ("ln_mix_g", "ln_mix_b", "ln_ffn_g", "ln_ffn_b", "ssm_lam_re", "ssm_lam_im", "ssm_log_dt",
              "ssm_b_re", "ssm_b_im", "ssm_c_re", "ssm_c_im", "kv_norm_g", "q_norm_g")
WEIGHTS = ("ln_mix_g", "ln_mix_b", "ln_ffn_g", "ln_ffn_b", "w_ff1", "w_ff2", "ssm_lam_re", "ssm_lam_im",
           "ssm_log_dt", "ssm_b_re", "ssm_b_im", "ssm_c_re", "ssm_c_im", "ssm_d", "ssm_w_glu", "ssm_w_out",
           "kv_w_a", "kv_norm_g", "kv_w_b", "q_w_a", "q_norm_g", "q_w_b", "attn_w_o")


def _pcall(body, **kw):
    return pl.pallas_call(body, **kw)


def _params(sem=None):
    return pltpu.CompilerParams(dimension_semantics=sem, vmem_limit_bytes=VMEM_LIMIT)


_ANY = pl.BlockSpec(memory_space=pl.ANY)


def _tile(dim, prefs):
    for p in prefs:
        if dim % p == 0:
            return p
    return dim


def mm(a, b, *, name, ta=False, tb=False, pro_a=None, epi=None, extras=(), out_dtypes=(F32,), n_dim=None,
       tiles=(None, None, None), b_view=None, out_view=None, into=None):
    if ta:
        k_dim, m_dim = a.shape
    else:
        m_dim, k_dim = a.shape
    if n_dim is None:
        n_dim = b.shape[0] if tb else b.shape[1]
    tm = tiles[0] or _tile(m_dim, (1024, 512, 256, 128))
    tn = tiles[1] or _tile(n_dim, (1024, 512, 256, 128))
    tk = tiles[2] or (k_dim if k_dim <= 1024 else _tile(k_dim, (1024, 512, 256, 128)))
    assert m_dim % tm == 0 and n_dim % tn == 0 and k_dim % tk == 0, (name, m_dim, n_dim, k_dim, tm, tn, tk)
    nk = k_dim // tk
    n_ex, n_out = len(extras), len(out_dtypes)
    n_into = 0 if into is None else 1
    dims = (((0 if ta else 1,), (1 if tb else 0,)), ((), ()))

    def body(a_ref, b_ref, *rest):
        ex_refs, out_refs = rest[:n_ex], rest[n_ex + n_into:n_ex + n_into + n_out]

        def partial():
            av = a_ref[...]
            if pro_a is not None:
                av = pro_a(av)
            return lax.dot_general(av.astype(BF16), b_ref[...].astype(BF16), dims, preferred_element_type=F32)

        def finish(r):
            res = epi(r, *[e[...] for e in ex_refs]) if epi is not None else (r,)
            for o_ref, v in zip(out_refs, res):
                o_ref[...] = v.astype(o_ref.dtype)

        if nk == 1:
            finish(partial())
            return
        acc = rest[-1]
        k = pl.program_id(2)

        @pl.when(k == 0)
        def _():
            acc[...] = partial()

        @pl.when(k > 0)
        def _():
            acc[...] += partial()

        @pl.when(k == nk - 1)
        def _():
            finish(acc[...])

    a_spec = pl.BlockSpec((tk, tm), lambda i, j, k: (k, i)) if ta else pl.BlockSpec((tm, tk), lambda i, j, k: (i, k))
    if b_view is not None:
        b_spec = b_view(tk, tn)
    else:
        b_spec = pl.BlockSpec((tn, tk), lambda i, j, k: (j, k)) if tb else pl.BlockSpec((tk, tn), lambda i, j, k: (k, j))
    o_spec = pl.BlockSpec((tm, tn), lambda i, j, k: (i, j))
    if out_view is None:
        out_specs = [o_spec] * n_out
        out_shape = [jax.ShapeDtypeStruct((m_dim, n_dim), dt) for dt in out_dtypes]
    else:
        assert n_out == 1
        out_specs = [out_view[1](tm, tn)]
        out_shape = [jax.ShapeDtypeStruct(out_view[0], out_dtypes[0])]
    outs = _pcall(
        body, name=name, grid=(m_dim // tm, n_dim // tn, nk),
        in_specs=[a_spec, b_spec] + [o_spec] * n_ex + [_ANY] * n_into,
        out_specs=out_specs, out_shape=out_shape,
        input_output_aliases={2 + n_ex: 0} if n_into else {},
        scratch_shapes=[pltpu.VMEM((tm, tn), F32)] if nk > 1 else [],
        compiler_params=_params(("parallel", "parallel", "arbitrary")),
    )(a, b, *extras, *([into] if n_into else []))
    return outs[0] if n_out == 1 else outs


def rowwise(fn, ins, outs, *, name, accs=(), tm=256):
    rows = ins[0].shape[0]
    tm = min(tm, rows)
    n_in, n_out, n_acc = len(ins), len(outs), len(accs)

    def body(*refs):
        in_refs, out_refs, acc_refs = refs[:n_in], refs[n_in:n_in + n_out], refs[n_in + n_out:]
        res, sums = fn(*[r[...] for r in in_refs])
        for o_ref, v in zip(out_refs, res):
            o_ref[...] = v.astype(o_ref.dtype)
        if n_acc:
            @pl.when(pl.program_id(0) == 0)
            def _():
                for a_ref in acc_refs:
                    a_ref[...] = jnp.zeros_like(a_ref)

            for a_ref, s in zip(acc_refs, sums):
                a_ref[...] += s

    def spec(arr):
        if arr.shape[0] == rows:
            return pl.BlockSpec((tm, arr.shape[1]), lambda i: (i, 0))
        return pl.BlockSpec(arr.shape, lambda i: (0, 0))

    res = _pcall(
        body, name=name, grid=(rows // tm,),
        in_specs=[spec(a) for a in ins],
        out_specs=[pl.BlockSpec((tm, w), lambda i: (i, 0)) for w, _ in outs]
        + [pl.BlockSpec((1, w), lambda i: (0, 0)) for w in accs],
        out_shape=[jax.ShapeDtypeStruct((rows, w), dt) for w, dt in outs]
        + [jax.ShapeDtypeStruct((1, w), F32) for w in accs],
        compiler_params=_params(("arbitrary",) if n_acc else ("parallel",)),
    )(*ins)
    return res


def _relu2(v):
    r = jnp.maximum(v, 0.0)
    return r * r


def _gelu(x):
    c = math.sqrt(2.0 / math.pi)
    return 0.5 * x * (1.0 + jnp.tanh(c * (x + 0.044715 * x * x * x)))


def _gelu_grad(x):
    c = math.sqrt(2.0 / math.pi)
    t = jnp.tanh(c * (x + 0.044715 * x * x * x))
    return 0.5 * (1.0 + t) + 0.5 * x * (1.0 - t * t) * c * (1.0 + 3 * 0.044715 * x * x)


def _sigmoid(x):
    return 1.0 / (1.0 + jnp.exp(-x))


def ln_fwd(h, mix, g, b, name):
    def fn(h, mix, g, b):
        r = DN_ALPHA * h + mix
        mu = jnp.mean(r, axis=-1, keepdims=True)
        xc = r - mu
        var = jnp.mean(xc * xc, axis=-1, keepdims=True)
        y = xc * lax.rsqrt(var + LN_EPS) * g + b
        return (y, y), ()
    return rowwise(fn, (h, mix, g, b), ((D_MODEL, F32), (D_MODEL, BF16)), name=name)


def ln_bwd(h, mix, g, dy, name):
    def fn(h, mix, g, dy):
        r = DN_ALPHA * h + mix
        mu = jnp.mean(r, axis=-1, keepdims=True)
        xc = r - mu
        var = jnp.mean(xc * xc, axis=-1, keepdims=True)
        rstd = lax.rsqrt(var + LN_EPS)
        xhat = xc * rstd
        dxh = dy * g
        m1 = jnp.mean(dxh, axis=-1, keepdims=True)
        m2 = jnp.mean(dxh * xhat, axis=-1, keepdims=True)
        dr = rstd * (dxh - m1 - xhat * m2)
        return (dr, dr), (jnp.sum(dy * xhat, axis=0, keepdims=True), jnp.sum(dy, axis=0, keepdims=True))
    return rowwise(fn, (h, mix, g, dy), ((D_MODEL, F32), (D_MODEL, BF16)), accs=(D_MODEL, D_MODEL), name=name)


def _rms(x, g):
    r = lax.rsqrt(jnp.mean(x * x, axis=-1, keepdims=True) + RMS_EPS)
    return x * r * g


def _rms_bwd(x, g, dy):
    r = lax.rsqrt(jnp.mean(x * x, axis=-1, keepdims=True) + RMS_EPS)
    xn = x * r
    dyg = dy * g
    dx = r * (dyg - xn * jnp.mean(dyg * xn, axis=-1, keepdims=True))
    return dx, jnp.sum(dy * xn, axis=0, keepdims=True)


def _s5_disc(lr, li, ldt):
    dt = jnp.exp(ldt)
    mag = jnp.exp(lr * dt)
    cs, sn = jnp.cos(li * dt), jnp.sin(li * dt)
    ar, ai = mag * cs, mag * sn
    inv = 1.0 / (lr * lr + li * li)
    n_re = (ar - 1.0) * lr + ai * li
    n_im = ai * lr - (ar - 1.0) * li
    return dt, mag, cs, sn, ar, ai, inv, n_re, n_im


def s5_prep(lr, li, ldt, b_re, b_im):
    def fn(lr, li, ldt, b_re, b_im):
        _, _, _, _, ar, ai, inv, n_re, n_im = _s5_disc(lr, li, ldt)
        cr, ci = n_re * inv, n_im * inv
        return (ar, ai, cr * b_re - ci * b_im, cr * b_im + ci * b_re), ()
    return rowwise(fn, (lr, li, ldt, b_re, b_im), ((1, F32), (1, F32), (SSM_GROUP, F32), (SSM_GROUP, F32)),
                   name="s5_prep", tm=512)


def s5_prep_bwd(lr, li, ldt, b_re, b_im, dar, dai, dbb_re, dbb_im):
    def fn(lr, li, ldt, b_re, b_im, dar, dai, dbb_re, dbb_im):
        dt, mag, cs, sn, ar, ai, inv, n_re, n_im = _s5_disc(lr, li, ldt)
        cr, ci = n_re * inv, n_im * inv
        db_re = cr * dbb_re + ci * dbb_im
        db_im = cr * dbb_im - ci * dbb_re
        dcr = jnp.sum(dbb_re * b_re + dbb_im * b_im, axis=-1, keepdims=True)
        dci = jnp.sum(dbb_im * b_re - dbb_re * b_im, axis=-1, keepdims=True)
        dar = dar + (dcr * lr - dci * li) * inv
        dai = dai + (dcr * li + dci * lr) * inv
        dinv = dcr * n_re + dci * n_im
        dlr = (dcr * (ar - 1.0) + dci * ai) * inv - 2.0 * lr * inv * inv * dinv
        dli = (dcr * ai - dci * (ar - 1.0)) * inv - 2.0 * li * inv * inv * dinv
        dmag = dar * cs + dai * sn
        dth = dai * ar - dar * ai
        dlr = dlr + dmag * mag * dt
        dli = dli + dth * dt
        ddt = dmag * mag * lr + dth * li
        return (dlr, dli, ddt * dt, db_re, db_im), ()
    return rowwise(fn, (lr, li, ldt, b_re, b_im, dar, dai, dbb_re, dbb_im),
                   ((1, F32), (1, F32), (1, F32), (SSM_GROUP, F32), (SSM_GROUP, F32)), name="s5_prep_bwd", tm=512)


def group_sum(x):
    def body(x_ref, o_ref):
        o_ref[...] = jnp.sum(x_ref[...], axis=1)
    return _pcall(body, name="s5_group_sum", out_shape=jax.ShapeDtypeStruct((N_GROUPS, 1), F32))(
        x.reshape(N_GROUPS, SSM_STATE, 1))


GROUPS_PER_TILE = LANES // SSM_GROUP
TILE_STATES = GROUPS_PER_TILE * SSM_STATE
N_UTILES = D_MODEL // LANES
TILES_PER_UTILE = TILE_STATES // LANES


SUBLANES = 8
SCAN_STRIP = 1024
N_STRIPS = N_STATES // SCAN_STRIP
_NT = (((1,), (1,)), ((), ()))
_TN = (((0,), (0,)), ((), ()))


def _scan_coefs(are, aim, shifted, reverse):
    ar = are[...]
    ai = -aim[...] if reverse else aim[...]
    powers = {1: (ar, ai)}
    for d in (2, 4):
        r, i = powers[d // 2]
        powers[d] = (r * r - i * i, 2.0 * r * i)
    rid = lax.broadcasted_iota(jnp.int32, (SUBLANES, N_STATES), 0)
    first = (rid == SUBLANES - 1) if reverse else (rid == 0)
    masks = [(1, first)] + [(d, (rid <= SUBLANES - 1 - d) if reverse else (rid >= d)) for d in (1, 2, 4)]
    for n, (d, keep) in enumerate(masks):
        for part in (0, 1):
            shifted[2 * n + part][...] = jnp.where(keep, jnp.broadcast_to(powers[d][part], (SUBLANES, N_STATES)), 0.0)


def _tile_scan(xr, xi, shifted, nbr_re, nbr_im, reverse):
    for n, d in enumerate((1, 1, 2, 4)):
        by = SUBLANES - d if reverse else d
        fr, fi = (nbr_re, nbr_im) if n == 0 else (xr, xi)
        sr, si = pltpu.roll(fr, by, 0), pltpu.roll(fi, by, 0)
        kr, ki = shifted[2 * n], shifted[2 * n + 1]
        xr, xi = xr + kr * sr - ki * si, xi + kr * si + ki * sr
    return xr, xi


def _tile_rows(t):
    return pl.ds(pl.multiple_of(t * SUBLANES, SUBLANES), SUBLANES)


def s5_fwd(u, bbd_re, bbd_im, cbd_re, cbd_imn, a_re, a_im, dskip, t_rows=256):
    seq = u.shape[0]
    t_rows = min(t_rows, seq)
    n_tiles = t_rows // SUBLANES

    def body(u_ref, bre, bim, cre, cimn, are, aim, d_ref, y_ref, hre_ref, him_ref, car_re, car_im, *shifted):
        @pl.when(pl.program_id(0) == 0)
        def _():
            car_re[...] = jnp.zeros_like(car_re)
            car_im[...] = jnp.zeros_like(car_im)
            _scan_coefs(are, aim, shifted, reverse=False)

        uf = u_ref[...]
        ub = uf.astype(BF16)
        for j in range(N_UTILES):
            uj = ub[:, LANES * j:LANES * (j + 1)]
            sl = slice(TILE_STATES * j, TILE_STATES * (j + 1))
            hre_ref[:, sl] = jnp.dot(uj, bre[j], preferred_element_type=F32)
            him_ref[:, sl] = jnp.dot(uj, bim[j], preferred_element_type=F32)
        for s in range(N_STRIPS):
            cols = pl.ds(s * SCAN_STRIP, SCAN_STRIP)
            coefs = [c[:, cols] for c in shifted]

            def step(t, before):
                rows = _tile_rows(t)
                hr, hi = _tile_scan(hre_ref[rows, cols], him_ref[rows, cols], coefs, before[0], before[1], False)
                hre_ref[rows, cols] = hr
                him_ref[rows, cols] = hi
                return hr, hi

            cr, ci = lax.fori_loop(0, n_tiles, step, (car_re[:, cols], car_im[:, cols]))
            car_re[:, cols] = cr
            car_im[:, cols] = ci
        dv = d_ref[...]
        for j in range(N_UTILES):
            st = slice(TILE_STATES * j, TILE_STATES * (j + 1))
            yj = (jnp.dot(hre_ref[:, st].astype(BF16), cre[j], preferred_element_type=F32)
                  + jnp.dot(him_ref[:, st].astype(BF16), cimn[j], preferred_element_type=F32))
            sl = slice(LANES * j, LANES * (j + 1))
            y_ref[:, sl] = yj + dv[:, sl] * uf[:, sl]

    full3 = lambda a: pl.BlockSpec(a.shape, lambda i: (0, 0, 0))
    full2 = lambda a: pl.BlockSpec(a.shape, lambda i: (0, 0))
    tile = pltpu.VMEM((SUBLANES, N_STATES), F32)
    return _pcall(
        body, name="s5_fwd", grid=(seq // t_rows,),
        in_specs=[pl.BlockSpec((t_rows, D_MODEL), lambda i: (i, 0)), full3(bbd_re), full3(bbd_im), full3(cbd_re),
                  full3(cbd_imn), full2(a_re), full2(a_im), full2(dskip)],
        out_specs=[pl.BlockSpec((t_rows, D_MODEL), lambda i: (i, 0)),
                   pl.BlockSpec((t_rows, N_STATES), lambda i: (i, 0)),
                   pl.BlockSpec((t_rows, N_STATES), lambda i: (i, 0))],
        out_shape=[jax.ShapeDtypeStruct((seq, D_MODEL), F32),
                   jax.ShapeDtypeStruct((seq, N_STATES), F32),
                   jax.ShapeDtypeStruct((seq, N_STATES), F32)],
        scratch_shapes=[tile] * 10,
        compiler_params=_params(("arbitrary",)),
    )(u, bbd_re, bbd_im, cbd_re, cbd_imn, a_re, a_im, dskip)


def s5_bwd(dy, u, dres, h_re, h_im, bbd_re, bbd_im, cbd_re, cbd_imn, a_re, a_im, dskip, t_rows=128):
    seq = u.shape[0]
    t_rows = min(t_rows, seq)
    n_chunks = seq // t_rows

    n_tiles = t_rows // SUBLANES

    def body(dy_ref, u_ref, dres_ref, hre_ref, him_ref, hpre_ref, hpim_ref, bre, bim, cre, cimn, are, aim, d_ref,
             dx_ref, dbre, dbim, dcre, dcimn, dar_ref, dai_ref, dd_ref, lre, lim, car_re, car_im, acc_re, acc_im,
             *shifted):
        i = pl.program_id(0)

        @pl.when(i == 0)
        def _():
            for r in (car_re, car_im, acc_re, acc_im, dbre, dbim, dcre, dcimn, dd_ref):
                r[...] = jnp.zeros_like(r)
            _scan_coefs(are, aim, shifted, reverse=True)

        dyf = dy_ref[...]
        dyb = dyf.astype(BF16)
        uf = u_ref[...]
        ub = uf.astype(BF16)
        for j in range(N_UTILES):
            dyj = dyb[:, LANES * j:LANES * (j + 1)]
            st = slice(TILE_STATES * j, TILE_STATES * (j + 1))
            lre[:, st] = lax.dot_general(dyj, cre[j], _NT, preferred_element_type=F32)
            lim[:, st] = lax.dot_general(dyj, cimn[j], _NT, preferred_element_type=F32)
        has_pred = (i < n_chunks - 1).astype(F32)
        last_row = lax.broadcasted_iota(jnp.int32, (SUBLANES, SCAN_STRIP), 0) == SUBLANES - 1
        for s in range(N_STRIPS):
            cols = pl.ds(s * SCAN_STRIP, SCAN_STRIP)
            coefs = [c[:, cols] for c in shifted]
            before_re, before_im = hpre_ref[:, cols] * has_pred, hpim_ref[:, cols] * has_pred

            def step(k, carry):
                after_re, after_im, dar, dai = carry
                t = n_tiles - 1 - k
                rows = _tile_rows(t)
                lr, li = _tile_scan(lre[rows, cols], lim[rows, cols], coefs, after_re, after_im, True)
                lre[rows, cols] = lr
                lim[rows, cols] = li
                prev = _tile_rows(jnp.maximum(t - 1, 0))
                pre_re = jnp.where(t == 0, before_re, hre_ref[prev, cols])
                pre_im = jnp.where(t == 0, before_im, him_ref[prev, cols])
                hpr = pltpu.roll(jnp.where(last_row, pre_re, hre_ref[rows, cols]), 1, 0)
                hpi = pltpu.roll(jnp.where(last_row, pre_im, him_ref[rows, cols]), 1, 0)
                return lr, li, dar + lr * hpr + li * hpi, dai + li * hpr - lr * hpi

            cr, ci, dar, dai = lax.fori_loop(0, n_tiles, step, (car_re[:, cols], car_im[:, cols],
                                                               acc_re[:, cols], acc_im[:, cols]))
            car_re[:, cols] = cr
            car_im[:, cols] = ci
            acc_re[:, cols] = dar
            acc_im[:, cols] = dai

        dv = d_ref[...]
        for j in range(N_UTILES):
            sl = slice(LANES * j, LANES * (j + 1))
            st = slice(TILE_STATES * j, TILE_STATES * (j + 1))
            lrj = lre[:, st].astype(BF16)
            lij = lim[:, st].astype(BF16)
            du = (lax.dot_general(lrj, bre[j], _NT, preferred_element_type=F32)
                  + lax.dot_general(lij, bim[j], _NT, preferred_element_type=F32))
            dx_ref[:, sl] = du + dv[:, sl] * dyf[:, sl] + DN_ALPHA * dres_ref[:, sl]
            uj = ub[:, sl]
            dbre[j] += lax.dot_general(uj, lrj, _TN, preferred_element_type=F32)
            dbim[j] += lax.dot_general(uj, lij, _TN, preferred_element_type=F32)
            dyj = dyb[:, sl]
            dcre[j] += lax.dot_general(hre_ref[:, st].astype(BF16), dyj, _TN, preferred_element_type=F32)
            dcimn[j] += lax.dot_general(him_ref[:, st].astype(BF16), dyj, _TN, preferred_element_type=F32)
        dd_ref[...] += jnp.sum(dyf * uf, axis=0, keepdims=True)

        @pl.when(i == n_chunks - 1)
        def _():
            dar_ref[...] = jnp.sum(acc_re[...], axis=0, keepdims=True)
            dai_ref[...] = jnp.sum(acc_im[...], axis=0, keepdims=True)

    rev = lambda i: (n_chunks - 1 - i, 0)
    prev_tile = lambda i: (jnp.maximum((n_chunks - 1 - i) * n_tiles - 1, 0), 0)
    full3 = lambda a: pl.BlockSpec(a.shape, lambda i: (0, 0, 0))
    full2 = lambda a: pl.BlockSpec(a.shape, lambda i: (0, 0))
    acc3 = lambda shape: pl.BlockSpec(shape, lambda i: (0, 0, 0))
    acc2 = lambda shape: pl.BlockSpec(shape, lambda i: (0, 0))
    tile = pltpu.VMEM((SUBLANES, N_STATES), F32)
    return _pcall(
        body, name="s5_bwd", grid=(n_chunks,),
        in_specs=[pl.BlockSpec((t_rows, D_MODEL), rev), pl.BlockSpec((t_rows, D_MODEL), rev),
                  pl.BlockSpec((t_rows, D_MODEL), rev),
                  pl.BlockSpec((t_rows, N_STATES), rev), pl.BlockSpec((t_rows, N_STATES), rev),
                  pl.BlockSpec((SUBLANES, N_STATES), prev_tile), pl.BlockSpec((SUBLANES, N_STATES), prev_tile),
                  full3(bbd_re), full3(bbd_im), full3(cbd_re), full3(cbd_imn), full2(a_re), full2(a_im), full2(dskip)],
        out_specs=[pl.BlockSpec((t_rows, D_MODEL), rev), acc3(bbd_re.shape), acc3(bbd_im.shape), acc3(cbd_re.shape),
                   acc3(cbd_imn.shape), acc2((1, N_STATES)), acc2((1, N_STATES)), acc2((1, D_MODEL))],
        out_shape=[jax.ShapeDtypeStruct((seq, D_MODEL), F32), jax.ShapeDtypeStruct(bbd_re.shape, F32),
                   jax.ShapeDtypeStruct(bbd_im.shape, F32), jax.ShapeDtypeStruct(cbd_re.shape, F32),
                   jax.ShapeDtypeStruct(cbd_imn.shape, F32), jax.ShapeDtypeStruct((1, N_STATES), F32),
                   jax.ShapeDtypeStruct((1, N_STATES), F32), jax.ShapeDtypeStruct((1, D_MODEL), F32)],
        scratch_shapes=[pltpu.VMEM((t_rows, N_STATES), F32), pltpu.VMEM((t_rows, N_STATES), F32)] + [tile] * 12,
        compiler_params=_params(("arbitrary",)),
    )(dy, u, dres, h_re, h_im, h_re, h_im, bbd_re, bbd_im, cbd_re, cbd_imn, a_re, a_im, dskip)


def _eye_groups():
    return jnp.eye(GROUPS_PER_TILE, dtype=F32)


def _blockdiag_in(bb):
    t = bb.transpose(0, 2, 1).reshape(N_UTILES, GROUPS_PER_TILE, SSM_GROUP, SSM_STATE)
    bd = jnp.einsum("jgcp,gh->jgchp", t, _eye_groups())
    return bd.reshape(N_UTILES, LANES, TILE_STATES)


def _blockdiag_in_t(d):
    t = jnp.einsum("jgchp,gh->jgcp", d.reshape(N_UTILES, GROUPS_PER_TILE, SSM_GROUP, GROUPS_PER_TILE, SSM_STATE),
                   _eye_groups())
    return t.reshape(N_GROUPS, SSM_GROUP, SSM_STATE).transpose(0, 2, 1)


def _blockdiag_out(c):
    t = c.transpose(0, 2, 1).reshape(N_UTILES, GROUPS_PER_TILE, SSM_STATE, SSM_GROUP)
    bd = jnp.einsum("jhpc,hg->jhpgc", t, _eye_groups())
    return bd.reshape(N_UTILES, TILE_STATES, LANES)


def _blockdiag_out_t(d):
    t = jnp.einsum("jhpgc,hg->jhpc", d.reshape(N_UTILES, GROUPS_PER_TILE, SSM_STATE, GROUPS_PER_TILE, SSM_GROUP),
                   _eye_groups())
    return t.reshape(N_GROUPS, SSM_STATE, SSM_GROUP).transpose(0, 2, 1)


ATT_TQ = 512
ATT_TK = 512
LOG2E = math.log2(math.e)
LN2 = math.log(2.0)
Q_PRESCALE = SM_SCALE * LOG2E


def _loop_in_pairs(n, step, carry, start=0):
    pairs = (n - start) // 2

    def two(t, c):
        return step(start + 2 * t + 1, step(start + 2 * t, c))

    carry = lax.fori_loop(0, pairs, two, carry)
    return lax.fori_loop(start + 2 * pairs, n, step, carry)


def _causal(s, off=0, transposed=False):
    r = lax.broadcasted_iota(jnp.int32, s.shape, 0)
    c = lax.broadcasted_iota(jnp.int32, s.shape, 1)
    keep = (r <= c + off) if transposed else (c <= r + off)
    return jnp.where(keep, s, NEG_INF)


def attn_fwd(q, k, v, tq=ATT_TQ, tk=ATT_TK):
    n_heads, seq, _ = q.shape
    tq, tk = min(tq, seq), min(tk, seq)

    def body(q_ref, k_ref, v_ref, o_ref, lse_ref):
        qi = pl.program_id(1)
        qv = q_ref[0]
        jd = (qi * tq) // tk

        def block(j, carry, diag):
            m, l, acc = carry
            rows = pl.ds(pl.multiple_of(j * tk, tk), tk)
            s = lax.dot_general(qv, k_ref[0, rows, :], _NT, preferred_element_type=F32)
            if diag:
                s = _causal(s, qi * tq - jd * tk)
            m_new = jnp.maximum(m, jnp.max(s, axis=-1, keepdims=True))
            p = jnp.exp2(s - m_new)
            corr = jnp.exp2(m - m_new)
            l = l * corr + jnp.sum(p, axis=-1, keepdims=True)
            acc = acc * corr + jnp.dot(p.astype(BF16), v_ref[rows, :], preferred_element_type=F32)
            return m_new, l, acc

        init = (jnp.full((tq, 1), NEG_INF, F32), jnp.zeros((tq, 1), F32), jnp.zeros((tq, V_HEAD), F32))
        carry = _loop_in_pairs(jd, lambda j, c: block(j, c, False), init)
        m, l, acc = block(jd, carry, True)
        o_ref[...] = acc / l
        lse_ref[0] = jnp.broadcast_to(m + jnp.log2(l), (tq, LANES))

    return _pcall(
        body, name="attn_fwd", grid=(n_heads, seq // tq),
        in_specs=[pl.BlockSpec((1, tq, QK_DIM), lambda h, i: (h, i, 0)),
                  pl.BlockSpec((1, seq, QK_DIM), lambda h, i: (h, 0, 0)),
                  pl.BlockSpec((seq, V_HEAD), lambda h, i: (0, h))],
        out_specs=[pl.BlockSpec((tq, V_HEAD), lambda h, i: (i, h)),
                   pl.BlockSpec((1, tq, LANES), lambda h, i: (h, i, 0))],
        out_shape=[jax.ShapeDtypeStruct((seq, n_heads * V_HEAD), F32),
                   jax.ShapeDtypeStruct((n_heads, seq, LANES), F32)],
        compiler_params=_params(("parallel", "parallel")),
    )(q, k, v)


def attn_bwd_dq(q, k, v, do, o, lse, tq=ATT_TQ, tk=ATT_TK):
    n_heads, seq, _ = q.shape
    tq, tk = min(tq, seq), min(tk, seq)

    def body(q_ref, k_ref, v_ref, do_ref, o_ref, lse_ref, dqn_ref, dqr_ref, delta_ref):
        qi = pl.program_id(1)
        qv = q_ref[0]
        dof = do_ref[...]
        dob = dof.astype(BF16)
        delta = jnp.sum(dof * o_ref[...], axis=-1, keepdims=True)
        lse = lse_ref[0][:, :1]
        jd = (qi * tq) // tk

        def block(j, dq, diag):
            rows = pl.ds(pl.multiple_of(j * tk, tk), tk)
            kv = k_ref[0, rows, :]
            s = lax.dot_general(qv, kv, _NT, preferred_element_type=F32)
            if diag:
                s = _causal(s, qi * tq - jd * tk)
            p = jnp.exp2(s - lse)
            dp = lax.dot_general(dob, v_ref[rows, :], _NT, preferred_element_type=F32)
            ds = p * (dp - delta)
            return dq + jnp.dot(ds.astype(BF16), kv, preferred_element_type=F32)

        dq = _loop_in_pairs(jd, lambda j, c: block(j, c, False), jnp.zeros((tq, QK_DIM), F32))
        dq = block(jd, dq, True) * SM_SCALE
        dqn_ref[...] = dq[:, :QK_NOPE]
        dqr_ref[0] = dq[:, QK_NOPE:]
        delta_ref[0] = jnp.broadcast_to(delta, (tq, LANES))

    return _pcall(
        body, name="attn_bwd_dq", grid=(n_heads, seq // tq),
        in_specs=[pl.BlockSpec((1, tq, QK_DIM), lambda h, i: (h, i, 0)),
                  pl.BlockSpec((1, seq, QK_DIM), lambda h, i: (h, 0, 0)),
                  pl.BlockSpec((seq, V_HEAD), lambda h, i: (0, h)),
                  pl.BlockSpec((tq, V_HEAD), lambda h, i: (i, h)),
                  pl.BlockSpec((tq, V_HEAD), lambda h, i: (i, h)),
                  pl.BlockSpec((1, tq, LANES), lambda h, i: (h, i, 0))],
        out_specs=[pl.BlockSpec((tq, QK_NOPE), lambda h, i: (i, h)),
                   pl.BlockSpec((1, tq, QK_ROPE), lambda h, i: (h, i, 0)),
                   pl.BlockSpec((1, tq, LANES), lambda h, i: (h, i, 0))],
        out_shape=[jax.ShapeDtypeStruct((seq, n_heads * QK_NOPE), F32),
                   jax.ShapeDtypeStruct((n_heads, seq, QK_ROPE), F32),
                   jax.ShapeDtypeStruct((n_heads, seq, LANES), F32)],
        compiler_params=_params(("parallel", "parallel")),
    )(q, k, v, do, o, lse)


def attn_bwd_dkv(q, k, v, do, lse_row, delta_row, tq=ATT_TK):
    n_heads, seq, _ = q.shape
    tq = min(tq, seq)
    n_blk = seq // tq

    def body(q_ref, k_ref, v_ref, do_ref, lse_ref, delta_ref, dkn_ref, dkr_ref, dv_ref):
        kj = pl.program_id(1)
        kv = k_ref[0]
        vv = v_ref[...]

        def block(i, carry, diag):
            dk, dv = carry
            rows = pl.ds(pl.multiple_of(i * tq, tq), tq)
            qv = q_ref[0, rows, :]
            st = lax.dot_general(kv, qv, _NT, preferred_element_type=F32)
            if diag:
                st = _causal(st, transposed=True)
            pt = jnp.exp2(st - lse_ref[0, pl.ds(i, 1), :])
            dob = do_ref[rows, :].astype(BF16)
            dv = dv + jnp.dot(pt.astype(BF16), dob, preferred_element_type=F32)
            dpt = lax.dot_general(vv, dob, _NT, preferred_element_type=F32)
            dst = pt * (dpt - delta_ref[0, pl.ds(i, 1), :])
            dk = dk + jnp.dot(dst.astype(BF16), qv, preferred_element_type=F32)
            return dk, dv

        carry = block(kj, (jnp.zeros((tq, QK_DIM), F32), jnp.zeros((tq, V_HEAD), F32)), True)
        dk, dv = _loop_in_pairs(n_blk, lambda i, c: block(i, c, False), carry, start=kj + 1)
        dk = dk * LN2
        dkn_ref[...] = dk[:, :QK_NOPE]
        dkr_ref[0] = dk[:, QK_NOPE:]
        dv_ref[...] = dv

    return _pcall(
        body, name="attn_bwd_dkv", grid=(n_heads, n_blk),
        in_specs=[pl.BlockSpec((1, seq, QK_DIM), lambda h, j: (h, 0, 0)),
                  pl.BlockSpec((1, tq, QK_DIM), lambda h, j: (h, j, 0)),
                  pl.BlockSpec((tq, V_HEAD), lambda h, j: (j, h)),
                  pl.BlockSpec((seq, V_HEAD), lambda h, j: (0, h)),
                  pl.BlockSpec((1, n_blk, tq), lambda h, j: (h, 0, 0)),
                  pl.BlockSpec((1, n_blk, tq), lambda h, j: (h, 0, 0))],
        out_specs=[pl.BlockSpec((tq, QK_NOPE), lambda h, j: (j, h)),
                   pl.BlockSpec((1, tq, QK_ROPE), lambda h, j: (h, j, 0)),
                   pl.BlockSpec((tq, V_HEAD), lambda h, j: (j, h))],
        out_shape=[jax.ShapeDtypeStruct((seq, n_heads * QK_NOPE), F32),
                   jax.ShapeDtypeStruct((n_heads, seq, QK_ROPE), F32),
                   jax.ShapeDtypeStruct((seq, n_heads * V_HEAD), F32)],
        compiler_params=_params(("parallel", "parallel")),
    )(q, k, v, do, lse_row, delta_row)


def head_sum(x, ts=512):
    n_heads, seq, w = x.shape
    ts = min(ts, seq)

    def body(x_ref, o_ref):
        o_ref[...] = jnp.sum(x_ref[...], axis=0)

    return _pcall(body, name="head_sum", grid=(seq // ts,),
                  in_specs=[pl.BlockSpec((n_heads, ts, w), lambda i: (0, i, 0))],
                  out_specs=pl.BlockSpec((ts, w), lambda i: (i, 0)),
                  out_shape=jax.ShapeDtypeStruct((seq, w), F32),
                  compiler_params=_params(("parallel",)))(x)


HEADS_PER_CHIP = N_HEADS // N_CHIPS
Q_CHIP = HEADS_PER_CHIP * QK_DIM
Q_CHIP_NOPE = HEADS_PER_CHIP * QK_NOPE


def _perm_q_cols(w):
    t = w.reshape(w.shape[0], HEADS_PER_CHIP, QK_DIM)
    return jnp.concatenate([t[:, :, :QK_NOPE].reshape(w.shape[0], -1),
                            t[:, :, QK_NOPE:QK_NOPE + HALF_ROPE].reshape(w.shape[0], -1),
                            t[:, :, QK_NOPE + HALF_ROPE:].reshape(w.shape[0], -1)], axis=1)


def _unperm_q_cols(w):
    r = w.shape[0]
    nope = w[:, :Q_CHIP_NOPE].reshape(r, HEADS_PER_CHIP, QK_NOPE)
    r1 = w[:, Q_CHIP_NOPE:Q_CHIP_NOPE + QK_ROPE].reshape(r, HEADS_PER_CHIP, HALF_ROPE)
    r2 = w[:, Q_CHIP_NOPE + QK_ROPE:].reshape(r, HEADS_PER_CHIP, HALF_ROPE)
    return jnp.concatenate([nope, r1, r2], axis=2).reshape(r, Q_CHIP)


def _pad_kva_cols(w):
    z = jnp.zeros((w.shape[0], HALF_ROPE), w.dtype)
    return jnp.concatenate([w[:, :KV_LORA], w[:, KV_LORA:KV_LORA + HALF_ROPE], z, w[:, KV_LORA + HALF_ROPE:], z], axis=1)


def _unpad_kva_cols(w):
    return jnp.concatenate([w[:, :KV_LORA], w[:, KV_LORA:KV_LORA + HALF_ROPE],
                            w[:, KV_LORA + QK_ROPE:KV_LORA + QK_ROPE + HALF_ROPE]], axis=1)


def _rope_tile(t, cs, sn):
    return t * cs + pltpu.roll(t, LANES // 2, 1) * sn


def _rope_tile_bwd(d, cs, sn):
    return d * cs + pltpu.roll(d * sn, LANES // 2, 1)


def _b_cols(tk, tn):
    return pl.BlockSpec((None, tk, tn), lambda i, j, k: (j, k, 0))


def _b_cols_t(tk, tn):
    return pl.BlockSpec((None, tn, tk), lambda i, j, k: (k, j, 0))


def _out_cols(shape):
    return shape, lambda tm, tn: pl.BlockSpec((None, tm, tn), lambda i, j, k: (j, i, 0))


def device_step(x, positions, target, w):
    seq = x.shape[0]

    inv_freq = ROPE_THETA ** (-jnp.arange(HALF_ROPE, dtype=F32) / HALF_ROPE)
    ang = positions.astype(F32)[:, None] * inv_freq
    cos, sin = jnp.cos(ang), jnp.sin(ang)
    zero = jnp.zeros_like(cos)
    cos_q, sin_q = jnp.concatenate([cos] * 4, 1), jnp.concatenate([-sin, -sin, sin, sin], 1)
    cos_k, sin_k = jnp.concatenate([cos, zero, cos, zero], 1), jnp.concatenate([-sin, zero, sin, zero], 1)
    w1, w2 = w["w_ff1"], w["w_ff2"]
    ff_tile = D_FF // N_CHIPS
    pack_shape = (N_CHIPS, G_PACK_ROWS, PACK_W)

    lr = w["ssm_lam_re"].reshape(N_STATES, 1)
    li = w["ssm_lam_im"].reshape(N_STATES, 1)
    ldt = jnp.repeat(w["ssm_log_dt"].reshape(N_GROUPS), SSM_STATE).reshape(N_STATES, 1)
    b_re = w["ssm_b_re"].reshape(N_STATES, SSM_GROUP)
    b_im = w["ssm_b_im"].reshape(N_STATES, SSM_GROUP)
    a_re, a_im, bb_re, bb_im = s5_prep(lr, li, ldt, b_re, b_im)
    a_re, a_im = a_re.reshape(1, N_STATES), a_im.reshape(1, N_STATES)
    bbd_re = _blockdiag_in(bb_re.reshape(N_GROUPS, SSM_STATE, SSM_GROUP)).astype(BF16)
    bbd_im = _blockdiag_in(bb_im.reshape(N_GROUPS, SSM_STATE, SSM_GROUP)).astype(BF16)
    cbd_re = _blockdiag_out(w["ssm_c_re"].reshape(N_GROUPS, SSM_GROUP, SSM_STATE)).astype(BF16)
    cbd_imn = _blockdiag_out(-w["ssm_c_im"].reshape(N_GROUPS, SSM_GROUP, SSM_STATE)).astype(BF16)
    dskip = w["ssm_d"].reshape(1, D_MODEL)
    ypre, h_re, h_im = s5_fwd(x, bbd_re, bbd_im, cbd_re, cbd_imn, a_re, a_im, dskip)
    (yg,) = rowwise(lambda y: ((_gelu(y),), ()), (ypre,), ((D_MODEL, BF16),), name="gelu")
    w_glu = w["ssm_w_glu"]
    glu_tile = w_glu.shape[2]
    vg = mm(yg, w_glu, n_dim=2 * D_MODEL, tiles=(None, glu_tile, None), b_view=_b_cols, name="glu_proj")

    def glu(v):
        return (v[:, :D_MODEL] * _sigmoid(v[:, D_MODEL:]),), ()
    (z,) = rowwise(glu, (vg,), ((D_MODEL, BF16),), name="glu")
    w_out = w["ssm_w_out"].reshape(D_MODEL, D_MODEL)
    mix0 = mm(z, w_out, name="ssm_out")

    def mlp_fwd(hb, layer):
        pre = mm(hb, w1, n_dim=D_FF, tiles=(None, ff_tile, None), name=f"ff1_{layer}",
                 b_view=lambda tk, tn: pl.BlockSpec((None, None, tk, tn), lambda i, j, k: (j, layer, k, 0)))
        f = mm(pre, w2, pro_a=_relu2, n_dim=D_MODEL, tiles=(None, D_MODEL, None), name=f"ff2_{layer}",
               b_view=lambda tk, tn: pl.BlockSpec((None, None, tk, tn),
                                                  lambda i, j, k: (k // (ff_tile // tk), layer, k % (ff_tile // tk), j)))
        return pre, f

    ln = lambda name, l: w[name][l].reshape(1, D_MODEL)
    h1, h1b = ln_fwd(x, mix0, ln("ln_mix_g", 0), ln("ln_mix_b", 0), "ln_mix_0")
    f1pre, f1 = mlp_fwd(h1b, 0)
    h2, h2b = ln_fwd(h1, f1, ln("ln_ffn_g", 0), ln("ln_ffn_b", 0), "ln_ffn_0")

    kv_w_a = w["kv_w_a"].reshape(D_MODEL, KVA_PAD)
    kv_w_b = w["kv_w_b"]
    q_w_a = w["q_w_a"].reshape(D_MODEL, Q_LORA)
    q_w_b = w["q_w_b"]
    w_o = w["attn_w_o"].reshape(D_MODEL, D_MODEL)
    kvb_tile = kv_w_b.shape[2]
    kvn_g = w["kv_norm_g"].reshape(1, KV_LORA)
    qn_g = w["q_norm_g"].reshape(1, Q_LORA)
    kva = mm(h2b, kv_w_a, name="kv_a")

    def kv_post(kva, g, cs, sn):
        return (_rms(kva[:, :KV_LORA], g), _rope_tile(kva[:, KV_LORA:], cs, sn)), ()
    ckv, krope = rowwise(kv_post, (kva, kvn_g, cos_k, sin_k), ((KV_LORA, BF16), (LANES, BF16)), name="kv_post")
    kvb = mm(ckv, kv_w_b, n_dim=N_CHIPS * kvb_tile, tiles=(None, kvb_tile, KV_LORA), b_view=_b_cols, name="kv_b",
             out_dtypes=(BF16,))
    cq_raw = mm(h2b, q_w_a, name="q_a")
    (cq,) = rowwise(lambda c, g: ((_rms(c, g),), ()), (cq_raw, qn_g), ((Q_LORA, BF16),), name="q_norm")
    qlin = mm(cq, q_w_b, n_dim=N_CHIPS * Q_CHIP, tiles=(None, Q_CHIP, Q_LORA), b_view=_b_cols, name="q_b")

    def on_rope_tiles(fn, scale=None):
        def apply(q, cs, sn):
            parts = []
            for k in range(N_CHIPS):
                parts.append(q[:, Q_CHIP * k:Q_CHIP * k + Q_CHIP_NOPE])
                parts.append(fn(q[:, Q_CHIP * k + Q_CHIP_NOPE:Q_CHIP * (k + 1)], cs, sn))
            out = jnp.concatenate(parts, axis=1)
            return (out if scale is None else out * scale,), ()
        return apply
    (qro,) = rowwise(on_rope_tiles(_rope_tile, Q_PRESCALE), (qlin, cos_q, sin_q), ((N_CHIPS * Q_CHIP, BF16),),
                     name="q_rope")
    qro3 = qro.reshape(seq, N_CHIPS, Q_CHIP)
    q_h = jnp.concatenate([qro3[:, :, :Q_CHIP_NOPE].reshape(seq, N_HEADS, QK_NOPE),
                           qro3[:, :, Q_CHIP_NOPE:Q_CHIP_NOPE + QK_ROPE].reshape(seq, N_HEADS, HALF_ROPE),
                           qro3[:, :, Q_CHIP_NOPE + QK_ROPE:].reshape(seq, N_HEADS, HALF_ROPE)], axis=2).transpose(1, 0, 2)
    kvb3 = kvb.reshape(seq, N_HEADS, QK_NOPE + V_HEAD)
    kr = jnp.concatenate([krope[:, :HALF_ROPE], krope[:, QK_ROPE:QK_ROPE + HALF_ROPE]], axis=1)
    k_h = jnp.concatenate([kvb3[:, :, :QK_NOPE], jnp.broadcast_to(kr[:, None, :], (seq, N_HEADS, QK_ROPE))],
                          axis=2).transpose(1, 0, 2)
    v2 = kvb3[:, :, QK_NOPE:].reshape(seq, N_HEADS * V_HEAD)
    o, lse = attn_fwd(q_h, k_h, v2)
    mix1 = mm(o, w_o, name="attn_out")
    h3, h3b = ln_fwd(h2, mix1, ln("ln_mix_g", 1), ln("ln_mix_b", 1), "ln_mix_1")
    f2pre, f2 = mlp_fwd(h3b, 1)
    h4, _ = ln_fwd(h3, f2, ln("ln_ffn_g", 1), ln("ln_ffn_b", 1), "ln_ffn_1")

    def loss_fn(y, t):
        e = y - t
        return (e * (1.0 / D_MODEL),), (jnp.broadcast_to(jnp.sum(e * e), (1, LANES)),)
    dh4, loss_acc = rowwise(loss_fn, (h4, target), ((D_MODEL, F32),), accs=(LANES,), name="loss")
    loss = loss_acc[0, 0] * (0.5 / D_MODEL)

    g = {}

    def into_rows(off, rows_per_chip):
        def view(tm, tn):
            nb = rows_per_chip // tm
            return pl.BlockSpec((None, tm, tn), lambda i, j, k: (i // nb, off // tm + i % nb, 0))
        return pack_shape, view

    def into_cols(off):
        return pack_shape, lambda tm, tn: pl.BlockSpec((None, tm, tn), lambda i, j, k: (j, off // tm + i, 0))

    def mlp_bwd(pack, dr, drb, hb, pre, layer):
        nb = lambda tk: ff_tile // tk
        dpre = mm(drb, w2, tb=True, epi=lambda r, p: (r * 2.0 * jnp.maximum(p, 0.0),), extras=(pre,),
                  out_dtypes=(BF16,), n_dim=D_FF, tiles=(None, ff_tile, None), name=f"ff2_dx_{layer}",
                  b_view=lambda tk, tn: pl.BlockSpec((None, None, tn, tk), lambda i, j, k: (j, layer, 0, k)))
        pack = mm(pre, drb, ta=True, pro_a=_relu2, name=f"ff2_dw_{layer}", tiles=(None, PACK_W, None), into=pack,
                  out_view=into_rows(DIRECT_OFF["w_ff2"] + layer * ff_tile, ff_tile))
        pack = mm(hb, dpre, ta=True, name=f"ff1_dw_{layer}", tiles=(None, PACK_W, None), into=pack,
                  out_view=into_cols(DIRECT_OFF["w_ff1"] + layer * D_MODEL))
        dh = mm(dpre, w1, tb=True, epi=lambda r, d: (r + DN_ALPHA * d,), extras=(dr,), n_dim=D_MODEL,
                tiles=(None, D_MODEL, None), name=f"ff1_dx_{layer}",
                b_view=lambda tk, tn: pl.BlockSpec((None, None, tn, tk), lambda i, j, k: (k // nb(tk), layer, 0, k % nb(tk))))
        return pack, dh

    dr4, dr4b, dg_f1, db_f1 = ln_bwd(h3, f2, ln("ln_ffn_g", 1), dh4, "ln_ffn_bwd_1")
    pack, dh3 = mlp_bwd(None, dr4, dr4b, h3b, f2pre, 1)
    dr3, dr3b, dg_m1, db_m1 = ln_bwd(h2, mix1, ln("ln_mix_g", 1), dh3, "ln_mix_bwd_1")
    shard_rows = D_MODEL // N_CHIPS
    pack = mm(o, dr3b, ta=True, name="attn_out_dw", tiles=(shard_rows, PACK_W, None), into=pack,
              out_view=into_rows(DIRECT_OFF["attn_w_o"], shard_rows))
    do = mm(dr3b, w_o, tb=True, name="attn_out_dx")
    dqn, dqr, delta = attn_bwd_dq(q_h, k_h, v2, do, o, lse)
    tb = min(ATT_TK, seq)
    lse_row = lse[:, :, 0].reshape(N_HEADS, seq // tb, tb)
    delta_row = delta[:, :, 0].reshape(N_HEADS, seq // tb, tb)
    dkn, dkr, dv = attn_bwd_dkv(q_h, k_h, v2, do, lse_row, delta_row)
    dqr_t = dqr.transpose(1, 0, 2)
    dq_cat = jnp.concatenate([dqn.reshape(seq, N_CHIPS, Q_CHIP_NOPE), dqr_t[:, :, :HALF_ROPE].reshape(seq, N_CHIPS, QK_ROPE),
                              dqr_t[:, :, HALF_ROPE:].reshape(seq, N_CHIPS, QK_ROPE)], 2).reshape(seq, N_CHIPS * Q_CHIP)
    (dqlin,) = rowwise(on_rope_tiles(_rope_tile_bwd), (dq_cat, cos_q, sin_q), ((N_CHIPS * Q_CHIP, BF16),), name="q_rope_bwd")
    g["q_w_b"] = mm(cq, dqlin, ta=True, name="q_b_dw", tiles=(Q_LORA, Q_CHIP, None), out_view=_out_cols(q_w_b.shape))
    dcq = mm(dqlin, q_w_b, tb=True, n_dim=Q_LORA, tiles=(None, Q_LORA, Q_CHIP), b_view=_b_cols_t, name="q_b_dx")

    def q_norm_bwd(c, gq, d):
        dx, dgq = _rms_bwd(c, gq, d)
        return (dx,), (dgq,)
    dcq_raw, dqn_g = rowwise(q_norm_bwd, (cq_raw, qn_g, dcq), ((Q_LORA, BF16),), accs=(Q_LORA,), name="q_norm_bwd")
    g["q_w_a"] = mm(h2b, dcq_raw, ta=True, name="q_a_dw")
    dkvb = jnp.concatenate([dkn.reshape(seq, N_HEADS, QK_NOPE), dv.reshape(seq, N_HEADS, V_HEAD)], 2).reshape(
        seq, N_HEADS * (QK_NOPE + V_HEAD)).astype(BF16)
    g["kv_w_b"] = mm(ckv, dkvb, ta=True, name="kv_b_dw", tiles=(KV_LORA, kvb_tile, None), out_view=_out_cols(kv_w_b.shape))
    dckv = mm(dkvb, kv_w_b, tb=True, n_dim=KV_LORA, tiles=(None, KV_LORA, kvb_tile), b_view=_b_cols_t, name="kv_b_dx")
    dkr_sum = head_sum(dkr)
    zpad = jnp.zeros((seq, HALF_ROPE), F32)
    dkr_tile = jnp.concatenate([dkr_sum[:, :HALF_ROPE], zpad, dkr_sum[:, HALF_ROPE:], zpad], 1)

    def kv_post_bwd(kva, gk, dc, dk, cs, sn):
        dx, dgk = _rms_bwd(kva[:, :KV_LORA], gk, dc)
        return (jnp.concatenate([dx, _rope_tile_bwd(dk, cs, sn)], axis=1),), (dgk,)
    dkva, dkvn_g = rowwise(kv_post_bwd, (kva, kvn_g, dckv, dkr_tile, cos_k, sin_k), ((KVA_PAD, BF16),),
                           accs=(KV_LORA,), name="kv_post_bwd")
    g["kv_w_a"] = mm(h2b, dkva, ta=True, name="kv_a_dw")
    dh2 = mm(dcq_raw, q_w_a, tb=True, epi=lambda r, d: (r + DN_ALPHA * d,), extras=(dr3,), name="q_a_dx")
    dh2 = mm(dkva, kv_w_a, tb=True, epi=lambda r, d: (r + d,), extras=(dh2,), name="kv_a_dx")

    dr2, dr2b, dg_f0, db_f0 = ln_bwd(h1, f1, ln("ln_ffn_g", 0), dh2, "ln_ffn_bwd_0")
    pack, dh1 = mlp_bwd(pack, dr2, dr2b, h1b, f1pre, 0)
    dr1, dr1b, dg_m0, db_m0 = ln_bwd(x, mix0, ln("ln_mix_g", 0), dh1, "ln_mix_bwd_0")
    pack = mm(z, dr1b, ta=True, name="ssm_out_dw", tiles=(shard_rows, PACK_W, None), into=pack,
              out_view=into_rows(DIRECT_OFF["ssm_w_out"], shard_rows))
    dz = mm(dr1b, w_out, tb=True, name="ssm_out_dx")

    def glu_bwd(v, dz):
        val, sg = v[:, :D_MODEL], _sigmoid(v[:, D_MODEL:])
        return (jnp.concatenate([dz * sg, dz * val * sg * (1.0 - sg)], axis=1),), ()
    (dvg,) = rowwise(glu_bwd, (vg, dz), ((2 * D_MODEL, BF16),), name="glu_bwd")
    g["ssm_w_glu"] = mm(yg, dvg, ta=True, name="glu_proj_dw", tiles=(None, glu_tile, None), out_view=_out_cols(w_glu.shape))
    dypre = mm(dvg, w_glu, tb=True, epi=lambda r, y: (r * _gelu_grad(y),), extras=(ypre,), n_dim=D_MODEL,
               tiles=(None, D_MODEL, glu_tile), b_view=_b_cols_t, name="glu_proj_dx")
    dx, dbbd_re, dbbd_im, dcbd_re, dcbd_imn, dar, dai, dd = s5_bwd(
        dypre, x, dr1, h_re, h_im, bbd_re, bbd_im, cbd_re, cbd_imn, a_re, a_im, dskip)
    dbb_re = _blockdiag_in_t(dbbd_re).reshape(N_STATES, SSM_GROUP)
    dbb_im = _blockdiag_in_t(dbbd_im).reshape(N_STATES, SSM_GROUP)
    dlr, dli, dldt, db_re, db_im = s5_prep_bwd(lr, li, ldt, b_re, b_im, dar.reshape(N_STATES, 1),
                                               dai.reshape(N_STATES, 1), dbb_re, dbb_im)
    g["ssm_lam_re"] = dlr.reshape(1, N_GROUPS, SSM_STATE)
    g["ssm_lam_im"] = dli.reshape(1, N_GROUPS, SSM_STATE)
    g["ssm_log_dt"] = group_sum(dldt).reshape(1, N_GROUPS)
    g["ssm_b_re"] = db_re.reshape(1, N_GROUPS, SSM_STATE, SSM_GROUP)
    g["ssm_b_im"] = db_im.reshape(1, N_GROUPS, SSM_STATE, SSM_GROUP)
    g["ssm_c_re"] = _blockdiag_out_t(dcbd_re).reshape(1, N_GROUPS, SSM_GROUP, SSM_STATE)
    g["ssm_c_im"] = -_blockdiag_out_t(dcbd_imn).reshape(1, N_GROUPS, SSM_GROUP, SSM_STATE)
    g["ssm_d"] = dd
    g["ln_mix_g"] = jnp.concatenate([dg_m0, dg_m1], 0)
    g["ln_mix_b"] = jnp.concatenate([db_m0, db_m1], 0)
    g["ln_ffn_g"] = jnp.concatenate([dg_f0, dg_f1], 0)
    g["ln_ffn_b"] = jnp.concatenate([db_f0, db_f1], 0)
    g["kv_norm_g"] = dkvn_g.reshape(KV_LORA)
    g["q_norm_g"] = dqn_g
    return loss, dx, pack, g


def _place():
    x, y, c = lax.axis_index("x"), lax.axis_index("y"), lax.axis_index("c")
    return x, y, c, [(1 - x, y), (x, 1 - y), (1 - x, 1 - y)]


def place(shard, me_idx, dtype, name):
    rows, cols = shard.shape
    tr = _tile(rows, (512, 256, 128))

    def body(m_ref, x_ref, o_ref):
        o_ref[...] = x_ref[...].astype(o_ref.dtype)

    return _pcall(
        body, name=name,
        grid_spec=pltpu.PrefetchScalarGridSpec(
            num_scalar_prefetch=1, grid=(rows // tr,),
            in_specs=[pl.BlockSpec((tr, cols), lambda i, m: (i, 0))],
            out_specs=pl.BlockSpec((None, tr, cols), lambda i, m: (m[0], i, 0))),
        out_shape=jax.ShapeDtypeStruct((N_CHIPS, rows, cols), dtype),
        compiler_params=_params(("parallel",)),
    )(me_idx, shard)


def gather_stacked(arrs, name):
    n = len(arrs)

    def body(*refs):
        outs, send_sems, recv_sems = refs[n:2 * n], refs[2 * n], refs[2 * n + 1]
        x, y, c, chips = _place()
        sibling = (x, y, 1 - c)
        me = 2 * x + y

        def copy(k, blk, to):
            return pltpu.make_async_remote_copy(src_ref=blk, dst_ref=blk, send_sem=send_sems.at[k],
                                                recv_sem=recv_sems.at[k], device_id=to, device_id_type=MESH)

        started = []
        for a, o in enumerate(outs):
            for j, (px, py) in enumerate(chips):
                cp = copy(6 * a + j, o.at[me, c], (px, py, c))
                cp.start()
                started.append(cp)
        for a, o in enumerate(outs):
            for j, (px, py) in enumerate(chips):
                blk = o.at[2 * px + py, c]
                copy(6 * a + j, blk, (px, py, c)).wait_recv()
                cp = copy(6 * a + 3 + j, blk, sibling)
                cp.start()
                started.append(cp)
        for a, o in enumerate(outs):
            for j, (px, py) in enumerate(chips):
                copy(6 * a + 3 + j, o.at[2 * px + py, 1 - c], sibling).wait_recv()
        for cp in started:
            cp.wait_send()

    return _pcall(body, name=name, in_specs=[_ANY] * n, out_specs=[_ANY] * n,
                  out_shape=[jax.ShapeDtypeStruct(a.shape, a.dtype) for a in arrs],
                  input_output_aliases={i: i for i in range(n)},
                  scratch_shapes=[pltpu.SemaphoreType.DMA((6 * n,)), pltpu.SemaphoreType.DMA((6 * n,))])(*arrs)


def put_rows(pack, rows, off):
    _, n, cols = rows.shape
    tr = math.gcd(math.gcd(off, n), 512)

    def body(r_ref, p_ref, o_ref):
        o_ref[...] = r_ref[...]

    return _pcall(body, name="grad_put_rows", grid=(N_CHIPS, n // tr),
                  in_specs=[pl.BlockSpec((None, tr, cols), lambda k, i: (k, i, 0)), _ANY],
                  out_specs=pl.BlockSpec((None, tr, cols), lambda k, i: (k, off // tr + i, 0)),
                  out_shape=jax.ShapeDtypeStruct(pack.shape, pack.dtype), input_output_aliases={1: 0},
                  compiler_params=_params(("parallel", "parallel")))(rows, pack)


def _my_cols(c, mine=True):
    start = (c if mine else 1 - c) * HALF_W
    return pl.ds(pl.multiple_of(start, HALF_W), HALF_W)


def swap_halves(gpack):
    n, rows, _ = gpack.shape

    def body(g_ref, got_ref, send_sem, recv_sem):
        x, y, c, _ = _place()
        cp = pltpu.make_async_remote_copy(src_ref=g_ref.at[:, :, _my_cols(c, mine=False)], dst_ref=got_ref,
                                          send_sem=send_sem, recv_sem=recv_sem, device_id=(x, y, 1 - c),
                                          device_id_type=MESH)
        cp.start()
        cp.wait()

    return _pcall(body, name="grad_swap_halves", in_specs=[_ANY], out_specs=_ANY,
                  out_shape=jax.ShapeDtypeStruct((n, rows, HALF_W), gpack.dtype),
                  scratch_shapes=[pltpu.SemaphoreType.DMA, pltpu.SemaphoreType.DMA])(gpack)


def add_halves(gpack, got, c_idx):
    n, rows, _ = gpack.shape
    blk = (None, G_BLOCK_ROWS, HALF_W)

    def body(c_ref, g_ref, r_ref, o_ref):
        o_ref[...] = (g_ref[...] + r_ref[...]).astype(o_ref.dtype)

    return _pcall(
        body, name="grad_add_halves",
        grid_spec=pltpu.PrefetchScalarGridSpec(
            num_scalar_prefetch=1, grid=(n, rows // G_BLOCK_ROWS),
            in_specs=[pl.BlockSpec(blk, lambda k, i, c: (k, i, c[0])), pl.BlockSpec(blk, lambda k, i, c: (k, i, 0))],
            out_specs=pl.BlockSpec(blk, lambda k, i, c: (k, i, 0))),
        out_shape=jax.ShapeDtypeStruct((n, rows, HALF_W), BF16),
        compiler_params=_params(("parallel", "parallel")),
    )(c_idx, gpack, got)


def send_to_owners(part):
    _, rh, cols = part.shape

    def body(p_ref, got_ref, send_sems, recv_sems):
        x, y, c, chips = _place()
        cps = [pltpu.make_async_remote_copy(src_ref=p_ref.at[2 * px + py], dst_ref=got_ref.at[j],
                                            send_sem=send_sems.at[j], recv_sem=recv_sems.at[j],
                                            device_id=(px, py, c), device_id_type=MESH)
               for j, (px, py) in enumerate(chips)]
        for cp in cps:
            cp.start()
        for cp in cps:
            cp.wait()

    return _pcall(body, name="grad_send_to_owners", in_specs=[_ANY], out_specs=_ANY,
                  out_shape=jax.ShapeDtypeStruct((3, rh, cols), part.dtype),
                  scratch_shapes=[pltpu.SemaphoreType.DMA((3,)), pltpu.SemaphoreType.DMA((3,))])(part)


def sum_owner(part, got, idx):
    _, rows, _ = part.shape
    tr = G_BLOCK_ROWS

    def body(m_ref, p_ref, g_ref, o_ref):
        up = lambda v: v.astype(F32)
        o_ref[...] = ((up(p_ref[...]) + up(g_ref[0])) + up(g_ref[1])) + up(g_ref[2])

    return _pcall(
        body, name="grad_sum_owner",
        grid_spec=pltpu.PrefetchScalarGridSpec(
            num_scalar_prefetch=1, grid=(rows // tr,),
            in_specs=[pl.BlockSpec((None, tr, HALF_W), lambda i, m: (m[0], i, 0)),
                      pl.BlockSpec((3, tr, HALF_W), lambda i, m: (0, i, 0))],
            out_specs=pl.BlockSpec((tr, HALF_W), lambda i, m: (i, m[1]))),
        out_shape=jax.ShapeDtypeStruct((rows, PACK_W), F32),
        compiler_params=_params(("parallel",)),
    )(idx, part, got)


def join_halves(red):
    def body(in_ref, out_ref, send_sem, recv_sem):
        x, y, c, _ = _place()
        sibling = (x, y, 1 - c)
        mine = out_ref.at[:, _my_cols(c)]
        cp = pltpu.make_async_remote_copy(src_ref=mine, dst_ref=mine, send_sem=send_sem, recv_sem=recv_sem,
                                          device_id=sibling, device_id_type=MESH)
        cp.start()
        cp.wait_send()
        other = out_ref.at[:, _my_cols(c, mine=False)]
        pltpu.make_async_remote_copy(src_ref=other, dst_ref=other, send_sem=send_sem, recv_sem=recv_sem,
                                     device_id=sibling, device_id_type=MESH).wait_recv()

    return _pcall(body, name="grad_join_halves", in_specs=[_ANY], out_specs=_ANY,
                  out_shape=jax.ShapeDtypeStruct(red.shape, red.dtype), input_output_aliases={0: 0},
                  scratch_shapes=[pltpu.SemaphoreType.DMA, pltpu.SemaphoreType.DMA])(red)


def adamw(gsrc, g_off, wt, m, v, name):
    n, cols = wt.shape
    tr = math.gcd(math.gcd(g_off, n), 256) if g_off else math.gcd(n, 256)
    off_blk = g_off // tr
    c1 = 1.0 / (1.0 - ADAM_B1 ** ADAM_STEP)
    c2 = 1.0 / (1.0 - ADAM_B2 ** ADAM_STEP)

    def body(g_ref, w_ref, m_ref, v_ref, go_ref, d_ref, mo_ref, vo_ref):
        gv = g_ref[...]
        mn = ADAM_B1 * m_ref[...] + (1.0 - ADAM_B1) * gv
        vn = ADAM_B2 * v_ref[...] + (1.0 - ADAM_B2) * gv * gv
        go_ref[...] = gv
        mo_ref[...] = mn
        vo_ref[...] = vn
        d_ref[...] = -ADAM_LR * ((mn * c1) / (jnp.sqrt(vn * c2) + ADAM_EPS) + ADAM_WD * w_ref[...])

    blk = pl.BlockSpec((tr, cols), lambda i: (i, 0))
    return _pcall(body, name=name, grid=(n // tr,),
                  in_specs=[pl.BlockSpec((tr, cols), lambda i: (off_blk + i, 0)), blk, blk, blk],
                  out_specs=[blk] * 4, out_shape=[jax.ShapeDtypeStruct((n, cols), F32)] * 4,
                  compiler_params=_params(("parallel",)))(gsrc, wt, m, v)


def _rows8(a):
    return -(-a.size // (8 * PACK_W)) * 8


def _as_rows(a, rows=None):
    flat = a.reshape(-1)
    n = _rows8(a) if rows is None else rows
    return jnp.pad(flat, (0, n * PACK_W - flat.shape[0])).reshape(n, PACK_W)


def local_shards_2d(wl):
    return {"w_ff1": wl["w_ff1"].reshape(2 * D_MODEL, D_FF // N_CHIPS), "w_ff2": wl["w_ff2"].reshape(2 * D_FF // N_CHIPS, D_MODEL),
            "ssm_w_glu": wl["ssm_w_glu"], "ssm_w_out": wl["ssm_w_out"], "kv_w_a": _pad_kva_cols(wl["kv_w_a"]),
            "kv_w_b": wl["kv_w_b"], "q_w_a": wl["q_w_a"], "q_w_b": _perm_q_cols(wl["q_w_b"]),
            "attn_w_o": wl["attn_w_o"], "ssm_d": wl["ssm_d"].reshape(2, -1)}


def misc_grad_shard(name, g, k):
    if name == "ssm_d":
        w = D_MODEL // N_CHIPS
        return g[:, w * k:w * (k + 1)]
    if name in ("ssm_w_glu", "kv_w_b"):
        return g[k]
    if name == "q_w_b":
        return _unperm_q_cols(g[k])
    rows = D_MODEL // N_CHIPS
    shard = g[rows * k:rows * (k + 1)]
    return _unpad_kva_cols(shard) if name == "kv_w_a" else shard


def kernel(x, positions, ln_mix_g, ln_mix_b, ln_ffn_g, ln_ffn_b, w_ff1, w_ff2, ssm_lam_re, ssm_lam_im, ssm_log_dt, ssm_b_re, ssm_b_im, ssm_c_re, ssm_c_im, ssm_d, ssm_w_glu, ssm_w_out, kv_w_a, kv_norm_g, kv_w_b, q_w_a, q_norm_g, q_w_b, attn_w_o, loss_target, m_ln_mix_g, m_ln_mix_b, m_ln_ffn_g, m_ln_ffn_b, m_w_ff1, m_w_ff2, m_ssm_lam_re, m_ssm_lam_im, m_ssm_log_dt, m_ssm_b_re, m_ssm_b_im, m_ssm_c_re, m_ssm_c_im, m_ssm_d, m_ssm_w_glu, m_ssm_w_out, m_kv_w_a, m_kv_norm_g, m_kv_w_b, m_q_w_a, m_q_norm_g, m_q_w_b, m_attn_w_o, v_ln_mix_g, v_ln_mix_b, v_ln_ffn_g, v_ln_ffn_b, v_w_ff1, v_w_ff2, v_ssm_lam_re, v_ssm_lam_im, v_ssm_log_dt, v_ssm_b_re, v_ssm_b_im, v_ssm_c_re, v_ssm_c_im, v_ssm_d, v_ssm_w_glu, v_ssm_w_out, v_kv_w_a, v_kv_norm_g, v_kv_w_b, v_q_w_a, v_q_norm_g, v_q_w_b, v_attn_w_o):
    env = dict(locals())
    wl = {n: env[n] for n in WEIGHTS}
    ml = {n: env["m_" + n] for n in WEIGHTS}
    vl = {n: env["v_" + n] for n in WEIGHTS}
    for n in ("ssm_w_glu", "ssm_w_out", "q_w_a", "q_w_b", "attn_w_o"):
        wl[n], ml[n], vl[n] = wl[n][0], ml[n][0], vl[n][0]

    c_idx = lax.axis_index("c").astype(jnp.int32).reshape(1)
    me_idx = (2 * lax.axis_index("x") + lax.axis_index("y")).astype(jnp.int32).reshape(1)

    local = local_shards_2d(wl)
    placed = [place(local[n], me_idx, F32 if n == "ssm_d" else BF16, "place_" + n) for n in SHARDED]
    halves = [p.reshape(N_CHIPS, 2, p.shape[1] // 2, p.shape[2]) for p in placed]
    gathered = gather_stacked(halves, "weight_all_gather")
    full = {n: a.reshape(p.shape) for n, a, p in zip(SHARDED, gathered, placed)}
    for n in ("w_ff1", "w_ff2"):
        full[n] = full[n].reshape(N_CHIPS, 2, D_MODEL, D_MODEL)
    full["ssm_d"] = full["ssm_d"].reshape(1, D_MODEL)
    for n in REPLICATED:
        full[n] = wl[n]

    loss_part, dx, gpack, g = device_step(x[0], positions[0], loss_target[0], full)
    loss = lax.psum(loss_part, ("x", "y", "c"))

    small = jnp.concatenate([_as_rows(g[n]) for n in REPLICATED], axis=0)
    small = jnp.pad(small, ((0, SMALL_ROWS - small.shape[0]), (0, 0)))
    blocks = []
    for k in range(N_CHIPS):
        rows = [small[SMALL_Q_ROWS * k:SMALL_Q_ROWS * (k + 1)]]
        rows += [_as_rows(misc_grad_shard(n, g[n], k), MISC_SHARD_ROWS[n]) for n in MISC_SHARDED]
        blk = jnp.concatenate(rows, axis=0)
        blocks.append(jnp.pad(blk, ((0, MISC_ROWS - blk.shape[0]), (0, 0))))
    gpack = put_rows(gpack, jnp.stack(blocks), MISC_OFF)
    chip_part = add_halves(gpack, swap_halves(gpack), c_idx)
    reduced = join_halves(sum_owner(chip_part, send_to_owners(chip_part), jnp.concatenate([me_idx, c_idx])))
    quarter = reduced[MISC_OFF:MISC_OFF + SMALL_Q_ROWS]
    small_tot = gather_stacked([place(quarter, me_idx, F32, "place_small_grads").reshape(
        N_CHIPS, 2, SMALL_Q_ROWS // 2, PACK_W)], "small_grad_all_gather")[0].reshape(SMALL_ROWS, PACK_W)

    out_g, out_d, out_m, out_v = {}, {}, {}, {}
    for n in DIRECT_OFF:
        res = adamw(reduced, DIRECT_OFF[n], wl[n].reshape(-1, PACK_W), ml[n].reshape(-1, PACK_W),
                    vl[n].reshape(-1, PACK_W), "adamw_" + n)
        out_g[n], out_d[n], out_m[n], out_v[n] = [a.reshape(env[n].shape) for a in res]
    pack3 = lambda d: jnp.concatenate([_as_rows(d[n], MISC_SHARD_ROWS[n]) for n in MISC_SHARDED], axis=0)
    res = adamw(reduced, MISC_OFF + SMALL_Q_ROWS, pack3(wl), pack3(ml), pack3(vl), "adamw_row_packed")
    for n in MISC_SHARDED:
        cnt = math.prod(env[n].shape)
        r0 = MISC_SHARD_OFF[n] - SMALL_Q_ROWS
        out_g[n], out_d[n], out_m[n], out_v[n] = [
            a[r0:r0 + MISC_SHARD_ROWS[n]].reshape(-1)[:cnt].reshape(env[n].shape) for a in res]
    ws = jnp.concatenate([_as_rows(wl[n]) for n in REPLICATED], axis=0)
    ms = jnp.concatenate([_as_rows(ml[n]) for n in REPLICATED], axis=0)
    vs = jnp.concatenate([_as_rows(vl[n]) for n in REPLICATED], axis=0)
    pad = ((0, SMALL_ROWS - ws.shape[0]), (0, 0))
    res = adamw(small_tot, 0, jnp.pad(ws, pad), jnp.pad(ms, pad), jnp.pad(vs, pad), "adamw_replicated")
    row = 0
    for n in REPLICATED:
        cnt = math.prod(env[n].shape)
        nrows = _rows8(env[n])
        out_g[n], out_d[n], out_m[n], out_v[n] = [a[row:row + nrows].reshape(-1)[:cnt].reshape(env[n].shape) for a in res]
        row += nrows

    return (loss, dx[None], *[out_g[n] for n in WEIGHTS], *[out_d[n] for n in WEIGHTS],
            *[out_m[n] for n in WEIGHTS], *[out_v[n] for n in WEIGHTS])
```

```python
import functools
import math

import jax
import jax.numpy as jnp
from jax import lax
from jax.experimental import pallas as pl
from jax.experimental.pallas import tpu as pltpu

F32 = jnp.float32
BF16 = jnp.bfloat16
MESH = pl.DeviceIdType.MESH

D_MODEL = 1024
DEPTH = 2
SSM_GROUP = 16
N_GROUPS = D_MODEL // SSM_GROUP
SSM_STATE = 64
N_STATES = N_GROUPS * SSM_STATE
N_HEADS = 8
QK_NOPE = 128
QK_ROPE = 64
HALF_ROPE = QK_ROPE // 2
V_HEAD = 128
QK_DIM = QK_NOPE + QK_ROPE
Q_LORA = 384
KV_LORA = 256
ROPE_THETA = 10000.0
SM_SCALE = QK_DIM ** -0.5
NEG_INF = -1e30
D_FF = 4 * D_MODEL
DN_ALPHA = (2 * DEPTH) ** 0.25
LN_EPS = 1e-5
RMS_EPS = 1e-6
ADAM_LR = 0.001
ADAM_B1 = 0.9
ADAM_B2 = 0.999
ADAM_EPS = 1e-08
ADAM_WD = 0.01
ADAM_STEP = 10

N_CHIPS = 4
LANES = 128
VMEM_LIMIT = 56 * 1024 * 1024
PACK_W = 1024
KVA_PAD = 384
HALF_W = PACK_W // 2

SHARDED = ("w_ff1", "w_ff2", "ssm_w_glu", "ssm_w_out", "kv_w_a", "kv_w_b", "q_w_a", "q_w_b", "attn_w_o", "ssm_d")
DIRECT_OFF = {"w_ff1": 0, "w_ff2": 2048, "ssm_w_out": 4096, "attn_w_o": 4352}
DIRECT_ROWS = {"w_ff1": 2048, "w_ff2": 2048, "ssm_w_out": 256, "attn_w_o": 256}
MISC_OFF = 4608
SMALL_Q_ROWS = 96
SMALL_ROWS = N_CHIPS * SMALL_Q_ROWS
MISC_SHARDED = ("ssm_d", "ssm_w_glu", "kv_w_b", "kv_w_a", "q_w_a", "q_w_b")
MISC_SHARD_ROWS = {"ssm_d": 16, "ssm_w_glu": 512, "kv_w_b": 128, "kv_w_a": 80, "q_w_a": 96, "q_w_b": 144}
MISC_SHARD_OFF = {}
_o = SMALL_Q_ROWS
for _n in MISC_SHARDED:
    MISC_SHARD_OFF[_n] = _o
    _o += MISC_SHARD_ROWS[_n]
MISC_USED = _o
G_PACK_ROWS = 5760
MISC_ROWS = G_PACK_ROWS - MISC_OFF
G_BLOCK_ROWS = 960
REPLICATED = ("ln_mix_g", "ln_mix_b", "ln_ffn_g", "ln_ffn_b", "ssm_lam_re", "ssm_lam_im", "ssm_log_dt",
              "ssm_b_re", "ssm_b_im", "ssm_c_re", "ssm_c_im", "kv_norm_g", "q_norm_g")
WEIGHTS = ("ln_mix_g", "ln_mix_b", "ln_ffn_g", "ln_ffn_b", "w_ff1", "w_ff2", "ssm_lam_re", "ssm_lam_im",
           "ssm_log_dt", "ssm_b_re", "ssm_b_im", "ssm_c_re", "ssm_c_im", "ssm_d", "ssm_w_glu", "ssm_w_out",
           "kv_w_a", "kv_norm_g", "kv_w_b", "q_w_a", "q_norm_g", "q_w_b", "attn_w_o")


def _pcall(body, **kw):
    return pl.pallas_call(body, **kw)


def _params(sem=None):
    return pltpu.CompilerParams(dimension_semantics=sem, vmem_limit_bytes=VMEM_LIMIT)


_ANY = pl.BlockSpec(memory_space=pl.ANY)


def _tile(dim, prefs):
    for p in prefs:
        if dim % p == 0:
            return p
    return dim


def mm(a, b, *, name, ta=False, tb=False, pro_a=None, epi=None, extras=(), out_dtypes=(F32,), n_dim=None,
       tiles=(None, None, None), b_view=None, out_view=None, into=None):
    if ta:
        k_dim, m_dim = a.shape
    else:
        m_dim, k_dim = a.shape
    if n_dim is None:
        n_dim = b.shape[0] if tb else b.shape[1]
    tm = tiles[0] or _tile(m_dim, (1024, 512, 256, 128))
    tn = tiles[1] or (n_dim if n_dim <= 1024 else _tile(n_dim, (1024, 512, 256, 128)))
    tk = tiles[2] or (k_dim if k_dim <= 1024 else _tile(k_dim, (1024, 512, 256, 128)))
    assert m_dim % tm == 0 and n_dim % tn == 0 and k_dim % tk == 0, (name, m_dim, n_dim, k_dim, tm, tn, tk)
    nk = k_dim // tk
    n_ex, n_out = len(extras), len(out_dtypes)
    n_into = 0 if into is None else 1
    dims = (((0 if ta else 1,), (1 if tb else 0,)), ((), ()))

    def body(a_ref, b_ref, *rest):
        ex_refs, out_refs = rest[:n_ex], rest[n_ex + n_into:n_ex + n_into + n_out]

        def partial():
            av = a_ref[...]
            if pro_a is not None:
                av = pro_a(av)
            return lax.dot_general(av.astype(BF16), b_ref[...].astype(BF16), dims, preferred_element_type=F32)

        def finish(r):
            res = epi(r, *[e[...] for e in ex_refs]) if epi is not None else (r,)
            for o_ref, v in zip(out_refs, res):
                o_ref[...] = v.astype(o_ref.dtype)

        if nk == 1:
            finish(partial())
            return
        acc = rest[-1]
        k = pl.program_id(2)

        @pl.when(k == 0)
        def _():
            acc[...] = partial()

        @pl.when(k > 0)
        def _():
            acc[...] += partial()

        @pl.when(k == nk - 1)
        def _():
            finish(acc[...])

    a_spec = pl.BlockSpec((tk, tm), lambda i, j, k: (k, i)) if ta else pl.BlockSpec((tm, tk), lambda i, j, k: (i, k))
    if b_view is not None:
        b_spec = b_view(tk, tn)
    else:
        b_spec = pl.BlockSpec((tn, tk), lambda i, j, k: (j, k)) if tb else pl.BlockSpec((tk, tn), lambda i, j, k: (k, j))
    o_spec = pl.BlockSpec((tm, tn), lambda i, j, k: (i, j))
    if out_view is None:
        out_specs = [o_spec] * n_out
        out_shape = [jax.ShapeDtypeStruct((m_dim, n_dim), dt) for dt in out_dtypes]
    else:
        assert n_out == 1
        out_specs = [out_view[1](tm, tn)]
        out_shape = [jax.ShapeDtypeStruct(out_view[0], out_dtypes[0])]
    outs = _pcall(
        body, name=name, grid=(m_dim // tm, n_dim // tn, nk),
        in_specs=[a_spec, b_spec] + [o_spec] * n_ex + [_ANY] * n_into,
        out_specs=out_specs, out_shape=out_shape,
        input_output_aliases={2 + n_ex: 0} if n_into else {},
        scratch_shapes=[pltpu.VMEM((tm, tn), F32)] if nk > 1 else [],
        compiler_params=_params(("parallel", "parallel", "arbitrary")),
    )(a, b, *extras, *([into] if n_into else []))
    return outs[0] if n_out == 1 else outs


def rowwise(fn, ins, outs, *, name, accs=(), tm=256):
    rows = ins[0].shape[0]
    tm = min(tm, rows)
    n_in, n_out, n_acc = len(ins), len(outs), len(accs)

    def body(*refs):
        in_refs, out_refs, acc_refs = refs[:n_in], refs[n_in:n_in + n_out], refs[n_in + n_out:]
        res, sums = fn(*[r[...] for r in in_refs])
        for o_ref, v in zip(out_refs, res):
            o_ref[...] = v.astype(o_ref.dtype)
        if n_acc:
            @pl.when(pl.program_id(0) == 0)
            def _():
                for a_ref in acc_refs:
                    a_ref[...] = jnp.zeros_like(a_ref)

            for a_ref, s in zip(acc_refs, sums):
                a_ref[...] += s

    def spec(arr):
        if arr.shape[0] == rows:
            return pl.BlockSpec((tm, arr.shape[1]), lambda i: (i, 0))
        return pl.BlockSpec(arr.shape, lambda i: (0, 0))

    res = _pcall(
        body, name=name, grid=(rows // tm,),
        in_specs=[spec(a) for a in ins],
        out_specs=[pl.BlockSpec((tm, w), lambda i: (i, 0)) for w, _ in outs]
        + [pl.BlockSpec((1, w), lambda i: (0, 0)) for w in accs],
        out_shape=[jax.ShapeDtypeStruct((rows, w), dt) for w, dt in outs]
        + [jax.ShapeDtypeStruct((1, w), F32) for w in accs],
        compiler_params=_params(("arbitrary",) if n_acc else ("parallel",)),
    )(*ins)
    return res


def _relu2(v):
    r = jnp.maximum(v, 0.0)
    return r * r


def _gelu(x):
    c = math.sqrt(2.0 / math.pi)
    return 0.5 * x * (1.0 + jnp.tanh(c * (x + 0.044715 * x * x * x)))


def _gelu_grad(x):
    c = math.sqrt(2.0 / math.pi)
    t = jnp.tanh(c * (x + 0.044715 * x * x * x))
    return 0.5 * (1.0 + t) + 0.5 * x * (1.0 - t * t) * c * (1.0 + 3 * 0.044715 * x * x)


def _sigmoid(x):
    return 1.0 / (1.0 + jnp.exp(-x))


def ln_fwd(h, mix, g, b, name):
    def fn(h, mix, g, b):
        r = DN_ALPHA * h + mix
        mu = jnp.mean(r, axis=-1, keepdims=True)
        xc = r - mu
        var = jnp.mean(xc * xc, axis=-1, keepdims=True)
        y = xc * lax.rsqrt(var + LN_EPS) * g + b
        return (y, y), ()
    return rowwise(fn, (h, mix, g, b), ((D_MODEL, F32), (D_MODEL, BF16)), name=name)


def ln_bwd(h, mix, g, dy, name):
    def fn(h, mix, g, dy):
        r = DN_ALPHA * h + mix
        mu = jnp.mean(r, axis=-1, keepdims=True)
        xc = r - mu
        var = jnp.mean(xc * xc, axis=-1, keepdims=True)
        rstd = lax.rsqrt(var + LN_EPS)
        xhat = xc * rstd
        dxh = dy * g
        m1 = jnp.mean(dxh, axis=-1, keepdims=True)
        m2 = jnp.mean(dxh * xhat, axis=-1, keepdims=True)
        dr = rstd * (dxh - m1 - xhat * m2)
        return (dr, dr), (jnp.sum(dy * xhat, axis=0, keepdims=True), jnp.sum(dy, axis=0, keepdims=True))
    return rowwise(fn, (h, mix, g, dy), ((D_MODEL, F32), (D_MODEL, BF16)), accs=(D_MODEL, D_MODEL), name=name)


def _rms(x, g):
    r = lax.rsqrt(jnp.mean(x * x, axis=-1, keepdims=True) + RMS_EPS)
    return x * r * g


def _rms_bwd(x, g, dy):
    r = lax.rsqrt(jnp.mean(x * x, axis=-1, keepdims=True) + RMS_EPS)
    xn = x * r
    dyg = dy * g
    dx = r * (dyg - xn * jnp.mean(dyg * xn, axis=-1, keepdims=True))
    return dx, jnp.sum(dy * xn, axis=0, keepdims=True)


def _s5_disc(lr, li, ldt):
    dt = jnp.exp(ldt)
    mag = jnp.exp(lr * dt)
    cs, sn = jnp.cos(li * dt), jnp.sin(li * dt)
    ar, ai = mag * cs, mag * sn
    inv = 1.0 / (lr * lr + li * li)
    n_re = (ar - 1.0) * lr + ai * li
    n_im = ai * lr - (ar - 1.0) * li
    return dt, mag, cs, sn, ar, ai, inv, n_re, n_im


def s5_prep(lr, li, ldt, b_re, b_im):
    def fn(lr, li, ldt, b_re, b_im):
        _, _, _, _, ar, ai, inv, n_re, n_im = _s5_disc(lr, li, ldt)
        cr, ci = n_re * inv, n_im * inv
        return (ar, ai, cr * b_re - ci * b_im, cr * b_im + ci * b_re), ()
    return rowwise(fn, (lr, li, ldt, b_re, b_im), ((1, F32), (1, F32), (SSM_GROUP, F32), (SSM_GROUP, F32)),
                   name="s5_prep", tm=512)


def s5_prep_bwd(lr, li, ldt, b_re, b_im, dar, dai, dbb_re, dbb_im):
    def fn(lr, li, ldt, b_re, b_im, dar, dai, dbb_re, dbb_im):
        dt, mag, cs, sn, ar, ai, inv, n_re, n_im = _s5_disc(lr, li, ldt)
        cr, ci = n_re * inv, n_im * inv
        db_re = cr * dbb_re + ci * dbb_im
        db_im = cr * dbb_im - ci * dbb_re
        dcr = jnp.sum(dbb_re * b_re + dbb_im * b_im, axis=-1, keepdims=True)
        dci = jnp.sum(dbb_im * b_re - dbb_re * b_im, axis=-1, keepdims=True)
        dar = dar + (dcr * lr - dci * li) * inv
        dai = dai + (dcr * li + dci * lr) * inv
        dinv = dcr * n_re + dci * n_im
        dlr = (dcr * (ar - 1.0) + dci * ai) * inv - 2.0 * lr * inv * inv * dinv
        dli = (dcr * ai - dci * (ar - 1.0)) * inv - 2.0 * li * inv * inv * dinv
        dmag = dar * cs + dai * sn
        dth = dai * ar - dar * ai
        dlr = dlr + dmag * mag * dt
        dli = dli + dth * dt
        ddt = dmag * mag * lr + dth * li
        return (dlr, dli, ddt * dt, db_re, db_im), ()
    return rowwise(fn, (lr, li, ldt, b_re, b_im, dar, dai, dbb_re, dbb_im),
                   ((1, F32), (1, F32), (1, F32), (SSM_GROUP, F32), (SSM_GROUP, F32)), name="s5_prep_bwd", tm=512)


def group_sum(x):
    def body(x_ref, o_ref):
        o_ref[...] = jnp.sum(x_ref[...], axis=1)
    return _pcall(body, name="s5_group_sum", out_shape=jax.ShapeDtypeStruct((N_GROUPS, 1), F32))(
        x.reshape(N_GROUPS, SSM_STATE, 1))


GROUPS_PER_TILE = LANES // SSM_GROUP
TILE_STATES = GROUPS_PER_TILE * SSM_STATE
N_UTILES = D_MODEL // LANES
TILES_PER_UTILE = TILE_STATES // LANES


SUBLANES = 8
SCAN_STRIP = 1024
N_STRIPS = N_STATES // SCAN_STRIP
_NT = (((1,), (1,)), ((), ()))
_TN = (((0,), (0,)), ((), ()))


def _scan_coefs(are, aim, shifted, reverse):
    ar = are[...]
    ai = -aim[...] if reverse else aim[...]
    powers = {1: (ar, ai)}
    for d in (2, 4):
        r, i = powers[d // 2]
        powers[d] = (r * r - i * i, 2.0 * r * i)
    rid = lax.broadcasted_iota(jnp.int32, (SUBLANES, N_STATES), 0)
    first = (rid == SUBLANES - 1) if reverse else (rid == 0)
    masks = [(1, first)] + [(d, (rid <= SUBLANES - 1 - d) if reverse else (rid >= d)) for d in (1, 2, 4)]
    for n, (d, keep) in enumerate(masks):
        for part in (0, 1):
            shifted[2 * n + part][...] = jnp.where(keep, jnp.broadcast_to(powers[d][part], (SUBLANES, N_STATES)), 0.0)


def _tile_scan(xr, xi, shifted, nbr_re, nbr_im, reverse):
    for n, d in enumerate((1, 1, 2, 4)):
        by = SUBLANES - d if reverse else d
        fr, fi = (nbr_re, nbr_im) if n == 0 else (xr, xi)
        sr, si = pltpu.roll(fr, by, 0), pltpu.roll(fi, by, 0)
        kr, ki = shifted[2 * n], shifted[2 * n + 1]
        xr, xi = xr + kr * sr - ki * si, xi + kr * si + ki * sr
    return xr, xi


def _tile_rows(t):
    return pl.ds(pl.multiple_of(t * SUBLANES, SUBLANES), SUBLANES)


def s5_fwd(u, bbd_re, bbd_im, cbd_re, cbd_imn, a_re, a_im, dskip, t_rows=256):
    seq = u.shape[0]
    t_rows = min(t_rows, seq)
    n_tiles = t_rows // SUBLANES

    def body(u_ref, bre, bim, cre, cimn, are, aim, d_ref, y_ref, hre_ref, him_ref, car_re, car_im, *shifted):
        @pl.when(pl.program_id(0) == 0)
        def _():
            car_re[...] = jnp.zeros_like(car_re)
            car_im[...] = jnp.zeros_like(car_im)
            _scan_coefs(are, aim, shifted, reverse=False)

        uf = u_ref[...]
        ub = uf.astype(BF16)
        for j in range(N_UTILES):
            uj = ub[:, LANES * j:LANES * (j + 1)]
            sl = slice(TILE_STATES * j, TILE_STATES * (j + 1))
            hre_ref[:, sl] = jnp.dot(uj, bre[j], preferred_element_type=F32)
            him_ref[:, sl] = jnp.dot(uj, bim[j], preferred_element_type=F32)
        for s in range(N_STRIPS):
            cols = pl.ds(s * SCAN_STRIP, SCAN_STRIP)
            coefs = [c[:, cols] for c in shifted]

            def step(t, before):
                rows = _tile_rows(t)
                hr, hi = _tile_scan(hre_ref[rows, cols], him_ref[rows, cols], coefs, before[0], before[1], False)
                hre_ref[rows, cols] = hr
                him_ref[rows, cols] = hi
                return hr, hi

            cr, ci = lax.fori_loop(0, n_tiles, step, (car_re[:, cols], car_im[:, cols]))
            car_re[:, cols] = cr
            car_im[:, cols] = ci
        dv = d_ref[...]
        for j in range(N_UTILES):
            st = slice(TILE_STATES * j, TILE_STATES * (j + 1))
            yj = (jnp.dot(hre_ref[:, st].astype(BF16), cre[j], preferred_element_type=F32)
                  + jnp.dot(him_ref[:, st].astype(BF16), cimn[j], preferred_element_type=F32))
            sl = slice(LANES * j, LANES * (j + 1))
            y_ref[:, sl] = yj + dv[:, sl] * uf[:, sl]

    full3 = lambda a: pl.BlockSpec(a.shape, lambda i: (0, 0, 0))
    full2 = lambda a: pl.BlockSpec(a.shape, lambda i: (0, 0))
    tile = pltpu.VMEM((SUBLANES, N_STATES), F32)
    return _pcall(
        body, name="s5_fwd", grid=(seq // t_rows,),
        in_specs=[pl.BlockSpec((t_rows, D_MODEL), lambda i: (i, 0)), full3(bbd_re), full3(bbd_im), full3(cbd_re),
                  full3(cbd_imn), full2(a_re), full2(a_im), full2(dskip)],
        out_specs=[pl.BlockSpec((t_rows, D_MODEL), lambda i: (i, 0)),
                   pl.BlockSpec((t_rows, N_STATES), lambda i: (i, 0)),
                   pl.BlockSpec((t_rows, N_STATES), lambda i: (i, 0))],
        out_shape=[jax.ShapeDtypeStruct((seq, D_MODEL), F32),
                   jax.ShapeDtypeStruct((seq, N_STATES), F32),
                   jax.ShapeDtypeStruct((seq, N_STATES), F32)],
        scratch_shapes=[tile] * 10,
        compiler_params=_params(("arbitrary",)),
    )(u, bbd_re, bbd_im, cbd_re, cbd_imn, a_re, a_im, dskip)


def s5_bwd(dy, u, dres, h_re, h_im, bbd_re, bbd_im, cbd_re, cbd_imn, a_re, a_im, dskip, t_rows=128):
    seq = u.shape[0]
    t_rows = min(t_rows, seq)
    n_chunks = seq // t_rows

    n_tiles = t_rows // SUBLANES

    def body(dy_ref, u_ref, dres_ref, hre_ref, him_ref, hpre_ref, hpim_ref, bre, bim, cre, cimn, are, aim, d_ref,
             dx_ref, dbre, dbim, dcre, dcimn, dar_ref, dai_ref, dd_ref, lre, lim, car_re, car_im, acc_re, acc_im,
             *shifted):
        i = pl.program_id(0)

        @pl.when(i == 0)
        def _():
            for r in (car_re, car_im, acc_re, acc_im, dbre, dbim, dcre, dcimn, dd_ref):
                r[...] = jnp.zeros_like(r)
            _scan_coefs(are, aim, shifted, reverse=True)

        dyf = dy_ref[...]
        dyb = dyf.astype(BF16)
        uf = u_ref[...]
        ub = uf.astype(BF16)
        for j in range(N_UTILES):
            dyj = dyb[:, LANES * j:LANES * (j + 1)]
            st = slice(TILE_STATES * j, TILE_STATES * (j + 1))
            lre[:, st] = lax.dot_general(dyj, cre[j], _NT, preferred_element_type=F32)
            lim[:, st] = lax.dot_general(dyj, cimn[j], _NT, preferred_element_type=F32)
        has_pred = (i < n_chunks - 1).astype(F32)
        last_row = lax.broadcasted_iota(jnp.int32, (SUBLANES, SCAN_STRIP), 0) == SUBLANES - 1
        for s in range(N_STRIPS):
            cols = pl.ds(s * SCAN_STRIP, SCAN_STRIP)
            coefs = [c[:, cols] for c in shifted]
            before_re, before_im = hpre_ref[:, cols] * has_pred, hpim_ref[:, cols] * has_pred

            def step(k, carry):
                after_re, after_im, dar, dai = carry
                t = n_tiles - 1 - k
                rows = _tile_rows(t)
                lr, li = _tile_scan(lre[rows, cols], lim[rows, cols], coefs, after_re, after_im, True)
                lre[rows, cols] = lr
                lim[rows, cols] = li
                prev = _tile_rows(jnp.maximum(t - 1, 0))
                pre_re = jnp.where(t == 0, before_re, hre_ref[prev, cols])
                pre_im = jnp.where(t == 0, before_im, him_ref[prev, cols])
                hpr = pltpu.roll(jnp.where(last_row, pre_re, hre_ref[rows, cols]), 1, 0)
                hpi = pltpu.roll(jnp.where(last_row, pre_im, him_ref[rows, cols]), 1, 0)
                return lr, li, dar + lr * hpr + li * hpi, dai + li * hpr - lr * hpi

            cr, ci, dar, dai = lax.fori_loop(0, n_tiles, step, (car_re[:, cols], car_im[:, cols],
                                                               acc_re[:, cols], acc_im[:, cols]))
            car_re[:, cols] = cr
            car_im[:, cols] = ci
            acc_re[:, cols] = dar
            acc_im[:, cols] = dai

        dv = d_ref[...]
        for j in range(N_UTILES):
            sl = slice(LANES * j, LANES * (j + 1))
            st = slice(TILE_STATES * j, TILE_STATES * (j + 1))
            lrj = lre[:, st].astype(BF16)
            lij = lim[:, st].astype(BF16)
            du = (lax.dot_general(lrj, bre[j], _NT, preferred_element_type=F32)
                  + lax.dot_general(lij, bim[j], _NT, preferred_element_type=F32))
            dx_ref[:, sl] = du + dv[:, sl] * dyf[:, sl] + DN_ALPHA * dres_ref[:, sl]
            uj = ub[:, sl]
            dbre[j] += lax.dot_general(uj, lrj, _TN, preferred_element_type=F32)
            dbim[j] += lax.dot_general(uj, lij, _TN, preferred_element_type=F32)
            dyj = dyb[:, sl]
            dcre[j] += lax.dot_general(hre_ref[:, st].astype(BF16), dyj, _TN, preferred_element_type=F32)
            dcimn[j] += lax.dot_general(him_ref[:, st].astype(BF16), dyj, _TN, preferred_element_type=F32)
        dd_ref[...] += jnp.sum(dyf * uf, axis=0, keepdims=True)

        @pl.when(i == n_chunks - 1)
        def _():
            dar_ref[...] = jnp.sum(acc_re[...], axis=0, keepdims=True)
            dai_ref[...] = jnp.sum(acc_im[...], axis=0, keepdims=True)

    rev = lambda i: (n_chunks - 1 - i, 0)
    prev_tile = lambda i: (jnp.maximum((n_chunks - 1 - i) * n_tiles - 1, 0), 0)
    full3 = lambda a: pl.BlockSpec(a.shape, lambda i: (0, 0, 0))
    full2 = lambda a: pl.BlockSpec(a.shape, lambda i: (0, 0))
    acc3 = lambda shape: pl.BlockSpec(shape, lambda i: (0, 0, 0))
    acc2 = lambda shape: pl.BlockSpec(shape, lambda i: (0, 0))
    tile = pltpu.VMEM((SUBLANES, N_STATES), F32)
    return _pcall(
        body, name="s5_bwd", grid=(n_chunks,),
        in_specs=[pl.BlockSpec((t_rows, D_MODEL), rev), pl.BlockSpec((t_rows, D_MODEL), rev),
                  pl.BlockSpec((t_rows, D_MODEL), rev),
                  pl.BlockSpec((t_rows, N_STATES), rev), pl.BlockSpec((t_rows, N_STATES), rev),
                  pl.BlockSpec((SUBLANES, N_STATES), prev_tile), pl.BlockSpec((SUBLANES, N_STATES), prev_tile),
                  full3(bbd_re), full3(bbd_im), full3(cbd_re), full3(cbd_imn), full2(a_re), full2(a_im), full2(dskip)],
        out_specs=[pl.BlockSpec((t_rows, D_MODEL), rev), acc3(bbd_re.shape), acc3(bbd_im.shape), acc3(cbd_re.shape),
                   acc3(cbd_imn.shape), acc2((1, N_STATES)), acc2((1, N_STATES)), acc2((1, D_MODEL))],
        out_shape=[jax.ShapeDtypeStruct((seq, D_MODEL), F32), jax.ShapeDtypeStruct(bbd_re.shape, F32),
                   jax.ShapeDtypeStruct(bbd_im.shape, F32), jax.ShapeDtypeStruct(cbd_re.shape, F32),
                   jax.ShapeDtypeStruct(cbd_imn.shape, F32), jax.ShapeDtypeStruct((1, N_STATES), F32),
                   jax.ShapeDtypeStruct((1, N_STATES), F32), jax.ShapeDtypeStruct((1, D_MODEL), F32)],
        scratch_shapes=[pltpu.VMEM((t_rows, N_STATES), F32), pltpu.VMEM((t_rows, N_STATES), F32)] + [tile] * 12,
        compiler_params=_params(("arbitrary",)),
    )(dy, u, dres, h_re, h_im, h_re, h_im, bbd_re, bbd_im, cbd_re, cbd_imn, a_re, a_im, dskip)


def _eye_groups():
    return jnp.eye(GROUPS_PER_TILE, dtype=F32)


def _blockdiag_in(bb):
    t = bb.transpose(0, 2, 1).reshape(N_UTILES, GROUPS_PER_TILE, SSM_GROUP, SSM_STATE)
    bd = jnp.einsum("jgcp,gh->jgchp", t, _eye_groups())
    return bd.reshape(N_UTILES, LANES, TILE_STATES)


def _blockdiag_in_t(d):
    t = jnp.einsum("jgchp,gh->jgcp", d.reshape(N_UTILES, GROUPS_PER_TILE, SSM_GROUP, GROUPS_PER_TILE, SSM_STATE),
                   _eye_groups())
    return t.reshape(N_GROUPS, SSM_GROUP, SSM_STATE).transpose(0, 2, 1)


def _blockdiag_out(c):
    t = c.transpose(0, 2, 1).reshape(N_UTILES, GROUPS_PER_TILE, SSM_STATE, SSM_GROUP)
    bd = jnp.einsum("jhpc,hg->jhpgc", t, _eye_groups())
    return bd.reshape(N_UTILES, TILE_STATES, LANES)


def _blockdiag_out_t(d):
    t = jnp.einsum("jhpgc,hg->jhpc", d.reshape(N_UTILES, GROUPS_PER_TILE, SSM_STATE, GROUPS_PER_TILE, SSM_GROUP),
                   _eye_groups())
    return t.reshape(N_GROUPS, SSM_STATE, SSM_GROUP).transpose(0, 2, 1)


ATT_TQ = 512
ATT_TK = 512
LOG2E = math.log2(math.e)
LN2 = math.log(2.0)
Q_PRESCALE = SM_SCALE * LOG2E


def _loop_in_pairs(n, step, carry, start=0):
    pairs = (n - start) // 2

    def two(t, c):
        return step(start + 2 * t + 1, step(start + 2 * t, c))

    carry = lax.fori_loop(0, pairs, two, carry)
    return lax.fori_loop(start + 2 * pairs, n, step, carry)


def _causal(s, off=0, transposed=False):
    r = lax.broadcasted_iota(jnp.int32, s.shape, 0)
    c = lax.broadcasted_iota(jnp.int32, s.shape, 1)
    keep = (r <= c + off) if transposed else (c <= r + off)
    return jnp.where(keep, s, NEG_INF)


def attn_fwd(q, k, v, tq=ATT_TQ, tk=ATT_TK):
    n_heads, seq, _ = q.shape
    tq, tk = min(tq, seq), min(tk, seq)

    def body(q_ref, k_ref, v_ref, o_ref, lse_ref):
        qi = pl.program_id(1)
        qv = q_ref[0]
        jd = (qi * tq) // tk

        def block(j, carry, diag):
            m, l, acc = carry
            rows = pl.ds(pl.multiple_of(j * tk, tk), tk)
            s = lax.dot_general(qv, k_ref[0, rows, :], _NT, preferred_element_type=F32)
            if diag:
                s = _causal(s, qi * tq - jd * tk)
            m_new = jnp.maximum(m, jnp.max(s, axis=-1, keepdims=True))
            p = jnp.exp2(s - m_new)
            corr = jnp.exp2(m - m_new)
            l = l * corr + jnp.sum(p, axis=-1, keepdims=True)
            acc = acc * corr + jnp.dot(p.astype(BF16), v_ref[rows, :], preferred_element_type=F32)
            return m_new, l, acc

        init = (jnp.full((tq, 1), NEG_INF, F32), jnp.zeros((tq, 1), F32), jnp.zeros((tq, V_HEAD), F32))
        carry = _loop_in_pairs(jd, lambda j, c: block(j, c, False), init)
        m, l, acc = block(jd, carry, True)
        o_ref[...] = acc / l
        lse_ref[0] = jnp.broadcast_to(m + jnp.log2(l), (tq, LANES))

    return _pcall(
        body, name="attn_fwd", grid=(n_heads, seq // tq),
        in_specs=[pl.BlockSpec((1, tq, QK_DIM), lambda h, i: (h, i, 0)),
                  pl.BlockSpec((1, seq, QK_DIM), lambda h, i: (h, 0, 0)),
                  pl.BlockSpec((seq, V_HEAD), lambda h, i: (0, h))],
        out_specs=[pl.BlockSpec((tq, V_HEAD), lambda h, i: (i, h)),
                   pl.BlockSpec((1, tq, LANES), lambda h, i: (h, i, 0))],
        out_shape=[jax.ShapeDtypeStruct((seq, n_heads * V_HEAD), F32),
                   jax.ShapeDtypeStruct((n_heads, seq, LANES), F32)],
        compiler_params=_params(("parallel", "parallel")),
    )(q, k, v)


def attn_bwd_dq(q, k, v, do, o, lse, tq=ATT_TQ, tk=ATT_TK):
    n_heads, seq, _ = q.shape
    tq, tk = min(tq, seq), min(tk, seq)

    def body(q_ref, k_ref, v_ref, do_ref, o_ref, lse_ref, dqn_ref, dqr_ref, delta_ref):
        qi = pl.program_id(1)
        qv = q_ref[0]
        dof = do_ref[...]
        dob = dof.astype(BF16)
        delta = jnp.sum(dof * o_ref[...], axis=-1, keepdims=True)
        lse = lse_ref[0][:, :1]
        jd = (qi * tq) // tk

        def block(j, dq, diag):
            rows = pl.ds(pl.multiple_of(j * tk, tk), tk)
            kv = k_ref[0, rows, :]
            s = lax.dot_general(qv, kv, _NT, preferred_element_type=F32)
            if diag:
                s = _causal(s, qi * tq - jd * tk)
            p = jnp.exp2(s - lse)
            dp = lax.dot_general(dob, v_ref[rows, :], _NT, preferred_element_type=F32)
            ds = p * (dp - delta)
            return dq + jnp.dot(ds.astype(BF16), kv, preferred_element_type=F32)

        dq = _loop_in_pairs(jd, lambda j, c: block(j, c, False), jnp.zeros((tq, QK_DIM), F32))
        dq = block(jd, dq, True) * SM_SCALE
        dqn_ref[...] = dq[:, :QK_NOPE]
        dqr_ref[0] = dq[:, QK_NOPE:]
        delta_ref[0] = jnp.broadcast_to(delta, (tq, LANES))

    return _pcall(
        body, name="attn_bwd_dq", grid=(n_heads, seq // tq),
        in_specs=[pl.BlockSpec((1, tq, QK_DIM), lambda h, i: (h, i, 0)),
                  pl.BlockSpec((1, seq, QK_DIM), lambda h, i: (h, 0, 0)),
                  pl.BlockSpec((seq, V_HEAD), lambda h, i: (0, h)),
                  pl.BlockSpec((tq, V_HEAD), lambda h, i: (i, h)),
                  pl.BlockSpec((tq, V_HEAD), lambda h, i: (i, h)),
                  pl.BlockSpec((1, tq, LANES), lambda h, i: (h, i, 0))],
        out_specs=[pl.BlockSpec((tq, QK_NOPE), lambda h, i: (i, h)),
                   pl.BlockSpec((1, tq, QK_ROPE), lambda h, i: (h, i, 0)),
                   pl.BlockSpec((1, tq, LANES), lambda h, i: (h, i, 0))],
        out_shape=[jax.ShapeDtypeStruct((seq, n_heads * QK_NOPE), F32),
                   jax.ShapeDtypeStruct((n_heads, seq, QK_ROPE), F32),
                   jax.ShapeDtypeStruct((n_heads, seq, LANES), F32)],
        compiler_params=_params(("parallel", "parallel")),
    )(q, k, v, do, o, lse)


def attn_bwd_dkv(q, k, v, do, lse_row, delta_row, tq=ATT_TK):
    n_heads, seq, _ = q.shape
    tq = min(tq, seq)
    n_blk = seq // tq

    def body(q_ref, k_ref, v_ref, do_ref, lse_ref, delta_ref, dkn_ref, dkr_ref, dv_ref):
        kj = pl.program_id(1)
        kv = k_ref[0]
        vv = v_ref[...]

        def block(i, carry, diag):
            dk, dv = carry
            rows = pl.ds(pl.multiple_of(i * tq, tq), tq)
            qv = q_ref[0, rows, :]
            st = lax.dot_general(kv, qv, _NT, preferred_element_type=F32)
            if diag:
                st = _causal(st, transposed=True)
            pt = jnp.exp2(st - lse_ref[0, pl.ds(i, 1), :])
            dob = do_ref[rows, :].astype(BF16)
            dv = dv + jnp.dot(pt.astype(BF16), dob, preferred_element_type=F32)
            dpt = lax.dot_general(vv, dob, _NT, preferred_element_type=F32)
            dst = pt * (dpt - delta_ref[0, pl.ds(i, 1), :])
            dk = dk + jnp.dot(dst.astype(BF16), qv, preferred_element_type=F32)
            return dk, dv

        carry = block(kj, (jnp.zeros((tq, QK_DIM), F32), jnp.zeros((tq, V_HEAD), F32)), True)
        dk, dv = _loop_in_pairs(n_blk, lambda i, c: block(i, c, False), carry, start=kj + 1)
        dk = dk * LN2
        dkn_ref[...] = dk[:, :QK_NOPE]
        dkr_ref[0] = dk[:, QK_NOPE:]
        dv_ref[...] = dv

    return _pcall(
        body, name="attn_bwd_dkv", grid=(n_heads, n_blk),
        in_specs=[pl.BlockSpec((1, seq, QK_DIM), lambda h, j: (h, 0, 0)),
                  pl.BlockSpec((1, tq, QK_DIM), lambda h, j: (h, j, 0)),
                  pl.BlockSpec((tq, V_HEAD), lambda h, j: (j, h)),
                  pl.BlockSpec((seq, V_HEAD), lambda h, j: (0, h)),
                  pl.BlockSpec((1, n_blk, tq), lambda h, j: (h, 0, 0)),
                  pl.BlockSpec((1, n_blk, tq), lambda h, j: (h, 0, 0))],
        out_specs=[pl.BlockSpec((tq, QK_NOPE), lambda h, j: (j, h)),
                   pl.BlockSpec((1, tq, QK_ROPE), lambda h, j: (h, j, 0)),
                   pl.BlockSpec((tq, V_HEAD), lambda h, j: (j, h))],
        out_shape=[jax.ShapeDtypeStruct((seq, n_heads * QK_NOPE), F32),
                   jax.ShapeDtypeStruct((n_heads, seq, QK_ROPE), F32),
                   jax.ShapeDtypeStruct((seq, n_heads * V_HEAD), F32)],
        compiler_params=_params(("parallel", "parallel")),
    )(q, k, v, do, lse_row, delta_row)


def head_sum(x, ts=512):
    n_heads, seq, w = x.shape
    ts = min(ts, seq)

    def body(x_ref, o_ref):
        o_ref[...] = jnp.sum(x_ref[...], axis=0)

    return _pcall(body, name="head_sum", grid=(seq // ts,),
                  in_specs=[pl.BlockSpec((n_heads, ts, w), lambda i: (0, i, 0))],
                  out_specs=pl.BlockSpec((ts, w), lambda i: (i, 0)),
                  out_shape=jax.ShapeDtypeStruct((seq, w), F32),
                  compiler_params=_params(("parallel",)))(x)


HEADS_PER_CHIP = N_HEADS // N_CHIPS
Q_CHIP = HEADS_PER_CHIP * QK_DIM
Q_CHIP_NOPE = HEADS_PER_CHIP * QK_NOPE


def _perm_q_cols(w):
    t = w.reshape(w.shape[0], HEADS_PER_CHIP, QK_DIM)
    return jnp.concatenate([t[:, :, :QK_NOPE].reshape(w.shape[0], -1),
                            t[:, :, QK_NOPE:QK_NOPE + HALF_ROPE].reshape(w.shape[0], -1),
                            t[:, :, QK_NOPE + HALF_ROPE:].reshape(w.shape[0], -1)], axis=1)


def _unperm_q_cols(w):
    r = w.shape[0]
    nope = w[:, :Q_CHIP_NOPE].reshape(r, HEADS_PER_CHIP, QK_NOPE)
    r1 = w[:, Q_CHIP_NOPE:Q_CHIP_NOPE + QK_ROPE].reshape(r, HEADS_PER_CHIP, HALF_ROPE)
    r2 = w[:, Q_CHIP_NOPE + QK_ROPE:].reshape(r, HEADS_PER_CHIP, HALF_ROPE)
    return jnp.concatenate([nope, r1, r2], axis=2).reshape(r, Q_CHIP)


def _pad_kva_cols(w):
    z = jnp.zeros((w.shape[0], HALF_ROPE), w.dtype)
    return jnp.concatenate([w[:, :KV_LORA], w[:, KV_LORA:KV_LORA + HALF_ROPE], z, w[:, KV_LORA + HALF_ROPE:], z], axis=1)


def _unpad_kva_cols(w):
    return jnp.concatenate([w[:, :KV_LORA], w[:, KV_LORA:KV_LORA + HALF_ROPE],
                            w[:, KV_LORA + QK_ROPE:KV_LORA + QK_ROPE + HALF_ROPE]], axis=1)


def _rope_tile(t, cs, sn):
    return t * cs + pltpu.roll(t, LANES // 2, 1) * sn


def _rope_tile_bwd(d, cs, sn):
    return d * cs + pltpu.roll(d * sn, LANES // 2, 1)


def _b_cols(tk, tn):
    return pl.BlockSpec((None, tk, tn), lambda i, j, k: (j, k, 0))


def _b_cols_t(tk, tn):
    return pl.BlockSpec((None, tn, tk), lambda i, j, k: (k, j, 0))


def _out_cols(shape):
    return shape, lambda tm, tn: pl.BlockSpec((None, tm, tn), lambda i, j, k: (j, i, 0))


def device_step(x, positions, target, w):
    seq = x.shape[0]

    inv_freq = ROPE_THETA ** (-jnp.arange(HALF_ROPE, dtype=F32) / HALF_ROPE)
    ang = positions.astype(F32)[:, None] * inv_freq
    cos, sin = jnp.cos(ang), jnp.sin(ang)
    zero = jnp.zeros_like(cos)
    cos_q, sin_q = jnp.concatenate([cos] * 4, 1), jnp.concatenate([-sin, -sin, sin, sin], 1)
    cos_k, sin_k = jnp.concatenate([cos, zero, cos, zero], 1), jnp.concatenate([-sin, zero, sin, zero], 1)
    w1, w2 = w["w_ff1"], w["w_ff2"]
    ff_tile = D_FF // N_CHIPS
    pack_shape = (N_CHIPS, G_PACK_ROWS, PACK_W)

    lr = w["ssm_lam_re"].reshape(N_STATES, 1)
    li = w["ssm_lam_im"].reshape(N_STATES, 1)
    ldt = jnp.repeat(w["ssm_log_dt"].reshape(N_GROUPS), SSM_STATE).reshape(N_STATES, 1)
    b_re = w["ssm_b_re"].reshape(N_STATES, SSM_GROUP)
    b_im = w["ssm_b_im"].reshape(N_STATES, SSM_GROUP)
    a_re, a_im, bb_re, bb_im = s5_prep(lr, li, ldt, b_re, b_im)
    a_re, a_im = a_re.reshape(1, N_STATES), a_im.reshape(1, N_STATES)
    bbd_re = _blockdiag_in(bb_re.reshape(N_GROUPS, SSM_STATE, SSM_GROUP)).astype(BF16)
    bbd_im = _blockdiag_in(bb_im.reshape(N_GROUPS, SSM_STATE, SSM_GROUP)).astype(BF16)
    cbd_re = _blockdiag_out(w["ssm_c_re"].reshape(N_GROUPS, SSM_GROUP, SSM_STATE)).astype(BF16)
    cbd_imn = _blockdiag_out(-w["ssm_c_im"].reshape(N_GROUPS, SSM_GROUP, SSM_STATE)).astype(BF16)
    dskip = w["ssm_d"].reshape(1, D_MODEL)
    ypre, h_re, h_im = s5_fwd(x, bbd_re, bbd_im, cbd_re, cbd_imn, a_re, a_im, dskip)
    (yg,) = rowwise(lambda y: ((_gelu(y),), ()), (ypre,), ((D_MODEL, BF16),), name="gelu")
    w_glu = w["ssm_w_glu"]
    glu_tile = w_glu.shape[2]
    vg = mm(yg, w_glu, n_dim=2 * D_MODEL, tiles=(None, glu_tile, None), b_view=_b_cols, name="glu_proj")

    def glu(v):
        return (v[:, :D_MODEL] * _sigmoid(v[:, D_MODEL:]),), ()
    (z,) = rowwise(glu, (vg,), ((D_MODEL, BF16),), name="glu")
    w_out = w["ssm_w_out"].reshape(D_MODEL, D_MODEL)
    mix0 = mm(z, w_out, name="ssm_out")

    def mlp_fwd(hb, layer):
        pre = mm(hb, w1, n_dim=D_FF, tiles=(None, ff_tile, None), name=f"ff1_{layer}", out_dtypes=(BF16,),
                 b_view=lambda tk, tn: pl.BlockSpec((None, None, tk, tn), lambda i, j, k: (j, layer, k, 0)))
        f = mm(pre, w2, pro_a=_relu2, n_dim=D_MODEL, tiles=(None, D_MODEL, None), name=f"ff2_{layer}",
               b_view=lambda tk, tn: pl.BlockSpec((None, None, tk, tn),
                                                  lambda i, j, k: (k // (ff_tile // tk), layer, k % (ff_tile // tk), j)))
        return pre, f

    ln = lambda name, l: w[name][l].reshape(1, D_MODEL)
    h1, h1b = ln_fwd(x, mix0, ln("ln_mix_g", 0), ln("ln_mix_b", 0), "ln_mix_0")
    f1pre, f1 = mlp_fwd(h1b, 0)
    h2, h2b = ln_fwd(h1, f1, ln("ln_ffn_g", 0), ln("ln_ffn_b", 0), "ln_ffn_0")

    kv_w_a = w["kv_w_a"].reshape(D_MODEL, KVA_PAD)
    kv_w_b = w["kv_w_b"]
    q_w_a = w["q_w_a"].reshape(D_MODEL, Q_LORA)
    q_w_b = w["q_w_b"]
    w_o = w["attn_w_o"].reshape(D_MODEL, D_MODEL)
    kvb_tile = kv_w_b.shape[2]
    kvn_g = w["kv_norm_g"].reshape(1, KV_LORA)
    qn_g = w["q_norm_g"].reshape(1, Q_LORA)
    kva = mm(h2b, kv_w_a, name="kv_a")

    def kv_post(kva, g, cs, sn):
        return (_rms(kva[:, :KV_LORA], g), _rope_tile(kva[:, KV_LORA:], cs, sn)), ()
    ckv, krope = rowwise(kv_post, (kva, kvn_g, cos_k, sin_k), ((KV_LORA, BF16), (LANES, BF16)), name="kv_post")
    kvb = mm(ckv, kv_w_b, n_dim=N_CHIPS * kvb_tile, tiles=(None, kvb_tile, KV_LORA), b_view=_b_cols, name="kv_b",
             out_dtypes=(BF16,))
    cq_raw = mm(h2b, q_w_a, name="q_a")
    (cq,) = rowwise(lambda c, g: ((_rms(c, g),), ()), (cq_raw, qn_g), ((Q_LORA, BF16),), name="q_norm")
    qlin = mm(cq, q_w_b, n_dim=N_CHIPS * Q_CHIP, tiles=(None, Q_CHIP, Q_LORA), b_view=_b_cols, name="q_b")

    def on_rope_tiles(fn, scale=None):
        def apply(q, cs, sn):
            parts = []
            for k in range(N_CHIPS):
                parts.append(q[:, Q_CHIP * k:Q_CHIP * k + Q_CHIP_NOPE])
                parts.append(fn(q[:, Q_CHIP * k + Q_CHIP_NOPE:Q_CHIP * (k + 1)], cs, sn))
            out = jnp.concatenate(parts, axis=1)
            return (out if scale is None else out * scale,), ()
        return apply
    (qro,) = rowwise(on_rope_tiles(_rope_tile, Q_PRESCALE), (qlin, cos_q, sin_q), ((N_CHIPS * Q_CHIP, BF16),),
                     name="q_rope")
    qro3 = qro.reshape(seq, N_CHIPS, Q_CHIP)
    q_h = jnp.concatenate([qro3[:, :, :Q_CHIP_NOPE].reshape(seq, N_HEADS, QK_NOPE),
                           qro3[:, :, Q_CHIP_NOPE:Q_CHIP_NOPE + QK_ROPE].reshape(seq, N_HEADS, HALF_ROPE),
                           qro3[:, :, Q_CHIP_NOPE + QK_ROPE:].reshape(seq, N_HEADS, HALF_ROPE)], axis=2).transpose(1, 0, 2)
    kvb3 = kvb.reshape(seq, N_HEADS, QK_NOPE + V_HEAD)
    kr = jnp.concatenate([krope[:, :HALF_ROPE], krope[:, QK_ROPE:QK_ROPE + HALF_ROPE]], axis=1)
    k_h = jnp.concatenate([kvb3[:, :, :QK_NOPE], jnp.broadcast_to(kr[:, None, :], (seq, N_HEADS, QK_ROPE))],
                          axis=2).transpose(1, 0, 2)
    v2 = kvb3[:, :, QK_NOPE:].reshape(seq, N_HEADS * V_HEAD)
    o, lse = attn_fwd(q_h, k_h, v2)
    mix1 = mm(o, w_o, name="attn_out")
    h3, h3b = ln_fwd(h2, mix1, ln("ln_mix_g", 1), ln("ln_mix_b", 1), "ln_mix_1")
    f2pre, f2 = mlp_fwd(h3b, 1)
    h4, _ = ln_fwd(h3, f2, ln("ln_ffn_g", 1), ln("ln_ffn_b", 1), "ln_ffn_1")

    def loss_fn(y, t):
        e = y - t
        return (e * (1.0 / D_MODEL),), (jnp.broadcast_to(jnp.sum(e * e), (1, LANES)),)
    dh4, loss_acc = rowwise(loss_fn, (h4, target), ((D_MODEL, F32),), accs=(LANES,), name="loss")
    loss = loss_acc[0, 0] * (0.5 / D_MODEL)

    g = {}

    def into_rows(off, rows_per_chip):
        def view(tm, tn):
            nb = rows_per_chip // tm
            return pl.BlockSpec((None, tm, tn), lambda i, j, k: (i // nb, off // tm + i % nb, 0))
        return pack_shape, view

    def into_cols(off):
        return pack_shape, lambda tm, tn: pl.BlockSpec((None, tm, tn), lambda i, j, k: (j, off // tm + i, 0))

    def mlp_bwd(pack, dr, drb, hb, pre, layer):
        nb = lambda tk: ff_tile // tk
        dpre = mm(drb, w2, tb=True, epi=lambda r, p: (r * 2.0 * jnp.maximum(p, 0.0),), extras=(pre,),
                  out_dtypes=(BF16,), n_dim=D_FF, tiles=(None, ff_tile, None), name=f"ff2_dx_{layer}",
                  b_view=lambda tk, tn: pl.BlockSpec((None, None, tn, tk), lambda i, j, k: (j, layer, 0, k)))
        pack = mm(pre, drb, ta=True, pro_a=_relu2, name=f"ff2_dw_{layer}", tiles=(None, PACK_W, None), into=pack,
                  out_view=into_rows(DIRECT_OFF["w_ff2"] + layer * ff_tile, ff_tile))
        pack = mm(hb, dpre, ta=True, name=f"ff1_dw_{layer}", tiles=(None, PACK_W, None), into=pack,
                  out_view=into_cols(DIRECT_OFF["w_ff1"] + layer * D_MODEL))
        dh = mm(dpre, w1, tb=True, epi=lambda r, d: (r + DN_ALPHA * d,), extras=(dr,), n_dim=D_MODEL,
                tiles=(None, D_MODEL, None), name=f"ff1_dx_{layer}",
                b_view=lambda tk, tn: pl.BlockSpec((None, None, tn, tk), lambda i, j, k: (k // nb(tk), layer, 0, k % nb(tk))))
        return pack, dh

    dr4, dr4b, dg_f1, db_f1 = ln_bwd(h3, f2, ln("ln_ffn_g", 1), dh4, "ln_ffn_bwd_1")
    pack, dh3 = mlp_bwd(None, dr4, dr4b, h3b, f2pre, 1)
    dr3, dr3b, dg_m1, db_m1 = ln_bwd(h2, mix1, ln("ln_mix_g", 1), dh3, "ln_mix_bwd_1")
    shard_rows = D_MODEL // N_CHIPS
    pack = mm(o, dr3b, ta=True, name="attn_out_dw", tiles=(shard_rows, PACK_W, None), into=pack,
              out_view=into_rows(DIRECT_OFF["attn_w_o"], shard_rows))
    do = mm(dr3b, w_o, tb=True, name="attn_out_dx")
    dqn, dqr, delta = attn_bwd_dq(q_h, k_h, v2, do, o, lse)
    tb = min(ATT_TK, seq)
    lse_row = lse[:, :, 0].reshape(N_HEADS, seq // tb, tb)
    delta_row = delta[:, :, 0].reshape(N_HEADS, seq // tb, tb)
    dkn, dkr, dv = attn_bwd_dkv(q_h, k_h, v2, do, lse_row, delta_row)
    dqr_t = dqr.transpose(1, 0, 2)
    dq_cat = jnp.concatenate([dqn.reshape(seq, N_CHIPS, Q_CHIP_NOPE), dqr_t[:, :, :HALF_ROPE].reshape(seq, N_CHIPS, QK_ROPE),
                              dqr_t[:, :, HALF_ROPE:].reshape(seq, N_CHIPS, QK_ROPE)], 2).reshape(seq, N_CHIPS * Q_CHIP)
    (dqlin,) = rowwise(on_rope_tiles(_rope_tile_bwd), (dq_cat, cos_q, sin_q), ((N_CHIPS * Q_CHIP, BF16),), name="q_rope_bwd")
    g["q_w_b"] = mm(cq, dqlin, ta=True, name="q_b_dw", tiles=(Q_LORA, Q_CHIP, None), out_view=_out_cols(q_w_b.shape))
    dcq = mm(dqlin, q_w_b, tb=True, n_dim=Q_LORA, tiles=(None, Q_LORA, Q_CHIP), b_view=_b_cols_t, name="q_b_dx")

    def q_norm_bwd(c, gq, d):
        dx, dgq = _rms_bwd(c, gq, d)
        return (dx,), (dgq,)
    dcq_raw, dqn_g = rowwise(q_norm_bwd, (cq_raw, qn_g, dcq), ((Q_LORA, BF16),), accs=(Q_LORA,), name="q_norm_bwd")
    g["q_w_a"] = mm(h2b, dcq_raw, ta=True, name="q_a_dw")
    dkvb = jnp.concatenate([dkn.reshape(seq, N_HEADS, QK_NOPE), dv.reshape(seq, N_HEADS, V_HEAD)], 2).reshape(
        seq, N_HEADS * (QK_NOPE + V_HEAD)).astype(BF16)
    g["kv_w_b"] = mm(ckv, dkvb, ta=True, name="kv_b_dw", tiles=(KV_LORA, kvb_tile, None), out_view=_out_cols(kv_w_b.shape))
    dckv = mm(dkvb, kv_w_b, tb=True, n_dim=KV_LORA, tiles=(None, KV_LORA, kvb_tile), b_view=_b_cols_t, name="kv_b_dx")
    dkr_sum = head_sum(dkr)
    zpad = jnp.zeros((seq, HALF_ROPE), F32)
    dkr_tile = jnp.concatenate([dkr_sum[:, :HALF_ROPE], zpad, dkr_sum[:, HALF_ROPE:], zpad], 1)

    def kv_post_bwd(kva, gk, dc, dk, cs, sn):
        dx, dgk = _rms_bwd(kva[:, :KV_LORA], gk, dc)
        return (jnp.concatenate([dx, _rope_tile_bwd(dk, cs, sn)], axis=1),), (dgk,)
    dkva, dkvn_g = rowwise(kv_post_bwd, (kva, kvn_g, dckv, dkr_tile, cos_k, sin_k), ((KVA_PAD, BF16),),
                           accs=(KV_LORA,), name="kv_post_bwd")
    g["kv_w_a"] = mm(h2b, dkva, ta=True, name="kv_a_dw")
    dh2 = mm(dcq_raw, q_w_a, tb=True, epi=lambda r, d: (r + DN_ALPHA * d,), extras=(dr3,), name="q_a_dx")
    dh2 = mm(dkva, kv_w_a, tb=True, epi=lambda r, d: (r + d,), extras=(dh2,), name="kv_a_dx")

    dr2, dr2b, dg_f0, db_f0 = ln_bwd(h1, f1, ln("ln_ffn_g", 0), dh2, "ln_ffn_bwd_0")
    pack, dh1 = mlp_bwd(pack, dr2, dr2b, h1b, f1pre, 0)
    dr1, dr1b, dg_m0, db_m0 = ln_bwd(x, mix0, ln("ln_mix_g", 0), dh1, "ln_mix_bwd_0")
    pack = mm(z, dr1b, ta=True, name="ssm_out_dw", tiles=(shard_rows, PACK_W, None), into=pack,
              out_view=into_rows(DIRECT_OFF["ssm_w_out"], shard_rows))
    dz = mm(dr1b, w_out, tb=True, name="ssm_out_dx")

    def glu_bwd(v, dz):
        val, sg = v[:, :D_MODEL], _sigmoid(v[:, D_MODEL:])
        return (jnp.concatenate([dz * sg, dz * val * sg * (1.0 - sg)], axis=1),), ()
    (dvg,) = rowwise(glu_bwd, (vg, dz), ((2 * D_MODEL, BF16),), name="glu_bwd")
    g["ssm_w_glu"] = mm(yg, dvg, ta=True, name="glu_proj_dw", tiles=(None, glu_tile, None), out_view=_out_cols(w_glu.shape))
    dypre = mm(dvg, w_glu, tb=True, epi=lambda r, y: (r * _gelu_grad(y),), extras=(ypre,), n_dim=D_MODEL,
               tiles=(None, D_MODEL, glu_tile), b_view=_b_cols_t, name="glu_proj_dx")
    dx, dbbd_re, dbbd_im, dcbd_re, dcbd_imn, dar, dai, dd = s5_bwd(
        dypre, x, dr1, h_re, h_im, bbd_re, bbd_im, cbd_re, cbd_imn, a_re, a_im, dskip)
    dbb_re = _blockdiag_in_t(dbbd_re).reshape(N_STATES, SSM_GROUP)
    dbb_im = _blockdiag_in_t(dbbd_im).reshape(N_STATES, SSM_GROUP)
    dlr, dli, dldt, db_re, db_im = s5_prep_bwd(lr, li, ldt, b_re, b_im, dar.reshape(N_STATES, 1),
                                               dai.reshape(N_STATES, 1), dbb_re, dbb_im)
    g["ssm_lam_re"] = dlr.reshape(1, N_GROUPS, SSM_STATE)
    g["ssm_lam_im"] = dli.reshape(1, N_GROUPS, SSM_STATE)
    g["ssm_log_dt"] = group_sum(dldt).reshape(1, N_GROUPS)
    g["ssm_b_re"] = db_re.reshape(1, N_GROUPS, SSM_STATE, SSM_GROUP)
    g["ssm_b_im"] = db_im.reshape(1, N_GROUPS, SSM_STATE, SSM_GROUP)
    g["ssm_c_re"] = _blockdiag_out_t(dcbd_re).reshape(1, N_GROUPS, SSM_GROUP, SSM_STATE)
    g["ssm_c_im"] = -_blockdiag_out_t(dcbd_imn).reshape(1, N_GROUPS, SSM_GROUP, SSM_STATE)
    g["ssm_d"] = dd
    g["ln_mix_g"] = jnp.concatenate([dg_m0, dg_m1], 0)
    g["ln_mix_b"] = jnp.concatenate([db_m0, db_m1], 0)
    g["ln_ffn_g"] = jnp.concatenate([dg_f0, dg_f1], 0)
    g["ln_ffn_b"] = jnp.concatenate([db_f0, db_f1], 0)
    g["kv_norm_g"] = dkvn_g.reshape(KV_LORA)
    g["q_norm_g"] = dqn_g
    return loss, dx, pack, g


def _place():
    x, y, c = lax.axis_index("x"), lax.axis_index("y"), lax.axis_index("c")
    return x, y, c, [(1 - x, y), (x, 1 - y), (1 - x, 1 - y)]


def place(shard, me_idx, dtype, name):
    rows, cols = shard.shape
    tr = _tile(rows, (512, 256, 128))

    def body(m_ref, x_ref, o_ref):
        o_ref[...] = x_ref[...].astype(o_ref.dtype)

    return _pcall(
        body, name=name,
        grid_spec=pltpu.PrefetchScalarGridSpec(
            num_scalar_prefetch=1, grid=(rows // tr,),
            in_specs=[pl.BlockSpec((tr, cols), lambda i, m: (i, 0))],
            out_specs=pl.BlockSpec((None, tr, cols), lambda i, m: (m[0], i, 0))),
        out_shape=jax.ShapeDtypeStruct((N_CHIPS, rows, cols), dtype),
        compiler_params=_params(("parallel",)),
    )(me_idx, shard)


def gather_stacked(arrs, name):
    n = len(arrs)

    def body(*refs):
        outs, send_sems, recv_sems = refs[n:2 * n], refs[2 * n], refs[2 * n + 1]
        x, y, c, chips = _place()
        sibling = (x, y, 1 - c)
        me = 2 * x + y

        def copy(k, blk, to):
            return pltpu.make_async_remote_copy(src_ref=blk, dst_ref=blk, send_sem=send_sems.at[k],
                                                recv_sem=recv_sems.at[k], device_id=to, device_id_type=MESH)

        started = []
        for a, o in enumerate(outs):
            for j, (px, py) in enumerate(chips):
                cp = copy(6 * a + j, o.at[me, c], (px, py, c))
                cp.start()
                started.append(cp)
        for a, o in enumerate(outs):
            for j, (px, py) in enumerate(chips):
                blk = o.at[2 * px + py, c]
                copy(6 * a + j, blk, (px, py, c)).wait_recv()
                cp = copy(6 * a + 3 + j, blk, sibling)
                cp.start()
                started.append(cp)
        for a, o in enumerate(outs):
            for j, (px, py) in enumerate(chips):
                copy(6 * a + 3 + j, o.at[2 * px + py, 1 - c], sibling).wait_recv()
        for cp in started:
            cp.wait_send()

    return _pcall(body, name=name, in_specs=[_ANY] * n, out_specs=[_ANY] * n,
                  out_shape=[jax.ShapeDtypeStruct(a.shape, a.dtype) for a in arrs],
                  input_output_aliases={i: i for i in range(n)},
                  scratch_shapes=[pltpu.SemaphoreType.DMA((6 * n,)), pltpu.SemaphoreType.DMA((6 * n,))])(*arrs)


def put_rows(pack, rows, off):
    _, n, cols = rows.shape
    tr = math.gcd(math.gcd(off, n), 512)

    def body(r_ref, p_ref, o_ref):
        o_ref[...] = r_ref[...]

    return _pcall(body, name="grad_put_rows", grid=(N_CHIPS, n // tr),
                  in_specs=[pl.BlockSpec((None, tr, cols), lambda k, i: (k, i, 0)), _ANY],
                  out_specs=pl.BlockSpec((None, tr, cols), lambda k, i: (k, off // tr + i, 0)),
                  out_shape=jax.ShapeDtypeStruct(pack.shape, pack.dtype), input_output_aliases={1: 0},
                  compiler_params=_params(("parallel", "parallel")))(rows, pack)


def _my_cols(c, mine=True):
    start = (c if mine else 1 - c) * HALF_W
    return pl.ds(pl.multiple_of(start, HALF_W), HALF_W)


def swap_halves(gpack):
    n, rows, _ = gpack.shape

    def body(g_ref, got_ref, send_sem, recv_sem):
        x, y, c, _ = _place()
        cp = pltpu.make_async_remote_copy(src_ref=g_ref.at[:, :, _my_cols(c, mine=False)], dst_ref=got_ref,
                                          send_sem=send_sem, recv_sem=recv_sem, device_id=(x, y, 1 - c),
                                          device_id_type=MESH)
        cp.start()
        cp.wait()

    return _pcall(body, name="grad_swap_halves", in_specs=[_ANY], out_specs=_ANY,
                  out_shape=jax.ShapeDtypeStruct((n, rows, HALF_W), gpack.dtype),
                  scratch_shapes=[pltpu.SemaphoreType.DMA, pltpu.SemaphoreType.DMA])(gpack)


def add_halves(gpack, got, c_idx):
    n, rows, _ = gpack.shape
    blk = (None, G_BLOCK_ROWS, HALF_W)

    def body(c_ref, g_ref, r_ref, o_ref):
        o_ref[...] = (g_ref[...] + r_ref[...]).astype(o_ref.dtype)

    return _pcall(
        body, name="grad_add_halves",
        grid_spec=pltpu.PrefetchScalarGridSpec(
            num_scalar_prefetch=1, grid=(n, rows // G_BLOCK_ROWS),
            in_specs=[pl.BlockSpec(blk, lambda k, i, c: (k, i, c[0])), pl.BlockSpec(blk, lambda k, i, c: (k, i, 0))],
            out_specs=pl.BlockSpec(blk, lambda k, i, c: (k, i, 0))),
        out_shape=jax.ShapeDtypeStruct((n, rows, HALF_W), BF16),
        compiler_params=_params(("parallel", "parallel")),
    )(c_idx, gpack, got)


def send_to_owners(part):
    _, rh, cols = part.shape

    def body(p_ref, got_ref, send_sems, recv_sems):
        x, y, c, chips = _place()
        cps = [pltpu.make_async_remote_copy(src_ref=p_ref.at[2 * px + py], dst_ref=got_ref.at[j],
                                            send_sem=send_sems.at[j], recv_sem=recv_sems.at[j],
                                            device_id=(px, py, c), device_id_type=MESH)
               for j, (px, py) in enumerate(chips)]
        for cp in cps:
            cp.start()
        for cp in cps:
            cp.wait()

    return _pcall(body, name="grad_send_to_owners", in_specs=[_ANY], out_specs=_ANY,
                  out_shape=jax.ShapeDtypeStruct((3, rh, cols), part.dtype),
                  scratch_shapes=[pltpu.SemaphoreType.DMA((3,)), pltpu.SemaphoreType.DMA((3,))])(part)


def sum_owner(part, got, idx):
    _, rows, _ = part.shape
    tr = G_BLOCK_ROWS

    def body(m_ref, p_ref, g_ref, o_ref):
        up = lambda v: v.astype(F32)
        o_ref[...] = ((up(p_ref[...]) + up(g_ref[0])) + up(g_ref[1])) + up(g_ref[2])

    return _pcall(
        body, name="grad_sum_owner",
        grid_spec=pltpu.PrefetchScalarGridSpec(
            num_scalar_prefetch=1, grid=(rows // tr,),
            in_specs=[pl.BlockSpec((None, tr, HALF_W), lambda i, m: (m[0], i, 0)),
                      pl.BlockSpec((3, tr, HALF_W), lambda i, m: (0, i, 0))],
            out_specs=pl.BlockSpec((tr, HALF_W), lambda i, m: (i, m[1]))),
        out_shape=jax.ShapeDtypeStruct((rows, PACK_W), F32),
        compiler_params=_params(("parallel",)),
    )(idx, part, got)


def join_halves(red):
    def body(in_ref, out_ref, send_sem, recv_sem):
        x, y, c, _ = _place()
        sibling = (x, y, 1 - c)
        mine = out_ref.at[:, _my_cols(c)]
        cp = pltpu.make_async_remote_copy(src_ref=mine, dst_ref=mine, send_sem=send_sem, recv_sem=recv_sem,
                                          device_id=sibling, device_id_type=MESH)
        cp.start()
        cp.wait_send()
        other = out_ref.at[:, _my_cols(c, mine=False)]
        pltpu.make_async_remote_copy(src_ref=other, dst_ref=other, send_sem=send_sem, recv_sem=recv_sem,
                                     device_id=sibling, device_id_type=MESH).wait_recv()

    return _pcall(body, name="grad_join_halves", in_specs=[_ANY], out_specs=_ANY,
                  out_shape=jax.ShapeDtypeStruct(red.shape, red.dtype), input_output_aliases={0: 0},
                  scratch_shapes=[pltpu.SemaphoreType.DMA, pltpu.SemaphoreType.DMA])(red)


def adamw(gsrc, g_off, wt, m, v, name):
    n, cols = wt.shape
    tr = math.gcd(math.gcd(g_off, n), 256) if g_off else math.gcd(n, 256)
    off_blk = g_off // tr
    c1 = 1.0 / (1.0 - ADAM_B1 ** ADAM_STEP)
    c2 = 1.0 / (1.0 - ADAM_B2 ** ADAM_STEP)

    def body(g_ref, w_ref, m_ref, v_ref, go_ref, d_ref, mo_ref, vo_ref):
        gv = g_ref[...]
        mn = ADAM_B1 * m_ref[...] + (1.0 - ADAM_B1) * gv
        vn = ADAM_B2 * v_ref[...] + (1.0 - ADAM_B2) * gv * gv
        go_ref[...] = gv
        mo_ref[...] = mn
        vo_ref[...] = vn
        d_ref[...] = -ADAM_LR * ((mn * c1) / (jnp.sqrt(vn * c2) + ADAM_EPS) + ADAM_WD * w_ref[...])

    blk = pl.BlockSpec((tr, cols), lambda i: (i, 0))
    return _pcall(body, name=name, grid=(n // tr,),
                  in_specs=[pl.BlockSpec((tr, cols), lambda i: (off_blk + i, 0)), blk, blk, blk],
                  out_specs=[blk] * 4, out_shape=[jax.ShapeDtypeStruct((n, cols), F32)] * 4,
                  compiler_params=_params(("parallel",)))(gsrc, wt, m, v)


def _rows8(a):
    return -(-a.size // (8 * PACK_W)) * 8


def _as_rows(a, rows=None):
    flat = a.reshape(-1)
    n = _rows8(a) if rows is None else rows
    return jnp.pad(flat, (0, n * PACK_W - flat.shape[0])).reshape(n, PACK_W)


def local_shards_2d(wl):
    return {"w_ff1": wl["w_ff1"].reshape(2 * D_MODEL, D_FF // N_CHIPS), "w_ff2": wl["w_ff2"].reshape(2 * D_FF // N_CHIPS, D_MODEL),
            "ssm_w_glu": wl["ssm_w_glu"], "ssm_w_out": wl["ssm_w_out"], "kv_w_a": _pad_kva_cols(wl["kv_w_a"]),
            "kv_w_b": wl["kv_w_b"], "q_w_a": wl["q_w_a"], "q_w_b": _perm_q_cols(wl["q_w_b"]),
            "attn_w_o": wl["attn_w_o"], "ssm_d": wl["ssm_d"].reshape(2, -1)}


def misc_grad_shard(name, g, k):
    if name == "ssm_d":
        w = D_MODEL // N_CHIPS
        return g[:, w * k:w * (k + 1)]
    if name in ("ssm_w_glu", "kv_w_b"):
        return g[k]
    if name == "q_w_b":
        return _unperm_q_cols(g[k])
    rows = D_MODEL // N_CHIPS
    shard = g[rows * k:rows * (k + 1)]
    return _unpad_kva_cols(shard) if name == "kv_w_a" else shard


def kernel(x, positions, ln_mix_g, ln_mix_b, ln_ffn_g, ln_ffn_b, w_ff1, w_ff2, ssm_lam_re, ssm_lam_im, ssm_log_dt, ssm_b_re, ssm_b_im, ssm_c_re, ssm_c_im, ssm_d, ssm_w_glu, ssm_w_out, kv_w_a, kv_norm_g, kv_w_b, q_w_a, q_norm_g, q_w_b, attn_w_o, loss_target, m_ln_mix_g, m_ln_mix_b, m_ln_ffn_g, m_ln_ffn_b, m_w_ff1, m_w_ff2, m_ssm_lam_re, m_ssm_lam_im, m_ssm_log_dt, m_ssm_b_re, m_ssm_b_im, m_ssm_c_re, m_ssm_c_im, m_ssm_d, m_ssm_w_glu, m_ssm_w_out, m_kv_w_a, m_kv_norm_g, m_kv_w_b, m_q_w_a, m_q_norm_g, m_q_w_b, m_attn_w_o, v_ln_mix_g, v_ln_mix_b, v_ln_ffn_g, v_ln_ffn_b, v_w_ff1, v_w_ff2, v_ssm_lam_re, v_ssm_lam_im, v_ssm_log_dt, v_ssm_b_re, v_ssm_b_im, v_ssm_c_re, v_ssm_c_im, v_ssm_d, v_ssm_w_glu, v_ssm_w_out, v_kv_w_a, v_kv_norm_g, v_kv_w_b, v_q_w_a, v_q_norm_g, v_q_w_b, v_attn_w_o):
    env = dict(locals())
    wl = {n: env[n] for n in WEIGHTS}
    ml = {n: env["m_" + n] for n in WEIGHTS}
    vl = {n: env["v_" + n] for n in WEIGHTS}
    for n in ("ssm_w_glu", "ssm_w_out", "q_w_a", "q_w_b", "attn_w_o"):
        wl[n], ml[n], vl[n] = wl[n][0], ml[n][0], vl[n][0]

    c_idx = lax.axis_index("c").astype(jnp.int32).reshape(1)
    me_idx = (2 * lax.axis_index("x") + lax.axis_index("y")).astype(jnp.int32).reshape(1)

    local = local_shards_2d(wl)
    placed = [place(local[n], me_idx, F32 if n == "ssm_d" else BF16, "place_" + n) for n in SHARDED]
    halves = [p.reshape(N_CHIPS, 2, p.shape[1] // 2, p.shape[2]) for p in placed]
    gathered = gather_stacked(halves, "weight_all_gather")
    full = {n: a.reshape(p.shape) for n, a, p in zip(SHARDED, gathered, placed)}
    for n in ("w_ff1", "w_ff2"):
        full[n] = full[n].reshape(N_CHIPS, 2, D_MODEL, D_MODEL)
    full["ssm_d"] = full["ssm_d"].reshape(1, D_MODEL)
    for n in REPLICATED:
        full[n] = wl[n]

    loss_part, dx, gpack, g = device_step(x[0], positions[0], loss_target[0], full)
    loss = lax.psum(loss_part, ("x", "y", "c"))

    small = jnp.concatenate([_as_rows(g[n]) for n in REPLICATED], axis=0)
    small = jnp.pad(small, ((0, SMALL_ROWS - small.shape[0]), (0, 0)))
    blocks = []
    for k in range(N_CHIPS):
        rows = [small[SMALL_Q_ROWS * k:SMALL_Q_ROWS * (k + 1)]]
        rows += [_as_rows(misc_grad_shard(n, g[n], k), MISC_SHARD_ROWS[n]) for n in MISC_SHARDED]
        blk = jnp.concatenate(rows, axis=0)
        blocks.append(jnp.pad(blk, ((0, MISC_ROWS - blk.shape[0]), (0, 0))))
    gpack = put_rows(gpack, jnp.stack(blocks), MISC_OFF)
    chip_part = add_halves(gpack, swap_halves(gpack), c_idx)
    reduced = join_halves(sum_owner(chip_part, send_to_owners(chip_part), jnp.concatenate([me_idx, c_idx])))
    quarter = reduced[MISC_OFF:MISC_OFF + SMALL_Q_ROWS]
    small_tot = gather_stacked([place(quarter, me_idx, F32, "place_small_grads").reshape(
        N_CHIPS, 2, SMALL_Q_ROWS // 2, PACK_W)], "small_grad_all_gather")[0].reshape(SMALL_ROWS, PACK_W)

    out_g, out_d, out_m, out_v = {}, {}, {}, {}
    for n in DIRECT_OFF:
        res = adamw(reduced, DIRECT_OFF[n], wl[n].reshape(-1, PACK_W), ml[n].reshape(-1, PACK_W),
                    vl[n].reshape(-1, PACK_W), "adamw_" + n)
        out_g[n], out_d[n], out_m[n], out_v[n] = [a.reshape(env[n].shape) for a in res]
    pack3 = lambda d: jnp.concatenate([_as_rows(d[n], MISC_SHARD_ROWS[n]) for n in MISC_SHARDED], axis=0)
    res = adamw(reduced, MISC_OFF + SMALL_Q_ROWS, pack3(wl), pack3(ml), pack3(vl), "adamw_row_packed")
    for n in MISC_SHARDED:
        cnt = math.prod(env[n].shape)
        r0 = MISC_SHARD_OFF[n] - SMALL_Q_ROWS
        out_g[n], out_d[n], out_m[n], out_v[n] = [
            a[r0:r0 + MISC_SHARD_ROWS[n]].reshape(-1)[:cnt].reshape(env[n].shape) for a in res]
    ws = jnp.concatenate([_as_rows(wl[n]) for n in REPLICATED], axis=0)
    ms = jnp.concatenate([_as_rows(ml[n]) for n in REPLICATED], axis=0)
    vs = jnp.concatenate([_as_rows(vl[n]) for n in REPLICATED], axis=0)
    pad = ((0, SMALL_ROWS - ws.shape[0]), (0, 0))
    res = adamw(small_tot, 0, jnp.pad(ws, pad), jnp.pad(ms, pad), jnp.pad(vs, pad), "adamw_replicated")
    row = 0
    for n in REPLICATED:
        cnt = math.prod(env[n].shape)
        nrows = _rows8(env[n])
        out_g[n], out_d[n], out_m[n], out_v[n] = [a[row:row + nrows].reshape(-1)[:cnt].reshape(env[n].shape) for a in res]
        row += nrows

    return (loss, dx[None], *[out_g[n] for n in WEIGHTS], *[out_d[n] for n in WEIGHTS],
            *[out_m[n] for n in WEIGHTS], *[out_v[n] for n in WEIGHTS])
```

```python
import functools
import math

import jax
import jax.numpy as jnp
from jax import lax
from jax.experimental import pallas as pl
from jax.experimental.pallas import tpu as pltpu

F32 = jnp.float32
BF16 = jnp.bfloat16
MESH = pl.DeviceIdType.MESH

D_MODEL = 1024
DEPTH = 2
SSM_GROUP = 16
N_GROUPS = D_MODEL // SSM_GROUP
SSM_STATE = 64
N_STATES = N_GROUPS * SSM_STATE
N_HEADS = 8
QK_NOPE = 128
QK_ROPE = 64
HALF_ROPE = QK_ROPE // 2
V_HEAD = 128
QK_DIM = QK_NOPE + QK_ROPE
Q_LORA = 384
KV_LORA = 256
ROPE_THETA = 10000.0
SM_SCALE = QK_DIM ** -0.5
NEG_INF = -1e30
D_FF = 4 * D_MODEL
DN_ALPHA = (2 * DEPTH) ** 0.25
LN_EPS = 1e-5
RMS_EPS = 1e-6
ADAM_LR = 0.001
ADAM_B1 = 0.9
ADAM_B2 = 0.999
ADAM_EPS = 1e-08
ADAM_WD = 0.01
ADAM_STEP = 10

N_CHIPS = 4
LANES = 128
VMEM_LIMIT = 56 * 1024 * 1024
PACK_W = 1024
KVA_PAD = 384
HALF_W = PACK_W // 2

SHARDED = ("w_ff1", "w_ff2", "ssm_w_glu", "ssm_w_out", "kv_w_a", "kv_w_b", "q_w_a", "q_w_b", "attn_w_o", "ssm_d")
G_BLOCK_ROWS = 960
EARLY_OFF = {"w_ff1": 0, "w_ff2": 2048, "attn_w_o": 4096}
MISC_EARLY = ("kv_w_b", "kv_w_a", "q_w_a", "q_w_b")
MISC_EARLY_OFF = 4352
EARLY_ROWS = 5 * G_BLOCK_ROWS
LATE_OFF = {"ssm_w_out": 0}
SMALL_Q_ROWS = 96
SMALL_ROWS = N_CHIPS * SMALL_Q_ROWS
SMALL_OFF = 256
MISC_LATE = ("ssm_d", "ssm_w_glu")
MISC_LATE_OFF = SMALL_OFF + SMALL_Q_ROWS
LATE_ROWS = G_BLOCK_ROWS
MISC_SHARD_ROWS = {"ssm_d": 16, "ssm_w_glu": 512, "kv_w_b": 128, "kv_w_a": 80, "q_w_a": 96, "q_w_b": 144}
REPLICATED = ("ln_mix_g", "ln_mix_b", "ln_ffn_g", "ln_ffn_b", "ssm_lam_re", "ssm_lam_im", "ssm_log_dt",
              "ssm_b_re", "ssm_b_im", "ssm_c_re", "ssm_c_im", "kv_norm_g", "q_norm_g")
WEIGHTS = ("ln_mix_g", "ln_mix_b", "ln_ffn_g", "ln_ffn_b", "w_ff1", "w_ff2", "ssm_lam_re", "ssm_lam_im",
           "ssm_log_dt", "ssm_b_re", "ssm_b_im", "ssm_c_re", "ssm_c_im", "ssm_d", "ssm_w_glu", "ssm_w_out",
           "kv_w_a", "kv_norm_g", "kv_w_b", "q_w_a", "q_norm_g", "q_w_b", "attn_w_o")


def _pcall(body, **kw):
    return pl.pallas_call(body, **kw)


def _params(sem=None):
    return pltpu.CompilerParams(dimension_semantics=sem, vmem_limit_bytes=VMEM_LIMIT)


_ANY = pl.BlockSpec(memory_space=pl.ANY)


def _tile(dim, prefs):
    for p in prefs:
        if dim % p == 0:
            return p
    return dim


def _place():
    x, y, c = lax.axis_index("x"), lax.axis_index("y"), lax.axis_index("c")
    return x, y, c, [(1 - x, y), (x, 1 - y), (1 - x, 1 - y)]


def _remote(k, src, dst, to, send_sems, recv_sems):
    return pltpu.make_async_remote_copy(src_ref=src, dst_ref=dst, send_sem=send_sems.at[k], recv_sem=recv_sems.at[k],
                                        device_id=to, device_id_type=MESH)


class GatherRide:
    def __init__(self, arrs):
        self.ins = list(arrs)
        self.out_shapes = [jax.ShapeDtypeStruct(a.shape, a.dtype) for a in arrs]
        self.aliases = {i: i for i in range(len(arrs))}
        self.n_sems = 6 * len(arrs)

    def start(self, ins, outs, send_sems, recv_sems):
        x, y, c, chips = _place()
        me = 2 * x + y
        for a, o in enumerate(outs):
            for j, (px, py) in enumerate(chips):
                _remote(6 * a + j, o.at[me, c], o.at[me, c], (px, py, c), send_sems, recv_sems).start()

    def finish(self, ins, outs, send_sems, recv_sems):
        x, y, c, chips = _place()
        me = 2 * x + y
        sibling = (x, y, 1 - c)
        passed = []
        for a, o in enumerate(outs):
            for j, (px, py) in enumerate(chips):
                blk = o.at[2 * px + py, c]
                _remote(6 * a + j, blk, blk, (px, py, c), send_sems, recv_sems).wait_recv()
                cp = _remote(6 * a + 3 + j, blk, blk, sibling, send_sems, recv_sems)
                cp.start()
                passed.append(cp)
        for a, o in enumerate(outs):
            for j, (px, py) in enumerate(chips):
                blk = o.at[2 * px + py, 1 - c]
                _remote(6 * a + 3 + j, blk, blk, sibling, send_sems, recv_sems).wait_recv()
                _remote(6 * a + j, o.at[me, c], o.at[me, c], (px, py, c), send_sems, recv_sems).wait_send()
        for cp in passed:
            cp.wait_send()


class SendRide:
    def __init__(self, part):
        self.ins = [part]
        self.out_shapes = [jax.ShapeDtypeStruct((3,) + part.shape[1:], part.dtype)]
        self.aliases = {}
        self.n_sems = 3

    def _copies(self, ins, outs, send_sems, recv_sems):
        x, y, c, chips = _place()
        return [_remote(j, ins[0].at[2 * px + py], outs[0].at[j], (px, py, c), send_sems, recv_sems)
                for j, (px, py) in enumerate(chips)]

    def start(self, ins, outs, send_sems, recv_sems):
        for cp in self._copies(ins, outs, send_sems, recv_sems):
            cp.start()

    def finish(self, ins, outs, send_sems, recv_sems):
        for cp in self._copies(ins, outs, send_sems, recv_sems):
            cp.wait()


def _pcall_riding(body, args, ride, first, last, *, in_specs, out_specs, out_shape, scratch_shapes=(), **kw):
    n_in, n_out = len(args), len(out_shape)
    if ride is None:
        return _pcall(body, in_specs=in_specs, out_specs=out_specs, out_shape=out_shape,
                      scratch_shapes=list(scratch_shapes), **kw)(*args), []
    k_in, k_out = len(ride.ins), len(ride.out_shapes)

    def riding(*refs):
        ins, r_in = refs[:n_in], refs[n_in:n_in + k_in]
        outs = refs[n_in + k_in:n_in + k_in + n_out]
        r_out = refs[n_in + k_in + n_out:n_in + k_in + n_out + k_out]
        scratch, (send_sems, recv_sems) = refs[n_in + k_in + n_out + k_out:-2], refs[-2:]

        @pl.when(first())
        def _():
            ride.start(r_in, r_out, send_sems, recv_sems)

        body(*ins, *outs, *scratch)

        @pl.when(last())
        def _():
            ride.finish(r_in, r_out, send_sems, recv_sems)

    res = _pcall(riding, in_specs=list(in_specs) + [_ANY] * k_in, out_specs=list(out_specs) + [_ANY] * k_out,
                 out_shape=list(out_shape) + ride.out_shapes,
                 input_output_aliases={n_in + i: n_out + o for i, o in ride.aliases.items()},
                 scratch_shapes=list(scratch_shapes) + [pltpu.SemaphoreType.DMA((ride.n_sems,))] * 2,
                 **kw)(*args, *ride.ins)
    return res[:n_out], res[n_out:]


def ride_alone(ride, name):
    def body(*refs):
        n = len(ride.ins)
        ins, outs, (send_sems, recv_sems) = refs[:n], refs[n:-2], refs[-2:]
        ride.start(ins, outs, send_sems, recv_sems)
        ride.finish(ins, outs, send_sems, recv_sems)

    return _pcall(body, name=name, in_specs=[_ANY] * len(ride.ins), out_specs=[_ANY] * len(ride.out_shapes),
                  out_shape=ride.out_shapes, input_output_aliases=dict(ride.aliases),
                  scratch_shapes=[pltpu.SemaphoreType.DMA((ride.n_sems,))] * 2)(*ride.ins)


def mm(a, b, *, name, ta=False, tb=False, pro_a=None, epi=None, extras=(), out_dtypes=(F32,), n_dim=None,
       tiles=(None, None, None), b_view=None, out_view=None, into=None):
    if ta:
        k_dim, m_dim = a.shape
    else:
        m_dim, k_dim = a.shape
    if n_dim is None:
        n_dim = b.shape[0] if tb else b.shape[1]
    tm = tiles[0] or _tile(m_dim, (1024, 512, 256, 128))
    tn = tiles[1] or (n_dim if n_dim <= 1024 else _tile(n_dim, (1024, 512, 256, 128)))
    tk = tiles[2] or (k_dim if k_dim <= 1024 else _tile(k_dim, (1024, 512, 256, 128)))
    assert m_dim % tm == 0 and n_dim % tn == 0 and k_dim % tk == 0, (name, m_dim, n_dim, k_dim, tm, tn, tk)
    nk = k_dim // tk
    n_ex, n_out = len(extras), len(out_dtypes)
    n_into = 0 if into is None else 1
    dims = (((0 if ta else 1,), (1 if tb else 0,)), ((), ()))

    def body(a_ref, b_ref, *rest):
        ex_refs, out_refs = rest[:n_ex], rest[n_ex + n_into:n_ex + n_into + n_out]

        def partial():
            av = a_ref[...]
            if pro_a is not None:
                av = pro_a(av)
            return lax.dot_general(av.astype(BF16), b_ref[...].astype(BF16), dims, preferred_element_type=F32)

        def finish(r):
            res = epi(r, *[e[...] for e in ex_refs]) if epi is not None else (r,)
            for o_ref, v in zip(out_refs, res):
                o_ref[...] = v.astype(o_ref.dtype)

        if nk == 1:
            finish(partial())
            return
        acc = rest[-1]
        k = pl.program_id(2)

        @pl.when(k == 0)
        def _():
            acc[...] = partial()

        @pl.when(k > 0)
        def _():
            acc[...] += partial()

        @pl.when(k == nk - 1)
        def _():
            finish(acc[...])

    a_spec = pl.BlockSpec((tk, tm), lambda i, j, k: (k, i)) if ta else pl.BlockSpec((tm, tk), lambda i, j, k: (i, k))
    if b_view is not None:
        b_spec = b_view(tk, tn)
    else:
        b_spec = pl.BlockSpec((tn, tk), lambda i, j, k: (j, k)) if tb else pl.BlockSpec((tk, tn), lambda i, j, k: (k, j))
    o_spec = pl.BlockSpec((tm, tn), lambda i, j, k: (i, j))
    if out_view is None:
        out_specs = [o_spec] * n_out
        out_shape = [jax.ShapeDtypeStruct((m_dim, n_dim), dt) for dt in out_dtypes]
    else:
        assert n_out == 1
        out_specs = [out_view[1](tm, tn)]
        out_shape = [jax.ShapeDtypeStruct(out_view[0], out_dtypes[0])]
    outs = _pcall(
        body, name=name, grid=(m_dim // tm, n_dim // tn, nk),
        in_specs=[a_spec, b_spec] + [o_spec] * n_ex + [_ANY] * n_into,
        out_specs=out_specs, out_shape=out_shape,
        input_output_aliases={2 + n_ex: 0} if n_into else {},
        scratch_shapes=[pltpu.VMEM((tm, tn), F32)] if nk > 1 else [],
        compiler_params=_params(("parallel", "parallel", "arbitrary")),
    )(a, b, *extras, *([into] if n_into else []))
    return outs[0] if n_out == 1 else outs


def rowwise(fn, ins, outs, *, name, accs=(), tm=256):
    rows = ins[0].shape[0]
    tm = min(tm, rows)
    n_in, n_out, n_acc = len(ins), len(outs), len(accs)

    def body(*refs):
        in_refs, out_refs, acc_refs = refs[:n_in], refs[n_in:n_in + n_out], refs[n_in + n_out:]
        res, sums = fn(*[r[...] for r in in_refs])
        for o_ref, v in zip(out_refs, res):
            o_ref[...] = v.astype(o_ref.dtype)
        if n_acc:
            @pl.when(pl.program_id(0) == 0)
            def _():
                for a_ref in acc_refs:
                    a_ref[...] = jnp.zeros_like(a_ref)

            for a_ref, s in zip(acc_refs, sums):
                a_ref[...] += s

    def spec(arr):
        if arr.shape[0] == rows:
            return pl.BlockSpec((tm, arr.shape[1]), lambda i: (i, 0))
        return pl.BlockSpec(arr.shape, lambda i: (0, 0))

    res = _pcall(
        body, name=name, grid=(rows // tm,),
        in_specs=[spec(a) for a in ins],
        out_specs=[pl.BlockSpec((tm, w), lambda i: (i, 0)) for w, _ in outs]
        + [pl.BlockSpec((1, w), lambda i: (0, 0)) for w in accs],
        out_shape=[jax.ShapeDtypeStruct((rows, w), dt) for w, dt in outs]
        + [jax.ShapeDtypeStruct((1, w), F32) for w in accs],
        compiler_params=_params(("arbitrary",) if n_acc else ("parallel",)),
    )(*ins)
    return res


def _relu2(v):
    r = jnp.maximum(v, 0.0)
    return r * r


def _gelu(x):
    c = math.sqrt(2.0 / math.pi)
    return 0.5 * x * (1.0 + jnp.tanh(c * (x + 0.044715 * x * x * x)))


def _gelu_grad(x):
    c = math.sqrt(2.0 / math.pi)
    t = jnp.tanh(c * (x + 0.044715 * x * x * x))
    return 0.5 * (1.0 + t) + 0.5 * x * (1.0 - t * t) * c * (1.0 + 3 * 0.044715 * x * x)


def _sigmoid(x):
    return 1.0 / (1.0 + jnp.exp(-x))


def ln_fwd(h, mix, g, b, name):
    def fn(h, mix, g, b):
        r = DN_ALPHA * h + mix
        mu = jnp.mean(r, axis=-1, keepdims=True)
        xc = r - mu
        var = jnp.mean(xc * xc, axis=-1, keepdims=True)
        y = xc * lax.rsqrt(var + LN_EPS) * g + b
        return (y, y), ()
    return rowwise(fn, (h, mix, g, b), ((D_MODEL, F32), (D_MODEL, BF16)), name=name)


def ln_bwd(h, mix, g, dy, name):
    def fn(h, mix, g, dy):
        r = DN_ALPHA * h + mix
        mu = jnp.mean(r, axis=-1, keepdims=True)
        xc = r - mu
        var = jnp.mean(xc * xc, axis=-1, keepdims=True)
        rstd = lax.rsqrt(var + LN_EPS)
        xhat = xc * rstd
        dxh = dy * g
        m1 = jnp.mean(dxh, axis=-1, keepdims=True)
        m2 = jnp.mean(dxh * xhat, axis=-1, keepdims=True)
        dr = rstd * (dxh - m1 - xhat * m2)
        return (dr, dr), (jnp.sum(dy * xhat, axis=0, keepdims=True), jnp.sum(dy, axis=0, keepdims=True))
    return rowwise(fn, (h, mix, g, dy), ((D_MODEL, F32), (D_MODEL, BF16)), accs=(D_MODEL, D_MODEL), name=name)


def _rms(x, g):
    r = lax.rsqrt(jnp.mean(x * x, axis=-1, keepdims=True) + RMS_EPS)
    return x * r * g


def _rms_bwd(x, g, dy):
    r = lax.rsqrt(jnp.mean(x * x, axis=-1, keepdims=True) + RMS_EPS)
    xn = x * r
    dyg = dy * g
    dx = r * (dyg - xn * jnp.mean(dyg * xn, axis=-1, keepdims=True))
    return dx, jnp.sum(dy * xn, axis=0, keepdims=True)


def _s5_disc(lr, li, ldt):
    dt = jnp.exp(ldt)
    mag = jnp.exp(lr * dt)
    cs, sn = jnp.cos(li * dt), jnp.sin(li * dt)
    ar, ai = mag * cs, mag * sn
    inv = 1.0 / (lr * lr + li * li)
    n_re = (ar - 1.0) * lr + ai * li
    n_im = ai * lr - (ar - 1.0) * li
    return dt, mag, cs, sn, ar, ai, inv, n_re, n_im


def s5_prep(lr, li, ldt, b_re, b_im):
    def fn(lr, li, ldt, b_re, b_im):
        _, _, _, _, ar, ai, inv, n_re, n_im = _s5_disc(lr, li, ldt)
        cr, ci = n_re * inv, n_im * inv
        return (ar, ai, cr * b_re - ci * b_im, cr * b_im + ci * b_re), ()
    return rowwise(fn, (lr, li, ldt, b_re, b_im), ((1, F32), (1, F32), (SSM_GROUP, F32), (SSM_GROUP, F32)),
                   name="s5_prep", tm=512)


def s5_prep_bwd(lr, li, ldt, b_re, b_im, dar, dai, dbb_re, dbb_im):
    def fn(lr, li, ldt, b_re, b_im, dar, dai, dbb_re, dbb_im):
        dt, mag, cs, sn, ar, ai, inv, n_re, n_im = _s5_disc(lr, li, ldt)
        cr, ci = n_re * inv, n_im * inv
        db_re = cr * dbb_re + ci * dbb_im
        db_im = cr * dbb_im - ci * dbb_re
        dcr = jnp.sum(dbb_re * b_re + dbb_im * b_im, axis=-1, keepdims=True)
        dci = jnp.sum(dbb_im * b_re - dbb_re * b_im, axis=-1, keepdims=True)
        dar = dar + (dcr * lr - dci * li) * inv
        dai = dai + (dcr * li + dci * lr) * inv
        dinv = dcr * n_re + dci * n_im
        dlr = (dcr * (ar - 1.0) + dci * ai) * inv - 2.0 * lr * inv * inv * dinv
        dli = (dcr * ai - dci * (ar - 1.0)) * inv - 2.0 * li * inv * inv * dinv
        dmag = dar * cs + dai * sn
        dth = dai * ar - dar * ai
        dlr = dlr + dmag * mag * dt
        dli = dli + dth * dt
        ddt = dmag * mag * lr + dth * li
        return (dlr, dli, ddt * dt, db_re, db_im), ()
    return rowwise(fn, (lr, li, ldt, b_re, b_im, dar, dai, dbb_re, dbb_im),
                   ((1, F32), (1, F32), (1, F32), (SSM_GROUP, F32), (SSM_GROUP, F32)), name="s5_prep_bwd", tm=512)


def group_sum(x):
    def body(x_ref, o_ref):
        o_ref[...] = jnp.sum(x_ref[...], axis=1)
    return _pcall(body, name="s5_group_sum", out_shape=jax.ShapeDtypeStruct((N_GROUPS, 1), F32))(
        x.reshape(N_GROUPS, SSM_STATE, 1))


GROUPS_PER_TILE = LANES // SSM_GROUP
TILE_STATES = GROUPS_PER_TILE * SSM_STATE
N_UTILES = D_MODEL // LANES
TILES_PER_UTILE = TILE_STATES // LANES


SUBLANES = 8
SCAN_STRIP = 1024
N_STRIPS = N_STATES // SCAN_STRIP
_NT = (((1,), (1,)), ((), ()))
_TN = (((0,), (0,)), ((), ()))


def _scan_coefs(are, aim, shifted, reverse):
    ar = are[...]
    ai = -aim[...] if reverse else aim[...]
    powers = {1: (ar, ai)}
    for d in (2, 4):
        r, i = powers[d // 2]
        powers[d] = (r * r - i * i, 2.0 * r * i)
    rid = lax.broadcasted_iota(jnp.int32, (SUBLANES, N_STATES), 0)
    first = (rid == SUBLANES - 1) if reverse else (rid == 0)
    masks = [(1, first)] + [(d, (rid <= SUBLANES - 1 - d) if reverse else (rid >= d)) for d in (1, 2, 4)]
    for n, (d, keep) in enumerate(masks):
        for part in (0, 1):
            shifted[2 * n + part][...] = jnp.where(keep, jnp.broadcast_to(powers[d][part], (SUBLANES, N_STATES)), 0.0)


def _tile_scan(xr, xi, shifted, nbr_re, nbr_im, reverse):
    for n, d in enumerate((1, 1, 2, 4)):
        by = SUBLANES - d if reverse else d
        fr, fi = (nbr_re, nbr_im) if n == 0 else (xr, xi)
        sr, si = pltpu.roll(fr, by, 0), pltpu.roll(fi, by, 0)
        kr, ki = shifted[2 * n], shifted[2 * n + 1]
        xr, xi = xr + kr * sr - ki * si, xi + kr * si + ki * sr
    return xr, xi


def _tile_rows(t):
    return pl.ds(pl.multiple_of(t * SUBLANES, SUBLANES), SUBLANES)


def s5_fwd(u, bbd_re, bbd_im, cbd_re, cbd_imn, a_re, a_im, dskip, ride=None, t_rows=256):
    seq = u.shape[0]
    t_rows = min(t_rows, seq)
    n_tiles = t_rows // SUBLANES

    def body(u_ref, bre, bim, cre, cimn, are, aim, d_ref, y_ref, hre_ref, him_ref, car_re, car_im, *shifted):
        @pl.when(pl.program_id(0) == 0)
        def _():
            car_re[...] = jnp.zeros_like(car_re)
            car_im[...] = jnp.zeros_like(car_im)
            _scan_coefs(are, aim, shifted, reverse=False)

        uf = u_ref[...]
        ub = uf.astype(BF16)
        for j in range(N_UTILES):
            uj = ub[:, LANES * j:LANES * (j + 1)]
            sl = slice(TILE_STATES * j, TILE_STATES * (j + 1))
            hre_ref[:, sl] = jnp.dot(uj, bre[j], preferred_element_type=F32)
            him_ref[:, sl] = jnp.dot(uj, bim[j], preferred_element_type=F32)
        for s in range(N_STRIPS):
            cols = pl.ds(s * SCAN_STRIP, SCAN_STRIP)
            coefs = [c[:, cols] for c in shifted]

            def step(t, before):
                rows = _tile_rows(t)
                hr, hi = _tile_scan(hre_ref[rows, cols], him_ref[rows, cols], coefs, before[0], before[1], False)
                hre_ref[rows, cols] = hr
                him_ref[rows, cols] = hi
                return hr, hi

            cr, ci = lax.fori_loop(0, n_tiles, step, (car_re[:, cols], car_im[:, cols]))
            car_re[:, cols] = cr
            car_im[:, cols] = ci
        dv = d_ref[...]
        for j in range(N_UTILES):
            st = slice(TILE_STATES * j, TILE_STATES * (j + 1))
            yj = (jnp.dot(hre_ref[:, st].astype(BF16), cre[j], preferred_element_type=F32)
                  + jnp.dot(him_ref[:, st].astype(BF16), cimn[j], preferred_element_type=F32))
            sl = slice(LANES * j, LANES * (j + 1))
            y_ref[:, sl] = yj + dv[:, sl] * uf[:, sl]

    full3 = lambda a: pl.BlockSpec(a.shape, lambda i: (0, 0, 0))
    full2 = lambda a: pl.BlockSpec(a.shape, lambda i: (0, 0))
    tile = pltpu.VMEM((SUBLANES, N_STATES), F32)
    n_chunks = seq // t_rows
    return _pcall_riding(
        body, (u, bbd_re, bbd_im, cbd_re, cbd_imn, a_re, a_im, dskip), ride,
        lambda: pl.program_id(0) == 0, lambda: pl.program_id(0) == n_chunks - 1,
        name="s5_fwd", grid=(n_chunks,),
        in_specs=[pl.BlockSpec((t_rows, D_MODEL), lambda i: (i, 0)), full3(bbd_re), full3(bbd_im), full3(cbd_re),
                  full3(cbd_imn), full2(a_re), full2(a_im), full2(dskip)],
        out_specs=[pl.BlockSpec((t_rows, D_MODEL), lambda i: (i, 0)),
                   pl.BlockSpec((t_rows, N_STATES), lambda i: (i, 0)),
                   pl.BlockSpec((t_rows, N_STATES), lambda i: (i, 0))],
        out_shape=[jax.ShapeDtypeStruct((seq, D_MODEL), F32),
                   jax.ShapeDtypeStruct((seq, N_STATES), F32),
                   jax.ShapeDtypeStruct((seq, N_STATES), F32)],
        scratch_shapes=[tile] * 10,
        compiler_params=_params(("arbitrary",)))


def s5_bwd(dy, u, dres, h_re, h_im, bbd_re, bbd_im, cbd_re, cbd_imn, a_re, a_im, dskip, ride=None, t_rows=128):
    seq = u.shape[0]
    t_rows = min(t_rows, seq)
    n_chunks = seq // t_rows

    n_tiles = t_rows // SUBLANES

    def body(dy_ref, u_ref, dres_ref, hre_ref, him_ref, hpre_ref, hpim_ref, bre, bim, cre, cimn, are, aim, d_ref,
             dx_ref, dbre, dbim, dcre, dcimn, dar_ref, dai_ref, dd_ref, lre, lim, car_re, car_im, acc_re, acc_im,
             *shifted):
        i = pl.program_id(0)

        @pl.when(i == 0)
        def _():
            for r in (car_re, car_im, acc_re, acc_im, dbre, dbim, dcre, dcimn, dd_ref):
                r[...] = jnp.zeros_like(r)
            _scan_coefs(are, aim, shifted, reverse=True)

        dyf = dy_ref[...]
        dyb = dyf.astype(BF16)
        uf = u_ref[...]
        ub = uf.astype(BF16)
        for j in range(N_UTILES):
            dyj = dyb[:, LANES * j:LANES * (j + 1)]
            st = slice(TILE_STATES * j, TILE_STATES * (j + 1))
            lre[:, st] = lax.dot_general(dyj, cre[j], _NT, preferred_element_type=F32)
            lim[:, st] = lax.dot_general(dyj, cimn[j], _NT, preferred_element_type=F32)
        has_pred = (i < n_chunks - 1).astype(F32)
        last_row = lax.broadcasted_iota(jnp.int32, (SUBLANES, SCAN_STRIP), 0) == SUBLANES - 1
        for s in range(N_STRIPS):
            cols = pl.ds(s * SCAN_STRIP, SCAN_STRIP)
            coefs = [c[:, cols] for c in shifted]
            before_re, before_im = hpre_ref[:, cols] * has_pred, hpim_ref[:, cols] * has_pred

            def step(k, carry):
                after_re, after_im, dar, dai = carry
                t = n_tiles - 1 - k
                rows = _tile_rows(t)
                lr, li = _tile_scan(lre[rows, cols], lim[rows, cols], coefs, after_re, after_im, True)
                lre[rows, cols] = lr
                lim[rows, cols] = li
                prev = _tile_rows(jnp.maximum(t - 1, 0))
                pre_re = jnp.where(t == 0, before_re, hre_ref[prev, cols])
                pre_im = jnp.where(t == 0, before_im, him_ref[prev, cols])
                hpr = pltpu.roll(jnp.where(last_row, pre_re, hre_ref[rows, cols]), 1, 0)
                hpi = pltpu.roll(jnp.where(last_row, pre_im, him_ref[rows, cols]), 1, 0)
                return lr, li, dar + lr * hpr + li * hpi, dai + li * hpr - lr * hpi

            cr, ci, dar, dai = lax.fori_loop(0, n_tiles, step, (car_re[:, cols], car_im[:, cols],
                                                               acc_re[:, cols], acc_im[:, cols]))
            car_re[:, cols] = cr
            car_im[:, cols] = ci
            acc_re[:, cols] = dar
            acc_im[:, cols] = dai

        dv = d_ref[...]
        for j in range(N_UTILES):
            sl = slice(LANES * j, LANES * (j + 1))
            st = slice(TILE_STATES * j, TILE_STATES * (j + 1))
            lrj = lre[:, st].astype(BF16)
            lij = lim[:, st].astype(BF16)
            du = (lax.dot_general(lrj, bre[j], _NT, preferred_element_type=F32)
                  + lax.dot_general(lij, bim[j], _NT, preferred_element_type=F32))
            dx_ref[:, sl] = du + dv[:, sl] * dyf[:, sl] + DN_ALPHA * dres_ref[:, sl]
            uj = ub[:, sl]
            dbre[j] += lax.dot_general(uj, lrj, _TN, preferred_element_type=F32)
            dbim[j] += lax.dot_general(uj, lij, _TN, preferred_element_type=F32)
            dyj = dyb[:, sl]
            dcre[j] += lax.dot_general(hre_ref[:, st].astype(BF16), dyj, _TN, preferred_element_type=F32)
            dcimn[j] += lax.dot_general(him_ref[:, st].astype(BF16), dyj, _TN, preferred_element_type=F32)
        dd_ref[...] += jnp.sum(dyf * uf, axis=0, keepdims=True)

        @pl.when(i == n_chunks - 1)
        def _():
            dar_ref[...] = jnp.sum(acc_re[...], axis=0, keepdims=True)
            dai_ref[...] = jnp.sum(acc_im[...], axis=0, keepdims=True)

    rev = lambda i: (n_chunks - 1 - i, 0)
    prev_tile = lambda i: (jnp.maximum((n_chunks - 1 - i) * n_tiles - 1, 0), 0)
    full3 = lambda a: pl.BlockSpec(a.shape, lambda i: (0, 0, 0))
    full2 = lambda a: pl.BlockSpec(a.shape, lambda i: (0, 0))
    acc3 = lambda shape: pl.BlockSpec(shape, lambda i: (0, 0, 0))
    acc2 = lambda shape: pl.BlockSpec(shape, lambda i: (0, 0))
    tile = pltpu.VMEM((SUBLANES, N_STATES), F32)
    return _pcall_riding(
        body, (dy, u, dres, h_re, h_im, h_re, h_im, bbd_re, bbd_im, cbd_re, cbd_imn, a_re, a_im, dskip), ride,
        lambda: pl.program_id(0) == 0, lambda: pl.program_id(0) == n_chunks - 1,
        name="s5_bwd", grid=(n_chunks,),
        in_specs=[pl.BlockSpec((t_rows, D_MODEL), rev), pl.BlockSpec((t_rows, D_MODEL), rev),
                  pl.BlockSpec((t_rows, D_MODEL), rev),
                  pl.BlockSpec((t_rows, N_STATES), rev), pl.BlockSpec((t_rows, N_STATES), rev),
                  pl.BlockSpec((SUBLANES, N_STATES), prev_tile), pl.BlockSpec((SUBLANES, N_STATES), prev_tile),
                  full3(bbd_re), full3(bbd_im), full3(cbd_re), full3(cbd_imn), full2(a_re), full2(a_im), full2(dskip)],
        out_specs=[pl.BlockSpec((t_rows, D_MODEL), rev), acc3(bbd_re.shape), acc3(bbd_im.shape), acc3(cbd_re.shape),
                   acc3(cbd_imn.shape), acc2((1, N_STATES)), acc2((1, N_STATES)), acc2((1, D_MODEL))],
        out_shape=[jax.ShapeDtypeStruct((seq, D_MODEL), F32), jax.ShapeDtypeStruct(bbd_re.shape, F32),
                   jax.ShapeDtypeStruct(bbd_im.shape, F32), jax.ShapeDtypeStruct(cbd_re.shape, F32),
                   jax.ShapeDtypeStruct(cbd_imn.shape, F32), jax.ShapeDtypeStruct((1, N_STATES), F32),
                   jax.ShapeDtypeStruct((1, N_STATES), F32), jax.ShapeDtypeStruct((1, D_MODEL), F32)],
        scratch_shapes=[pltpu.VMEM((t_rows, N_STATES), F32), pltpu.VMEM((t_rows, N_STATES), F32)] + [tile] * 12,
        compiler_params=_params(("arbitrary",)))


def _eye_groups():
    return jnp.eye(GROUPS_PER_TILE, dtype=F32)


def _blockdiag_in(bb):
    t = bb.transpose(0, 2, 1).reshape(N_UTILES, GROUPS_PER_TILE, SSM_GROUP, SSM_STATE)
    bd = jnp.einsum("jgcp,gh->jgchp", t, _eye_groups())
    return bd.reshape(N_UTILES, LANES, TILE_STATES)


def _blockdiag_in_t(d):
    t = jnp.einsum("jgchp,gh->jgcp", d.reshape(N_UTILES, GROUPS_PER_TILE, SSM_GROUP, GROUPS_PER_TILE, SSM_STATE),
                   _eye_groups())
    return t.reshape(N_GROUPS, SSM_GROUP, SSM_STATE).transpose(0, 2, 1)


def _blockdiag_out(c):
    t = c.transpose(0, 2, 1).reshape(N_UTILES, GROUPS_PER_TILE, SSM_STATE, SSM_GROUP)
    bd = jnp.einsum("jhpc,hg->jhpgc", t, _eye_groups())
    return bd.reshape(N_UTILES, TILE_STATES, LANES)


def _blockdiag_out_t(d):
    t = jnp.einsum("jhpgc,hg->jhpc", d.reshape(N_UTILES, GROUPS_PER_TILE, SSM_STATE, GROUPS_PER_TILE, SSM_GROUP),
                   _eye_groups())
    return t.reshape(N_GROUPS, SSM_STATE, SSM_GROUP).transpose(0, 2, 1)


ATT_TQ = 512
ATT_TK = 512
LOG2E = math.log2(math.e)
LN2 = math.log(2.0)
Q_PRESCALE = SM_SCALE * LOG2E


def _loop_in_pairs(n, step, carry, start=0):
    pairs = (n - start) // 2

    def two(t, c):
        return step(start + 2 * t + 1, step(start + 2 * t, c))

    carry = lax.fori_loop(0, pairs, two, carry)
    return lax.fori_loop(start + 2 * pairs, n, step, carry)


def _causal(s, off=0, transposed=False):
    r = lax.broadcasted_iota(jnp.int32, s.shape, 0)
    c = lax.broadcasted_iota(jnp.int32, s.shape, 1)
    keep = (r <= c + off) if transposed else (c <= r + off)
    return jnp.where(keep, s, NEG_INF)


def attn_fwd(q, k, v, ride=None, tq=ATT_TQ, tk=ATT_TK):
    n_heads, seq, _ = q.shape
    tq, tk = min(tq, seq), min(tk, seq)

    def body(q_ref, k_ref, v_ref, o_ref, lse_ref):
        qi = pl.program_id(1)
        qv = q_ref[0]
        jd = (qi * tq) // tk

        def block(j, carry, diag):
            m, l, acc = carry
            rows = pl.ds(pl.multiple_of(j * tk, tk), tk)
            s = lax.dot_general(qv, k_ref[0, rows, :], _NT, preferred_element_type=F32)
            if diag:
                s = _causal(s, qi * tq - jd * tk)
            m_new = jnp.maximum(m, jnp.max(s, axis=-1, keepdims=True))
            p = jnp.exp2(s - m_new)
            corr = jnp.exp2(m - m_new)
            l = l * corr + jnp.sum(p, axis=-1, keepdims=True)
            acc = acc * corr + jnp.dot(p.astype(BF16), v_ref[rows, :], preferred_element_type=F32)
            return m_new, l, acc

        init = (jnp.full((tq, 1), NEG_INF, F32), jnp.zeros((tq, 1), F32), jnp.zeros((tq, V_HEAD), F32))
        carry = _loop_in_pairs(jd, lambda j, c: block(j, c, False), init)
        m, l, acc = block(jd, carry, True)
        o_ref[...] = acc / l
        lse_ref[0] = jnp.broadcast_to(m + jnp.log2(l), (tq, LANES))

    n_q = seq // tq
    return _pcall_riding(
        body, (q, k, v), ride,
        lambda: (pl.program_id(0) == 0) & (pl.program_id(1) == 0),
        lambda: (pl.program_id(0) == n_heads - 1) & (pl.program_id(1) == n_q - 1),
        name="attn_fwd", grid=(n_heads, n_q),
        in_specs=[pl.BlockSpec((1, tq, QK_DIM), lambda h, i: (h, i, 0)),
                  pl.BlockSpec((1, seq, QK_DIM), lambda h, i: (h, 0, 0)),
                  pl.BlockSpec((seq, V_HEAD), lambda h, i: (0, h))],
        out_specs=[pl.BlockSpec((tq, V_HEAD), lambda h, i: (i, h)),
                   pl.BlockSpec((1, tq, LANES), lambda h, i: (h, i, 0))],
        out_shape=[jax.ShapeDtypeStruct((seq, n_heads * V_HEAD), F32),
                   jax.ShapeDtypeStruct((n_heads, seq, LANES), F32)],
        compiler_params=_params(("arbitrary", "arbitrary")))


def attn_bwd_dq(q, k, v, do, o, lse, tq=ATT_TQ, tk=ATT_TK):
    n_heads, seq, _ = q.shape
    tq, tk = min(tq, seq), min(tk, seq)

    def body(q_ref, k_ref, v_ref, do_ref, o_ref, lse_ref, dqn_ref, dqr_ref, delta_ref):
        qi = pl.program_id(1)
        qv = q_ref[0]
        dof = do_ref[...]
        dob = dof.astype(BF16)
        delta = jnp.sum(dof * o_ref[...], axis=-1, keepdims=True)
        lse = lse_ref[0][:, :1]
        jd = (qi * tq) // tk

        def block(j, dq, diag):
            rows = pl.ds(pl.multiple_of(j * tk, tk), tk)
            kv = k_ref[0, rows, :]
            s = lax.dot_general(qv, kv, _NT, preferred_element_type=F32)
            if diag:
                s = _causal(s, qi * tq - jd * tk)
            p = jnp.exp2(s - lse)
            dp = lax.dot_general(dob, v_ref[rows, :], _NT, preferred_element_type=F32)
            ds = p * (dp - delta)
            return dq + jnp.dot(ds.astype(BF16), kv, preferred_element_type=F32)

        dq = _loop_in_pairs(jd, lambda j, c: block(j, c, False), jnp.zeros((tq, QK_DIM), F32))
        dq = block(jd, dq, True) * SM_SCALE
        dqn_ref[...] = dq[:, :QK_NOPE]
        dqr_ref[0] = dq[:, QK_NOPE:]
        delta_ref[0] = jnp.broadcast_to(delta, (tq, LANES))

    return _pcall(
        body, name="attn_bwd_dq", grid=(n_heads, seq // tq),
        in_specs=[pl.BlockSpec((1, tq, QK_DIM), lambda h, i: (h, i, 0)),
                  pl.BlockSpec((1, seq, QK_DIM), lambda h, i: (h, 0, 0)),
                  pl.BlockSpec((seq, V_HEAD), lambda h, i: (0, h)),
                  pl.BlockSpec((tq, V_HEAD), lambda h, i: (i, h)),
                  pl.BlockSpec((tq, V_HEAD), lambda h, i: (i, h)),
                  pl.BlockSpec((1, tq, LANES), lambda h, i: (h, i, 0))],
        out_specs=[pl.BlockSpec((tq, QK_NOPE), lambda h, i: (i, h)),
                   pl.BlockSpec((1, tq, QK_ROPE), lambda h, i: (h, i, 0)),
                   pl.BlockSpec((1, tq, LANES), lambda h, i: (h, i, 0))],
        out_shape=[jax.ShapeDtypeStruct((seq, n_heads * QK_NOPE), F32),
                   jax.ShapeDtypeStruct((n_heads, seq, QK_ROPE), F32),
                   jax.ShapeDtypeStruct((n_heads, seq, LANES), F32)],
        compiler_params=_params(("parallel", "parallel")),
    )(q, k, v, do, o, lse)


def attn_bwd_dkv(q, k, v, do, lse_row, delta_row, tq=ATT_TK):
    n_heads, seq, _ = q.shape
    tq = min(tq, seq)
    n_blk = seq // tq

    def body(q_ref, k_ref, v_ref, do_ref, lse_ref, delta_ref, dkn_ref, dkr_ref, dv_ref):
        kj = pl.program_id(1)
        kv = k_ref[0]
        vv = v_ref[...]

        def block(i, carry, diag):
            dk, dv = carry
            rows = pl.ds(pl.multiple_of(i * tq, tq), tq)
            qv = q_ref[0, rows, :]
            st = lax.dot_general(kv, qv, _NT, preferred_element_type=F32)
            if diag:
                st = _causal(st, transposed=True)
            pt = jnp.exp2(st - lse_ref[0, pl.ds(i, 1), :])
            dob = do_ref[rows, :].astype(BF16)
            dv = dv + jnp.dot(pt.astype(BF16), dob, preferred_element_type=F32)
            dpt = lax.dot_general(vv, dob, _NT, preferred_element_type=F32)
            dst = pt * (dpt - delta_ref[0, pl.ds(i, 1), :])
            dk = dk + jnp.dot(dst.astype(BF16), qv, preferred_element_type=F32)
            return dk, dv

        carry = block(kj, (jnp.zeros((tq, QK_DIM), F32), jnp.zeros((tq, V_HEAD), F32)), True)
        dk, dv = _loop_in_pairs(n_blk, lambda i, c: block(i, c, False), carry, start=kj + 1)
        dk = dk * LN2
        dkn_ref[...] = dk[:, :QK_NOPE]
        dkr_ref[0] = dk[:, QK_NOPE:]
        dv_ref[...] = dv

    return _pcall(
        body, name="attn_bwd_dkv", grid=(n_heads, n_blk),
        in_specs=[pl.BlockSpec((1, seq, QK_DIM), lambda h, j: (h, 0, 0)),
                  pl.BlockSpec((1, tq, QK_DIM), lambda h, j: (h, j, 0)),
                  pl.BlockSpec((tq, V_HEAD), lambda h, j: (j, h)),
                  pl.BlockSpec((seq, V_HEAD), lambda h, j: (0, h)),
                  pl.BlockSpec((1, n_blk, tq), lambda h, j: (h, 0, 0)),
                  pl.BlockSpec((1, n_blk, tq), lambda h, j: (h, 0, 0))],
        out_specs=[pl.BlockSpec((tq, QK_NOPE), lambda h, j: (j, h)),
                   pl.BlockSpec((1, tq, QK_ROPE), lambda h, j: (h, j, 0)),
                   pl.BlockSpec((tq, V_HEAD), lambda h, j: (j, h))],
        out_shape=[jax.ShapeDtypeStruct((seq, n_heads * QK_NOPE), F32),
                   jax.ShapeDtypeStruct((n_heads, seq, QK_ROPE), F32),
                   jax.ShapeDtypeStruct((seq, n_heads * V_HEAD), F32)],
        compiler_params=_params(("parallel", "parallel")),
    )(q, k, v, do, lse_row, delta_row)


def head_sum(x, ts=512):
    n_heads, seq, w = x.shape
    ts = min(ts, seq)

    def body(x_ref, o_ref):
        o_ref[...] = jnp.sum(x_ref[...], axis=0)

    return _pcall(body, name="head_sum", grid=(seq // ts,),
                  in_specs=[pl.BlockSpec((n_heads, ts, w), lambda i: (0, i, 0))],
                  out_specs=pl.BlockSpec((ts, w), lambda i: (i, 0)),
                  out_shape=jax.ShapeDtypeStruct((seq, w), F32),
                  compiler_params=_params(("parallel",)))(x)


HEADS_PER_CHIP = N_HEADS // N_CHIPS
Q_CHIP = HEADS_PER_CHIP * QK_DIM
Q_CHIP_NOPE = HEADS_PER_CHIP * QK_NOPE


def _perm_q_cols(w):
    t = w.reshape(w.shape[0], HEADS_PER_CHIP, QK_DIM)
    return jnp.concatenate([t[:, :, :QK_NOPE].reshape(w.shape[0], -1),
                            t[:, :, QK_NOPE:QK_NOPE + HALF_ROPE].reshape(w.shape[0], -1),
                            t[:, :, QK_NOPE + HALF_ROPE:].reshape(w.shape[0], -1)], axis=1)


def _unperm_q_cols(w):
    r = w.shape[0]
    nope = w[:, :Q_CHIP_NOPE].reshape(r, HEADS_PER_CHIP, QK_NOPE)
    r1 = w[:, Q_CHIP_NOPE:Q_CHIP_NOPE + QK_ROPE].reshape(r, HEADS_PER_CHIP, HALF_ROPE)
    r2 = w[:, Q_CHIP_NOPE + QK_ROPE:].reshape(r, HEADS_PER_CHIP, HALF_ROPE)
    return jnp.concatenate([nope, r1, r2], axis=2).reshape(r, Q_CHIP)


def _pad_kva_cols(w):
    z = jnp.zeros((w.shape[0], HALF_ROPE), w.dtype)
    return jnp.concatenate([w[:, :KV_LORA], w[:, KV_LORA:KV_LORA + HALF_ROPE], z, w[:, KV_LORA + HALF_ROPE:], z], axis=1)


def _unpad_kva_cols(w):
    return jnp.concatenate([w[:, :KV_LORA], w[:, KV_LORA:KV_LORA + HALF_ROPE],
                            w[:, KV_LORA + QK_ROPE:KV_LORA + QK_ROPE + HALF_ROPE]], axis=1)


def _rope_tile(t, cs, sn):
    return t * cs + pltpu.roll(t, LANES // 2, 1) * sn


def _rope_tile_bwd(d, cs, sn):
    return d * cs + pltpu.roll(d * sn, LANES // 2, 1)


def _b_cols(tk, tn):
    return pl.BlockSpec((None, tk, tn), lambda i, j, k: (j, k, 0))


def _b_cols_t(tk, tn):
    return pl.BlockSpec((None, tn, tk), lambda i, j, k: (k, j, 0))


def _out_cols(shape):
    return shape, lambda tm, tn: pl.BlockSpec((None, tm, tn), lambda i, j, k: (j, i, 0))


def _halves(a):
    return a.reshape(N_CHIPS, 2, a.shape[1] // 2, a.shape[2])


def device_step(x, positions, target, w, comm=None):
    seq = x.shape[0]
    w = dict(w)

    def gathered(names, outs):
        for n, a in zip(names, outs):
            if isinstance(n, tuple):
                w[n[0]] = [a.reshape(v.shape) if l == n[1] else v for l, v in enumerate(w[n[0]])]
            else:
                w[n] = a.reshape(w[n].shape)

    def ride_for(names):
        if comm is None:
            return None
        return GatherRide([_halves(w[n[0]][n[1]] if isinstance(n, tuple) else w[n]) for n in names])

    first_ride = ("ssm_w_glu", "ssm_w_out", ("w_ff1", 0), ("w_ff2", 0), "kv_w_a", "kv_w_b", "q_w_a", "q_w_b", "attn_w_o")
    second_ride = (("w_ff1", 1), ("w_ff2", 1))

    inv_freq = ROPE_THETA ** (-jnp.arange(HALF_ROPE, dtype=F32) / HALF_ROPE)
    ang = positions.astype(F32)[:, None] * inv_freq
    cos, sin = jnp.cos(ang), jnp.sin(ang)
    zero = jnp.zeros_like(cos)
    cos_q, sin_q = jnp.concatenate([cos] * 4, 1), jnp.concatenate([-sin, -sin, sin, sin], 1)
    cos_k, sin_k = jnp.concatenate([cos, zero, cos, zero], 1), jnp.concatenate([-sin, zero, sin, zero], 1)
    ff_tile = D_FF // N_CHIPS
    pack_shape = (N_CHIPS, EARLY_ROWS, PACK_W)

    lr = w["ssm_lam_re"].reshape(N_STATES, 1)
    li = w["ssm_lam_im"].reshape(N_STATES, 1)
    ldt = jnp.repeat(w["ssm_log_dt"].reshape(N_GROUPS), SSM_STATE).reshape(N_STATES, 1)
    b_re = w["ssm_b_re"].reshape(N_STATES, SSM_GROUP)
    b_im = w["ssm_b_im"].reshape(N_STATES, SSM_GROUP)
    a_re, a_im, bb_re, bb_im = s5_prep(lr, li, ldt, b_re, b_im)
    a_re, a_im = a_re.reshape(1, N_STATES), a_im.reshape(1, N_STATES)
    bbd_re = _blockdiag_in(bb_re.reshape(N_GROUPS, SSM_STATE, SSM_GROUP)).astype(BF16)
    bbd_im = _blockdiag_in(bb_im.reshape(N_GROUPS, SSM_STATE, SSM_GROUP)).astype(BF16)
    cbd_re = _blockdiag_out(w["ssm_c_re"].reshape(N_GROUPS, SSM_GROUP, SSM_STATE)).astype(BF16)
    cbd_imn = _blockdiag_out(-w["ssm_c_im"].reshape(N_GROUPS, SSM_GROUP, SSM_STATE)).astype(BF16)
    dskip = w["ssm_d"].reshape(1, D_MODEL)
    (ypre, h_re, h_im), landed = s5_fwd(x, bbd_re, bbd_im, cbd_re, cbd_imn, a_re, a_im, dskip, ride_for(first_ride))
    gathered(first_ride, landed)
    (yg,) = rowwise(lambda y: ((_gelu(y),), ()), (ypre,), ((D_MODEL, BF16),), name="gelu")
    w_glu = w["ssm_w_glu"]
    glu_tile = w_glu.shape[2]
    vg = mm(yg, w_glu, n_dim=2 * D_MODEL, tiles=(None, glu_tile, None), b_view=_b_cols, name="glu_proj")

    def glu(v):
        return (v[:, :D_MODEL] * _sigmoid(v[:, D_MODEL:]),), ()
    (z,) = rowwise(glu, (vg,), ((D_MODEL, BF16),), name="glu")
    w_out = w["ssm_w_out"].reshape(D_MODEL, D_MODEL)
    mix0 = mm(z, w_out, name="ssm_out")

    def mlp_fwd(hb, layer):
        pre = mm(hb, w["w_ff1"][layer], n_dim=D_FF, tiles=(None, ff_tile, None), b_view=_b_cols, name=f"ff1_{layer}",
                 out_dtypes=(BF16,))
        f = mm(pre, w["w_ff2"][layer].reshape(D_FF, D_MODEL), pro_a=_relu2, name=f"ff2_{layer}")
        return pre, f

    ln = lambda name, l: w[name][l].reshape(1, D_MODEL)
    h1, h1b = ln_fwd(x, mix0, ln("ln_mix_g", 0), ln("ln_mix_b", 0), "ln_mix_0")
    f1pre, f1 = mlp_fwd(h1b, 0)
    h2, h2b = ln_fwd(h1, f1, ln("ln_ffn_g", 0), ln("ln_ffn_b", 0), "ln_ffn_0")

    kv_w_a = w["kv_w_a"].reshape(D_MODEL, KVA_PAD)
    kv_w_b = w["kv_w_b"]
    q_w_a = w["q_w_a"].reshape(D_MODEL, Q_LORA)
    q_w_b = w["q_w_b"]
    w_o = w["attn_w_o"].reshape(D_MODEL, D_MODEL)
    kvb_tile = kv_w_b.shape[2]
    kvn_g = w["kv_norm_g"].reshape(1, KV_LORA)
    qn_g = w["q_norm_g"].reshape(1, Q_LORA)
    kva = mm(h2b, kv_w_a, name="kv_a")

    def kv_post(kva, g, cs, sn):
        return (_rms(kva[:, :KV_LORA], g), _rope_tile(kva[:, KV_LORA:], cs, sn)), ()
    ckv, krope = rowwise(kv_post, (kva, kvn_g, cos_k, sin_k), ((KV_LORA, BF16), (LANES, BF16)), name="kv_post")
    kvb = mm(ckv, kv_w_b, n_dim=N_CHIPS * kvb_tile, tiles=(None, kvb_tile, KV_LORA), b_view=_b_cols, name="kv_b",
             out_dtypes=(BF16,))
    cq_raw = mm(h2b, q_w_a, name="q_a")
    (cq,) = rowwise(lambda c, g: ((_rms(c, g),), ()), (cq_raw, qn_g), ((Q_LORA, BF16),), name="q_norm")
    qlin = mm(cq, q_w_b, n_dim=N_CHIPS * Q_CHIP, tiles=(None, Q_CHIP, Q_LORA), b_view=_b_cols, name="q_b")

    def on_rope_tiles(fn, scale=None):
        def apply(q, cs, sn):
            parts = []
            for k in range(N_CHIPS):
                parts.append(q[:, Q_CHIP * k:Q_CHIP * k + Q_CHIP_NOPE])
                parts.append(fn(q[:, Q_CHIP * k + Q_CHIP_NOPE:Q_CHIP * (k + 1)], cs, sn))
            out = jnp.concatenate(parts, axis=1)
            return (out if scale is None else out * scale,), ()
        return apply
    (qro,) = rowwise(on_rope_tiles(_rope_tile, Q_PRESCALE), (qlin, cos_q, sin_q), ((N_CHIPS * Q_CHIP, BF16),),
                     name="q_rope")
    qro3 = qro.reshape(seq, N_CHIPS, Q_CHIP)
    q_h = jnp.concatenate([qro3[:, :, :Q_CHIP_NOPE].reshape(seq, N_HEADS, QK_NOPE),
                           qro3[:, :, Q_CHIP_NOPE:Q_CHIP_NOPE + QK_ROPE].reshape(seq, N_HEADS, HALF_ROPE),
                           qro3[:, :, Q_CHIP_NOPE + QK_ROPE:].reshape(seq, N_HEADS, HALF_ROPE)], axis=2).transpose(1, 0, 2)
    kvb3 = kvb.reshape(seq, N_HEADS, QK_NOPE + V_HEAD)
    kr = jnp.concatenate([krope[:, :HALF_ROPE], krope[:, QK_ROPE:QK_ROPE + HALF_ROPE]], axis=1)
    k_h = jnp.concatenate([kvb3[:, :, :QK_NOPE], jnp.broadcast_to(kr[:, None, :], (seq, N_HEADS, QK_ROPE))],
                          axis=2).transpose(1, 0, 2)
    v2 = kvb3[:, :, QK_NOPE:].reshape(seq, N_HEADS * V_HEAD)
    (o, lse), landed = attn_fwd(q_h, k_h, v2, ride_for(second_ride))
    gathered(second_ride, landed)
    mix1 = mm(o, w_o, name="attn_out")
    h3, h3b = ln_fwd(h2, mix1, ln("ln_mix_g", 1), ln("ln_mix_b", 1), "ln_mix_1")
    f2pre, f2 = mlp_fwd(h3b, 1)
    h4, _ = ln_fwd(h3, f2, ln("ln_ffn_g", 1), ln("ln_ffn_b", 1), "ln_ffn_1")

    def loss_fn(y, t):
        e = y - t
        return (e * (1.0 / D_MODEL),), (jnp.broadcast_to(jnp.sum(e * e), (1, LANES)),)
    dh4, loss_acc = rowwise(loss_fn, (h4, target), ((D_MODEL, F32),), accs=(LANES,), name="loss")
    loss = loss_acc[0, 0] * (0.5 / D_MODEL)

    g = {}

    def into_rows(off, rows_per_chip, shape=pack_shape):
        def view(tm, tn):
            nb = rows_per_chip // tm
            return pl.BlockSpec((None, tm, tn), lambda i, j, k: (i // nb, off // tm + i % nb, 0))
        return shape, view

    def into_cols(off):
        return pack_shape, lambda tm, tn: pl.BlockSpec((None, tm, tn), lambda i, j, k: (j, off // tm + i, 0))

    def mlp_bwd(pack, dr, drb, hb, pre, layer):
        dpre = mm(drb, w["w_ff2"][layer].reshape(D_FF, D_MODEL), tb=True, epi=lambda r, p: (r * 2.0 * jnp.maximum(p, 0.0),),
                  extras=(pre,), out_dtypes=(BF16,), tiles=(None, ff_tile, None), name=f"ff2_dx_{layer}")
        pack = mm(pre, drb, ta=True, pro_a=_relu2, name=f"ff2_dw_{layer}", tiles=(None, PACK_W, None), into=pack,
                  out_view=into_rows(EARLY_OFF["w_ff2"] + layer * ff_tile, ff_tile))
        pack = mm(hb, dpre, ta=True, name=f"ff1_dw_{layer}", tiles=(None, PACK_W, None), into=pack,
                  out_view=into_cols(EARLY_OFF["w_ff1"] + layer * D_MODEL))
        dh = mm(dpre, w["w_ff1"][layer], tb=True, epi=lambda r, d: (r + DN_ALPHA * d,), extras=(dr,), n_dim=D_MODEL,
                tiles=(None, D_MODEL, ff_tile), b_view=_b_cols_t, name=f"ff1_dx_{layer}")
        return pack, dh

    dr4, dr4b, dg_f1, db_f1 = ln_bwd(h3, f2, ln("ln_ffn_g", 1), dh4, "ln_ffn_bwd_1")
    pack, dh3 = mlp_bwd(None, dr4, dr4b, h3b, f2pre, 1)
    dr3, dr3b, dg_m1, db_m1 = ln_bwd(h2, mix1, ln("ln_mix_g", 1), dh3, "ln_mix_bwd_1")
    shard_rows = D_MODEL // N_CHIPS
    pack = mm(o, dr3b, ta=True, name="attn_out_dw", tiles=(shard_rows, PACK_W, None), into=pack,
              out_view=into_rows(EARLY_OFF["attn_w_o"], shard_rows))
    do = mm(dr3b, w_o, tb=True, name="attn_out_dx")
    dqn, dqr, delta = attn_bwd_dq(q_h, k_h, v2, do, o, lse)
    tb = min(ATT_TK, seq)
    lse_row = lse[:, :, 0].reshape(N_HEADS, seq // tb, tb)
    delta_row = delta[:, :, 0].reshape(N_HEADS, seq // tb, tb)
    dkn, dkr, dv = attn_bwd_dkv(q_h, k_h, v2, do, lse_row, delta_row)
    dqr_t = dqr.transpose(1, 0, 2)
    dq_cat = jnp.concatenate([dqn.reshape(seq, N_CHIPS, Q_CHIP_NOPE), dqr_t[:, :, :HALF_ROPE].reshape(seq, N_CHIPS, QK_ROPE),
                              dqr_t[:, :, HALF_ROPE:].reshape(seq, N_CHIPS, QK_ROPE)], 2).reshape(seq, N_CHIPS * Q_CHIP)
    (dqlin,) = rowwise(on_rope_tiles(_rope_tile_bwd), (dq_cat, cos_q, sin_q), ((N_CHIPS * Q_CHIP, BF16),), name="q_rope_bwd")
    g["q_w_b"] = mm(cq, dqlin, ta=True, name="q_b_dw", tiles=(Q_LORA, Q_CHIP, None), out_view=_out_cols(q_w_b.shape))
    dcq = mm(dqlin, q_w_b, tb=True, n_dim=Q_LORA, tiles=(None, Q_LORA, Q_CHIP), b_view=_b_cols_t, name="q_b_dx")

    def q_norm_bwd(c, gq, d):
        dx, dgq = _rms_bwd(c, gq, d)
        return (dx,), (dgq,)
    dcq_raw, dqn_g = rowwise(q_norm_bwd, (cq_raw, qn_g, dcq), ((Q_LORA, BF16),), accs=(Q_LORA,), name="q_norm_bwd")
    g["q_w_a"] = mm(h2b, dcq_raw, ta=True, name="q_a_dw")
    dkvb = jnp.concatenate([dkn.reshape(seq, N_HEADS, QK_NOPE), dv.reshape(seq, N_HEADS, V_HEAD)], 2).reshape(
        seq, N_HEADS * (QK_NOPE + V_HEAD)).astype(BF16)
    g["kv_w_b"] = mm(ckv, dkvb, ta=True, name="kv_b_dw", tiles=(KV_LORA, kvb_tile, None), out_view=_out_cols(kv_w_b.shape))
    dckv = mm(dkvb, kv_w_b, tb=True, n_dim=KV_LORA, tiles=(None, KV_LORA, kvb_tile), b_view=_b_cols_t, name="kv_b_dx")
    dkr_sum = head_sum(dkr)
    zpad = jnp.zeros((seq, HALF_ROPE), F32)
    dkr_tile = jnp.concatenate([dkr_sum[:, :HALF_ROPE], zpad, dkr_sum[:, HALF_ROPE:], zpad], 1)

    def kv_post_bwd(kva, gk, dc, dk, cs, sn):
        dx, dgk = _rms_bwd(kva[:, :KV_LORA], gk, dc)
        return (jnp.concatenate([dx, _rope_tile_bwd(dk, cs, sn)], axis=1),), (dgk,)
    dkva, dkvn_g = rowwise(kv_post_bwd, (kva, kvn_g, dckv, dkr_tile, cos_k, sin_k), ((KVA_PAD, BF16),),
                           accs=(KV_LORA,), name="kv_post_bwd")
    g["kv_w_a"] = mm(h2b, dkva, ta=True, name="kv_a_dw")
    dh2 = mm(dcq_raw, q_w_a, tb=True, epi=lambda r, d: (r + DN_ALPHA * d,), extras=(dr3,), name="q_a_dx")
    dh2 = mm(dkva, kv_w_a, tb=True, epi=lambda r, d: (r + d,), extras=(dh2,), name="kv_a_dx")

    dr2, dr2b, dg_f0, db_f0 = ln_bwd(h1, f1, ln("ln_ffn_g", 0), dh2, "ln_ffn_bwd_0")
    pack, dh1 = mlp_bwd(pack, dr2, dr2b, h1b, f1pre, 0)
    pack = put_rows(pack, packed_shards(g, MISC_EARLY, EARLY_ROWS - MISC_EARLY_OFF), MISC_EARLY_OFF)
    early_ride = None
    if comm is not None:
        chip_sums = add_halves(pack, swap_halves(pack), comm[1])
        early_ride = SendRide(chip_sums)
    dr1, dr1b, dg_m0, db_m0 = ln_bwd(x, mix0, ln("ln_mix_g", 0), dh1, "ln_mix_bwd_0")
    late = mm(z, dr1b, ta=True, name="ssm_out_dw", tiles=(shard_rows, PACK_W, None),
              out_view=into_rows(LATE_OFF["ssm_w_out"], shard_rows, (N_CHIPS, LATE_ROWS, PACK_W)))
    dz = mm(dr1b, w_out, tb=True, name="ssm_out_dx")

    def glu_bwd(v, dz):
        val, sg = v[:, :D_MODEL], _sigmoid(v[:, D_MODEL:])
        return (jnp.concatenate([dz * sg, dz * val * sg * (1.0 - sg)], axis=1),), ()
    (dvg,) = rowwise(glu_bwd, (vg, dz), ((2 * D_MODEL, BF16),), name="glu_bwd")
    g["ssm_w_glu"] = mm(yg, dvg, ta=True, name="glu_proj_dw", tiles=(None, glu_tile, None), out_view=_out_cols(w_glu.shape))
    dypre = mm(dvg, w_glu, tb=True, epi=lambda r, y: (r * _gelu_grad(y),), extras=(ypre,), n_dim=D_MODEL,
               tiles=(None, D_MODEL, glu_tile), b_view=_b_cols_t, name="glu_proj_dx")
    (dx, dbbd_re, dbbd_im, dcbd_re, dcbd_imn, dar, dai, dd), got_early = s5_bwd(
        dypre, x, dr1, h_re, h_im, bbd_re, bbd_im, cbd_re, cbd_imn, a_re, a_im, dskip, early_ride)
    dbb_re = _blockdiag_in_t(dbbd_re).reshape(N_STATES, SSM_GROUP)
    dbb_im = _blockdiag_in_t(dbbd_im).reshape(N_STATES, SSM_GROUP)
    dlr, dli, dldt, db_re, db_im = s5_prep_bwd(lr, li, ldt, b_re, b_im, dar.reshape(N_STATES, 1),
                                               dai.reshape(N_STATES, 1), dbb_re, dbb_im)
    g["ssm_lam_re"] = dlr.reshape(1, N_GROUPS, SSM_STATE)
    g["ssm_lam_im"] = dli.reshape(1, N_GROUPS, SSM_STATE)
    g["ssm_log_dt"] = group_sum(dldt).reshape(1, N_GROUPS)
    g["ssm_b_re"] = db_re.reshape(1, N_GROUPS, SSM_STATE, SSM_GROUP)
    g["ssm_b_im"] = db_im.reshape(1, N_GROUPS, SSM_STATE, SSM_GROUP)
    g["ssm_c_re"] = _blockdiag_out_t(dcbd_re).reshape(1, N_GROUPS, SSM_GROUP, SSM_STATE)
    g["ssm_c_im"] = -_blockdiag_out_t(dcbd_imn).reshape(1, N_GROUPS, SSM_GROUP, SSM_STATE)
    g["ssm_d"] = dd
    g["ln_mix_g"] = jnp.concatenate([dg_m0, dg_m1], 0)
    g["ln_mix_b"] = jnp.concatenate([db_m0, db_m1], 0)
    g["ln_ffn_g"] = jnp.concatenate([dg_f0, dg_f1], 0)
    g["ln_ffn_b"] = jnp.concatenate([db_f0, db_f1], 0)
    g["kv_norm_g"] = dkvn_g.reshape(KV_LORA)
    g["q_norm_g"] = dqn_g
    return loss, dx, pack, late, g, (early_ride.ins[0], got_early[0]) if comm is not None else None


def place(shard, me_idx, dtype, name):
    rows, cols = shard.shape
    tr = _tile(rows, (512, 256, 128))

    def body(m_ref, x_ref, o_ref):
        o_ref[...] = x_ref[...].astype(o_ref.dtype)

    return _pcall(
        body, name=name,
        grid_spec=pltpu.PrefetchScalarGridSpec(
            num_scalar_prefetch=1, grid=(rows // tr,),
            in_specs=[pl.BlockSpec((tr, cols), lambda i, m: (i, 0))],
            out_specs=pl.BlockSpec((None, tr, cols), lambda i, m: (m[0], i, 0))),
        out_shape=jax.ShapeDtypeStruct((N_CHIPS, rows, cols), dtype),
        compiler_params=_params(("parallel",)),
    )(me_idx, shard)


def put_rows(pack, rows, off):
    _, n, cols = rows.shape
    tr = math.gcd(math.gcd(off, n), 512)

    def body(r_ref, p_ref, o_ref):
        o_ref[...] = r_ref[...]

    return _pcall(body, name="grad_put_rows", grid=(N_CHIPS, n // tr),
                  in_specs=[pl.BlockSpec((None, tr, cols), lambda k, i: (k, i, 0)), _ANY],
                  out_specs=pl.BlockSpec((None, tr, cols), lambda k, i: (k, off // tr + i, 0)),
                  out_shape=jax.ShapeDtypeStruct(pack.shape, pack.dtype), input_output_aliases={1: 0},
                  compiler_params=_params(("parallel", "parallel")))(rows, pack)


def _my_cols(c, mine=True):
    start = (c if mine else 1 - c) * HALF_W
    return pl.ds(pl.multiple_of(start, HALF_W), HALF_W)


def swap_halves(gpack):
    n, rows, _ = gpack.shape

    def body(g_ref, got_ref, send_sem, recv_sem):
        x, y, c, _ = _place()
        cp = pltpu.make_async_remote_copy(src_ref=g_ref.at[:, :, _my_cols(c, mine=False)], dst_ref=got_ref,
                                          send_sem=send_sem, recv_sem=recv_sem, device_id=(x, y, 1 - c),
                                          device_id_type=MESH)
        cp.start()
        cp.wait()

    return _pcall(body, name="grad_swap_halves", in_specs=[_ANY], out_specs=_ANY,
                  out_shape=jax.ShapeDtypeStruct((n, rows, HALF_W), gpack.dtype),
                  scratch_shapes=[pltpu.SemaphoreType.DMA, pltpu.SemaphoreType.DMA])(gpack)


def add_halves(gpack, got, c_idx):
    n, rows, _ = gpack.shape
    blk = (None, G_BLOCK_ROWS, HALF_W)

    def body(c_ref, g_ref, r_ref, o_ref):
        o_ref[...] = (g_ref[...] + r_ref[...]).astype(o_ref.dtype)

    return _pcall(
        body, name="grad_add_halves",
        grid_spec=pltpu.PrefetchScalarGridSpec(
            num_scalar_prefetch=1, grid=(n, rows // G_BLOCK_ROWS),
            in_specs=[pl.BlockSpec(blk, lambda k, i, c: (k, i, c[0])), pl.BlockSpec(blk, lambda k, i, c: (k, i, 0))],
            out_specs=pl.BlockSpec(blk, lambda k, i, c: (k, i, 0))),
        out_shape=jax.ShapeDtypeStruct((n, rows, HALF_W), BF16),
        compiler_params=_params(("parallel", "parallel")),
    )(c_idx, gpack, got)


def sum_owner(part, got, idx, total_rows, row_off=0, into=None):
    _, rows, _ = part.shape
    tr = G_BLOCK_ROWS
    n_into = 0 if into is None else 1

    def body(m_ref, p_ref, g_ref, *rest):
        up = lambda v: v.astype(F32)
        rest[-1][...] = ((up(p_ref[...]) + up(g_ref[0])) + up(g_ref[1])) + up(g_ref[2])

    return _pcall(
        body, name="grad_sum_owner",
        grid_spec=pltpu.PrefetchScalarGridSpec(
            num_scalar_prefetch=1, grid=(rows // tr,),
            in_specs=[pl.BlockSpec((None, tr, HALF_W), lambda i, m: (m[0], i, 0)),
                      pl.BlockSpec((3, tr, HALF_W), lambda i, m: (0, i, 0))] + [_ANY] * n_into,
            out_specs=pl.BlockSpec((tr, HALF_W), lambda i, m: (row_off // tr + i, m[1]))),
        out_shape=jax.ShapeDtypeStruct((total_rows, PACK_W), F32),
        input_output_aliases={3: 0} if n_into else {},
        compiler_params=_params(("parallel",)),
    )(idx, part, got, *([into] if n_into else []))


def join_halves(red):
    def body(in_ref, out_ref, send_sem, recv_sem):
        x, y, c, _ = _place()
        sibling = (x, y, 1 - c)
        mine = out_ref.at[:, _my_cols(c)]
        cp = pltpu.make_async_remote_copy(src_ref=mine, dst_ref=mine, send_sem=send_sem, recv_sem=recv_sem,
                                          device_id=sibling, device_id_type=MESH)
        cp.start()
        cp.wait_send()
        other = out_ref.at[:, _my_cols(c, mine=False)]
        pltpu.make_async_remote_copy(src_ref=other, dst_ref=other, send_sem=send_sem, recv_sem=recv_sem,
                                     device_id=sibling, device_id_type=MESH).wait_recv()

    return _pcall(body, name="grad_join_halves", in_specs=[_ANY], out_specs=_ANY,
                  out_shape=jax.ShapeDtypeStruct(red.shape, red.dtype), input_output_aliases={0: 0},
                  scratch_shapes=[pltpu.SemaphoreType.DMA, pltpu.SemaphoreType.DMA])(red)


def adamw(gsrc, g_off, wt, m, v, name):
    n, cols = wt.shape
    tr = math.gcd(math.gcd(g_off, n), 256) if g_off else math.gcd(n, 256)
    off_blk = g_off // tr
    c1 = 1.0 / (1.0 - ADAM_B1 ** ADAM_STEP)
    c2 = 1.0 / (1.0 - ADAM_B2 ** ADAM_STEP)

    def body(g_ref, w_ref, m_ref, v_ref, go_ref, d_ref, mo_ref, vo_ref):
        gv = g_ref[...]
        mn = ADAM_B1 * m_ref[...] + (1.0 - ADAM_B1) * gv
        vn = ADAM_B2 * v_ref[...] + (1.0 - ADAM_B2) * gv * gv
        go_ref[...] = gv
        mo_ref[...] = mn
        vo_ref[...] = vn
        d_ref[...] = -ADAM_LR * ((mn * c1) / (jnp.sqrt(vn * c2) + ADAM_EPS) + ADAM_WD * w_ref[...])

    blk = pl.BlockSpec((tr, cols), lambda i: (i, 0))
    return _pcall(body, name=name, grid=(n // tr,),
                  in_specs=[pl.BlockSpec((tr, cols), lambda i: (off_blk + i, 0)), blk, blk, blk],
                  out_specs=[blk] * 4, out_shape=[jax.ShapeDtypeStruct((n, cols), F32)] * 4,
                  compiler_params=_params(("parallel",)))(gsrc, wt, m, v)


def _rows8(a):
    return -(-a.size // (8 * PACK_W)) * 8


def _as_rows(a, rows=None):
    flat = a.reshape(-1)
    n = _rows8(a) if rows is None else rows
    return jnp.pad(flat, (0, n * PACK_W - flat.shape[0])).reshape(n, PACK_W)


def local_shards_2d(wl):
    return {"w_ff1": [wl["w_ff1"][0], wl["w_ff1"][1]], "w_ff2": [wl["w_ff2"][0], wl["w_ff2"][1]],
            "ssm_w_glu": wl["ssm_w_glu"], "ssm_w_out": wl["ssm_w_out"], "kv_w_a": _pad_kva_cols(wl["kv_w_a"]),
            "kv_w_b": wl["kv_w_b"], "q_w_a": wl["q_w_a"], "q_w_b": _perm_q_cols(wl["q_w_b"]),
            "attn_w_o": wl["attn_w_o"], "ssm_d": wl["ssm_d"].reshape(2, -1)}


def misc_grad_shard(name, g, k):
    if name == "ssm_d":
        w = D_MODEL // N_CHIPS
        return g[:, w * k:w * (k + 1)]
    if name in ("ssm_w_glu", "kv_w_b"):
        return g[k]
    if name == "q_w_b":
        return _unperm_q_cols(g[k])
    rows = D_MODEL // N_CHIPS
    shard = g[rows * k:rows * (k + 1)]
    return _unpad_kva_cols(shard) if name == "kv_w_a" else shard


def packed_shards(g, names, rows, lead=None):
    blocks = []
    for k in range(N_CHIPS):
        parts = [] if lead is None else [lead[k * (lead.shape[0] // N_CHIPS):(k + 1) * (lead.shape[0] // N_CHIPS)]]
        parts += [_as_rows(misc_grad_shard(n, g[n], k), MISC_SHARD_ROWS[n]) for n in names]
        blk = jnp.concatenate(parts, axis=0)
        blocks.append(jnp.pad(blk, ((0, rows - blk.shape[0]), (0, 0))))
    return jnp.stack(blocks)


def kernel(x, positions, ln_mix_g, ln_mix_b, ln_ffn_g, ln_ffn_b, w_ff1, w_ff2, ssm_lam_re, ssm_lam_im, ssm_log_dt, ssm_b_re, ssm_b_im, ssm_c_re, ssm_c_im, ssm_d, ssm_w_glu, ssm_w_out, kv_w_a, kv_norm_g, kv_w_b, q_w_a, q_norm_g, q_w_b, attn_w_o, loss_target, m_ln_mix_g, m_ln_mix_b, m_ln_ffn_g, m_ln_ffn_b, m_w_ff1, m_w_ff2, m_ssm_lam_re, m_ssm_lam_im, m_ssm_log_dt, m_ssm_b_re, m_ssm_b_im, m_ssm_c_re, m_ssm_c_im, m_ssm_d, m_ssm_w_glu, m_ssm_w_out, m_kv_w_a, m_kv_norm_g, m_kv_w_b, m_q_w_a, m_q_norm_g, m_q_w_b, m_attn_w_o, v_ln_mix_g, v_ln_mix_b, v_ln_ffn_g, v_ln_ffn_b, v_w_ff1, v_w_ff2, v_ssm_lam_re, v_ssm_lam_im, v_ssm_log_dt, v_ssm_b_re, v_ssm_b_im, v_ssm_c_re, v_ssm_c_im, v_ssm_d, v_ssm_w_glu, v_ssm_w_out, v_kv_w_a, v_kv_norm_g, v_kv_w_b, v_q_w_a, v_q_norm_g, v_q_w_b, v_attn_w_o):
    env = dict(locals())
    wl = {n: env[n] for n in WEIGHTS}
    ml = {n: env["m_" + n] for n in WEIGHTS}
    vl = {n: env["v_" + n] for n in WEIGHTS}
    for n in ("ssm_w_glu", "ssm_w_out", "q_w_a", "q_w_b", "attn_w_o"):
        wl[n], ml[n], vl[n] = wl[n][0], ml[n][0], vl[n][0]

    c_idx = lax.axis_index("c").astype(jnp.int32).reshape(1)
    me_idx = (2 * lax.axis_index("x") + lax.axis_index("y")).astype(jnp.int32).reshape(1)

    local = local_shards_2d(wl)
    put = lambda a, n: place(a, me_idx, F32 if n == "ssm_d" else BF16, "place_" + n)
    stacked = {n: [put(a, f"{n}_{l}") for l, a in enumerate(local[n])] if isinstance(local[n], list) else put(local[n], n)
               for n in SHARDED}
    stacked["ssm_d"] = ride_alone(GatherRide([_halves(stacked["ssm_d"])]), "ssm_d_all_gather")[0].reshape(1, D_MODEL)
    for n in REPLICATED:
        stacked[n] = wl[n]

    loss_part, dx, early, late, g, (early_sums, early_got) = device_step(
        x[0], positions[0], loss_target[0], stacked, comm=(me_idx, c_idx))
    loss = lax.psum(loss_part, ("x", "y", "c"))

    small = jnp.concatenate([_as_rows(g[n]) for n in REPLICATED], axis=0)
    small = jnp.pad(small, ((0, SMALL_ROWS - small.shape[0]), (0, 0)))
    late = put_rows(late, packed_shards(g, MISC_LATE, LATE_ROWS - SMALL_OFF, lead=small), SMALL_OFF)
    late_sums = add_halves(late, swap_halves(late), c_idx)
    late_got = ride_alone(SendRide(late_sums), "grad_send_to_owners")[0]
    where = jnp.concatenate([me_idx, c_idx])
    total_rows = EARLY_ROWS + LATE_ROWS
    reduced = sum_owner(early_sums, early_got, where, total_rows)
    reduced = join_halves(sum_owner(late_sums, late_got, where, total_rows, row_off=EARLY_ROWS, into=reduced))
    quarter = reduced[EARLY_ROWS + SMALL_OFF:EARLY_ROWS + SMALL_OFF + SMALL_Q_ROWS]
    small_tot = ride_alone(GatherRide([_halves(place(quarter, me_idx, F32, "place_small_grads"))]),
                           "small_grad_all_gather")[0].reshape(SMALL_ROWS, PACK_W)

    out_g, out_d, out_m, out_v = {}, {}, {}, {}
    direct = {**EARLY_OFF, **{n: EARLY_ROWS + o for n, o in LATE_OFF.items()}}
    for n, off in direct.items():
        res = adamw(reduced, off, wl[n].reshape(-1, PACK_W), ml[n].reshape(-1, PACK_W), vl[n].reshape(-1, PACK_W),
                    "adamw_" + n)
        out_g[n], out_d[n], out_m[n], out_v[n] = [a.reshape(env[n].shape) for a in res]
    for names, off in ((MISC_EARLY, MISC_EARLY_OFF), (MISC_LATE, EARLY_ROWS + MISC_LATE_OFF)):
        pack3 = lambda d: jnp.concatenate([_as_rows(d[n], MISC_SHARD_ROWS[n]) for n in names], axis=0)
        res = adamw(reduced, off, pack3(wl), pack3(ml), pack3(vl), "adamw_packed_" + names[0])
        r0 = 0
        for n in names:
            cnt = math.prod(env[n].shape)
            out_g[n], out_d[n], out_m[n], out_v[n] = [
                a[r0:r0 + MISC_SHARD_ROWS[n]].reshape(-1)[:cnt].reshape(env[n].shape) for a in res]
            r0 += MISC_SHARD_ROWS[n]
    ws = jnp.concatenate([_as_rows(wl[n]) for n in REPLICATED], axis=0)
    ms = jnp.concatenate([_as_rows(ml[n]) for n in REPLICATED], axis=0)
    vs = jnp.concatenate([_as_rows(vl[n]) for n in REPLICATED], axis=0)
    pad = ((0, SMALL_ROWS - ws.shape[0]), (0, 0))
    res = adamw(small_tot, 0, jnp.pad(ws, pad), jnp.pad(ms, pad), jnp.pad(vs, pad), "adamw_replicated")
    row = 0
    for n in REPLICATED:
        cnt = math.prod(env[n].shape)
        nrows = _rows8(env[n])
        out_g[n], out_d[n], out_m[n], out_v[n] = [a[row:row + nrows].reshape(-1)[:cnt].reshape(env[n].shape) for a in res]
        row += nrows

    return (loss, dx[None], *[out_g[n] for n in WEIGHTS], *[out_d[n] for n in WEIGHTS],
            *[out_m[n] for n in WEIGHTS], *[out_v[n] for n in WEIGHTS])
```

```python
import functools
import math

import jax
import jax.numpy as jnp
from jax import lax
from jax.experimental import pallas as pl
from jax.experimental.pallas import tpu as pltpu

F32 = jnp.float32
BF16 = jnp.bfloat16
MESH = pl.DeviceIdType.MESH

D_MODEL = 1024
DEPTH = 2
SSM_GROUP = 16
N_GROUPS = D_MODEL // SSM_GROUP
SSM_STATE = 64
N_STATES = N_GROUPS * SSM_STATE
N_HEADS = 8
QK_NOPE = 128
QK_ROPE = 64
HALF_ROPE = QK_ROPE // 2
V_HEAD = 128
QK_DIM = QK_NOPE + QK_ROPE
Q_LORA = 384
KV_LORA = 256
ROPE_THETA = 10000.0
SM_SCALE = QK_DIM ** -0.5
NEG_INF = -1e30
D_FF = 4 * D_MODEL
DN_ALPHA = (2 * DEPTH) ** 0.25
LN_EPS = 1e-5
RMS_EPS = 1e-6
ADAM_LR = 0.001
ADAM_B1 = 0.9
ADAM_B2 = 0.999
ADAM_EPS = 1e-08
ADAM_WD = 0.01
ADAM_STEP = 10

N_CHIPS = 4
LANES = 128
VMEM_LIMIT = 56 * 1024 * 1024
PACK_W = 1024
KVA_PAD = 384
HALF_W = PACK_W // 2

SHARDED = ("w_ff1", "w_ff2", "ssm_w_glu", "ssm_w_out", "kv_w_a", "kv_w_b", "q_w_a", "q_w_b", "attn_w_o", "ssm_d")
G_BLOCK_ROWS = 960
EARLY_OFF = {"w_ff1": 0, "w_ff2": 2048, "attn_w_o": 4096}
MISC_EARLY = ("kv_w_b", "kv_w_a", "q_w_a", "q_w_b")
MISC_EARLY_OFF = 4352
EARLY_ROWS = 5 * G_BLOCK_ROWS
LATE_OFF = {"ssm_w_out": 0}
SMALL_Q_ROWS = 96
SMALL_ROWS = N_CHIPS * SMALL_Q_ROWS
SMALL_OFF = 256
MISC_LATE = ("ssm_d", "ssm_w_glu")
MISC_LATE_OFF = SMALL_OFF + SMALL_Q_ROWS
LATE_ROWS = G_BLOCK_ROWS
MISC_SHARD_ROWS = {"ssm_d": 16, "ssm_w_glu": 512, "kv_w_b": 128, "kv_w_a": 80, "q_w_a": 96, "q_w_b": 144}
REPLICATED = ("ln_mix_g", "ln_mix_b", "ln_ffn_g", "ln_ffn_b", "ssm_lam_re", "ssm_lam_im", "ssm_log_dt",
              "ssm_b_re", "ssm_b_im", "ssm_c_re", "ssm_c_im", "kv_norm_g", "q_norm_g")
WEIGHTS = ("ln_mix_g", "ln_mix_b", "ln_ffn_g", "ln_ffn_b", "w_ff1", "w_ff2", "ssm_lam_re", "ssm_lam_im",
           "ssm_log_dt", "ssm_b_re", "ssm_b_im", "ssm_c_re", "ssm_c_im", "ssm_d", "ssm_w_glu", "ssm_w_out",
           "kv_w_a", "kv_norm_g", "kv_w_b", "q_w_a", "q_norm_g", "q_w_b", "attn_w_o")


def _pcall(body, **kw):
    return pl.pallas_call(body, **kw)


def _params(sem=None):
    return pltpu.CompilerParams(dimension_semantics=sem, vmem_limit_bytes=VMEM_LIMIT)


_ANY = pl.BlockSpec(memory_space=pl.ANY)


def _tile(dim, prefs):
    for p in prefs:
        if dim % p == 0:
            return p
    return dim


def _place():
    x, y, c = lax.axis_index("x"), lax.axis_index("y"), lax.axis_index("c")
    return x, y, c, [(1 - x, y), (x, 1 - y), (1 - x, 1 - y)]


def _remote(k, src, dst, to, send_sems, recv_sems):
    return pltpu.make_async_remote_copy(src_ref=src, dst_ref=dst, send_sem=send_sems.at[k], recv_sem=recv_sems.at[k],
                                        device_id=to, device_id_type=MESH)


class GatherRide:
    def __init__(self, arrs):
        self.ins = list(arrs)
        self.out_shapes = [jax.ShapeDtypeStruct(a.shape, a.dtype) for a in arrs]
        self.aliases = {i: i for i in range(len(arrs))}
        self.n_sems = 6 * len(arrs)

    def start(self, ins, outs, send_sems, recv_sems):
        x, y, c, chips = _place()
        me = 2 * x + y
        for a, o in enumerate(outs):
            for j, (px, py) in enumerate(chips):
                _remote(6 * a + j, o.at[me, c], o.at[me, c], (px, py, c), send_sems, recv_sems).start()

    def finish(self, ins, outs, send_sems, recv_sems):
        x, y, c, chips = _place()
        me = 2 * x + y
        sibling = (x, y, 1 - c)
        passed = []
        for a, o in enumerate(outs):
            for j, (px, py) in enumerate(chips):
                blk = o.at[2 * px + py, c]
                _remote(6 * a + j, blk, blk, (px, py, c), send_sems, recv_sems).wait_recv()
                cp = _remote(6 * a + 3 + j, blk, blk, sibling, send_sems, recv_sems)
                cp.start()
                passed.append(cp)
        for a, o in enumerate(outs):
            for j, (px, py) in enumerate(chips):
                blk = o.at[2 * px + py, 1 - c]
                _remote(6 * a + 3 + j, blk, blk, sibling, send_sems, recv_sems).wait_recv()
                _remote(6 * a + j, o.at[me, c], o.at[me, c], (px, py, c), send_sems, recv_sems).wait_send()
        for cp in passed:
            cp.wait_send()


class SendRide:
    def __init__(self, part):
        self.ins = [part]
        self.out_shapes = [jax.ShapeDtypeStruct((3,) + part.shape[1:], part.dtype)]
        self.aliases = {}
        self.n_sems = 3

    def _copies(self, ins, outs, send_sems, recv_sems):
        x, y, c, chips = _place()
        return [_remote(j, ins[0].at[2 * px + py], outs[0].at[j], (px, py, c), send_sems, recv_sems)
                for j, (px, py) in enumerate(chips)]

    def start(self, ins, outs, send_sems, recv_sems):
        for cp in self._copies(ins, outs, send_sems, recv_sems):
            cp.start()

    def finish(self, ins, outs, send_sems, recv_sems):
        for cp in self._copies(ins, outs, send_sems, recv_sems):
            cp.wait()


def _pcall_riding(body, args, ride, first, last, *, in_specs, out_specs, out_shape, scratch_shapes=(), **kw):
    n_in, n_out = len(args), len(out_shape)
    if ride is None:
        return _pcall(body, in_specs=in_specs, out_specs=out_specs, out_shape=out_shape,
                      scratch_shapes=list(scratch_shapes), **kw)(*args), []
    k_in, k_out = len(ride.ins), len(ride.out_shapes)

    def riding(*refs):
        ins, r_in = refs[:n_in], refs[n_in:n_in + k_in]
        outs = refs[n_in + k_in:n_in + k_in + n_out]
        r_out = refs[n_in + k_in + n_out:n_in + k_in + n_out + k_out]
        scratch, (send_sems, recv_sems) = refs[n_in + k_in + n_out + k_out:-2], refs[-2:]

        @pl.when(first())
        def _():
            ride.start(r_in, r_out, send_sems, recv_sems)

        body(*ins, *outs, *scratch)

        @pl.when(last())
        def _():
            ride.finish(r_in, r_out, send_sems, recv_sems)

    res = _pcall(riding, in_specs=list(in_specs) + [_ANY] * k_in, out_specs=list(out_specs) + [_ANY] * k_out,
                 out_shape=list(out_shape) + ride.out_shapes,
                 input_output_aliases={n_in + i: n_out + o for i, o in ride.aliases.items()},
                 scratch_shapes=list(scratch_shapes) + [pltpu.SemaphoreType.DMA((ride.n_sems,))] * 2,
                 **kw)(*args, *ride.ins)
    return res[:n_out], res[n_out:]


def ride_alone(ride, name):
    def body(*refs):
        n = len(ride.ins)
        ins, outs, (send_sems, recv_sems) = refs[:n], refs[n:-2], refs[-2:]
        ride.start(ins, outs, send_sems, recv_sems)
        ride.finish(ins, outs, send_sems, recv_sems)

    return _pcall(body, name=name, in_specs=[_ANY] * len(ride.ins), out_specs=[_ANY] * len(ride.out_shapes),
                  out_shape=ride.out_shapes, input_output_aliases=dict(ride.aliases),
                  scratch_shapes=[pltpu.SemaphoreType.DMA((ride.n_sems,))] * 2)(*ride.ins)


def mm(a, b, *, name, ta=False, tb=False, pro_a=None, epi=None, extras=(), out_dtypes=(F32,), n_dim=None,
       tiles=(None, None, None), b_view=None, out_view=None, into=None):
    if ta:
        k_dim, m_dim = a.shape
    else:
        m_dim, k_dim = a.shape
    if n_dim is None:
        n_dim = b.shape[0] if tb else b.shape[1]
    tm = tiles[0] or _tile(m_dim, (1024, 512, 256, 128))
    tn = tiles[1] or (n_dim if n_dim <= 1024 else _tile(n_dim, (1024, 512, 256, 128)))
    tk = tiles[2] or (k_dim if k_dim <= 1024 else _tile(k_dim, (1024, 512, 256, 128)))
    assert m_dim % tm == 0 and n_dim % tn == 0 and k_dim % tk == 0, (name, m_dim, n_dim, k_dim, tm, tn, tk)
    nk = k_dim // tk
    n_ex, n_out = len(extras), len(out_dtypes)
    n_into = 0 if into is None else 1
    dims = (((0 if ta else 1,), (1 if tb else 0,)), ((), ()))

    def body(a_ref, b_ref, *rest):
        ex_refs, out_refs = rest[:n_ex], rest[n_ex + n_into:n_ex + n_into + n_out]

        def partial():
            av = a_ref[...]
            if pro_a is not None:
                av = pro_a(av)
            return lax.dot_general(av.astype(BF16), b_ref[...].astype(BF16), dims, preferred_element_type=F32)

        def finish(r):
            res = epi(r, *[e[...] for e in ex_refs]) if epi is not None else (r,)
            for o_ref, v in zip(out_refs, res):
                o_ref[...] = v.astype(o_ref.dtype)

        if nk == 1:
            finish(partial())
            return
        acc = rest[-1]
        k = pl.program_id(2)

        @pl.when(k == 0)
        def _():
            acc[...] = partial()

        @pl.when(k > 0)
        def _():
            acc[...] += partial()

        @pl.when(k == nk - 1)
        def _():
            finish(acc[...])

    a_spec = pl.BlockSpec((tk, tm), lambda i, j, k: (k, i)) if ta else pl.BlockSpec((tm, tk), lambda i, j, k: (i, k))
    if b_view is not None:
        b_spec = b_view(tk, tn)
    else:
        b_spec = pl.BlockSpec((tn, tk), lambda i, j, k: (j, k)) if tb else pl.BlockSpec((tk, tn), lambda i, j, k: (k, j))
    o_spec = pl.BlockSpec((tm, tn), lambda i, j, k: (i, j))
    if out_view is None:
        out_specs = [o_spec] * n_out
        out_shape = [jax.ShapeDtypeStruct((m_dim, n_dim), dt) for dt in out_dtypes]
    else:
        assert n_out == 1
        out_specs = [out_view[1](tm, tn)]
        out_shape = [jax.ShapeDtypeStruct(out_view[0], out_dtypes[0])]
    outs = _pcall(
        body, name=name, grid=(m_dim // tm, n_dim // tn, nk),
        in_specs=[a_spec, b_spec] + [o_spec] * n_ex + [_ANY] * n_into,
        out_specs=out_specs, out_shape=out_shape,
        input_output_aliases={2 + n_ex: 0} if n_into else {},
        scratch_shapes=[pltpu.VMEM((tm, tn), F32)] if nk > 1 else [],
        compiler_params=_params(("parallel", "parallel", "arbitrary")),
    )(a, b, *extras, *([into] if n_into else []))
    return outs[0] if n_out == 1 else outs


def rowwise(fn, ins, outs, *, name, accs=(), tm=256):
    rows = ins[0].shape[0]
    tm = min(tm, rows)
    n_in, n_out, n_acc = len(ins), len(outs), len(accs)

    def body(*refs):
        in_refs, out_refs, acc_refs = refs[:n_in], refs[n_in:n_in + n_out], refs[n_in + n_out:]
        res, sums = fn(*[r[...] for r in in_refs])
        for o_ref, v in zip(out_refs, res):
            o_ref[...] = v.astype(o_ref.dtype)
        if n_acc:
            @pl.when(pl.program_id(0) == 0)
            def _():
                for a_ref in acc_refs:
                    a_ref[...] = jnp.zeros_like(a_ref)

            for a_ref, s in zip(acc_refs, sums):
                a_ref[...] += s

    def spec(arr):
        if arr.shape[0] == rows:
            return pl.BlockSpec((tm, arr.shape[1]), lambda i: (i, 0))
        return pl.BlockSpec(arr.shape, lambda i: (0, 0))

    res = _pcall(
        body, name=name, grid=(rows // tm,),
        in_specs=[spec(a) for a in ins],
        out_specs=[pl.BlockSpec((tm, w), lambda i: (i, 0)) for w, _ in outs]
        + [pl.BlockSpec((1, w), lambda i: (0, 0)) for w in accs],
        out_shape=[jax.ShapeDtypeStruct((rows, w), dt) for w, dt in outs]
        + [jax.ShapeDtypeStruct((1, w), F32) for w in accs],
        compiler_params=_params(("arbitrary",) if n_acc else ("parallel",)),
    )(*ins)
    return res


def _relu2(v):
    r = jnp.maximum(v, 0.0)
    return r * r


def _gelu(x):
    c = math.sqrt(2.0 / math.pi)
    return 0.5 * x * (1.0 + jnp.tanh(c * (x + 0.044715 * x * x * x)))


def _gelu_grad(x):
    c = math.sqrt(2.0 / math.pi)
    t = jnp.tanh(c * (x + 0.044715 * x * x * x))
    return 0.5 * (1.0 + t) + 0.5 * x * (1.0 - t * t) * c * (1.0 + 3 * 0.044715 * x * x)


def _sigmoid(x):
    return 1.0 / (1.0 + jnp.exp(-x))


def ln_fwd(h, mix, g, b, name):
    def fn(h, mix, g, b):
        r = DN_ALPHA * h + mix
        mu = jnp.mean(r, axis=-1, keepdims=True)
        xc = r - mu
        var = jnp.mean(xc * xc, axis=-1, keepdims=True)
        y = xc * lax.rsqrt(var + LN_EPS) * g + b
        return (y, y), ()
    return rowwise(fn, (h, mix, g, b), ((D_MODEL, F32), (D_MODEL, BF16)), name=name)


def ln_bwd(h, mix, g, dy, name):
    def fn(h, mix, g, dy):
        r = DN_ALPHA * h + mix
        mu = jnp.mean(r, axis=-1, keepdims=True)
        xc = r - mu
        var = jnp.mean(xc * xc, axis=-1, keepdims=True)
        rstd = lax.rsqrt(var + LN_EPS)
        xhat = xc * rstd
        dxh = dy * g
        m1 = jnp.mean(dxh, axis=-1, keepdims=True)
        m2 = jnp.mean(dxh * xhat, axis=-1, keepdims=True)
        dr = rstd * (dxh - m1 - xhat * m2)
        return (dr, dr), (jnp.sum(dy * xhat, axis=0, keepdims=True), jnp.sum(dy, axis=0, keepdims=True))
    return rowwise(fn, (h, mix, g, dy), ((D_MODEL, F32), (D_MODEL, BF16)), accs=(D_MODEL, D_MODEL), name=name)


def _rms(x, g):
    r = lax.rsqrt(jnp.mean(x * x, axis=-1, keepdims=True) + RMS_EPS)
    return x * r * g


def _rms_bwd(x, g, dy):
    r = lax.rsqrt(jnp.mean(x * x, axis=-1, keepdims=True) + RMS_EPS)
    xn = x * r
    dyg = dy * g
    dx = r * (dyg - xn * jnp.mean(dyg * xn, axis=-1, keepdims=True))
    return dx, jnp.sum(dy * xn, axis=0, keepdims=True)


def _s5_disc(lr, li, ldt):
    dt = jnp.exp(ldt)
    mag = jnp.exp(lr * dt)
    cs, sn = jnp.cos(li * dt), jnp.sin(li * dt)
    ar, ai = mag * cs, mag * sn
    inv = 1.0 / (lr * lr + li * li)
    n_re = (ar - 1.0) * lr + ai * li
    n_im = ai * lr - (ar - 1.0) * li
    return dt, mag, cs, sn, ar, ai, inv, n_re, n_im


def s5_prep(lr, li, ldt, b_re, b_im):
    def fn(lr, li, ldt, b_re, b_im):
        _, _, _, _, ar, ai, inv, n_re, n_im = _s5_disc(lr, li, ldt)
        cr, ci = n_re * inv, n_im * inv
        return (ar, ai, cr * b_re - ci * b_im, cr * b_im + ci * b_re), ()
    return rowwise(fn, (lr, li, ldt, b_re, b_im), ((1, F32), (1, F32), (SSM_GROUP, F32), (SSM_GROUP, F32)),
                   name="s5_prep", tm=512)


def s5_prep_bwd(lr, li, ldt, b_re, b_im, dar, dai, dbb_re, dbb_im):
    def fn(lr, li, ldt, b_re, b_im, dar, dai, dbb_re, dbb_im):
        dt, mag, cs, sn, ar, ai, inv, n_re, n_im = _s5_disc(lr, li, ldt)
        cr, ci = n_re * inv, n_im * inv
        db_re = cr * dbb_re + ci * dbb_im
        db_im = cr * dbb_im - ci * dbb_re
        dcr = jnp.sum(dbb_re * b_re + dbb_im * b_im, axis=-1, keepdims=True)
        dci = jnp.sum(dbb_im * b_re - dbb_re * b_im, axis=-1, keepdims=True)
        dar = dar + (dcr * lr - dci * li) * inv
        dai = dai + (dcr * li + dci * lr) * inv
        dinv = dcr * n_re + dci * n_im
        dlr = (dcr * (ar - 1.0) + dci * ai) * inv - 2.0 * lr * inv * inv * dinv
        dli = (dcr * ai - dci * (ar - 1.0)) * inv - 2.0 * li * inv * inv * dinv
        dmag = dar * cs + dai * sn
        dth = dai * ar - dar * ai
        dlr = dlr + dmag * mag * dt
        dli = dli + dth * dt
        ddt = dmag * mag * lr + dth * li
        return (dlr, dli, ddt * dt, db_re, db_im), ()
    return rowwise(fn, (lr, li, ldt, b_re, b_im, dar, dai, dbb_re, dbb_im),
                   ((1, F32), (1, F32), (1, F32), (SSM_GROUP, F32), (SSM_GROUP, F32)), name="s5_prep_bwd", tm=512)


def group_sum(x):
    def body(x_ref, o_ref):
        o_ref[...] = jnp.sum(x_ref[...], axis=1)
    return _pcall(body, name="s5_group_sum", out_shape=jax.ShapeDtypeStruct((N_GROUPS, 1), F32))(
        x.reshape(N_GROUPS, SSM_STATE, 1))


GROUPS_PER_TILE = LANES // SSM_GROUP
TILE_STATES = GROUPS_PER_TILE * SSM_STATE
N_UTILES = D_MODEL // LANES
TILES_PER_UTILE = TILE_STATES // LANES


SUBLANES = 8
SCAN_STRIP = 1024
N_STRIPS = N_STATES // SCAN_STRIP
_NT = (((1,), (1,)), ((), ()))
_TN = (((0,), (0,)), ((), ()))


def _scan_coefs(are, aim, shifted, reverse):
    ar = are[...]
    ai = -aim[...] if reverse else aim[...]
    powers = {1: (ar, ai)}
    for d in (2, 4):
        r, i = powers[d // 2]
        powers[d] = (r * r - i * i, 2.0 * r * i)
    rid = lax.broadcasted_iota(jnp.int32, (SUBLANES, N_STATES), 0)
    first = (rid == SUBLANES - 1) if reverse else (rid == 0)
    masks = [(1, first)] + [(d, (rid <= SUBLANES - 1 - d) if reverse else (rid >= d)) for d in (1, 2, 4)]
    for n, (d, keep) in enumerate(masks):
        for part in (0, 1):
            shifted[2 * n + part][...] = jnp.where(keep, jnp.broadcast_to(powers[d][part], (SUBLANES, N_STATES)), 0.0)


def _tile_scan(xr, xi, shifted, nbr_re, nbr_im, reverse):
    for n, d in enumerate((1, 1, 2, 4)):
        by = SUBLANES - d if reverse else d
        fr, fi = (nbr_re, nbr_im) if n == 0 else (xr, xi)
        sr, si = pltpu.roll(fr, by, 0), pltpu.roll(fi, by, 0)
        kr, ki = shifted[2 * n], shifted[2 * n + 1]
        xr, xi = xr + kr * sr - ki * si, xi + kr * si + ki * sr
    return xr, xi


def _tile_rows(t):
    return pl.ds(pl.multiple_of(t * SUBLANES, SUBLANES), SUBLANES)


def s5_fwd(u, bbd_re, bbd_im, cbd_re, cbd_imn, a_re, a_im, dskip, ride=None, t_rows=256):
    seq = u.shape[0]
    t_rows = min(t_rows, seq)
    n_tiles = t_rows // SUBLANES

    def body(u_ref, bre, bim, cre, cimn, are, aim, d_ref, y_ref, hre_ref, him_ref, car_re, car_im, *shifted):
        @pl.when(pl.program_id(0) == 0)
        def _():
            car_re[...] = jnp.zeros_like(car_re)
            car_im[...] = jnp.zeros_like(car_im)
            _scan_coefs(are, aim, shifted, reverse=False)

        uf = u_ref[...]
        ub = uf.astype(BF16)
        for j in range(N_UTILES):
            uj = ub[:, LANES * j:LANES * (j + 1)]
            sl = slice(TILE_STATES * j, TILE_STATES * (j + 1))
            hre_ref[:, sl] = jnp.dot(uj, bre[j], preferred_element_type=F32)
            him_ref[:, sl] = jnp.dot(uj, bim[j], preferred_element_type=F32)
        for s in range(N_STRIPS):
            cols = pl.ds(s * SCAN_STRIP, SCAN_STRIP)
            coefs = [c[:, cols] for c in shifted]

            def step(t, before):
                rows = _tile_rows(t)
                hr, hi = _tile_scan(hre_ref[rows, cols], him_ref[rows, cols], coefs, before[0], before[1], False)
                hre_ref[rows, cols] = hr
                him_ref[rows, cols] = hi
                return hr, hi

            cr, ci = lax.fori_loop(0, n_tiles, step, (car_re[:, cols], car_im[:, cols]))
            car_re[:, cols] = cr
            car_im[:, cols] = ci
        dv = d_ref[...]
        for j in range(N_UTILES):
            st = slice(TILE_STATES * j, TILE_STATES * (j + 1))
            yj = (jnp.dot(hre_ref[:, st].astype(BF16), cre[j], preferred_element_type=F32)
                  + jnp.dot(him_ref[:, st].astype(BF16), cimn[j], preferred_element_type=F32))
            sl = slice(LANES * j, LANES * (j + 1))
            y_ref[:, sl] = yj + dv[:, sl] * uf[:, sl]

    full3 = lambda a: pl.BlockSpec(a.shape, lambda i: (0, 0, 0))
    full2 = lambda a: pl.BlockSpec(a.shape, lambda i: (0, 0))
    tile = pltpu.VMEM((SUBLANES, N_STATES), F32)
    n_chunks = seq // t_rows
    return _pcall_riding(
        body, (u, bbd_re, bbd_im, cbd_re, cbd_imn, a_re, a_im, dskip), ride,
        lambda: pl.program_id(0) == 0, lambda: pl.program_id(0) == n_chunks - 1,
        name="s5_fwd", grid=(n_chunks,),
        in_specs=[pl.BlockSpec((t_rows, D_MODEL), lambda i: (i, 0)), full3(bbd_re), full3(bbd_im), full3(cbd_re),
                  full3(cbd_imn), full2(a_re), full2(a_im), full2(dskip)],
        out_specs=[pl.BlockSpec((t_rows, D_MODEL), lambda i: (i, 0)),
                   pl.BlockSpec((t_rows, N_STATES), lambda i: (i, 0)),
                   pl.BlockSpec((t_rows, N_STATES), lambda i: (i, 0))],
        out_shape=[jax.ShapeDtypeStruct((seq, D_MODEL), F32),
                   jax.ShapeDtypeStruct((seq, N_STATES), F32),
                   jax.ShapeDtypeStruct((seq, N_STATES), F32)],
        scratch_shapes=[tile] * 10,
        compiler_params=_params(("arbitrary",)))


def s5_bwd(dy, u, dres, h_re, h_im, bbd_re, bbd_im, cbd_re, cbd_imn, a_re, a_im, dskip, ride=None, t_rows=128):
    seq = u.shape[0]
    t_rows = min(t_rows, seq)
    n_chunks = seq // t_rows

    n_tiles = t_rows // SUBLANES

    def body(dy_ref, u_ref, dres_ref, hre_ref, him_ref, hpre_ref, hpim_ref, bre, bim, cre, cimn, are, aim, d_ref,
             dx_ref, dbre, dbim, dcre, dcimn, dar_ref, dai_ref, dd_ref, lre, lim, car_re, car_im, acc_re, acc_im,
             *shifted):
        i = pl.program_id(0)

        @pl.when(i == 0)
        def _():
            for r in (car_re, car_im, acc_re, acc_im, dbre, dbim, dcre, dcimn, dd_ref):
                r[...] = jnp.zeros_like(r)
            _scan_coefs(are, aim, shifted, reverse=True)

        dyf = dy_ref[...]
        dyb = dyf.astype(BF16)
        uf = u_ref[...]
        ub = uf.astype(BF16)
        for j in range(N_UTILES):
            dyj = dyb[:, LANES * j:LANES * (j + 1)]
            st = slice(TILE_STATES * j, TILE_STATES * (j + 1))
            lre[:, st] = lax.dot_general(dyj, cre[j], _NT, preferred_element_type=F32)
            lim[:, st] = lax.dot_general(dyj, cimn[j], _NT, preferred_element_type=F32)
        has_pred = (i < n_chunks - 1).astype(F32)
        last_row = lax.broadcasted_iota(jnp.int32, (SUBLANES, SCAN_STRIP), 0) == SUBLANES - 1
        for s in range(N_STRIPS):
            cols = pl.ds(s * SCAN_STRIP, SCAN_STRIP)
            coefs = [c[:, cols] for c in shifted]
            before_re, before_im = hpre_ref[:, cols] * has_pred, hpim_ref[:, cols] * has_pred

            def step(k, carry):
                after_re, after_im, dar, dai = carry
                t = n_tiles - 1 - k
                rows = _tile_rows(t)
                lr, li = _tile_scan(lre[rows, cols], lim[rows, cols], coefs, after_re, after_im, True)
                lre[rows, cols] = lr
                lim[rows, cols] = li
                prev = _tile_rows(jnp.maximum(t - 1, 0))
                pre_re = jnp.where(t == 0, before_re, hre_ref[prev, cols])
                pre_im = jnp.where(t == 0, before_im, him_ref[prev, cols])
                hpr = pltpu.roll(jnp.where(last_row, pre_re, hre_ref[rows, cols]), 1, 0)
                hpi = pltpu.roll(jnp.where(last_row, pre_im, him_ref[rows, cols]), 1, 0)
                return lr, li, dar + lr * hpr + li * hpi, dai + li * hpr - lr * hpi

            cr, ci, dar, dai = lax.fori_loop(0, n_tiles, step, (car_re[:, cols], car_im[:, cols],
                                                               acc_re[:, cols], acc_im[:, cols]))
            car_re[:, cols] = cr
            car_im[:, cols] = ci
            acc_re[:, cols] = dar
            acc_im[:, cols] = dai

        dv = d_ref[...]
        for j in range(N_UTILES):
            sl = slice(LANES * j, LANES * (j + 1))
            st = slice(TILE_STATES * j, TILE_STATES * (j + 1))
            lrj = lre[:, st].astype(BF16)
            lij = lim[:, st].astype(BF16)
            du = (lax.dot_general(lrj, bre[j], _NT, preferred_element_type=F32)
                  + lax.dot_general(lij, bim[j], _NT, preferred_element_type=F32))
            dx_ref[:, sl] = du + dv[:, sl] * dyf[:, sl] + DN_ALPHA * dres_ref[:, sl]
            uj = ub[:, sl]
            dbre[j] += lax.dot_general(uj, lrj, _TN, preferred_element_type=F32)
            dbim[j] += lax.dot_general(uj, lij, _TN, preferred_element_type=F32)
            dyj = dyb[:, sl]
            dcre[j] += lax.dot_general(hre_ref[:, st].astype(BF16), dyj, _TN, preferred_element_type=F32)
            dcimn[j] += lax.dot_general(him_ref[:, st].astype(BF16), dyj, _TN, preferred_element_type=F32)
        dd_ref[...] += jnp.sum(dyf * uf, axis=0, keepdims=True)

        @pl.when(i == n_chunks - 1)
        def _():
            dar_ref[...] = jnp.sum(acc_re[...], axis=0, keepdims=True)
            dai_ref[...] = jnp.sum(acc_im[...], axis=0, keepdims=True)

    rev = lambda i: (n_chunks - 1 - i, 0)
    prev_tile = lambda i: (jnp.maximum((n_chunks - 1 - i) * n_tiles - 1, 0), 0)
    full3 = lambda a: pl.BlockSpec(a.shape, lambda i: (0, 0, 0))
    full2 = lambda a: pl.BlockSpec(a.shape, lambda i: (0, 0))
    acc3 = lambda shape: pl.BlockSpec(shape, lambda i: (0, 0, 0))
    acc2 = lambda shape: pl.BlockSpec(shape, lambda i: (0, 0))
    tile = pltpu.VMEM((SUBLANES, N_STATES), F32)
    return _pcall_riding(
        body, (dy, u, dres, h_re, h_im, h_re, h_im, bbd_re, bbd_im, cbd_re, cbd_imn, a_re, a_im, dskip), ride,
        lambda: pl.program_id(0) == 0, lambda: pl.program_id(0) == n_chunks - 1,
        name="s5_bwd", grid=(n_chunks,),
        in_specs=[pl.BlockSpec((t_rows, D_MODEL), rev), pl.BlockSpec((t_rows, D_MODEL), rev),
                  pl.BlockSpec((t_rows, D_MODEL), rev),
                  pl.BlockSpec((t_rows, N_STATES), rev), pl.BlockSpec((t_rows, N_STATES), rev),
                  pl.BlockSpec((SUBLANES, N_STATES), prev_tile), pl.BlockSpec((SUBLANES, N_STATES), prev_tile),
                  full3(bbd_re), full3(bbd_im), full3(cbd_re), full3(cbd_imn), full2(a_re), full2(a_im), full2(dskip)],
        out_specs=[pl.BlockSpec((t_rows, D_MODEL), rev), acc3(bbd_re.shape), acc3(bbd_im.shape), acc3(cbd_re.shape),
                   acc3(cbd_imn.shape), acc2((1, N_STATES)), acc2((1, N_STATES)), acc2((1, D_MODEL))],
        out_shape=[jax.ShapeDtypeStruct((seq, D_MODEL), F32), jax.ShapeDtypeStruct(bbd_re.shape, F32),
                   jax.ShapeDtypeStruct(bbd_im.shape, F32), jax.ShapeDtypeStruct(cbd_re.shape, F32),
                   jax.ShapeDtypeStruct(cbd_imn.shape, F32), jax.ShapeDtypeStruct((1, N_STATES), F32),
                   jax.ShapeDtypeStruct((1, N_STATES), F32), jax.ShapeDtypeStruct((1, D_MODEL), F32)],
        scratch_shapes=[pltpu.VMEM((t_rows, N_STATES), F32), pltpu.VMEM((t_rows, N_STATES), F32)] + [tile] * 12,
        compiler_params=_params(("arbitrary",)))


def _eye_groups():
    return jnp.eye(GROUPS_PER_TILE, dtype=F32)


def _blockdiag_in(bb):
    t = bb.transpose(0, 2, 1).reshape(N_UTILES, GROUPS_PER_TILE, SSM_GROUP, SSM_STATE)
    bd = jnp.einsum("jgcp,gh->jgchp", t, _eye_groups())
    return bd.reshape(N_UTILES, LANES, TILE_STATES)


def _blockdiag_in_t(d):
    t = jnp.einsum("jgchp,gh->jgcp", d.reshape(N_UTILES, GROUPS_PER_TILE, SSM_GROUP, GROUPS_PER_TILE, SSM_STATE),
                   _eye_groups())
    return t.reshape(N_GROUPS, SSM_GROUP, SSM_STATE).transpose(0, 2, 1)


def _blockdiag_out(c):
    t = c.transpose(0, 2, 1).reshape(N_UTILES, GROUPS_PER_TILE, SSM_STATE, SSM_GROUP)
    bd = jnp.einsum("jhpc,hg->jhpgc", t, _eye_groups())
    return bd.reshape(N_UTILES, TILE_STATES, LANES)


def _blockdiag_out_t(d):
    t = jnp.einsum("jhpgc,hg->jhpc", d.reshape(N_UTILES, GROUPS_PER_TILE, SSM_STATE, GROUPS_PER_TILE, SSM_GROUP),
                   _eye_groups())
    return t.reshape(N_GROUPS, SSM_STATE, SSM_GROUP).transpose(0, 2, 1)


ATT_TQ = 512
ATT_TK = 512
LOG2E = math.log2(math.e)
LN2 = math.log(2.0)
Q_PRESCALE = SM_SCALE * LOG2E


def _loop_in_pairs(n, step, carry, start=0):
    pairs = (n - start) // 2

    def two(t, c):
        return step(start + 2 * t + 1, step(start + 2 * t, c))

    carry = lax.fori_loop(0, pairs, two, carry)
    return lax.fori_loop(start + 2 * pairs, n, step, carry)


def _causal(s, off=0, transposed=False):
    r = lax.broadcasted_iota(jnp.int32, s.shape, 0)
    c = lax.broadcasted_iota(jnp.int32, s.shape, 1)
    keep = (r <= c + off) if transposed else (c <= r + off)
    return jnp.where(keep, s, NEG_INF)


def _q_specs(rows, at):
    def nope(*ids):
        r, h = at(*ids)
        return r, 3 * (h // HEADS_PER_CHIP) + h % HEADS_PER_CHIP

    def rope(*ids):
        r, h = at(*ids)
        return r, 3 * (h // HEADS_PER_CHIP) + HEADS_PER_CHIP

    return [pl.BlockSpec((rows, LANES), nope), pl.BlockSpec((rows, LANES), rope)]


def _kv_specs(rows, at):
    def col(f):
        def index(*ids):
            r, h = at(*ids)
            return r, f(h)
        return index

    return [pl.BlockSpec((rows, LANES), col(lambda h: 2 * h)), pl.BlockSpec((rows, LANES), col(lambda h: h % HEADS_PER_CHIP)),
            pl.BlockSpec((rows, LANES), col(lambda h: 2 * h + 1))]


def _cat(a, b):
    return jnp.concatenate([a, b], axis=1)


def attn_fwd(q, kv, kr, ride=None, tq=ATT_TQ, tk=ATT_TK):
    seq = q.shape[0]
    n_heads = N_HEADS
    tq, tk = min(tq, seq), min(tk, seq)

    def body(qn_ref, qr_ref, kn_ref, kr_ref, v_ref, o_ref, lse_ref):
        qi = pl.program_id(1)
        qv = _cat(qn_ref[...], qr_ref[...])
        jd = (qi * tq) // tk

        def block(j, carry, diag):
            m, l, acc = carry
            rows = pl.ds(pl.multiple_of(j * tk, tk), tk)
            s = lax.dot_general(qv, _cat(kn_ref[rows, :], kr_ref[rows, :]), _NT, preferred_element_type=F32)
            if diag:
                s = _causal(s, qi * tq - jd * tk)
            m_new = jnp.maximum(m, jnp.max(s, axis=-1, keepdims=True))
            p = jnp.exp2(s - m_new)
            corr = jnp.exp2(m - m_new)
            l = l * corr + jnp.sum(p, axis=-1, keepdims=True)
            acc = acc * corr + jnp.dot(p.astype(BF16), v_ref[rows, :], preferred_element_type=F32)
            return m_new, l, acc

        init = (jnp.full((tq, 1), NEG_INF, F32), jnp.zeros((tq, 1), F32), jnp.zeros((tq, V_HEAD), F32))
        carry = _loop_in_pairs(jd, lambda j, c: block(j, c, False), init)
        m, l, acc = block(jd, carry, True)
        o_ref[...] = acc / l
        lse_ref[0] = jnp.broadcast_to(m + jnp.log2(l), (tq, LANES))

    n_q = seq // tq
    return _pcall_riding(
        body, (q, q, kv, kr, kv), ride,
        lambda: (pl.program_id(0) == 0) & (pl.program_id(1) == 0),
        lambda: (pl.program_id(0) == n_heads - 1) & (pl.program_id(1) == n_q - 1),
        name="attn_fwd", grid=(n_heads, n_q),
        in_specs=_q_specs(tq, lambda h, i: (i, h)) + _kv_specs(seq, lambda h, i: (0, h)),
        out_specs=[pl.BlockSpec((tq, V_HEAD), lambda h, i: (i, h)),
                   pl.BlockSpec((1, tq, LANES), lambda h, i: (h, i, 0))],
        out_shape=[jax.ShapeDtypeStruct((seq, n_heads * V_HEAD), F32),
                   jax.ShapeDtypeStruct((n_heads, seq, LANES), F32)],
        compiler_params=_params(("arbitrary", "arbitrary")))


def attn_bwd_dq(q, kv, kr, do, o, lse, tq=ATT_TQ, tk=ATT_TK):
    seq = q.shape[0]
    tq, tk = min(tq, seq), min(tk, seq)
    head = lambda c, i, hh: HEADS_PER_CHIP * c + hh

    def body(qn_ref, qr_ref, kn_ref, kr_ref, v_ref, do_ref, o_ref, lse_ref, dqn_ref, dqr_ref, delta_ref):
        qi = pl.program_id(1)
        qv = _cat(qn_ref[...], qr_ref[...])
        dof = do_ref[...]
        dob = dof.astype(BF16)
        delta = jnp.sum(dof * o_ref[...], axis=-1, keepdims=True)
        lse = lse_ref[0][:, :1]
        jd = (qi * tq) // tk

        def block(j, dq, diag):
            rows = pl.ds(pl.multiple_of(j * tk, tk), tk)
            kv = _cat(kn_ref[rows, :], kr_ref[rows, :])
            s = lax.dot_general(qv, kv, _NT, preferred_element_type=F32)
            if diag:
                s = _causal(s, qi * tq - jd * tk)
            p = jnp.exp2(s - lse)
            dp = lax.dot_general(dob, v_ref[rows, :], _NT, preferred_element_type=F32)
            ds = p * (dp - delta)
            return dq + jnp.dot(ds.astype(BF16), kv, preferred_element_type=F32)

        dq = _loop_in_pairs(jd, lambda j, c: block(j, c, False), jnp.zeros((tq, 2 * LANES), F32))
        dq = block(jd, dq, True) * SM_SCALE
        dqn_ref[...] = dq[:, :LANES]

        @pl.when(pl.program_id(2) == 0)
        def _():
            dqr_ref[...] = dq[:, LANES:]

        @pl.when(pl.program_id(2) > 0)
        def _():
            dqr_ref[...] += dq[:, LANES:]

        delta_ref[0] = jnp.broadcast_to(delta, (tq, LANES))

    by_head = lambda c, i, hh: (i, head(c, i, hh))
    dq_nope, dq_rope, delta = _pcall(
        body, name="attn_bwd_dq", grid=(N_CHIPS, seq // tq, HEADS_PER_CHIP),
        in_specs=_q_specs(tq, by_head) + _kv_specs(seq, lambda c, i, hh: (0, head(c, i, hh)))
        + [pl.BlockSpec((tq, V_HEAD), by_head), pl.BlockSpec((tq, V_HEAD), by_head),
           pl.BlockSpec((1, tq, LANES), lambda c, i, hh: (head(c, i, hh), i, 0))],
        out_specs=[pl.BlockSpec((tq, LANES), lambda c, i, hh: (i, head(c, i, hh))),
                   pl.BlockSpec((tq, LANES), lambda c, i, hh: (i, c)),
                   pl.BlockSpec((1, tq, LANES), lambda c, i, hh: (head(c, i, hh), i, 0))],
        out_shape=[jax.ShapeDtypeStruct((seq, N_HEADS * QK_NOPE), F32),
                   jax.ShapeDtypeStruct((seq, N_CHIPS * LANES), F32),
                   jax.ShapeDtypeStruct((N_HEADS, seq, LANES), F32)],
        compiler_params=_params(("parallel", "parallel", "arbitrary")),
    )(q, q, kv, kr, kv, do, o, lse)
    return dq_nope, dq_rope, delta


def attn_bwd_dkv(q, kv, kr, do, lse_row, delta_row, tq=ATT_TK):
    seq = q.shape[0]
    tq = min(tq, seq)
    n_blk = seq // tq

    def body(qn_ref, qr_ref, kn_ref, kr_ref, v_ref, do_ref, lse_ref, delta_ref, dkv_ref, dkr_ref):
        kj = pl.program_id(1)
        kv = _cat(kn_ref[...], kr_ref[...])
        vv = v_ref[...]

        def block(i, carry, diag):
            dk, dv = carry
            rows = pl.ds(pl.multiple_of(i * tq, tq), tq)
            qv = _cat(qn_ref[rows, :], qr_ref[rows, :])
            st = lax.dot_general(kv, qv, _NT, preferred_element_type=F32)
            if diag:
                st = _causal(st, transposed=True)
            pt = jnp.exp2(st - lse_ref[0, pl.ds(i, 1), :])
            dob = do_ref[rows, :].astype(BF16)
            dv = dv + jnp.dot(pt.astype(BF16), dob, preferred_element_type=F32)
            dpt = lax.dot_general(vv, dob, _NT, preferred_element_type=F32)
            dst = pt * (dpt - delta_ref[0, pl.ds(i, 1), :])
            dk = dk + jnp.dot(dst.astype(BF16), qv, preferred_element_type=F32)
            return dk, dv

        carry = block(kj, (jnp.zeros((tq, 2 * LANES), F32), jnp.zeros((tq, V_HEAD), F32)), True)
        dk, dv = _loop_in_pairs(n_blk, lambda i, c: block(i, c, False), carry, start=kj + 1)
        dk = dk * LN2
        dkv_ref[...] = _cat(dk[:, :LANES], dv).astype(dkv_ref.dtype)
        lane = lax.broadcasted_iota(jnp.int32, (tq, LANES), 1)
        mine = (lane // HALF_ROPE) % HEADS_PER_CHIP == pl.program_id(0) % HEADS_PER_CHIP
        dkr_ref[0] = jnp.where(mine, dk[:, LANES:], 0.0)

    return _pcall(
        body, name="attn_bwd_dkv", grid=(N_HEADS, n_blk),
        in_specs=_q_specs(seq, lambda h, j: (0, h)) + _kv_specs(tq, lambda h, j: (j, h))
        + [pl.BlockSpec((seq, V_HEAD), lambda h, j: (0, h)),
           pl.BlockSpec((1, n_blk, tq), lambda h, j: (h, 0, 0)),
           pl.BlockSpec((1, n_blk, tq), lambda h, j: (h, 0, 0))],
        out_specs=[pl.BlockSpec((tq, QK_NOPE + V_HEAD), lambda h, j: (j, h)),
                   pl.BlockSpec((1, tq, LANES), lambda h, j: (h, j, 0))],
        out_shape=[jax.ShapeDtypeStruct((seq, N_HEADS * (QK_NOPE + V_HEAD)), BF16),
                   jax.ShapeDtypeStruct((N_HEADS, seq, LANES), F32)],
        compiler_params=_params(("parallel", "parallel")),
    )(q, q, kv, kr, kv, do, lse_row, delta_row)


def head_sum(x, ts=512):
    n_heads, seq, w = x.shape
    ts = min(ts, seq)

    def body(x_ref, o_ref):
        o_ref[...] = jnp.sum(x_ref[...], axis=0)

    return _pcall(body, name="head_sum", grid=(seq // ts,),
                  in_specs=[pl.BlockSpec((n_heads, ts, w), lambda i: (0, i, 0))],
                  out_specs=pl.BlockSpec((ts, w), lambda i: (i, 0)),
                  out_shape=jax.ShapeDtypeStruct((seq, w), F32),
                  compiler_params=_params(("parallel",)))(x)


HEADS_PER_CHIP = N_HEADS // N_CHIPS
Q_CHIP = HEADS_PER_CHIP * QK_DIM
Q_CHIP_NOPE = HEADS_PER_CHIP * QK_NOPE


def _perm_q_cols(w):
    t = w.reshape(w.shape[0], HEADS_PER_CHIP, QK_DIM)
    return jnp.concatenate([t[:, :, :QK_NOPE].reshape(w.shape[0], -1),
                            t[:, :, QK_NOPE:QK_NOPE + HALF_ROPE].reshape(w.shape[0], -1),
                            t[:, :, QK_NOPE + HALF_ROPE:].reshape(w.shape[0], -1)], axis=1)


def _unperm_q_cols(w):
    r = w.shape[0]
    nope = w[:, :Q_CHIP_NOPE].reshape(r, HEADS_PER_CHIP, QK_NOPE)
    r1 = w[:, Q_CHIP_NOPE:Q_CHIP_NOPE + QK_ROPE].reshape(r, HEADS_PER_CHIP, HALF_ROPE)
    r2 = w[:, Q_CHIP_NOPE + QK_ROPE:].reshape(r, HEADS_PER_CHIP, HALF_ROPE)
    return jnp.concatenate([nope, r1, r2], axis=2).reshape(r, Q_CHIP)


def _pad_kva_cols(w):
    z = jnp.zeros((w.shape[0], HALF_ROPE), w.dtype)
    return jnp.concatenate([w[:, :KV_LORA], w[:, KV_LORA:KV_LORA + HALF_ROPE], z, w[:, KV_LORA + HALF_ROPE:], z], axis=1)


def _unpad_kva_cols(w):
    return jnp.concatenate([w[:, :KV_LORA], w[:, KV_LORA:KV_LORA + HALF_ROPE],
                            w[:, KV_LORA + QK_ROPE:KV_LORA + QK_ROPE + HALF_ROPE]], axis=1)


def _rope_tile(t, cs, sn):
    return t * cs + pltpu.roll(t, LANES // 2, 1) * sn


def _rope_tile_bwd(d, cs, sn):
    return d * cs + pltpu.roll(d * sn, LANES // 2, 1)


def _b_cols(tk, tn):
    return pl.BlockSpec((None, tk, tn), lambda i, j, k: (j, k, 0))


def _b_cols_t(tk, tn):
    return pl.BlockSpec((None, tn, tk), lambda i, j, k: (k, j, 0))


def _out_cols(shape):
    return shape, lambda tm, tn: pl.BlockSpec((None, tm, tn), lambda i, j, k: (j, i, 0))


def _halves(a):
    return a.reshape(N_CHIPS, 2, a.shape[1] // 2, a.shape[2])


def device_step(x, positions, target, w, comm=None):
    seq = x.shape[0]
    w = dict(w)

    def gathered(names, outs):
        for n, a in zip(names, outs):
            if isinstance(n, tuple):
                w[n[0]] = [a.reshape(v.shape) if l == n[1] else v for l, v in enumerate(w[n[0]])]
            else:
                w[n] = a.reshape(w[n].shape)

    def ride_for(names):
        if comm is None:
            return None
        return GatherRide([_halves(w[n[0]][n[1]] if isinstance(n, tuple) else w[n]) for n in names])

    first_ride = ("ssm_w_glu", "ssm_w_out", ("w_ff1", 0), ("w_ff2", 0), "kv_w_a", "kv_w_b", "q_w_a", "q_w_b", "attn_w_o")
    second_ride = (("w_ff1", 1), ("w_ff2", 1))

    inv_freq = ROPE_THETA ** (-jnp.arange(HALF_ROPE, dtype=F32) / HALF_ROPE)
    ang = positions.astype(F32)[:, None] * inv_freq
    cos, sin = jnp.cos(ang), jnp.sin(ang)
    zero = jnp.zeros_like(cos)
    cos_q, sin_q = jnp.concatenate([cos] * 4, 1), jnp.concatenate([-sin, -sin, sin, sin], 1)
    cos_k, sin_k = jnp.concatenate([cos, zero, cos, zero], 1), jnp.concatenate([-sin, zero, sin, zero], 1)
    ff_tile = D_FF // N_CHIPS
    pack_shape = (N_CHIPS, EARLY_ROWS, PACK_W)

    lr = w["ssm_lam_re"].reshape(N_STATES, 1)
    li = w["ssm_lam_im"].reshape(N_STATES, 1)
    ldt = jnp.repeat(w["ssm_log_dt"].reshape(N_GROUPS), SSM_STATE).reshape(N_STATES, 1)
    b_re = w["ssm_b_re"].reshape(N_STATES, SSM_GROUP)
    b_im = w["ssm_b_im"].reshape(N_STATES, SSM_GROUP)
    a_re, a_im, bb_re, bb_im = s5_prep(lr, li, ldt, b_re, b_im)
    a_re, a_im = a_re.reshape(1, N_STATES), a_im.reshape(1, N_STATES)
    bbd_re = _blockdiag_in(bb_re.reshape(N_GROUPS, SSM_STATE, SSM_GROUP)).astype(BF16)
    bbd_im = _blockdiag_in(bb_im.reshape(N_GROUPS, SSM_STATE, SSM_GROUP)).astype(BF16)
    cbd_re = _blockdiag_out(w["ssm_c_re"].reshape(N_GROUPS, SSM_GROUP, SSM_STATE)).astype(BF16)
    cbd_imn = _blockdiag_out(-w["ssm_c_im"].reshape(N_GROUPS, SSM_GROUP, SSM_STATE)).astype(BF16)
    dskip = w["ssm_d"].reshape(1, D_MODEL)
    (ypre, h_re, h_im), landed = s5_fwd(x, bbd_re, bbd_im, cbd_re, cbd_imn, a_re, a_im, dskip, ride_for(first_ride))
    gathered(first_ride, landed)
    (yg,) = rowwise(lambda y: ((_gelu(y),), ()), (ypre,), ((D_MODEL, BF16),), name="gelu")
    w_glu = w["ssm_w_glu"]
    glu_tile = w_glu.shape[2]
    vg = mm(yg, w_glu, n_dim=2 * D_MODEL, tiles=(None, glu_tile, None), b_view=_b_cols, name="glu_proj")

    def glu(v):
        return (v[:, :D_MODEL] * _sigmoid(v[:, D_MODEL:]),), ()
    (z,) = rowwise(glu, (vg,), ((D_MODEL, BF16),), name="glu")
    w_out = w["ssm_w_out"].reshape(D_MODEL, D_MODEL)
    mix0 = mm(z, w_out, name="ssm_out")

    def mlp_fwd(hb, layer):
        pre = mm(hb, w["w_ff1"][layer], n_dim=D_FF, tiles=(None, ff_tile, None), b_view=_b_cols, name=f"ff1_{layer}",
                 out_dtypes=(BF16,))
        f = mm(pre, w["w_ff2"][layer].reshape(D_FF, D_MODEL), pro_a=_relu2, name=f"ff2_{layer}")
        return pre, f

    ln = lambda name, l: w[name][l].reshape(1, D_MODEL)
    h1, h1b = ln_fwd(x, mix0, ln("ln_mix_g", 0), ln("ln_mix_b", 0), "ln_mix_0")
    f1pre, f1 = mlp_fwd(h1b, 0)
    h2, h2b = ln_fwd(h1, f1, ln("ln_ffn_g", 0), ln("ln_ffn_b", 0), "ln_ffn_0")

    kv_w_a = w["kv_w_a"].reshape(D_MODEL, KVA_PAD)
    kv_w_b = w["kv_w_b"]
    q_w_a = w["q_w_a"].reshape(D_MODEL, Q_LORA)
    q_w_b = w["q_w_b"]
    w_o = w["attn_w_o"].reshape(D_MODEL, D_MODEL)
    kvb_tile = kv_w_b.shape[2]
    kvn_g = w["kv_norm_g"].reshape(1, KV_LORA)
    qn_g = w["q_norm_g"].reshape(1, Q_LORA)
    kva = mm(h2b, kv_w_a, name="kv_a")

    def kv_post(kva, g, cs, sn):
        tile = _rope_tile(kva[:, KV_LORA:], cs, sn)
        return (_rms(kva[:, :KV_LORA], g), _cat(tile, pltpu.roll(tile, HALF_ROPE, 1))), ()
    ckv, krope = rowwise(kv_post, (kva, kvn_g, cos_k, sin_k), ((KV_LORA, BF16), (2 * LANES, BF16)), name="kv_post")
    kvb = mm(ckv, kv_w_b, n_dim=N_CHIPS * kvb_tile, tiles=(None, kvb_tile, KV_LORA), b_view=_b_cols, name="kv_b",
             out_dtypes=(BF16,))
    cq_raw = mm(h2b, q_w_a, name="q_a")
    (cq,) = rowwise(lambda c, g: ((_rms(c, g),), ()), (cq_raw, qn_g), ((Q_LORA, BF16),), name="q_norm")
    qlin = mm(cq, q_w_b, n_dim=N_CHIPS * Q_CHIP, tiles=(None, Q_CHIP, Q_LORA), b_view=_b_cols, name="q_b")

    def on_rope_tiles(fn, scale=None):
        def apply(q, cs, sn):
            parts = []
            for k in range(N_CHIPS):
                parts.append(q[:, Q_CHIP * k:Q_CHIP * k + Q_CHIP_NOPE])
                parts.append(fn(q[:, Q_CHIP * k + Q_CHIP_NOPE:Q_CHIP * (k + 1)], cs, sn))
            out = jnp.concatenate(parts, axis=1)
            return (out if scale is None else out * scale,), ()
        return apply
    (qro,) = rowwise(on_rope_tiles(_rope_tile, Q_PRESCALE), (qlin, cos_q, sin_q), ((N_CHIPS * Q_CHIP, BF16),),
                     name="q_rope")
    (o, lse), landed = attn_fwd(qro, kvb, krope, ride_for(second_ride))
    gathered(second_ride, landed)
    mix1 = mm(o, w_o, name="attn_out")
    h3, h3b = ln_fwd(h2, mix1, ln("ln_mix_g", 1), ln("ln_mix_b", 1), "ln_mix_1")
    f2pre, f2 = mlp_fwd(h3b, 1)
    h4, _ = ln_fwd(h3, f2, ln("ln_ffn_g", 1), ln("ln_ffn_b", 1), "ln_ffn_1")

    def loss_fn(y, t):
        e = y - t
        return (e * (1.0 / D_MODEL),), (jnp.broadcast_to(jnp.sum(e * e), (1, LANES)),)
    dh4, loss_acc = rowwise(loss_fn, (h4, target), ((D_MODEL, F32),), accs=(LANES,), name="loss")
    loss = loss_acc[0, 0] * (0.5 / D_MODEL)

    g = {}

    def into_rows(off, rows_per_chip, shape=pack_shape):
        def view(tm, tn):
            nb = rows_per_chip // tm
            return pl.BlockSpec((None, tm, tn), lambda i, j, k: (i // nb, off // tm + i % nb, 0))
        return shape, view

    def into_cols(off):
        return pack_shape, lambda tm, tn: pl.BlockSpec((None, tm, tn), lambda i, j, k: (j, off // tm + i, 0))

    def mlp_bwd(pack, dr, drb, hb, pre, layer):
        dpre = mm(drb, w["w_ff2"][layer].reshape(D_FF, D_MODEL), tb=True, epi=lambda r, p: (r * 2.0 * jnp.maximum(p, 0.0),),
                  extras=(pre,), out_dtypes=(BF16,), tiles=(None, ff_tile, None), name=f"ff2_dx_{layer}")
        pack = mm(pre, drb, ta=True, pro_a=_relu2, name=f"ff2_dw_{layer}", tiles=(None, PACK_W, None), into=pack,
                  out_view=into_rows(EARLY_OFF["w_ff2"] + layer * ff_tile, ff_tile))
        pack = mm(hb, dpre, ta=True, name=f"ff1_dw_{layer}", tiles=(None, PACK_W, None), into=pack,
                  out_view=into_cols(EARLY_OFF["w_ff1"] + layer * D_MODEL))
        dh = mm(dpre, w["w_ff1"][layer], tb=True, epi=lambda r, d: (r + DN_ALPHA * d,), extras=(dr,), n_dim=D_MODEL,
                tiles=(None, D_MODEL, ff_tile), b_view=_b_cols_t, name=f"ff1_dx_{layer}")
        return pack, dh

    dr4, dr4b, dg_f1, db_f1 = ln_bwd(h3, f2, ln("ln_ffn_g", 1), dh4, "ln_ffn_bwd_1")
    pack, dh3 = mlp_bwd(None, dr4, dr4b, h3b, f2pre, 1)
    dr3, dr3b, dg_m1, db_m1 = ln_bwd(h2, mix1, ln("ln_mix_g", 1), dh3, "ln_mix_bwd_1")
    shard_rows = D_MODEL // N_CHIPS
    pack = mm(o, dr3b, ta=True, name="attn_out_dw", tiles=(shard_rows, PACK_W, None), into=pack,
              out_view=into_rows(EARLY_OFF["attn_w_o"], shard_rows))
    do = mm(dr3b, w_o, tb=True, name="attn_out_dx")
    dqn, dqr, delta = attn_bwd_dq(qro, kvb, krope, do, o, lse)
    tb = min(ATT_TK, seq)
    lse_row = lse[:, :, 0].reshape(N_HEADS, seq // tb, tb)
    delta_row = delta[:, :, 0].reshape(N_HEADS, seq // tb, tb)
    dkvb, dkr = attn_bwd_dkv(qro, kvb, krope, do, lse_row, delta_row)

    def q_rope_bwd(dn, dr, cs, sn):
        parts = []
        for k in range(N_CHIPS):
            parts.append(dn[:, Q_CHIP_NOPE * k:Q_CHIP_NOPE * (k + 1)])
            parts.append(_rope_tile_bwd(dr[:, LANES * k:LANES * (k + 1)], cs, sn))
        return (jnp.concatenate(parts, axis=1),), ()
    (dqlin,) = rowwise(q_rope_bwd, (dqn, dqr, cos_q, sin_q), ((N_CHIPS * Q_CHIP, BF16),), name="q_rope_bwd")
    g["q_w_b"] = mm(cq, dqlin, ta=True, name="q_b_dw", tiles=(Q_LORA, Q_CHIP, None), out_view=_out_cols(q_w_b.shape))
    dcq = mm(dqlin, q_w_b, tb=True, n_dim=Q_LORA, tiles=(None, Q_LORA, Q_CHIP), b_view=_b_cols_t, name="q_b_dx")

    def q_norm_bwd(c, gq, d):
        dx, dgq = _rms_bwd(c, gq, d)
        return (dx,), (dgq,)
    dcq_raw, dqn_g = rowwise(q_norm_bwd, (cq_raw, qn_g, dcq), ((Q_LORA, BF16),), accs=(Q_LORA,), name="q_norm_bwd")
    g["q_w_a"] = mm(h2b, dcq_raw, ta=True, name="q_a_dw")
    g["kv_w_b"] = mm(ckv, dkvb, ta=True, name="kv_b_dw", tiles=(KV_LORA, kvb_tile, None), out_view=_out_cols(kv_w_b.shape))
    dckv = mm(dkvb, kv_w_b, tb=True, n_dim=KV_LORA, tiles=(None, KV_LORA, kvb_tile), b_view=_b_cols_t, name="kv_b_dx")
    dkr_sum = head_sum(dkr)

    def kv_post_bwd(kva, gk, dc, dk, cs, sn):
        dx, dgk = _rms_bwd(kva[:, :KV_LORA], gk, dc)
        dk = dk + pltpu.roll(dk, LANES - HALF_ROPE, 1)
        return (jnp.concatenate([dx, _rope_tile_bwd(dk, cs, sn)], axis=1),), (dgk,)
    dkva, dkvn_g = rowwise(kv_post_bwd, (kva, kvn_g, dckv, dkr_sum, cos_k, sin_k), ((KVA_PAD, BF16),),
                           accs=(KV_LORA,), name="kv_post_bwd")
    g["kv_w_a"] = mm(h2b, dkva, ta=True, name="kv_a_dw")
    dh2 = mm(dcq_raw, q_w_a, tb=True, epi=lambda r, d: (r + DN_ALPHA * d,), extras=(dr3,), name="q_a_dx")
    dh2 = mm(dkva, kv_w_a, tb=True, epi=lambda r, d: (r + d,), extras=(dh2,), name="kv_a_dx")

    dr2, dr2b, dg_f0, db_f0 = ln_bwd(h1, f1, ln("ln_ffn_g", 0), dh2, "ln_ffn_bwd_0")
    pack, dh1 = mlp_bwd(pack, dr2, dr2b, h1b, f1pre, 0)
    pack = put_rows(pack, packed_shards(g, MISC_EARLY, EARLY_ROWS - MISC_EARLY_OFF), MISC_EARLY_OFF)
    early_ride = None
    if comm is not None:
        chip_sums = add_halves(pack, swap_halves(pack), comm[1])
        early_ride = SendRide(chip_sums)
    dr1, dr1b, dg_m0, db_m0 = ln_bwd(x, mix0, ln("ln_mix_g", 0), dh1, "ln_mix_bwd_0")
    late = mm(z, dr1b, ta=True, name="ssm_out_dw", tiles=(shard_rows, PACK_W, None),
              out_view=into_rows(LATE_OFF["ssm_w_out"], shard_rows, (N_CHIPS, LATE_ROWS, PACK_W)))
    dz = mm(dr1b, w_out, tb=True, name="ssm_out_dx")

    def glu_bwd(v, dz):
        val, sg = v[:, :D_MODEL], _sigmoid(v[:, D_MODEL:])
        return (jnp.concatenate([dz * sg, dz * val * sg * (1.0 - sg)], axis=1),), ()
    (dvg,) = rowwise(glu_bwd, (vg, dz), ((2 * D_MODEL, BF16),), name="glu_bwd")
    g["ssm_w_glu"] = mm(yg, dvg, ta=True, name="glu_proj_dw", tiles=(None, glu_tile, None), out_view=_out_cols(w_glu.shape))
    dypre = mm(dvg, w_glu, tb=True, epi=lambda r, y: (r * _gelu_grad(y),), extras=(ypre,), n_dim=D_MODEL,
               tiles=(None, D_MODEL, glu_tile), b_view=_b_cols_t, name="glu_proj_dx")
    (dx, dbbd_re, dbbd_im, dcbd_re, dcbd_imn, dar, dai, dd), got_early = s5_bwd(
        dypre, x, dr1, h_re, h_im, bbd_re, bbd_im, cbd_re, cbd_imn, a_re, a_im, dskip, early_ride)
    dbb_re = _blockdiag_in_t(dbbd_re).reshape(N_STATES, SSM_GROUP)
    dbb_im = _blockdiag_in_t(dbbd_im).reshape(N_STATES, SSM_GROUP)
    dlr, dli, dldt, db_re, db_im = s5_prep_bwd(lr, li, ldt, b_re, b_im, dar.reshape(N_STATES, 1),
                                               dai.reshape(N_STATES, 1), dbb_re, dbb_im)
    g["ssm_lam_re"] = dlr.reshape(1, N_GROUPS, SSM_STATE)
    g["ssm_lam_im"] = dli.reshape(1, N_GROUPS, SSM_STATE)
    g["ssm_log_dt"] = group_sum(dldt).reshape(1, N_GROUPS)
    g["ssm_b_re"] = db_re.reshape(1, N_GROUPS, SSM_STATE, SSM_GROUP)
    g["ssm_b_im"] = db_im.reshape(1, N_GROUPS, SSM_STATE, SSM_GROUP)
    g["ssm_c_re"] = _blockdiag_out_t(dcbd_re).reshape(1, N_GROUPS, SSM_GROUP, SSM_STATE)
    g["ssm_c_im"] = -_blockdiag_out_t(dcbd_imn).reshape(1, N_GROUPS, SSM_GROUP, SSM_STATE)
    g["ssm_d"] = dd
    g["ln_mix_g"] = jnp.concatenate([dg_m0, dg_m1], 0)
    g["ln_mix_b"] = jnp.concatenate([db_m0, db_m1], 0)
    g["ln_ffn_g"] = jnp.concatenate([dg_f0, dg_f1], 0)
    g["ln_ffn_b"] = jnp.concatenate([db_f0, db_f1], 0)
    g["kv_norm_g"] = dkvn_g.reshape(KV_LORA)
    g["q_norm_g"] = dqn_g
    return loss, dx, pack, late, g, (early_ride.ins[0], got_early[0]) if comm is not None else None


def place(shard, me_idx, dtype, name):
    rows, cols = shard.shape
    tr = _tile(rows, (512, 256, 128))

    def body(m_ref, x_ref, o_ref):
        o_ref[...] = x_ref[...].astype(o_ref.dtype)

    return _pcall(
        body, name=name,
        grid_spec=pltpu.PrefetchScalarGridSpec(
            num_scalar_prefetch=1, grid=(rows // tr,),
            in_specs=[pl.BlockSpec((tr, cols), lambda i, m: (i, 0))],
            out_specs=pl.BlockSpec((None, tr, cols), lambda i, m: (m[0], i, 0))),
        out_shape=jax.ShapeDtypeStruct((N_CHIPS, rows, cols), dtype),
        compiler_params=_params(("parallel",)),
    )(me_idx, shard)


def put_rows(pack, rows, off):
    _, n, cols = rows.shape
    tr = math.gcd(math.gcd(off, n), 512)

    def body(r_ref, p_ref, o_ref):
        o_ref[...] = r_ref[...]

    return _pcall(body, name="grad_put_rows", grid=(N_CHIPS, n // tr),
                  in_specs=[pl.BlockSpec((None, tr, cols), lambda k, i: (k, i, 0)), _ANY],
                  out_specs=pl.BlockSpec((None, tr, cols), lambda k, i: (k, off // tr + i, 0)),
                  out_shape=jax.ShapeDtypeStruct(pack.shape, pack.dtype), input_output_aliases={1: 0},
                  compiler_params=_params(("parallel", "parallel")))(rows, pack)


def _my_cols(c, mine=True):
    start = (c if mine else 1 - c) * HALF_W
    return pl.ds(pl.multiple_of(start, HALF_W), HALF_W)


def swap_halves(gpack):
    n, rows, _ = gpack.shape

    def body(g_ref, got_ref, send_sem, recv_sem):
        x, y, c, _ = _place()
        cp = pltpu.make_async_remote_copy(src_ref=g_ref.at[:, :, _my_cols(c, mine=False)], dst_ref=got_ref,
                                          send_sem=send_sem, recv_sem=recv_sem, device_id=(x, y, 1 - c),
                                          device_id_type=MESH)
        cp.start()
        cp.wait()

    return _pcall(body, name="grad_swap_halves", in_specs=[_ANY], out_specs=_ANY,
                  out_shape=jax.ShapeDtypeStruct((n, rows, HALF_W), gpack.dtype),
                  scratch_shapes=[pltpu.SemaphoreType.DMA, pltpu.SemaphoreType.DMA])(gpack)


def add_halves(gpack, got, c_idx):
    n, rows, _ = gpack.shape
    blk = (None, G_BLOCK_ROWS, HALF_W)

    def body(c_ref, g_ref, r_ref, o_ref):
        o_ref[...] = (g_ref[...] + r_ref[...]).astype(o_ref.dtype)

    return _pcall(
        body, name="grad_add_halves",
        grid_spec=pltpu.PrefetchScalarGridSpec(
            num_scalar_prefetch=1, grid=(n, rows // G_BLOCK_ROWS),
            in_specs=[pl.BlockSpec(blk, lambda k, i, c: (k, i, c[0])), pl.BlockSpec(blk, lambda k, i, c: (k, i, 0))],
            out_specs=pl.BlockSpec(blk, lambda k, i, c: (k, i, 0))),
        out_shape=jax.ShapeDtypeStruct((n, rows, HALF_W), BF16),
        compiler_params=_params(("parallel", "parallel")),
    )(c_idx, gpack, got)


def sum_owner(part, got, idx, total_rows, row_off=0, into=None):
    _, rows, _ = part.shape
    tr = G_BLOCK_ROWS
    n_into = 0 if into is None else 1

    def body(m_ref, p_ref, g_ref, *rest):
        up = lambda v: v.astype(F32)
        rest[-1][...] = ((up(p_ref[...]) + up(g_ref[0])) + up(g_ref[1])) + up(g_ref[2])

    return _pcall(
        body, name="grad_sum_owner",
        grid_spec=pltpu.PrefetchScalarGridSpec(
            num_scalar_prefetch=1, grid=(rows // tr,),
            in_specs=[pl.BlockSpec((None, tr, HALF_W), lambda i, m: (m[0], i, 0)),
                      pl.BlockSpec((3, tr, HALF_W), lambda i, m: (0, i, 0))] + [_ANY] * n_into,
            out_specs=pl.BlockSpec((tr, HALF_W), lambda i, m: (row_off // tr + i, m[1]))),
        out_shape=jax.ShapeDtypeStruct((total_rows, PACK_W), F32),
        input_output_aliases={3: 0} if n_into else {},
        compiler_params=_params(("parallel",)),
    )(idx, part, got, *([into] if n_into else []))


def join_halves(red):
    def body(in_ref, out_ref, send_sem, recv_sem):
        x, y, c, _ = _place()
        sibling = (x, y, 1 - c)
        mine = out_ref.at[:, _my_cols(c)]
        cp = pltpu.make_async_remote_copy(src_ref=mine, dst_ref=mine, send_sem=send_sem, recv_sem=recv_sem,
                                          device_id=sibling, device_id_type=MESH)
        cp.start()
        cp.wait_send()
        other = out_ref.at[:, _my_cols(c, mine=False)]
        pltpu.make_async_remote_copy(src_ref=other, dst_ref=other, send_sem=send_sem, recv_sem=recv_sem,
                                     device_id=sibling, device_id_type=MESH).wait_recv()

    return _pcall(body, name="grad_join_halves", in_specs=[_ANY], out_specs=_ANY,
                  out_shape=jax.ShapeDtypeStruct(red.shape, red.dtype), input_output_aliases={0: 0},
                  scratch_shapes=[pltpu.SemaphoreType.DMA, pltpu.SemaphoreType.DMA])(red)


def adamw(gsrc, g_off, wt, m, v, name):
    n, cols = wt.shape
    tr = math.gcd(math.gcd(g_off, n), 256) if g_off else math.gcd(n, 256)
    off_blk = g_off // tr
    c1 = 1.0 / (1.0 - ADAM_B1 ** ADAM_STEP)
    c2 = 1.0 / (1.0 - ADAM_B2 ** ADAM_STEP)

    def body(g_ref, w_ref, m_ref, v_ref, go_ref, d_ref, mo_ref, vo_ref):
        gv = g_ref[...]
        mn = ADAM_B1 * m_ref[...] + (1.0 - ADAM_B1) * gv
        vn = ADAM_B2 * v_ref[...] + (1.0 - ADAM_B2) * gv * gv
        go_ref[...] = gv
        mo_ref[...] = mn
        vo_ref[...] = vn
        d_ref[...] = -ADAM_LR * ((mn * c1) / (jnp.sqrt(vn * c2) + ADAM_EPS) + ADAM_WD * w_ref[...])

    blk = pl.BlockSpec((tr, cols), lambda i: (i, 0))
    return _pcall(body, name=name, grid=(n // tr,),
                  in_specs=[pl.BlockSpec((tr, cols), lambda i: (off_blk + i, 0)), blk, blk, blk],
                  out_specs=[blk] * 4, out_shape=[jax.ShapeDtypeStruct((n, cols), F32)] * 4,
                  compiler_params=_params(("parallel",)))(gsrc, wt, m, v)


def _rows8(a):
    return -(-a.size // (8 * PACK_W)) * 8


def _as_rows(a, rows=None):
    flat = a.reshape(-1)
    n = _rows8(a) if rows is None else rows
    return jnp.pad(flat, (0, n * PACK_W - flat.shape[0])).reshape(n, PACK_W)


def local_shards_2d(wl):
    return {"w_ff1": [wl["w_ff1"][0], wl["w_ff1"][1]], "w_ff2": [wl["w_ff2"][0], wl["w_ff2"][1]],
            "ssm_w_glu": wl["ssm_w_glu"], "ssm_w_out": wl["ssm_w_out"], "kv_w_a": _pad_kva_cols(wl["kv_w_a"]),
            "kv_w_b": wl["kv_w_b"], "q_w_a": wl["q_w_a"], "q_w_b": _perm_q_cols(wl["q_w_b"]),
            "attn_w_o": wl["attn_w_o"], "ssm_d": wl["ssm_d"].reshape(2, -1)}


def misc_grad_shard(name, g, k):
    if name == "ssm_d":
        w = D_MODEL // N_CHIPS
        return g[:, w * k:w * (k + 1)]
    if name in ("ssm_w_glu", "kv_w_b"):
        return g[k]
    if name == "q_w_b":
        return _unperm_q_cols(g[k])
    rows = D_MODEL // N_CHIPS
    shard = g[rows * k:rows * (k + 1)]
    return _unpad_kva_cols(shard) if name == "kv_w_a" else shard


def packed_shards(g, names, rows, lead=None):
    blocks = []
    for k in range(N_CHIPS):
        parts = [] if lead is None else [lead[k * (lead.shape[0] // N_CHIPS):(k + 1) * (lead.shape[0] // N_CHIPS)]]
        parts += [_as_rows(misc_grad_shard(n, g[n], k), MISC_SHARD_ROWS[n]) for n in names]
        blk = jnp.concatenate(parts, axis=0)
        blocks.append(jnp.pad(blk, ((0, rows - blk.shape[0]), (0, 0))))
    return jnp.stack(blocks)


def kernel(x, positions, ln_mix_g, ln_mix_b, ln_ffn_g, ln_ffn_b, w_ff1, w_ff2, ssm_lam_re, ssm_lam_im, ssm_log_dt, ssm_b_re, ssm_b_im, ssm_c_re, ssm_c_im, ssm_d, ssm_w_glu, ssm_w_out, kv_w_a, kv_norm_g, kv_w_b, q_w_a, q_norm_g, q_w_b, attn_w_o, loss_target, m_ln_mix_g, m_ln_mix_b, m_ln_ffn_g, m_ln_ffn_b, m_w_ff1, m_w_ff2, m_ssm_lam_re, m_ssm_lam_im, m_ssm_log_dt, m_ssm_b_re, m_ssm_b_im, m_ssm_c_re, m_ssm_c_im, m_ssm_d, m_ssm_w_glu, m_ssm_w_out, m_kv_w_a, m_kv_norm_g, m_kv_w_b, m_q_w_a, m_q_norm_g, m_q_w_b, m_attn_w_o, v_ln_mix_g, v_ln_mix_b, v_ln_ffn_g, v_ln_ffn_b, v_w_ff1, v_w_ff2, v_ssm_lam_re, v_ssm_lam_im, v_ssm_log_dt, v_ssm_b_re, v_ssm_b_im, v_ssm_c_re, v_ssm_c_im, v_ssm_d, v_ssm_w_glu, v_ssm_w_out, v_kv_w_a, v_kv_norm_g, v_kv_w_b, v_q_w_a, v_q_norm_g, v_q_w_b, v_attn_w_o):
    env = dict(locals())
    wl = {n: env[n] for n in WEIGHTS}
    ml = {n: env["m_" + n] for n in WEIGHTS}
    vl = {n: env["v_" + n] for n in WEIGHTS}
    for n in ("ssm_w_glu", "ssm_w_out", "q_w_a", "q_w_b", "attn_w_o"):
        wl[n], ml[n], vl[n] = wl[n][0], ml[n][0], vl[n][0]

    c_idx = lax.axis_index("c").astype(jnp.int32).reshape(1)
    me_idx = (2 * lax.axis_index("x") + lax.axis_index("y")).astype(jnp.int32).reshape(1)

    local = local_shards_2d(wl)
    put = lambda a, n: place(a, me_idx, F32 if n == "ssm_d" else BF16, "place_" + n)
    stacked = {n: [put(a, f"{n}_{l}") for l, a in enumerate(local[n])] if isinstance(local[n], list) else put(local[n], n)
               for n in SHARDED}
    stacked["ssm_d"] = ride_alone(GatherRide([_halves(stacked["ssm_d"])]), "ssm_d_all_gather")[0].reshape(1, D_MODEL)
    for n in REPLICATED:
        stacked[n] = wl[n]

    loss_part, dx, early, late, g, (early_sums, early_got) = device_step(
        x[0], positions[0], loss_target[0], stacked, comm=(me_idx, c_idx))
    loss = lax.psum(loss_part, ("x", "y", "c"))

    small = jnp.concatenate([_as_rows(g[n]) for n in REPLICATED], axis=0)
    small = jnp.pad(small, ((0, SMALL_ROWS - small.shape[0]), (0, 0)))
    late = put_rows(late, packed_shards(g, MISC_LATE, LATE_ROWS - SMALL_OFF, lead=small), SMALL_OFF)
    late_sums = add_halves(late, swap_halves(late), c_idx)
    late_got = ride_alone(SendRide(late_sums), "grad_send_to_owners")[0]
    where = jnp.concatenate([me_idx, c_idx])
    total_rows = EARLY_ROWS + LATE_ROWS
    reduced = sum_owner(early_sums, early_got, where, total_rows)
    reduced = join_halves(sum_owner(late_sums, late_got, where, total_rows, row_off=EARLY_ROWS, into=reduced))
    quarter = reduced[EARLY_ROWS + SMALL_OFF:EARLY_ROWS + SMALL_OFF + SMALL_Q_ROWS]
    small_tot = ride_alone(GatherRide([_halves(place(quarter, me_idx, F32, "place_small_grads"))]),
                           "small_grad_all_gather")[0].reshape(SMALL_ROWS, PACK_W)

    out_g, out_d, out_m, out_v = {}, {}, {}, {}
    direct = {**EARLY_OFF, **{n: EARLY_ROWS + o for n, o in LATE_OFF.items()}}
    for n, off in direct.items():
        res = adamw(reduced, off, wl[n].reshape(-1, PACK_W), ml[n].reshape(-1, PACK_W), vl[n].reshape(-1, PACK_W),
                    "adamw_" + n)
        out_g[n], out_d[n], out_m[n], out_v[n] = [a.reshape(env[n].shape) for a in res]
    for names, off in ((MISC_EARLY, MISC_EARLY_OFF), (MISC_LATE, EARLY_ROWS + MISC_LATE_OFF)):
        pack3 = lambda d: jnp.concatenate([_as_rows(d[n], MISC_SHARD_ROWS[n]) for n in names], axis=0)
        res = adamw(reduced, off, pack3(wl), pack3(ml), pack3(vl), "adamw_packed_" + names[0])
        r0 = 0
        for n in names:
            cnt = math.prod(env[n].shape)
            out_g[n], out_d[n], out_m[n], out_v[n] = [
                a[r0:r0 + MISC_SHARD_ROWS[n]].reshape(-1)[:cnt].reshape(env[n].shape) for a in res]
            r0 += MISC_SHARD_ROWS[n]
    ws = jnp.concatenate([_as_rows(wl[n]) for n in REPLICATED], axis=0)
    ms = jnp.concatenate([_as_rows(ml[n]) for n in REPLICATED], axis=0)
    vs = jnp.concatenate([_as_rows(vl[n]) for n in REPLICATED], axis=0)
    pad = ((0, SMALL_ROWS - ws.shape[0]), (0, 0))
    res = adamw(small_tot, 0, jnp.pad(ws, pad), jnp.pad(ms, pad), jnp.pad(vs, pad), "adamw_replicated")
    row = 0
    for n in REPLICATED:
        cnt = math.prod(env[n].shape)
        nrows = _rows8(env[n])
        out_g[n], out_d[n], out_m[n], out_v[n] = [a[row:row + nrows].reshape(-1)[:cnt].reshape(env[n].shape) for a in res]
        row += nrows

    return (loss, dx[None], *[out_g[n] for n in WEIGHTS], *[out_d[n] for n in WEIGHTS],
            *[out_m[n] for n in WEIGHTS], *[out_v[n] for n in WEIGHTS])
```

```python
import functools
import math

import jax
import jax.numpy as jnp
from jax import lax
from jax.experimental import pallas as pl
from jax.experimental.pallas import tpu as pltpu

F32 = jnp.float32
BF16 = jnp.bfloat16
MESH = pl.DeviceIdType.MESH

D_MODEL = 1024
DEPTH = 2
SSM_GROUP = 16
N_GROUPS = D_MODEL // SSM_GROUP
SSM_STATE = 64
N_STATES = N_GROUPS * SSM_STATE
N_HEADS = 8
QK_NOPE = 128
QK_ROPE = 64
HALF_ROPE = QK_ROPE // 2
V_HEAD = 128
QK_DIM = QK_NOPE + QK_ROPE
Q_LORA = 384
KV_LORA = 256
ROPE_THETA = 10000.0
SM_SCALE = QK_DIM ** -0.5
NEG_INF = -1e30
D_FF = 4 * D_MODEL
DN_ALPHA = (2 * DEPTH) ** 0.25
LN_EPS = 1e-5
RMS_EPS = 1e-6
ADAM_LR = 0.001
ADAM_B1 = 0.9
ADAM_B2 = 0.999
ADAM_EPS = 1e-08
ADAM_WD = 0.01
ADAM_STEP = 10

N_CHIPS = 4
LANES = 128
VMEM_LIMIT = 56 * 1024 * 1024
PACK_W = 1024
KVA_PAD = 384
HALF_W = PACK_W // 2

SHARDED = ("w_ff1", "w_ff2", "ssm_w_glu", "ssm_w_out", "kv_w_a", "kv_w_b", "q_w_a", "q_w_b", "attn_w_o", "ssm_d")
G_BLOCK_ROWS = 960
EARLY_OFF = {"w_ff1": 0, "w_ff2": 2048, "attn_w_o": 4096}
MISC_EARLY = ("kv_w_b", "kv_w_a", "q_w_a", "q_w_b")
MISC_EARLY_OFF = 4352
EARLY_ROWS = 5 * G_BLOCK_ROWS
LATE_OFF = {"ssm_w_out": 0}
SMALL_Q_ROWS = 96
SMALL_ROWS = N_CHIPS * SMALL_Q_ROWS
MISC_LATE = ("ssm_w_glu", "ssm_d")
MISC_LATE_OFF = 256
SMALL_OFF = MISC_LATE_OFF + 528
LATE_ROWS = G_BLOCK_ROWS
MISC_SHARD_ROWS = {"ssm_d": 16, "ssm_w_glu": 512, "kv_w_b": 128, "kv_w_a": 80, "q_w_a": 96, "q_w_b": 144}
REPLICATED = ("ln_mix_g", "ln_mix_b", "ln_ffn_g", "ln_ffn_b", "ssm_lam_re", "ssm_lam_im", "ssm_log_dt",
              "ssm_b_re", "ssm_b_im", "ssm_c_re", "ssm_c_im", "kv_norm_g", "q_norm_g")
WEIGHTS = ("ln_mix_g", "ln_mix_b", "ln_ffn_g", "ln_ffn_b", "w_ff1", "w_ff2", "ssm_lam_re", "ssm_lam_im",
           "ssm_log_dt", "ssm_b_re", "ssm_b_im", "ssm_c_re", "ssm_c_im", "ssm_d", "ssm_w_glu", "ssm_w_out",
           "kv_w_a", "kv_norm_g", "kv_w_b", "q_w_a", "q_norm_g", "q_w_b", "attn_w_o")


def _pcall(body, **kw):
    return pl.pallas_call(body, **kw)


def _params(sem=None):
    return pltpu.CompilerParams(dimension_semantics=sem, vmem_limit_bytes=VMEM_LIMIT)


_ANY = pl.BlockSpec(memory_space=pl.ANY)


def _tile(dim, prefs):
    for p in prefs:
        if dim % p == 0:
            return p
    return dim


def _place():
    x, y, c = lax.axis_index("x"), lax.axis_index("y"), lax.axis_index("c")
    return x, y, c, [(1 - x, y), (x, 1 - y), (1 - x, 1 - y)]


def _remote(k, src, dst, to, send_sems, recv_sems):
    return pltpu.make_async_remote_copy(src_ref=src, dst_ref=dst, send_sem=send_sems.at[k], recv_sem=recv_sems.at[k],
                                        device_id=to, device_id_type=MESH)


class GatherRide:
    def __init__(self, arrs):
        self.ins = list(arrs)
        self.out_shapes = [jax.ShapeDtypeStruct(a.shape, a.dtype) for a in arrs]
        self.aliases = {i: i for i in range(len(arrs))}
        self.n_sems = 6 * len(arrs)

    def start(self, ins, outs, send_sems, recv_sems):
        x, y, c, chips = _place()
        me = 2 * x + y
        for a, o in enumerate(outs):
            for j, (px, py) in enumerate(chips):
                _remote(6 * a + j, o.at[me, c], o.at[me, c], (px, py, c), send_sems, recv_sems).start()

    def finish(self, ins, outs, send_sems, recv_sems):
        x, y, c, chips = _place()
        me = 2 * x + y
        sibling = (x, y, 1 - c)
        passed = []
        for a, o in enumerate(outs):
            for j, (px, py) in enumerate(chips):
                blk = o.at[2 * px + py, c]
                _remote(6 * a + j, blk, blk, (px, py, c), send_sems, recv_sems).wait_recv()
                cp = _remote(6 * a + 3 + j, blk, blk, sibling, send_sems, recv_sems)
                cp.start()
                passed.append(cp)
        for a, o in enumerate(outs):
            for j, (px, py) in enumerate(chips):
                blk = o.at[2 * px + py, 1 - c]
                _remote(6 * a + 3 + j, blk, blk, sibling, send_sems, recv_sems).wait_recv()
                _remote(6 * a + j, o.at[me, c], o.at[me, c], (px, py, c), send_sems, recv_sems).wait_send()
        for cp in passed:
            cp.wait_send()


class SendRide:
    def __init__(self, part):
        self.ins = [part]
        self.out_shapes = [jax.ShapeDtypeStruct((3,) + part.shape[1:], part.dtype)]
        self.aliases = {}
        self.n_sems = 3

    def _copies(self, ins, outs, send_sems, recv_sems):
        x, y, c, chips = _place()
        return [_remote(j, ins[0].at[2 * px + py], outs[0].at[j], (px, py, c), send_sems, recv_sems)
                for j, (px, py) in enumerate(chips)]

    def start(self, ins, outs, send_sems, recv_sems):
        for cp in self._copies(ins, outs, send_sems, recv_sems):
            cp.start()

    def finish(self, ins, outs, send_sems, recv_sems):
        for cp in self._copies(ins, outs, send_sems, recv_sems):
            cp.wait()


class SwapRide:
    def __init__(self, pack):
        self.ins = [pack]
        self.out_shapes = [jax.ShapeDtypeStruct(pack.shape[:2] + (HALF_W,), pack.dtype)]
        self.aliases = {}
        self.n_sems = 1

    def _copy(self, ins, outs, send_sems, recv_sems):
        x, y, c, _ = _place()
        return _remote(0, ins[0].at[:, :, _my_cols(c, mine=False)], outs[0], (x, y, 1 - c), send_sems, recv_sems)

    def start(self, ins, outs, send_sems, recv_sems):
        self._copy(ins, outs, send_sems, recv_sems).start()

    def finish(self, ins, outs, send_sems, recv_sems):
        self._copy(ins, outs, send_sems, recv_sems).wait()


def _pcall_riding(body, args, ride, first, last, *, in_specs, out_specs, out_shape, scratch_shapes=(), **kw):
    n_in, n_out = len(args), len(out_shape)
    if ride is None:
        return _pcall(body, in_specs=in_specs, out_specs=out_specs, out_shape=out_shape,
                      scratch_shapes=list(scratch_shapes), **kw)(*args), []
    k_in, k_out = len(ride.ins), len(ride.out_shapes)

    def riding(*refs):
        ins, r_in = refs[:n_in], refs[n_in:n_in + k_in]
        outs = refs[n_in + k_in:n_in + k_in + n_out]
        r_out = refs[n_in + k_in + n_out:n_in + k_in + n_out + k_out]
        scratch, (send_sems, recv_sems) = refs[n_in + k_in + n_out + k_out:-2], refs[-2:]

        @pl.when(first())
        def _():
            ride.start(r_in, r_out, send_sems, recv_sems)

        body(*ins, *outs, *scratch)

        @pl.when(last())
        def _():
            ride.finish(r_in, r_out, send_sems, recv_sems)

    res = _pcall(riding, in_specs=list(in_specs) + [_ANY] * k_in, out_specs=list(out_specs) + [_ANY] * k_out,
                 out_shape=list(out_shape) + ride.out_shapes,
                 input_output_aliases={n_in + i: n_out + o for i, o in ride.aliases.items()},
                 scratch_shapes=list(scratch_shapes) + [pltpu.SemaphoreType.DMA((ride.n_sems,))] * 2,
                 **kw)(*args, *ride.ins)
    return res[:n_out], res[n_out:]


def ride_alone(ride, name):
    def body(*refs):
        n = len(ride.ins)
        ins, outs, (send_sems, recv_sems) = refs[:n], refs[n:-2], refs[-2:]
        ride.start(ins, outs, send_sems, recv_sems)
        ride.finish(ins, outs, send_sems, recv_sems)

    return _pcall(body, name=name, in_specs=[_ANY] * len(ride.ins), out_specs=[_ANY] * len(ride.out_shapes),
                  out_shape=ride.out_shapes, input_output_aliases=dict(ride.aliases),
                  scratch_shapes=[pltpu.SemaphoreType.DMA((ride.n_sems,))] * 2)(*ride.ins)


def mm(a, b, *, name, ta=False, tb=False, pro_a=None, epi=None, extras=(), out_dtypes=(F32,), n_dim=None,
       tiles=(None, None, None), b_view=None, out_view=None, into=None, ride=None):
    if ta:
        k_dim, m_dim = a.shape
    else:
        m_dim, k_dim = a.shape
    if n_dim is None:
        n_dim = b.shape[0] if tb else b.shape[1]
    tn = tiles[1] or (n_dim if n_dim <= 1024 else _tile(n_dim, (1024, 512, 256, 128)))
    tm = tiles[0] or _tile(m_dim, (4096, 2048, 1024, 512, 256, 128) if tn <= 512 else (1024, 512, 256, 128))
    tk = tiles[2] or (k_dim if k_dim <= 1024 else _tile(k_dim, (1024, 512, 256, 128)))
    assert m_dim % tm == 0 and n_dim % tn == 0 and k_dim % tk == 0, (name, m_dim, n_dim, k_dim, tm, tn, tk)
    nk = k_dim // tk
    n_ex, n_out = len(extras), len(out_dtypes)
    n_into = 0 if into is None else 1
    dims = (((0 if ta else 1,), (1 if tb else 0,)), ((), ()))

    def body(a_ref, b_ref, *rest):
        ex_refs, out_refs = rest[:n_ex], rest[n_ex + n_into:n_ex + n_into + n_out]

        def partial():
            av = a_ref[...]
            if pro_a is not None:
                av = pro_a(av)
            return lax.dot_general(av.astype(BF16), b_ref[...].astype(BF16), dims, preferred_element_type=F32)

        def finish(r):
            res = epi(r, *[e[...] for e in ex_refs]) if epi is not None else (r,)
            for o_ref, v in zip(out_refs, res):
                o_ref[...] = v.astype(o_ref.dtype)

        if nk == 1:
            finish(partial())
            return
        acc = rest[-1]
        k = pl.program_id(2)

        @pl.when(k == 0)
        def _():
            acc[...] = partial()

        @pl.when(k > 0)
        def _():
            acc[...] += partial()

        @pl.when(k == nk - 1)
        def _():
            finish(acc[...])

    a_spec = pl.BlockSpec((tk, tm), lambda i, j, k: (k, i)) if ta else pl.BlockSpec((tm, tk), lambda i, j, k: (i, k))
    if b_view is not None:
        b_spec = b_view(tk, tn)
    else:
        b_spec = pl.BlockSpec((tn, tk), lambda i, j, k: (j, k)) if tb else pl.BlockSpec((tk, tn), lambda i, j, k: (k, j))
    o_spec = pl.BlockSpec((tm, tn), lambda i, j, k: (i, j))
    if out_view is None:
        out_specs = [o_spec] * n_out
        out_shape = [jax.ShapeDtypeStruct((m_dim, n_dim), dt) for dt in out_dtypes]
    else:
        assert n_out == 1
        out_specs = [out_view[1](tm, tn)]
        out_shape = [jax.ShapeDtypeStruct(out_view[0], out_dtypes[0])]
    grid = (m_dim // tm, n_dim // tn, nk)
    scratch = [pltpu.VMEM((tm, tn), F32)] if nk > 1 else []
    if ride is not None:
        assert into is None
        at = lambda ids: functools.reduce(jnp.logical_and, [pl.program_id(d) == i for d, i in enumerate(ids)])
        outs, landed = _pcall_riding(
            body, (a, b, *extras), ride, lambda: at((0, 0, 0)), lambda: at([g - 1 for g in grid]),
            name=name, grid=grid, in_specs=[a_spec, b_spec] + [o_spec] * n_ex, out_specs=out_specs,
            out_shape=out_shape, scratch_shapes=scratch, compiler_params=_params(("arbitrary",) * 3))
        return (outs[0] if n_out == 1 else outs), landed
    outs = _pcall(
        body, name=name, grid=grid,
        in_specs=[a_spec, b_spec] + [o_spec] * n_ex + [_ANY] * n_into,
        out_specs=out_specs, out_shape=out_shape,
        input_output_aliases={2 + n_ex: 0} if n_into else {},
        scratch_shapes=scratch,
        compiler_params=_params(("parallel", "parallel", "arbitrary")),
    )(a, b, *extras, *([into] if n_into else []))
    return outs[0] if n_out == 1 else outs


def rowwise(fn, ins, outs, *, name, accs=(), tm=256):
    rows = ins[0].shape[0]
    tm = min(tm, rows)
    n_in, n_out, n_acc = len(ins), len(outs), len(accs)

    def body(*refs):
        in_refs, out_refs, acc_refs = refs[:n_in], refs[n_in:n_in + n_out], refs[n_in + n_out:]
        res, sums = fn(*[r[...] for r in in_refs])
        for o_ref, v in zip(out_refs, res):
            o_ref[...] = v.astype(o_ref.dtype)
        if n_acc:
            @pl.when(pl.program_id(0) == 0)
            def _():
                for a_ref in acc_refs:
                    a_ref[...] = jnp.zeros_like(a_ref)

            for a_ref, s in zip(acc_refs, sums):
                a_ref[...] += s

    def spec(arr):
        if arr.shape[0] == rows:
            return pl.BlockSpec((tm, arr.shape[1]), lambda i: (i, 0))
        return pl.BlockSpec(arr.shape, lambda i: (0, 0))

    res = _pcall(
        body, name=name, grid=(rows // tm,),
        in_specs=[spec(a) for a in ins],
        out_specs=[pl.BlockSpec((tm, w), lambda i: (i, 0)) for w, _ in outs]
        + [pl.BlockSpec((1, w), lambda i: (0, 0)) for w in accs],
        out_shape=[jax.ShapeDtypeStruct((rows, w), dt) for w, dt in outs]
        + [jax.ShapeDtypeStruct((1, w), F32) for w in accs],
        compiler_params=_params(("arbitrary",) if n_acc else ("parallel",)),
    )(*ins)
    return res


def _relu2(v):
    r = jnp.maximum(v, 0.0)
    return r * r


def _gelu(x):
    c = math.sqrt(2.0 / math.pi)
    return 0.5 * x * (1.0 + jnp.tanh(c * (x + 0.044715 * x * x * x)))


def _gelu_grad(x):
    c = math.sqrt(2.0 / math.pi)
    t = jnp.tanh(c * (x + 0.044715 * x * x * x))
    return 0.5 * (1.0 + t) + 0.5 * x * (1.0 - t * t) * c * (1.0 + 3 * 0.044715 * x * x)


def _sigmoid(x):
    return 1.0 / (1.0 + jnp.exp(-x))


def ln_fwd(h, mix, g, b, name):
    def fn(h, mix, g, b):
        r = DN_ALPHA * h + mix
        mu = jnp.mean(r, axis=-1, keepdims=True)
        xc = r - mu
        var = jnp.mean(xc * xc, axis=-1, keepdims=True)
        y = xc * lax.rsqrt(var + LN_EPS) * g + b
        return (y, y), ()
    return rowwise(fn, (h, mix, g, b), ((D_MODEL, F32), (D_MODEL, BF16)), name=name)


def ln_bwd(h, mix, g, dy, name):
    def fn(h, mix, g, dy):
        r = DN_ALPHA * h + mix
        mu = jnp.mean(r, axis=-1, keepdims=True)
        xc = r - mu
        var = jnp.mean(xc * xc, axis=-1, keepdims=True)
        rstd = lax.rsqrt(var + LN_EPS)
        xhat = xc * rstd
        dxh = dy * g
        m1 = jnp.mean(dxh, axis=-1, keepdims=True)
        m2 = jnp.mean(dxh * xhat, axis=-1, keepdims=True)
        dr = rstd * (dxh - m1 - xhat * m2)
        return (dr, dr), (jnp.sum(dy * xhat, axis=0, keepdims=True), jnp.sum(dy, axis=0, keepdims=True))
    return rowwise(fn, (h, mix, g, dy), ((D_MODEL, F32), (D_MODEL, BF16)), accs=(D_MODEL, D_MODEL), name=name)


def _rms(x, g):
    r = lax.rsqrt(jnp.mean(x * x, axis=-1, keepdims=True) + RMS_EPS)
    return x * r * g


def _rms_bwd(x, g, dy):
    r = lax.rsqrt(jnp.mean(x * x, axis=-1, keepdims=True) + RMS_EPS)
    xn = x * r
    dyg = dy * g
    dx = r * (dyg - xn * jnp.mean(dyg * xn, axis=-1, keepdims=True))
    return dx, jnp.sum(dy * xn, axis=0, keepdims=True)


def _s5_disc(lr, li, ldt):
    dt = jnp.exp(ldt)
    mag = jnp.exp(lr * dt)
    cs, sn = jnp.cos(li * dt), jnp.sin(li * dt)
    ar, ai = mag * cs, mag * sn
    inv = 1.0 / (lr * lr + li * li)
    n_re = (ar - 1.0) * lr + ai * li
    n_im = ai * lr - (ar - 1.0) * li
    return dt, mag, cs, sn, ar, ai, inv, n_re, n_im


def s5_prep(lr, li, ldt, b_re, b_im):
    def fn(lr, li, ldt, b_re, b_im):
        _, _, _, _, ar, ai, inv, n_re, n_im = _s5_disc(lr, li, ldt)
        cr, ci = n_re * inv, n_im * inv
        return (ar, ai, cr * b_re - ci * b_im, cr * b_im + ci * b_re), ()
    return rowwise(fn, (lr, li, ldt, b_re, b_im), ((1, F32), (1, F32), (SSM_GROUP, F32), (SSM_GROUP, F32)),
                   name="s5_prep", tm=512)


def s5_prep_bwd(lr, li, ldt, b_re, b_im, dar, dai, dbb_re, dbb_im):
    def fn(lr, li, ldt, b_re, b_im, dar, dai, dbb_re, dbb_im):
        dt, mag, cs, sn, ar, ai, inv, n_re, n_im = _s5_disc(lr, li, ldt)
        cr, ci = n_re * inv, n_im * inv
        db_re = cr * dbb_re + ci * dbb_im
        db_im = cr * dbb_im - ci * dbb_re
        dcr = jnp.sum(dbb_re * b_re + dbb_im * b_im, axis=-1, keepdims=True)
        dci = jnp.sum(dbb_im * b_re - dbb_re * b_im, axis=-1, keepdims=True)
        dar = dar + (dcr * lr - dci * li) * inv
        dai = dai + (dcr * li + dci * lr) * inv
        dinv = dcr * n_re + dci * n_im
        dlr = (dcr * (ar - 1.0) + dci * ai) * inv - 2.0 * lr * inv * inv * dinv
        dli = (dcr * ai - dci * (ar - 1.0)) * inv - 2.0 * li * inv * inv * dinv
        dmag = dar * cs + dai * sn
        dth = dai * ar - dar * ai
        dlr = dlr + dmag * mag * dt
        dli = dli + dth * dt
        ddt = dmag * mag * lr + dth * li
        return (dlr, dli, ddt * dt, db_re, db_im), ()
    return rowwise(fn, (lr, li, ldt, b_re, b_im, dar, dai, dbb_re, dbb_im),
                   ((1, F32), (1, F32), (1, F32), (SSM_GROUP, F32), (SSM_GROUP, F32)), name="s5_prep_bwd", tm=512)


def group_sum(x):
    def body(x_ref, o_ref):
        o_ref[...] = jnp.sum(x_ref[...], axis=1)
    return _pcall(body, name="s5_group_sum", out_shape=jax.ShapeDtypeStruct((N_GROUPS, 1), F32))(
        x.reshape(N_GROUPS, SSM_STATE, 1))


GROUPS_PER_TILE = LANES // SSM_GROUP
TILE_STATES = GROUPS_PER_TILE * SSM_STATE
N_UTILES = D_MODEL // LANES
TILES_PER_UTILE = TILE_STATES // LANES


SUBLANES = 8
SCAN_STRIP = 1024
N_STRIPS = N_STATES // SCAN_STRIP
_NT = (((1,), (1,)), ((), ()))
_TN = (((0,), (0,)), ((), ()))


def _scan_coefs(are, aim, shifted, reverse):
    ar = are[...]
    ai = -aim[...] if reverse else aim[...]
    powers = {1: (ar, ai)}
    for d in (2, 4):
        r, i = powers[d // 2]
        powers[d] = (r * r - i * i, 2.0 * r * i)
    rid = lax.broadcasted_iota(jnp.int32, (SUBLANES, N_STATES), 0)
    first = (rid == SUBLANES - 1) if reverse else (rid == 0)
    masks = [(1, first)] + [(d, (rid <= SUBLANES - 1 - d) if reverse else (rid >= d)) for d in (1, 2, 4)]
    for n, (d, keep) in enumerate(masks):
        for part in (0, 1):
            shifted[2 * n + part][...] = jnp.where(keep, jnp.broadcast_to(powers[d][part], (SUBLANES, N_STATES)), 0.0)


def _tile_scan(xr, xi, shifted, nbr_re, nbr_im, reverse):
    for n, d in enumerate((1, 1, 2, 4)):
        by = SUBLANES - d if reverse else d
        fr, fi = (nbr_re, nbr_im) if n == 0 else (xr, xi)
        sr, si = pltpu.roll(fr, by, 0), pltpu.roll(fi, by, 0)
        kr, ki = shifted[2 * n], shifted[2 * n + 1]
        xr, xi = xr + kr * sr - ki * si, xi + kr * si + ki * sr
    return xr, xi


def _tile_rows(t):
    return pl.ds(pl.multiple_of(t * SUBLANES, SUBLANES), SUBLANES)


def s5_fwd(u, bbd_re, bbd_im, cbd_re, cbd_imn, a_re, a_im, dskip, ride=None, t_rows=256):
    seq = u.shape[0]
    t_rows = min(t_rows, seq)
    n_tiles = t_rows // SUBLANES

    def body(u_ref, bre, bim, cre, cimn, are, aim, d_ref, y_ref, hre_ref, him_ref, car_re, car_im, *shifted):
        @pl.when(pl.program_id(0) == 0)
        def _():
            car_re[...] = jnp.zeros_like(car_re)
            car_im[...] = jnp.zeros_like(car_im)
            _scan_coefs(are, aim, shifted, reverse=False)

        uf = u_ref[...]
        ub = uf.astype(BF16)
        for j in range(N_UTILES):
            uj = ub[:, LANES * j:LANES * (j + 1)]
            sl = slice(TILE_STATES * j, TILE_STATES * (j + 1))
            hre_ref[:, sl] = jnp.dot(uj, bre[j], preferred_element_type=F32)
            him_ref[:, sl] = jnp.dot(uj, bim[j], preferred_element_type=F32)
        for s in range(N_STRIPS):
            cols = pl.ds(s * SCAN_STRIP, SCAN_STRIP)
            coefs = [c[:, cols] for c in shifted]

            def step(t, before):
                rows = _tile_rows(t)
                hr, hi = _tile_scan(hre_ref[rows, cols], him_ref[rows, cols], coefs, before[0], before[1], False)
                hre_ref[rows, cols] = hr
                him_ref[rows, cols] = hi
                return hr, hi

            cr, ci = lax.fori_loop(0, n_tiles, step, (car_re[:, cols], car_im[:, cols]))
            car_re[:, cols] = cr
            car_im[:, cols] = ci
        dv = d_ref[...]
        for j in range(N_UTILES):
            st = slice(TILE_STATES * j, TILE_STATES * (j + 1))
            yj = (jnp.dot(hre_ref[:, st].astype(BF16), cre[j], preferred_element_type=F32)
                  + jnp.dot(him_ref[:, st].astype(BF16), cimn[j], preferred_element_type=F32))
            sl = slice(LANES * j, LANES * (j + 1))
            y_ref[:, sl] = yj + dv[:, sl] * uf[:, sl]

    full3 = lambda a: pl.BlockSpec(a.shape, lambda i: (0, 0, 0))
    full2 = lambda a: pl.BlockSpec(a.shape, lambda i: (0, 0))
    tile = pltpu.VMEM((SUBLANES, N_STATES), F32)
    n_chunks = seq // t_rows
    return _pcall_riding(
        body, (u, bbd_re, bbd_im, cbd_re, cbd_imn, a_re, a_im, dskip), ride,
        lambda: pl.program_id(0) == 0, lambda: pl.program_id(0) == n_chunks - 1,
        name="s5_fwd", grid=(n_chunks,),
        in_specs=[pl.BlockSpec((t_rows, D_MODEL), lambda i: (i, 0)), full3(bbd_re), full3(bbd_im), full3(cbd_re),
                  full3(cbd_imn), full2(a_re), full2(a_im), full2(dskip)],
        out_specs=[pl.BlockSpec((t_rows, D_MODEL), lambda i: (i, 0)),
                   pl.BlockSpec((t_rows, N_STATES), lambda i: (i, 0)),
                   pl.BlockSpec((t_rows, N_STATES), lambda i: (i, 0))],
        out_shape=[jax.ShapeDtypeStruct((seq, D_MODEL), F32),
                   jax.ShapeDtypeStruct((seq, N_STATES), F32),
                   jax.ShapeDtypeStruct((seq, N_STATES), F32)],
        scratch_shapes=[tile] * 10,
        compiler_params=_params(("arbitrary",)))


def s5_bwd(dy, u, dres, h_re, h_im, bbd_re, bbd_im, cbd_re, cbd_imn, a_re, a_im, dskip, ride=None, t_rows=128):
    seq = u.shape[0]
    t_rows = min(t_rows, seq)
    n_chunks = seq // t_rows

    n_tiles = t_rows // SUBLANES

    def body(dy_ref, u_ref, dres_ref, hre_ref, him_ref, hpre_ref, hpim_ref, bre, bim, cre, cimn, are, aim, d_ref,
             dx_ref, dbre, dbim, dcre, dcimn, dar_ref, dai_ref, dd_ref, lre, lim, car_re, car_im, acc_re, acc_im,
             *shifted):
        i = pl.program_id(0)

        @pl.when(i == 0)
        def _():
            for r in (car_re, car_im, acc_re, acc_im, dbre, dbim, dcre, dcimn, dd_ref):
                r[...] = jnp.zeros_like(r)
            _scan_coefs(are, aim, shifted, reverse=True)

        dyf = dy_ref[...]
        dyb = dyf.astype(BF16)
        uf = u_ref[...]
        ub = uf.astype(BF16)
        for j in range(N_UTILES):
            dyj = dyb[:, LANES * j:LANES * (j + 1)]
            st = slice(TILE_STATES * j, TILE_STATES * (j + 1))
            lre[:, st] = lax.dot_general(dyj, cre[j], _NT, preferred_element_type=F32)
            lim[:, st] = lax.dot_general(dyj, cimn[j], _NT, preferred_element_type=F32)
        has_pred = (i < n_chunks - 1).astype(F32)
        last_row = lax.broadcasted_iota(jnp.int32, (SUBLANES, SCAN_STRIP), 0) == SUBLANES - 1
        for s in range(N_STRIPS):
            cols = pl.ds(s * SCAN_STRIP, SCAN_STRIP)
            coefs = [c[:, cols] for c in shifted]
            before_re, before_im = hpre_ref[:, cols] * has_pred, hpim_ref[:, cols] * has_pred

            def step(k, carry):
                after_re, after_im, dar, dai = carry
                t = n_tiles - 1 - k
                rows = _tile_rows(t)
                lr, li = _tile_scan(lre[rows, cols], lim[rows, cols], coefs, after_re, after_im, True)
                lre[rows, cols] = lr
                lim[rows, cols] = li
                prev = _tile_rows(jnp.maximum(t - 1, 0))
                pre_re = jnp.where(t == 0, before_re, hre_ref[prev, cols])
                pre_im = jnp.where(t == 0, before_im, him_ref[prev, cols])
                hpr = pltpu.roll(jnp.where(last_row, pre_re, hre_ref[rows, cols]), 1, 0)
                hpi = pltpu.roll(jnp.where(last_row, pre_im, him_ref[rows, cols]), 1, 0)
                return lr, li, dar + lr * hpr + li * hpi, dai + li * hpr - lr * hpi

            cr, ci, dar, dai = lax.fori_loop(0, n_tiles, step, (car_re[:, cols], car_im[:, cols],
                                                               acc_re[:, cols], acc_im[:, cols]))
            car_re[:, cols] = cr
            car_im[:, cols] = ci
            acc_re[:, cols] = dar
            acc_im[:, cols] = dai

        dv = d_ref[...]
        for j in range(N_UTILES):
            sl = slice(LANES * j, LANES * (j + 1))
            st = slice(TILE_STATES * j, TILE_STATES * (j + 1))
            lrj = lre[:, st].astype(BF16)
            lij = lim[:, st].astype(BF16)
            du = (lax.dot_general(lrj, bre[j], _NT, preferred_element_type=F32)
                  + lax.dot_general(lij, bim[j], _NT, preferred_element_type=F32))
            dx_ref[:, sl] = du + dv[:, sl] * dyf[:, sl] + DN_ALPHA * dres_ref[:, sl]
            uj = ub[:, sl]
            dbre[j] += lax.dot_general(uj, lrj, _TN, preferred_element_type=F32)
            dbim[j] += lax.dot_general(uj, lij, _TN, preferred_element_type=F32)
            dyj = dyb[:, sl]
            dcre[j] += lax.dot_general(hre_ref[:, st].astype(BF16), dyj, _TN, preferred_element_type=F32)
            dcimn[j] += lax.dot_general(him_ref[:, st].astype(BF16), dyj, _TN, preferred_element_type=F32)
        dd_ref[...] += jnp.sum(dyf * uf, axis=0, keepdims=True)

        @pl.when(i == n_chunks - 1)
        def _():
            dar_ref[...] = jnp.sum(acc_re[...], axis=0, keepdims=True)
            dai_ref[...] = jnp.sum(acc_im[...], axis=0, keepdims=True)

    rev = lambda i: (n_chunks - 1 - i, 0)
    prev_tile = lambda i: (jnp.maximum((n_chunks - 1 - i) * n_tiles - 1, 0), 0)
    full3 = lambda a: pl.BlockSpec(a.shape, lambda i: (0, 0, 0))
    full2 = lambda a: pl.BlockSpec(a.shape, lambda i: (0, 0))
    acc3 = lambda shape: pl.BlockSpec(shape, lambda i: (0, 0, 0))
    acc2 = lambda shape: pl.BlockSpec(shape, lambda i: (0, 0))
    tile = pltpu.VMEM((SUBLANES, N_STATES), F32)
    return _pcall_riding(
        body, (dy, u, dres, h_re, h_im, h_re, h_im, bbd_re, bbd_im, cbd_re, cbd_imn, a_re, a_im, dskip), ride,
        lambda: pl.program_id(0) == 0, lambda: pl.program_id(0) == n_chunks - 1,
        name="s5_bwd", grid=(n_chunks,),
        in_specs=[pl.BlockSpec((t_rows, D_MODEL), rev), pl.BlockSpec((t_rows, D_MODEL), rev),
                  pl.BlockSpec((t_rows, D_MODEL), rev),
                  pl.BlockSpec((t_rows, N_STATES), rev), pl.BlockSpec((t_rows, N_STATES), rev),
                  pl.BlockSpec((SUBLANES, N_STATES), prev_tile), pl.BlockSpec((SUBLANES, N_STATES), prev_tile),
                  full3(bbd_re), full3(bbd_im), full3(cbd_re), full3(cbd_imn), full2(a_re), full2(a_im), full2(dskip)],
        out_specs=[pl.BlockSpec((t_rows, D_MODEL), rev), acc3(bbd_re.shape), acc3(bbd_im.shape), acc3(cbd_re.shape),
                   acc3(cbd_imn.shape), acc2((1, N_STATES)), acc2((1, N_STATES)), acc2((1, D_MODEL))],
        out_shape=[jax.ShapeDtypeStruct((seq, D_MODEL), F32), jax.ShapeDtypeStruct(bbd_re.shape, F32),
                   jax.ShapeDtypeStruct(bbd_im.shape, F32), jax.ShapeDtypeStruct(cbd_re.shape, F32),
                   jax.ShapeDtypeStruct(cbd_imn.shape, F32), jax.ShapeDtypeStruct((1, N_STATES), F32),
                   jax.ShapeDtypeStruct((1, N_STATES), F32), jax.ShapeDtypeStruct((1, D_MODEL), F32)],
        scratch_shapes=[pltpu.VMEM((t_rows, N_STATES), F32), pltpu.VMEM((t_rows, N_STATES), F32)] + [tile] * 12,
        compiler_params=_params(("arbitrary",)))


def _eye_groups():
    return jnp.eye(GROUPS_PER_TILE, dtype=F32)


def _blockdiag_in(bb):
    t = bb.transpose(0, 2, 1).reshape(N_UTILES, GROUPS_PER_TILE, SSM_GROUP, SSM_STATE)
    bd = jnp.einsum("jgcp,gh->jgchp", t, _eye_groups())
    return bd.reshape(N_UTILES, LANES, TILE_STATES)


def _blockdiag_in_t(d):
    t = jnp.einsum("jgchp,gh->jgcp", d.reshape(N_UTILES, GROUPS_PER_TILE, SSM_GROUP, GROUPS_PER_TILE, SSM_STATE),
                   _eye_groups())
    return t.reshape(N_GROUPS, SSM_GROUP, SSM_STATE).transpose(0, 2, 1)


def _blockdiag_out(c):
    t = c.transpose(0, 2, 1).reshape(N_UTILES, GROUPS_PER_TILE, SSM_STATE, SSM_GROUP)
    bd = jnp.einsum("jhpc,hg->jhpgc", t, _eye_groups())
    return bd.reshape(N_UTILES, TILE_STATES, LANES)


def _blockdiag_out_t(d):
    t = jnp.einsum("jhpgc,hg->jhpc", d.reshape(N_UTILES, GROUPS_PER_TILE, SSM_STATE, GROUPS_PER_TILE, SSM_GROUP),
                   _eye_groups())
    return t.reshape(N_GROUPS, SSM_STATE, SSM_GROUP).transpose(0, 2, 1)


ATT_TQ = 512
ATT_TK = 512
LOG2E = math.log2(math.e)
LN2 = math.log(2.0)
Q_PRESCALE = SM_SCALE * LOG2E


def _loop_in_pairs(n, step, carry, start=0):
    pairs = (n - start) // 2

    def two(t, c):
        return step(start + 2 * t + 1, step(start + 2 * t, c))

    carry = lax.fori_loop(0, pairs, two, carry)
    return lax.fori_loop(start + 2 * pairs, n, step, carry)


def _causal(s, off=0, transposed=False):
    r = lax.broadcasted_iota(jnp.int32, s.shape, 0)
    c = lax.broadcasted_iota(jnp.int32, s.shape, 1)
    keep = (r <= c + off) if transposed else (c <= r + off)
    return jnp.where(keep, s, NEG_INF)


def _q_specs(rows, at):
    def nope(*ids):
        r, h = at(*ids)
        return r, 3 * (h // HEADS_PER_CHIP) + h % HEADS_PER_CHIP

    def rope(*ids):
        r, h = at(*ids)
        return r, 3 * (h // HEADS_PER_CHIP) + HEADS_PER_CHIP

    return [pl.BlockSpec((rows, LANES), nope), pl.BlockSpec((rows, LANES), rope)]


def _kv_specs(rows, at):
    def col(f):
        def index(*ids):
            r, h = at(*ids)
            return r, f(h)
        return index

    return [pl.BlockSpec((rows, LANES), col(lambda h: 2 * h)), pl.BlockSpec((rows, LANES), col(lambda h: h % HEADS_PER_CHIP)),
            pl.BlockSpec((rows, LANES), col(lambda h: 2 * h + 1))]


def _cat(a, b):
    return jnp.concatenate([a, b], axis=1)


def attn_fwd(q, kv, kr, ride=None, tq=ATT_TQ, tk=ATT_TK):
    seq = q.shape[0]
    n_heads = N_HEADS
    tq, tk = min(tq, seq), min(tk, seq)

    def body(qn_ref, qr_ref, kn_ref, kr_ref, v_ref, o_ref, lse_ref):
        qi = pl.program_id(1)
        qv = _cat(qn_ref[...], qr_ref[...])
        jd = (qi * tq) // tk

        def block(j, carry, diag):
            m, l, acc = carry
            rows = pl.ds(pl.multiple_of(j * tk, tk), tk)
            s = lax.dot_general(qv, _cat(kn_ref[rows, :], kr_ref[rows, :]), _NT, preferred_element_type=F32)
            if diag:
                s = _causal(s, qi * tq - jd * tk)
            m_new = jnp.maximum(m, jnp.max(s, axis=-1, keepdims=True))
            p = jnp.exp2(s - m_new)
            corr = jnp.exp2(m - m_new)
            l = l * corr + jnp.sum(p, axis=-1, keepdims=True)
            acc = acc * corr + jnp.dot(p.astype(BF16), v_ref[rows, :], preferred_element_type=F32)
            return m_new, l, acc

        init = (jnp.full((tq, 1), NEG_INF, F32), jnp.zeros((tq, 1), F32), jnp.zeros((tq, V_HEAD), F32))
        carry = _loop_in_pairs(jd, lambda j, c: block(j, c, False), init)
        m, l, acc = block(jd, carry, True)
        o_ref[...] = acc / l
        lse_ref[0] = jnp.broadcast_to(m + jnp.log2(l), (tq, LANES))

    n_q = seq // tq
    return _pcall_riding(
        body, (q, q, kv, kr, kv), ride,
        lambda: (pl.program_id(0) == 0) & (pl.program_id(1) == 0),
        lambda: (pl.program_id(0) == n_heads - 1) & (pl.program_id(1) == n_q - 1),
        name="attn_fwd", grid=(n_heads, n_q),
        in_specs=_q_specs(tq, lambda h, i: (i, h)) + _kv_specs(seq, lambda h, i: (0, h)),
        out_specs=[pl.BlockSpec((tq, V_HEAD), lambda h, i: (i, h)),
                   pl.BlockSpec((1, tq, LANES), lambda h, i: (h, i, 0))],
        out_shape=[jax.ShapeDtypeStruct((seq, n_heads * V_HEAD), F32),
                   jax.ShapeDtypeStruct((n_heads, seq, LANES), F32)],
        compiler_params=_params(("arbitrary", "arbitrary")))


def attn_bwd_dq(q, kv, kr, do, o, lse, tq=ATT_TQ, tk=ATT_TK):
    seq = q.shape[0]
    tq, tk = min(tq, seq), min(tk, seq)
    head = lambda c, i, hh: HEADS_PER_CHIP * c + hh

    def body(qn_ref, qr_ref, kn_ref, kr_ref, v_ref, do_ref, o_ref, lse_ref, dqn_ref, dqr_ref, delta_ref):
        qi = pl.program_id(1)
        qv = _cat(qn_ref[...], qr_ref[...])
        dof = do_ref[...]
        dob = dof.astype(BF16)
        delta = jnp.sum(dof * o_ref[...], axis=-1, keepdims=True)
        lse = lse_ref[0][:, :1]
        jd = (qi * tq) // tk

        def block(j, dq, diag):
            rows = pl.ds(pl.multiple_of(j * tk, tk), tk)
            kv = _cat(kn_ref[rows, :], kr_ref[rows, :])
            s = lax.dot_general(qv, kv, _NT, preferred_element_type=F32)
            if diag:
                s = _causal(s, qi * tq - jd * tk)
            p = jnp.exp2(s - lse)
            dp = lax.dot_general(dob, v_ref[rows, :], _NT, preferred_element_type=F32)
            ds = p * (dp - delta)
            return dq + jnp.dot(ds.astype(BF16), kv, preferred_element_type=F32)

        dq = _loop_in_pairs(jd, lambda j, c: block(j, c, False), jnp.zeros((tq, 2 * LANES), F32))
        dq = block(jd, dq, True) * SM_SCALE
        dqn_ref[...] = dq[:, :LANES]

        @pl.when(pl.program_id(2) == 0)
        def _():
            dqr_ref[...] = dq[:, LANES:]

        @pl.when(pl.program_id(2) > 0)
        def _():
            dqr_ref[...] += dq[:, LANES:]

        delta_ref[0] = jnp.broadcast_to(delta, (tq, LANES))

    by_head = lambda c, i, hh: (i, head(c, i, hh))
    dq_nope, dq_rope, delta = _pcall(
        body, name="attn_bwd_dq", grid=(N_CHIPS, seq // tq, HEADS_PER_CHIP),
        in_specs=_q_specs(tq, by_head) + _kv_specs(seq, lambda c, i, hh: (0, head(c, i, hh)))
        + [pl.BlockSpec((tq, V_HEAD), by_head), pl.BlockSpec((tq, V_HEAD), by_head),
           pl.BlockSpec((1, tq, LANES), lambda c, i, hh: (head(c, i, hh), i, 0))],
        out_specs=[pl.BlockSpec((tq, LANES), lambda c, i, hh: (i, head(c, i, hh))),
                   pl.BlockSpec((tq, LANES), lambda c, i, hh: (i, c)),
                   pl.BlockSpec((1, tq, LANES), lambda c, i, hh: (head(c, i, hh), i, 0))],
        out_shape=[jax.ShapeDtypeStruct((seq, N_HEADS * QK_NOPE), F32),
                   jax.ShapeDtypeStruct((seq, N_CHIPS * LANES), F32),
                   jax.ShapeDtypeStruct((N_HEADS, seq, LANES), F32)],
        compiler_params=_params(("parallel", "parallel", "arbitrary")),
    )(q, q, kv, kr, kv, do, o, lse)
    return dq_nope, dq_rope, delta


def attn_bwd_dkv(q, kv, kr, do, lse_row, delta_row, tq=ATT_TK):
    seq = q.shape[0]
    tq = min(tq, seq)
    n_blk = seq // tq

    def body(qn_ref, qr_ref, kn_ref, kr_ref, v_ref, do_ref, lse_ref, delta_ref, dkv_ref, dkr_ref):
        kj = pl.program_id(1)
        kv = _cat(kn_ref[...], kr_ref[...])
        vv = v_ref[...]

        def block(i, carry, diag):
            dk, dv = carry
            rows = pl.ds(pl.multiple_of(i * tq, tq), tq)
            qv = _cat(qn_ref[rows, :], qr_ref[rows, :])
            st = lax.dot_general(kv, qv, _NT, preferred_element_type=F32)
            if diag:
                st = _causal(st, transposed=True)
            pt = jnp.exp2(st - lse_ref[0, pl.ds(i, 1), :])
            dob = do_ref[rows, :].astype(BF16)
            dv = dv + jnp.dot(pt.astype(BF16), dob, preferred_element_type=F32)
            dpt = lax.dot_general(vv, dob, _NT, preferred_element_type=F32)
            dst = pt * (dpt - delta_ref[0, pl.ds(i, 1), :])
            dk = dk + jnp.dot(dst.astype(BF16), qv, preferred_element_type=F32)
            return dk, dv

        carry = block(kj, (jnp.zeros((tq, 2 * LANES), F32), jnp.zeros((tq, V_HEAD), F32)), True)
        dk, dv = _loop_in_pairs(n_blk, lambda i, c: block(i, c, False), carry, start=kj + 1)
        dk = dk * LN2
        dkv_ref[...] = _cat(dk[:, :LANES], dv).astype(dkv_ref.dtype)
        lane = lax.broadcasted_iota(jnp.int32, (tq, LANES), 1)
        mine = (lane // HALF_ROPE) % HEADS_PER_CHIP == pl.program_id(0) % HEADS_PER_CHIP
        dkr_ref[0] = jnp.where(mine, dk[:, LANES:], 0.0)

    return _pcall(
        body, name="attn_bwd_dkv", grid=(N_HEADS, n_blk),
        in_specs=_q_specs(seq, lambda h, j: (0, h)) + _kv_specs(tq, lambda h, j: (j, h))
        + [pl.BlockSpec((seq, V_HEAD), lambda h, j: (0, h)),
           pl.BlockSpec((1, n_blk, tq), lambda h, j: (h, 0, 0)),
           pl.BlockSpec((1, n_blk, tq), lambda h, j: (h, 0, 0))],
        out_specs=[pl.BlockSpec((tq, QK_NOPE + V_HEAD), lambda h, j: (j, h)),
                   pl.BlockSpec((1, tq, LANES), lambda h, j: (h, j, 0))],
        out_shape=[jax.ShapeDtypeStruct((seq, N_HEADS * (QK_NOPE + V_HEAD)), BF16),
                   jax.ShapeDtypeStruct((N_HEADS, seq, LANES), F32)],
        compiler_params=_params(("parallel", "parallel")),
    )(q, q, kv, kr, kv, do, lse_row, delta_row)


def head_sum(x, ts=512):
    n_heads, seq, w = x.shape
    ts = min(ts, seq)

    def body(x_ref, o_ref):
        o_ref[...] = jnp.sum(x_ref[...], axis=0)

    return _pcall(body, name="head_sum", grid=(seq // ts,),
                  in_specs=[pl.BlockSpec((n_heads, ts, w), lambda i: (0, i, 0))],
                  out_specs=pl.BlockSpec((ts, w), lambda i: (i, 0)),
                  out_shape=jax.ShapeDtypeStruct((seq, w), F32),
                  compiler_params=_params(("parallel",)))(x)


HEADS_PER_CHIP = N_HEADS // N_CHIPS
Q_CHIP = HEADS_PER_CHIP * QK_DIM
Q_CHIP_NOPE = HEADS_PER_CHIP * QK_NOPE


def _perm_q_cols(w):
    t = w.reshape(w.shape[0], HEADS_PER_CHIP, QK_DIM)
    return jnp.concatenate([t[:, :, :QK_NOPE].reshape(w.shape[0], -1),
                            t[:, :, QK_NOPE:QK_NOPE + HALF_ROPE].reshape(w.shape[0], -1),
                            t[:, :, QK_NOPE + HALF_ROPE:].reshape(w.shape[0], -1)], axis=1)


def _unperm_q_cols(w):
    r = w.shape[0]
    nope = w[:, :Q_CHIP_NOPE].reshape(r, HEADS_PER_CHIP, QK_NOPE)
    r1 = w[:, Q_CHIP_NOPE:Q_CHIP_NOPE + QK_ROPE].reshape(r, HEADS_PER_CHIP, HALF_ROPE)
    r2 = w[:, Q_CHIP_NOPE + QK_ROPE:].reshape(r, HEADS_PER_CHIP, HALF_ROPE)
    return jnp.concatenate([nope, r1, r2], axis=2).reshape(r, Q_CHIP)


def _pad_kva_cols(w):
    z = jnp.zeros((w.shape[0], HALF_ROPE), w.dtype)
    return jnp.concatenate([w[:, :KV_LORA], w[:, KV_LORA:KV_LORA + HALF_ROPE], z, w[:, KV_LORA + HALF_ROPE:], z], axis=1)


def _unpad_kva_cols(w):
    return jnp.concatenate([w[:, :KV_LORA], w[:, KV_LORA:KV_LORA + HALF_ROPE],
                            w[:, KV_LORA + QK_ROPE:KV_LORA + QK_ROPE + HALF_ROPE]], axis=1)


def _rope_tile(t, cs, sn):
    return t * cs + pltpu.roll(t, LANES // 2, 1) * sn


def _rope_tile_bwd(d, cs, sn):
    return d * cs + pltpu.roll(d * sn, LANES // 2, 1)


def _b_cols(tk, tn):
    return pl.BlockSpec((None, tk, tn), lambda i, j, k: (j, k, 0))


def _b_cols_t(tk, tn):
    return pl.BlockSpec((None, tn, tk), lambda i, j, k: (k, j, 0))


def _out_cols(shape):
    return shape, lambda tm, tn: pl.BlockSpec((None, tm, tn), lambda i, j, k: (j, i, 0))


def _halves(a):
    return a.reshape(N_CHIPS, 2, a.shape[1] // 2, a.shape[2])


def device_step(x, positions, target, w, comm=None):
    seq = x.shape[0]
    w = dict(w)

    def gathered(names, outs):
        for n, a in zip(names, outs):
            if isinstance(n, tuple):
                w[n[0]] = [a.reshape(v.shape) if l == n[1] else v for l, v in enumerate(w[n[0]])]
            else:
                w[n] = a.reshape(w[n].shape)

    def ride_for(names):
        if comm is None:
            return None
        return GatherRide([_halves(w[n[0]][n[1]] if isinstance(n, tuple) else w[n]) for n in names])

    first_ride = ("ssm_w_glu", "ssm_w_out", ("w_ff1", 0), ("w_ff2", 0), "kv_w_a", "kv_w_b", "q_w_a", "q_w_b", "attn_w_o")
    second_ride = (("w_ff1", 1), ("w_ff2", 1))

    inv_freq = ROPE_THETA ** (-jnp.arange(HALF_ROPE, dtype=F32) / HALF_ROPE)
    ang = positions.astype(F32)[:, None] * inv_freq
    cos, sin = jnp.cos(ang), jnp.sin(ang)
    zero = jnp.zeros_like(cos)
    cos_q, sin_q = jnp.concatenate([cos] * 4, 1), jnp.concatenate([-sin, -sin, sin, sin], 1)
    cos_k, sin_k = jnp.concatenate([cos, zero, cos, zero], 1), jnp.concatenate([-sin, zero, sin, zero], 1)
    ff_tile = D_FF // N_CHIPS
    pack_shape = (N_CHIPS, EARLY_ROWS, PACK_W)

    lr = w["ssm_lam_re"].reshape(N_STATES, 1)
    li = w["ssm_lam_im"].reshape(N_STATES, 1)
    ldt = jnp.repeat(w["ssm_log_dt"].reshape(N_GROUPS), SSM_STATE).reshape(N_STATES, 1)
    b_re = w["ssm_b_re"].reshape(N_STATES, SSM_GROUP)
    b_im = w["ssm_b_im"].reshape(N_STATES, SSM_GROUP)
    a_re, a_im, bb_re, bb_im = s5_prep(lr, li, ldt, b_re, b_im)
    a_re, a_im = a_re.reshape(1, N_STATES), a_im.reshape(1, N_STATES)
    bbd_re = _blockdiag_in(bb_re.reshape(N_GROUPS, SSM_STATE, SSM_GROUP)).astype(BF16)
    bbd_im = _blockdiag_in(bb_im.reshape(N_GROUPS, SSM_STATE, SSM_GROUP)).astype(BF16)
    cbd_re = _blockdiag_out(w["ssm_c_re"].reshape(N_GROUPS, SSM_GROUP, SSM_STATE)).astype(BF16)
    cbd_imn = _blockdiag_out(-w["ssm_c_im"].reshape(N_GROUPS, SSM_GROUP, SSM_STATE)).astype(BF16)
    dskip = w["ssm_d"].reshape(1, D_MODEL)
    (ypre, h_re, h_im), landed = s5_fwd(x, bbd_re, bbd_im, cbd_re, cbd_imn, a_re, a_im, dskip, ride_for(first_ride))
    gathered(first_ride, landed)
    (yg,) = rowwise(lambda y: ((_gelu(y),), ()), (ypre,), ((D_MODEL, BF16),), name="gelu")
    w_glu = w["ssm_w_glu"]
    glu_tile = w_glu.shape[2]
    vg = mm(yg, w_glu, n_dim=2 * D_MODEL, tiles=(None, glu_tile, None), b_view=_b_cols, name="glu_proj")

    def glu(v):
        return (v[:, :D_MODEL] * _sigmoid(v[:, D_MODEL:]),), ()
    (z,) = rowwise(glu, (vg,), ((D_MODEL, BF16),), name="glu")
    w_out = w["ssm_w_out"].reshape(D_MODEL, D_MODEL)
    mix0 = mm(z, w_out, name="ssm_out")

    def mlp_fwd(hb, layer):
        pre = mm(hb, w["w_ff1"][layer], n_dim=D_FF, tiles=(None, ff_tile, None), b_view=_b_cols, name=f"ff1_{layer}",
                 out_dtypes=(BF16,))
        f = mm(pre, w["w_ff2"][layer].reshape(D_FF, D_MODEL), pro_a=_relu2, name=f"ff2_{layer}")
        return pre, f

    ln = lambda name, l: w[name][l].reshape(1, D_MODEL)
    h1, h1b = ln_fwd(x, mix0, ln("ln_mix_g", 0), ln("ln_mix_b", 0), "ln_mix_0")
    f1pre, f1 = mlp_fwd(h1b, 0)
    h2, h2b = ln_fwd(h1, f1, ln("ln_ffn_g", 0), ln("ln_ffn_b", 0), "ln_ffn_0")

    kv_w_a = w["kv_w_a"].reshape(D_MODEL, KVA_PAD)
    kv_w_b = w["kv_w_b"]
    q_w_a = w["q_w_a"].reshape(D_MODEL, Q_LORA)
    q_w_b = w["q_w_b"]
    w_o = w["attn_w_o"].reshape(D_MODEL, D_MODEL)
    kvb_tile = kv_w_b.shape[2]
    kvn_g = w["kv_norm_g"].reshape(1, KV_LORA)
    qn_g = w["q_norm_g"].reshape(1, Q_LORA)
    kva = mm(h2b, kv_w_a, name="kv_a")

    def kv_post(kva, g, cs, sn):
        tile = _rope_tile(kva[:, KV_LORA:], cs, sn)
        return (_rms(kva[:, :KV_LORA], g), _cat(tile, pltpu.roll(tile, HALF_ROPE, 1))), ()
    ckv, krope = rowwise(kv_post, (kva, kvn_g, cos_k, sin_k), ((KV_LORA, BF16), (2 * LANES, BF16)), name="kv_post")
    kvb = mm(ckv, kv_w_b, n_dim=N_CHIPS * kvb_tile, tiles=(None, kvb_tile, KV_LORA), b_view=_b_cols, name="kv_b",
             out_dtypes=(BF16,))
    cq_raw = mm(h2b, q_w_a, name="q_a")
    (cq,) = rowwise(lambda c, g: ((_rms(c, g),), ()), (cq_raw, qn_g), ((Q_LORA, BF16),), name="q_norm")
    qlin = mm(cq, q_w_b, n_dim=N_CHIPS * Q_CHIP, tiles=(None, Q_CHIP, Q_LORA), b_view=_b_cols, name="q_b")

    def on_rope_tiles(fn, scale=None):
        def apply(q, cs, sn):
            parts = []
            for k in range(N_CHIPS):
                parts.append(q[:, Q_CHIP * k:Q_CHIP * k + Q_CHIP_NOPE])
                parts.append(fn(q[:, Q_CHIP * k + Q_CHIP_NOPE:Q_CHIP * (k + 1)], cs, sn))
            out = jnp.concatenate(parts, axis=1)
            return (out if scale is None else out * scale,), ()
        return apply
    (qro,) = rowwise(on_rope_tiles(_rope_tile, Q_PRESCALE), (qlin, cos_q, sin_q), ((N_CHIPS * Q_CHIP, BF16),),
                     name="q_rope")
    (o, lse), landed = attn_fwd(qro, kvb, krope, ride_for(second_ride))
    gathered(second_ride, landed)
    mix1 = mm(o, w_o, name="attn_out")
    h3, h3b = ln_fwd(h2, mix1, ln("ln_mix_g", 1), ln("ln_mix_b", 1), "ln_mix_1")
    f2pre, f2 = mlp_fwd(h3b, 1)
    h4, _ = ln_fwd(h3, f2, ln("ln_ffn_g", 1), ln("ln_ffn_b", 1), "ln_ffn_1")

    def loss_fn(y, t):
        e = y - t
        return (e * (1.0 / D_MODEL),), (jnp.broadcast_to(jnp.sum(e * e), (1, LANES)),)
    dh4, loss_acc = rowwise(loss_fn, (h4, target), ((D_MODEL, F32),), accs=(LANES,), name="loss")
    loss = loss_acc[0, 0] * (0.5 / D_MODEL)

    g = {}

    def into_rows(off, rows_per_chip, shape=pack_shape):
        def view(tm, tn):
            nb = rows_per_chip // tm
            return pl.BlockSpec((None, tm, tn), lambda i, j, k: (i // nb, off // tm + i % nb, 0))
        return shape, view

    def into_cols(off):
        return pack_shape, lambda tm, tn: pl.BlockSpec((None, tm, tn), lambda i, j, k: (j, off // tm + i, 0))

    def mlp_bwd(pack, dr, drb, hb, pre, layer, swap=False):
        dpre = mm(drb, w["w_ff2"][layer].reshape(D_FF, D_MODEL), tb=True, epi=lambda r, p: (r * 2.0 * jnp.maximum(p, 0.0),),
                  extras=(pre,), out_dtypes=(BF16,), tiles=(None, ff_tile, None), name=f"ff2_dx_{layer}")
        pack = mm(pre, drb, ta=True, pro_a=_relu2, name=f"ff2_dw_{layer}", tiles=(None, PACK_W, None), into=pack,
                  out_view=into_rows(EARLY_OFF["w_ff2"] + layer * ff_tile, ff_tile))
        pack = mm(hb, dpre, ta=True, name=f"ff1_dw_{layer}", tiles=(None, PACK_W, None), into=pack,
                  out_view=into_cols(EARLY_OFF["w_ff1"] + layer * D_MODEL))
        dh = mm(dpre, w["w_ff1"][layer], tb=True, epi=lambda r, d: (r + DN_ALPHA * d,), extras=(dr,), n_dim=D_MODEL,
                tiles=(None, D_MODEL, ff_tile), b_view=_b_cols_t, name=f"ff1_dx_{layer}",
                ride=SwapRide(pack) if swap else None)
        return (pack, *dh) if swap else (pack, dh)

    dr4, dr4b, dg_f1, db_f1 = ln_bwd(h3, f2, ln("ln_ffn_g", 1), dh4, "ln_ffn_bwd_1")
    pack, dh3 = mlp_bwd(None, dr4, dr4b, h3b, f2pre, 1)
    dr3, dr3b, dg_m1, db_m1 = ln_bwd(h2, mix1, ln("ln_mix_g", 1), dh3, "ln_mix_bwd_1")
    shard_rows = D_MODEL // N_CHIPS
    pack = mm(o, dr3b, ta=True, name="attn_out_dw", tiles=(shard_rows, PACK_W, None), into=pack,
              out_view=into_rows(EARLY_OFF["attn_w_o"], shard_rows))
    do = mm(dr3b, w_o, tb=True, name="attn_out_dx")
    dqn, dqr, delta = attn_bwd_dq(qro, kvb, krope, do, o, lse)
    tb = min(ATT_TK, seq)
    lse_row = lse[:, :, 0].reshape(N_HEADS, seq // tb, tb)
    delta_row = delta[:, :, 0].reshape(N_HEADS, seq // tb, tb)
    dkvb, dkr = attn_bwd_dkv(qro, kvb, krope, do, lse_row, delta_row)

    def q_rope_bwd(dn, dr, cs, sn):
        parts = []
        for k in range(N_CHIPS):
            parts.append(dn[:, Q_CHIP_NOPE * k:Q_CHIP_NOPE * (k + 1)])
            parts.append(_rope_tile_bwd(dr[:, LANES * k:LANES * (k + 1)], cs, sn))
        return (jnp.concatenate(parts, axis=1),), ()
    (dqlin,) = rowwise(q_rope_bwd, (dqn, dqr, cos_q, sin_q), ((N_CHIPS * Q_CHIP, BF16),), name="q_rope_bwd")
    g["q_w_b"] = mm(cq, dqlin, ta=True, name="q_b_dw", tiles=(Q_LORA, Q_CHIP, None), out_view=_out_cols(q_w_b.shape))
    dcq = mm(dqlin, q_w_b, tb=True, n_dim=Q_LORA, tiles=(None, Q_LORA, Q_CHIP), b_view=_b_cols_t, name="q_b_dx")

    def q_norm_bwd(c, gq, d):
        dx, dgq = _rms_bwd(c, gq, d)
        return (dx,), (dgq,)
    dcq_raw, dqn_g = rowwise(q_norm_bwd, (cq_raw, qn_g, dcq), ((Q_LORA, BF16),), accs=(Q_LORA,), name="q_norm_bwd")
    g["q_w_a"] = mm(h2b, dcq_raw, ta=True, name="q_a_dw")
    g["kv_w_b"] = mm(ckv, dkvb, ta=True, name="kv_b_dw", tiles=(KV_LORA, kvb_tile, None), out_view=_out_cols(kv_w_b.shape))
    dckv = mm(dkvb, kv_w_b, tb=True, n_dim=KV_LORA, tiles=(None, KV_LORA, kvb_tile), b_view=_b_cols_t, name="kv_b_dx")
    dkr_sum = head_sum(dkr)

    def kv_post_bwd(kva, gk, dc, dk, cs, sn):
        dx, dgk = _rms_bwd(kva[:, :KV_LORA], gk, dc)
        dk = dk + pltpu.roll(dk, LANES - HALF_ROPE, 1)
        return (jnp.concatenate([dx, _rope_tile_bwd(dk, cs, sn)], axis=1),), (dgk,)
    dkva, dkvn_g = rowwise(kv_post_bwd, (kva, kvn_g, dckv, dkr_sum, cos_k, sin_k), ((KVA_PAD, BF16),),
                           accs=(KV_LORA,), name="kv_post_bwd")
    g["kv_w_a"] = mm(h2b, dkva, ta=True, name="kv_a_dw")
    dh2 = mm(dcq_raw, q_w_a, tb=True, epi=lambda r, d: (r + DN_ALPHA * d,), extras=(dr3,), name="q_a_dx")
    dh2 = mm(dkva, kv_w_a, tb=True, epi=lambda r, d: (r + d,), extras=(dh2,), name="kv_a_dx")

    dr2, dr2b, dg_f0, db_f0 = ln_bwd(h1, f1, ln("ln_ffn_g", 0), dh2, "ln_ffn_bwd_0")
    pack = put_rows(pack, packed_shards(g, MISC_EARLY, EARLY_ROWS - MISC_EARLY_OFF), MISC_EARLY_OFF)
    early_ride = None
    if comm is None:
        pack, dh1 = mlp_bwd(pack, dr2, dr2b, h1b, f1pre, 0)
    else:
        pack, dh1, (theirs,) = mlp_bwd(pack, dr2, dr2b, h1b, f1pre, 0, swap=True)
        early_ride = SendRide(add_halves(pack, theirs, comm[1]))
    dr1, dr1b, dg_m0, db_m0 = ln_bwd(x, mix0, ln("ln_mix_g", 0), dh1, "ln_mix_bwd_0")
    late = mm(z, dr1b, ta=True, name="ssm_out_dw", tiles=(shard_rows, PACK_W, None),
              out_view=into_rows(LATE_OFF["ssm_w_out"], shard_rows, (N_CHIPS, LATE_ROWS, PACK_W)))
    dz = mm(dr1b, w_out, tb=True, name="ssm_out_dx")

    def glu_bwd(v, dz):
        val, sg = v[:, :D_MODEL], _sigmoid(v[:, D_MODEL:])
        return (jnp.concatenate([dz * sg, dz * val * sg * (1.0 - sg)], axis=1),), ()
    (dvg,) = rowwise(glu_bwd, (vg, dz), ((2 * D_MODEL, BF16),), name="glu_bwd")
    g["ssm_w_glu"] = mm(yg, dvg, ta=True, name="glu_proj_dw", tiles=(None, glu_tile, None), out_view=_out_cols(w_glu.shape))
    dypre = mm(dvg, w_glu, tb=True, epi=lambda r, y: (r * _gelu_grad(y),), extras=(ypre,), n_dim=D_MODEL,
               tiles=(None, D_MODEL, glu_tile), b_view=_b_cols_t, name="glu_proj_dx")
    (dx, dbbd_re, dbbd_im, dcbd_re, dcbd_imn, dar, dai, dd), got_early = s5_bwd(
        dypre, x, dr1, h_re, h_im, bbd_re, bbd_im, cbd_re, cbd_imn, a_re, a_im, dskip, early_ride)
    dbb_re = _blockdiag_in_t(dbbd_re).reshape(N_STATES, SSM_GROUP)
    dbb_im = _blockdiag_in_t(dbbd_im).reshape(N_STATES, SSM_GROUP)
    dlr, dli, dldt, db_re, db_im = s5_prep_bwd(lr, li, ldt, b_re, b_im, dar.reshape(N_STATES, 1),
                                               dai.reshape(N_STATES, 1), dbb_re, dbb_im)
    g["ssm_lam_re"] = dlr.reshape(1, N_GROUPS, SSM_STATE)
    g["ssm_lam_im"] = dli.reshape(1, N_GROUPS, SSM_STATE)
    g["ssm_log_dt"] = group_sum(dldt).reshape(1, N_GROUPS)
    g["ssm_b_re"] = db_re.reshape(1, N_GROUPS, SSM_STATE, SSM_GROUP)
    g["ssm_b_im"] = db_im.reshape(1, N_GROUPS, SSM_STATE, SSM_GROUP)
    g["ssm_c_re"] = _blockdiag_out_t(dcbd_re).reshape(1, N_GROUPS, SSM_GROUP, SSM_STATE)
    g["ssm_c_im"] = -_blockdiag_out_t(dcbd_imn).reshape(1, N_GROUPS, SSM_GROUP, SSM_STATE)
    g["ssm_d"] = dd
    g["ln_mix_g"] = jnp.concatenate([dg_m0, dg_m1], 0)
    g["ln_mix_b"] = jnp.concatenate([db_m0, db_m1], 0)
    g["ln_ffn_g"] = jnp.concatenate([dg_f0, dg_f1], 0)
    g["ln_ffn_b"] = jnp.concatenate([db_f0, db_f1], 0)
    g["kv_norm_g"] = dkvn_g.reshape(KV_LORA)
    g["q_norm_g"] = dqn_g
    return loss, dx, pack, late, g, (early_ride.ins[0], got_early[0]) if comm is not None else None


def place(shard, me_idx, dtype, name):
    rows, cols = shard.shape
    tr = _tile(rows, (512, 256, 128))

    def body(m_ref, x_ref, o_ref):
        o_ref[...] = x_ref[...].astype(o_ref.dtype)

    return _pcall(
        body, name=name,
        grid_spec=pltpu.PrefetchScalarGridSpec(
            num_scalar_prefetch=1, grid=(rows // tr,),
            in_specs=[pl.BlockSpec((tr, cols), lambda i, m: (i, 0))],
            out_specs=pl.BlockSpec((None, tr, cols), lambda i, m: (m[0], i, 0))),
        out_shape=jax.ShapeDtypeStruct((N_CHIPS, rows, cols), dtype),
        compiler_params=_params(("parallel",)),
    )(me_idx, shard)


def put_rows(pack, rows, off):
    _, n, cols = rows.shape
    tr = math.gcd(math.gcd(off, n), 512)

    def body(r_ref, p_ref, o_ref):
        o_ref[...] = r_ref[...]

    return _pcall(body, name="grad_put_rows", grid=(N_CHIPS, n // tr),
                  in_specs=[pl.BlockSpec((None, tr, cols), lambda k, i: (k, i, 0)), _ANY],
                  out_specs=pl.BlockSpec((None, tr, cols), lambda k, i: (k, off // tr + i, 0)),
                  out_shape=jax.ShapeDtypeStruct(pack.shape, pack.dtype), input_output_aliases={1: 0},
                  compiler_params=_params(("parallel", "parallel")))(rows, pack)


def _my_cols(c, mine=True):
    start = (c if mine else 1 - c) * HALF_W
    return pl.ds(pl.multiple_of(start, HALF_W), HALF_W)


def add_halves(gpack, got, c_idx):
    n, rows, _ = gpack.shape
    blk = (None, G_BLOCK_ROWS, HALF_W)

    def body(c_ref, g_ref, r_ref, o_ref):
        o_ref[...] = (g_ref[...] + r_ref[...]).astype(o_ref.dtype)

    return _pcall(
        body, name="grad_add_halves",
        grid_spec=pltpu.PrefetchScalarGridSpec(
            num_scalar_prefetch=1, grid=(n, rows // G_BLOCK_ROWS),
            in_specs=[pl.BlockSpec(blk, lambda k, i, c: (k, i, c[0])), pl.BlockSpec(blk, lambda k, i, c: (k, i, 0))],
            out_specs=pl.BlockSpec(blk, lambda k, i, c: (k, i, 0))),
        out_shape=jax.ShapeDtypeStruct((n, rows, HALF_W), BF16),
        compiler_params=_params(("parallel", "parallel")),
    )(c_idx, gpack, got)


def sum_owner(part, got, idx, total_rows, row_off=0, into=None):
    _, rows, _ = part.shape
    tr = G_BLOCK_ROWS
    n_into = 0 if into is None else 1

    def body(m_ref, p_ref, g_ref, *rest):
        up = lambda v: v.astype(F32)
        rest[-1][...] = ((up(p_ref[...]) + up(g_ref[0])) + up(g_ref[1])) + up(g_ref[2])

    return _pcall(
        body, name="grad_sum_owner",
        grid_spec=pltpu.PrefetchScalarGridSpec(
            num_scalar_prefetch=1, grid=(rows // tr,),
            in_specs=[pl.BlockSpec((None, tr, HALF_W), lambda i, m: (m[0], i, 0)),
                      pl.BlockSpec((3, tr, HALF_W), lambda i, m: (0, i, 0))] + [_ANY] * n_into,
            out_specs=pl.BlockSpec((tr, HALF_W), lambda i, m: (row_off // tr + i, m[1]))),
        out_shape=jax.ShapeDtypeStruct((total_rows, PACK_W), F32),
        input_output_aliases={3: 0} if n_into else {},
        compiler_params=_params(("parallel",)),
    )(idx, part, got, *([into] if n_into else []))


def join_halves(red):
    def body(in_ref, out_ref, send_sem, recv_sem):
        x, y, c, _ = _place()
        sibling = (x, y, 1 - c)
        mine = out_ref.at[:, _my_cols(c)]
        cp = pltpu.make_async_remote_copy(src_ref=mine, dst_ref=mine, send_sem=send_sem, recv_sem=recv_sem,
                                          device_id=sibling, device_id_type=MESH)
        cp.start()
        cp.wait_send()
        other = out_ref.at[:, _my_cols(c, mine=False)]
        pltpu.make_async_remote_copy(src_ref=other, dst_ref=other, send_sem=send_sem, recv_sem=recv_sem,
                                     device_id=sibling, device_id_type=MESH).wait_recv()

    return _pcall(body, name="grad_join_halves", in_specs=[_ANY], out_specs=_ANY,
                  out_shape=jax.ShapeDtypeStruct(red.shape, red.dtype), input_output_aliases={0: 0},
                  scratch_shapes=[pltpu.SemaphoreType.DMA, pltpu.SemaphoreType.DMA])(red)


def adamw(gsrc, g_off, wt, m, v, name):
    n, cols = wt.shape
    tr = math.gcd(math.gcd(g_off, n), 256) if g_off else math.gcd(n, 256)
    off_blk = g_off // tr
    c1 = 1.0 / (1.0 - ADAM_B1 ** ADAM_STEP)
    c2 = 1.0 / (1.0 - ADAM_B2 ** ADAM_STEP)

    def body(g_ref, w_ref, m_ref, v_ref, go_ref, d_ref, mo_ref, vo_ref):
        gv = g_ref[...]
        mn = ADAM_B1 * m_ref[...] + (1.0 - ADAM_B1) * gv
        vn = ADAM_B2 * v_ref[...] + (1.0 - ADAM_B2) * gv * gv
        go_ref[...] = gv
        mo_ref[...] = mn
        vo_ref[...] = vn
        d_ref[...] = -ADAM_LR * ((mn * c1) / (jnp.sqrt(vn * c2) + ADAM_EPS) + ADAM_WD * w_ref[...])

    blk = pl.BlockSpec((tr, cols), lambda i: (i, 0))
    return _pcall(body, name=name, grid=(n // tr,),
                  in_specs=[pl.BlockSpec((tr, cols), lambda i: (off_blk + i, 0)), blk, blk, blk],
                  out_specs=[blk] * 4, out_shape=[jax.ShapeDtypeStruct((n, cols), F32)] * 4,
                  compiler_params=_params(("parallel",)))(gsrc, wt, m, v)


def _rows8(a):
    return -(-a.size // (8 * PACK_W)) * 8


def _as_rows(a, rows=None):
    flat = a.reshape(-1)
    n = _rows8(a) if rows is None else rows
    return jnp.pad(flat, (0, n * PACK_W - flat.shape[0])).reshape(n, PACK_W)


def local_shards_2d(wl):
    return {"w_ff1": [wl["w_ff1"][0], wl["w_ff1"][1]], "w_ff2": [wl["w_ff2"][0], wl["w_ff2"][1]],
            "ssm_w_glu": wl["ssm_w_glu"], "ssm_w_out": wl["ssm_w_out"], "kv_w_a": _pad_kva_cols(wl["kv_w_a"]),
            "kv_w_b": wl["kv_w_b"], "q_w_a": wl["q_w_a"], "q_w_b": _perm_q_cols(wl["q_w_b"]),
            "attn_w_o": wl["attn_w_o"], "ssm_d": wl["ssm_d"].reshape(2, -1)}


def misc_grad_shard(name, g, k):
    if name == "ssm_d":
        w = D_MODEL // N_CHIPS
        return g[:, w * k:w * (k + 1)]
    if name in ("ssm_w_glu", "kv_w_b"):
        return g[k]
    if name == "q_w_b":
        return _unperm_q_cols(g[k])
    rows = D_MODEL // N_CHIPS
    shard = g[rows * k:rows * (k + 1)]
    return _unpad_kva_cols(shard) if name == "kv_w_a" else shard


def packed_shards(g, names, rows, tail=None):
    blocks = []
    for k in range(N_CHIPS):
        parts = [_as_rows(misc_grad_shard(n, g[n], k), MISC_SHARD_ROWS[n]) for n in names]
        if tail is not None:
            parts.append(tail[k * (tail.shape[0] // N_CHIPS):(k + 1) * (tail.shape[0] // N_CHIPS)])
        blk = jnp.concatenate(parts, axis=0)
        blocks.append(jnp.pad(blk, ((0, rows - blk.shape[0]), (0, 0))))
    return jnp.stack(blocks)


def kernel(x, positions, ln_mix_g, ln_mix_b, ln_ffn_g, ln_ffn_b, w_ff1, w_ff2, ssm_lam_re, ssm_lam_im, ssm_log_dt, ssm_b_re, ssm_b_im, ssm_c_re, ssm_c_im, ssm_d, ssm_w_glu, ssm_w_out, kv_w_a, kv_norm_g, kv_w_b, q_w_a, q_norm_g, q_w_b, attn_w_o, loss_target, m_ln_mix_g, m_ln_mix_b, m_ln_ffn_g, m_ln_ffn_b, m_w_ff1, m_w_ff2, m_ssm_lam_re, m_ssm_lam_im, m_ssm_log_dt, m_ssm_b_re, m_ssm_b_im, m_ssm_c_re, m_ssm_c_im, m_ssm_d, m_ssm_w_glu, m_ssm_w_out, m_kv_w_a, m_kv_norm_g, m_kv_w_b, m_q_w_a, m_q_norm_g, m_q_w_b, m_attn_w_o, v_ln_mix_g, v_ln_mix_b, v_ln_ffn_g, v_ln_ffn_b, v_w_ff1, v_w_ff2, v_ssm_lam_re, v_ssm_lam_im, v_ssm_log_dt, v_ssm_b_re, v_ssm_b_im, v_ssm_c_re, v_ssm_c_im, v_ssm_d, v_ssm_w_glu, v_ssm_w_out, v_kv_w_a, v_kv_norm_g, v_kv_w_b, v_q_w_a, v_q_norm_g, v_q_w_b, v_attn_w_o):
    env = dict(locals())
    wl = {n: env[n] for n in WEIGHTS}
    ml = {n: env["m_" + n] for n in WEIGHTS}
    vl = {n: env["v_" + n] for n in WEIGHTS}
    for n in ("ssm_w_glu", "ssm_w_out", "q_w_a", "q_w_b", "attn_w_o"):
        wl[n], ml[n], vl[n] = wl[n][0], ml[n][0], vl[n][0]

    c_idx = lax.axis_index("c").astype(jnp.int32).reshape(1)
    me_idx = (2 * lax.axis_index("x") + lax.axis_index("y")).astype(jnp.int32).reshape(1)

    local = local_shards_2d(wl)
    put = lambda a, n: place(a, me_idx, F32 if n == "ssm_d" else BF16, "place_" + n)
    stacked = {n: [put(a, f"{n}_{l}") for l, a in enumerate(local[n])] if isinstance(local[n], list) else put(local[n], n)
               for n in SHARDED}
    stacked["ssm_d"] = ride_alone(GatherRide([_halves(stacked["ssm_d"])]), "ssm_d_all_gather")[0].reshape(1, D_MODEL)
    for n in REPLICATED:
        stacked[n] = wl[n]

    loss_part, dx, early, late, g, (early_sums, early_got) = device_step(
        x[0], positions[0], loss_target[0], stacked, comm=(me_idx, c_idx))
    loss = lax.psum(loss_part, ("x", "y", "c"))

    small = jnp.concatenate([_as_rows(g[n]) for n in REPLICATED], axis=0)
    small = jnp.pad(small, ((0, SMALL_ROWS - small.shape[0]), (0, 0)))
    late = put_rows(late, packed_shards(g, MISC_LATE, LATE_ROWS - MISC_LATE_OFF, tail=small), MISC_LATE_OFF)
    late_sums = add_halves(late, ride_alone(SwapRide(late), "grad_swap_halves")[0], c_idx)
    late_got = ride_alone(SendRide(late_sums), "grad_send_to_owners")[0]
    where = jnp.concatenate([me_idx, c_idx])
    total_rows = EARLY_ROWS + LATE_ROWS
    reduced = sum_owner(early_sums, early_got, where, total_rows)
    reduced = join_halves(sum_owner(late_sums, late_got, where, total_rows, row_off=EARLY_ROWS, into=reduced))
    quarter = reduced[EARLY_ROWS + SMALL_OFF:EARLY_ROWS + SMALL_OFF + SMALL_Q_ROWS]
    small_tot = ride_alone(GatherRide([_halves(place(quarter, me_idx, F32, "place_small_grads"))]),
                           "small_grad_all_gather")[0].reshape(SMALL_ROWS, PACK_W)

    out_g, out_d, out_m, out_v = {}, {}, {}, {}
    direct = {**EARLY_OFF, **{n: EARLY_ROWS + o for n, o in LATE_OFF.items()}}
    for n, off in direct.items():
        res = adamw(reduced, off, wl[n].reshape(-1, PACK_W), ml[n].reshape(-1, PACK_W), vl[n].reshape(-1, PACK_W),
                    "adamw_" + n)
        out_g[n], out_d[n], out_m[n], out_v[n] = [a.reshape(env[n].shape) for a in res]
    late_offs = [EARLY_ROWS + MISC_LATE_OFF + sum(MISC_SHARD_ROWS[m] for m in MISC_LATE[:i]) for i in range(len(MISC_LATE))]
    for names, off in [(MISC_EARLY, MISC_EARLY_OFF)] + [((n,), o) for n, o in zip(MISC_LATE, late_offs)]:
        pack3 = lambda d: jnp.concatenate([_as_rows(d[n], MISC_SHARD_ROWS[n]) for n in names], axis=0)
        res = adamw(reduced, off, pack3(wl), pack3(ml), pack3(vl), "adamw_packed_" + names[0])
        r0 = 0
        for n in names:
            cnt = math.prod(env[n].shape)
            out_g[n], out_d[n], out_m[n], out_v[n] = [
                a[r0:r0 + MISC_SHARD_ROWS[n]].reshape(-1)[:cnt].reshape(env[n].shape) for a in res]
            r0 += MISC_SHARD_ROWS[n]
    ws = jnp.concatenate([_as_rows(wl[n]) for n in REPLICATED], axis=0)
    ms = jnp.concatenate([_as_rows(ml[n]) for n in REPLICATED], axis=0)
    vs = jnp.concatenate([_as_rows(vl[n]) for n in REPLICATED], axis=0)
    pad = ((0, SMALL_ROWS - ws.shape[0]), (0, 0))
    res = adamw(small_tot, 0, jnp.pad(ws, pad), jnp.pad(ms, pad), jnp.pad(vs, pad), "adamw_replicated")
    row = 0
    for n in REPLICATED:
        cnt = math.prod(env[n].shape)
        nrows = _rows8(env[n])
        out_g[n], out_d[n], out_m[n], out_v[n] = [a[row:row + nrows].reshape(-1)[:cnt].reshape(env[n].shape) for a in res]
        row += nrows

    return (loss, dx[None], *[out_g[n] for n in WEIGHTS], *[out_d[n] for n in WEIGHTS],
            *[out_m[n] for n in WEIGHTS], *[out_v[n] for n in WEIGHTS])
```

```python
import functools
import math

import jax
import jax.numpy as jnp
from jax import lax
from jax.experimental import pallas as pl
from jax.experimental.pallas import tpu as pltpu

F32 = jnp.float32
BF16 = jnp.bfloat16
MESH = pl.DeviceIdType.MESH

D_MODEL = 1024
DEPTH = 2
SSM_GROUP = 16
N_GROUPS = D_MODEL // SSM_GROUP
SSM_STATE = 64
N_STATES = N_GROUPS * SSM_STATE
N_HEADS = 8
QK_NOPE = 128
QK_ROPE = 64
HALF_ROPE = QK_ROPE // 2
V_HEAD = 128
QK_DIM = QK_NOPE + QK_ROPE
Q_LORA = 384
KV_LORA = 256
ROPE_THETA = 10000.0
SM_SCALE = QK_DIM ** -0.5
NEG_INF = -1e30
D_FF = 4 * D_MODEL
DN_ALPHA = (2 * DEPTH) ** 0.25
LN_EPS = 1e-5
RMS_EPS = 1e-6
ADAM_LR = 0.001
ADAM_B1 = 0.9
ADAM_B2 = 0.999
ADAM_EPS = 1e-08
ADAM_WD = 0.01
ADAM_STEP = 10

N_CHIPS = 4
LANES = 128
VMEM_LIMIT = 56 * 1024 * 1024
MM_VMEM_BUDGET = 40 * 1024 * 1024
PACK_W = 1024
KVA_PAD = 384
HALF_W = PACK_W // 2

SHARDED = ("w_ff1", "w_ff2", "ssm_w_glu", "ssm_w_out", "kv_w_a", "kv_w_b", "q_w_a", "q_w_b", "attn_w_o", "ssm_d")
G_BLOCK_ROWS = 960
EARLY_OFF = {"w_ff1": 0, "w_ff2": 2048, "attn_w_o": 4096}
MISC_EARLY = ("kv_w_b", "kv_w_a", "q_w_a", "q_w_b")
MISC_EARLY_OFF = 4352
EARLY_ROWS = 5 * G_BLOCK_ROWS
LATE_OFF = {"ssm_w_out": 0}
SMALL_Q_ROWS = 96
SMALL_ROWS = N_CHIPS * SMALL_Q_ROWS
MISC_LATE = ("ssm_w_glu", "ssm_d")
MISC_LATE_OFF = 256
SMALL_OFF = MISC_LATE_OFF + 528
LATE_ROWS = G_BLOCK_ROWS
MISC_SHARD_ROWS = {"ssm_d": 16, "ssm_w_glu": 512, "kv_w_b": 128, "kv_w_a": 80, "q_w_a": 96, "q_w_b": 144}
REPLICATED = ("ln_mix_g", "ln_mix_b", "ln_ffn_g", "ln_ffn_b", "ssm_lam_re", "ssm_lam_im", "ssm_log_dt",
              "ssm_b_re", "ssm_b_im", "ssm_c_re", "ssm_c_im", "kv_norm_g", "q_norm_g")
WEIGHTS = ("ln_mix_g", "ln_mix_b", "ln_ffn_g", "ln_ffn_b", "w_ff1", "w_ff2", "ssm_lam_re", "ssm_lam_im",
           "ssm_log_dt", "ssm_b_re", "ssm_b_im", "ssm_c_re", "ssm_c_im", "ssm_d", "ssm_w_glu", "ssm_w_out",
           "kv_w_a", "kv_norm_g", "kv_w_b", "q_w_a", "q_norm_g", "q_w_b", "attn_w_o")


def _pcall(body, **kw):
    return pl.pallas_call(body, **kw)


def _params(sem=None):
    return pltpu.CompilerParams(dimension_semantics=sem, vmem_limit_bytes=VMEM_LIMIT)


_ANY = pl.BlockSpec(memory_space=pl.ANY)


def _tile(dim, prefs):
    for p in prefs:
        if dim % p == 0:
            return p
    return dim


def _place():
    x, y, c = lax.axis_index("x"), lax.axis_index("y"), lax.axis_index("c")
    return x, y, c, [(1 - x, y), (x, 1 - y), (1 - x, 1 - y)]


def _remote(k, src, dst, to, send_sems, recv_sems):
    return pltpu.make_async_remote_copy(src_ref=src, dst_ref=dst, send_sem=send_sems.at[k], recv_sem=recv_sems.at[k],
                                        device_id=to, device_id_type=MESH)


class GatherRide:
    def __init__(self, arrs):
        self.ins = list(arrs)
        self.out_shapes = [jax.ShapeDtypeStruct(a.shape, a.dtype) for a in arrs]
        self.aliases = {i: i for i in range(len(arrs))}
        self.n_sems = 6 * len(arrs)

    def start(self, ins, outs, send_sems, recv_sems):
        x, y, c, chips = _place()
        me = 2 * x + y
        for a, o in enumerate(outs):
            for j, (px, py) in enumerate(chips):
                _remote(6 * a + j, o.at[me, c], o.at[me, c], (px, py, c), send_sems, recv_sems).start()

    def pass_on(self, ins, outs, send_sems, recv_sems):
        x, y, c, chips = _place()
        for a, o in enumerate(outs):
            for j, (px, py) in enumerate(chips):
                blk = o.at[2 * px + py, c]
                _remote(6 * a + j, blk, blk, (px, py, c), send_sems, recv_sems).wait_recv()
                _remote(6 * a + 3 + j, blk, blk, (x, y, 1 - c), send_sems, recv_sems).start()

    def finish(self, ins, outs, send_sems, recv_sems, passed_on=False):
        if not passed_on:
            self.pass_on(ins, outs, send_sems, recv_sems)
        x, y, c, chips = _place()
        me = 2 * x + y
        sibling = (x, y, 1 - c)
        for a, o in enumerate(outs):
            for j, (px, py) in enumerate(chips):
                blk = o.at[2 * px + py, 1 - c]
                _remote(6 * a + 3 + j, blk, blk, sibling, send_sems, recv_sems).wait_recv()
                _remote(6 * a + j, o.at[me, c], o.at[me, c], (px, py, c), send_sems, recv_sems).wait_send()
                mine = o.at[2 * px + py, c]
                _remote(6 * a + 3 + j, mine, mine, sibling, send_sems, recv_sems).wait_send()


class SendRide:
    def __init__(self, part):
        self.ins = [part]
        self.out_shapes = [jax.ShapeDtypeStruct((3,) + part.shape[1:], part.dtype)]
        self.aliases = {}
        self.n_sems = 3

    def _copies(self, ins, outs, send_sems, recv_sems):
        x, y, c, chips = _place()
        return [_remote(j, ins[0].at[2 * px + py], outs[0].at[j], (px, py, c), send_sems, recv_sems)
                for j, (px, py) in enumerate(chips)]

    def start(self, ins, outs, send_sems, recv_sems):
        for cp in self._copies(ins, outs, send_sems, recv_sems):
            cp.start()

    def finish(self, ins, outs, send_sems, recv_sems):
        for cp in self._copies(ins, outs, send_sems, recv_sems):
            cp.wait()


class SwapRide:
    def __init__(self, pack):
        self.ins = [pack]
        self.out_shapes = [jax.ShapeDtypeStruct(pack.shape[:2] + (HALF_W,), pack.dtype)]
        self.aliases = {}
        self.n_sems = 1

    def _copy(self, ins, outs, send_sems, recv_sems):
        x, y, c, _ = _place()
        return _remote(0, ins[0].at[:, :, _my_cols(c, mine=False)], outs[0], (x, y, 1 - c), send_sems, recv_sems)

    def start(self, ins, outs, send_sems, recv_sems):
        self._copy(ins, outs, send_sems, recv_sems).start()

    def finish(self, ins, outs, send_sems, recv_sems):
        self._copy(ins, outs, send_sems, recv_sems).wait()


def _pcall_riding(body, args, ride, first, last, *, in_specs, out_specs, out_shape, scratch_shapes=(), middle=None,
                  **kw):
    n_in, n_out = len(args), len(out_shape)
    if ride is None:
        return _pcall(body, in_specs=in_specs, out_specs=out_specs, out_shape=out_shape,
                      scratch_shapes=list(scratch_shapes), **kw)(*args), []
    k_in, k_out = len(ride.ins), len(ride.out_shapes)

    def riding(*refs):
        ins, r_in = refs[:n_in], refs[n_in:n_in + k_in]
        outs = refs[n_in + k_in:n_in + k_in + n_out]
        r_out = refs[n_in + k_in + n_out:n_in + k_in + n_out + k_out]
        scratch, (send_sems, recv_sems) = refs[n_in + k_in + n_out + k_out:-2], refs[-2:]

        @pl.when(first())
        def _():
            ride.start(r_in, r_out, send_sems, recv_sems)

        if middle is not None:
            @pl.when(middle())
            def _():
                ride.pass_on(r_in, r_out, send_sems, recv_sems)

        body(*ins, *outs, *scratch)

        @pl.when(last())
        def _():
            if middle is not None:
                ride.finish(r_in, r_out, send_sems, recv_sems, passed_on=True)
            else:
                ride.finish(r_in, r_out, send_sems, recv_sems)

    res = _pcall(riding, in_specs=list(in_specs) + [_ANY] * k_in, out_specs=list(out_specs) + [_ANY] * k_out,
                 out_shape=list(out_shape) + ride.out_shapes,
                 input_output_aliases={n_in + i: n_out + o for i, o in ride.aliases.items()},
                 scratch_shapes=list(scratch_shapes) + [pltpu.SemaphoreType.DMA((ride.n_sems,))] * 2,
                 **kw)(*args, *ride.ins)
    return res[:n_out], res[n_out:]


def ride_alone(ride, name):
    def body(*refs):
        n = len(ride.ins)
        ins, outs, (send_sems, recv_sems) = refs[:n], refs[n:-2], refs[-2:]
        ride.start(ins, outs, send_sems, recv_sems)
        ride.finish(ins, outs, send_sems, recv_sems)

    return _pcall(body, name=name, in_specs=[_ANY] * len(ride.ins), out_specs=[_ANY] * len(ride.out_shapes),
                  out_shape=ride.out_shapes, input_output_aliases=dict(ride.aliases),
                  scratch_shapes=[pltpu.SemaphoreType.DMA((ride.n_sems,))] * 2)(*ride.ins)


def mm(a, b, *, name, ta=False, tb=False, pro_a=None, epi=None, extras=(), out_dtypes=(F32,), n_dim=None,
       tiles=(None, None, None), b_view=None, out_view=None, into=None, ride=None):
    if ta:
        k_dim, m_dim = a.shape
    else:
        m_dim, k_dim = a.shape
    if n_dim is None:
        n_dim = b.shape[0] if tb else b.shape[1]
    tn = tiles[1] or (n_dim if n_dim <= 1024 else _tile(n_dim, (1024, 512, 256, 128)))
    tk = tiles[2] or (k_dim if k_dim <= 1024 else _tile(k_dim, (1024, 512, 256, 128)))
    nk = k_dim // tk

    def vmem_bytes(tm):
        blocks = tm * tk * a.dtype.itemsize + tk * tn * b.dtype.itemsize
        blocks += tm * tn * (sum(e.dtype.itemsize for e in extras) + sum(jnp.dtype(d).itemsize for d in out_dtypes))
        return 2 * blocks + tm * tn * 4

    tm = tiles[0] or next((t for t in (4096, 2048, 1024, 512, 256) if m_dim % t == 0 and vmem_bytes(t) <= MM_VMEM_BUDGET),
                          _tile(m_dim, (128,)))
    assert m_dim % tm == 0 and n_dim % tn == 0 and k_dim % tk == 0, (name, m_dim, n_dim, k_dim, tm, tn, tk)
    n_ex, n_out = len(extras), len(out_dtypes)
    n_into = 0 if into is None else 1
    dims = (((0 if ta else 1,), (1 if tb else 0,)), ((), ()))

    def body(a_ref, b_ref, *rest):
        ex_refs, out_refs = rest[:n_ex], rest[n_ex + n_into:n_ex + n_into + n_out]

        def partial():
            av = a_ref[...]
            if pro_a is not None:
                av = pro_a(av)
            return lax.dot_general(av.astype(BF16), b_ref[...].astype(BF16), dims, preferred_element_type=F32)

        def finish(r):
            res = epi(r, *[e[...] for e in ex_refs]) if epi is not None else (r,)
            for o_ref, v in zip(out_refs, res):
                o_ref[...] = v.astype(o_ref.dtype)

        if nk == 1:
            finish(partial())
            return
        acc = rest[-1]
        k = pl.program_id(2)

        @pl.when(k == 0)
        def _():
            acc[...] = partial()

        @pl.when(k > 0)
        def _():
            acc[...] += partial()

        @pl.when(k == nk - 1)
        def _():
            finish(acc[...])

    a_spec = pl.BlockSpec((tk, tm), lambda i, j, k: (k, i)) if ta else pl.BlockSpec((tm, tk), lambda i, j, k: (i, k))
    if b_view is not None:
        b_spec = b_view(tk, tn)
    else:
        b_spec = pl.BlockSpec((tn, tk), lambda i, j, k: (j, k)) if tb else pl.BlockSpec((tk, tn), lambda i, j, k: (k, j))
    o_spec = pl.BlockSpec((tm, tn), lambda i, j, k: (i, j))
    if out_view is None:
        out_specs = [o_spec] * n_out
        out_shape = [jax.ShapeDtypeStruct((m_dim, n_dim), dt) for dt in out_dtypes]
    else:
        assert n_out == 1
        out_specs = [out_view[1](tm, tn)]
        out_shape = [jax.ShapeDtypeStruct(out_view[0], out_dtypes[0])]
    grid = (m_dim // tm, n_dim // tn, nk)
    scratch = [pltpu.VMEM((tm, tn), F32)] if nk > 1 else []
    if ride is not None:
        assert into is None
        at = lambda ids: functools.reduce(jnp.logical_and, [pl.program_id(d) == i for d, i in enumerate(ids)])
        outs, landed = _pcall_riding(
            body, (a, b, *extras), ride, lambda: at((0, 0, 0)), lambda: at([g - 1 for g in grid]),
            name=name, grid=grid, in_specs=[a_spec, b_spec] + [o_spec] * n_ex, out_specs=out_specs,
            out_shape=out_shape, scratch_shapes=scratch, compiler_params=_params(("arbitrary",) * 3))
        return (outs[0] if n_out == 1 else outs), landed
    outs = _pcall(
        body, name=name, grid=grid,
        in_specs=[a_spec, b_spec] + [o_spec] * n_ex + [_ANY] * n_into,
        out_specs=out_specs, out_shape=out_shape,
        input_output_aliases={2 + n_ex: 0} if n_into else {},
        scratch_shapes=scratch,
        compiler_params=_params(("parallel", "parallel", "arbitrary")),
    )(a, b, *extras, *([into] if n_into else []))
    return outs[0] if n_out == 1 else outs


def rowwise(fn, ins, outs, *, name, accs=(), tm=256):
    rows = ins[0].shape[0]
    tm = min(tm, rows)
    n_in, n_out, n_acc = len(ins), len(outs), len(accs)

    def body(*refs):
        in_refs, out_refs, acc_refs = refs[:n_in], refs[n_in:n_in + n_out], refs[n_in + n_out:]
        res, sums = fn(*[r[...] for r in in_refs])
        for o_ref, v in zip(out_refs, res):
            o_ref[...] = v.astype(o_ref.dtype)
        if n_acc:
            @pl.when(pl.program_id(0) == 0)
            def _():
                for a_ref in acc_refs:
                    a_ref[...] = jnp.zeros_like(a_ref)

            for a_ref, s in zip(acc_refs, sums):
                a_ref[...] += s

    def spec(arr):
        if arr.shape[0] == rows:
            return pl.BlockSpec((tm, arr.shape[1]), lambda i: (i, 0))
        return pl.BlockSpec(arr.shape, lambda i: (0, 0))

    res = _pcall(
        body, name=name, grid=(rows // tm,),
        in_specs=[spec(a) for a in ins],
        out_specs=[pl.BlockSpec((tm, w), lambda i: (i, 0)) for w, _ in outs]
        + [pl.BlockSpec((1, w), lambda i: (0, 0)) for w in accs],
        out_shape=[jax.ShapeDtypeStruct((rows, w), dt) for w, dt in outs]
        + [jax.ShapeDtypeStruct((1, w), F32) for w in accs],
        compiler_params=_params(("arbitrary",) if n_acc else ("parallel",)),
    )(*ins)
    return res


def _relu2(v):
    r = jnp.maximum(v, 0.0)
    return r * r


def _gelu(x):
    c = math.sqrt(2.0 / math.pi)
    return 0.5 * x * (1.0 + jnp.tanh(c * (x + 0.044715 * x * x * x)))


def _gelu_grad(x):
    c = math.sqrt(2.0 / math.pi)
    t = jnp.tanh(c * (x + 0.044715 * x * x * x))
    return 0.5 * (1.0 + t) + 0.5 * x * (1.0 - t * t) * c * (1.0 + 3 * 0.044715 * x * x)


def _sigmoid(x):
    return 1.0 / (1.0 + jnp.exp(-x))


def _layer_norm(h, mix, g, b):
    r = DN_ALPHA * h + mix
    mu = jnp.mean(r, axis=-1, keepdims=True)
    xc = r - mu
    var = jnp.mean(xc * xc, axis=-1, keepdims=True)
    return xc * lax.rsqrt(var + LN_EPS) * g + b


def ln_fwd(h, mix, g, b, name):
    def fn(h, mix, g, b):
        y = _layer_norm(h, mix, g, b)
        return (y, y), ()
    return rowwise(fn, (h, mix, g, b), ((D_MODEL, F32), (D_MODEL, BF16)), name=name)


def ln_bwd(h, mix, g, dy, name):
    def fn(h, mix, g, dy):
        r = DN_ALPHA * h + mix
        mu = jnp.mean(r, axis=-1, keepdims=True)
        xc = r - mu
        var = jnp.mean(xc * xc, axis=-1, keepdims=True)
        rstd = lax.rsqrt(var + LN_EPS)
        xhat = xc * rstd
        dxh = dy * g
        m1 = jnp.mean(dxh, axis=-1, keepdims=True)
        m2 = jnp.mean(dxh * xhat, axis=-1, keepdims=True)
        dr = rstd * (dxh - m1 - xhat * m2)
        return (dr, dr), (jnp.sum(dy * xhat, axis=0, keepdims=True), jnp.sum(dy, axis=0, keepdims=True))
    return rowwise(fn, (h, mix, g, dy), ((D_MODEL, F32), (D_MODEL, BF16)), accs=(D_MODEL, D_MODEL), name=name)


def _rms(x, g):
    r = lax.rsqrt(jnp.mean(x * x, axis=-1, keepdims=True) + RMS_EPS)
    return x * r * g


def _rms_bwd(x, g, dy):
    r = lax.rsqrt(jnp.mean(x * x, axis=-1, keepdims=True) + RMS_EPS)
    xn = x * r
    dyg = dy * g
    dx = r * (dyg - xn * jnp.mean(dyg * xn, axis=-1, keepdims=True))
    return dx, jnp.sum(dy * xn, axis=0, keepdims=True)


def _s5_disc(lr, li, ldt):
    dt = jnp.exp(ldt)
    mag = jnp.exp(lr * dt)
    cs, sn = jnp.cos(li * dt), jnp.sin(li * dt)
    ar, ai = mag * cs, mag * sn
    inv = 1.0 / (lr * lr + li * li)
    n_re = (ar - 1.0) * lr + ai * li
    n_im = ai * lr - (ar - 1.0) * li
    return dt, mag, cs, sn, ar, ai, inv, n_re, n_im


def s5_prep(lr, li, ldt, b_re, b_im):
    def fn(lr, li, ldt, b_re, b_im):
        _, _, _, _, ar, ai, inv, n_re, n_im = _s5_disc(lr, li, ldt)
        cr, ci = n_re * inv, n_im * inv
        return (ar, ai, cr * b_re - ci * b_im, cr * b_im + ci * b_re), ()
    return rowwise(fn, (lr, li, ldt, b_re, b_im), ((1, F32), (1, F32), (SSM_GROUP, F32), (SSM_GROUP, F32)),
                   name="s5_prep", tm=512)


def s5_prep_bwd(lr, li, ldt, b_re, b_im, dar, dai, dbb_re, dbb_im):
    def fn(lr, li, ldt, b_re, b_im, dar, dai, dbb_re, dbb_im):
        dt, mag, cs, sn, ar, ai, inv, n_re, n_im = _s5_disc(lr, li, ldt)
        cr, ci = n_re * inv, n_im * inv
        db_re = cr * dbb_re + ci * dbb_im
        db_im = cr * dbb_im - ci * dbb_re
        dcr = jnp.sum(dbb_re * b_re + dbb_im * b_im, axis=-1, keepdims=True)
        dci = jnp.sum(dbb_im * b_re - dbb_re * b_im, axis=-1, keepdims=True)
        dar = dar + (dcr * lr - dci * li) * inv
        dai = dai + (dcr * li + dci * lr) * inv
        dinv = dcr * n_re + dci * n_im
        dlr = (dcr * (ar - 1.0) + dci * ai) * inv - 2.0 * lr * inv * inv * dinv
        dli = (dcr * ai - dci * (ar - 1.0)) * inv - 2.0 * li * inv * inv * dinv
        dmag = dar * cs + dai * sn
        dth = dai * ar - dar * ai
        dlr = dlr + dmag * mag * dt
        dli = dli + dth * dt
        ddt = dmag * mag * lr + dth * li
        return (dlr, dli, ddt * dt, db_re, db_im), ()
    return rowwise(fn, (lr, li, ldt, b_re, b_im, dar, dai, dbb_re, dbb_im),
                   ((1, F32), (1, F32), (1, F32), (SSM_GROUP, F32), (SSM_GROUP, F32)), name="s5_prep_bwd", tm=512)


def group_sum(x):
    def body(x_ref, o_ref):
        o_ref[...] = jnp.sum(x_ref[...], axis=1)
    return _pcall(body, name="s5_group_sum", out_shape=jax.ShapeDtypeStruct((N_GROUPS, 1), F32))(
        x.reshape(N_GROUPS, SSM_STATE, 1))


GROUPS_PER_TILE = LANES // SSM_GROUP
TILE_STATES = GROUPS_PER_TILE * SSM_STATE
N_UTILES = D_MODEL // LANES
TILES_PER_UTILE = TILE_STATES // LANES


SUBLANES = 8
SCAN_STRIP = 1024
N_STRIPS = N_STATES // SCAN_STRIP
_NT = (((1,), (1,)), ((), ()))
_TN = (((0,), (0,)), ((), ()))


def _scan_coefs(are, aim, shifted, reverse):
    ar = are[...]
    ai = -aim[...] if reverse else aim[...]
    powers = {1: (ar, ai)}
    for d in (2, 4):
        r, i = powers[d // 2]
        powers[d] = (r * r - i * i, 2.0 * r * i)
    rid = lax.broadcasted_iota(jnp.int32, (SUBLANES, N_STATES), 0)
    first = (rid == SUBLANES - 1) if reverse else (rid == 0)
    masks = [(1, first)] + [(d, (rid <= SUBLANES - 1 - d) if reverse else (rid >= d)) for d in (1, 2, 4)]
    for n, (d, keep) in enumerate(masks):
        for part in (0, 1):
            shifted[2 * n + part][...] = jnp.where(keep, jnp.broadcast_to(powers[d][part], (SUBLANES, N_STATES)), 0.0)


def _tile_scan(xr, xi, shifted, nbr_re, nbr_im, reverse):
    for n, d in enumerate((1, 1, 2, 4)):
        by = SUBLANES - d if reverse else d
        fr, fi = (nbr_re, nbr_im) if n == 0 else (xr, xi)
        sr, si = pltpu.roll(fr, by, 0), pltpu.roll(fi, by, 0)
        kr, ki = shifted[2 * n], shifted[2 * n + 1]
        xr, xi = xr + kr * sr - ki * si, xi + kr * si + ki * sr
    return xr, xi


def _tile_rows(t):
    return pl.ds(pl.multiple_of(t * SUBLANES, SUBLANES), SUBLANES)


def s5_fwd(u, bbd_re, bbd_im, cbd_re, cbd_imn, a_re, a_im, dskip, ride=None, t_rows=256):
    seq = u.shape[0]
    t_rows = min(t_rows, seq)
    n_tiles = t_rows // SUBLANES

    def body(u_ref, bre, bim, cre, cimn, are, aim, d_ref, y_ref, gelu_ref, hre_ref, him_ref, car_re, car_im, *shifted):
        @pl.when(pl.program_id(0) == 0)
        def _():
            car_re[...] = jnp.zeros_like(car_re)
            car_im[...] = jnp.zeros_like(car_im)
            _scan_coefs(are, aim, shifted, reverse=False)

        uf = u_ref[...]
        ub = uf.astype(BF16)
        for j in range(N_UTILES):
            uj = ub[:, LANES * j:LANES * (j + 1)]
            sl = slice(TILE_STATES * j, TILE_STATES * (j + 1))
            hre_ref[:, sl] = jnp.dot(uj, bre[j], preferred_element_type=F32)
            him_ref[:, sl] = jnp.dot(uj, bim[j], preferred_element_type=F32)
        for s in range(N_STRIPS):
            cols = pl.ds(s * SCAN_STRIP, SCAN_STRIP)
            coefs = [c[:, cols] for c in shifted]

            def step(t, before):
                rows = _tile_rows(t)
                hr, hi = _tile_scan(hre_ref[rows, cols], him_ref[rows, cols], coefs, before[0], before[1], False)
                hre_ref[rows, cols] = hr
                him_ref[rows, cols] = hi
                return hr, hi

            cr, ci = lax.fori_loop(0, n_tiles, step, (car_re[:, cols], car_im[:, cols]))
            car_re[:, cols] = cr
            car_im[:, cols] = ci
        dv = d_ref[...]
        for j in range(N_UTILES):
            st = slice(TILE_STATES * j, TILE_STATES * (j + 1))
            yj = (jnp.dot(hre_ref[:, st].astype(BF16), cre[j], preferred_element_type=F32)
                  + jnp.dot(him_ref[:, st].astype(BF16), cimn[j], preferred_element_type=F32))
            sl = slice(LANES * j, LANES * (j + 1))
            yj = yj + dv[:, sl] * uf[:, sl]
            y_ref[:, sl] = yj
            gelu_ref[:, sl] = _gelu(yj).astype(gelu_ref.dtype)

    full3 = lambda a: pl.BlockSpec(a.shape, lambda i: (0, 0, 0))
    full2 = lambda a: pl.BlockSpec(a.shape, lambda i: (0, 0))
    tile = pltpu.VMEM((SUBLANES, N_STATES), F32)
    n_chunks = seq // t_rows
    return _pcall_riding(
        body, (u, bbd_re, bbd_im, cbd_re, cbd_imn, a_re, a_im, dskip), ride,
        lambda: pl.program_id(0) == 0, lambda: pl.program_id(0) == n_chunks - 1,
        middle=(lambda: pl.program_id(0) == (7 * n_chunks) // 8) if ride is not None else None,
        name="s5_fwd", grid=(n_chunks,),
        in_specs=[pl.BlockSpec((t_rows, D_MODEL), lambda i: (i, 0)), full3(bbd_re), full3(bbd_im), full3(cbd_re),
                  full3(cbd_imn), full2(a_re), full2(a_im), full2(dskip)],
        out_specs=[pl.BlockSpec((t_rows, D_MODEL), lambda i: (i, 0)),
                   pl.BlockSpec((t_rows, D_MODEL), lambda i: (i, 0)),
                   pl.BlockSpec((t_rows, N_STATES), lambda i: (i, 0)),
                   pl.BlockSpec((t_rows, N_STATES), lambda i: (i, 0))],
        out_shape=[jax.ShapeDtypeStruct((seq, D_MODEL), F32),
                   jax.ShapeDtypeStruct((seq, D_MODEL), BF16),
                   jax.ShapeDtypeStruct((seq, N_STATES), F32),
                   jax.ShapeDtypeStruct((seq, N_STATES), F32)],
        scratch_shapes=[tile] * 10,
        compiler_params=_params(("arbitrary",)))


def s5_bwd(dy, u, dres, h_re, h_im, bbd_re, bbd_im, cbd_re, cbd_imn, a_re, a_im, dskip, ride=None, t_rows=128):
    seq = u.shape[0]
    t_rows = min(t_rows, seq)
    n_chunks = seq // t_rows

    n_tiles = t_rows // SUBLANES

    def body(dy_ref, u_ref, dres_ref, hre_ref, him_ref, hpre_ref, hpim_ref, bre, bim, cre, cimn, are, aim, d_ref,
             dx_ref, dbre, dbim, dcre, dcimn, dar_ref, dai_ref, dd_ref, lre, lim, car_re, car_im, acc_re, acc_im,
             *shifted):
        i = pl.program_id(0)

        @pl.when(i == 0)
        def _():
            for r in (car_re, car_im, acc_re, acc_im, dbre, dbim, dcre, dcimn, dd_ref):
                r[...] = jnp.zeros_like(r)
            _scan_coefs(are, aim, shifted, reverse=True)

        dyf = dy_ref[...]
        dyb = dyf.astype(BF16)
        uf = u_ref[...]
        ub = uf.astype(BF16)
        for j in range(N_UTILES):
            dyj = dyb[:, LANES * j:LANES * (j + 1)]
            st = slice(TILE_STATES * j, TILE_STATES * (j + 1))
            lre[:, st] = lax.dot_general(dyj, cre[j], _NT, preferred_element_type=F32)
            lim[:, st] = lax.dot_general(dyj, cimn[j], _NT, preferred_element_type=F32)
        has_pred = (i < n_chunks - 1).astype(F32)
        last_row = lax.broadcasted_iota(jnp.int32, (SUBLANES, SCAN_STRIP), 0) == SUBLANES - 1
        for s in range(N_STRIPS):
            cols = pl.ds(s * SCAN_STRIP, SCAN_STRIP)
            coefs = [c[:, cols] for c in shifted]
            before_re, before_im = hpre_ref[:, cols] * has_pred, hpim_ref[:, cols] * has_pred

            def step(k, carry):
                after_re, after_im, dar, dai = carry
                t = n_tiles - 1 - k
                rows = _tile_rows(t)
                lr, li = _tile_scan(lre[rows, cols], lim[rows, cols], coefs, after_re, after_im, True)
                lre[rows, cols] = lr
                lim[rows, cols] = li
                prev = _tile_rows(jnp.maximum(t - 1, 0))
                pre_re = jnp.where(t == 0, before_re, hre_ref[prev, cols])
                pre_im = jnp.where(t == 0, before_im, him_ref[prev, cols])
                hpr = pltpu.roll(jnp.where(last_row, pre_re, hre_ref[rows, cols]), 1, 0)
                hpi = pltpu.roll(jnp.where(last_row, pre_im, him_ref[rows, cols]), 1, 0)
                return lr, li, dar + lr * hpr + li * hpi, dai + li * hpr - lr * hpi

            cr, ci, dar, dai = lax.fori_loop(0, n_tiles, step, (car_re[:, cols], car_im[:, cols],
                                                               acc_re[:, cols], acc_im[:, cols]))
            car_re[:, cols] = cr
            car_im[:, cols] = ci
            acc_re[:, cols] = dar
            acc_im[:, cols] = dai

        dv = d_ref[...]
        for j in range(N_UTILES):
            sl = slice(LANES * j, LANES * (j + 1))
            st = slice(TILE_STATES * j, TILE_STATES * (j + 1))
            lrj = lre[:, st].astype(BF16)
            lij = lim[:, st].astype(BF16)
            du = (lax.dot_general(lrj, bre[j], _NT, preferred_element_type=F32)
                  + lax.dot_general(lij, bim[j], _NT, preferred_element_type=F32))
            dx_ref[:, sl] = du + dv[:, sl] * dyf[:, sl] + DN_ALPHA * dres_ref[:, sl]
            uj = ub[:, sl]
            dbre[j] += lax.dot_general(uj, lrj, _TN, preferred_element_type=F32)
            dbim[j] += lax.dot_general(uj, lij, _TN, preferred_element_type=F32)
            dyj = dyb[:, sl]
            dcre[j] += lax.dot_general(hre_ref[:, st].astype(BF16), dyj, _TN, preferred_element_type=F32)
            dcimn[j] += lax.dot_general(him_ref[:, st].astype(BF16), dyj, _TN, preferred_element_type=F32)
        dd_ref[...] += jnp.sum(dyf * uf, axis=0, keepdims=True)

        @pl.when(i == n_chunks - 1)
        def _():
            dar_ref[...] = jnp.sum(acc_re[...], axis=0, keepdims=True)
            dai_ref[...] = jnp.sum(acc_im[...], axis=0, keepdims=True)

    rev = lambda i: (n_chunks - 1 - i, 0)
    prev_tile = lambda i: (jnp.maximum((n_chunks - 1 - i) * n_tiles - 1, 0), 0)
    full3 = lambda a: pl.BlockSpec(a.shape, lambda i: (0, 0, 0))
    full2 = lambda a: pl.BlockSpec(a.shape, lambda i: (0, 0))
    acc3 = lambda shape: pl.BlockSpec(shape, lambda i: (0, 0, 0))
    acc2 = lambda shape: pl.BlockSpec(shape, lambda i: (0, 0))
    tile = pltpu.VMEM((SUBLANES, N_STATES), F32)
    return _pcall_riding(
        body, (dy, u, dres, h_re, h_im, h_re, h_im, bbd_re, bbd_im, cbd_re, cbd_imn, a_re, a_im, dskip), ride,
        lambda: pl.program_id(0) == 0, lambda: pl.program_id(0) == n_chunks - 1,
        name="s5_bwd", grid=(n_chunks,),
        in_specs=[pl.BlockSpec((t_rows, D_MODEL), rev), pl.BlockSpec((t_rows, D_MODEL), rev),
                  pl.BlockSpec((t_rows, D_MODEL), rev),
                  pl.BlockSpec((t_rows, N_STATES), rev), pl.BlockSpec((t_rows, N_STATES), rev),
                  pl.BlockSpec((SUBLANES, N_STATES), prev_tile), pl.BlockSpec((SUBLANES, N_STATES), prev_tile),
                  full3(bbd_re), full3(bbd_im), full3(cbd_re), full3(cbd_imn), full2(a_re), full2(a_im), full2(dskip)],
        out_specs=[pl.BlockSpec((t_rows, D_MODEL), rev), acc3(bbd_re.shape), acc3(bbd_im.shape), acc3(cbd_re.shape),
                   acc3(cbd_imn.shape), acc2((1, N_STATES)), acc2((1, N_STATES)), acc2((1, D_MODEL))],
        out_shape=[jax.ShapeDtypeStruct((seq, D_MODEL), F32), jax.ShapeDtypeStruct(bbd_re.shape, F32),
                   jax.ShapeDtypeStruct(bbd_im.shape, F32), jax.ShapeDtypeStruct(cbd_re.shape, F32),
                   jax.ShapeDtypeStruct(cbd_imn.shape, F32), jax.ShapeDtypeStruct((1, N_STATES), F32),
                   jax.ShapeDtypeStruct((1, N_STATES), F32), jax.ShapeDtypeStruct((1, D_MODEL), F32)],
        scratch_shapes=[pltpu.VMEM((t_rows, N_STATES), F32), pltpu.VMEM((t_rows, N_STATES), F32)] + [tile] * 12,
        compiler_params=_params(("arbitrary",)))


def _eye_groups():
    return jnp.eye(GROUPS_PER_TILE, dtype=F32)


def _blockdiag_in(bb):
    t = bb.transpose(0, 2, 1).reshape(N_UTILES, GROUPS_PER_TILE, SSM_GROUP, SSM_STATE)
    bd = jnp.einsum("jgcp,gh->jgchp", t, _eye_groups())
    return bd.reshape(N_UTILES, LANES, TILE_STATES)


def _blockdiag_in_t(d):
    t = jnp.einsum("jgchp,gh->jgcp", d.reshape(N_UTILES, GROUPS_PER_TILE, SSM_GROUP, GROUPS_PER_TILE, SSM_STATE),
                   _eye_groups())
    return t.reshape(N_GROUPS, SSM_GROUP, SSM_STATE).transpose(0, 2, 1)


def _blockdiag_out(c):
    t = c.transpose(0, 2, 1).reshape(N_UTILES, GROUPS_PER_TILE, SSM_STATE, SSM_GROUP)
    bd = jnp.einsum("jhpc,hg->jhpgc", t, _eye_groups())
    return bd.reshape(N_UTILES, TILE_STATES, LANES)


def _blockdiag_out_t(d):
    t = jnp.einsum("jhpgc,hg->jhpc", d.reshape(N_UTILES, GROUPS_PER_TILE, SSM_STATE, GROUPS_PER_TILE, SSM_GROUP),
                   _eye_groups())
    return t.reshape(N_GROUPS, SSM_STATE, SSM_GROUP).transpose(0, 2, 1)


ATT_TQ = 512
ATT_TK = 512
LOG2E = math.log2(math.e)
LN2 = math.log(2.0)
Q_PRESCALE = SM_SCALE * LOG2E


def _loop_in_pairs(n, step, carry, start=0):
    pairs = (n - start) // 2

    def two(t, c):
        return step(start + 2 * t + 1, step(start + 2 * t, c))

    carry = lax.fori_loop(0, pairs, two, carry)
    return lax.fori_loop(start + 2 * pairs, n, step, carry)


def _causal(s, off=0, transposed=False):
    r = lax.broadcasted_iota(jnp.int32, s.shape, 0)
    c = lax.broadcasted_iota(jnp.int32, s.shape, 1)
    keep = (r <= c + off) if transposed else (c <= r + off)
    return jnp.where(keep, s, NEG_INF)


def _q_specs(rows, at):
    def nope(*ids):
        r, h = at(*ids)
        return r, 3 * (h // HEADS_PER_CHIP) + h % HEADS_PER_CHIP

    def rope(*ids):
        r, h = at(*ids)
        return r, 3 * (h // HEADS_PER_CHIP) + HEADS_PER_CHIP

    return [pl.BlockSpec((rows, LANES), nope), pl.BlockSpec((rows, LANES), rope)]


def _kv_specs(rows, at):
    def col(f):
        def index(*ids):
            r, h = at(*ids)
            return r, f(h)
        return index

    return [pl.BlockSpec((rows, LANES), col(lambda h: 2 * h)), pl.BlockSpec((rows, LANES), col(lambda h: h % HEADS_PER_CHIP)),
            pl.BlockSpec((rows, LANES), col(lambda h: 2 * h + 1))]


def _cat(a, b):
    return jnp.concatenate([a, b], axis=1)


def attn_fwd(q, kv, kr, ride=None, tq=ATT_TQ, tk=ATT_TK):
    seq = q.shape[0]
    n_heads = N_HEADS
    tq, tk = min(tq, seq), min(tk, seq)

    def body(qn_ref, qr_ref, kn_ref, kr_ref, v_ref, o_ref, lse_ref):
        qi = pl.program_id(1)
        qv = _cat(qn_ref[...], qr_ref[...])
        jd = (qi * tq) // tk

        def block(j, carry, diag):
            m, l, acc = carry
            rows = pl.ds(pl.multiple_of(j * tk, tk), tk)
            s = lax.dot_general(qv, _cat(kn_ref[rows, :], kr_ref[rows, :]), _NT, preferred_element_type=F32)
            if diag:
                s = _causal(s, qi * tq - jd * tk)
            m_new = jnp.maximum(m, jnp.max(s, axis=-1, keepdims=True))
            p = jnp.exp2(s - m_new)
            corr = jnp.exp2(m - m_new)
            l = l * corr + jnp.sum(p, axis=-1, keepdims=True)
            acc = acc * corr + jnp.dot(p.astype(BF16), v_ref[rows, :], preferred_element_type=F32)
            return m_new, l, acc

        init = (jnp.full((tq, 1), NEG_INF, F32), jnp.zeros((tq, 1), F32), jnp.zeros((tq, V_HEAD), F32))
        carry = _loop_in_pairs(jd, lambda j, c: block(j, c, False), init)
        m, l, acc = block(jd, carry, True)
        o_ref[...] = acc / l
        lse_ref[0] = jnp.broadcast_to(m + jnp.log2(l), (tq, LANES))

    n_q = seq // tq
    return _pcall_riding(
        body, (q, q, kv, kr, kv), ride,
        lambda: (pl.program_id(0) == 0) & (pl.program_id(1) == 0),
        lambda: (pl.program_id(0) == n_heads - 1) & (pl.program_id(1) == n_q - 1),
        middle=(lambda: (pl.program_id(0) == (5 * n_heads) // 8) & (pl.program_id(1) == 0)) if ride is not None else None,
        name="attn_fwd", grid=(n_heads, n_q),
        in_specs=_q_specs(tq, lambda h, i: (i, h)) + _kv_specs(seq, lambda h, i: (0, h)),
        out_specs=[pl.BlockSpec((tq, V_HEAD), lambda h, i: (i, h)),
                   pl.BlockSpec((1, tq, LANES), lambda h, i: (h, i, 0))],
        out_shape=[jax.ShapeDtypeStruct((seq, n_heads * V_HEAD), F32),
                   jax.ShapeDtypeStruct((n_heads, seq, LANES), F32)],
        compiler_params=_params(("arbitrary", "arbitrary")))


def attn_bwd_dq(q, kv, kr, do, o, lse, tq=ATT_TQ, tk=ATT_TK):
    seq = q.shape[0]
    tq, tk = min(tq, seq), min(tk, seq)
    head = lambda c, i, hh: HEADS_PER_CHIP * c + hh

    def body(qn_ref, qr_ref, kn_ref, kr_ref, v_ref, do_ref, o_ref, lse_ref, dqn_ref, dqr_ref, delta_ref):
        qi = pl.program_id(1)
        qv = _cat(qn_ref[...], qr_ref[...])
        dof = do_ref[...]
        dob = dof.astype(BF16)
        delta = jnp.sum(dof * o_ref[...], axis=-1, keepdims=True)
        lse = lse_ref[0][:, :1]
        jd = (qi * tq) // tk

        def block(j, dq, diag):
            rows = pl.ds(pl.multiple_of(j * tk, tk), tk)
            kv = _cat(kn_ref[rows, :], kr_ref[rows, :])
            s = lax.dot_general(qv, kv, _NT, preferred_element_type=F32)
            if diag:
                s = _causal(s, qi * tq - jd * tk)
            p = jnp.exp2(s - lse)
            dp = lax.dot_general(dob, v_ref[rows, :], _NT, preferred_element_type=F32)
            ds = p * (dp - delta)
            return dq + jnp.dot(ds.astype(BF16), kv, preferred_element_type=F32)

        dq = _loop_in_pairs(jd, lambda j, c: block(j, c, False), jnp.zeros((tq, 2 * LANES), F32))
        dq = block(jd, dq, True) * SM_SCALE
        dqn_ref[...] = dq[:, :LANES]

        @pl.when(pl.program_id(2) == 0)
        def _():
            dqr_ref[...] = dq[:, LANES:]

        @pl.when(pl.program_id(2) > 0)
        def _():
            dqr_ref[...] += dq[:, LANES:]

        delta_ref[0] = jnp.broadcast_to(delta, (tq, LANES))

    by_head = lambda c, i, hh: (i, head(c, i, hh))
    dq_nope, dq_rope, delta = _pcall(
        body, name="attn_bwd_dq", grid=(N_CHIPS, seq // tq, HEADS_PER_CHIP),
        in_specs=_q_specs(tq, by_head) + _kv_specs(seq, lambda c, i, hh: (0, head(c, i, hh)))
        + [pl.BlockSpec((tq, V_HEAD), by_head), pl.BlockSpec((tq, V_HEAD), by_head),
           pl.BlockSpec((1, tq, LANES), lambda c, i, hh: (head(c, i, hh), i, 0))],
        out_specs=[pl.BlockSpec((tq, LANES), lambda c, i, hh: (i, head(c, i, hh))),
                   pl.BlockSpec((tq, LANES), lambda c, i, hh: (i, c)),
                   pl.BlockSpec((1, tq, LANES), lambda c, i, hh: (head(c, i, hh), i, 0))],
        out_shape=[jax.ShapeDtypeStruct((seq, N_HEADS * QK_NOPE), F32),
                   jax.ShapeDtypeStruct((seq, N_CHIPS * LANES), F32),
                   jax.ShapeDtypeStruct((N_HEADS, seq, LANES), F32)],
        compiler_params=_params(("parallel", "parallel", "arbitrary")),
    )(q, q, kv, kr, kv, do, o, lse)
    return dq_nope, dq_rope, delta


def attn_bwd_dkv(q, kv, kr, do, lse_row, delta_row, tq=ATT_TK):
    seq = q.shape[0]
    tq = min(tq, seq)
    n_blk = seq // tq

    def body(qn_ref, qr_ref, kn_ref, kr_ref, v_ref, do_ref, lse_ref, delta_ref, dkv_ref, dkr_ref):
        kj = pl.program_id(1)
        kv = _cat(kn_ref[...], kr_ref[...])
        vv = v_ref[...]

        def block(i, carry, diag):
            dk, dv = carry
            rows = pl.ds(pl.multiple_of(i * tq, tq), tq)
            qv = _cat(qn_ref[rows, :], qr_ref[rows, :])
            st = lax.dot_general(kv, qv, _NT, preferred_element_type=F32)
            if diag:
                st = _causal(st, transposed=True)
            pt = jnp.exp2(st - lse_ref[0, pl.ds(i, 1), :])
            dob = do_ref[rows, :].astype(BF16)
            dv = dv + jnp.dot(pt.astype(BF16), dob, preferred_element_type=F32)
            dpt = lax.dot_general(vv, dob, _NT, preferred_element_type=F32)
            dst = pt * (dpt - delta_ref[0, pl.ds(i, 1), :])
            dk = dk + jnp.dot(dst.astype(BF16), qv, preferred_element_type=F32)
            return dk, dv

        carry = block(kj, (jnp.zeros((tq, 2 * LANES), F32), jnp.zeros((tq, V_HEAD), F32)), True)
        dk, dv = _loop_in_pairs(n_blk, lambda i, c: block(i, c, False), carry, start=kj + 1)
        dk = dk * LN2
        dkv_ref[...] = _cat(dk[:, :LANES], dv).astype(dkv_ref.dtype)
        lane = lax.broadcasted_iota(jnp.int32, (tq, LANES), 1)
        mine = (lane // HALF_ROPE) % HEADS_PER_CHIP == pl.program_id(0) % HEADS_PER_CHIP
        dkr_ref[0] = jnp.where(mine, dk[:, LANES:], 0.0)

    return _pcall(
        body, name="attn_bwd_dkv", grid=(N_HEADS, n_blk),
        in_specs=_q_specs(seq, lambda h, j: (0, h)) + _kv_specs(tq, lambda h, j: (j, h))
        + [pl.BlockSpec((seq, V_HEAD), lambda h, j: (0, h)),
           pl.BlockSpec((1, n_blk, tq), lambda h, j: (h, 0, 0)),
           pl.BlockSpec((1, n_blk, tq), lambda h, j: (h, 0, 0))],
        out_specs=[pl.BlockSpec((tq, QK_NOPE + V_HEAD), lambda h, j: (j, h)),
                   pl.BlockSpec((1, tq, LANES), lambda h, j: (h, j, 0))],
        out_shape=[jax.ShapeDtypeStruct((seq, N_HEADS * (QK_NOPE + V_HEAD)), BF16),
                   jax.ShapeDtypeStruct((N_HEADS, seq, LANES), F32)],
        compiler_params=_params(("parallel", "parallel")),
    )(q, q, kv, kr, kv, do, lse_row, delta_row)


def head_sum(x, ts=512):
    n_heads, seq, w = x.shape
    ts = min(ts, seq)

    def body(x_ref, o_ref):
        o_ref[...] = jnp.sum(x_ref[...], axis=0)

    return _pcall(body, name="head_sum", grid=(seq // ts,),
                  in_specs=[pl.BlockSpec((n_heads, ts, w), lambda i: (0, i, 0))],
                  out_specs=pl.BlockSpec((ts, w), lambda i: (i, 0)),
                  out_shape=jax.ShapeDtypeStruct((seq, w), F32),
                  compiler_params=_params(("parallel",)))(x)


HEADS_PER_CHIP = N_HEADS // N_CHIPS
Q_CHIP = HEADS_PER_CHIP * QK_DIM
Q_CHIP_NOPE = HEADS_PER_CHIP * QK_NOPE


def _perm_q_cols(w):
    t = w.reshape(w.shape[0], HEADS_PER_CHIP, QK_DIM)
    return jnp.concatenate([t[:, :, :QK_NOPE].reshape(w.shape[0], -1),
                            t[:, :, QK_NOPE:QK_NOPE + HALF_ROPE].reshape(w.shape[0], -1),
                            t[:, :, QK_NOPE + HALF_ROPE:].reshape(w.shape[0], -1)], axis=1)


def _unperm_q_cols(w):
    r = w.shape[0]
    nope = w[:, :Q_CHIP_NOPE].reshape(r, HEADS_PER_CHIP, QK_NOPE)
    r1 = w[:, Q_CHIP_NOPE:Q_CHIP_NOPE + QK_ROPE].reshape(r, HEADS_PER_CHIP, HALF_ROPE)
    r2 = w[:, Q_CHIP_NOPE + QK_ROPE:].reshape(r, HEADS_PER_CHIP, HALF_ROPE)
    return jnp.concatenate([nope, r1, r2], axis=2).reshape(r, Q_CHIP)


def _pad_kva_cols(w):
    z = jnp.zeros((w.shape[0], HALF_ROPE), w.dtype)
    return jnp.concatenate([w[:, :KV_LORA], w[:, KV_LORA:KV_LORA + HALF_ROPE], z, w[:, KV_LORA + HALF_ROPE:], z], axis=1)


def _unpad_kva_cols(w):
    return jnp.concatenate([w[:, :KV_LORA], w[:, KV_LORA:KV_LORA + HALF_ROPE],
                            w[:, KV_LORA + QK_ROPE:KV_LORA + QK_ROPE + HALF_ROPE]], axis=1)


def _rope_tile(t, cs, sn):
    return t * cs + pltpu.roll(t, LANES // 2, 1) * sn


def _rope_tile_bwd(d, cs, sn):
    return d * cs + pltpu.roll(d * sn, LANES // 2, 1)


def _b_cols(tk, tn):
    return pl.BlockSpec((None, tk, tn), lambda i, j, k: (j, k, 0))


def _b_cols_t(tk, tn):
    return pl.BlockSpec((None, tn, tk), lambda i, j, k: (k, j, 0))


def _out_cols(shape):
    return shape, lambda tm, tn: pl.BlockSpec((None, tm, tn), lambda i, j, k: (j, i, 0))


def _halves(a):
    return a.reshape(N_CHIPS, 2, a.shape[1] // 2, a.shape[2])


def device_step(x, positions, target, w, comm=None):
    seq = x.shape[0]
    w = dict(w)

    def gathered(names, outs):
        for n, a in zip(names, outs):
            if isinstance(n, tuple):
                w[n[0]] = [a.reshape(v.shape) if l == n[1] else v for l, v in enumerate(w[n[0]])]
            else:
                w[n] = a.reshape(w[n].shape)

    def ride_for(names):
        if comm is None:
            return None
        return GatherRide([_halves(w[n[0]][n[1]] if isinstance(n, tuple) else w[n]) for n in names])

    first_ride = ("ssm_w_glu", "ssm_w_out", ("w_ff1", 0), ("w_ff2", 0))
    mla_ride = ("kv_w_a", "kv_w_b", "q_w_a", "q_w_b", "attn_w_o")
    second_ride = (("w_ff1", 1), ("w_ff2", 1))

    inv_freq = ROPE_THETA ** (-jnp.arange(HALF_ROPE, dtype=F32) / HALF_ROPE)
    ang = positions.astype(F32)[:, None] * inv_freq
    cos, sin = jnp.cos(ang), jnp.sin(ang)
    zero = jnp.zeros_like(cos)
    cos_q, sin_q = jnp.concatenate([cos] * 4, 1), jnp.concatenate([-sin, -sin, sin, sin], 1)
    cos_k, sin_k = jnp.concatenate([cos, zero, cos, zero], 1), jnp.concatenate([-sin, zero, sin, zero], 1)
    ff_tile = D_FF // N_CHIPS
    pack_shape = (N_CHIPS, EARLY_ROWS, PACK_W)

    lr = w["ssm_lam_re"].reshape(N_STATES, 1)
    li = w["ssm_lam_im"].reshape(N_STATES, 1)
    ldt = jnp.repeat(w["ssm_log_dt"].reshape(N_GROUPS), SSM_STATE).reshape(N_STATES, 1)
    b_re = w["ssm_b_re"].reshape(N_STATES, SSM_GROUP)
    b_im = w["ssm_b_im"].reshape(N_STATES, SSM_GROUP)
    a_re, a_im, bb_re, bb_im = s5_prep(lr, li, ldt, b_re, b_im)
    a_re, a_im = a_re.reshape(1, N_STATES), a_im.reshape(1, N_STATES)
    bbd_re = _blockdiag_in(bb_re.reshape(N_GROUPS, SSM_STATE, SSM_GROUP)).astype(BF16)
    bbd_im = _blockdiag_in(bb_im.reshape(N_GROUPS, SSM_STATE, SSM_GROUP)).astype(BF16)
    cbd_re = _blockdiag_out(w["ssm_c_re"].reshape(N_GROUPS, SSM_GROUP, SSM_STATE)).astype(BF16)
    cbd_imn = _blockdiag_out(-w["ssm_c_im"].reshape(N_GROUPS, SSM_GROUP, SSM_STATE)).astype(BF16)
    dskip = w["ssm_d"].reshape(1, D_MODEL)
    (ypre, yg, h_re, h_im), landed = s5_fwd(x, bbd_re, bbd_im, cbd_re, cbd_imn, a_re, a_im, dskip, ride_for(first_ride))
    gathered(first_ride, landed)
    w_glu = w["ssm_w_glu"]
    glu_tile = w_glu.shape[2]
    vg = mm(yg, w_glu, n_dim=2 * D_MODEL, tiles=(None, glu_tile, None), b_view=_b_cols, name="glu_proj")

    def glu(v):
        return (v[:, :D_MODEL] * _sigmoid(v[:, D_MODEL:]),), ()
    (z,) = rowwise(glu, (vg,), ((D_MODEL, BF16),), name="glu")
    w_out = w["ssm_w_out"].reshape(D_MODEL, D_MODEL)
    mix0 = mm(z, w_out, name="ssm_out")

    def mlp_fwd(hb, layer, riding=None):
        pre = mm(hb, w["w_ff1"][layer], n_dim=D_FF, tiles=(None, ff_tile, None), b_view=_b_cols, name=f"ff1_{layer}",
                 out_dtypes=(BF16,), ride=ride_for(riding) if riding else None)
        if riding and comm is not None:
            pre, landed = pre
            gathered(riding, landed)
        f = mm(pre, w["w_ff2"][layer].reshape(D_FF, D_MODEL), pro_a=_relu2, name=f"ff2_{layer}")
        return pre, f

    ln = lambda name, l: w[name][l].reshape(1, D_MODEL)
    h1, h1b = ln_fwd(x, mix0, ln("ln_mix_g", 0), ln("ln_mix_b", 0), "ln_mix_0")
    f1pre, f1 = mlp_fwd(h1b, 0, mla_ride)
    h2, h2b = ln_fwd(h1, f1, ln("ln_ffn_g", 0), ln("ln_ffn_b", 0), "ln_ffn_0")

    kv_w_a = w["kv_w_a"].reshape(D_MODEL, KVA_PAD)
    kv_w_b = w["kv_w_b"]
    q_w_a = w["q_w_a"].reshape(D_MODEL, Q_LORA)
    q_w_b = w["q_w_b"]
    w_o = w["attn_w_o"].reshape(D_MODEL, D_MODEL)
    kvb_tile = kv_w_b.shape[2]
    kvn_g = w["kv_norm_g"].reshape(1, KV_LORA)
    qn_g = w["q_norm_g"].reshape(1, Q_LORA)
    kva = mm(h2b, kv_w_a, name="kv_a")

    def kv_post(kva, g, cs, sn):
        tile = _rope_tile(kva[:, KV_LORA:], cs, sn)
        return (_rms(kva[:, :KV_LORA], g), _cat(tile, pltpu.roll(tile, HALF_ROPE, 1))), ()
    ckv, krope = rowwise(kv_post, (kva, kvn_g, cos_k, sin_k), ((KV_LORA, BF16), (2 * LANES, BF16)), name="kv_post")
    kvb = mm(ckv, kv_w_b, n_dim=N_CHIPS * kvb_tile, tiles=(None, kvb_tile, KV_LORA), b_view=_b_cols, name="kv_b",
             out_dtypes=(BF16,))
    cq_raw = mm(h2b, q_w_a, name="q_a")
    (cq,) = rowwise(lambda c, g: ((_rms(c, g),), ()), (cq_raw, qn_g), ((Q_LORA, BF16),), name="q_norm")
    qlin = mm(cq, q_w_b, n_dim=N_CHIPS * Q_CHIP, tiles=(None, Q_CHIP, Q_LORA), b_view=_b_cols, name="q_b")

    def on_rope_tiles(fn, scale=None):
        def apply(q, cs, sn):
            parts = []
            for k in range(N_CHIPS):
                parts.append(q[:, Q_CHIP * k:Q_CHIP * k + Q_CHIP_NOPE])
                parts.append(fn(q[:, Q_CHIP * k + Q_CHIP_NOPE:Q_CHIP * (k + 1)], cs, sn))
            out = jnp.concatenate(parts, axis=1)
            return (out if scale is None else out * scale,), ()
        return apply
    (qro,) = rowwise(on_rope_tiles(_rope_tile, Q_PRESCALE), (qlin, cos_q, sin_q), ((N_CHIPS * Q_CHIP, BF16),),
                     name="q_rope")
    (o, lse), landed = attn_fwd(qro, kvb, krope, ride_for(second_ride))
    gathered(second_ride, landed)
    mix1 = mm(o, w_o, name="attn_out")
    h3, h3b = ln_fwd(h2, mix1, ln("ln_mix_g", 1), ln("ln_mix_b", 1), "ln_mix_1")
    f2pre, f2 = mlp_fwd(h3b, 1)
    def last_ln_and_loss(h, mix, gl, bl, t):
        e = _layer_norm(h, mix, gl, bl) - t
        return (e * (1.0 / D_MODEL),), (jnp.broadcast_to(jnp.sum(e * e), (1, LANES)),)
    dh4, loss_acc = rowwise(last_ln_and_loss, (h3, f2, ln("ln_ffn_g", 1), ln("ln_ffn_b", 1), target), ((D_MODEL, F32),),
                            accs=(LANES,), name="ln_ffn_1_loss")
    loss = loss_acc[0, 0] * (0.5 / D_MODEL)

    g = {}

    def into_rows(off, rows_per_chip, shape=pack_shape):
        def view(tm, tn):
            nb = rows_per_chip // tm
            return pl.BlockSpec((None, tm, tn), lambda i, j, k: (i // nb, off // tm + i % nb, 0))
        return shape, view

    def into_cols(off):
        return pack_shape, lambda tm, tn: pl.BlockSpec((None, tm, tn), lambda i, j, k: (j, off // tm + i, 0))

    def mlp_bwd(pack, dr, drb, hb, pre, layer, swap=False):
        dpre = mm(drb, w["w_ff2"][layer].reshape(D_FF, D_MODEL), tb=True, epi=lambda r, p: (r * 2.0 * jnp.maximum(p, 0.0),),
                  extras=(pre,), out_dtypes=(BF16,), tiles=(None, ff_tile, None), name=f"ff2_dx_{layer}")
        pack = mm(pre, drb, ta=True, pro_a=_relu2, name=f"ff2_dw_{layer}", tiles=(ff_tile, PACK_W, None), into=pack,
                  out_view=into_rows(EARLY_OFF["w_ff2"] + layer * ff_tile, ff_tile))
        pack = mm(hb, dpre, ta=True, name=f"ff1_dw_{layer}", tiles=(None, PACK_W, None), into=pack,
                  out_view=into_cols(EARLY_OFF["w_ff1"] + layer * D_MODEL))
        dh = mm(dpre, w["w_ff1"][layer], tb=True, epi=lambda r, d: (r + DN_ALPHA * d,), extras=(dr,), n_dim=D_MODEL,
                tiles=(None, D_MODEL, ff_tile), b_view=_b_cols_t, name=f"ff1_dx_{layer}",
                ride=SwapRide(pack) if swap else None)
        return (pack, *dh) if swap else (pack, dh)

    dr4, dr4b, dg_f1, db_f1 = ln_bwd(h3, f2, ln("ln_ffn_g", 1), dh4, "ln_ffn_bwd_1")
    pack, dh3 = mlp_bwd(None, dr4, dr4b, h3b, f2pre, 1)
    dr3, dr3b, dg_m1, db_m1 = ln_bwd(h2, mix1, ln("ln_mix_g", 1), dh3, "ln_mix_bwd_1")
    shard_rows = D_MODEL // N_CHIPS
    pack = mm(o, dr3b, ta=True, name="attn_out_dw", tiles=(shard_rows, PACK_W, None), into=pack,
              out_view=into_rows(EARLY_OFF["attn_w_o"], shard_rows))
    do = mm(dr3b, w_o, tb=True, name="attn_out_dx")
    dqn, dqr, delta = attn_bwd_dq(qro, kvb, krope, do, o, lse)
    tb = min(ATT_TK, seq)
    lse_row = lse[:, :, 0].reshape(N_HEADS, seq // tb, tb)
    delta_row = delta[:, :, 0].reshape(N_HEADS, seq // tb, tb)
    dkvb, dkr = attn_bwd_dkv(qro, kvb, krope, do, lse_row, delta_row)

    def q_rope_bwd(dn, dr, cs, sn):
        parts = []
        for k in range(N_CHIPS):
            parts.append(dn[:, Q_CHIP_NOPE * k:Q_CHIP_NOPE * (k + 1)])
            parts.append(_rope_tile_bwd(dr[:, LANES * k:LANES * (k + 1)], cs, sn))
        return (jnp.concatenate(parts, axis=1),), ()
    (dqlin,) = rowwise(q_rope_bwd, (dqn, dqr, cos_q, sin_q), ((N_CHIPS * Q_CHIP, BF16),), name="q_rope_bwd")
    g["q_w_b"] = mm(cq, dqlin, ta=True, name="q_b_dw", tiles=(Q_LORA, Q_CHIP, None), out_view=_out_cols(q_w_b.shape))
    dcq = mm(dqlin, q_w_b, tb=True, n_dim=Q_LORA, tiles=(None, Q_LORA, Q_CHIP), b_view=_b_cols_t, name="q_b_dx")

    def q_norm_bwd(c, gq, d):
        dx, dgq = _rms_bwd(c, gq, d)
        return (dx,), (dgq,)
    dcq_raw, dqn_g = rowwise(q_norm_bwd, (cq_raw, qn_g, dcq), ((Q_LORA, BF16),), accs=(Q_LORA,), name="q_norm_bwd")
    g["q_w_a"] = mm(h2b, dcq_raw, ta=True, name="q_a_dw")
    g["kv_w_b"] = mm(ckv, dkvb, ta=True, name="kv_b_dw", tiles=(KV_LORA, kvb_tile, None), out_view=_out_cols(kv_w_b.shape))
    dckv = mm(dkvb, kv_w_b, tb=True, n_dim=KV_LORA, tiles=(None, KV_LORA, kvb_tile), b_view=_b_cols_t, name="kv_b_dx")
    dkr_sum = head_sum(dkr)

    def kv_post_bwd(kva, gk, dc, dk, cs, sn):
        dx, dgk = _rms_bwd(kva[:, :KV_LORA], gk, dc)
        dk = dk + pltpu.roll(dk, LANES - HALF_ROPE, 1)
        return (jnp.concatenate([dx, _rope_tile_bwd(dk, cs, sn)], axis=1),), (dgk,)
    dkva, dkvn_g = rowwise(kv_post_bwd, (kva, kvn_g, dckv, dkr_sum, cos_k, sin_k), ((KVA_PAD, BF16),),
                           accs=(KV_LORA,), name="kv_post_bwd")
    g["kv_w_a"] = mm(h2b, dkva, ta=True, name="kv_a_dw")
    dh2 = mm(dcq_raw, q_w_a, tb=True, epi=lambda r, d: (r + DN_ALPHA * d,), extras=(dr3,), name="q_a_dx")
    dh2 = mm(dkva, kv_w_a, tb=True, epi=lambda r, d: (r + d,), extras=(dh2,), name="kv_a_dx")

    dr2, dr2b, dg_f0, db_f0 = ln_bwd(h1, f1, ln("ln_ffn_g", 0), dh2, "ln_ffn_bwd_0")
    pack = put_rows(pack, packed_shards(g, MISC_EARLY, EARLY_ROWS - MISC_EARLY_OFF), MISC_EARLY_OFF)
    early_ride = None
    if comm is None:
        pack, dh1 = mlp_bwd(pack, dr2, dr2b, h1b, f1pre, 0)
    else:
        pack, dh1, (theirs,) = mlp_bwd(pack, dr2, dr2b, h1b, f1pre, 0, swap=True)
        early_ride = SendRide(add_halves(pack, theirs, comm[1]))
    dr1, dr1b, dg_m0, db_m0 = ln_bwd(x, mix0, ln("ln_mix_g", 0), dh1, "ln_mix_bwd_0")
    late = mm(z, dr1b, ta=True, name="ssm_out_dw", tiles=(shard_rows, PACK_W, None),
              out_view=into_rows(LATE_OFF["ssm_w_out"], shard_rows, (N_CHIPS, LATE_ROWS, PACK_W)))
    dz = mm(dr1b, w_out, tb=True, name="ssm_out_dx")

    def glu_bwd(v, dz):
        val, sg = v[:, :D_MODEL], _sigmoid(v[:, D_MODEL:])
        return (jnp.concatenate([dz * sg, dz * val * sg * (1.0 - sg)], axis=1),), ()
    (dvg,) = rowwise(glu_bwd, (vg, dz), ((2 * D_MODEL, BF16),), name="glu_bwd")
    g["ssm_w_glu"] = mm(yg, dvg, ta=True, name="glu_proj_dw", tiles=(None, glu_tile, None), out_view=_out_cols(w_glu.shape))
    dypre = mm(dvg, w_glu, tb=True, epi=lambda r, y: (r * _gelu_grad(y),), extras=(ypre,), n_dim=D_MODEL,
               tiles=(None, D_MODEL, glu_tile), b_view=_b_cols_t, name="glu_proj_dx")
    (dx, dbbd_re, dbbd_im, dcbd_re, dcbd_imn, dar, dai, dd), got_early = s5_bwd(
        dypre, x, dr1, h_re, h_im, bbd_re, bbd_im, cbd_re, cbd_imn, a_re, a_im, dskip, early_ride)
    dbb_re = _blockdiag_in_t(dbbd_re).reshape(N_STATES, SSM_GROUP)
    dbb_im = _blockdiag_in_t(dbbd_im).reshape(N_STATES, SSM_GROUP)
    dlr, dli, dldt, db_re, db_im = s5_prep_bwd(lr, li, ldt, b_re, b_im, dar.reshape(N_STATES, 1),
                                               dai.reshape(N_STATES, 1), dbb_re, dbb_im)
    g["ssm_lam_re"] = dlr.reshape(1, N_GROUPS, SSM_STATE)
    g["ssm_lam_im"] = dli.reshape(1, N_GROUPS, SSM_STATE)
    g["ssm_log_dt"] = group_sum(dldt).reshape(1, N_GROUPS)
    g["ssm_b_re"] = db_re.reshape(1, N_GROUPS, SSM_STATE, SSM_GROUP)
    g["ssm_b_im"] = db_im.reshape(1, N_GROUPS, SSM_STATE, SSM_GROUP)
    g["ssm_c_re"] = _blockdiag_out_t(dcbd_re).reshape(1, N_GROUPS, SSM_GROUP, SSM_STATE)
    g["ssm_c_im"] = -_blockdiag_out_t(dcbd_imn).reshape(1, N_GROUPS, SSM_GROUP, SSM_STATE)
    g["ssm_d"] = dd
    g["ln_mix_g"] = jnp.concatenate([dg_m0, dg_m1], 0)
    g["ln_mix_b"] = jnp.concatenate([db_m0, db_m1], 0)
    g["ln_ffn_g"] = jnp.concatenate([dg_f0, dg_f1], 0)
    g["ln_ffn_b"] = jnp.concatenate([db_f0, db_f1], 0)
    g["kv_norm_g"] = dkvn_g.reshape(KV_LORA)
    g["q_norm_g"] = dqn_g
    return loss, dx, pack, late, g, (early_ride.ins[0], got_early[0]) if comm is not None else None


def place(shard, me_idx, dtype, name):
    rows, cols = shard.shape
    tr = _tile(rows, (512, 256, 128))

    def body(m_ref, x_ref, o_ref):
        o_ref[...] = x_ref[...].astype(o_ref.dtype)

    return _pcall(
        body, name=name,
        grid_spec=pltpu.PrefetchScalarGridSpec(
            num_scalar_prefetch=1, grid=(rows // tr,),
            in_specs=[pl.BlockSpec((tr, cols), lambda i, m: (i, 0))],
            out_specs=pl.BlockSpec((None, tr, cols), lambda i, m: (m[0], i, 0))),
        out_shape=jax.ShapeDtypeStruct((N_CHIPS, rows, cols), dtype),
        compiler_params=_params(("parallel",)),
    )(me_idx, shard)


def put_rows(pack, rows, off):
    _, n, cols = rows.shape
    tr = math.gcd(math.gcd(off, n), 512)

    def body(r_ref, p_ref, o_ref):
        o_ref[...] = r_ref[...]

    return _pcall(body, name="grad_put_rows", grid=(N_CHIPS, n // tr),
                  in_specs=[pl.BlockSpec((None, tr, cols), lambda k, i: (k, i, 0)), _ANY],
                  out_specs=pl.BlockSpec((None, tr, cols), lambda k, i: (k, off // tr + i, 0)),
                  out_shape=jax.ShapeDtypeStruct(pack.shape, pack.dtype), input_output_aliases={1: 0},
                  compiler_params=_params(("parallel", "parallel")))(rows, pack)


def _my_cols(c, mine=True):
    start = (c if mine else 1 - c) * HALF_W
    return pl.ds(pl.multiple_of(start, HALF_W), HALF_W)


def add_halves(gpack, got, c_idx):
    n, rows, _ = gpack.shape
    blk = (None, G_BLOCK_ROWS, HALF_W)

    def body(c_ref, g_ref, r_ref, o_ref):
        o_ref[...] = (g_ref[...] + r_ref[...]).astype(o_ref.dtype)

    return _pcall(
        body, name="grad_add_halves",
        grid_spec=pltpu.PrefetchScalarGridSpec(
            num_scalar_prefetch=1, grid=(n, rows // G_BLOCK_ROWS),
            in_specs=[pl.BlockSpec(blk, lambda k, i, c: (k, i, c[0])), pl.BlockSpec(blk, lambda k, i, c: (k, i, 0))],
            out_specs=pl.BlockSpec(blk, lambda k, i, c: (k, i, 0))),
        out_shape=jax.ShapeDtypeStruct((n, rows, HALF_W), BF16),
        compiler_params=_params(("parallel", "parallel")),
    )(c_idx, gpack, got)


def sum_owner(part, got, idx, total_rows, row_off=0, into=None):
    _, rows, _ = part.shape
    tr = G_BLOCK_ROWS
    n_into = 0 if into is None else 1

    def body(m_ref, p_ref, g_ref, *rest):
        up = lambda v: v.astype(F32)
        rest[-1][...] = ((up(p_ref[...]) + up(g_ref[0])) + up(g_ref[1])) + up(g_ref[2])

    return _pcall(
        body, name="grad_sum_owner",
        grid_spec=pltpu.PrefetchScalarGridSpec(
            num_scalar_prefetch=1, grid=(rows // tr,),
            in_specs=[pl.BlockSpec((None, tr, HALF_W), lambda i, m: (m[0], i, 0)),
                      pl.BlockSpec((3, tr, HALF_W), lambda i, m: (0, i, 0))] + [_ANY] * n_into,
            out_specs=pl.BlockSpec((tr, HALF_W), lambda i, m: (row_off // tr + i, m[1]))),
        out_shape=jax.ShapeDtypeStruct((total_rows, PACK_W), F32),
        input_output_aliases={3: 0} if n_into else {},
        compiler_params=_params(("parallel",)),
    )(idx, part, got, *([into] if n_into else []))


def join_halves(red):
    def body(in_ref, out_ref, send_sem, recv_sem):
        x, y, c, _ = _place()
        sibling = (x, y, 1 - c)
        mine = out_ref.at[:, _my_cols(c)]
        cp = pltpu.make_async_remote_copy(src_ref=mine, dst_ref=mine, send_sem=send_sem, recv_sem=recv_sem,
                                          device_id=sibling, device_id_type=MESH)
        cp.start()
        cp.wait_send()
        other = out_ref.at[:, _my_cols(c, mine=False)]
        pltpu.make_async_remote_copy(src_ref=other, dst_ref=other, send_sem=send_sem, recv_sem=recv_sem,
                                     device_id=sibling, device_id_type=MESH).wait_recv()

    return _pcall(body, name="grad_join_halves", in_specs=[_ANY], out_specs=_ANY,
                  out_shape=jax.ShapeDtypeStruct(red.shape, red.dtype), input_output_aliases={0: 0},
                  scratch_shapes=[pltpu.SemaphoreType.DMA, pltpu.SemaphoreType.DMA])(red)


def adamw(gsrc, g_off, wt, m, v, name):
    n, cols = wt.shape
    tr = math.gcd(math.gcd(g_off, n), 256) if g_off else math.gcd(n, 256)
    off_blk = g_off // tr
    c1 = 1.0 / (1.0 - ADAM_B1 ** ADAM_STEP)
    c2 = 1.0 / (1.0 - ADAM_B2 ** ADAM_STEP)

    def body(g_ref, w_ref, m_ref, v_ref, go_ref, d_ref, mo_ref, vo_ref):
        gv = g_ref[...]
        mn = ADAM_B1 * m_ref[...] + (1.0 - ADAM_B1) * gv
        vn = ADAM_B2 * v_ref[...] + (1.0 - ADAM_B2) * gv * gv
        go_ref[...] = gv
        mo_ref[...] = mn
        vo_ref[...] = vn
        d_ref[...] = -ADAM_LR * ((mn * c1) / (jnp.sqrt(vn * c2) + ADAM_EPS) + ADAM_WD * w_ref[...])

    blk = pl.BlockSpec((tr, cols), lambda i: (i, 0))
    return _pcall(body, name=name, grid=(n // tr,),
                  in_specs=[pl.BlockSpec((tr, cols), lambda i: (off_blk + i, 0)), blk, blk, blk],
                  out_specs=[blk] * 4, out_shape=[jax.ShapeDtypeStruct((n, cols), F32)] * 4,
                  compiler_params=_params(("parallel",)))(gsrc, wt, m, v)


def _rows8(a):
    return -(-a.size // (8 * PACK_W)) * 8


def _as_rows(a, rows=None):
    flat = a.reshape(-1)
    n = _rows8(a) if rows is None else rows
    return jnp.pad(flat, (0, n * PACK_W - flat.shape[0])).reshape(n, PACK_W)


def local_shards_2d(wl):
    return {"w_ff1": [wl["w_ff1"][0], wl["w_ff1"][1]], "w_ff2": [wl["w_ff2"][0], wl["w_ff2"][1]],
            "ssm_w_glu": wl["ssm_w_glu"], "ssm_w_out": wl["ssm_w_out"], "kv_w_a": _pad_kva_cols(wl["kv_w_a"]),
            "kv_w_b": wl["kv_w_b"], "q_w_a": wl["q_w_a"], "q_w_b": _perm_q_cols(wl["q_w_b"]),
            "attn_w_o": wl["attn_w_o"], "ssm_d": wl["ssm_d"].reshape(2, -1)}


def misc_grad_shard(name, g, k):
    if name == "ssm_d":
        w = D_MODEL // N_CHIPS
        return g[:, w * k:w * (k + 1)]
    if name in ("ssm_w_glu", "kv_w_b"):
        return g[k]
    if name == "q_w_b":
        return _unperm_q_cols(g[k])
    rows = D_MODEL // N_CHIPS
    shard = g[rows * k:rows * (k + 1)]
    return _unpad_kva_cols(shard) if name == "kv_w_a" else shard


def packed_shards(g, names, rows, tail=None):
    blocks = []
    for k in range(N_CHIPS):
        parts = [_as_rows(misc_grad_shard(n, g[n], k), MISC_SHARD_ROWS[n]) for n in names]
        if tail is not None:
            parts.append(tail[k * (tail.shape[0] // N_CHIPS):(k + 1) * (tail.shape[0] // N_CHIPS)])
        blk = jnp.concatenate(parts, axis=0)
        blocks.append(jnp.pad(blk, ((0, rows - blk.shape[0]), (0, 0))))
    return jnp.stack(blocks)


def kernel(x, positions, ln_mix_g, ln_mix_b, ln_ffn_g, ln_ffn_b, w_ff1, w_ff2, ssm_lam_re, ssm_lam_im, ssm_log_dt, ssm_b_re, ssm_b_im, ssm_c_re, ssm_c_im, ssm_d, ssm_w_glu, ssm_w_out, kv_w_a, kv_norm_g, kv_w_b, q_w_a, q_norm_g, q_w_b, attn_w_o, loss_target, m_ln_mix_g, m_ln_mix_b, m_ln_ffn_g, m_ln_ffn_b, m_w_ff1, m_w_ff2, m_ssm_lam_re, m_ssm_lam_im, m_ssm_log_dt, m_ssm_b_re, m_ssm_b_im, m_ssm_c_re, m_ssm_c_im, m_ssm_d, m_ssm_w_glu, m_ssm_w_out, m_kv_w_a, m_kv_norm_g, m_kv_w_b, m_q_w_a, m_q_norm_g, m_q_w_b, m_attn_w_o, v_ln_mix_g, v_ln_mix_b, v_ln_ffn_g, v_ln_ffn_b, v_w_ff1, v_w_ff2, v_ssm_lam_re, v_ssm_lam_im, v_ssm_log_dt, v_ssm_b_re, v_ssm_b_im, v_ssm_c_re, v_ssm_c_im, v_ssm_d, v_ssm_w_glu, v_ssm_w_out, v_kv_w_a, v_kv_norm_g, v_kv_w_b, v_q_w_a, v_q_norm_g, v_q_w_b, v_attn_w_o):
    env = dict(locals())
    wl = {n: env[n] for n in WEIGHTS}
    ml = {n: env["m_" + n] for n in WEIGHTS}
    vl = {n: env["v_" + n] for n in WEIGHTS}
    for n in ("ssm_w_glu", "ssm_w_out", "q_w_a", "q_w_b", "attn_w_o"):
        wl[n], ml[n], vl[n] = wl[n][0], ml[n][0], vl[n][0]

    c_idx = lax.axis_index("c").astype(jnp.int32).reshape(1)
    me_idx = (2 * lax.axis_index("x") + lax.axis_index("y")).astype(jnp.int32).reshape(1)

    local = local_shards_2d(wl)
    put = lambda a, n: place(a, me_idx, F32 if n == "ssm_d" else BF16, "place_" + n)
    stacked = {n: [put(a, f"{n}_{l}") for l, a in enumerate(local[n])] if isinstance(local[n], list) else put(local[n], n)
               for n in SHARDED}
    stacked["ssm_d"] = ride_alone(GatherRide([_halves(stacked["ssm_d"])]), "ssm_d_all_gather")[0].reshape(1, D_MODEL)
    for n in REPLICATED:
        stacked[n] = wl[n]

    loss_part, dx, early, late, g, (early_sums, early_got) = device_step(
        x[0], positions[0], loss_target[0], stacked, comm=(me_idx, c_idx))
    loss = lax.psum(loss_part, ("x", "y", "c"))

    small = jnp.concatenate([_as_rows(g[n]) for n in REPLICATED], axis=0)
    small = jnp.pad(small, ((0, SMALL_ROWS - small.shape[0]), (0, 0)))
    late = put_rows(late, packed_shards(g, MISC_LATE, LATE_ROWS - MISC_LATE_OFF, tail=small), MISC_LATE_OFF)
    late_sums = add_halves(late, ride_alone(SwapRide(late), "grad_swap_halves")[0], c_idx)
    late_got = ride_alone(SendRide(late_sums), "grad_send_to_owners")[0]
    where = jnp.concatenate([me_idx, c_idx])
    total_rows = EARLY_ROWS + LATE_ROWS
    reduced = sum_owner(early_sums, early_got, where, total_rows)
    reduced = join_halves(sum_owner(late_sums, late_got, where, total_rows, row_off=EARLY_ROWS, into=reduced))
    quarter = reduced[EARLY_ROWS + SMALL_OFF:EARLY_ROWS + SMALL_OFF + SMALL_Q_ROWS]
    small_tot = ride_alone(GatherRide([_halves(place(quarter, me_idx, F32, "place_small_grads"))]),
                           "small_grad_all_gather")[0].reshape(SMALL_ROWS, PACK_W)

    out_g, out_d, out_m, out_v = {}, {}, {}, {}
    direct = {**EARLY_OFF, **{n: EARLY_ROWS + o for n, o in LATE_OFF.items()}}
    for n, off in direct.items():
        res = adamw(reduced, off, wl[n].reshape(-1, PACK_W), ml[n].reshape(-1, PACK_W), vl[n].reshape(-1, PACK_W),
                    "adamw_" + n)
        out_g[n], out_d[n], out_m[n], out_v[n] = [a.reshape(env[n].shape) for a in res]
    late_offs = [EARLY_ROWS + MISC_LATE_OFF + sum(MISC_SHARD_ROWS[m] for m in MISC_LATE[:i]) for i in range(len(MISC_LATE))]
    for names, off in [(MISC_EARLY, MISC_EARLY_OFF)] + [((n,), o) for n, o in zip(MISC_LATE, late_offs)]:
        pack3 = lambda d: jnp.concatenate([_as_rows(d[n], MISC_SHARD_ROWS[n]) for n in names], axis=0)
        res = adamw(reduced, off, pack3(wl), pack3(ml), pack3(vl), "adamw_packed_" + names[0])
        r0 = 0
        for n in names:
            cnt = math.prod(env[n].shape)
            out_g[n], out_d[n], out_m[n], out_v[n] = [
                a[r0:r0 + MISC_SHARD_ROWS[n]].reshape(-1)[:cnt].reshape(env[n].shape) for a in res]
            r0 += MISC_SHARD_ROWS[n]
    ws = jnp.concatenate([_as_rows(wl[n]) for n in REPLICATED], axis=0)
    ms = jnp.concatenate([_as_rows(ml[n]) for n in REPLICATED], axis=0)
    vs = jnp.concatenate([_as_rows(vl[n]) for n in REPLICATED], axis=0)
    pad = ((0, SMALL_ROWS - ws.shape[0]), (0, 0))
    res = adamw(small_tot, 0, jnp.pad(ws, pad), jnp.pad(ms, pad), jnp.pad(vs, pad), "adamw_replicated")
    row = 0
    for n in REPLICATED:
        cnt = math.prod(env[n].shape)
        nrows = _rows8(env[n])
        out_g[n], out_d[n], out_m[n], out_v[n] = [a[row:row + nrows].reshape(-1)[:cnt].reshape(env[n].shape) for a in res]
        row += nrows

    return (loss, dx[None], *[out_g[n] for n in WEIGHTS], *[out_d[n] for n in WEIGHTS],
            *[out_m[n] for n in WEIGHTS], *[out_v[n] for n in WEIGHTS])
```

```python
import functools
import math

import jax
import jax.numpy as jnp
from jax import lax
from jax.experimental import pallas as pl
from jax.experimental.pallas import tpu as pltpu

F32 = jnp.float32
BF16 = jnp.bfloat16
MESH = pl.DeviceIdType.MESH

D_MODEL = 1024
DEPTH = 2
SSM_GROUP = 16
N_GROUPS = D_MODEL // SSM_GROUP
SSM_STATE = 64
N_STATES = N_GROUPS * SSM_STATE
N_HEADS = 8
QK_NOPE = 128
QK_ROPE = 64
HALF_ROPE = QK_ROPE // 2
V_HEAD = 128
QK_DIM = QK_NOPE + QK_ROPE
Q_LORA = 384
KV_LORA = 256
ROPE_THETA = 10000.0
SM_SCALE = QK_DIM ** -0.5
NEG_INF = -1e30
D_FF = 4 * D_MODEL
DN_ALPHA = (2 * DEPTH) ** 0.25
LN_EPS = 1e-5
RMS_EPS = 1e-6
ADAM_LR = 0.001
ADAM_B1 = 0.9
ADAM_B2 = 0.999
ADAM_EPS = 1e-08
ADAM_WD = 0.01
ADAM_STEP = 10

N_CHIPS = 4
LANES = 128
VMEM_LIMIT = 56 * 1024 * 1024
MM_VMEM_BUDGET = 40 * 1024 * 1024
PACK_W = 1024
KVA_PAD = 384
HALF_W = PACK_W // 2

SHARDED = ("w_ff1", "w_ff2", "ssm_w_glu", "ssm_w_out", "kv_w_a", "kv_w_b", "q_w_a", "q_w_b", "attn_w_o", "ssm_d")
G_BLOCK_ROWS = 960
EARLY_OFF = {"w_ff1": 0, "w_ff2": 2048, "attn_w_o": 4096}
MISC_EARLY = ("kv_w_b", "kv_w_a", "q_w_a", "q_w_b")
MISC_EARLY_OFF = 4352
EARLY_ROWS = 5 * G_BLOCK_ROWS
LATE_OFF = {"ssm_w_out": 0}
SMALL_Q_ROWS = 96
SMALL_ROWS = N_CHIPS * SMALL_Q_ROWS
MISC_LATE = ("ssm_w_glu", "ssm_d")
MISC_LATE_OFF = 256
SMALL_OFF = MISC_LATE_OFF + 528
LATE_ROWS = G_BLOCK_ROWS
MISC_SHARD_ROWS = {"ssm_d": 16, "ssm_w_glu": 512, "kv_w_b": 128, "kv_w_a": 80, "q_w_a": 96, "q_w_b": 144}
REPLICATED = ("ln_mix_g", "ln_mix_b", "ln_ffn_g", "ln_ffn_b", "ssm_lam_re", "ssm_lam_im", "ssm_log_dt",
              "ssm_b_re", "ssm_b_im", "ssm_c_re", "ssm_c_im", "kv_norm_g", "q_norm_g")
WEIGHTS = ("ln_mix_g", "ln_mix_b", "ln_ffn_g", "ln_ffn_b", "w_ff1", "w_ff2", "ssm_lam_re", "ssm_lam_im",
           "ssm_log_dt", "ssm_b_re", "ssm_b_im", "ssm_c_re", "ssm_c_im", "ssm_d", "ssm_w_glu", "ssm_w_out",
           "kv_w_a", "kv_norm_g", "kv_w_b", "q_w_a", "q_norm_g", "q_w_b", "attn_w_o")


def _pcall(body, **kw):
    return pl.pallas_call(body, **kw)


def _params(sem=None):
    return pltpu.CompilerParams(dimension_semantics=sem, vmem_limit_bytes=VMEM_LIMIT)


_ANY = pl.BlockSpec(memory_space=pl.ANY)


def _tile(dim, prefs):
    for p in prefs:
        if dim % p == 0:
            return p
    return dim


def _place():
    x, y, c = lax.axis_index("x"), lax.axis_index("y"), lax.axis_index("c")
    return x, y, c, [(1 - x, y), (x, 1 - y), (1 - x, 1 - y)]


def _remote(k, src, dst, to, send_sems, recv_sems):
    return pltpu.make_async_remote_copy(src_ref=src, dst_ref=dst, send_sem=send_sems.at[k], recv_sem=recv_sems.at[k],
                                        device_id=to, device_id_type=MESH)


class GatherRide:
    def __init__(self, arrs):
        self.ins = list(arrs)
        self.out_shapes = [jax.ShapeDtypeStruct(a.shape, a.dtype) for a in arrs]
        self.aliases = {i: i for i in range(len(arrs))}
        self.n_sems = 6 * len(arrs)

    def start(self, ins, outs, send_sems, recv_sems):
        x, y, c, chips = _place()
        me = 2 * x + y
        for a, o in enumerate(outs):
            for j, (px, py) in enumerate(chips):
                _remote(6 * a + j, o.at[me, c], o.at[me, c], (px, py, c), send_sems, recv_sems).start()

    def pass_on(self, ins, outs, send_sems, recv_sems):
        x, y, c, chips = _place()
        for a, o in enumerate(outs):
            for j, (px, py) in enumerate(chips):
                blk = o.at[2 * px + py, c]
                _remote(6 * a + j, blk, blk, (px, py, c), send_sems, recv_sems).wait_recv()
                _remote(6 * a + 3 + j, blk, blk, (x, y, 1 - c), send_sems, recv_sems).start()

    def finish(self, ins, outs, send_sems, recv_sems, passed_on=False):
        if not passed_on:
            self.pass_on(ins, outs, send_sems, recv_sems)
        x, y, c, chips = _place()
        me = 2 * x + y
        sibling = (x, y, 1 - c)
        for a, o in enumerate(outs):
            for j, (px, py) in enumerate(chips):
                blk = o.at[2 * px + py, 1 - c]
                _remote(6 * a + 3 + j, blk, blk, sibling, send_sems, recv_sems).wait_recv()
                _remote(6 * a + j, o.at[me, c], o.at[me, c], (px, py, c), send_sems, recv_sems).wait_send()
                mine = o.at[2 * px + py, c]
                _remote(6 * a + 3 + j, mine, mine, sibling, send_sems, recv_sems).wait_send()


class SendRide:
    def __init__(self, part):
        self.ins = [part]
        self.out_shapes = [jax.ShapeDtypeStruct((3,) + part.shape[1:], part.dtype)]
        self.aliases = {}
        self.n_sems = 3

    def _copies(self, ins, outs, send_sems, recv_sems):
        x, y, c, chips = _place()
        return [_remote(j, ins[0].at[2 * px + py], outs[0].at[j], (px, py, c), send_sems, recv_sems)
                for j, (px, py) in enumerate(chips)]

    def start(self, ins, outs, send_sems, recv_sems):
        for cp in self._copies(ins, outs, send_sems, recv_sems):
            cp.start()

    def finish(self, ins, outs, send_sems, recv_sems):
        for cp in self._copies(ins, outs, send_sems, recv_sems):
            cp.wait()


class SwapRide:
    def __init__(self, pack):
        self.ins = [pack]
        self.out_shapes = [jax.ShapeDtypeStruct(pack.shape[:2] + (HALF_W,), pack.dtype)]
        self.aliases = {}
        self.n_sems = 1

    def _copy(self, ins, outs, send_sems, recv_sems):
        x, y, c, _ = _place()
        return _remote(0, ins[0].at[:, :, _my_cols(c, mine=False)], outs[0], (x, y, 1 - c), send_sems, recv_sems)

    def start(self, ins, outs, send_sems, recv_sems):
        self._copy(ins, outs, send_sems, recv_sems).start()

    def finish(self, ins, outs, send_sems, recv_sems):
        self._copy(ins, outs, send_sems, recv_sems).wait()


def _pcall_riding(body, args, ride, first, last, *, in_specs, out_specs, out_shape, scratch_shapes=(), middle=None,
                  **kw):
    n_in, n_out = len(args), len(out_shape)
    if ride is None:
        return _pcall(body, in_specs=in_specs, out_specs=out_specs, out_shape=out_shape,
                      scratch_shapes=list(scratch_shapes), **kw)(*args), []
    k_in, k_out = len(ride.ins), len(ride.out_shapes)

    def riding(*refs):
        ins, r_in = refs[:n_in], refs[n_in:n_in + k_in]
        outs = refs[n_in + k_in:n_in + k_in + n_out]
        r_out = refs[n_in + k_in + n_out:n_in + k_in + n_out + k_out]
        scratch, (send_sems, recv_sems) = refs[n_in + k_in + n_out + k_out:-2], refs[-2:]

        @pl.when(first())
        def _():
            ride.start(r_in, r_out, send_sems, recv_sems)

        if middle is not None:
            @pl.when(middle())
            def _():
                ride.pass_on(r_in, r_out, send_sems, recv_sems)

        body(*ins, *outs, *scratch)

        @pl.when(last())
        def _():
            if middle is not None:
                ride.finish(r_in, r_out, send_sems, recv_sems, passed_on=True)
            else:
                ride.finish(r_in, r_out, send_sems, recv_sems)

    res = _pcall(riding, in_specs=list(in_specs) + [_ANY] * k_in, out_specs=list(out_specs) + [_ANY] * k_out,
                 out_shape=list(out_shape) + ride.out_shapes,
                 input_output_aliases={n_in + i: n_out + o for i, o in ride.aliases.items()},
                 scratch_shapes=list(scratch_shapes) + [pltpu.SemaphoreType.DMA((ride.n_sems,))] * 2,
                 **kw)(*args, *ride.ins)
    return res[:n_out], res[n_out:]


def ride_alone(ride, name):
    def body(*refs):
        n = len(ride.ins)
        ins, outs, (send_sems, recv_sems) = refs[:n], refs[n:-2], refs[-2:]
        ride.start(ins, outs, send_sems, recv_sems)
        ride.finish(ins, outs, send_sems, recv_sems)

    return _pcall(body, name=name, in_specs=[_ANY] * len(ride.ins), out_specs=[_ANY] * len(ride.out_shapes),
                  out_shape=ride.out_shapes, input_output_aliases=dict(ride.aliases),
                  scratch_shapes=[pltpu.SemaphoreType.DMA((ride.n_sems,))] * 2)(*ride.ins)


def mm(a, b, *, name, ta=False, tb=False, pro_a=None, epi=None, extras=(), out_dtypes=(F32,), n_dim=None,
       tiles=(None, None, None), b_view=None, out_view=None, into=None, ride=None):
    if ta:
        k_dim, m_dim = a.shape
    else:
        m_dim, k_dim = a.shape
    if n_dim is None:
        n_dim = b.shape[0] if tb else b.shape[1]
    tn = tiles[1] or (n_dim if n_dim <= 1024 else _tile(n_dim, (1024, 512, 256, 128)))
    tk = tiles[2] or (k_dim if k_dim <= 1024 else _tile(k_dim, (1024, 512, 256, 128)))
    nk = k_dim // tk

    def vmem_bytes(tm):
        blocks = tm * tk * a.dtype.itemsize + tk * tn * b.dtype.itemsize
        blocks += tm * tn * (sum(e.dtype.itemsize for e in extras) + sum(jnp.dtype(d).itemsize for d in out_dtypes))
        return 2 * blocks + tm * tn * 4

    tm = tiles[0] or next((t for t in (4096, 2048, 1024, 512, 256) if m_dim % t == 0 and vmem_bytes(t) <= MM_VMEM_BUDGET),
                          _tile(m_dim, (128,)))
    assert m_dim % tm == 0 and n_dim % tn == 0 and k_dim % tk == 0, (name, m_dim, n_dim, k_dim, tm, tn, tk)
    n_ex, n_out = len(extras), len(out_dtypes)
    n_into = 0 if into is None else 1
    dims = (((0 if ta else 1,), (1 if tb else 0,)), ((), ()))

    def body(a_ref, b_ref, *rest):
        ex_refs, out_refs = rest[:n_ex], rest[n_ex + n_into:n_ex + n_into + n_out]

        def partial():
            av = a_ref[...]
            if pro_a is not None:
                av = pro_a(av)
            return lax.dot_general(av.astype(BF16), b_ref[...].astype(BF16), dims, preferred_element_type=F32)

        def finish(r):
            res = epi(r, *[e[...] for e in ex_refs]) if epi is not None else (r,)
            for o_ref, v in zip(out_refs, res):
                o_ref[...] = v.astype(o_ref.dtype)

        if nk == 1:
            finish(partial())
            return
        acc = rest[-1]
        k = pl.program_id(2)

        @pl.when(k == 0)
        def _():
            acc[...] = partial()

        @pl.when(k > 0)
        def _():
            acc[...] += partial()

        @pl.when(k == nk - 1)
        def _():
            finish(acc[...])

    a_spec = pl.BlockSpec((tk, tm), lambda i, j, k: (k, i)) if ta else pl.BlockSpec((tm, tk), lambda i, j, k: (i, k))
    if b_view is not None:
        b_spec = b_view(tk, tn)
    else:
        b_spec = pl.BlockSpec((tn, tk), lambda i, j, k: (j, k)) if tb else pl.BlockSpec((tk, tn), lambda i, j, k: (k, j))
    o_spec = pl.BlockSpec((tm, tn), lambda i, j, k: (i, j))
    if out_view is None:
        out_specs = [o_spec] * n_out
        out_shape = [jax.ShapeDtypeStruct((m_dim, n_dim), dt) for dt in out_dtypes]
    else:
        assert n_out == 1
        out_specs = [out_view[1](tm, tn)]
        out_shape = [jax.ShapeDtypeStruct(out_view[0], out_dtypes[0])]
    grid = (m_dim // tm, n_dim // tn, nk)
    scratch = [pltpu.VMEM((tm, tn), F32)] if nk > 1 else []
    if ride is not None:
        assert into is None
        at = lambda ids: functools.reduce(jnp.logical_and, [pl.program_id(d) == i for d, i in enumerate(ids)])
        outs, landed = _pcall_riding(
            body, (a, b, *extras), ride, lambda: at((0, 0, 0)), lambda: at([g - 1 for g in grid]),
            name=name, grid=grid, in_specs=[a_spec, b_spec] + [o_spec] * n_ex, out_specs=out_specs,
            out_shape=out_shape, scratch_shapes=scratch, compiler_params=_params(("arbitrary",) * 3))
        return (outs[0] if n_out == 1 else outs), landed
    outs = _pcall(
        body, name=name, grid=grid,
        in_specs=[a_spec, b_spec] + [o_spec] * n_ex + [_ANY] * n_into,
        out_specs=out_specs, out_shape=out_shape,
        input_output_aliases={2 + n_ex: 0} if n_into else {},
        scratch_shapes=scratch,
        compiler_params=_params(("parallel", "parallel", "arbitrary")),
    )(a, b, *extras, *([into] if n_into else []))
    return outs[0] if n_out == 1 else outs


def rowwise(fn, ins, outs, *, name, accs=(), tm=256):
    rows = ins[0].shape[0]
    tm = min(tm, rows)
    n_in, n_out, n_acc = len(ins), len(outs), len(accs)

    def body(*refs):
        in_refs, out_refs, acc_refs = refs[:n_in], refs[n_in:n_in + n_out], refs[n_in + n_out:]
        res, sums = fn(*[r[...] for r in in_refs])
        for o_ref, v in zip(out_refs, res):
            o_ref[...] = v.astype(o_ref.dtype)
        if n_acc:
            @pl.when(pl.program_id(0) == 0)
            def _():
                for a_ref in acc_refs:
                    a_ref[...] = jnp.zeros_like(a_ref)

            for a_ref, s in zip(acc_refs, sums):
                a_ref[...] += s

    def spec(arr):
        if arr.shape[0] == rows:
            return pl.BlockSpec((tm, arr.shape[1]), lambda i: (i, 0))
        return pl.BlockSpec(arr.shape, lambda i: (0, 0))

    res = _pcall(
        body, name=name, grid=(rows // tm,),
        in_specs=[spec(a) for a in ins],
        out_specs=[pl.BlockSpec((tm, w), lambda i: (i, 0)) for w, _ in outs]
        + [pl.BlockSpec((1, w), lambda i: (0, 0)) for w in accs],
        out_shape=[jax.ShapeDtypeStruct((rows, w), dt) for w, dt in outs]
        + [jax.ShapeDtypeStruct((1, w), F32) for w in accs],
        compiler_params=_params(("arbitrary",) if n_acc else ("parallel",)),
    )(*ins)
    return res


def _relu2(v):
    r = jnp.maximum(v, 0.0)
    return r * r


def _gelu(x):
    c = math.sqrt(2.0 / math.pi)
    return 0.5 * x * (1.0 + jnp.tanh(c * (x + 0.044715 * x * x * x)))


def _gelu_grad(x):
    c = math.sqrt(2.0 / math.pi)
    t = jnp.tanh(c * (x + 0.044715 * x * x * x))
    return 0.5 * (1.0 + t) + 0.5 * x * (1.0 - t * t) * c * (1.0 + 3 * 0.044715 * x * x)


def _sigmoid(x):
    return 1.0 / (1.0 + jnp.exp(-x))


def _layer_norm(h, mix, g, b):
    r = DN_ALPHA * h + mix
    mu = jnp.mean(r, axis=-1, keepdims=True)
    xc = r - mu
    var = jnp.mean(xc * xc, axis=-1, keepdims=True)
    return xc * lax.rsqrt(var + LN_EPS) * g + b


def ln_fwd(h, mix, g, b, name):
    def fn(h, mix, g, b):
        y = _layer_norm(h, mix, g, b)
        return (y, y), ()
    return rowwise(fn, (h, mix, g, b), ((D_MODEL, F32), (D_MODEL, BF16)), name=name)


def ln_bwd(h, mix, g, dy, name):
    def fn(h, mix, g, dy):
        r = DN_ALPHA * h + mix
        mu = jnp.mean(r, axis=-1, keepdims=True)
        xc = r - mu
        var = jnp.mean(xc * xc, axis=-1, keepdims=True)
        rstd = lax.rsqrt(var + LN_EPS)
        xhat = xc * rstd
        dxh = dy * g
        m1 = jnp.mean(dxh, axis=-1, keepdims=True)
        m2 = jnp.mean(dxh * xhat, axis=-1, keepdims=True)
        dr = rstd * (dxh - m1 - xhat * m2)
        return (dr, dr), (jnp.sum(dy * xhat, axis=0, keepdims=True), jnp.sum(dy, axis=0, keepdims=True))
    return rowwise(fn, (h, mix, g, dy), ((D_MODEL, F32), (D_MODEL, BF16)), accs=(D_MODEL, D_MODEL), name=name)


def _rms(x, g):
    r = lax.rsqrt(jnp.mean(x * x, axis=-1, keepdims=True) + RMS_EPS)
    return x * r * g


def _rms_bwd(x, g, dy):
    r = lax.rsqrt(jnp.mean(x * x, axis=-1, keepdims=True) + RMS_EPS)
    xn = x * r
    dyg = dy * g
    dx = r * (dyg - xn * jnp.mean(dyg * xn, axis=-1, keepdims=True))
    return dx, jnp.sum(dy * xn, axis=0, keepdims=True)


def _s5_disc(lr, li, ldt):
    dt = jnp.exp(ldt)
    mag = jnp.exp(lr * dt)
    cs, sn = jnp.cos(li * dt), jnp.sin(li * dt)
    ar, ai = mag * cs, mag * sn
    inv = 1.0 / (lr * lr + li * li)
    n_re = (ar - 1.0) * lr + ai * li
    n_im = ai * lr - (ar - 1.0) * li
    return dt, mag, cs, sn, ar, ai, inv, n_re, n_im


def s5_prep(lr, li, ldt, b_re, b_im):
    def fn(lr, li, ldt, b_re, b_im):
        _, _, _, _, ar, ai, inv, n_re, n_im = _s5_disc(lr, li, ldt)
        cr, ci = n_re * inv, n_im * inv
        return (ar, ai, cr * b_re - ci * b_im, cr * b_im + ci * b_re), ()
    return rowwise(fn, (lr, li, ldt, b_re, b_im), ((1, F32), (1, F32), (SSM_GROUP, F32), (SSM_GROUP, F32)),
                   name="s5_prep", tm=512)


def s5_prep_bwd(lr, li, ldt, b_re, b_im, dar, dai, dbb_re, dbb_im):
    def fn(lr, li, ldt, b_re, b_im, dar, dai, dbb_re, dbb_im):
        dt, mag, cs, sn, ar, ai, inv, n_re, n_im = _s5_disc(lr, li, ldt)
        cr, ci = n_re * inv, n_im * inv
        db_re = cr * dbb_re + ci * dbb_im
        db_im = cr * dbb_im - ci * dbb_re
        dcr = jnp.sum(dbb_re * b_re + dbb_im * b_im, axis=-1, keepdims=True)
        dci = jnp.sum(dbb_im * b_re - dbb_re * b_im, axis=-1, keepdims=True)
        dar = dar + (dcr * lr - dci * li) * inv
        dai = dai + (dcr * li + dci * lr) * inv
        dinv = dcr * n_re + dci * n_im
        dlr = (dcr * (ar - 1.0) + dci * ai) * inv - 2.0 * lr * inv * inv * dinv
        dli = (dcr * ai - dci * (ar - 1.0)) * inv - 2.0 * li * inv * inv * dinv
        dmag = dar * cs + dai * sn
        dth = dai * ar - dar * ai
        dlr = dlr + dmag * mag * dt
        dli = dli + dth * dt
        ddt = dmag * mag * lr + dth * li
        return (dlr, dli, ddt * dt, db_re, db_im), ()
    return rowwise(fn, (lr, li, ldt, b_re, b_im, dar, dai, dbb_re, dbb_im),
                   ((1, F32), (1, F32), (1, F32), (SSM_GROUP, F32), (SSM_GROUP, F32)), name="s5_prep_bwd", tm=512)


def group_sum(x):
    def body(x_ref, o_ref):
        o_ref[...] = jnp.sum(x_ref[...], axis=1)
    return _pcall(body, name="s5_group_sum", out_shape=jax.ShapeDtypeStruct((N_GROUPS, 1), F32))(
        x.reshape(N_GROUPS, SSM_STATE, 1))


GROUPS_PER_TILE = LANES // SSM_GROUP
TILE_STATES = GROUPS_PER_TILE * SSM_STATE
N_UTILES = D_MODEL // LANES
TILES_PER_UTILE = TILE_STATES // LANES


SUBLANES = 8
SCAN_STRIP = 1024
N_STRIPS = N_STATES // SCAN_STRIP
_NT = (((1,), (1,)), ((), ()))
_TN = (((0,), (0,)), ((), ()))


def _scan_coefs(are, aim, shifted, reverse):
    ar = are[...]
    ai = -aim[...] if reverse else aim[...]
    powers = {1: (ar, ai)}
    for d in (2, 4):
        r, i = powers[d // 2]
        powers[d] = (r * r - i * i, 2.0 * r * i)
    rid = lax.broadcasted_iota(jnp.int32, (SUBLANES, N_STATES), 0)
    first = (rid == SUBLANES - 1) if reverse else (rid == 0)
    masks = [(1, first)] + [(d, (rid <= SUBLANES - 1 - d) if reverse else (rid >= d)) for d in (1, 2, 4)]
    for n, (d, keep) in enumerate(masks):
        for part in (0, 1):
            shifted[2 * n + part][...] = jnp.where(keep, jnp.broadcast_to(powers[d][part], (SUBLANES, N_STATES)), 0.0)


def _tile_scan(xr, xi, shifted, nbr_re, nbr_im, reverse):
    for n, d in enumerate((1, 1, 2, 4)):
        by = SUBLANES - d if reverse else d
        fr, fi = (nbr_re, nbr_im) if n == 0 else (xr, xi)
        sr, si = pltpu.roll(fr, by, 0), pltpu.roll(fi, by, 0)
        kr, ki = shifted[2 * n], shifted[2 * n + 1]
        xr, xi = xr + kr * sr - ki * si, xi + kr * si + ki * sr
    return xr, xi


def _tile_rows(t):
    return pl.ds(pl.multiple_of(t * SUBLANES, SUBLANES), SUBLANES)


def s5_fwd(u, bbd_re, bbd_im, cbd_re, cbd_imn, a_re, a_im, dskip, ride=None, t_rows=256):
    seq = u.shape[0]
    t_rows = min(t_rows, seq)
    n_tiles = t_rows // SUBLANES

    def body(u_ref, bre, bim, cre, cimn, are, aim, d_ref, y_ref, gelu_ref, hre_ref, him_ref, car_re, car_im, *shifted):
        @pl.when(pl.program_id(0) == 0)
        def _():
            car_re[...] = jnp.zeros_like(car_re)
            car_im[...] = jnp.zeros_like(car_im)
            _scan_coefs(are, aim, shifted, reverse=False)

        uf = u_ref[...]
        ub = uf.astype(BF16)
        for j in range(N_UTILES):
            uj = ub[:, LANES * j:LANES * (j + 1)]
            sl = slice(TILE_STATES * j, TILE_STATES * (j + 1))
            hre_ref[:, sl] = jnp.dot(uj, bre[j], preferred_element_type=F32)
            him_ref[:, sl] = jnp.dot(uj, bim[j], preferred_element_type=F32)
        for s in range(N_STRIPS):
            cols = pl.ds(s * SCAN_STRIP, SCAN_STRIP)
            coefs = [c[:, cols] for c in shifted]

            def step(t, before):
                rows = _tile_rows(t)
                hr, hi = _tile_scan(hre_ref[rows, cols], him_ref[rows, cols], coefs, before[0], before[1], False)
                hre_ref[rows, cols] = hr
                him_ref[rows, cols] = hi
                return hr, hi

            cr, ci = lax.fori_loop(0, n_tiles, step, (car_re[:, cols], car_im[:, cols]))
            car_re[:, cols] = cr
            car_im[:, cols] = ci
        dv = d_ref[...]
        for j in range(N_UTILES):
            st = slice(TILE_STATES * j, TILE_STATES * (j + 1))
            yj = (jnp.dot(hre_ref[:, st].astype(BF16), cre[j], preferred_element_type=F32)
                  + jnp.dot(him_ref[:, st].astype(BF16), cimn[j], preferred_element_type=F32))
            sl = slice(LANES * j, LANES * (j + 1))
            yj = yj + dv[:, sl] * uf[:, sl]
            y_ref[:, sl] = yj
            gelu_ref[:, sl] = _gelu(yj).astype(gelu_ref.dtype)

    full3 = lambda a: pl.BlockSpec(a.shape, lambda i: (0, 0, 0))
    full2 = lambda a: pl.BlockSpec(a.shape, lambda i: (0, 0))
    tile = pltpu.VMEM((SUBLANES, N_STATES), F32)
    n_chunks = seq // t_rows
    return _pcall_riding(
        body, (u, bbd_re, bbd_im, cbd_re, cbd_imn, a_re, a_im, dskip), ride,
        lambda: pl.program_id(0) == 0, lambda: pl.program_id(0) == n_chunks - 1,
        middle=(lambda: pl.program_id(0) == (7 * n_chunks) // 8) if ride is not None else None,
        name="s5_fwd", grid=(n_chunks,),
        in_specs=[pl.BlockSpec((t_rows, D_MODEL), lambda i: (i, 0)), full3(bbd_re), full3(bbd_im), full3(cbd_re),
                  full3(cbd_imn), full2(a_re), full2(a_im), full2(dskip)],
        out_specs=[pl.BlockSpec((t_rows, D_MODEL), lambda i: (i, 0)),
                   pl.BlockSpec((t_rows, D_MODEL), lambda i: (i, 0)),
                   pl.BlockSpec((t_rows, N_STATES), lambda i: (i, 0)),
                   pl.BlockSpec((t_rows, N_STATES), lambda i: (i, 0))],
        out_shape=[jax.ShapeDtypeStruct((seq, D_MODEL), F32),
                   jax.ShapeDtypeStruct((seq, D_MODEL), BF16),
                   jax.ShapeDtypeStruct((seq, N_STATES), F32),
                   jax.ShapeDtypeStruct((seq, N_STATES), F32)],
        scratch_shapes=[tile] * 10,
        compiler_params=_params(("arbitrary",)))


def s5_bwd(dy, u, dres, h_re, h_im, bbd_re, bbd_im, cbd_re, cbd_imn, a_re, a_im, dskip, ride=None, t_rows=128):
    seq = u.shape[0]
    t_rows = min(t_rows, seq)
    n_chunks = seq // t_rows

    n_tiles = t_rows // SUBLANES

    def body(dy_ref, u_ref, dres_ref, hre_ref, him_ref, hpre_ref, hpim_ref, bre, bim, cre, cimn, are, aim, d_ref,
             dx_ref, dbre, dbim, dcre, dcimn, dar_ref, dai_ref, dd_ref, lre, lim, car_re, car_im, acc_re, acc_im,
             *shifted):
        i = pl.program_id(0)

        @pl.when(i == 0)
        def _():
            for r in (car_re, car_im, acc_re, acc_im, dbre, dbim, dcre, dcimn, dd_ref):
                r[...] = jnp.zeros_like(r)
            _scan_coefs(are, aim, shifted, reverse=True)

        dyf = dy_ref[...]
        dyb = dyf.astype(BF16)
        uf = u_ref[...]
        ub = uf.astype(BF16)
        for j in range(N_UTILES):
            dyj = dyb[:, LANES * j:LANES * (j + 1)]
            st = slice(TILE_STATES * j, TILE_STATES * (j + 1))
            lre[:, st] = lax.dot_general(dyj, cre[j], _NT, preferred_element_type=F32)
            lim[:, st] = lax.dot_general(dyj, cimn[j], _NT, preferred_element_type=F32)
        has_pred = (i < n_chunks - 1).astype(F32)
        last_row = lax.broadcasted_iota(jnp.int32, (SUBLANES, SCAN_STRIP), 0) == SUBLANES - 1
        for s in range(N_STRIPS):
            cols = pl.ds(s * SCAN_STRIP, SCAN_STRIP)
            coefs = [c[:, cols] for c in shifted]
            before_re, before_im = hpre_ref[:, cols] * has_pred, hpim_ref[:, cols] * has_pred

            def step(k, carry):
                after_re, after_im, dar, dai = carry
                t = n_tiles - 1 - k
                rows = _tile_rows(t)
                lr, li = _tile_scan(lre[rows, cols], lim[rows, cols], coefs, after_re, after_im, True)
                lre[rows, cols] = lr
                lim[rows, cols] = li
                prev = _tile_rows(jnp.maximum(t - 1, 0))
                pre_re = jnp.where(t == 0, before_re, hre_ref[prev, cols])
                pre_im = jnp.where(t == 0, before_im, him_ref[prev, cols])
                hpr = pltpu.roll(jnp.where(last_row, pre_re, hre_ref[rows, cols]), 1, 0)
                hpi = pltpu.roll(jnp.where(last_row, pre_im, him_ref[rows, cols]), 1, 0)
                return lr, li, dar + lr * hpr + li * hpi, dai + li * hpr - lr * hpi

            cr, ci, dar, dai = lax.fori_loop(0, n_tiles, step, (car_re[:, cols], car_im[:, cols],
                                                               acc_re[:, cols], acc_im[:, cols]))
            car_re[:, cols] = cr
            car_im[:, cols] = ci
            acc_re[:, cols] = dar
            acc_im[:, cols] = dai

        dv = d_ref[...]
        for j in range(N_UTILES):
            sl = slice(LANES * j, LANES * (j + 1))
            st = slice(TILE_STATES * j, TILE_STATES * (j + 1))
            lrj = lre[:, st].astype(BF16)
            lij = lim[:, st].astype(BF16)
            du = (lax.dot_general(lrj, bre[j], _NT, preferred_element_type=F32)
                  + lax.dot_general(lij, bim[j], _NT, preferred_element_type=F32))
            dx_ref[:, sl] = du + dv[:, sl] * dyf[:, sl] + DN_ALPHA * dres_ref[:, sl]
            uj = ub[:, sl]
            dbre[j] += lax.dot_general(uj, lrj, _TN, preferred_element_type=F32)
            dbim[j] += lax.dot_general(uj, lij, _TN, preferred_element_type=F32)
            dyj = dyb[:, sl]
            dcre[j] += lax.dot_general(hre_ref[:, st].astype(BF16), dyj, _TN, preferred_element_type=F32)
            dcimn[j] += lax.dot_general(him_ref[:, st].astype(BF16), dyj, _TN, preferred_element_type=F32)
        dd_ref[...] += jnp.sum(dyf * uf, axis=0, keepdims=True)

        @pl.when(i == n_chunks - 1)
        def _():
            dar_ref[...] = jnp.sum(acc_re[...], axis=0, keepdims=True)
            dai_ref[...] = jnp.sum(acc_im[...], axis=0, keepdims=True)

    rev = lambda i: (n_chunks - 1 - i, 0)
    prev_tile = lambda i: (jnp.maximum((n_chunks - 1 - i) * n_tiles - 1, 0), 0)
    full3 = lambda a: pl.BlockSpec(a.shape, lambda i: (0, 0, 0))
    full2 = lambda a: pl.BlockSpec(a.shape, lambda i: (0, 0))
    acc3 = lambda shape: pl.BlockSpec(shape, lambda i: (0, 0, 0))
    acc2 = lambda shape: pl.BlockSpec(shape, lambda i: (0, 0))
    tile = pltpu.VMEM((SUBLANES, N_STATES), F32)
    return _pcall_riding(
        body, (dy, u, dres, h_re, h_im, h_re, h_im, bbd_re, bbd_im, cbd_re, cbd_imn, a_re, a_im, dskip), ride,
        lambda: pl.program_id(0) == 0, lambda: pl.program_id(0) == n_chunks - 1,
        name="s5_bwd", grid=(n_chunks,),
        in_specs=[pl.BlockSpec((t_rows, D_MODEL), rev), pl.BlockSpec((t_rows, D_MODEL), rev),
                  pl.BlockSpec((t_rows, D_MODEL), rev),
                  pl.BlockSpec((t_rows, N_STATES), rev), pl.BlockSpec((t_rows, N_STATES), rev),
                  pl.BlockSpec((SUBLANES, N_STATES), prev_tile), pl.BlockSpec((SUBLANES, N_STATES), prev_tile),
                  full3(bbd_re), full3(bbd_im), full3(cbd_re), full3(cbd_imn), full2(a_re), full2(a_im), full2(dskip)],
        out_specs=[pl.BlockSpec((t_rows, D_MODEL), rev), acc3(bbd_re.shape), acc3(bbd_im.shape), acc3(cbd_re.shape),
                   acc3(cbd_imn.shape), acc2((1, N_STATES)), acc2((1, N_STATES)), acc2((1, D_MODEL))],
        out_shape=[jax.ShapeDtypeStruct((seq, D_MODEL), F32), jax.ShapeDtypeStruct(bbd_re.shape, F32),
                   jax.ShapeDtypeStruct(bbd_im.shape, F32), jax.ShapeDtypeStruct(cbd_re.shape, F32),
                   jax.ShapeDtypeStruct(cbd_imn.shape, F32), jax.ShapeDtypeStruct((1, N_STATES), F32),
                   jax.ShapeDtypeStruct((1, N_STATES), F32), jax.ShapeDtypeStruct((1, D_MODEL), F32)],
        scratch_shapes=[pltpu.VMEM((t_rows, N_STATES), F32), pltpu.VMEM((t_rows, N_STATES), F32)] + [tile] * 12,
        compiler_params=_params(("arbitrary",)))


def _eye_groups():
    return jnp.eye(GROUPS_PER_TILE, dtype=F32)


def _blockdiag_in(bb):
    t = bb.transpose(0, 2, 1).reshape(N_UTILES, GROUPS_PER_TILE, SSM_GROUP, SSM_STATE)
    bd = jnp.einsum("jgcp,gh->jgchp", t, _eye_groups())
    return bd.reshape(N_UTILES, LANES, TILE_STATES)


def _blockdiag_in_t(d):
    t = jnp.einsum("jgchp,gh->jgcp", d.reshape(N_UTILES, GROUPS_PER_TILE, SSM_GROUP, GROUPS_PER_TILE, SSM_STATE),
                   _eye_groups())
    return t.reshape(N_GROUPS, SSM_GROUP, SSM_STATE).transpose(0, 2, 1)


def _blockdiag_out(c):
    t = c.transpose(0, 2, 1).reshape(N_UTILES, GROUPS_PER_TILE, SSM_STATE, SSM_GROUP)
    bd = jnp.einsum("jhpc,hg->jhpgc", t, _eye_groups())
    return bd.reshape(N_UTILES, TILE_STATES, LANES)


def _blockdiag_out_t(d):
    t = jnp.einsum("jhpgc,hg->jhpc", d.reshape(N_UTILES, GROUPS_PER_TILE, SSM_STATE, GROUPS_PER_TILE, SSM_GROUP),
                   _eye_groups())
    return t.reshape(N_GROUPS, SSM_STATE, SSM_GROUP).transpose(0, 2, 1)


ATT_TQ = 512
ATT_TK = 512
LOG2E = math.log2(math.e)
LN2 = math.log(2.0)
Q_PRESCALE = SM_SCALE * LOG2E


def _loop_in_pairs(n, step, carry, start=0):
    pairs = (n - start) // 2

    def two(t, c):
        return step(start + 2 * t + 1, step(start + 2 * t, c))

    carry = lax.fori_loop(0, pairs, two, carry)
    return lax.fori_loop(start + 2 * pairs, n, step, carry)


def _causal(s, off=0, transposed=False):
    r = lax.broadcasted_iota(jnp.int32, s.shape, 0)
    c = lax.broadcasted_iota(jnp.int32, s.shape, 1)
    keep = (r <= c + off) if transposed else (c <= r + off)
    return jnp.where(keep, s, NEG_INF)


def _q_specs(rows, at):
    def nope(*ids):
        r, h = at(*ids)
        return r, 3 * (h // HEADS_PER_CHIP) + h % HEADS_PER_CHIP

    def rope(*ids):
        r, h = at(*ids)
        return r, 3 * (h // HEADS_PER_CHIP) + HEADS_PER_CHIP

    return [pl.BlockSpec((rows, LANES), nope), pl.BlockSpec((rows, LANES), rope)]


def _kv_specs(rows, at):
    def col(f):
        def index(*ids):
            r, h = at(*ids)
            return r, f(h)
        return index

    return [pl.BlockSpec((rows, LANES), col(lambda h: 2 * h)), pl.BlockSpec((rows, LANES), col(lambda h: h % HEADS_PER_CHIP)),
            pl.BlockSpec((rows, LANES), col(lambda h: 2 * h + 1))]


def _cat(a, b):
    return jnp.concatenate([a, b], axis=1)


def attn_fwd(q, kv, kr, ride=None, tq=ATT_TQ, tk=ATT_TK):
    seq = q.shape[0]
    n_heads = N_HEADS
    tq, tk = min(tq, seq), min(tk, seq)

    def body(qn_ref, qr_ref, kn_ref, kr_ref, v_ref, o_ref, lse_ref):
        qi = pl.program_id(1)
        qv = _cat(qn_ref[...], qr_ref[...])
        jd = (qi * tq) // tk

        def block(j, carry, diag):
            m, l, acc = carry
            rows = pl.ds(pl.multiple_of(j * tk, tk), tk)
            s = lax.dot_general(qv, _cat(kn_ref[rows, :], kr_ref[rows, :]), _NT, preferred_element_type=F32)
            if diag:
                s = _causal(s, qi * tq - jd * tk)
            m_new = jnp.maximum(m, jnp.max(s, axis=-1, keepdims=True))
            p = jnp.exp2(s - m_new)
            corr = jnp.exp2(m - m_new)
            l = l * corr + jnp.sum(p, axis=-1, keepdims=True)
            acc = acc * corr + jnp.dot(p.astype(BF16), v_ref[rows, :], preferred_element_type=F32)
            return m_new, l, acc

        init = (jnp.full((tq, 1), NEG_INF, F32), jnp.zeros((tq, 1), F32), jnp.zeros((tq, V_HEAD), F32))
        carry = _loop_in_pairs(jd, lambda j, c: block(j, c, False), init)
        m, l, acc = block(jd, carry, True)
        o_ref[...] = acc / l
        lse_ref[0] = jnp.broadcast_to(m + jnp.log2(l), (tq, LANES))

    n_q = seq // tq
    return _pcall_riding(
        body, (q, q, kv, kr, kv), ride,
        lambda: (pl.program_id(0) == 0) & (pl.program_id(1) == 0),
        lambda: (pl.program_id(0) == n_heads - 1) & (pl.program_id(1) == n_q - 1),
        middle=(lambda: (pl.program_id(0) == (5 * n_heads) // 8) & (pl.program_id(1) == 0)) if ride is not None else None,
        name="attn_fwd", grid=(n_heads, n_q),
        in_specs=_q_specs(tq, lambda h, i: (i, h)) + _kv_specs(seq, lambda h, i: (0, h)),
        out_specs=[pl.BlockSpec((tq, V_HEAD), lambda h, i: (i, h)),
                   pl.BlockSpec((1, tq, LANES), lambda h, i: (h, i, 0))],
        out_shape=[jax.ShapeDtypeStruct((seq, n_heads * V_HEAD), F32),
                   jax.ShapeDtypeStruct((n_heads, seq, LANES), F32)],
        compiler_params=_params(("arbitrary", "arbitrary")))


def attn_bwd(q, kv, kr, do, lse_row, delta_row, tq=ATT_TK):
    seq = q.shape[0]
    tq = min(tq, seq)
    n_blk = seq // tq

    def body(qn_ref, qr_ref, kn_ref, kr_ref, v_ref, do_ref, lse_ref, delta_ref, dqn_ref, dqr_ref, dkv_ref, dkr_ref, dq_acc):
        head, kj = pl.program_id(0), pl.program_id(1)

        @pl.when(kj == 0)
        def _():
            dq_acc[...] = jnp.zeros_like(dq_acc)

        kc = _cat(kn_ref[...], kr_ref[...])
        vv = v_ref[...]

        def block(i, carry, diag):
            dk, dv = carry
            rows = pl.ds(pl.multiple_of(i * tq, tq), tq)
            qv = _cat(qn_ref[rows, :], qr_ref[rows, :])
            st = lax.dot_general(kc, qv, _NT, preferred_element_type=F32)
            if diag:
                st = _causal(st, transposed=True)
            pt = jnp.exp2(st - lse_ref[0, pl.ds(i, 1), :])
            dob = do_ref[rows, :].astype(BF16)
            dv = dv + jnp.dot(pt.astype(BF16), dob, preferred_element_type=F32)
            dpt = lax.dot_general(vv, dob, _NT, preferred_element_type=F32)
            dst = (pt * (dpt - delta_ref[0, pl.ds(i, 1), :])).astype(BF16)
            dk = dk + jnp.dot(dst, qv, preferred_element_type=F32)
            dq_acc[rows, :] += lax.dot_general(dst, kc, _TN, preferred_element_type=F32)
            return dk, dv

        carry = block(kj, (jnp.zeros((tq, 2 * LANES), F32), jnp.zeros((tq, V_HEAD), F32)), True)
        dk, dv = _loop_in_pairs(n_blk, lambda i, c: block(i, c, False), carry, start=kj + 1)
        dk = dk * LN2
        dkv_ref[...] = _cat(dk[:, :LANES], dv).astype(dkv_ref.dtype)
        lane = lax.broadcasted_iota(jnp.int32, (tq, LANES), 1)
        mine = (lane // HALF_ROPE) % HEADS_PER_CHIP == head % HEADS_PER_CHIP
        dkr_ref[0] = jnp.where(mine, dk[:, LANES:], 0.0)

        @pl.when(kj == n_blk - 1)
        def _():
            dqn_ref[...] = dq_acc[:, :LANES] * SM_SCALE

        @pl.when((kj == n_blk - 1) & (head % HEADS_PER_CHIP == 0))
        def _():
            dqr_ref[...] = dq_acc[:, LANES:] * SM_SCALE

        @pl.when((kj == n_blk - 1) & (head % HEADS_PER_CHIP > 0))
        def _():
            dqr_ref[...] += dq_acc[:, LANES:] * SM_SCALE

    return _pcall(
        body, name="attn_bwd", grid=(N_HEADS, n_blk),
        in_specs=_q_specs(seq, lambda h, j: (0, h)) + _kv_specs(tq, lambda h, j: (j, h))
        + [pl.BlockSpec((seq, V_HEAD), lambda h, j: (0, h)),
           pl.BlockSpec((1, n_blk, tq), lambda h, j: (h, 0, 0)),
           pl.BlockSpec((1, n_blk, tq), lambda h, j: (h, 0, 0))],
        out_specs=[pl.BlockSpec((seq, LANES), lambda h, j: (0, h)),
                   pl.BlockSpec((seq, LANES), lambda h, j: (0, h // HEADS_PER_CHIP)),
                   pl.BlockSpec((tq, QK_NOPE + V_HEAD), lambda h, j: (j, h)),
                   pl.BlockSpec((1, tq, LANES), lambda h, j: (h, j, 0))],
        out_shape=[jax.ShapeDtypeStruct((seq, N_HEADS * QK_NOPE), F32),
                   jax.ShapeDtypeStruct((seq, N_CHIPS * LANES), F32),
                   jax.ShapeDtypeStruct((seq, N_HEADS * (QK_NOPE + V_HEAD)), BF16),
                   jax.ShapeDtypeStruct((N_HEADS, seq, LANES), F32)],
        scratch_shapes=[pltpu.VMEM((seq, 2 * LANES), F32)],
        compiler_params=_params(("arbitrary", "arbitrary")),
    )(q, q, kv, kr, kv, do, lse_row, delta_row)


def head_sum(x, ts=512):
    n_heads, seq, w = x.shape
    ts = min(ts, seq)

    def body(x_ref, o_ref):
        o_ref[...] = jnp.sum(x_ref[...], axis=0)

    return _pcall(body, name="head_sum", grid=(seq // ts,),
                  in_specs=[pl.BlockSpec((n_heads, ts, w), lambda i: (0, i, 0))],
                  out_specs=pl.BlockSpec((ts, w), lambda i: (i, 0)),
                  out_shape=jax.ShapeDtypeStruct((seq, w), F32),
                  compiler_params=_params(("parallel",)))(x)


HEADS_PER_CHIP = N_HEADS // N_CHIPS
Q_CHIP = HEADS_PER_CHIP * QK_DIM
Q_CHIP_NOPE = HEADS_PER_CHIP * QK_NOPE


def _perm_q_cols(w):
    t = w.reshape(w.shape[0], HEADS_PER_CHIP, QK_DIM)
    return jnp.concatenate([t[:, :, :QK_NOPE].reshape(w.shape[0], -1),
                            t[:, :, QK_NOPE:QK_NOPE + HALF_ROPE].reshape(w.shape[0], -1),
                            t[:, :, QK_NOPE + HALF_ROPE:].reshape(w.shape[0], -1)], axis=1)


def _unperm_q_cols(w):
    r = w.shape[0]
    nope = w[:, :Q_CHIP_NOPE].reshape(r, HEADS_PER_CHIP, QK_NOPE)
    r1 = w[:, Q_CHIP_NOPE:Q_CHIP_NOPE + QK_ROPE].reshape(r, HEADS_PER_CHIP, HALF_ROPE)
    r2 = w[:, Q_CHIP_NOPE + QK_ROPE:].reshape(r, HEADS_PER_CHIP, HALF_ROPE)
    return jnp.concatenate([nope, r1, r2], axis=2).reshape(r, Q_CHIP)


def _pad_kva_cols(w):
    z = jnp.zeros((w.shape[0], HALF_ROPE), w.dtype)
    return jnp.concatenate([w[:, :KV_LORA], w[:, KV_LORA:KV_LORA + HALF_ROPE], z, w[:, KV_LORA + HALF_ROPE:], z], axis=1)


def _unpad_kva_cols(w):
    return jnp.concatenate([w[:, :KV_LORA], w[:, KV_LORA:KV_LORA + HALF_ROPE],
                            w[:, KV_LORA + QK_ROPE:KV_LORA + QK_ROPE + HALF_ROPE]], axis=1)


def _rope_tile(t, cs, sn):
    return t * cs + pltpu.roll(t, LANES // 2, 1) * sn


def _rope_tile_bwd(d, cs, sn):
    return d * cs + pltpu.roll(d * sn, LANES // 2, 1)


def _b_cols(tk, tn):
    return pl.BlockSpec((None, tk, tn), lambda i, j, k: (j, k, 0))


def _b_cols_t(tk, tn):
    return pl.BlockSpec((None, tn, tk), lambda i, j, k: (k, j, 0))


def _out_cols(shape):
    return shape, lambda tm, tn: pl.BlockSpec((None, tm, tn), lambda i, j, k: (j, i, 0))


def _halves(a):
    return a.reshape(N_CHIPS, 2, a.shape[1] // 2, a.shape[2])


def device_step(x, positions, target, w, comm=None):
    seq = x.shape[0]
    w = dict(w)

    def gathered(names, outs):
        for n, a in zip(names, outs):
            if isinstance(n, tuple):
                w[n[0]] = [a.reshape(v.shape) if l == n[1] else v for l, v in enumerate(w[n[0]])]
            else:
                w[n] = a.reshape(w[n].shape)

    def ride_for(names):
        if comm is None:
            return None
        return GatherRide([_halves(w[n[0]][n[1]] if isinstance(n, tuple) else w[n]) for n in names])

    first_ride = ("ssm_w_glu", "ssm_w_out", ("w_ff1", 0), ("w_ff2", 0))
    mla_ride = ("kv_w_a", "kv_w_b", "q_w_a", "q_w_b", "attn_w_o")
    second_ride = (("w_ff1", 1), ("w_ff2", 1))

    inv_freq = ROPE_THETA ** (-jnp.arange(HALF_ROPE, dtype=F32) / HALF_ROPE)
    ang = positions.astype(F32)[:, None] * inv_freq
    cos, sin = jnp.cos(ang), jnp.sin(ang)
    zero = jnp.zeros_like(cos)
    cos_q, sin_q = jnp.concatenate([cos] * 4, 1), jnp.concatenate([-sin, -sin, sin, sin], 1)
    cos_k, sin_k = jnp.concatenate([cos, zero, cos, zero], 1), jnp.concatenate([-sin, zero, sin, zero], 1)
    ff_tile = D_FF // N_CHIPS
    pack_shape = (N_CHIPS, EARLY_ROWS, PACK_W)

    lr = w["ssm_lam_re"].reshape(N_STATES, 1)
    li = w["ssm_lam_im"].reshape(N_STATES, 1)
    ldt = jnp.repeat(w["ssm_log_dt"].reshape(N_GROUPS), SSM_STATE).reshape(N_STATES, 1)
    b_re = w["ssm_b_re"].reshape(N_STATES, SSM_GROUP)
    b_im = w["ssm_b_im"].reshape(N_STATES, SSM_GROUP)
    a_re, a_im, bb_re, bb_im = s5_prep(lr, li, ldt, b_re, b_im)
    a_re, a_im = a_re.reshape(1, N_STATES), a_im.reshape(1, N_STATES)
    bbd_re = _blockdiag_in(bb_re.reshape(N_GROUPS, SSM_STATE, SSM_GROUP)).astype(BF16)
    bbd_im = _blockdiag_in(bb_im.reshape(N_GROUPS, SSM_STATE, SSM_GROUP)).astype(BF16)
    cbd_re = _blockdiag_out(w["ssm_c_re"].reshape(N_GROUPS, SSM_GROUP, SSM_STATE)).astype(BF16)
    cbd_imn = _blockdiag_out(-w["ssm_c_im"].reshape(N_GROUPS, SSM_GROUP, SSM_STATE)).astype(BF16)
    dskip = w["ssm_d"].reshape(1, D_MODEL)
    (ypre, yg, h_re, h_im), landed = s5_fwd(x, bbd_re, bbd_im, cbd_re, cbd_imn, a_re, a_im, dskip, ride_for(first_ride))
    gathered(first_ride, landed)
    w_glu = w["ssm_w_glu"]
    glu_tile = w_glu.shape[2]
    vg = mm(yg, w_glu, n_dim=2 * D_MODEL, tiles=(None, glu_tile, None), b_view=_b_cols, name="glu_proj")

    def glu(v):
        return (v[:, :D_MODEL] * _sigmoid(v[:, D_MODEL:]),), ()
    (z,) = rowwise(glu, (vg,), ((D_MODEL, BF16),), name="glu")
    w_out = w["ssm_w_out"].reshape(D_MODEL, D_MODEL)
    mix0 = mm(z, w_out, name="ssm_out")

    def mlp_fwd(hb, layer, riding=None):
        pre = mm(hb, w["w_ff1"][layer], n_dim=D_FF, tiles=(None, ff_tile, None), b_view=_b_cols, name=f"ff1_{layer}",
                 out_dtypes=(BF16,), ride=ride_for(riding) if riding else None)
        if riding and comm is not None:
            pre, landed = pre
            gathered(riding, landed)
        f = mm(pre, w["w_ff2"][layer].reshape(D_FF, D_MODEL), pro_a=_relu2, name=f"ff2_{layer}")
        return pre, f

    ln = lambda name, l: w[name][l].reshape(1, D_MODEL)
    h1, h1b = ln_fwd(x, mix0, ln("ln_mix_g", 0), ln("ln_mix_b", 0), "ln_mix_0")
    f1pre, f1 = mlp_fwd(h1b, 0, mla_ride)
    h2, h2b = ln_fwd(h1, f1, ln("ln_ffn_g", 0), ln("ln_ffn_b", 0), "ln_ffn_0")

    kv_w_a = w["kv_w_a"].reshape(D_MODEL, KVA_PAD)
    kv_w_b = w["kv_w_b"]
    q_w_a = w["q_w_a"].reshape(D_MODEL, Q_LORA)
    q_w_b = w["q_w_b"]
    w_o = w["attn_w_o"].reshape(D_MODEL, D_MODEL)
    kvb_tile = kv_w_b.shape[2]
    kvn_g = w["kv_norm_g"].reshape(1, KV_LORA)
    qn_g = w["q_norm_g"].reshape(1, Q_LORA)
    kva = mm(h2b, kv_w_a, name="kv_a")

    def kv_post(kva, g, cs, sn):
        tile = _rope_tile(kva[:, KV_LORA:], cs, sn)
        return (_rms(kva[:, :KV_LORA], g), _cat(tile, pltpu.roll(tile, HALF_ROPE, 1))), ()
    ckv, krope = rowwise(kv_post, (kva, kvn_g, cos_k, sin_k), ((KV_LORA, BF16), (2 * LANES, BF16)), name="kv_post")
    kvb = mm(ckv, kv_w_b, n_dim=N_CHIPS * kvb_tile, tiles=(None, kvb_tile, KV_LORA), b_view=_b_cols, name="kv_b",
             out_dtypes=(BF16,))
    cq_raw = mm(h2b, q_w_a, name="q_a")
    (cq,) = rowwise(lambda c, g: ((_rms(c, g),), ()), (cq_raw, qn_g), ((Q_LORA, BF16),), name="q_norm")
    qlin = mm(cq, q_w_b, n_dim=N_CHIPS * Q_CHIP, tiles=(None, Q_CHIP, Q_LORA), b_view=_b_cols, name="q_b")

    def on_rope_tiles(fn, scale=None):
        def apply(q, cs, sn):
            parts = []
            for k in range(N_CHIPS):
                parts.append(q[:, Q_CHIP * k:Q_CHIP * k + Q_CHIP_NOPE])
                parts.append(fn(q[:, Q_CHIP * k + Q_CHIP_NOPE:Q_CHIP * (k + 1)], cs, sn))
            out = jnp.concatenate(parts, axis=1)
            return (out if scale is None else out * scale,), ()
        return apply
    (qro,) = rowwise(on_rope_tiles(_rope_tile, Q_PRESCALE), (qlin, cos_q, sin_q), ((N_CHIPS * Q_CHIP, BF16),),
                     name="q_rope")
    (o, lse), landed = attn_fwd(qro, kvb, krope, ride_for(second_ride))
    gathered(second_ride, landed)
    mix1 = mm(o, w_o, name="attn_out")
    h3, h3b = ln_fwd(h2, mix1, ln("ln_mix_g", 1), ln("ln_mix_b", 1), "ln_mix_1")
    f2pre, f2 = mlp_fwd(h3b, 1)
    def last_ln_and_loss(h, mix, gl, bl, t):
        e = _layer_norm(h, mix, gl, bl) - t
        return (e * (1.0 / D_MODEL),), (jnp.broadcast_to(jnp.sum(e * e), (1, LANES)),)
    dh4, loss_acc = rowwise(last_ln_and_loss, (h3, f2, ln("ln_ffn_g", 1), ln("ln_ffn_b", 1), target), ((D_MODEL, F32),),
                            accs=(LANES,), name="ln_ffn_1_loss")
    loss = loss_acc[0, 0] * (0.5 / D_MODEL)

    g = {}

    def into_rows(off, rows_per_chip, shape=pack_shape):
        def view(tm, tn):
            nb = rows_per_chip // tm
            return pl.BlockSpec((None, tm, tn), lambda i, j, k: (i // nb, off // tm + i % nb, 0))
        return shape, view

    def into_cols(off):
        return pack_shape, lambda tm, tn: pl.BlockSpec((None, tm, tn), lambda i, j, k: (j, off // tm + i, 0))

    def mlp_bwd(pack, dr, drb, hb, pre, layer, swap=False):
        dpre = mm(drb, w["w_ff2"][layer].reshape(D_FF, D_MODEL), tb=True, epi=lambda r, p: (r * 2.0 * jnp.maximum(p, 0.0),),
                  extras=(pre,), out_dtypes=(BF16,), tiles=(None, ff_tile, None), name=f"ff2_dx_{layer}")
        pack = mm(pre, drb, ta=True, pro_a=_relu2, name=f"ff2_dw_{layer}", tiles=(ff_tile, PACK_W, None), into=pack,
                  out_view=into_rows(EARLY_OFF["w_ff2"] + layer * ff_tile, ff_tile))
        pack = mm(hb, dpre, ta=True, name=f"ff1_dw_{layer}", tiles=(None, PACK_W, None), into=pack,
                  out_view=into_cols(EARLY_OFF["w_ff1"] + layer * D_MODEL))
        dh = mm(dpre, w["w_ff1"][layer], tb=True, epi=lambda r, d: (r + DN_ALPHA * d,), extras=(dr,), n_dim=D_MODEL,
                tiles=(None, D_MODEL, ff_tile), b_view=_b_cols_t, name=f"ff1_dx_{layer}",
                ride=SwapRide(pack) if swap else None)
        return (pack, *dh) if swap else (pack, dh)

    dr4, dr4b, dg_f1, db_f1 = ln_bwd(h3, f2, ln("ln_ffn_g", 1), dh4, "ln_ffn_bwd_1")
    pack, dh3 = mlp_bwd(None, dr4, dr4b, h3b, f2pre, 1)
    dr3, dr3b, dg_m1, db_m1 = ln_bwd(h2, mix1, ln("ln_mix_g", 1), dh3, "ln_mix_bwd_1")
    shard_rows = D_MODEL // N_CHIPS
    pack = mm(o, dr3b, ta=True, name="attn_out_dw", tiles=(shard_rows, PACK_W, None), into=pack,
              out_view=into_rows(EARLY_OFF["attn_w_o"], shard_rows))
    do = mm(dr3b, w_o, tb=True, name="attn_out_dx")
    def head_dots(do, o):
        return (jnp.concatenate([jnp.sum(do[:, V_HEAD * h:V_HEAD * (h + 1)] * o[:, V_HEAD * h:V_HEAD * (h + 1)], axis=1,
                                         keepdims=True) for h in range(N_HEADS)], axis=1),), ()
    (delta,) = rowwise(head_dots, (do, o), ((N_HEADS, F32),), name="attn_delta")
    tb = min(ATT_TK, seq)
    lse_row = lse[:, :, 0].reshape(N_HEADS, seq // tb, tb)
    delta_row = delta.T.reshape(N_HEADS, seq // tb, tb)
    dqn, dqr, dkvb, dkr = attn_bwd(qro, kvb, krope, do, lse_row, delta_row)

    def q_rope_bwd(dn, dr, cs, sn):
        parts = []
        for k in range(N_CHIPS):
            parts.append(dn[:, Q_CHIP_NOPE * k:Q_CHIP_NOPE * (k + 1)])
            parts.append(_rope_tile_bwd(dr[:, LANES * k:LANES * (k + 1)], cs, sn))
        return (jnp.concatenate(parts, axis=1),), ()
    (dqlin,) = rowwise(q_rope_bwd, (dqn, dqr, cos_q, sin_q), ((N_CHIPS * Q_CHIP, BF16),), name="q_rope_bwd")
    g["q_w_b"] = mm(cq, dqlin, ta=True, name="q_b_dw", tiles=(Q_LORA, Q_CHIP, None), out_view=_out_cols(q_w_b.shape))
    dcq = mm(dqlin, q_w_b, tb=True, n_dim=Q_LORA, tiles=(None, Q_LORA, Q_CHIP), b_view=_b_cols_t, name="q_b_dx")

    def q_norm_bwd(c, gq, d):
        dx, dgq = _rms_bwd(c, gq, d)
        return (dx,), (dgq,)
    dcq_raw, dqn_g = rowwise(q_norm_bwd, (cq_raw, qn_g, dcq), ((Q_LORA, BF16),), accs=(Q_LORA,), name="q_norm_bwd")
    g["q_w_a"] = mm(h2b, dcq_raw, ta=True, name="q_a_dw")
    g["kv_w_b"] = mm(ckv, dkvb, ta=True, name="kv_b_dw", tiles=(KV_LORA, kvb_tile, None), out_view=_out_cols(kv_w_b.shape))
    dckv = mm(dkvb, kv_w_b, tb=True, n_dim=KV_LORA, tiles=(None, KV_LORA, kvb_tile), b_view=_b_cols_t, name="kv_b_dx")
    dkr_sum = head_sum(dkr)

    def kv_post_bwd(kva, gk, dc, dk, cs, sn):
        dx, dgk = _rms_bwd(kva[:, :KV_LORA], gk, dc)
        dk = dk + pltpu.roll(dk, LANES - HALF_ROPE, 1)
        return (jnp.concatenate([dx, _rope_tile_bwd(dk, cs, sn)], axis=1),), (dgk,)
    dkva, dkvn_g = rowwise(kv_post_bwd, (kva, kvn_g, dckv, dkr_sum, cos_k, sin_k), ((KVA_PAD, BF16),),
                           accs=(KV_LORA,), name="kv_post_bwd")
    g["kv_w_a"] = mm(h2b, dkva, ta=True, name="kv_a_dw")
    dh2 = mm(dcq_raw, q_w_a, tb=True, epi=lambda r, d: (r + DN_ALPHA * d,), extras=(dr3,), name="q_a_dx")
    dh2 = mm(dkva, kv_w_a, tb=True, epi=lambda r, d: (r + d,), extras=(dh2,), name="kv_a_dx")

    dr2, dr2b, dg_f0, db_f0 = ln_bwd(h1, f1, ln("ln_ffn_g", 0), dh2, "ln_ffn_bwd_0")
    pack = put_rows(pack, packed_shards(g, MISC_EARLY, EARLY_ROWS - MISC_EARLY_OFF), MISC_EARLY_OFF)
    early_ride = None
    if comm is None:
        pack, dh1 = mlp_bwd(pack, dr2, dr2b, h1b, f1pre, 0)
    else:
        pack, dh1, (theirs,) = mlp_bwd(pack, dr2, dr2b, h1b, f1pre, 0, swap=True)
        early_ride = SendRide(add_halves(pack, theirs, comm[1]))
    dr1, dr1b, dg_m0, db_m0 = ln_bwd(x, mix0, ln("ln_mix_g", 0), dh1, "ln_mix_bwd_0")
    late = mm(z, dr1b, ta=True, name="ssm_out_dw", tiles=(shard_rows, PACK_W, None),
              out_view=into_rows(LATE_OFF["ssm_w_out"], shard_rows, (N_CHIPS, LATE_ROWS, PACK_W)))
    dz = mm(dr1b, w_out, tb=True, name="ssm_out_dx")

    def glu_bwd(v, dz):
        val, sg = v[:, :D_MODEL], _sigmoid(v[:, D_MODEL:])
        return (jnp.concatenate([dz * sg, dz * val * sg * (1.0 - sg)], axis=1),), ()
    (dvg,) = rowwise(glu_bwd, (vg, dz), ((2 * D_MODEL, BF16),), name="glu_bwd")
    g["ssm_w_glu"] = mm(yg, dvg, ta=True, name="glu_proj_dw", tiles=(None, glu_tile, None), out_view=_out_cols(w_glu.shape))
    dypre = mm(dvg, w_glu, tb=True, epi=lambda r, y: (r * _gelu_grad(y),), extras=(ypre,), n_dim=D_MODEL,
               tiles=(None, D_MODEL, glu_tile), b_view=_b_cols_t, name="glu_proj_dx")
    (dx, dbbd_re, dbbd_im, dcbd_re, dcbd_imn, dar, dai, dd), got_early = s5_bwd(
        dypre, x, dr1, h_re, h_im, bbd_re, bbd_im, cbd_re, cbd_imn, a_re, a_im, dskip, early_ride)
    dbb_re = _blockdiag_in_t(dbbd_re).reshape(N_STATES, SSM_GROUP)
    dbb_im = _blockdiag_in_t(dbbd_im).reshape(N_STATES, SSM_GROUP)
    dlr, dli, dldt, db_re, db_im = s5_prep_bwd(lr, li, ldt, b_re, b_im, dar.reshape(N_STATES, 1),
                                               dai.reshape(N_STATES, 1), dbb_re, dbb_im)
    g["ssm_lam_re"] = dlr.reshape(1, N_GROUPS, SSM_STATE)
    g["ssm_lam_im"] = dli.reshape(1, N_GROUPS, SSM_STATE)
    g["ssm_log_dt"] = group_sum(dldt).reshape(1, N_GROUPS)
    g["ssm_b_re"] = db_re.reshape(1, N_GROUPS, SSM_STATE, SSM_GROUP)
    g["ssm_b_im"] = db_im.reshape(1, N_GROUPS, SSM_STATE, SSM_GROUP)
    g["ssm_c_re"] = _blockdiag_out_t(dcbd_re).reshape(1, N_GROUPS, SSM_GROUP, SSM_STATE)
    g["ssm_c_im"] = -_blockdiag_out_t(dcbd_imn).reshape(1, N_GROUPS, SSM_GROUP, SSM_STATE)
    g["ssm_d"] = dd
    g["ln_mix_g"] = jnp.concatenate([dg_m0, dg_m1], 0)
    g["ln_mix_b"] = jnp.concatenate([db_m0, db_m1], 0)
    g["ln_ffn_g"] = jnp.concatenate([dg_f0, dg_f1], 0)
    g["ln_ffn_b"] = jnp.concatenate([db_f0, db_f1], 0)
    g["kv_norm_g"] = dkvn_g.reshape(KV_LORA)
    g["q_norm_g"] = dqn_g
    return loss, dx, pack, late, g, (early_ride.ins[0], got_early[0]) if comm is not None else None


def place(shard, me_idx, dtype, name):
    rows, cols = shard.shape
    tr = _tile(rows, (512, 256, 128))

    def body(m_ref, x_ref, o_ref):
        o_ref[...] = x_ref[...].astype(o_ref.dtype)

    return _pcall(
        body, name=name,
        grid_spec=pltpu.PrefetchScalarGridSpec(
            num_scalar_prefetch=1, grid=(rows // tr,),
            in_specs=[pl.BlockSpec((tr, cols), lambda i, m: (i, 0))],
            out_specs=pl.BlockSpec((None, tr, cols), lambda i, m: (m[0], i, 0))),
        out_shape=jax.ShapeDtypeStruct((N_CHIPS, rows, cols), dtype),
        compiler_params=_params(("parallel",)),
    )(me_idx, shard)


def put_rows(pack, rows, off):
    _, n, cols = rows.shape
    tr = math.gcd(math.gcd(off, n), 512)

    def body(r_ref, p_ref, o_ref):
        o_ref[...] = r_ref[...]

    return _pcall(body, name="grad_put_rows", grid=(N_CHIPS, n // tr),
                  in_specs=[pl.BlockSpec((None, tr, cols), lambda k, i: (k, i, 0)), _ANY],
                  out_specs=pl.BlockSpec((None, tr, cols), lambda k, i: (k, off // tr + i, 0)),
                  out_shape=jax.ShapeDtypeStruct(pack.shape, pack.dtype), input_output_aliases={1: 0},
                  compiler_params=_params(("parallel", "parallel")))(rows, pack)


def _my_cols(c, mine=True):
    start = (c if mine else 1 - c) * HALF_W
    return pl.ds(pl.multiple_of(start, HALF_W), HALF_W)


def add_halves(gpack, got, c_idx):
    n, rows, _ = gpack.shape
    blk = (None, G_BLOCK_ROWS, HALF_W)

    def body(c_ref, g_ref, r_ref, o_ref):
        o_ref[...] = (g_ref[...] + r_ref[...]).astype(o_ref.dtype)

    return _pcall(
        body, name="grad_add_halves",
        grid_spec=pltpu.PrefetchScalarGridSpec(
            num_scalar_prefetch=1, grid=(n, rows // G_BLOCK_ROWS),
            in_specs=[pl.BlockSpec(blk, lambda k, i, c: (k, i, c[0])), pl.BlockSpec(blk, lambda k, i, c: (k, i, 0))],
            out_specs=pl.BlockSpec(blk, lambda k, i, c: (k, i, 0))),
        out_shape=jax.ShapeDtypeStruct((n, rows, HALF_W), BF16),
        compiler_params=_params(("parallel", "parallel")),
    )(c_idx, gpack, got)


def sum_owner(part, got, idx, total_rows, row_off=0, into=None):
    _, rows, _ = part.shape
    tr = G_BLOCK_ROWS
    n_into = 0 if into is None else 1

    def body(m_ref, p_ref, g_ref, *rest):
        up = lambda v: v.astype(F32)
        rest[-1][...] = ((up(p_ref[...]) + up(g_ref[0])) + up(g_ref[1])) + up(g_ref[2])

    return _pcall(
        body, name="grad_sum_owner",
        grid_spec=pltpu.PrefetchScalarGridSpec(
            num_scalar_prefetch=1, grid=(rows // tr,),
            in_specs=[pl.BlockSpec((None, tr, HALF_W), lambda i, m: (m[0], i, 0)),
                      pl.BlockSpec((3, tr, HALF_W), lambda i, m: (0, i, 0))] + [_ANY] * n_into,
            out_specs=pl.BlockSpec((tr, HALF_W), lambda i, m: (row_off // tr + i, m[1]))),
        out_shape=jax.ShapeDtypeStruct((total_rows, PACK_W), F32),
        input_output_aliases={3: 0} if n_into else {},
        compiler_params=_params(("parallel",)),
    )(idx, part, got, *([into] if n_into else []))


def join_halves(red):
    def body(in_ref, out_ref, send_sem, recv_sem):
        x, y, c, _ = _place()
        sibling = (x, y, 1 - c)
        mine = out_ref.at[:, _my_cols(c)]
        cp = pltpu.make_async_remote_copy(src_ref=mine, dst_ref=mine, send_sem=send_sem, recv_sem=recv_sem,
                                          device_id=sibling, device_id_type=MESH)
        cp.start()
        cp.wait_send()
        other = out_ref.at[:, _my_cols(c, mine=False)]
        pltpu.make_async_remote_copy(src_ref=other, dst_ref=other, send_sem=send_sem, recv_sem=recv_sem,
                                     device_id=sibling, device_id_type=MESH).wait_recv()

    return _pcall(body, name="grad_join_halves", in_specs=[_ANY], out_specs=_ANY,
                  out_shape=jax.ShapeDtypeStruct(red.shape, red.dtype), input_output_aliases={0: 0},
                  scratch_shapes=[pltpu.SemaphoreType.DMA, pltpu.SemaphoreType.DMA])(red)


def adamw(gsrc, g_off, wt, m, v, name):
    n, cols = wt.shape
    tr = math.gcd(math.gcd(g_off, n), 256) if g_off else math.gcd(n, 256)
    off_blk = g_off // tr
    c1 = 1.0 / (1.0 - ADAM_B1 ** ADAM_STEP)
    c2 = 1.0 / (1.0 - ADAM_B2 ** ADAM_STEP)

    def body(g_ref, w_ref, m_ref, v_ref, go_ref, d_ref, mo_ref, vo_ref):
        gv = g_ref[...]
        mn = ADAM_B1 * m_ref[...] + (1.0 - ADAM_B1) * gv
        vn = ADAM_B2 * v_ref[...] + (1.0 - ADAM_B2) * gv * gv
        go_ref[...] = gv
        mo_ref[...] = mn
        vo_ref[...] = vn
        d_ref[...] = -ADAM_LR * ((mn * c1) / (jnp.sqrt(vn * c2) + ADAM_EPS) + ADAM_WD * w_ref[...])

    blk = pl.BlockSpec((tr, cols), lambda i: (i, 0))
    return _pcall(body, name=name, grid=(n // tr,),
                  in_specs=[pl.BlockSpec((tr, cols), lambda i: (off_blk + i, 0)), blk, blk, blk],
                  out_specs=[blk] * 4, out_shape=[jax.ShapeDtypeStruct((n, cols), F32)] * 4,
                  compiler_params=_params(("parallel",)))(gsrc, wt, m, v)


def _rows8(a):
    return -(-a.size // (8 * PACK_W)) * 8


def _as_rows(a, rows=None):
    flat = a.reshape(-1)
    n = _rows8(a) if rows is None else rows
    return jnp.pad(flat, (0, n * PACK_W - flat.shape[0])).reshape(n, PACK_W)


def local_shards_2d(wl):
    return {"w_ff1": [wl["w_ff1"][0], wl["w_ff1"][1]], "w_ff2": [wl["w_ff2"][0], wl["w_ff2"][1]],
            "ssm_w_glu": wl["ssm_w_glu"], "ssm_w_out": wl["ssm_w_out"], "kv_w_a": _pad_kva_cols(wl["kv_w_a"]),
            "kv_w_b": wl["kv_w_b"], "q_w_a": wl["q_w_a"], "q_w_b": _perm_q_cols(wl["q_w_b"]),
            "attn_w_o": wl["attn_w_o"], "ssm_d": wl["ssm_d"].reshape(2, -1)}


def misc_grad_shard(name, g, k):
    if name == "ssm_d":
        w = D_MODEL // N_CHIPS
        return g[:, w * k:w * (k + 1)]
    if name in ("ssm_w_glu", "kv_w_b"):
        return g[k]
    if name == "q_w_b":
        return _unperm_q_cols(g[k])
    rows = D_MODEL // N_CHIPS
    shard = g[rows * k:rows * (k + 1)]
    return _unpad_kva_cols(shard) if name == "kv_w_a" else shard


def packed_shards(g, names, rows, tail=None):
    blocks = []
    for k in range(N_CHIPS):
        parts = [_as_rows(misc_grad_shard(n, g[n], k), MISC_SHARD_ROWS[n]) for n in names]
        if tail is not None:
            parts.append(tail[k * (tail.shape[0] // N_CHIPS):(k + 1) * (tail.shape[0] // N_CHIPS)])
        blk = jnp.concatenate(parts, axis=0)
        blocks.append(jnp.pad(blk, ((0, rows - blk.shape[0]), (0, 0))))
    return jnp.stack(blocks)


def kernel(x, positions, ln_mix_g, ln_mix_b, ln_ffn_g, ln_ffn_b, w_ff1, w_ff2, ssm_lam_re, ssm_lam_im, ssm_log_dt, ssm_b_re, ssm_b_im, ssm_c_re, ssm_c_im, ssm_d, ssm_w_glu, ssm_w_out, kv_w_a, kv_norm_g, kv_w_b, q_w_a, q_norm_g, q_w_b, attn_w_o, loss_target, m_ln_mix_g, m_ln_mix_b, m_ln_ffn_g, m_ln_ffn_b, m_w_ff1, m_w_ff2, m_ssm_lam_re, m_ssm_lam_im, m_ssm_log_dt, m_ssm_b_re, m_ssm_b_im, m_ssm_c_re, m_ssm_c_im, m_ssm_d, m_ssm_w_glu, m_ssm_w_out, m_kv_w_a, m_kv_norm_g, m_kv_w_b, m_q_w_a, m_q_norm_g, m_q_w_b, m_attn_w_o, v_ln_mix_g, v_ln_mix_b, v_ln_ffn_g, v_ln_ffn_b, v_w_ff1, v_w_ff2, v_ssm_lam_re, v_ssm_lam_im, v_ssm_log_dt, v_ssm_b_re, v_ssm_b_im, v_ssm_c_re, v_ssm_c_im, v_ssm_d, v_ssm_w_glu, v_ssm_w_out, v_kv_w_a, v_kv_norm_g, v_kv_w_b, v_q_w_a, v_q_norm_g, v_q_w_b, v_attn_w_o):
    env = dict(locals())
    wl = {n: env[n] for n in WEIGHTS}
    ml = {n: env["m_" + n] for n in WEIGHTS}
    vl = {n: env["v_" + n] for n in WEIGHTS}
    for n in ("ssm_w_glu", "ssm_w_out", "q_w_a", "q_w_b", "attn_w_o"):
        wl[n], ml[n], vl[n] = wl[n][0], ml[n][0], vl[n][0]

    c_idx = lax.axis_index("c").astype(jnp.int32).reshape(1)
    me_idx = (2 * lax.axis_index("x") + lax.axis_index("y")).astype(jnp.int32).reshape(1)

    local = local_shards_2d(wl)
    put = lambda a, n: place(a, me_idx, F32 if n == "ssm_d" else BF16, "place_" + n)
    stacked = {n: [put(a, f"{n}_{l}") for l, a in enumerate(local[n])] if isinstance(local[n], list) else put(local[n], n)
               for n in SHARDED}
    stacked["ssm_d"] = ride_alone(GatherRide([_halves(stacked["ssm_d"])]), "ssm_d_all_gather")[0].reshape(1, D_MODEL)
    for n in REPLICATED:
        stacked[n] = wl[n]

    loss_part, dx, early, late, g, (early_sums, early_got) = device_step(
        x[0], positions[0], loss_target[0], stacked, comm=(me_idx, c_idx))
    loss = lax.psum(loss_part, ("x", "y", "c"))

    small = jnp.concatenate([_as_rows(g[n]) for n in REPLICATED], axis=0)
    small = jnp.pad(small, ((0, SMALL_ROWS - small.shape[0]), (0, 0)))
    late = put_rows(late, packed_shards(g, MISC_LATE, LATE_ROWS - MISC_LATE_OFF, tail=small), MISC_LATE_OFF)
    late_sums = add_halves(late, ride_alone(SwapRide(late), "grad_swap_halves")[0], c_idx)
    late_got = ride_alone(SendRide(late_sums), "grad_send_to_owners")[0]
    where = jnp.concatenate([me_idx, c_idx])
    total_rows = EARLY_ROWS + LATE_ROWS
    reduced = sum_owner(early_sums, early_got, where, total_rows)
    reduced = join_halves(sum_owner(late_sums, late_got, where, total_rows, row_off=EARLY_ROWS, into=reduced))
    quarter = reduced[EARLY_ROWS + SMALL_OFF:EARLY_ROWS + SMALL_OFF + SMALL_Q_ROWS]
    small_tot = ride_alone(GatherRide([_halves(place(quarter, me_idx, F32, "place_small_grads"))]),
                           "small_grad_all_gather")[0].reshape(SMALL_ROWS, PACK_W)

    out_g, out_d, out_m, out_v = {}, {}, {}, {}
    direct = {**EARLY_OFF, **{n: EARLY_ROWS + o for n, o in LATE_OFF.items()}}
    for n, off in direct.items():
        res = adamw(reduced, off, wl[n].reshape(-1, PACK_W), ml[n].reshape(-1, PACK_W), vl[n].reshape(-1, PACK_W),
                    "adamw_" + n)
        out_g[n], out_d[n], out_m[n], out_v[n] = [a.reshape(env[n].shape) for a in res]
    late_offs = [EARLY_ROWS + MISC_LATE_OFF + sum(MISC_SHARD_ROWS[m] for m in MISC_LATE[:i]) for i in range(len(MISC_LATE))]
    for names, off in [(MISC_EARLY, MISC_EARLY_OFF)] + [((n,), o) for n, o in zip(MISC_LATE, late_offs)]:
        pack3 = lambda d: jnp.concatenate([_as_rows(d[n], MISC_SHARD_ROWS[n]) for n in names], axis=0)
        res = adamw(reduced, off, pack3(wl), pack3(ml), pack3(vl), "adamw_packed_" + names[0])
        r0 = 0
        for n in names:
            cnt = math.prod(env[n].shape)
            out_g[n], out_d[n], out_m[n], out_v[n] = [
                a[r0:r0 + MISC_SHARD_ROWS[n]].reshape(-1)[:cnt].reshape(env[n].shape) for a in res]
            r0 += MISC_SHARD_ROWS[n]
    ws = jnp.concatenate([_as_rows(wl[n]) for n in REPLICATED], axis=0)
    ms = jnp.concatenate([_as_rows(ml[n]) for n in REPLICATED], axis=0)
    vs = jnp.concatenate([_as_rows(vl[n]) for n in REPLICATED], axis=0)
    pad = ((0, SMALL_ROWS - ws.shape[0]), (0, 0))
    res = adamw(small_tot, 0, jnp.pad(ws, pad), jnp.pad(ms, pad), jnp.pad(vs, pad), "adamw_replicated")
    row = 0
    for n in REPLICATED:
        cnt = math.prod(env[n].shape)
        nrows = _rows8(env[n])
        out_g[n], out_d[n], out_m[n], out_v[n] = [a[row:row + nrows].reshape(-1)[:cnt].reshape(env[n].shape) for a in res]
        row += nrows

    return (loss, dx[None], *[out_g[n] for n in WEIGHTS], *[out_d[n] for n in WEIGHTS],
            *[out_m[n] for n in WEIGHTS], *[out_v[n] for n in WEIGHTS])
```

```python
import functools
import math

import jax
import jax.numpy as jnp
from jax import lax
from jax.experimental import pallas as pl
from jax.experimental.pallas import tpu as pltpu

F32 = jnp.float32
BF16 = jnp.bfloat16
MESH = pl.DeviceIdType.MESH

D_MODEL = 1024
DEPTH = 2
SSM_GROUP = 16
N_GROUPS = D_MODEL // SSM_GROUP
SSM_STATE = 64
N_STATES = N_GROUPS * SSM_STATE
N_HEADS = 8
QK_NOPE = 128
QK_ROPE = 64
HALF_ROPE = QK_ROPE // 2
V_HEAD = 128
QK_DIM = QK_NOPE + QK_ROPE
Q_LORA = 384
KV_LORA = 256
ROPE_THETA = 10000.0
SM_SCALE = QK_DIM ** -0.5
NEG_INF = -1e30
D_FF = 4 * D_MODEL
DN_ALPHA = (2 * DEPTH) ** 0.25
LN_EPS = 1e-5
RMS_EPS = 1e-6
ADAM_LR = 0.001
ADAM_B1 = 0.9
ADAM_B2 = 0.999
ADAM_EPS = 1e-08
ADAM_WD = 0.01
ADAM_STEP = 10

N_CHIPS = 4
LANES = 128
VMEM_LIMIT = 56 * 1024 * 1024
MM_VMEM_BUDGET = 40 * 1024 * 1024
PACK_W = 1024
KVA_PAD = 384
HALF_W = PACK_W // 2

SHARDED = ("w_ff1", "w_ff2", "ssm_w_glu", "ssm_w_out", "kv_w_a", "kv_w_b", "q_w_a", "q_w_b", "attn_w_o", "ssm_d")
G_BLOCK_ROWS = 960
EARLY_OFF = {"w_ff1": 0, "w_ff2": 2048, "attn_w_o": 4096}
MISC_EARLY = ("kv_w_b", "kv_w_a", "q_w_a", "q_w_b")
MISC_EARLY_OFF = 4352
EARLY_ROWS = 5 * G_BLOCK_ROWS
MID_OFF = {"ssm_w_out": 0}
MISC_MID = ("ssm_w_glu",)
MISC_MID_OFF = 256
MID_ROWS = G_BLOCK_ROWS
MISC_LATE = ("ssm_d",)
SMALL_Q_ROWS = 96
SMALL_ROWS = N_CHIPS * SMALL_Q_ROWS
SMALL_OFF = 16
LATE_ROWS = 128
MISC_SHARD_ROWS = {"ssm_d": 16, "ssm_w_glu": 512, "kv_w_b": 128, "kv_w_a": 80, "q_w_a": 96, "q_w_b": 144}
REPLICATED = ("ln_mix_g", "ln_mix_b", "ln_ffn_g", "ln_ffn_b", "ssm_lam_re", "ssm_lam_im", "ssm_log_dt",
              "ssm_b_re", "ssm_b_im", "ssm_c_re", "ssm_c_im", "kv_norm_g", "q_norm_g")
WEIGHTS = ("ln_mix_g", "ln_mix_b", "ln_ffn_g", "ln_ffn_b", "w_ff1", "w_ff2", "ssm_lam_re", "ssm_lam_im",
           "ssm_log_dt", "ssm_b_re", "ssm_b_im", "ssm_c_re", "ssm_c_im", "ssm_d", "ssm_w_glu", "ssm_w_out",
           "kv_w_a", "kv_norm_g", "kv_w_b", "q_w_a", "q_norm_g", "q_w_b", "attn_w_o")


def _pcall(body, **kw):
    return pl.pallas_call(body, **kw)


def _params(sem=None):
    return pltpu.CompilerParams(dimension_semantics=sem, vmem_limit_bytes=VMEM_LIMIT)


_ANY = pl.BlockSpec(memory_space=pl.ANY)


def _tile(dim, prefs):
    for p in prefs:
        if dim % p == 0:
            return p
    return dim


def _place():
    x, y, c = lax.axis_index("x"), lax.axis_index("y"), lax.axis_index("c")
    return x, y, c, [(1 - x, y), (x, 1 - y), (1 - x, 1 - y)]


def _remote(k, src, dst, to, send_sems, recv_sems):
    return pltpu.make_async_remote_copy(src_ref=src, dst_ref=dst, send_sem=send_sems.at[k], recv_sem=recv_sems.at[k],
                                        device_id=to, device_id_type=MESH)


class GatherRide:
    def __init__(self, arrs):
        self.ins = list(arrs)
        self.out_shapes = [jax.ShapeDtypeStruct(a.shape, a.dtype) for a in arrs]
        self.aliases = {i: i for i in range(len(arrs))}
        self.n_sems = 6 * len(arrs)

    def start(self, ins, outs, send_sems, recv_sems):
        x, y, c, chips = _place()
        me = 2 * x + y
        for a, o in enumerate(outs):
            for j, (px, py) in enumerate(chips):
                _remote(6 * a + j, o.at[me, c], o.at[me, c], (px, py, c), send_sems, recv_sems).start()

    def pass_on(self, ins, outs, send_sems, recv_sems):
        x, y, c, chips = _place()
        for a, o in enumerate(outs):
            for j, (px, py) in enumerate(chips):
                blk = o.at[2 * px + py, c]
                _remote(6 * a + j, blk, blk, (px, py, c), send_sems, recv_sems).wait_recv()
                _remote(6 * a + 3 + j, blk, blk, (x, y, 1 - c), send_sems, recv_sems).start()

    def finish(self, ins, outs, send_sems, recv_sems, passed_on=False):
        if not passed_on:
            self.pass_on(ins, outs, send_sems, recv_sems)
        x, y, c, chips = _place()
        me = 2 * x + y
        sibling = (x, y, 1 - c)
        for a, o in enumerate(outs):
            for j, (px, py) in enumerate(chips):
                blk = o.at[2 * px + py, 1 - c]
                _remote(6 * a + 3 + j, blk, blk, sibling, send_sems, recv_sems).wait_recv()
                _remote(6 * a + j, o.at[me, c], o.at[me, c], (px, py, c), send_sems, recv_sems).wait_send()
                mine = o.at[2 * px + py, c]
                _remote(6 * a + 3 + j, mine, mine, sibling, send_sems, recv_sems).wait_send()


class SendRide:
    def __init__(self, parts):
        self.ins = list(parts)
        self.out_shapes = [jax.ShapeDtypeStruct((3,) + p.shape[1:], p.dtype) for p in parts]
        self.aliases = {}
        self.n_sems = 3 * len(parts)

    def _copies(self, ins, outs, send_sems, recv_sems):
        x, y, c, chips = _place()
        return [_remote(3 * a + j, ins[a].at[2 * px + py], outs[a].at[j], (px, py, c), send_sems, recv_sems)
                for a in range(len(ins)) for j, (px, py) in enumerate(chips)]

    def start(self, ins, outs, send_sems, recv_sems):
        for cp in self._copies(ins, outs, send_sems, recv_sems):
            cp.start()

    def finish(self, ins, outs, send_sems, recv_sems):
        for cp in self._copies(ins, outs, send_sems, recv_sems):
            cp.wait()


class SwapRide:
    def __init__(self, pack):
        self.ins = [pack]
        self.out_shapes = [jax.ShapeDtypeStruct(pack.shape[:2] + (HALF_W,), pack.dtype)]
        self.aliases = {}
        self.n_sems = 1

    def _copy(self, ins, outs, send_sems, recv_sems):
        x, y, c, _ = _place()
        return _remote(0, ins[0].at[:, :, _my_cols(c, mine=False)], outs[0], (x, y, 1 - c), send_sems, recv_sems)

    def start(self, ins, outs, send_sems, recv_sems):
        self._copy(ins, outs, send_sems, recv_sems).start()

    def finish(self, ins, outs, send_sems, recv_sems):
        self._copy(ins, outs, send_sems, recv_sems).wait()


def _pcall_riding(body, args, ride, first, last, *, in_specs, out_specs, out_shape, scratch_shapes=(), middle=None,
                  **kw):
    n_in, n_out = len(args), len(out_shape)
    if ride is None:
        return _pcall(body, in_specs=in_specs, out_specs=out_specs, out_shape=out_shape,
                      scratch_shapes=list(scratch_shapes), **kw)(*args), []
    k_in, k_out = len(ride.ins), len(ride.out_shapes)

    def riding(*refs):
        ins, r_in = refs[:n_in], refs[n_in:n_in + k_in]
        outs = refs[n_in + k_in:n_in + k_in + n_out]
        r_out = refs[n_in + k_in + n_out:n_in + k_in + n_out + k_out]
        scratch, (send_sems, recv_sems) = refs[n_in + k_in + n_out + k_out:-2], refs[-2:]

        @pl.when(first())
        def _():
            ride.start(r_in, r_out, send_sems, recv_sems)

        if middle is not None:
            @pl.when(middle())
            def _():
                ride.pass_on(r_in, r_out, send_sems, recv_sems)

        body(*ins, *outs, *scratch)

        @pl.when(last())
        def _():
            if middle is not None:
                ride.finish(r_in, r_out, send_sems, recv_sems, passed_on=True)
            else:
                ride.finish(r_in, r_out, send_sems, recv_sems)

    res = _pcall(riding, in_specs=list(in_specs) + [_ANY] * k_in, out_specs=list(out_specs) + [_ANY] * k_out,
                 out_shape=list(out_shape) + ride.out_shapes,
                 input_output_aliases={n_in + i: n_out + o for i, o in ride.aliases.items()},
                 scratch_shapes=list(scratch_shapes) + [pltpu.SemaphoreType.DMA((ride.n_sems,))] * 2,
                 **kw)(*args, *ride.ins)
    return res[:n_out], res[n_out:]


def ride_alone(ride, name):
    def body(*refs):
        n = len(ride.ins)
        ins, outs, (send_sems, recv_sems) = refs[:n], refs[n:-2], refs[-2:]
        ride.start(ins, outs, send_sems, recv_sems)
        ride.finish(ins, outs, send_sems, recv_sems)

    return _pcall(body, name=name, in_specs=[_ANY] * len(ride.ins), out_specs=[_ANY] * len(ride.out_shapes),
                  out_shape=ride.out_shapes, input_output_aliases=dict(ride.aliases),
                  scratch_shapes=[pltpu.SemaphoreType.DMA((ride.n_sems,))] * 2)(*ride.ins)


def mm(a, b, *, name, ta=False, tb=False, pro_a=None, epi=None, extras=(), out_dtypes=(F32,), n_dim=None,
       tiles=(None, None, None), b_view=None, out_view=None, into=None, ride=None):
    if ta:
        k_dim, m_dim = a.shape
    else:
        m_dim, k_dim = a.shape
    if n_dim is None:
        n_dim = b.shape[0] if tb else b.shape[1]
    tn = tiles[1] or (n_dim if n_dim <= 1024 else _tile(n_dim, (1024, 512, 256, 128)))
    tk = tiles[2] or (k_dim if k_dim <= 1024 else _tile(k_dim, (1024, 512, 256, 128)))
    nk = k_dim // tk

    def vmem_bytes(tm):
        blocks = tm * tk * a.dtype.itemsize + tk * tn * b.dtype.itemsize
        blocks += tm * tn * (sum(e.dtype.itemsize for e in extras) + sum(jnp.dtype(d).itemsize for d in out_dtypes))
        return 2 * blocks + tm * tn * 4

    tm = tiles[0] or next((t for t in (4096, 2048, 1024, 512, 256) if m_dim % t == 0 and vmem_bytes(t) <= MM_VMEM_BUDGET),
                          _tile(m_dim, (128,)))
    assert m_dim % tm == 0 and n_dim % tn == 0 and k_dim % tk == 0, (name, m_dim, n_dim, k_dim, tm, tn, tk)
    n_ex, n_out = len(extras), len(out_dtypes)
    n_into = 0 if into is None else 1
    dims = (((0 if ta else 1,), (1 if tb else 0,)), ((), ()))

    def body(a_ref, b_ref, *rest):
        ex_refs, out_refs = rest[:n_ex], rest[n_ex + n_into:n_ex + n_into + n_out]

        def partial():
            av = a_ref[...]
            if pro_a is not None:
                av = pro_a(av)
            return lax.dot_general(av.astype(BF16), b_ref[...].astype(BF16), dims, preferred_element_type=F32)

        def finish(r):
            res = epi(r, *[e[...] for e in ex_refs]) if epi is not None else (r,)
            for o_ref, v in zip(out_refs, res):
                o_ref[...] = v.astype(o_ref.dtype)

        if nk == 1:
            finish(partial())
            return
        acc = rest[-1]
        k = pl.program_id(2)

        @pl.when(k == 0)
        def _():
            acc[...] = partial()

        @pl.when(k > 0)
        def _():
            acc[...] += partial()

        @pl.when(k == nk - 1)
        def _():
            finish(acc[...])

    a_spec = pl.BlockSpec((tk, tm), lambda i, j, k: (k, i)) if ta else pl.BlockSpec((tm, tk), lambda i, j, k: (i, k))
    if b_view is not None:
        b_spec = b_view(tk, tn)
    else:
        b_spec = pl.BlockSpec((tn, tk), lambda i, j, k: (j, k)) if tb else pl.BlockSpec((tk, tn), lambda i, j, k: (k, j))
    o_spec = pl.BlockSpec((tm, tn), lambda i, j, k: (i, j))
    if out_view is None:
        out_specs = [o_spec] * n_out
        out_shape = [jax.ShapeDtypeStruct((m_dim, n_dim), dt) for dt in out_dtypes]
    else:
        assert n_out == 1
        out_specs = [out_view[1](tm, tn)]
        out_shape = [jax.ShapeDtypeStruct(out_view[0], out_dtypes[0])]
    grid = (m_dim // tm, n_dim // tn, nk)
    scratch = [pltpu.VMEM((tm, tn), F32)] if nk > 1 else []
    if ride is not None:
        assert into is None
        at = lambda ids: functools.reduce(jnp.logical_and, [pl.program_id(d) == i for d, i in enumerate(ids)])
        outs, landed = _pcall_riding(
            body, (a, b, *extras), ride, lambda: at((0, 0, 0)), lambda: at([g - 1 for g in grid]),
            name=name, grid=grid, in_specs=[a_spec, b_spec] + [o_spec] * n_ex, out_specs=out_specs,
            out_shape=out_shape, scratch_shapes=scratch, compiler_params=_params(("arbitrary",) * 3))
        return (outs[0] if n_out == 1 else outs), landed
    outs = _pcall(
        body, name=name, grid=grid,
        in_specs=[a_spec, b_spec] + [o_spec] * n_ex + [_ANY] * n_into,
        out_specs=out_specs, out_shape=out_shape,
        input_output_aliases={2 + n_ex: 0} if n_into else {},
        scratch_shapes=scratch,
        compiler_params=_params(("parallel", "parallel", "arbitrary")),
    )(a, b, *extras, *([into] if n_into else []))
    return outs[0] if n_out == 1 else outs


def rowwise(fn, ins, outs, *, name, accs=(), tm=256):
    rows = ins[0].shape[0]
    tm = min(tm, rows)
    n_in, n_out, n_acc = len(ins), len(outs), len(accs)

    def body(*refs):
        in_refs, out_refs, acc_refs = refs[:n_in], refs[n_in:n_in + n_out], refs[n_in + n_out:]
        res, sums = fn(*[r[...] for r in in_refs])
        for o_ref, v in zip(out_refs, res):
            o_ref[...] = v.astype(o_ref.dtype)
        if n_acc:
            @pl.when(pl.program_id(0) == 0)
            def _():
                for a_ref in acc_refs:
                    a_ref[...] = jnp.zeros_like(a_ref)

            for a_ref, s in zip(acc_refs, sums):
                a_ref[...] += s

    def spec(arr):
        if arr.shape[0] == rows:
            return pl.BlockSpec((tm, arr.shape[1]), lambda i: (i, 0))
        return pl.BlockSpec(arr.shape, lambda i: (0, 0))

    res = _pcall(
        body, name=name, grid=(rows // tm,),
        in_specs=[spec(a) for a in ins],
        out_specs=[pl.BlockSpec((tm, w), lambda i: (i, 0)) for w, _ in outs]
        + [pl.BlockSpec((1, w), lambda i: (0, 0)) for w in accs],
        out_shape=[jax.ShapeDtypeStruct((rows, w), dt) for w, dt in outs]
        + [jax.ShapeDtypeStruct((1, w), F32) for w in accs],
        compiler_params=_params(("arbitrary",) if n_acc else ("parallel",)),
    )(*ins)
    return res


def _relu2(v):
    r = jnp.maximum(v, 0.0)
    return r * r


def _gelu(x):
    c = math.sqrt(2.0 / math.pi)
    return 0.5 * x * (1.0 + jnp.tanh(c * (x + 0.044715 * x * x * x)))


def _gelu_grad(x):
    c = math.sqrt(2.0 / math.pi)
    t = jnp.tanh(c * (x + 0.044715 * x * x * x))
    return 0.5 * (1.0 + t) + 0.5 * x * (1.0 - t * t) * c * (1.0 + 3 * 0.044715 * x * x)


def _sigmoid(x):
    return 1.0 / (1.0 + jnp.exp(-x))


def _layer_norm(h, mix, g, b):
    r = DN_ALPHA * h + mix
    mu = jnp.mean(r, axis=-1, keepdims=True)
    xc = r - mu
    var = jnp.mean(xc * xc, axis=-1, keepdims=True)
    return xc * lax.rsqrt(var + LN_EPS) * g + b


def ln_fwd(h, mix, g, b, name):
    def fn(h, mix, g, b):
        y = _layer_norm(h, mix, g, b)
        return (y, y), ()
    return rowwise(fn, (h, mix, g, b), ((D_MODEL, F32), (D_MODEL, BF16)), name=name)


def ln_bwd(h, mix, g, dy, name):
    def fn(h, mix, g, dy):
        r = DN_ALPHA * h + mix
        mu = jnp.mean(r, axis=-1, keepdims=True)
        xc = r - mu
        var = jnp.mean(xc * xc, axis=-1, keepdims=True)
        rstd = lax.rsqrt(var + LN_EPS)
        xhat = xc * rstd
        dxh = dy * g
        m1 = jnp.mean(dxh, axis=-1, keepdims=True)
        m2 = jnp.mean(dxh * xhat, axis=-1, keepdims=True)
        dr = rstd * (dxh - m1 - xhat * m2)
        return (dr, dr), (jnp.sum(dy * xhat, axis=0, keepdims=True), jnp.sum(dy, axis=0, keepdims=True))
    return rowwise(fn, (h, mix, g, dy), ((D_MODEL, F32), (D_MODEL, BF16)), accs=(D_MODEL, D_MODEL), name=name)


def _rms(x, g):
    r = lax.rsqrt(jnp.mean(x * x, axis=-1, keepdims=True) + RMS_EPS)
    return x * r * g


def _rms_bwd(x, g, dy):
    r = lax.rsqrt(jnp.mean(x * x, axis=-1, keepdims=True) + RMS_EPS)
    xn = x * r
    dyg = dy * g
    dx = r * (dyg - xn * jnp.mean(dyg * xn, axis=-1, keepdims=True))
    return dx, jnp.sum(dy * xn, axis=0, keepdims=True)


def _s5_disc(lr, li, ldt):
    dt = jnp.exp(ldt)
    mag = jnp.exp(lr * dt)
    cs, sn = jnp.cos(li * dt), jnp.sin(li * dt)
    ar, ai = mag * cs, mag * sn
    inv = 1.0 / (lr * lr + li * li)
    n_re = (ar - 1.0) * lr + ai * li
    n_im = ai * lr - (ar - 1.0) * li
    return dt, mag, cs, sn, ar, ai, inv, n_re, n_im


def s5_prep(lr, li, ldt, b_re, b_im):
    def fn(lr, li, ldt, b_re, b_im):
        _, _, _, _, ar, ai, inv, n_re, n_im = _s5_disc(lr, li, ldt)
        cr, ci = n_re * inv, n_im * inv
        return (ar, ai, cr * b_re - ci * b_im, cr * b_im + ci * b_re), ()
    return rowwise(fn, (lr, li, ldt, b_re, b_im), ((1, F32), (1, F32), (SSM_GROUP, F32), (SSM_GROUP, F32)),
                   name="s5_prep", tm=512)


def s5_prep_bwd(lr, li, ldt, b_re, b_im, dar, dai, dbb_re, dbb_im):
    def fn(lr, li, ldt, b_re, b_im, dar, dai, dbb_re, dbb_im):
        dt, mag, cs, sn, ar, ai, inv, n_re, n_im = _s5_disc(lr, li, ldt)
        cr, ci = n_re * inv, n_im * inv
        db_re = cr * dbb_re + ci * dbb_im
        db_im = cr * dbb_im - ci * dbb_re
        dcr = jnp.sum(dbb_re * b_re + dbb_im * b_im, axis=-1, keepdims=True)
        dci = jnp.sum(dbb_im * b_re - dbb_re * b_im, axis=-1, keepdims=True)
        dar = dar + (dcr * lr - dci * li) * inv
        dai = dai + (dcr * li + dci * lr) * inv
        dinv = dcr * n_re + dci * n_im
        dlr = (dcr * (ar - 1.0) + dci * ai) * inv - 2.0 * lr * inv * inv * dinv
        dli = (dcr * ai - dci * (ar - 1.0)) * inv - 2.0 * li * inv * inv * dinv
        dmag = dar * cs + dai * sn
        dth = dai * ar - dar * ai
        dlr = dlr + dmag * mag * dt
        dli = dli + dth * dt
        ddt = dmag * mag * lr + dth * li
        return (dlr, dli, ddt * dt, db_re, db_im), ()
    return rowwise(fn, (lr, li, ldt, b_re, b_im, dar, dai, dbb_re, dbb_im),
                   ((1, F32), (1, F32), (1, F32), (SSM_GROUP, F32), (SSM_GROUP, F32)), name="s5_prep_bwd", tm=512)


def group_sum(x):
    def body(x_ref, o_ref):
        o_ref[...] = jnp.sum(x_ref[...], axis=1)
    return _pcall(body, name="s5_group_sum", out_shape=jax.ShapeDtypeStruct((N_GROUPS, 1), F32))(
        x.reshape(N_GROUPS, SSM_STATE, 1))


GROUPS_PER_TILE = LANES // SSM_GROUP
TILE_STATES = GROUPS_PER_TILE * SSM_STATE
N_UTILES = D_MODEL // LANES
TILES_PER_UTILE = TILE_STATES // LANES


SUBLANES = 8
SCAN_STRIP = 1024
N_STRIPS = N_STATES // SCAN_STRIP
_NT = (((1,), (1,)), ((), ()))
_TN = (((0,), (0,)), ((), ()))


def _scan_coefs(are, aim, shifted, reverse):
    ar = are[...]
    ai = -aim[...] if reverse else aim[...]
    powers = {1: (ar, ai)}
    for d in (2, 4):
        r, i = powers[d // 2]
        powers[d] = (r * r - i * i, 2.0 * r * i)
    rid = lax.broadcasted_iota(jnp.int32, (SUBLANES, N_STATES), 0)
    first = (rid == SUBLANES - 1) if reverse else (rid == 0)
    masks = [(1, first)] + [(d, (rid <= SUBLANES - 1 - d) if reverse else (rid >= d)) for d in (1, 2, 4)]
    for n, (d, keep) in enumerate(masks):
        for part in (0, 1):
            shifted[2 * n + part][...] = jnp.where(keep, jnp.broadcast_to(powers[d][part], (SUBLANES, N_STATES)), 0.0)


def _tile_scan(xr, xi, shifted, nbr_re, nbr_im, reverse):
    for n, d in enumerate((1, 1, 2, 4)):
        by = SUBLANES - d if reverse else d
        fr, fi = (nbr_re, nbr_im) if n == 0 else (xr, xi)
        sr, si = pltpu.roll(fr, by, 0), pltpu.roll(fi, by, 0)
        kr, ki = shifted[2 * n], shifted[2 * n + 1]
        xr, xi = xr + kr * sr - ki * si, xi + kr * si + ki * sr
    return xr, xi


def _tile_rows(t):
    return pl.ds(pl.multiple_of(t * SUBLANES, SUBLANES), SUBLANES)


def s5_fwd(u, bbd_re, bbd_im, cbd_re, cbd_imn, a_re, a_im, dskip, ride=None, t_rows=256):
    seq = u.shape[0]
    t_rows = min(t_rows, seq)
    n_tiles = t_rows // SUBLANES

    def body(u_ref, bre, bim, cre, cimn, are, aim, d_ref, y_ref, gelu_ref, hre_ref, him_ref, car_re, car_im, *shifted):
        @pl.when(pl.program_id(0) == 0)
        def _():
            car_re[...] = jnp.zeros_like(car_re)
            car_im[...] = jnp.zeros_like(car_im)
            _scan_coefs(are, aim, shifted, reverse=False)

        uf = u_ref[...]
        ub = uf.astype(BF16)
        for j in range(N_UTILES):
            uj = ub[:, LANES * j:LANES * (j + 1)]
            sl = slice(TILE_STATES * j, TILE_STATES * (j + 1))
            hre_ref[:, sl] = jnp.dot(uj, bre[j], preferred_element_type=F32)
            him_ref[:, sl] = jnp.dot(uj, bim[j], preferred_element_type=F32)
        for s in range(N_STRIPS):
            cols = pl.ds(s * SCAN_STRIP, SCAN_STRIP)
            coefs = [c[:, cols] for c in shifted]

            def step(t, before):
                rows = _tile_rows(t)
                hr, hi = _tile_scan(hre_ref[rows, cols], him_ref[rows, cols], coefs, before[0], before[1], False)
                hre_ref[rows, cols] = hr
                him_ref[rows, cols] = hi
                return hr, hi

            cr, ci = lax.fori_loop(0, n_tiles, step, (car_re[:, cols], car_im[:, cols]))
            car_re[:, cols] = cr
            car_im[:, cols] = ci
        dv = d_ref[...]
        for j in range(N_UTILES):
            st = slice(TILE_STATES * j, TILE_STATES * (j + 1))
            yj = (jnp.dot(hre_ref[:, st].astype(BF16), cre[j], preferred_element_type=F32)
                  + jnp.dot(him_ref[:, st].astype(BF16), cimn[j], preferred_element_type=F32))
            sl = slice(LANES * j, LANES * (j + 1))
            yj = yj + dv[:, sl] * uf[:, sl]
            y_ref[:, sl] = yj
            gelu_ref[:, sl] = _gelu(yj).astype(gelu_ref.dtype)

    full3 = lambda a: pl.BlockSpec(a.shape, lambda i: (0, 0, 0))
    full2 = lambda a: pl.BlockSpec(a.shape, lambda i: (0, 0))
    tile = pltpu.VMEM((SUBLANES, N_STATES), F32)
    n_chunks = seq // t_rows
    return _pcall_riding(
        body, (u, bbd_re, bbd_im, cbd_re, cbd_imn, a_re, a_im, dskip), ride,
        lambda: pl.program_id(0) == 0, lambda: pl.program_id(0) == n_chunks - 1,
        middle=(lambda: pl.program_id(0) == (7 * n_chunks) // 8) if ride is not None else None,
        name="s5_fwd", grid=(n_chunks,),
        in_specs=[pl.BlockSpec((t_rows, D_MODEL), lambda i: (i, 0)), full3(bbd_re), full3(bbd_im), full3(cbd_re),
                  full3(cbd_imn), full2(a_re), full2(a_im), full2(dskip)],
        out_specs=[pl.BlockSpec((t_rows, D_MODEL), lambda i: (i, 0)),
                   pl.BlockSpec((t_rows, D_MODEL), lambda i: (i, 0)),
                   pl.BlockSpec((t_rows, N_STATES), lambda i: (i, 0)),
                   pl.BlockSpec((t_rows, N_STATES), lambda i: (i, 0))],
        out_shape=[jax.ShapeDtypeStruct((seq, D_MODEL), F32),
                   jax.ShapeDtypeStruct((seq, D_MODEL), BF16),
                   jax.ShapeDtypeStruct((seq, N_STATES), F32),
                   jax.ShapeDtypeStruct((seq, N_STATES), F32)],
        scratch_shapes=[tile] * 10,
        compiler_params=_params(("arbitrary",)))


def s5_bwd(dy, u, dres, h_re, h_im, bbd_re, bbd_im, cbd_re, cbd_imn, a_re, a_im, dskip, ride=None, t_rows=128):
    seq = u.shape[0]
    t_rows = min(t_rows, seq)
    n_chunks = seq // t_rows

    n_tiles = t_rows // SUBLANES

    def body(dy_ref, u_ref, dres_ref, hre_ref, him_ref, hpre_ref, hpim_ref, bre, bim, cre, cimn, are, aim, d_ref,
             dx_ref, dbre, dbim, dcre, dcimn, dar_ref, dai_ref, dd_ref, lre, lim, car_re, car_im, acc_re, acc_im,
             *shifted):
        i = pl.program_id(0)

        @pl.when(i == 0)
        def _():
            for r in (car_re, car_im, acc_re, acc_im, dbre, dbim, dcre, dcimn, dd_ref):
                r[...] = jnp.zeros_like(r)
            _scan_coefs(are, aim, shifted, reverse=True)

        dyf = dy_ref[...]
        dyb = dyf.astype(BF16)
        uf = u_ref[...]
        ub = uf.astype(BF16)
        for j in range(N_UTILES):
            dyj = dyb[:, LANES * j:LANES * (j + 1)]
            st = slice(TILE_STATES * j, TILE_STATES * (j + 1))
            lre[:, st] = lax.dot_general(dyj, cre[j], _NT, preferred_element_type=F32)
            lim[:, st] = lax.dot_general(dyj, cimn[j], _NT, preferred_element_type=F32)
        has_pred = (i < n_chunks - 1).astype(F32)
        last_row = lax.broadcasted_iota(jnp.int32, (SUBLANES, SCAN_STRIP), 0) == SUBLANES - 1
        for s in range(N_STRIPS):
            cols = pl.ds(s * SCAN_STRIP, SCAN_STRIP)
            coefs = [c[:, cols] for c in shifted]
            before_re, before_im = hpre_ref[:, cols] * has_pred, hpim_ref[:, cols] * has_pred

            def step(k, carry):
                after_re, after_im, dar, dai = carry
                t = n_tiles - 1 - k
                rows = _tile_rows(t)
                lr, li = _tile_scan(lre[rows, cols], lim[rows, cols], coefs, after_re, after_im, True)
                lre[rows, cols] = lr
                lim[rows, cols] = li
                prev = _tile_rows(jnp.maximum(t - 1, 0))
                pre_re = jnp.where(t == 0, before_re, hre_ref[prev, cols])
                pre_im = jnp.where(t == 0, before_im, him_ref[prev, cols])
                hpr = pltpu.roll(jnp.where(last_row, pre_re, hre_ref[rows, cols]), 1, 0)
                hpi = pltpu.roll(jnp.where(last_row, pre_im, him_ref[rows, cols]), 1, 0)
                return lr, li, dar + lr * hpr + li * hpi, dai + li * hpr - lr * hpi

            cr, ci, dar, dai = lax.fori_loop(0, n_tiles, step, (car_re[:, cols], car_im[:, cols],
                                                               acc_re[:, cols], acc_im[:, cols]))
            car_re[:, cols] = cr
            car_im[:, cols] = ci
            acc_re[:, cols] = dar
            acc_im[:, cols] = dai

        dv = d_ref[...]
        for j in range(N_UTILES):
            sl = slice(LANES * j, LANES * (j + 1))
            st = slice(TILE_STATES * j, TILE_STATES * (j + 1))
            lrj = lre[:, st].astype(BF16)
            lij = lim[:, st].astype(BF16)
            du = (lax.dot_general(lrj, bre[j], _NT, preferred_element_type=F32)
                  + lax.dot_general(lij, bim[j], _NT, preferred_element_type=F32))
            dx_ref[:, sl] = du + dv[:, sl] * dyf[:, sl] + DN_ALPHA * dres_ref[:, sl]
            uj = ub[:, sl]
            dbre[j] += lax.dot_general(uj, lrj, _TN, preferred_element_type=F32)
            dbim[j] += lax.dot_general(uj, lij, _TN, preferred_element_type=F32)
            dyj = dyb[:, sl]
            dcre[j] += lax.dot_general(hre_ref[:, st].astype(BF16), dyj, _TN, preferred_element_type=F32)
            dcimn[j] += lax.dot_general(him_ref[:, st].astype(BF16), dyj, _TN, preferred_element_type=F32)
        dd_ref[...] += jnp.sum(dyf * uf, axis=0, keepdims=True)

        @pl.when(i == n_chunks - 1)
        def _():
            dar_ref[...] = jnp.sum(acc_re[...], axis=0, keepdims=True)
            dai_ref[...] = jnp.sum(acc_im[...], axis=0, keepdims=True)

    rev = lambda i: (n_chunks - 1 - i, 0)
    prev_tile = lambda i: (jnp.maximum((n_chunks - 1 - i) * n_tiles - 1, 0), 0)
    full3 = lambda a: pl.BlockSpec(a.shape, lambda i: (0, 0, 0))
    full2 = lambda a: pl.BlockSpec(a.shape, lambda i: (0, 0))
    acc3 = lambda shape: pl.BlockSpec(shape, lambda i: (0, 0, 0))
    acc2 = lambda shape: pl.BlockSpec(shape, lambda i: (0, 0))
    tile = pltpu.VMEM((SUBLANES, N_STATES), F32)
    return _pcall_riding(
        body, (dy, u, dres, h_re, h_im, h_re, h_im, bbd_re, bbd_im, cbd_re, cbd_imn, a_re, a_im, dskip), ride,
        lambda: pl.program_id(0) == 0, lambda: pl.program_id(0) == n_chunks - 1,
        name="s5_bwd", grid=(n_chunks,),
        in_specs=[pl.BlockSpec((t_rows, D_MODEL), rev), pl.BlockSpec((t_rows, D_MODEL), rev),
                  pl.BlockSpec((t_rows, D_MODEL), rev),
                  pl.BlockSpec((t_rows, N_STATES), rev), pl.BlockSpec((t_rows, N_STATES), rev),
                  pl.BlockSpec((SUBLANES, N_STATES), prev_tile), pl.BlockSpec((SUBLANES, N_STATES), prev_tile),
                  full3(bbd_re), full3(bbd_im), full3(cbd_re), full3(cbd_imn), full2(a_re), full2(a_im), full2(dskip)],
        out_specs=[pl.BlockSpec((t_rows, D_MODEL), rev), acc3(bbd_re.shape), acc3(bbd_im.shape), acc3(cbd_re.shape),
                   acc3(cbd_imn.shape), acc2((1, N_STATES)), acc2((1, N_STATES)), acc2((1, D_MODEL))],
        out_shape=[jax.ShapeDtypeStruct((seq, D_MODEL), F32), jax.ShapeDtypeStruct(bbd_re.shape, F32),
                   jax.ShapeDtypeStruct(bbd_im.shape, F32), jax.ShapeDtypeStruct(cbd_re.shape, F32),
                   jax.ShapeDtypeStruct(cbd_imn.shape, F32), jax.ShapeDtypeStruct((1, N_STATES), F32),
                   jax.ShapeDtypeStruct((1, N_STATES), F32), jax.ShapeDtypeStruct((1, D_MODEL), F32)],
        scratch_shapes=[pltpu.VMEM((t_rows, N_STATES), F32), pltpu.VMEM((t_rows, N_STATES), F32)] + [tile] * 12,
        compiler_params=_params(("arbitrary",)))


def _eye_groups():
    return jnp.eye(GROUPS_PER_TILE, dtype=F32)


def _blockdiag_in(bb):
    t = bb.transpose(0, 2, 1).reshape(N_UTILES, GROUPS_PER_TILE, SSM_GROUP, SSM_STATE)
    bd = jnp.einsum("jgcp,gh->jgchp", t, _eye_groups())
    return bd.reshape(N_UTILES, LANES, TILE_STATES)


def _blockdiag_in_t(d):
    t = jnp.einsum("jgchp,gh->jgcp", d.reshape(N_UTILES, GROUPS_PER_TILE, SSM_GROUP, GROUPS_PER_TILE, SSM_STATE),
                   _eye_groups())
    return t.reshape(N_GROUPS, SSM_GROUP, SSM_STATE).transpose(0, 2, 1)


def _blockdiag_out(c):
    t = c.transpose(0, 2, 1).reshape(N_UTILES, GROUPS_PER_TILE, SSM_STATE, SSM_GROUP)
    bd = jnp.einsum("jhpc,hg->jhpgc", t, _eye_groups())
    return bd.reshape(N_UTILES, TILE_STATES, LANES)


def _blockdiag_out_t(d):
    t = jnp.einsum("jhpgc,hg->jhpc", d.reshape(N_UTILES, GROUPS_PER_TILE, SSM_STATE, GROUPS_PER_TILE, SSM_GROUP),
                   _eye_groups())
    return t.reshape(N_GROUPS, SSM_STATE, SSM_GROUP).transpose(0, 2, 1)


ATT_TQ = 512
ATT_TK = 512
LOG2E = math.log2(math.e)
LN2 = math.log(2.0)
Q_PRESCALE = SM_SCALE * LOG2E


def _loop_in_pairs(n, step, carry, start=0):
    pairs = (n - start) // 2

    def two(t, c):
        return step(start + 2 * t + 1, step(start + 2 * t, c))

    carry = lax.fori_loop(0, pairs, two, carry)
    return lax.fori_loop(start + 2 * pairs, n, step, carry)


def _causal(s, off=0, transposed=False):
    r = lax.broadcasted_iota(jnp.int32, s.shape, 0)
    c = lax.broadcasted_iota(jnp.int32, s.shape, 1)
    keep = (r <= c + off) if transposed else (c <= r + off)
    return jnp.where(keep, s, NEG_INF)


def _q_specs(rows, at):
    def nope(*ids):
        r, h = at(*ids)
        return r, 3 * (h // HEADS_PER_CHIP) + h % HEADS_PER_CHIP

    def rope(*ids):
        r, h = at(*ids)
        return r, 3 * (h // HEADS_PER_CHIP) + HEADS_PER_CHIP

    return [pl.BlockSpec((rows, LANES), nope), pl.BlockSpec((rows, LANES), rope)]


def _kv_specs(rows, at):
    def col(f):
        def index(*ids):
            r, h = at(*ids)
            return r, f(h)
        return index

    return [pl.BlockSpec((rows, LANES), col(lambda h: 2 * h)), pl.BlockSpec((rows, LANES), col(lambda h: h % HEADS_PER_CHIP)),
            pl.BlockSpec((rows, LANES), col(lambda h: 2 * h + 1))]


def _cat(a, b):
    return jnp.concatenate([a, b], axis=1)


def attn_fwd(q, kv, kr, ride=None, tq=ATT_TQ, tk=ATT_TK):
    seq = q.shape[0]
    n_heads = N_HEADS
    tq, tk = min(tq, seq), min(tk, seq)

    def body(qn_ref, qr_ref, kn_ref, kr_ref, v_ref, o_ref, lse_ref):
        qi = pl.program_id(1)
        qv = _cat(qn_ref[...], qr_ref[...])
        jd = (qi * tq) // tk

        def block(j, carry, diag):
            m, l, acc = carry
            rows = pl.ds(pl.multiple_of(j * tk, tk), tk)
            s = lax.dot_general(qv, _cat(kn_ref[rows, :], kr_ref[rows, :]), _NT, preferred_element_type=F32)
            if diag:
                s = _causal(s, qi * tq - jd * tk)
            m_new = jnp.maximum(m, jnp.max(s, axis=-1, keepdims=True))
            p = jnp.exp2(s - m_new)
            corr = jnp.exp2(m - m_new)
            l = l * corr + jnp.sum(p, axis=-1, keepdims=True)
            acc = acc * corr + jnp.dot(p.astype(BF16), v_ref[rows, :], preferred_element_type=F32)
            return m_new, l, acc

        init = (jnp.full((tq, 1), NEG_INF, F32), jnp.zeros((tq, 1), F32), jnp.zeros((tq, V_HEAD), F32))
        carry = _loop_in_pairs(jd, lambda j, c: block(j, c, False), init)
        m, l, acc = block(jd, carry, True)
        o_ref[...] = acc / l
        lse_ref[0] = jnp.broadcast_to(m + jnp.log2(l), (tq, LANES))

    n_q = seq // tq
    return _pcall_riding(
        body, (q, q, kv, kr, kv), ride,
        lambda: (pl.program_id(0) == 0) & (pl.program_id(1) == 0),
        lambda: (pl.program_id(0) == n_heads - 1) & (pl.program_id(1) == n_q - 1),
        middle=(lambda: (pl.program_id(0) == (5 * n_heads) // 8) & (pl.program_id(1) == 0)) if ride is not None else None,
        name="attn_fwd", grid=(n_heads, n_q),
        in_specs=_q_specs(tq, lambda h, i: (i, h)) + _kv_specs(seq, lambda h, i: (0, h)),
        out_specs=[pl.BlockSpec((tq, V_HEAD), lambda h, i: (i, h)),
                   pl.BlockSpec((1, tq, LANES), lambda h, i: (h, i, 0))],
        out_shape=[jax.ShapeDtypeStruct((seq, n_heads * V_HEAD), F32),
                   jax.ShapeDtypeStruct((n_heads, seq, LANES), F32)],
        compiler_params=_params(("arbitrary", "arbitrary")))


def attn_bwd(q, kv, kr, do, lse_row, delta_row, tq=ATT_TK):
    seq = q.shape[0]
    tq = min(tq, seq)
    n_blk = seq // tq

    def body(qn_ref, qr_ref, kn_ref, kr_ref, v_ref, do_ref, lse_ref, delta_ref, dqn_ref, dqr_ref, dkv_ref, dkr_ref, dq_acc):
        head, kj = pl.program_id(0), pl.program_id(1)

        @pl.when(kj == 0)
        def _():
            dq_acc[...] = jnp.zeros_like(dq_acc)

        kc = _cat(kn_ref[...], kr_ref[...])
        vv = v_ref[...]

        def block(i, carry, diag):
            dk, dv = carry
            rows = pl.ds(pl.multiple_of(i * tq, tq), tq)
            qv = _cat(qn_ref[rows, :], qr_ref[rows, :])
            st = lax.dot_general(kc, qv, _NT, preferred_element_type=F32)
            if diag:
                st = _causal(st, transposed=True)
            pt = jnp.exp2(st - lse_ref[0, pl.ds(i, 1), :])
            dob = do_ref[rows, :].astype(BF16)
            dv = dv + jnp.dot(pt.astype(BF16), dob, preferred_element_type=F32)
            dpt = lax.dot_general(vv, dob, _NT, preferred_element_type=F32)
            dst = (pt * (dpt - delta_ref[0, pl.ds(i, 1), :])).astype(BF16)
            dk = dk + jnp.dot(dst, qv, preferred_element_type=F32)
            dq_acc[rows, :] += lax.dot_general(dst, kc, _TN, preferred_element_type=F32)
            return dk, dv

        carry = block(kj, (jnp.zeros((tq, 2 * LANES), F32), jnp.zeros((tq, V_HEAD), F32)), True)
        dk, dv = _loop_in_pairs(n_blk, lambda i, c: block(i, c, False), carry, start=kj + 1)
        dk = dk * LN2
        dkv_ref[...] = _cat(dk[:, :LANES], dv).astype(dkv_ref.dtype)
        lane = lax.broadcasted_iota(jnp.int32, (tq, LANES), 1)
        mine = (lane // HALF_ROPE) % HEADS_PER_CHIP == head % HEADS_PER_CHIP
        dkr_ref[0] = jnp.where(mine, dk[:, LANES:], 0.0)

        @pl.when(kj == n_blk - 1)
        def _():
            dqn_ref[...] = dq_acc[:, :LANES] * SM_SCALE

        @pl.when((kj == n_blk - 1) & (head % HEADS_PER_CHIP == 0))
        def _():
            dqr_ref[...] = dq_acc[:, LANES:] * SM_SCALE

        @pl.when((kj == n_blk - 1) & (head % HEADS_PER_CHIP > 0))
        def _():
            dqr_ref[...] += dq_acc[:, LANES:] * SM_SCALE

    return _pcall(
        body, name="attn_bwd", grid=(N_HEADS, n_blk),
        in_specs=_q_specs(seq, lambda h, j: (0, h)) + _kv_specs(tq, lambda h, j: (j, h))
        + [pl.BlockSpec((seq, V_HEAD), lambda h, j: (0, h)),
           pl.BlockSpec((1, n_blk, tq), lambda h, j: (h, 0, 0)),
           pl.BlockSpec((1, n_blk, tq), lambda h, j: (h, 0, 0))],
        out_specs=[pl.BlockSpec((seq, LANES), lambda h, j: (0, h)),
                   pl.BlockSpec((seq, LANES), lambda h, j: (0, h // HEADS_PER_CHIP)),
                   pl.BlockSpec((tq, QK_NOPE + V_HEAD), lambda h, j: (j, h)),
                   pl.BlockSpec((1, tq, LANES), lambda h, j: (h, j, 0))],
        out_shape=[jax.ShapeDtypeStruct((seq, N_HEADS * QK_NOPE), F32),
                   jax.ShapeDtypeStruct((seq, N_CHIPS * LANES), F32),
                   jax.ShapeDtypeStruct((seq, N_HEADS * (QK_NOPE + V_HEAD)), BF16),
                   jax.ShapeDtypeStruct((N_HEADS, seq, LANES), F32)],
        scratch_shapes=[pltpu.VMEM((seq, 2 * LANES), F32)],
        compiler_params=_params(("arbitrary", "arbitrary")),
    )(q, q, kv, kr, kv, do, lse_row, delta_row)


def head_sum(x, ts=512):
    n_heads, seq, w = x.shape
    ts = min(ts, seq)

    def body(x_ref, o_ref):
        o_ref[...] = jnp.sum(x_ref[...], axis=0)

    return _pcall(body, name="head_sum", grid=(seq // ts,),
                  in_specs=[pl.BlockSpec((n_heads, ts, w), lambda i: (0, i, 0))],
                  out_specs=pl.BlockSpec((ts, w), lambda i: (i, 0)),
                  out_shape=jax.ShapeDtypeStruct((seq, w), F32),
                  compiler_params=_params(("parallel",)))(x)


HEADS_PER_CHIP = N_HEADS // N_CHIPS
Q_CHIP = HEADS_PER_CHIP * QK_DIM
Q_CHIP_NOPE = HEADS_PER_CHIP * QK_NOPE


def _perm_q_cols(w):
    t = w.reshape(w.shape[0], HEADS_PER_CHIP, QK_DIM)
    return jnp.concatenate([t[:, :, :QK_NOPE].reshape(w.shape[0], -1),
                            t[:, :, QK_NOPE:QK_NOPE + HALF_ROPE].reshape(w.shape[0], -1),
                            t[:, :, QK_NOPE + HALF_ROPE:].reshape(w.shape[0], -1)], axis=1)


def _unperm_q_cols(w):
    r = w.shape[0]
    nope = w[:, :Q_CHIP_NOPE].reshape(r, HEADS_PER_CHIP, QK_NOPE)
    r1 = w[:, Q_CHIP_NOPE:Q_CHIP_NOPE + QK_ROPE].reshape(r, HEADS_PER_CHIP, HALF_ROPE)
    r2 = w[:, Q_CHIP_NOPE + QK_ROPE:].reshape(r, HEADS_PER_CHIP, HALF_ROPE)
    return jnp.concatenate([nope, r1, r2], axis=2).reshape(r, Q_CHIP)


def _pad_kva_cols(w):
    z = jnp.zeros((w.shape[0], HALF_ROPE), w.dtype)
    return jnp.concatenate([w[:, :KV_LORA], w[:, KV_LORA:KV_LORA + HALF_ROPE], z, w[:, KV_LORA + HALF_ROPE:], z], axis=1)


def _unpad_kva_cols(w):
    return jnp.concatenate([w[:, :KV_LORA], w[:, KV_LORA:KV_LORA + HALF_ROPE],
                            w[:, KV_LORA + QK_ROPE:KV_LORA + QK_ROPE + HALF_ROPE]], axis=1)


def _rope_tile(t, cs, sn):
    return t * cs + pltpu.roll(t, LANES // 2, 1) * sn


def _rope_tile_bwd(d, cs, sn):
    return d * cs + pltpu.roll(d * sn, LANES // 2, 1)


def _b_cols(tk, tn):
    return pl.BlockSpec((None, tk, tn), lambda i, j, k: (j, k, 0))


def _b_cols_t(tk, tn):
    return pl.BlockSpec((None, tn, tk), lambda i, j, k: (k, j, 0))


def _out_cols(shape):
    return shape, lambda tm, tn: pl.BlockSpec((None, tm, tn), lambda i, j, k: (j, i, 0))


def _halves(a):
    return a.reshape(N_CHIPS, 2, a.shape[1] // 2, a.shape[2])


def device_step(x, positions, target, w, comm=None):
    seq = x.shape[0]
    w = dict(w)

    def gathered(names, outs):
        for n, a in zip(names, outs):
            if isinstance(n, tuple):
                w[n[0]] = [a.reshape(v.shape) if l == n[1] else v for l, v in enumerate(w[n[0]])]
            else:
                w[n] = a.reshape(w[n].shape)

    def ride_for(names):
        if comm is None:
            return None
        return GatherRide([_halves(w[n[0]][n[1]] if isinstance(n, tuple) else w[n]) for n in names])

    first_ride = ("ssm_w_glu", "ssm_w_out", ("w_ff1", 0), ("w_ff2", 0))
    mla_ride = ("kv_w_a", "kv_w_b", "q_w_a", "q_w_b", "attn_w_o")
    second_ride = (("w_ff1", 1), ("w_ff2", 1))

    inv_freq = ROPE_THETA ** (-jnp.arange(HALF_ROPE, dtype=F32) / HALF_ROPE)
    ang = positions.astype(F32)[:, None] * inv_freq
    cos, sin = jnp.cos(ang), jnp.sin(ang)
    zero = jnp.zeros_like(cos)
    cos_q, sin_q = jnp.concatenate([cos] * 4, 1), jnp.concatenate([-sin, -sin, sin, sin], 1)
    cos_k, sin_k = jnp.concatenate([cos, zero, cos, zero], 1), jnp.concatenate([-sin, zero, sin, zero], 1)
    ff_tile = D_FF // N_CHIPS
    pack_shape = (N_CHIPS, EARLY_ROWS, PACK_W)

    lr = w["ssm_lam_re"].reshape(N_STATES, 1)
    li = w["ssm_lam_im"].reshape(N_STATES, 1)
    ldt = jnp.repeat(w["ssm_log_dt"].reshape(N_GROUPS), SSM_STATE).reshape(N_STATES, 1)
    b_re = w["ssm_b_re"].reshape(N_STATES, SSM_GROUP)
    b_im = w["ssm_b_im"].reshape(N_STATES, SSM_GROUP)
    a_re, a_im, bb_re, bb_im = s5_prep(lr, li, ldt, b_re, b_im)
    a_re, a_im = a_re.reshape(1, N_STATES), a_im.reshape(1, N_STATES)
    bbd_re = _blockdiag_in(bb_re.reshape(N_GROUPS, SSM_STATE, SSM_GROUP)).astype(BF16)
    bbd_im = _blockdiag_in(bb_im.reshape(N_GROUPS, SSM_STATE, SSM_GROUP)).astype(BF16)
    cbd_re = _blockdiag_out(w["ssm_c_re"].reshape(N_GROUPS, SSM_GROUP, SSM_STATE)).astype(BF16)
    cbd_imn = _blockdiag_out(-w["ssm_c_im"].reshape(N_GROUPS, SSM_GROUP, SSM_STATE)).astype(BF16)
    dskip = w["ssm_d"].reshape(1, D_MODEL)
    (ypre, yg, h_re, h_im), landed = s5_fwd(x, bbd_re, bbd_im, cbd_re, cbd_imn, a_re, a_im, dskip, ride_for(first_ride))
    gathered(first_ride, landed)
    w_glu = w["ssm_w_glu"]
    glu_tile = w_glu.shape[2]
    vg = mm(yg, w_glu, n_dim=2 * D_MODEL, tiles=(None, glu_tile, None), b_view=_b_cols, name="glu_proj")

    def glu(v):
        return (v[:, :D_MODEL] * _sigmoid(v[:, D_MODEL:]),), ()
    (z,) = rowwise(glu, (vg,), ((D_MODEL, BF16),), name="glu")
    w_out = w["ssm_w_out"].reshape(D_MODEL, D_MODEL)
    mix0 = mm(z, w_out, name="ssm_out")

    def mlp_fwd(hb, layer, riding=None):
        pre = mm(hb, w["w_ff1"][layer], n_dim=D_FF, tiles=(None, ff_tile, None), b_view=_b_cols, name=f"ff1_{layer}",
                 out_dtypes=(BF16,), ride=ride_for(riding) if riding else None)
        if riding and comm is not None:
            pre, landed = pre
            gathered(riding, landed)
        f = mm(pre, w["w_ff2"][layer].reshape(D_FF, D_MODEL), pro_a=_relu2, name=f"ff2_{layer}")
        return pre, f

    ln = lambda name, l: w[name][l].reshape(1, D_MODEL)
    h1, h1b = ln_fwd(x, mix0, ln("ln_mix_g", 0), ln("ln_mix_b", 0), "ln_mix_0")
    f1pre, f1 = mlp_fwd(h1b, 0, mla_ride)
    h2, h2b = ln_fwd(h1, f1, ln("ln_ffn_g", 0), ln("ln_ffn_b", 0), "ln_ffn_0")

    kv_w_a = w["kv_w_a"].reshape(D_MODEL, KVA_PAD)
    kv_w_b = w["kv_w_b"]
    q_w_a = w["q_w_a"].reshape(D_MODEL, Q_LORA)
    q_w_b = w["q_w_b"]
    w_o = w["attn_w_o"].reshape(D_MODEL, D_MODEL)
    kvb_tile = kv_w_b.shape[2]
    kvn_g = w["kv_norm_g"].reshape(1, KV_LORA)
    qn_g = w["q_norm_g"].reshape(1, Q_LORA)
    kva = mm(h2b, kv_w_a, name="kv_a")

    def kv_post(kva, g, cs, sn):
        tile = _rope_tile(kva[:, KV_LORA:], cs, sn)
        return (_rms(kva[:, :KV_LORA], g), _cat(tile, pltpu.roll(tile, HALF_ROPE, 1))), ()
    ckv, krope = rowwise(kv_post, (kva, kvn_g, cos_k, sin_k), ((KV_LORA, BF16), (2 * LANES, BF16)), name="kv_post")
    kvb = mm(ckv, kv_w_b, n_dim=N_CHIPS * kvb_tile, tiles=(None, kvb_tile, KV_LORA), b_view=_b_cols, name="kv_b",
             out_dtypes=(BF16,))
    cq_raw = mm(h2b, q_w_a, name="q_a")
    (cq,) = rowwise(lambda c, g: ((_rms(c, g),), ()), (cq_raw, qn_g), ((Q_LORA, BF16),), name="q_norm")
    qlin = mm(cq, q_w_b, n_dim=N_CHIPS * Q_CHIP, tiles=(None, Q_CHIP, Q_LORA), b_view=_b_cols, name="q_b")

    def on_rope_tiles(fn, scale=None):
        def apply(q, cs, sn):
            parts = []
            for k in range(N_CHIPS):
                parts.append(q[:, Q_CHIP * k:Q_CHIP * k + Q_CHIP_NOPE])
                parts.append(fn(q[:, Q_CHIP * k + Q_CHIP_NOPE:Q_CHIP * (k + 1)], cs, sn))
            out = jnp.concatenate(parts, axis=1)
            return (out if scale is None else out * scale,), ()
        return apply
    (qro,) = rowwise(on_rope_tiles(_rope_tile, Q_PRESCALE), (qlin, cos_q, sin_q), ((N_CHIPS * Q_CHIP, BF16),),
                     name="q_rope")
    (o, lse), landed = attn_fwd(qro, kvb, krope, ride_for(second_ride))
    gathered(second_ride, landed)
    mix1 = mm(o, w_o, name="attn_out")
    h3, h3b = ln_fwd(h2, mix1, ln("ln_mix_g", 1), ln("ln_mix_b", 1), "ln_mix_1")
    f2pre, f2 = mlp_fwd(h3b, 1)
    def last_ln_and_loss(h, mix, gl, bl, t):
        e = _layer_norm(h, mix, gl, bl) - t
        return (e * (1.0 / D_MODEL),), (jnp.broadcast_to(jnp.sum(e * e), (1, LANES)),)
    dh4, loss_acc = rowwise(last_ln_and_loss, (h3, f2, ln("ln_ffn_g", 1), ln("ln_ffn_b", 1), target), ((D_MODEL, F32),),
                            accs=(LANES,), name="ln_ffn_1_loss")
    loss = loss_acc[0, 0] * (0.5 / D_MODEL)

    g = {}

    def into_rows(off, rows_per_chip, shape=pack_shape):
        def view(tm, tn):
            nb = rows_per_chip // tm
            return pl.BlockSpec((None, tm, tn), lambda i, j, k: (i // nb, off // tm + i % nb, 0))
        return shape, view

    def into_cols(off):
        return pack_shape, lambda tm, tn: pl.BlockSpec((None, tm, tn), lambda i, j, k: (j, off // tm + i, 0))

    def mlp_bwd(pack, dr, drb, hb, pre, layer, swap=False):
        dpre = mm(drb, w["w_ff2"][layer].reshape(D_FF, D_MODEL), tb=True, epi=lambda r, p: (r * 2.0 * jnp.maximum(p, 0.0),),
                  extras=(pre,), out_dtypes=(BF16,), tiles=(None, ff_tile, None), name=f"ff2_dx_{layer}")
        pack = mm(pre, drb, ta=True, pro_a=_relu2, name=f"ff2_dw_{layer}", tiles=(ff_tile, PACK_W, None), into=pack,
                  out_view=into_rows(EARLY_OFF["w_ff2"] + layer * ff_tile, ff_tile))
        pack = mm(hb, dpre, ta=True, name=f"ff1_dw_{layer}", tiles=(None, PACK_W, None), into=pack,
                  out_view=into_cols(EARLY_OFF["w_ff1"] + layer * D_MODEL))
        dh = mm(dpre, w["w_ff1"][layer], tb=True, epi=lambda r, d: (r + DN_ALPHA * d,), extras=(dr,), n_dim=D_MODEL,
                tiles=(None, D_MODEL, ff_tile), b_view=_b_cols_t, name=f"ff1_dx_{layer}",
                ride=SwapRide(pack) if swap else None)
        return (pack, *dh) if swap else (pack, dh)

    dr4, dr4b, dg_f1, db_f1 = ln_bwd(h3, f2, ln("ln_ffn_g", 1), dh4, "ln_ffn_bwd_1")
    pack, dh3 = mlp_bwd(None, dr4, dr4b, h3b, f2pre, 1)
    dr3, dr3b, dg_m1, db_m1 = ln_bwd(h2, mix1, ln("ln_mix_g", 1), dh3, "ln_mix_bwd_1")
    shard_rows = D_MODEL // N_CHIPS
    pack = mm(o, dr3b, ta=True, name="attn_out_dw", tiles=(shard_rows, PACK_W, None), into=pack,
              out_view=into_rows(EARLY_OFF["attn_w_o"], shard_rows))
    do = mm(dr3b, w_o, tb=True, name="attn_out_dx")
    def head_dots(do, o):
        return (jnp.concatenate([jnp.sum(do[:, V_HEAD * h:V_HEAD * (h + 1)] * o[:, V_HEAD * h:V_HEAD * (h + 1)], axis=1,
                                         keepdims=True) for h in range(N_HEADS)], axis=1),), ()
    (delta,) = rowwise(head_dots, (do, o), ((N_HEADS, F32),), name="attn_delta")
    tb = min(ATT_TK, seq)
    lse_row = lse[:, :, 0].reshape(N_HEADS, seq // tb, tb)
    delta_row = delta.T.reshape(N_HEADS, seq // tb, tb)
    dqn, dqr, dkvb, dkr = attn_bwd(qro, kvb, krope, do, lse_row, delta_row)

    def q_rope_bwd(dn, dr, cs, sn):
        parts = []
        for k in range(N_CHIPS):
            parts.append(dn[:, Q_CHIP_NOPE * k:Q_CHIP_NOPE * (k + 1)])
            parts.append(_rope_tile_bwd(dr[:, LANES * k:LANES * (k + 1)], cs, sn))
        return (jnp.concatenate(parts, axis=1),), ()
    (dqlin,) = rowwise(q_rope_bwd, (dqn, dqr, cos_q, sin_q), ((N_CHIPS * Q_CHIP, BF16),), name="q_rope_bwd")
    g["q_w_b"] = mm(cq, dqlin, ta=True, name="q_b_dw", tiles=(Q_LORA, Q_CHIP, None), out_view=_out_cols(q_w_b.shape))
    dcq = mm(dqlin, q_w_b, tb=True, n_dim=Q_LORA, tiles=(None, Q_LORA, Q_CHIP), b_view=_b_cols_t, name="q_b_dx")

    def q_norm_bwd(c, gq, d):
        dx, dgq = _rms_bwd(c, gq, d)
        return (dx,), (dgq,)
    dcq_raw, dqn_g = rowwise(q_norm_bwd, (cq_raw, qn_g, dcq), ((Q_LORA, BF16),), accs=(Q_LORA,), name="q_norm_bwd")
    g["q_w_a"] = mm(h2b, dcq_raw, ta=True, name="q_a_dw")
    g["kv_w_b"] = mm(ckv, dkvb, ta=True, name="kv_b_dw", tiles=(KV_LORA, kvb_tile, None), out_view=_out_cols(kv_w_b.shape))
    dckv = mm(dkvb, kv_w_b, tb=True, n_dim=KV_LORA, tiles=(None, KV_LORA, kvb_tile), b_view=_b_cols_t, name="kv_b_dx")
    dkr_sum = head_sum(dkr)

    def kv_post_bwd(kva, gk, dc, dk, cs, sn):
        dx, dgk = _rms_bwd(kva[:, :KV_LORA], gk, dc)
        dk = dk + pltpu.roll(dk, LANES - HALF_ROPE, 1)
        return (jnp.concatenate([dx, _rope_tile_bwd(dk, cs, sn)], axis=1),), (dgk,)
    dkva, dkvn_g = rowwise(kv_post_bwd, (kva, kvn_g, dckv, dkr_sum, cos_k, sin_k), ((KVA_PAD, BF16),),
                           accs=(KV_LORA,), name="kv_post_bwd")
    g["kv_w_a"] = mm(h2b, dkva, ta=True, name="kv_a_dw")
    dh2 = mm(dcq_raw, q_w_a, tb=True, epi=lambda r, d: (r + DN_ALPHA * d,), extras=(dr3,), name="q_a_dx")
    dh2 = mm(dkva, kv_w_a, tb=True, epi=lambda r, d: (r + d,), extras=(dh2,), name="kv_a_dx")

    dr2, dr2b, dg_f0, db_f0 = ln_bwd(h1, f1, ln("ln_ffn_g", 0), dh2, "ln_ffn_bwd_0")
    pack = put_rows(pack, packed_shards(g, MISC_EARLY, EARLY_ROWS - MISC_EARLY_OFF), MISC_EARLY_OFF)
    if comm is None:
        pack, dh1 = mlp_bwd(pack, dr2, dr2b, h1b, f1pre, 0)
    else:
        pack, dh1, (theirs,) = mlp_bwd(pack, dr2, dr2b, h1b, f1pre, 0, swap=True)
        early_sums = add_halves(pack, theirs, comm[1])
    dr1, dr1b, dg_m0, db_m0 = ln_bwd(x, mix0, ln("ln_mix_g", 0), dh1, "ln_mix_bwd_0")
    mid = mm(z, dr1b, ta=True, name="ssm_out_dw", tiles=(shard_rows, PACK_W, None),
             out_view=into_rows(MID_OFF["ssm_w_out"], shard_rows, (N_CHIPS, MID_ROWS, PACK_W)))
    dz = mm(dr1b, w_out, tb=True, name="ssm_out_dx")

    def glu_bwd(v, dz):
        val, sg = v[:, :D_MODEL], _sigmoid(v[:, D_MODEL:])
        return (jnp.concatenate([dz * sg, dz * val * sg * (1.0 - sg)], axis=1),), ()
    (dvg,) = rowwise(glu_bwd, (vg, dz), ((2 * D_MODEL, BF16),), name="glu_bwd")
    g["ssm_w_glu"] = mm(yg, dvg, ta=True, name="glu_proj_dw", tiles=(None, glu_tile, None), out_view=_out_cols(w_glu.shape))
    mid = put_rows(mid, packed_shards(g, MISC_MID, MID_ROWS - MISC_MID_OFF), MISC_MID_OFF)
    dypre = mm(dvg, w_glu, tb=True, epi=lambda r, y: (r * _gelu_grad(y),), extras=(ypre,), n_dim=D_MODEL,
               tiles=(None, D_MODEL, glu_tile), b_view=_b_cols_t, name="glu_proj_dx",
               ride=SwapRide(mid) if comm is not None else None)
    sends = None
    if comm is not None:
        dypre, (theirs,) = dypre
        sends = SendRide([early_sums, add_halves(mid, theirs, comm[1])])
    (dx, dbbd_re, dbbd_im, dcbd_re, dcbd_imn, dar, dai, dd), got = s5_bwd(
        dypre, x, dr1, h_re, h_im, bbd_re, bbd_im, cbd_re, cbd_imn, a_re, a_im, dskip, sends)
    dbb_re = _blockdiag_in_t(dbbd_re).reshape(N_STATES, SSM_GROUP)
    dbb_im = _blockdiag_in_t(dbbd_im).reshape(N_STATES, SSM_GROUP)
    dlr, dli, dldt, db_re, db_im = s5_prep_bwd(lr, li, ldt, b_re, b_im, dar.reshape(N_STATES, 1),
                                               dai.reshape(N_STATES, 1), dbb_re, dbb_im)
    g["ssm_lam_re"] = dlr.reshape(1, N_GROUPS, SSM_STATE)
    g["ssm_lam_im"] = dli.reshape(1, N_GROUPS, SSM_STATE)
    g["ssm_log_dt"] = group_sum(dldt).reshape(1, N_GROUPS)
    g["ssm_b_re"] = db_re.reshape(1, N_GROUPS, SSM_STATE, SSM_GROUP)
    g["ssm_b_im"] = db_im.reshape(1, N_GROUPS, SSM_STATE, SSM_GROUP)
    g["ssm_c_re"] = _blockdiag_out_t(dcbd_re).reshape(1, N_GROUPS, SSM_GROUP, SSM_STATE)
    g["ssm_c_im"] = -_blockdiag_out_t(dcbd_imn).reshape(1, N_GROUPS, SSM_GROUP, SSM_STATE)
    g["ssm_d"] = dd
    g["ln_mix_g"] = jnp.concatenate([dg_m0, dg_m1], 0)
    g["ln_mix_b"] = jnp.concatenate([db_m0, db_m1], 0)
    g["ln_ffn_g"] = jnp.concatenate([dg_f0, dg_f1], 0)
    g["ln_ffn_b"] = jnp.concatenate([db_f0, db_f1], 0)
    g["kv_norm_g"] = dkvn_g.reshape(KV_LORA)
    g["q_norm_g"] = dqn_g
    return loss, dx, pack, mid, g, list(zip(sends.ins, got)) if comm is not None else None


def place(shard, me_idx, dtype, name, layer=None):
    rows, cols = shard.shape[-2:]
    tr = _tile(rows, (512, 256, 128))

    def body(m_ref, x_ref, o_ref):
        o_ref[...] = x_ref[...].astype(o_ref.dtype)

    in_spec = (pl.BlockSpec((tr, cols), lambda i, m: (i, 0)) if layer is None
               else pl.BlockSpec((None, tr, cols), lambda i, m: (layer, i, 0)))
    return _pcall(
        body, name=name,
        grid_spec=pltpu.PrefetchScalarGridSpec(
            num_scalar_prefetch=1, grid=(rows // tr,), in_specs=[in_spec],
            out_specs=pl.BlockSpec((None, tr, cols), lambda i, m: (m[0], i, 0))),
        out_shape=jax.ShapeDtypeStruct((N_CHIPS, rows, cols), dtype),
        compiler_params=_params(("parallel",)),
    )(me_idx, shard)


def place_many(shards, dtypes, me_idx, name):
    def body(m_ref, *refs):
        for x_ref, o_ref in zip(refs[:len(shards)], refs[len(shards):]):
            o_ref[...] = x_ref[...].astype(o_ref.dtype)

    return _pcall(
        body, name=name,
        grid_spec=pltpu.PrefetchScalarGridSpec(
            num_scalar_prefetch=1, grid=(1,),
            in_specs=[pl.BlockSpec(s.shape, lambda i, m: (0, 0)) for s in shards],
            out_specs=[pl.BlockSpec((None,) + s.shape, lambda i, m: (m[0], 0, 0)) for s in shards]),
        out_shape=[jax.ShapeDtypeStruct((N_CHIPS,) + s.shape, d) for s, d in zip(shards, dtypes)],
        compiler_params=_params(("arbitrary",)),
    )(me_idx, *shards)


def put_rows(pack, rows, off):
    _, n, cols = rows.shape
    tr = math.gcd(math.gcd(off, n), 512)

    def body(r_ref, p_ref, o_ref):
        o_ref[...] = r_ref[...]

    return _pcall(body, name="grad_put_rows", grid=(N_CHIPS, n // tr),
                  in_specs=[pl.BlockSpec((None, tr, cols), lambda k, i: (k, i, 0)), _ANY],
                  out_specs=pl.BlockSpec((None, tr, cols), lambda k, i: (k, off // tr + i, 0)),
                  out_shape=jax.ShapeDtypeStruct(pack.shape, pack.dtype), input_output_aliases={1: 0},
                  compiler_params=_params(("parallel", "parallel")))(rows, pack)


def _my_cols(c, mine=True):
    start = (c if mine else 1 - c) * HALF_W
    return pl.ds(pl.multiple_of(start, HALF_W), HALF_W)


def add_halves(gpack, got, c_idx):
    n, rows, _ = gpack.shape
    tr = min(G_BLOCK_ROWS, rows)
    blk = (None, tr, HALF_W)

    def body(c_ref, g_ref, r_ref, o_ref):
        o_ref[...] = (g_ref[...] + r_ref[...]).astype(o_ref.dtype)

    return _pcall(
        body, name="grad_add_halves",
        grid_spec=pltpu.PrefetchScalarGridSpec(
            num_scalar_prefetch=1, grid=(n, rows // tr),
            in_specs=[pl.BlockSpec(blk, lambda k, i, c: (k, i, c[0])), pl.BlockSpec(blk, lambda k, i, c: (k, i, 0))],
            out_specs=pl.BlockSpec(blk, lambda k, i, c: (k, i, 0))),
        out_shape=jax.ShapeDtypeStruct((n, rows, HALF_W), BF16),
        compiler_params=_params(("parallel", "parallel")),
    )(c_idx, gpack, got)


def sum_owner(part, got, idx, total_rows, row_off=0, into=None):
    _, rows, _ = part.shape
    tr = min(G_BLOCK_ROWS, rows)
    n_into = 0 if into is None else 1

    def body(m_ref, p_ref, g_ref, *rest):
        up = lambda v: v.astype(F32)
        rest[-1][...] = ((up(p_ref[...]) + up(g_ref[0])) + up(g_ref[1])) + up(g_ref[2])

    return _pcall(
        body, name="grad_sum_owner",
        grid_spec=pltpu.PrefetchScalarGridSpec(
            num_scalar_prefetch=1, grid=(rows // tr,),
            in_specs=[pl.BlockSpec((None, tr, HALF_W), lambda i, m: (m[0], i, 0)),
                      pl.BlockSpec((3, tr, HALF_W), lambda i, m: (0, i, 0))] + [_ANY] * n_into,
            out_specs=pl.BlockSpec((tr, HALF_W), lambda i, m: (row_off // tr + i, m[1]))),
        out_shape=jax.ShapeDtypeStruct((total_rows, PACK_W), F32),
        input_output_aliases={3: 0} if n_into else {},
        compiler_params=_params(("parallel",)),
    )(idx, part, got, *([into] if n_into else []))


def join_halves(red):
    def body(in_ref, out_ref, send_sem, recv_sem):
        x, y, c, _ = _place()
        sibling = (x, y, 1 - c)
        mine = out_ref.at[:, _my_cols(c)]
        cp = pltpu.make_async_remote_copy(src_ref=mine, dst_ref=mine, send_sem=send_sem, recv_sem=recv_sem,
                                          device_id=sibling, device_id_type=MESH)
        cp.start()
        cp.wait_send()
        other = out_ref.at[:, _my_cols(c, mine=False)]
        pltpu.make_async_remote_copy(src_ref=other, dst_ref=other, send_sem=send_sem, recv_sem=recv_sem,
                                     device_id=sibling, device_id_type=MESH).wait_recv()

    return _pcall(body, name="grad_join_halves", in_specs=[_ANY], out_specs=_ANY,
                  out_shape=jax.ShapeDtypeStruct(red.shape, red.dtype), input_output_aliases={0: 0},
                  scratch_shapes=[pltpu.SemaphoreType.DMA, pltpu.SemaphoreType.DMA])(red)


def adamw(gsrc, g_off, wt, m, v, name):
    n, cols = wt.shape
    tr = math.gcd(math.gcd(g_off, n), 256) if g_off else math.gcd(n, 256)
    off_blk = g_off // tr
    c1 = 1.0 / (1.0 - ADAM_B1 ** ADAM_STEP)
    c2 = 1.0 / (1.0 - ADAM_B2 ** ADAM_STEP)

    def body(g_ref, w_ref, m_ref, v_ref, go_ref, d_ref, mo_ref, vo_ref):
        gv = g_ref[...]
        mn = ADAM_B1 * m_ref[...] + (1.0 - ADAM_B1) * gv
        vn = ADAM_B2 * v_ref[...] + (1.0 - ADAM_B2) * gv * gv
        go_ref[...] = gv
        mo_ref[...] = mn
        vo_ref[...] = vn
        d_ref[...] = -ADAM_LR * ((mn * c1) / (jnp.sqrt(vn * c2) + ADAM_EPS) + ADAM_WD * w_ref[...])

    blk = pl.BlockSpec((tr, cols), lambda i: (i, 0))
    return _pcall(body, name=name, grid=(n // tr,),
                  in_specs=[pl.BlockSpec((tr, cols), lambda i: (off_blk + i, 0)), blk, blk, blk],
                  out_specs=[blk] * 4, out_shape=[jax.ShapeDtypeStruct((n, cols), F32)] * 4,
                  compiler_params=_params(("parallel",)))(gsrc, wt, m, v)


def _rows8(a):
    return -(-a.size // (8 * PACK_W)) * 8


def _as_rows(a, rows=None):
    flat = a.reshape(-1)
    n = _rows8(a) if rows is None else rows
    return jnp.pad(flat, (0, n * PACK_W - flat.shape[0])).reshape(n, PACK_W)


def local_shards_2d(wl):
    return {"w_ff1": [wl["w_ff1"][0], wl["w_ff1"][1]], "w_ff2": [wl["w_ff2"][0], wl["w_ff2"][1]],
            "ssm_w_glu": wl["ssm_w_glu"], "ssm_w_out": wl["ssm_w_out"], "kv_w_a": _pad_kva_cols(wl["kv_w_a"]),
            "kv_w_b": wl["kv_w_b"], "q_w_a": wl["q_w_a"], "q_w_b": _perm_q_cols(wl["q_w_b"]),
            "attn_w_o": wl["attn_w_o"], "ssm_d": wl["ssm_d"].reshape(2, -1)}


def misc_grad_shard(name, g, k):
    if name == "ssm_d":
        w = D_MODEL // N_CHIPS
        return g[:, w * k:w * (k + 1)]
    if name in ("ssm_w_glu", "kv_w_b"):
        return g[k]
    if name == "q_w_b":
        return _unperm_q_cols(g[k])
    rows = D_MODEL // N_CHIPS
    shard = g[rows * k:rows * (k + 1)]
    return _unpad_kva_cols(shard) if name == "kv_w_a" else shard


def packed_shards(g, names, rows, tail=None):
    blocks = []
    for k in range(N_CHIPS):
        parts = [_as_rows(misc_grad_shard(n, g[n], k), MISC_SHARD_ROWS[n]) for n in names]
        if tail is not None:
            parts.append(tail[k * (tail.shape[0] // N_CHIPS):(k + 1) * (tail.shape[0] // N_CHIPS)])
        blk = jnp.concatenate(parts, axis=0)
        blocks.append(jnp.pad(blk, ((0, rows - blk.shape[0]), (0, 0))))
    return jnp.stack(blocks)


def kernel(x, positions, ln_mix_g, ln_mix_b, ln_ffn_g, ln_ffn_b, w_ff1, w_ff2, ssm_lam_re, ssm_lam_im, ssm_log_dt, ssm_b_re, ssm_b_im, ssm_c_re, ssm_c_im, ssm_d, ssm_w_glu, ssm_w_out, kv_w_a, kv_norm_g, kv_w_b, q_w_a, q_norm_g, q_w_b, attn_w_o, loss_target, m_ln_mix_g, m_ln_mix_b, m_ln_ffn_g, m_ln_ffn_b, m_w_ff1, m_w_ff2, m_ssm_lam_re, m_ssm_lam_im, m_ssm_log_dt, m_ssm_b_re, m_ssm_b_im, m_ssm_c_re, m_ssm_c_im, m_ssm_d, m_ssm_w_glu, m_ssm_w_out, m_kv_w_a, m_kv_norm_g, m_kv_w_b, m_q_w_a, m_q_norm_g, m_q_w_b, m_attn_w_o, v_ln_mix_g, v_ln_mix_b, v_ln_ffn_g, v_ln_ffn_b, v_w_ff1, v_w_ff2, v_ssm_lam_re, v_ssm_lam_im, v_ssm_log_dt, v_ssm_b_re, v_ssm_b_im, v_ssm_c_re, v_ssm_c_im, v_ssm_d, v_ssm_w_glu, v_ssm_w_out, v_kv_w_a, v_kv_norm_g, v_kv_w_b, v_q_w_a, v_q_norm_g, v_q_w_b, v_attn_w_o):
    env = dict(locals())
    wl = {n: env[n] for n in WEIGHTS}
    ml = {n: env["m_" + n] for n in WEIGHTS}
    vl = {n: env["v_" + n] for n in WEIGHTS}
    for n in ("ssm_w_glu", "ssm_w_out", "q_w_a", "q_w_b", "attn_w_o"):
        wl[n], ml[n], vl[n] = wl[n][0], ml[n][0], vl[n][0]

    c_idx = lax.axis_index("c").astype(jnp.int32).reshape(1)
    me_idx = (2 * lax.axis_index("x") + lax.axis_index("y")).astype(jnp.int32).reshape(1)

    local = local_shards_2d(wl)
    stacked = {n: [place(wl[n], me_idx, BF16, f"place_{n}_{l}", layer=l) for l in range(DEPTH)] for n in ("w_ff1", "w_ff2")}
    others = [n for n in SHARDED if n not in stacked]
    stacked.update(zip(others, place_many([local[n] for n in others], [F32 if n == "ssm_d" else BF16 for n in others],
                                          me_idx, "place_others")))
    stacked["ssm_d"] = ride_alone(GatherRide([_halves(stacked["ssm_d"])]), "ssm_d_all_gather")[0].reshape(1, D_MODEL)
    for n in REPLICATED:
        stacked[n] = wl[n]

    loss_part, dx, early, mid, g, sent = device_step(x[0], positions[0], loss_target[0], stacked, comm=(me_idx, c_idx))
    loss = lax.psum(loss_part, ("x", "y", "c"))

    small = jnp.concatenate([_as_rows(g[n]) for n in REPLICATED], axis=0)
    small = jnp.pad(small, ((0, SMALL_ROWS - small.shape[0]), (0, 0)))
    late = packed_shards(g, MISC_LATE, LATE_ROWS, tail=small)
    late_sums = add_halves(late, ride_alone(SwapRide(late), "grad_swap_halves")[0], c_idx)
    sent.append((late_sums, ride_alone(SendRide([late_sums]), "grad_send_to_owners")[0]))
    where = jnp.concatenate([me_idx, c_idx])
    starts = (0, EARLY_ROWS, EARLY_ROWS + MID_ROWS)
    total_rows = EARLY_ROWS + MID_ROWS + LATE_ROWS
    reduced = None
    for (sums, got), off in zip(sent, starts):
        reduced = sum_owner(sums, got, where, total_rows, row_off=off, into=reduced)
    reduced = join_halves(reduced)
    quarter = reduced[starts[2] + SMALL_OFF:starts[2] + SMALL_OFF + SMALL_Q_ROWS]
    small_tot = ride_alone(GatherRide([_halves(place(quarter, me_idx, F32, "place_small_grads"))]),
                           "small_grad_all_gather")[0].reshape(SMALL_ROWS, PACK_W)

    out_g, out_d, out_m, out_v = {}, {}, {}, {}
    direct = {**EARLY_OFF, **{n: starts[1] + o for n, o in MID_OFF.items()}}
    for n, off in direct.items():
        res = adamw(reduced, off, wl[n].reshape(-1, PACK_W), ml[n].reshape(-1, PACK_W), vl[n].reshape(-1, PACK_W),
                    "adamw_" + n)
        out_g[n], out_d[n], out_m[n], out_v[n] = [a.reshape(env[n].shape) for a in res]
    for names, off in ((MISC_EARLY, MISC_EARLY_OFF), (MISC_MID, starts[1] + MISC_MID_OFF), (MISC_LATE, starts[2])):
        pack3 = lambda d: jnp.concatenate([_as_rows(d[n], MISC_SHARD_ROWS[n]) for n in names], axis=0)
        res = adamw(reduced, off, pack3(wl), pack3(ml), pack3(vl), "adamw_packed_" + names[0])
        r0 = 0
        for n in names:
            cnt = math.prod(env[n].shape)
            out_g[n], out_d[n], out_m[n], out_v[n] = [
                a[r0:r0 + MISC_SHARD_ROWS[n]].reshape(-1)[:cnt].reshape(env[n].shape) for a in res]
            r0 += MISC_SHARD_ROWS[n]
    ws = jnp.concatenate([_as_rows(wl[n]) for n in REPLICATED], axis=0)
    ms = jnp.concatenate([_as_rows(ml[n]) for n in REPLICATED], axis=0)
    vs = jnp.concatenate([_as_rows(vl[n]) for n in REPLICATED], axis=0)
    pad = ((0, SMALL_ROWS - ws.shape[0]), (0, 0))
    res = adamw(small_tot, 0, jnp.pad(ws, pad), jnp.pad(ms, pad), jnp.pad(vs, pad), "adamw_replicated")
    row = 0
    for n in REPLICATED:
        cnt = math.prod(env[n].shape)
        nrows = _rows8(env[n])
        out_g[n], out_d[n], out_m[n], out_v[n] = [a[row:row + nrows].reshape(-1)[:cnt].reshape(env[n].shape) for a in res]
        row += nrows

    return (loss, dx[None], *[out_g[n] for n in WEIGHTS], *[out_d[n] for n in WEIGHTS],
            *[out_m[n] for n in WEIGHTS], *[out_v[n] for n in WEIGHTS])
```

```python
import functools
import math

import jax
import jax.numpy as jnp
from jax import lax
from jax.experimental import pallas as pl
from jax.experimental.pallas import tpu as pltpu

F32 = jnp.float32
BF16 = jnp.bfloat16
MESH = pl.DeviceIdType.MESH

D_MODEL = 1024
DEPTH = 2
SSM_GROUP = 16
N_GROUPS = D_MODEL // SSM_GROUP
SSM_STATE = 64
N_STATES = N_GROUPS * SSM_STATE
N_HEADS = 8
QK_NOPE = 128
QK_ROPE = 64
HALF_ROPE = QK_ROPE // 2
V_HEAD = 128
QK_DIM = QK_NOPE + QK_ROPE
Q_LORA = 384
KV_LORA = 256
ROPE_THETA = 10000.0
SM_SCALE = QK_DIM ** -0.5
NEG_INF = -1e30
D_FF = 4 * D_MODEL
DN_ALPHA = (2 * DEPTH) ** 0.25
LN_EPS = 1e-5
RMS_EPS = 1e-6
ADAM_LR = 0.001
ADAM_B1 = 0.9
ADAM_B2 = 0.999
ADAM_EPS = 1e-08
ADAM_WD = 0.01
ADAM_STEP = 10

N_CHIPS = 4
LANES = 128
VMEM_LIMIT = 56 * 1024 * 1024
MM_VMEM_BUDGET = 40 * 1024 * 1024
PACK_W = 1024
KVA_PAD = 384
HALF_W = PACK_W // 2

SHARDED = ("w_ff1", "w_ff2", "ssm_w_glu", "ssm_w_out", "kv_w_a", "kv_w_b", "q_w_a", "q_w_b", "attn_w_o", "ssm_d")
G_BLOCK_ROWS = 960
EARLY_OFF = {"w_ff1": 0, "w_ff2": 2048, "attn_w_o": 4096}
MISC_EARLY = ("kv_w_b", "kv_w_a", "q_w_a", "q_w_b")
MISC_EARLY_OFF = 4352
EARLY_ROWS = 5 * G_BLOCK_ROWS
MID_OFF = {"ssm_w_out": 0}
MISC_MID = ("ssm_w_glu",)
MISC_MID_OFF = 256
MID_ROWS = G_BLOCK_ROWS
MISC_LATE = ("ssm_d",)
SMALL_Q_ROWS = 96
SMALL_ROWS = N_CHIPS * SMALL_Q_ROWS
SMALL_OFF = 16
LATE_ROWS = 128
MISC_SHARD_ROWS = {"ssm_d": 16, "ssm_w_glu": 512, "kv_w_b": 128, "kv_w_a": 80, "q_w_a": 96, "q_w_b": 144}
REPLICATED = ("ln_mix_g", "ln_mix_b", "ln_ffn_g", "ln_ffn_b", "ssm_lam_re", "ssm_lam_im", "ssm_log_dt",
              "ssm_b_re", "ssm_b_im", "ssm_c_re", "ssm_c_im", "kv_norm_g", "q_norm_g")
WEIGHTS = ("ln_mix_g", "ln_mix_b", "ln_ffn_g", "ln_ffn_b", "w_ff1", "w_ff2", "ssm_lam_re", "ssm_lam_im",
           "ssm_log_dt", "ssm_b_re", "ssm_b_im", "ssm_c_re", "ssm_c_im", "ssm_d", "ssm_w_glu", "ssm_w_out",
           "kv_w_a", "kv_norm_g", "kv_w_b", "q_w_a", "q_norm_g", "q_w_b", "attn_w_o")


def _pcall(body, **kw):
    return pl.pallas_call(body, **kw)


def _params(sem=None):
    return pltpu.CompilerParams(dimension_semantics=sem, vmem_limit_bytes=VMEM_LIMIT)


_ANY = pl.BlockSpec(memory_space=pl.ANY)


def _tile(dim, prefs):
    for p in prefs:
        if dim % p == 0:
            return p
    return dim


def _place():
    x, y, c = lax.axis_index("x"), lax.axis_index("y"), lax.axis_index("c")
    return x, y, c, [(1 - x, y), (x, 1 - y), (1 - x, 1 - y)]


def _remote(k, src, dst, to, send_sems, recv_sems):
    return pltpu.make_async_remote_copy(src_ref=src, dst_ref=dst, send_sem=send_sems.at[k], recv_sem=recv_sems.at[k],
                                        device_id=to, device_id_type=MESH)


class GatherRide:
    def __init__(self, arrs):
        self.ins = list(arrs)
        self.out_shapes = [jax.ShapeDtypeStruct(a.shape, a.dtype) for a in arrs]
        self.aliases = {i: i for i in range(len(arrs))}
        self.n_sems = 6 * len(arrs)

    def start(self, ins, outs, send_sems, recv_sems):
        x, y, c, chips = _place()
        me = 2 * x + y
        for a, o in enumerate(outs):
            for j, (px, py) in enumerate(chips):
                _remote(6 * a + j, o.at[me, c], o.at[me, c], (px, py, c), send_sems, recv_sems).start()

    def pass_on(self, ins, outs, send_sems, recv_sems):
        x, y, c, chips = _place()
        for a, o in enumerate(outs):
            for j, (px, py) in enumerate(chips):
                blk = o.at[2 * px + py, c]
                _remote(6 * a + j, blk, blk, (px, py, c), send_sems, recv_sems).wait_recv()
                _remote(6 * a + 3 + j, blk, blk, (x, y, 1 - c), send_sems, recv_sems).start()

    def finish(self, ins, outs, send_sems, recv_sems, passed_on=False):
        if not passed_on:
            self.pass_on(ins, outs, send_sems, recv_sems)
        x, y, c, chips = _place()
        me = 2 * x + y
        sibling = (x, y, 1 - c)
        for a, o in enumerate(outs):
            for j, (px, py) in enumerate(chips):
                blk = o.at[2 * px + py, 1 - c]
                _remote(6 * a + 3 + j, blk, blk, sibling, send_sems, recv_sems).wait_recv()
                _remote(6 * a + j, o.at[me, c], o.at[me, c], (px, py, c), send_sems, recv_sems).wait_send()
                mine = o.at[2 * px + py, c]
                _remote(6 * a + 3 + j, mine, mine, sibling, send_sems, recv_sems).wait_send()


class SendRide:
    def __init__(self, parts):
        self.ins = list(parts)
        self.out_shapes = [jax.ShapeDtypeStruct((3,) + p.shape[1:], p.dtype) for p in parts]
        self.aliases = {}
        self.n_sems = 3 * len(parts)

    def _copies(self, ins, outs, send_sems, recv_sems):
        x, y, c, chips = _place()
        return [_remote(3 * a + j, ins[a].at[2 * px + py], outs[a].at[j], (px, py, c), send_sems, recv_sems)
                for a in range(len(ins)) for j, (px, py) in enumerate(chips)]

    def start(self, ins, outs, send_sems, recv_sems):
        for cp in self._copies(ins, outs, send_sems, recv_sems):
            cp.start()

    def finish(self, ins, outs, send_sems, recv_sems):
        for cp in self._copies(ins, outs, send_sems, recv_sems):
            cp.wait()


class SwapRide:
    def __init__(self, pack):
        self.ins = [pack]
        self.out_shapes = [jax.ShapeDtypeStruct(pack.shape[:2] + (HALF_W,), pack.dtype)]
        self.aliases = {}
        self.n_sems = 1

    def _copy(self, ins, outs, send_sems, recv_sems):
        x, y, c, _ = _place()
        return _remote(0, ins[0].at[:, :, _my_cols(c, mine=False)], outs[0], (x, y, 1 - c), send_sems, recv_sems)

    def start(self, ins, outs, send_sems, recv_sems):
        self._copy(ins, outs, send_sems, recv_sems).start()

    def finish(self, ins, outs, send_sems, recv_sems):
        self._copy(ins, outs, send_sems, recv_sems).wait()


def _pcall_riding(body, args, ride, first, last, *, in_specs, out_specs, out_shape, scratch_shapes=(), middle=None,
                  **kw):
    n_in, n_out = len(args), len(out_shape)
    if ride is None:
        return _pcall(body, in_specs=in_specs, out_specs=out_specs, out_shape=out_shape,
                      scratch_shapes=list(scratch_shapes), **kw)(*args), []
    k_in, k_out = len(ride.ins), len(ride.out_shapes)

    def riding(*refs):
        ins, r_in = refs[:n_in], refs[n_in:n_in + k_in]
        outs = refs[n_in + k_in:n_in + k_in + n_out]
        r_out = refs[n_in + k_in + n_out:n_in + k_in + n_out + k_out]
        scratch, (send_sems, recv_sems) = refs[n_in + k_in + n_out + k_out:-2], refs[-2:]

        @pl.when(first())
        def _():
            ride.start(r_in, r_out, send_sems, recv_sems)

        if middle is not None:
            @pl.when(middle())
            def _():
                ride.pass_on(r_in, r_out, send_sems, recv_sems)

        body(*ins, *outs, *scratch)

        @pl.when(last())
        def _():
            if middle is not None:
                ride.finish(r_in, r_out, send_sems, recv_sems, passed_on=True)
            else:
                ride.finish(r_in, r_out, send_sems, recv_sems)

    res = _pcall(riding, in_specs=list(in_specs) + [_ANY] * k_in, out_specs=list(out_specs) + [_ANY] * k_out,
                 out_shape=list(out_shape) + ride.out_shapes,
                 input_output_aliases={n_in + i: n_out + o for i, o in ride.aliases.items()},
                 scratch_shapes=list(scratch_shapes) + [pltpu.SemaphoreType.DMA((ride.n_sems,))] * 2,
                 **kw)(*args, *ride.ins)
    return res[:n_out], res[n_out:]


def ride_alone(ride, name):
    def body(*refs):
        n = len(ride.ins)
        ins, outs, (send_sems, recv_sems) = refs[:n], refs[n:-2], refs[-2:]
        ride.start(ins, outs, send_sems, recv_sems)
        ride.finish(ins, outs, send_sems, recv_sems)

    return _pcall(body, name=name, in_specs=[_ANY] * len(ride.ins), out_specs=[_ANY] * len(ride.out_shapes),
                  out_shape=ride.out_shapes, input_output_aliases=dict(ride.aliases),
                  scratch_shapes=[pltpu.SemaphoreType.DMA((ride.n_sems,))] * 2)(*ride.ins)


def mm(a, b, *, name, ta=False, tb=False, pro_a=None, epi=None, extras=(), out_dtypes=(F32,), n_dim=None,
       tiles=(None, None, None), b_view=None, out_view=None, into=None, ride=None):
    if ta:
        k_dim, m_dim = a.shape
    else:
        m_dim, k_dim = a.shape
    if n_dim is None:
        n_dim = b.shape[0] if tb else b.shape[1]
    tn = tiles[1] or (n_dim if n_dim <= 1024 else _tile(n_dim, (1024, 512, 256, 128)))
    tk = tiles[2] or (k_dim if k_dim <= 1024 else _tile(k_dim, (1024, 512, 256, 128)))
    nk = k_dim // tk

    def vmem_bytes(tm):
        blocks = tm * tk * a.dtype.itemsize + tk * tn * b.dtype.itemsize
        blocks += tm * tn * (sum(e.dtype.itemsize for e in extras) + sum(jnp.dtype(d).itemsize for d in out_dtypes))
        return 2 * blocks + tm * tn * 4

    tm = tiles[0] or next((t for t in (4096, 2048, 1024, 512, 256) if m_dim % t == 0 and vmem_bytes(t) <= MM_VMEM_BUDGET),
                          _tile(m_dim, (128,)))
    assert m_dim % tm == 0 and n_dim % tn == 0 and k_dim % tk == 0, (name, m_dim, n_dim, k_dim, tm, tn, tk)
    n_ex, n_out = len(extras), len(out_dtypes)
    n_into = 0 if into is None else 1
    dims = (((0 if ta else 1,), (1 if tb else 0,)), ((), ()))

    def body(a_ref, b_ref, *rest):
        ex_refs, out_refs = rest[:n_ex], rest[n_ex + n_into:n_ex + n_into + n_out]

        def partial():
            av = a_ref[...]
            if pro_a is not None:
                av = pro_a(av)
            return lax.dot_general(av.astype(BF16), b_ref[...].astype(BF16), dims, preferred_element_type=F32)

        def finish(r):
            res = epi(r, *[e[...] for e in ex_refs]) if epi is not None else (r,)
            for o_ref, v in zip(out_refs, res):
                o_ref[...] = v.astype(o_ref.dtype)

        if nk == 1:
            finish(partial())
            return
        acc = rest[-1]
        k = pl.program_id(2)

        @pl.when(k == 0)
        def _():
            acc[...] = partial()

        @pl.when(k > 0)
        def _():
            acc[...] += partial()

        @pl.when(k == nk - 1)
        def _():
            finish(acc[...])

    def ex_spec(e):
        if e.shape == (m_dim, n_dim):
            return o_spec
        if e.shape[0] == m_dim:
            return pl.BlockSpec((tm, e.shape[1]), lambda i, j, k: (i, 0))
        return pl.BlockSpec(e.shape, lambda i, j, k: (0, 0))

    a_spec = pl.BlockSpec((tk, tm), lambda i, j, k: (k, i)) if ta else pl.BlockSpec((tm, tk), lambda i, j, k: (i, k))
    if b_view is not None:
        b_spec = b_view(tk, tn)
    else:
        b_spec = pl.BlockSpec((tn, tk), lambda i, j, k: (j, k)) if tb else pl.BlockSpec((tk, tn), lambda i, j, k: (k, j))
    o_spec = pl.BlockSpec((tm, tn), lambda i, j, k: (i, j))
    if out_view is None:
        out_specs = [o_spec] * n_out
        out_shape = [jax.ShapeDtypeStruct((m_dim, n_dim), dt) for dt in out_dtypes]
    else:
        assert n_out == 1
        out_specs = [out_view[1](tm, tn)]
        out_shape = [jax.ShapeDtypeStruct(out_view[0], out_dtypes[0])]
    grid = (m_dim // tm, n_dim // tn, nk)
    scratch = [pltpu.VMEM((tm, tn), F32)] if nk > 1 else []
    if ride is not None:
        assert into is None
        at = lambda ids: functools.reduce(jnp.logical_and, [pl.program_id(d) == i for d, i in enumerate(ids)])
        outs, landed = _pcall_riding(
            body, (a, b, *extras), ride, lambda: at((0, 0, 0)), lambda: at([g - 1 for g in grid]),
            name=name, grid=grid, in_specs=[a_spec, b_spec] + [ex_spec(e) for e in extras], out_specs=out_specs,
            out_shape=out_shape, scratch_shapes=scratch, compiler_params=_params(("arbitrary",) * 3))
        return (outs[0] if n_out == 1 else outs), landed
    outs = _pcall(
        body, name=name, grid=grid,
        in_specs=[a_spec, b_spec] + [ex_spec(e) for e in extras] + [_ANY] * n_into,
        out_specs=out_specs, out_shape=out_shape,
        input_output_aliases={2 + n_ex: 0} if n_into else {},
        scratch_shapes=scratch,
        compiler_params=_params(("parallel", "parallel", "arbitrary")),
    )(a, b, *extras, *([into] if n_into else []))
    return outs[0] if n_out == 1 else outs


def rowwise(fn, ins, outs, *, name, accs=(), tm=256):
    rows = ins[0].shape[0]
    tm = min(tm, rows)
    n_in, n_out, n_acc = len(ins), len(outs), len(accs)

    def body(*refs):
        in_refs, out_refs, acc_refs = refs[:n_in], refs[n_in:n_in + n_out], refs[n_in + n_out:]
        res, sums = fn(*[r[...] for r in in_refs])
        for o_ref, v in zip(out_refs, res):
            o_ref[...] = v.astype(o_ref.dtype)
        if n_acc:
            @pl.when(pl.program_id(0) == 0)
            def _():
                for a_ref in acc_refs:
                    a_ref[...] = jnp.zeros_like(a_ref)

            for a_ref, s in zip(acc_refs, sums):
                a_ref[...] += s

    def spec(arr):
        if arr.shape[0] == rows:
            return pl.BlockSpec((tm, arr.shape[1]), lambda i: (i, 0))
        return pl.BlockSpec(arr.shape, lambda i: (0, 0))

    res = _pcall(
        body, name=name, grid=(rows // tm,),
        in_specs=[spec(a) for a in ins],
        out_specs=[pl.BlockSpec((tm, w), lambda i: (i, 0)) for w, _ in outs]
        + [pl.BlockSpec((1, w), lambda i: (0, 0)) for w in accs],
        out_shape=[jax.ShapeDtypeStruct((rows, w), dt) for w, dt in outs]
        + [jax.ShapeDtypeStruct((1, w), F32) for w in accs],
        compiler_params=_params(("arbitrary",) if n_acc else ("parallel",)),
    )(*ins)
    return res


def _relu2(v):
    r = jnp.maximum(v, 0.0)
    return r * r


def _gelu(x):
    c = math.sqrt(2.0 / math.pi)
    return 0.5 * x * (1.0 + jnp.tanh(c * (x + 0.044715 * x * x * x)))


def _gelu_grad(x):
    c = math.sqrt(2.0 / math.pi)
    t = jnp.tanh(c * (x + 0.044715 * x * x * x))
    return 0.5 * (1.0 + t) + 0.5 * x * (1.0 - t * t) * c * (1.0 + 3 * 0.044715 * x * x)


def _sigmoid(x):
    return 1.0 / (1.0 + jnp.exp(-x))


def _layer_norm(h, mix, g, b):
    r = DN_ALPHA * h + mix
    mu = jnp.mean(r, axis=-1, keepdims=True)
    xc = r - mu
    var = jnp.mean(xc * xc, axis=-1, keepdims=True)
    return xc * lax.rsqrt(var + LN_EPS) * g + b


def ln_fwd(h, mix, g, b, name):
    def fn(h, mix, g, b):
        y = _layer_norm(h, mix, g, b)
        return (y, y), ()
    return rowwise(fn, (h, mix, g, b), ((D_MODEL, F32), (D_MODEL, BF16)), name=name)


def ln_bwd(h, mix, g, dy, name):
    def fn(h, mix, g, dy):
        r = DN_ALPHA * h + mix
        mu = jnp.mean(r, axis=-1, keepdims=True)
        xc = r - mu
        var = jnp.mean(xc * xc, axis=-1, keepdims=True)
        rstd = lax.rsqrt(var + LN_EPS)
        xhat = xc * rstd
        dxh = dy * g
        m1 = jnp.mean(dxh, axis=-1, keepdims=True)
        m2 = jnp.mean(dxh * xhat, axis=-1, keepdims=True)
        dr = rstd * (dxh - m1 - xhat * m2)
        return (dr, dr), (jnp.sum(dy * xhat, axis=0, keepdims=True), jnp.sum(dy, axis=0, keepdims=True))
    return rowwise(fn, (h, mix, g, dy), ((D_MODEL, F32), (D_MODEL, BF16)), accs=(D_MODEL, D_MODEL), name=name)


def _rms(x, g):
    r = lax.rsqrt(jnp.mean(x * x, axis=-1, keepdims=True) + RMS_EPS)
    return x * r * g


def _rms_bwd(x, g, dy):
    r = lax.rsqrt(jnp.mean(x * x, axis=-1, keepdims=True) + RMS_EPS)
    xn = x * r
    dyg = dy * g
    dx = r * (dyg - xn * jnp.mean(dyg * xn, axis=-1, keepdims=True))
    return dx, jnp.sum(dy * xn, axis=0, keepdims=True)


def _s5_disc(lr, li, ldt):
    dt = jnp.exp(ldt)
    mag = jnp.exp(lr * dt)
    cs, sn = jnp.cos(li * dt), jnp.sin(li * dt)
    ar, ai = mag * cs, mag * sn
    inv = 1.0 / (lr * lr + li * li)
    n_re = (ar - 1.0) * lr + ai * li
    n_im = ai * lr - (ar - 1.0) * li
    return dt, mag, cs, sn, ar, ai, inv, n_re, n_im


def s5_prep(lr, li, ldt, b_re, b_im):
    def fn(lr, li, ldt, b_re, b_im):
        _, _, _, _, ar, ai, inv, n_re, n_im = _s5_disc(lr, li, ldt)
        cr, ci = n_re * inv, n_im * inv
        return (ar, ai, cr * b_re - ci * b_im, cr * b_im + ci * b_re), ()
    return rowwise(fn, (lr, li, ldt, b_re, b_im), ((1, F32), (1, F32), (SSM_GROUP, F32), (SSM_GROUP, F32)),
                   name="s5_prep", tm=512)


def s5_prep_bwd(lr, li, ldt, b_re, b_im, dar, dai, dbb_re, dbb_im):
    def fn(lr, li, ldt, b_re, b_im, dar, dai, dbb_re, dbb_im):
        dt, mag, cs, sn, ar, ai, inv, n_re, n_im = _s5_disc(lr, li, ldt)
        cr, ci = n_re * inv, n_im * inv
        db_re = cr * dbb_re + ci * dbb_im
        db_im = cr * dbb_im - ci * dbb_re
        dcr = jnp.sum(dbb_re * b_re + dbb_im * b_im, axis=-1, keepdims=True)
        dci = jnp.sum(dbb_im * b_re - dbb_re * b_im, axis=-1, keepdims=True)
        dar = dar + (dcr * lr - dci * li) * inv
        dai = dai + (dcr * li + dci * lr) * inv
        dinv = dcr * n_re + dci * n_im
        dlr = (dcr * (ar - 1.0) + dci * ai) * inv - 2.0 * lr * inv * inv * dinv
        dli = (dcr * ai - dci * (ar - 1.0)) * inv - 2.0 * li * inv * inv * dinv
        dmag = dar * cs + dai * sn
        dth = dai * ar - dar * ai
        dlr = dlr + dmag * mag * dt
        dli = dli + dth * dt
        ddt = dmag * mag * lr + dth * li
        return (dlr, dli, ddt * dt, db_re, db_im), ()
    return rowwise(fn, (lr, li, ldt, b_re, b_im, dar, dai, dbb_re, dbb_im),
                   ((1, F32), (1, F32), (1, F32), (SSM_GROUP, F32), (SSM_GROUP, F32)), name="s5_prep_bwd", tm=512)


def group_sum(x):
    def body(x_ref, o_ref):
        o_ref[...] = jnp.sum(x_ref[...], axis=1)
    return _pcall(body, name="s5_group_sum", out_shape=jax.ShapeDtypeStruct((N_GROUPS, 1), F32))(
        x.reshape(N_GROUPS, SSM_STATE, 1))


GROUPS_PER_TILE = LANES // SSM_GROUP
TILE_STATES = GROUPS_PER_TILE * SSM_STATE
N_UTILES = D_MODEL // LANES
TILES_PER_UTILE = TILE_STATES // LANES


SUBLANES = 8
SCAN_STRIP = 1024
N_STRIPS = N_STATES // SCAN_STRIP
_NT = (((1,), (1,)), ((), ()))
_TN = (((0,), (0,)), ((), ()))


def _scan_coefs(are, aim, shifted, reverse):
    ar = are[...]
    ai = -aim[...] if reverse else aim[...]
    powers = {1: (ar, ai)}
    for d in (2, 4):
        r, i = powers[d // 2]
        powers[d] = (r * r - i * i, 2.0 * r * i)
    rid = lax.broadcasted_iota(jnp.int32, (SUBLANES, N_STATES), 0)
    first = (rid == SUBLANES - 1) if reverse else (rid == 0)
    masks = [(1, first)] + [(d, (rid <= SUBLANES - 1 - d) if reverse else (rid >= d)) for d in (1, 2, 4)]
    for n, (d, keep) in enumerate(masks):
        for part in (0, 1):
            shifted[2 * n + part][...] = jnp.where(keep, jnp.broadcast_to(powers[d][part], (SUBLANES, N_STATES)), 0.0)


def _tile_scan(xr, xi, shifted, nbr_re, nbr_im, reverse):
    for n, d in enumerate((1, 1, 2, 4)):
        by = SUBLANES - d if reverse else d
        fr, fi = (nbr_re, nbr_im) if n == 0 else (xr, xi)
        sr, si = pltpu.roll(fr, by, 0), pltpu.roll(fi, by, 0)
        kr, ki = shifted[2 * n], shifted[2 * n + 1]
        xr, xi = xr + kr * sr - ki * si, xi + kr * si + ki * sr
    return xr, xi


def _tile_rows(t):
    return pl.ds(pl.multiple_of(t * SUBLANES, SUBLANES), SUBLANES)


def s5_fwd(u, bbd_re, bbd_im, cbd_re, cbd_imn, a_re, a_im, dskip, ride=None, t_rows=256):
    seq = u.shape[0]
    t_rows = min(t_rows, seq)
    n_tiles = t_rows // SUBLANES

    def body(u_ref, bre, bim, cre, cimn, are, aim, d_ref, y_ref, gelu_ref, hre_ref, him_ref, car_re, car_im, *shifted):
        @pl.when(pl.program_id(0) == 0)
        def _():
            car_re[...] = jnp.zeros_like(car_re)
            car_im[...] = jnp.zeros_like(car_im)
            _scan_coefs(are, aim, shifted, reverse=False)

        uf = u_ref[...]
        ub = uf.astype(BF16)
        for j in range(N_UTILES):
            uj = ub[:, LANES * j:LANES * (j + 1)]
            sl = slice(TILE_STATES * j, TILE_STATES * (j + 1))
            hre_ref[:, sl] = jnp.dot(uj, bre[j], preferred_element_type=F32)
            him_ref[:, sl] = jnp.dot(uj, bim[j], preferred_element_type=F32)
        for s in range(N_STRIPS):
            cols = pl.ds(s * SCAN_STRIP, SCAN_STRIP)
            coefs = [c[:, cols] for c in shifted]

            def step(t, before):
                rows = _tile_rows(t)
                hr, hi = _tile_scan(hre_ref[rows, cols], him_ref[rows, cols], coefs, before[0], before[1], False)
                hre_ref[rows, cols] = hr
                him_ref[rows, cols] = hi
                return hr, hi

            cr, ci = lax.fori_loop(0, n_tiles, step, (car_re[:, cols], car_im[:, cols]))
            car_re[:, cols] = cr
            car_im[:, cols] = ci
        dv = d_ref[...]
        for j in range(N_UTILES):
            st = slice(TILE_STATES * j, TILE_STATES * (j + 1))
            yj = (jnp.dot(hre_ref[:, st].astype(BF16), cre[j], preferred_element_type=F32)
                  + jnp.dot(him_ref[:, st].astype(BF16), cimn[j], preferred_element_type=F32))
            sl = slice(LANES * j, LANES * (j + 1))
            yj = yj + dv[:, sl] * uf[:, sl]
            y_ref[:, sl] = yj
            gelu_ref[:, sl] = _gelu(yj).astype(gelu_ref.dtype)

    full3 = lambda a: pl.BlockSpec(a.shape, lambda i: (0, 0, 0))
    full2 = lambda a: pl.BlockSpec(a.shape, lambda i: (0, 0))
    tile = pltpu.VMEM((SUBLANES, N_STATES), F32)
    n_chunks = seq // t_rows
    return _pcall_riding(
        body, (u, bbd_re, bbd_im, cbd_re, cbd_imn, a_re, a_im, dskip), ride,
        lambda: pl.program_id(0) == 0, lambda: pl.program_id(0) == n_chunks - 1,
        middle=(lambda: pl.program_id(0) == (7 * n_chunks) // 8) if ride is not None else None,
        name="s5_fwd", grid=(n_chunks,),
        in_specs=[pl.BlockSpec((t_rows, D_MODEL), lambda i: (i, 0)), full3(bbd_re), full3(bbd_im), full3(cbd_re),
                  full3(cbd_imn), full2(a_re), full2(a_im), full2(dskip)],
        out_specs=[pl.BlockSpec((t_rows, D_MODEL), lambda i: (i, 0)),
                   pl.BlockSpec((t_rows, D_MODEL), lambda i: (i, 0)),
                   pl.BlockSpec((t_rows, N_STATES), lambda i: (i, 0)),
                   pl.BlockSpec((t_rows, N_STATES), lambda i: (i, 0))],
        out_shape=[jax.ShapeDtypeStruct((seq, D_MODEL), F32),
                   jax.ShapeDtypeStruct((seq, D_MODEL), BF16),
                   jax.ShapeDtypeStruct((seq, N_STATES), F32),
                   jax.ShapeDtypeStruct((seq, N_STATES), F32)],
        scratch_shapes=[tile] * 10,
        compiler_params=_params(("arbitrary",)))


def s5_bwd(dy, u, dres, h_re, h_im, bbd_re, bbd_im, cbd_re, cbd_imn, a_re, a_im, dskip, ride=None, t_rows=128):
    seq = u.shape[0]
    t_rows = min(t_rows, seq)
    n_chunks = seq // t_rows

    n_tiles = t_rows // SUBLANES

    def body(dy_ref, u_ref, dres_ref, hre_ref, him_ref, hpre_ref, hpim_ref, bre, bim, cre, cimn, are, aim, d_ref,
             dx_ref, dbre, dbim, dcre, dcimn, dar_ref, dai_ref, dd_ref, lre, lim, car_re, car_im, acc_re, acc_im,
             *shifted):
        i = pl.program_id(0)

        @pl.when(i == 0)
        def _():
            for r in (car_re, car_im, acc_re, acc_im, dbre, dbim, dcre, dcimn, dd_ref):
                r[...] = jnp.zeros_like(r)
            _scan_coefs(are, aim, shifted, reverse=True)

        dyf = dy_ref[...]
        dyb = dyf.astype(BF16)
        uf = u_ref[...]
        ub = uf.astype(BF16)
        for j in range(N_UTILES):
            dyj = dyb[:, LANES * j:LANES * (j + 1)]
            st = slice(TILE_STATES * j, TILE_STATES * (j + 1))
            lre[:, st] = lax.dot_general(dyj, cre[j], _NT, preferred_element_type=F32)
            lim[:, st] = lax.dot_general(dyj, cimn[j], _NT, preferred_element_type=F32)
        has_pred = (i < n_chunks - 1).astype(F32)
        last_row = lax.broadcasted_iota(jnp.int32, (SUBLANES, SCAN_STRIP), 0) == SUBLANES - 1
        for s in range(N_STRIPS):
            cols = pl.ds(s * SCAN_STRIP, SCAN_STRIP)
            coefs = [c[:, cols] for c in shifted]
            before_re, before_im = hpre_ref[:, cols] * has_pred, hpim_ref[:, cols] * has_pred

            def step(k, carry):
                after_re, after_im, dar, dai = carry
                t = n_tiles - 1 - k
                rows = _tile_rows(t)
                lr, li = _tile_scan(lre[rows, cols], lim[rows, cols], coefs, after_re, after_im, True)
                lre[rows, cols] = lr
                lim[rows, cols] = li
                prev = _tile_rows(jnp.maximum(t - 1, 0))
                pre_re = jnp.where(t == 0, before_re, hre_ref[prev, cols])
                pre_im = jnp.where(t == 0, before_im, him_ref[prev, cols])
                hpr = pltpu.roll(jnp.where(last_row, pre_re, hre_ref[rows, cols]), 1, 0)
                hpi = pltpu.roll(jnp.where(last_row, pre_im, him_ref[rows, cols]), 1, 0)
                return lr, li, dar + lr * hpr + li * hpi, dai + li * hpr - lr * hpi

            cr, ci, dar, dai = lax.fori_loop(0, n_tiles, step, (car_re[:, cols], car_im[:, cols],
                                                               acc_re[:, cols], acc_im[:, cols]))
            car_re[:, cols] = cr
            car_im[:, cols] = ci
            acc_re[:, cols] = dar
            acc_im[:, cols] = dai

        dv = d_ref[...]
        for j in range(N_UTILES):
            sl = slice(LANES * j, LANES * (j + 1))
            st = slice(TILE_STATES * j, TILE_STATES * (j + 1))
            lrj = lre[:, st].astype(BF16)
            lij = lim[:, st].astype(BF16)
            du = (lax.dot_general(lrj, bre[j], _NT, preferred_element_type=F32)
                  + lax.dot_general(lij, bim[j], _NT, preferred_element_type=F32))
            dx_ref[:, sl] = du + dv[:, sl] * dyf[:, sl] + DN_ALPHA * dres_ref[:, sl]
            uj = ub[:, sl]
            dbre[j] += lax.dot_general(uj, lrj, _TN, preferred_element_type=F32)
            dbim[j] += lax.dot_general(uj, lij, _TN, preferred_element_type=F32)
            dyj = dyb[:, sl]
            dcre[j] += lax.dot_general(hre_ref[:, st].astype(BF16), dyj, _TN, preferred_element_type=F32)
            dcimn[j] += lax.dot_general(him_ref[:, st].astype(BF16), dyj, _TN, preferred_element_type=F32)
        dd_ref[...] += jnp.sum(dyf * uf, axis=0, keepdims=True)

        @pl.when(i == n_chunks - 1)
        def _():
            dar_ref[...] = jnp.sum(acc_re[...], axis=0, keepdims=True)
            dai_ref[...] = jnp.sum(acc_im[...], axis=0, keepdims=True)

    rev = lambda i: (n_chunks - 1 - i, 0)
    prev_tile = lambda i: (jnp.maximum((n_chunks - 1 - i) * n_tiles - 1, 0), 0)
    full3 = lambda a: pl.BlockSpec(a.shape, lambda i: (0, 0, 0))
    full2 = lambda a: pl.BlockSpec(a.shape, lambda i: (0, 0))
    acc3 = lambda shape: pl.BlockSpec(shape, lambda i: (0, 0, 0))
    acc2 = lambda shape: pl.BlockSpec(shape, lambda i: (0, 0))
    tile = pltpu.VMEM((SUBLANES, N_STATES), F32)
    return _pcall_riding(
        body, (dy, u, dres, h_re, h_im, h_re, h_im, bbd_re, bbd_im, cbd_re, cbd_imn, a_re, a_im, dskip), ride,
        lambda: pl.program_id(0) == 0, lambda: pl.program_id(0) == n_chunks - 1,
        name="s5_bwd", grid=(n_chunks,),
        in_specs=[pl.BlockSpec((t_rows, D_MODEL), rev), pl.BlockSpec((t_rows, D_MODEL), rev),
                  pl.BlockSpec((t_rows, D_MODEL), rev),
                  pl.BlockSpec((t_rows, N_STATES), rev), pl.BlockSpec((t_rows, N_STATES), rev),
                  pl.BlockSpec((SUBLANES, N_STATES), prev_tile), pl.BlockSpec((SUBLANES, N_STATES), prev_tile),
                  full3(bbd_re), full3(bbd_im), full3(cbd_re), full3(cbd_imn), full2(a_re), full2(a_im), full2(dskip)],
        out_specs=[pl.BlockSpec((t_rows, D_MODEL), rev), acc3(bbd_re.shape), acc3(bbd_im.shape), acc3(cbd_re.shape),
                   acc3(cbd_imn.shape), acc2((1, N_STATES)), acc2((1, N_STATES)), acc2((1, D_MODEL))],
        out_shape=[jax.ShapeDtypeStruct((seq, D_MODEL), F32), jax.ShapeDtypeStruct(bbd_re.shape, F32),
                   jax.ShapeDtypeStruct(bbd_im.shape, F32), jax.ShapeDtypeStruct(cbd_re.shape, F32),
                   jax.ShapeDtypeStruct(cbd_imn.shape, F32), jax.ShapeDtypeStruct((1, N_STATES), F32),
                   jax.ShapeDtypeStruct((1, N_STATES), F32), jax.ShapeDtypeStruct((1, D_MODEL), F32)],
        scratch_shapes=[pltpu.VMEM((t_rows, N_STATES), F32), pltpu.VMEM((t_rows, N_STATES), F32)] + [tile] * 12,
        compiler_params=_params(("arbitrary",)))


def _eye_groups():
    return jnp.eye(GROUPS_PER_TILE, dtype=F32)


def _blockdiag_in(bb):
    t = bb.transpose(0, 2, 1).reshape(N_UTILES, GROUPS_PER_TILE, SSM_GROUP, SSM_STATE)
    bd = jnp.einsum("jgcp,gh->jgchp", t, _eye_groups())
    return bd.reshape(N_UTILES, LANES, TILE_STATES)


def _blockdiag_in_t(d):
    t = jnp.einsum("jgchp,gh->jgcp", d.reshape(N_UTILES, GROUPS_PER_TILE, SSM_GROUP, GROUPS_PER_TILE, SSM_STATE),
                   _eye_groups())
    return t.reshape(N_GROUPS, SSM_GROUP, SSM_STATE).transpose(0, 2, 1)


def _blockdiag_out(c):
    t = c.transpose(0, 2, 1).reshape(N_UTILES, GROUPS_PER_TILE, SSM_STATE, SSM_GROUP)
    bd = jnp.einsum("jhpc,hg->jhpgc", t, _eye_groups())
    return bd.reshape(N_UTILES, TILE_STATES, LANES)


def _blockdiag_out_t(d):
    t = jnp.einsum("jhpgc,hg->jhpc", d.reshape(N_UTILES, GROUPS_PER_TILE, SSM_STATE, GROUPS_PER_TILE, SSM_GROUP),
                   _eye_groups())
    return t.reshape(N_GROUPS, SSM_STATE, SSM_GROUP).transpose(0, 2, 1)


ATT_TQ = 512
ATT_TK = 512
LOG2E = math.log2(math.e)
LN2 = math.log(2.0)
Q_PRESCALE = SM_SCALE * LOG2E


def _loop_in_pairs(n, step, carry, start=0):
    pairs = (n - start) // 2

    def two(t, c):
        return step(start + 2 * t + 1, step(start + 2 * t, c))

    carry = lax.fori_loop(0, pairs, two, carry)
    return lax.fori_loop(start + 2 * pairs, n, step, carry)


def _causal(s, off=0, transposed=False):
    r = lax.broadcasted_iota(jnp.int32, s.shape, 0)
    c = lax.broadcasted_iota(jnp.int32, s.shape, 1)
    keep = (r <= c + off) if transposed else (c <= r + off)
    return jnp.where(keep, s, NEG_INF)


def _q_specs(rows, at):
    def nope(*ids):
        r, h = at(*ids)
        return r, 3 * (h // HEADS_PER_CHIP) + h % HEADS_PER_CHIP

    def rope(*ids):
        r, h = at(*ids)
        return r, 3 * (h // HEADS_PER_CHIP) + HEADS_PER_CHIP

    return [pl.BlockSpec((rows, LANES), nope), pl.BlockSpec((rows, LANES), rope)]


def _kv_specs(rows, at):
    def col(f):
        def index(*ids):
            r, h = at(*ids)
            return r, f(h)
        return index

    return [pl.BlockSpec((rows, LANES), col(lambda h: 2 * h)), pl.BlockSpec((rows, LANES), col(lambda h: h % HEADS_PER_CHIP)),
            pl.BlockSpec((rows, LANES), col(lambda h: 2 * h + 1))]


def _cat(a, b):
    return jnp.concatenate([a, b], axis=1)


def attn_fwd(q, kv, kr, ride=None, tq=ATT_TQ, tk=ATT_TK):
    seq = q.shape[0]
    n_heads = N_HEADS
    tq, tk = min(tq, seq), min(tk, seq)

    def body(qn_ref, qr_ref, kn_ref, kr_ref, v_ref, o_ref, lse_ref):
        qi = pl.program_id(1)
        qv = _cat(qn_ref[...], qr_ref[...])
        jd = (qi * tq) // tk

        def block(j, carry, diag):
            m, l, acc = carry
            rows = pl.ds(pl.multiple_of(j * tk, tk), tk)
            s = lax.dot_general(qv, _cat(kn_ref[rows, :], kr_ref[rows, :]), _NT, preferred_element_type=F32)
            if diag:
                s = _causal(s, qi * tq - jd * tk)
            m_new = jnp.maximum(m, jnp.max(s, axis=-1, keepdims=True))
            p = jnp.exp2(s - m_new)
            corr = jnp.exp2(m - m_new)
            l = l * corr + jnp.sum(p, axis=-1, keepdims=True)
            acc = acc * corr + jnp.dot(p.astype(BF16), v_ref[rows, :], preferred_element_type=F32)
            return m_new, l, acc

        init = (jnp.full((tq, 1), NEG_INF, F32), jnp.zeros((tq, 1), F32), jnp.zeros((tq, V_HEAD), F32))
        carry = _loop_in_pairs(jd, lambda j, c: block(j, c, False), init)
        m, l, acc = block(jd, carry, True)
        o_ref[...] = acc / l
        lse_ref[0] = jnp.broadcast_to(m + jnp.log2(l), (tq, LANES))

    n_q = seq // tq
    return _pcall_riding(
        body, (q, q, kv, kr, kv), ride,
        lambda: (pl.program_id(0) == 0) & (pl.program_id(1) == 0),
        lambda: (pl.program_id(0) == n_heads - 1) & (pl.program_id(1) == n_q - 1),
        middle=(lambda: (pl.program_id(0) == (5 * n_heads) // 8) & (pl.program_id(1) == 0)) if ride is not None else None,
        name="attn_fwd", grid=(n_heads, n_q),
        in_specs=_q_specs(tq, lambda h, i: (i, h)) + _kv_specs(seq, lambda h, i: (0, h)),
        out_specs=[pl.BlockSpec((tq, V_HEAD), lambda h, i: (i, h)),
                   pl.BlockSpec((1, tq, LANES), lambda h, i: (h, i, 0))],
        out_shape=[jax.ShapeDtypeStruct((seq, n_heads * V_HEAD), F32),
                   jax.ShapeDtypeStruct((n_heads, seq, LANES), F32)],
        compiler_params=_params(("arbitrary", "arbitrary")))


def attn_bwd(q, kv, kr, do, lse_row, delta_row, tq=ATT_TK):
    seq = q.shape[0]
    tq = min(tq, seq)
    n_blk = seq // tq

    def body(qn_ref, qr_ref, kn_ref, kr_ref, v_ref, do_ref, lse_ref, delta_ref, dqn_ref, dqr_ref, dkv_ref, dkr_ref, dq_acc):
        head, kj = pl.program_id(0), pl.program_id(1)

        @pl.when(kj == 0)
        def _():
            dq_acc[...] = jnp.zeros_like(dq_acc)

        kc = _cat(kn_ref[...], kr_ref[...])
        vv = v_ref[...]

        def block(i, carry, diag):
            dk, dv = carry
            rows = pl.ds(pl.multiple_of(i * tq, tq), tq)
            qv = _cat(qn_ref[rows, :], qr_ref[rows, :])
            st = lax.dot_general(kc, qv, _NT, preferred_element_type=F32)
            if diag:
                st = _causal(st, transposed=True)
            pt = jnp.exp2(st - lse_ref[0, pl.ds(i, 1), :])
            dob = do_ref[rows, :].astype(BF16)
            dv = dv + jnp.dot(pt.astype(BF16), dob, preferred_element_type=F32)
            dpt = lax.dot_general(vv, dob, _NT, preferred_element_type=F32)
            dst = (pt * (dpt - delta_ref[0, pl.ds(i, 1), :])).astype(BF16)
            dk = dk + jnp.dot(dst, qv, preferred_element_type=F32)
            dq_acc[rows, :] += lax.dot_general(dst, kc, _TN, preferred_element_type=F32)
            return dk, dv

        carry = block(kj, (jnp.zeros((tq, 2 * LANES), F32), jnp.zeros((tq, V_HEAD), F32)), True)
        dk, dv = _loop_in_pairs(n_blk, lambda i, c: block(i, c, False), carry, start=kj + 1)
        dk = dk * LN2
        dkv_ref[...] = _cat(dk[:, :LANES], dv).astype(dkv_ref.dtype)
        lane = lax.broadcasted_iota(jnp.int32, (tq, LANES), 1)
        mine = (lane // HALF_ROPE) % HEADS_PER_CHIP == head % HEADS_PER_CHIP
        dkr_ref[0] = jnp.where(mine, dk[:, LANES:], 0.0)

        @pl.when(kj == n_blk - 1)
        def _():
            dqn_ref[...] = dq_acc[:, :LANES] * SM_SCALE

        @pl.when((kj == n_blk - 1) & (head % HEADS_PER_CHIP == 0))
        def _():
            dqr_ref[...] = dq_acc[:, LANES:] * SM_SCALE

        @pl.when((kj == n_blk - 1) & (head % HEADS_PER_CHIP > 0))
        def _():
            dqr_ref[...] += dq_acc[:, LANES:] * SM_SCALE

    return _pcall(
        body, name="attn_bwd", grid=(N_HEADS, n_blk),
        in_specs=_q_specs(seq, lambda h, j: (0, h)) + _kv_specs(tq, lambda h, j: (j, h))
        + [pl.BlockSpec((seq, V_HEAD), lambda h, j: (0, h)),
           pl.BlockSpec((1, n_blk, tq), lambda h, j: (h, 0, 0)),
           pl.BlockSpec((1, n_blk, tq), lambda h, j: (h, 0, 0))],
        out_specs=[pl.BlockSpec((seq, LANES), lambda h, j: (0, h)),
                   pl.BlockSpec((seq, LANES), lambda h, j: (0, h // HEADS_PER_CHIP)),
                   pl.BlockSpec((tq, QK_NOPE + V_HEAD), lambda h, j: (j, h)),
                   pl.BlockSpec((1, tq, LANES), lambda h, j: (h, j, 0))],
        out_shape=[jax.ShapeDtypeStruct((seq, N_HEADS * QK_NOPE), F32),
                   jax.ShapeDtypeStruct((seq, N_CHIPS * LANES), F32),
                   jax.ShapeDtypeStruct((seq, N_HEADS * (QK_NOPE + V_HEAD)), BF16),
                   jax.ShapeDtypeStruct((N_HEADS, seq, LANES), F32)],
        scratch_shapes=[pltpu.VMEM((seq, 2 * LANES), F32)],
        compiler_params=_params(("arbitrary", "arbitrary")),
    )(q, q, kv, kr, kv, do, lse_row, delta_row)


def head_sum(x, ts=512):
    n_heads, seq, w = x.shape
    ts = min(ts, seq)

    def body(x_ref, o_ref):
        o_ref[...] = jnp.sum(x_ref[...], axis=0)

    return _pcall(body, name="head_sum", grid=(seq // ts,),
                  in_specs=[pl.BlockSpec((n_heads, ts, w), lambda i: (0, i, 0))],
                  out_specs=pl.BlockSpec((ts, w), lambda i: (i, 0)),
                  out_shape=jax.ShapeDtypeStruct((seq, w), F32),
                  compiler_params=_params(("parallel",)))(x)


HEADS_PER_CHIP = N_HEADS // N_CHIPS
Q_CHIP = HEADS_PER_CHIP * QK_DIM
Q_CHIP_NOPE = HEADS_PER_CHIP * QK_NOPE


def _perm_q_cols(w):
    t = w.reshape(w.shape[0], HEADS_PER_CHIP, QK_DIM)
    return jnp.concatenate([t[:, :, :QK_NOPE].reshape(w.shape[0], -1),
                            t[:, :, QK_NOPE:QK_NOPE + HALF_ROPE].reshape(w.shape[0], -1),
                            t[:, :, QK_NOPE + HALF_ROPE:].reshape(w.shape[0], -1)], axis=1)


def _unperm_q_cols(w):
    r = w.shape[0]
    nope = w[:, :Q_CHIP_NOPE].reshape(r, HEADS_PER_CHIP, QK_NOPE)
    r1 = w[:, Q_CHIP_NOPE:Q_CHIP_NOPE + QK_ROPE].reshape(r, HEADS_PER_CHIP, HALF_ROPE)
    r2 = w[:, Q_CHIP_NOPE + QK_ROPE:].reshape(r, HEADS_PER_CHIP, HALF_ROPE)
    return jnp.concatenate([nope, r1, r2], axis=2).reshape(r, Q_CHIP)


def _pad_kva_cols(w):
    z = jnp.zeros((w.shape[0], HALF_ROPE), w.dtype)
    return jnp.concatenate([w[:, :KV_LORA], w[:, KV_LORA:KV_LORA + HALF_ROPE], z, w[:, KV_LORA + HALF_ROPE:], z], axis=1)


def _unpad_kva_cols(w):
    return jnp.concatenate([w[:, :KV_LORA], w[:, KV_LORA:KV_LORA + HALF_ROPE],
                            w[:, KV_LORA + QK_ROPE:KV_LORA + QK_ROPE + HALF_ROPE]], axis=1)


def _rope_tile(t, cs, sn):
    return t * cs + pltpu.roll(t, LANES // 2, 1) * sn


def _rope_tile_bwd(d, cs, sn):
    return d * cs + pltpu.roll(d * sn, LANES // 2, 1)


def _b_cols(tk, tn):
    return pl.BlockSpec((None, tk, tn), lambda i, j, k: (j, k, 0))


def _b_cols_t(tk, tn):
    return pl.BlockSpec((None, tn, tk), lambda i, j, k: (k, j, 0))


def _out_cols(shape):
    return shape, lambda tm, tn: pl.BlockSpec((None, tm, tn), lambda i, j, k: (j, i, 0))


def _halves(a):
    return a.reshape(N_CHIPS, 2, a.shape[1] // 2, a.shape[2])


def device_step(x, positions, target, w, comm=None):
    seq = x.shape[0]
    w = dict(w)

    def gathered(names, outs):
        for n, a in zip(names, outs):
            if isinstance(n, tuple):
                w[n[0]] = [a.reshape(v.shape) if l == n[1] else v for l, v in enumerate(w[n[0]])]
            else:
                w[n] = a.reshape(w[n].shape)

    def ride_for(names):
        if comm is None:
            return None
        return GatherRide([_halves(w[n[0]][n[1]] if isinstance(n, tuple) else w[n]) for n in names])

    first_ride = ("ssm_w_glu", "ssm_w_out", ("w_ff1", 0), ("w_ff2", 0))
    mla_ride = ("kv_w_a", "kv_w_b", "q_w_a", "q_w_b", "attn_w_o")
    second_ride = (("w_ff1", 1), ("w_ff2", 1))

    inv_freq = ROPE_THETA ** (-jnp.arange(HALF_ROPE, dtype=F32) / HALF_ROPE)
    ang = positions.astype(F32)[:, None] * inv_freq
    cos, sin = jnp.cos(ang), jnp.sin(ang)
    zero = jnp.zeros_like(cos)
    cos_q, sin_q = jnp.concatenate([cos] * 4, 1), jnp.concatenate([-sin, -sin, sin, sin], 1)
    cos_k, sin_k = jnp.concatenate([cos, zero, cos, zero], 1), jnp.concatenate([-sin, zero, sin, zero], 1)
    ff_tile = D_FF // N_CHIPS
    pack_shape = (N_CHIPS, EARLY_ROWS, PACK_W)

    lr = w["ssm_lam_re"].reshape(N_STATES, 1)
    li = w["ssm_lam_im"].reshape(N_STATES, 1)
    ldt = jnp.repeat(w["ssm_log_dt"].reshape(N_GROUPS), SSM_STATE).reshape(N_STATES, 1)
    b_re = w["ssm_b_re"].reshape(N_STATES, SSM_GROUP)
    b_im = w["ssm_b_im"].reshape(N_STATES, SSM_GROUP)
    a_re, a_im, bb_re, bb_im = s5_prep(lr, li, ldt, b_re, b_im)
    a_re, a_im = a_re.reshape(1, N_STATES), a_im.reshape(1, N_STATES)
    bbd_re = _blockdiag_in(bb_re.reshape(N_GROUPS, SSM_STATE, SSM_GROUP)).astype(BF16)
    bbd_im = _blockdiag_in(bb_im.reshape(N_GROUPS, SSM_STATE, SSM_GROUP)).astype(BF16)
    cbd_re = _blockdiag_out(w["ssm_c_re"].reshape(N_GROUPS, SSM_GROUP, SSM_STATE)).astype(BF16)
    cbd_imn = _blockdiag_out(-w["ssm_c_im"].reshape(N_GROUPS, SSM_GROUP, SSM_STATE)).astype(BF16)
    dskip = w["ssm_d"].reshape(1, D_MODEL)
    (ypre, yg, h_re, h_im), landed = s5_fwd(x, bbd_re, bbd_im, cbd_re, cbd_imn, a_re, a_im, dskip, ride_for(first_ride))
    gathered(first_ride, landed)
    w_glu = w["ssm_w_glu"]
    glu_tile = w_glu.shape[2]
    vg = mm(yg, w_glu, n_dim=2 * D_MODEL, tiles=(None, glu_tile, None), b_view=_b_cols, name="glu_proj")

    def glu(v):
        return (v[:, :D_MODEL] * _sigmoid(v[:, D_MODEL:]),), ()
    (z,) = rowwise(glu, (vg,), ((D_MODEL, BF16),), name="glu")
    w_out = w["ssm_w_out"].reshape(D_MODEL, D_MODEL)
    mix0 = mm(z, w_out, name="ssm_out")

    def mlp_fwd(hb, layer, riding=None):
        pre = mm(hb, w["w_ff1"][layer], n_dim=D_FF, tiles=(None, ff_tile, None), b_view=_b_cols, name=f"ff1_{layer}",
                 out_dtypes=(BF16,), ride=ride_for(riding) if riding else None)
        if riding and comm is not None:
            pre, landed = pre
            gathered(riding, landed)
        f = mm(pre, w["w_ff2"][layer].reshape(D_FF, D_MODEL), pro_a=_relu2, name=f"ff2_{layer}")
        return pre, f

    ln = lambda name, l: w[name][l].reshape(1, D_MODEL)
    h1, h1b = ln_fwd(x, mix0, ln("ln_mix_g", 0), ln("ln_mix_b", 0), "ln_mix_0")
    f1pre, f1 = mlp_fwd(h1b, 0, mla_ride)
    h2, h2b = ln_fwd(h1, f1, ln("ln_ffn_g", 0), ln("ln_ffn_b", 0), "ln_ffn_0")

    kv_w_a = w["kv_w_a"].reshape(D_MODEL, KVA_PAD)
    kv_w_b = w["kv_w_b"]
    q_w_a = w["q_w_a"].reshape(D_MODEL, Q_LORA)
    q_w_b = w["q_w_b"]
    w_o = w["attn_w_o"].reshape(D_MODEL, D_MODEL)
    kvb_tile = kv_w_b.shape[2]
    kvn_g = w["kv_norm_g"].reshape(1, KV_LORA)
    qn_g = w["q_norm_g"].reshape(1, Q_LORA)
    kva = mm(h2b, kv_w_a, name="kv_a")

    def kv_post(kva, g, cs, sn):
        tile = _rope_tile(kva[:, KV_LORA:], cs, sn)
        return (_rms(kva[:, :KV_LORA], g), _cat(tile, pltpu.roll(tile, HALF_ROPE, 1))), ()
    ckv, krope = rowwise(kv_post, (kva, kvn_g, cos_k, sin_k), ((KV_LORA, BF16), (2 * LANES, BF16)), name="kv_post")
    kvb = mm(ckv, kv_w_b, n_dim=N_CHIPS * kvb_tile, tiles=(None, kvb_tile, KV_LORA), b_view=_b_cols, name="kv_b",
             out_dtypes=(BF16,))
    cq_raw, cq = mm(h2b, q_w_a, epi=lambda r, gq: (r, _rms(r, gq)), extras=(qn_g,), out_dtypes=(F32, BF16), name="q_a")

    def rope_and_scale(r, cs, sn):
        return (_cat(r[:, :Q_CHIP_NOPE], _rope_tile(r[:, Q_CHIP_NOPE:], cs, sn)) * Q_PRESCALE,)
    qro = mm(cq, q_w_b, n_dim=N_CHIPS * Q_CHIP, tiles=(None, Q_CHIP, Q_LORA), b_view=_b_cols, epi=rope_and_scale,
             extras=(cos_q, sin_q), out_dtypes=(BF16,), name="q_b")
    (o, lse), landed = attn_fwd(qro, kvb, krope, ride_for(second_ride))
    gathered(second_ride, landed)
    mix1 = mm(o, w_o, name="attn_out")
    h3, h3b = ln_fwd(h2, mix1, ln("ln_mix_g", 1), ln("ln_mix_b", 1), "ln_mix_1")
    f2pre, f2 = mlp_fwd(h3b, 1)
    def last_ln_and_loss(h, mix, gl, bl, t):
        e = _layer_norm(h, mix, gl, bl) - t
        return (e * (1.0 / D_MODEL),), (jnp.broadcast_to(jnp.sum(e * e), (1, LANES)),)
    dh4, loss_acc = rowwise(last_ln_and_loss, (h3, f2, ln("ln_ffn_g", 1), ln("ln_ffn_b", 1), target), ((D_MODEL, F32),),
                            accs=(LANES,), name="ln_ffn_1_loss")
    loss = loss_acc[0, 0] * (0.5 / D_MODEL)

    g = {}

    def into_rows(off, rows_per_chip, shape=pack_shape):
        def view(tm, tn):
            nb = rows_per_chip // tm
            return pl.BlockSpec((None, tm, tn), lambda i, j, k: (i // nb, off // tm + i % nb, 0))
        return shape, view

    def into_cols(off):
        return pack_shape, lambda tm, tn: pl.BlockSpec((None, tm, tn), lambda i, j, k: (j, off // tm + i, 0))

    def mlp_bwd(pack, dr, drb, hb, pre, layer, swap=False):
        dpre = mm(drb, w["w_ff2"][layer].reshape(D_FF, D_MODEL), tb=True, epi=lambda r, p: (r * 2.0 * jnp.maximum(p, 0.0),),
                  extras=(pre,), out_dtypes=(BF16,), tiles=(None, ff_tile, None), name=f"ff2_dx_{layer}")
        pack = mm(pre, drb, ta=True, pro_a=_relu2, name=f"ff2_dw_{layer}", tiles=(ff_tile, PACK_W, None), into=pack,
                  out_view=into_rows(EARLY_OFF["w_ff2"] + layer * ff_tile, ff_tile))
        pack = mm(hb, dpre, ta=True, name=f"ff1_dw_{layer}", tiles=(None, PACK_W, None), into=pack,
                  out_view=into_cols(EARLY_OFF["w_ff1"] + layer * D_MODEL))
        dh = mm(dpre, w["w_ff1"][layer], tb=True, epi=lambda r, d: (r + DN_ALPHA * d,), extras=(dr,), n_dim=D_MODEL,
                tiles=(None, D_MODEL, ff_tile), b_view=_b_cols_t, name=f"ff1_dx_{layer}",
                ride=SwapRide(pack) if swap else None)
        return (pack, *dh) if swap else (pack, dh)

    dr4, dr4b, dg_f1, db_f1 = ln_bwd(h3, f2, ln("ln_ffn_g", 1), dh4, "ln_ffn_bwd_1")
    pack, dh3 = mlp_bwd(None, dr4, dr4b, h3b, f2pre, 1)
    dr3, dr3b, dg_m1, db_m1 = ln_bwd(h2, mix1, ln("ln_mix_g", 1), dh3, "ln_mix_bwd_1")
    shard_rows = D_MODEL // N_CHIPS
    pack = mm(o, dr3b, ta=True, name="attn_out_dw", tiles=(shard_rows, PACK_W, None), into=pack,
              out_view=into_rows(EARLY_OFF["attn_w_o"], shard_rows))
    do = mm(dr3b, w_o, tb=True, name="attn_out_dx")
    def head_dots(do, o):
        return (jnp.concatenate([jnp.sum(do[:, V_HEAD * h:V_HEAD * (h + 1)] * o[:, V_HEAD * h:V_HEAD * (h + 1)], axis=1,
                                         keepdims=True) for h in range(N_HEADS)], axis=1),), ()
    (delta,) = rowwise(head_dots, (do, o), ((N_HEADS, F32),), name="attn_delta")
    tb = min(ATT_TK, seq)
    lse_row = lse[:, :, 0].reshape(N_HEADS, seq // tb, tb)
    delta_row = delta.T.reshape(N_HEADS, seq // tb, tb)
    dqn, dqr, dkvb, dkr = attn_bwd(qro, kvb, krope, do, lse_row, delta_row)

    def q_rope_bwd(dn, dr, cs, sn):
        parts = []
        for k in range(N_CHIPS):
            parts.append(dn[:, Q_CHIP_NOPE * k:Q_CHIP_NOPE * (k + 1)])
            parts.append(_rope_tile_bwd(dr[:, LANES * k:LANES * (k + 1)], cs, sn))
        return (jnp.concatenate(parts, axis=1),), ()
    (dqlin,) = rowwise(q_rope_bwd, (dqn, dqr, cos_q, sin_q), ((N_CHIPS * Q_CHIP, BF16),), name="q_rope_bwd")
    g["q_w_b"] = mm(cq, dqlin, ta=True, name="q_b_dw", tiles=(Q_LORA, Q_CHIP, None), out_view=_out_cols(q_w_b.shape))
    dcq = mm(dqlin, q_w_b, tb=True, n_dim=Q_LORA, tiles=(None, Q_LORA, Q_CHIP), b_view=_b_cols_t, name="q_b_dx")

    def q_norm_bwd(c, gq, d):
        dx, dgq = _rms_bwd(c, gq, d)
        return (dx,), (dgq,)
    dcq_raw, dqn_g = rowwise(q_norm_bwd, (cq_raw, qn_g, dcq), ((Q_LORA, BF16),), accs=(Q_LORA,), name="q_norm_bwd")
    g["q_w_a"] = mm(h2b, dcq_raw, ta=True, name="q_a_dw")
    g["kv_w_b"] = mm(ckv, dkvb, ta=True, name="kv_b_dw", tiles=(KV_LORA, kvb_tile, None), out_view=_out_cols(kv_w_b.shape))
    dckv = mm(dkvb, kv_w_b, tb=True, n_dim=KV_LORA, tiles=(None, KV_LORA, kvb_tile), b_view=_b_cols_t, name="kv_b_dx")
    dkr_sum = head_sum(dkr)

    def kv_post_bwd(kva, gk, dc, dk, cs, sn):
        dx, dgk = _rms_bwd(kva[:, :KV_LORA], gk, dc)
        dk = dk + pltpu.roll(dk, LANES - HALF_ROPE, 1)
        return (jnp.concatenate([dx, _rope_tile_bwd(dk, cs, sn)], axis=1),), (dgk,)
    dkva, dkvn_g = rowwise(kv_post_bwd, (kva, kvn_g, dckv, dkr_sum, cos_k, sin_k), ((KVA_PAD, BF16),),
                           accs=(KV_LORA,), name="kv_post_bwd")
    g["kv_w_a"] = mm(h2b, dkva, ta=True, name="kv_a_dw")
    dh2 = mm(dcq_raw, q_w_a, tb=True, epi=lambda r, d: (r + DN_ALPHA * d,), extras=(dr3,), name="q_a_dx")
    dh2 = mm(dkva, kv_w_a, tb=True, epi=lambda r, d: (r + d,), extras=(dh2,), name="kv_a_dx")

    dr2, dr2b, dg_f0, db_f0 = ln_bwd(h1, f1, ln("ln_ffn_g", 0), dh2, "ln_ffn_bwd_0")
    pack = put_rows(pack, packed_shards(g, MISC_EARLY, EARLY_ROWS - MISC_EARLY_OFF), MISC_EARLY_OFF)
    if comm is None:
        pack, dh1 = mlp_bwd(pack, dr2, dr2b, h1b, f1pre, 0)
    else:
        pack, dh1, (theirs,) = mlp_bwd(pack, dr2, dr2b, h1b, f1pre, 0, swap=True)
        early_sums = add_halves(pack, theirs, comm[1])
    dr1, dr1b, dg_m0, db_m0 = ln_bwd(x, mix0, ln("ln_mix_g", 0), dh1, "ln_mix_bwd_0")
    mid = mm(z, dr1b, ta=True, name="ssm_out_dw", tiles=(shard_rows, PACK_W, None),
             out_view=into_rows(MID_OFF["ssm_w_out"], shard_rows, (N_CHIPS, MID_ROWS, PACK_W)))
    dz = mm(dr1b, w_out, tb=True, name="ssm_out_dx")

    def glu_bwd(v, dz):
        val, sg = v[:, :D_MODEL], _sigmoid(v[:, D_MODEL:])
        return (jnp.concatenate([dz * sg, dz * val * sg * (1.0 - sg)], axis=1),), ()
    (dvg,) = rowwise(glu_bwd, (vg, dz), ((2 * D_MODEL, BF16),), name="glu_bwd")
    g["ssm_w_glu"] = mm(yg, dvg, ta=True, name="glu_proj_dw", tiles=(None, glu_tile, None), out_view=_out_cols(w_glu.shape))
    mid = put_rows(mid, packed_shards(g, MISC_MID, MID_ROWS - MISC_MID_OFF), MISC_MID_OFF)
    dypre = mm(dvg, w_glu, tb=True, epi=lambda r, y: (r * _gelu_grad(y),), extras=(ypre,), n_dim=D_MODEL,
               tiles=(None, D_MODEL, glu_tile), b_view=_b_cols_t, name="glu_proj_dx",
               ride=SwapRide(mid) if comm is not None else None)
    sends = None
    if comm is not None:
        dypre, (theirs,) = dypre
        sends = SendRide([early_sums, add_halves(mid, theirs, comm[1])])
    (dx, dbbd_re, dbbd_im, dcbd_re, dcbd_imn, dar, dai, dd), got = s5_bwd(
        dypre, x, dr1, h_re, h_im, bbd_re, bbd_im, cbd_re, cbd_imn, a_re, a_im, dskip, sends)
    dbb_re = _blockdiag_in_t(dbbd_re).reshape(N_STATES, SSM_GROUP)
    dbb_im = _blockdiag_in_t(dbbd_im).reshape(N_STATES, SSM_GROUP)
    dlr, dli, dldt, db_re, db_im = s5_prep_bwd(lr, li, ldt, b_re, b_im, dar.reshape(N_STATES, 1),
                                               dai.reshape(N_STATES, 1), dbb_re, dbb_im)
    g["ssm_lam_re"] = dlr.reshape(1, N_GROUPS, SSM_STATE)
    g["ssm_lam_im"] = dli.reshape(1, N_GROUPS, SSM_STATE)
    g["ssm_log_dt"] = group_sum(dldt).reshape(1, N_GROUPS)
    g["ssm_b_re"] = db_re.reshape(1, N_GROUPS, SSM_STATE, SSM_GROUP)
    g["ssm_b_im"] = db_im.reshape(1, N_GROUPS, SSM_STATE, SSM_GROUP)
    g["ssm_c_re"] = _blockdiag_out_t(dcbd_re).reshape(1, N_GROUPS, SSM_GROUP, SSM_STATE)
    g["ssm_c_im"] = -_blockdiag_out_t(dcbd_imn).reshape(1, N_GROUPS, SSM_GROUP, SSM_STATE)
    g["ssm_d"] = dd
    g["ln_mix_g"] = jnp.concatenate([dg_m0, dg_m1], 0)
    g["ln_mix_b"] = jnp.concatenate([db_m0, db_m1], 0)
    g["ln_ffn_g"] = jnp.concatenate([dg_f0, dg_f1], 0)
    g["ln_ffn_b"] = jnp.concatenate([db_f0, db_f1], 0)
    g["kv_norm_g"] = dkvn_g.reshape(KV_LORA)
    g["q_norm_g"] = dqn_g
    return loss, dx, pack, mid, g, list(zip(sends.ins, got)) if comm is not None else None


def place(shard, me_idx, dtype, name, layer=None):
    rows, cols = shard.shape[-2:]
    tr = _tile(rows, (512, 256, 128))

    def body(m_ref, x_ref, o_ref):
        o_ref[...] = x_ref[...].astype(o_ref.dtype)

    in_spec = (pl.BlockSpec((tr, cols), lambda i, m: (i, 0)) if layer is None
               else pl.BlockSpec((None, tr, cols), lambda i, m: (layer, i, 0)))
    return _pcall(
        body, name=name,
        grid_spec=pltpu.PrefetchScalarGridSpec(
            num_scalar_prefetch=1, grid=(rows // tr,), in_specs=[in_spec],
            out_specs=pl.BlockSpec((None, tr, cols), lambda i, m: (m[0], i, 0))),
        out_shape=jax.ShapeDtypeStruct((N_CHIPS, rows, cols), dtype),
        compiler_params=_params(("parallel",)),
    )(me_idx, shard)


def place_many(shards, dtypes, me_idx, name):
    def body(m_ref, *refs):
        for x_ref, o_ref in zip(refs[:len(shards)], refs[len(shards):]):
            o_ref[...] = x_ref[...].astype(o_ref.dtype)

    return _pcall(
        body, name=name,
        grid_spec=pltpu.PrefetchScalarGridSpec(
            num_scalar_prefetch=1, grid=(1,),
            in_specs=[pl.BlockSpec(s.shape, lambda i, m: (0, 0)) for s in shards],
            out_specs=[pl.BlockSpec((None,) + s.shape, lambda i, m: (m[0], 0, 0)) for s in shards]),
        out_shape=[jax.ShapeDtypeStruct((N_CHIPS,) + s.shape, d) for s, d in zip(shards, dtypes)],
        compiler_params=_params(("arbitrary",)),
    )(me_idx, *shards)


def put_rows(pack, rows, off):
    _, n, cols = rows.shape

    def body(r_ref, p_ref, o_ref, sem):
        cp = pltpu.make_async_copy(r_ref.at[0], o_ref.at[pl.program_id(0), pl.ds(off, n), :], sem)
        cp.start()
        cp.wait()

    return _pcall(body, name="grad_put_rows", grid=(N_CHIPS,),
                  in_specs=[pl.BlockSpec((1, n, cols), lambda k: (k, 0, 0)), _ANY], out_specs=_ANY,
                  out_shape=jax.ShapeDtypeStruct(pack.shape, pack.dtype), input_output_aliases={1: 0},
                  scratch_shapes=[pltpu.SemaphoreType.DMA],
                  compiler_params=_params(("arbitrary",)))(rows, pack)


def _my_cols(c, mine=True):
    start = (c if mine else 1 - c) * HALF_W
    return pl.ds(pl.multiple_of(start, HALF_W), HALF_W)


def add_halves(gpack, got, c_idx):
    n, rows, _ = gpack.shape
    tr = min(G_BLOCK_ROWS, rows)
    blk = (None, tr, HALF_W)

    def body(c_ref, g_ref, r_ref, o_ref):
        o_ref[...] = (g_ref[...] + r_ref[...]).astype(o_ref.dtype)

    return _pcall(
        body, name="grad_add_halves",
        grid_spec=pltpu.PrefetchScalarGridSpec(
            num_scalar_prefetch=1, grid=(n, rows // tr),
            in_specs=[pl.BlockSpec(blk, lambda k, i, c: (k, i, c[0])), pl.BlockSpec(blk, lambda k, i, c: (k, i, 0))],
            out_specs=pl.BlockSpec(blk, lambda k, i, c: (k, i, 0))),
        out_shape=jax.ShapeDtypeStruct((n, rows, HALF_W), BF16),
        compiler_params=_params(("parallel", "parallel")),
    )(c_idx, gpack, got)


def sum_owner(part, got, idx, total_rows, row_off=0, into=None):
    _, rows, _ = part.shape
    tr = min(G_BLOCK_ROWS, rows)
    n_into = 0 if into is None else 1

    def body(m_ref, p_ref, g_ref, *rest):
        up = lambda v: v.astype(F32)
        rest[-1][...] = ((up(p_ref[...]) + up(g_ref[0])) + up(g_ref[1])) + up(g_ref[2])

    return _pcall(
        body, name="grad_sum_owner",
        grid_spec=pltpu.PrefetchScalarGridSpec(
            num_scalar_prefetch=1, grid=(rows // tr,),
            in_specs=[pl.BlockSpec((None, tr, HALF_W), lambda i, m: (m[0], i, 0)),
                      pl.BlockSpec((3, tr, HALF_W), lambda i, m: (0, i, 0))] + [_ANY] * n_into,
            out_specs=pl.BlockSpec((tr, HALF_W), lambda i, m: (row_off // tr + i, m[1]))),
        out_shape=jax.ShapeDtypeStruct((total_rows, PACK_W), F32),
        input_output_aliases={3: 0} if n_into else {},
        compiler_params=_params(("parallel",)),
    )(idx, part, got, *([into] if n_into else []))


def join_halves(red):
    def body(in_ref, out_ref, send_sem, recv_sem):
        x, y, c, _ = _place()
        sibling = (x, y, 1 - c)
        mine = out_ref.at[:, _my_cols(c)]
        cp = pltpu.make_async_remote_copy(src_ref=mine, dst_ref=mine, send_sem=send_sem, recv_sem=recv_sem,
                                          device_id=sibling, device_id_type=MESH)
        cp.start()
        cp.wait_send()
        other = out_ref.at[:, _my_cols(c, mine=False)]
        pltpu.make_async_remote_copy(src_ref=other, dst_ref=other, send_sem=send_sem, recv_sem=recv_sem,
                                     device_id=sibling, device_id_type=MESH).wait_recv()

    return _pcall(body, name="grad_join_halves", in_specs=[_ANY], out_specs=_ANY,
                  out_shape=jax.ShapeDtypeStruct(red.shape, red.dtype), input_output_aliases={0: 0},
                  scratch_shapes=[pltpu.SemaphoreType.DMA, pltpu.SemaphoreType.DMA])(red)


def adamw(gsrc, g_off, wt, m, v, name):
    n, cols = wt.shape
    tr = math.gcd(math.gcd(g_off, n), 256) if g_off else math.gcd(n, 256)
    off_blk = g_off // tr
    c1 = 1.0 / (1.0 - ADAM_B1 ** ADAM_STEP)
    c2 = 1.0 / (1.0 - ADAM_B2 ** ADAM_STEP)

    def body(g_ref, w_ref, m_ref, v_ref, go_ref, d_ref, mo_ref, vo_ref):
        gv = g_ref[...]
        mn = ADAM_B1 * m_ref[...] + (1.0 - ADAM_B1) * gv
        vn = ADAM_B2 * v_ref[...] + (1.0 - ADAM_B2) * gv * gv
        go_ref[...] = gv
        mo_ref[...] = mn
        vo_ref[...] = vn
        d_ref[...] = -ADAM_LR * ((mn * c1) / (jnp.sqrt(vn * c2) + ADAM_EPS) + ADAM_WD * w_ref[...])

    blk = pl.BlockSpec((tr, cols), lambda i: (i, 0))
    return _pcall(body, name=name, grid=(n // tr,),
                  in_specs=[pl.BlockSpec((tr, cols), lambda i: (off_blk + i, 0)), blk, blk, blk],
                  out_specs=[blk] * 4, out_shape=[jax.ShapeDtypeStruct((n, cols), F32)] * 4,
                  compiler_params=_params(("parallel",)))(gsrc, wt, m, v)


def _rows8(a):
    return -(-a.size // (8 * PACK_W)) * 8


def _as_rows(a, rows=None):
    flat = a.reshape(-1)
    n = _rows8(a) if rows is None else rows
    return jnp.pad(flat, (0, n * PACK_W - flat.shape[0])).reshape(n, PACK_W)


def local_shards_2d(wl):
    return {"w_ff1": [wl["w_ff1"][0], wl["w_ff1"][1]], "w_ff2": [wl["w_ff2"][0], wl["w_ff2"][1]],
            "ssm_w_glu": wl["ssm_w_glu"], "ssm_w_out": wl["ssm_w_out"], "kv_w_a": _pad_kva_cols(wl["kv_w_a"]),
            "kv_w_b": wl["kv_w_b"], "q_w_a": wl["q_w_a"], "q_w_b": _perm_q_cols(wl["q_w_b"]),
            "attn_w_o": wl["attn_w_o"], "ssm_d": wl["ssm_d"].reshape(2, -1)}


def misc_grad_shard(name, g, k):
    if name == "ssm_d":
        w = D_MODEL // N_CHIPS
        return g[:, w * k:w * (k + 1)]
    if name in ("ssm_w_glu", "kv_w_b"):
        return g[k]
    if name == "q_w_b":
        return _unperm_q_cols(g[k])
    rows = D_MODEL // N_CHIPS
    shard = g[rows * k:rows * (k + 1)]
    return _unpad_kva_cols(shard) if name == "kv_w_a" else shard


def packed_shards(g, names, rows, tail=None):
    blocks = []
    for k in range(N_CHIPS):
        parts = [_as_rows(misc_grad_shard(n, g[n], k), MISC_SHARD_ROWS[n]) for n in names]
        if tail is not None:
            parts.append(tail[k * (tail.shape[0] // N_CHIPS):(k + 1) * (tail.shape[0] // N_CHIPS)])
        blk = jnp.concatenate(parts, axis=0)
        blocks.append(jnp.pad(blk, ((0, rows - blk.shape[0]), (0, 0))))
    return jnp.stack(blocks)


def kernel(x, positions, ln_mix_g, ln_mix_b, ln_ffn_g, ln_ffn_b, w_ff1, w_ff2, ssm_lam_re, ssm_lam_im, ssm_log_dt, ssm_b_re, ssm_b_im, ssm_c_re, ssm_c_im, ssm_d, ssm_w_glu, ssm_w_out, kv_w_a, kv_norm_g, kv_w_b, q_w_a, q_norm_g, q_w_b, attn_w_o, loss_target, m_ln_mix_g, m_ln_mix_b, m_ln_ffn_g, m_ln_ffn_b, m_w_ff1, m_w_ff2, m_ssm_lam_re, m_ssm_lam_im, m_ssm_log_dt, m_ssm_b_re, m_ssm_b_im, m_ssm_c_re, m_ssm_c_im, m_ssm_d, m_ssm_w_glu, m_ssm_w_out, m_kv_w_a, m_kv_norm_g, m_kv_w_b, m_q_w_a, m_q_norm_g, m_q_w_b, m_attn_w_o, v_ln_mix_g, v_ln_mix_b, v_ln_ffn_g, v_ln_ffn_b, v_w_ff1, v_w_ff2, v_ssm_lam_re, v_ssm_lam_im, v_ssm_log_dt, v_ssm_b_re, v_ssm_b_im, v_ssm_c_re, v_ssm_c_im, v_ssm_d, v_ssm_w_glu, v_ssm_w_out, v_kv_w_a, v_kv_norm_g, v_kv_w_b, v_q_w_a, v_q_norm_g, v_q_w_b, v_attn_w_o):
    env = dict(locals())
    wl = {n: env[n] for n in WEIGHTS}
    ml = {n: env["m_" + n] for n in WEIGHTS}
    vl = {n: env["v_" + n] for n in WEIGHTS}
    for n in ("ssm_w_glu", "ssm_w_out", "q_w_a", "q_w_b", "attn_w_o"):
        wl[n], ml[n], vl[n] = wl[n][0], ml[n][0], vl[n][0]

    c_idx = lax.axis_index("c").astype(jnp.int32).reshape(1)
    me_idx = (2 * lax.axis_index("x") + lax.axis_index("y")).astype(jnp.int32).reshape(1)

    local = local_shards_2d(wl)
    stacked = {n: [place(wl[n], me_idx, BF16, f"place_{n}_{l}", layer=l) for l in range(DEPTH)] for n in ("w_ff1", "w_ff2")}
    others = [n for n in SHARDED if n not in stacked]
    stacked.update(zip(others, place_many([local[n] for n in others], [F32 if n == "ssm_d" else BF16 for n in others],
                                          me_idx, "place_others")))
    stacked["ssm_d"] = ride_alone(GatherRide([_halves(stacked["ssm_d"])]), "ssm_d_all_gather")[0].reshape(1, D_MODEL)
    for n in REPLICATED:
        stacked[n] = wl[n]

    loss_part, dx, early, mid, g, sent = device_step(x[0], positions[0], loss_target[0], stacked, comm=(me_idx, c_idx))
    loss = lax.psum(loss_part, ("x", "y", "c"))

    small = jnp.concatenate([_as_rows(g[n]) for n in REPLICATED], axis=0)
    small = jnp.pad(small, ((0, SMALL_ROWS - small.shape[0]), (0, 0)))
    late = packed_shards(g, MISC_LATE, LATE_ROWS, tail=small)
    late_sums = add_halves(late, ride_alone(SwapRide(late), "grad_swap_halves")[0], c_idx)
    sent.append((late_sums, ride_alone(SendRide([late_sums]), "grad_send_to_owners")[0]))
    where = jnp.concatenate([me_idx, c_idx])
    starts = (0, EARLY_ROWS, EARLY_ROWS + MID_ROWS)
    total_rows = EARLY_ROWS + MID_ROWS + LATE_ROWS
    reduced = None
    for (sums, got), off in zip(sent, starts):
        reduced = sum_owner(sums, got, where, total_rows, row_off=off, into=reduced)
    reduced = join_halves(reduced)
    quarter = reduced[starts[2] + SMALL_OFF:starts[2] + SMALL_OFF + SMALL_Q_ROWS]
    small_tot = ride_alone(GatherRide([_halves(place(quarter, me_idx, F32, "place_small_grads"))]),
                           "small_grad_all_gather")[0].reshape(SMALL_ROWS, PACK_W)

    out_g, out_d, out_m, out_v = {}, {}, {}, {}
    direct = {**EARLY_OFF, **{n: starts[1] + o for n, o in MID_OFF.items()}}
    for n, off in direct.items():
        res = adamw(reduced, off, wl[n].reshape(-1, PACK_W), ml[n].reshape(-1, PACK_W), vl[n].reshape(-1, PACK_W),
                    "adamw_" + n)
        out_g[n], out_d[n], out_m[n], out_v[n] = [a.reshape(env[n].shape) for a in res]
    for names, off in ((MISC_EARLY, MISC_EARLY_OFF), (MISC_MID, starts[1] + MISC_MID_OFF), (MISC_LATE, starts[2])):
        pack3 = lambda d: jnp.concatenate([_as_rows(d[n], MISC_SHARD_ROWS[n]) for n in names], axis=0)
        res = adamw(reduced, off, pack3(wl), pack3(ml), pack3(vl), "adamw_packed_" + names[0])
        r0 = 0
        for n in names:
            cnt = math.prod(env[n].shape)
            out_g[n], out_d[n], out_m[n], out_v[n] = [
                a[r0:r0 + MISC_SHARD_ROWS[n]].reshape(-1)[:cnt].reshape(env[n].shape) for a in res]
            r0 += MISC_SHARD_ROWS[n]
    ws = jnp.concatenate([_as_rows(wl[n]) for n in REPLICATED], axis=0)
    ms = jnp.concatenate([_as_rows(ml[n]) for n in REPLICATED], axis=0)
    vs = jnp.concatenate([_as_rows(vl[n]) for n in REPLICATED], axis=0)
    pad = ((0, SMALL_ROWS - ws.shape[0]), (0, 0))
    res = adamw(small_tot, 0, jnp.pad(ws, pad), jnp.pad(ms, pad), jnp.pad(vs, pad), "adamw_replicated")
    row = 0
    for n in REPLICATED:
        cnt = math.prod(env[n].shape)
        nrows = _rows8(env[n])
        out_g[n], out_d[n], out_m[n], out_v[n] = [a[row:row + nrows].reshape(-1)[:cnt].reshape(env[n].shape) for a in res]
        row += nrows

    return (loss, dx[None], *[out_g[n] for n in WEIGHTS], *[out_d[n] for n in WEIGHTS],
            *[out_m[n] for n in WEIGHTS], *[out_v[n] for n in WEIGHTS])
```

```python
import functools
import math

import jax
import jax.numpy as jnp
from jax import lax
from jax.experimental import pallas as pl
from jax.experimental.pallas import tpu as pltpu

F32 = jnp.float32
BF16 = jnp.bfloat16
MESH = pl.DeviceIdType.MESH

D_MODEL = 1024
DEPTH = 2
SSM_GROUP = 16
N_GROUPS = D_MODEL // SSM_GROUP
SSM_STATE = 64
N_STATES = N_GROUPS * SSM_STATE
N_HEADS = 8
QK_NOPE = 128
QK_ROPE = 64
HALF_ROPE = QK_ROPE // 2
V_HEAD = 128
QK_DIM = QK_NOPE + QK_ROPE
Q_LORA = 384
KV_LORA = 256
ROPE_THETA = 10000.0
SM_SCALE = QK_DIM ** -0.5
NEG_INF = -1e30
D_FF = 4 * D_MODEL
DN_ALPHA = (2 * DEPTH) ** 0.25
LN_EPS = 1e-5
RMS_EPS = 1e-6
ADAM_LR = 0.001
ADAM_B1 = 0.9
ADAM_B2 = 0.999
ADAM_EPS = 1e-08
ADAM_WD = 0.01
ADAM_STEP = 10

N_CHIPS = 4
LANES = 128
VMEM_LIMIT = 56 * 1024 * 1024
MM_VMEM_BUDGET = 40 * 1024 * 1024
PACK_W = 1024
KVA_PAD = 384
HALF_W = PACK_W // 2

SHARDED = ("w_ff1", "w_ff2", "ssm_w_glu", "ssm_w_out", "kv_w_a", "kv_w_b", "q_w_a", "q_w_b", "attn_w_o", "ssm_d")
G_BLOCK_ROWS = 960
EARLY_OFF = {"w_ff1": 0, "w_ff2": 2048, "attn_w_o": 4096}
MISC_EARLY = ("kv_w_b", "kv_w_a", "q_w_a", "q_w_b")
MISC_EARLY_OFF = 4352
EARLY_ROWS = 5 * G_BLOCK_ROWS
MID_OFF = {"ssm_w_out": 0}
MISC_MID = ("ssm_w_glu",)
MISC_MID_OFF = 256
MID_ROWS = G_BLOCK_ROWS
MISC_LATE = ("ssm_d",)
SMALL_Q_ROWS = 96
SMALL_ROWS = N_CHIPS * SMALL_Q_ROWS
SMALL_OFF = 16
LATE_ROWS = 128
MISC_SHARD_ROWS = {"ssm_d": 16, "ssm_w_glu": 512, "kv_w_b": 128, "kv_w_a": 80, "q_w_a": 96, "q_w_b": 144}
REPLICATED = ("ln_mix_g", "ln_mix_b", "ln_ffn_g", "ln_ffn_b", "ssm_lam_re", "ssm_lam_im", "ssm_log_dt",
              "ssm_b_re", "ssm_b_im", "ssm_c_re", "ssm_c_im", "kv_norm_g", "q_norm_g")
WEIGHTS = ("ln_mix_g", "ln_mix_b", "ln_ffn_g", "ln_ffn_b", "w_ff1", "w_ff2", "ssm_lam_re", "ssm_lam_im",
           "ssm_log_dt", "ssm_b_re", "ssm_b_im", "ssm_c_re", "ssm_c_im", "ssm_d", "ssm_w_glu", "ssm_w_out",
           "kv_w_a", "kv_norm_g", "kv_w_b", "q_w_a", "q_norm_g", "q_w_b", "attn_w_o")


def _pcall(body, **kw):
    return pl.pallas_call(body, **kw)


def _params(sem=None):
    return pltpu.CompilerParams(dimension_semantics=sem, vmem_limit_bytes=VMEM_LIMIT)


_ANY = pl.BlockSpec(memory_space=pl.ANY)


def _tile(dim, prefs):
    for p in prefs:
        if dim % p == 0:
            return p
    return dim


def _place():
    x, y, c = lax.axis_index("x"), lax.axis_index("y"), lax.axis_index("c")
    return x, y, c, [(1 - x, y), (x, 1 - y), (1 - x, 1 - y)]


def _remote(k, src, dst, to, send_sems, recv_sems):
    return pltpu.make_async_remote_copy(src_ref=src, dst_ref=dst, send_sem=send_sems.at[k], recv_sem=recv_sems.at[k],
                                        device_id=to, device_id_type=MESH)


class GatherRide:
    def __init__(self, arrs):
        self.ins = list(arrs)
        self.out_shapes = [jax.ShapeDtypeStruct(a.shape, a.dtype) for a in arrs]
        self.aliases = {i: i for i in range(len(arrs))}
        self.n_sems = 6 * len(arrs)

    def start(self, ins, outs, send_sems, recv_sems):
        x, y, c, chips = _place()
        me = 2 * x + y
        for a, o in enumerate(outs):
            for j, (px, py) in enumerate(chips):
                _remote(6 * a + j, o.at[me, c], o.at[me, c], (px, py, c), send_sems, recv_sems).start()

    def pass_on(self, ins, outs, send_sems, recv_sems):
        x, y, c, chips = _place()
        for a, o in enumerate(outs):
            for j, (px, py) in enumerate(chips):
                blk = o.at[2 * px + py, c]
                _remote(6 * a + j, blk, blk, (px, py, c), send_sems, recv_sems).wait_recv()
                _remote(6 * a + 3 + j, blk, blk, (x, y, 1 - c), send_sems, recv_sems).start()

    def finish(self, ins, outs, send_sems, recv_sems, passed_on=False):
        if not passed_on:
            self.pass_on(ins, outs, send_sems, recv_sems)
        x, y, c, chips = _place()
        me = 2 * x + y
        sibling = (x, y, 1 - c)
        for a, o in enumerate(outs):
            for j, (px, py) in enumerate(chips):
                blk = o.at[2 * px + py, 1 - c]
                _remote(6 * a + 3 + j, blk, blk, sibling, send_sems, recv_sems).wait_recv()
                _remote(6 * a + j, o.at[me, c], o.at[me, c], (px, py, c), send_sems, recv_sems).wait_send()
                mine = o.at[2 * px + py, c]
                _remote(6 * a + 3 + j, mine, mine, sibling, send_sems, recv_sems).wait_send()


class SendRide:
    def __init__(self, parts):
        self.ins = list(parts)
        self.out_shapes = [jax.ShapeDtypeStruct((3,) + p.shape[1:], p.dtype) for p in parts]
        self.aliases = {}
        self.n_sems = 3 * len(parts)

    def _copies(self, ins, outs, send_sems, recv_sems):
        x, y, c, chips = _place()
        return [_remote(3 * a + j, ins[a].at[2 * px + py], outs[a].at[j], (px, py, c), send_sems, recv_sems)
                for a in range(len(ins)) for j, (px, py) in enumerate(chips)]

    def start(self, ins, outs, send_sems, recv_sems):
        for cp in self._copies(ins, outs, send_sems, recv_sems):
            cp.start()

    def finish(self, ins, outs, send_sems, recv_sems):
        for cp in self._copies(ins, outs, send_sems, recv_sems):
            cp.wait()


class SwapRide:
    def __init__(self, pack):
        self.ins = [pack]
        self.out_shapes = [jax.ShapeDtypeStruct(pack.shape[:2] + (HALF_W,), pack.dtype)]
        self.aliases = {}
        self.n_sems = 1

    def _copy(self, ins, outs, send_sems, recv_sems):
        x, y, c, _ = _place()
        return _remote(0, ins[0].at[:, :, _my_cols(c, mine=False)], outs[0], (x, y, 1 - c), send_sems, recv_sems)

    def start(self, ins, outs, send_sems, recv_sems):
        self._copy(ins, outs, send_sems, recv_sems).start()

    def finish(self, ins, outs, send_sems, recv_sems):
        self._copy(ins, outs, send_sems, recv_sems).wait()


def _pcall_riding(body, args, ride, first, last, *, in_specs, out_specs, out_shape, scratch_shapes=(), middle=None,
                  **kw):
    n_in, n_out = len(args), len(out_shape)
    if ride is None:
        return _pcall(body, in_specs=in_specs, out_specs=out_specs, out_shape=out_shape,
                      scratch_shapes=list(scratch_shapes), **kw)(*args), []
    k_in, k_out = len(ride.ins), len(ride.out_shapes)

    def riding(*refs):
        ins, r_in = refs[:n_in], refs[n_in:n_in + k_in]
        outs = refs[n_in + k_in:n_in + k_in + n_out]
        r_out = refs[n_in + k_in + n_out:n_in + k_in + n_out + k_out]
        scratch, (send_sems, recv_sems) = refs[n_in + k_in + n_out + k_out:-2], refs[-2:]

        @pl.when(first())
        def _():
            ride.start(r_in, r_out, send_sems, recv_sems)

        if middle is not None:
            @pl.when(middle())
            def _():
                ride.pass_on(r_in, r_out, send_sems, recv_sems)

        body(*ins, *outs, *scratch)

        @pl.when(last())
        def _():
            if middle is not None:
                ride.finish(r_in, r_out, send_sems, recv_sems, passed_on=True)
            else:
                ride.finish(r_in, r_out, send_sems, recv_sems)

    res = _pcall(riding, in_specs=list(in_specs) + [_ANY] * k_in, out_specs=list(out_specs) + [_ANY] * k_out,
                 out_shape=list(out_shape) + ride.out_shapes,
                 input_output_aliases={n_in + i: n_out + o for i, o in ride.aliases.items()},
                 scratch_shapes=list(scratch_shapes) + [pltpu.SemaphoreType.DMA((ride.n_sems,))] * 2,
                 **kw)(*args, *ride.ins)
    return res[:n_out], res[n_out:]


def ride_alone(ride, name):
    def body(*refs):
        n = len(ride.ins)
        ins, outs, (send_sems, recv_sems) = refs[:n], refs[n:-2], refs[-2:]
        ride.start(ins, outs, send_sems, recv_sems)
        ride.finish(ins, outs, send_sems, recv_sems)

    return _pcall(body, name=name, in_specs=[_ANY] * len(ride.ins), out_specs=[_ANY] * len(ride.out_shapes),
                  out_shape=ride.out_shapes, input_output_aliases=dict(ride.aliases),
                  scratch_shapes=[pltpu.SemaphoreType.DMA((ride.n_sems,))] * 2)(*ride.ins)


def mm(a, b, *, name, ta=False, tb=False, pro_a=None, epi=None, extras=(), out_dtypes=(F32,), n_dim=None,
       tiles=(None, None, None), b_view=None, out_view=None, into=None, ride=None):
    if ta:
        k_dim, m_dim = a.shape
    else:
        m_dim, k_dim = a.shape
    if n_dim is None:
        n_dim = b.shape[0] if tb else b.shape[1]
    tn = tiles[1] or (n_dim if n_dim <= 1024 else _tile(n_dim, (1024, 512, 256, 128)))
    tk = tiles[2] or (k_dim if k_dim <= 1024 else _tile(k_dim, (1024, 512, 256, 128)))
    nk = k_dim // tk

    def vmem_bytes(tm):
        blocks = tm * tk * a.dtype.itemsize + tk * tn * b.dtype.itemsize
        blocks += tm * tn * (sum(e.dtype.itemsize for e in extras) + sum(jnp.dtype(d).itemsize for d in out_dtypes))
        return 2 * blocks + tm * tn * 4

    tm = tiles[0] or next((t for t in (4096, 2048, 1024, 512, 256) if m_dim % t == 0 and vmem_bytes(t) <= MM_VMEM_BUDGET),
                          _tile(m_dim, (128,)))
    assert m_dim % tm == 0 and n_dim % tn == 0 and k_dim % tk == 0, (name, m_dim, n_dim, k_dim, tm, tn, tk)
    n_ex, n_out = len(extras), len(out_dtypes)
    n_into = 0 if into is None else 1
    dims = (((0 if ta else 1,), (1 if tb else 0,)), ((), ()))

    def body(a_ref, b_ref, *rest):
        ex_refs, out_refs = rest[:n_ex], rest[n_ex + n_into:n_ex + n_into + n_out]

        def partial():
            av = a_ref[...]
            if pro_a is not None:
                av = pro_a(av)
            return lax.dot_general(av.astype(BF16), b_ref[...].astype(BF16), dims, preferred_element_type=F32)

        def finish(r):
            res = epi(r, *[e[...] for e in ex_refs]) if epi is not None else (r,)
            for o_ref, v in zip(out_refs, res):
                o_ref[...] = v.astype(o_ref.dtype)

        if nk == 1:
            finish(partial())
            return
        acc = rest[-1]
        k = pl.program_id(2)

        @pl.when(k == 0)
        def _():
            acc[...] = partial()

        @pl.when(k > 0)
        def _():
            acc[...] += partial()

        @pl.when(k == nk - 1)
        def _():
            finish(acc[...])

    def ex_spec(e):
        if e.shape == (m_dim, n_dim):
            return o_spec
        if e.shape[0] == m_dim:
            return pl.BlockSpec((tm, e.shape[1]), lambda i, j, k: (i, 0))
        return pl.BlockSpec(e.shape, lambda i, j, k: (0, 0))

    a_spec = pl.BlockSpec((tk, tm), lambda i, j, k: (k, i)) if ta else pl.BlockSpec((tm, tk), lambda i, j, k: (i, k))
    if b_view is not None:
        b_spec = b_view(tk, tn)
    else:
        b_spec = pl.BlockSpec((tn, tk), lambda i, j, k: (j, k)) if tb else pl.BlockSpec((tk, tn), lambda i, j, k: (k, j))
    o_spec = pl.BlockSpec((tm, tn), lambda i, j, k: (i, j))
    if out_view is None:
        out_specs = [o_spec] * n_out
        out_shape = [jax.ShapeDtypeStruct((m_dim, n_dim), dt) for dt in out_dtypes]
    else:
        assert n_out == 1
        out_specs = [out_view[1](tm, tn)]
        out_shape = [jax.ShapeDtypeStruct(out_view[0], out_dtypes[0])]
    grid = (m_dim // tm, n_dim // tn, nk)
    scratch = [pltpu.VMEM((tm, tn), F32)] if nk > 1 else []
    if ride is not None:
        assert into is None
        at = lambda ids: functools.reduce(jnp.logical_and, [pl.program_id(d) == i for d, i in enumerate(ids)])
        outs, landed = _pcall_riding(
            body, (a, b, *extras), ride, lambda: at((0, 0, 0)), lambda: at([g - 1 for g in grid]),
            name=name, grid=grid, in_specs=[a_spec, b_spec] + [ex_spec(e) for e in extras], out_specs=out_specs,
            out_shape=out_shape, scratch_shapes=scratch, compiler_params=_params(("arbitrary",) * 3))
        return (outs[0] if n_out == 1 else outs), landed
    outs = _pcall(
        body, name=name, grid=grid,
        in_specs=[a_spec, b_spec] + [ex_spec(e) for e in extras] + [_ANY] * n_into,
        out_specs=out_specs, out_shape=out_shape,
        input_output_aliases={2 + n_ex: 0} if n_into else {},
        scratch_shapes=scratch,
        compiler_params=_params(("parallel", "parallel", "arbitrary")),
    )(a, b, *extras, *([into] if n_into else []))
    return outs[0] if n_out == 1 else outs


def rowwise(fn, ins, outs, *, name, accs=(), tm=256):
    rows = ins[0].shape[0]
    tm = min(tm, rows)
    n_in, n_out, n_acc = len(ins), len(outs), len(accs)

    def body(*refs):
        in_refs, out_refs, acc_refs = refs[:n_in], refs[n_in:n_in + n_out], refs[n_in + n_out:]
        res, sums = fn(*[r[...] for r in in_refs])
        for o_ref, v in zip(out_refs, res):
            o_ref[...] = v.astype(o_ref.dtype)
        if n_acc:
            @pl.when(pl.program_id(0) == 0)
            def _():
                for a_ref in acc_refs:
                    a_ref[...] = jnp.zeros_like(a_ref)

            for a_ref, s in zip(acc_refs, sums):
                a_ref[...] += s

    def spec(arr):
        if arr.shape[0] == rows:
            return pl.BlockSpec((tm, arr.shape[1]), lambda i: (i, 0))
        return pl.BlockSpec(arr.shape, lambda i: (0, 0))

    res = _pcall(
        body, name=name, grid=(rows // tm,),
        in_specs=[spec(a) for a in ins],
        out_specs=[pl.BlockSpec((tm, w), lambda i: (i, 0)) for w, _ in outs]
        + [pl.BlockSpec((1, w), lambda i: (0, 0)) for w in accs],
        out_shape=[jax.ShapeDtypeStruct((rows, w), dt) for w, dt in outs]
        + [jax.ShapeDtypeStruct((1, w), F32) for w in accs],
        compiler_params=_params(("arbitrary",) if n_acc else ("parallel",)),
    )(*ins)
    return res


def _relu2(v):
    r = jnp.maximum(v, 0.0)
    return r * r


def _gelu(x):
    c = math.sqrt(2.0 / math.pi)
    return 0.5 * x * (1.0 + jnp.tanh(c * (x + 0.044715 * x * x * x)))


def _gelu_grad(x):
    c = math.sqrt(2.0 / math.pi)
    t = jnp.tanh(c * (x + 0.044715 * x * x * x))
    return 0.5 * (1.0 + t) + 0.5 * x * (1.0 - t * t) * c * (1.0 + 3 * 0.044715 * x * x)


def _sigmoid(x):
    return 1.0 / (1.0 + jnp.exp(-x))


def _layer_norm(h, mix, g, b):
    r = DN_ALPHA * h + mix
    mu = jnp.mean(r, axis=-1, keepdims=True)
    xc = r - mu
    var = jnp.mean(xc * xc, axis=-1, keepdims=True)
    return xc * lax.rsqrt(var + LN_EPS) * g + b


def ln_fwd(h, mix, g, b, name):
    def fn(h, mix, g, b):
        y = _layer_norm(h, mix, g, b)
        return (y, y), ()
    return rowwise(fn, (h, mix, g, b), ((D_MODEL, F32), (D_MODEL, BF16)), name=name)


def ln_bwd(h, mix, g, dy, name):
    def fn(h, mix, g, dy):
        r = DN_ALPHA * h + mix
        mu = jnp.mean(r, axis=-1, keepdims=True)
        xc = r - mu
        var = jnp.mean(xc * xc, axis=-1, keepdims=True)
        rstd = lax.rsqrt(var + LN_EPS)
        xhat = xc * rstd
        dxh = dy * g
        m1 = jnp.mean(dxh, axis=-1, keepdims=True)
        m2 = jnp.mean(dxh * xhat, axis=-1, keepdims=True)
        dr = rstd * (dxh - m1 - xhat * m2)
        return (dr, dr), (jnp.sum(dy * xhat, axis=0, keepdims=True), jnp.sum(dy, axis=0, keepdims=True))
    return rowwise(fn, (h, mix, g, dy), ((D_MODEL, F32), (D_MODEL, BF16)), accs=(D_MODEL, D_MODEL), name=name)


def _rms(x, g):
    r = lax.rsqrt(jnp.mean(x * x, axis=-1, keepdims=True) + RMS_EPS)
    return x * r * g


def _rms_bwd(x, g, dy):
    r = lax.rsqrt(jnp.mean(x * x, axis=-1, keepdims=True) + RMS_EPS)
    xn = x * r
    dyg = dy * g
    dx = r * (dyg - xn * jnp.mean(dyg * xn, axis=-1, keepdims=True))
    return dx, jnp.sum(dy * xn, axis=0, keepdims=True)


def _s5_disc(lr, li, ldt):
    dt = jnp.exp(ldt)
    mag = jnp.exp(lr * dt)
    cs, sn = jnp.cos(li * dt), jnp.sin(li * dt)
    ar, ai = mag * cs, mag * sn
    inv = 1.0 / (lr * lr + li * li)
    n_re = (ar - 1.0) * lr + ai * li
    n_im = ai * lr - (ar - 1.0) * li
    return dt, mag, cs, sn, ar, ai, inv, n_re, n_im


def s5_prep(lr, li, ldt, b_re, b_im):
    def fn(lr, li, ldt, b_re, b_im):
        _, _, _, _, ar, ai, inv, n_re, n_im = _s5_disc(lr, li, ldt)
        cr, ci = n_re * inv, n_im * inv
        return (ar, ai, cr * b_re - ci * b_im, cr * b_im + ci * b_re), ()
    return rowwise(fn, (lr, li, ldt, b_re, b_im), ((1, F32), (1, F32), (SSM_GROUP, F32), (SSM_GROUP, F32)),
                   name="s5_prep", tm=512)


def s5_prep_bwd(lr, li, ldt, b_re, b_im, dar, dai, dbb_re, dbb_im):
    def fn(lr, li, ldt, b_re, b_im, dar, dai, dbb_re, dbb_im):
        dt, mag, cs, sn, ar, ai, inv, n_re, n_im = _s5_disc(lr, li, ldt)
        cr, ci = n_re * inv, n_im * inv
        db_re = cr * dbb_re + ci * dbb_im
        db_im = cr * dbb_im - ci * dbb_re
        dcr = jnp.sum(dbb_re * b_re + dbb_im * b_im, axis=-1, keepdims=True)
        dci = jnp.sum(dbb_im * b_re - dbb_re * b_im, axis=-1, keepdims=True)
        dar = dar + (dcr * lr - dci * li) * inv
        dai = dai + (dcr * li + dci * lr) * inv
        dinv = dcr * n_re + dci * n_im
        dlr = (dcr * (ar - 1.0) + dci * ai) * inv - 2.0 * lr * inv * inv * dinv
        dli = (dcr * ai - dci * (ar - 1.0)) * inv - 2.0 * li * inv * inv * dinv
        dmag = dar * cs + dai * sn
        dth = dai * ar - dar * ai
        dlr = dlr + dmag * mag * dt
        dli = dli + dth * dt
        ddt = dmag * mag * lr + dth * li
        return (dlr, dli, ddt * dt, db_re, db_im), ()
    return rowwise(fn, (lr, li, ldt, b_re, b_im, dar, dai, dbb_re, dbb_im),
                   ((1, F32), (1, F32), (1, F32), (SSM_GROUP, F32), (SSM_GROUP, F32)), name="s5_prep_bwd", tm=512)


def group_sum(x):
    def body(x_ref, o_ref):
        o_ref[...] = jnp.sum(x_ref[...], axis=1)
    return _pcall(body, name="s5_group_sum", out_shape=jax.ShapeDtypeStruct((N_GROUPS, 1), F32))(
        x.reshape(N_GROUPS, SSM_STATE, 1))


GROUPS_PER_TILE = LANES // SSM_GROUP
TILE_STATES = GROUPS_PER_TILE * SSM_STATE
N_UTILES = D_MODEL // LANES
TILES_PER_UTILE = TILE_STATES // LANES


SUBLANES = 8
SCAN_STRIP = 1024
N_STRIPS = N_STATES // SCAN_STRIP
_NT = (((1,), (1,)), ((), ()))
_TN = (((0,), (0,)), ((), ()))


def _scan_coefs(are, aim, shifted, reverse):
    ar = are[...]
    ai = -aim[...] if reverse else aim[...]
    powers = {1: (ar, ai)}
    for d in (2, 4):
        r, i = powers[d // 2]
        powers[d] = (r * r - i * i, 2.0 * r * i)
    rid = lax.broadcasted_iota(jnp.int32, (SUBLANES, N_STATES), 0)
    first = (rid == SUBLANES - 1) if reverse else (rid == 0)
    masks = [(1, first)] + [(d, (rid <= SUBLANES - 1 - d) if reverse else (rid >= d)) for d in (1, 2, 4)]
    for n, (d, keep) in enumerate(masks):
        for part in (0, 1):
            shifted[2 * n + part][...] = jnp.where(keep, jnp.broadcast_to(powers[d][part], (SUBLANES, N_STATES)), 0.0)


def _tile_scan(xr, xi, shifted, nbr_re, nbr_im, reverse):
    for n, d in enumerate((1, 1, 2, 4)):
        by = SUBLANES - d if reverse else d
        fr, fi = (nbr_re, nbr_im) if n == 0 else (xr, xi)
        sr, si = pltpu.roll(fr, by, 0), pltpu.roll(fi, by, 0)
        kr, ki = shifted[2 * n], shifted[2 * n + 1]
        xr, xi = xr + kr * sr - ki * si, xi + kr * si + ki * sr
    return xr, xi


def _tile_rows(t):
    return pl.ds(pl.multiple_of(t * SUBLANES, SUBLANES), SUBLANES)


def s5_fwd(u, bbd_re, bbd_im, cbd_re, cbd_imn, a_re, a_im, dskip, ride=None, t_rows=256):
    seq = u.shape[0]
    t_rows = min(t_rows, seq)
    n_tiles = t_rows // SUBLANES

    def body(u_ref, bre, bim, cre, cimn, are, aim, d_ref, y_ref, gelu_ref, hre_ref, him_ref, car_re, car_im, *shifted):
        @pl.when(pl.program_id(0) == 0)
        def _():
            car_re[...] = jnp.zeros_like(car_re)
            car_im[...] = jnp.zeros_like(car_im)
            _scan_coefs(are, aim, shifted, reverse=False)

        uf = u_ref[...]
        ub = uf.astype(BF16)
        for j in range(N_UTILES):
            uj = ub[:, LANES * j:LANES * (j + 1)]
            sl = slice(TILE_STATES * j, TILE_STATES * (j + 1))
            hre_ref[:, sl] = jnp.dot(uj, bre[j], preferred_element_type=F32)
            him_ref[:, sl] = jnp.dot(uj, bim[j], preferred_element_type=F32)
        for s in range(N_STRIPS):
            cols = pl.ds(s * SCAN_STRIP, SCAN_STRIP)
            coefs = [c[:, cols] for c in shifted]

            def step(t, before):
                rows = _tile_rows(t)
                hr, hi = _tile_scan(hre_ref[rows, cols], him_ref[rows, cols], coefs, before[0], before[1], False)
                hre_ref[rows, cols] = hr
                him_ref[rows, cols] = hi
                return hr, hi

            cr, ci = lax.fori_loop(0, n_tiles, step, (car_re[:, cols], car_im[:, cols]))
            car_re[:, cols] = cr
            car_im[:, cols] = ci
        dv = d_ref[...]
        for j in range(N_UTILES):
            st = slice(TILE_STATES * j, TILE_STATES * (j + 1))
            yj = (jnp.dot(hre_ref[:, st].astype(BF16), cre[j], preferred_element_type=F32)
                  + jnp.dot(him_ref[:, st].astype(BF16), cimn[j], preferred_element_type=F32))
            sl = slice(LANES * j, LANES * (j + 1))
            yj = yj + dv[:, sl] * uf[:, sl]
            y_ref[:, sl] = yj
            gelu_ref[:, sl] = _gelu(yj).astype(gelu_ref.dtype)

    full3 = lambda a: pl.BlockSpec(a.shape, lambda i: (0, 0, 0))
    full2 = lambda a: pl.BlockSpec(a.shape, lambda i: (0, 0))
    tile = pltpu.VMEM((SUBLANES, N_STATES), F32)
    n_chunks = seq // t_rows
    return _pcall_riding(
        body, (u, bbd_re, bbd_im, cbd_re, cbd_imn, a_re, a_im, dskip), ride,
        lambda: pl.program_id(0) == 0, lambda: pl.program_id(0) == n_chunks - 1,
        middle=(lambda: pl.program_id(0) == (7 * n_chunks) // 8) if ride is not None else None,
        name="s5_fwd", grid=(n_chunks,),
        in_specs=[pl.BlockSpec((t_rows, D_MODEL), lambda i: (i, 0)), full3(bbd_re), full3(bbd_im), full3(cbd_re),
                  full3(cbd_imn), full2(a_re), full2(a_im), full2(dskip)],
        out_specs=[pl.BlockSpec((t_rows, D_MODEL), lambda i: (i, 0)),
                   pl.BlockSpec((t_rows, D_MODEL), lambda i: (i, 0)),
                   pl.BlockSpec((t_rows, N_STATES), lambda i: (i, 0)),
                   pl.BlockSpec((t_rows, N_STATES), lambda i: (i, 0))],
        out_shape=[jax.ShapeDtypeStruct((seq, D_MODEL), F32),
                   jax.ShapeDtypeStruct((seq, D_MODEL), BF16),
                   jax.ShapeDtypeStruct((seq, N_STATES), F32),
                   jax.ShapeDtypeStruct((seq, N_STATES), F32)],
        scratch_shapes=[tile] * 10,
        compiler_params=_params(("arbitrary",)))


def s5_bwd(dy, u, dres, h_re, h_im, bbd_re, bbd_im, cbd_re, cbd_imn, a_re, a_im, dskip, ride=None, t_rows=128):
    seq = u.shape[0]
    t_rows = min(t_rows, seq)
    n_chunks = seq // t_rows

    n_tiles = t_rows // SUBLANES

    def body(dy_ref, u_ref, dres_ref, hre_ref, him_ref, hpre_ref, hpim_ref, bre, bim, cre, cimn, are, aim, d_ref,
             dx_ref, dbre, dbim, dcre, dcimn, dar_ref, dai_ref, dd_ref, lre, lim, car_re, car_im, acc_re, acc_im,
             *shifted):
        i = pl.program_id(0)

        @pl.when(i == 0)
        def _():
            for r in (car_re, car_im, acc_re, acc_im, dbre, dbim, dcre, dcimn, dd_ref):
                r[...] = jnp.zeros_like(r)
            _scan_coefs(are, aim, shifted, reverse=True)

        dyf = dy_ref[...]
        dyb = dyf.astype(BF16)
        uf = u_ref[...]
        ub = uf.astype(BF16)
        for j in range(N_UTILES):
            dyj = dyb[:, LANES * j:LANES * (j + 1)]
            st = slice(TILE_STATES * j, TILE_STATES * (j + 1))
            lre[:, st] = lax.dot_general(dyj, cre[j], _NT, preferred_element_type=F32)
            lim[:, st] = lax.dot_general(dyj, cimn[j], _NT, preferred_element_type=F32)
        has_pred = (i < n_chunks - 1).astype(F32)
        last_row = lax.broadcasted_iota(jnp.int32, (SUBLANES, SCAN_STRIP), 0) == SUBLANES - 1
        for s in range(N_STRIPS):
            cols = pl.ds(s * SCAN_STRIP, SCAN_STRIP)
            coefs = [c[:, cols] for c in shifted]
            before_re, before_im = hpre_ref[:, cols] * has_pred, hpim_ref[:, cols] * has_pred

            def step(k, carry):
                after_re, after_im, dar, dai = carry
                t = n_tiles - 1 - k
                rows = _tile_rows(t)
                lr, li = _tile_scan(lre[rows, cols], lim[rows, cols], coefs, after_re, after_im, True)
                lre[rows, cols] = lr
                lim[rows, cols] = li
                prev = _tile_rows(jnp.maximum(t - 1, 0))
                pre_re = jnp.where(t == 0, before_re, hre_ref[prev, cols])
                pre_im = jnp.where(t == 0, before_im, him_ref[prev, cols])
                hpr = pltpu.roll(jnp.where(last_row, pre_re, hre_ref[rows, cols]), 1, 0)
                hpi = pltpu.roll(jnp.where(last_row, pre_im, him_ref[rows, cols]), 1, 0)
                return lr, li, dar + lr * hpr + li * hpi, dai + li * hpr - lr * hpi

            cr, ci, dar, dai = lax.fori_loop(0, n_tiles, step, (car_re[:, cols], car_im[:, cols],
                                                               acc_re[:, cols], acc_im[:, cols]))
            car_re[:, cols] = cr
            car_im[:, cols] = ci
            acc_re[:, cols] = dar
            acc_im[:, cols] = dai

        dv = d_ref[...]
        for j in range(N_UTILES):
            sl = slice(LANES * j, LANES * (j + 1))
            st = slice(TILE_STATES * j, TILE_STATES * (j + 1))
            lrj = lre[:, st].astype(BF16)
            lij = lim[:, st].astype(BF16)
            du = (lax.dot_general(lrj, bre[j], _NT, preferred_element_type=F32)
                  + lax.dot_general(lij, bim[j], _NT, preferred_element_type=F32))
            dx_ref[:, sl] = du + dv[:, sl] * dyf[:, sl] + DN_ALPHA * dres_ref[:, sl]
            uj = ub[:, sl]
            dbre[j] += lax.dot_general(uj, lrj, _TN, preferred_element_type=F32)
            dbim[j] += lax.dot_general(uj, lij, _TN, preferred_element_type=F32)
            dyj = dyb[:, sl]
            dcre[j] += lax.dot_general(hre_ref[:, st].astype(BF16), dyj, _TN, preferred_element_type=F32)
            dcimn[j] += lax.dot_general(him_ref[:, st].astype(BF16), dyj, _TN, preferred_element_type=F32)
        dd_ref[...] += jnp.sum(dyf * uf, axis=0, keepdims=True)

        @pl.when(i == n_chunks - 1)
        def _():
            dar_ref[...] = jnp.sum(acc_re[...], axis=0, keepdims=True)
            dai_ref[...] = jnp.sum(acc_im[...], axis=0, keepdims=True)

    rev = lambda i: (n_chunks - 1 - i, 0)
    prev_tile = lambda i: (jnp.maximum((n_chunks - 1 - i) * n_tiles - 1, 0), 0)
    full3 = lambda a: pl.BlockSpec(a.shape, lambda i: (0, 0, 0))
    full2 = lambda a: pl.BlockSpec(a.shape, lambda i: (0, 0))
    acc3 = lambda shape: pl.BlockSpec(shape, lambda i: (0, 0, 0))
    acc2 = lambda shape: pl.BlockSpec(shape, lambda i: (0, 0))
    tile = pltpu.VMEM((SUBLANES, N_STATES), F32)
    return _pcall_riding(
        body, (dy, u, dres, h_re, h_im, h_re, h_im, bbd_re, bbd_im, cbd_re, cbd_imn, a_re, a_im, dskip), ride,
        lambda: pl.program_id(0) == 0, lambda: pl.program_id(0) == n_chunks - 1,
        name="s5_bwd", grid=(n_chunks,),
        in_specs=[pl.BlockSpec((t_rows, D_MODEL), rev), pl.BlockSpec((t_rows, D_MODEL), rev),
                  pl.BlockSpec((t_rows, D_MODEL), rev),
                  pl.BlockSpec((t_rows, N_STATES), rev), pl.BlockSpec((t_rows, N_STATES), rev),
                  pl.BlockSpec((SUBLANES, N_STATES), prev_tile), pl.BlockSpec((SUBLANES, N_STATES), prev_tile),
                  full3(bbd_re), full3(bbd_im), full3(cbd_re), full3(cbd_imn), full2(a_re), full2(a_im), full2(dskip)],
        out_specs=[pl.BlockSpec((t_rows, D_MODEL), rev), acc3(bbd_re.shape), acc3(bbd_im.shape), acc3(cbd_re.shape),
                   acc3(cbd_imn.shape), acc2((1, N_STATES)), acc2((1, N_STATES)), acc2((1, D_MODEL))],
        out_shape=[jax.ShapeDtypeStruct((seq, D_MODEL), F32), jax.ShapeDtypeStruct(bbd_re.shape, F32),
                   jax.ShapeDtypeStruct(bbd_im.shape, F32), jax.ShapeDtypeStruct(cbd_re.shape, F32),
                   jax.ShapeDtypeStruct(cbd_imn.shape, F32), jax.ShapeDtypeStruct((1, N_STATES), F32),
                   jax.ShapeDtypeStruct((1, N_STATES), F32), jax.ShapeDtypeStruct((1, D_MODEL), F32)],
        scratch_shapes=[pltpu.VMEM((t_rows, N_STATES), F32), pltpu.VMEM((t_rows, N_STATES), F32)] + [tile] * 12,
        compiler_params=_params(("arbitrary",)))


def _eye_groups():
    return jnp.eye(GROUPS_PER_TILE, dtype=F32)


def _blockdiag_in(bb):
    t = bb.transpose(0, 2, 1).reshape(N_UTILES, GROUPS_PER_TILE, SSM_GROUP, SSM_STATE)
    bd = jnp.einsum("jgcp,gh->jgchp", t, _eye_groups())
    return bd.reshape(N_UTILES, LANES, TILE_STATES)


def _blockdiag_in_t(d):
    t = jnp.einsum("jgchp,gh->jgcp", d.reshape(N_UTILES, GROUPS_PER_TILE, SSM_GROUP, GROUPS_PER_TILE, SSM_STATE),
                   _eye_groups())
    return t.reshape(N_GROUPS, SSM_GROUP, SSM_STATE).transpose(0, 2, 1)


def _blockdiag_out(c):
    t = c.transpose(0, 2, 1).reshape(N_UTILES, GROUPS_PER_TILE, SSM_STATE, SSM_GROUP)
    bd = jnp.einsum("jhpc,hg->jhpgc", t, _eye_groups())
    return bd.reshape(N_UTILES, TILE_STATES, LANES)


def _blockdiag_out_t(d):
    t = jnp.einsum("jhpgc,hg->jhpc", d.reshape(N_UTILES, GROUPS_PER_TILE, SSM_STATE, GROUPS_PER_TILE, SSM_GROUP),
                   _eye_groups())
    return t.reshape(N_GROUPS, SSM_STATE, SSM_GROUP).transpose(0, 2, 1)


ATT_TQ = 512
ATT_TK = 512
LOG2E = math.log2(math.e)
LN2 = math.log(2.0)
Q_PRESCALE = SM_SCALE * LOG2E


def _loop_in_pairs(n, step, carry, start=0):
    pairs = (n - start) // 2

    def two(t, c):
        return step(start + 2 * t + 1, step(start + 2 * t, c))

    carry = lax.fori_loop(0, pairs, two, carry)
    return lax.fori_loop(start + 2 * pairs, n, step, carry)


def _causal_bias(t, transposed=False):
    r = lax.broadcasted_iota(jnp.int32, (t, t), 0)
    c = lax.broadcasted_iota(jnp.int32, (t, t), 1)
    return jnp.where((r <= c) if transposed else (c <= r), 0.0, NEG_INF).astype(F32)


def _q_specs(rows, at):
    def nope(*ids):
        r, h = at(*ids)
        return r, 3 * (h // HEADS_PER_CHIP) + h % HEADS_PER_CHIP

    def rope(*ids):
        r, h = at(*ids)
        return r, 3 * (h // HEADS_PER_CHIP) + HEADS_PER_CHIP

    return [pl.BlockSpec((rows, LANES), nope), pl.BlockSpec((rows, LANES), rope)]


def _kv_specs(rows, at):
    def col(f):
        def index(*ids):
            r, h = at(*ids)
            return r, f(h)
        return index

    return [pl.BlockSpec((rows, LANES), col(lambda h: 2 * h)), pl.BlockSpec((rows, LANES), col(lambda h: h % HEADS_PER_CHIP)),
            pl.BlockSpec((rows, LANES), col(lambda h: 2 * h + 1))]


def _cat(a, b):
    return jnp.concatenate([a, b], axis=1)


def attn_fwd(q, kv, kr, ride=None, tq=ATT_TQ, tk=ATT_TK):
    seq = q.shape[0]
    n_heads = N_HEADS
    tq, tk = min(tq, seq), min(tk, seq)
    assert tq == tk

    def body(qn_ref, qr_ref, kn_ref, kr_ref, v_ref, bias_ref, o_ref, lse_ref):
        qi = pl.program_id(1)
        qv = _cat(qn_ref[...], qr_ref[...])
        jd = qi

        def block(j, carry, diag):
            m, l, acc = carry
            rows = pl.ds(pl.multiple_of(j * tk, tk), tk)
            s = lax.dot_general(qv, _cat(kn_ref[rows, :], kr_ref[rows, :]), _NT, preferred_element_type=F32)
            if diag:
                s = s + bias_ref[...]
            m_new = jnp.maximum(m, jnp.max(s, axis=-1, keepdims=True))
            p = jnp.exp2(s - m_new)
            corr = jnp.exp2(m - m_new)
            l = l * corr + jnp.sum(p, axis=-1, keepdims=True)
            acc = acc * corr + jnp.dot(p.astype(BF16), v_ref[rows, :], preferred_element_type=F32)
            return m_new, l, acc

        init = (jnp.full((tq, 1), NEG_INF, F32), jnp.zeros((tq, 1), F32), jnp.zeros((tq, V_HEAD), F32))
        carry = _loop_in_pairs(jd, lambda j, c: block(j, c, False), init)
        m, l, acc = block(jd, carry, True)
        o_ref[...] = acc / l
        lse_ref[...] = jnp.transpose(jnp.broadcast_to(m + jnp.log2(l), (tq, LANES)))[:1, :]

    n_q = seq // tq
    return _pcall_riding(
        body, (q, q, kv, kr, kv, _causal_bias(tq)), ride,
        lambda: (pl.program_id(0) == 0) & (pl.program_id(1) == 0),
        lambda: (pl.program_id(0) == n_heads - 1) & (pl.program_id(1) == n_q - 1),
        middle=(lambda: (pl.program_id(0) == (5 * n_heads) // 8) & (pl.program_id(1) == 0)) if ride is not None else None,
        name="attn_fwd", grid=(n_heads, n_q),
        in_specs=_q_specs(tq, lambda h, i: (i, h)) + _kv_specs(seq, lambda h, i: (0, h))
        + [pl.BlockSpec((tq, tq), lambda h, i: (0, 0))],
        out_specs=[pl.BlockSpec((tq, V_HEAD), lambda h, i: (i, h)),
                   pl.BlockSpec((None, None, 1, tq), lambda h, i: (h, i, 0, 0))],
        out_shape=[jax.ShapeDtypeStruct((seq, n_heads * V_HEAD), F32),
                   jax.ShapeDtypeStruct((n_heads, n_q, 1, tq), F32)],
        compiler_params=_params(("arbitrary", "arbitrary")))


def attn_bwd(q, kv, kr, do, lse_row, delta_row, tq=ATT_TK):
    seq = q.shape[0]
    tq = min(tq, seq)
    n_blk = seq // tq

    def body(qn_ref, qr_ref, kn_ref, kr_ref, v_ref, do_ref, lse_ref, delta_ref, bias_ref, dqn_ref, dqr_ref, dkv_ref, dkr_ref,
             dq_acc):
        head, kj = pl.program_id(0), pl.program_id(1)

        @pl.when(kj == 0)
        def _():
            dq_acc[...] = jnp.zeros_like(dq_acc)

        kc = _cat(kn_ref[...], kr_ref[...])
        vv = v_ref[...]

        def block(i, carry, diag):
            dk, dv = carry
            rows = pl.ds(pl.multiple_of(i * tq, tq), tq)
            qv = _cat(qn_ref[rows, :], qr_ref[rows, :])
            st = lax.dot_general(kc, qv, _NT, preferred_element_type=F32)
            if diag:
                st = st + bias_ref[...]
            pt = jnp.exp2(st - lse_ref[0, pl.ds(i, 1), :])
            dob = do_ref[rows, :].astype(BF16)
            dv = dv + jnp.dot(pt.astype(BF16), dob, preferred_element_type=F32)
            dpt = lax.dot_general(vv, dob, _NT, preferred_element_type=F32)
            dst = (pt * (dpt - delta_ref[0, pl.ds(i, 1), :])).astype(BF16)
            dk = dk + jnp.dot(dst, qv, preferred_element_type=F32)
            dq_acc[rows, :] += lax.dot_general(dst, kc, _TN, preferred_element_type=F32)
            return dk, dv

        carry = block(kj, (jnp.zeros((tq, 2 * LANES), F32), jnp.zeros((tq, V_HEAD), F32)), True)
        dk, dv = _loop_in_pairs(n_blk, lambda i, c: block(i, c, False), carry, start=kj + 1)
        dk = dk * LN2
        dkv_ref[...] = _cat(dk[:, :LANES], dv).astype(dkv_ref.dtype)
        lane = lax.broadcasted_iota(jnp.int32, (tq, LANES), 1)
        mine = (lane // HALF_ROPE) % HEADS_PER_CHIP == head % HEADS_PER_CHIP
        dkr_ref[0] = jnp.where(mine, dk[:, LANES:], 0.0)

        @pl.when(kj == n_blk - 1)
        def _():
            dqn_ref[...] = dq_acc[:, :LANES] * SM_SCALE

        @pl.when((kj == n_blk - 1) & (head % HEADS_PER_CHIP == 0))
        def _():
            dqr_ref[...] = dq_acc[:, LANES:] * SM_SCALE

        @pl.when((kj == n_blk - 1) & (head % HEADS_PER_CHIP > 0))
        def _():
            dqr_ref[...] += dq_acc[:, LANES:] * SM_SCALE

    return _pcall(
        body, name="attn_bwd", grid=(N_HEADS, n_blk),
        in_specs=_q_specs(seq, lambda h, j: (0, h)) + _kv_specs(tq, lambda h, j: (j, h))
        + [pl.BlockSpec((seq, V_HEAD), lambda h, j: (0, h)),
           pl.BlockSpec((1, n_blk, tq), lambda h, j: (h, 0, 0)),
           pl.BlockSpec((1, n_blk, tq), lambda h, j: (h, 0, 0)),
           pl.BlockSpec((tq, tq), lambda h, j: (0, 0))],
        out_specs=[pl.BlockSpec((seq, LANES), lambda h, j: (0, h)),
                   pl.BlockSpec((seq, LANES), lambda h, j: (0, h // HEADS_PER_CHIP)),
                   pl.BlockSpec((tq, QK_NOPE + V_HEAD), lambda h, j: (j, h)),
                   pl.BlockSpec((1, tq, LANES), lambda h, j: (h, j, 0))],
        out_shape=[jax.ShapeDtypeStruct((seq, N_HEADS * QK_NOPE), F32),
                   jax.ShapeDtypeStruct((seq, N_CHIPS * LANES), F32),
                   jax.ShapeDtypeStruct((seq, N_HEADS * (QK_NOPE + V_HEAD)), BF16),
                   jax.ShapeDtypeStruct((N_HEADS, seq, LANES), F32)],
        scratch_shapes=[pltpu.VMEM((seq, 2 * LANES), F32)],
        compiler_params=_params(("arbitrary", "arbitrary")),
    )(q, q, kv, kr, kv, do, lse_row, delta_row, _causal_bias(tq, transposed=True))


def head_sum(x, ts=512):
    n_heads, seq, w = x.shape
    ts = min(ts, seq)

    def body(x_ref, o_ref):
        o_ref[...] = jnp.sum(x_ref[...], axis=0)

    return _pcall(body, name="head_sum", grid=(seq // ts,),
                  in_specs=[pl.BlockSpec((n_heads, ts, w), lambda i: (0, i, 0))],
                  out_specs=pl.BlockSpec((ts, w), lambda i: (i, 0)),
                  out_shape=jax.ShapeDtypeStruct((seq, w), F32),
                  compiler_params=_params(("parallel",)))(x)


HEADS_PER_CHIP = N_HEADS // N_CHIPS
Q_CHIP = HEADS_PER_CHIP * QK_DIM
Q_CHIP_NOPE = HEADS_PER_CHIP * QK_NOPE


def _perm_q_cols(w):
    t = w.reshape(w.shape[0], HEADS_PER_CHIP, QK_DIM)
    return jnp.concatenate([t[:, :, :QK_NOPE].reshape(w.shape[0], -1),
                            t[:, :, QK_NOPE:QK_NOPE + HALF_ROPE].reshape(w.shape[0], -1),
                            t[:, :, QK_NOPE + HALF_ROPE:].reshape(w.shape[0], -1)], axis=1)


def _unperm_q_cols(w):
    r = w.shape[0]
    nope = w[:, :Q_CHIP_NOPE].reshape(r, HEADS_PER_CHIP, QK_NOPE)
    r1 = w[:, Q_CHIP_NOPE:Q_CHIP_NOPE + QK_ROPE].reshape(r, HEADS_PER_CHIP, HALF_ROPE)
    r2 = w[:, Q_CHIP_NOPE + QK_ROPE:].reshape(r, HEADS_PER_CHIP, HALF_ROPE)
    return jnp.concatenate([nope, r1, r2], axis=2).reshape(r, Q_CHIP)


def _pad_kva_cols(w):
    z = jnp.zeros((w.shape[0], HALF_ROPE), w.dtype)
    return jnp.concatenate([w[:, :KV_LORA], w[:, KV_LORA:KV_LORA + HALF_ROPE], z, w[:, KV_LORA + HALF_ROPE:], z], axis=1)


def _unpad_kva_cols(w):
    return jnp.concatenate([w[:, :KV_LORA], w[:, KV_LORA:KV_LORA + HALF_ROPE],
                            w[:, KV_LORA + QK_ROPE:KV_LORA + QK_ROPE + HALF_ROPE]], axis=1)


def _rope_tile(t, cs, sn):
    return t * cs + pltpu.roll(t, LANES // 2, 1) * sn


def _rope_tile_bwd(d, cs, sn):
    return d * cs + pltpu.roll(d * sn, LANES // 2, 1)


def _b_cols(tk, tn):
    return pl.BlockSpec((None, tk, tn), lambda i, j, k: (j, k, 0))


def _b_cols_t(tk, tn):
    return pl.BlockSpec((None, tn, tk), lambda i, j, k: (k, j, 0))


def _out_cols(shape):
    return shape, lambda tm, tn: pl.BlockSpec((None, tm, tn), lambda i, j, k: (j, i, 0))


def _halves(a):
    return a.reshape(N_CHIPS, 2, a.shape[1] // 2, a.shape[2])


def device_step(x, positions, target, w, comm=None):
    seq = x.shape[0]
    w = dict(w)

    def gathered(names, outs):
        for n, a in zip(names, outs):
            if isinstance(n, tuple):
                w[n[0]] = [a.reshape(v.shape) if l == n[1] else v for l, v in enumerate(w[n[0]])]
            else:
                w[n] = a.reshape(w[n].shape)

    def ride_for(names):
        if comm is None:
            return None
        return GatherRide([_halves(w[n[0]][n[1]] if isinstance(n, tuple) else w[n]) for n in names])

    first_ride = ("ssm_w_glu", "ssm_w_out", ("w_ff1", 0), ("w_ff2", 0))
    mla_ride = ("kv_w_a", "kv_w_b", "q_w_a", "q_w_b", "attn_w_o")
    second_ride = (("w_ff1", 1), ("w_ff2", 1))

    inv_freq = ROPE_THETA ** (-jnp.arange(HALF_ROPE, dtype=F32) / HALF_ROPE)
    ang = positions.astype(F32)[:, None] * inv_freq
    cos, sin = jnp.cos(ang), jnp.sin(ang)
    zero = jnp.zeros_like(cos)
    cos_q, sin_q = jnp.concatenate([cos] * 4, 1), jnp.concatenate([-sin, -sin, sin, sin], 1)
    cos_k, sin_k = jnp.concatenate([cos, zero, cos, zero], 1), jnp.concatenate([-sin, zero, sin, zero], 1)
    ff_tile = D_FF // N_CHIPS
    pack_shape = (N_CHIPS, EARLY_ROWS, PACK_W)

    lr = w["ssm_lam_re"].reshape(N_STATES, 1)
    li = w["ssm_lam_im"].reshape(N_STATES, 1)
    ldt = jnp.repeat(w["ssm_log_dt"].reshape(N_GROUPS), SSM_STATE).reshape(N_STATES, 1)
    b_re = w["ssm_b_re"].reshape(N_STATES, SSM_GROUP)
    b_im = w["ssm_b_im"].reshape(N_STATES, SSM_GROUP)
    a_re, a_im, bb_re, bb_im = s5_prep(lr, li, ldt, b_re, b_im)
    a_re, a_im = a_re.reshape(1, N_STATES), a_im.reshape(1, N_STATES)
    bbd_re = _blockdiag_in(bb_re.reshape(N_GROUPS, SSM_STATE, SSM_GROUP)).astype(BF16)
    bbd_im = _blockdiag_in(bb_im.reshape(N_GROUPS, SSM_STATE, SSM_GROUP)).astype(BF16)
    cbd_re = _blockdiag_out(w["ssm_c_re"].reshape(N_GROUPS, SSM_GROUP, SSM_STATE)).astype(BF16)
    cbd_imn = _blockdiag_out(-w["ssm_c_im"].reshape(N_GROUPS, SSM_GROUP, SSM_STATE)).astype(BF16)
    dskip = w["ssm_d"].reshape(1, D_MODEL)
    (ypre, yg, h_re, h_im), landed = s5_fwd(x, bbd_re, bbd_im, cbd_re, cbd_imn, a_re, a_im, dskip, ride_for(first_ride))
    gathered(first_ride, landed)
    w_glu = w["ssm_w_glu"]
    glu_tile = w_glu.shape[2]
    vg = mm(yg, w_glu, n_dim=2 * D_MODEL, tiles=(None, glu_tile, None), b_view=_b_cols, name="glu_proj")

    def glu(v):
        return (v[:, :D_MODEL] * _sigmoid(v[:, D_MODEL:]),), ()
    (z,) = rowwise(glu, (vg,), ((D_MODEL, BF16),), name="glu")
    w_out = w["ssm_w_out"].reshape(D_MODEL, D_MODEL)
    mix0 = mm(z, w_out, name="ssm_out")

    def mlp_fwd(hb, layer, riding=None):
        pre = mm(hb, w["w_ff1"][layer], n_dim=D_FF, tiles=(None, ff_tile, None), b_view=_b_cols, name=f"ff1_{layer}",
                 out_dtypes=(BF16,), ride=ride_for(riding) if riding else None)
        if riding and comm is not None:
            pre, landed = pre
            gathered(riding, landed)
        f = mm(pre, w["w_ff2"][layer].reshape(D_FF, D_MODEL), pro_a=_relu2, name=f"ff2_{layer}")
        return pre, f

    ln = lambda name, l: w[name][l].reshape(1, D_MODEL)
    h1, h1b = ln_fwd(x, mix0, ln("ln_mix_g", 0), ln("ln_mix_b", 0), "ln_mix_0")
    f1pre, f1 = mlp_fwd(h1b, 0, mla_ride)
    h2, h2b = ln_fwd(h1, f1, ln("ln_ffn_g", 0), ln("ln_ffn_b", 0), "ln_ffn_0")

    kv_w_a = w["kv_w_a"].reshape(D_MODEL, KVA_PAD)
    kv_w_b = w["kv_w_b"]
    q_w_a = w["q_w_a"].reshape(D_MODEL, Q_LORA)
    q_w_b = w["q_w_b"]
    w_o = w["attn_w_o"].reshape(D_MODEL, D_MODEL)
    kvb_tile = kv_w_b.shape[2]
    kvn_g = w["kv_norm_g"].reshape(1, KV_LORA)
    qn_g = w["q_norm_g"].reshape(1, Q_LORA)
    kva = mm(h2b, kv_w_a, name="kv_a")

    def kv_post(kva, g, cs, sn):
        tile = _rope_tile(kva[:, KV_LORA:], cs, sn)
        return (_rms(kva[:, :KV_LORA], g), _cat(tile, pltpu.roll(tile, HALF_ROPE, 1))), ()
    ckv, krope = rowwise(kv_post, (kva, kvn_g, cos_k, sin_k), ((KV_LORA, BF16), (2 * LANES, BF16)), name="kv_post")
    kvb = mm(ckv, kv_w_b, n_dim=N_CHIPS * kvb_tile, tiles=(None, kvb_tile, KV_LORA), b_view=_b_cols, name="kv_b",
             out_dtypes=(BF16,))
    cq_raw, cq = mm(h2b, q_w_a, epi=lambda r, gq: (r, _rms(r, gq)), extras=(qn_g,), out_dtypes=(F32, BF16), name="q_a")

    def rope_and_scale(r, cs, sn):
        return (_cat(r[:, :Q_CHIP_NOPE], _rope_tile(r[:, Q_CHIP_NOPE:], cs, sn)) * Q_PRESCALE,)
    qro = mm(cq, q_w_b, n_dim=N_CHIPS * Q_CHIP, tiles=(None, Q_CHIP, Q_LORA), b_view=_b_cols, epi=rope_and_scale,
             extras=(cos_q, sin_q), out_dtypes=(BF16,), name="q_b")
    (o, lse), landed = attn_fwd(qro, kvb, krope, ride_for(second_ride))
    gathered(second_ride, landed)
    mix1 = mm(o, w_o, name="attn_out")
    h3, h3b = ln_fwd(h2, mix1, ln("ln_mix_g", 1), ln("ln_mix_b", 1), "ln_mix_1")
    f2pre, f2 = mlp_fwd(h3b, 1)
    def last_ln_and_loss(h, mix, gl, bl, t):
        e = _layer_norm(h, mix, gl, bl) - t
        return (e * (1.0 / D_MODEL),), (jnp.broadcast_to(jnp.sum(e * e), (1, LANES)),)
    dh4, loss_acc = rowwise(last_ln_and_loss, (h3, f2, ln("ln_ffn_g", 1), ln("ln_ffn_b", 1), target), ((D_MODEL, F32),),
                            accs=(LANES,), name="ln_ffn_1_loss")
    loss = loss_acc[0, 0] * (0.5 / D_MODEL)

    g = {}

    def into_rows(off, rows_per_chip, shape=pack_shape):
        def view(tm, tn):
            nb = rows_per_chip // tm
            return pl.BlockSpec((None, tm, tn), lambda i, j, k: (i // nb, off // tm + i % nb, 0))
        return shape, view

    def into_cols(off):
        return pack_shape, lambda tm, tn: pl.BlockSpec((None, tm, tn), lambda i, j, k: (j, off // tm + i, 0))

    def mlp_bwd(pack, dr, drb, hb, pre, layer, swap=False):
        dpre = mm(drb, w["w_ff2"][layer].reshape(D_FF, D_MODEL), tb=True, epi=lambda r, p: (r * 2.0 * jnp.maximum(p, 0.0),),
                  extras=(pre,), out_dtypes=(BF16,), tiles=(None, ff_tile, None), name=f"ff2_dx_{layer}")
        pack = mm(pre, drb, ta=True, pro_a=_relu2, name=f"ff2_dw_{layer}", tiles=(ff_tile, PACK_W, None), into=pack,
                  out_view=into_rows(EARLY_OFF["w_ff2"] + layer * ff_tile, ff_tile))
        pack = mm(hb, dpre, ta=True, name=f"ff1_dw_{layer}", tiles=(None, PACK_W, None), into=pack,
                  out_view=into_cols(EARLY_OFF["w_ff1"] + layer * D_MODEL))
        dh = mm(dpre, w["w_ff1"][layer], tb=True, epi=lambda r, d: (r + DN_ALPHA * d,), extras=(dr,), n_dim=D_MODEL,
                tiles=(None, D_MODEL, ff_tile), b_view=_b_cols_t, name=f"ff1_dx_{layer}",
                ride=SwapRide(pack) if swap else None)
        return (pack, *dh) if swap else (pack, dh)

    dr4, dr4b, dg_f1, db_f1 = ln_bwd(h3, f2, ln("ln_ffn_g", 1), dh4, "ln_ffn_bwd_1")
    pack, dh3 = mlp_bwd(None, dr4, dr4b, h3b, f2pre, 1)
    dr3, dr3b, dg_m1, db_m1 = ln_bwd(h2, mix1, ln("ln_mix_g", 1), dh3, "ln_mix_bwd_1")
    shard_rows = D_MODEL // N_CHIPS
    pack = mm(o, dr3b, ta=True, name="attn_out_dw", tiles=(shard_rows, PACK_W, None), into=pack,
              out_view=into_rows(EARLY_OFF["attn_w_o"], shard_rows))
    do = mm(dr3b, w_o, tb=True, name="attn_out_dx")
    def head_dots(do, o):
        return (jnp.concatenate([jnp.sum(do[:, V_HEAD * h:V_HEAD * (h + 1)] * o[:, V_HEAD * h:V_HEAD * (h + 1)], axis=1,
                                         keepdims=True) for h in range(N_HEADS)], axis=1),), ()
    (delta,) = rowwise(head_dots, (do, o), ((N_HEADS, F32),), name="attn_delta")
    tb = min(ATT_TK, seq)
    lse_row = lse.reshape(N_HEADS, seq // tb, tb)
    delta_row = delta.T.reshape(N_HEADS, seq // tb, tb)
    dqn, dqr, dkvb, dkr = attn_bwd(qro, kvb, krope, do, lse_row, delta_row)

    def q_rope_bwd(dn, dr, cs, sn):
        parts = []
        for k in range(N_CHIPS):
            parts.append(dn[:, Q_CHIP_NOPE * k:Q_CHIP_NOPE * (k + 1)])
            parts.append(_rope_tile_bwd(dr[:, LANES * k:LANES * (k + 1)], cs, sn))
        return (jnp.concatenate(parts, axis=1),), ()
    (dqlin,) = rowwise(q_rope_bwd, (dqn, dqr, cos_q, sin_q), ((N_CHIPS * Q_CHIP, BF16),), name="q_rope_bwd")
    g["q_w_b"] = mm(cq, dqlin, ta=True, name="q_b_dw", tiles=(Q_LORA, Q_CHIP, None), out_view=_out_cols(q_w_b.shape))
    dcq = mm(dqlin, q_w_b, tb=True, n_dim=Q_LORA, tiles=(None, Q_LORA, Q_CHIP), b_view=_b_cols_t, name="q_b_dx")

    def q_norm_bwd(c, gq, d):
        dx, dgq = _rms_bwd(c, gq, d)
        return (dx,), (dgq,)
    dcq_raw, dqn_g = rowwise(q_norm_bwd, (cq_raw, qn_g, dcq), ((Q_LORA, BF16),), accs=(Q_LORA,), name="q_norm_bwd")
    g["q_w_a"] = mm(h2b, dcq_raw, ta=True, name="q_a_dw")
    g["kv_w_b"] = mm(ckv, dkvb, ta=True, name="kv_b_dw", tiles=(KV_LORA, kvb_tile, None), out_view=_out_cols(kv_w_b.shape))
    dckv = mm(dkvb, kv_w_b, tb=True, n_dim=KV_LORA, tiles=(None, KV_LORA, kvb_tile), b_view=_b_cols_t, name="kv_b_dx")
    dkr_sum = head_sum(dkr)

    def kv_post_bwd(kva, gk, dc, dk, cs, sn):
        dx, dgk = _rms_bwd(kva[:, :KV_LORA], gk, dc)
        dk = dk + pltpu.roll(dk, LANES - HALF_ROPE, 1)
        return (jnp.concatenate([dx, _rope_tile_bwd(dk, cs, sn)], axis=1),), (dgk,)
    dkva, dkvn_g = rowwise(kv_post_bwd, (kva, kvn_g, dckv, dkr_sum, cos_k, sin_k), ((KVA_PAD, BF16),),
                           accs=(KV_LORA,), name="kv_post_bwd")
    g["kv_w_a"] = mm(h2b, dkva, ta=True, name="kv_a_dw")
    dh2 = mm(dcq_raw, q_w_a, tb=True, epi=lambda r, d: (r + DN_ALPHA * d,), extras=(dr3,), name="q_a_dx")
    dh2 = mm(dkva, kv_w_a, tb=True, epi=lambda r, d: (r + d,), extras=(dh2,), name="kv_a_dx")

    dr2, dr2b, dg_f0, db_f0 = ln_bwd(h1, f1, ln("ln_ffn_g", 0), dh2, "ln_ffn_bwd_0")
    pack = put_rows(pack, packed_shards(g, MISC_EARLY, EARLY_ROWS - MISC_EARLY_OFF), MISC_EARLY_OFF)
    if comm is None:
        pack, dh1 = mlp_bwd(pack, dr2, dr2b, h1b, f1pre, 0)
    else:
        pack, dh1, (theirs,) = mlp_bwd(pack, dr2, dr2b, h1b, f1pre, 0, swap=True)
        early_sums = add_halves(pack, theirs, comm[1])
    dr1, dr1b, dg_m0, db_m0 = ln_bwd(x, mix0, ln("ln_mix_g", 0), dh1, "ln_mix_bwd_0")
    mid = mm(z, dr1b, ta=True, name="ssm_out_dw", tiles=(shard_rows, PACK_W, None),
             out_view=into_rows(MID_OFF["ssm_w_out"], shard_rows, (N_CHIPS, MID_ROWS, PACK_W)))
    dz = mm(dr1b, w_out, tb=True, name="ssm_out_dx")

    def glu_bwd(v, dz):
        val, sg = v[:, :D_MODEL], _sigmoid(v[:, D_MODEL:])
        return (jnp.concatenate([dz * sg, dz * val * sg * (1.0 - sg)], axis=1),), ()
    (dvg,) = rowwise(glu_bwd, (vg, dz), ((2 * D_MODEL, BF16),), name="glu_bwd")
    g["ssm_w_glu"] = mm(yg, dvg, ta=True, name="glu_proj_dw", tiles=(None, glu_tile, None), out_view=_out_cols(w_glu.shape))
    mid = put_rows(mid, packed_shards(g, MISC_MID, MID_ROWS - MISC_MID_OFF), MISC_MID_OFF)
    dypre = mm(dvg, w_glu, tb=True, epi=lambda r, y: (r * _gelu_grad(y),), extras=(ypre,), n_dim=D_MODEL,
               tiles=(None, D_MODEL, glu_tile), b_view=_b_cols_t, name="glu_proj_dx",
               ride=SwapRide(mid) if comm is not None else None)
    sends = None
    if comm is not None:
        dypre, (theirs,) = dypre
        sends = SendRide([early_sums, add_halves(mid, theirs, comm[1])])
    (dx, dbbd_re, dbbd_im, dcbd_re, dcbd_imn, dar, dai, dd), got = s5_bwd(
        dypre, x, dr1, h_re, h_im, bbd_re, bbd_im, cbd_re, cbd_imn, a_re, a_im, dskip, sends)
    dbb_re = _blockdiag_in_t(dbbd_re).reshape(N_STATES, SSM_GROUP)
    dbb_im = _blockdiag_in_t(dbbd_im).reshape(N_STATES, SSM_GROUP)
    dlr, dli, dldt, db_re, db_im = s5_prep_bwd(lr, li, ldt, b_re, b_im, dar.reshape(N_STATES, 1),
                                               dai.reshape(N_STATES, 1), dbb_re, dbb_im)
    g["ssm_lam_re"] = dlr.reshape(1, N_GROUPS, SSM_STATE)
    g["ssm_lam_im"] = dli.reshape(1, N_GROUPS, SSM_STATE)
    g["ssm_log_dt"] = group_sum(dldt).reshape(1, N_GROUPS)
    g["ssm_b_re"] = db_re.reshape(1, N_GROUPS, SSM_STATE, SSM_GROUP)
    g["ssm_b_im"] = db_im.reshape(1, N_GROUPS, SSM_STATE, SSM_GROUP)
    g["ssm_c_re"] = _blockdiag_out_t(dcbd_re).reshape(1, N_GROUPS, SSM_GROUP, SSM_STATE)
    g["ssm_c_im"] = -_blockdiag_out_t(dcbd_imn).reshape(1, N_GROUPS, SSM_GROUP, SSM_STATE)
    g["ssm_d"] = dd
    g["ln_mix_g"] = jnp.concatenate([dg_m0, dg_m1], 0)
    g["ln_mix_b"] = jnp.concatenate([db_m0, db_m1], 0)
    g["ln_ffn_g"] = jnp.concatenate([dg_f0, dg_f1], 0)
    g["ln_ffn_b"] = jnp.concatenate([db_f0, db_f1], 0)
    g["kv_norm_g"] = dkvn_g.reshape(KV_LORA)
    g["q_norm_g"] = dqn_g
    return loss, dx, pack, mid, g, list(zip(sends.ins, got)) if comm is not None else None


def place(shard, me_idx, dtype, name, layer=None):
    rows, cols = shard.shape[-2:]
    tr = _tile(rows, (512, 256, 128))

    def body(m_ref, x_ref, o_ref):
        o_ref[...] = x_ref[...].astype(o_ref.dtype)

    in_spec = (pl.BlockSpec((tr, cols), lambda i, m: (i, 0)) if layer is None
               else pl.BlockSpec((None, tr, cols), lambda i, m: (layer, i, 0)))
    return _pcall(
        body, name=name,
        grid_spec=pltpu.PrefetchScalarGridSpec(
            num_scalar_prefetch=1, grid=(rows // tr,), in_specs=[in_spec],
            out_specs=pl.BlockSpec((None, tr, cols), lambda i, m: (m[0], i, 0))),
        out_shape=jax.ShapeDtypeStruct((N_CHIPS, rows, cols), dtype),
        compiler_params=_params(("parallel",)),
    )(me_idx, shard)


def place_many(shards, dtypes, me_idx, name):
    def body(m_ref, *refs):
        for x_ref, o_ref in zip(refs[:len(shards)], refs[len(shards):]):
            o_ref[...] = x_ref[...].astype(o_ref.dtype)

    return _pcall(
        body, name=name,
        grid_spec=pltpu.PrefetchScalarGridSpec(
            num_scalar_prefetch=1, grid=(1,),
            in_specs=[pl.BlockSpec(s.shape, lambda i, m: (0, 0)) for s in shards],
            out_specs=[pl.BlockSpec((None,) + s.shape, lambda i, m: (m[0], 0, 0)) for s in shards]),
        out_shape=[jax.ShapeDtypeStruct((N_CHIPS,) + s.shape, d) for s, d in zip(shards, dtypes)],
        compiler_params=_params(("arbitrary",)),
    )(me_idx, *shards)


def put_rows(pack, rows, off):
    _, n, cols = rows.shape

    def body(r_ref, p_ref, o_ref, sem):
        cp = pltpu.make_async_copy(r_ref.at[0], o_ref.at[pl.program_id(0), pl.ds(off, n), :], sem)
        cp.start()
        cp.wait()

    return _pcall(body, name="grad_put_rows", grid=(N_CHIPS,),
                  in_specs=[pl.BlockSpec((1, n, cols), lambda k: (k, 0, 0)), _ANY], out_specs=_ANY,
                  out_shape=jax.ShapeDtypeStruct(pack.shape, pack.dtype), input_output_aliases={1: 0},
                  scratch_shapes=[pltpu.SemaphoreType.DMA],
                  compiler_params=_params(("arbitrary",)))(rows, pack)


def _my_cols(c, mine=True):
    start = (c if mine else 1 - c) * HALF_W
    return pl.ds(pl.multiple_of(start, HALF_W), HALF_W)


def add_halves(gpack, got, c_idx):
    n, rows, _ = gpack.shape
    tr = min(G_BLOCK_ROWS, rows)
    blk = (None, tr, HALF_W)

    def body(c_ref, g_ref, r_ref, o_ref):
        o_ref[...] = (g_ref[...] + r_ref[...]).astype(o_ref.dtype)

    return _pcall(
        body, name="grad_add_halves",
        grid_spec=pltpu.PrefetchScalarGridSpec(
            num_scalar_prefetch=1, grid=(n, rows // tr),
            in_specs=[pl.BlockSpec(blk, lambda k, i, c: (k, i, c[0])), pl.BlockSpec(blk, lambda k, i, c: (k, i, 0))],
            out_specs=pl.BlockSpec(blk, lambda k, i, c: (k, i, 0))),
        out_shape=jax.ShapeDtypeStruct((n, rows, HALF_W), BF16),
        compiler_params=_params(("parallel", "parallel")),
    )(c_idx, gpack, got)


def sum_owner(part, got, idx, total_rows, row_off=0, into=None):
    _, rows, _ = part.shape
    tr = min(G_BLOCK_ROWS, rows)
    n_into = 0 if into is None else 1

    def body(m_ref, p_ref, g_ref, *rest):
        up = lambda v: v.astype(F32)
        rest[-1][...] = ((up(p_ref[...]) + up(g_ref[0])) + up(g_ref[1])) + up(g_ref[2])

    return _pcall(
        body, name="grad_sum_owner",
        grid_spec=pltpu.PrefetchScalarGridSpec(
            num_scalar_prefetch=1, grid=(rows // tr,),
            in_specs=[pl.BlockSpec((None, tr, HALF_W), lambda i, m: (m[0], i, 0)),
                      pl.BlockSpec((3, tr, HALF_W), lambda i, m: (0, i, 0))] + [_ANY] * n_into,
            out_specs=pl.BlockSpec((tr, HALF_W), lambda i, m: (row_off // tr + i, m[1]))),
        out_shape=jax.ShapeDtypeStruct((total_rows, PACK_W), F32),
        input_output_aliases={3: 0} if n_into else {},
        compiler_params=_params(("parallel",)),
    )(idx, part, got, *([into] if n_into else []))


def join_halves(red):
    def body(in_ref, out_ref, send_sem, recv_sem):
        x, y, c, _ = _place()
        sibling = (x, y, 1 - c)
        mine = out_ref.at[:, _my_cols(c)]
        cp = pltpu.make_async_remote_copy(src_ref=mine, dst_ref=mine, send_sem=send_sem, recv_sem=recv_sem,
                                          device_id=sibling, device_id_type=MESH)
        cp.start()
        cp.wait_send()
        other = out_ref.at[:, _my_cols(c, mine=False)]
        pltpu.make_async_remote_copy(src_ref=other, dst_ref=other, send_sem=send_sem, recv_sem=recv_sem,
                                     device_id=sibling, device_id_type=MESH).wait_recv()

    return _pcall(body, name="grad_join_halves", in_specs=[_ANY], out_specs=_ANY,
                  out_shape=jax.ShapeDtypeStruct(red.shape, red.dtype), input_output_aliases={0: 0},
                  scratch_shapes=[pltpu.SemaphoreType.DMA, pltpu.SemaphoreType.DMA])(red)


def adamw(gsrc, g_off, wt, m, v, name):
    n, cols = wt.shape
    tr = math.gcd(math.gcd(g_off, n), 256) if g_off else math.gcd(n, 256)
    off_blk = g_off // tr
    c1 = 1.0 / (1.0 - ADAM_B1 ** ADAM_STEP)
    c2 = 1.0 / (1.0 - ADAM_B2 ** ADAM_STEP)

    def body(g_ref, w_ref, m_ref, v_ref, go_ref, d_ref, mo_ref, vo_ref):
        gv = g_ref[...]
        mn = ADAM_B1 * m_ref[...] + (1.0 - ADAM_B1) * gv
        vn = ADAM_B2 * v_ref[...] + (1.0 - ADAM_B2) * gv * gv
        go_ref[...] = gv
        mo_ref[...] = mn
        vo_ref[...] = vn
        d_ref[...] = -ADAM_LR * ((mn * c1) / (jnp.sqrt(vn * c2) + ADAM_EPS) + ADAM_WD * w_ref[...])

    blk = pl.BlockSpec((tr, cols), lambda i: (i, 0))
    return _pcall(body, name=name, grid=(n // tr,),
                  in_specs=[pl.BlockSpec((tr, cols), lambda i: (off_blk + i, 0)), blk, blk, blk],
                  out_specs=[blk] * 4, out_shape=[jax.ShapeDtypeStruct((n, cols), F32)] * 4,
                  compiler_params=_params(("parallel",)))(gsrc, wt, m, v)


def _rows8(a):
    return -(-a.size // (8 * PACK_W)) * 8


def _as_rows(a, rows=None):
    flat = a.reshape(-1)
    n = _rows8(a) if rows is None else rows
    return jnp.pad(flat, (0, n * PACK_W - flat.shape[0])).reshape(n, PACK_W)


def local_shards_2d(wl):
    return {"w_ff1": [wl["w_ff1"][0], wl["w_ff1"][1]], "w_ff2": [wl["w_ff2"][0], wl["w_ff2"][1]],
            "ssm_w_glu": wl["ssm_w_glu"], "ssm_w_out": wl["ssm_w_out"], "kv_w_a": _pad_kva_cols(wl["kv_w_a"]),
            "kv_w_b": wl["kv_w_b"], "q_w_a": wl["q_w_a"], "q_w_b": _perm_q_cols(wl["q_w_b"]),
            "attn_w_o": wl["attn_w_o"], "ssm_d": wl["ssm_d"].reshape(2, -1)}


def misc_grad_shard(name, g, k):
    if name == "ssm_d":
        w = D_MODEL // N_CHIPS
        return g[:, w * k:w * (k + 1)]
    if name in ("ssm_w_glu", "kv_w_b"):
        return g[k]
    if name == "q_w_b":
        return _unperm_q_cols(g[k])
    rows = D_MODEL // N_CHIPS
    shard = g[rows * k:rows * (k + 1)]
    return _unpad_kva_cols(shard) if name == "kv_w_a" else shard


def packed_shards(g, names, rows, tail=None):
    blocks = []
    for k in range(N_CHIPS):
        parts = [_as_rows(misc_grad_shard(n, g[n], k), MISC_SHARD_ROWS[n]) for n in names]
        if tail is not None:
            parts.append(tail[k * (tail.shape[0] // N_CHIPS):(k + 1) * (tail.shape[0] // N_CHIPS)])
        blk = jnp.concatenate(parts, axis=0)
        blocks.append(jnp.pad(blk, ((0, rows - blk.shape[0]), (0, 0))))
    return jnp.stack(blocks)


def kernel(x, positions, ln_mix_g, ln_mix_b, ln_ffn_g, ln_ffn_b, w_ff1, w_ff2, ssm_lam_re, ssm_lam_im, ssm_log_dt, ssm_b_re, ssm_b_im, ssm_c_re, ssm_c_im, ssm_d, ssm_w_glu, ssm_w_out, kv_w_a, kv_norm_g, kv_w_b, q_w_a, q_norm_g, q_w_b, attn_w_o, loss_target, m_ln_mix_g, m_ln_mix_b, m_ln_ffn_g, m_ln_ffn_b, m_w_ff1, m_w_ff2, m_ssm_lam_re, m_ssm_lam_im, m_ssm_log_dt, m_ssm_b_re, m_ssm_b_im, m_ssm_c_re, m_ssm_c_im, m_ssm_d, m_ssm_w_glu, m_ssm_w_out, m_kv_w_a, m_kv_norm_g, m_kv_w_b, m_q_w_a, m_q_norm_g, m_q_w_b, m_attn_w_o, v_ln_mix_g, v_ln_mix_b, v_ln_ffn_g, v_ln_ffn_b, v_w_ff1, v_w_ff2, v_ssm_lam_re, v_ssm_lam_im, v_ssm_log_dt, v_ssm_b_re, v_ssm_b_im, v_ssm_c_re, v_ssm_c_im, v_ssm_d, v_ssm_w_glu, v_ssm_w_out, v_kv_w_a, v_kv_norm_g, v_kv_w_b, v_q_w_a, v_q_norm_g, v_q_w_b, v_attn_w_o):
    env = dict(locals())
    wl = {n: env[n] for n in WEIGHTS}
    ml = {n: env["m_" + n] for n in WEIGHTS}
    vl = {n: env["v_" + n] for n in WEIGHTS}
    for n in ("ssm_w_glu", "ssm_w_out", "q_w_a", "q_w_b", "attn_w_o"):
        wl[n], ml[n], vl[n] = wl[n][0], ml[n][0], vl[n][0]

    c_idx = lax.axis_index("c").astype(jnp.int32).reshape(1)
    me_idx = (2 * lax.axis_index("x") + lax.axis_index("y")).astype(jnp.int32).reshape(1)

    local = local_shards_2d(wl)
    stacked = {n: [place(wl[n], me_idx, BF16, f"place_{n}_{l}", layer=l) for l in range(DEPTH)] for n in ("w_ff1", "w_ff2")}
    others = [n for n in SHARDED if n not in stacked]
    stacked.update(zip(others, place_many([local[n] for n in others], [F32 if n == "ssm_d" else BF16 for n in others],
                                          me_idx, "place_others")))
    stacked["ssm_d"] = ride_alone(GatherRide([_halves(stacked["ssm_d"])]), "ssm_d_all_gather")[0].reshape(1, D_MODEL)
    for n in REPLICATED:
        stacked[n] = wl[n]

    loss_part, dx, early, mid, g, sent = device_step(x[0], positions[0], loss_target[0], stacked, comm=(me_idx, c_idx))
    loss = lax.psum(loss_part, ("x", "y", "c"))

    small = jnp.concatenate([_as_rows(g[n]) for n in REPLICATED], axis=0)
    small = jnp.pad(small, ((0, SMALL_ROWS - small.shape[0]), (0, 0)))
    late = packed_shards(g, MISC_LATE, LATE_ROWS, tail=small)
    late_sums = add_halves(late, ride_alone(SwapRide(late), "grad_swap_halves")[0], c_idx)
    sent.append((late_sums, ride_alone(SendRide([late_sums]), "grad_send_to_owners")[0]))
    where = jnp.concatenate([me_idx, c_idx])
    starts = (0, EARLY_ROWS, EARLY_ROWS + MID_ROWS)
    total_rows = EARLY_ROWS + MID_ROWS + LATE_ROWS
    reduced = None
    for (sums, got), off in zip(sent, starts):
        reduced = sum_owner(sums, got, where, total_rows, row_off=off, into=reduced)
    reduced = join_halves(reduced)
    quarter = reduced[starts[2] + SMALL_OFF:starts[2] + SMALL_OFF + SMALL_Q_ROWS]
    small_tot = ride_alone(GatherRide([_halves(place(quarter, me_idx, F32, "place_small_grads"))]),
                           "small_grad_all_gather")[0].reshape(SMALL_ROWS, PACK_W)

    out_g, out_d, out_m, out_v = {}, {}, {}, {}
    direct = {**EARLY_OFF, **{n: starts[1] + o for n, o in MID_OFF.items()}}
    for n, off in direct.items():
        res = adamw(reduced, off, wl[n].reshape(-1, PACK_W), ml[n].reshape(-1, PACK_W), vl[n].reshape(-1, PACK_W),
                    "adamw_" + n)
        out_g[n], out_d[n], out_m[n], out_v[n] = [a.reshape(env[n].shape) for a in res]
    for names, off in ((MISC_EARLY, MISC_EARLY_OFF), (MISC_MID, starts[1] + MISC_MID_OFF), (MISC_LATE, starts[2])):
        pack3 = lambda d: jnp.concatenate([_as_rows(d[n], MISC_SHARD_ROWS[n]) for n in names], axis=0)
        res = adamw(reduced, off, pack3(wl), pack3(ml), pack3(vl), "adamw_packed_" + names[0])
        r0 = 0
        for n in names:
            cnt = math.prod(env[n].shape)
            out_g[n], out_d[n], out_m[n], out_v[n] = [
                a[r0:r0 + MISC_SHARD_ROWS[n]].reshape(-1)[:cnt].reshape(env[n].shape) for a in res]
            r0 += MISC_SHARD_ROWS[n]
    ws = jnp.concatenate([_as_rows(wl[n]) for n in REPLICATED], axis=0)
    ms = jnp.concatenate([_as_rows(ml[n]) for n in REPLICATED], axis=0)
    vs = jnp.concatenate([_as_rows(vl[n]) for n in REPLICATED], axis=0)
    pad = ((0, SMALL_ROWS - ws.shape[0]), (0, 0))
    res = adamw(small_tot, 0, jnp.pad(ws, pad), jnp.pad(ms, pad), jnp.pad(vs, pad), "adamw_replicated")
    row = 0
    for n in REPLICATED:
        cnt = math.prod(env[n].shape)
        nrows = _rows8(env[n])
        out_g[n], out_d[n], out_m[n], out_v[n] = [a[row:row + nrows].reshape(-1)[:cnt].reshape(env[n].shape) for a in res]
        row += nrows

    return (loss, dx[None], *[out_g[n] for n in WEIGHTS], *[out_d[n] for n in WEIGHTS],
            *[out_m[n] for n in WEIGHTS], *[out_v[n] for n in WEIGHTS])
```

```python
import functools
import math

import jax
import jax.numpy as jnp
from jax import lax
from jax.experimental import pallas as pl
from jax.experimental.pallas import tpu as pltpu

F32 = jnp.float32
BF16 = jnp.bfloat16
MESH = pl.DeviceIdType.MESH

D_MODEL = 1024
DEPTH = 2
SSM_GROUP = 16
N_GROUPS = D_MODEL // SSM_GROUP
SSM_STATE = 64
N_STATES = N_GROUPS * SSM_STATE
N_HEADS = 8
QK_NOPE = 128
QK_ROPE = 64
HALF_ROPE = QK_ROPE // 2
V_HEAD = 128
QK_DIM = QK_NOPE + QK_ROPE
Q_LORA = 384
KV_LORA = 256
ROPE_THETA = 10000.0
SM_SCALE = QK_DIM ** -0.5
NEG_INF = -1e30
D_FF = 4 * D_MODEL
DN_ALPHA = (2 * DEPTH) ** 0.25
LN_EPS = 1e-5
RMS_EPS = 1e-6
ADAM_LR = 0.001
ADAM_B1 = 0.9
ADAM_B2 = 0.999
ADAM_EPS = 1e-08
ADAM_WD = 0.01
ADAM_STEP = 10

N_CHIPS = 4
LANES = 128
VMEM_LIMIT = 56 * 1024 * 1024
MM_VMEM_BUDGET = 40 * 1024 * 1024
PACK_W = 1024
KVA_PAD = 384
HALF_W = PACK_W // 2

SHARDED = ("w_ff1", "w_ff2", "ssm_w_glu", "ssm_w_out", "kv_w_a", "kv_w_b", "q_w_a", "q_w_b", "attn_w_o", "ssm_d")
G_BLOCK_ROWS = 960
EARLY_OFF = {"w_ff1": 0, "w_ff2": 2048, "attn_w_o": 4096}
MISC_EARLY = ("kv_w_b", "kv_w_a", "q_w_a", "q_w_b")
MISC_EARLY_OFF = 4352
EARLY_ROWS = 5 * G_BLOCK_ROWS
MID_OFF = {"ssm_w_out": 0}
MISC_MID = ("ssm_w_glu",)
MISC_MID_OFF = 256
MID_ROWS = G_BLOCK_ROWS
MISC_LATE = ("ssm_d",)
SMALL_Q_ROWS = 96
SMALL_ROWS = N_CHIPS * SMALL_Q_ROWS
SMALL_OFF = 16
LATE_ROWS = 128
MISC_SHARD_ROWS = {"ssm_d": 16, "ssm_w_glu": 512, "kv_w_b": 128, "kv_w_a": 80, "q_w_a": 96, "q_w_b": 144}
REPLICATED = ("ln_mix_g", "ln_mix_b", "ln_ffn_g", "ln_ffn_b", "ssm_lam_re", "ssm_lam_im", "ssm_log_dt",
              "ssm_b_re", "ssm_b_im", "ssm_c_re", "ssm_c_im", "kv_norm_g", "q_norm_g")
WEIGHTS = ("ln_mix_g", "ln_mix_b", "ln_ffn_g", "ln_ffn_b", "w_ff1", "w_ff2", "ssm_lam_re", "ssm_lam_im",
           "ssm_log_dt", "ssm_b_re", "ssm_b_im", "ssm_c_re", "ssm_c_im", "ssm_d", "ssm_w_glu", "ssm_w_out",
           "kv_w_a", "kv_norm_g", "kv_w_b", "q_w_a", "q_norm_g", "q_w_b", "attn_w_o")


def _pcall(body, **kw):
    return pl.pallas_call(body, **kw)


def _params(sem=None):
    return pltpu.CompilerParams(dimension_semantics=sem, vmem_limit_bytes=VMEM_LIMIT)


_ANY = pl.BlockSpec(memory_space=pl.ANY)


def _tile(dim, prefs):
    for p in prefs:
        if dim % p == 0:
            return p
    return dim


def _place():
    x, y, c = lax.axis_index("x"), lax.axis_index("y"), lax.axis_index("c")
    return x, y, c, [(1 - x, y), (x, 1 - y), (1 - x, 1 - y)]


def _remote(k, src, dst, to, send_sems, recv_sems):
    return pltpu.make_async_remote_copy(src_ref=src, dst_ref=dst, send_sem=send_sems.at[k], recv_sem=recv_sems.at[k],
                                        device_id=to, device_id_type=MESH)


class GatherRide:
    def __init__(self, arrs):
        self.ins = list(arrs)
        self.out_shapes = [jax.ShapeDtypeStruct(a.shape, a.dtype) for a in arrs]
        self.aliases = {i: i for i in range(len(arrs))}
        self.n_sems = 6 * len(arrs)

    def start(self, ins, outs, send_sems, recv_sems):
        x, y, c, chips = _place()
        me = 2 * x + y
        for a, o in enumerate(outs):
            for j, (px, py) in enumerate(chips):
                _remote(6 * a + j, o.at[me, c], o.at[me, c], (px, py, c), send_sems, recv_sems).start()

    def pass_on(self, ins, outs, send_sems, recv_sems):
        x, y, c, chips = _place()
        for a, o in enumerate(outs):
            for j, (px, py) in enumerate(chips):
                blk = o.at[2 * px + py, c]
                _remote(6 * a + j, blk, blk, (px, py, c), send_sems, recv_sems).wait_recv()
                _remote(6 * a + 3 + j, blk, blk, (x, y, 1 - c), send_sems, recv_sems).start()

    def finish(self, ins, outs, send_sems, recv_sems, passed_on=False):
        if not passed_on:
            self.pass_on(ins, outs, send_sems, recv_sems)
        x, y, c, chips = _place()
        me = 2 * x + y
        sibling = (x, y, 1 - c)
        for a, o in enumerate(outs):
            for j, (px, py) in enumerate(chips):
                blk = o.at[2 * px + py, 1 - c]
                _remote(6 * a + 3 + j, blk, blk, sibling, send_sems, recv_sems).wait_recv()
                _remote(6 * a + j, o.at[me, c], o.at[me, c], (px, py, c), send_sems, recv_sems).wait_send()
                mine = o.at[2 * px + py, c]
                _remote(6 * a + 3 + j, mine, mine, sibling, send_sems, recv_sems).wait_send()


class SendRide:
    def __init__(self, parts):
        self.ins = list(parts)
        self.out_shapes = [jax.ShapeDtypeStruct((3,) + p.shape[1:], p.dtype) for p in parts]
        self.aliases = {}
        self.n_sems = 3 * len(parts)

    def _copies(self, ins, outs, send_sems, recv_sems):
        x, y, c, chips = _place()
        return [_remote(3 * a + j, ins[a].at[2 * px + py], outs[a].at[j], (px, py, c), send_sems, recv_sems)
                for a in range(len(ins)) for j, (px, py) in enumerate(chips)]

    def start(self, ins, outs, send_sems, recv_sems):
        for cp in self._copies(ins, outs, send_sems, recv_sems):
            cp.start()

    def finish(self, ins, outs, send_sems, recv_sems):
        for cp in self._copies(ins, outs, send_sems, recv_sems):
            cp.wait()


class SwapRide:
    def __init__(self, pack, ranges=None, into=None):
        self.ins = [pack] if into is None else [pack, into]
        self.out_shapes = [jax.ShapeDtypeStruct(pack.shape[:2] + (HALF_W,), pack.dtype)]
        self.aliases = {} if into is None else {1: 0}
        self.ranges = ranges or [(0, pack.shape[1])]
        self.n_sems = len(self.ranges)

    def _copies(self, ins, outs, send_sems, recv_sems):
        x, y, c, _ = _place()
        return [_remote(k, ins[0].at[:, pl.ds(r0, n), _my_cols(c, mine=False)], outs[0].at[:, pl.ds(r0, n), :],
                        (x, y, 1 - c), send_sems, recv_sems) for k, (r0, n) in enumerate(self.ranges)]

    def start(self, ins, outs, send_sems, recv_sems):
        for cp in self._copies(ins, outs, send_sems, recv_sems):
            cp.start()

    def finish(self, ins, outs, send_sems, recv_sems):
        for cp in self._copies(ins, outs, send_sems, recv_sems):
            cp.wait()


def _pcall_riding(body, args, ride, first, last, *, in_specs, out_specs, out_shape, scratch_shapes=(), middle=None,
                  **kw):
    n_in, n_out = len(args), len(out_shape)
    if ride is None:
        return _pcall(body, in_specs=in_specs, out_specs=out_specs, out_shape=out_shape,
                      scratch_shapes=list(scratch_shapes), **kw)(*args), []
    k_in, k_out = len(ride.ins), len(ride.out_shapes)

    def riding(*refs):
        ins, r_in = refs[:n_in], refs[n_in:n_in + k_in]
        outs = refs[n_in + k_in:n_in + k_in + n_out]
        r_out = refs[n_in + k_in + n_out:n_in + k_in + n_out + k_out]
        scratch, (send_sems, recv_sems) = refs[n_in + k_in + n_out + k_out:-2], refs[-2:]

        @pl.when(first())
        def _():
            ride.start(r_in, r_out, send_sems, recv_sems)

        if middle is not None:
            @pl.when(middle())
            def _():
                ride.pass_on(r_in, r_out, send_sems, recv_sems)

        body(*ins, *outs, *scratch)

        @pl.when(last())
        def _():
            if middle is not None:
                ride.finish(r_in, r_out, send_sems, recv_sems, passed_on=True)
            else:
                ride.finish(r_in, r_out, send_sems, recv_sems)

    res = _pcall(riding, in_specs=list(in_specs) + [_ANY] * k_in, out_specs=list(out_specs) + [_ANY] * k_out,
                 out_shape=list(out_shape) + ride.out_shapes,
                 input_output_aliases={n_in + i: n_out + o for i, o in ride.aliases.items()},
                 scratch_shapes=list(scratch_shapes) + [pltpu.SemaphoreType.DMA((ride.n_sems,))] * 2,
                 **kw)(*args, *ride.ins)
    return res[:n_out], res[n_out:]


def ride_alone(ride, name):
    def body(*refs):
        n = len(ride.ins)
        ins, outs, (send_sems, recv_sems) = refs[:n], refs[n:-2], refs[-2:]
        ride.start(ins, outs, send_sems, recv_sems)
        ride.finish(ins, outs, send_sems, recv_sems)

    return _pcall(body, name=name, in_specs=[_ANY] * len(ride.ins), out_specs=[_ANY] * len(ride.out_shapes),
                  out_shape=ride.out_shapes, input_output_aliases=dict(ride.aliases),
                  scratch_shapes=[pltpu.SemaphoreType.DMA((ride.n_sems,))] * 2)(*ride.ins)


def mm(a, b, *, name, ta=False, tb=False, pro_a=None, epi=None, extras=(), out_dtypes=(F32,), n_dim=None,
       tiles=(None, None, None), b_view=None, out_view=None, into=None, ride=None):
    if ta:
        k_dim, m_dim = a.shape
    else:
        m_dim, k_dim = a.shape
    if n_dim is None:
        n_dim = b.shape[0] if tb else b.shape[1]
    tn = tiles[1] or (n_dim if n_dim <= 1024 else _tile(n_dim, (1024, 512, 256, 128)))
    tk = tiles[2] or (k_dim if k_dim <= 1024 else _tile(k_dim, (1024, 512, 256, 128)))
    nk = k_dim // tk

    def vmem_bytes(tm):
        blocks = tm * tk * a.dtype.itemsize + tk * tn * b.dtype.itemsize
        blocks += tm * tn * (sum(e.dtype.itemsize for e in extras) + sum(jnp.dtype(d).itemsize for d in out_dtypes))
        return 2 * blocks + tm * tn * 4

    tm = tiles[0] or next((t for t in (4096, 2048, 1024, 512, 256) if m_dim % t == 0 and vmem_bytes(t) <= MM_VMEM_BUDGET),
                          _tile(m_dim, (128,)))
    assert m_dim % tm == 0 and n_dim % tn == 0 and k_dim % tk == 0, (name, m_dim, n_dim, k_dim, tm, tn, tk)
    n_ex, n_out = len(extras), len(out_dtypes)
    n_into = 0 if into is None else 1
    dims = (((0 if ta else 1,), (1 if tb else 0,)), ((), ()))

    def body(a_ref, b_ref, *rest):
        ex_refs, out_refs = rest[:n_ex], rest[n_ex + n_into:n_ex + n_into + n_out]

        def partial():
            av = a_ref[...]
            if pro_a is not None:
                av = pro_a(av)
            return lax.dot_general(av.astype(BF16), b_ref[...].astype(BF16), dims, preferred_element_type=F32)

        def finish(r):
            res = epi(r, *[e[...] for e in ex_refs]) if epi is not None else (r,)
            for o_ref, v in zip(out_refs, res):
                o_ref[...] = v.astype(o_ref.dtype)

        if nk == 1:
            finish(partial())
            return
        acc = rest[-1]
        k = pl.program_id(2)

        @pl.when(k == 0)
        def _():
            acc[...] = partial()

        @pl.when(k > 0)
        def _():
            acc[...] += partial()

        @pl.when(k == nk - 1)
        def _():
            finish(acc[...])

    def ex_spec(e):
        if e.shape == (m_dim, n_dim):
            return o_spec
        if e.shape[0] == m_dim:
            return pl.BlockSpec((tm, e.shape[1]), lambda i, j, k: (i, 0))
        return pl.BlockSpec(e.shape, lambda i, j, k: (0, 0))

    a_spec = pl.BlockSpec((tk, tm), lambda i, j, k: (k, i)) if ta else pl.BlockSpec((tm, tk), lambda i, j, k: (i, k))
    if b_view is not None:
        b_spec = b_view(tk, tn)
    else:
        b_spec = pl.BlockSpec((tn, tk), lambda i, j, k: (j, k)) if tb else pl.BlockSpec((tk, tn), lambda i, j, k: (k, j))
    o_spec = pl.BlockSpec((tm, tn), lambda i, j, k: (i, j))
    if out_view is None:
        out_specs = [o_spec] * n_out
        out_shape = [jax.ShapeDtypeStruct((m_dim, n_dim), dt) for dt in out_dtypes]
    else:
        assert n_out == 1
        out_specs = [out_view[1](tm, tn)]
        out_shape = [jax.ShapeDtypeStruct(out_view[0], out_dtypes[0])]
    grid = (m_dim // tm, n_dim // tn, nk)
    scratch = [pltpu.VMEM((tm, tn), F32)] if nk > 1 else []
    if ride is not None:
        assert into is None
        at = lambda ids: functools.reduce(jnp.logical_and, [pl.program_id(d) == i for d, i in enumerate(ids)])
        outs, landed = _pcall_riding(
            body, (a, b, *extras), ride, lambda: at((0, 0, 0)), lambda: at([g - 1 for g in grid]),
            name=name, grid=grid, in_specs=[a_spec, b_spec] + [ex_spec(e) for e in extras], out_specs=out_specs,
            out_shape=out_shape, scratch_shapes=scratch, compiler_params=_params(("arbitrary",) * 3))
        return (outs[0] if n_out == 1 else outs), landed
    outs = _pcall(
        body, name=name, grid=grid,
        in_specs=[a_spec, b_spec] + [ex_spec(e) for e in extras] + [_ANY] * n_into,
        out_specs=out_specs, out_shape=out_shape,
        input_output_aliases={2 + n_ex: 0} if n_into else {},
        scratch_shapes=scratch,
        compiler_params=_params(("parallel", "parallel", "arbitrary")),
    )(a, b, *extras, *([into] if n_into else []))
    return outs[0] if n_out == 1 else outs


def rowwise(fn, ins, outs, *, name, accs=(), tm=256):
    rows = ins[0].shape[0]
    tm = min(tm, rows)
    n_in, n_out, n_acc = len(ins), len(outs), len(accs)

    def body(*refs):
        in_refs, out_refs, acc_refs = refs[:n_in], refs[n_in:n_in + n_out], refs[n_in + n_out:]
        res, sums = fn(*[r[...] for r in in_refs])
        for o_ref, v in zip(out_refs, res):
            o_ref[...] = v.astype(o_ref.dtype)
        if n_acc:
            @pl.when(pl.program_id(0) == 0)
            def _():
                for a_ref in acc_refs:
                    a_ref[...] = jnp.zeros_like(a_ref)

            for a_ref, s in zip(acc_refs, sums):
                a_ref[...] += s

    def spec(arr):
        if arr.shape[0] == rows:
            return pl.BlockSpec((tm, arr.shape[1]), lambda i: (i, 0))
        return pl.BlockSpec(arr.shape, lambda i: (0, 0))

    res = _pcall(
        body, name=name, grid=(rows // tm,),
        in_specs=[spec(a) for a in ins],
        out_specs=[pl.BlockSpec((tm, w), lambda i: (i, 0)) for w, _ in outs]
        + [pl.BlockSpec((1, w), lambda i: (0, 0)) for w in accs],
        out_shape=[jax.ShapeDtypeStruct((rows, w), dt) for w, dt in outs]
        + [jax.ShapeDtypeStruct((1, w), F32) for w in accs],
        compiler_params=_params(("arbitrary",) if n_acc else ("parallel",)),
    )(*ins)
    return res


def _relu2(v):
    r = jnp.maximum(v, 0.0)
    return r * r


def _gelu(x):
    c = math.sqrt(2.0 / math.pi)
    return 0.5 * x * (1.0 + jnp.tanh(c * (x + 0.044715 * x * x * x)))


def _gelu_grad(x):
    c = math.sqrt(2.0 / math.pi)
    t = jnp.tanh(c * (x + 0.044715 * x * x * x))
    return 0.5 * (1.0 + t) + 0.5 * x * (1.0 - t * t) * c * (1.0 + 3 * 0.044715 * x * x)


def _sigmoid(x):
    return 1.0 / (1.0 + jnp.exp(-x))


def _layer_norm(h, mix, g, b):
    r = DN_ALPHA * h + mix
    mu = jnp.mean(r, axis=-1, keepdims=True)
    xc = r - mu
    var = jnp.mean(xc * xc, axis=-1, keepdims=True)
    return xc * lax.rsqrt(var + LN_EPS) * g + b


def ln_fwd(h, mix, g, b, name):
    def fn(h, mix, g, b):
        y = _layer_norm(h, mix, g, b)
        return (y, y), ()
    return rowwise(fn, (h, mix, g, b), ((D_MODEL, F32), (D_MODEL, BF16)), name=name)


def ln_bwd(h, mix, g, dy, name):
    def fn(h, mix, g, dy):
        r = DN_ALPHA * h + mix
        mu = jnp.mean(r, axis=-1, keepdims=True)
        xc = r - mu
        var = jnp.mean(xc * xc, axis=-1, keepdims=True)
        rstd = lax.rsqrt(var + LN_EPS)
        xhat = xc * rstd
        dxh = dy * g
        m1 = jnp.mean(dxh, axis=-1, keepdims=True)
        m2 = jnp.mean(dxh * xhat, axis=-1, keepdims=True)
        dr = rstd * (dxh - m1 - xhat * m2)
        return (dr, dr), (jnp.sum(dy * xhat, axis=0, keepdims=True), jnp.sum(dy, axis=0, keepdims=True))
    return rowwise(fn, (h, mix, g, dy), ((D_MODEL, F32), (D_MODEL, BF16)), accs=(D_MODEL, D_MODEL), name=name)


def _rms(x, g):
    r = lax.rsqrt(jnp.mean(x * x, axis=-1, keepdims=True) + RMS_EPS)
    return x * r * g


def _rms_bwd(x, g, dy):
    r = lax.rsqrt(jnp.mean(x * x, axis=-1, keepdims=True) + RMS_EPS)
    xn = x * r
    dyg = dy * g
    dx = r * (dyg - xn * jnp.mean(dyg * xn, axis=-1, keepdims=True))
    return dx, jnp.sum(dy * xn, axis=0, keepdims=True)


def _s5_disc(lr, li, ldt):
    dt = jnp.exp(ldt)
    mag = jnp.exp(lr * dt)
    cs, sn = jnp.cos(li * dt), jnp.sin(li * dt)
    ar, ai = mag * cs, mag * sn
    inv = 1.0 / (lr * lr + li * li)
    n_re = (ar - 1.0) * lr + ai * li
    n_im = ai * lr - (ar - 1.0) * li
    return dt, mag, cs, sn, ar, ai, inv, n_re, n_im


def s5_prep(lr, li, ldt, b_re, b_im):
    def fn(lr, li, ldt, b_re, b_im):
        _, _, _, _, ar, ai, inv, n_re, n_im = _s5_disc(lr, li, ldt)
        cr, ci = n_re * inv, n_im * inv
        return (ar, ai, cr * b_re - ci * b_im, cr * b_im + ci * b_re), ()
    return rowwise(fn, (lr, li, ldt, b_re, b_im), ((1, F32), (1, F32), (SSM_GROUP, F32), (SSM_GROUP, F32)),
                   name="s5_prep", tm=512)


def s5_prep_bwd(lr, li, ldt, b_re, b_im, dar, dai, dbb_re, dbb_im):
    def fn(lr, li, ldt, b_re, b_im, dar, dai, dbb_re, dbb_im):
        dt, mag, cs, sn, ar, ai, inv, n_re, n_im = _s5_disc(lr, li, ldt)
        cr, ci = n_re * inv, n_im * inv
        db_re = cr * dbb_re + ci * dbb_im
        db_im = cr * dbb_im - ci * dbb_re
        dcr = jnp.sum(dbb_re * b_re + dbb_im * b_im, axis=-1, keepdims=True)
        dci = jnp.sum(dbb_im * b_re - dbb_re * b_im, axis=-1, keepdims=True)
        dar = dar + (dcr * lr - dci * li) * inv
        dai = dai + (dcr * li + dci * lr) * inv
        dinv = dcr * n_re + dci * n_im
        dlr = (dcr * (ar - 1.0) + dci * ai) * inv - 2.0 * lr * inv * inv * dinv
        dli = (dcr * ai - dci * (ar - 1.0)) * inv - 2.0 * li * inv * inv * dinv
        dmag = dar * cs + dai * sn
        dth = dai * ar - dar * ai
        dlr = dlr + dmag * mag * dt
        dli = dli + dth * dt
        ddt = dmag * mag * lr + dth * li
        return (dlr, dli, ddt * dt, db_re, db_im), ()
    return rowwise(fn, (lr, li, ldt, b_re, b_im, dar, dai, dbb_re, dbb_im),
                   ((1, F32), (1, F32), (1, F32), (SSM_GROUP, F32), (SSM_GROUP, F32)), name="s5_prep_bwd", tm=512)


def group_sum(x):
    def body(x_ref, o_ref):
        o_ref[...] = jnp.sum(x_ref[...], axis=1)
    return _pcall(body, name="s5_group_sum", out_shape=jax.ShapeDtypeStruct((N_GROUPS, 1), F32))(
        x.reshape(N_GROUPS, SSM_STATE, 1))


GROUPS_PER_TILE = LANES // SSM_GROUP
TILE_STATES = GROUPS_PER_TILE * SSM_STATE
N_UTILES = D_MODEL // LANES


SUBLANES = 8
SCAN_STRIP = 1024
N_STRIPS = N_STATES // SCAN_STRIP
_NT = (((1,), (1,)), ((), ()))
_TN = (((0,), (0,)), ((), ()))


def _scan_coefs(are, aim, shifted, reverse):
    ar = are[...]
    ai = -aim[...] if reverse else aim[...]
    powers = {1: (ar, ai)}
    for d in (2, 4):
        r, i = powers[d // 2]
        powers[d] = (r * r - i * i, 2.0 * r * i)
    rid = lax.broadcasted_iota(jnp.int32, (SUBLANES, N_STATES), 0)
    first = (rid == SUBLANES - 1) if reverse else (rid == 0)
    masks = [(1, first)] + [(d, (rid <= SUBLANES - 1 - d) if reverse else (rid >= d)) for d in (1, 2, 4)]
    for n, (d, keep) in enumerate(masks):
        for part in (0, 1):
            shifted[2 * n + part][...] = jnp.where(keep, jnp.broadcast_to(powers[d][part], (SUBLANES, N_STATES)), 0.0)


def _tile_scan(xr, xi, shifted, nbr_re, nbr_im, reverse):
    for n, d in enumerate((1, 1, 2, 4)):
        by = SUBLANES - d if reverse else d
        fr, fi = (nbr_re, nbr_im) if n == 0 else (xr, xi)
        sr, si = pltpu.roll(fr, by, 0), pltpu.roll(fi, by, 0)
        kr, ki = shifted[2 * n], shifted[2 * n + 1]
        xr, xi = xr + kr * sr - ki * si, xi + kr * si + ki * sr
    return xr, xi


def _tile_rows(t):
    return pl.ds(pl.multiple_of(t * SUBLANES, SUBLANES), SUBLANES)


def s5_fwd(u, bbd_re, bbd_im, cbd_re, cbd_imn, a_re, a_im, dskip, ride=None, t_rows=256):
    seq = u.shape[0]
    t_rows = min(t_rows, seq)
    n_tiles = t_rows // SUBLANES

    def body(u_ref, bre, bim, cre, cimn, are, aim, d_ref, y_ref, gelu_ref, hre_ref, him_ref, car_re, car_im, *shifted):
        @pl.when(pl.program_id(0) == 0)
        def _():
            car_re[...] = jnp.zeros_like(car_re)
            car_im[...] = jnp.zeros_like(car_im)
            _scan_coefs(are, aim, shifted, reverse=False)

        uf = u_ref[...]
        ub = uf.astype(BF16)
        for j in range(N_UTILES):
            uj = ub[:, LANES * j:LANES * (j + 1)]
            sl = slice(TILE_STATES * j, TILE_STATES * (j + 1))
            hre_ref[:, sl] = jnp.dot(uj, bre[j], preferred_element_type=F32)
            him_ref[:, sl] = jnp.dot(uj, bim[j], preferred_element_type=F32)
        for s in range(N_STRIPS):
            cols = pl.ds(s * SCAN_STRIP, SCAN_STRIP)
            coefs = [c[:, cols] for c in shifted]

            def step(t, before):
                rows = _tile_rows(t)
                hr, hi = _tile_scan(hre_ref[rows, cols], him_ref[rows, cols], coefs, before[0], before[1], False)
                hre_ref[rows, cols] = hr
                him_ref[rows, cols] = hi
                return hr, hi

            cr, ci = lax.fori_loop(0, n_tiles, step, (car_re[:, cols], car_im[:, cols]))
            car_re[:, cols] = cr
            car_im[:, cols] = ci
        dv = d_ref[...]
        for j in range(N_UTILES):
            st = slice(TILE_STATES * j, TILE_STATES * (j + 1))
            yj = (jnp.dot(hre_ref[:, st].astype(BF16), cre[j], preferred_element_type=F32)
                  + jnp.dot(him_ref[:, st].astype(BF16), cimn[j], preferred_element_type=F32))
            sl = slice(LANES * j, LANES * (j + 1))
            yj = yj + dv[:, sl] * uf[:, sl]
            y_ref[:, sl] = yj
            gelu_ref[:, sl] = _gelu(yj).astype(gelu_ref.dtype)

    full3 = lambda a: pl.BlockSpec(a.shape, lambda i: (0, 0, 0))
    full2 = lambda a: pl.BlockSpec(a.shape, lambda i: (0, 0))
    tile = pltpu.VMEM((SUBLANES, N_STATES), F32)
    n_chunks = seq // t_rows
    return _pcall_riding(
        body, (u, bbd_re, bbd_im, cbd_re, cbd_imn, a_re, a_im, dskip), ride,
        lambda: pl.program_id(0) == 0, lambda: pl.program_id(0) == n_chunks - 1,
        middle=(lambda: pl.program_id(0) == (7 * n_chunks) // 8) if ride is not None else None,
        name="s5_fwd", grid=(n_chunks,),
        in_specs=[pl.BlockSpec((t_rows, D_MODEL), lambda i: (i, 0)), full3(bbd_re), full3(bbd_im), full3(cbd_re),
                  full3(cbd_imn), full2(a_re), full2(a_im), full2(dskip)],
        out_specs=[pl.BlockSpec((t_rows, D_MODEL), lambda i: (i, 0)),
                   pl.BlockSpec((t_rows, D_MODEL), lambda i: (i, 0)),
                   pl.BlockSpec((t_rows, N_STATES), lambda i: (i, 0)),
                   pl.BlockSpec((t_rows, N_STATES), lambda i: (i, 0))],
        out_shape=[jax.ShapeDtypeStruct((seq, D_MODEL), F32),
                   jax.ShapeDtypeStruct((seq, D_MODEL), BF16),
                   jax.ShapeDtypeStruct((seq, N_STATES), F32),
                   jax.ShapeDtypeStruct((seq, N_STATES), F32)],
        scratch_shapes=[tile] * 10,
        compiler_params=_params(("arbitrary",)))


def s5_bwd(dy, u, dres, h_re, h_im, bbd_re, bbd_im, cbd_re, cbd_imn, a_re, a_im, dskip, ride=None, t_rows=256):
    seq = u.shape[0]
    t_rows = min(t_rows, seq)
    n_chunks = seq // t_rows

    n_tiles = t_rows // SUBLANES

    def body(dy_ref, u_ref, dres_ref, hre_ref, him_ref, hpre_ref, hpim_ref, bre, bim, cre, cimn, are, aim, d_ref,
             dx_ref, dbre, dbim, dcre, dcimn, dar_ref, dai_ref, dd_ref, lre, lim, car_re, car_im, acc_re, acc_im,
             *shifted):
        i = pl.program_id(0)

        @pl.when(i == 0)
        def _():
            for r in (car_re, car_im, acc_re, acc_im, dbre, dbim, dcre, dcimn, dd_ref):
                r[...] = jnp.zeros_like(r)
            _scan_coefs(are, aim, shifted, reverse=True)

        dyf = dy_ref[...]
        dyb = dyf.astype(BF16)
        uf = u_ref[...]
        ub = uf.astype(BF16)
        for j in range(N_UTILES):
            dyj = dyb[:, LANES * j:LANES * (j + 1)]
            st = slice(TILE_STATES * j, TILE_STATES * (j + 1))
            lre[:, st] = lax.dot_general(dyj, cre[j], _NT, preferred_element_type=F32)
            lim[:, st] = lax.dot_general(dyj, cimn[j], _NT, preferred_element_type=F32)
        has_pred = (i < n_chunks - 1).astype(F32)
        last_row = lax.broadcasted_iota(jnp.int32, (SUBLANES, SCAN_STRIP), 0) == SUBLANES - 1
        for s in range(N_STRIPS):
            cols = pl.ds(s * SCAN_STRIP, SCAN_STRIP)
            coefs = [c[:, cols] for c in shifted]
            before_re, before_im = hpre_ref[:, cols] * has_pred, hpim_ref[:, cols] * has_pred

            def step(k, carry):
                after_re, after_im, dar, dai = carry
                t = n_tiles - 1 - k
                rows = _tile_rows(t)
                lr, li = _tile_scan(lre[rows, cols], lim[rows, cols], coefs, after_re, after_im, True)
                lre[rows, cols] = lr
                lim[rows, cols] = li
                prev = _tile_rows(jnp.maximum(t - 1, 0))
                pre_re = jnp.where(t == 0, before_re, hre_ref[prev, cols])
                pre_im = jnp.where(t == 0, before_im, him_ref[prev, cols])
                hpr = pltpu.roll(jnp.where(last_row, pre_re, hre_ref[rows, cols]), 1, 0)
                hpi = pltpu.roll(jnp.where(last_row, pre_im, him_ref[rows, cols]), 1, 0)
                return lr, li, dar + lr * hpr + li * hpi, dai + li * hpr - lr * hpi

            cr, ci, dar, dai = lax.fori_loop(0, n_tiles, step, (car_re[:, cols], car_im[:, cols],
                                                               acc_re[:, cols], acc_im[:, cols]))
            car_re[:, cols] = cr
            car_im[:, cols] = ci
            acc_re[:, cols] = dar
            acc_im[:, cols] = dai

        dv = d_ref[...]
        for j in range(N_UTILES):
            sl = slice(LANES * j, LANES * (j + 1))
            st = slice(TILE_STATES * j, TILE_STATES * (j + 1))
            lrj = lre[:, st].astype(BF16)
            lij = lim[:, st].astype(BF16)
            du = (lax.dot_general(lrj, bre[j], _NT, preferred_element_type=F32)
                  + lax.dot_general(lij, bim[j], _NT, preferred_element_type=F32))
            dx_ref[:, sl] = du + dv[:, sl] * dyf[:, sl] + DN_ALPHA * dres_ref[:, sl]
            uj = ub[:, sl]
            dbre[j] += lax.dot_general(uj, lrj, _TN, preferred_element_type=F32)
            dbim[j] += lax.dot_general(uj, lij, _TN, preferred_element_type=F32)
            dyj = dyb[:, sl]
            dcre[j] += lax.dot_general(hre_ref[:, st].astype(BF16), dyj, _TN, preferred_element_type=F32)
            dcimn[j] += lax.dot_general(him_ref[:, st].astype(BF16), dyj, _TN, preferred_element_type=F32)
        dd_ref[...] += jnp.sum(dyf * uf, axis=0, keepdims=True)

        @pl.when(i == n_chunks - 1)
        def _():
            dar_ref[...] = jnp.sum(acc_re[...], axis=0, keepdims=True)
            dai_ref[...] = jnp.sum(acc_im[...], axis=0, keepdims=True)

    rev = lambda i: (n_chunks - 1 - i, 0)
    prev_tile = lambda i: (jnp.maximum((n_chunks - 1 - i) * n_tiles - 1, 0), 0)
    once = pl.Buffered(1)
    full3 = lambda a: pl.BlockSpec(a.shape, lambda i: (0, 0, 0), pipeline_mode=once)
    full2 = lambda a: pl.BlockSpec(a.shape, lambda i: (0, 0), pipeline_mode=once)
    acc3 = lambda shape: pl.BlockSpec(shape, lambda i: (0, 0, 0))
    acc2 = lambda shape: pl.BlockSpec(shape, lambda i: (0, 0))
    tile = pltpu.VMEM((SUBLANES, N_STATES), F32)
    return _pcall_riding(
        body, (dy, u, dres, h_re, h_im, h_re, h_im, bbd_re, bbd_im, cbd_re, cbd_imn, a_re, a_im, dskip), ride,
        lambda: pl.program_id(0) == 0, lambda: pl.program_id(0) == n_chunks - 1,
        name="s5_bwd", grid=(n_chunks,),
        in_specs=[pl.BlockSpec((t_rows, D_MODEL), rev), pl.BlockSpec((t_rows, D_MODEL), rev),
                  pl.BlockSpec((t_rows, D_MODEL), rev),
                  pl.BlockSpec((t_rows, N_STATES), rev), pl.BlockSpec((t_rows, N_STATES), rev),
                  pl.BlockSpec((SUBLANES, N_STATES), prev_tile), pl.BlockSpec((SUBLANES, N_STATES), prev_tile),
                  full3(bbd_re), full3(bbd_im), full3(cbd_re), full3(cbd_imn), full2(a_re), full2(a_im), full2(dskip)],
        out_specs=[pl.BlockSpec((t_rows, D_MODEL), rev), acc3(bbd_re.shape), acc3(bbd_im.shape), acc3(cbd_re.shape),
                   acc3(cbd_imn.shape), acc2((1, N_STATES)), acc2((1, N_STATES)), acc2((1, D_MODEL))],
        out_shape=[jax.ShapeDtypeStruct((seq, D_MODEL), F32), jax.ShapeDtypeStruct(bbd_re.shape, F32),
                   jax.ShapeDtypeStruct(bbd_im.shape, F32), jax.ShapeDtypeStruct(cbd_re.shape, F32),
                   jax.ShapeDtypeStruct(cbd_imn.shape, F32), jax.ShapeDtypeStruct((1, N_STATES), F32),
                   jax.ShapeDtypeStruct((1, N_STATES), F32), jax.ShapeDtypeStruct((1, D_MODEL), F32)],
        scratch_shapes=[pltpu.VMEM((t_rows, N_STATES), F32), pltpu.VMEM((t_rows, N_STATES), F32)] + [tile] * 12,
        compiler_params=_params(("arbitrary",)))


def _eye_groups():
    return jnp.eye(GROUPS_PER_TILE, dtype=F32)


def _blockdiag_in(bb):
    t = bb.transpose(0, 2, 1).reshape(N_UTILES, GROUPS_PER_TILE, SSM_GROUP, SSM_STATE)
    bd = jnp.einsum("jgcp,gh->jgchp", t, _eye_groups())
    return bd.reshape(N_UTILES, LANES, TILE_STATES)


def _blockdiag_in_t(d):
    t = jnp.einsum("jgchp,gh->jgcp", d.reshape(N_UTILES, GROUPS_PER_TILE, SSM_GROUP, GROUPS_PER_TILE, SSM_STATE),
                   _eye_groups())
    return t.reshape(N_GROUPS, SSM_GROUP, SSM_STATE).transpose(0, 2, 1)


def _blockdiag_out(c):
    t = c.transpose(0, 2, 1).reshape(N_UTILES, GROUPS_PER_TILE, SSM_STATE, SSM_GROUP)
    bd = jnp.einsum("jhpc,hg->jhpgc", t, _eye_groups())
    return bd.reshape(N_UTILES, TILE_STATES, LANES)


def _blockdiag_out_t(d):
    t = jnp.einsum("jhpgc,hg->jhpc", d.reshape(N_UTILES, GROUPS_PER_TILE, SSM_STATE, GROUPS_PER_TILE, SSM_GROUP),
                   _eye_groups())
    return t.reshape(N_GROUPS, SSM_STATE, SSM_GROUP).transpose(0, 2, 1)


ATT_TQ = 512
ATT_TK = 512
LOG2E = math.log2(math.e)
LN2 = math.log(2.0)
Q_PRESCALE = SM_SCALE * LOG2E


def _loop_in_pairs(n, step, carry, start=0):
    pairs = (n - start) // 2

    def two(t, c):
        return step(start + 2 * t + 1, step(start + 2 * t, c))

    carry = lax.fori_loop(0, pairs, two, carry)
    return lax.fori_loop(start + 2 * pairs, n, step, carry)


def _causal(s, transposed=False):
    r = lax.broadcasted_iota(jnp.int32, s.shape, 0)
    c = lax.broadcasted_iota(jnp.int32, s.shape, 1)
    return jnp.where((r <= c) if transposed else (c <= r), s, NEG_INF)


def _q_specs(rows, at):
    def nope(*ids):
        r, h = at(*ids)
        return r, 3 * (h // HEADS_PER_CHIP) + h % HEADS_PER_CHIP

    def rope(*ids):
        r, h = at(*ids)
        return r, 3 * (h // HEADS_PER_CHIP) + HEADS_PER_CHIP

    return [pl.BlockSpec((rows, LANES), nope), pl.BlockSpec((rows, LANES), rope)]


def _kv_specs(rows, at):
    def col(f):
        def index(*ids):
            r, h = at(*ids)
            return r, f(h)
        return index

    return [pl.BlockSpec((rows, LANES), col(lambda h: 2 * h)), pl.BlockSpec((rows, LANES), col(lambda h: h % HEADS_PER_CHIP)),
            pl.BlockSpec((rows, LANES), col(lambda h: 2 * h + 1))]


def _cat(a, b):
    return jnp.concatenate([a, b], axis=1)


def attn_fwd(q, kv, kr, ride=None, tq=ATT_TQ, tk=ATT_TK):
    seq = q.shape[0]
    n_heads = N_HEADS
    tq, tk = min(tq, seq), min(tk, seq)
    assert tq == tk

    def body(qn_ref, qr_ref, kn_ref, kr_ref, v_ref, o_ref, lse_ref):
        qi = pl.program_id(1)
        qv = _cat(qn_ref[...], qr_ref[...])
        jd = qi

        def block(j, carry, diag):
            m, l, acc = carry
            rows = pl.ds(pl.multiple_of(j * tk, tk), tk)
            s = lax.dot_general(qv, _cat(kn_ref[rows, :], kr_ref[rows, :]), _NT, preferred_element_type=F32)
            if diag:
                s = _causal(s)
            m_new = jnp.maximum(m, jnp.max(s, axis=-1, keepdims=True))
            p = jnp.exp2(s - m_new)
            corr = jnp.exp2(m - m_new)
            l = l * corr + jnp.sum(p, axis=-1, keepdims=True)
            acc = acc * corr + jnp.dot(p.astype(BF16), v_ref[rows, :], preferred_element_type=F32)
            return m_new, l, acc

        init = (jnp.full((tq, 1), NEG_INF, F32), jnp.zeros((tq, 1), F32), jnp.zeros((tq, V_HEAD), F32))
        carry = _loop_in_pairs(jd, lambda j, c: block(j, c, False), init)
        m, l, acc = block(jd, carry, True)
        o_ref[...] = acc / l
        lse_ref[...] = jnp.transpose(jnp.broadcast_to(m + jnp.log2(l), (tq, LANES)))[:1, :]

    n_q = seq // tq
    return _pcall_riding(
        body, (q, q, kv, kr, kv), ride,
        lambda: (pl.program_id(0) == 0) & (pl.program_id(1) == 0),
        lambda: (pl.program_id(0) == n_heads - 1) & (pl.program_id(1) == n_q - 1),
        middle=(lambda: (pl.program_id(0) == (5 * n_heads) // 8) & (pl.program_id(1) == 0)) if ride is not None else None,
        name="attn_fwd", grid=(n_heads, n_q),
        in_specs=_q_specs(tq, lambda h, i: (i, h)) + _kv_specs(seq, lambda h, i: (0, h)),
        out_specs=[pl.BlockSpec((tq, V_HEAD), lambda h, i: (i, h)),
                   pl.BlockSpec((None, None, 1, tq), lambda h, i: (h, i, 0, 0))],
        out_shape=[jax.ShapeDtypeStruct((seq, n_heads * V_HEAD), F32),
                   jax.ShapeDtypeStruct((n_heads, n_q, 1, tq), F32)],
        compiler_params=_params(("arbitrary", "arbitrary")))


def attn_bwd(q, kv, kr, do, lse_row, delta_row, tq=ATT_TK):
    seq = q.shape[0]
    tq = min(tq, seq)
    n_blk = seq // tq

    def body(qn_ref, qr_ref, kn_ref, kr_ref, v_ref, do_ref, lse_ref, delta_ref, dqn_ref, dqr_ref, dkv_ref, dkr_ref, dq_acc):
        head, kj = pl.program_id(0), pl.program_id(1)

        @pl.when(kj == 0)
        def _():
            dq_acc[...] = jnp.zeros_like(dq_acc)

        kc = _cat(kn_ref[...], kr_ref[...])
        vv = v_ref[...]

        def block(i, carry, diag):
            dk, dv = carry
            rows = pl.ds(pl.multiple_of(i * tq, tq), tq)
            qv = _cat(qn_ref[rows, :], qr_ref[rows, :])
            st = lax.dot_general(kc, qv, _NT, preferred_element_type=F32)
            if diag:
                st = _causal(st, transposed=True)
            pt = jnp.exp2(st - lse_ref[0, pl.ds(i, 1), :])
            dob = do_ref[rows, :].astype(BF16)
            dv = dv + jnp.dot(pt.astype(BF16), dob, preferred_element_type=F32)
            dpt = lax.dot_general(vv, dob, _NT, preferred_element_type=F32)
            dst = (pt * (dpt - delta_ref[0, pl.ds(i, 1), :])).astype(BF16)
            dk = dk + jnp.dot(dst, qv, preferred_element_type=F32)
            dq_acc[rows, :] += lax.dot_general(dst, kc, _TN, preferred_element_type=F32)
            return dk, dv

        carry = block(kj, (jnp.zeros((tq, 2 * LANES), F32), jnp.zeros((tq, V_HEAD), F32)), True)
        dk, dv = _loop_in_pairs(n_blk, lambda i, c: block(i, c, False), carry, start=kj + 1)
        dk = dk * LN2
        dkv_ref[...] = _cat(dk[:, :LANES], dv).astype(dkv_ref.dtype)
        lane = lax.broadcasted_iota(jnp.int32, (tq, LANES), 1)
        mine = (lane // HALF_ROPE) % HEADS_PER_CHIP == head % HEADS_PER_CHIP
        dkr_ref[0] = jnp.where(mine, dk[:, LANES:], 0.0)

        @pl.when(kj == n_blk - 1)
        def _():
            dqn_ref[...] = dq_acc[:, :LANES] * SM_SCALE

        @pl.when((kj == n_blk - 1) & (head % HEADS_PER_CHIP == 0))
        def _():
            dqr_ref[...] = dq_acc[:, LANES:] * SM_SCALE

        @pl.when((kj == n_blk - 1) & (head % HEADS_PER_CHIP > 0))
        def _():
            dqr_ref[...] += dq_acc[:, LANES:] * SM_SCALE

    return _pcall(
        body, name="attn_bwd", grid=(N_HEADS, n_blk),
        in_specs=_q_specs(seq, lambda h, j: (0, h)) + _kv_specs(tq, lambda h, j: (j, h))
        + [pl.BlockSpec((seq, V_HEAD), lambda h, j: (0, h)),
           pl.BlockSpec((1, n_blk, tq), lambda h, j: (h, 0, 0)),
           pl.BlockSpec((1, n_blk, tq), lambda h, j: (h, 0, 0))],
        out_specs=[pl.BlockSpec((seq, LANES), lambda h, j: (0, h)),
                   pl.BlockSpec((seq, LANES), lambda h, j: (0, h // HEADS_PER_CHIP)),
                   pl.BlockSpec((tq, QK_NOPE + V_HEAD), lambda h, j: (j, h)),
                   pl.BlockSpec((1, tq, LANES), lambda h, j: (h, j, 0))],
        out_shape=[jax.ShapeDtypeStruct((seq, N_HEADS * QK_NOPE), F32),
                   jax.ShapeDtypeStruct((seq, N_CHIPS * LANES), F32),
                   jax.ShapeDtypeStruct((seq, N_HEADS * (QK_NOPE + V_HEAD)), BF16),
                   jax.ShapeDtypeStruct((N_HEADS, seq, LANES), F32)],
        scratch_shapes=[pltpu.VMEM((seq, 2 * LANES), F32)],
        compiler_params=_params(("arbitrary", "arbitrary")),
    )(q, q, kv, kr, kv, do, lse_row, delta_row)


def head_sum(x, ts=512):
    n_heads, seq, w = x.shape
    ts = min(ts, seq)

    def body(x_ref, o_ref):
        o_ref[...] = jnp.sum(x_ref[...], axis=0)

    return _pcall(body, name="head_sum", grid=(seq // ts,),
                  in_specs=[pl.BlockSpec((n_heads, ts, w), lambda i: (0, i, 0))],
                  out_specs=pl.BlockSpec((ts, w), lambda i: (i, 0)),
                  out_shape=jax.ShapeDtypeStruct((seq, w), F32),
                  compiler_params=_params(("parallel",)))(x)


HEADS_PER_CHIP = N_HEADS // N_CHIPS
Q_CHIP = HEADS_PER_CHIP * QK_DIM
Q_CHIP_NOPE = HEADS_PER_CHIP * QK_NOPE


def _perm_q_cols(w):
    t = w.reshape(w.shape[0], HEADS_PER_CHIP, QK_DIM)
    return jnp.concatenate([t[:, :, :QK_NOPE].reshape(w.shape[0], -1),
                            t[:, :, QK_NOPE:QK_NOPE + HALF_ROPE].reshape(w.shape[0], -1),
                            t[:, :, QK_NOPE + HALF_ROPE:].reshape(w.shape[0], -1)], axis=1)


def _unperm_q_cols(w):
    r = w.shape[0]
    nope = w[:, :Q_CHIP_NOPE].reshape(r, HEADS_PER_CHIP, QK_NOPE)
    r1 = w[:, Q_CHIP_NOPE:Q_CHIP_NOPE + QK_ROPE].reshape(r, HEADS_PER_CHIP, HALF_ROPE)
    r2 = w[:, Q_CHIP_NOPE + QK_ROPE:].reshape(r, HEADS_PER_CHIP, HALF_ROPE)
    return jnp.concatenate([nope, r1, r2], axis=2).reshape(r, Q_CHIP)


def _pad_kva_cols(w):
    z = jnp.zeros((w.shape[0], HALF_ROPE), w.dtype)
    return jnp.concatenate([w[:, :KV_LORA], w[:, KV_LORA:KV_LORA + HALF_ROPE], z, w[:, KV_LORA + HALF_ROPE:], z], axis=1)


def _unpad_kva_cols(w):
    return jnp.concatenate([w[:, :KV_LORA], w[:, KV_LORA:KV_LORA + HALF_ROPE],
                            w[:, KV_LORA + QK_ROPE:KV_LORA + QK_ROPE + HALF_ROPE]], axis=1)


def _rope_tile(t, cs, sn):
    return t * cs + pltpu.roll(t, LANES // 2, 1) * sn


def _rope_tile_bwd(d, cs, sn):
    return d * cs + pltpu.roll(d * sn, LANES // 2, 1)


def _b_cols(tk, tn):
    return pl.BlockSpec((None, tk, tn), lambda i, j, k: (j, k, 0))


def _b_cols_t(tk, tn):
    return pl.BlockSpec((None, tn, tk), lambda i, j, k: (k, j, 0))


def _out_cols(shape):
    return shape, lambda tm, tn: pl.BlockSpec((None, tm, tn), lambda i, j, k: (j, i, 0))


def _halves(a):
    return a.reshape(N_CHIPS, 2, a.shape[1] // 2, a.shape[2])


def device_step(x, positions, target, w, comm=None):
    seq = x.shape[0]
    w = dict(w)

    def gathered(names, outs):
        for n, a in zip(names, outs):
            if isinstance(n, tuple):
                w[n[0]] = [a.reshape(v.shape) if l == n[1] else v for l, v in enumerate(w[n[0]])]
            else:
                w[n] = a.reshape(w[n].shape)

    def ride_for(names):
        if comm is None:
            return None
        return GatherRide([_halves(w[n[0]][n[1]] if isinstance(n, tuple) else w[n]) for n in names])

    first_ride = ("ssm_w_glu", "ssm_w_out", ("w_ff1", 0), ("w_ff2", 0))
    mla_ride = ("kv_w_a", "kv_w_b", "q_w_a", "q_w_b", "attn_w_o")
    second_ride = (("w_ff1", 1), ("w_ff2", 1))

    inv_freq = ROPE_THETA ** (-jnp.arange(HALF_ROPE, dtype=F32) / HALF_ROPE)
    ang = positions.astype(F32)[:, None] * inv_freq
    cos, sin = jnp.cos(ang), jnp.sin(ang)
    zero = jnp.zeros_like(cos)
    cos_q, sin_q = jnp.concatenate([cos] * 4, 1), jnp.concatenate([-sin, -sin, sin, sin], 1)
    cos_k, sin_k = jnp.concatenate([cos, zero, cos, zero], 1), jnp.concatenate([-sin, zero, sin, zero], 1)
    ff_tile = D_FF // N_CHIPS
    pack_shape = (N_CHIPS, EARLY_ROWS, PACK_W)

    lr = w["ssm_lam_re"].reshape(N_STATES, 1)
    li = w["ssm_lam_im"].reshape(N_STATES, 1)
    ldt = jnp.repeat(w["ssm_log_dt"].reshape(N_GROUPS), SSM_STATE).reshape(N_STATES, 1)
    b_re = w["ssm_b_re"].reshape(N_STATES, SSM_GROUP)
    b_im = w["ssm_b_im"].reshape(N_STATES, SSM_GROUP)
    a_re, a_im, bb_re, bb_im = s5_prep(lr, li, ldt, b_re, b_im)
    a_re, a_im = a_re.reshape(1, N_STATES), a_im.reshape(1, N_STATES)
    bbd_re = _blockdiag_in(bb_re.reshape(N_GROUPS, SSM_STATE, SSM_GROUP)).astype(BF16)
    bbd_im = _blockdiag_in(bb_im.reshape(N_GROUPS, SSM_STATE, SSM_GROUP)).astype(BF16)
    cbd_re = _blockdiag_out(w["ssm_c_re"].reshape(N_GROUPS, SSM_GROUP, SSM_STATE)).astype(BF16)
    cbd_imn = _blockdiag_out(-w["ssm_c_im"].reshape(N_GROUPS, SSM_GROUP, SSM_STATE)).astype(BF16)
    dskip = w["ssm_d"].reshape(1, D_MODEL)
    (ypre, yg, h_re, h_im), landed = s5_fwd(x, bbd_re, bbd_im, cbd_re, cbd_imn, a_re, a_im, dskip, ride_for(first_ride))
    gathered(first_ride, landed)
    w_glu = w["ssm_w_glu"]
    glu_tile = w_glu.shape[2]
    vg = mm(yg, w_glu, n_dim=2 * D_MODEL, tiles=(None, glu_tile, None), b_view=_b_cols, name="glu_proj")

    def glu(v):
        return (v[:, :D_MODEL] * _sigmoid(v[:, D_MODEL:]),), ()
    (z,) = rowwise(glu, (vg,), ((D_MODEL, BF16),), name="glu")
    w_out = w["ssm_w_out"].reshape(D_MODEL, D_MODEL)
    mix0 = mm(z, w_out, name="ssm_out")

    def mlp_fwd(hb, layer, riding=None):
        pre = mm(hb, w["w_ff1"][layer], n_dim=D_FF, tiles=(None, ff_tile, None), b_view=_b_cols, name=f"ff1_{layer}",
                 out_dtypes=(BF16,), ride=ride_for(riding) if riding else None)
        if riding and comm is not None:
            pre, landed = pre
            gathered(riding, landed)
        f = mm(pre, w["w_ff2"][layer].reshape(D_FF, D_MODEL), pro_a=_relu2, name=f"ff2_{layer}")
        return pre, f

    ln = lambda name, l: w[name][l].reshape(1, D_MODEL)
    h1, h1b = ln_fwd(x, mix0, ln("ln_mix_g", 0), ln("ln_mix_b", 0), "ln_mix_0")
    f1pre, f1 = mlp_fwd(h1b, 0, mla_ride)
    h2, h2b = ln_fwd(h1, f1, ln("ln_ffn_g", 0), ln("ln_ffn_b", 0), "ln_ffn_0")

    kv_w_a = w["kv_w_a"].reshape(D_MODEL, KVA_PAD)
    kv_w_b = w["kv_w_b"]
    q_w_a = w["q_w_a"].reshape(D_MODEL, Q_LORA)
    q_w_b = w["q_w_b"]
    w_o = w["attn_w_o"].reshape(D_MODEL, D_MODEL)
    kvb_tile = kv_w_b.shape[2]
    kvn_g = w["kv_norm_g"].reshape(1, KV_LORA)
    qn_g = w["q_norm_g"].reshape(1, Q_LORA)
    kva = mm(h2b, kv_w_a, name="kv_a")

    def kv_post(kva, g, cs, sn):
        tile = _rope_tile(kva[:, KV_LORA:], cs, sn)
        return (_rms(kva[:, :KV_LORA], g), _cat(tile, pltpu.roll(tile, HALF_ROPE, 1))), ()
    ckv, krope = rowwise(kv_post, (kva, kvn_g, cos_k, sin_k), ((KV_LORA, BF16), (2 * LANES, BF16)), name="kv_post")
    kvb = mm(ckv, kv_w_b, n_dim=N_CHIPS * kvb_tile, tiles=(None, kvb_tile, KV_LORA), b_view=_b_cols, name="kv_b",
             out_dtypes=(BF16,))
    cq_raw, cq = mm(h2b, q_w_a, epi=lambda r, gq: (r, _rms(r, gq)), extras=(qn_g,), out_dtypes=(F32, BF16), name="q_a")

    def rope_and_scale(r, cs, sn):
        return (_cat(r[:, :Q_CHIP_NOPE], _rope_tile(r[:, Q_CHIP_NOPE:], cs, sn)) * Q_PRESCALE,)
    qro = mm(cq, q_w_b, n_dim=N_CHIPS * Q_CHIP, tiles=(None, Q_CHIP, Q_LORA), b_view=_b_cols, epi=rope_and_scale,
             extras=(cos_q, sin_q), out_dtypes=(BF16,), name="q_b")
    (o, lse), landed = attn_fwd(qro, kvb, krope, ride_for(second_ride))
    gathered(second_ride, landed)
    mix1 = mm(o, w_o, name="attn_out")
    h3, h3b = ln_fwd(h2, mix1, ln("ln_mix_g", 1), ln("ln_mix_b", 1), "ln_mix_1")
    f2pre, f2 = mlp_fwd(h3b, 1)
    def last_ln_and_loss(h, mix, gl, bl, t):
        e = _layer_norm(h, mix, gl, bl) - t
        return (e * (1.0 / D_MODEL),), (jnp.broadcast_to(jnp.sum(e * e), (1, LANES)),)
    dh4, loss_acc = rowwise(last_ln_and_loss, (h3, f2, ln("ln_ffn_g", 1), ln("ln_ffn_b", 1), target), ((D_MODEL, F32),),
                            accs=(LANES,), name="ln_ffn_1_loss")
    loss = loss_acc[0, 0] * (0.5 / D_MODEL)

    g = {}

    def into_rows(off, rows_per_chip, shape=pack_shape):
        def view(tm, tn):
            nb = rows_per_chip // tm
            return pl.BlockSpec((None, tm, tn), lambda i, j, k: (i // nb, off // tm + i % nb, 0))
        return shape, view

    def into_cols(off):
        return pack_shape, lambda tm, tn: pl.BlockSpec((None, tm, tn), lambda i, j, k: (j, off // tm + i, 0))

    def mlp_bwd(pack, dr, drb, hb, pre, layer, swap=False):
        w2_rows = (EARLY_OFF["w_ff2"] + layer * ff_tile, ff_tile)
        w1_rows = (EARLY_OFF["w_ff1"] + layer * D_MODEL, D_MODEL)
        ready = [(w1_rows[0] + w1_rows[1], w2_rows[0] - w1_rows[0] - w1_rows[1]), (w2_rows[0] + w2_rows[1], EARLY_ROWS - w2_rows[0] - w2_rows[1])]
        dpre = mm(drb, w["w_ff2"][layer].reshape(D_FF, D_MODEL), tb=True, epi=lambda r, p: (r * 2.0 * jnp.maximum(p, 0.0),),
                  extras=(pre,), out_dtypes=(BF16,), tiles=(None, ff_tile, None), name=f"ff2_dx_{layer}",
                  ride=SwapRide(pack, ready) if swap else None)
        if swap:
            dpre, (theirs,) = dpre
        pack = mm(pre, drb, ta=True, pro_a=_relu2, name=f"ff2_dw_{layer}", tiles=(ff_tile, PACK_W, None), into=pack,
                  out_view=into_rows(w2_rows[0], ff_tile))
        pack = mm(hb, dpre, ta=True, name=f"ff1_dw_{layer}", tiles=(None, PACK_W, None), into=pack,
                  out_view=into_cols(w1_rows[0]))
        dh = mm(dpre, w["w_ff1"][layer], tb=True, epi=lambda r, d: (r + DN_ALPHA * d,), extras=(dr,), n_dim=D_MODEL,
                tiles=(None, D_MODEL, ff_tile), b_view=_b_cols_t, name=f"ff1_dx_{layer}",
                ride=SwapRide(pack, [w1_rows, w2_rows], into=theirs) if swap else None)
        return (pack, *dh) if swap else (pack, dh)

    dr4, dr4b, dg_f1, db_f1 = ln_bwd(h3, f2, ln("ln_ffn_g", 1), dh4, "ln_ffn_bwd_1")
    pack, dh3 = mlp_bwd(None, dr4, dr4b, h3b, f2pre, 1)
    dr3, dr3b, dg_m1, db_m1 = ln_bwd(h2, mix1, ln("ln_mix_g", 1), dh3, "ln_mix_bwd_1")
    shard_rows = D_MODEL // N_CHIPS
    pack = mm(o, dr3b, ta=True, name="attn_out_dw", tiles=(shard_rows, PACK_W, None), into=pack,
              out_view=into_rows(EARLY_OFF["attn_w_o"], shard_rows))
    do = mm(dr3b, w_o, tb=True, name="attn_out_dx")
    def head_dots(do, o):
        return (jnp.concatenate([jnp.sum(do[:, V_HEAD * h:V_HEAD * (h + 1)] * o[:, V_HEAD * h:V_HEAD * (h + 1)], axis=1,
                                         keepdims=True) for h in range(N_HEADS)], axis=1),), ()
    (delta,) = rowwise(head_dots, (do, o), ((N_HEADS, F32),), name="attn_delta")
    tb = min(ATT_TK, seq)
    lse_row = lse.reshape(N_HEADS, seq // tb, tb)
    delta_row = delta.T.reshape(N_HEADS, seq // tb, tb)
    dqn, dqr, dkvb, dkr = attn_bwd(qro, kvb, krope, do, lse_row, delta_row)

    def q_rope_bwd(dn, dr, cs, sn):
        parts = []
        for k in range(N_CHIPS):
            parts.append(dn[:, Q_CHIP_NOPE * k:Q_CHIP_NOPE * (k + 1)])
            parts.append(_rope_tile_bwd(dr[:, LANES * k:LANES * (k + 1)], cs, sn))
        return (jnp.concatenate(parts, axis=1),), ()
    (dqlin,) = rowwise(q_rope_bwd, (dqn, dqr, cos_q, sin_q), ((N_CHIPS * Q_CHIP, BF16),), name="q_rope_bwd")
    g["q_w_b"] = mm(cq, dqlin, ta=True, name="q_b_dw", tiles=(Q_LORA, Q_CHIP, None), out_view=_out_cols(q_w_b.shape))
    dcq = mm(dqlin, q_w_b, tb=True, n_dim=Q_LORA, tiles=(None, Q_LORA, Q_CHIP), b_view=_b_cols_t, name="q_b_dx")

    def q_norm_bwd(c, gq, d):
        dx, dgq = _rms_bwd(c, gq, d)
        return (dx,), (dgq,)
    dcq_raw, dqn_g = rowwise(q_norm_bwd, (cq_raw, qn_g, dcq), ((Q_LORA, BF16),), accs=(Q_LORA,), name="q_norm_bwd")
    g["q_w_a"] = mm(h2b, dcq_raw, ta=True, name="q_a_dw")
    g["kv_w_b"] = mm(ckv, dkvb, ta=True, name="kv_b_dw", tiles=(KV_LORA, kvb_tile, None), out_view=_out_cols(kv_w_b.shape))
    dckv = mm(dkvb, kv_w_b, tb=True, n_dim=KV_LORA, tiles=(None, KV_LORA, kvb_tile), b_view=_b_cols_t, name="kv_b_dx")
    dkr_sum = head_sum(dkr)

    def kv_post_bwd(kva, gk, dc, dk, cs, sn):
        dx, dgk = _rms_bwd(kva[:, :KV_LORA], gk, dc)
        dk = dk + pltpu.roll(dk, LANES - HALF_ROPE, 1)
        return (jnp.concatenate([dx, _rope_tile_bwd(dk, cs, sn)], axis=1),), (dgk,)
    dkva, dkvn_g = rowwise(kv_post_bwd, (kva, kvn_g, dckv, dkr_sum, cos_k, sin_k), ((KVA_PAD, BF16),),
                           accs=(KV_LORA,), name="kv_post_bwd")
    g["kv_w_a"] = mm(h2b, dkva, ta=True, name="kv_a_dw")
    dh2 = mm(dcq_raw, q_w_a, tb=True, epi=lambda r, d: (r + DN_ALPHA * d,), extras=(dr3,), name="q_a_dx")
    dh2 = mm(dkva, kv_w_a, tb=True, epi=lambda r, d: (r + d,), extras=(dh2,), name="kv_a_dx")

    dr2, dr2b, dg_f0, db_f0 = ln_bwd(h1, f1, ln("ln_ffn_g", 0), dh2, "ln_ffn_bwd_0")
    pack = put_rows(pack, packed_shards(g, MISC_EARLY, EARLY_ROWS - MISC_EARLY_OFF), MISC_EARLY_OFF)
    if comm is None:
        pack, dh1 = mlp_bwd(pack, dr2, dr2b, h1b, f1pre, 0)
    else:
        pack, dh1, (theirs,) = mlp_bwd(pack, dr2, dr2b, h1b, f1pre, 0, swap=True)
        early_sums = add_halves(pack, theirs, comm[1])
    dr1, dr1b, dg_m0, db_m0 = ln_bwd(x, mix0, ln("ln_mix_g", 0), dh1, "ln_mix_bwd_0")
    mid = mm(z, dr1b, ta=True, name="ssm_out_dw", tiles=(shard_rows, PACK_W, None),
             out_view=into_rows(MID_OFF["ssm_w_out"], shard_rows, (N_CHIPS, MID_ROWS, PACK_W)))
    dz = mm(dr1b, w_out, tb=True, name="ssm_out_dx")

    def glu_bwd(v, dz):
        val, sg = v[:, :D_MODEL], _sigmoid(v[:, D_MODEL:])
        return (jnp.concatenate([dz * sg, dz * val * sg * (1.0 - sg)], axis=1),), ()
    (dvg,) = rowwise(glu_bwd, (vg, dz), ((2 * D_MODEL, BF16),), name="glu_bwd")
    g["ssm_w_glu"] = mm(yg, dvg, ta=True, name="glu_proj_dw", tiles=(None, glu_tile, None), out_view=_out_cols(w_glu.shape))
    mid = put_rows(mid, packed_shards(g, MISC_MID, MID_ROWS - MISC_MID_OFF), MISC_MID_OFF)
    dypre = mm(dvg, w_glu, tb=True, epi=lambda r, y: (r * _gelu_grad(y),), extras=(ypre,), n_dim=D_MODEL,
               tiles=(None, D_MODEL, glu_tile), b_view=_b_cols_t, name="glu_proj_dx",
               ride=SwapRide(mid) if comm is not None else None)
    sends = None
    if comm is not None:
        dypre, (theirs,) = dypre
        sends = SendRide([early_sums, add_halves(mid, theirs, comm[1])])
    (dx, dbbd_re, dbbd_im, dcbd_re, dcbd_imn, dar, dai, dd), got = s5_bwd(
        dypre, x, dr1, h_re, h_im, bbd_re, bbd_im, cbd_re, cbd_imn, a_re, a_im, dskip, sends)
    dbb_re = _blockdiag_in_t(dbbd_re).reshape(N_STATES, SSM_GROUP)
    dbb_im = _blockdiag_in_t(dbbd_im).reshape(N_STATES, SSM_GROUP)
    dlr, dli, dldt, db_re, db_im = s5_prep_bwd(lr, li, ldt, b_re, b_im, dar.reshape(N_STATES, 1),
                                               dai.reshape(N_STATES, 1), dbb_re, dbb_im)
    g["ssm_lam_re"] = dlr.reshape(1, N_GROUPS, SSM_STATE)
    g["ssm_lam_im"] = dli.reshape(1, N_GROUPS, SSM_STATE)
    g["ssm_log_dt"] = group_sum(dldt).reshape(1, N_GROUPS)
    g["ssm_b_re"] = db_re.reshape(1, N_GROUPS, SSM_STATE, SSM_GROUP)
    g["ssm_b_im"] = db_im.reshape(1, N_GROUPS, SSM_STATE, SSM_GROUP)
    g["ssm_c_re"] = _blockdiag_out_t(dcbd_re).reshape(1, N_GROUPS, SSM_GROUP, SSM_STATE)
    g["ssm_c_im"] = -_blockdiag_out_t(dcbd_imn).reshape(1, N_GROUPS, SSM_GROUP, SSM_STATE)
    g["ssm_d"] = dd
    g["ln_mix_g"] = jnp.concatenate([dg_m0, dg_m1], 0)
    g["ln_mix_b"] = jnp.concatenate([db_m0, db_m1], 0)
    g["ln_ffn_g"] = jnp.concatenate([dg_f0, dg_f1], 0)
    g["ln_ffn_b"] = jnp.concatenate([db_f0, db_f1], 0)
    g["kv_norm_g"] = dkvn_g.reshape(KV_LORA)
    g["q_norm_g"] = dqn_g
    return loss, dx, pack, mid, g, list(zip(sends.ins, got)) if comm is not None else None


def place(shard, me_idx, dtype, name, layer=None):
    rows, cols = shard.shape[-2:]
    tr = _tile(rows, (512, 256, 128))

    def body(m_ref, x_ref, o_ref):
        o_ref[...] = x_ref[...].astype(o_ref.dtype)

    in_spec = (pl.BlockSpec((tr, cols), lambda i, m: (i, 0)) if layer is None
               else pl.BlockSpec((None, tr, cols), lambda i, m: (layer, i, 0)))
    return _pcall(
        body, name=name,
        grid_spec=pltpu.PrefetchScalarGridSpec(
            num_scalar_prefetch=1, grid=(rows // tr,), in_specs=[in_spec],
            out_specs=pl.BlockSpec((None, tr, cols), lambda i, m: (m[0], i, 0))),
        out_shape=jax.ShapeDtypeStruct((N_CHIPS, rows, cols), dtype),
        compiler_params=_params(("parallel",)),
    )(me_idx, shard)


def place_many(shards, dtypes, me_idx, name):
    def body(m_ref, *refs):
        for x_ref, o_ref in zip(refs[:len(shards)], refs[len(shards):]):
            o_ref[...] = x_ref[...].astype(o_ref.dtype)

    return _pcall(
        body, name=name,
        grid_spec=pltpu.PrefetchScalarGridSpec(
            num_scalar_prefetch=1, grid=(1,),
            in_specs=[pl.BlockSpec(s.shape, lambda i, m: (0, 0)) for s in shards],
            out_specs=[pl.BlockSpec((None,) + s.shape, lambda i, m: (m[0], 0, 0)) for s in shards]),
        out_shape=[jax.ShapeDtypeStruct((N_CHIPS,) + s.shape, d) for s, d in zip(shards, dtypes)],
        compiler_params=_params(("arbitrary",)),
    )(me_idx, *shards)


def put_rows(pack, rows, off):
    _, n, cols = rows.shape

    def body(r_ref, p_ref, o_ref, sem):
        cp = pltpu.make_async_copy(r_ref.at[0], o_ref.at[pl.program_id(0), pl.ds(off, n), :], sem)
        cp.start()
        cp.wait()

    return _pcall(body, name="grad_put_rows", grid=(N_CHIPS,),
                  in_specs=[pl.BlockSpec((1, n, cols), lambda k: (k, 0, 0)), _ANY], out_specs=_ANY,
                  out_shape=jax.ShapeDtypeStruct(pack.shape, pack.dtype), input_output_aliases={1: 0},
                  scratch_shapes=[pltpu.SemaphoreType.DMA],
                  compiler_params=_params(("arbitrary",)))(rows, pack)


def _my_cols(c, mine=True):
    start = (c if mine else 1 - c) * HALF_W
    return pl.ds(pl.multiple_of(start, HALF_W), HALF_W)


def add_halves(gpack, got, c_idx):
    n, rows, _ = gpack.shape
    tr = min(G_BLOCK_ROWS, rows)
    blk = (None, tr, HALF_W)

    def body(c_ref, g_ref, r_ref, o_ref):
        o_ref[...] = (g_ref[...] + r_ref[...]).astype(o_ref.dtype)

    return _pcall(
        body, name="grad_add_halves",
        grid_spec=pltpu.PrefetchScalarGridSpec(
            num_scalar_prefetch=1, grid=(n, rows // tr),
            in_specs=[pl.BlockSpec(blk, lambda k, i, c: (k, i, c[0])), pl.BlockSpec(blk, lambda k, i, c: (k, i, 0))],
            out_specs=pl.BlockSpec(blk, lambda k, i, c: (k, i, 0))),
        out_shape=jax.ShapeDtypeStruct((n, rows, HALF_W), BF16),
        compiler_params=_params(("parallel", "parallel")),
    )(c_idx, gpack, got)


def sum_owner(part, got, idx, total_rows, row_off=0, into=None):
    _, rows, _ = part.shape
    tr = min(G_BLOCK_ROWS, rows)
    n_into = 0 if into is None else 1

    def body(m_ref, p_ref, g_ref, *rest):
        up = lambda v: v.astype(F32)
        rest[-1][...] = ((up(p_ref[...]) + up(g_ref[0])) + up(g_ref[1])) + up(g_ref[2])

    return _pcall(
        body, name="grad_sum_owner",
        grid_spec=pltpu.PrefetchScalarGridSpec(
            num_scalar_prefetch=1, grid=(rows // tr,),
            in_specs=[pl.BlockSpec((None, tr, HALF_W), lambda i, m: (m[0], i, 0)),
                      pl.BlockSpec((3, tr, HALF_W), lambda i, m: (0, i, 0))] + [_ANY] * n_into,
            out_specs=pl.BlockSpec((tr, HALF_W), lambda i, m: (row_off // tr + i, m[1]))),
        out_shape=jax.ShapeDtypeStruct((total_rows, PACK_W), F32),
        input_output_aliases={3: 0} if n_into else {},
        compiler_params=_params(("parallel",)),
    )(idx, part, got, *([into] if n_into else []))


def join_halves(red):
    def body(in_ref, out_ref, send_sem, recv_sem):
        x, y, c, _ = _place()
        sibling = (x, y, 1 - c)
        mine = out_ref.at[:, _my_cols(c)]
        cp = pltpu.make_async_remote_copy(src_ref=mine, dst_ref=mine, send_sem=send_sem, recv_sem=recv_sem,
                                          device_id=sibling, device_id_type=MESH)
        cp.start()
        cp.wait_send()
        other = out_ref.at[:, _my_cols(c, mine=False)]
        pltpu.make_async_remote_copy(src_ref=other, dst_ref=other, send_sem=send_sem, recv_sem=recv_sem,
                                     device_id=sibling, device_id_type=MESH).wait_recv()

    return _pcall(body, name="grad_join_halves", in_specs=[_ANY], out_specs=_ANY,
                  out_shape=jax.ShapeDtypeStruct(red.shape, red.dtype), input_output_aliases={0: 0},
                  scratch_shapes=[pltpu.SemaphoreType.DMA, pltpu.SemaphoreType.DMA])(red)


def adamw(gsrc, g_off, wt, m, v, name):
    n, cols = wt.shape
    tr = math.gcd(math.gcd(g_off, n), 256) if g_off else math.gcd(n, 256)
    off_blk = g_off // tr
    c1 = 1.0 / (1.0 - ADAM_B1 ** ADAM_STEP)
    c2 = 1.0 / (1.0 - ADAM_B2 ** ADAM_STEP)

    def body(g_ref, w_ref, m_ref, v_ref, go_ref, d_ref, mo_ref, vo_ref):
        gv = g_ref[...]
        mn = ADAM_B1 * m_ref[...] + (1.0 - ADAM_B1) * gv
        vn = ADAM_B2 * v_ref[...] + (1.0 - ADAM_B2) * gv * gv
        go_ref[...] = gv
        mo_ref[...] = mn
        vo_ref[...] = vn
        d_ref[...] = -ADAM_LR * ((mn * c1) / (jnp.sqrt(vn * c2) + ADAM_EPS) + ADAM_WD * w_ref[...])

    blk = pl.BlockSpec((tr, cols), lambda i: (i, 0))
    return _pcall(body, name=name, grid=(n // tr,),
                  in_specs=[pl.BlockSpec((tr, cols), lambda i: (off_blk + i, 0)), blk, blk, blk],
                  out_specs=[blk] * 4, out_shape=[jax.ShapeDtypeStruct((n, cols), F32)] * 4,
                  compiler_params=_params(("parallel",)))(gsrc, wt, m, v)


def _rows8(a):
    return -(-a.size // (8 * PACK_W)) * 8


def _as_rows(a, rows=None):
    flat = a.reshape(-1)
    n = _rows8(a) if rows is None else rows
    return jnp.pad(flat, (0, n * PACK_W - flat.shape[0])).reshape(n, PACK_W)


def local_shards_2d(wl):
    return {"w_ff1": [wl["w_ff1"][0], wl["w_ff1"][1]], "w_ff2": [wl["w_ff2"][0], wl["w_ff2"][1]],
            "ssm_w_glu": wl["ssm_w_glu"], "ssm_w_out": wl["ssm_w_out"], "kv_w_a": _pad_kva_cols(wl["kv_w_a"]),
            "kv_w_b": wl["kv_w_b"], "q_w_a": wl["q_w_a"], "q_w_b": _perm_q_cols(wl["q_w_b"]),
            "attn_w_o": wl["attn_w_o"], "ssm_d": wl["ssm_d"].reshape(2, -1)}


def misc_grad_shard(name, g, k):
    if name == "ssm_d":
        w = D_MODEL // N_CHIPS
        return g[:, w * k:w * (k + 1)]
    if name in ("ssm_w_glu", "kv_w_b"):
        return g[k]
    if name == "q_w_b":
        return _unperm_q_cols(g[k])
    rows = D_MODEL // N_CHIPS
    shard = g[rows * k:rows * (k + 1)]
    return _unpad_kva_cols(shard) if name == "kv_w_a" else shard


def packed_shards(g, names, rows, tail=None):
    blocks = []
    for k in range(N_CHIPS):
        parts = [_as_rows(misc_grad_shard(n, g[n], k), MISC_SHARD_ROWS[n]) for n in names]
        if tail is not None:
            parts.append(tail[k * (tail.shape[0] // N_CHIPS):(k + 1) * (tail.shape[0] // N_CHIPS)])
        blk = jnp.concatenate(parts, axis=0)
        blocks.append(jnp.pad(blk, ((0, rows - blk.shape[0]), (0, 0))))
    return jnp.stack(blocks)


def kernel(x, positions, ln_mix_g, ln_mix_b, ln_ffn_g, ln_ffn_b, w_ff1, w_ff2, ssm_lam_re, ssm_lam_im, ssm_log_dt, ssm_b_re, ssm_b_im, ssm_c_re, ssm_c_im, ssm_d, ssm_w_glu, ssm_w_out, kv_w_a, kv_norm_g, kv_w_b, q_w_a, q_norm_g, q_w_b, attn_w_o, loss_target, m_ln_mix_g, m_ln_mix_b, m_ln_ffn_g, m_ln_ffn_b, m_w_ff1, m_w_ff2, m_ssm_lam_re, m_ssm_lam_im, m_ssm_log_dt, m_ssm_b_re, m_ssm_b_im, m_ssm_c_re, m_ssm_c_im, m_ssm_d, m_ssm_w_glu, m_ssm_w_out, m_kv_w_a, m_kv_norm_g, m_kv_w_b, m_q_w_a, m_q_norm_g, m_q_w_b, m_attn_w_o, v_ln_mix_g, v_ln_mix_b, v_ln_ffn_g, v_ln_ffn_b, v_w_ff1, v_w_ff2, v_ssm_lam_re, v_ssm_lam_im, v_ssm_log_dt, v_ssm_b_re, v_ssm_b_im, v_ssm_c_re, v_ssm_c_im, v_ssm_d, v_ssm_w_glu, v_ssm_w_out, v_kv_w_a, v_kv_norm_g, v_kv_w_b, v_q_w_a, v_q_norm_g, v_q_w_b, v_attn_w_o):
    env = dict(locals())
    wl = {n: env[n] for n in WEIGHTS}
    ml = {n: env["m_" + n] for n in WEIGHTS}
    vl = {n: env["v_" + n] for n in WEIGHTS}
    for n in ("ssm_w_glu", "ssm_w_out", "q_w_a", "q_w_b", "attn_w_o"):
        wl[n], ml[n], vl[n] = wl[n][0], ml[n][0], vl[n][0]

    c_idx = lax.axis_index("c").astype(jnp.int32).reshape(1)
    me_idx = (2 * lax.axis_index("x") + lax.axis_index("y")).astype(jnp.int32).reshape(1)

    local = local_shards_2d(wl)
    stacked = {n: [place(wl[n], me_idx, BF16, f"place_{n}_{l}", layer=l) for l in range(DEPTH)] for n in ("w_ff1", "w_ff2")}
    others = [n for n in SHARDED if n not in stacked]
    stacked.update(zip(others, place_many([local[n] for n in others], [F32 if n == "ssm_d" else BF16 for n in others],
                                          me_idx, "place_others")))
    stacked["ssm_d"] = ride_alone(GatherRide([_halves(stacked["ssm_d"])]), "ssm_d_all_gather")[0].reshape(1, D_MODEL)
    for n in REPLICATED:
        stacked[n] = wl[n]

    loss_part, dx, early, mid, g, sent = device_step(x[0], positions[0], loss_target[0], stacked, comm=(me_idx, c_idx))
    loss = lax.psum(loss_part, ("x", "y", "c"))

    small = jnp.concatenate([_as_rows(g[n]) for n in REPLICATED], axis=0)
    small = jnp.pad(small, ((0, SMALL_ROWS - small.shape[0]), (0, 0)))
    late = packed_shards(g, MISC_LATE, LATE_ROWS, tail=small)
    late_sums = add_halves(late, ride_alone(SwapRide(late), "grad_swap_halves")[0], c_idx)
    sent.append((late_sums, ride_alone(SendRide([late_sums]), "grad_send_to_owners")[0]))
    where = jnp.concatenate([me_idx, c_idx])
    starts = (0, EARLY_ROWS, EARLY_ROWS + MID_ROWS)
    total_rows = EARLY_ROWS + MID_ROWS + LATE_ROWS
    reduced = None
    for (sums, got), off in zip(sent, starts):
        reduced = sum_owner(sums, got, where, total_rows, row_off=off, into=reduced)
    reduced = join_halves(reduced)
    quarter = reduced[starts[2] + SMALL_OFF:starts[2] + SMALL_OFF + SMALL_Q_ROWS]
    small_tot = ride_alone(GatherRide([_halves(place(quarter, me_idx, F32, "place_small_grads"))]),
                           "small_grad_all_gather")[0].reshape(SMALL_ROWS, PACK_W)

    out_g, out_d, out_m, out_v = {}, {}, {}, {}
    direct = {**EARLY_OFF, **{n: starts[1] + o for n, o in MID_OFF.items()}}
    for n, off in direct.items():
        res = adamw(reduced, off, wl[n].reshape(-1, PACK_W), ml[n].reshape(-1, PACK_W), vl[n].reshape(-1, PACK_W),
                    "adamw_" + n)
        out_g[n], out_d[n], out_m[n], out_v[n] = [a.reshape(env[n].shape) for a in res]
    for names, off in ((MISC_EARLY, MISC_EARLY_OFF), (MISC_MID, starts[1] + MISC_MID_OFF), (MISC_LATE, starts[2])):
        pack3 = lambda d: jnp.concatenate([_as_rows(d[n], MISC_SHARD_ROWS[n]) for n in names], axis=0)
        res = adamw(reduced, off, pack3(wl), pack3(ml), pack3(vl), "adamw_packed_" + names[0])
        r0 = 0
        for n in names:
            cnt = math.prod(env[n].shape)
            out_g[n], out_d[n], out_m[n], out_v[n] = [
                a[r0:r0 + MISC_SHARD_ROWS[n]].reshape(-1)[:cnt].reshape(env[n].shape) for a in res]
            r0 += MISC_SHARD_ROWS[n]
    ws = jnp.concatenate([_as_rows(wl[n]) for n in REPLICATED], axis=0)
    ms = jnp.concatenate([_as_rows(ml[n]) for n in REPLICATED], axis=0)
    vs = jnp.concatenate([_as_rows(vl[n]) for n in REPLICATED], axis=0)
    pad = ((0, SMALL_ROWS - ws.shape[0]), (0, 0))
    res = adamw(small_tot, 0, jnp.pad(ws, pad), jnp.pad(ms, pad), jnp.pad(vs, pad), "adamw_replicated")
    row = 0
    for n in REPLICATED:
        cnt = math.prod(env[n].shape)
        nrows = _rows8(env[n])
        out_g[n], out_d[n], out_m[n], out_v[n] = [a[row:row + nrows].reshape(-1)[:cnt].reshape(env[n].shape) for a in res]
        row += nrows

    return (loss, dx[None], *[out_g[n] for n in WEIGHTS], *[out_d[n] for n in WEIGHTS],
            *[out_m[n] for n in WEIGHTS], *[out_v[n] for n in WEIGHTS])
```

```python
import functools
import math

import jax
import jax.numpy as jnp
from jax import lax
from jax.experimental import pallas as pl
from jax.experimental.pallas import tpu as pltpu

F32 = jnp.float32
BF16 = jnp.bfloat16
MESH = pl.DeviceIdType.MESH

D_MODEL = 1024
DEPTH = 2
SSM_GROUP = 16
N_GROUPS = D_MODEL // SSM_GROUP
SSM_STATE = 64
N_STATES = N_GROUPS * SSM_STATE
N_HEADS = 8
QK_NOPE = 128
QK_ROPE = 64
HALF_ROPE = QK_ROPE // 2
V_HEAD = 128
QK_DIM = QK_NOPE + QK_ROPE
Q_LORA = 384
KV_LORA = 256
ROPE_THETA = 10000.0
SM_SCALE = QK_DIM ** -0.5
NEG_INF = -1e30
D_FF = 4 * D_MODEL
DN_ALPHA = (2 * DEPTH) ** 0.25
LN_EPS = 1e-5
RMS_EPS = 1e-6
ADAM_LR = 0.001
ADAM_B1 = 0.9
ADAM_B2 = 0.999
ADAM_EPS = 1e-08
ADAM_WD = 0.01
ADAM_STEP = 10

N_CHIPS = 4
LANES = 128
VMEM_LIMIT = 56 * 1024 * 1024
MM_VMEM_BUDGET = 40 * 1024 * 1024
PACK_W = 1024
KVA_PAD = 384
HALF_W = PACK_W // 2

SHARDED = ("w_ff1", "w_ff2", "ssm_w_glu", "ssm_w_out", "kv_w_a", "kv_w_b", "q_w_a", "q_w_b", "attn_w_o", "ssm_d")
G_BLOCK_ROWS = 960
EARLY_OFF = {"w_ff1": 0, "w_ff2": 2048, "attn_w_o": 4096}
MISC_EARLY = ("kv_w_b", "kv_w_a", "q_w_a", "q_w_b")
MISC_EARLY_OFF = 4352
EARLY_ROWS = 5 * G_BLOCK_ROWS
MID_OFF = {"ssm_w_out": 0}
MISC_MID = ("ssm_w_glu",)
MISC_MID_OFF = 256
MID_ROWS = MISC_MID_OFF + 512
MISC_LATE = ("ssm_d",)
SMALL_Q_ROWS = 96
SMALL_ROWS = N_CHIPS * SMALL_Q_ROWS
SMALL_OFF = 16
LATE_ROWS = 192
MISC_SHARD_ROWS = {"ssm_d": 16, "ssm_w_glu": 512, "kv_w_b": 128, "kv_w_a": 80, "q_w_a": 96, "q_w_b": 144}
REPLICATED = ("ln_mix_g", "ln_mix_b", "ln_ffn_g", "ln_ffn_b", "ssm_lam_re", "ssm_lam_im", "ssm_log_dt",
              "ssm_b_re", "ssm_b_im", "ssm_c_re", "ssm_c_im", "kv_norm_g", "q_norm_g")
WEIGHTS = ("ln_mix_g", "ln_mix_b", "ln_ffn_g", "ln_ffn_b", "w_ff1", "w_ff2", "ssm_lam_re", "ssm_lam_im",
           "ssm_log_dt", "ssm_b_re", "ssm_b_im", "ssm_c_re", "ssm_c_im", "ssm_d", "ssm_w_glu", "ssm_w_out",
           "kv_w_a", "kv_norm_g", "kv_w_b", "q_w_a", "q_norm_g", "q_w_b", "attn_w_o")


def _pcall(body, **kw):
    return pl.pallas_call(body, **kw)


def _params(sem=None):
    return pltpu.CompilerParams(dimension_semantics=sem, vmem_limit_bytes=VMEM_LIMIT)


_ANY = pl.BlockSpec(memory_space=pl.ANY)


def _tile(dim, prefs):
    for p in prefs:
        if dim % p == 0:
            return p
    return dim


def _place():
    x, y, c = lax.axis_index("x"), lax.axis_index("y"), lax.axis_index("c")
    return x, y, c, [(1 - x, y), (x, 1 - y), (1 - x, 1 - y)]


def _remote(k, src, dst, to, send_sems, recv_sems):
    return pltpu.make_async_remote_copy(src_ref=src, dst_ref=dst, send_sem=send_sems.at[k], recv_sem=recv_sems.at[k],
                                        device_id=to, device_id_type=MESH)


class GatherRide:
    def __init__(self, arrs):
        self.ins = list(arrs)
        self.out_shapes = [jax.ShapeDtypeStruct(a.shape, a.dtype) for a in arrs]
        self.aliases = {i: i for i in range(len(arrs))}
        self.n_sems = 6 * len(arrs)

    def start(self, ins, outs, send_sems, recv_sems):
        x, y, c, chips = _place()
        me = 2 * x + y
        for a, o in enumerate(outs):
            for j, (px, py) in enumerate(chips):
                _remote(6 * a + j, o.at[me, c], o.at[me, c], (px, py, c), send_sems, recv_sems).start()

    def pass_on(self, ins, outs, send_sems, recv_sems):
        x, y, c, chips = _place()
        for a, o in enumerate(outs):
            for j, (px, py) in enumerate(chips):
                blk = o.at[2 * px + py, c]
                _remote(6 * a + j, blk, blk, (px, py, c), send_sems, recv_sems).wait_recv()
                _remote(6 * a + 3 + j, blk, blk, (x, y, 1 - c), send_sems, recv_sems).start()

    def finish(self, ins, outs, send_sems, recv_sems, passed_on=False):
        if not passed_on:
            self.pass_on(ins, outs, send_sems, recv_sems)
        x, y, c, chips = _place()
        me = 2 * x + y
        sibling = (x, y, 1 - c)
        for a, o in enumerate(outs):
            for j, (px, py) in enumerate(chips):
                blk = o.at[2 * px + py, 1 - c]
                _remote(6 * a + 3 + j, blk, blk, sibling, send_sems, recv_sems).wait_recv()
                _remote(6 * a + j, o.at[me, c], o.at[me, c], (px, py, c), send_sems, recv_sems).wait_send()
                mine = o.at[2 * px + py, c]
                _remote(6 * a + 3 + j, mine, mine, sibling, send_sems, recv_sems).wait_send()


class SendRide:
    def __init__(self, parts):
        self.ins = list(parts)
        self.out_shapes = [jax.ShapeDtypeStruct((3,) + p.shape[1:], p.dtype) for p in parts]
        self.aliases = {}
        self.n_sems = 3 * len(parts)

    def _copies(self, ins, outs, send_sems, recv_sems):
        x, y, c, chips = _place()
        return [_remote(3 * a + j, ins[a].at[2 * px + py], outs[a].at[j], (px, py, c), send_sems, recv_sems)
                for a in range(len(ins)) for j, (px, py) in enumerate(chips)]

    def start(self, ins, outs, send_sems, recv_sems):
        for cp in self._copies(ins, outs, send_sems, recv_sems):
            cp.start()

    def finish(self, ins, outs, send_sems, recv_sems):
        for cp in self._copies(ins, outs, send_sems, recv_sems):
            cp.wait()


class SwapRide:
    def __init__(self, pack, ranges=None, into=None):
        self.ins = [pack] if into is None else [pack, into]
        self.out_shapes = [jax.ShapeDtypeStruct(pack.shape[:2] + (HALF_W,), pack.dtype)]
        self.aliases = {} if into is None else {1: 0}
        self.ranges = ranges or [(0, pack.shape[1])]
        self.n_sems = len(self.ranges)

    def _copies(self, ins, outs, send_sems, recv_sems):
        x, y, c, _ = _place()
        return [_remote(k, ins[0].at[:, pl.ds(r0, n), _my_cols(c, mine=False)], outs[0].at[:, pl.ds(r0, n), :],
                        (x, y, 1 - c), send_sems, recv_sems) for k, (r0, n) in enumerate(self.ranges)]

    def start(self, ins, outs, send_sems, recv_sems):
        for cp in self._copies(ins, outs, send_sems, recv_sems):
            cp.start()

    def finish(self, ins, outs, send_sems, recv_sems):
        for cp in self._copies(ins, outs, send_sems, recv_sems):
            cp.wait()


def _pcall_riding(body, args, ride, first, last, *, in_specs, out_specs, out_shape, scratch_shapes=(), middle=None,
                  **kw):
    n_in, n_out = len(args), len(out_shape)
    if ride is None:
        return _pcall(body, in_specs=in_specs, out_specs=out_specs, out_shape=out_shape,
                      scratch_shapes=list(scratch_shapes), **kw)(*args), []
    k_in, k_out = len(ride.ins), len(ride.out_shapes)

    def riding(*refs):
        ins, r_in = refs[:n_in], refs[n_in:n_in + k_in]
        outs = refs[n_in + k_in:n_in + k_in + n_out]
        r_out = refs[n_in + k_in + n_out:n_in + k_in + n_out + k_out]
        scratch, (send_sems, recv_sems) = refs[n_in + k_in + n_out + k_out:-2], refs[-2:]

        @pl.when(first())
        def _():
            ride.start(r_in, r_out, send_sems, recv_sems)

        if middle is not None:
            @pl.when(middle())
            def _():
                ride.pass_on(r_in, r_out, send_sems, recv_sems)

        body(*ins, *outs, *scratch)

        @pl.when(last())
        def _():
            if middle is not None:
                ride.finish(r_in, r_out, send_sems, recv_sems, passed_on=True)
            else:
                ride.finish(r_in, r_out, send_sems, recv_sems)

    res = _pcall(riding, in_specs=list(in_specs) + [_ANY] * k_in, out_specs=list(out_specs) + [_ANY] * k_out,
                 out_shape=list(out_shape) + ride.out_shapes,
                 input_output_aliases={n_in + i: n_out + o for i, o in ride.aliases.items()},
                 scratch_shapes=list(scratch_shapes) + [pltpu.SemaphoreType.DMA((ride.n_sems,))] * 2,
                 **kw)(*args, *ride.ins)
    return res[:n_out], res[n_out:]


def ride_alone(ride, name):
    def body(*refs):
        n = len(ride.ins)
        ins, outs, (send_sems, recv_sems) = refs[:n], refs[n:-2], refs[-2:]
        ride.start(ins, outs, send_sems, recv_sems)
        ride.finish(ins, outs, send_sems, recv_sems)

    return _pcall(body, name=name, in_specs=[_ANY] * len(ride.ins), out_specs=[_ANY] * len(ride.out_shapes),
                  out_shape=ride.out_shapes, input_output_aliases=dict(ride.aliases),
                  scratch_shapes=[pltpu.SemaphoreType.DMA((ride.n_sems,))] * 2)(*ride.ins)


def mm(a, b, *, name, ta=False, tb=False, pro_a=None, epi=None, extras=(), out_dtypes=(F32,), n_dim=None,
       tiles=(None, None, None), b_view=None, out_view=None, into=None, ride=None):
    if ta:
        k_dim, m_dim = a.shape
    else:
        m_dim, k_dim = a.shape
    if n_dim is None:
        n_dim = b.shape[0] if tb else b.shape[1]
    tn = tiles[1] or (n_dim if n_dim <= 1024 else _tile(n_dim, (1024, 512, 256, 128)))
    tk = tiles[2] or (k_dim if k_dim <= 1024 else _tile(k_dim, (1024, 512, 256, 128)))
    nk = k_dim // tk

    def vmem_bytes(tm):
        blocks = tm * tk * a.dtype.itemsize + tk * tn * b.dtype.itemsize
        blocks += tm * tn * (sum(e.dtype.itemsize for e in extras) + sum(jnp.dtype(d).itemsize for d in out_dtypes))
        return 2 * blocks + tm * tn * 4

    tm = tiles[0] or next((t for t in (4096, 2048, 1024, 512, 256) if m_dim % t == 0 and vmem_bytes(t) <= MM_VMEM_BUDGET),
                          _tile(m_dim, (128,)))
    assert m_dim % tm == 0 and n_dim % tn == 0 and k_dim % tk == 0, (name, m_dim, n_dim, k_dim, tm, tn, tk)
    n_ex, n_out = len(extras), len(out_dtypes)
    n_into = 0 if into is None else 1
    dims = (((0 if ta else 1,), (1 if tb else 0,)), ((), ()))

    def body(a_ref, b_ref, *rest):
        ex_refs, out_refs = rest[:n_ex], rest[n_ex + n_into:n_ex + n_into + n_out]

        def partial():
            av = a_ref[...]
            if pro_a is not None:
                av = pro_a(av)
            return lax.dot_general(av.astype(BF16), b_ref[...].astype(BF16), dims, preferred_element_type=F32)

        def finish(r):
            res = epi(r, *[e[...] for e in ex_refs]) if epi is not None else (r,)
            for o_ref, v in zip(out_refs, res):
                o_ref[...] = v.reshape(o_ref.shape).astype(o_ref.dtype)

        if nk == 1:
            finish(partial())
            return
        acc = rest[-1]
        k = pl.program_id(2)

        @pl.when(k == 0)
        def _():
            acc[...] = partial()

        @pl.when(k > 0)
        def _():
            acc[...] += partial()

        @pl.when(k == nk - 1)
        def _():
            finish(acc[...])

    def ex_spec(e):
        if e.shape == (m_dim, n_dim):
            return o_spec
        if e.shape[0] == m_dim:
            return pl.BlockSpec((tm, e.shape[1]), lambda i, j, k: (i, 0))
        return pl.BlockSpec(e.shape, lambda i, j, k: (0, 0))

    a_spec = pl.BlockSpec((tk, tm), lambda i, j, k: (k, i)) if ta else pl.BlockSpec((tm, tk), lambda i, j, k: (i, k))
    if b_view is not None:
        b_spec = b_view(tk, tn)
    else:
        b_spec = pl.BlockSpec((tn, tk), lambda i, j, k: (j, k)) if tb else pl.BlockSpec((tk, tn), lambda i, j, k: (k, j))
    o_spec = pl.BlockSpec((tm, tn), lambda i, j, k: (i, j))
    if out_view is None:
        out_specs = [o_spec] * n_out
        out_shape = [jax.ShapeDtypeStruct((m_dim, n_dim), dt) for dt in out_dtypes]
    else:
        assert n_out == 1
        out_specs = [out_view[1](tm, tn)]
        out_shape = [jax.ShapeDtypeStruct(out_view[0], out_dtypes[0])]
    grid = (m_dim // tm, n_dim // tn, nk)
    scratch = [pltpu.VMEM((tm, tn), F32)] if nk > 1 else []
    if ride is not None:
        assert into is None
        at = lambda ids: functools.reduce(jnp.logical_and, [pl.program_id(d) == i for d, i in enumerate(ids)])
        outs, landed = _pcall_riding(
            body, (a, b, *extras), ride, lambda: at((0, 0, 0)), lambda: at([g - 1 for g in grid]),
            name=name, grid=grid, in_specs=[a_spec, b_spec] + [ex_spec(e) for e in extras], out_specs=out_specs,
            out_shape=out_shape, scratch_shapes=scratch, compiler_params=_params(("arbitrary",) * 3))
        return (outs[0] if n_out == 1 else outs), landed
    outs = _pcall(
        body, name=name, grid=grid,
        in_specs=[a_spec, b_spec] + [ex_spec(e) for e in extras] + [_ANY] * n_into,
        out_specs=out_specs, out_shape=out_shape,
        input_output_aliases={2 + n_ex: 0} if n_into else {},
        scratch_shapes=scratch,
        compiler_params=_params(("parallel", "parallel", "arbitrary")),
    )(a, b, *extras, *([into] if n_into else []))
    return outs[0] if n_out == 1 else outs


def rowwise(fn, ins, outs, *, name, accs=(), tm=256):
    rows = ins[0].shape[0]
    tm = min(tm, rows)
    n_in, n_out, n_acc = len(ins), len(outs), len(accs)

    def body(*refs):
        in_refs, out_refs, acc_refs = refs[:n_in], refs[n_in:n_in + n_out], refs[n_in + n_out:]
        res, sums = fn(*[r[...] for r in in_refs])
        for o_ref, v in zip(out_refs, res):
            o_ref[...] = v.astype(o_ref.dtype)
        if n_acc:
            @pl.when(pl.program_id(0) == 0)
            def _():
                for a_ref in acc_refs:
                    a_ref[...] = jnp.zeros_like(a_ref)

            for a_ref, s in zip(acc_refs, sums):
                a_ref[...] += s

    def spec(arr):
        if arr.shape[0] == rows:
            return pl.BlockSpec((tm, arr.shape[1]), lambda i: (i, 0))
        return pl.BlockSpec(arr.shape, lambda i: (0, 0))

    res = _pcall(
        body, name=name, grid=(rows // tm,),
        in_specs=[spec(a) for a in ins],
        out_specs=[pl.BlockSpec((tm, w), lambda i: (i, 0)) for w, _ in outs]
        + [pl.BlockSpec((1, w), lambda i: (0, 0)) for w in accs],
        out_shape=[jax.ShapeDtypeStruct((rows, w), dt) for w, dt in outs]
        + [jax.ShapeDtypeStruct((1, w), F32) for w in accs],
        compiler_params=_params(("arbitrary",) if n_acc else ("parallel",)),
    )(*ins)
    return res


def _relu2(v):
    r = jnp.maximum(v, 0.0)
    return r * r


def _gelu(x):
    c = math.sqrt(2.0 / math.pi)
    return 0.5 * x * (1.0 + jnp.tanh(c * (x + 0.044715 * x * x * x)))


def _gelu_grad(x):
    c = math.sqrt(2.0 / math.pi)
    t = jnp.tanh(c * (x + 0.044715 * x * x * x))
    return 0.5 * (1.0 + t) + 0.5 * x * (1.0 - t * t) * c * (1.0 + 3 * 0.044715 * x * x)


def _sigmoid(x):
    return 1.0 / (1.0 + jnp.exp(-x))


def _layer_norm(h, mix, g, b):
    r = DN_ALPHA * h + mix
    mu = jnp.mean(r, axis=-1, keepdims=True)
    xc = r - mu
    var = jnp.mean(xc * xc, axis=-1, keepdims=True)
    return xc * lax.rsqrt(var + LN_EPS) * g + b


def ln_fwd(h, mix, g, b, name):
    def fn(h, mix, g, b):
        y = _layer_norm(h, mix, g, b)
        return (y, y), ()
    return rowwise(fn, (h, mix, g, b), ((D_MODEL, F32), (D_MODEL, BF16)), name=name)


def ln_bwd(h, mix, g, dy, name):
    def fn(h, mix, g, dy):
        r = DN_ALPHA * h + mix
        mu = jnp.mean(r, axis=-1, keepdims=True)
        xc = r - mu
        var = jnp.mean(xc * xc, axis=-1, keepdims=True)
        rstd = lax.rsqrt(var + LN_EPS)
        xhat = xc * rstd
        dxh = dy * g
        m1 = jnp.mean(dxh, axis=-1, keepdims=True)
        m2 = jnp.mean(dxh * xhat, axis=-1, keepdims=True)
        dr = rstd * (dxh - m1 - xhat * m2)
        return (dr, dr), (jnp.sum(dy * xhat, axis=0, keepdims=True), jnp.sum(dy, axis=0, keepdims=True))
    return rowwise(fn, (h, mix, g, dy), ((D_MODEL, F32), (D_MODEL, BF16)), accs=(D_MODEL, D_MODEL), name=name)


def _rms(x, g):
    r = lax.rsqrt(jnp.mean(x * x, axis=-1, keepdims=True) + RMS_EPS)
    return x * r * g


def _rms_bwd(x, g, dy):
    r = lax.rsqrt(jnp.mean(x * x, axis=-1, keepdims=True) + RMS_EPS)
    xn = x * r
    dyg = dy * g
    dx = r * (dyg - xn * jnp.mean(dyg * xn, axis=-1, keepdims=True))
    return dx, jnp.sum(dy * xn, axis=0, keepdims=True)


def _s5_disc(lr, li, ldt):
    dt = jnp.exp(ldt)
    mag = jnp.exp(lr * dt)
    cs, sn = jnp.cos(li * dt), jnp.sin(li * dt)
    ar, ai = mag * cs, mag * sn
    inv = 1.0 / (lr * lr + li * li)
    n_re = (ar - 1.0) * lr + ai * li
    n_im = ai * lr - (ar - 1.0) * li
    return dt, mag, cs, sn, ar, ai, inv, n_re, n_im


def s5_prep(lr, li, ldt, b_re, b_im):
    def fn(lr, li, ldt, b_re, b_im):
        _, _, _, _, ar, ai, inv, n_re, n_im = _s5_disc(lr, li, ldt)
        cr, ci = n_re * inv, n_im * inv
        return (ar, ai, cr * b_re - ci * b_im, cr * b_im + ci * b_re), ()
    return rowwise(fn, (lr, li, ldt, b_re, b_im), ((1, F32), (1, F32), (SSM_GROUP, F32), (SSM_GROUP, F32)),
                   name="s5_prep", tm=512)


def s5_prep_bwd(lr, li, ldt, b_re, b_im, dar, dai, dbb_re, dbb_im):
    def fn(lr, li, ldt, b_re, b_im, dar, dai, dbb_re, dbb_im):
        dt, mag, cs, sn, ar, ai, inv, n_re, n_im = _s5_disc(lr, li, ldt)
        cr, ci = n_re * inv, n_im * inv
        db_re = cr * dbb_re + ci * dbb_im
        db_im = cr * dbb_im - ci * dbb_re
        dcr = jnp.sum(dbb_re * b_re + dbb_im * b_im, axis=-1, keepdims=True)
        dci = jnp.sum(dbb_im * b_re - dbb_re * b_im, axis=-1, keepdims=True)
        dar = dar + (dcr * lr - dci * li) * inv
        dai = dai + (dcr * li + dci * lr) * inv
        dinv = dcr * n_re + dci * n_im
        dlr = (dcr * (ar - 1.0) + dci * ai) * inv - 2.0 * lr * inv * inv * dinv
        dli = (dcr * ai - dci * (ar - 1.0)) * inv - 2.0 * li * inv * inv * dinv
        dmag = dar * cs + dai * sn
        dth = dai * ar - dar * ai
        dlr = dlr + dmag * mag * dt
        dli = dli + dth * dt
        ddt = dmag * mag * lr + dth * li
        return (dlr, dli, ddt * dt, db_re, db_im), ()
    return rowwise(fn, (lr, li, ldt, b_re, b_im, dar, dai, dbb_re, dbb_im),
                   ((1, F32), (1, F32), (1, F32), (SSM_GROUP, F32), (SSM_GROUP, F32)), name="s5_prep_bwd", tm=512)


def group_sum(x):
    def body(x_ref, o_ref):
        o_ref[...] = jnp.sum(x_ref[...], axis=1)
    return _pcall(body, name="s5_group_sum", out_shape=jax.ShapeDtypeStruct((N_GROUPS, 1), F32))(
        x.reshape(N_GROUPS, SSM_STATE, 1))


GROUPS_PER_TILE = LANES // SSM_GROUP
TILE_STATES = GROUPS_PER_TILE * SSM_STATE
N_UTILES = D_MODEL // LANES


SUBLANES = 8
SCAN_STRIP = 1024
N_STRIPS = N_STATES // SCAN_STRIP
_NT = (((1,), (1,)), ((), ()))
_TN = (((0,), (0,)), ((), ()))


def _scan_coefs(are, aim, shifted, reverse):
    ar = are[...]
    ai = -aim[...] if reverse else aim[...]
    powers = {1: (ar, ai)}
    for d in (2, 4):
        r, i = powers[d // 2]
        powers[d] = (r * r - i * i, 2.0 * r * i)
    rid = lax.broadcasted_iota(jnp.int32, (SUBLANES, N_STATES), 0)
    first = (rid == SUBLANES - 1) if reverse else (rid == 0)
    masks = [(1, first)] + [(d, (rid <= SUBLANES - 1 - d) if reverse else (rid >= d)) for d in (1, 2, 4)]
    for n, (d, keep) in enumerate(masks):
        for part in (0, 1):
            shifted[2 * n + part][...] = jnp.where(keep, jnp.broadcast_to(powers[d][part], (SUBLANES, N_STATES)), 0.0)


def _tile_scan(xr, xi, shifted, nbr_re, nbr_im, reverse):
    for n, d in enumerate((1, 1, 2, 4)):
        by = SUBLANES - d if reverse else d
        fr, fi = (nbr_re, nbr_im) if n == 0 else (xr, xi)
        sr, si = pltpu.roll(fr, by, 0), pltpu.roll(fi, by, 0)
        kr, ki = shifted[2 * n], shifted[2 * n + 1]
        xr, xi = xr + kr * sr - ki * si, xi + kr * si + ki * sr
    return xr, xi


def _tile_rows(t):
    return pl.ds(pl.multiple_of(t * SUBLANES, SUBLANES), SUBLANES)


def s5_fwd(u, bbd_re, bbd_im, cbd_re, cbd_imn, a_re, a_im, dskip, ride=None, t_rows=256):
    seq = u.shape[0]
    t_rows = min(t_rows, seq)
    n_tiles = t_rows // SUBLANES

    def body(u_ref, bre, bim, cre, cimn, are, aim, d_ref, y_ref, gelu_ref, hre_ref, him_ref, car_re, car_im, *shifted):
        @pl.when(pl.program_id(0) == 0)
        def _():
            car_re[...] = jnp.zeros_like(car_re)
            car_im[...] = jnp.zeros_like(car_im)
            _scan_coefs(are, aim, shifted, reverse=False)

        uf = u_ref[...]
        ub = uf.astype(BF16)
        for j in range(N_UTILES):
            uj = ub[:, LANES * j:LANES * (j + 1)]
            sl = slice(TILE_STATES * j, TILE_STATES * (j + 1))
            hre_ref[:, sl] = jnp.dot(uj, bre[j], preferred_element_type=F32)
            him_ref[:, sl] = jnp.dot(uj, bim[j], preferred_element_type=F32)
        for s in range(N_STRIPS):
            cols = pl.ds(s * SCAN_STRIP, SCAN_STRIP)
            coefs = [c[:, cols] for c in shifted]

            def step(t, before):
                rows = _tile_rows(t)
                hr, hi = _tile_scan(hre_ref[rows, cols], him_ref[rows, cols], coefs, before[0], before[1], False)
                hre_ref[rows, cols] = hr
                him_ref[rows, cols] = hi
                return hr, hi

            cr, ci = lax.fori_loop(0, n_tiles, step, (car_re[:, cols], car_im[:, cols]))
            car_re[:, cols] = cr
            car_im[:, cols] = ci
        dv = d_ref[...]
        for j in range(N_UTILES):
            st = slice(TILE_STATES * j, TILE_STATES * (j + 1))
            yj = (jnp.dot(hre_ref[:, st].astype(BF16), cre[j], preferred_element_type=F32)
                  + jnp.dot(him_ref[:, st].astype(BF16), cimn[j], preferred_element_type=F32))
            sl = slice(LANES * j, LANES * (j + 1))
            yj = yj + dv[:, sl] * uf[:, sl]
            y_ref[:, sl] = yj
            gelu_ref[:, sl] = _gelu(yj).astype(gelu_ref.dtype)

    full3 = lambda a: pl.BlockSpec(a.shape, lambda i: (0, 0, 0))
    full2 = lambda a: pl.BlockSpec(a.shape, lambda i: (0, 0))
    tile = pltpu.VMEM((SUBLANES, N_STATES), F32)
    n_chunks = seq // t_rows
    return _pcall_riding(
        body, (u, bbd_re, bbd_im, cbd_re, cbd_imn, a_re, a_im, dskip), ride,
        lambda: pl.program_id(0) == 0, lambda: pl.program_id(0) == n_chunks - 1,
        middle=(lambda: pl.program_id(0) == (7 * n_chunks) // 8) if ride is not None else None,
        name="s5_fwd", grid=(n_chunks,),
        in_specs=[pl.BlockSpec((t_rows, D_MODEL), lambda i: (i, 0)), full3(bbd_re), full3(bbd_im), full3(cbd_re),
                  full3(cbd_imn), full2(a_re), full2(a_im), full2(dskip)],
        out_specs=[pl.BlockSpec((t_rows, D_MODEL), lambda i: (i, 0)),
                   pl.BlockSpec((t_rows, D_MODEL), lambda i: (i, 0)),
                   pl.BlockSpec((t_rows, N_STATES), lambda i: (i, 0)),
                   pl.BlockSpec((t_rows, N_STATES), lambda i: (i, 0))],
        out_shape=[jax.ShapeDtypeStruct((seq, D_MODEL), F32),
                   jax.ShapeDtypeStruct((seq, D_MODEL), BF16),
                   jax.ShapeDtypeStruct((seq, N_STATES), F32),
                   jax.ShapeDtypeStruct((seq, N_STATES), F32)],
        scratch_shapes=[tile] * 10,
        compiler_params=_params(("arbitrary",)))


def s5_bwd(dy, u, dres, h_re, h_im, bbd_re, bbd_im, cbd_re, cbd_imn, a_re, a_im, dskip, ride=None, t_rows=256):
    seq = u.shape[0]
    t_rows = min(t_rows, seq)
    n_chunks = seq // t_rows

    n_tiles = t_rows // SUBLANES

    def body(dy_ref, u_ref, dres_ref, hre_ref, him_ref, hpre_ref, hpim_ref, bre, bim, cre, cimn, are, aim, d_ref,
             dx_ref, dbre, dbim, dcre, dcimn, dar_ref, dai_ref, dd_ref, lre, lim, car_re, car_im, acc_re, acc_im,
             *shifted):
        i = pl.program_id(0)

        @pl.when(i == 0)
        def _():
            for r in (car_re, car_im, acc_re, acc_im, dbre, dbim, dcre, dcimn, dd_ref):
                r[...] = jnp.zeros_like(r)
            _scan_coefs(are, aim, shifted, reverse=True)

        dyf = dy_ref[...]
        dyb = dyf.astype(BF16)
        uf = u_ref[...]
        ub = uf.astype(BF16)
        for j in range(N_UTILES):
            dyj = dyb[:, LANES * j:LANES * (j + 1)]
            st = slice(TILE_STATES * j, TILE_STATES * (j + 1))
            lre[:, st] = lax.dot_general(dyj, cre[j], _NT, preferred_element_type=F32)
            lim[:, st] = lax.dot_general(dyj, cimn[j], _NT, preferred_element_type=F32)
        has_pred = (i < n_chunks - 1).astype(F32)
        last_row = lax.broadcasted_iota(jnp.int32, (SUBLANES, SCAN_STRIP), 0) == SUBLANES - 1
        for s in range(N_STRIPS):
            cols = pl.ds(s * SCAN_STRIP, SCAN_STRIP)
            coefs = [c[:, cols] for c in shifted]
            before_re, before_im = hpre_ref[:, cols] * has_pred, hpim_ref[:, cols] * has_pred

            def step(k, carry):
                after_re, after_im, dar, dai = carry
                t = n_tiles - 1 - k
                rows = _tile_rows(t)
                lr, li = _tile_scan(lre[rows, cols], lim[rows, cols], coefs, after_re, after_im, True)
                lre[rows, cols] = lr
                lim[rows, cols] = li
                prev = _tile_rows(jnp.maximum(t - 1, 0))
                pre_re = jnp.where(t == 0, before_re, hre_ref[prev, cols])
                pre_im = jnp.where(t == 0, before_im, him_ref[prev, cols])
                hpr = pltpu.roll(jnp.where(last_row, pre_re, hre_ref[rows, cols]), 1, 0)
                hpi = pltpu.roll(jnp.where(last_row, pre_im, him_ref[rows, cols]), 1, 0)
                return lr, li, dar + lr * hpr + li * hpi, dai + li * hpr - lr * hpi

            cr, ci, dar, dai = lax.fori_loop(0, n_tiles, step, (car_re[:, cols], car_im[:, cols],
                                                               acc_re[:, cols], acc_im[:, cols]))
            car_re[:, cols] = cr
            car_im[:, cols] = ci
            acc_re[:, cols] = dar
            acc_im[:, cols] = dai

        dv = d_ref[...]
        for j in range(N_UTILES):
            sl = slice(LANES * j, LANES * (j + 1))
            st = slice(TILE_STATES * j, TILE_STATES * (j + 1))
            lrj = lre[:, st].astype(BF16)
            lij = lim[:, st].astype(BF16)
            du = (lax.dot_general(lrj, bre[j], _NT, preferred_element_type=F32)
                  + lax.dot_general(lij, bim[j], _NT, preferred_element_type=F32))
            dx_ref[:, sl] = du + dv[:, sl] * dyf[:, sl] + DN_ALPHA * dres_ref[:, sl]
            uj = ub[:, sl]
            dbre[j] += lax.dot_general(uj, lrj, _TN, preferred_element_type=F32)
            dbim[j] += lax.dot_general(uj, lij, _TN, preferred_element_type=F32)
            dyj = dyb[:, sl]
            dcre[j] += lax.dot_general(hre_ref[:, st].astype(BF16), dyj, _TN, preferred_element_type=F32)
            dcimn[j] += lax.dot_general(him_ref[:, st].astype(BF16), dyj, _TN, preferred_element_type=F32)
        dd_ref[...] += jnp.sum(dyf * uf, axis=0, keepdims=True)

        @pl.when(i == n_chunks - 1)
        def _():
            dar_ref[...] = jnp.sum(acc_re[...], axis=0, keepdims=True)
            dai_ref[...] = jnp.sum(acc_im[...], axis=0, keepdims=True)

    rev = lambda i: (n_chunks - 1 - i, 0)
    prev_tile = lambda i: (jnp.maximum((n_chunks - 1 - i) * n_tiles - 1, 0), 0)
    once = pl.Buffered(1)
    full3 = lambda a: pl.BlockSpec(a.shape, lambda i: (0, 0, 0), pipeline_mode=once)
    full2 = lambda a: pl.BlockSpec(a.shape, lambda i: (0, 0), pipeline_mode=once)
    acc3 = lambda shape: pl.BlockSpec(shape, lambda i: (0, 0, 0))
    acc2 = lambda shape: pl.BlockSpec(shape, lambda i: (0, 0))
    tile = pltpu.VMEM((SUBLANES, N_STATES), F32)
    return _pcall_riding(
        body, (dy, u, dres, h_re, h_im, h_re, h_im, bbd_re, bbd_im, cbd_re, cbd_imn, a_re, a_im, dskip), ride,
        lambda: pl.program_id(0) == 0, lambda: pl.program_id(0) == n_chunks - 1,
        name="s5_bwd", grid=(n_chunks,),
        in_specs=[pl.BlockSpec((t_rows, D_MODEL), rev), pl.BlockSpec((t_rows, D_MODEL), rev),
                  pl.BlockSpec((t_rows, D_MODEL), rev),
                  pl.BlockSpec((t_rows, N_STATES), rev), pl.BlockSpec((t_rows, N_STATES), rev),
                  pl.BlockSpec((SUBLANES, N_STATES), prev_tile), pl.BlockSpec((SUBLANES, N_STATES), prev_tile),
                  full3(bbd_re), full3(bbd_im), full3(cbd_re), full3(cbd_imn), full2(a_re), full2(a_im), full2(dskip)],
        out_specs=[pl.BlockSpec((t_rows, D_MODEL), rev), acc3(bbd_re.shape), acc3(bbd_im.shape), acc3(cbd_re.shape),
                   acc3(cbd_imn.shape), acc2((1, N_STATES)), acc2((1, N_STATES)), acc2((1, D_MODEL))],
        out_shape=[jax.ShapeDtypeStruct((seq, D_MODEL), F32), jax.ShapeDtypeStruct(bbd_re.shape, F32),
                   jax.ShapeDtypeStruct(bbd_im.shape, F32), jax.ShapeDtypeStruct(cbd_re.shape, F32),
                   jax.ShapeDtypeStruct(cbd_imn.shape, F32), jax.ShapeDtypeStruct((1, N_STATES), F32),
                   jax.ShapeDtypeStruct((1, N_STATES), F32), jax.ShapeDtypeStruct((1, D_MODEL), F32)],
        scratch_shapes=[pltpu.VMEM((t_rows, N_STATES), F32), pltpu.VMEM((t_rows, N_STATES), F32)] + [tile] * 12,
        compiler_params=_params(("arbitrary",)))


def _eye_groups():
    return jnp.eye(GROUPS_PER_TILE, dtype=F32)


def _blockdiag_in(bb):
    t = bb.transpose(0, 2, 1).reshape(N_UTILES, GROUPS_PER_TILE, SSM_GROUP, SSM_STATE)
    bd = jnp.einsum("jgcp,gh->jgchp", t, _eye_groups())
    return bd.reshape(N_UTILES, LANES, TILE_STATES)


def _blockdiag_in_t(d):
    t = jnp.einsum("jgchp,gh->jgcp", d.reshape(N_UTILES, GROUPS_PER_TILE, SSM_GROUP, GROUPS_PER_TILE, SSM_STATE),
                   _eye_groups())
    return t.reshape(N_GROUPS, SSM_GROUP, SSM_STATE).transpose(0, 2, 1)


def _blockdiag_out(c):
    t = c.transpose(0, 2, 1).reshape(N_UTILES, GROUPS_PER_TILE, SSM_STATE, SSM_GROUP)
    bd = jnp.einsum("jhpc,hg->jhpgc", t, _eye_groups())
    return bd.reshape(N_UTILES, TILE_STATES, LANES)


def _blockdiag_out_t(d):
    t = jnp.einsum("jhpgc,hg->jhpc", d.reshape(N_UTILES, GROUPS_PER_TILE, SSM_STATE, GROUPS_PER_TILE, SSM_GROUP),
                   _eye_groups())
    return t.reshape(N_GROUPS, SSM_STATE, SSM_GROUP).transpose(0, 2, 1)


ATT_TQ = 512
ATT_TK = 512
LOG2E = math.log2(math.e)
LN2 = math.log(2.0)
Q_PRESCALE = SM_SCALE * LOG2E


def _loop_in_pairs(n, step, carry, start=0):
    pairs = (n - start) // 2

    def two(t, c):
        return step(start + 2 * t + 1, step(start + 2 * t, c))

    carry = lax.fori_loop(0, pairs, two, carry)
    return lax.fori_loop(start + 2 * pairs, n, step, carry)


def _causal(s, transposed=False):
    r = lax.broadcasted_iota(jnp.int32, s.shape, 0)
    c = lax.broadcasted_iota(jnp.int32, s.shape, 1)
    return jnp.where((r <= c) if transposed else (c <= r), s, NEG_INF)


def _q_specs(rows, at):
    def nope(*ids):
        r, h = at(*ids)
        return r, 3 * (h // HEADS_PER_CHIP) + h % HEADS_PER_CHIP

    def rope(*ids):
        r, h = at(*ids)
        return r, 3 * (h // HEADS_PER_CHIP) + HEADS_PER_CHIP

    return [pl.BlockSpec((rows, LANES), nope), pl.BlockSpec((rows, LANES), rope)]


def _kv_specs(rows, at):
    def col(f):
        def index(*ids):
            r, h = at(*ids)
            return r, f(h)
        return index

    return [pl.BlockSpec((rows, LANES), col(lambda h: 2 * h)), pl.BlockSpec((rows, LANES), col(lambda h: h % HEADS_PER_CHIP)),
            pl.BlockSpec((rows, LANES), col(lambda h: 2 * h + 1))]


def _cat(a, b):
    return jnp.concatenate([a, b], axis=1)


def attn_fwd(q, kv, kr, ride=None, tq=ATT_TQ, tk=ATT_TK):
    seq = q.shape[0]
    n_heads = N_HEADS
    tq, tk = min(tq, seq), min(tk, seq)
    assert tq == tk

    def body(qn_ref, qr_ref, kn_ref, kr_ref, v_ref, o_ref, lse_ref):
        qi = pl.program_id(1)
        qv = _cat(qn_ref[...], qr_ref[...])
        jd = qi

        def block(j, carry, diag):
            m, l, acc = carry
            rows = pl.ds(pl.multiple_of(j * tk, tk), tk)
            s = lax.dot_general(qv, _cat(kn_ref[rows, :], kr_ref[rows, :]), _NT, preferred_element_type=F32)
            if diag:
                s = _causal(s)
            m_new = jnp.maximum(m, jnp.max(s, axis=-1, keepdims=True))
            p = jnp.exp2(s - m_new)
            corr = jnp.exp2(m - m_new)
            l = l * corr + jnp.sum(p, axis=-1, keepdims=True)
            acc = acc * corr + jnp.dot(p.astype(BF16), v_ref[rows, :], preferred_element_type=F32)
            return m_new, l, acc

        init = (jnp.full((tq, 1), NEG_INF, F32), jnp.zeros((tq, 1), F32), jnp.zeros((tq, V_HEAD), F32))
        carry = _loop_in_pairs(jd, lambda j, c: block(j, c, False), init)
        m, l, acc = block(jd, carry, True)
        o_ref[...] = acc / l
        lse_ref[...] = jnp.transpose(jnp.broadcast_to(m + jnp.log2(l), (tq, LANES)))[:1, :]

    n_q = seq // tq
    return _pcall_riding(
        body, (q, q, kv, kr, kv), ride,
        lambda: (pl.program_id(0) == 0) & (pl.program_id(1) == 0),
        lambda: (pl.program_id(0) == n_heads - 1) & (pl.program_id(1) == n_q - 1),
        middle=(lambda: (pl.program_id(0) == (5 * n_heads) // 8) & (pl.program_id(1) == 0)) if ride is not None else None,
        name="attn_fwd", grid=(n_heads, n_q),
        in_specs=_q_specs(tq, lambda h, i: (i, h)) + _kv_specs(seq, lambda h, i: (0, h)),
        out_specs=[pl.BlockSpec((tq, V_HEAD), lambda h, i: (i, h)),
                   pl.BlockSpec((None, None, 1, tq), lambda h, i: (h, i, 0, 0))],
        out_shape=[jax.ShapeDtypeStruct((seq, n_heads * V_HEAD), F32),
                   jax.ShapeDtypeStruct((n_heads, n_q, 1, tq), F32)],
        compiler_params=_params(("arbitrary", "arbitrary")))


def attn_bwd(q, kv, kr, do, lse_row, delta_row, tq=ATT_TK):
    seq = q.shape[0]
    tq = min(tq, seq)
    n_blk = seq // tq

    def body(qn_ref, qr_ref, kn_ref, kr_ref, v_ref, do_ref, lse_ref, delta_ref, dqn_ref, dqr_ref, dkv_ref, dkr_ref, dq_acc):
        head, kj = pl.program_id(0), pl.program_id(1)

        @pl.when(kj == 0)
        def _():
            dq_acc[...] = jnp.zeros_like(dq_acc)

        kc = _cat(kn_ref[...], kr_ref[...])
        vv = v_ref[...]

        def block(i, carry, diag):
            dk, dv = carry
            rows = pl.ds(pl.multiple_of(i * tq, tq), tq)
            qv = _cat(qn_ref[rows, :], qr_ref[rows, :])
            st = lax.dot_general(kc, qv, _NT, preferred_element_type=F32)
            if diag:
                st = _causal(st, transposed=True)
            pt = jnp.exp2(st - lse_ref[0, pl.ds(i, 1), :])
            dob = do_ref[rows, :].astype(BF16)
            dv = dv + jnp.dot(pt.astype(BF16), dob, preferred_element_type=F32)
            dpt = lax.dot_general(vv, dob, _NT, preferred_element_type=F32)
            dst = (pt * (dpt - delta_ref[0, pl.ds(i, 1), :])).astype(BF16)
            dk = dk + jnp.dot(dst, qv, preferred_element_type=F32)
            dq_acc[rows, :] += lax.dot_general(dst, kc, _TN, preferred_element_type=F32)
            return dk, dv

        carry = block(kj, (jnp.zeros((tq, 2 * LANES), F32), jnp.zeros((tq, V_HEAD), F32)), True)
        dk, dv = _loop_in_pairs(n_blk, lambda i, c: block(i, c, False), carry, start=kj + 1)
        dk = dk * LN2
        dkv_ref[...] = _cat(dk[:, :LANES], dv).astype(dkv_ref.dtype)
        lane = lax.broadcasted_iota(jnp.int32, (tq, LANES), 1)
        mine = (lane // HALF_ROPE) % HEADS_PER_CHIP == head % HEADS_PER_CHIP
        dkr_ref[0] = jnp.where(mine, dk[:, LANES:], 0.0)

        @pl.when(kj == n_blk - 1)
        def _():
            dqn_ref[...] = dq_acc[:, :LANES] * SM_SCALE

        @pl.when((kj == n_blk - 1) & (head % HEADS_PER_CHIP == 0))
        def _():
            dqr_ref[...] = dq_acc[:, LANES:] * SM_SCALE

        @pl.when((kj == n_blk - 1) & (head % HEADS_PER_CHIP > 0))
        def _():
            dqr_ref[...] += dq_acc[:, LANES:] * SM_SCALE

    return _pcall(
        body, name="attn_bwd", grid=(N_HEADS, n_blk),
        in_specs=_q_specs(seq, lambda h, j: (0, h)) + _kv_specs(tq, lambda h, j: (j, h))
        + [pl.BlockSpec((seq, V_HEAD), lambda h, j: (0, h)),
           pl.BlockSpec((1, n_blk, tq), lambda h, j: (h, 0, 0)),
           pl.BlockSpec((1, n_blk, tq), lambda h, j: (h, 0, 0))],
        out_specs=[pl.BlockSpec((seq, LANES), lambda h, j: (0, h)),
                   pl.BlockSpec((seq, LANES), lambda h, j: (0, h // HEADS_PER_CHIP)),
                   pl.BlockSpec((tq, QK_NOPE + V_HEAD), lambda h, j: (j, h)),
                   pl.BlockSpec((1, tq, LANES), lambda h, j: (h, j, 0))],
        out_shape=[jax.ShapeDtypeStruct((seq, N_HEADS * QK_NOPE), F32),
                   jax.ShapeDtypeStruct((seq, N_CHIPS * LANES), F32),
                   jax.ShapeDtypeStruct((seq, N_HEADS * (QK_NOPE + V_HEAD)), BF16),
                   jax.ShapeDtypeStruct((N_HEADS, seq, LANES), F32)],
        scratch_shapes=[pltpu.VMEM((seq, 2 * LANES), F32)],
        compiler_params=_params(("arbitrary", "arbitrary")),
    )(q, q, kv, kr, kv, do, lse_row, delta_row)


def head_sum(x, ts=512):
    n_heads, seq, w = x.shape
    ts = min(ts, seq)

    def body(x_ref, o_ref):
        o_ref[...] = jnp.sum(x_ref[...], axis=0)

    return _pcall(body, name="head_sum", grid=(seq // ts,),
                  in_specs=[pl.BlockSpec((n_heads, ts, w), lambda i: (0, i, 0))],
                  out_specs=pl.BlockSpec((ts, w), lambda i: (i, 0)),
                  out_shape=jax.ShapeDtypeStruct((seq, w), F32),
                  compiler_params=_params(("parallel",)))(x)


HEADS_PER_CHIP = N_HEADS // N_CHIPS
Q_CHIP = HEADS_PER_CHIP * QK_DIM
Q_CHIP_NOPE = HEADS_PER_CHIP * QK_NOPE


def _perm_q_cols(w):
    t = w.reshape(w.shape[0], HEADS_PER_CHIP, QK_DIM)
    return jnp.concatenate([t[:, :, :QK_NOPE].reshape(w.shape[0], -1),
                            t[:, :, QK_NOPE:QK_NOPE + HALF_ROPE].reshape(w.shape[0], -1),
                            t[:, :, QK_NOPE + HALF_ROPE:].reshape(w.shape[0], -1)], axis=1)


def _unperm_q_cols(w):
    r = w.shape[0]
    nope = w[:, :Q_CHIP_NOPE].reshape(r, HEADS_PER_CHIP, QK_NOPE)
    r1 = w[:, Q_CHIP_NOPE:Q_CHIP_NOPE + QK_ROPE].reshape(r, HEADS_PER_CHIP, HALF_ROPE)
    r2 = w[:, Q_CHIP_NOPE + QK_ROPE:].reshape(r, HEADS_PER_CHIP, HALF_ROPE)
    return jnp.concatenate([nope, r1, r2], axis=2).reshape(r, Q_CHIP)


def _pad_kva_cols(w):
    z = jnp.zeros((w.shape[0], HALF_ROPE), w.dtype)
    return jnp.concatenate([w[:, :KV_LORA], w[:, KV_LORA:KV_LORA + HALF_ROPE], z, w[:, KV_LORA + HALF_ROPE:], z], axis=1)


def _unpad_kva_cols(w):
    return jnp.concatenate([w[:, :KV_LORA], w[:, KV_LORA:KV_LORA + HALF_ROPE],
                            w[:, KV_LORA + QK_ROPE:KV_LORA + QK_ROPE + HALF_ROPE]], axis=1)


def _rope_tile(t, cs, sn):
    return t * cs + pltpu.roll(t, LANES // 2, 1) * sn


def _rope_tile_bwd(d, cs, sn):
    return d * cs + pltpu.roll(d * sn, LANES // 2, 1)


def _b_cols(tk, tn):
    return pl.BlockSpec((None, tk, tn), lambda i, j, k: (j, k, 0))


def _b_cols_t(tk, tn):
    return pl.BlockSpec((None, tn, tk), lambda i, j, k: (k, j, 0))


def _out_cols(shape):
    return shape, lambda tm, tn: pl.BlockSpec((None, tm, tn), lambda i, j, k: (j, i, 0))


def _halves(a):
    return a.reshape(N_CHIPS, 2, a.shape[1] // 2, a.shape[2])


def device_step(x, positions, target, w, comm=None):
    seq = x.shape[0]
    w = dict(w)

    def gathered(names, outs):
        for n, a in zip(names, outs):
            if isinstance(n, tuple):
                w[n[0]] = [a.reshape(v.shape) if l == n[1] else v for l, v in enumerate(w[n[0]])]
            else:
                w[n] = a.reshape(w[n].shape)

    def ride_for(names):
        if comm is None:
            return None
        return GatherRide([_halves(w[n[0]][n[1]] if isinstance(n, tuple) else w[n]) for n in names])

    first_ride = ("ssm_w_glu", "ssm_w_out", ("w_ff1", 0), ("w_ff2", 0))
    mla_ride = ("kv_w_a", "kv_w_b", "q_w_a", "q_w_b", "attn_w_o")
    second_ride = (("w_ff1", 1), ("w_ff2", 1))

    inv_freq = ROPE_THETA ** (-jnp.arange(HALF_ROPE, dtype=F32) / HALF_ROPE)
    ang = positions.astype(F32)[:, None] * inv_freq
    cos, sin = jnp.cos(ang), jnp.sin(ang)
    zero = jnp.zeros_like(cos)
    cos_q, sin_q = jnp.concatenate([cos] * 4, 1), jnp.concatenate([-sin, -sin, sin, sin], 1)
    cos_k, sin_k = jnp.concatenate([cos, zero, cos, zero], 1), jnp.concatenate([-sin, zero, sin, zero], 1)
    ff_tile = D_FF // N_CHIPS
    pack_shape = (N_CHIPS, EARLY_ROWS, PACK_W)

    lr = w["ssm_lam_re"].reshape(N_STATES, 1)
    li = w["ssm_lam_im"].reshape(N_STATES, 1)
    ldt = jnp.repeat(w["ssm_log_dt"].reshape(N_GROUPS), SSM_STATE).reshape(N_STATES, 1)
    b_re = w["ssm_b_re"].reshape(N_STATES, SSM_GROUP)
    b_im = w["ssm_b_im"].reshape(N_STATES, SSM_GROUP)
    a_re, a_im, bb_re, bb_im = s5_prep(lr, li, ldt, b_re, b_im)
    a_re, a_im = a_re.reshape(1, N_STATES), a_im.reshape(1, N_STATES)
    bbd_re = _blockdiag_in(bb_re.reshape(N_GROUPS, SSM_STATE, SSM_GROUP)).astype(BF16)
    bbd_im = _blockdiag_in(bb_im.reshape(N_GROUPS, SSM_STATE, SSM_GROUP)).astype(BF16)
    cbd_re = _blockdiag_out(w["ssm_c_re"].reshape(N_GROUPS, SSM_GROUP, SSM_STATE)).astype(BF16)
    cbd_imn = _blockdiag_out(-w["ssm_c_im"].reshape(N_GROUPS, SSM_GROUP, SSM_STATE)).astype(BF16)
    dskip = w["ssm_d"].reshape(1, D_MODEL)
    (ypre, yg, h_re, h_im), landed = s5_fwd(x, bbd_re, bbd_im, cbd_re, cbd_imn, a_re, a_im, dskip, ride_for(first_ride))
    gathered(first_ride, landed)
    w_glu = w["ssm_w_glu"]
    glu_tile = w_glu.shape[2]
    vg = mm(yg, w_glu, n_dim=2 * D_MODEL, tiles=(None, glu_tile, None), b_view=_b_cols, name="glu_proj")

    def glu(v):
        return (v[:, :D_MODEL] * _sigmoid(v[:, D_MODEL:]),), ()
    (z,) = rowwise(glu, (vg,), ((D_MODEL, BF16),), name="glu")
    w_out = w["ssm_w_out"].reshape(D_MODEL, D_MODEL)
    mix0 = mm(z, w_out, name="ssm_out")

    def mlp_fwd(hb, layer, riding=None):
        pre = mm(hb, w["w_ff1"][layer], n_dim=D_FF, tiles=(None, ff_tile, None), b_view=_b_cols, name=f"ff1_{layer}",
                 out_dtypes=(BF16,), ride=ride_for(riding) if riding else None)
        if riding and comm is not None:
            pre, landed = pre
            gathered(riding, landed)
        f = mm(pre, w["w_ff2"][layer].reshape(D_FF, D_MODEL), pro_a=_relu2, name=f"ff2_{layer}")
        return pre, f

    ln = lambda name, l: w[name][l].reshape(1, D_MODEL)
    h1, h1b = ln_fwd(x, mix0, ln("ln_mix_g", 0), ln("ln_mix_b", 0), "ln_mix_0")
    f1pre, f1 = mlp_fwd(h1b, 0, mla_ride)
    h2, h2b = ln_fwd(h1, f1, ln("ln_ffn_g", 0), ln("ln_ffn_b", 0), "ln_ffn_0")

    kv_w_a = w["kv_w_a"].reshape(D_MODEL, KVA_PAD)
    kv_w_b = w["kv_w_b"]
    q_w_a = w["q_w_a"].reshape(D_MODEL, Q_LORA)
    q_w_b = w["q_w_b"]
    w_o = w["attn_w_o"].reshape(D_MODEL, D_MODEL)
    kvb_tile = kv_w_b.shape[2]
    kvn_g = w["kv_norm_g"].reshape(1, KV_LORA)
    qn_g = w["q_norm_g"].reshape(1, Q_LORA)
    kva = mm(h2b, kv_w_a, name="kv_a")

    def kv_post(kva, g, cs, sn):
        tile = _rope_tile(kva[:, KV_LORA:], cs, sn)
        return (_rms(kva[:, :KV_LORA], g), _cat(tile, pltpu.roll(tile, HALF_ROPE, 1))), ()
    ckv, krope = rowwise(kv_post, (kva, kvn_g, cos_k, sin_k), ((KV_LORA, BF16), (2 * LANES, BF16)), name="kv_post")
    kvb = mm(ckv, kv_w_b, n_dim=N_CHIPS * kvb_tile, tiles=(None, kvb_tile, KV_LORA), b_view=_b_cols, name="kv_b",
             out_dtypes=(BF16,))
    cq_raw, cq = mm(h2b, q_w_a, epi=lambda r, gq: (r, _rms(r, gq)), extras=(qn_g,), out_dtypes=(F32, BF16), name="q_a")

    def rope_and_scale(r, cs, sn):
        return (_cat(r[:, :Q_CHIP_NOPE], _rope_tile(r[:, Q_CHIP_NOPE:], cs, sn)) * Q_PRESCALE,)
    qro = mm(cq, q_w_b, n_dim=N_CHIPS * Q_CHIP, tiles=(None, Q_CHIP, Q_LORA), b_view=_b_cols, epi=rope_and_scale,
             extras=(cos_q, sin_q), out_dtypes=(BF16,), name="q_b")
    (o, lse), landed = attn_fwd(qro, kvb, krope, ride_for(second_ride))
    gathered(second_ride, landed)
    mix1 = mm(o, w_o, name="attn_out")
    h3, h3b = ln_fwd(h2, mix1, ln("ln_mix_g", 1), ln("ln_mix_b", 1), "ln_mix_1")
    f2pre, f2 = mlp_fwd(h3b, 1)
    def last_ln_and_loss(h, mix, gl, bl, t):
        e = _layer_norm(h, mix, gl, bl) - t
        return (e * (1.0 / D_MODEL),), (jnp.broadcast_to(jnp.sum(e * e), (1, LANES)),)
    dh4, loss_acc = rowwise(last_ln_and_loss, (h3, f2, ln("ln_ffn_g", 1), ln("ln_ffn_b", 1), target), ((D_MODEL, F32),),
                            accs=(LANES,), name="ln_ffn_1_loss")
    loss = loss_acc[0, 0] * (0.5 / D_MODEL)

    g = {}

    def into_rows(off, rows_per_chip, shape=pack_shape):
        def view(tm, tn):
            if tm == N_CHIPS * rows_per_chip:
                return pl.BlockSpec((N_CHIPS, rows_per_chip, tn), lambda i, j, k: (0, off // rows_per_chip, 0))
            nb = rows_per_chip // tm
            return pl.BlockSpec((None, tm, tn), lambda i, j, k: (i // nb, off // tm + i % nb, 0))
        return shape, view

    def into_cols(off):
        return pack_shape, lambda tm, tn: pl.BlockSpec((None, tm, tn), lambda i, j, k: (j, off // tm + i, 0))

    def mlp_bwd(pack, dr, drb, hb, pre, layer, swap=False):
        w2_rows = (EARLY_OFF["w_ff2"] + layer * ff_tile, ff_tile)
        w1_rows = (EARLY_OFF["w_ff1"] + layer * D_MODEL, D_MODEL)
        ready = [(w1_rows[0] + w1_rows[1], w2_rows[0] - w1_rows[0] - w1_rows[1]), (w2_rows[0] + w2_rows[1], EARLY_ROWS - w2_rows[0] - w2_rows[1])]
        dpre = mm(drb, w["w_ff2"][layer].reshape(D_FF, D_MODEL), tb=True, epi=lambda r, p: (r * 2.0 * jnp.maximum(p, 0.0),),
                  extras=(pre,), out_dtypes=(BF16,), tiles=(None, ff_tile, None), name=f"ff2_dx_{layer}",
                  ride=SwapRide(pack, ready) if swap else None)
        if swap:
            dpre, (theirs,) = dpre
        pack = mm(pre, drb, ta=True, pro_a=_relu2, name=f"ff2_dw_{layer}", tiles=(ff_tile, PACK_W, None), into=pack,
                  out_view=into_rows(w2_rows[0], ff_tile))
        pack = mm(hb, dpre, ta=True, name=f"ff1_dw_{layer}", tiles=(None, PACK_W, None), into=pack,
                  out_view=into_cols(w1_rows[0]))
        dh = mm(dpre, w["w_ff1"][layer], tb=True, epi=lambda r, d: (r + DN_ALPHA * d,), extras=(dr,), n_dim=D_MODEL,
                tiles=(None, D_MODEL, ff_tile), b_view=_b_cols_t, name=f"ff1_dx_{layer}",
                ride=SwapRide(pack, [w1_rows, w2_rows], into=theirs) if swap else None)
        return (pack, *dh) if swap else (pack, dh)

    dr4, dr4b, dg_f1, db_f1 = ln_bwd(h3, f2, ln("ln_ffn_g", 1), dh4, "ln_ffn_bwd_1")
    pack, dh3 = mlp_bwd(None, dr4, dr4b, h3b, f2pre, 1)
    dr3, dr3b, dg_m1, db_m1 = ln_bwd(h2, mix1, ln("ln_mix_g", 1), dh3, "ln_mix_bwd_1")
    shard_rows = D_MODEL // N_CHIPS
    pack = mm(o, dr3b, ta=True, name="attn_out_dw", tiles=(D_MODEL, PACK_W, None), into=pack,
              out_view=into_rows(EARLY_OFF["attn_w_o"], shard_rows))
    do = mm(dr3b, w_o, tb=True, name="attn_out_dx")
    def head_dots(do, o):
        return (jnp.concatenate([jnp.sum(do[:, V_HEAD * h:V_HEAD * (h + 1)] * o[:, V_HEAD * h:V_HEAD * (h + 1)], axis=1,
                                         keepdims=True) for h in range(N_HEADS)], axis=1),), ()
    (delta,) = rowwise(head_dots, (do, o), ((N_HEADS, F32),), name="attn_delta")
    tb = min(ATT_TK, seq)
    lse_row = lse.reshape(N_HEADS, seq // tb, tb)
    delta_row = delta.T.reshape(N_HEADS, seq // tb, tb)
    dqn, dqr, dkvb, dkr = attn_bwd(qro, kvb, krope, do, lse_row, delta_row)

    def q_rope_bwd(dn, dr, cs, sn):
        parts = []
        for k in range(N_CHIPS):
            parts.append(dn[:, Q_CHIP_NOPE * k:Q_CHIP_NOPE * (k + 1)])
            parts.append(_rope_tile_bwd(dr[:, LANES * k:LANES * (k + 1)], cs, sn))
        return (jnp.concatenate(parts, axis=1),), ()
    (dqlin,) = rowwise(q_rope_bwd, (dqn, dqr, cos_q, sin_q), ((N_CHIPS * Q_CHIP, BF16),), name="q_rope_bwd")
    g["q_w_b"] = mm(cq, dqlin, ta=True, name="q_b_dw", tiles=(Q_LORA, Q_CHIP, None), out_view=_out_cols(q_w_b.shape))
    dcq = mm(dqlin, q_w_b, tb=True, n_dim=Q_LORA, tiles=(None, Q_LORA, Q_CHIP), b_view=_b_cols_t, name="q_b_dx")

    def q_norm_bwd(c, gq, d):
        dx, dgq = _rms_bwd(c, gq, d)
        return (dx,), (dgq,)
    dcq_raw, dqn_g = rowwise(q_norm_bwd, (cq_raw, qn_g, dcq), ((Q_LORA, BF16),), accs=(Q_LORA,), name="q_norm_bwd")
    g["q_w_a"] = mm(h2b, dcq_raw, ta=True, name="q_a_dw")
    g["kv_w_b"] = mm(ckv, dkvb, ta=True, name="kv_b_dw", tiles=(KV_LORA, kvb_tile, None), out_view=_out_cols(kv_w_b.shape))
    dckv = mm(dkvb, kv_w_b, tb=True, n_dim=KV_LORA, tiles=(None, KV_LORA, kvb_tile), b_view=_b_cols_t, name="kv_b_dx")
    dkr_sum = head_sum(dkr)

    def kv_post_bwd(kva, gk, dc, dk, cs, sn):
        dx, dgk = _rms_bwd(kva[:, :KV_LORA], gk, dc)
        dk = dk + pltpu.roll(dk, LANES - HALF_ROPE, 1)
        return (jnp.concatenate([dx, _rope_tile_bwd(dk, cs, sn)], axis=1),), (dgk,)
    dkva, dkvn_g = rowwise(kv_post_bwd, (kva, kvn_g, dckv, dkr_sum, cos_k, sin_k), ((KVA_PAD, BF16),),
                           accs=(KV_LORA,), name="kv_post_bwd")
    g["kv_w_a"] = mm(h2b, dkva, ta=True, name="kv_a_dw")
    dh2 = mm(dcq_raw, q_w_a, tb=True, epi=lambda r, d: (r + DN_ALPHA * d,), extras=(dr3,), name="q_a_dx")
    dh2 = mm(dkva, kv_w_a, tb=True, epi=lambda r, d: (r + d,), extras=(dh2,), name="kv_a_dx")

    dr2, dr2b, dg_f0, db_f0 = ln_bwd(h1, f1, ln("ln_ffn_g", 0), dh2, "ln_ffn_bwd_0")
    pack = put_rows(pack, packed_shards(g, MISC_EARLY, EARLY_ROWS - MISC_EARLY_OFF), MISC_EARLY_OFF)
    if comm is None:
        pack, dh1 = mlp_bwd(pack, dr2, dr2b, h1b, f1pre, 0)
    else:
        pack, dh1, (theirs,) = mlp_bwd(pack, dr2, dr2b, h1b, f1pre, 0, swap=True)
        early_sums = add_halves(pack, theirs, comm[1])
    dr1, dr1b, dg_m0, db_m0 = ln_bwd(x, mix0, ln("ln_mix_g", 0), dh1, "ln_mix_bwd_0")
    mid = mm(z, dr1b, ta=True, name="ssm_out_dw", tiles=(D_MODEL, PACK_W, None),
             out_view=into_rows(MID_OFF["ssm_w_out"], shard_rows, (N_CHIPS, MID_ROWS, PACK_W)))
    dz = mm(dr1b, w_out, tb=True, name="ssm_out_dx")

    def glu_bwd(v, dz):
        val, sg = v[:, :D_MODEL], _sigmoid(v[:, D_MODEL:])
        return (jnp.concatenate([dz * sg, dz * val * sg * (1.0 - sg)], axis=1),), ()
    (dvg,) = rowwise(glu_bwd, (vg, dz), ((2 * D_MODEL, BF16),), name="glu_bwd")
    g["ssm_w_glu"] = mm(yg, dvg, ta=True, name="glu_proj_dw", tiles=(None, glu_tile, None), out_view=_out_cols(w_glu.shape))
    mid = put_rows(mid, packed_shards(g, MISC_MID, MID_ROWS - MISC_MID_OFF), MISC_MID_OFF)
    dypre = mm(dvg, w_glu, tb=True, epi=lambda r, y: (r * _gelu_grad(y),), extras=(ypre,), n_dim=D_MODEL,
               tiles=(None, D_MODEL, glu_tile), b_view=_b_cols_t, name="glu_proj_dx",
               ride=SwapRide(mid) if comm is not None else None)
    sends = None
    if comm is not None:
        dypre, (theirs,) = dypre
        sends = SendRide([early_sums, add_halves(mid, theirs, comm[1])])
    (dx, dbbd_re, dbbd_im, dcbd_re, dcbd_imn, dar, dai, dd), got = s5_bwd(
        dypre, x, dr1, h_re, h_im, bbd_re, bbd_im, cbd_re, cbd_imn, a_re, a_im, dskip, sends)
    dbb_re = _blockdiag_in_t(dbbd_re).reshape(N_STATES, SSM_GROUP)
    dbb_im = _blockdiag_in_t(dbbd_im).reshape(N_STATES, SSM_GROUP)
    dlr, dli, dldt, db_re, db_im = s5_prep_bwd(lr, li, ldt, b_re, b_im, dar.reshape(N_STATES, 1),
                                               dai.reshape(N_STATES, 1), dbb_re, dbb_im)
    g["ssm_lam_re"] = dlr.reshape(1, N_GROUPS, SSM_STATE)
    g["ssm_lam_im"] = dli.reshape(1, N_GROUPS, SSM_STATE)
    g["ssm_log_dt"] = group_sum(dldt).reshape(1, N_GROUPS)
    g["ssm_b_re"] = db_re.reshape(1, N_GROUPS, SSM_STATE, SSM_GROUP)
    g["ssm_b_im"] = db_im.reshape(1, N_GROUPS, SSM_STATE, SSM_GROUP)
    g["ssm_c_re"] = _blockdiag_out_t(dcbd_re).reshape(1, N_GROUPS, SSM_GROUP, SSM_STATE)
    g["ssm_c_im"] = -_blockdiag_out_t(dcbd_imn).reshape(1, N_GROUPS, SSM_GROUP, SSM_STATE)
    g["ssm_d"] = dd
    g["ln_mix_g"] = jnp.concatenate([dg_m0, dg_m1], 0)
    g["ln_mix_b"] = jnp.concatenate([db_m0, db_m1], 0)
    g["ln_ffn_g"] = jnp.concatenate([dg_f0, dg_f1], 0)
    g["ln_ffn_b"] = jnp.concatenate([db_f0, db_f1], 0)
    g["kv_norm_g"] = dkvn_g.reshape(KV_LORA)
    g["q_norm_g"] = dqn_g
    return loss, dx, pack, mid, g, list(zip(sends.ins, got)) if comm is not None else None


def place(shard, me_idx, dtype, name, layer=None):
    rows, cols = shard.shape[-2:]
    tr = _tile(rows, (512, 256, 128))

    def body(m_ref, x_ref, o_ref):
        o_ref[...] = x_ref[...].astype(o_ref.dtype)

    in_spec = (pl.BlockSpec((tr, cols), lambda i, m: (i, 0)) if layer is None
               else pl.BlockSpec((None, tr, cols), lambda i, m: (layer, i, 0)))
    return _pcall(
        body, name=name,
        grid_spec=pltpu.PrefetchScalarGridSpec(
            num_scalar_prefetch=1, grid=(rows // tr,), in_specs=[in_spec],
            out_specs=pl.BlockSpec((None, tr, cols), lambda i, m: (m[0], i, 0))),
        out_shape=jax.ShapeDtypeStruct((N_CHIPS, rows, cols), dtype),
        compiler_params=_params(("parallel",)),
    )(me_idx, shard)


def place_many(shards, dtypes, me_idx, name):
    def body(m_ref, *refs):
        for x_ref, o_ref in zip(refs[:len(shards)], refs[len(shards):]):
            o_ref[...] = x_ref[...].astype(o_ref.dtype)

    return _pcall(
        body, name=name,
        grid_spec=pltpu.PrefetchScalarGridSpec(
            num_scalar_prefetch=1, grid=(1,),
            in_specs=[pl.BlockSpec(s.shape, lambda i, m: (0, 0)) for s in shards],
            out_specs=[pl.BlockSpec((None,) + s.shape, lambda i, m: (m[0], 0, 0)) for s in shards]),
        out_shape=[jax.ShapeDtypeStruct((N_CHIPS,) + s.shape, d) for s, d in zip(shards, dtypes)],
        compiler_params=_params(("arbitrary",)),
    )(me_idx, *shards)


def put_rows(pack, rows, off):
    _, n, cols = rows.shape

    def body(r_ref, p_ref, o_ref, sem):
        cp = pltpu.make_async_copy(r_ref.at[0], o_ref.at[pl.program_id(0), pl.ds(off, n), :], sem)
        cp.start()
        cp.wait()

    return _pcall(body, name="grad_put_rows", grid=(N_CHIPS,),
                  in_specs=[pl.BlockSpec((1, n, cols), lambda k: (k, 0, 0)), _ANY], out_specs=_ANY,
                  out_shape=jax.ShapeDtypeStruct(pack.shape, pack.dtype), input_output_aliases={1: 0},
                  scratch_shapes=[pltpu.SemaphoreType.DMA],
                  compiler_params=_params(("arbitrary",)))(rows, pack)


def _my_cols(c, mine=True):
    start = (c if mine else 1 - c) * HALF_W
    return pl.ds(pl.multiple_of(start, HALF_W), HALF_W)


def add_halves(gpack, got, c_idx):
    n, rows, _ = gpack.shape
    tr = min(G_BLOCK_ROWS, rows)
    blk = (None, tr, HALF_W)

    def body(c_ref, g_ref, r_ref, o_ref):
        o_ref[...] = (g_ref[...] + r_ref[...]).astype(o_ref.dtype)

    return _pcall(
        body, name="grad_add_halves",
        grid_spec=pltpu.PrefetchScalarGridSpec(
            num_scalar_prefetch=1, grid=(n, rows // tr),
            in_specs=[pl.BlockSpec(blk, lambda k, i, c: (k, i, c[0])), pl.BlockSpec(blk, lambda k, i, c: (k, i, 0))],
            out_specs=pl.BlockSpec(blk, lambda k, i, c: (k, i, 0))),
        out_shape=jax.ShapeDtypeStruct((n, rows, HALF_W), BF16),
        compiler_params=_params(("parallel", "parallel")),
    )(c_idx, gpack, got)


def sum_owner(part, got, idx, total_rows, row_off=0, into=None):
    _, rows, _ = part.shape
    tr = math.gcd(math.gcd(rows, row_off), G_BLOCK_ROWS)
    n_into = 0 if into is None else 1

    def body(m_ref, p_ref, g_ref, *rest):
        up = lambda v: v.astype(F32)
        rest[-1][...] = ((up(p_ref[...]) + up(g_ref[0])) + up(g_ref[1])) + up(g_ref[2])

    return _pcall(
        body, name="grad_sum_owner",
        grid_spec=pltpu.PrefetchScalarGridSpec(
            num_scalar_prefetch=1, grid=(rows // tr,),
            in_specs=[pl.BlockSpec((None, tr, HALF_W), lambda i, m: (m[0], i, 0)),
                      pl.BlockSpec((3, tr, HALF_W), lambda i, m: (0, i, 0))] + [_ANY] * n_into,
            out_specs=pl.BlockSpec((tr, HALF_W), lambda i, m: (row_off // tr + i, m[1]))),
        out_shape=jax.ShapeDtypeStruct((total_rows, PACK_W), F32),
        input_output_aliases={3: 0} if n_into else {},
        compiler_params=_params(("parallel",)),
    )(idx, part, got, *([into] if n_into else []))


def join_halves(red):
    def body(in_ref, out_ref, send_sem, recv_sem):
        x, y, c, _ = _place()
        sibling = (x, y, 1 - c)
        mine = out_ref.at[:, _my_cols(c)]
        cp = pltpu.make_async_remote_copy(src_ref=mine, dst_ref=mine, send_sem=send_sem, recv_sem=recv_sem,
                                          device_id=sibling, device_id_type=MESH)
        cp.start()
        cp.wait_send()
        other = out_ref.at[:, _my_cols(c, mine=False)]
        pltpu.make_async_remote_copy(src_ref=other, dst_ref=other, send_sem=send_sem, recv_sem=recv_sem,
                                     device_id=sibling, device_id_type=MESH).wait_recv()

    return _pcall(body, name="grad_join_halves", in_specs=[_ANY], out_specs=_ANY,
                  out_shape=jax.ShapeDtypeStruct(red.shape, red.dtype), input_output_aliases={0: 0},
                  scratch_shapes=[pltpu.SemaphoreType.DMA, pltpu.SemaphoreType.DMA])(red)


def adamw(gsrc, g_off, wt, m, v, name):
    n, cols = wt.shape
    tr = math.gcd(math.gcd(g_off, n), 256) if g_off else math.gcd(n, 256)
    off_blk = g_off // tr
    c1 = 1.0 / (1.0 - ADAM_B1 ** ADAM_STEP)
    c2 = 1.0 / (1.0 - ADAM_B2 ** ADAM_STEP)

    def body(g_ref, w_ref, m_ref, v_ref, go_ref, d_ref, mo_ref, vo_ref):
        gv = g_ref[...]
        mn = ADAM_B1 * m_ref[...] + (1.0 - ADAM_B1) * gv
        vn = ADAM_B2 * v_ref[...] + (1.0 - ADAM_B2) * gv * gv
        go_ref[...] = gv
        mo_ref[...] = mn
        vo_ref[...] = vn
        d_ref[...] = -ADAM_LR * ((mn * c1) / (jnp.sqrt(vn * c2) + ADAM_EPS) + ADAM_WD * w_ref[...])

    blk = pl.BlockSpec((tr, cols), lambda i: (i, 0))
    return _pcall(body, name=name, grid=(n // tr,),
                  in_specs=[pl.BlockSpec((tr, cols), lambda i: (off_blk + i, 0)), blk, blk, blk],
                  out_specs=[blk] * 4, out_shape=[jax.ShapeDtypeStruct((n, cols), F32)] * 4,
                  compiler_params=_params(("parallel",)))(gsrc, wt, m, v)


def _rows8(a):
    return -(-a.size // (8 * PACK_W)) * 8


def _as_rows(a, rows=None):
    flat = a.reshape(-1)
    n = _rows8(a) if rows is None else rows
    return jnp.pad(flat, (0, n * PACK_W - flat.shape[0])).reshape(n, PACK_W)


def local_shards_2d(wl):
    return {"w_ff1": [wl["w_ff1"][0], wl["w_ff1"][1]], "w_ff2": [wl["w_ff2"][0], wl["w_ff2"][1]],
            "ssm_w_glu": wl["ssm_w_glu"], "ssm_w_out": wl["ssm_w_out"], "kv_w_a": _pad_kva_cols(wl["kv_w_a"]),
            "kv_w_b": wl["kv_w_b"], "q_w_a": wl["q_w_a"], "q_w_b": _perm_q_cols(wl["q_w_b"]),
            "attn_w_o": wl["attn_w_o"], "ssm_d": wl["ssm_d"].reshape(2, -1)}


def misc_grad_shard(name, g, k):
    if name == "ssm_d":
        w = D_MODEL // N_CHIPS
        return g[:, w * k:w * (k + 1)]
    if name in ("ssm_w_glu", "kv_w_b"):
        return g[k]
    if name == "q_w_b":
        return _unperm_q_cols(g[k])
    rows = D_MODEL // N_CHIPS
    shard = g[rows * k:rows * (k + 1)]
    return _unpad_kva_cols(shard) if name == "kv_w_a" else shard


def packed_shards(g, names, rows, tail=None):
    blocks = []
    for k in range(N_CHIPS):
        parts = [_as_rows(misc_grad_shard(n, g[n], k), MISC_SHARD_ROWS[n]) for n in names]
        if tail is not None:
            parts.append(tail[k * (tail.shape[0] // N_CHIPS):(k + 1) * (tail.shape[0] // N_CHIPS)])
        blk = jnp.concatenate(parts, axis=0)
        blocks.append(jnp.pad(blk, ((0, rows - blk.shape[0]), (0, 0))))
    return jnp.stack(blocks)


def kernel(x, positions, ln_mix_g, ln_mix_b, ln_ffn_g, ln_ffn_b, w_ff1, w_ff2, ssm_lam_re, ssm_lam_im, ssm_log_dt, ssm_b_re, ssm_b_im, ssm_c_re, ssm_c_im, ssm_d, ssm_w_glu, ssm_w_out, kv_w_a, kv_norm_g, kv_w_b, q_w_a, q_norm_g, q_w_b, attn_w_o, loss_target, m_ln_mix_g, m_ln_mix_b, m_ln_ffn_g, m_ln_ffn_b, m_w_ff1, m_w_ff2, m_ssm_lam_re, m_ssm_lam_im, m_ssm_log_dt, m_ssm_b_re, m_ssm_b_im, m_ssm_c_re, m_ssm_c_im, m_ssm_d, m_ssm_w_glu, m_ssm_w_out, m_kv_w_a, m_kv_norm_g, m_kv_w_b, m_q_w_a, m_q_norm_g, m_q_w_b, m_attn_w_o, v_ln_mix_g, v_ln_mix_b, v_ln_ffn_g, v_ln_ffn_b, v_w_ff1, v_w_ff2, v_ssm_lam_re, v_ssm_lam_im, v_ssm_log_dt, v_ssm_b_re, v_ssm_b_im, v_ssm_c_re, v_ssm_c_im, v_ssm_d, v_ssm_w_glu, v_ssm_w_out, v_kv_w_a, v_kv_norm_g, v_kv_w_b, v_q_w_a, v_q_norm_g, v_q_w_b, v_attn_w_o):
    env = dict(locals())
    wl = {n: env[n] for n in WEIGHTS}
    ml = {n: env["m_" + n] for n in WEIGHTS}
    vl = {n: env["v_" + n] for n in WEIGHTS}
    for n in ("ssm_w_glu", "ssm_w_out", "q_w_a", "q_w_b", "attn_w_o"):
        wl[n], ml[n], vl[n] = wl[n][0], ml[n][0], vl[n][0]

    c_idx = lax.axis_index("c").astype(jnp.int32).reshape(1)
    me_idx = (2 * lax.axis_index("x") + lax.axis_index("y")).astype(jnp.int32).reshape(1)

    local = local_shards_2d(wl)
    stacked = {n: [place(wl[n], me_idx, BF16, f"place_{n}_{l}", layer=l) for l in range(DEPTH)] for n in ("w_ff1", "w_ff2")}
    others = [n for n in SHARDED if n not in stacked]
    stacked.update(zip(others, place_many([local[n] for n in others], [F32 if n == "ssm_d" else BF16 for n in others],
                                          me_idx, "place_others")))
    stacked["ssm_d"] = ride_alone(GatherRide([_halves(stacked["ssm_d"])]), "ssm_d_all_gather")[0].reshape(1, D_MODEL)
    for n in REPLICATED:
        stacked[n] = wl[n]

    loss_part, dx, early, mid, g, sent = device_step(x[0], positions[0], loss_target[0], stacked, comm=(me_idx, c_idx))
    loss = lax.psum(loss_part, ("x", "y", "c"))

    small = jnp.concatenate([_as_rows(g[n]) for n in REPLICATED], axis=0)
    small = jnp.pad(small, ((0, SMALL_ROWS - small.shape[0]), (0, 0)))
    late = packed_shards(g, MISC_LATE, LATE_ROWS, tail=small)
    late_sums = add_halves(late, ride_alone(SwapRide(late), "grad_swap_halves")[0], c_idx)
    sent.append((late_sums, ride_alone(SendRide([late_sums]), "grad_send_to_owners")[0]))
    where = jnp.concatenate([me_idx, c_idx])
    starts = (0, EARLY_ROWS, EARLY_ROWS + MID_ROWS)
    total_rows = EARLY_ROWS + MID_ROWS + LATE_ROWS
    reduced = None
    for (sums, got), off in zip(sent, starts):
        reduced = sum_owner(sums, got, where, total_rows, row_off=off, into=reduced)
    reduced = join_halves(reduced)
    quarter = reduced[starts[2] + SMALL_OFF:starts[2] + SMALL_OFF + SMALL_Q_ROWS]
    small_tot = ride_alone(GatherRide([_halves(place(quarter, me_idx, F32, "place_small_grads"))]),
                           "small_grad_all_gather")[0].reshape(SMALL_ROWS, PACK_W)

    out_g, out_d, out_m, out_v = {}, {}, {}, {}
    direct = {**EARLY_OFF, **{n: starts[1] + o for n, o in MID_OFF.items()}}
    for n, off in direct.items():
        res = adamw(reduced, off, wl[n].reshape(-1, PACK_W), ml[n].reshape(-1, PACK_W), vl[n].reshape(-1, PACK_W),
                    "adamw_" + n)
        out_g[n], out_d[n], out_m[n], out_v[n] = [a.reshape(env[n].shape) for a in res]
    for names, off in ((MISC_EARLY, MISC_EARLY_OFF), (MISC_MID, starts[1] + MISC_MID_OFF), (MISC_LATE, starts[2])):
        pack3 = lambda d: jnp.concatenate([_as_rows(d[n], MISC_SHARD_ROWS[n]) for n in names], axis=0)
        res = adamw(reduced, off, pack3(wl), pack3(ml), pack3(vl), "adamw_packed_" + names[0])
        r0 = 0
        for n in names:
            cnt = math.prod(env[n].shape)
            out_g[n], out_d[n], out_m[n], out_v[n] = [
                a[r0:r0 + MISC_SHARD_ROWS[n]].reshape(-1)[:cnt].reshape(env[n].shape) for a in res]
            r0 += MISC_SHARD_ROWS[n]
    ws = jnp.concatenate([_as_rows(wl[n]) for n in REPLICATED], axis=0)
    ms = jnp.concatenate([_as_rows(ml[n]) for n in REPLICATED], axis=0)
    vs = jnp.concatenate([_as_rows(vl[n]) for n in REPLICATED], axis=0)
    pad = ((0, SMALL_ROWS - ws.shape[0]), (0, 0))
    res = adamw(small_tot, 0, jnp.pad(ws, pad), jnp.pad(ms, pad), jnp.pad(vs, pad), "adamw_replicated")
    row = 0
    for n in REPLICATED:
        cnt = math.prod(env[n].shape)
        nrows = _rows8(env[n])
        out_g[n], out_d[n], out_m[n], out_v[n] = [a[row:row + nrows].reshape(-1)[:cnt].reshape(env[n].shape) for a in res]
        row += nrows

    return (loss, dx[None], *[out_g[n] for n in WEIGHTS], *[out_d[n] for n in WEIGHTS],
            *[out_m[n] for n in WEIGHTS], *[out_v[n] for n in WEIGHTS])
```

```python
import functools
import math

import jax
import jax.numpy as jnp
from jax import lax
from jax.experimental import pallas as pl
from jax.experimental.pallas import tpu as pltpu

F32 = jnp.float32
BF16 = jnp.bfloat16
MESH = pl.DeviceIdType.MESH

D_MODEL = 1024
DEPTH = 2
SSM_GROUP = 16
N_GROUPS = D_MODEL // SSM_GROUP
SSM_STATE = 64
N_STATES = N_GROUPS * SSM_STATE
N_HEADS = 8
QK_NOPE = 128
QK_ROPE = 64
HALF_ROPE = QK_ROPE // 2
V_HEAD = 128
QK_DIM = QK_NOPE + QK_ROPE
Q_LORA = 384
KV_LORA = 256
ROPE_THETA = 10000.0
SM_SCALE = QK_DIM ** -0.5
NEG_INF = -1e30
D_FF = 4 * D_MODEL
DN_ALPHA = (2 * DEPTH) ** 0.25
LN_EPS = 1e-5
RMS_EPS = 1e-6
ADAM_LR = 0.001
ADAM_B1 = 0.9
ADAM_B2 = 0.999
ADAM_EPS = 1e-08
ADAM_WD = 0.01
ADAM_STEP = 10

N_CHIPS = 4
LANES = 128
VMEM_LIMIT = 56 * 1024 * 1024
MM_VMEM_BUDGET = 40 * 1024 * 1024
PACK_W = 1024
KVA_PAD = 384
HALF_W = PACK_W // 2

SHARDED = ("w_ff1", "w_ff2", "ssm_w_glu", "ssm_w_out", "kv_w_a", "kv_w_b", "q_w_a", "q_w_b", "attn_w_o", "ssm_d")
G_BLOCK_ROWS = 960
EARLY_OFF = {"w_ff1": 0, "w_ff2": 2048, "attn_w_o": 4096}
MISC_EARLY = ("kv_w_b", "kv_w_a", "q_w_a", "q_w_b")
MISC_EARLY_OFF = 4352
EARLY_ROWS = 5 * G_BLOCK_ROWS
MID_OFF = {"ssm_w_out": 0}
MISC_MID = ("ssm_w_glu",)
MISC_MID_OFF = 256
MID_ROWS = MISC_MID_OFF + 512
MISC_LATE = ("ssm_d",)
SMALL_Q_ROWS = 96
SMALL_ROWS = N_CHIPS * SMALL_Q_ROWS
SMALL_OFF = 16
LATE_ROWS = 192
MISC_SHARD_ROWS = {"ssm_d": 16, "ssm_w_glu": 512, "kv_w_b": 128, "kv_w_a": 80, "q_w_a": 96, "q_w_b": 144}
REPLICATED = ("ln_mix_g", "ln_mix_b", "ln_ffn_g", "ln_ffn_b", "ssm_lam_re", "ssm_lam_im", "ssm_log_dt",
              "ssm_b_re", "ssm_b_im", "ssm_c_re", "ssm_c_im", "kv_norm_g", "q_norm_g")
WEIGHTS = ("ln_mix_g", "ln_mix_b", "ln_ffn_g", "ln_ffn_b", "w_ff1", "w_ff2", "ssm_lam_re", "ssm_lam_im",
           "ssm_log_dt", "ssm_b_re", "ssm_b_im", "ssm_c_re", "ssm_c_im", "ssm_d", "ssm_w_glu", "ssm_w_out",
           "kv_w_a", "kv_norm_g", "kv_w_b", "q_w_a", "q_norm_g", "q_w_b", "attn_w_o")


def _pcall(body, **kw):
    return pl.pallas_call(body, **kw)


def _params(sem=None):
    return pltpu.CompilerParams(dimension_semantics=sem, vmem_limit_bytes=VMEM_LIMIT)


_ANY = pl.BlockSpec(memory_space=pl.ANY)


def _tile(dim, prefs):
    for p in prefs:
        if dim % p == 0:
            return p
    return dim


def _place():
    x, y, c = lax.axis_index("x"), lax.axis_index("y"), lax.axis_index("c")
    return x, y, c, [(1 - x, y), (x, 1 - y), (1 - x, 1 - y)]


def _remote(k, src, dst, to, send_sems, recv_sems):
    return pltpu.make_async_remote_copy(src_ref=src, dst_ref=dst, send_sem=send_sems.at[k], recv_sem=recv_sems.at[k],
                                        device_id=to, device_id_type=MESH)


class GatherRide:
    def __init__(self, arrs):
        self.ins = list(arrs)
        self.out_shapes = [jax.ShapeDtypeStruct(a.shape, a.dtype) for a in arrs]
        self.aliases = {i: i for i in range(len(arrs))}
        self.n_sems = 6 * len(arrs)

    def start(self, ins, outs, send_sems, recv_sems):
        x, y, c, chips = _place()
        me = 2 * x + y
        for a, o in enumerate(outs):
            for j, (px, py) in enumerate(chips):
                _remote(6 * a + j, o.at[me, c], o.at[me, c], (px, py, c), send_sems, recv_sems).start()

    def pass_on(self, ins, outs, send_sems, recv_sems):
        x, y, c, chips = _place()
        for a, o in enumerate(outs):
            for j, (px, py) in enumerate(chips):
                blk = o.at[2 * px + py, c]
                _remote(6 * a + j, blk, blk, (px, py, c), send_sems, recv_sems).wait_recv()
                _remote(6 * a + 3 + j, blk, blk, (x, y, 1 - c), send_sems, recv_sems).start()

    def finish(self, ins, outs, send_sems, recv_sems, passed_on=False):
        if not passed_on:
            self.pass_on(ins, outs, send_sems, recv_sems)
        x, y, c, chips = _place()
        me = 2 * x + y
        sibling = (x, y, 1 - c)
        for a, o in enumerate(outs):
            for j, (px, py) in enumerate(chips):
                blk = o.at[2 * px + py, 1 - c]
                _remote(6 * a + 3 + j, blk, blk, sibling, send_sems, recv_sems).wait_recv()
                _remote(6 * a + j, o.at[me, c], o.at[me, c], (px, py, c), send_sems, recv_sems).wait_send()
                mine = o.at[2 * px + py, c]
                _remote(6 * a + 3 + j, mine, mine, sibling, send_sems, recv_sems).wait_send()


class SendRide:
    def __init__(self, parts):
        self.ins = list(parts)
        self.out_shapes = [jax.ShapeDtypeStruct((3,) + p.shape[1:], p.dtype) for p in parts]
        self.aliases = {}
        self.n_sems = 3 * len(parts)

    def _copies(self, ins, outs, send_sems, recv_sems):
        x, y, c, chips = _place()
        return [_remote(3 * a + j, ins[a].at[2 * px + py], outs[a].at[j], (px, py, c), send_sems, recv_sems)
                for a in range(len(ins)) for j, (px, py) in enumerate(chips)]

    def start(self, ins, outs, send_sems, recv_sems):
        for cp in self._copies(ins, outs, send_sems, recv_sems):
            cp.start()

    def finish(self, ins, outs, send_sems, recv_sems):
        for cp in self._copies(ins, outs, send_sems, recv_sems):
            cp.wait()


class SwapRide:
    def __init__(self, pack, ranges=None, into=None):
        self.ins = [pack] if into is None else [pack, into]
        self.out_shapes = [jax.ShapeDtypeStruct(pack.shape[:2] + (HALF_W,), pack.dtype)]
        self.aliases = {} if into is None else {1: 0}
        self.ranges = ranges or [(0, pack.shape[1])]
        self.n_sems = len(self.ranges)

    def _copies(self, ins, outs, send_sems, recv_sems):
        x, y, c, _ = _place()
        return [_remote(k, ins[0].at[:, pl.ds(r0, n), _my_cols(c, mine=False)], outs[0].at[:, pl.ds(r0, n), :],
                        (x, y, 1 - c), send_sems, recv_sems) for k, (r0, n) in enumerate(self.ranges)]

    def start(self, ins, outs, send_sems, recv_sems):
        for cp in self._copies(ins, outs, send_sems, recv_sems):
            cp.start()

    def finish(self, ins, outs, send_sems, recv_sems):
        for cp in self._copies(ins, outs, send_sems, recv_sems):
            cp.wait()


def _pcall_riding(body, args, ride, first, last, *, in_specs, out_specs, out_shape, scratch_shapes=(), middle=None,
                  **kw):
    n_in, n_out = len(args), len(out_shape)
    if ride is None:
        return _pcall(body, in_specs=in_specs, out_specs=out_specs, out_shape=out_shape,
                      scratch_shapes=list(scratch_shapes), **kw)(*args), []
    k_in, k_out = len(ride.ins), len(ride.out_shapes)

    def riding(*refs):
        ins, r_in = refs[:n_in], refs[n_in:n_in + k_in]
        outs = refs[n_in + k_in:n_in + k_in + n_out]
        r_out = refs[n_in + k_in + n_out:n_in + k_in + n_out + k_out]
        scratch, (send_sems, recv_sems) = refs[n_in + k_in + n_out + k_out:-2], refs[-2:]

        @pl.when(first())
        def _():
            ride.start(r_in, r_out, send_sems, recv_sems)

        if middle is not None:
            @pl.when(middle())
            def _():
                ride.pass_on(r_in, r_out, send_sems, recv_sems)

        body(*ins, *outs, *scratch)

        @pl.when(last())
        def _():
            if middle is not None:
                ride.finish(r_in, r_out, send_sems, recv_sems, passed_on=True)
            else:
                ride.finish(r_in, r_out, send_sems, recv_sems)

    res = _pcall(riding, in_specs=list(in_specs) + [_ANY] * k_in, out_specs=list(out_specs) + [_ANY] * k_out,
                 out_shape=list(out_shape) + ride.out_shapes,
                 input_output_aliases={n_in + i: n_out + o for i, o in ride.aliases.items()},
                 scratch_shapes=list(scratch_shapes) + [pltpu.SemaphoreType.DMA((ride.n_sems,))] * 2,
                 **kw)(*args, *ride.ins)
    return res[:n_out], res[n_out:]


def ride_alone(ride, name):
    def body(*refs):
        n = len(ride.ins)
        ins, outs, (send_sems, recv_sems) = refs[:n], refs[n:-2], refs[-2:]
        ride.start(ins, outs, send_sems, recv_sems)
        ride.finish(ins, outs, send_sems, recv_sems)

    return _pcall(body, name=name, in_specs=[_ANY] * len(ride.ins), out_specs=[_ANY] * len(ride.out_shapes),
                  out_shape=ride.out_shapes, input_output_aliases=dict(ride.aliases),
                  scratch_shapes=[pltpu.SemaphoreType.DMA((ride.n_sems,))] * 2)(*ride.ins)


def mm(a, b, *, name, ta=False, tb=False, pro_a=None, epi=None, extras=(), out_dtypes=(F32,), n_dim=None,
       tiles=(None, None, None), b_view=None, out_view=None, into=None, ride=None):
    if ta:
        k_dim, m_dim = a.shape
    else:
        m_dim, k_dim = a.shape
    if n_dim is None:
        n_dim = b.shape[0] if tb else b.shape[1]
    tn = tiles[1] or (n_dim if n_dim <= 1024 else _tile(n_dim, (1024, 512, 256, 128)))
    tk = tiles[2] or (k_dim if k_dim <= 1024 else _tile(k_dim, (1024, 512, 256, 128)))
    nk = k_dim // tk

    def vmem_bytes(tm):
        blocks = tm * tk * a.dtype.itemsize + tk * tn * b.dtype.itemsize
        blocks += sum(tm * (tn if e.shape[1] == n_dim else e.shape[1]) * e.dtype.itemsize for e in extras if e.shape[0] > 1)
        blocks += tm * tn * sum(jnp.dtype(d).itemsize for d in out_dtypes)
        return 2 * blocks + tm * tn * 4

    tm = tiles[0] or next((t for t in (4096, 2048, 1024, 512, 256) if m_dim % t == 0 and vmem_bytes(t) <= MM_VMEM_BUDGET),
                          _tile(m_dim, (128,)))
    assert m_dim % tm == 0 and n_dim % tn == 0 and k_dim % tk == 0, (name, m_dim, n_dim, k_dim, tm, tn, tk)
    n_ex, n_out = len(extras), len(out_dtypes)
    n_into = 0 if into is None else 1
    dims = (((0 if ta else 1,), (1 if tb else 0,)), ((), ()))

    def body(a_ref, b_ref, *rest):
        ex_refs, out_refs = rest[:n_ex], rest[n_ex + n_into:n_ex + n_into + n_out]

        def partial():
            av = a_ref[...]
            if pro_a is not None:
                av = pro_a(av)
            return lax.dot_general(av.astype(BF16), b_ref[...].astype(BF16), dims, preferred_element_type=F32)

        def finish(r):
            res = epi(r, *[e[...] for e in ex_refs]) if epi is not None else (r,)
            for o_ref, v in zip(out_refs, res):
                o_ref[...] = v.reshape(o_ref.shape).astype(o_ref.dtype)

        if nk == 1:
            finish(partial())
            return
        acc = rest[-1]
        k = pl.program_id(2)

        @pl.when(k == 0)
        def _():
            acc[...] = partial()

        @pl.when(k > 0)
        def _():
            acc[...] += partial()

        @pl.when(k == nk - 1)
        def _():
            finish(acc[...])

    def ex_spec(e):
        if e.shape == (m_dim, n_dim):
            return o_spec
        if e.shape[0] == m_dim:
            return pl.BlockSpec((tm, e.shape[1]), lambda i, j, k: (i, 0))
        return pl.BlockSpec(e.shape, lambda i, j, k: (0, 0))

    a_spec = pl.BlockSpec((tk, tm), lambda i, j, k: (k, i)) if ta else pl.BlockSpec((tm, tk), lambda i, j, k: (i, k))
    if b_view is not None:
        b_spec = b_view(tk, tn)
    else:
        b_spec = pl.BlockSpec((tn, tk), lambda i, j, k: (j, k)) if tb else pl.BlockSpec((tk, tn), lambda i, j, k: (k, j))
    o_spec = pl.BlockSpec((tm, tn), lambda i, j, k: (i, j))
    if out_view is None:
        out_specs = [o_spec] * n_out
        out_shape = [jax.ShapeDtypeStruct((m_dim, n_dim), dt) for dt in out_dtypes]
    else:
        assert n_out == 1
        out_specs = [out_view[1](tm, tn)]
        out_shape = [jax.ShapeDtypeStruct(out_view[0], out_dtypes[0])]
    grid = (m_dim // tm, n_dim // tn, nk)
    scratch = [pltpu.VMEM((tm, tn), F32)] if nk > 1 else []
    if ride is not None:
        assert into is None
        at = lambda ids: functools.reduce(jnp.logical_and, [pl.program_id(d) == i for d, i in enumerate(ids)])
        outs, landed = _pcall_riding(
            body, (a, b, *extras), ride, lambda: at((0, 0, 0)), lambda: at([g - 1 for g in grid]),
            name=name, grid=grid, in_specs=[a_spec, b_spec] + [ex_spec(e) for e in extras], out_specs=out_specs,
            out_shape=out_shape, scratch_shapes=scratch, compiler_params=_params(("arbitrary",) * 3))
        return (outs[0] if n_out == 1 else outs), landed
    outs = _pcall(
        body, name=name, grid=grid,
        in_specs=[a_spec, b_spec] + [ex_spec(e) for e in extras] + [_ANY] * n_into,
        out_specs=out_specs, out_shape=out_shape,
        input_output_aliases={2 + n_ex: 0} if n_into else {},
        scratch_shapes=scratch,
        compiler_params=_params(("parallel", "parallel", "arbitrary")),
    )(a, b, *extras, *([into] if n_into else []))
    return outs[0] if n_out == 1 else outs


def rowwise(fn, ins, outs, *, name, accs=(), tm=256):
    rows = ins[0].shape[0]
    tm = min(tm, rows)
    n_in, n_out, n_acc = len(ins), len(outs), len(accs)

    def body(*refs):
        in_refs, out_refs, acc_refs = refs[:n_in], refs[n_in:n_in + n_out], refs[n_in + n_out:]
        res, sums = fn(*[r[...] for r in in_refs])
        for o_ref, v in zip(out_refs, res):
            o_ref[...] = v.astype(o_ref.dtype)
        if n_acc:
            @pl.when(pl.program_id(0) == 0)
            def _():
                for a_ref in acc_refs:
                    a_ref[...] = jnp.zeros_like(a_ref)

            for a_ref, s in zip(acc_refs, sums):
                a_ref[...] += s

    def spec(arr):
        if arr.shape[0] == rows:
            return pl.BlockSpec((tm, arr.shape[1]), lambda i: (i, 0))
        return pl.BlockSpec(arr.shape, lambda i: (0, 0))

    res = _pcall(
        body, name=name, grid=(rows // tm,),
        in_specs=[spec(a) for a in ins],
        out_specs=[pl.BlockSpec((tm, w), lambda i: (i, 0)) for w, _ in outs]
        + [pl.BlockSpec((1, w), lambda i: (0, 0)) for w in accs],
        out_shape=[jax.ShapeDtypeStruct((rows, w), dt) for w, dt in outs]
        + [jax.ShapeDtypeStruct((1, w), F32) for w in accs],
        compiler_params=_params(("arbitrary",) if n_acc else ("parallel",)),
    )(*ins)
    return res


def _relu2(v):
    r = jnp.maximum(v, 0.0)
    return r * r


def _gelu(x):
    c = math.sqrt(2.0 / math.pi)
    return 0.5 * x * (1.0 + jnp.tanh(c * (x + 0.044715 * x * x * x)))


def _gelu_grad(x):
    c = math.sqrt(2.0 / math.pi)
    t = jnp.tanh(c * (x + 0.044715 * x * x * x))
    return 0.5 * (1.0 + t) + 0.5 * x * (1.0 - t * t) * c * (1.0 + 3 * 0.044715 * x * x)


def _sigmoid(x):
    return 1.0 / (1.0 + jnp.exp(-x))


def _layer_norm(h, mix, g, b):
    r = DN_ALPHA * h + mix
    mu = jnp.mean(r, axis=-1, keepdims=True)
    xc = r - mu
    var = jnp.mean(xc * xc, axis=-1, keepdims=True)
    return xc * lax.rsqrt(var + LN_EPS) * g + b


def ln_bwd(h, mix, g, dy, name):
    def fn(h, mix, g, dy):
        r = DN_ALPHA * h + mix
        mu = jnp.mean(r, axis=-1, keepdims=True)
        xc = r - mu
        var = jnp.mean(xc * xc, axis=-1, keepdims=True)
        rstd = lax.rsqrt(var + LN_EPS)
        xhat = xc * rstd
        dxh = dy * g
        m1 = jnp.mean(dxh, axis=-1, keepdims=True)
        m2 = jnp.mean(dxh * xhat, axis=-1, keepdims=True)
        dr = rstd * (dxh - m1 - xhat * m2)
        return (dr, dr), (jnp.sum(dy * xhat, axis=0, keepdims=True), jnp.sum(dy, axis=0, keepdims=True))
    return rowwise(fn, (h, mix, g, dy), ((D_MODEL, F32), (D_MODEL, BF16)), accs=(D_MODEL, D_MODEL), name=name)


def _rms(x, g):
    r = lax.rsqrt(jnp.mean(x * x, axis=-1, keepdims=True) + RMS_EPS)
    return x * r * g


def _rms_bwd(x, g, dy):
    r = lax.rsqrt(jnp.mean(x * x, axis=-1, keepdims=True) + RMS_EPS)
    xn = x * r
    dyg = dy * g
    dx = r * (dyg - xn * jnp.mean(dyg * xn, axis=-1, keepdims=True))
    return dx, jnp.sum(dy * xn, axis=0, keepdims=True)


def _s5_disc(lr, li, ldt):
    dt = jnp.exp(ldt)
    mag = jnp.exp(lr * dt)
    cs, sn = jnp.cos(li * dt), jnp.sin(li * dt)
    ar, ai = mag * cs, mag * sn
    inv = 1.0 / (lr * lr + li * li)
    n_re = (ar - 1.0) * lr + ai * li
    n_im = ai * lr - (ar - 1.0) * li
    return dt, mag, cs, sn, ar, ai, inv, n_re, n_im


def s5_prep(lr, li, ldt, b_re, b_im):
    def fn(lr, li, ldt, b_re, b_im):
        _, _, _, _, ar, ai, inv, n_re, n_im = _s5_disc(lr, li, ldt)
        cr, ci = n_re * inv, n_im * inv
        return (ar, ai, cr * b_re - ci * b_im, cr * b_im + ci * b_re), ()
    return rowwise(fn, (lr, li, ldt, b_re, b_im), ((1, F32), (1, F32), (SSM_GROUP, F32), (SSM_GROUP, F32)),
                   name="s5_prep", tm=512)


def s5_prep_bwd(lr, li, ldt, b_re, b_im, dar, dai, dbb_re, dbb_im):
    def fn(lr, li, ldt, b_re, b_im, dar, dai, dbb_re, dbb_im):
        dt, mag, cs, sn, ar, ai, inv, n_re, n_im = _s5_disc(lr, li, ldt)
        cr, ci = n_re * inv, n_im * inv
        db_re = cr * dbb_re + ci * dbb_im
        db_im = cr * dbb_im - ci * dbb_re
        dcr = jnp.sum(dbb_re * b_re + dbb_im * b_im, axis=-1, keepdims=True)
        dci = jnp.sum(dbb_im * b_re - dbb_re * b_im, axis=-1, keepdims=True)
        dar = dar + (dcr * lr - dci * li) * inv
        dai = dai + (dcr * li + dci * lr) * inv
        dinv = dcr * n_re + dci * n_im
        dlr = (dcr * (ar - 1.0) + dci * ai) * inv - 2.0 * lr * inv * inv * dinv
        dli = (dcr * ai - dci * (ar - 1.0)) * inv - 2.0 * li * inv * inv * dinv
        dmag = dar * cs + dai * sn
        dth = dai * ar - dar * ai
        dlr = dlr + dmag * mag * dt
        dli = dli + dth * dt
        ddt = dmag * mag * lr + dth * li
        return (dlr, dli, ddt * dt, db_re, db_im), ()
    return rowwise(fn, (lr, li, ldt, b_re, b_im, dar, dai, dbb_re, dbb_im),
                   ((1, F32), (1, F32), (1, F32), (SSM_GROUP, F32), (SSM_GROUP, F32)), name="s5_prep_bwd", tm=512)


def group_sum(x):
    def body(x_ref, o_ref):
        o_ref[...] = jnp.sum(x_ref[...], axis=1)
    return _pcall(body, name="s5_group_sum", out_shape=jax.ShapeDtypeStruct((N_GROUPS, 1), F32))(
        x.reshape(N_GROUPS, SSM_STATE, 1))


GROUPS_PER_TILE = LANES // SSM_GROUP
TILE_STATES = GROUPS_PER_TILE * SSM_STATE
N_UTILES = D_MODEL // LANES


SUBLANES = 8
SCAN_STRIP = 1024
N_STRIPS = N_STATES // SCAN_STRIP
_NT = (((1,), (1,)), ((), ()))
_TN = (((0,), (0,)), ((), ()))


def _scan_coefs(are, aim, shifted, reverse):
    ar = are[...]
    ai = -aim[...] if reverse else aim[...]
    powers = {1: (ar, ai)}
    for d in (2, 4):
        r, i = powers[d // 2]
        powers[d] = (r * r - i * i, 2.0 * r * i)
    rid = lax.broadcasted_iota(jnp.int32, (SUBLANES, N_STATES), 0)
    first = (rid == SUBLANES - 1) if reverse else (rid == 0)
    masks = [(1, first)] + [(d, (rid <= SUBLANES - 1 - d) if reverse else (rid >= d)) for d in (1, 2, 4)]
    for n, (d, keep) in enumerate(masks):
        for part in (0, 1):
            shifted[2 * n + part][...] = jnp.where(keep, jnp.broadcast_to(powers[d][part], (SUBLANES, N_STATES)), 0.0)


def _tile_scan(xr, xi, shifted, nbr_re, nbr_im, reverse):
    for n, d in enumerate((1, 1, 2, 4)):
        by = SUBLANES - d if reverse else d
        fr, fi = (nbr_re, nbr_im) if n == 0 else (xr, xi)
        sr, si = pltpu.roll(fr, by, 0), pltpu.roll(fi, by, 0)
        kr, ki = shifted[2 * n], shifted[2 * n + 1]
        xr, xi = xr + kr * sr - ki * si, xi + kr * si + ki * sr
    return xr, xi


def _tile_rows(t):
    return pl.ds(pl.multiple_of(t * SUBLANES, SUBLANES), SUBLANES)


def s5_fwd(u, bbd_re, bbd_im, cbd_re, cbd_imn, a_re, a_im, dskip, ride=None, t_rows=256):
    seq = u.shape[0]
    t_rows = min(t_rows, seq)
    n_tiles = t_rows // SUBLANES

    def body(u_ref, bre, bim, cre, cimn, are, aim, d_ref, y_ref, gelu_ref, hre_ref, him_ref, car_re, car_im, *shifted):
        @pl.when(pl.program_id(0) == 0)
        def _():
            car_re[...] = jnp.zeros_like(car_re)
            car_im[...] = jnp.zeros_like(car_im)
            _scan_coefs(are, aim, shifted, reverse=False)

        uf = u_ref[...]
        ub = uf.astype(BF16)
        for j in range(N_UTILES):
            uj = ub[:, LANES * j:LANES * (j + 1)]
            sl = slice(TILE_STATES * j, TILE_STATES * (j + 1))
            hre_ref[:, sl] = jnp.dot(uj, bre[j], preferred_element_type=F32)
            him_ref[:, sl] = jnp.dot(uj, bim[j], preferred_element_type=F32)
        for s in range(N_STRIPS):
            cols = pl.ds(s * SCAN_STRIP, SCAN_STRIP)
            coefs = [c[:, cols] for c in shifted]

            def step(t, before):
                rows = _tile_rows(t)
                hr, hi = _tile_scan(hre_ref[rows, cols], him_ref[rows, cols], coefs, before[0], before[1], False)
                hre_ref[rows, cols] = hr
                him_ref[rows, cols] = hi
                return hr, hi

            cr, ci = lax.fori_loop(0, n_tiles, step, (car_re[:, cols], car_im[:, cols]))
            car_re[:, cols] = cr
            car_im[:, cols] = ci
        dv = d_ref[...]
        for j in range(N_UTILES):
            st = slice(TILE_STATES * j, TILE_STATES * (j + 1))
            yj = (jnp.dot(hre_ref[:, st].astype(BF16), cre[j], preferred_element_type=F32)
                  + jnp.dot(him_ref[:, st].astype(BF16), cimn[j], preferred_element_type=F32))
            sl = slice(LANES * j, LANES * (j + 1))
            yj = yj + dv[:, sl] * uf[:, sl]
            y_ref[:, sl] = yj
            gelu_ref[:, sl] = _gelu(yj).astype(gelu_ref.dtype)

    full3 = lambda a: pl.BlockSpec(a.shape, lambda i: (0, 0, 0))
    full2 = lambda a: pl.BlockSpec(a.shape, lambda i: (0, 0))
    tile = pltpu.VMEM((SUBLANES, N_STATES), F32)
    n_chunks = seq // t_rows
    return _pcall_riding(
        body, (u, bbd_re, bbd_im, cbd_re, cbd_imn, a_re, a_im, dskip), ride,
        lambda: pl.program_id(0) == 0, lambda: pl.program_id(0) == n_chunks - 1,
        middle=(lambda: pl.program_id(0) == (7 * n_chunks) // 8) if ride is not None else None,
        name="s5_fwd", grid=(n_chunks,),
        in_specs=[pl.BlockSpec((t_rows, D_MODEL), lambda i: (i, 0)), full3(bbd_re), full3(bbd_im), full3(cbd_re),
                  full3(cbd_imn), full2(a_re), full2(a_im), full2(dskip)],
        out_specs=[pl.BlockSpec((t_rows, D_MODEL), lambda i: (i, 0)),
                   pl.BlockSpec((t_rows, D_MODEL), lambda i: (i, 0)),
                   pl.BlockSpec((t_rows, N_STATES), lambda i: (i, 0)),
                   pl.BlockSpec((t_rows, N_STATES), lambda i: (i, 0))],
        out_shape=[jax.ShapeDtypeStruct((seq, D_MODEL), F32),
                   jax.ShapeDtypeStruct((seq, D_MODEL), BF16),
                   jax.ShapeDtypeStruct((seq, N_STATES), F32),
                   jax.ShapeDtypeStruct((seq, N_STATES), F32)],
        scratch_shapes=[tile] * 10,
        compiler_params=_params(("arbitrary",)))


def s5_bwd(dy, u, dres, h_re, h_im, bbd_re, bbd_im, cbd_re, cbd_imn, a_re, a_im, dskip, ride=None, t_rows=256):
    seq = u.shape[0]
    t_rows = min(t_rows, seq)
    n_chunks = seq // t_rows

    n_tiles = t_rows // SUBLANES

    def body(dy_ref, u_ref, dres_ref, hre_ref, him_ref, hpre_ref, hpim_ref, bre, bim, cre, cimn, are, aim, d_ref,
             dx_ref, dbre, dbim, dcre, dcimn, dar_ref, dai_ref, dd_ref, lre, lim, car_re, car_im, acc_re, acc_im,
             *shifted):
        i = pl.program_id(0)

        @pl.when(i == 0)
        def _():
            for r in (car_re, car_im, acc_re, acc_im, dbre, dbim, dcre, dcimn, dd_ref):
                r[...] = jnp.zeros_like(r)
            _scan_coefs(are, aim, shifted, reverse=True)

        dyf = dy_ref[...]
        dyb = dyf.astype(BF16)
        uf = u_ref[...]
        ub = uf.astype(BF16)
        for j in range(N_UTILES):
            dyj = dyb[:, LANES * j:LANES * (j + 1)]
            st = slice(TILE_STATES * j, TILE_STATES * (j + 1))
            lre[:, st] = lax.dot_general(dyj, cre[j], _NT, preferred_element_type=F32)
            lim[:, st] = lax.dot_general(dyj, cimn[j], _NT, preferred_element_type=F32)
        has_pred = (i < n_chunks - 1).astype(F32)
        last_row = lax.broadcasted_iota(jnp.int32, (SUBLANES, SCAN_STRIP), 0) == SUBLANES - 1
        for s in range(N_STRIPS):
            cols = pl.ds(s * SCAN_STRIP, SCAN_STRIP)
            coefs = [c[:, cols] for c in shifted]
            before_re, before_im = hpre_ref[:, cols] * has_pred, hpim_ref[:, cols] * has_pred

            def step(k, carry):
                after_re, after_im, dar, dai = carry
                t = n_tiles - 1 - k
                rows = _tile_rows(t)
                lr, li = _tile_scan(lre[rows, cols], lim[rows, cols], coefs, after_re, after_im, True)
                lre[rows, cols] = lr
                lim[rows, cols] = li
                prev = _tile_rows(jnp.maximum(t - 1, 0))
                pre_re = jnp.where(t == 0, before_re, hre_ref[prev, cols])
                pre_im = jnp.where(t == 0, before_im, him_ref[prev, cols])
                hpr = pltpu.roll(jnp.where(last_row, pre_re, hre_ref[rows, cols]), 1, 0)
                hpi = pltpu.roll(jnp.where(last_row, pre_im, him_ref[rows, cols]), 1, 0)
                return lr, li, dar + lr * hpr + li * hpi, dai + li * hpr - lr * hpi

            cr, ci, dar, dai = lax.fori_loop(0, n_tiles, step, (car_re[:, cols], car_im[:, cols],
                                                               acc_re[:, cols], acc_im[:, cols]))
            car_re[:, cols] = cr
            car_im[:, cols] = ci
            acc_re[:, cols] = dar
            acc_im[:, cols] = dai

        dv = d_ref[...]
        for j in range(N_UTILES):
            sl = slice(LANES * j, LANES * (j + 1))
            st = slice(TILE_STATES * j, TILE_STATES * (j + 1))
            lrj = lre[:, st].astype(BF16)
            lij = lim[:, st].astype(BF16)
            du = (lax.dot_general(lrj, bre[j], _NT, preferred_element_type=F32)
                  + lax.dot_general(lij, bim[j], _NT, preferred_element_type=F32))
            dx_ref[:, sl] = du + dv[:, sl] * dyf[:, sl] + DN_ALPHA * dres_ref[:, sl]
            uj = ub[:, sl]
            dbre[j] += lax.dot_general(uj, lrj, _TN, preferred_element_type=F32)
            dbim[j] += lax.dot_general(uj, lij, _TN, preferred_element_type=F32)
            dyj = dyb[:, sl]
            dcre[j] += lax.dot_general(hre_ref[:, st].astype(BF16), dyj, _TN, preferred_element_type=F32)
            dcimn[j] += lax.dot_general(him_ref[:, st].astype(BF16), dyj, _TN, preferred_element_type=F32)
        dd_ref[...] += jnp.sum(dyf * uf, axis=0, keepdims=True)

        @pl.when(i == n_chunks - 1)
        def _():
            dar_ref[...] = jnp.sum(acc_re[...], axis=0, keepdims=True)
            dai_ref[...] = jnp.sum(acc_im[...], axis=0, keepdims=True)

    rev = lambda i: (n_chunks - 1 - i, 0)
    prev_tile = lambda i: (jnp.maximum((n_chunks - 1 - i) * n_tiles - 1, 0), 0)
    once = pl.Buffered(1)
    full3 = lambda a: pl.BlockSpec(a.shape, lambda i: (0, 0, 0), pipeline_mode=once)
    full2 = lambda a: pl.BlockSpec(a.shape, lambda i: (0, 0), pipeline_mode=once)
    acc3 = lambda shape: pl.BlockSpec(shape, lambda i: (0, 0, 0))
    acc2 = lambda shape: pl.BlockSpec(shape, lambda i: (0, 0))
    tile = pltpu.VMEM((SUBLANES, N_STATES), F32)
    return _pcall_riding(
        body, (dy, u, dres, h_re, h_im, h_re, h_im, bbd_re, bbd_im, cbd_re, cbd_imn, a_re, a_im, dskip), ride,
        lambda: pl.program_id(0) == 0, lambda: pl.program_id(0) == n_chunks - 1,
        name="s5_bwd", grid=(n_chunks,),
        in_specs=[pl.BlockSpec((t_rows, D_MODEL), rev), pl.BlockSpec((t_rows, D_MODEL), rev),
                  pl.BlockSpec((t_rows, D_MODEL), rev),
                  pl.BlockSpec((t_rows, N_STATES), rev), pl.BlockSpec((t_rows, N_STATES), rev),
                  pl.BlockSpec((SUBLANES, N_STATES), prev_tile), pl.BlockSpec((SUBLANES, N_STATES), prev_tile),
                  full3(bbd_re), full3(bbd_im), full3(cbd_re), full3(cbd_imn), full2(a_re), full2(a_im), full2(dskip)],
        out_specs=[pl.BlockSpec((t_rows, D_MODEL), rev), acc3(bbd_re.shape), acc3(bbd_im.shape), acc3(cbd_re.shape),
                   acc3(cbd_imn.shape), acc2((1, N_STATES)), acc2((1, N_STATES)), acc2((1, D_MODEL))],
        out_shape=[jax.ShapeDtypeStruct((seq, D_MODEL), F32), jax.ShapeDtypeStruct(bbd_re.shape, F32),
                   jax.ShapeDtypeStruct(bbd_im.shape, F32), jax.ShapeDtypeStruct(cbd_re.shape, F32),
                   jax.ShapeDtypeStruct(cbd_imn.shape, F32), jax.ShapeDtypeStruct((1, N_STATES), F32),
                   jax.ShapeDtypeStruct((1, N_STATES), F32), jax.ShapeDtypeStruct((1, D_MODEL), F32)],
        scratch_shapes=[pltpu.VMEM((t_rows, N_STATES), F32), pltpu.VMEM((t_rows, N_STATES), F32)] + [tile] * 12,
        compiler_params=_params(("arbitrary",)))


def _eye_groups():
    return jnp.eye(GROUPS_PER_TILE, dtype=F32)


def _blockdiag_in(bb):
    t = bb.transpose(0, 2, 1).reshape(N_UTILES, GROUPS_PER_TILE, SSM_GROUP, SSM_STATE)
    bd = jnp.einsum("jgcp,gh->jgchp", t, _eye_groups())
    return bd.reshape(N_UTILES, LANES, TILE_STATES)


def _blockdiag_in_t(d):
    t = jnp.einsum("jgchp,gh->jgcp", d.reshape(N_UTILES, GROUPS_PER_TILE, SSM_GROUP, GROUPS_PER_TILE, SSM_STATE),
                   _eye_groups())
    return t.reshape(N_GROUPS, SSM_GROUP, SSM_STATE).transpose(0, 2, 1)


def _blockdiag_out(c):
    t = c.transpose(0, 2, 1).reshape(N_UTILES, GROUPS_PER_TILE, SSM_STATE, SSM_GROUP)
    bd = jnp.einsum("jhpc,hg->jhpgc", t, _eye_groups())
    return bd.reshape(N_UTILES, TILE_STATES, LANES)


def _blockdiag_out_t(d):
    t = jnp.einsum("jhpgc,hg->jhpc", d.reshape(N_UTILES, GROUPS_PER_TILE, SSM_STATE, GROUPS_PER_TILE, SSM_GROUP),
                   _eye_groups())
    return t.reshape(N_GROUPS, SSM_STATE, SSM_GROUP).transpose(0, 2, 1)


ATT_TQ = 512
ATT_TK = 512
LOG2E = math.log2(math.e)
LN2 = math.log(2.0)
Q_PRESCALE = SM_SCALE * LOG2E


def _loop_in_pairs(n, step, carry, start=0):
    pairs = (n - start) // 2

    def two(t, c):
        return step(start + 2 * t + 1, step(start + 2 * t, c))

    carry = lax.fori_loop(0, pairs, two, carry)
    return lax.fori_loop(start + 2 * pairs, n, step, carry)


def _causal(s, transposed=False):
    r = lax.broadcasted_iota(jnp.int32, s.shape, 0)
    c = lax.broadcasted_iota(jnp.int32, s.shape, 1)
    return jnp.where((r <= c) if transposed else (c <= r), s, NEG_INF)


def _q_specs(rows, at):
    def nope(*ids):
        r, h = at(*ids)
        return r, 3 * (h // HEADS_PER_CHIP) + h % HEADS_PER_CHIP

    def rope(*ids):
        r, h = at(*ids)
        return r, 3 * (h // HEADS_PER_CHIP) + HEADS_PER_CHIP

    return [pl.BlockSpec((rows, LANES), nope), pl.BlockSpec((rows, LANES), rope)]


def _kv_specs(rows, at):
    def col(f):
        def index(*ids):
            r, h = at(*ids)
            return r, f(h)
        return index

    return [pl.BlockSpec((rows, LANES), col(lambda h: 2 * h)), pl.BlockSpec((rows, LANES), col(lambda h: h % HEADS_PER_CHIP)),
            pl.BlockSpec((rows, LANES), col(lambda h: 2 * h + 1))]


def _cat(a, b):
    return jnp.concatenate([a, b], axis=1)


def attn_fwd(q, kv, kr, ride=None, tq=ATT_TQ, tk=ATT_TK):
    seq = q.shape[0]
    n_heads = N_HEADS
    tq, tk = min(tq, seq), min(tk, seq)
    assert tq == tk

    def body(qn_ref, qr_ref, kn_ref, kr_ref, v_ref, o_ref, lse_ref):
        qi = pl.program_id(1)
        qv = _cat(qn_ref[...], qr_ref[...])
        jd = qi

        def block(j, carry, diag):
            m, l, acc = carry
            rows = pl.ds(pl.multiple_of(j * tk, tk), tk)
            s = lax.dot_general(qv, _cat(kn_ref[rows, :], kr_ref[rows, :]), _NT, preferred_element_type=F32)
            if diag:
                s = _causal(s)
            m_new = jnp.maximum(m, jnp.max(s, axis=-1, keepdims=True))
            p = jnp.exp2(s - m_new)
            corr = jnp.exp2(m - m_new)
            l = l * corr + jnp.sum(p, axis=-1, keepdims=True)
            acc = acc * corr + jnp.dot(p.astype(BF16), v_ref[rows, :], preferred_element_type=F32)
            return m_new, l, acc

        init = (jnp.full((tq, 1), NEG_INF, F32), jnp.zeros((tq, 1), F32), jnp.zeros((tq, V_HEAD), F32))
        carry = _loop_in_pairs(jd, lambda j, c: block(j, c, False), init)
        m, l, acc = block(jd, carry, True)
        o_ref[...] = acc / l
        lse_ref[...] = jnp.transpose(jnp.broadcast_to(m + jnp.log2(l), (tq, LANES)))[:1, :]

    n_q = seq // tq
    return _pcall_riding(
        body, (q, q, kv, kr, kv), ride,
        lambda: (pl.program_id(0) == 0) & (pl.program_id(1) == 0),
        lambda: (pl.program_id(0) == n_heads - 1) & (pl.program_id(1) == n_q - 1),
        middle=(lambda: (pl.program_id(0) == (5 * n_heads) // 8) & (pl.program_id(1) == 0)) if ride is not None else None,
        name="attn_fwd", grid=(n_heads, n_q),
        in_specs=_q_specs(tq, lambda h, i: (i, h)) + _kv_specs(seq, lambda h, i: (0, h)),
        out_specs=[pl.BlockSpec((tq, V_HEAD), lambda h, i: (i, h)),
                   pl.BlockSpec((None, None, 1, tq), lambda h, i: (h, i, 0, 0))],
        out_shape=[jax.ShapeDtypeStruct((seq, n_heads * V_HEAD), F32),
                   jax.ShapeDtypeStruct((n_heads, n_q, 1, tq), F32)],
        compiler_params=_params(("arbitrary", "arbitrary")))


def attn_bwd(q, kv, kr, do, lse_row, delta_row, tq=ATT_TK):
    seq = q.shape[0]
    tq = min(tq, seq)
    n_blk = seq // tq

    def body(qn_ref, qr_ref, kn_ref, kr_ref, v_ref, do_ref, lse_ref, delta_ref, dqn_ref, dqr_ref, dkv_ref, dkr_ref, dq_acc):
        head, kj = pl.program_id(0), pl.program_id(1)

        @pl.when(kj == 0)
        def _():
            dq_acc[...] = jnp.zeros_like(dq_acc)

        kc = _cat(kn_ref[...], kr_ref[...])
        vv = v_ref[...]

        def block(i, carry, diag):
            dk, dv = carry
            rows = pl.ds(pl.multiple_of(i * tq, tq), tq)
            qv = _cat(qn_ref[rows, :], qr_ref[rows, :])
            st = lax.dot_general(kc, qv, _NT, preferred_element_type=F32)
            if diag:
                st = _causal(st, transposed=True)
            pt = jnp.exp2(st - lse_ref[0, pl.ds(i, 1), :])
            dob = do_ref[rows, :].astype(BF16)
            dv = dv + jnp.dot(pt.astype(BF16), dob, preferred_element_type=F32)
            dpt = lax.dot_general(vv, dob, _NT, preferred_element_type=F32)
            dst = (pt * (dpt - delta_ref[0, pl.ds(i, 1), :])).astype(BF16)
            dk = dk + jnp.dot(dst, qv, preferred_element_type=F32)
            dq_acc[rows, :] += lax.dot_general(dst, kc, _TN, preferred_element_type=F32)
            return dk, dv

        carry = block(kj, (jnp.zeros((tq, 2 * LANES), F32), jnp.zeros((tq, V_HEAD), F32)), True)
        dk, dv = _loop_in_pairs(n_blk, lambda i, c: block(i, c, False), carry, start=kj + 1)
        dk = dk * LN2
        dkv_ref[...] = _cat(dk[:, :LANES], dv).astype(dkv_ref.dtype)
        lane = lax.broadcasted_iota(jnp.int32, (tq, LANES), 1)
        mine = (lane // HALF_ROPE) % HEADS_PER_CHIP == head % HEADS_PER_CHIP
        dkr_ref[0] = jnp.where(mine, dk[:, LANES:], 0.0)

        @pl.when(kj == n_blk - 1)
        def _():
            dqn_ref[...] = dq_acc[:, :LANES] * SM_SCALE

        @pl.when((kj == n_blk - 1) & (head % HEADS_PER_CHIP == 0))
        def _():
            dqr_ref[...] = dq_acc[:, LANES:] * SM_SCALE

        @pl.when((kj == n_blk - 1) & (head % HEADS_PER_CHIP > 0))
        def _():
            dqr_ref[...] += dq_acc[:, LANES:] * SM_SCALE

    return _pcall(
        body, name="attn_bwd", grid=(N_HEADS, n_blk),
        in_specs=_q_specs(seq, lambda h, j: (0, h)) + _kv_specs(tq, lambda h, j: (j, h))
        + [pl.BlockSpec((seq, V_HEAD), lambda h, j: (0, h)),
           pl.BlockSpec((1, n_blk, tq), lambda h, j: (h, 0, 0)),
           pl.BlockSpec((1, n_blk, tq), lambda h, j: (h, 0, 0))],
        out_specs=[pl.BlockSpec((seq, LANES), lambda h, j: (0, h)),
                   pl.BlockSpec((seq, LANES), lambda h, j: (0, h // HEADS_PER_CHIP)),
                   pl.BlockSpec((tq, QK_NOPE + V_HEAD), lambda h, j: (j, h)),
                   pl.BlockSpec((1, tq, LANES), lambda h, j: (h, j, 0))],
        out_shape=[jax.ShapeDtypeStruct((seq, N_HEADS * QK_NOPE), F32),
                   jax.ShapeDtypeStruct((seq, N_CHIPS * LANES), F32),
                   jax.ShapeDtypeStruct((seq, N_HEADS * (QK_NOPE + V_HEAD)), BF16),
                   jax.ShapeDtypeStruct((N_HEADS, seq, LANES), F32)],
        scratch_shapes=[pltpu.VMEM((seq, 2 * LANES), F32)],
        compiler_params=_params(("arbitrary", "arbitrary")),
    )(q, q, kv, kr, kv, do, lse_row, delta_row)


def head_sum(x, ts=512):
    n_heads, seq, w = x.shape
    ts = min(ts, seq)

    def body(x_ref, o_ref):
        o_ref[...] = jnp.sum(x_ref[...], axis=0)

    return _pcall(body, name="head_sum", grid=(seq // ts,),
                  in_specs=[pl.BlockSpec((n_heads, ts, w), lambda i: (0, i, 0))],
                  out_specs=pl.BlockSpec((ts, w), lambda i: (i, 0)),
                  out_shape=jax.ShapeDtypeStruct((seq, w), F32),
                  compiler_params=_params(("parallel",)))(x)


HEADS_PER_CHIP = N_HEADS // N_CHIPS
Q_CHIP = HEADS_PER_CHIP * QK_DIM
Q_CHIP_NOPE = HEADS_PER_CHIP * QK_NOPE


def _perm_q_cols(w):
    t = w.reshape(w.shape[0], HEADS_PER_CHIP, QK_DIM)
    return jnp.concatenate([t[:, :, :QK_NOPE].reshape(w.shape[0], -1),
                            t[:, :, QK_NOPE:QK_NOPE + HALF_ROPE].reshape(w.shape[0], -1),
                            t[:, :, QK_NOPE + HALF_ROPE:].reshape(w.shape[0], -1)], axis=1)


def _unperm_q_cols(w):
    r = w.shape[0]
    nope = w[:, :Q_CHIP_NOPE].reshape(r, HEADS_PER_CHIP, QK_NOPE)
    r1 = w[:, Q_CHIP_NOPE:Q_CHIP_NOPE + QK_ROPE].reshape(r, HEADS_PER_CHIP, HALF_ROPE)
    r2 = w[:, Q_CHIP_NOPE + QK_ROPE:].reshape(r, HEADS_PER_CHIP, HALF_ROPE)
    return jnp.concatenate([nope, r1, r2], axis=2).reshape(r, Q_CHIP)


def _pad_kva_cols(w):
    z = jnp.zeros((w.shape[0], HALF_ROPE), w.dtype)
    return jnp.concatenate([w[:, :KV_LORA], w[:, KV_LORA:KV_LORA + HALF_ROPE], z, w[:, KV_LORA + HALF_ROPE:], z], axis=1)


def _unpad_kva_cols(w):
    return jnp.concatenate([w[:, :KV_LORA], w[:, KV_LORA:KV_LORA + HALF_ROPE],
                            w[:, KV_LORA + QK_ROPE:KV_LORA + QK_ROPE + HALF_ROPE]], axis=1)


def _rope_tile(t, cs, sn):
    return t * cs + pltpu.roll(t, LANES // 2, 1) * sn


def _rope_tile_bwd(d, cs, sn):
    return d * cs + pltpu.roll(d * sn, LANES // 2, 1)


def _b_cols(tk, tn):
    return pl.BlockSpec((None, tk, tn), lambda i, j, k: (j, k, 0))


def _b_cols_t(tk, tn):
    return pl.BlockSpec((None, tn, tk), lambda i, j, k: (k, j, 0))


def _out_cols(shape):
    return shape, lambda tm, tn: pl.BlockSpec((None, tm, tn), lambda i, j, k: (j, i, 0))


def _halves(a):
    return a.reshape(N_CHIPS, 2, a.shape[1] // 2, a.shape[2])


def device_step(x, positions, target, w, comm=None):
    seq = x.shape[0]
    w = dict(w)

    def gathered(names, outs):
        for n, a in zip(names, outs):
            if isinstance(n, tuple):
                w[n[0]] = [a.reshape(v.shape) if l == n[1] else v for l, v in enumerate(w[n[0]])]
            else:
                w[n] = a.reshape(w[n].shape)

    def ride_for(names):
        if comm is None:
            return None
        return GatherRide([_halves(w[n[0]][n[1]] if isinstance(n, tuple) else w[n]) for n in names])

    first_ride = ("ssm_w_glu", "ssm_w_out", ("w_ff1", 0), ("w_ff2", 0))
    mla_ride = ("kv_w_a", "kv_w_b", "q_w_a", "q_w_b", "attn_w_o")
    second_ride = (("w_ff1", 1), ("w_ff2", 1))

    inv_freq = ROPE_THETA ** (-jnp.arange(HALF_ROPE, dtype=F32) / HALF_ROPE)
    ang = positions.astype(F32)[:, None] * inv_freq
    cos, sin = jnp.cos(ang), jnp.sin(ang)
    zero = jnp.zeros_like(cos)
    cos_q, sin_q = jnp.concatenate([cos] * 4, 1), jnp.concatenate([-sin, -sin, sin, sin], 1)
    cos_k, sin_k = jnp.concatenate([cos, zero, cos, zero], 1), jnp.concatenate([-sin, zero, sin, zero], 1)
    ff_tile = D_FF // N_CHIPS
    pack_shape = (N_CHIPS, EARLY_ROWS, PACK_W)

    lr = w["ssm_lam_re"].reshape(N_STATES, 1)
    li = w["ssm_lam_im"].reshape(N_STATES, 1)
    ldt = jnp.repeat(w["ssm_log_dt"].reshape(N_GROUPS), SSM_STATE).reshape(N_STATES, 1)
    b_re = w["ssm_b_re"].reshape(N_STATES, SSM_GROUP)
    b_im = w["ssm_b_im"].reshape(N_STATES, SSM_GROUP)
    a_re, a_im, bb_re, bb_im = s5_prep(lr, li, ldt, b_re, b_im)
    a_re, a_im = a_re.reshape(1, N_STATES), a_im.reshape(1, N_STATES)
    bbd_re = _blockdiag_in(bb_re.reshape(N_GROUPS, SSM_STATE, SSM_GROUP)).astype(BF16)
    bbd_im = _blockdiag_in(bb_im.reshape(N_GROUPS, SSM_STATE, SSM_GROUP)).astype(BF16)
    cbd_re = _blockdiag_out(w["ssm_c_re"].reshape(N_GROUPS, SSM_GROUP, SSM_STATE)).astype(BF16)
    cbd_imn = _blockdiag_out(-w["ssm_c_im"].reshape(N_GROUPS, SSM_GROUP, SSM_STATE)).astype(BF16)
    dskip = w["ssm_d"].reshape(1, D_MODEL)
    (ypre, yg, h_re, h_im), landed = s5_fwd(x, bbd_re, bbd_im, cbd_re, cbd_imn, a_re, a_im, dskip, ride_for(first_ride))
    gathered(first_ride, landed)
    w_glu = w["ssm_w_glu"]
    glu_tile = w_glu.shape[2]
    vg = mm(yg, w_glu, n_dim=2 * D_MODEL, tiles=(None, glu_tile, None), b_view=_b_cols, name="glu_proj")

    def glu(v):
        return (v[:, :D_MODEL] * _sigmoid(v[:, D_MODEL:]),), ()
    (z,) = rowwise(glu, (vg,), ((D_MODEL, BF16),), name="glu")
    w_out = w["ssm_w_out"].reshape(D_MODEL, D_MODEL)
    ln = lambda name, l: w[name][l].reshape(1, D_MODEL)

    def then_ln(h, names, layer):
        def epi(r, hv, gl, bl):
            y = _layer_norm(hv, r, gl, bl)
            return r, y, y
        return dict(epi=epi, extras=(h, ln(names[0], layer), ln(names[1], layer)), out_dtypes=(F32, F32, BF16))

    mix0, h1, h1b = mm(z, w_out, name="ssm_out", **then_ln(x, ("ln_mix_g", "ln_mix_b"), 0))

    def mlp_fwd(h, hb, layer, riding=None, with_ln=True):
        pre = mm(hb, w["w_ff1"][layer], n_dim=D_FF, tiles=(None, ff_tile, None), b_view=_b_cols, name=f"ff1_{layer}",
                 out_dtypes=(BF16,), ride=ride_for(riding) if riding else None)
        if riding and comm is not None:
            pre, landed = pre
            gathered(riding, landed)
        post = then_ln(h, ("ln_ffn_g", "ln_ffn_b"), layer) if with_ln else {}
        return pre, mm(pre, w["w_ff2"][layer].reshape(D_FF, D_MODEL), pro_a=_relu2, name=f"ff2_{layer}", **post)

    f1pre, (f1, h2, h2b) = mlp_fwd(h1, h1b, 0, mla_ride)

    kv_w_a = w["kv_w_a"].reshape(D_MODEL, KVA_PAD)
    kv_w_b = w["kv_w_b"]
    q_w_a = w["q_w_a"].reshape(D_MODEL, Q_LORA)
    q_w_b = w["q_w_b"]
    w_o = w["attn_w_o"].reshape(D_MODEL, D_MODEL)
    kvb_tile = kv_w_b.shape[2]
    kvn_g = w["kv_norm_g"].reshape(1, KV_LORA)
    qn_g = w["q_norm_g"].reshape(1, Q_LORA)
    kva = mm(h2b, kv_w_a, name="kv_a")

    def kv_post(kva, g, cs, sn):
        tile = _rope_tile(kva[:, KV_LORA:], cs, sn)
        return (_rms(kva[:, :KV_LORA], g), _cat(tile, pltpu.roll(tile, HALF_ROPE, 1))), ()
    ckv, krope = rowwise(kv_post, (kva, kvn_g, cos_k, sin_k), ((KV_LORA, BF16), (2 * LANES, BF16)), name="kv_post")
    kvb = mm(ckv, kv_w_b, n_dim=N_CHIPS * kvb_tile, tiles=(None, kvb_tile, KV_LORA), b_view=_b_cols, name="kv_b",
             out_dtypes=(BF16,))
    cq_raw, cq = mm(h2b, q_w_a, epi=lambda r, gq: (r, _rms(r, gq)), extras=(qn_g,), out_dtypes=(F32, BF16), name="q_a")

    def rope_and_scale(r, cs, sn):
        return (_cat(r[:, :Q_CHIP_NOPE], _rope_tile(r[:, Q_CHIP_NOPE:], cs, sn)) * Q_PRESCALE,)
    qro = mm(cq, q_w_b, n_dim=N_CHIPS * Q_CHIP, tiles=(None, Q_CHIP, Q_LORA), b_view=_b_cols, epi=rope_and_scale,
             extras=(cos_q, sin_q), out_dtypes=(BF16,), name="q_b")
    (o, lse), landed = attn_fwd(qro, kvb, krope, ride_for(second_ride))
    gathered(second_ride, landed)
    mix1, h3, h3b = mm(o, w_o, name="attn_out", **then_ln(h2, ("ln_mix_g", "ln_mix_b"), 1))
    f2pre, f2 = mlp_fwd(h3, h3b, 1, with_ln=False)
    def last_ln_and_loss(h, mix, gl, bl, t):
        e = _layer_norm(h, mix, gl, bl) - t
        return (e * (1.0 / D_MODEL),), (jnp.broadcast_to(jnp.sum(e * e), (1, LANES)),)
    dh4, loss_acc = rowwise(last_ln_and_loss, (h3, f2, ln("ln_ffn_g", 1), ln("ln_ffn_b", 1), target), ((D_MODEL, F32),),
                            accs=(LANES,), name="ln_ffn_1_loss")
    loss = loss_acc[0, 0] * (0.5 / D_MODEL)

    g = {}

    def into_rows(off, rows_per_chip, shape=pack_shape):
        def view(tm, tn):
            if tm == N_CHIPS * rows_per_chip:
                return pl.BlockSpec((N_CHIPS, rows_per_chip, tn), lambda i, j, k: (0, off // rows_per_chip, 0))
            nb = rows_per_chip // tm
            return pl.BlockSpec((None, tm, tn), lambda i, j, k: (i // nb, off // tm + i % nb, 0))
        return shape, view

    def into_cols(off):
        return pack_shape, lambda tm, tn: pl.BlockSpec((None, tm, tn), lambda i, j, k: (j, off // tm + i, 0))

    def mlp_bwd(pack, dr, drb, hb, pre, layer, swap=False):
        w2_rows = (EARLY_OFF["w_ff2"] + layer * ff_tile, ff_tile)
        w1_rows = (EARLY_OFF["w_ff1"] + layer * D_MODEL, D_MODEL)
        ready = [(w1_rows[0] + w1_rows[1], w2_rows[0] - w1_rows[0] - w1_rows[1]), (w2_rows[0] + w2_rows[1], EARLY_ROWS - w2_rows[0] - w2_rows[1])]
        dpre = mm(drb, w["w_ff2"][layer].reshape(D_FF, D_MODEL), tb=True, epi=lambda r, p: (r * 2.0 * jnp.maximum(p, 0.0),),
                  extras=(pre,), out_dtypes=(BF16,), tiles=(None, ff_tile, None), name=f"ff2_dx_{layer}",
                  ride=SwapRide(pack, ready) if swap else None)
        if swap:
            dpre, (theirs,) = dpre
        pack = mm(pre, drb, ta=True, pro_a=_relu2, name=f"ff2_dw_{layer}", tiles=(ff_tile, PACK_W, None), into=pack,
                  out_view=into_rows(w2_rows[0], ff_tile))
        pack = mm(hb, dpre, ta=True, name=f"ff1_dw_{layer}", tiles=(None, PACK_W, None), into=pack,
                  out_view=into_cols(w1_rows[0]))
        dh = mm(dpre, w["w_ff1"][layer], tb=True, epi=lambda r, d: (r + DN_ALPHA * d,), extras=(dr,), n_dim=D_MODEL,
                tiles=(None, D_MODEL, ff_tile), b_view=_b_cols_t, name=f"ff1_dx_{layer}",
                ride=SwapRide(pack, [w1_rows, w2_rows], into=theirs) if swap else None)
        return (pack, *dh) if swap else (pack, dh)

    dr4, dr4b, dg_f1, db_f1 = ln_bwd(h3, f2, ln("ln_ffn_g", 1), dh4, "ln_ffn_bwd_1")
    pack, dh3 = mlp_bwd(None, dr4, dr4b, h3b, f2pre, 1)
    dr3, dr3b, dg_m1, db_m1 = ln_bwd(h2, mix1, ln("ln_mix_g", 1), dh3, "ln_mix_bwd_1")
    shard_rows = D_MODEL // N_CHIPS
    pack = mm(o, dr3b, ta=True, name="attn_out_dw", tiles=(D_MODEL, PACK_W, None), into=pack,
              out_view=into_rows(EARLY_OFF["attn_w_o"], shard_rows))
    do = mm(dr3b, w_o, tb=True, name="attn_out_dx")
    def head_dots(do, o):
        return (jnp.concatenate([jnp.sum(do[:, V_HEAD * h:V_HEAD * (h + 1)] * o[:, V_HEAD * h:V_HEAD * (h + 1)], axis=1,
                                         keepdims=True) for h in range(N_HEADS)], axis=1),), ()
    (delta,) = rowwise(head_dots, (do, o), ((N_HEADS, F32),), name="attn_delta")
    tb = min(ATT_TK, seq)
    lse_row = lse.reshape(N_HEADS, seq // tb, tb)
    delta_row = delta.T.reshape(N_HEADS, seq // tb, tb)
    dqn, dqr, dkvb, dkr = attn_bwd(qro, kvb, krope, do, lse_row, delta_row)

    def q_rope_bwd(dn, dr, cs, sn):
        parts = []
        for k in range(N_CHIPS):
            parts.append(dn[:, Q_CHIP_NOPE * k:Q_CHIP_NOPE * (k + 1)])
            parts.append(_rope_tile_bwd(dr[:, LANES * k:LANES * (k + 1)], cs, sn))
        return (jnp.concatenate(parts, axis=1),), ()
    (dqlin,) = rowwise(q_rope_bwd, (dqn, dqr, cos_q, sin_q), ((N_CHIPS * Q_CHIP, BF16),), name="q_rope_bwd")
    g["q_w_b"] = mm(cq, dqlin, ta=True, name="q_b_dw", tiles=(Q_LORA, Q_CHIP, None), out_view=_out_cols(q_w_b.shape))
    dcq = mm(dqlin, q_w_b, tb=True, n_dim=Q_LORA, tiles=(None, Q_LORA, Q_CHIP), b_view=_b_cols_t, name="q_b_dx")

    def q_norm_bwd(c, gq, d):
        dx, dgq = _rms_bwd(c, gq, d)
        return (dx,), (dgq,)
    dcq_raw, dqn_g = rowwise(q_norm_bwd, (cq_raw, qn_g, dcq), ((Q_LORA, BF16),), accs=(Q_LORA,), name="q_norm_bwd")
    g["q_w_a"] = mm(h2b, dcq_raw, ta=True, name="q_a_dw")
    g["kv_w_b"] = mm(ckv, dkvb, ta=True, name="kv_b_dw", tiles=(KV_LORA, kvb_tile, None), out_view=_out_cols(kv_w_b.shape))
    dckv = mm(dkvb, kv_w_b, tb=True, n_dim=KV_LORA, tiles=(None, KV_LORA, kvb_tile), b_view=_b_cols_t, name="kv_b_dx")
    dkr_sum = head_sum(dkr)

    def kv_post_bwd(kva, gk, dc, dk, cs, sn):
        dx, dgk = _rms_bwd(kva[:, :KV_LORA], gk, dc)
        dk = dk + pltpu.roll(dk, LANES - HALF_ROPE, 1)
        return (jnp.concatenate([dx, _rope_tile_bwd(dk, cs, sn)], axis=1),), (dgk,)
    dkva, dkvn_g = rowwise(kv_post_bwd, (kva, kvn_g, dckv, dkr_sum, cos_k, sin_k), ((KVA_PAD, BF16),),
                           accs=(KV_LORA,), name="kv_post_bwd")
    g["kv_w_a"] = mm(h2b, dkva, ta=True, name="kv_a_dw")
    dh2 = mm(dcq_raw, q_w_a, tb=True, epi=lambda r, d: (r + DN_ALPHA * d,), extras=(dr3,), name="q_a_dx")
    dh2 = mm(dkva, kv_w_a, tb=True, epi=lambda r, d: (r + d,), extras=(dh2,), name="kv_a_dx")

    dr2, dr2b, dg_f0, db_f0 = ln_bwd(h1, f1, ln("ln_ffn_g", 0), dh2, "ln_ffn_bwd_0")
    pack = put_rows(pack, packed_shards(g, MISC_EARLY, EARLY_ROWS - MISC_EARLY_OFF), MISC_EARLY_OFF)
    if comm is None:
        pack, dh1 = mlp_bwd(pack, dr2, dr2b, h1b, f1pre, 0)
    else:
        pack, dh1, (theirs,) = mlp_bwd(pack, dr2, dr2b, h1b, f1pre, 0, swap=True)
        early_sums = add_halves(pack, theirs, comm[1])
    dr1, dr1b, dg_m0, db_m0 = ln_bwd(x, mix0, ln("ln_mix_g", 0), dh1, "ln_mix_bwd_0")
    mid = mm(z, dr1b, ta=True, name="ssm_out_dw", tiles=(D_MODEL, PACK_W, None),
             out_view=into_rows(MID_OFF["ssm_w_out"], shard_rows, (N_CHIPS, MID_ROWS, PACK_W)))
    dz = mm(dr1b, w_out, tb=True, name="ssm_out_dx")

    def glu_bwd(v, dz):
        val, sg = v[:, :D_MODEL], _sigmoid(v[:, D_MODEL:])
        return (jnp.concatenate([dz * sg, dz * val * sg * (1.0 - sg)], axis=1),), ()
    (dvg,) = rowwise(glu_bwd, (vg, dz), ((2 * D_MODEL, BF16),), name="glu_bwd")
    g["ssm_w_glu"] = mm(yg, dvg, ta=True, name="glu_proj_dw", tiles=(None, glu_tile, None), out_view=_out_cols(w_glu.shape))
    mid = put_rows(mid, packed_shards(g, MISC_MID, MID_ROWS - MISC_MID_OFF), MISC_MID_OFF)
    dypre = mm(dvg, w_glu, tb=True, epi=lambda r, y: (r * _gelu_grad(y),), extras=(ypre,), n_dim=D_MODEL,
               tiles=(None, D_MODEL, glu_tile), b_view=_b_cols_t, name="glu_proj_dx",
               ride=SwapRide(mid) if comm is not None else None)
    sends = None
    if comm is not None:
        dypre, (theirs,) = dypre
        sends = SendRide([early_sums, add_halves(mid, theirs, comm[1])])
    (dx, dbbd_re, dbbd_im, dcbd_re, dcbd_imn, dar, dai, dd), got = s5_bwd(
        dypre, x, dr1, h_re, h_im, bbd_re, bbd_im, cbd_re, cbd_imn, a_re, a_im, dskip, sends)
    dbb_re = _blockdiag_in_t(dbbd_re).reshape(N_STATES, SSM_GROUP)
    dbb_im = _blockdiag_in_t(dbbd_im).reshape(N_STATES, SSM_GROUP)
    dlr, dli, dldt, db_re, db_im = s5_prep_bwd(lr, li, ldt, b_re, b_im, dar.reshape(N_STATES, 1),
                                               dai.reshape(N_STATES, 1), dbb_re, dbb_im)
    g["ssm_lam_re"] = dlr.reshape(1, N_GROUPS, SSM_STATE)
    g["ssm_lam_im"] = dli.reshape(1, N_GROUPS, SSM_STATE)
    g["ssm_log_dt"] = group_sum(dldt).reshape(1, N_GROUPS)
    g["ssm_b_re"] = db_re.reshape(1, N_GROUPS, SSM_STATE, SSM_GROUP)
    g["ssm_b_im"] = db_im.reshape(1, N_GROUPS, SSM_STATE, SSM_GROUP)
    g["ssm_c_re"] = _blockdiag_out_t(dcbd_re).reshape(1, N_GROUPS, SSM_GROUP, SSM_STATE)
    g["ssm_c_im"] = -_blockdiag_out_t(dcbd_imn).reshape(1, N_GROUPS, SSM_GROUP, SSM_STATE)
    g["ssm_d"] = dd
    g["ln_mix_g"] = jnp.concatenate([dg_m0, dg_m1], 0)
    g["ln_mix_b"] = jnp.concatenate([db_m0, db_m1], 0)
    g["ln_ffn_g"] = jnp.concatenate([dg_f0, dg_f1], 0)
    g["ln_ffn_b"] = jnp.concatenate([db_f0, db_f1], 0)
    g["kv_norm_g"] = dkvn_g.reshape(KV_LORA)
    g["q_norm_g"] = dqn_g
    return loss, dx, pack, mid, g, list(zip(sends.ins, got)) if comm is not None else None


def place(shard, me_idx, dtype, name, layer=None):
    rows, cols = shard.shape[-2:]
    tr = _tile(rows, (512, 256, 128))

    def body(m_ref, x_ref, o_ref):
        o_ref[...] = x_ref[...].astype(o_ref.dtype)

    in_spec = (pl.BlockSpec((tr, cols), lambda i, m: (i, 0)) if layer is None
               else pl.BlockSpec((None, tr, cols), lambda i, m: (layer, i, 0)))
    return _pcall(
        body, name=name,
        grid_spec=pltpu.PrefetchScalarGridSpec(
            num_scalar_prefetch=1, grid=(rows // tr,), in_specs=[in_spec],
            out_specs=pl.BlockSpec((None, tr, cols), lambda i, m: (m[0], i, 0))),
        out_shape=jax.ShapeDtypeStruct((N_CHIPS, rows, cols), dtype),
        compiler_params=_params(("parallel",)),
    )(me_idx, shard)


def place_many(shards, dtypes, me_idx, name):
    def body(m_ref, *refs):
        for x_ref, o_ref in zip(refs[:len(shards)], refs[len(shards):]):
            o_ref[...] = x_ref[...].astype(o_ref.dtype)

    return _pcall(
        body, name=name,
        grid_spec=pltpu.PrefetchScalarGridSpec(
            num_scalar_prefetch=1, grid=(1,),
            in_specs=[pl.BlockSpec(s.shape, lambda i, m: (0, 0)) for s in shards],
            out_specs=[pl.BlockSpec((None,) + s.shape, lambda i, m: (m[0], 0, 0)) for s in shards]),
        out_shape=[jax.ShapeDtypeStruct((N_CHIPS,) + s.shape, d) for s, d in zip(shards, dtypes)],
        compiler_params=_params(("arbitrary",)),
    )(me_idx, *shards)


def put_rows(pack, rows, off):
    _, n, cols = rows.shape

    def body(r_ref, p_ref, o_ref, sem):
        cp = pltpu.make_async_copy(r_ref.at[0], o_ref.at[pl.program_id(0), pl.ds(off, n), :], sem)
        cp.start()
        cp.wait()

    return _pcall(body, name="grad_put_rows", grid=(N_CHIPS,),
                  in_specs=[pl.BlockSpec((1, n, cols), lambda k: (k, 0, 0)), _ANY], out_specs=_ANY,
                  out_shape=jax.ShapeDtypeStruct(pack.shape, pack.dtype), input_output_aliases={1: 0},
                  scratch_shapes=[pltpu.SemaphoreType.DMA],
                  compiler_params=_params(("arbitrary",)))(rows, pack)


def _my_cols(c, mine=True):
    start = (c if mine else 1 - c) * HALF_W
    return pl.ds(pl.multiple_of(start, HALF_W), HALF_W)


def add_halves(gpack, got, c_idx):
    n, rows, _ = gpack.shape
    tr = min(G_BLOCK_ROWS, rows)
    blk = (None, tr, HALF_W)

    def body(c_ref, g_ref, r_ref, o_ref):
        o_ref[...] = (g_ref[...] + r_ref[...]).astype(o_ref.dtype)

    return _pcall(
        body, name="grad_add_halves",
        grid_spec=pltpu.PrefetchScalarGridSpec(
            num_scalar_prefetch=1, grid=(n, rows // tr),
            in_specs=[pl.BlockSpec(blk, lambda k, i, c: (k, i, c[0])), pl.BlockSpec(blk, lambda k, i, c: (k, i, 0))],
            out_specs=pl.BlockSpec(blk, lambda k, i, c: (k, i, 0))),
        out_shape=jax.ShapeDtypeStruct((n, rows, HALF_W), BF16),
        compiler_params=_params(("parallel", "parallel")),
    )(c_idx, gpack, got)


def sum_owner(part, got, idx, total_rows, row_off=0, into=None):
    _, rows, _ = part.shape
    tr = math.gcd(math.gcd(rows, row_off), G_BLOCK_ROWS)
    n_into = 0 if into is None else 1

    def body(m_ref, p_ref, g_ref, *rest):
        up = lambda v: v.astype(F32)
        rest[-1][...] = ((up(p_ref[...]) + up(g_ref[0])) + up(g_ref[1])) + up(g_ref[2])

    return _pcall(
        body, name="grad_sum_owner",
        grid_spec=pltpu.PrefetchScalarGridSpec(
            num_scalar_prefetch=1, grid=(rows // tr,),
            in_specs=[pl.BlockSpec((None, tr, HALF_W), lambda i, m: (m[0], i, 0)),
                      pl.BlockSpec((3, tr, HALF_W), lambda i, m: (0, i, 0))] + [_ANY] * n_into,
            out_specs=pl.BlockSpec((tr, HALF_W), lambda i, m: (row_off // tr + i, m[1]))),
        out_shape=jax.ShapeDtypeStruct((total_rows, PACK_W), F32),
        input_output_aliases={3: 0} if n_into else {},
        compiler_params=_params(("parallel",)),
    )(idx, part, got, *([into] if n_into else []))


def join_halves(red):
    def body(in_ref, out_ref, send_sem, recv_sem):
        x, y, c, _ = _place()
        sibling = (x, y, 1 - c)
        mine = out_ref.at[:, _my_cols(c)]
        cp = pltpu.make_async_remote_copy(src_ref=mine, dst_ref=mine, send_sem=send_sem, recv_sem=recv_sem,
                                          device_id=sibling, device_id_type=MESH)
        cp.start()
        cp.wait_send()
        other = out_ref.at[:, _my_cols(c, mine=False)]
        pltpu.make_async_remote_copy(src_ref=other, dst_ref=other, send_sem=send_sem, recv_sem=recv_sem,
                                     device_id=sibling, device_id_type=MESH).wait_recv()

    return _pcall(body, name="grad_join_halves", in_specs=[_ANY], out_specs=_ANY,
                  out_shape=jax.ShapeDtypeStruct(red.shape, red.dtype), input_output_aliases={0: 0},
                  scratch_shapes=[pltpu.SemaphoreType.DMA, pltpu.SemaphoreType.DMA])(red)


def adamw(gsrc, g_off, wt, m, v, name):
    n, cols = wt.shape
    tr = math.gcd(math.gcd(g_off, n), 256) if g_off else math.gcd(n, 256)
    off_blk = g_off // tr
    c1 = 1.0 / (1.0 - ADAM_B1 ** ADAM_STEP)
    c2 = 1.0 / (1.0 - ADAM_B2 ** ADAM_STEP)

    def body(g_ref, w_ref, m_ref, v_ref, go_ref, d_ref, mo_ref, vo_ref):
        gv = g_ref[...]
        mn = ADAM_B1 * m_ref[...] + (1.0 - ADAM_B1) * gv
        vn = ADAM_B2 * v_ref[...] + (1.0 - ADAM_B2) * gv * gv
        go_ref[...] = gv
        mo_ref[...] = mn
        vo_ref[...] = vn
        d_ref[...] = -ADAM_LR * ((mn * c1) / (jnp.sqrt(vn * c2) + ADAM_EPS) + ADAM_WD * w_ref[...])

    blk = pl.BlockSpec((tr, cols), lambda i: (i, 0))
    return _pcall(body, name=name, grid=(n // tr,),
                  in_specs=[pl.BlockSpec((tr, cols), lambda i: (off_blk + i, 0)), blk, blk, blk],
                  out_specs=[blk] * 4, out_shape=[jax.ShapeDtypeStruct((n, cols), F32)] * 4,
                  compiler_params=_params(("parallel",)))(gsrc, wt, m, v)


def _rows8(a):
    return -(-a.size // (8 * PACK_W)) * 8


def _as_rows(a, rows=None):
    flat = a.reshape(-1)
    n = _rows8(a) if rows is None else rows
    return jnp.pad(flat, (0, n * PACK_W - flat.shape[0])).reshape(n, PACK_W)


def local_shards_2d(wl):
    return {"w_ff1": [wl["w_ff1"][0], wl["w_ff1"][1]], "w_ff2": [wl["w_ff2"][0], wl["w_ff2"][1]],
            "ssm_w_glu": wl["ssm_w_glu"], "ssm_w_out": wl["ssm_w_out"], "kv_w_a": _pad_kva_cols(wl["kv_w_a"]),
            "kv_w_b": wl["kv_w_b"], "q_w_a": wl["q_w_a"], "q_w_b": _perm_q_cols(wl["q_w_b"]),
            "attn_w_o": wl["attn_w_o"], "ssm_d": wl["ssm_d"].reshape(2, -1)}


def misc_grad_shard(name, g, k):
    if name == "ssm_d":
        w = D_MODEL // N_CHIPS
        return g[:, w * k:w * (k + 1)]
    if name in ("ssm_w_glu", "kv_w_b"):
        return g[k]
    if name == "q_w_b":
        return _unperm_q_cols(g[k])
    rows = D_MODEL // N_CHIPS
    shard = g[rows * k:rows * (k + 1)]
    return _unpad_kva_cols(shard) if name == "kv_w_a" else shard


def packed_shards(g, names, rows, tail=None):
    blocks = []
    for k in range(N_CHIPS):
        parts = [_as_rows(misc_grad_shard(n, g[n], k), MISC_SHARD_ROWS[n]) for n in names]
        if tail is not None:
            parts.append(tail[k * (tail.shape[0] // N_CHIPS):(k + 1) * (tail.shape[0] // N_CHIPS)])
        blk = jnp.concatenate(parts, axis=0)
        blocks.append(jnp.pad(blk, ((0, rows - blk.shape[0]), (0, 0))))
    return jnp.stack(blocks)


def kernel(x, positions, ln_mix_g, ln_mix_b, ln_ffn_g, ln_ffn_b, w_ff1, w_ff2, ssm_lam_re, ssm_lam_im, ssm_log_dt, ssm_b_re, ssm_b_im, ssm_c_re, ssm_c_im, ssm_d, ssm_w_glu, ssm_w_out, kv_w_a, kv_norm_g, kv_w_b, q_w_a, q_norm_g, q_w_b, attn_w_o, loss_target, m_ln_mix_g, m_ln_mix_b, m_ln_ffn_g, m_ln_ffn_b, m_w_ff1, m_w_ff2, m_ssm_lam_re, m_ssm_lam_im, m_ssm_log_dt, m_ssm_b_re, m_ssm_b_im, m_ssm_c_re, m_ssm_c_im, m_ssm_d, m_ssm_w_glu, m_ssm_w_out, m_kv_w_a, m_kv_norm_g, m_kv_w_b, m_q_w_a, m_q_norm_g, m_q_w_b, m_attn_w_o, v_ln_mix_g, v_ln_mix_b, v_ln_ffn_g, v_ln_ffn_b, v_w_ff1, v_w_ff2, v_ssm_lam_re, v_ssm_lam_im, v_ssm_log_dt, v_ssm_b_re, v_ssm_b_im, v_ssm_c_re, v_ssm_c_im, v_ssm_d, v_ssm_w_glu, v_ssm_w_out, v_kv_w_a, v_kv_norm_g, v_kv_w_b, v_q_w_a, v_q_norm_g, v_q_w_b, v_attn_w_o):
    env = dict(locals())
    wl = {n: env[n] for n in WEIGHTS}
    ml = {n: env["m_" + n] for n in WEIGHTS}
    vl = {n: env["v_" + n] for n in WEIGHTS}
    for n in ("ssm_w_glu", "ssm_w_out", "q_w_a", "q_w_b", "attn_w_o"):
        wl[n], ml[n], vl[n] = wl[n][0], ml[n][0], vl[n][0]

    c_idx = lax.axis_index("c").astype(jnp.int32).reshape(1)
    me_idx = (2 * lax.axis_index("x") + lax.axis_index("y")).astype(jnp.int32).reshape(1)

    local = local_shards_2d(wl)
    stacked = {n: [place(wl[n], me_idx, BF16, f"place_{n}_{l}", layer=l) for l in range(DEPTH)] for n in ("w_ff1", "w_ff2")}
    others = [n for n in SHARDED if n not in stacked]
    stacked.update(zip(others, place_many([local[n] for n in others], [F32 if n == "ssm_d" else BF16 for n in others],
                                          me_idx, "place_others")))
    stacked["ssm_d"] = ride_alone(GatherRide([_halves(stacked["ssm_d"])]), "ssm_d_all_gather")[0].reshape(1, D_MODEL)
    for n in REPLICATED:
        stacked[n] = wl[n]

    loss_part, dx, early, mid, g, sent = device_step(x[0], positions[0], loss_target[0], stacked, comm=(me_idx, c_idx))
    loss = lax.psum(loss_part, ("x", "y", "c"))

    small = jnp.concatenate([_as_rows(g[n]) for n in REPLICATED], axis=0)
    small = jnp.pad(small, ((0, SMALL_ROWS - small.shape[0]), (0, 0)))
    late = packed_shards(g, MISC_LATE, LATE_ROWS, tail=small)
    late_sums = add_halves(late, ride_alone(SwapRide(late), "grad_swap_halves")[0], c_idx)
    sent.append((late_sums, ride_alone(SendRide([late_sums]), "grad_send_to_owners")[0]))
    where = jnp.concatenate([me_idx, c_idx])
    starts = (0, EARLY_ROWS, EARLY_ROWS + MID_ROWS)
    total_rows = EARLY_ROWS + MID_ROWS + LATE_ROWS
    reduced = None
    for (sums, got), off in zip(sent, starts):
        reduced = sum_owner(sums, got, where, total_rows, row_off=off, into=reduced)
    reduced = join_halves(reduced)
    quarter = reduced[starts[2] + SMALL_OFF:starts[2] + SMALL_OFF + SMALL_Q_ROWS]
    small_tot = ride_alone(GatherRide([_halves(place(quarter, me_idx, F32, "place_small_grads"))]),
                           "small_grad_all_gather")[0].reshape(SMALL_ROWS, PACK_W)

    out_g, out_d, out_m, out_v = {}, {}, {}, {}
    direct = {**EARLY_OFF, **{n: starts[1] + o for n, o in MID_OFF.items()}}
    for n, off in direct.items():
        res = adamw(reduced, off, wl[n].reshape(-1, PACK_W), ml[n].reshape(-1, PACK_W), vl[n].reshape(-1, PACK_W),
                    "adamw_" + n)
        out_g[n], out_d[n], out_m[n], out_v[n] = [a.reshape(env[n].shape) for a in res]
    for names, off in ((MISC_EARLY, MISC_EARLY_OFF), (MISC_MID, starts[1] + MISC_MID_OFF), (MISC_LATE, starts[2])):
        pack3 = lambda d: jnp.concatenate([_as_rows(d[n], MISC_SHARD_ROWS[n]) for n in names], axis=0)
        res = adamw(reduced, off, pack3(wl), pack3(ml), pack3(vl), "adamw_packed_" + names[0])
        r0 = 0
        for n in names:
            cnt = math.prod(env[n].shape)
            out_g[n], out_d[n], out_m[n], out_v[n] = [
                a[r0:r0 + MISC_SHARD_ROWS[n]].reshape(-1)[:cnt].reshape(env[n].shape) for a in res]
            r0 += MISC_SHARD_ROWS[n]
    ws = jnp.concatenate([_as_rows(wl[n]) for n in REPLICATED], axis=0)
    ms = jnp.concatenate([_as_rows(ml[n]) for n in REPLICATED], axis=0)
    vs = jnp.concatenate([_as_rows(vl[n]) for n in REPLICATED], axis=0)
    pad = ((0, SMALL_ROWS - ws.shape[0]), (0, 0))
    res = adamw(small_tot, 0, jnp.pad(ws, pad), jnp.pad(ms, pad), jnp.pad(vs, pad), "adamw_replicated")
    row = 0
    for n in REPLICATED:
        cnt = math.prod(env[n].shape)
        nrows = _rows8(env[n])
        out_g[n], out_d[n], out_m[n], out_v[n] = [a[row:row + nrows].reshape(-1)[:cnt].reshape(env[n].shape) for a in res]
        row += nrows

    return (loss, dx[None], *[out_g[n] for n in WEIGHTS], *[out_d[n] for n in WEIGHTS],
            *[out_m[n] for n in WEIGHTS], *[out_v[n] for n in WEIGHTS])
```

```python
import functools
import math

import jax
import jax.numpy as jnp
from jax import lax
from jax.experimental import pallas as pl
from jax.experimental.pallas import tpu as pltpu

F32 = jnp.float32
BF16 = jnp.bfloat16
MESH = pl.DeviceIdType.MESH

D_MODEL = 1024
DEPTH = 2
SSM_GROUP = 16
N_GROUPS = D_MODEL // SSM_GROUP
SSM_STATE = 64
N_STATES = N_GROUPS * SSM_STATE
N_HEADS = 8
QK_NOPE = 128
QK_ROPE = 64
HALF_ROPE = QK_ROPE // 2
V_HEAD = 128
QK_DIM = QK_NOPE + QK_ROPE
Q_LORA = 384
KV_LORA = 256
ROPE_THETA = 10000.0
SM_SCALE = QK_DIM ** -0.5
NEG_INF = -1e30
D_FF = 4 * D_MODEL
DN_ALPHA = (2 * DEPTH) ** 0.25
LN_EPS = 1e-5
RMS_EPS = 1e-6
ADAM_LR = 0.001
ADAM_B1 = 0.9
ADAM_B2 = 0.999
ADAM_EPS = 1e-08
ADAM_WD = 0.01
ADAM_STEP = 10

N_CHIPS = 4
LANES = 128
VMEM_LIMIT = 56 * 1024 * 1024
MM_VMEM_BUDGET = 40 * 1024 * 1024
PACK_W = 1024
KVA_PAD = 384
HALF_W = PACK_W // 2

SHARDED = ("w_ff1", "w_ff2", "ssm_w_glu", "ssm_w_out", "kv_w_a", "kv_w_b", "q_w_a", "q_w_b", "attn_w_o", "ssm_d")
G_BLOCK_ROWS = 960
EARLY_OFF = {"w_ff1": 0, "w_ff2": 2048, "attn_w_o": 4096}
MISC_EARLY = ("kv_w_b", "kv_w_a", "q_w_a", "q_w_b")
MISC_EARLY_OFF = 4352
EARLY_ROWS = 5 * G_BLOCK_ROWS
MID_OFF = {"ssm_w_out": 0}
MISC_MID = ("ssm_w_glu",)
MISC_MID_OFF = 256
MID_ROWS = MISC_MID_OFF + 512
MISC_LATE = ("ssm_d",)
SMALL_Q_ROWS = 96
SMALL_ROWS = N_CHIPS * SMALL_Q_ROWS
SMALL_OFF = 16
LATE_ROWS = 192
MISC_SHARD_ROWS = {"ssm_d": 16, "ssm_w_glu": 512, "kv_w_b": 128, "kv_w_a": 80, "q_w_a": 96, "q_w_b": 144}
REPLICATED = ("ln_mix_g", "ln_mix_b", "ln_ffn_g", "ln_ffn_b", "ssm_lam_re", "ssm_lam_im", "ssm_log_dt",
              "ssm_b_re", "ssm_b_im", "ssm_c_re", "ssm_c_im", "kv_norm_g", "q_norm_g")
WEIGHTS = ("ln_mix_g", "ln_mix_b", "ln_ffn_g", "ln_ffn_b", "w_ff1", "w_ff2", "ssm_lam_re", "ssm_lam_im",
           "ssm_log_dt", "ssm_b_re", "ssm_b_im", "ssm_c_re", "ssm_c_im", "ssm_d", "ssm_w_glu", "ssm_w_out",
           "kv_w_a", "kv_norm_g", "kv_w_b", "q_w_a", "q_norm_g", "q_w_b", "attn_w_o")


def _pcall(body, **kw):
    return pl.pallas_call(body, **kw)


def _params(sem=None):
    return pltpu.CompilerParams(dimension_semantics=sem, vmem_limit_bytes=VMEM_LIMIT)


_ANY = pl.BlockSpec(memory_space=pl.ANY)


def _tile(dim, prefs):
    for p in prefs:
        if dim % p == 0:
            return p
    return dim


def _place():
    x, y, c = lax.axis_index("x"), lax.axis_index("y"), lax.axis_index("c")
    return x, y, c, [(1 - x, y), (x, 1 - y), (1 - x, 1 - y)]


def _remote(k, src, dst, to, send_sems, recv_sems):
    return pltpu.make_async_remote_copy(src_ref=src, dst_ref=dst, send_sem=send_sems.at[k], recv_sem=recv_sems.at[k],
                                        device_id=to, device_id_type=MESH)


class GatherRide:
    def __init__(self, arrs):
        self.ins = list(arrs)
        self.out_shapes = [jax.ShapeDtypeStruct(a.shape, a.dtype) for a in arrs]
        self.aliases = {i: i for i in range(len(arrs))}
        self.n_sems = 6 * len(arrs)

    def start(self, ins, outs, send_sems, recv_sems):
        x, y, c, chips = _place()
        me = 2 * x + y
        for a, o in enumerate(outs):
            for j, (px, py) in enumerate(chips):
                _remote(6 * a + j, o.at[me, c], o.at[me, c], (px, py, c), send_sems, recv_sems).start()

    def pass_on(self, ins, outs, send_sems, recv_sems):
        x, y, c, chips = _place()
        for a, o in enumerate(outs):
            for j, (px, py) in enumerate(chips):
                blk = o.at[2 * px + py, c]
                _remote(6 * a + j, blk, blk, (px, py, c), send_sems, recv_sems).wait_recv()
                _remote(6 * a + 3 + j, blk, blk, (x, y, 1 - c), send_sems, recv_sems).start()

    def finish(self, ins, outs, send_sems, recv_sems, passed_on=False):
        if not passed_on:
            self.pass_on(ins, outs, send_sems, recv_sems)
        x, y, c, chips = _place()
        me = 2 * x + y
        sibling = (x, y, 1 - c)
        for a, o in enumerate(outs):
            for j, (px, py) in enumerate(chips):
                blk = o.at[2 * px + py, 1 - c]
                _remote(6 * a + 3 + j, blk, blk, sibling, send_sems, recv_sems).wait_recv()
                _remote(6 * a + j, o.at[me, c], o.at[me, c], (px, py, c), send_sems, recv_sems).wait_send()
                mine = o.at[2 * px + py, c]
                _remote(6 * a + 3 + j, mine, mine, sibling, send_sems, recv_sems).wait_send()


class SendRide:
    def __init__(self, parts):
        self.ins = list(parts)
        self.out_shapes = [jax.ShapeDtypeStruct((3,) + p.shape[1:], p.dtype) for p in parts]
        self.aliases = {}
        self.n_sems = 3 * len(parts)

    def _copies(self, ins, outs, send_sems, recv_sems):
        x, y, c, chips = _place()
        return [_remote(3 * a + j, ins[a].at[2 * px + py], outs[a].at[j], (px, py, c), send_sems, recv_sems)
                for a in range(len(ins)) for j, (px, py) in enumerate(chips)]

    def start(self, ins, outs, send_sems, recv_sems):
        for cp in self._copies(ins, outs, send_sems, recv_sems):
            cp.start()

    def finish(self, ins, outs, send_sems, recv_sems):
        for cp in self._copies(ins, outs, send_sems, recv_sems):
            cp.wait()


class SwapRide:
    def __init__(self, pack, ranges=None, into=None):
        self.ins = [pack] if into is None else [pack, into]
        self.out_shapes = [jax.ShapeDtypeStruct(pack.shape[:2] + (HALF_W,), pack.dtype)]
        self.aliases = {} if into is None else {1: 0}
        self.ranges = ranges or [(0, pack.shape[1])]
        self.n_sems = len(self.ranges)

    def _copies(self, ins, outs, send_sems, recv_sems):
        x, y, c, _ = _place()
        return [_remote(k, ins[0].at[:, pl.ds(r0, n), _my_cols(c, mine=False)], outs[0].at[:, pl.ds(r0, n), :],
                        (x, y, 1 - c), send_sems, recv_sems) for k, (r0, n) in enumerate(self.ranges)]

    def start(self, ins, outs, send_sems, recv_sems):
        for cp in self._copies(ins, outs, send_sems, recv_sems):
            cp.start()

    def finish(self, ins, outs, send_sems, recv_sems):
        for cp in self._copies(ins, outs, send_sems, recv_sems):
            cp.wait()


def _pcall_riding(body, args, ride, first, last, *, in_specs, out_specs, out_shape, scratch_shapes=(), middle=None,
                  **kw):
    n_in, n_out = len(args), len(out_shape)
    if ride is None:
        return _pcall(body, in_specs=in_specs, out_specs=out_specs, out_shape=out_shape,
                      scratch_shapes=list(scratch_shapes), **kw)(*args), []
    k_in, k_out = len(ride.ins), len(ride.out_shapes)

    def riding(*refs):
        ins, r_in = refs[:n_in], refs[n_in:n_in + k_in]
        outs = refs[n_in + k_in:n_in + k_in + n_out]
        r_out = refs[n_in + k_in + n_out:n_in + k_in + n_out + k_out]
        scratch, (send_sems, recv_sems) = refs[n_in + k_in + n_out + k_out:-2], refs[-2:]

        @pl.when(first())
        def _():
            ride.start(r_in, r_out, send_sems, recv_sems)

        if middle is not None:
            @pl.when(middle())
            def _():
                ride.pass_on(r_in, r_out, send_sems, recv_sems)

        body(*ins, *outs, *scratch)

        @pl.when(last())
        def _():
            if middle is not None:
                ride.finish(r_in, r_out, send_sems, recv_sems, passed_on=True)
            else:
                ride.finish(r_in, r_out, send_sems, recv_sems)

    res = _pcall(riding, in_specs=list(in_specs) + [_ANY] * k_in, out_specs=list(out_specs) + [_ANY] * k_out,
                 out_shape=list(out_shape) + ride.out_shapes,
                 input_output_aliases={n_in + i: n_out + o for i, o in ride.aliases.items()},
                 scratch_shapes=list(scratch_shapes) + [pltpu.SemaphoreType.DMA((ride.n_sems,))] * 2,
                 **kw)(*args, *ride.ins)
    return res[:n_out], res[n_out:]


def ride_alone(ride, name):
    def body(*refs):
        n = len(ride.ins)
        ins, outs, (send_sems, recv_sems) = refs[:n], refs[n:-2], refs[-2:]
        ride.start(ins, outs, send_sems, recv_sems)
        ride.finish(ins, outs, send_sems, recv_sems)

    return _pcall(body, name=name, in_specs=[_ANY] * len(ride.ins), out_specs=[_ANY] * len(ride.out_shapes),
                  out_shape=ride.out_shapes, input_output_aliases=dict(ride.aliases),
                  scratch_shapes=[pltpu.SemaphoreType.DMA((ride.n_sems,))] * 2)(*ride.ins)


def mm(a, b, *, name, ta=False, tb=False, pro_a=None, epi=None, extras=(), out_dtypes=(F32,), n_dim=None,
       tiles=(None, None, None), b_view=None, out_view=None, into=None, ride=None):
    if ta:
        k_dim, m_dim = a.shape
    else:
        m_dim, k_dim = a.shape
    if n_dim is None:
        n_dim = b.shape[0] if tb else b.shape[1]
    tn = tiles[1] or (n_dim if n_dim <= 1024 else _tile(n_dim, (1024, 512, 256, 128)))
    tk = tiles[2] or (k_dim if k_dim <= 1024 else _tile(k_dim, (1024, 512, 256, 128)))
    nk = k_dim // tk

    def vmem_bytes(tm):
        blocks = tm * tk * a.dtype.itemsize + tk * tn * b.dtype.itemsize
        blocks += sum(tm * (tn if e.shape[1] == n_dim else e.shape[1]) * e.dtype.itemsize for e in extras if e.shape[0] > 1)
        blocks += tm * tn * sum(jnp.dtype(d).itemsize for d in out_dtypes)
        return 2 * blocks + tm * tn * 4

    tm = tiles[0] or next((t for t in (4096, 2048, 1024, 512, 256) if m_dim % t == 0 and vmem_bytes(t) <= MM_VMEM_BUDGET),
                          _tile(m_dim, (128,)))
    assert m_dim % tm == 0 and n_dim % tn == 0 and k_dim % tk == 0, (name, m_dim, n_dim, k_dim, tm, tn, tk)
    n_ex, n_out = len(extras), len(out_dtypes)
    n_into = 0 if into is None else 1
    dims = (((0 if ta else 1,), (1 if tb else 0,)), ((), ()))

    def body(a_ref, b_ref, *rest):
        ex_refs, out_refs = rest[:n_ex], rest[n_ex + n_into:n_ex + n_into + n_out]

        def partial():
            av = a_ref[...]
            if pro_a is not None:
                av = pro_a(av)
            return lax.dot_general(av.astype(BF16), b_ref[...].astype(BF16), dims, preferred_element_type=F32)

        def finish(r):
            res = epi(r, *[e[...] for e in ex_refs]) if epi is not None else (r,)
            for o_ref, v in zip(out_refs, res):
                o_ref[...] = v.reshape(o_ref.shape).astype(o_ref.dtype)

        if nk == 1:
            finish(partial())
            return
        acc = rest[-1]
        k = pl.program_id(2)

        @pl.when(k == 0)
        def _():
            acc[...] = partial()

        @pl.when(k > 0)
        def _():
            acc[...] += partial()

        @pl.when(k == nk - 1)
        def _():
            finish(acc[...])

    def ex_spec(e):
        if e.shape == (m_dim, n_dim):
            return o_spec
        if e.shape[0] == m_dim:
            return pl.BlockSpec((tm, e.shape[1]), lambda i, j, k: (i, 0))
        return pl.BlockSpec(e.shape, lambda i, j, k: (0, 0))

    a_spec = pl.BlockSpec((tk, tm), lambda i, j, k: (k, i)) if ta else pl.BlockSpec((tm, tk), lambda i, j, k: (i, k))
    if b_view is not None:
        b_spec = b_view(tk, tn)
    else:
        b_spec = pl.BlockSpec((tn, tk), lambda i, j, k: (j, k)) if tb else pl.BlockSpec((tk, tn), lambda i, j, k: (k, j))
    o_spec = pl.BlockSpec((tm, tn), lambda i, j, k: (i, j))
    if out_view is None:
        out_specs = [o_spec] * n_out
        out_shape = [jax.ShapeDtypeStruct((m_dim, n_dim), dt) for dt in out_dtypes]
    else:
        assert n_out == 1
        out_specs = [out_view[1](tm, tn)]
        out_shape = [jax.ShapeDtypeStruct(out_view[0], out_dtypes[0])]
    grid = (m_dim // tm, n_dim // tn, nk)
    scratch = [pltpu.VMEM((tm, tn), F32)] if nk > 1 else []
    if ride is not None:
        assert into is None
        at = lambda ids: functools.reduce(jnp.logical_and, [pl.program_id(d) == i for d, i in enumerate(ids)])
        outs, landed = _pcall_riding(
            body, (a, b, *extras), ride, lambda: at((0, 0, 0)), lambda: at([g - 1 for g in grid]),
            name=name, grid=grid, in_specs=[a_spec, b_spec] + [ex_spec(e) for e in extras], out_specs=out_specs,
            out_shape=out_shape, scratch_shapes=scratch, compiler_params=_params(("arbitrary",) * 3))
        return (outs[0] if n_out == 1 else outs), landed
    outs = _pcall(
        body, name=name, grid=grid,
        in_specs=[a_spec, b_spec] + [ex_spec(e) for e in extras] + [_ANY] * n_into,
        out_specs=out_specs, out_shape=out_shape,
        input_output_aliases={2 + n_ex: 0} if n_into else {},
        scratch_shapes=scratch,
        compiler_params=_params(("parallel", "parallel", "arbitrary")),
    )(a, b, *extras, *([into] if n_into else []))
    return outs[0] if n_out == 1 else outs


def rowwise(fn, ins, outs, *, name, accs=(), tm=256):
    rows = ins[0].shape[0]
    tm = min(tm, rows)
    n_in, n_out, n_acc = len(ins), len(outs), len(accs)

    def body(*refs):
        in_refs, out_refs, acc_refs = refs[:n_in], refs[n_in:n_in + n_out], refs[n_in + n_out:]
        res, sums = fn(*[r[...] for r in in_refs])
        for o_ref, v in zip(out_refs, res):
            o_ref[...] = v.astype(o_ref.dtype)
        if n_acc:
            @pl.when(pl.program_id(0) == 0)
            def _():
                for a_ref in acc_refs:
                    a_ref[...] = jnp.zeros_like(a_ref)

            for a_ref, s in zip(acc_refs, sums):
                a_ref[...] += s

    def spec(arr):
        if arr.shape[0] == rows:
            return pl.BlockSpec((tm, arr.shape[1]), lambda i: (i, 0))
        return pl.BlockSpec(arr.shape, lambda i: (0, 0))

    res = _pcall(
        body, name=name, grid=(rows // tm,),
        in_specs=[spec(a) for a in ins],
        out_specs=[pl.BlockSpec((tm, w), lambda i: (i, 0)) for w, _ in outs]
        + [pl.BlockSpec((1, w), lambda i: (0, 0)) for w in accs],
        out_shape=[jax.ShapeDtypeStruct((rows, w), dt) for w, dt in outs]
        + [jax.ShapeDtypeStruct((1, w), F32) for w in accs],
        compiler_params=_params(("arbitrary",) if n_acc else ("parallel",)),
    )(*ins)
    return res


def _relu2(v):
    r = jnp.maximum(v, 0.0)
    return r * r


def _gelu(x):
    c = math.sqrt(2.0 / math.pi)
    return 0.5 * x * (1.0 + jnp.tanh(c * (x + 0.044715 * x * x * x)))


def _gelu_grad(x):
    c = math.sqrt(2.0 / math.pi)
    t = jnp.tanh(c * (x + 0.044715 * x * x * x))
    return 0.5 * (1.0 + t) + 0.5 * x * (1.0 - t * t) * c * (1.0 + 3 * 0.044715 * x * x)


def _sigmoid(x):
    return 1.0 / (1.0 + jnp.exp(-x))


def _layer_norm(h, mix, g, b):
    r = DN_ALPHA * h + mix
    mu = jnp.mean(r, axis=-1, keepdims=True)
    xc = r - mu
    var = jnp.mean(xc * xc, axis=-1, keepdims=True)
    return xc * lax.rsqrt(var + LN_EPS) * g + b


def _layer_norm_bwd(h, mix, g, dy):
    r = DN_ALPHA * h + mix
    mu = jnp.mean(r, axis=-1, keepdims=True)
    xc = r - mu
    var = jnp.mean(xc * xc, axis=-1, keepdims=True)
    rstd = lax.rsqrt(var + LN_EPS)
    xhat = xc * rstd
    dxh = dy * g
    m1 = jnp.mean(dxh, axis=-1, keepdims=True)
    m2 = jnp.mean(dxh * xhat, axis=-1, keepdims=True)
    dr = rstd * (dxh - m1 - xhat * m2)
    return dr, jnp.sum(dy * xhat, axis=0, keepdims=True), jnp.sum(dy, axis=0, keepdims=True)


def ln_bwd(h, mix, g, dy, name):
    def fn(h, mix, g, dy):
        dr, dg, db = _layer_norm_bwd(h, mix, g, dy)
        return (dr, dr), (dg, db)
    return rowwise(fn, (h, mix, g, dy), ((D_MODEL, F32), (D_MODEL, BF16)), accs=(D_MODEL, D_MODEL), name=name)


def _rms(x, g):
    r = lax.rsqrt(jnp.mean(x * x, axis=-1, keepdims=True) + RMS_EPS)
    return x * r * g


def _rms_bwd(x, g, dy):
    r = lax.rsqrt(jnp.mean(x * x, axis=-1, keepdims=True) + RMS_EPS)
    xn = x * r
    dyg = dy * g
    dx = r * (dyg - xn * jnp.mean(dyg * xn, axis=-1, keepdims=True))
    return dx, jnp.sum(dy * xn, axis=0, keepdims=True)


def _s5_disc(lr, li, ldt):
    dt = jnp.exp(ldt)
    mag = jnp.exp(lr * dt)
    cs, sn = jnp.cos(li * dt), jnp.sin(li * dt)
    ar, ai = mag * cs, mag * sn
    inv = 1.0 / (lr * lr + li * li)
    n_re = (ar - 1.0) * lr + ai * li
    n_im = ai * lr - (ar - 1.0) * li
    return dt, mag, cs, sn, ar, ai, inv, n_re, n_im


def s5_prep(lr, li, ldt, b_re, b_im):
    def fn(lr, li, ldt, b_re, b_im):
        _, _, _, _, ar, ai, inv, n_re, n_im = _s5_disc(lr, li, ldt)
        cr, ci = n_re * inv, n_im * inv
        return (ar, ai, cr * b_re - ci * b_im, cr * b_im + ci * b_re), ()
    return rowwise(fn, (lr, li, ldt, b_re, b_im), ((1, F32), (1, F32), (SSM_GROUP, F32), (SSM_GROUP, F32)),
                   name="s5_prep", tm=512)


def s5_prep_bwd(lr, li, ldt, b_re, b_im, dar, dai, dbb_re, dbb_im):
    def fn(lr, li, ldt, b_re, b_im, dar, dai, dbb_re, dbb_im):
        dt, mag, cs, sn, ar, ai, inv, n_re, n_im = _s5_disc(lr, li, ldt)
        cr, ci = n_re * inv, n_im * inv
        db_re = cr * dbb_re + ci * dbb_im
        db_im = cr * dbb_im - ci * dbb_re
        dcr = jnp.sum(dbb_re * b_re + dbb_im * b_im, axis=-1, keepdims=True)
        dci = jnp.sum(dbb_im * b_re - dbb_re * b_im, axis=-1, keepdims=True)
        dar = dar + (dcr * lr - dci * li) * inv
        dai = dai + (dcr * li + dci * lr) * inv
        dinv = dcr * n_re + dci * n_im
        dlr = (dcr * (ar - 1.0) + dci * ai) * inv - 2.0 * lr * inv * inv * dinv
        dli = (dcr * ai - dci * (ar - 1.0)) * inv - 2.0 * li * inv * inv * dinv
        dmag = dar * cs + dai * sn
        dth = dai * ar - dar * ai
        dlr = dlr + dmag * mag * dt
        dli = dli + dth * dt
        ddt = dmag * mag * lr + dth * li
        return (dlr, dli, ddt * dt, db_re, db_im), ()
    return rowwise(fn, (lr, li, ldt, b_re, b_im, dar, dai, dbb_re, dbb_im),
                   ((1, F32), (1, F32), (1, F32), (SSM_GROUP, F32), (SSM_GROUP, F32)), name="s5_prep_bwd", tm=512)


def group_sum(x):
    def body(x_ref, o_ref):
        o_ref[...] = jnp.sum(x_ref[...], axis=1)
    return _pcall(body, name="s5_group_sum", out_shape=jax.ShapeDtypeStruct((N_GROUPS, 1), F32))(
        x.reshape(N_GROUPS, SSM_STATE, 1))


GROUPS_PER_TILE = LANES // SSM_GROUP
TILE_STATES = GROUPS_PER_TILE * SSM_STATE
N_UTILES = D_MODEL // LANES


SUBLANES = 8
SCAN_STRIP = 1024
N_STRIPS = N_STATES // SCAN_STRIP
_NT = (((1,), (1,)), ((), ()))
_TN = (((0,), (0,)), ((), ()))


def _scan_coefs(are, aim, shifted, reverse):
    ar = are[...]
    ai = -aim[...] if reverse else aim[...]
    powers = {1: (ar, ai)}
    for d in (2, 4):
        r, i = powers[d // 2]
        powers[d] = (r * r - i * i, 2.0 * r * i)
    rid = lax.broadcasted_iota(jnp.int32, (SUBLANES, N_STATES), 0)
    first = (rid == SUBLANES - 1) if reverse else (rid == 0)
    masks = [(1, first)] + [(d, (rid <= SUBLANES - 1 - d) if reverse else (rid >= d)) for d in (1, 2, 4)]
    for n, (d, keep) in enumerate(masks):
        for part in (0, 1):
            shifted[2 * n + part][...] = jnp.where(keep, jnp.broadcast_to(powers[d][part], (SUBLANES, N_STATES)), 0.0)


def _tile_scan(xr, xi, shifted, nbr_re, nbr_im, reverse):
    for n, d in enumerate((1, 1, 2, 4)):
        by = SUBLANES - d if reverse else d
        fr, fi = (nbr_re, nbr_im) if n == 0 else (xr, xi)
        sr, si = pltpu.roll(fr, by, 0), pltpu.roll(fi, by, 0)
        kr, ki = shifted[2 * n], shifted[2 * n + 1]
        xr, xi = xr + kr * sr - ki * si, xi + kr * si + ki * sr
    return xr, xi


def _tile_rows(t):
    return pl.ds(pl.multiple_of(t * SUBLANES, SUBLANES), SUBLANES)


def s5_fwd(u, bbd_re, bbd_im, cbd_re, cbd_imn, a_re, a_im, dskip, ride=None, t_rows=256):
    seq = u.shape[0]
    t_rows = min(t_rows, seq)
    n_tiles = t_rows // SUBLANES

    def body(u_ref, bre, bim, cre, cimn, are, aim, d_ref, y_ref, gelu_ref, hre_ref, him_ref, car_re, car_im, *shifted):
        @pl.when(pl.program_id(0) == 0)
        def _():
            car_re[...] = jnp.zeros_like(car_re)
            car_im[...] = jnp.zeros_like(car_im)
            _scan_coefs(are, aim, shifted, reverse=False)

        uf = u_ref[...]
        ub = uf.astype(BF16)
        for j in range(N_UTILES):
            uj = ub[:, LANES * j:LANES * (j + 1)]
            sl = slice(TILE_STATES * j, TILE_STATES * (j + 1))
            hre_ref[:, sl] = jnp.dot(uj, bre[j], preferred_element_type=F32)
            him_ref[:, sl] = jnp.dot(uj, bim[j], preferred_element_type=F32)
        for s in range(N_STRIPS):
            cols = pl.ds(s * SCAN_STRIP, SCAN_STRIP)
            coefs = [c[:, cols] for c in shifted]

            def step(t, before):
                rows = _tile_rows(t)
                hr, hi = _tile_scan(hre_ref[rows, cols], him_ref[rows, cols], coefs, before[0], before[1], False)
                hre_ref[rows, cols] = hr
                him_ref[rows, cols] = hi
                return hr, hi

            cr, ci = lax.fori_loop(0, n_tiles, step, (car_re[:, cols], car_im[:, cols]))
            car_re[:, cols] = cr
            car_im[:, cols] = ci
        dv = d_ref[...]
        for j in range(N_UTILES):
            st = slice(TILE_STATES * j, TILE_STATES * (j + 1))
            yj = (jnp.dot(hre_ref[:, st].astype(BF16), cre[j], preferred_element_type=F32)
                  + jnp.dot(him_ref[:, st].astype(BF16), cimn[j], preferred_element_type=F32))
            sl = slice(LANES * j, LANES * (j + 1))
            yj = yj + dv[:, sl] * uf[:, sl]
            y_ref[:, sl] = yj
            gelu_ref[:, sl] = _gelu(yj).astype(gelu_ref.dtype)

    full3 = lambda a: pl.BlockSpec(a.shape, lambda i: (0, 0, 0))
    full2 = lambda a: pl.BlockSpec(a.shape, lambda i: (0, 0))
    tile = pltpu.VMEM((SUBLANES, N_STATES), F32)
    n_chunks = seq // t_rows
    return _pcall_riding(
        body, (u, bbd_re, bbd_im, cbd_re, cbd_imn, a_re, a_im, dskip), ride,
        lambda: pl.program_id(0) == 0, lambda: pl.program_id(0) == n_chunks - 1,
        middle=(lambda: pl.program_id(0) == (7 * n_chunks) // 8) if ride is not None else None,
        name="s5_fwd", grid=(n_chunks,),
        in_specs=[pl.BlockSpec((t_rows, D_MODEL), lambda i: (i, 0)), full3(bbd_re), full3(bbd_im), full3(cbd_re),
                  full3(cbd_imn), full2(a_re), full2(a_im), full2(dskip)],
        out_specs=[pl.BlockSpec((t_rows, D_MODEL), lambda i: (i, 0)),
                   pl.BlockSpec((t_rows, D_MODEL), lambda i: (i, 0)),
                   pl.BlockSpec((t_rows, N_STATES), lambda i: (i, 0)),
                   pl.BlockSpec((t_rows, N_STATES), lambda i: (i, 0))],
        out_shape=[jax.ShapeDtypeStruct((seq, D_MODEL), F32),
                   jax.ShapeDtypeStruct((seq, D_MODEL), BF16),
                   jax.ShapeDtypeStruct((seq, N_STATES), F32),
                   jax.ShapeDtypeStruct((seq, N_STATES), F32)],
        scratch_shapes=[tile] * 10,
        compiler_params=_params(("arbitrary",)))


def s5_bwd(dy, u, dres, h_re, h_im, bbd_re, bbd_im, cbd_re, cbd_imn, a_re, a_im, dskip, ride=None, t_rows=256):
    seq = u.shape[0]
    t_rows = min(t_rows, seq)
    n_chunks = seq // t_rows

    n_tiles = t_rows // SUBLANES

    def body(dy_ref, u_ref, dres_ref, hre_ref, him_ref, hpre_ref, hpim_ref, bre, bim, cre, cimn, are, aim, d_ref,
             dx_ref, dbre, dbim, dcre, dcimn, dar_ref, dai_ref, dd_ref, lre, lim, car_re, car_im, acc_re, acc_im,
             *shifted):
        i = pl.program_id(0)

        @pl.when(i == 0)
        def _():
            for r in (car_re, car_im, acc_re, acc_im, dbre, dbim, dcre, dcimn, dd_ref):
                r[...] = jnp.zeros_like(r)
            _scan_coefs(are, aim, shifted, reverse=True)

        dyf = dy_ref[...]
        dyb = dyf.astype(BF16)
        uf = u_ref[...]
        ub = uf.astype(BF16)
        for j in range(N_UTILES):
            dyj = dyb[:, LANES * j:LANES * (j + 1)]
            st = slice(TILE_STATES * j, TILE_STATES * (j + 1))
            lre[:, st] = lax.dot_general(dyj, cre[j], _NT, preferred_element_type=F32)
            lim[:, st] = lax.dot_general(dyj, cimn[j], _NT, preferred_element_type=F32)
        has_pred = (i < n_chunks - 1).astype(F32)
        last_row = lax.broadcasted_iota(jnp.int32, (SUBLANES, SCAN_STRIP), 0) == SUBLANES - 1
        for s in range(N_STRIPS):
            cols = pl.ds(s * SCAN_STRIP, SCAN_STRIP)
            coefs = [c[:, cols] for c in shifted]
            before_re, before_im = hpre_ref[:, cols] * has_pred, hpim_ref[:, cols] * has_pred

            def step(k, carry):
                after_re, after_im, dar, dai = carry
                t = n_tiles - 1 - k
                rows = _tile_rows(t)
                lr, li = _tile_scan(lre[rows, cols], lim[rows, cols], coefs, after_re, after_im, True)
                lre[rows, cols] = lr
                lim[rows, cols] = li
                prev = _tile_rows(jnp.maximum(t - 1, 0))
                pre_re = jnp.where(t == 0, before_re, hre_ref[prev, cols])
                pre_im = jnp.where(t == 0, before_im, him_ref[prev, cols])
                hpr = pltpu.roll(jnp.where(last_row, pre_re, hre_ref[rows, cols]), 1, 0)
                hpi = pltpu.roll(jnp.where(last_row, pre_im, him_ref[rows, cols]), 1, 0)
                return lr, li, dar + lr * hpr + li * hpi, dai + li * hpr - lr * hpi

            cr, ci, dar, dai = lax.fori_loop(0, n_tiles, step, (car_re[:, cols], car_im[:, cols],
                                                               acc_re[:, cols], acc_im[:, cols]))
            car_re[:, cols] = cr
            car_im[:, cols] = ci
            acc_re[:, cols] = dar
            acc_im[:, cols] = dai

        dv = d_ref[...]
        for j in range(N_UTILES):
            sl = slice(LANES * j, LANES * (j + 1))
            st = slice(TILE_STATES * j, TILE_STATES * (j + 1))
            lrj = lre[:, st].astype(BF16)
            lij = lim[:, st].astype(BF16)
            du = (lax.dot_general(lrj, bre[j], _NT, preferred_element_type=F32)
                  + lax.dot_general(lij, bim[j], _NT, preferred_element_type=F32))
            dx_ref[:, sl] = du + dv[:, sl] * dyf[:, sl] + DN_ALPHA * dres_ref[:, sl]
            uj = ub[:, sl]
            dbre[j] += lax.dot_general(uj, lrj, _TN, preferred_element_type=F32)
            dbim[j] += lax.dot_general(uj, lij, _TN, preferred_element_type=F32)
            dyj = dyb[:, sl]
            dcre[j] += lax.dot_general(hre_ref[:, st].astype(BF16), dyj, _TN, preferred_element_type=F32)
            dcimn[j] += lax.dot_general(him_ref[:, st].astype(BF16), dyj, _TN, preferred_element_type=F32)
        dd_ref[...] += jnp.sum(dyf * uf, axis=0, keepdims=True)

        @pl.when(i == n_chunks - 1)
        def _():
            dar_ref[...] = jnp.sum(acc_re[...], axis=0, keepdims=True)
            dai_ref[...] = jnp.sum(acc_im[...], axis=0, keepdims=True)

    rev = lambda i: (n_chunks - 1 - i, 0)
    prev_tile = lambda i: (jnp.maximum((n_chunks - 1 - i) * n_tiles - 1, 0), 0)
    once = pl.Buffered(1)
    full3 = lambda a: pl.BlockSpec(a.shape, lambda i: (0, 0, 0), pipeline_mode=once)
    full2 = lambda a: pl.BlockSpec(a.shape, lambda i: (0, 0), pipeline_mode=once)
    acc3 = lambda shape: pl.BlockSpec(shape, lambda i: (0, 0, 0))
    acc2 = lambda shape: pl.BlockSpec(shape, lambda i: (0, 0))
    tile = pltpu.VMEM((SUBLANES, N_STATES), F32)
    return _pcall_riding(
        body, (dy, u, dres, h_re, h_im, h_re, h_im, bbd_re, bbd_im, cbd_re, cbd_imn, a_re, a_im, dskip), ride,
        lambda: pl.program_id(0) == 0, lambda: pl.program_id(0) == n_chunks - 1,
        name="s5_bwd", grid=(n_chunks,),
        in_specs=[pl.BlockSpec((t_rows, D_MODEL), rev), pl.BlockSpec((t_rows, D_MODEL), rev),
                  pl.BlockSpec((t_rows, D_MODEL), rev),
                  pl.BlockSpec((t_rows, N_STATES), rev), pl.BlockSpec((t_rows, N_STATES), rev),
                  pl.BlockSpec((SUBLANES, N_STATES), prev_tile), pl.BlockSpec((SUBLANES, N_STATES), prev_tile),
                  full3(bbd_re), full3(bbd_im), full3(cbd_re), full3(cbd_imn), full2(a_re), full2(a_im), full2(dskip)],
        out_specs=[pl.BlockSpec((t_rows, D_MODEL), rev), acc3(bbd_re.shape), acc3(bbd_im.shape), acc3(cbd_re.shape),
                   acc3(cbd_imn.shape), acc2((1, N_STATES)), acc2((1, N_STATES)), acc2((1, D_MODEL))],
        out_shape=[jax.ShapeDtypeStruct((seq, D_MODEL), F32), jax.ShapeDtypeStruct(bbd_re.shape, F32),
                   jax.ShapeDtypeStruct(bbd_im.shape, F32), jax.ShapeDtypeStruct(cbd_re.shape, F32),
                   jax.ShapeDtypeStruct(cbd_imn.shape, F32), jax.ShapeDtypeStruct((1, N_STATES), F32),
                   jax.ShapeDtypeStruct((1, N_STATES), F32), jax.ShapeDtypeStruct((1, D_MODEL), F32)],
        scratch_shapes=[pltpu.VMEM((t_rows, N_STATES), F32), pltpu.VMEM((t_rows, N_STATES), F32)] + [tile] * 12,
        compiler_params=_params(("arbitrary",)))


def _eye_groups():
    return jnp.eye(GROUPS_PER_TILE, dtype=F32)


def _blockdiag_in(bb):
    t = bb.transpose(0, 2, 1).reshape(N_UTILES, GROUPS_PER_TILE, SSM_GROUP, SSM_STATE)
    bd = jnp.einsum("jgcp,gh->jgchp", t, _eye_groups())
    return bd.reshape(N_UTILES, LANES, TILE_STATES)


def _blockdiag_in_t(d):
    t = jnp.einsum("jgchp,gh->jgcp", d.reshape(N_UTILES, GROUPS_PER_TILE, SSM_GROUP, GROUPS_PER_TILE, SSM_STATE),
                   _eye_groups())
    return t.reshape(N_GROUPS, SSM_GROUP, SSM_STATE).transpose(0, 2, 1)


def _blockdiag_out(c):
    t = c.transpose(0, 2, 1).reshape(N_UTILES, GROUPS_PER_TILE, SSM_STATE, SSM_GROUP)
    bd = jnp.einsum("jhpc,hg->jhpgc", t, _eye_groups())
    return bd.reshape(N_UTILES, TILE_STATES, LANES)


def _blockdiag_out_t(d):
    t = jnp.einsum("jhpgc,hg->jhpc", d.reshape(N_UTILES, GROUPS_PER_TILE, SSM_STATE, GROUPS_PER_TILE, SSM_GROUP),
                   _eye_groups())
    return t.reshape(N_GROUPS, SSM_STATE, SSM_GROUP).transpose(0, 2, 1)


ATT_TQ = 512
ATT_TK = 512
LOG2E = math.log2(math.e)
LN2 = math.log(2.0)
Q_PRESCALE = SM_SCALE * LOG2E


def _loop_in_pairs(n, step, carry, start=0):
    pairs = (n - start) // 2

    def two(t, c):
        return step(start + 2 * t + 1, step(start + 2 * t, c))

    carry = lax.fori_loop(0, pairs, two, carry)
    return lax.fori_loop(start + 2 * pairs, n, step, carry)


def _causal(s, transposed=False):
    r = lax.broadcasted_iota(jnp.int32, s.shape, 0)
    c = lax.broadcasted_iota(jnp.int32, s.shape, 1)
    return jnp.where((r <= c) if transposed else (c <= r), s, NEG_INF)


def _q_specs(rows, at):
    def nope(*ids):
        r, h = at(*ids)
        return r, 3 * (h // HEADS_PER_CHIP) + h % HEADS_PER_CHIP

    def rope(*ids):
        r, h = at(*ids)
        return r, 3 * (h // HEADS_PER_CHIP) + HEADS_PER_CHIP

    return [pl.BlockSpec((rows, LANES), nope), pl.BlockSpec((rows, LANES), rope)]


def _kv_specs(rows, at):
    def col(f):
        def index(*ids):
            r, h = at(*ids)
            return r, f(h)
        return index

    return [pl.BlockSpec((rows, LANES), col(lambda h: 2 * h)), pl.BlockSpec((rows, LANES), col(lambda h: h % HEADS_PER_CHIP)),
            pl.BlockSpec((rows, LANES), col(lambda h: 2 * h + 1))]


def _cat(a, b):
    return jnp.concatenate([a, b], axis=1)


def attn_fwd(q, kv, kr, ride=None, tq=ATT_TQ, tk=ATT_TK):
    seq = q.shape[0]
    n_heads = N_HEADS
    tq, tk = min(tq, seq), min(tk, seq)
    assert tq == tk

    def body(qn_ref, qr_ref, kn_ref, kr_ref, v_ref, o_ref, lse_ref):
        qi = pl.program_id(1)
        qv = _cat(qn_ref[...], qr_ref[...])
        jd = qi

        def block(j, carry, diag):
            m, l, acc = carry
            rows = pl.ds(pl.multiple_of(j * tk, tk), tk)
            s = lax.dot_general(qv, _cat(kn_ref[rows, :], kr_ref[rows, :]), _NT, preferred_element_type=F32)
            if diag:
                s = _causal(s)
            m_new = jnp.maximum(m, jnp.max(s, axis=-1, keepdims=True))
            p = jnp.exp2(s - m_new)
            corr = jnp.exp2(m - m_new)
            l = l * corr + jnp.sum(p, axis=-1, keepdims=True)
            acc = acc * corr + jnp.dot(p.astype(BF16), v_ref[rows, :], preferred_element_type=F32)
            return m_new, l, acc

        init = (jnp.full((tq, 1), NEG_INF, F32), jnp.zeros((tq, 1), F32), jnp.zeros((tq, V_HEAD), F32))
        carry = _loop_in_pairs(jd, lambda j, c: block(j, c, False), init)
        m, l, acc = block(jd, carry, True)
        o_ref[...] = acc / l
        lse_ref[...] = jnp.transpose(jnp.broadcast_to(m + jnp.log2(l), (tq, LANES)))[:1, :]

    n_q = seq // tq
    return _pcall_riding(
        body, (q, q, kv, kr, kv), ride,
        lambda: (pl.program_id(0) == 0) & (pl.program_id(1) == 0),
        lambda: (pl.program_id(0) == n_heads - 1) & (pl.program_id(1) == n_q - 1),
        middle=(lambda: (pl.program_id(0) == (5 * n_heads) // 8) & (pl.program_id(1) == 0)) if ride is not None else None,
        name="attn_fwd", grid=(n_heads, n_q),
        in_specs=_q_specs(tq, lambda h, i: (i, h)) + _kv_specs(seq, lambda h, i: (0, h)),
        out_specs=[pl.BlockSpec((tq, V_HEAD), lambda h, i: (i, h)),
                   pl.BlockSpec((None, None, 1, tq), lambda h, i: (h, i, 0, 0))],
        out_shape=[jax.ShapeDtypeStruct((seq, n_heads * V_HEAD), F32),
                   jax.ShapeDtypeStruct((n_heads, n_q, 1, tq), F32)],
        compiler_params=_params(("arbitrary", "arbitrary")))


def attn_bwd(q, kv, kr, do, lse_row, delta_row, tq=ATT_TK):
    seq = q.shape[0]
    tq = min(tq, seq)
    n_blk = seq // tq

    def body(qn_ref, qr_ref, kn_ref, kr_ref, v_ref, do_ref, lse_ref, delta_ref, dqn_ref, dqr_ref, dkv_ref, dkr_ref, dq_acc):
        head, kj = pl.program_id(0), pl.program_id(1)

        @pl.when(kj == 0)
        def _():
            dq_acc[...] = jnp.zeros_like(dq_acc)

        kc = _cat(kn_ref[...], kr_ref[...])
        vv = v_ref[...]

        def block(i, carry, diag):
            dk, dv = carry
            rows = pl.ds(pl.multiple_of(i * tq, tq), tq)
            qv = _cat(qn_ref[rows, :], qr_ref[rows, :])
            st = lax.dot_general(kc, qv, _NT, preferred_element_type=F32)
            if diag:
                st = _causal(st, transposed=True)
            pt = jnp.exp2(st - lse_ref[0, pl.ds(i, 1), :])
            dob = do_ref[rows, :].astype(BF16)
            dv = dv + jnp.dot(pt.astype(BF16), dob, preferred_element_type=F32)
            dpt = lax.dot_general(vv, dob, _NT, preferred_element_type=F32)
            dst = (pt * (dpt - delta_ref[0, pl.ds(i, 1), :])).astype(BF16)
            dk = dk + jnp.dot(dst, qv, preferred_element_type=F32)
            dq_acc[rows, :] += lax.dot_general(dst, kc, _TN, preferred_element_type=F32)
            return dk, dv

        carry = block(kj, (jnp.zeros((tq, 2 * LANES), F32), jnp.zeros((tq, V_HEAD), F32)), True)
        dk, dv = _loop_in_pairs(n_blk, lambda i, c: block(i, c, False), carry, start=kj + 1)
        dk = dk * LN2
        dkv_ref[...] = _cat(dk[:, :LANES], dv).astype(dkv_ref.dtype)
        lane = lax.broadcasted_iota(jnp.int32, (tq, LANES), 1)
        mine = (lane // HALF_ROPE) % HEADS_PER_CHIP == head % HEADS_PER_CHIP
        dkr_ref[0] = jnp.where(mine, dk[:, LANES:], 0.0)

        @pl.when(kj == n_blk - 1)
        def _():
            dqn_ref[...] = dq_acc[:, :LANES] * SM_SCALE

        @pl.when((kj == n_blk - 1) & (head % HEADS_PER_CHIP == 0))
        def _():
            dqr_ref[...] = dq_acc[:, LANES:] * SM_SCALE

        @pl.when((kj == n_blk - 1) & (head % HEADS_PER_CHIP > 0))
        def _():
            dqr_ref[...] += dq_acc[:, LANES:] * SM_SCALE

    return _pcall(
        body, name="attn_bwd", grid=(N_HEADS, n_blk),
        in_specs=_q_specs(seq, lambda h, j: (0, h)) + _kv_specs(tq, lambda h, j: (j, h))
        + [pl.BlockSpec((seq, V_HEAD), lambda h, j: (0, h)),
           pl.BlockSpec((1, n_blk, tq), lambda h, j: (h, 0, 0)),
           pl.BlockSpec((1, n_blk, tq), lambda h, j: (h, 0, 0))],
        out_specs=[pl.BlockSpec((seq, LANES), lambda h, j: (0, h)),
                   pl.BlockSpec((seq, LANES), lambda h, j: (0, h // HEADS_PER_CHIP)),
                   pl.BlockSpec((tq, QK_NOPE + V_HEAD), lambda h, j: (j, h)),
                   pl.BlockSpec((1, tq, LANES), lambda h, j: (h, j, 0))],
        out_shape=[jax.ShapeDtypeStruct((seq, N_HEADS * QK_NOPE), F32),
                   jax.ShapeDtypeStruct((seq, N_CHIPS * LANES), F32),
                   jax.ShapeDtypeStruct((seq, N_HEADS * (QK_NOPE + V_HEAD)), BF16),
                   jax.ShapeDtypeStruct((N_HEADS, seq, LANES), F32)],
        scratch_shapes=[pltpu.VMEM((seq, 2 * LANES), F32)],
        compiler_params=_params(("arbitrary", "arbitrary")),
    )(q, q, kv, kr, kv, do, lse_row, delta_row)


def head_sum(x, ts=512):
    n_heads, seq, w = x.shape
    ts = min(ts, seq)

    def body(x_ref, o_ref):
        o_ref[...] = jnp.sum(x_ref[...], axis=0)

    return _pcall(body, name="head_sum", grid=(seq // ts,),
                  in_specs=[pl.BlockSpec((n_heads, ts, w), lambda i: (0, i, 0))],
                  out_specs=pl.BlockSpec((ts, w), lambda i: (i, 0)),
                  out_shape=jax.ShapeDtypeStruct((seq, w), F32),
                  compiler_params=_params(("parallel",)))(x)


HEADS_PER_CHIP = N_HEADS // N_CHIPS
Q_CHIP = HEADS_PER_CHIP * QK_DIM
Q_CHIP_NOPE = HEADS_PER_CHIP * QK_NOPE


def _perm_q_cols(w):
    t = w.reshape(w.shape[0], HEADS_PER_CHIP, QK_DIM)
    return jnp.concatenate([t[:, :, :QK_NOPE].reshape(w.shape[0], -1),
                            t[:, :, QK_NOPE:QK_NOPE + HALF_ROPE].reshape(w.shape[0], -1),
                            t[:, :, QK_NOPE + HALF_ROPE:].reshape(w.shape[0], -1)], axis=1)


def _unperm_q_cols(w):
    r = w.shape[0]
    nope = w[:, :Q_CHIP_NOPE].reshape(r, HEADS_PER_CHIP, QK_NOPE)
    r1 = w[:, Q_CHIP_NOPE:Q_CHIP_NOPE + QK_ROPE].reshape(r, HEADS_PER_CHIP, HALF_ROPE)
    r2 = w[:, Q_CHIP_NOPE + QK_ROPE:].reshape(r, HEADS_PER_CHIP, HALF_ROPE)
    return jnp.concatenate([nope, r1, r2], axis=2).reshape(r, Q_CHIP)


def _pad_kva_cols(w):
    z = jnp.zeros((w.shape[0], HALF_ROPE), w.dtype)
    return jnp.concatenate([w[:, :KV_LORA], w[:, KV_LORA:KV_LORA + HALF_ROPE], z, w[:, KV_LORA + HALF_ROPE:], z], axis=1)


def _unpad_kva_cols(w):
    return jnp.concatenate([w[:, :KV_LORA], w[:, KV_LORA:KV_LORA + HALF_ROPE],
                            w[:, KV_LORA + QK_ROPE:KV_LORA + QK_ROPE + HALF_ROPE]], axis=1)


def _rope_tile(t, cs, sn):
    return t * cs + pltpu.roll(t, LANES // 2, 1) * sn


def _rope_tile_bwd(d, cs, sn):
    return d * cs + pltpu.roll(d * sn, LANES // 2, 1)


def _b_cols(tk, tn):
    return pl.BlockSpec((None, tk, tn), lambda i, j, k: (j, k, 0))


def _b_cols_t(tk, tn):
    return pl.BlockSpec((None, tn, tk), lambda i, j, k: (k, j, 0))


def _out_cols(shape):
    return shape, lambda tm, tn: pl.BlockSpec((None, tm, tn), lambda i, j, k: (j, i, 0))


def _halves(a):
    return a.reshape(N_CHIPS, 2, a.shape[1] // 2, a.shape[2])


def device_step(x, positions, target, w, comm=None):
    seq = x.shape[0]
    w = dict(w)

    def gathered(names, outs):
        for n, a in zip(names, outs):
            if isinstance(n, tuple):
                w[n[0]] = [a.reshape(v.shape) if l == n[1] else v for l, v in enumerate(w[n[0]])]
            else:
                w[n] = a.reshape(w[n].shape)

    def ride_for(names):
        if comm is None:
            return None
        return GatherRide([_halves(w[n[0]][n[1]] if isinstance(n, tuple) else w[n]) for n in names])

    first_ride = ("ssm_w_glu", "ssm_w_out", ("w_ff1", 0), ("w_ff2", 0))
    mla_ride = ("kv_w_a", "kv_w_b", "q_w_a", "q_w_b", "attn_w_o")
    second_ride = (("w_ff1", 1), ("w_ff2", 1))

    inv_freq = ROPE_THETA ** (-jnp.arange(HALF_ROPE, dtype=F32) / HALF_ROPE)
    ang = positions.astype(F32)[:, None] * inv_freq
    cos, sin = jnp.cos(ang), jnp.sin(ang)
    zero = jnp.zeros_like(cos)
    cos_q, sin_q = jnp.concatenate([cos] * 4, 1), jnp.concatenate([-sin, -sin, sin, sin], 1)
    cos_k, sin_k = jnp.concatenate([cos, zero, cos, zero], 1), jnp.concatenate([-sin, zero, sin, zero], 1)
    ff_tile = D_FF // N_CHIPS
    pack_shape = (N_CHIPS, EARLY_ROWS, PACK_W)

    lr = w["ssm_lam_re"].reshape(N_STATES, 1)
    li = w["ssm_lam_im"].reshape(N_STATES, 1)
    ldt = jnp.repeat(w["ssm_log_dt"].reshape(N_GROUPS), SSM_STATE).reshape(N_STATES, 1)
    b_re = w["ssm_b_re"].reshape(N_STATES, SSM_GROUP)
    b_im = w["ssm_b_im"].reshape(N_STATES, SSM_GROUP)
    a_re, a_im, bb_re, bb_im = s5_prep(lr, li, ldt, b_re, b_im)
    a_re, a_im = a_re.reshape(1, N_STATES), a_im.reshape(1, N_STATES)
    bbd_re = _blockdiag_in(bb_re.reshape(N_GROUPS, SSM_STATE, SSM_GROUP)).astype(BF16)
    bbd_im = _blockdiag_in(bb_im.reshape(N_GROUPS, SSM_STATE, SSM_GROUP)).astype(BF16)
    cbd_re = _blockdiag_out(w["ssm_c_re"].reshape(N_GROUPS, SSM_GROUP, SSM_STATE)).astype(BF16)
    cbd_imn = _blockdiag_out(-w["ssm_c_im"].reshape(N_GROUPS, SSM_GROUP, SSM_STATE)).astype(BF16)
    dskip = w["ssm_d"].reshape(1, D_MODEL)
    (ypre, yg, h_re, h_im), landed = s5_fwd(x, bbd_re, bbd_im, cbd_re, cbd_imn, a_re, a_im, dskip, ride_for(first_ride))
    gathered(first_ride, landed)
    w_glu = w["ssm_w_glu"]
    glu_tile = w_glu.shape[2]
    vg = mm(yg, w_glu, n_dim=2 * D_MODEL, tiles=(None, glu_tile, None), b_view=_b_cols, name="glu_proj")

    def glu(v):
        return (v[:, :D_MODEL] * _sigmoid(v[:, D_MODEL:]),), ()
    (z,) = rowwise(glu, (vg,), ((D_MODEL, BF16),), name="glu")
    w_out = w["ssm_w_out"].reshape(D_MODEL, D_MODEL)
    ln = lambda name, l: w[name][l].reshape(1, D_MODEL)

    def then_ln(h, names, layer):
        def epi(r, hv, gl, bl):
            y = _layer_norm(hv, r, gl, bl)
            return r, y, y
        return dict(epi=epi, extras=(h, ln(names[0], layer), ln(names[1], layer)), out_dtypes=(F32, F32, BF16))

    mix0, h1, h1b = mm(z, w_out, name="ssm_out", **then_ln(x, ("ln_mix_g", "ln_mix_b"), 0))

    def mlp_fwd(h, hb, layer, riding=None, with_ln=True):
        pre = mm(hb, w["w_ff1"][layer], n_dim=D_FF, tiles=(None, ff_tile, None), b_view=_b_cols, name=f"ff1_{layer}",
                 out_dtypes=(BF16,), ride=ride_for(riding) if riding else None)
        if riding and comm is not None:
            pre, landed = pre
            gathered(riding, landed)
        post = then_ln(h, ("ln_ffn_g", "ln_ffn_b"), layer) if with_ln else {}
        return pre, mm(pre, w["w_ff2"][layer].reshape(D_FF, D_MODEL), pro_a=_relu2, name=f"ff2_{layer}", **post)

    f1pre, (f1, h2, h2b) = mlp_fwd(h1, h1b, 0, mla_ride)

    kv_w_a = w["kv_w_a"].reshape(D_MODEL, KVA_PAD)
    kv_w_b = w["kv_w_b"]
    q_w_a = w["q_w_a"].reshape(D_MODEL, Q_LORA)
    q_w_b = w["q_w_b"]
    w_o = w["attn_w_o"].reshape(D_MODEL, D_MODEL)
    kvb_tile = kv_w_b.shape[2]
    kvn_g = w["kv_norm_g"].reshape(1, KV_LORA)
    qn_g = w["q_norm_g"].reshape(1, Q_LORA)
    kva = mm(h2b, kv_w_a, name="kv_a")

    def kv_post(kva, g, cs, sn):
        tile = _rope_tile(kva[:, KV_LORA:], cs, sn)
        return (_rms(kva[:, :KV_LORA], g), _cat(tile, pltpu.roll(tile, HALF_ROPE, 1))), ()
    ckv, krope = rowwise(kv_post, (kva, kvn_g, cos_k, sin_k), ((KV_LORA, BF16), (2 * LANES, BF16)), name="kv_post")
    kvb = mm(ckv, kv_w_b, n_dim=N_CHIPS * kvb_tile, tiles=(None, kvb_tile, KV_LORA), b_view=_b_cols, name="kv_b",
             out_dtypes=(BF16,))
    cq_raw, cq = mm(h2b, q_w_a, epi=lambda r, gq: (r, _rms(r, gq)), extras=(qn_g,), out_dtypes=(F32, BF16), name="q_a")

    def rope_and_scale(r, cs, sn):
        return (_cat(r[:, :Q_CHIP_NOPE], _rope_tile(r[:, Q_CHIP_NOPE:], cs, sn)) * Q_PRESCALE,)
    qro = mm(cq, q_w_b, n_dim=N_CHIPS * Q_CHIP, tiles=(None, Q_CHIP, Q_LORA), b_view=_b_cols, epi=rope_and_scale,
             extras=(cos_q, sin_q), out_dtypes=(BF16,), name="q_b")
    (o, lse), landed = attn_fwd(qro, kvb, krope, ride_for(second_ride))
    gathered(second_ride, landed)
    mix1, h3, h3b = mm(o, w_o, name="attn_out", **then_ln(h2, ("ln_mix_g", "ln_mix_b"), 1))
    f2pre, f2 = mlp_fwd(h3, h3b, 1, with_ln=False)
    def last_ln_loss_and_back(h, mix, gl, bl, t):
        e = _layer_norm(h, mix, gl, bl) - t
        dr, dg, db = _layer_norm_bwd(h, mix, gl, e * (1.0 / D_MODEL))
        return (dr, dr), (jnp.broadcast_to(jnp.sum(e * e), (1, LANES)), dg, db)
    dr4, dr4b, loss_acc, dg_f1, db_f1 = rowwise(
        last_ln_loss_and_back, (h3, f2, ln("ln_ffn_g", 1), ln("ln_ffn_b", 1), target),
        ((D_MODEL, F32), (D_MODEL, BF16)), accs=(LANES, D_MODEL, D_MODEL), name="ln_ffn_1_loss")
    loss = loss_acc[0, 0] * (0.5 / D_MODEL)

    g = {}

    def into_rows(off, rows_per_chip, shape=pack_shape):
        def view(tm, tn):
            if tm == N_CHIPS * rows_per_chip:
                return pl.BlockSpec((N_CHIPS, rows_per_chip, tn), lambda i, j, k: (0, off // rows_per_chip, 0))
            nb = rows_per_chip // tm
            return pl.BlockSpec((None, tm, tn), lambda i, j, k: (i // nb, off // tm + i % nb, 0))
        return shape, view

    def into_cols(off):
        return pack_shape, lambda tm, tn: pl.BlockSpec((None, tm, tn), lambda i, j, k: (j, off // tm + i, 0))

    def mlp_bwd(pack, dr, drb, hb, pre, layer, swap=False):
        w2_rows = (EARLY_OFF["w_ff2"] + layer * ff_tile, ff_tile)
        w1_rows = (EARLY_OFF["w_ff1"] + layer * D_MODEL, D_MODEL)
        ready = [(w1_rows[0] + w1_rows[1], w2_rows[0] - w1_rows[0] - w1_rows[1]), (w2_rows[0] + w2_rows[1], EARLY_ROWS - w2_rows[0] - w2_rows[1])]
        dpre = mm(drb, w["w_ff2"][layer].reshape(D_FF, D_MODEL), tb=True, epi=lambda r, p: (r * 2.0 * jnp.maximum(p, 0.0),),
                  extras=(pre,), out_dtypes=(BF16,), tiles=(None, ff_tile, None), name=f"ff2_dx_{layer}",
                  ride=SwapRide(pack, ready) if swap else None)
        if swap:
            dpre, (theirs,) = dpre
        pack = mm(pre, drb, ta=True, pro_a=_relu2, name=f"ff2_dw_{layer}", tiles=(ff_tile, PACK_W, None), into=pack,
                  out_view=into_rows(w2_rows[0], ff_tile))
        pack = mm(hb, dpre, ta=True, name=f"ff1_dw_{layer}", tiles=(None, PACK_W, None), into=pack,
                  out_view=into_cols(w1_rows[0]))
        dh = mm(dpre, w["w_ff1"][layer], tb=True, epi=lambda r, d: (r + DN_ALPHA * d,), extras=(dr,), n_dim=D_MODEL,
                tiles=(None, D_MODEL, ff_tile), b_view=_b_cols_t, name=f"ff1_dx_{layer}",
                ride=SwapRide(pack, [w1_rows, w2_rows], into=theirs) if swap else None)
        return (pack, *dh) if swap else (pack, dh)

    pack, dh3 = mlp_bwd(None, dr4, dr4b, h3b, f2pre, 1)
    dr3, dr3b, dg_m1, db_m1 = ln_bwd(h2, mix1, ln("ln_mix_g", 1), dh3, "ln_mix_bwd_1")
    shard_rows = D_MODEL // N_CHIPS
    pack = mm(o, dr3b, ta=True, name="attn_out_dw", tiles=(D_MODEL, PACK_W, None), into=pack,
              out_view=into_rows(EARLY_OFF["attn_w_o"], shard_rows))
    do = mm(dr3b, w_o, tb=True, name="attn_out_dx")
    def head_dots(do, o):
        return (jnp.concatenate([jnp.sum(do[:, V_HEAD * h:V_HEAD * (h + 1)] * o[:, V_HEAD * h:V_HEAD * (h + 1)], axis=1,
                                         keepdims=True) for h in range(N_HEADS)], axis=1),), ()
    (delta,) = rowwise(head_dots, (do, o), ((N_HEADS, F32),), name="attn_delta")
    tb = min(ATT_TK, seq)
    lse_row = lse.reshape(N_HEADS, seq // tb, tb)
    delta_row = delta.T.reshape(N_HEADS, seq // tb, tb)
    dqn, dqr, dkvb, dkr = attn_bwd(qro, kvb, krope, do, lse_row, delta_row)

    def q_rope_bwd(dn, dr, cs, sn):
        parts = []
        for k in range(N_CHIPS):
            parts.append(dn[:, Q_CHIP_NOPE * k:Q_CHIP_NOPE * (k + 1)])
            parts.append(_rope_tile_bwd(dr[:, LANES * k:LANES * (k + 1)], cs, sn))
        return (jnp.concatenate(parts, axis=1),), ()
    (dqlin,) = rowwise(q_rope_bwd, (dqn, dqr, cos_q, sin_q), ((N_CHIPS * Q_CHIP, BF16),), name="q_rope_bwd")
    g["q_w_b"] = mm(cq, dqlin, ta=True, name="q_b_dw", tiles=(Q_LORA, Q_CHIP, None), out_view=_out_cols(q_w_b.shape))
    dcq = mm(dqlin, q_w_b, tb=True, n_dim=Q_LORA, tiles=(None, Q_LORA, Q_CHIP), b_view=_b_cols_t, name="q_b_dx")

    def q_norm_bwd(c, gq, d):
        dx, dgq = _rms_bwd(c, gq, d)
        return (dx,), (dgq,)
    dcq_raw, dqn_g = rowwise(q_norm_bwd, (cq_raw, qn_g, dcq), ((Q_LORA, BF16),), accs=(Q_LORA,), name="q_norm_bwd")
    g["q_w_a"] = mm(h2b, dcq_raw, ta=True, name="q_a_dw")
    g["kv_w_b"] = mm(ckv, dkvb, ta=True, name="kv_b_dw", tiles=(KV_LORA, kvb_tile, None), out_view=_out_cols(kv_w_b.shape))
    dckv = mm(dkvb, kv_w_b, tb=True, n_dim=KV_LORA, tiles=(None, KV_LORA, kvb_tile), b_view=_b_cols_t, name="kv_b_dx")
    dkr_sum = head_sum(dkr)

    def kv_post_bwd(kva, gk, dc, dk, cs, sn):
        dx, dgk = _rms_bwd(kva[:, :KV_LORA], gk, dc)
        dk = dk + pltpu.roll(dk, LANES - HALF_ROPE, 1)
        return (jnp.concatenate([dx, _rope_tile_bwd(dk, cs, sn)], axis=1),), (dgk,)
    dkva, dkvn_g = rowwise(kv_post_bwd, (kva, kvn_g, dckv, dkr_sum, cos_k, sin_k), ((KVA_PAD, BF16),),
                           accs=(KV_LORA,), name="kv_post_bwd")
    g["kv_w_a"] = mm(h2b, dkva, ta=True, name="kv_a_dw")
    dh2 = mm(dcq_raw, q_w_a, tb=True, epi=lambda r, d: (r + DN_ALPHA * d,), extras=(dr3,), name="q_a_dx")
    dh2 = mm(dkva, kv_w_a, tb=True, epi=lambda r, d: (r + d,), extras=(dh2,), name="kv_a_dx")

    dr2, dr2b, dg_f0, db_f0 = ln_bwd(h1, f1, ln("ln_ffn_g", 0), dh2, "ln_ffn_bwd_0")
    pack = put_rows(pack, packed_shards(g, MISC_EARLY, EARLY_ROWS - MISC_EARLY_OFF), MISC_EARLY_OFF)
    if comm is None:
        pack, dh1 = mlp_bwd(pack, dr2, dr2b, h1b, f1pre, 0)
    else:
        pack, dh1, (theirs,) = mlp_bwd(pack, dr2, dr2b, h1b, f1pre, 0, swap=True)
        early_sums = add_halves(pack, theirs, comm[1])
    dr1, dr1b, dg_m0, db_m0 = ln_bwd(x, mix0, ln("ln_mix_g", 0), dh1, "ln_mix_bwd_0")
    mid = mm(z, dr1b, ta=True, name="ssm_out_dw", tiles=(D_MODEL, PACK_W, None),
             out_view=into_rows(MID_OFF["ssm_w_out"], shard_rows, (N_CHIPS, MID_ROWS, PACK_W)))
    dz = mm(dr1b, w_out, tb=True, name="ssm_out_dx")

    def glu_bwd(v, dz):
        val, sg = v[:, :D_MODEL], _sigmoid(v[:, D_MODEL:])
        return (jnp.concatenate([dz * sg, dz * val * sg * (1.0 - sg)], axis=1),), ()
    (dvg,) = rowwise(glu_bwd, (vg, dz), ((2 * D_MODEL, BF16),), name="glu_bwd")
    g["ssm_w_glu"] = mm(yg, dvg, ta=True, name="glu_proj_dw", tiles=(None, glu_tile, None), out_view=_out_cols(w_glu.shape))
    mid = put_rows(mid, packed_shards(g, MISC_MID, MID_ROWS - MISC_MID_OFF), MISC_MID_OFF)
    dypre = mm(dvg, w_glu, tb=True, epi=lambda r, y: (r * _gelu_grad(y),), extras=(ypre,), n_dim=D_MODEL,
               tiles=(None, D_MODEL, glu_tile), b_view=_b_cols_t, name="glu_proj_dx",
               ride=SwapRide(mid) if comm is not None else None)
    sends = None
    if comm is not None:
        dypre, (theirs,) = dypre
        sends = SendRide([early_sums, add_halves(mid, theirs, comm[1])])
    (dx, dbbd_re, dbbd_im, dcbd_re, dcbd_imn, dar, dai, dd), got = s5_bwd(
        dypre, x, dr1, h_re, h_im, bbd_re, bbd_im, cbd_re, cbd_imn, a_re, a_im, dskip, sends)
    dbb_re = _blockdiag_in_t(dbbd_re).reshape(N_STATES, SSM_GROUP)
    dbb_im = _blockdiag_in_t(dbbd_im).reshape(N_STATES, SSM_GROUP)
    dlr, dli, dldt, db_re, db_im = s5_prep_bwd(lr, li, ldt, b_re, b_im, dar.reshape(N_STATES, 1),
                                               dai.reshape(N_STATES, 1), dbb_re, dbb_im)
    g["ssm_lam_re"] = dlr.reshape(1, N_GROUPS, SSM_STATE)
    g["ssm_lam_im"] = dli.reshape(1, N_GROUPS, SSM_STATE)
    g["ssm_log_dt"] = group_sum(dldt).reshape(1, N_GROUPS)
    g["ssm_b_re"] = db_re.reshape(1, N_GROUPS, SSM_STATE, SSM_GROUP)
    g["ssm_b_im"] = db_im.reshape(1, N_GROUPS, SSM_STATE, SSM_GROUP)
    g["ssm_c_re"] = _blockdiag_out_t(dcbd_re).reshape(1, N_GROUPS, SSM_GROUP, SSM_STATE)
    g["ssm_c_im"] = -_blockdiag_out_t(dcbd_imn).reshape(1, N_GROUPS, SSM_GROUP, SSM_STATE)
    g["ssm_d"] = dd
    g["ln_mix_g"] = jnp.concatenate([dg_m0, dg_m1], 0)
    g["ln_mix_b"] = jnp.concatenate([db_m0, db_m1], 0)
    g["ln_ffn_g"] = jnp.concatenate([dg_f0, dg_f1], 0)
    g["ln_ffn_b"] = jnp.concatenate([db_f0, db_f1], 0)
    g["kv_norm_g"] = dkvn_g.reshape(KV_LORA)
    g["q_norm_g"] = dqn_g
    return loss, dx, pack, mid, g, list(zip(sends.ins, got)) if comm is not None else None


def place(shard, me_idx, dtype, name, layer=None):
    rows, cols = shard.shape[-2:]
    tr = _tile(rows, (512, 256, 128))

    def body(m_ref, x_ref, o_ref):
        o_ref[...] = x_ref[...].astype(o_ref.dtype)

    in_spec = (pl.BlockSpec((tr, cols), lambda i, m: (i, 0)) if layer is None
               else pl.BlockSpec((None, tr, cols), lambda i, m: (layer, i, 0)))
    return _pcall(
        body, name=name,
        grid_spec=pltpu.PrefetchScalarGridSpec(
            num_scalar_prefetch=1, grid=(rows // tr,), in_specs=[in_spec],
            out_specs=pl.BlockSpec((None, tr, cols), lambda i, m: (m[0], i, 0))),
        out_shape=jax.ShapeDtypeStruct((N_CHIPS, rows, cols), dtype),
        compiler_params=_params(("parallel",)),
    )(me_idx, shard)


def place_many(shards, dtypes, me_idx, name):
    def body(m_ref, *refs):
        for x_ref, o_ref in zip(refs[:len(shards)], refs[len(shards):]):
            o_ref[...] = x_ref[...].astype(o_ref.dtype)

    return _pcall(
        body, name=name,
        grid_spec=pltpu.PrefetchScalarGridSpec(
            num_scalar_prefetch=1, grid=(1,),
            in_specs=[pl.BlockSpec(s.shape, lambda i, m: (0, 0)) for s in shards],
            out_specs=[pl.BlockSpec((None,) + s.shape, lambda i, m: (m[0], 0, 0)) for s in shards]),
        out_shape=[jax.ShapeDtypeStruct((N_CHIPS,) + s.shape, d) for s, d in zip(shards, dtypes)],
        compiler_params=_params(("arbitrary",)),
    )(me_idx, *shards)


def put_rows(pack, rows, off):
    _, n, cols = rows.shape

    def body(r_ref, p_ref, o_ref, sem):
        cp = pltpu.make_async_copy(r_ref.at[0], o_ref.at[pl.program_id(0), pl.ds(off, n), :], sem)
        cp.start()
        cp.wait()

    return _pcall(body, name="grad_put_rows", grid=(N_CHIPS,),
                  in_specs=[pl.BlockSpec((1, n, cols), lambda k: (k, 0, 0)), _ANY], out_specs=_ANY,
                  out_shape=jax.ShapeDtypeStruct(pack.shape, pack.dtype), input_output_aliases={1: 0},
                  scratch_shapes=[pltpu.SemaphoreType.DMA],
                  compiler_params=_params(("arbitrary",)))(rows, pack)


def _my_cols(c, mine=True):
    start = (c if mine else 1 - c) * HALF_W
    return pl.ds(pl.multiple_of(start, HALF_W), HALF_W)


def add_halves(gpack, got, c_idx):
    n, rows, _ = gpack.shape
    tr = min(G_BLOCK_ROWS, rows)
    blk = (None, tr, HALF_W)

    def body(c_ref, g_ref, r_ref, o_ref):
        o_ref[...] = (g_ref[...] + r_ref[...]).astype(o_ref.dtype)

    return _pcall(
        body, name="grad_add_halves",
        grid_spec=pltpu.PrefetchScalarGridSpec(
            num_scalar_prefetch=1, grid=(n, rows // tr),
            in_specs=[pl.BlockSpec(blk, lambda k, i, c: (k, i, c[0])), pl.BlockSpec(blk, lambda k, i, c: (k, i, 0))],
            out_specs=pl.BlockSpec(blk, lambda k, i, c: (k, i, 0))),
        out_shape=jax.ShapeDtypeStruct((n, rows, HALF_W), BF16),
        compiler_params=_params(("parallel", "parallel")),
    )(c_idx, gpack, got)


def sum_owner(part, got, idx, total_rows, row_off=0, into=None):
    _, rows, _ = part.shape
    tr = math.gcd(math.gcd(rows, row_off), G_BLOCK_ROWS)
    n_into = 0 if into is None else 1

    def body(m_ref, p_ref, g_ref, *rest):
        up = lambda v: v.astype(F32)
        rest[-1][...] = ((up(p_ref[...]) + up(g_ref[0])) + up(g_ref[1])) + up(g_ref[2])

    return _pcall(
        body, name="grad_sum_owner",
        grid_spec=pltpu.PrefetchScalarGridSpec(
            num_scalar_prefetch=1, grid=(rows // tr,),
            in_specs=[pl.BlockSpec((None, tr, HALF_W), lambda i, m: (m[0], i, 0)),
                      pl.BlockSpec((3, tr, HALF_W), lambda i, m: (0, i, 0))] + [_ANY] * n_into,
            out_specs=pl.BlockSpec((tr, HALF_W), lambda i, m: (row_off // tr + i, m[1]))),
        out_shape=jax.ShapeDtypeStruct((total_rows, PACK_W), F32),
        input_output_aliases={3: 0} if n_into else {},
        compiler_params=_params(("parallel",)),
    )(idx, part, got, *([into] if n_into else []))


def join_halves(red):
    def body(in_ref, out_ref, send_sem, recv_sem):
        x, y, c, _ = _place()
        sibling = (x, y, 1 - c)
        mine = out_ref.at[:, _my_cols(c)]
        cp = pltpu.make_async_remote_copy(src_ref=mine, dst_ref=mine, send_sem=send_sem, recv_sem=recv_sem,
                                          device_id=sibling, device_id_type=MESH)
        cp.start()
        cp.wait_send()
        other = out_ref.at[:, _my_cols(c, mine=False)]
        pltpu.make_async_remote_copy(src_ref=other, dst_ref=other, send_sem=send_sem, recv_sem=recv_sem,
                                     device_id=sibling, device_id_type=MESH).wait_recv()

    return _pcall(body, name="grad_join_halves", in_specs=[_ANY], out_specs=_ANY,
                  out_shape=jax.ShapeDtypeStruct(red.shape, red.dtype), input_output_aliases={0: 0},
                  scratch_shapes=[pltpu.SemaphoreType.DMA, pltpu.SemaphoreType.DMA])(red)


def adamw(gsrc, g_off, wt, m, v, name):
    n, cols = wt.shape
    tr = math.gcd(math.gcd(g_off, n), 256) if g_off else math.gcd(n, 256)
    off_blk = g_off // tr
    c1 = 1.0 / (1.0 - ADAM_B1 ** ADAM_STEP)
    c2 = 1.0 / (1.0 - ADAM_B2 ** ADAM_STEP)

    def body(g_ref, w_ref, m_ref, v_ref, go_ref, d_ref, mo_ref, vo_ref):
        gv = g_ref[...]
        mn = ADAM_B1 * m_ref[...] + (1.0 - ADAM_B1) * gv
        vn = ADAM_B2 * v_ref[...] + (1.0 - ADAM_B2) * gv * gv
        go_ref[...] = gv
        mo_ref[...] = mn
        vo_ref[...] = vn
        d_ref[...] = -ADAM_LR * ((mn * c1) / (jnp.sqrt(vn * c2) + ADAM_EPS) + ADAM_WD * w_ref[...])

    blk = pl.BlockSpec((tr, cols), lambda i: (i, 0))
    return _pcall(body, name=name, grid=(n // tr,),
                  in_specs=[pl.BlockSpec((tr, cols), lambda i: (off_blk + i, 0)), blk, blk, blk],
                  out_specs=[blk] * 4, out_shape=[jax.ShapeDtypeStruct((n, cols), F32)] * 4,
                  compiler_params=_params(("parallel",)))(gsrc, wt, m, v)


def _rows8(a):
    return -(-a.size // (8 * PACK_W)) * 8


def _as_rows(a, rows=None):
    flat = a.reshape(-1)
    n = _rows8(a) if rows is None else rows
    return jnp.pad(flat, (0, n * PACK_W - flat.shape[0])).reshape(n, PACK_W)


def local_shards_2d(wl):
    return {"w_ff1": [wl["w_ff1"][0], wl["w_ff1"][1]], "w_ff2": [wl["w_ff2"][0], wl["w_ff2"][1]],
            "ssm_w_glu": wl["ssm_w_glu"], "ssm_w_out": wl["ssm_w_out"], "kv_w_a": _pad_kva_cols(wl["kv_w_a"]),
            "kv_w_b": wl["kv_w_b"], "q_w_a": wl["q_w_a"], "q_w_b": _perm_q_cols(wl["q_w_b"]),
            "attn_w_o": wl["attn_w_o"], "ssm_d": wl["ssm_d"].reshape(2, -1)}


def misc_grad_shard(name, g, k):
    if name == "ssm_d":
        w = D_MODEL // N_CHIPS
        return g[:, w * k:w * (k + 1)]
    if name in ("ssm_w_glu", "kv_w_b"):
        return g[k]
    if name == "q_w_b":
        return _unperm_q_cols(g[k])
    rows = D_MODEL // N_CHIPS
    shard = g[rows * k:rows * (k + 1)]
    return _unpad_kva_cols(shard) if name == "kv_w_a" else shard


def packed_shards(g, names, rows, tail=None):
    blocks = []
    for k in range(N_CHIPS):
        parts = [_as_rows(misc_grad_shard(n, g[n], k), MISC_SHARD_ROWS[n]) for n in names]
        if tail is not None:
            parts.append(tail[k * (tail.shape[0] // N_CHIPS):(k + 1) * (tail.shape[0] // N_CHIPS)])
        blk = jnp.concatenate(parts, axis=0)
        blocks.append(jnp.pad(blk, ((0, rows - blk.shape[0]), (0, 0))))
    return jnp.stack(blocks)


def kernel(x, positions, ln_mix_g, ln_mix_b, ln_ffn_g, ln_ffn_b, w_ff1, w_ff2, ssm_lam_re, ssm_lam_im, ssm_log_dt, ssm_b_re, ssm_b_im, ssm_c_re, ssm_c_im, ssm_d, ssm_w_glu, ssm_w_out, kv_w_a, kv_norm_g, kv_w_b, q_w_a, q_norm_g, q_w_b, attn_w_o, loss_target, m_ln_mix_g, m_ln_mix_b, m_ln_ffn_g, m_ln_ffn_b, m_w_ff1, m_w_ff2, m_ssm_lam_re, m_ssm_lam_im, m_ssm_log_dt, m_ssm_b_re, m_ssm_b_im, m_ssm_c_re, m_ssm_c_im, m_ssm_d, m_ssm_w_glu, m_ssm_w_out, m_kv_w_a, m_kv_norm_g, m_kv_w_b, m_q_w_a, m_q_norm_g, m_q_w_b, m_attn_w_o, v_ln_mix_g, v_ln_mix_b, v_ln_ffn_g, v_ln_ffn_b, v_w_ff1, v_w_ff2, v_ssm_lam_re, v_ssm_lam_im, v_ssm_log_dt, v_ssm_b_re, v_ssm_b_im, v_ssm_c_re, v_ssm_c_im, v_ssm_d, v_ssm_w_glu, v_ssm_w_out, v_kv_w_a, v_kv_norm_g, v_kv_w_b, v_q_w_a, v_q_norm_g, v_q_w_b, v_attn_w_o):
    env = dict(locals())
    wl = {n: env[n] for n in WEIGHTS}
    ml = {n: env["m_" + n] for n in WEIGHTS}
    vl = {n: env["v_" + n] for n in WEIGHTS}
    for n in ("ssm_w_glu", "ssm_w_out", "q_w_a", "q_w_b", "attn_w_o"):
        wl[n], ml[n], vl[n] = wl[n][0], ml[n][0], vl[n][0]

    c_idx = lax.axis_index("c").astype(jnp.int32).reshape(1)
    me_idx = (2 * lax.axis_index("x") + lax.axis_index("y")).astype(jnp.int32).reshape(1)

    local = local_shards_2d(wl)
    stacked = {n: [place(wl[n], me_idx, BF16, f"place_{n}_{l}", layer=l) for l in range(DEPTH)] for n in ("w_ff1", "w_ff2")}
    others = [n for n in SHARDED if n not in stacked]
    stacked.update(zip(others, place_many([local[n] for n in others], [F32 if n == "ssm_d" else BF16 for n in others],
                                          me_idx, "place_others")))
    stacked["ssm_d"] = ride_alone(GatherRide([_halves(stacked["ssm_d"])]), "ssm_d_all_gather")[0].reshape(1, D_MODEL)
    for n in REPLICATED:
        stacked[n] = wl[n]

    loss_part, dx, early, mid, g, sent = device_step(x[0], positions[0], loss_target[0], stacked, comm=(me_idx, c_idx))
    loss = lax.psum(loss_part, ("x", "y", "c"))

    small = jnp.concatenate([_as_rows(g[n]) for n in REPLICATED], axis=0)
    small = jnp.pad(small, ((0, SMALL_ROWS - small.shape[0]), (0, 0)))
    late = packed_shards(g, MISC_LATE, LATE_ROWS, tail=small)
    late_sums = add_halves(late, ride_alone(SwapRide(late), "grad_swap_halves")[0], c_idx)
    sent.append((late_sums, ride_alone(SendRide([late_sums]), "grad_send_to_owners")[0]))
    where = jnp.concatenate([me_idx, c_idx])
    starts = (0, EARLY_ROWS, EARLY_ROWS + MID_ROWS)
    total_rows = EARLY_ROWS + MID_ROWS + LATE_ROWS
    reduced = None
    for (sums, got), off in zip(sent, starts):
        reduced = sum_owner(sums, got, where, total_rows, row_off=off, into=reduced)
    reduced = join_halves(reduced)
    quarter = reduced[starts[2] + SMALL_OFF:starts[2] + SMALL_OFF + SMALL_Q_ROWS]
    small_tot = ride_alone(GatherRide([_halves(place(quarter, me_idx, F32, "place_small_grads"))]),
                           "small_grad_all_gather")[0].reshape(SMALL_ROWS, PACK_W)

    out_g, out_d, out_m, out_v = {}, {}, {}, {}
    direct = {**EARLY_OFF, **{n: starts[1] + o for n, o in MID_OFF.items()}}
    for n, off in direct.items():
        res = adamw(reduced, off, wl[n].reshape(-1, PACK_W), ml[n].reshape(-1, PACK_W), vl[n].reshape(-1, PACK_W),
                    "adamw_" + n)
        out_g[n], out_d[n], out_m[n], out_v[n] = [a.reshape(env[n].shape) for a in res]
    for names, off in ((MISC_EARLY, MISC_EARLY_OFF), (MISC_MID, starts[1] + MISC_MID_OFF), (MISC_LATE, starts[2])):
        pack3 = lambda d: jnp.concatenate([_as_rows(d[n], MISC_SHARD_ROWS[n]) for n in names], axis=0)
        res = adamw(reduced, off, pack3(wl), pack3(ml), pack3(vl), "adamw_packed_" + names[0])
        r0 = 0
        for n in names:
            cnt = math.prod(env[n].shape)
            out_g[n], out_d[n], out_m[n], out_v[n] = [
                a[r0:r0 + MISC_SHARD_ROWS[n]].reshape(-1)[:cnt].reshape(env[n].shape) for a in res]
            r0 += MISC_SHARD_ROWS[n]
    ws = jnp.concatenate([_as_rows(wl[n]) for n in REPLICATED], axis=0)
    ms = jnp.concatenate([_as_rows(ml[n]) for n in REPLICATED], axis=0)
    vs = jnp.concatenate([_as_rows(vl[n]) for n in REPLICATED], axis=0)
    pad = ((0, SMALL_ROWS - ws.shape[0]), (0, 0))
    res = adamw(small_tot, 0, jnp.pad(ws, pad), jnp.pad(ms, pad), jnp.pad(vs, pad), "adamw_replicated")
    row = 0
    for n in REPLICATED:
        cnt = math.prod(env[n].shape)
        nrows = _rows8(env[n])
        out_g[n], out_d[n], out_m[n], out_v[n] = [a[row:row + nrows].reshape(-1)[:cnt].reshape(env[n].shape) for a in res]
        row += nrows

    return (loss, dx[None], *[out_g[n] for n in WEIGHTS], *[out_d[n] for n in WEIGHTS],
            *[out_m[n] for n in WEIGHTS], *[out_v[n] for n in WEIGHTS])
```

```python
import functools
import math

import jax
import jax.numpy as jnp
from jax import lax
from jax.experimental import pallas as pl
from jax.experimental.pallas import tpu as pltpu

F32 = jnp.float32
BF16 = jnp.bfloat16
MESH = pl.DeviceIdType.MESH

D_MODEL = 1024
DEPTH = 2
SSM_GROUP = 16
N_GROUPS = D_MODEL // SSM_GROUP
SSM_STATE = 64
N_STATES = N_GROUPS * SSM_STATE
N_HEADS = 8
QK_NOPE = 128
QK_ROPE = 64
HALF_ROPE = QK_ROPE // 2
V_HEAD = 128
QK_DIM = QK_NOPE + QK_ROPE
Q_LORA = 384
KV_LORA = 256
ROPE_THETA = 10000.0
SM_SCALE = QK_DIM ** -0.5
NEG_INF = -1e30
D_FF = 4 * D_MODEL
DN_ALPHA = (2 * DEPTH) ** 0.25
LN_EPS = 1e-5
RMS_EPS = 1e-6
ADAM_LR = 0.001
ADAM_B1 = 0.9
ADAM_B2 = 0.999
ADAM_EPS = 1e-08
ADAM_WD = 0.01
ADAM_STEP = 10

N_CHIPS = 4
LANES = 128
VMEM_LIMIT = 56 * 1024 * 1024
MM_VMEM_BUDGET = 40 * 1024 * 1024
PACK_W = 1024
KVA_PAD = 384
HALF_W = PACK_W // 2

SHARDED = ("w_ff1", "w_ff2", "ssm_w_glu", "ssm_w_out", "kv_w_a", "kv_w_b", "q_w_a", "q_w_b", "attn_w_o", "ssm_d")
G_BLOCK_ROWS = 960
EARLY_OFF = {"w_ff1": 0, "w_ff2": 2048, "attn_w_o": 4096}
MISC_EARLY = ("kv_w_b", "kv_w_a", "q_w_a", "q_w_b")
MISC_EARLY_OFF = 4352
EARLY_ROWS = 5 * G_BLOCK_ROWS
EARLY_HEAD = G_BLOCK_ROWS
MID_OFF = {"ssm_w_out": 0}
MISC_MID = ("ssm_w_glu",)
MISC_MID_OFF = 256
MID_ROWS = MISC_MID_OFF + 512
MISC_LATE = ("ssm_d",)
SMALL_Q_ROWS = 96
SMALL_ROWS = N_CHIPS * SMALL_Q_ROWS
SMALL_OFF = 16
LATE_ROWS = 192
MISC_SHARD_ROWS = {"ssm_d": 16, "ssm_w_glu": 512, "kv_w_b": 128, "kv_w_a": 80, "q_w_a": 96, "q_w_b": 144}
REPLICATED = ("ln_mix_g", "ln_mix_b", "ln_ffn_g", "ln_ffn_b", "ssm_lam_re", "ssm_lam_im", "ssm_log_dt",
              "ssm_b_re", "ssm_b_im", "ssm_c_re", "ssm_c_im", "kv_norm_g", "q_norm_g")
WEIGHTS = ("ln_mix_g", "ln_mix_b", "ln_ffn_g", "ln_ffn_b", "w_ff1", "w_ff2", "ssm_lam_re", "ssm_lam_im",
           "ssm_log_dt", "ssm_b_re", "ssm_b_im", "ssm_c_re", "ssm_c_im", "ssm_d", "ssm_w_glu", "ssm_w_out",
           "kv_w_a", "kv_norm_g", "kv_w_b", "q_w_a", "q_norm_g", "q_w_b", "attn_w_o")


def _pcall(body, **kw):
    return pl.pallas_call(body, **kw)


def _params(sem=None):
    return pltpu.CompilerParams(dimension_semantics=sem, vmem_limit_bytes=VMEM_LIMIT)


_ANY = pl.BlockSpec(memory_space=pl.ANY)


def _tile(dim, prefs):
    for p in prefs:
        if dim % p == 0:
            return p
    return dim


def _place():
    x, y, c = lax.axis_index("x"), lax.axis_index("y"), lax.axis_index("c")
    return x, y, c, [(1 - x, y), (x, 1 - y), (1 - x, 1 - y)]


def _remote(k, src, dst, to, send_sems, recv_sems):
    return pltpu.make_async_remote_copy(src_ref=src, dst_ref=dst, send_sem=send_sems.at[k], recv_sem=recv_sems.at[k],
                                        device_id=to, device_id_type=MESH)


class GatherRide:
    def __init__(self, arrs):
        self.ins = list(arrs)
        self.out_shapes = [jax.ShapeDtypeStruct(a.shape, a.dtype) for a in arrs]
        self.aliases = {i: i for i in range(len(arrs))}
        self.n_sems = 6 * len(arrs)

    def start(self, ins, outs, send_sems, recv_sems):
        x, y, c, chips = _place()
        me = 2 * x + y
        for a, o in enumerate(outs):
            for j, (px, py) in enumerate(chips):
                _remote(6 * a + j, o.at[me, c], o.at[me, c], (px, py, c), send_sems, recv_sems).start()

    def pass_on(self, ins, outs, send_sems, recv_sems):
        x, y, c, chips = _place()
        for a, o in enumerate(outs):
            for j, (px, py) in enumerate(chips):
                blk = o.at[2 * px + py, c]
                _remote(6 * a + j, blk, blk, (px, py, c), send_sems, recv_sems).wait_recv()
                _remote(6 * a + 3 + j, blk, blk, (x, y, 1 - c), send_sems, recv_sems).start()

    def finish(self, ins, outs, send_sems, recv_sems, passed_on=False):
        if not passed_on:
            self.pass_on(ins, outs, send_sems, recv_sems)
        x, y, c, chips = _place()
        me = 2 * x + y
        sibling = (x, y, 1 - c)
        for a, o in enumerate(outs):
            for j, (px, py) in enumerate(chips):
                blk = o.at[2 * px + py, 1 - c]
                _remote(6 * a + 3 + j, blk, blk, sibling, send_sems, recv_sems).wait_recv()
                _remote(6 * a + j, o.at[me, c], o.at[me, c], (px, py, c), send_sems, recv_sems).wait_send()
                mine = o.at[2 * px + py, c]
                _remote(6 * a + 3 + j, mine, mine, sibling, send_sems, recv_sems).wait_send()


class SendRide:
    base = 0

    def __init__(self, parts):
        parts = [p if isinstance(p, tuple) else (p, (0, p.shape[1]), None) for p in parts]
        self.rows = [rows for _, rows, _ in parts]
        self.n_parts = len(parts)
        self.ins = [p for p, _, _ in parts] + [into for _, _, into in parts if into is not None]
        self.out_shapes = [jax.ShapeDtypeStruct((3,) + p.shape[1:], p.dtype) for p, _, _ in parts]
        given = [a for a, (_, _, into) in enumerate(parts) if into is not None]
        self.aliases = {self.n_parts + i: a for i, a in enumerate(given)}
        self.n_sems = 3 * self.n_parts

    def _copies(self, ins, outs, send_sems, recv_sems):
        x, y, c, chips = _place()
        return [_remote(self.base + 3 * a + j, ins[a].at[2 * px + py, pl.ds(r0, n)], outs[a].at[j, pl.ds(r0, n)],
                        (px, py, c), send_sems, recv_sems)
                for a, (r0, n) in enumerate(self.rows) for j, (px, py) in enumerate(chips)]

    def start(self, ins, outs, send_sems, recv_sems):
        for cp in self._copies(ins, outs, send_sems, recv_sems):
            cp.start()

    def finish(self, ins, outs, send_sems, recv_sems):
        for cp in self._copies(ins, outs, send_sems, recv_sems):
            cp.wait()


class SwapRide:
    base = 0

    def __init__(self, pack, ranges=None, into=None):
        self.ins = [pack] if into is None else [pack, into]
        self.out_shapes = [jax.ShapeDtypeStruct(pack.shape[:2] + (HALF_W,), pack.dtype)]
        self.aliases = {} if into is None else {1: 0}
        self.ranges = ranges or [(0, pack.shape[1])]
        self.n_sems = len(self.ranges)

    def _copies(self, ins, outs, send_sems, recv_sems):
        x, y, c, _ = _place()
        return [_remote(self.base + k, ins[0].at[:, pl.ds(r0, n), _my_cols(c, mine=False)], outs[0].at[:, pl.ds(r0, n), :],
                        (x, y, 1 - c), send_sems, recv_sems) for k, (r0, n) in enumerate(self.ranges)]

    def start(self, ins, outs, send_sems, recv_sems):
        for cp in self._copies(ins, outs, send_sems, recv_sems):
            cp.start()

    def finish(self, ins, outs, send_sems, recv_sems):
        for cp in self._copies(ins, outs, send_sems, recv_sems):
            cp.wait()


class Together:
    def __init__(self, rides):
        self.rides = rides
        self.ins, self.out_shapes, self.aliases, self.n_sems = [], [], {}, 0
        for r in rides:
            r.base = self.n_sems
            self.aliases.update({len(self.ins) + i: len(self.out_shapes) + o for i, o in r.aliases.items()})
            self.ins += r.ins
            self.out_shapes += r.out_shapes
            self.n_sems += r.n_sems

    def _each(self, step, ins, outs, send_sems, recv_sems):
        i = o = 0
        for r in self.rides:
            getattr(r, step)(ins[i:i + len(r.ins)], outs[o:o + len(r.out_shapes)], send_sems, recv_sems)
            i, o = i + len(r.ins), o + len(r.out_shapes)

    def start(self, *refs):
        self._each("start", *refs)

    def finish(self, *refs):
        self._each("finish", *refs)


def _pcall_riding(body, args, ride, first, last, *, in_specs, out_specs, out_shape, scratch_shapes=(), middle=None,
                  **kw):
    n_in, n_out = len(args), len(out_shape)
    if ride is None:
        return _pcall(body, in_specs=in_specs, out_specs=out_specs, out_shape=out_shape,
                      scratch_shapes=list(scratch_shapes), **kw)(*args), []
    k_in, k_out = len(ride.ins), len(ride.out_shapes)

    def riding(*refs):
        ins, r_in = refs[:n_in], refs[n_in:n_in + k_in]
        outs = refs[n_in + k_in:n_in + k_in + n_out]
        r_out = refs[n_in + k_in + n_out:n_in + k_in + n_out + k_out]
        scratch, (send_sems, recv_sems) = refs[n_in + k_in + n_out + k_out:-2], refs[-2:]

        @pl.when(first())
        def _():
            ride.start(r_in, r_out, send_sems, recv_sems)

        if middle is not None:
            @pl.when(middle())
            def _():
                ride.pass_on(r_in, r_out, send_sems, recv_sems)

        body(*ins, *outs, *scratch)

        @pl.when(last())
        def _():
            if middle is not None:
                ride.finish(r_in, r_out, send_sems, recv_sems, passed_on=True)
            else:
                ride.finish(r_in, r_out, send_sems, recv_sems)

    res = _pcall(riding, in_specs=list(in_specs) + [_ANY] * k_in, out_specs=list(out_specs) + [_ANY] * k_out,
                 out_shape=list(out_shape) + ride.out_shapes,
                 input_output_aliases={n_in + i: n_out + o for i, o in ride.aliases.items()},
                 scratch_shapes=list(scratch_shapes) + [pltpu.SemaphoreType.DMA((ride.n_sems,))] * 2,
                 **kw)(*args, *ride.ins)
    return res[:n_out], res[n_out:]


def ride_alone(ride, name):
    def body(*refs):
        n = len(ride.ins)
        ins, outs, (send_sems, recv_sems) = refs[:n], refs[n:-2], refs[-2:]
        ride.start(ins, outs, send_sems, recv_sems)
        ride.finish(ins, outs, send_sems, recv_sems)

    return _pcall(body, name=name, in_specs=[_ANY] * len(ride.ins), out_specs=[_ANY] * len(ride.out_shapes),
                  out_shape=ride.out_shapes, input_output_aliases=dict(ride.aliases),
                  scratch_shapes=[pltpu.SemaphoreType.DMA((ride.n_sems,))] * 2)(*ride.ins)


def mm(a, b, *, name, ta=False, tb=False, pro_a=None, epi=None, extras=(), out_dtypes=(F32,), n_dim=None,
       tiles=(None, None, None), b_view=None, out_view=None, into=None, ride=None):
    if ta:
        k_dim, m_dim = a.shape
    else:
        m_dim, k_dim = a.shape
    if n_dim is None:
        n_dim = b.shape[0] if tb else b.shape[1]
    tn = tiles[1] or (n_dim if n_dim <= 1024 else _tile(n_dim, (1024, 512, 256, 128)))
    tk = tiles[2] or (k_dim if k_dim <= 1024 else _tile(k_dim, (1024, 512, 256, 128)))
    nk = k_dim // tk

    def vmem_bytes(tm):
        blocks = tm * tk * a.dtype.itemsize + tk * tn * b.dtype.itemsize
        blocks += sum(tm * (tn if e.shape[1] == n_dim else e.shape[1]) * e.dtype.itemsize for e in extras if e.shape[0] > 1)
        blocks += tm * tn * sum(jnp.dtype(d).itemsize for d in out_dtypes)
        return 2 * blocks + tm * tn * 4

    tm = tiles[0] or next((t for t in (4096, 2048, 1024, 512, 256) if m_dim % t == 0 and vmem_bytes(t) <= MM_VMEM_BUDGET),
                          _tile(m_dim, (128,)))
    assert m_dim % tm == 0 and n_dim % tn == 0 and k_dim % tk == 0, (name, m_dim, n_dim, k_dim, tm, tn, tk)
    n_ex, n_out = len(extras), len(out_dtypes)
    n_into = 0 if into is None else 1
    dims = (((0 if ta else 1,), (1 if tb else 0,)), ((), ()))

    def body(a_ref, b_ref, *rest):
        ex_refs, out_refs = rest[:n_ex], rest[n_ex + n_into:n_ex + n_into + n_out]

        def partial():
            av = a_ref[...]
            if pro_a is not None:
                av = pro_a(av)
            return lax.dot_general(av.astype(BF16), b_ref[...].astype(BF16), dims, preferred_element_type=F32)

        def finish(r):
            res = epi(r, *[e[...] for e in ex_refs]) if epi is not None else (r,)
            for o_ref, v in zip(out_refs, res):
                o_ref[...] = v.reshape(o_ref.shape).astype(o_ref.dtype)

        if nk == 1:
            finish(partial())
            return
        acc = rest[-1]
        k = pl.program_id(2)

        @pl.when(k == 0)
        def _():
            acc[...] = partial()

        @pl.when(k > 0)
        def _():
            acc[...] += partial()

        @pl.when(k == nk - 1)
        def _():
            finish(acc[...])

    def ex_spec(e):
        if e.shape == (m_dim, n_dim):
            return o_spec
        if e.shape[0] == m_dim:
            return pl.BlockSpec((tm, e.shape[1]), lambda i, j, k: (i, 0))
        return pl.BlockSpec(e.shape, lambda i, j, k: (0, 0))

    a_spec = pl.BlockSpec((tk, tm), lambda i, j, k: (k, i)) if ta else pl.BlockSpec((tm, tk), lambda i, j, k: (i, k))
    if b_view is not None:
        b_spec = b_view(tk, tn)
    else:
        b_spec = pl.BlockSpec((tn, tk), lambda i, j, k: (j, k)) if tb else pl.BlockSpec((tk, tn), lambda i, j, k: (k, j))
    o_spec = pl.BlockSpec((tm, tn), lambda i, j, k: (i, j))
    if out_view is None:
        out_specs = [o_spec] * n_out
        out_shape = [jax.ShapeDtypeStruct((m_dim, n_dim), dt) for dt in out_dtypes]
    else:
        assert n_out == 1
        out_specs = [out_view[1](tm, tn)]
        out_shape = [jax.ShapeDtypeStruct(out_view[0], out_dtypes[0])]
    grid = (m_dim // tm, n_dim // tn, nk)
    scratch = [pltpu.VMEM((tm, tn), F32)] if nk > 1 else []
    if ride is not None:
        assert into is None
        at = lambda ids: functools.reduce(jnp.logical_and, [pl.program_id(d) == i for d, i in enumerate(ids)])
        outs, landed = _pcall_riding(
            body, (a, b, *extras), ride, lambda: at((0, 0, 0)), lambda: at([g - 1 for g in grid]),
            name=name, grid=grid, in_specs=[a_spec, b_spec] + [ex_spec(e) for e in extras], out_specs=out_specs,
            out_shape=out_shape, scratch_shapes=scratch, compiler_params=_params(("arbitrary",) * 3))
        return (outs[0] if n_out == 1 else outs), landed
    outs = _pcall(
        body, name=name, grid=grid,
        in_specs=[a_spec, b_spec] + [ex_spec(e) for e in extras] + [_ANY] * n_into,
        out_specs=out_specs, out_shape=out_shape,
        input_output_aliases={2 + n_ex: 0} if n_into else {},
        scratch_shapes=scratch,
        compiler_params=_params(("parallel", "parallel", "arbitrary")),
    )(a, b, *extras, *([into] if n_into else []))
    return outs[0] if n_out == 1 else outs


def rowwise(fn, ins, outs, *, name, accs=(), tm=256):
    rows = ins[0].shape[0]
    tm = min(tm, rows)
    n_in, n_out, n_acc = len(ins), len(outs), len(accs)

    def body(*refs):
        in_refs, out_refs, acc_refs = refs[:n_in], refs[n_in:n_in + n_out], refs[n_in + n_out:]
        res, sums = fn(*[r[...] for r in in_refs])
        for o_ref, v in zip(out_refs, res):
            o_ref[...] = v.astype(o_ref.dtype)
        if n_acc:
            @pl.when(pl.program_id(0) == 0)
            def _():
                for a_ref in acc_refs:
                    a_ref[...] = jnp.zeros_like(a_ref)

            for a_ref, s in zip(acc_refs, sums):
                a_ref[...] += s

    def spec(arr):
        if arr.shape[0] == rows:
            return pl.BlockSpec((tm, arr.shape[1]), lambda i: (i, 0))
        return pl.BlockSpec(arr.shape, lambda i: (0, 0))

    res = _pcall(
        body, name=name, grid=(rows // tm,),
        in_specs=[spec(a) for a in ins],
        out_specs=[pl.BlockSpec((tm, w), lambda i: (i, 0)) for w, _ in outs]
        + [pl.BlockSpec((1, w), lambda i: (0, 0)) for w in accs],
        out_shape=[jax.ShapeDtypeStruct((rows, w), dt) for w, dt in outs]
        + [jax.ShapeDtypeStruct((1, w), F32) for w in accs],
        compiler_params=_params(("arbitrary",) if n_acc else ("parallel",)),
    )(*ins)
    return res


def _relu2(v):
    r = jnp.maximum(v, 0.0)
    return r * r


def _gelu(x):
    c = math.sqrt(2.0 / math.pi)
    return 0.5 * x * (1.0 + jnp.tanh(c * (x + 0.044715 * x * x * x)))


def _gelu_grad(x):
    c = math.sqrt(2.0 / math.pi)
    t = jnp.tanh(c * (x + 0.044715 * x * x * x))
    return 0.5 * (1.0 + t) + 0.5 * x * (1.0 - t * t) * c * (1.0 + 3 * 0.044715 * x * x)


def _sigmoid(x):
    return 1.0 / (1.0 + jnp.exp(-x))


def _layer_norm(h, mix, g, b):
    r = DN_ALPHA * h + mix
    mu = jnp.mean(r, axis=-1, keepdims=True)
    xc = r - mu
    var = jnp.mean(xc * xc, axis=-1, keepdims=True)
    return xc * lax.rsqrt(var + LN_EPS) * g + b


def _layer_norm_bwd(h, mix, g, dy):
    r = DN_ALPHA * h + mix
    mu = jnp.mean(r, axis=-1, keepdims=True)
    xc = r - mu
    var = jnp.mean(xc * xc, axis=-1, keepdims=True)
    rstd = lax.rsqrt(var + LN_EPS)
    xhat = xc * rstd
    dxh = dy * g
    m1 = jnp.mean(dxh, axis=-1, keepdims=True)
    m2 = jnp.mean(dxh * xhat, axis=-1, keepdims=True)
    dr = rstd * (dxh - m1 - xhat * m2)
    return dr, jnp.sum(dy * xhat, axis=0, keepdims=True), jnp.sum(dy, axis=0, keepdims=True)


def ln_bwd(h, mix, g, dy, name):
    def fn(h, mix, g, dy):
        dr, dg, db = _layer_norm_bwd(h, mix, g, dy)
        return (dr, dr), (dg, db)
    return rowwise(fn, (h, mix, g, dy), ((D_MODEL, F32), (D_MODEL, BF16)), accs=(D_MODEL, D_MODEL), name=name)


def _rms(x, g):
    r = lax.rsqrt(jnp.mean(x * x, axis=-1, keepdims=True) + RMS_EPS)
    return x * r * g


def _rms_bwd(x, g, dy):
    r = lax.rsqrt(jnp.mean(x * x, axis=-1, keepdims=True) + RMS_EPS)
    xn = x * r
    dyg = dy * g
    dx = r * (dyg - xn * jnp.mean(dyg * xn, axis=-1, keepdims=True))
    return dx, jnp.sum(dy * xn, axis=0, keepdims=True)


def _s5_disc(lr, li, ldt):
    dt = jnp.exp(ldt)
    mag = jnp.exp(lr * dt)
    cs, sn = jnp.cos(li * dt), jnp.sin(li * dt)
    ar, ai = mag * cs, mag * sn
    inv = 1.0 / (lr * lr + li * li)
    n_re = (ar - 1.0) * lr + ai * li
    n_im = ai * lr - (ar - 1.0) * li
    return dt, mag, cs, sn, ar, ai, inv, n_re, n_im


def s5_prep(lr, li, ldt, b_re, b_im):
    def fn(lr, li, ldt, b_re, b_im):
        _, _, _, _, ar, ai, inv, n_re, n_im = _s5_disc(lr, li, ldt)
        cr, ci = n_re * inv, n_im * inv
        return (ar, ai, cr * b_re - ci * b_im, cr * b_im + ci * b_re), ()
    return rowwise(fn, (lr, li, ldt, b_re, b_im), ((1, F32), (1, F32), (SSM_GROUP, F32), (SSM_GROUP, F32)),
                   name="s5_prep", tm=512)


def s5_prep_bwd(lr, li, ldt, b_re, b_im, dar, dai, dbb_re, dbb_im):
    def fn(lr, li, ldt, b_re, b_im, dar, dai, dbb_re, dbb_im):
        dt, mag, cs, sn, ar, ai, inv, n_re, n_im = _s5_disc(lr, li, ldt)
        cr, ci = n_re * inv, n_im * inv
        db_re = cr * dbb_re + ci * dbb_im
        db_im = cr * dbb_im - ci * dbb_re
        dcr = jnp.sum(dbb_re * b_re + dbb_im * b_im, axis=-1, keepdims=True)
        dci = jnp.sum(dbb_im * b_re - dbb_re * b_im, axis=-1, keepdims=True)
        dar = dar + (dcr * lr - dci * li) * inv
        dai = dai + (dcr * li + dci * lr) * inv
        dinv = dcr * n_re + dci * n_im
        dlr = (dcr * (ar - 1.0) + dci * ai) * inv - 2.0 * lr * inv * inv * dinv
        dli = (dcr * ai - dci * (ar - 1.0)) * inv - 2.0 * li * inv * inv * dinv
        dmag = dar * cs + dai * sn
        dth = dai * ar - dar * ai
        dlr = dlr + dmag * mag * dt
        dli = dli + dth * dt
        ddt = dmag * mag * lr + dth * li
        return (dlr, dli, ddt * dt, db_re, db_im), ()
    return rowwise(fn, (lr, li, ldt, b_re, b_im, dar, dai, dbb_re, dbb_im),
                   ((1, F32), (1, F32), (1, F32), (SSM_GROUP, F32), (SSM_GROUP, F32)), name="s5_prep_bwd", tm=512)


def group_sum(x):
    def body(x_ref, o_ref):
        o_ref[...] = jnp.sum(x_ref[...], axis=1)
    return _pcall(body, name="s5_group_sum", out_shape=jax.ShapeDtypeStruct((N_GROUPS, 1), F32))(
        x.reshape(N_GROUPS, SSM_STATE, 1))


GROUPS_PER_TILE = LANES // SSM_GROUP
TILE_STATES = GROUPS_PER_TILE * SSM_STATE
N_UTILES = D_MODEL // LANES


SUBLANES = 8
SCAN_STRIP = 1024
N_STRIPS = N_STATES // SCAN_STRIP
_NT = (((1,), (1,)), ((), ()))
_TN = (((0,), (0,)), ((), ()))


def _scan_coefs(are, aim, shifted, reverse):
    ar = are[...]
    ai = -aim[...] if reverse else aim[...]
    powers = {1: (ar, ai)}
    for d in (2, 4):
        r, i = powers[d // 2]
        powers[d] = (r * r - i * i, 2.0 * r * i)
    rid = lax.broadcasted_iota(jnp.int32, (SUBLANES, N_STATES), 0)
    first = (rid == SUBLANES - 1) if reverse else (rid == 0)
    masks = [(1, first)] + [(d, (rid <= SUBLANES - 1 - d) if reverse else (rid >= d)) for d in (1, 2, 4)]
    for n, (d, keep) in enumerate(masks):
        for part in (0, 1):
            shifted[2 * n + part][...] = jnp.where(keep, jnp.broadcast_to(powers[d][part], (SUBLANES, N_STATES)), 0.0)


def _tile_scan(xr, xi, shifted, nbr_re, nbr_im, reverse):
    for n, d in enumerate((1, 1, 2, 4)):
        by = SUBLANES - d if reverse else d
        fr, fi = (nbr_re, nbr_im) if n == 0 else (xr, xi)
        sr, si = pltpu.roll(fr, by, 0), pltpu.roll(fi, by, 0)
        kr, ki = shifted[2 * n], shifted[2 * n + 1]
        xr, xi = xr + kr * sr - ki * si, xi + kr * si + ki * sr
    return xr, xi


def _tile_rows(t):
    return pl.ds(pl.multiple_of(t * SUBLANES, SUBLANES), SUBLANES)


def s5_fwd(u, bbd_re, bbd_im, cbd_re, cbd_imn, a_re, a_im, dskip, ride=None, t_rows=256):
    seq = u.shape[0]
    t_rows = min(t_rows, seq)
    n_tiles = t_rows // SUBLANES

    def body(u_ref, bre, bim, cre, cimn, are, aim, d_ref, y_ref, gelu_ref, hre_ref, him_ref, car_re, car_im, *shifted):
        @pl.when(pl.program_id(0) == 0)
        def _():
            car_re[...] = jnp.zeros_like(car_re)
            car_im[...] = jnp.zeros_like(car_im)
            _scan_coefs(are, aim, shifted, reverse=False)

        uf = u_ref[...]
        ub = uf.astype(BF16)
        for j in range(N_UTILES):
            uj = ub[:, LANES * j:LANES * (j + 1)]
            sl = slice(TILE_STATES * j, TILE_STATES * (j + 1))
            hre_ref[:, sl] = jnp.dot(uj, bre[j], preferred_element_type=F32)
            him_ref[:, sl] = jnp.dot(uj, bim[j], preferred_element_type=F32)
        for s in range(N_STRIPS):
            cols = pl.ds(s * SCAN_STRIP, SCAN_STRIP)
            coefs = [c[:, cols] for c in shifted]

            def step(t, before):
                rows = _tile_rows(t)
                hr, hi = _tile_scan(hre_ref[rows, cols], him_ref[rows, cols], coefs, before[0], before[1], False)
                hre_ref[rows, cols] = hr
                him_ref[rows, cols] = hi
                return hr, hi

            cr, ci = lax.fori_loop(0, n_tiles, step, (car_re[:, cols], car_im[:, cols]))
            car_re[:, cols] = cr
            car_im[:, cols] = ci
        dv = d_ref[...]
        for j in range(N_UTILES):
            st = slice(TILE_STATES * j, TILE_STATES * (j + 1))
            yj = (jnp.dot(hre_ref[:, st].astype(BF16), cre[j], preferred_element_type=F32)
                  + jnp.dot(him_ref[:, st].astype(BF16), cimn[j], preferred_element_type=F32))
            sl = slice(LANES * j, LANES * (j + 1))
            yj = yj + dv[:, sl] * uf[:, sl]
            y_ref[:, sl] = yj
            gelu_ref[:, sl] = _gelu(yj).astype(gelu_ref.dtype)

    full3 = lambda a: pl.BlockSpec(a.shape, lambda i: (0, 0, 0))
    full2 = lambda a: pl.BlockSpec(a.shape, lambda i: (0, 0))
    tile = pltpu.VMEM((SUBLANES, N_STATES), F32)
    n_chunks = seq // t_rows
    return _pcall_riding(
        body, (u, bbd_re, bbd_im, cbd_re, cbd_imn, a_re, a_im, dskip), ride,
        lambda: pl.program_id(0) == 0, lambda: pl.program_id(0) == n_chunks - 1,
        middle=(lambda: pl.program_id(0) == (7 * n_chunks) // 8) if ride is not None else None,
        name="s5_fwd", grid=(n_chunks,),
        in_specs=[pl.BlockSpec((t_rows, D_MODEL), lambda i: (i, 0)), full3(bbd_re), full3(bbd_im), full3(cbd_re),
                  full3(cbd_imn), full2(a_re), full2(a_im), full2(dskip)],
        out_specs=[pl.BlockSpec((t_rows, D_MODEL), lambda i: (i, 0)),
                   pl.BlockSpec((t_rows, D_MODEL), lambda i: (i, 0)),
                   pl.BlockSpec((t_rows, N_STATES), lambda i: (i, 0)),
                   pl.BlockSpec((t_rows, N_STATES), lambda i: (i, 0))],
        out_shape=[jax.ShapeDtypeStruct((seq, D_MODEL), F32),
                   jax.ShapeDtypeStruct((seq, D_MODEL), BF16),
                   jax.ShapeDtypeStruct((seq, N_STATES), F32),
                   jax.ShapeDtypeStruct((seq, N_STATES), F32)],
        scratch_shapes=[tile] * 10,
        compiler_params=_params(("arbitrary",)))


def s5_bwd(dy, u, dres, h_re, h_im, bbd_re, bbd_im, cbd_re, cbd_imn, a_re, a_im, dskip, ride=None, t_rows=256):
    seq = u.shape[0]
    t_rows = min(t_rows, seq)
    n_chunks = seq // t_rows

    n_tiles = t_rows // SUBLANES

    def body(dy_ref, u_ref, dres_ref, hre_ref, him_ref, hpre_ref, hpim_ref, bre, bim, cre, cimn, are, aim, d_ref,
             dx_ref, dbre, dbim, dcre, dcimn, dar_ref, dai_ref, dd_ref, lre, lim, car_re, car_im, acc_re, acc_im,
             *shifted):
        i = pl.program_id(0)

        @pl.when(i == 0)
        def _():
            for r in (car_re, car_im, acc_re, acc_im, dbre, dbim, dcre, dcimn, dd_ref):
                r[...] = jnp.zeros_like(r)
            _scan_coefs(are, aim, shifted, reverse=True)

        dyf = dy_ref[...]
        dyb = dyf.astype(BF16)
        uf = u_ref[...]
        ub = uf.astype(BF16)
        for j in range(N_UTILES):
            dyj = dyb[:, LANES * j:LANES * (j + 1)]
            st = slice(TILE_STATES * j, TILE_STATES * (j + 1))
            lre[:, st] = lax.dot_general(dyj, cre[j], _NT, preferred_element_type=F32)
            lim[:, st] = lax.dot_general(dyj, cimn[j], _NT, preferred_element_type=F32)
        has_pred = (i < n_chunks - 1).astype(F32)
        last_row = lax.broadcasted_iota(jnp.int32, (SUBLANES, SCAN_STRIP), 0) == SUBLANES - 1
        for s in range(N_STRIPS):
            cols = pl.ds(s * SCAN_STRIP, SCAN_STRIP)
            coefs = [c[:, cols] for c in shifted]
            before_re, before_im = hpre_ref[:, cols] * has_pred, hpim_ref[:, cols] * has_pred

            def step(k, carry):
                after_re, after_im, dar, dai = carry
                t = n_tiles - 1 - k
                rows = _tile_rows(t)
                lr, li = _tile_scan(lre[rows, cols], lim[rows, cols], coefs, after_re, after_im, True)
                lre[rows, cols] = lr
                lim[rows, cols] = li
                prev = _tile_rows(jnp.maximum(t - 1, 0))
                pre_re = jnp.where(t == 0, before_re, hre_ref[prev, cols])
                pre_im = jnp.where(t == 0, before_im, him_ref[prev, cols])
                hpr = pltpu.roll(jnp.where(last_row, pre_re, hre_ref[rows, cols]), 1, 0)
                hpi = pltpu.roll(jnp.where(last_row, pre_im, him_ref[rows, cols]), 1, 0)
                return lr, li, dar + lr * hpr + li * hpi, dai + li * hpr - lr * hpi

            cr, ci, dar, dai = lax.fori_loop(0, n_tiles, step, (car_re[:, cols], car_im[:, cols],
                                                               acc_re[:, cols], acc_im[:, cols]))
            car_re[:, cols] = cr
            car_im[:, cols] = ci
            acc_re[:, cols] = dar
            acc_im[:, cols] = dai

        dv = d_ref[...]
        for j in range(N_UTILES):
            sl = slice(LANES * j, LANES * (j + 1))
            st = slice(TILE_STATES * j, TILE_STATES * (j + 1))
            lrj = lre[:, st].astype(BF16)
            lij = lim[:, st].astype(BF16)
            du = (lax.dot_general(lrj, bre[j], _NT, preferred_element_type=F32)
                  + lax.dot_general(lij, bim[j], _NT, preferred_element_type=F32))
            dx_ref[:, sl] = du + dv[:, sl] * dyf[:, sl] + DN_ALPHA * dres_ref[:, sl]
            uj = ub[:, sl]
            dbre[j] += lax.dot_general(uj, lrj, _TN, preferred_element_type=F32)
            dbim[j] += lax.dot_general(uj, lij, _TN, preferred_element_type=F32)
            dyj = dyb[:, sl]
            dcre[j] += lax.dot_general(hre_ref[:, st].astype(BF16), dyj, _TN, preferred_element_type=F32)
            dcimn[j] += lax.dot_general(him_ref[:, st].astype(BF16), dyj, _TN, preferred_element_type=F32)
        dd_ref[...] += jnp.sum(dyf * uf, axis=0, keepdims=True)

        @pl.when(i == n_chunks - 1)
        def _():
            dar_ref[...] = jnp.sum(acc_re[...], axis=0, keepdims=True)
            dai_ref[...] = jnp.sum(acc_im[...], axis=0, keepdims=True)

    rev = lambda i: (n_chunks - 1 - i, 0)
    prev_tile = lambda i: (jnp.maximum((n_chunks - 1 - i) * n_tiles - 1, 0), 0)
    once = pl.Buffered(1)
    full3 = lambda a: pl.BlockSpec(a.shape, lambda i: (0, 0, 0), pipeline_mode=once)
    full2 = lambda a: pl.BlockSpec(a.shape, lambda i: (0, 0), pipeline_mode=once)
    acc3 = lambda shape: pl.BlockSpec(shape, lambda i: (0, 0, 0))
    acc2 = lambda shape: pl.BlockSpec(shape, lambda i: (0, 0))
    tile = pltpu.VMEM((SUBLANES, N_STATES), F32)
    return _pcall_riding(
        body, (dy, u, dres, h_re, h_im, h_re, h_im, bbd_re, bbd_im, cbd_re, cbd_imn, a_re, a_im, dskip), ride,
        lambda: pl.program_id(0) == 0, lambda: pl.program_id(0) == n_chunks - 1,
        name="s5_bwd", grid=(n_chunks,),
        in_specs=[pl.BlockSpec((t_rows, D_MODEL), rev), pl.BlockSpec((t_rows, D_MODEL), rev),
                  pl.BlockSpec((t_rows, D_MODEL), rev),
                  pl.BlockSpec((t_rows, N_STATES), rev), pl.BlockSpec((t_rows, N_STATES), rev),
                  pl.BlockSpec((SUBLANES, N_STATES), prev_tile), pl.BlockSpec((SUBLANES, N_STATES), prev_tile),
                  full3(bbd_re), full3(bbd_im), full3(cbd_re), full3(cbd_imn), full2(a_re), full2(a_im), full2(dskip)],
        out_specs=[pl.BlockSpec((t_rows, D_MODEL), rev), acc3(bbd_re.shape), acc3(bbd_im.shape), acc3(cbd_re.shape),
                   acc3(cbd_imn.shape), acc2((1, N_STATES)), acc2((1, N_STATES)), acc2((1, D_MODEL))],
        out_shape=[jax.ShapeDtypeStruct((seq, D_MODEL), F32), jax.ShapeDtypeStruct(bbd_re.shape, F32),
                   jax.ShapeDtypeStruct(bbd_im.shape, F32), jax.ShapeDtypeStruct(cbd_re.shape, F32),
                   jax.ShapeDtypeStruct(cbd_imn.shape, F32), jax.ShapeDtypeStruct((1, N_STATES), F32),
                   jax.ShapeDtypeStruct((1, N_STATES), F32), jax.ShapeDtypeStruct((1, D_MODEL), F32)],
        scratch_shapes=[pltpu.VMEM((t_rows, N_STATES), F32), pltpu.VMEM((t_rows, N_STATES), F32)] + [tile] * 12,
        compiler_params=_params(("arbitrary",)))


def _eye_groups():
    return jnp.eye(GROUPS_PER_TILE, dtype=F32)


def _blockdiag_in(bb):
    t = bb.transpose(0, 2, 1).reshape(N_UTILES, GROUPS_PER_TILE, SSM_GROUP, SSM_STATE)
    bd = jnp.einsum("jgcp,gh->jgchp", t, _eye_groups())
    return bd.reshape(N_UTILES, LANES, TILE_STATES)


def _blockdiag_in_t(d):
    t = jnp.einsum("jgchp,gh->jgcp", d.reshape(N_UTILES, GROUPS_PER_TILE, SSM_GROUP, GROUPS_PER_TILE, SSM_STATE),
                   _eye_groups())
    return t.reshape(N_GROUPS, SSM_GROUP, SSM_STATE).transpose(0, 2, 1)


def _blockdiag_out(c):
    t = c.transpose(0, 2, 1).reshape(N_UTILES, GROUPS_PER_TILE, SSM_STATE, SSM_GROUP)
    bd = jnp.einsum("jhpc,hg->jhpgc", t, _eye_groups())
    return bd.reshape(N_UTILES, TILE_STATES, LANES)


def _blockdiag_out_t(d):
    t = jnp.einsum("jhpgc,hg->jhpc", d.reshape(N_UTILES, GROUPS_PER_TILE, SSM_STATE, GROUPS_PER_TILE, SSM_GROUP),
                   _eye_groups())
    return t.reshape(N_GROUPS, SSM_STATE, SSM_GROUP).transpose(0, 2, 1)


ATT_TQ = 512
ATT_TK = 512
LOG2E = math.log2(math.e)
LN2 = math.log(2.0)
Q_PRESCALE = SM_SCALE * LOG2E


def _loop_in_pairs(n, step, carry, start=0):
    pairs = (n - start) // 2

    def two(t, c):
        return step(start + 2 * t + 1, step(start + 2 * t, c))

    carry = lax.fori_loop(0, pairs, two, carry)
    return lax.fori_loop(start + 2 * pairs, n, step, carry)


def _causal(s, transposed=False):
    r = lax.broadcasted_iota(jnp.int32, s.shape, 0)
    c = lax.broadcasted_iota(jnp.int32, s.shape, 1)
    return jnp.where((r <= c) if transposed else (c <= r), s, NEG_INF)


def _q_specs(rows, at):
    def nope(*ids):
        r, h = at(*ids)
        return r, 3 * (h // HEADS_PER_CHIP) + h % HEADS_PER_CHIP

    def rope(*ids):
        r, h = at(*ids)
        return r, 3 * (h // HEADS_PER_CHIP) + HEADS_PER_CHIP

    return [pl.BlockSpec((rows, LANES), nope), pl.BlockSpec((rows, LANES), rope)]


def _kv_specs(rows, at):
    def col(f):
        def index(*ids):
            r, h = at(*ids)
            return r, f(h)
        return index

    return [pl.BlockSpec((rows, LANES), col(lambda h: 2 * h)), pl.BlockSpec((rows, LANES), col(lambda h: h % HEADS_PER_CHIP)),
            pl.BlockSpec((rows, LANES), col(lambda h: 2 * h + 1))]


def _cat(a, b):
    return jnp.concatenate([a, b], axis=1)


def attn_fwd(q, kv, kr, ride=None, tq=ATT_TQ, tk=ATT_TK):
    seq = q.shape[0]
    n_heads = N_HEADS
    tq, tk = min(tq, seq), min(tk, seq)
    assert tq == tk

    def body(qn_ref, qr_ref, kn_ref, kr_ref, v_ref, o_ref, lse_ref):
        qi = pl.program_id(1)
        qv = _cat(qn_ref[...], qr_ref[...])
        jd = qi

        def block(j, carry, diag):
            m, l, acc = carry
            rows = pl.ds(pl.multiple_of(j * tk, tk), tk)
            s = lax.dot_general(qv, _cat(kn_ref[rows, :], kr_ref[rows, :]), _NT, preferred_element_type=F32)
            if diag:
                s = _causal(s)
            m_new = jnp.maximum(m, jnp.max(s, axis=-1, keepdims=True))
            p = jnp.exp2(s - m_new)
            corr = jnp.exp2(m - m_new)
            l = l * corr + jnp.sum(p, axis=-1, keepdims=True)
            acc = acc * corr + jnp.dot(p.astype(BF16), v_ref[rows, :], preferred_element_type=F32)
            return m_new, l, acc

        init = (jnp.full((tq, 1), NEG_INF, F32), jnp.zeros((tq, 1), F32), jnp.zeros((tq, V_HEAD), F32))
        carry = _loop_in_pairs(jd, lambda j, c: block(j, c, False), init)
        m, l, acc = block(jd, carry, True)
        o_ref[...] = acc / l
        lse_ref[...] = jnp.transpose(jnp.broadcast_to(m + jnp.log2(l), (tq, LANES)))[:1, :]

    n_q = seq // tq
    return _pcall_riding(
        body, (q, q, kv, kr, kv), ride,
        lambda: (pl.program_id(0) == 0) & (pl.program_id(1) == 0),
        lambda: (pl.program_id(0) == n_heads - 1) & (pl.program_id(1) == n_q - 1),
        middle=(lambda: (pl.program_id(0) == (5 * n_heads) // 8) & (pl.program_id(1) == 0)) if ride is not None else None,
        name="attn_fwd", grid=(n_heads, n_q),
        in_specs=_q_specs(tq, lambda h, i: (i, h)) + _kv_specs(seq, lambda h, i: (0, h)),
        out_specs=[pl.BlockSpec((tq, V_HEAD), lambda h, i: (i, h)),
                   pl.BlockSpec((None, None, 1, tq), lambda h, i: (h, i, 0, 0))],
        out_shape=[jax.ShapeDtypeStruct((seq, n_heads * V_HEAD), F32),
                   jax.ShapeDtypeStruct((n_heads, n_q, 1, tq), F32)],
        compiler_params=_params(("arbitrary", "arbitrary")))


def attn_bwd(q, kv, kr, do, lse_row, delta_row, tq=ATT_TK):
    seq = q.shape[0]
    tq = min(tq, seq)
    n_blk = seq // tq

    def body(qn_ref, qr_ref, kn_ref, kr_ref, v_ref, do_ref, lse_ref, delta_ref, dqn_ref, dqr_ref, dkv_ref, dkr_ref, dq_acc):
        head, kj = pl.program_id(0), pl.program_id(1)

        @pl.when(kj == 0)
        def _():
            dq_acc[...] = jnp.zeros_like(dq_acc)

        kc = _cat(kn_ref[...], kr_ref[...])
        vv = v_ref[...]

        def block(i, carry, diag):
            dk, dv = carry
            rows = pl.ds(pl.multiple_of(i * tq, tq), tq)
            qv = _cat(qn_ref[rows, :], qr_ref[rows, :])
            st = lax.dot_general(kc, qv, _NT, preferred_element_type=F32)
            if diag:
                st = _causal(st, transposed=True)
            pt = jnp.exp2(st - lse_ref[0, pl.ds(i, 1), :])
            dob = do_ref[rows, :].astype(BF16)
            dv = dv + jnp.dot(pt.astype(BF16), dob, preferred_element_type=F32)
            dpt = lax.dot_general(vv, dob, _NT, preferred_element_type=F32)
            dst = (pt * (dpt - delta_ref[0, pl.ds(i, 1), :])).astype(BF16)
            dk = dk + jnp.dot(dst, qv, preferred_element_type=F32)
            dq_acc[rows, :] += lax.dot_general(dst, kc, _TN, preferred_element_type=F32)
            return dk, dv

        carry = block(kj, (jnp.zeros((tq, 2 * LANES), F32), jnp.zeros((tq, V_HEAD), F32)), True)
        dk, dv = _loop_in_pairs(n_blk, lambda i, c: block(i, c, False), carry, start=kj + 1)
        dk = dk * LN2
        dkv_ref[...] = _cat(dk[:, :LANES], dv).astype(dkv_ref.dtype)
        lane = lax.broadcasted_iota(jnp.int32, (tq, LANES), 1)
        mine = (lane // HALF_ROPE) % HEADS_PER_CHIP == head % HEADS_PER_CHIP
        dkr_ref[0] = jnp.where(mine, dk[:, LANES:], 0.0)

        @pl.when(kj == n_blk - 1)
        def _():
            dqn_ref[...] = dq_acc[:, :LANES] * SM_SCALE

        @pl.when((kj == n_blk - 1) & (head % HEADS_PER_CHIP == 0))
        def _():
            dqr_ref[...] = dq_acc[:, LANES:] * SM_SCALE

        @pl.when((kj == n_blk - 1) & (head % HEADS_PER_CHIP > 0))
        def _():
            dqr_ref[...] += dq_acc[:, LANES:] * SM_SCALE

    return _pcall(
        body, name="attn_bwd", grid=(N_HEADS, n_blk),
        in_specs=_q_specs(seq, lambda h, j: (0, h)) + _kv_specs(tq, lambda h, j: (j, h))
        + [pl.BlockSpec((seq, V_HEAD), lambda h, j: (0, h)),
           pl.BlockSpec((1, n_blk, tq), lambda h, j: (h, 0, 0)),
           pl.BlockSpec((1, n_blk, tq), lambda h, j: (h, 0, 0))],
        out_specs=[pl.BlockSpec((seq, LANES), lambda h, j: (0, h)),
                   pl.BlockSpec((seq, LANES), lambda h, j: (0, h // HEADS_PER_CHIP)),
                   pl.BlockSpec((tq, QK_NOPE + V_HEAD), lambda h, j: (j, h)),
                   pl.BlockSpec((1, tq, LANES), lambda h, j: (h, j, 0))],
        out_shape=[jax.ShapeDtypeStruct((seq, N_HEADS * QK_NOPE), F32),
                   jax.ShapeDtypeStruct((seq, N_CHIPS * LANES), F32),
                   jax.ShapeDtypeStruct((seq, N_HEADS * (QK_NOPE + V_HEAD)), BF16),
                   jax.ShapeDtypeStruct((N_HEADS, seq, LANES), F32)],
        scratch_shapes=[pltpu.VMEM((seq, 2 * LANES), F32)],
        compiler_params=_params(("arbitrary", "arbitrary")),
    )(q, q, kv, kr, kv, do, lse_row, delta_row)


def head_sum(x, ts=512):
    n_heads, seq, w = x.shape
    ts = min(ts, seq)

    def body(x_ref, o_ref):
        o_ref[...] = jnp.sum(x_ref[...], axis=0)

    return _pcall(body, name="head_sum", grid=(seq // ts,),
                  in_specs=[pl.BlockSpec((n_heads, ts, w), lambda i: (0, i, 0))],
                  out_specs=pl.BlockSpec((ts, w), lambda i: (i, 0)),
                  out_shape=jax.ShapeDtypeStruct((seq, w), F32),
                  compiler_params=_params(("parallel",)))(x)


HEADS_PER_CHIP = N_HEADS // N_CHIPS
Q_CHIP = HEADS_PER_CHIP * QK_DIM
Q_CHIP_NOPE = HEADS_PER_CHIP * QK_NOPE


def _perm_q_cols(w):
    t = w.reshape(w.shape[0], HEADS_PER_CHIP, QK_DIM)
    return jnp.concatenate([t[:, :, :QK_NOPE].reshape(w.shape[0], -1),
                            t[:, :, QK_NOPE:QK_NOPE + HALF_ROPE].reshape(w.shape[0], -1),
                            t[:, :, QK_NOPE + HALF_ROPE:].reshape(w.shape[0], -1)], axis=1)


def _unperm_q_cols(w):
    r = w.shape[0]
    nope = w[:, :Q_CHIP_NOPE].reshape(r, HEADS_PER_CHIP, QK_NOPE)
    r1 = w[:, Q_CHIP_NOPE:Q_CHIP_NOPE + QK_ROPE].reshape(r, HEADS_PER_CHIP, HALF_ROPE)
    r2 = w[:, Q_CHIP_NOPE + QK_ROPE:].reshape(r, HEADS_PER_CHIP, HALF_ROPE)
    return jnp.concatenate([nope, r1, r2], axis=2).reshape(r, Q_CHIP)


def _pad_kva_cols(w):
    z = jnp.zeros((w.shape[0], HALF_ROPE), w.dtype)
    return jnp.concatenate([w[:, :KV_LORA], w[:, KV_LORA:KV_LORA + HALF_ROPE], z, w[:, KV_LORA + HALF_ROPE:], z], axis=1)


def _unpad_kva_cols(w):
    return jnp.concatenate([w[:, :KV_LORA], w[:, KV_LORA:KV_LORA + HALF_ROPE],
                            w[:, KV_LORA + QK_ROPE:KV_LORA + QK_ROPE + HALF_ROPE]], axis=1)


def _rope_tile(t, cs, sn):
    return t * cs + pltpu.roll(t, LANES // 2, 1) * sn


def _rope_tile_bwd(d, cs, sn):
    return d * cs + pltpu.roll(d * sn, LANES // 2, 1)


def _b_cols(tk, tn):
    return pl.BlockSpec((None, tk, tn), lambda i, j, k: (j, k, 0))


def _b_cols_t(tk, tn):
    return pl.BlockSpec((None, tn, tk), lambda i, j, k: (k, j, 0))


def _out_cols(shape):
    return shape, lambda tm, tn: pl.BlockSpec((None, tm, tn), lambda i, j, k: (j, i, 0))


def _halves(a):
    return a.reshape(N_CHIPS, 2, a.shape[1] // 2, a.shape[2])


def device_step(x, positions, target, w, comm=None):
    seq = x.shape[0]
    w = dict(w)

    def gathered(names, outs):
        for n, a in zip(names, outs):
            if isinstance(n, tuple):
                w[n[0]] = [a.reshape(v.shape) if l == n[1] else v for l, v in enumerate(w[n[0]])]
            else:
                w[n] = a.reshape(w[n].shape)

    def ride_for(names):
        if comm is None:
            return None
        return GatherRide([_halves(w[n[0]][n[1]] if isinstance(n, tuple) else w[n]) for n in names])

    first_ride = ("ssm_w_glu", "ssm_w_out", ("w_ff1", 0), ("w_ff2", 0))
    mla_ride = ("kv_w_a", "kv_w_b", "q_w_a", "q_w_b", "attn_w_o")
    second_ride = (("w_ff1", 1), ("w_ff2", 1))

    inv_freq = ROPE_THETA ** (-jnp.arange(HALF_ROPE, dtype=F32) / HALF_ROPE)
    ang = positions.astype(F32)[:, None] * inv_freq
    cos, sin = jnp.cos(ang), jnp.sin(ang)
    zero = jnp.zeros_like(cos)
    cos_q, sin_q = jnp.concatenate([cos] * 4, 1), jnp.concatenate([-sin, -sin, sin, sin], 1)
    cos_k, sin_k = jnp.concatenate([cos, zero, cos, zero], 1), jnp.concatenate([-sin, zero, sin, zero], 1)
    ff_tile = D_FF // N_CHIPS
    pack_shape = (N_CHIPS, EARLY_ROWS, PACK_W)

    lr = w["ssm_lam_re"].reshape(N_STATES, 1)
    li = w["ssm_lam_im"].reshape(N_STATES, 1)
    ldt = jnp.repeat(w["ssm_log_dt"].reshape(N_GROUPS), SSM_STATE).reshape(N_STATES, 1)
    b_re = w["ssm_b_re"].reshape(N_STATES, SSM_GROUP)
    b_im = w["ssm_b_im"].reshape(N_STATES, SSM_GROUP)
    a_re, a_im, bb_re, bb_im = s5_prep(lr, li, ldt, b_re, b_im)
    a_re, a_im = a_re.reshape(1, N_STATES), a_im.reshape(1, N_STATES)
    bbd_re = _blockdiag_in(bb_re.reshape(N_GROUPS, SSM_STATE, SSM_GROUP)).astype(BF16)
    bbd_im = _blockdiag_in(bb_im.reshape(N_GROUPS, SSM_STATE, SSM_GROUP)).astype(BF16)
    cbd_re = _blockdiag_out(w["ssm_c_re"].reshape(N_GROUPS, SSM_GROUP, SSM_STATE)).astype(BF16)
    cbd_imn = _blockdiag_out(-w["ssm_c_im"].reshape(N_GROUPS, SSM_GROUP, SSM_STATE)).astype(BF16)
    dskip = w["ssm_d"].reshape(1, D_MODEL)
    (ypre, yg, h_re, h_im), landed = s5_fwd(x, bbd_re, bbd_im, cbd_re, cbd_imn, a_re, a_im, dskip, ride_for(first_ride))
    gathered(first_ride, landed)
    w_glu = w["ssm_w_glu"]
    glu_tile = w_glu.shape[2]
    vg = mm(yg, w_glu, n_dim=2 * D_MODEL, tiles=(None, glu_tile, None), b_view=_b_cols, name="glu_proj")

    def glu(v):
        return (v[:, :D_MODEL] * _sigmoid(v[:, D_MODEL:]),), ()
    (z,) = rowwise(glu, (vg,), ((D_MODEL, BF16),), name="glu")
    w_out = w["ssm_w_out"].reshape(D_MODEL, D_MODEL)
    ln = lambda name, l: w[name][l].reshape(1, D_MODEL)

    def then_ln(h, names, layer):
        def epi(r, hv, gl, bl):
            y = _layer_norm(hv, r, gl, bl)
            return r, y, y
        return dict(epi=epi, extras=(h, ln(names[0], layer), ln(names[1], layer)), out_dtypes=(F32, F32, BF16))

    mix0, h1, h1b = mm(z, w_out, name="ssm_out", **then_ln(x, ("ln_mix_g", "ln_mix_b"), 0))

    def mlp_fwd(h, hb, layer, riding=None, with_ln=True):
        pre = mm(hb, w["w_ff1"][layer], n_dim=D_FF, tiles=(None, ff_tile, None), b_view=_b_cols, name=f"ff1_{layer}",
                 out_dtypes=(BF16,), ride=ride_for(riding) if riding else None)
        if riding and comm is not None:
            pre, landed = pre
            gathered(riding, landed)
        post = then_ln(h, ("ln_ffn_g", "ln_ffn_b"), layer) if with_ln else {}
        return pre, mm(pre, w["w_ff2"][layer].reshape(D_FF, D_MODEL), pro_a=_relu2, name=f"ff2_{layer}", **post)

    f1pre, (f1, h2, h2b) = mlp_fwd(h1, h1b, 0, mla_ride)

    kv_w_a = w["kv_w_a"].reshape(D_MODEL, KVA_PAD)
    kv_w_b = w["kv_w_b"]
    q_w_a = w["q_w_a"].reshape(D_MODEL, Q_LORA)
    q_w_b = w["q_w_b"]
    w_o = w["attn_w_o"].reshape(D_MODEL, D_MODEL)
    kvb_tile = kv_w_b.shape[2]
    kvn_g = w["kv_norm_g"].reshape(1, KV_LORA)
    qn_g = w["q_norm_g"].reshape(1, Q_LORA)
    kva = mm(h2b, kv_w_a, name="kv_a")

    def kv_post(kva, g, cs, sn):
        tile = _rope_tile(kva[:, KV_LORA:], cs, sn)
        return (_rms(kva[:, :KV_LORA], g), _cat(tile, pltpu.roll(tile, HALF_ROPE, 1))), ()
    ckv, krope = rowwise(kv_post, (kva, kvn_g, cos_k, sin_k), ((KV_LORA, BF16), (2 * LANES, BF16)), name="kv_post")
    kvb = mm(ckv, kv_w_b, n_dim=N_CHIPS * kvb_tile, tiles=(None, kvb_tile, KV_LORA), b_view=_b_cols, name="kv_b",
             out_dtypes=(BF16,))
    cq_raw, cq = mm(h2b, q_w_a, epi=lambda r, gq: (r, _rms(r, gq)), extras=(qn_g,), out_dtypes=(F32, BF16), name="q_a")

    def rope_and_scale(r, cs, sn):
        return (_cat(r[:, :Q_CHIP_NOPE], _rope_tile(r[:, Q_CHIP_NOPE:], cs, sn)) * Q_PRESCALE,)
    qro = mm(cq, q_w_b, n_dim=N_CHIPS * Q_CHIP, tiles=(None, Q_CHIP, Q_LORA), b_view=_b_cols, epi=rope_and_scale,
             extras=(cos_q, sin_q), out_dtypes=(BF16,), name="q_b")
    (o, lse), landed = attn_fwd(qro, kvb, krope, ride_for(second_ride))
    gathered(second_ride, landed)
    mix1, h3, h3b = mm(o, w_o, name="attn_out", **then_ln(h2, ("ln_mix_g", "ln_mix_b"), 1))
    f2pre, f2 = mlp_fwd(h3, h3b, 1, with_ln=False)
    def last_ln_loss_and_back(h, mix, gl, bl, t):
        e = _layer_norm(h, mix, gl, bl) - t
        dr, dg, db = _layer_norm_bwd(h, mix, gl, e * (1.0 / D_MODEL))
        return (dr, dr), (jnp.broadcast_to(jnp.sum(e * e), (1, LANES)), dg, db)
    dr4, dr4b, loss_acc, dg_f1, db_f1 = rowwise(
        last_ln_loss_and_back, (h3, f2, ln("ln_ffn_g", 1), ln("ln_ffn_b", 1), target),
        ((D_MODEL, F32), (D_MODEL, BF16)), accs=(LANES, D_MODEL, D_MODEL), name="ln_ffn_1_loss")
    loss = loss_acc[0, 0] * (0.5 / D_MODEL)

    g = {}

    def into_rows(off, rows_per_chip, shape=pack_shape):
        def view(tm, tn):
            if tm == N_CHIPS * rows_per_chip:
                return pl.BlockSpec((N_CHIPS, rows_per_chip, tn), lambda i, j, k: (0, off // rows_per_chip, 0))
            nb = rows_per_chip // tm
            return pl.BlockSpec((None, tm, tn), lambda i, j, k: (i // nb, off // tm + i % nb, 0))
        return shape, view

    def into_cols(off):
        return pack_shape, lambda tm, tn: pl.BlockSpec((None, tm, tn), lambda i, j, k: (j, off // tm + i, 0))

    def mlp_bwd(pack, dr, drb, hb, pre, layer, swap=False):
        w2_rows = (EARLY_OFF["w_ff2"] + layer * ff_tile, ff_tile)
        w1_rows = (EARLY_OFF["w_ff1"] + layer * D_MODEL, D_MODEL)
        ready = [(w1_rows[0] + w1_rows[1], w2_rows[0] - w1_rows[0] - w1_rows[1]), (w2_rows[0] + w2_rows[1], EARLY_ROWS - w2_rows[0] - w2_rows[1])]
        dpre = mm(drb, w["w_ff2"][layer].reshape(D_FF, D_MODEL), tb=True, epi=lambda r, p: (r * 2.0 * jnp.maximum(p, 0.0),),
                  extras=(pre,), out_dtypes=(BF16,), tiles=(None, ff_tile, None), name=f"ff2_dx_{layer}",
                  ride=SwapRide(pack, ready) if swap else None)
        if swap:
            dpre, (theirs,) = dpre
        pack = mm(pre, drb, ta=True, pro_a=_relu2, name=f"ff2_dw_{layer}", tiles=(ff_tile, PACK_W, None), into=pack,
                  out_view=into_rows(w2_rows[0], ff_tile))
        pack = mm(hb, dpre, ta=True, name=f"ff1_dw_{layer}", tiles=(None, PACK_W, None), into=pack,
                  out_view=into_cols(w1_rows[0]))
        dh = mm(dpre, w["w_ff1"][layer], tb=True, epi=lambda r, d: (r + DN_ALPHA * d,), extras=(dr,), n_dim=D_MODEL,
                tiles=(None, D_MODEL, ff_tile), b_view=_b_cols_t, name=f"ff1_dx_{layer}",
                ride=SwapRide(pack, [w1_rows, w2_rows], into=theirs) if swap else None)
        return (pack, *dh) if swap else (pack, dh)

    pack, dh3 = mlp_bwd(None, dr4, dr4b, h3b, f2pre, 1)
    dr3, dr3b, dg_m1, db_m1 = ln_bwd(h2, mix1, ln("ln_mix_g", 1), dh3, "ln_mix_bwd_1")
    shard_rows = D_MODEL // N_CHIPS
    pack = mm(o, dr3b, ta=True, name="attn_out_dw", tiles=(D_MODEL, PACK_W, None), into=pack,
              out_view=into_rows(EARLY_OFF["attn_w_o"], shard_rows))
    do = mm(dr3b, w_o, tb=True, name="attn_out_dx")
    def head_dots(do, o):
        return (jnp.concatenate([jnp.sum(do[:, V_HEAD * h:V_HEAD * (h + 1)] * o[:, V_HEAD * h:V_HEAD * (h + 1)], axis=1,
                                         keepdims=True) for h in range(N_HEADS)], axis=1),), ()
    (delta,) = rowwise(head_dots, (do, o), ((N_HEADS, F32),), name="attn_delta")
    tb = min(ATT_TK, seq)
    lse_row = lse.reshape(N_HEADS, seq // tb, tb)
    delta_row = delta.T.reshape(N_HEADS, seq // tb, tb)
    dqn, dqr, dkvb, dkr = attn_bwd(qro, kvb, krope, do, lse_row, delta_row)

    def q_rope_bwd(dn, dr, cs, sn):
        parts = []
        for k in range(N_CHIPS):
            parts.append(dn[:, Q_CHIP_NOPE * k:Q_CHIP_NOPE * (k + 1)])
            parts.append(_rope_tile_bwd(dr[:, LANES * k:LANES * (k + 1)], cs, sn))
        return (jnp.concatenate(parts, axis=1),), ()
    (dqlin,) = rowwise(q_rope_bwd, (dqn, dqr, cos_q, sin_q), ((N_CHIPS * Q_CHIP, BF16),), name="q_rope_bwd")
    g["q_w_b"] = mm(cq, dqlin, ta=True, name="q_b_dw", tiles=(Q_LORA, Q_CHIP, None), out_view=_out_cols(q_w_b.shape))
    dcq = mm(dqlin, q_w_b, tb=True, n_dim=Q_LORA, tiles=(None, Q_LORA, Q_CHIP), b_view=_b_cols_t, name="q_b_dx")

    def q_norm_bwd(c, gq, d):
        dx, dgq = _rms_bwd(c, gq, d)
        return (dx,), (dgq,)
    dcq_raw, dqn_g = rowwise(q_norm_bwd, (cq_raw, qn_g, dcq), ((Q_LORA, BF16),), accs=(Q_LORA,), name="q_norm_bwd")
    g["q_w_a"] = mm(h2b, dcq_raw, ta=True, name="q_a_dw")
    g["kv_w_b"] = mm(ckv, dkvb, ta=True, name="kv_b_dw", tiles=(KV_LORA, kvb_tile, None), out_view=_out_cols(kv_w_b.shape))
    dckv = mm(dkvb, kv_w_b, tb=True, n_dim=KV_LORA, tiles=(None, KV_LORA, kvb_tile), b_view=_b_cols_t, name="kv_b_dx")
    dkr_sum = head_sum(dkr)

    def kv_post_bwd(kva, gk, dc, dk, cs, sn):
        dx, dgk = _rms_bwd(kva[:, :KV_LORA], gk, dc)
        dk = dk + pltpu.roll(dk, LANES - HALF_ROPE, 1)
        return (jnp.concatenate([dx, _rope_tile_bwd(dk, cs, sn)], axis=1),), (dgk,)
    dkva, dkvn_g = rowwise(kv_post_bwd, (kva, kvn_g, dckv, dkr_sum, cos_k, sin_k), ((KVA_PAD, BF16),),
                           accs=(KV_LORA,), name="kv_post_bwd")
    g["kv_w_a"] = mm(h2b, dkva, ta=True, name="kv_a_dw")
    dh2 = mm(dcq_raw, q_w_a, tb=True, epi=lambda r, d: (r + DN_ALPHA * d,), extras=(dr3,), name="q_a_dx")
    dh2 = mm(dkva, kv_w_a, tb=True, epi=lambda r, d: (r + d,), extras=(dh2,), name="kv_a_dx")

    dr2, dr2b, dg_f0, db_f0 = ln_bwd(h1, f1, ln("ln_ffn_g", 0), dh2, "ln_ffn_bwd_0")
    pack = put_rows(pack, packed_shards(g, MISC_EARLY, EARLY_ROWS - MISC_EARLY_OFF), MISC_EARLY_OFF)
    if comm is None:
        pack, dh1 = mlp_bwd(pack, dr2, dr2b, h1b, f1pre, 0)
    else:
        pack, dh1, (theirs,) = mlp_bwd(pack, dr2, dr2b, h1b, f1pre, 0, swap=True)
        early_sums = add_halves(pack, theirs, comm[1])
    dr1, dr1b, dg_m0, db_m0 = ln_bwd(x, mix0, ln("ln_mix_g", 0), dh1, "ln_mix_bwd_0")
    mid = mm(z, dr1b, ta=True, name="ssm_out_dw", tiles=(D_MODEL, PACK_W, None),
             out_view=into_rows(MID_OFF["ssm_w_out"], shard_rows, (N_CHIPS, MID_ROWS, PACK_W)))
    dz = mm(dr1b, w_out, tb=True, name="ssm_out_dx")

    def glu_bwd(v, dz):
        val, sg = v[:, :D_MODEL], _sigmoid(v[:, D_MODEL:])
        return (jnp.concatenate([dz * sg, dz * val * sg * (1.0 - sg)], axis=1),), ()
    (dvg,) = rowwise(glu_bwd, (vg, dz), ((2 * D_MODEL, BF16),), name="glu_bwd")
    g["ssm_w_glu"] = mm(yg, dvg, ta=True, name="glu_proj_dw", tiles=(None, glu_tile, None), out_view=_out_cols(w_glu.shape))
    mid = put_rows(mid, packed_shards(g, MISC_MID, MID_ROWS - MISC_MID_OFF), MISC_MID_OFF)
    dypre = mm(dvg, w_glu, tb=True, epi=lambda r, y: (r * _gelu_grad(y),), extras=(ypre,), n_dim=D_MODEL,
               tiles=(None, D_MODEL, glu_tile), b_view=_b_cols_t, name="glu_proj_dx",
               ride=Together([SwapRide(mid), SendRide([(early_sums, (0, EARLY_HEAD), None)])]) if comm is not None else None)
    sends = None
    if comm is not None:
        dypre, (theirs, early_got) = dypre
        sends = SendRide([(early_sums, (EARLY_HEAD, EARLY_ROWS - EARLY_HEAD), early_got), add_halves(mid, theirs, comm[1])])
    (dx, dbbd_re, dbbd_im, dcbd_re, dcbd_imn, dar, dai, dd), got = s5_bwd(
        dypre, x, dr1, h_re, h_im, bbd_re, bbd_im, cbd_re, cbd_imn, a_re, a_im, dskip, sends)
    dbb_re = _blockdiag_in_t(dbbd_re).reshape(N_STATES, SSM_GROUP)
    dbb_im = _blockdiag_in_t(dbbd_im).reshape(N_STATES, SSM_GROUP)
    dlr, dli, dldt, db_re, db_im = s5_prep_bwd(lr, li, ldt, b_re, b_im, dar.reshape(N_STATES, 1),
                                               dai.reshape(N_STATES, 1), dbb_re, dbb_im)
    g["ssm_lam_re"] = dlr.reshape(1, N_GROUPS, SSM_STATE)
    g["ssm_lam_im"] = dli.reshape(1, N_GROUPS, SSM_STATE)
    g["ssm_log_dt"] = group_sum(dldt).reshape(1, N_GROUPS)
    g["ssm_b_re"] = db_re.reshape(1, N_GROUPS, SSM_STATE, SSM_GROUP)
    g["ssm_b_im"] = db_im.reshape(1, N_GROUPS, SSM_STATE, SSM_GROUP)
    g["ssm_c_re"] = _blockdiag_out_t(dcbd_re).reshape(1, N_GROUPS, SSM_GROUP, SSM_STATE)
    g["ssm_c_im"] = -_blockdiag_out_t(dcbd_imn).reshape(1, N_GROUPS, SSM_GROUP, SSM_STATE)
    g["ssm_d"] = dd
    g["ln_mix_g"] = jnp.concatenate([dg_m0, dg_m1], 0)
    g["ln_mix_b"] = jnp.concatenate([db_m0, db_m1], 0)
    g["ln_ffn_g"] = jnp.concatenate([dg_f0, dg_f1], 0)
    g["ln_ffn_b"] = jnp.concatenate([db_f0, db_f1], 0)
    g["kv_norm_g"] = dkvn_g.reshape(KV_LORA)
    g["q_norm_g"] = dqn_g
    return loss, dx, pack, mid, g, list(zip(sends.ins, got)) if comm is not None else None


def place(shard, me_idx, dtype, name, layer=None):
    rows, cols = shard.shape[-2:]
    tr = _tile(rows, (512, 256, 128))

    def body(m_ref, x_ref, o_ref):
        o_ref[...] = x_ref[...].astype(o_ref.dtype)

    in_spec = (pl.BlockSpec((tr, cols), lambda i, m: (i, 0)) if layer is None
               else pl.BlockSpec((None, tr, cols), lambda i, m: (layer, i, 0)))
    return _pcall(
        body, name=name,
        grid_spec=pltpu.PrefetchScalarGridSpec(
            num_scalar_prefetch=1, grid=(rows // tr,), in_specs=[in_spec],
            out_specs=pl.BlockSpec((None, tr, cols), lambda i, m: (m[0], i, 0))),
        out_shape=jax.ShapeDtypeStruct((N_CHIPS, rows, cols), dtype),
        compiler_params=_params(("parallel",)),
    )(me_idx, shard)


def place_many(shards, dtypes, me_idx, name):
    def body(m_ref, *refs):
        for x_ref, o_ref in zip(refs[:len(shards)], refs[len(shards):]):
            o_ref[...] = x_ref[...].astype(o_ref.dtype)

    return _pcall(
        body, name=name,
        grid_spec=pltpu.PrefetchScalarGridSpec(
            num_scalar_prefetch=1, grid=(1,),
            in_specs=[pl.BlockSpec(s.shape, lambda i, m: (0, 0)) for s in shards],
            out_specs=[pl.BlockSpec((None,) + s.shape, lambda i, m: (m[0], 0, 0)) for s in shards]),
        out_shape=[jax.ShapeDtypeStruct((N_CHIPS,) + s.shape, d) for s, d in zip(shards, dtypes)],
        compiler_params=_params(("arbitrary",)),
    )(me_idx, *shards)


def put_rows(pack, rows, off):
    _, n, cols = rows.shape

    def body(r_ref, p_ref, o_ref, sem):
        cp = pltpu.make_async_copy(r_ref.at[0], o_ref.at[pl.program_id(0), pl.ds(off, n), :], sem)
        cp.start()
        cp.wait()

    return _pcall(body, name="grad_put_rows", grid=(N_CHIPS,),
                  in_specs=[pl.BlockSpec((1, n, cols), lambda k: (k, 0, 0)), _ANY], out_specs=_ANY,
                  out_shape=jax.ShapeDtypeStruct(pack.shape, pack.dtype), input_output_aliases={1: 0},
                  scratch_shapes=[pltpu.SemaphoreType.DMA],
                  compiler_params=_params(("arbitrary",)))(rows, pack)


def _my_cols(c, mine=True):
    start = (c if mine else 1 - c) * HALF_W
    return pl.ds(pl.multiple_of(start, HALF_W), HALF_W)


def add_halves(gpack, got, c_idx):
    n, rows, _ = gpack.shape
    tr = min(G_BLOCK_ROWS, rows)
    blk = (None, tr, HALF_W)

    def body(c_ref, g_ref, r_ref, o_ref):
        o_ref[...] = (g_ref[...] + r_ref[...]).astype(o_ref.dtype)

    return _pcall(
        body, name="grad_add_halves",
        grid_spec=pltpu.PrefetchScalarGridSpec(
            num_scalar_prefetch=1, grid=(n, rows // tr),
            in_specs=[pl.BlockSpec(blk, lambda k, i, c: (k, i, c[0])), pl.BlockSpec(blk, lambda k, i, c: (k, i, 0))],
            out_specs=pl.BlockSpec(blk, lambda k, i, c: (k, i, 0))),
        out_shape=jax.ShapeDtypeStruct((n, rows, HALF_W), BF16),
        compiler_params=_params(("parallel", "parallel")),
    )(c_idx, gpack, got)


def sum_owner(part, got, idx, total_rows, row_off=0, into=None):
    _, rows, _ = part.shape
    tr = math.gcd(math.gcd(rows, row_off), G_BLOCK_ROWS)
    n_into = 0 if into is None else 1

    def body(m_ref, p_ref, g_ref, *rest):
        up = lambda v: v.astype(F32)
        rest[-1][...] = ((up(p_ref[...]) + up(g_ref[0])) + up(g_ref[1])) + up(g_ref[2])

    return _pcall(
        body, name="grad_sum_owner",
        grid_spec=pltpu.PrefetchScalarGridSpec(
            num_scalar_prefetch=1, grid=(rows // tr,),
            in_specs=[pl.BlockSpec((None, tr, HALF_W), lambda i, m: (m[0], i, 0)),
                      pl.BlockSpec((3, tr, HALF_W), lambda i, m: (0, i, 0))] + [_ANY] * n_into,
            out_specs=pl.BlockSpec((tr, HALF_W), lambda i, m: (row_off // tr + i, m[1]))),
        out_shape=jax.ShapeDtypeStruct((total_rows, PACK_W), F32),
        input_output_aliases={3: 0} if n_into else {},
        compiler_params=_params(("parallel",)),
    )(idx, part, got, *([into] if n_into else []))


def join_halves(red):
    def body(in_ref, out_ref, send_sem, recv_sem):
        x, y, c, _ = _place()
        sibling = (x, y, 1 - c)
        mine = out_ref.at[:, _my_cols(c)]
        cp = pltpu.make_async_remote_copy(src_ref=mine, dst_ref=mine, send_sem=send_sem, recv_sem=recv_sem,
                                          device_id=sibling, device_id_type=MESH)
        cp.start()
        cp.wait_send()
        other = out_ref.at[:, _my_cols(c, mine=False)]
        pltpu.make_async_remote_copy(src_ref=other, dst_ref=other, send_sem=send_sem, recv_sem=recv_sem,
                                     device_id=sibling, device_id_type=MESH).wait_recv()

    return _pcall(body, name="grad_join_halves", in_specs=[_ANY], out_specs=_ANY,
                  out_shape=jax.ShapeDtypeStruct(red.shape, red.dtype), input_output_aliases={0: 0},
                  scratch_shapes=[pltpu.SemaphoreType.DMA, pltpu.SemaphoreType.DMA])(red)


def adamw(gsrc, g_off, wt, m, v, name):
    n, cols = wt.shape
    tr = math.gcd(math.gcd(g_off, n), 256) if g_off else math.gcd(n, 256)
    off_blk = g_off // tr
    c1 = 1.0 / (1.0 - ADAM_B1 ** ADAM_STEP)
    c2 = 1.0 / (1.0 - ADAM_B2 ** ADAM_STEP)

    def body(g_ref, w_ref, m_ref, v_ref, go_ref, d_ref, mo_ref, vo_ref):
        gv = g_ref[...]
        mn = ADAM_B1 * m_ref[...] + (1.0 - ADAM_B1) * gv
        vn = ADAM_B2 * v_ref[...] + (1.0 - ADAM_B2) * gv * gv
        go_ref[...] = gv
        mo_ref[...] = mn
        vo_ref[...] = vn
        d_ref[...] = -ADAM_LR * ((mn * c1) / (jnp.sqrt(vn * c2) + ADAM_EPS) + ADAM_WD * w_ref[...])

    blk = pl.BlockSpec((tr, cols), lambda i: (i, 0))
    return _pcall(body, name=name, grid=(n // tr,),
                  in_specs=[pl.BlockSpec((tr, cols), lambda i: (off_blk + i, 0)), blk, blk, blk],
                  out_specs=[blk] * 4, out_shape=[jax.ShapeDtypeStruct((n, cols), F32)] * 4,
                  compiler_params=_params(("parallel",)))(gsrc, wt, m, v)


def _rows8(a):
    return -(-a.size // (8 * PACK_W)) * 8


def _as_rows(a, rows=None):
    flat = a.reshape(-1)
    n = _rows8(a) if rows is None else rows
    return jnp.pad(flat, (0, n * PACK_W - flat.shape[0])).reshape(n, PACK_W)


def local_shards_2d(wl):
    return {"w_ff1": [wl["w_ff1"][0], wl["w_ff1"][1]], "w_ff2": [wl["w_ff2"][0], wl["w_ff2"][1]],
            "ssm_w_glu": wl["ssm_w_glu"], "ssm_w_out": wl["ssm_w_out"], "kv_w_a": _pad_kva_cols(wl["kv_w_a"]),
            "kv_w_b": wl["kv_w_b"], "q_w_a": wl["q_w_a"], "q_w_b": _perm_q_cols(wl["q_w_b"]),
            "attn_w_o": wl["attn_w_o"], "ssm_d": wl["ssm_d"].reshape(2, -1)}


def misc_grad_shard(name, g, k):
    if name == "ssm_d":
        w = D_MODEL // N_CHIPS
        return g[:, w * k:w * (k + 1)]
    if name in ("ssm_w_glu", "kv_w_b"):
        return g[k]
    if name == "q_w_b":
        return _unperm_q_cols(g[k])
    rows = D_MODEL // N_CHIPS
    shard = g[rows * k:rows * (k + 1)]
    return _unpad_kva_cols(shard) if name == "kv_w_a" else shard


def packed_shards(g, names, rows, tail=None):
    blocks = []
    for k in range(N_CHIPS):
        parts = [_as_rows(misc_grad_shard(n, g[n], k), MISC_SHARD_ROWS[n]) for n in names]
        if tail is not None:
            parts.append(tail[k * (tail.shape[0] // N_CHIPS):(k + 1) * (tail.shape[0] // N_CHIPS)])
        blk = jnp.concatenate(parts, axis=0)
        blocks.append(jnp.pad(blk, ((0, rows - blk.shape[0]), (0, 0))))
    return jnp.stack(blocks)


def kernel(x, positions, ln_mix_g, ln_mix_b, ln_ffn_g, ln_ffn_b, w_ff1, w_ff2, ssm_lam_re, ssm_lam_im, ssm_log_dt, ssm_b_re, ssm_b_im, ssm_c_re, ssm_c_im, ssm_d, ssm_w_glu, ssm_w_out, kv_w_a, kv_norm_g, kv_w_b, q_w_a, q_norm_g, q_w_b, attn_w_o, loss_target, m_ln_mix_g, m_ln_mix_b, m_ln_ffn_g, m_ln_ffn_b, m_w_ff1, m_w_ff2, m_ssm_lam_re, m_ssm_lam_im, m_ssm_log_dt, m_ssm_b_re, m_ssm_b_im, m_ssm_c_re, m_ssm_c_im, m_ssm_d, m_ssm_w_glu, m_ssm_w_out, m_kv_w_a, m_kv_norm_g, m_kv_w_b, m_q_w_a, m_q_norm_g, m_q_w_b, m_attn_w_o, v_ln_mix_g, v_ln_mix_b, v_ln_ffn_g, v_ln_ffn_b, v_w_ff1, v_w_ff2, v_ssm_lam_re, v_ssm_lam_im, v_ssm_log_dt, v_ssm_b_re, v_ssm_b_im, v_ssm_c_re, v_ssm_c_im, v_ssm_d, v_ssm_w_glu, v_ssm_w_out, v_kv_w_a, v_kv_norm_g, v_kv_w_b, v_q_w_a, v_q_norm_g, v_q_w_b, v_attn_w_o):
    env = dict(locals())
    wl = {n: env[n] for n in WEIGHTS}
    ml = {n: env["m_" + n] for n in WEIGHTS}
    vl = {n: env["v_" + n] for n in WEIGHTS}
    for n in ("ssm_w_glu", "ssm_w_out", "q_w_a", "q_w_b", "attn_w_o"):
        wl[n], ml[n], vl[n] = wl[n][0], ml[n][0], vl[n][0]

    c_idx = lax.axis_index("c").astype(jnp.int32).reshape(1)
    me_idx = (2 * lax.axis_index("x") + lax.axis_index("y")).astype(jnp.int32).reshape(1)

    local = local_shards_2d(wl)
    stacked = {n: [place(wl[n], me_idx, BF16, f"place_{n}_{l}", layer=l) for l in range(DEPTH)] for n in ("w_ff1", "w_ff2")}
    others = [n for n in SHARDED if n not in stacked]
    stacked.update(zip(others, place_many([local[n] for n in others], [F32 if n == "ssm_d" else BF16 for n in others],
                                          me_idx, "place_others")))
    stacked["ssm_d"] = ride_alone(GatherRide([_halves(stacked["ssm_d"])]), "ssm_d_all_gather")[0].reshape(1, D_MODEL)
    for n in REPLICATED:
        stacked[n] = wl[n]

    loss_part, dx, early, mid, g, sent = device_step(x[0], positions[0], loss_target[0], stacked, comm=(me_idx, c_idx))
    loss = lax.psum(loss_part, ("x", "y", "c"))

    small = jnp.concatenate([_as_rows(g[n]) for n in REPLICATED], axis=0)
    small = jnp.pad(small, ((0, SMALL_ROWS - small.shape[0]), (0, 0)))
    late = packed_shards(g, MISC_LATE, LATE_ROWS, tail=small)
    late_sums = add_halves(late, ride_alone(SwapRide(late), "grad_swap_halves")[0], c_idx)
    sent.append((late_sums, ride_alone(SendRide([late_sums]), "grad_send_to_owners")[0]))
    where = jnp.concatenate([me_idx, c_idx])
    starts = (0, EARLY_ROWS, EARLY_ROWS + MID_ROWS)
    total_rows = EARLY_ROWS + MID_ROWS + LATE_ROWS
    reduced = None
    for (sums, got), off in zip(sent, starts):
        reduced = sum_owner(sums, got, where, total_rows, row_off=off, into=reduced)
    reduced = join_halves(reduced)
    quarter = reduced[starts[2] + SMALL_OFF:starts[2] + SMALL_OFF + SMALL_Q_ROWS]
    small_tot = ride_alone(GatherRide([_halves(place(quarter, me_idx, F32, "place_small_grads"))]),
                           "small_grad_all_gather")[0].reshape(SMALL_ROWS, PACK_W)

    out_g, out_d, out_m, out_v = {}, {}, {}, {}
    direct = {**EARLY_OFF, **{n: starts[1] + o for n, o in MID_OFF.items()}}
    for n, off in direct.items():
        res = adamw(reduced, off, wl[n].reshape(-1, PACK_W), ml[n].reshape(-1, PACK_W), vl[n].reshape(-1, PACK_W),
                    "adamw_" + n)
        out_g[n], out_d[n], out_m[n], out_v[n] = [a.reshape(env[n].shape) for a in res]
    for names, off in ((MISC_EARLY, MISC_EARLY_OFF), (MISC_MID, starts[1] + MISC_MID_OFF), (MISC_LATE, starts[2])):
        pack3 = lambda d: jnp.concatenate([_as_rows(d[n], MISC_SHARD_ROWS[n]) for n in names], axis=0)
        res = adamw(reduced, off, pack3(wl), pack3(ml), pack3(vl), "adamw_packed_" + names[0])
        r0 = 0
        for n in names:
            cnt = math.prod(env[n].shape)
            out_g[n], out_d[n], out_m[n], out_v[n] = [
                a[r0:r0 + MISC_SHARD_ROWS[n]].reshape(-1)[:cnt].reshape(env[n].shape) for a in res]
            r0 += MISC_SHARD_ROWS[n]
    ws = jnp.concatenate([_as_rows(wl[n]) for n in REPLICATED], axis=0)
    ms = jnp.concatenate([_as_rows(ml[n]) for n in REPLICATED], axis=0)
    vs = jnp.concatenate([_as_rows(vl[n]) for n in REPLICATED], axis=0)
    pad = ((0, SMALL_ROWS - ws.shape[0]), (0, 0))
    res = adamw(small_tot, 0, jnp.pad(ws, pad), jnp.pad(ms, pad), jnp.pad(vs, pad), "adamw_replicated")
    row = 0
    for n in REPLICATED:
        cnt = math.prod(env[n].shape)
        nrows = _rows8(env[n])
        out_g[n], out_d[n], out_m[n], out_v[n] = [a[row:row + nrows].reshape(-1)[:cnt].reshape(env[n].shape) for a in res]
        row += nrows

    return (loss, dx[None], *[out_g[n] for n in WEIGHTS], *[out_d[n] for n in WEIGHTS],
            *[out_m[n] for n in WEIGHTS], *[out_v[n] for n in WEIGHTS])
```

```python
import functools
import math

import jax
import jax.numpy as jnp
from jax import lax
from jax.experimental import pallas as pl
from jax.experimental.pallas import tpu as pltpu

F32 = jnp.float32
BF16 = jnp.bfloat16
MESH = pl.DeviceIdType.MESH

D_MODEL = 1024
DEPTH = 2
SSM_GROUP = 16
N_GROUPS = D_MODEL // SSM_GROUP
SSM_STATE = 64
N_STATES = N_GROUPS * SSM_STATE
N_HEADS = 8
QK_NOPE = 128
QK_ROPE = 64
HALF_ROPE = QK_ROPE // 2
V_HEAD = 128
QK_DIM = QK_NOPE + QK_ROPE
Q_LORA = 384
KV_LORA = 256
ROPE_THETA = 10000.0
SM_SCALE = QK_DIM ** -0.5
NEG_INF = -1e30
D_FF = 4 * D_MODEL
DN_ALPHA = (2 * DEPTH) ** 0.25
LN_EPS = 1e-5
RMS_EPS = 1e-6
ADAM_LR = 0.001
ADAM_B1 = 0.9
ADAM_B2 = 0.999
ADAM_EPS = 1e-08
ADAM_WD = 0.01
ADAM_STEP = 10

N_CHIPS = 4
LANES = 128
VMEM_LIMIT = 56 * 1024 * 1024
MM_VMEM_BUDGET = 40 * 1024 * 1024
PACK_W = 1024
KVA_PAD = 384
HALF_W = PACK_W // 2

SHARDED = ("w_ff1", "w_ff2", "ssm_w_glu", "ssm_w_out", "kv_w_a", "kv_w_b", "q_w_a", "q_w_b", "attn_w_o", "ssm_d")
G_BLOCK_ROWS = 960
EARLY_OFF = {"w_ff1": 0, "w_ff2": 2048, "attn_w_o": 4096}
MISC_EARLY = ("kv_w_b", "kv_w_a", "q_w_a", "q_w_b")
MISC_EARLY_OFF = 4352
EARLY_ROWS = 5 * G_BLOCK_ROWS
EARLY_HEAD = G_BLOCK_ROWS
MID_OFF = {"ssm_w_out": 0}
MISC_MID = ("ssm_w_glu",)
MISC_MID_OFF = 256
MID_ROWS = MISC_MID_OFF + 512
MISC_LATE = ("ssm_d",)
SMALL_Q_ROWS = 96
SMALL_ROWS = N_CHIPS * SMALL_Q_ROWS
SMALL_OFF = 16
LATE_ROWS = 192
MISC_SHARD_ROWS = {"ssm_d": 16, "ssm_w_glu": 512, "kv_w_b": 128, "kv_w_a": 80, "q_w_a": 96, "q_w_b": 144}
REPLICATED = ("ln_mix_g", "ln_mix_b", "ln_ffn_g", "ln_ffn_b", "ssm_lam_re", "ssm_lam_im", "ssm_log_dt",
              "ssm_b_re", "ssm_b_im", "ssm_c_re", "ssm_c_im", "kv_norm_g", "q_norm_g")
WEIGHTS = ("ln_mix_g", "ln_mix_b", "ln_ffn_g", "ln_ffn_b", "w_ff1", "w_ff2", "ssm_lam_re", "ssm_lam_im",
           "ssm_log_dt", "ssm_b_re", "ssm_b_im", "ssm_c_re", "ssm_c_im", "ssm_d", "ssm_w_glu", "ssm_w_out",
           "kv_w_a", "kv_norm_g", "kv_w_b", "q_w_a", "q_norm_g", "q_w_b", "attn_w_o")


def _pcall(body, **kw):
    return pl.pallas_call(body, **kw)


def _params(sem=None):
    return pltpu.CompilerParams(dimension_semantics=sem, vmem_limit_bytes=VMEM_LIMIT)


_ANY = pl.BlockSpec(memory_space=pl.ANY)


def _tile(dim, prefs):
    for p in prefs:
        if dim % p == 0:
            return p
    return dim


def _place():
    x, y, c = lax.axis_index("x"), lax.axis_index("y"), lax.axis_index("c")
    return x, y, c, [(1 - x, y), (x, 1 - y), (1 - x, 1 - y)]


def _remote(k, src, dst, to, send_sems, recv_sems):
    return pltpu.make_async_remote_copy(src_ref=src, dst_ref=dst, send_sem=send_sems.at[k], recv_sem=recv_sems.at[k],
                                        device_id=to, device_id_type=MESH)


class GatherRide:
    def __init__(self, arrs):
        self.ins = list(arrs)
        self.out_shapes = [jax.ShapeDtypeStruct(a.shape, a.dtype) for a in arrs]
        self.aliases = {i: i for i in range(len(arrs))}
        self.n_sems = 6 * len(arrs)

    def start(self, ins, outs, send_sems, recv_sems):
        x, y, c, chips = _place()
        me = 2 * x + y
        for a, o in enumerate(outs):
            for j, (px, py) in enumerate(chips):
                _remote(6 * a + j, o.at[me, c], o.at[me, c], (px, py, c), send_sems, recv_sems).start()

    def pass_on(self, ins, outs, send_sems, recv_sems):
        x, y, c, chips = _place()
        for a, o in enumerate(outs):
            for j, (px, py) in enumerate(chips):
                blk = o.at[2 * px + py, c]
                _remote(6 * a + j, blk, blk, (px, py, c), send_sems, recv_sems).wait_recv()
                _remote(6 * a + 3 + j, blk, blk, (x, y, 1 - c), send_sems, recv_sems).start()

    def finish(self, ins, outs, send_sems, recv_sems, passed_on=False):
        if not passed_on:
            self.pass_on(ins, outs, send_sems, recv_sems)
        x, y, c, chips = _place()
        me = 2 * x + y
        sibling = (x, y, 1 - c)
        for a, o in enumerate(outs):
            for j, (px, py) in enumerate(chips):
                blk = o.at[2 * px + py, 1 - c]
                _remote(6 * a + 3 + j, blk, blk, sibling, send_sems, recv_sems).wait_recv()
                _remote(6 * a + j, o.at[me, c], o.at[me, c], (px, py, c), send_sems, recv_sems).wait_send()
                mine = o.at[2 * px + py, c]
                _remote(6 * a + 3 + j, mine, mine, sibling, send_sems, recv_sems).wait_send()


class SendRide:
    base = 0

    def __init__(self, parts):
        parts = [p if isinstance(p, tuple) else (p, (0, p.shape[1]), None) for p in parts]
        self.rows = [rows for _, rows, _ in parts]
        self.n_parts = len(parts)
        self.ins = [p for p, _, _ in parts] + [into for _, _, into in parts if into is not None]
        self.out_shapes = [jax.ShapeDtypeStruct((3,) + p.shape[1:], p.dtype) for p, _, _ in parts]
        given = [a for a, (_, _, into) in enumerate(parts) if into is not None]
        self.aliases = {self.n_parts + i: a for i, a in enumerate(given)}
        self.n_sems = 3 * self.n_parts

    def _copies(self, ins, outs, send_sems, recv_sems):
        x, y, c, chips = _place()
        return [_remote(self.base + 3 * a + j, ins[a].at[2 * px + py, pl.ds(r0, n)], outs[a].at[j, pl.ds(r0, n)],
                        (px, py, c), send_sems, recv_sems)
                for a, (r0, n) in enumerate(self.rows) for j, (px, py) in enumerate(chips)]

    def start(self, ins, outs, send_sems, recv_sems):
        for cp in self._copies(ins, outs, send_sems, recv_sems):
            cp.start()

    def finish(self, ins, outs, send_sems, recv_sems):
        for cp in self._copies(ins, outs, send_sems, recv_sems):
            cp.wait()


class SwapRide:
    base = 0

    def __init__(self, pack, ranges=None, into=None):
        self.ins = [pack] if into is None else [pack, into]
        self.out_shapes = [jax.ShapeDtypeStruct(pack.shape[:2] + (HALF_W,), pack.dtype)]
        self.aliases = {} if into is None else {1: 0}
        self.ranges = ranges or [(0, pack.shape[1])]
        self.n_sems = len(self.ranges)

    def _copies(self, ins, outs, send_sems, recv_sems):
        x, y, c, _ = _place()
        return [_remote(self.base + k, ins[0].at[:, pl.ds(r0, n), _my_cols(c, mine=False)], outs[0].at[:, pl.ds(r0, n), :],
                        (x, y, 1 - c), send_sems, recv_sems) for k, (r0, n) in enumerate(self.ranges)]

    def start(self, ins, outs, send_sems, recv_sems):
        for cp in self._copies(ins, outs, send_sems, recv_sems):
            cp.start()

    def finish(self, ins, outs, send_sems, recv_sems):
        for cp in self._copies(ins, outs, send_sems, recv_sems):
            cp.wait()


class Together:
    def __init__(self, rides):
        self.rides = rides
        self.ins, self.out_shapes, self.aliases, self.n_sems = [], [], {}, 0
        for r in rides:
            r.base = self.n_sems
            self.aliases.update({len(self.ins) + i: len(self.out_shapes) + o for i, o in r.aliases.items()})
            self.ins += r.ins
            self.out_shapes += r.out_shapes
            self.n_sems += r.n_sems

    def _each(self, step, ins, outs, send_sems, recv_sems):
        i = o = 0
        for r in self.rides:
            getattr(r, step)(ins[i:i + len(r.ins)], outs[o:o + len(r.out_shapes)], send_sems, recv_sems)
            i, o = i + len(r.ins), o + len(r.out_shapes)

    def start(self, *refs):
        self._each("start", *refs)

    def finish(self, *refs):
        self._each("finish", *refs)


def _pcall_riding(body, args, ride, first, last, *, in_specs, out_specs, out_shape, scratch_shapes=(), middle=None,
                  **kw):
    n_in, n_out = len(args), len(out_shape)
    if ride is None:
        return _pcall(body, in_specs=in_specs, out_specs=out_specs, out_shape=out_shape,
                      scratch_shapes=list(scratch_shapes), **kw)(*args), []
    k_in, k_out = len(ride.ins), len(ride.out_shapes)

    def riding(*refs):
        ins, r_in = refs[:n_in], refs[n_in:n_in + k_in]
        outs = refs[n_in + k_in:n_in + k_in + n_out]
        r_out = refs[n_in + k_in + n_out:n_in + k_in + n_out + k_out]
        scratch, (send_sems, recv_sems) = refs[n_in + k_in + n_out + k_out:-2], refs[-2:]

        @pl.when(first())
        def _():
            ride.start(r_in, r_out, send_sems, recv_sems)

        if middle is not None:
            @pl.when(middle())
            def _():
                ride.pass_on(r_in, r_out, send_sems, recv_sems)

        body(*ins, *outs, *scratch)

        @pl.when(last())
        def _():
            if middle is not None:
                ride.finish(r_in, r_out, send_sems, recv_sems, passed_on=True)
            else:
                ride.finish(r_in, r_out, send_sems, recv_sems)

    res = _pcall(riding, in_specs=list(in_specs) + [_ANY] * k_in, out_specs=list(out_specs) + [_ANY] * k_out,
                 out_shape=list(out_shape) + ride.out_shapes,
                 input_output_aliases={n_in + i: n_out + o for i, o in ride.aliases.items()},
                 scratch_shapes=list(scratch_shapes) + [pltpu.SemaphoreType.DMA((ride.n_sems,))] * 2,
                 **kw)(*args, *ride.ins)
    return res[:n_out], res[n_out:]


def ride_alone(ride, name):
    def body(*refs):
        n = len(ride.ins)
        ins, outs, (send_sems, recv_sems) = refs[:n], refs[n:-2], refs[-2:]
        ride.start(ins, outs, send_sems, recv_sems)
        ride.finish(ins, outs, send_sems, recv_sems)

    return _pcall(body, name=name, in_specs=[_ANY] * len(ride.ins), out_specs=[_ANY] * len(ride.out_shapes),
                  out_shape=ride.out_shapes, input_output_aliases=dict(ride.aliases),
                  scratch_shapes=[pltpu.SemaphoreType.DMA((ride.n_sems,))] * 2)(*ride.ins)


def mm(a, b, *, name, ta=False, tb=False, pro_a=None, epi=None, extras=(), out_dtypes=(F32,), n_dim=None,
       tiles=(None, None, None), b_view=None, out_view=None, into=None, ride=None):
    if ta:
        k_dim, m_dim = a.shape
    else:
        m_dim, k_dim = a.shape
    if n_dim is None:
        n_dim = b.shape[0] if tb else b.shape[1]
    tn = tiles[1] or (n_dim if n_dim <= 1024 else _tile(n_dim, (1024, 512, 256, 128)))
    tk = tiles[2] or (k_dim if k_dim <= 1024 else _tile(k_dim, (1024, 512, 256, 128)))
    nk = k_dim // tk

    def vmem_bytes(tm):
        blocks = tm * tk * a.dtype.itemsize + tk * tn * b.dtype.itemsize
        blocks += sum(tm * (tn if e.shape[1] == n_dim else e.shape[1]) * e.dtype.itemsize for e in extras if e.shape[0] > 1)
        blocks += tm * tn * sum(jnp.dtype(d).itemsize for d in out_dtypes)
        return 2 * blocks + tm * tn * 4

    tm = tiles[0] or next((t for t in (4096, 2048, 1024, 512, 256) if m_dim % t == 0 and vmem_bytes(t) <= MM_VMEM_BUDGET),
                          _tile(m_dim, (128,)))
    assert m_dim % tm == 0 and n_dim % tn == 0 and k_dim % tk == 0, (name, m_dim, n_dim, k_dim, tm, tn, tk)
    n_ex, n_out = len(extras), len(out_dtypes)
    n_into = 0 if into is None else 1
    dims = (((0 if ta else 1,), (1 if tb else 0,)), ((), ()))

    def body(a_ref, b_ref, *rest):
        ex_refs, out_refs = rest[:n_ex], rest[n_ex + n_into:n_ex + n_into + n_out]

        def partial():
            av = a_ref[...]
            if pro_a is not None:
                av = pro_a(av)
            return lax.dot_general(av.astype(BF16), b_ref[...].astype(BF16), dims, preferred_element_type=F32)

        def finish(r):
            res = epi(r, *[e[...] for e in ex_refs]) if epi is not None else (r,)
            for o_ref, v in zip(out_refs, res):
                o_ref[...] = v.reshape(o_ref.shape).astype(o_ref.dtype)

        if nk == 1:
            finish(partial())
            return
        acc = rest[-1]
        k = pl.program_id(2)

        @pl.when(k == 0)
        def _():
            acc[...] = partial()

        @pl.when(k > 0)
        def _():
            acc[...] += partial()

        @pl.when(k == nk - 1)
        def _():
            finish(acc[...])

    def ex_spec(e):
        if e.shape == (m_dim, n_dim):
            return o_spec
        if e.shape[0] == m_dim:
            return pl.BlockSpec((tm, e.shape[1]), lambda i, j, k: (i, 0))
        return pl.BlockSpec(e.shape, lambda i, j, k: (0, 0))

    a_spec = pl.BlockSpec((tk, tm), lambda i, j, k: (k, i)) if ta else pl.BlockSpec((tm, tk), lambda i, j, k: (i, k))
    if b_view is not None:
        b_spec = b_view(tk, tn)
    else:
        b_spec = pl.BlockSpec((tn, tk), lambda i, j, k: (j, k)) if tb else pl.BlockSpec((tk, tn), lambda i, j, k: (k, j))
    o_spec = pl.BlockSpec((tm, tn), lambda i, j, k: (i, j))
    if out_view is None:
        out_specs = [o_spec] * n_out
        out_shape = [jax.ShapeDtypeStruct((m_dim, n_dim), dt) for dt in out_dtypes]
    else:
        assert n_out == 1
        out_specs = [out_view[1](tm, tn)]
        out_shape = [jax.ShapeDtypeStruct(out_view[0], out_dtypes[0])]
    grid = (m_dim // tm, n_dim // tn, nk)
    scratch = [pltpu.VMEM((tm, tn), F32)] if nk > 1 else []
    if ride is not None:
        assert into is None
        at = lambda ids: functools.reduce(jnp.logical_and, [pl.program_id(d) == i for d, i in enumerate(ids)])
        outs, landed = _pcall_riding(
            body, (a, b, *extras), ride, lambda: at((0, 0, 0)), lambda: at([g - 1 for g in grid]),
            name=name, grid=grid, in_specs=[a_spec, b_spec] + [ex_spec(e) for e in extras], out_specs=out_specs,
            out_shape=out_shape, scratch_shapes=scratch, compiler_params=_params(("arbitrary",) * 3))
        return (outs[0] if n_out == 1 else outs), landed
    outs = _pcall(
        body, name=name, grid=grid,
        in_specs=[a_spec, b_spec] + [ex_spec(e) for e in extras] + [_ANY] * n_into,
        out_specs=out_specs, out_shape=out_shape,
        input_output_aliases={2 + n_ex: 0} if n_into else {},
        scratch_shapes=scratch,
        compiler_params=_params(("parallel", "parallel", "arbitrary")),
    )(a, b, *extras, *([into] if n_into else []))
    return outs[0] if n_out == 1 else outs


def rowwise(fn, ins, outs, *, name, accs=(), tm=256):
    rows = ins[0].shape[0]
    tm = min(tm, rows)
    n_in, n_out, n_acc = len(ins), len(outs), len(accs)

    def body(*refs):
        in_refs, out_refs, acc_refs = refs[:n_in], refs[n_in:n_in + n_out], refs[n_in + n_out:]
        res, sums = fn(*[r[...] for r in in_refs])
        for o_ref, v in zip(out_refs, res):
            o_ref[...] = v.astype(o_ref.dtype)
        if n_acc:
            @pl.when(pl.program_id(0) == 0)
            def _():
                for a_ref in acc_refs:
                    a_ref[...] = jnp.zeros_like(a_ref)

            for a_ref, s in zip(acc_refs, sums):
                a_ref[...] += s

    def spec(arr):
        if arr.shape[0] == rows:
            return pl.BlockSpec((tm, arr.shape[1]), lambda i: (i, 0))
        return pl.BlockSpec(arr.shape, lambda i: (0, 0))

    res = _pcall(
        body, name=name, grid=(rows // tm,),
        in_specs=[spec(a) for a in ins],
        out_specs=[pl.BlockSpec((tm, w), lambda i: (i, 0)) for w, _ in outs]
        + [pl.BlockSpec((1, w), lambda i: (0, 0)) for w in accs],
        out_shape=[jax.ShapeDtypeStruct((rows, w), dt) for w, dt in outs]
        + [jax.ShapeDtypeStruct((1, w), F32) for w in accs],
        compiler_params=_params(("arbitrary",) if n_acc else ("parallel",)),
    )(*ins)
    return res


def _relu2(v):
    r = jnp.maximum(v, 0.0)
    return r * r


def _gelu(x):
    c = math.sqrt(2.0 / math.pi)
    return 0.5 * x * (1.0 + jnp.tanh(c * (x + 0.044715 * x * x * x)))


def _gelu_grad(x):
    c = math.sqrt(2.0 / math.pi)
    t = jnp.tanh(c * (x + 0.044715 * x * x * x))
    return 0.5 * (1.0 + t) + 0.5 * x * (1.0 - t * t) * c * (1.0 + 3 * 0.044715 * x * x)


def _sigmoid(x):
    return 1.0 / (1.0 + jnp.exp(-x))


def _layer_norm(h, mix, g, b):
    r = DN_ALPHA * h + mix
    mu = jnp.mean(r, axis=-1, keepdims=True)
    xc = r - mu
    var = jnp.mean(xc * xc, axis=-1, keepdims=True)
    return xc * lax.rsqrt(var + LN_EPS) * g + b


def _layer_norm_bwd(h, mix, g, dy):
    r = DN_ALPHA * h + mix
    mu = jnp.mean(r, axis=-1, keepdims=True)
    xc = r - mu
    var = jnp.mean(xc * xc, axis=-1, keepdims=True)
    rstd = lax.rsqrt(var + LN_EPS)
    xhat = xc * rstd
    dxh = dy * g
    m1 = jnp.mean(dxh, axis=-1, keepdims=True)
    m2 = jnp.mean(dxh * xhat, axis=-1, keepdims=True)
    dr = rstd * (dxh - m1 - xhat * m2)
    return dr, jnp.sum(dy * xhat, axis=0, keepdims=True), jnp.sum(dy, axis=0, keepdims=True)


def ln_bwd(h, mix, g, dy, name):
    def fn(h, mix, g, dy):
        dr, dg, db = _layer_norm_bwd(h, mix, g, dy)
        return (dr, dr), (dg, db)
    return rowwise(fn, (h, mix, g, dy), ((D_MODEL, F32), (D_MODEL, BF16)), accs=(D_MODEL, D_MODEL), name=name)


def _rms(x, g):
    r = lax.rsqrt(jnp.mean(x * x, axis=-1, keepdims=True) + RMS_EPS)
    return x * r * g


def _rms_bwd(x, g, dy):
    r = lax.rsqrt(jnp.mean(x * x, axis=-1, keepdims=True) + RMS_EPS)
    xn = x * r
    dyg = dy * g
    dx = r * (dyg - xn * jnp.mean(dyg * xn, axis=-1, keepdims=True))
    return dx, jnp.sum(dy * xn, axis=0, keepdims=True)


def _s5_disc(lr, li, ldt):
    dt = jnp.exp(ldt)
    mag = jnp.exp(lr * dt)
    cs, sn = jnp.cos(li * dt), jnp.sin(li * dt)
    ar, ai = mag * cs, mag * sn
    inv = 1.0 / (lr * lr + li * li)
    n_re = (ar - 1.0) * lr + ai * li
    n_im = ai * lr - (ar - 1.0) * li
    return dt, mag, cs, sn, ar, ai, inv, n_re, n_im


def s5_prep(lr, li, ldt, b_re, b_im):
    def fn(lr, li, ldt, b_re, b_im):
        _, _, _, _, ar, ai, inv, n_re, n_im = _s5_disc(lr, li, ldt)
        cr, ci = n_re * inv, n_im * inv
        return (ar, ai, cr * b_re - ci * b_im, cr * b_im + ci * b_re), ()
    return rowwise(fn, (lr, li, ldt, b_re, b_im), ((1, F32), (1, F32), (SSM_GROUP, F32), (SSM_GROUP, F32)),
                   name="s5_prep", tm=512)


def s5_prep_bwd(lr, li, ldt, b_re, b_im, dar, dai, dbb_re, dbb_im):
    def fn(lr, li, ldt, b_re, b_im, dar, dai, dbb_re, dbb_im):
        dt, mag, cs, sn, ar, ai, inv, n_re, n_im = _s5_disc(lr, li, ldt)
        cr, ci = n_re * inv, n_im * inv
        db_re = cr * dbb_re + ci * dbb_im
        db_im = cr * dbb_im - ci * dbb_re
        dcr = jnp.sum(dbb_re * b_re + dbb_im * b_im, axis=-1, keepdims=True)
        dci = jnp.sum(dbb_im * b_re - dbb_re * b_im, axis=-1, keepdims=True)
        dar = dar + (dcr * lr - dci * li) * inv
        dai = dai + (dcr * li + dci * lr) * inv
        dinv = dcr * n_re + dci * n_im
        dlr = (dcr * (ar - 1.0) + dci * ai) * inv - 2.0 * lr * inv * inv * dinv
        dli = (dcr * ai - dci * (ar - 1.0)) * inv - 2.0 * li * inv * inv * dinv
        dmag = dar * cs + dai * sn
        dth = dai * ar - dar * ai
        dlr = dlr + dmag * mag * dt
        dli = dli + dth * dt
        ddt = dmag * mag * lr + dth * li
        return (dlr, dli, ddt * dt, db_re, db_im), ()
    return rowwise(fn, (lr, li, ldt, b_re, b_im, dar, dai, dbb_re, dbb_im),
                   ((1, F32), (1, F32), (1, F32), (SSM_GROUP, F32), (SSM_GROUP, F32)), name="s5_prep_bwd", tm=512)


def group_sum(x):
    def body(x_ref, o_ref):
        o_ref[...] = jnp.sum(x_ref[...], axis=1)
    return _pcall(body, name="s5_group_sum", out_shape=jax.ShapeDtypeStruct((N_GROUPS, 1), F32))(
        x.reshape(N_GROUPS, SSM_STATE, 1))


GROUPS_PER_TILE = LANES // SSM_GROUP
TILE_STATES = GROUPS_PER_TILE * SSM_STATE
N_UTILES = D_MODEL // LANES


SUBLANES = 8
SCAN_STRIP = 1024
N_STRIPS = N_STATES // SCAN_STRIP
_NT = (((1,), (1,)), ((), ()))
_TN = (((0,), (0,)), ((), ()))


def _scan_coefs(are, aim, shifted, reverse):
    ar = are[...]
    ai = -aim[...] if reverse else aim[...]
    powers = {1: (ar, ai)}
    for d in (2, 4):
        r, i = powers[d // 2]
        powers[d] = (r * r - i * i, 2.0 * r * i)
    rid = lax.broadcasted_iota(jnp.int32, (SUBLANES, N_STATES), 0)
    first = (rid == SUBLANES - 1) if reverse else (rid == 0)
    masks = [(1, first)] + [(d, (rid <= SUBLANES - 1 - d) if reverse else (rid >= d)) for d in (1, 2, 4)]
    for n, (d, keep) in enumerate(masks):
        for part in (0, 1):
            shifted[2 * n + part][...] = jnp.where(keep, jnp.broadcast_to(powers[d][part], (SUBLANES, N_STATES)), 0.0)


def _tile_scan(xr, xi, shifted, nbr_re, nbr_im, reverse):
    for n, d in enumerate((1, 1, 2, 4)):
        by = SUBLANES - d if reverse else d
        fr, fi = (nbr_re, nbr_im) if n == 0 else (xr, xi)
        sr, si = pltpu.roll(fr, by, 0), pltpu.roll(fi, by, 0)
        kr, ki = shifted[2 * n], shifted[2 * n + 1]
        xr, xi = xr + kr * sr - ki * si, xi + kr * si + ki * sr
    return xr, xi


def _tile_rows(t):
    return pl.ds(pl.multiple_of(t * SUBLANES, SUBLANES), SUBLANES)


def s5_fwd(u, bbd_re, bbd_im, cbd_re, cbd_imn, a_re, a_im, dskip, ride=None, t_rows=256):
    seq = u.shape[0]
    t_rows = min(t_rows, seq)
    n_tiles = t_rows // SUBLANES

    def body(u_ref, bre, bim, cre, cimn, are, aim, d_ref, y_ref, gelu_ref, hre_ref, him_ref, car_re, car_im, *shifted):
        @pl.when(pl.program_id(0) == 0)
        def _():
            car_re[...] = jnp.zeros_like(car_re)
            car_im[...] = jnp.zeros_like(car_im)
            _scan_coefs(are, aim, shifted, reverse=False)

        uf = u_ref[...]
        ub = uf.astype(BF16)
        for j in range(N_UTILES):
            uj = ub[:, LANES * j:LANES * (j + 1)]
            sl = slice(TILE_STATES * j, TILE_STATES * (j + 1))
            hre_ref[:, sl] = jnp.dot(uj, bre[j], preferred_element_type=F32)
            him_ref[:, sl] = jnp.dot(uj, bim[j], preferred_element_type=F32)
        for s in range(N_STRIPS):
            cols = pl.ds(s * SCAN_STRIP, SCAN_STRIP)
            coefs = [c[:, cols] for c in shifted]

            def step(t, before):
                rows = _tile_rows(t)
                hr, hi = _tile_scan(hre_ref[rows, cols], him_ref[rows, cols], coefs, before[0], before[1], False)
                hre_ref[rows, cols] = hr
                him_ref[rows, cols] = hi
                return hr, hi

            cr, ci = lax.fori_loop(0, n_tiles, step, (car_re[:, cols], car_im[:, cols]))
            car_re[:, cols] = cr
            car_im[:, cols] = ci
        dv = d_ref[...]
        for j in range(N_UTILES):
            st = slice(TILE_STATES * j, TILE_STATES * (j + 1))
            yj = (jnp.dot(hre_ref[:, st].astype(BF16), cre[j], preferred_element_type=F32)
                  + jnp.dot(him_ref[:, st].astype(BF16), cimn[j], preferred_element_type=F32))
            sl = slice(LANES * j, LANES * (j + 1))
            yj = yj + dv[:, sl] * uf[:, sl]
            y_ref[:, sl] = yj
            gelu_ref[:, sl] = _gelu(yj).astype(gelu_ref.dtype)

    full3 = lambda a: pl.BlockSpec(a.shape, lambda i: (0, 0, 0))
    full2 = lambda a: pl.BlockSpec(a.shape, lambda i: (0, 0))
    tile = pltpu.VMEM((SUBLANES, N_STATES), F32)
    n_chunks = seq // t_rows
    return _pcall_riding(
        body, (u, bbd_re, bbd_im, cbd_re, cbd_imn, a_re, a_im, dskip), ride,
        lambda: pl.program_id(0) == 0, lambda: pl.program_id(0) == n_chunks - 1,
        middle=(lambda: pl.program_id(0) == (7 * n_chunks) // 8) if ride is not None else None,
        name="s5_fwd", grid=(n_chunks,),
        in_specs=[pl.BlockSpec((t_rows, D_MODEL), lambda i: (i, 0)), full3(bbd_re), full3(bbd_im), full3(cbd_re),
                  full3(cbd_imn), full2(a_re), full2(a_im), full2(dskip)],
        out_specs=[pl.BlockSpec((t_rows, D_MODEL), lambda i: (i, 0)),
                   pl.BlockSpec((t_rows, D_MODEL), lambda i: (i, 0)),
                   pl.BlockSpec((t_rows, N_STATES), lambda i: (i, 0)),
                   pl.BlockSpec((t_rows, N_STATES), lambda i: (i, 0))],
        out_shape=[jax.ShapeDtypeStruct((seq, D_MODEL), F32),
                   jax.ShapeDtypeStruct((seq, D_MODEL), BF16),
                   jax.ShapeDtypeStruct((seq, N_STATES), F32),
                   jax.ShapeDtypeStruct((seq, N_STATES), F32)],
        scratch_shapes=[tile] * 10,
        compiler_params=_params(("arbitrary",)))


def s5_bwd(dy, u, dres, h_re, h_im, bbd_re, bbd_im, cbd_re, cbd_imn, a_re, a_im, dskip, ride=None, t_rows=256):
    seq = u.shape[0]
    t_rows = min(t_rows, seq)
    n_chunks = seq // t_rows

    n_tiles = t_rows // SUBLANES

    def body(dy_ref, u_ref, dres_ref, hre_ref, him_ref, hpre_ref, hpim_ref, bre, bim, cre, cimn, are, aim, d_ref,
             dx_ref, dbre, dbim, dcre, dcimn, dar_ref, dai_ref, dd_ref, lre, lim, car_re, car_im, acc_re, acc_im,
             *shifted):
        i = pl.program_id(0)

        @pl.when(i == 0)
        def _():
            for r in (car_re, car_im, acc_re, acc_im, dbre, dbim, dcre, dcimn, dd_ref):
                r[...] = jnp.zeros_like(r)
            _scan_coefs(are, aim, shifted, reverse=True)

        dyf = dy_ref[...]
        dyb = dyf.astype(BF16)
        uf = u_ref[...]
        ub = uf.astype(BF16)
        for j in range(N_UTILES):
            dyj = dyb[:, LANES * j:LANES * (j + 1)]
            st = slice(TILE_STATES * j, TILE_STATES * (j + 1))
            lre[:, st] = lax.dot_general(dyj, cre[j], _NT, preferred_element_type=F32)
            lim[:, st] = lax.dot_general(dyj, cimn[j], _NT, preferred_element_type=F32)
        has_pred = (i < n_chunks - 1).astype(F32)
        last_row = lax.broadcasted_iota(jnp.int32, (SUBLANES, SCAN_STRIP), 0) == SUBLANES - 1
        for s in range(N_STRIPS):
            cols = pl.ds(s * SCAN_STRIP, SCAN_STRIP)
            coefs = [c[:, cols] for c in shifted]
            before_re, before_im = hpre_ref[:, cols] * has_pred, hpim_ref[:, cols] * has_pred

            def step(k, carry):
                after_re, after_im, dar, dai = carry
                t = n_tiles - 1 - k
                rows = _tile_rows(t)
                lr, li = _tile_scan(lre[rows, cols], lim[rows, cols], coefs, after_re, after_im, True)
                lre[rows, cols] = lr
                lim[rows, cols] = li
                prev = _tile_rows(jnp.maximum(t - 1, 0))
                pre_re = jnp.where(t == 0, before_re, hre_ref[prev, cols])
                pre_im = jnp.where(t == 0, before_im, him_ref[prev, cols])
                hpr = pltpu.roll(jnp.where(last_row, pre_re, hre_ref[rows, cols]), 1, 0)
                hpi = pltpu.roll(jnp.where(last_row, pre_im, him_ref[rows, cols]), 1, 0)
                return lr, li, dar + lr * hpr + li * hpi, dai + li * hpr - lr * hpi

            cr, ci, dar, dai = lax.fori_loop(0, n_tiles, step, (car_re[:, cols], car_im[:, cols],
                                                               acc_re[:, cols], acc_im[:, cols]))
            car_re[:, cols] = cr
            car_im[:, cols] = ci
            acc_re[:, cols] = dar
            acc_im[:, cols] = dai

        dv = d_ref[...]
        for j in range(N_UTILES):
            sl = slice(LANES * j, LANES * (j + 1))
            st = slice(TILE_STATES * j, TILE_STATES * (j + 1))
            lrj = lre[:, st].astype(BF16)
            lij = lim[:, st].astype(BF16)
            du = (lax.dot_general(lrj, bre[j], _NT, preferred_element_type=F32)
                  + lax.dot_general(lij, bim[j], _NT, preferred_element_type=F32))
            dx_ref[:, sl] = du + dv[:, sl] * dyf[:, sl] + DN_ALPHA * dres_ref[:, sl]
            uj = ub[:, sl]
            dbre[j] += lax.dot_general(uj, lrj, _TN, preferred_element_type=F32)
            dbim[j] += lax.dot_general(uj, lij, _TN, preferred_element_type=F32)
            dyj = dyb[:, sl]
            dcre[j] += lax.dot_general(hre_ref[:, st].astype(BF16), dyj, _TN, preferred_element_type=F32)
            dcimn[j] += lax.dot_general(him_ref[:, st].astype(BF16), dyj, _TN, preferred_element_type=F32)
        dd_ref[...] += jnp.sum(dyf * uf, axis=0, keepdims=True)

        @pl.when(i == n_chunks - 1)
        def _():
            dar_ref[...] = jnp.sum(acc_re[...], axis=0, keepdims=True)
            dai_ref[...] = jnp.sum(acc_im[...], axis=0, keepdims=True)

    rev = lambda i: (n_chunks - 1 - i, 0)
    prev_tile = lambda i: (jnp.maximum((n_chunks - 1 - i) * n_tiles - 1, 0), 0)
    once = pl.Buffered(1)
    full3 = lambda a: pl.BlockSpec(a.shape, lambda i: (0, 0, 0), pipeline_mode=once)
    full2 = lambda a: pl.BlockSpec(a.shape, lambda i: (0, 0), pipeline_mode=once)
    acc3 = lambda shape: pl.BlockSpec(shape, lambda i: (0, 0, 0))
    acc2 = lambda shape: pl.BlockSpec(shape, lambda i: (0, 0))
    tile = pltpu.VMEM((SUBLANES, N_STATES), F32)
    return _pcall_riding(
        body, (dy, u, dres, h_re, h_im, h_re, h_im, bbd_re, bbd_im, cbd_re, cbd_imn, a_re, a_im, dskip), ride,
        lambda: pl.program_id(0) == 0, lambda: pl.program_id(0) == n_chunks - 1,
        name="s5_bwd", grid=(n_chunks,),
        in_specs=[pl.BlockSpec((t_rows, D_MODEL), rev), pl.BlockSpec((t_rows, D_MODEL), rev),
                  pl.BlockSpec((t_rows, D_MODEL), rev),
                  pl.BlockSpec((t_rows, N_STATES), rev), pl.BlockSpec((t_rows, N_STATES), rev),
                  pl.BlockSpec((SUBLANES, N_STATES), prev_tile), pl.BlockSpec((SUBLANES, N_STATES), prev_tile),
                  full3(bbd_re), full3(bbd_im), full3(cbd_re), full3(cbd_imn), full2(a_re), full2(a_im), full2(dskip)],
        out_specs=[pl.BlockSpec((t_rows, D_MODEL), rev), acc3(bbd_re.shape), acc3(bbd_im.shape), acc3(cbd_re.shape),
                   acc3(cbd_imn.shape), acc2((1, N_STATES)), acc2((1, N_STATES)), acc2((1, D_MODEL))],
        out_shape=[jax.ShapeDtypeStruct((seq, D_MODEL), F32), jax.ShapeDtypeStruct(bbd_re.shape, F32),
                   jax.ShapeDtypeStruct(bbd_im.shape, F32), jax.ShapeDtypeStruct(cbd_re.shape, F32),
                   jax.ShapeDtypeStruct(cbd_imn.shape, F32), jax.ShapeDtypeStruct((1, N_STATES), F32),
                   jax.ShapeDtypeStruct((1, N_STATES), F32), jax.ShapeDtypeStruct((1, D_MODEL), F32)],
        scratch_shapes=[pltpu.VMEM((t_rows, N_STATES), F32), pltpu.VMEM((t_rows, N_STATES), F32)] + [tile] * 12,
        compiler_params=_params(("arbitrary",)))


def _eye_groups():
    return jnp.eye(GROUPS_PER_TILE, dtype=F32)


def _blockdiag_in(bb):
    t = bb.transpose(0, 2, 1).reshape(N_UTILES, GROUPS_PER_TILE, SSM_GROUP, SSM_STATE)
    bd = jnp.einsum("jgcp,gh->jgchp", t, _eye_groups())
    return bd.reshape(N_UTILES, LANES, TILE_STATES)


def _blockdiag_in_t(d):
    t = jnp.einsum("jgchp,gh->jgcp", d.reshape(N_UTILES, GROUPS_PER_TILE, SSM_GROUP, GROUPS_PER_TILE, SSM_STATE),
                   _eye_groups())
    return t.reshape(N_GROUPS, SSM_GROUP, SSM_STATE).transpose(0, 2, 1)


def _blockdiag_out(c):
    t = c.transpose(0, 2, 1).reshape(N_UTILES, GROUPS_PER_TILE, SSM_STATE, SSM_GROUP)
    bd = jnp.einsum("jhpc,hg->jhpgc", t, _eye_groups())
    return bd.reshape(N_UTILES, TILE_STATES, LANES)


def _blockdiag_out_t(d):
    t = jnp.einsum("jhpgc,hg->jhpc", d.reshape(N_UTILES, GROUPS_PER_TILE, SSM_STATE, GROUPS_PER_TILE, SSM_GROUP),
                   _eye_groups())
    return t.reshape(N_GROUPS, SSM_STATE, SSM_GROUP).transpose(0, 2, 1)


ATT_TQ = 512
ATT_TK = 512
LOG2E = math.log2(math.e)
LN2 = math.log(2.0)
Q_PRESCALE = SM_SCALE * LOG2E


def _loop_in_pairs(n, step, carry, start=0):
    pairs = (n - start) // 2

    def two(t, c):
        return step(start + 2 * t + 1, step(start + 2 * t, c))

    carry = lax.fori_loop(0, pairs, two, carry)
    return lax.fori_loop(start + 2 * pairs, n, step, carry)


def _causal(s, transposed=False):
    r = lax.broadcasted_iota(jnp.int32, s.shape, 0)
    c = lax.broadcasted_iota(jnp.int32, s.shape, 1)
    return jnp.where((r <= c) if transposed else (c <= r), s, NEG_INF)


def _q_specs(rows, at):
    def nope(*ids):
        r, h = at(*ids)
        return r, 3 * (h // HEADS_PER_CHIP) + h % HEADS_PER_CHIP

    def rope(*ids):
        r, h = at(*ids)
        return r, 3 * (h // HEADS_PER_CHIP) + HEADS_PER_CHIP

    return [pl.BlockSpec((rows, LANES), nope), pl.BlockSpec((rows, LANES), rope)]


def _kv_specs(rows, at):
    def col(f):
        def index(*ids):
            r, h = at(*ids)
            return r, f(h)
        return index

    return [pl.BlockSpec((rows, LANES), col(lambda h: 2 * h)), pl.BlockSpec((rows, LANES), col(lambda h: h % HEADS_PER_CHIP)),
            pl.BlockSpec((rows, LANES), col(lambda h: 2 * h + 1))]


def _cat(a, b):
    return jnp.concatenate([a, b], axis=1)


def attn_fwd(q, kv, kr, ride=None, tq=ATT_TQ, tk=ATT_TK):
    seq = q.shape[0]
    n_heads = N_HEADS
    tq, tk = min(tq, seq), min(tk, seq)
    assert tq == tk

    def body(qn_ref, qr_ref, kn_ref, kr_ref, v_ref, o_ref, lse_ref):
        qi = pl.program_id(1)
        qv = _cat(qn_ref[...], qr_ref[...])
        jd = qi

        def block(j, carry, diag):
            m, l, acc = carry
            rows = pl.ds(pl.multiple_of(j * tk, tk), tk)
            s = lax.dot_general(qv, _cat(kn_ref[rows, :], kr_ref[rows, :]), _NT, preferred_element_type=F32)
            if diag:
                s = _causal(s)
            m_new = jnp.maximum(m, jnp.max(s, axis=-1, keepdims=True))
            p = jnp.exp2(s - m_new)
            corr = jnp.exp2(m - m_new)
            l = l * corr + jnp.sum(p, axis=-1, keepdims=True)
            acc = acc * corr + jnp.dot(p.astype(BF16), v_ref[rows, :], preferred_element_type=F32)
            return m_new, l, acc

        init = (jnp.full((tq, 1), NEG_INF, F32), jnp.zeros((tq, 1), F32), jnp.zeros((tq, V_HEAD), F32))
        carry = _loop_in_pairs(jd, lambda j, c: block(j, c, False), init)
        m, l, acc = block(jd, carry, True)
        o_ref[...] = acc / l
        lse_ref[...] = jnp.transpose(jnp.broadcast_to(m + jnp.log2(l), (tq, LANES)))[:1, :]

    n_q = seq // tq
    return _pcall_riding(
        body, (q, q, kv, kr, kv), ride,
        lambda: (pl.program_id(0) == 0) & (pl.program_id(1) == 0),
        lambda: (pl.program_id(0) == n_heads - 1) & (pl.program_id(1) == n_q - 1),
        middle=(lambda: (pl.program_id(0) == (5 * n_heads) // 8) & (pl.program_id(1) == 0)) if ride is not None else None,
        name="attn_fwd", grid=(n_heads, n_q),
        in_specs=_q_specs(tq, lambda h, i: (i, h)) + _kv_specs(seq, lambda h, i: (0, h)),
        out_specs=[pl.BlockSpec((tq, V_HEAD), lambda h, i: (i, h)),
                   pl.BlockSpec((None, None, 1, tq), lambda h, i: (h, i, 0, 0))],
        out_shape=[jax.ShapeDtypeStruct((seq, n_heads * V_HEAD), F32),
                   jax.ShapeDtypeStruct((n_heads, n_q, 1, tq), F32)],
        compiler_params=_params(("arbitrary", "arbitrary")))


def attn_bwd(q, kv, kr, do, lse_row, delta_row, tq=ATT_TK):
    seq = q.shape[0]
    tq = min(tq, seq)
    n_blk = seq // tq

    def body(qn_ref, qr_ref, kn_ref, kr_ref, v_ref, do_ref, lse_ref, delta_ref, dqn_ref, dqr_ref, dkv_ref, dkr_ref, dq_acc):
        head, kj = pl.program_id(0), pl.program_id(1)

        @pl.when(kj == 0)
        def _():
            dq_acc[...] = jnp.zeros_like(dq_acc)

        kc = _cat(kn_ref[...], kr_ref[...])
        vv = v_ref[...]

        def block(i, carry, diag):
            dk, dv = carry
            rows = pl.ds(pl.multiple_of(i * tq, tq), tq)
            qv = _cat(qn_ref[rows, :], qr_ref[rows, :])
            st = lax.dot_general(kc, qv, _NT, preferred_element_type=F32)
            if diag:
                st = _causal(st, transposed=True)
            pt = jnp.exp2(st - lse_ref[0, pl.ds(i, 1), :])
            dob = do_ref[rows, :].astype(BF16)
            dv = dv + jnp.dot(pt.astype(BF16), dob, preferred_element_type=F32)
            dpt = lax.dot_general(vv, dob, _NT, preferred_element_type=F32)
            dst = (pt * (dpt - delta_ref[0, pl.ds(i, 1), :])).astype(BF16)
            dk = dk + jnp.dot(dst, qv, preferred_element_type=F32)
            dq_acc[rows, :] += lax.dot_general(dst, kc, _TN, preferred_element_type=F32)
            return dk, dv

        carry = block(kj, (jnp.zeros((tq, 2 * LANES), F32), jnp.zeros((tq, V_HEAD), F32)), True)
        dk, dv = _loop_in_pairs(n_blk, lambda i, c: block(i, c, False), carry, start=kj + 1)
        dk = dk * LN2
        dkv_ref[...] = _cat(dk[:, :LANES], dv).astype(dkv_ref.dtype)
        lane = lax.broadcasted_iota(jnp.int32, (tq, LANES), 1)
        mine = (lane // HALF_ROPE) % HEADS_PER_CHIP == head % HEADS_PER_CHIP
        dkr_ref[0] = jnp.where(mine, dk[:, LANES:], 0.0)

        @pl.when(kj == n_blk - 1)
        def _():
            dqn_ref[...] = dq_acc[:, :LANES] * SM_SCALE

        @pl.when((kj == n_blk - 1) & (head % HEADS_PER_CHIP == 0))
        def _():
            dqr_ref[...] = dq_acc[:, LANES:] * SM_SCALE

        @pl.when((kj == n_blk - 1) & (head % HEADS_PER_CHIP > 0))
        def _():
            dqr_ref[...] += dq_acc[:, LANES:] * SM_SCALE

    return _pcall(
        body, name="attn_bwd", grid=(N_HEADS, n_blk),
        in_specs=_q_specs(seq, lambda h, j: (0, h)) + _kv_specs(tq, lambda h, j: (j, h))
        + [pl.BlockSpec((seq, V_HEAD), lambda h, j: (0, h)),
           pl.BlockSpec((1, n_blk, tq), lambda h, j: (h, 0, 0)),
           pl.BlockSpec((1, n_blk, tq), lambda h, j: (h, 0, 0))],
        out_specs=[pl.BlockSpec((seq, LANES), lambda h, j: (0, h)),
                   pl.BlockSpec((seq, LANES), lambda h, j: (0, h // HEADS_PER_CHIP)),
                   pl.BlockSpec((tq, QK_NOPE + V_HEAD), lambda h, j: (j, h)),
                   pl.BlockSpec((1, tq, LANES), lambda h, j: (h, j, 0))],
        out_shape=[jax.ShapeDtypeStruct((seq, N_HEADS * QK_NOPE), F32),
                   jax.ShapeDtypeStruct((seq, N_CHIPS * LANES), F32),
                   jax.ShapeDtypeStruct((seq, N_HEADS * (QK_NOPE + V_HEAD)), BF16),
                   jax.ShapeDtypeStruct((N_HEADS, seq, LANES), F32)],
        scratch_shapes=[pltpu.VMEM((seq, 2 * LANES), F32)],
        compiler_params=_params(("arbitrary", "arbitrary")),
    )(q, q, kv, kr, kv, do, lse_row, delta_row)


def head_sum(x, ts=512):
    n_heads, seq, w = x.shape
    ts = min(ts, seq)

    def body(x_ref, o_ref):
        o_ref[...] = jnp.sum(x_ref[...], axis=0)

    return _pcall(body, name="head_sum", grid=(seq // ts,),
                  in_specs=[pl.BlockSpec((n_heads, ts, w), lambda i: (0, i, 0))],
                  out_specs=pl.BlockSpec((ts, w), lambda i: (i, 0)),
                  out_shape=jax.ShapeDtypeStruct((seq, w), F32),
                  compiler_params=_params(("parallel",)))(x)


HEADS_PER_CHIP = N_HEADS // N_CHIPS
Q_CHIP = HEADS_PER_CHIP * QK_DIM
Q_CHIP_NOPE = HEADS_PER_CHIP * QK_NOPE


def _perm_q_cols(w):
    t = w.reshape(w.shape[0], HEADS_PER_CHIP, QK_DIM)
    return jnp.concatenate([t[:, :, :QK_NOPE].reshape(w.shape[0], -1),
                            t[:, :, QK_NOPE:QK_NOPE + HALF_ROPE].reshape(w.shape[0], -1),
                            t[:, :, QK_NOPE + HALF_ROPE:].reshape(w.shape[0], -1)], axis=1)


def _unperm_q_cols(w):
    r = w.shape[0]
    nope = w[:, :Q_CHIP_NOPE].reshape(r, HEADS_PER_CHIP, QK_NOPE)
    r1 = w[:, Q_CHIP_NOPE:Q_CHIP_NOPE + QK_ROPE].reshape(r, HEADS_PER_CHIP, HALF_ROPE)
    r2 = w[:, Q_CHIP_NOPE + QK_ROPE:].reshape(r, HEADS_PER_CHIP, HALF_ROPE)
    return jnp.concatenate([nope, r1, r2], axis=2).reshape(r, Q_CHIP)


def _pad_kva_cols(w):
    z = jnp.zeros((w.shape[0], HALF_ROPE), w.dtype)
    return jnp.concatenate([w[:, :KV_LORA], w[:, KV_LORA:KV_LORA + HALF_ROPE], z, w[:, KV_LORA + HALF_ROPE:], z], axis=1)


def _unpad_kva_cols(w):
    return jnp.concatenate([w[:, :KV_LORA], w[:, KV_LORA:KV_LORA + HALF_ROPE],
                            w[:, KV_LORA + QK_ROPE:KV_LORA + QK_ROPE + HALF_ROPE]], axis=1)


def _rope_tile(t, cs, sn):
    return t * cs + pltpu.roll(t, LANES // 2, 1) * sn


def _rope_tile_bwd(d, cs, sn):
    return d * cs + pltpu.roll(d * sn, LANES // 2, 1)


def _b_cols(tk, tn):
    return pl.BlockSpec((None, tk, tn), lambda i, j, k: (j, k, 0))


def _b_cols_t(tk, tn):
    return pl.BlockSpec((None, tn, tk), lambda i, j, k: (k, j, 0))


def _out_cols(shape):
    return shape, lambda tm, tn: pl.BlockSpec((None, tm, tn), lambda i, j, k: (j, i, 0))


def glu_proj(y, w_glu, tm=1024):
    seq, k_dim = y.shape
    tn = w_glu.shape[2]
    tm = min(tm, seq)
    half = N_CHIPS // 2

    def body(y_ref, wv_ref, wg_ref, val_ref, gate_ref, z_ref):
        yv = y_ref[...]
        v = jnp.dot(yv, wv_ref[...], preferred_element_type=F32)
        gt = jnp.dot(yv, wg_ref[...], preferred_element_type=F32)
        val_ref[...] = v
        gate_ref[...] = gt
        z_ref[...] = (v * _sigmoid(gt)).astype(z_ref.dtype)

    tile = pl.BlockSpec((tm, tn), lambda i, j: (i, j))
    return _pcall(
        body, name="glu_proj", grid=(seq // tm, half),
        in_specs=[pl.BlockSpec((tm, k_dim), lambda i, j: (i, 0)),
                  pl.BlockSpec((None, k_dim, tn), lambda i, j: (j, 0, 0)),
                  pl.BlockSpec((None, k_dim, tn), lambda i, j: (j + half, 0, 0))],
        out_specs=[tile, tile, tile],
        out_shape=[jax.ShapeDtypeStruct((seq, half * tn), F32), jax.ShapeDtypeStruct((seq, half * tn), F32),
                   jax.ShapeDtypeStruct((seq, half * tn), BF16)],
        compiler_params=_params(("parallel", "parallel")),
    )(y, w_glu, w_glu)


def _halves(a):
    return a.reshape(N_CHIPS, 2, a.shape[1] // 2, a.shape[2])


def device_step(x, positions, target, w, comm=None):
    seq = x.shape[0]
    w = dict(w)

    def gathered(names, outs):
        for n, a in zip(names, outs):
            if isinstance(n, tuple):
                w[n[0]] = [a.reshape(v.shape) if l == n[1] else v for l, v in enumerate(w[n[0]])]
            else:
                w[n] = a.reshape(w[n].shape)

    def ride_for(names):
        if comm is None:
            return None
        return GatherRide([_halves(w[n[0]][n[1]] if isinstance(n, tuple) else w[n]) for n in names])

    first_ride = ("ssm_w_glu", "ssm_w_out", ("w_ff1", 0), ("w_ff2", 0))
    mla_ride = ("kv_w_a", "kv_w_b", "q_w_a", "q_w_b", "attn_w_o")
    second_ride = (("w_ff1", 1), ("w_ff2", 1))

    inv_freq = ROPE_THETA ** (-jnp.arange(HALF_ROPE, dtype=F32) / HALF_ROPE)
    ang = positions.astype(F32)[:, None] * inv_freq
    cos, sin = jnp.cos(ang), jnp.sin(ang)
    zero = jnp.zeros_like(cos)
    cos_q, sin_q = jnp.concatenate([cos] * 4, 1), jnp.concatenate([-sin, -sin, sin, sin], 1)
    cos_k, sin_k = jnp.concatenate([cos, zero, cos, zero], 1), jnp.concatenate([-sin, zero, sin, zero], 1)
    ff_tile = D_FF // N_CHIPS
    pack_shape = (N_CHIPS, EARLY_ROWS, PACK_W)

    lr = w["ssm_lam_re"].reshape(N_STATES, 1)
    li = w["ssm_lam_im"].reshape(N_STATES, 1)
    ldt = jnp.repeat(w["ssm_log_dt"].reshape(N_GROUPS), SSM_STATE).reshape(N_STATES, 1)
    b_re = w["ssm_b_re"].reshape(N_STATES, SSM_GROUP)
    b_im = w["ssm_b_im"].reshape(N_STATES, SSM_GROUP)
    a_re, a_im, bb_re, bb_im = s5_prep(lr, li, ldt, b_re, b_im)
    a_re, a_im = a_re.reshape(1, N_STATES), a_im.reshape(1, N_STATES)
    bbd_re = _blockdiag_in(bb_re.reshape(N_GROUPS, SSM_STATE, SSM_GROUP)).astype(BF16)
    bbd_im = _blockdiag_in(bb_im.reshape(N_GROUPS, SSM_STATE, SSM_GROUP)).astype(BF16)
    cbd_re = _blockdiag_out(w["ssm_c_re"].reshape(N_GROUPS, SSM_GROUP, SSM_STATE)).astype(BF16)
    cbd_imn = _blockdiag_out(-w["ssm_c_im"].reshape(N_GROUPS, SSM_GROUP, SSM_STATE)).astype(BF16)
    dskip = w["ssm_d"].reshape(1, D_MODEL)
    (ypre, yg, h_re, h_im), landed = s5_fwd(x, bbd_re, bbd_im, cbd_re, cbd_imn, a_re, a_im, dskip, ride_for(first_ride))
    gathered(first_ride, landed)
    w_glu = w["ssm_w_glu"]
    glu_tile = w_glu.shape[2]
    val, gate, z = glu_proj(yg, w_glu)
    w_out = w["ssm_w_out"].reshape(D_MODEL, D_MODEL)
    ln = lambda name, l: w[name][l].reshape(1, D_MODEL)

    def then_ln(h, names, layer):
        def epi(r, hv, gl, bl):
            y = _layer_norm(hv, r, gl, bl)
            return r, y, y
        return dict(epi=epi, extras=(h, ln(names[0], layer), ln(names[1], layer)), out_dtypes=(F32, F32, BF16))

    mix0, h1, h1b = mm(z, w_out, name="ssm_out", **then_ln(x, ("ln_mix_g", "ln_mix_b"), 0))

    def mlp_fwd(h, hb, layer, riding=None, with_ln=True):
        pre = mm(hb, w["w_ff1"][layer], n_dim=D_FF, tiles=(None, ff_tile, None), b_view=_b_cols, name=f"ff1_{layer}",
                 out_dtypes=(BF16,), ride=ride_for(riding) if riding else None)
        if riding and comm is not None:
            pre, landed = pre
            gathered(riding, landed)
        post = then_ln(h, ("ln_ffn_g", "ln_ffn_b"), layer) if with_ln else {}
        return pre, mm(pre, w["w_ff2"][layer].reshape(D_FF, D_MODEL), pro_a=_relu2, name=f"ff2_{layer}", **post)

    f1pre, (f1, h2, h2b) = mlp_fwd(h1, h1b, 0, mla_ride)

    kv_w_a = w["kv_w_a"].reshape(D_MODEL, KVA_PAD)
    kv_w_b = w["kv_w_b"]
    q_w_a = w["q_w_a"].reshape(D_MODEL, Q_LORA)
    q_w_b = w["q_w_b"]
    w_o = w["attn_w_o"].reshape(D_MODEL, D_MODEL)
    kvb_tile = kv_w_b.shape[2]
    kvn_g = w["kv_norm_g"].reshape(1, KV_LORA)
    qn_g = w["q_norm_g"].reshape(1, Q_LORA)
    kva = mm(h2b, kv_w_a, name="kv_a")

    def kv_post(kva, g, cs, sn):
        tile = _rope_tile(kva[:, KV_LORA:], cs, sn)
        return (_rms(kva[:, :KV_LORA], g), _cat(tile, pltpu.roll(tile, HALF_ROPE, 1))), ()
    ckv, krope = rowwise(kv_post, (kva, kvn_g, cos_k, sin_k), ((KV_LORA, BF16), (2 * LANES, BF16)), name="kv_post")
    kvb = mm(ckv, kv_w_b, n_dim=N_CHIPS * kvb_tile, tiles=(None, kvb_tile, KV_LORA), b_view=_b_cols, name="kv_b",
             out_dtypes=(BF16,))
    cq_raw, cq = mm(h2b, q_w_a, epi=lambda r, gq: (r, _rms(r, gq)), extras=(qn_g,), out_dtypes=(F32, BF16), name="q_a")

    def rope_and_scale(r, cs, sn):
        return (_cat(r[:, :Q_CHIP_NOPE], _rope_tile(r[:, Q_CHIP_NOPE:], cs, sn)) * Q_PRESCALE,)
    qro = mm(cq, q_w_b, n_dim=N_CHIPS * Q_CHIP, tiles=(None, Q_CHIP, Q_LORA), b_view=_b_cols, epi=rope_and_scale,
             extras=(cos_q, sin_q), out_dtypes=(BF16,), name="q_b")
    (o, lse), landed = attn_fwd(qro, kvb, krope, ride_for(second_ride))
    gathered(second_ride, landed)
    mix1, h3, h3b = mm(o, w_o, name="attn_out", **then_ln(h2, ("ln_mix_g", "ln_mix_b"), 1))
    f2pre, f2 = mlp_fwd(h3, h3b, 1, with_ln=False)
    def last_ln_loss_and_back(h, mix, gl, bl, t):
        e = _layer_norm(h, mix, gl, bl) - t
        dr, dg, db = _layer_norm_bwd(h, mix, gl, e * (1.0 / D_MODEL))
        return (dr, dr), (jnp.broadcast_to(jnp.sum(e * e), (1, LANES)), dg, db)
    dr4, dr4b, loss_acc, dg_f1, db_f1 = rowwise(
        last_ln_loss_and_back, (h3, f2, ln("ln_ffn_g", 1), ln("ln_ffn_b", 1), target),
        ((D_MODEL, F32), (D_MODEL, BF16)), accs=(LANES, D_MODEL, D_MODEL), name="ln_ffn_1_loss")
    loss = loss_acc[0, 0] * (0.5 / D_MODEL)

    g = {}

    def into_rows(off, rows_per_chip, shape=pack_shape):
        def view(tm, tn):
            if tm == N_CHIPS * rows_per_chip:
                return pl.BlockSpec((N_CHIPS, rows_per_chip, tn), lambda i, j, k: (0, off // rows_per_chip, 0))
            nb = rows_per_chip // tm
            return pl.BlockSpec((None, tm, tn), lambda i, j, k: (i // nb, off // tm + i % nb, 0))
        return shape, view

    def into_cols(off):
        return pack_shape, lambda tm, tn: pl.BlockSpec((None, tm, tn), lambda i, j, k: (j, off // tm + i, 0))

    def mlp_bwd(pack, dr, drb, hb, pre, layer, swap=False):
        w2_rows = (EARLY_OFF["w_ff2"] + layer * ff_tile, ff_tile)
        w1_rows = (EARLY_OFF["w_ff1"] + layer * D_MODEL, D_MODEL)
        ready = [(w1_rows[0] + w1_rows[1], w2_rows[0] - w1_rows[0] - w1_rows[1]), (w2_rows[0] + w2_rows[1], EARLY_ROWS - w2_rows[0] - w2_rows[1])]
        dpre = mm(drb, w["w_ff2"][layer].reshape(D_FF, D_MODEL), tb=True, epi=lambda r, p: (r * 2.0 * jnp.maximum(p, 0.0),),
                  extras=(pre,), out_dtypes=(BF16,), tiles=(None, ff_tile, None), name=f"ff2_dx_{layer}",
                  ride=SwapRide(pack, ready) if swap else None)
        if swap:
            dpre, (theirs,) = dpre
        pack = mm(pre, drb, ta=True, pro_a=_relu2, name=f"ff2_dw_{layer}", tiles=(ff_tile, PACK_W, None), into=pack,
                  out_view=into_rows(w2_rows[0], ff_tile))
        pack = mm(hb, dpre, ta=True, name=f"ff1_dw_{layer}", tiles=(None, PACK_W, None), into=pack,
                  out_view=into_cols(w1_rows[0]))
        dh = mm(dpre, w["w_ff1"][layer], tb=True, epi=lambda r, d: (r + DN_ALPHA * d,), extras=(dr,), n_dim=D_MODEL,
                tiles=(None, D_MODEL, ff_tile), b_view=_b_cols_t, name=f"ff1_dx_{layer}",
                ride=SwapRide(pack, [w1_rows, w2_rows], into=theirs) if swap else None)
        return (pack, *dh) if swap else (pack, dh)

    pack, dh3 = mlp_bwd(None, dr4, dr4b, h3b, f2pre, 1)
    dr3, dr3b, dg_m1, db_m1 = ln_bwd(h2, mix1, ln("ln_mix_g", 1), dh3, "ln_mix_bwd_1")
    shard_rows = D_MODEL // N_CHIPS
    pack = mm(o, dr3b, ta=True, name="attn_out_dw", tiles=(D_MODEL, PACK_W, None), into=pack,
              out_view=into_rows(EARLY_OFF["attn_w_o"], shard_rows))
    do = mm(dr3b, w_o, tb=True, name="attn_out_dx")
    def head_dots(do, o):
        return (jnp.concatenate([jnp.sum(do[:, V_HEAD * h:V_HEAD * (h + 1)] * o[:, V_HEAD * h:V_HEAD * (h + 1)], axis=1,
                                         keepdims=True) for h in range(N_HEADS)], axis=1),), ()
    (delta,) = rowwise(head_dots, (do, o), ((N_HEADS, F32),), name="attn_delta")
    tb = min(ATT_TK, seq)
    lse_row = lse.reshape(N_HEADS, seq // tb, tb)
    delta_row = delta.T.reshape(N_HEADS, seq // tb, tb)
    dqn, dqr, dkvb, dkr = attn_bwd(qro, kvb, krope, do, lse_row, delta_row)

    def q_rope_bwd(dn, dr, cs, sn):
        parts = []
        for k in range(N_CHIPS):
            parts.append(dn[:, Q_CHIP_NOPE * k:Q_CHIP_NOPE * (k + 1)])
            parts.append(_rope_tile_bwd(dr[:, LANES * k:LANES * (k + 1)], cs, sn))
        return (jnp.concatenate(parts, axis=1),), ()
    (dqlin,) = rowwise(q_rope_bwd, (dqn, dqr, cos_q, sin_q), ((N_CHIPS * Q_CHIP, BF16),), name="q_rope_bwd")
    g["q_w_b"] = mm(cq, dqlin, ta=True, name="q_b_dw", tiles=(Q_LORA, Q_CHIP, None), out_view=_out_cols(q_w_b.shape))
    dcq = mm(dqlin, q_w_b, tb=True, n_dim=Q_LORA, tiles=(None, Q_LORA, Q_CHIP), b_view=_b_cols_t, name="q_b_dx")

    def q_norm_bwd(c, gq, d):
        dx, dgq = _rms_bwd(c, gq, d)
        return (dx,), (dgq,)
    dcq_raw, dqn_g = rowwise(q_norm_bwd, (cq_raw, qn_g, dcq), ((Q_LORA, BF16),), accs=(Q_LORA,), name="q_norm_bwd")
    g["q_w_a"] = mm(h2b, dcq_raw, ta=True, name="q_a_dw")
    g["kv_w_b"] = mm(ckv, dkvb, ta=True, name="kv_b_dw", tiles=(KV_LORA, kvb_tile, None), out_view=_out_cols(kv_w_b.shape))
    dckv = mm(dkvb, kv_w_b, tb=True, n_dim=KV_LORA, tiles=(None, KV_LORA, kvb_tile), b_view=_b_cols_t, name="kv_b_dx")
    dkr_sum = head_sum(dkr)

    def kv_post_bwd(kva, gk, dc, dk, cs, sn):
        dx, dgk = _rms_bwd(kva[:, :KV_LORA], gk, dc)
        dk = dk + pltpu.roll(dk, LANES - HALF_ROPE, 1)
        return (jnp.concatenate([dx, _rope_tile_bwd(dk, cs, sn)], axis=1),), (dgk,)
    dkva, dkvn_g = rowwise(kv_post_bwd, (kva, kvn_g, dckv, dkr_sum, cos_k, sin_k), ((KVA_PAD, BF16),),
                           accs=(KV_LORA,), name="kv_post_bwd")
    g["kv_w_a"] = mm(h2b, dkva, ta=True, name="kv_a_dw")
    dh2 = mm(dcq_raw, q_w_a, tb=True, epi=lambda r, d: (r + DN_ALPHA * d,), extras=(dr3,), name="q_a_dx")
    dh2 = mm(dkva, kv_w_a, tb=True, epi=lambda r, d: (r + d,), extras=(dh2,), name="kv_a_dx")

    dr2, dr2b, dg_f0, db_f0 = ln_bwd(h1, f1, ln("ln_ffn_g", 0), dh2, "ln_ffn_bwd_0")
    pack = put_rows(pack, packed_shards(g, MISC_EARLY, EARLY_ROWS - MISC_EARLY_OFF), MISC_EARLY_OFF)
    if comm is None:
        pack, dh1 = mlp_bwd(pack, dr2, dr2b, h1b, f1pre, 0)
    else:
        pack, dh1, (theirs,) = mlp_bwd(pack, dr2, dr2b, h1b, f1pre, 0, swap=True)
        early_sums = add_halves(pack, theirs, comm[1])
    dr1, dr1b, dg_m0, db_m0 = ln_bwd(x, mix0, ln("ln_mix_g", 0), dh1, "ln_mix_bwd_0")
    mid = mm(z, dr1b, ta=True, name="ssm_out_dw", tiles=(D_MODEL, PACK_W, None),
             out_view=into_rows(MID_OFF["ssm_w_out"], shard_rows, (N_CHIPS, MID_ROWS, PACK_W)))
    dz = mm(dr1b, w_out, tb=True, name="ssm_out_dx")

    def glu_bwd(vl, gt, dz):
        sg = _sigmoid(gt)
        return (jnp.concatenate([dz * sg, dz * vl * sg * (1.0 - sg)], axis=1),), ()
    (dvg,) = rowwise(glu_bwd, (val, gate, dz), ((2 * D_MODEL, BF16),), name="glu_bwd")
    g["ssm_w_glu"] = mm(yg, dvg, ta=True, name="glu_proj_dw", tiles=(None, glu_tile, None), out_view=_out_cols(w_glu.shape))
    mid = put_rows(mid, packed_shards(g, MISC_MID, MID_ROWS - MISC_MID_OFF), MISC_MID_OFF)
    dypre = mm(dvg, w_glu, tb=True, epi=lambda r, y: (r * _gelu_grad(y),), extras=(ypre,), n_dim=D_MODEL,
               tiles=(None, D_MODEL, glu_tile), b_view=_b_cols_t, name="glu_proj_dx",
               ride=Together([SwapRide(mid), SendRide([(early_sums, (0, EARLY_HEAD), None)])]) if comm is not None else None)
    sends = None
    if comm is not None:
        dypre, (theirs, early_got) = dypre
        sends = SendRide([(early_sums, (EARLY_HEAD, EARLY_ROWS - EARLY_HEAD), early_got), add_halves(mid, theirs, comm[1])])
    (dx, dbbd_re, dbbd_im, dcbd_re, dcbd_imn, dar, dai, dd), got = s5_bwd(
        dypre, x, dr1, h_re, h_im, bbd_re, bbd_im, cbd_re, cbd_imn, a_re, a_im, dskip, sends)
    dbb_re = _blockdiag_in_t(dbbd_re).reshape(N_STATES, SSM_GROUP)
    dbb_im = _blockdiag_in_t(dbbd_im).reshape(N_STATES, SSM_GROUP)
    dlr, dli, dldt, db_re, db_im = s5_prep_bwd(lr, li, ldt, b_re, b_im, dar.reshape(N_STATES, 1),
                                               dai.reshape(N_STATES, 1), dbb_re, dbb_im)
    g["ssm_lam_re"] = dlr.reshape(1, N_GROUPS, SSM_STATE)
    g["ssm_lam_im"] = dli.reshape(1, N_GROUPS, SSM_STATE)
    g["ssm_log_dt"] = group_sum(dldt).reshape(1, N_GROUPS)
    g["ssm_b_re"] = db_re.reshape(1, N_GROUPS, SSM_STATE, SSM_GROUP)
    g["ssm_b_im"] = db_im.reshape(1, N_GROUPS, SSM_STATE, SSM_GROUP)
    g["ssm_c_re"] = _blockdiag_out_t(dcbd_re).reshape(1, N_GROUPS, SSM_GROUP, SSM_STATE)
    g["ssm_c_im"] = -_blockdiag_out_t(dcbd_imn).reshape(1, N_GROUPS, SSM_GROUP, SSM_STATE)
    g["ssm_d"] = dd
    g["ln_mix_g"] = jnp.concatenate([dg_m0, dg_m1], 0)
    g["ln_mix_b"] = jnp.concatenate([db_m0, db_m1], 0)
    g["ln_ffn_g"] = jnp.concatenate([dg_f0, dg_f1], 0)
    g["ln_ffn_b"] = jnp.concatenate([db_f0, db_f1], 0)
    g["kv_norm_g"] = dkvn_g.reshape(KV_LORA)
    g["q_norm_g"] = dqn_g
    return loss, dx, pack, mid, g, list(zip(sends.ins, got)) if comm is not None else None


def place(shard, me_idx, dtype, name, layer=None):
    rows, cols = shard.shape[-2:]
    tr = _tile(rows, (512, 256, 128))

    def body(m_ref, x_ref, o_ref):
        o_ref[...] = x_ref[...].astype(o_ref.dtype)

    in_spec = (pl.BlockSpec((tr, cols), lambda i, m: (i, 0)) if layer is None
               else pl.BlockSpec((None, tr, cols), lambda i, m: (layer, i, 0)))
    return _pcall(
        body, name=name,
        grid_spec=pltpu.PrefetchScalarGridSpec(
            num_scalar_prefetch=1, grid=(rows // tr,), in_specs=[in_spec],
            out_specs=pl.BlockSpec((None, tr, cols), lambda i, m: (m[0], i, 0))),
        out_shape=jax.ShapeDtypeStruct((N_CHIPS, rows, cols), dtype),
        compiler_params=_params(("parallel",)),
    )(me_idx, shard)


def place_many(shards, dtypes, me_idx, name):
    def body(m_ref, *refs):
        for x_ref, o_ref in zip(refs[:len(shards)], refs[len(shards):]):
            o_ref[...] = x_ref[...].astype(o_ref.dtype)

    return _pcall(
        body, name=name,
        grid_spec=pltpu.PrefetchScalarGridSpec(
            num_scalar_prefetch=1, grid=(1,),
            in_specs=[pl.BlockSpec(s.shape, lambda i, m: (0, 0)) for s in shards],
            out_specs=[pl.BlockSpec((None,) + s.shape, lambda i, m: (m[0], 0, 0)) for s in shards]),
        out_shape=[jax.ShapeDtypeStruct((N_CHIPS,) + s.shape, d) for s, d in zip(shards, dtypes)],
        compiler_params=_params(("arbitrary",)),
    )(me_idx, *shards)


def put_rows(pack, rows, off):
    _, n, cols = rows.shape

    def body(r_ref, p_ref, o_ref, sem):
        cp = pltpu.make_async_copy(r_ref.at[0], o_ref.at[pl.program_id(0), pl.ds(off, n), :], sem)
        cp.start()
        cp.wait()

    return _pcall(body, name="grad_put_rows", grid=(N_CHIPS,),
                  in_specs=[pl.BlockSpec((1, n, cols), lambda k: (k, 0, 0)), _ANY], out_specs=_ANY,
                  out_shape=jax.ShapeDtypeStruct(pack.shape, pack.dtype), input_output_aliases={1: 0},
                  scratch_shapes=[pltpu.SemaphoreType.DMA],
                  compiler_params=_params(("arbitrary",)))(rows, pack)


def _my_cols(c, mine=True):
    start = (c if mine else 1 - c) * HALF_W
    return pl.ds(pl.multiple_of(start, HALF_W), HALF_W)


def add_halves(gpack, got, c_idx):
    n, rows, _ = gpack.shape
    tr = min(G_BLOCK_ROWS, rows)
    blk = (None, tr, HALF_W)

    def body(c_ref, g_ref, r_ref, o_ref):
        o_ref[...] = (g_ref[...] + r_ref[...]).astype(o_ref.dtype)

    return _pcall(
        body, name="grad_add_halves",
        grid_spec=pltpu.PrefetchScalarGridSpec(
            num_scalar_prefetch=1, grid=(n, rows // tr),
            in_specs=[pl.BlockSpec(blk, lambda k, i, c: (k, i, c[0])), pl.BlockSpec(blk, lambda k, i, c: (k, i, 0))],
            out_specs=pl.BlockSpec(blk, lambda k, i, c: (k, i, 0))),
        out_shape=jax.ShapeDtypeStruct((n, rows, HALF_W), BF16),
        compiler_params=_params(("parallel", "parallel")),
    )(c_idx, gpack, got)


def sum_owner(part, got, idx, total_rows, row_off=0, into=None):
    _, rows, _ = part.shape
    tr = math.gcd(math.gcd(rows, row_off), G_BLOCK_ROWS)
    n_into = 0 if into is None else 1

    def body(m_ref, p_ref, g_ref, *rest):
        up = lambda v: v.astype(F32)
        rest[-1][...] = ((up(p_ref[...]) + up(g_ref[0])) + up(g_ref[1])) + up(g_ref[2])

    return _pcall(
        body, name="grad_sum_owner",
        grid_spec=pltpu.PrefetchScalarGridSpec(
            num_scalar_prefetch=1, grid=(rows // tr,),
            in_specs=[pl.BlockSpec((None, tr, HALF_W), lambda i, m: (m[0], i, 0)),
                      pl.BlockSpec((3, tr, HALF_W), lambda i, m: (0, i, 0))] + [_ANY] * n_into,
            out_specs=pl.BlockSpec((tr, HALF_W), lambda i, m: (row_off // tr + i, m[1]))),
        out_shape=jax.ShapeDtypeStruct((total_rows, PACK_W), F32),
        input_output_aliases={3: 0} if n_into else {},
        compiler_params=_params(("parallel",)),
    )(idx, part, got, *([into] if n_into else []))


def join_halves(red):
    def body(in_ref, out_ref, send_sem, recv_sem):
        x, y, c, _ = _place()
        sibling = (x, y, 1 - c)
        mine = out_ref.at[:, _my_cols(c)]
        cp = pltpu.make_async_remote_copy(src_ref=mine, dst_ref=mine, send_sem=send_sem, recv_sem=recv_sem,
                                          device_id=sibling, device_id_type=MESH)
        cp.start()
        cp.wait_send()
        other = out_ref.at[:, _my_cols(c, mine=False)]
        pltpu.make_async_remote_copy(src_ref=other, dst_ref=other, send_sem=send_sem, recv_sem=recv_sem,
                                     device_id=sibling, device_id_type=MESH).wait_recv()

    return _pcall(body, name="grad_join_halves", in_specs=[_ANY], out_specs=_ANY,
                  out_shape=jax.ShapeDtypeStruct(red.shape, red.dtype), input_output_aliases={0: 0},
                  scratch_shapes=[pltpu.SemaphoreType.DMA, pltpu.SemaphoreType.DMA])(red)


def adamw(gsrc, g_off, wt, m, v, name):
    n, cols = wt.shape
    tr = math.gcd(math.gcd(g_off, n), 256) if g_off else math.gcd(n, 256)
    off_blk = g_off // tr
    c1 = 1.0 / (1.0 - ADAM_B1 ** ADAM_STEP)
    c2 = 1.0 / (1.0 - ADAM_B2 ** ADAM_STEP)

    def body(g_ref, w_ref, m_ref, v_ref, go_ref, d_ref, mo_ref, vo_ref):
        gv = g_ref[...]
        mn = ADAM_B1 * m_ref[...] + (1.0 - ADAM_B1) * gv
        vn = ADAM_B2 * v_ref[...] + (1.0 - ADAM_B2) * gv * gv
        go_ref[...] = gv
        mo_ref[...] = mn
        vo_ref[...] = vn
        d_ref[...] = -ADAM_LR * ((mn * c1) / (jnp.sqrt(vn * c2) + ADAM_EPS) + ADAM_WD * w_ref[...])

    blk = pl.BlockSpec((tr, cols), lambda i: (i, 0))
    return _pcall(body, name=name, grid=(n // tr,),
                  in_specs=[pl.BlockSpec((tr, cols), lambda i: (off_blk + i, 0)), blk, blk, blk],
                  out_specs=[blk] * 4, out_shape=[jax.ShapeDtypeStruct((n, cols), F32)] * 4,
                  compiler_params=_params(("parallel",)))(gsrc, wt, m, v)


def _rows8(a):
    return -(-a.size // (8 * PACK_W)) * 8


def _as_rows(a, rows=None):
    flat = a.reshape(-1)
    n = _rows8(a) if rows is None else rows
    return jnp.pad(flat, (0, n * PACK_W - flat.shape[0])).reshape(n, PACK_W)


def local_shards_2d(wl):
    return {"w_ff1": [wl["w_ff1"][0], wl["w_ff1"][1]], "w_ff2": [wl["w_ff2"][0], wl["w_ff2"][1]],
            "ssm_w_glu": wl["ssm_w_glu"], "ssm_w_out": wl["ssm_w_out"], "kv_w_a": _pad_kva_cols(wl["kv_w_a"]),
            "kv_w_b": wl["kv_w_b"], "q_w_a": wl["q_w_a"], "q_w_b": _perm_q_cols(wl["q_w_b"]),
            "attn_w_o": wl["attn_w_o"], "ssm_d": wl["ssm_d"].reshape(2, -1)}


def misc_grad_shard(name, g, k):
    if name == "ssm_d":
        w = D_MODEL // N_CHIPS
        return g[:, w * k:w * (k + 1)]
    if name in ("ssm_w_glu", "kv_w_b"):
        return g[k]
    if name == "q_w_b":
        return _unperm_q_cols(g[k])
    rows = D_MODEL // N_CHIPS
    shard = g[rows * k:rows * (k + 1)]
    return _unpad_kva_cols(shard) if name == "kv_w_a" else shard


def packed_shards(g, names, rows, tail=None):
    blocks = []
    for k in range(N_CHIPS):
        parts = [_as_rows(misc_grad_shard(n, g[n], k), MISC_SHARD_ROWS[n]) for n in names]
        if tail is not None:
            parts.append(tail[k * (tail.shape[0] // N_CHIPS):(k + 1) * (tail.shape[0] // N_CHIPS)])
        blk = jnp.concatenate(parts, axis=0)
        blocks.append(jnp.pad(blk, ((0, rows - blk.shape[0]), (0, 0))))
    return jnp.stack(blocks)


def kernel(x, positions, ln_mix_g, ln_mix_b, ln_ffn_g, ln_ffn_b, w_ff1, w_ff2, ssm_lam_re, ssm_lam_im, ssm_log_dt, ssm_b_re, ssm_b_im, ssm_c_re, ssm_c_im, ssm_d, ssm_w_glu, ssm_w_out, kv_w_a, kv_norm_g, kv_w_b, q_w_a, q_norm_g, q_w_b, attn_w_o, loss_target, m_ln_mix_g, m_ln_mix_b, m_ln_ffn_g, m_ln_ffn_b, m_w_ff1, m_w_ff2, m_ssm_lam_re, m_ssm_lam_im, m_ssm_log_dt, m_ssm_b_re, m_ssm_b_im, m_ssm_c_re, m_ssm_c_im, m_ssm_d, m_ssm_w_glu, m_ssm_w_out, m_kv_w_a, m_kv_norm_g, m_kv_w_b, m_q_w_a, m_q_norm_g, m_q_w_b, m_attn_w_o, v_ln_mix_g, v_ln_mix_b, v_ln_ffn_g, v_ln_ffn_b, v_w_ff1, v_w_ff2, v_ssm_lam_re, v_ssm_lam_im, v_ssm_log_dt, v_ssm_b_re, v_ssm_b_im, v_ssm_c_re, v_ssm_c_im, v_ssm_d, v_ssm_w_glu, v_ssm_w_out, v_kv_w_a, v_kv_norm_g, v_kv_w_b, v_q_w_a, v_q_norm_g, v_q_w_b, v_attn_w_o):
    env = dict(locals())
    wl = {n: env[n] for n in WEIGHTS}
    ml = {n: env["m_" + n] for n in WEIGHTS}
    vl = {n: env["v_" + n] for n in WEIGHTS}
    for n in ("ssm_w_glu", "ssm_w_out", "q_w_a", "q_w_b", "attn_w_o"):
        wl[n], ml[n], vl[n] = wl[n][0], ml[n][0], vl[n][0]

    c_idx = lax.axis_index("c").astype(jnp.int32).reshape(1)
    me_idx = (2 * lax.axis_index("x") + lax.axis_index("y")).astype(jnp.int32).reshape(1)

    local = local_shards_2d(wl)
    stacked = {n: [place(wl[n], me_idx, BF16, f"place_{n}_{l}", layer=l) for l in range(DEPTH)] for n in ("w_ff1", "w_ff2")}
    others = [n for n in SHARDED if n not in stacked]
    stacked.update(zip(others, place_many([local[n] for n in others], [F32 if n == "ssm_d" else BF16 for n in others],
                                          me_idx, "place_others")))
    stacked["ssm_d"] = ride_alone(GatherRide([_halves(stacked["ssm_d"])]), "ssm_d_all_gather")[0].reshape(1, D_MODEL)
    for n in REPLICATED:
        stacked[n] = wl[n]

    loss_part, dx, early, mid, g, sent = device_step(x[0], positions[0], loss_target[0], stacked, comm=(me_idx, c_idx))
    loss = lax.psum(loss_part, ("x", "y", "c"))

    small = jnp.concatenate([_as_rows(g[n]) for n in REPLICATED], axis=0)
    small = jnp.pad(small, ((0, SMALL_ROWS - small.shape[0]), (0, 0)))
    late = packed_shards(g, MISC_LATE, LATE_ROWS, tail=small)
    late_sums = add_halves(late, ride_alone(SwapRide(late), "grad_swap_halves")[0], c_idx)
    sent.append((late_sums, ride_alone(SendRide([late_sums]), "grad_send_to_owners")[0]))
    where = jnp.concatenate([me_idx, c_idx])
    starts = (0, EARLY_ROWS, EARLY_ROWS + MID_ROWS)
    total_rows = EARLY_ROWS + MID_ROWS + LATE_ROWS
    reduced = None
    for (sums, got), off in zip(sent, starts):
        reduced = sum_owner(sums, got, where, total_rows, row_off=off, into=reduced)
    reduced = join_halves(reduced)
    quarter = reduced[starts[2] + SMALL_OFF:starts[2] + SMALL_OFF + SMALL_Q_ROWS]
    small_tot = ride_alone(GatherRide([_halves(place(quarter, me_idx, F32, "place_small_grads"))]),
                           "small_grad_all_gather")[0].reshape(SMALL_ROWS, PACK_W)

    out_g, out_d, out_m, out_v = {}, {}, {}, {}
    direct = {**EARLY_OFF, **{n: starts[1] + o for n, o in MID_OFF.items()}}
    for n, off in direct.items():
        res = adamw(reduced, off, wl[n].reshape(-1, PACK_W), ml[n].reshape(-1, PACK_W), vl[n].reshape(-1, PACK_W),
                    "adamw_" + n)
        out_g[n], out_d[n], out_m[n], out_v[n] = [a.reshape(env[n].shape) for a in res]
    for names, off in ((MISC_EARLY, MISC_EARLY_OFF), (MISC_MID, starts[1] + MISC_MID_OFF), (MISC_LATE, starts[2])):
        pack3 = lambda d: jnp.concatenate([_as_rows(d[n], MISC_SHARD_ROWS[n]) for n in names], axis=0)
        res = adamw(reduced, off, pack3(wl), pack3(ml), pack3(vl), "adamw_packed_" + names[0])
        r0 = 0
        for n in names:
            cnt = math.prod(env[n].shape)
            out_g[n], out_d[n], out_m[n], out_v[n] = [
                a[r0:r0 + MISC_SHARD_ROWS[n]].reshape(-1)[:cnt].reshape(env[n].shape) for a in res]
            r0 += MISC_SHARD_ROWS[n]
    ws = jnp.concatenate([_as_rows(wl[n]) for n in REPLICATED], axis=0)
    ms = jnp.concatenate([_as_rows(ml[n]) for n in REPLICATED], axis=0)
    vs = jnp.concatenate([_as_rows(vl[n]) for n in REPLICATED], axis=0)
    pad = ((0, SMALL_ROWS - ws.shape[0]), (0, 0))
    res = adamw(small_tot, 0, jnp.pad(ws, pad), jnp.pad(ms, pad), jnp.pad(vs, pad), "adamw_replicated")
    row = 0
    for n in REPLICATED:
        cnt = math.prod(env[n].shape)
        nrows = _rows8(env[n])
        out_g[n], out_d[n], out_m[n], out_v[n] = [a[row:row + nrows].reshape(-1)[:cnt].reshape(env[n].shape) for a in res]
        row += nrows

    return (loss, dx[None], *[out_g[n] for n in WEIGHTS], *[out_d[n] for n in WEIGHTS],
            *[out_m[n] for n in WEIGHTS], *[out_v[n] for n in WEIGHTS])
```

```python
import functools
import math

import jax
import jax.numpy as jnp
from jax import lax
from jax.experimental import pallas as pl
from jax.experimental.pallas import tpu as pltpu

F32 = jnp.float32
BF16 = jnp.bfloat16
MESH = pl.DeviceIdType.MESH

D_MODEL = 1024
DEPTH = 2
SSM_GROUP = 16
N_GROUPS = D_MODEL // SSM_GROUP
SSM_STATE = 64
N_STATES = N_GROUPS * SSM_STATE
N_HEADS = 8
QK_NOPE = 128
QK_ROPE = 64
HALF_ROPE = QK_ROPE // 2
V_HEAD = 128
QK_DIM = QK_NOPE + QK_ROPE
Q_LORA = 384
KV_LORA = 256
ROPE_THETA = 10000.0
SM_SCALE = QK_DIM ** -0.5
NEG_INF = -1e30
D_FF = 4 * D_MODEL
DN_ALPHA = (2 * DEPTH) ** 0.25
LN_EPS = 1e-5
RMS_EPS = 1e-6
ADAM_LR = 0.001
ADAM_B1 = 0.9
ADAM_B2 = 0.999
ADAM_EPS = 1e-08
ADAM_WD = 0.01
ADAM_STEP = 10

N_CHIPS = 4
LANES = 128
VMEM_LIMIT = 56 * 1024 * 1024
MM_VMEM_BUDGET = 40 * 1024 * 1024
PACK_W = 1024
KVA_PAD = 384
HALF_W = PACK_W // 2

SHARDED = ("w_ff1", "w_ff2", "ssm_w_glu", "ssm_w_out", "kv_w_a", "kv_w_b", "q_w_a", "q_w_b", "attn_w_o", "ssm_d")
G_BLOCK_ROWS = 960
EARLY_OFF = {"w_ff1": 0, "w_ff2": 2048, "attn_w_o": 4096}
MISC_EARLY = ("kv_w_b", "kv_w_a", "q_w_a", "q_w_b")
MISC_EARLY_OFF = 4352
EARLY_ROWS = 5 * G_BLOCK_ROWS
EARLY_HEAD = G_BLOCK_ROWS
MID_OFF = {"ssm_w_out": 0}
MISC_MID = ("ssm_w_glu",)
MISC_MID_OFF = 256
MID_ROWS = MISC_MID_OFF + 512
MISC_LATE = ("ssm_d",)
SMALL_Q_ROWS = 96
SMALL_ROWS = N_CHIPS * SMALL_Q_ROWS
SMALL_OFF = 16
LATE_ROWS = 192
MISC_SHARD_ROWS = {"ssm_d": 16, "ssm_w_glu": 512, "kv_w_b": 128, "kv_w_a": 80, "q_w_a": 96, "q_w_b": 144}
REPLICATED = ("ln_mix_g", "ln_mix_b", "ln_ffn_g", "ln_ffn_b", "ssm_lam_re", "ssm_lam_im", "ssm_log_dt",
              "ssm_b_re", "ssm_b_im", "ssm_c_re", "ssm_c_im", "kv_norm_g", "q_norm_g")
WEIGHTS = ("ln_mix_g", "ln_mix_b", "ln_ffn_g", "ln_ffn_b", "w_ff1", "w_ff2", "ssm_lam_re", "ssm_lam_im",
           "ssm_log_dt", "ssm_b_re", "ssm_b_im", "ssm_c_re", "ssm_c_im", "ssm_d", "ssm_w_glu", "ssm_w_out",
           "kv_w_a", "kv_norm_g", "kv_w_b", "q_w_a", "q_norm_g", "q_w_b", "attn_w_o")


def _pcall(body, **kw):
    return pl.pallas_call(body, **kw)


def _params(sem=None):
    return pltpu.CompilerParams(dimension_semantics=sem, vmem_limit_bytes=VMEM_LIMIT)


_ANY = pl.BlockSpec(memory_space=pl.ANY)


def _tile(dim, prefs):
    for p in prefs:
        if dim % p == 0:
            return p
    return dim


def _place():
    x, y, c = lax.axis_index("x"), lax.axis_index("y"), lax.axis_index("c")
    return x, y, c, [(1 - x, y), (x, 1 - y), (1 - x, 1 - y)]


def _remote(k, src, dst, to, send_sems, recv_sems):
    return pltpu.make_async_remote_copy(src_ref=src, dst_ref=dst, send_sem=send_sems.at[k], recv_sem=recv_sems.at[k],
                                        device_id=to, device_id_type=MESH)


class GatherRide:
    def __init__(self, arrs):
        self.ins = list(arrs)
        self.out_shapes = [jax.ShapeDtypeStruct(a.shape, a.dtype) for a in arrs]
        self.aliases = {i: i for i in range(len(arrs))}
        self.n_sems = 6 * len(arrs)

    def start(self, ins, outs, send_sems, recv_sems):
        x, y, c, chips = _place()
        me = 2 * x + y
        for a, o in enumerate(outs):
            for j, (px, py) in enumerate(chips):
                _remote(6 * a + j, o.at[me, c], o.at[me, c], (px, py, c), send_sems, recv_sems).start()

    def pass_on(self, ins, outs, send_sems, recv_sems):
        x, y, c, chips = _place()
        for a, o in enumerate(outs):
            for j, (px, py) in enumerate(chips):
                blk = o.at[2 * px + py, c]
                _remote(6 * a + j, blk, blk, (px, py, c), send_sems, recv_sems).wait_recv()
                _remote(6 * a + 3 + j, blk, blk, (x, y, 1 - c), send_sems, recv_sems).start()

    def finish(self, ins, outs, send_sems, recv_sems, passed_on=False):
        if not passed_on:
            self.pass_on(ins, outs, send_sems, recv_sems)
        x, y, c, chips = _place()
        me = 2 * x + y
        sibling = (x, y, 1 - c)
        for a, o in enumerate(outs):
            for j, (px, py) in enumerate(chips):
                blk = o.at[2 * px + py, 1 - c]
                _remote(6 * a + 3 + j, blk, blk, sibling, send_sems, recv_sems).wait_recv()
                _remote(6 * a + j, o.at[me, c], o.at[me, c], (px, py, c), send_sems, recv_sems).wait_send()
                mine = o.at[2 * px + py, c]
                _remote(6 * a + 3 + j, mine, mine, sibling, send_sems, recv_sems).wait_send()


class SendRide:
    base = 0

    def __init__(self, parts):
        parts = [p if isinstance(p, tuple) else (p, (0, p.shape[1]), None) for p in parts]
        self.rows = [rows for _, rows, _ in parts]
        self.n_parts = len(parts)
        self.ins = [p for p, _, _ in parts] + [into for _, _, into in parts if into is not None]
        self.out_shapes = [jax.ShapeDtypeStruct((3,) + p.shape[1:], p.dtype) for p, _, _ in parts]
        given = [a for a, (_, _, into) in enumerate(parts) if into is not None]
        self.aliases = {self.n_parts + i: a for i, a in enumerate(given)}
        self.n_sems = 3 * self.n_parts

    def _copies(self, ins, outs, send_sems, recv_sems):
        x, y, c, chips = _place()
        return [_remote(self.base + 3 * a + j, ins[a].at[2 * px + py, pl.ds(r0, n)], outs[a].at[j, pl.ds(r0, n)],
                        (px, py, c), send_sems, recv_sems)
                for a, (r0, n) in enumerate(self.rows) for j, (px, py) in enumerate(chips)]

    def start(self, ins, outs, send_sems, recv_sems):
        for cp in self._copies(ins, outs, send_sems, recv_sems):
            cp.start()

    def finish(self, ins, outs, send_sems, recv_sems):
        for cp in self._copies(ins, outs, send_sems, recv_sems):
            cp.wait()


class SwapRide:
    base = 0

    def __init__(self, pack, ranges=None, into=None):
        self.ins = [pack] if into is None else [pack, into]
        self.out_shapes = [jax.ShapeDtypeStruct(pack.shape[:2] + (HALF_W,), pack.dtype)]
        self.aliases = {} if into is None else {1: 0}
        self.ranges = ranges or [(0, pack.shape[1])]
        self.n_sems = len(self.ranges)

    def _copies(self, ins, outs, send_sems, recv_sems):
        x, y, c, _ = _place()
        return [_remote(self.base + k, ins[0].at[:, pl.ds(r0, n), _my_cols(c, mine=False)], outs[0].at[:, pl.ds(r0, n), :],
                        (x, y, 1 - c), send_sems, recv_sems) for k, (r0, n) in enumerate(self.ranges)]

    def start(self, ins, outs, send_sems, recv_sems):
        for cp in self._copies(ins, outs, send_sems, recv_sems):
            cp.start()

    def finish(self, ins, outs, send_sems, recv_sems):
        for cp in self._copies(ins, outs, send_sems, recv_sems):
            cp.wait()


class Together:
    def __init__(self, rides):
        self.rides = rides
        self.ins, self.out_shapes, self.aliases, self.n_sems = [], [], {}, 0
        for r in rides:
            r.base = self.n_sems
            self.aliases.update({len(self.ins) + i: len(self.out_shapes) + o for i, o in r.aliases.items()})
            self.ins += r.ins
            self.out_shapes += r.out_shapes
            self.n_sems += r.n_sems

    def _each(self, step, ins, outs, send_sems, recv_sems):
        i = o = 0
        for r in self.rides:
            getattr(r, step)(ins[i:i + len(r.ins)], outs[o:o + len(r.out_shapes)], send_sems, recv_sems)
            i, o = i + len(r.ins), o + len(r.out_shapes)

    def start(self, *refs):
        self._each("start", *refs)

    def finish(self, *refs):
        self._each("finish", *refs)


def _pcall_riding(body, args, ride, first, last, *, in_specs, out_specs, out_shape, scratch_shapes=(), middle=None,
                  **kw):
    n_in, n_out = len(args), len(out_shape)
    if ride is None:
        return _pcall(body, in_specs=in_specs, out_specs=out_specs, out_shape=out_shape,
                      scratch_shapes=list(scratch_shapes), **kw)(*args), []
    k_in, k_out = len(ride.ins), len(ride.out_shapes)

    def riding(*refs):
        ins, r_in = refs[:n_in], refs[n_in:n_in + k_in]
        outs = refs[n_in + k_in:n_in + k_in + n_out]
        r_out = refs[n_in + k_in + n_out:n_in + k_in + n_out + k_out]
        scratch, (send_sems, recv_sems) = refs[n_in + k_in + n_out + k_out:-2], refs[-2:]

        @pl.when(first())
        def _():
            ride.start(r_in, r_out, send_sems, recv_sems)

        if middle is not None:
            @pl.when(middle())
            def _():
                ride.pass_on(r_in, r_out, send_sems, recv_sems)

        body(*ins, *outs, *scratch)

        @pl.when(last())
        def _():
            if middle is not None:
                ride.finish(r_in, r_out, send_sems, recv_sems, passed_on=True)
            else:
                ride.finish(r_in, r_out, send_sems, recv_sems)

    res = _pcall(riding, in_specs=list(in_specs) + [_ANY] * k_in, out_specs=list(out_specs) + [_ANY] * k_out,
                 out_shape=list(out_shape) + ride.out_shapes,
                 input_output_aliases={n_in + i: n_out + o for i, o in ride.aliases.items()},
                 scratch_shapes=list(scratch_shapes) + [pltpu.SemaphoreType.DMA((ride.n_sems,))] * 2,
                 **kw)(*args, *ride.ins)
    return res[:n_out], res[n_out:]


def ride_alone(ride, name):
    def body(*refs):
        n = len(ride.ins)
        ins, outs, (send_sems, recv_sems) = refs[:n], refs[n:-2], refs[-2:]
        ride.start(ins, outs, send_sems, recv_sems)
        ride.finish(ins, outs, send_sems, recv_sems)

    return _pcall(body, name=name, in_specs=[_ANY] * len(ride.ins), out_specs=[_ANY] * len(ride.out_shapes),
                  out_shape=ride.out_shapes, input_output_aliases=dict(ride.aliases),
                  scratch_shapes=[pltpu.SemaphoreType.DMA((ride.n_sems,))] * 2)(*ride.ins)


def mm(a, b, *, name, ta=False, tb=False, pro_a=None, epi=None, extras=(), out_dtypes=(F32,), n_dim=None,
       tiles=(None, None, None), b_view=None, out_view=None, into=None, ride=None):
    widths = [d[0] if isinstance(d, tuple) else None for d in out_dtypes]
    out_dtypes = [d[1] if isinstance(d, tuple) else d for d in out_dtypes]
    if ta:
        k_dim, m_dim = a.shape
    else:
        m_dim, k_dim = a.shape
    if n_dim is None:
        n_dim = b.shape[0] if tb else b.shape[1]
    tn = tiles[1] or (n_dim if n_dim <= 1024 else _tile(n_dim, (1024, 512, 256, 128)))
    tk = tiles[2] or (k_dim if k_dim <= 1024 else _tile(k_dim, (1024, 512, 256, 128)))
    nk = k_dim // tk

    def vmem_bytes(tm):
        blocks = tm * tk * a.dtype.itemsize + tk * tn * b.dtype.itemsize
        blocks += sum(tm * (tn if e.shape[1] == n_dim else e.shape[1]) * e.dtype.itemsize for e in extras if e.shape[0] > 1)
        blocks += tm * sum((w or tn) * jnp.dtype(d).itemsize for w, d in zip(widths, out_dtypes))
        return 2 * blocks + tm * tn * 4

    tm = tiles[0] or next((t for t in (4096, 2048, 1024, 512, 256) if m_dim % t == 0 and vmem_bytes(t) <= MM_VMEM_BUDGET),
                          _tile(m_dim, (128,)))
    assert m_dim % tm == 0 and n_dim % tn == 0 and k_dim % tk == 0, (name, m_dim, n_dim, k_dim, tm, tn, tk)
    assert tn == n_dim or not any(widths), name
    n_ex, n_out = len(extras), len(out_dtypes)
    n_into = 0 if into is None else 1
    dims = (((0 if ta else 1,), (1 if tb else 0,)), ((), ()))

    def body(a_ref, b_ref, *rest):
        ex_refs, out_refs = rest[:n_ex], rest[n_ex + n_into:n_ex + n_into + n_out]

        def partial():
            av = a_ref[...]
            if pro_a is not None:
                av = pro_a(av)
            return lax.dot_general(av.astype(BF16), b_ref[...].astype(BF16), dims, preferred_element_type=F32)

        def finish(r):
            res = epi(r, *[e[...] for e in ex_refs]) if epi is not None else (r,)
            for o_ref, v in zip(out_refs, res):
                o_ref[...] = v.reshape(o_ref.shape).astype(o_ref.dtype)

        if nk == 1:
            finish(partial())
            return
        acc = rest[-1]
        k = pl.program_id(2)

        @pl.when(k == 0)
        def _():
            acc[...] = partial()

        @pl.when(k > 0)
        def _():
            acc[...] += partial()

        @pl.when(k == nk - 1)
        def _():
            finish(acc[...])

    def ex_spec(e):
        if e.shape == (m_dim, n_dim):
            return o_spec
        if e.shape[0] == m_dim:
            return pl.BlockSpec((tm, e.shape[1]), lambda i, j, k: (i, 0))
        return pl.BlockSpec(e.shape, lambda i, j, k: (0, 0))

    a_spec = pl.BlockSpec((tk, tm), lambda i, j, k: (k, i)) if ta else pl.BlockSpec((tm, tk), lambda i, j, k: (i, k))
    if b_view is not None:
        b_spec = b_view(tk, tn)
    else:
        b_spec = pl.BlockSpec((tn, tk), lambda i, j, k: (j, k)) if tb else pl.BlockSpec((tk, tn), lambda i, j, k: (k, j))
    o_spec = pl.BlockSpec((tm, tn), lambda i, j, k: (i, j))
    if out_view is None:
        out_specs = [o_spec if w is None else pl.BlockSpec((tm, w), lambda i, j, k: (i, 0)) for w in widths]
        out_shape = [jax.ShapeDtypeStruct((m_dim, w or n_dim), dt) for w, dt in zip(widths, out_dtypes)]
    else:
        assert n_out == 1
        out_specs = [out_view[1](tm, tn)]
        out_shape = [jax.ShapeDtypeStruct(out_view[0], out_dtypes[0])]
    grid = (m_dim // tm, n_dim // tn, nk)
    scratch = [pltpu.VMEM((tm, tn), F32)] if nk > 1 else []
    if ride is not None:
        assert into is None
        at = lambda ids: functools.reduce(jnp.logical_and, [pl.program_id(d) == i for d, i in enumerate(ids)])
        outs, landed = _pcall_riding(
            body, (a, b, *extras), ride, lambda: at((0, 0, 0)), lambda: at([g - 1 for g in grid]),
            name=name, grid=grid, in_specs=[a_spec, b_spec] + [ex_spec(e) for e in extras], out_specs=out_specs,
            out_shape=out_shape, scratch_shapes=scratch, compiler_params=_params(("arbitrary",) * 3))
        return (outs[0] if n_out == 1 else outs), landed
    outs = _pcall(
        body, name=name, grid=grid,
        in_specs=[a_spec, b_spec] + [ex_spec(e) for e in extras] + [_ANY] * n_into,
        out_specs=out_specs, out_shape=out_shape,
        input_output_aliases={2 + n_ex: 0} if n_into else {},
        scratch_shapes=scratch,
        compiler_params=_params(("parallel", "parallel", "arbitrary")),
    )(a, b, *extras, *([into] if n_into else []))
    return outs[0] if n_out == 1 else outs


def rowwise(fn, ins, outs, *, name, accs=(), tm=256):
    rows = ins[0].shape[0]
    tm = min(tm, rows)
    n_in, n_out, n_acc = len(ins), len(outs), len(accs)

    def body(*refs):
        in_refs, out_refs, acc_refs = refs[:n_in], refs[n_in:n_in + n_out], refs[n_in + n_out:]
        res, sums = fn(*[r[...] for r in in_refs])
        for o_ref, v in zip(out_refs, res):
            o_ref[...] = v.astype(o_ref.dtype)
        if n_acc:
            @pl.when(pl.program_id(0) == 0)
            def _():
                for a_ref in acc_refs:
                    a_ref[...] = jnp.zeros_like(a_ref)

            for a_ref, s in zip(acc_refs, sums):
                a_ref[...] += s

    def spec(arr):
        if arr.shape[0] == rows:
            return pl.BlockSpec((tm, arr.shape[1]), lambda i: (i, 0))
        return pl.BlockSpec(arr.shape, lambda i: (0, 0))

    res = _pcall(
        body, name=name, grid=(rows // tm,),
        in_specs=[spec(a) for a in ins],
        out_specs=[pl.BlockSpec((tm, w), lambda i: (i, 0)) for w, _ in outs]
        + [pl.BlockSpec((1, w), lambda i: (0, 0)) for w in accs],
        out_shape=[jax.ShapeDtypeStruct((rows, w), dt) for w, dt in outs]
        + [jax.ShapeDtypeStruct((1, w), F32) for w in accs],
        compiler_params=_params(("arbitrary",) if n_acc else ("parallel",)),
    )(*ins)
    return res


def _relu2(v):
    r = jnp.maximum(v, 0.0)
    return r * r


def _gelu(x):
    c = math.sqrt(2.0 / math.pi)
    return 0.5 * x * (1.0 + jnp.tanh(c * (x + 0.044715 * x * x * x)))


def _gelu_grad(x):
    c = math.sqrt(2.0 / math.pi)
    t = jnp.tanh(c * (x + 0.044715 * x * x * x))
    return 0.5 * (1.0 + t) + 0.5 * x * (1.0 - t * t) * c * (1.0 + 3 * 0.044715 * x * x)


def _sigmoid(x):
    return 1.0 / (1.0 + jnp.exp(-x))


def _layer_norm(h, mix, g, b):
    r = DN_ALPHA * h + mix
    mu = jnp.mean(r, axis=-1, keepdims=True)
    xc = r - mu
    var = jnp.mean(xc * xc, axis=-1, keepdims=True)
    return xc * lax.rsqrt(var + LN_EPS) * g + b


def _layer_norm_bwd(h, mix, g, dy):
    r = DN_ALPHA * h + mix
    mu = jnp.mean(r, axis=-1, keepdims=True)
    xc = r - mu
    var = jnp.mean(xc * xc, axis=-1, keepdims=True)
    rstd = lax.rsqrt(var + LN_EPS)
    xhat = xc * rstd
    dxh = dy * g
    m1 = jnp.mean(dxh, axis=-1, keepdims=True)
    m2 = jnp.mean(dxh * xhat, axis=-1, keepdims=True)
    dr = rstd * (dxh - m1 - xhat * m2)
    return dr, jnp.sum(dy * xhat, axis=0, keepdims=True), jnp.sum(dy, axis=0, keepdims=True)


def ln_bwd(h, mix, g, dy, name):
    def fn(h, mix, g, dy):
        dr, dg, db = _layer_norm_bwd(h, mix, g, dy)
        return (dr, dr), (dg, db)
    return rowwise(fn, (h, mix, g, dy), ((D_MODEL, F32), (D_MODEL, BF16)), accs=(D_MODEL, D_MODEL), name=name)


def _rms(x, g):
    r = lax.rsqrt(jnp.mean(x * x, axis=-1, keepdims=True) + RMS_EPS)
    return x * r * g


def _rms_bwd(x, g, dy):
    r = lax.rsqrt(jnp.mean(x * x, axis=-1, keepdims=True) + RMS_EPS)
    xn = x * r
    dyg = dy * g
    dx = r * (dyg - xn * jnp.mean(dyg * xn, axis=-1, keepdims=True))
    return dx, jnp.sum(dy * xn, axis=0, keepdims=True)


def _s5_disc(lr, li, ldt):
    dt = jnp.exp(ldt)
    mag = jnp.exp(lr * dt)
    cs, sn = jnp.cos(li * dt), jnp.sin(li * dt)
    ar, ai = mag * cs, mag * sn
    inv = 1.0 / (lr * lr + li * li)
    n_re = (ar - 1.0) * lr + ai * li
    n_im = ai * lr - (ar - 1.0) * li
    return dt, mag, cs, sn, ar, ai, inv, n_re, n_im


def s5_prep(lr, li, ldt, b_re, b_im):
    def fn(lr, li, ldt, b_re, b_im):
        _, _, _, _, ar, ai, inv, n_re, n_im = _s5_disc(lr, li, ldt)
        cr, ci = n_re * inv, n_im * inv
        return (ar, ai, cr * b_re - ci * b_im, cr * b_im + ci * b_re), ()
    return rowwise(fn, (lr, li, ldt, b_re, b_im), ((1, F32), (1, F32), (SSM_GROUP, F32), (SSM_GROUP, F32)),
                   name="s5_prep", tm=512)


def s5_prep_bwd(lr, li, ldt, b_re, b_im, dar, dai, dbb_re, dbb_im):
    def fn(lr, li, ldt, b_re, b_im, dar, dai, dbb_re, dbb_im):
        dt, mag, cs, sn, ar, ai, inv, n_re, n_im = _s5_disc(lr, li, ldt)
        cr, ci = n_re * inv, n_im * inv
        db_re = cr * dbb_re + ci * dbb_im
        db_im = cr * dbb_im - ci * dbb_re
        dcr = jnp.sum(dbb_re * b_re + dbb_im * b_im, axis=-1, keepdims=True)
        dci = jnp.sum(dbb_im * b_re - dbb_re * b_im, axis=-1, keepdims=True)
        dar = dar + (dcr * lr - dci * li) * inv
        dai = dai + (dcr * li + dci * lr) * inv
        dinv = dcr * n_re + dci * n_im
        dlr = (dcr * (ar - 1.0) + dci * ai) * inv - 2.0 * lr * inv * inv * dinv
        dli = (dcr * ai - dci * (ar - 1.0)) * inv - 2.0 * li * inv * inv * dinv
        dmag = dar * cs + dai * sn
        dth = dai * ar - dar * ai
        dlr = dlr + dmag * mag * dt
        dli = dli + dth * dt
        ddt = dmag * mag * lr + dth * li
        return (dlr, dli, ddt * dt, db_re, db_im), ()
    return rowwise(fn, (lr, li, ldt, b_re, b_im, dar, dai, dbb_re, dbb_im),
                   ((1, F32), (1, F32), (1, F32), (SSM_GROUP, F32), (SSM_GROUP, F32)), name="s5_prep_bwd", tm=512)


def group_sum(x):
    def body(x_ref, o_ref):
        o_ref[...] = jnp.sum(x_ref[...], axis=1)
    return _pcall(body, name="s5_group_sum", out_shape=jax.ShapeDtypeStruct((N_GROUPS, 1), F32))(
        x.reshape(N_GROUPS, SSM_STATE, 1))


GROUPS_PER_TILE = LANES // SSM_GROUP
TILE_STATES = GROUPS_PER_TILE * SSM_STATE
N_UTILES = D_MODEL // LANES


SUBLANES = 8
SCAN_STRIP = 1024
N_STRIPS = N_STATES // SCAN_STRIP
_NT = (((1,), (1,)), ((), ()))
_TN = (((0,), (0,)), ((), ()))


def _scan_coefs(are, aim, shifted, reverse):
    ar = are[...]
    ai = -aim[...] if reverse else aim[...]
    powers = {1: (ar, ai)}
    for d in (2, 4):
        r, i = powers[d // 2]
        powers[d] = (r * r - i * i, 2.0 * r * i)
    rid = lax.broadcasted_iota(jnp.int32, (SUBLANES, N_STATES), 0)
    first = (rid == SUBLANES - 1) if reverse else (rid == 0)
    masks = [(1, first)] + [(d, (rid <= SUBLANES - 1 - d) if reverse else (rid >= d)) for d in (1, 2, 4)]
    for n, (d, keep) in enumerate(masks):
        for part in (0, 1):
            shifted[2 * n + part][...] = jnp.where(keep, jnp.broadcast_to(powers[d][part], (SUBLANES, N_STATES)), 0.0)


def _tile_scan(xr, xi, shifted, nbr_re, nbr_im, reverse):
    for n, d in enumerate((1, 1, 2, 4)):
        by = SUBLANES - d if reverse else d
        fr, fi = (nbr_re, nbr_im) if n == 0 else (xr, xi)
        sr, si = pltpu.roll(fr, by, 0), pltpu.roll(fi, by, 0)
        kr, ki = shifted[2 * n], shifted[2 * n + 1]
        xr, xi = xr + kr * sr - ki * si, xi + kr * si + ki * sr
    return xr, xi


def _tile_rows(t):
    return pl.ds(pl.multiple_of(t * SUBLANES, SUBLANES), SUBLANES)


def s5_fwd(u, bbd_re, bbd_im, cbd_re, cbd_imn, a_re, a_im, dskip, ride=None, t_rows=256):
    seq = u.shape[0]
    t_rows = min(t_rows, seq)
    n_tiles = t_rows // SUBLANES

    def body(u_ref, bre, bim, cre, cimn, are, aim, d_ref, y_ref, gelu_ref, hre_ref, him_ref, car_re, car_im, *shifted):
        @pl.when(pl.program_id(0) == 0)
        def _():
            car_re[...] = jnp.zeros_like(car_re)
            car_im[...] = jnp.zeros_like(car_im)
            _scan_coefs(are, aim, shifted, reverse=False)

        uf = u_ref[...]
        ub = uf.astype(BF16)
        for j in range(N_UTILES):
            uj = ub[:, LANES * j:LANES * (j + 1)]
            sl = slice(TILE_STATES * j, TILE_STATES * (j + 1))
            hre_ref[:, sl] = jnp.dot(uj, bre[j], preferred_element_type=F32)
            him_ref[:, sl] = jnp.dot(uj, bim[j], preferred_element_type=F32)
        for s in range(N_STRIPS):
            cols = pl.ds(s * SCAN_STRIP, SCAN_STRIP)
            coefs = [c[:, cols] for c in shifted]

            def step(t, before):
                rows = _tile_rows(t)
                hr, hi = _tile_scan(hre_ref[rows, cols], him_ref[rows, cols], coefs, before[0], before[1], False)
                hre_ref[rows, cols] = hr
                him_ref[rows, cols] = hi
                return hr, hi

            cr, ci = lax.fori_loop(0, n_tiles, step, (car_re[:, cols], car_im[:, cols]))
            car_re[:, cols] = cr
            car_im[:, cols] = ci
        dv = d_ref[...]
        for j in range(N_UTILES):
            st = slice(TILE_STATES * j, TILE_STATES * (j + 1))
            yj = (jnp.dot(hre_ref[:, st].astype(BF16), cre[j], preferred_element_type=F32)
                  + jnp.dot(him_ref[:, st].astype(BF16), cimn[j], preferred_element_type=F32))
            sl = slice(LANES * j, LANES * (j + 1))
            yj = yj + dv[:, sl] * uf[:, sl]
            y_ref[:, sl] = yj
            gelu_ref[:, sl] = _gelu(yj).astype(gelu_ref.dtype)

    full3 = lambda a: pl.BlockSpec(a.shape, lambda i: (0, 0, 0))
    full2 = lambda a: pl.BlockSpec(a.shape, lambda i: (0, 0))
    tile = pltpu.VMEM((SUBLANES, N_STATES), F32)
    n_chunks = seq // t_rows
    return _pcall_riding(
        body, (u, bbd_re, bbd_im, cbd_re, cbd_imn, a_re, a_im, dskip), ride,
        lambda: pl.program_id(0) == 0, lambda: pl.program_id(0) == n_chunks - 1,
        middle=(lambda: pl.program_id(0) == (7 * n_chunks) // 8) if ride is not None else None,
        name="s5_fwd", grid=(n_chunks,),
        in_specs=[pl.BlockSpec((t_rows, D_MODEL), lambda i: (i, 0)), full3(bbd_re), full3(bbd_im), full3(cbd_re),
                  full3(cbd_imn), full2(a_re), full2(a_im), full2(dskip)],
        out_specs=[pl.BlockSpec((t_rows, D_MODEL), lambda i: (i, 0)),
                   pl.BlockSpec((t_rows, D_MODEL), lambda i: (i, 0)),
                   pl.BlockSpec((t_rows, N_STATES), lambda i: (i, 0)),
                   pl.BlockSpec((t_rows, N_STATES), lambda i: (i, 0))],
        out_shape=[jax.ShapeDtypeStruct((seq, D_MODEL), F32),
                   jax.ShapeDtypeStruct((seq, D_MODEL), BF16),
                   jax.ShapeDtypeStruct((seq, N_STATES), F32),
                   jax.ShapeDtypeStruct((seq, N_STATES), F32)],
        scratch_shapes=[tile] * 10,
        compiler_params=_params(("arbitrary",)))


def s5_bwd(dy, u, dres, h_re, h_im, bbd_re, bbd_im, cbd_re, cbd_imn, a_re, a_im, dskip, ride=None, t_rows=256):
    seq = u.shape[0]
    t_rows = min(t_rows, seq)
    n_chunks = seq // t_rows

    n_tiles = t_rows // SUBLANES

    def body(dy_ref, u_ref, dres_ref, hre_ref, him_ref, hpre_ref, hpim_ref, bre, bim, cre, cimn, are, aim, d_ref,
             dx_ref, dbre, dbim, dcre, dcimn, dar_ref, dai_ref, dd_ref, lre, lim, car_re, car_im, acc_re, acc_im,
             *shifted):
        i = pl.program_id(0)

        @pl.when(i == 0)
        def _():
            for r in (car_re, car_im, acc_re, acc_im, dbre, dbim, dcre, dcimn, dd_ref):
                r[...] = jnp.zeros_like(r)
            _scan_coefs(are, aim, shifted, reverse=True)

        dyf = dy_ref[...]
        dyb = dyf.astype(BF16)
        uf = u_ref[...]
        ub = uf.astype(BF16)
        for j in range(N_UTILES):
            dyj = dyb[:, LANES * j:LANES * (j + 1)]
            st = slice(TILE_STATES * j, TILE_STATES * (j + 1))
            lre[:, st] = lax.dot_general(dyj, cre[j], _NT, preferred_element_type=F32)
            lim[:, st] = lax.dot_general(dyj, cimn[j], _NT, preferred_element_type=F32)
        has_pred = (i < n_chunks - 1).astype(F32)
        last_row = lax.broadcasted_iota(jnp.int32, (SUBLANES, SCAN_STRIP), 0) == SUBLANES - 1
        for s in range(N_STRIPS):
            cols = pl.ds(s * SCAN_STRIP, SCAN_STRIP)
            coefs = [c[:, cols] for c in shifted]
            before_re, before_im = hpre_ref[:, cols] * has_pred, hpim_ref[:, cols] * has_pred

            def step(k, carry):
                after_re, after_im, dar, dai = carry
                t = n_tiles - 1 - k
                rows = _tile_rows(t)
                lr, li = _tile_scan(lre[rows, cols], lim[rows, cols], coefs, after_re, after_im, True)
                lre[rows, cols] = lr
                lim[rows, cols] = li
                prev = _tile_rows(jnp.maximum(t - 1, 0))
                pre_re = jnp.where(t == 0, before_re, hre_ref[prev, cols])
                pre_im = jnp.where(t == 0, before_im, him_ref[prev, cols])
                hpr = pltpu.roll(jnp.where(last_row, pre_re, hre_ref[rows, cols]), 1, 0)
                hpi = pltpu.roll(jnp.where(last_row, pre_im, him_ref[rows, cols]), 1, 0)
                return lr, li, dar + lr * hpr + li * hpi, dai + li * hpr - lr * hpi

            cr, ci, dar, dai = lax.fori_loop(0, n_tiles, step, (car_re[:, cols], car_im[:, cols],
                                                               acc_re[:, cols], acc_im[:, cols]))
            car_re[:, cols] = cr
            car_im[:, cols] = ci
            acc_re[:, cols] = dar
            acc_im[:, cols] = dai

        dv = d_ref[...]
        for j in range(N_UTILES):
            sl = slice(LANES * j, LANES * (j + 1))
            st = slice(TILE_STATES * j, TILE_STATES * (j + 1))
            lrj = lre[:, st].astype(BF16)
            lij = lim[:, st].astype(BF16)
            du = (lax.dot_general(lrj, bre[j], _NT, preferred_element_type=F32)
                  + lax.dot_general(lij, bim[j], _NT, preferred_element_type=F32))
            dx_ref[:, sl] = du + dv[:, sl] * dyf[:, sl] + DN_ALPHA * dres_ref[:, sl]
            uj = ub[:, sl]
            dbre[j] += lax.dot_general(uj, lrj, _TN, preferred_element_type=F32)
            dbim[j] += lax.dot_general(uj, lij, _TN, preferred_element_type=F32)
            dyj = dyb[:, sl]
            dcre[j] += lax.dot_general(hre_ref[:, st].astype(BF16), dyj, _TN, preferred_element_type=F32)
            dcimn[j] += lax.dot_general(him_ref[:, st].astype(BF16), dyj, _TN, preferred_element_type=F32)
        dd_ref[...] += jnp.sum(dyf * uf, axis=0, keepdims=True)

        @pl.when(i == n_chunks - 1)
        def _():
            dar_ref[...] = jnp.sum(acc_re[...], axis=0, keepdims=True)
            dai_ref[...] = jnp.sum(acc_im[...], axis=0, keepdims=True)

    rev = lambda i: (n_chunks - 1 - i, 0)
    prev_tile = lambda i: (jnp.maximum((n_chunks - 1 - i) * n_tiles - 1, 0), 0)
    once = pl.Buffered(1)
    full3 = lambda a: pl.BlockSpec(a.shape, lambda i: (0, 0, 0), pipeline_mode=once)
    full2 = lambda a: pl.BlockSpec(a.shape, lambda i: (0, 0), pipeline_mode=once)
    acc3 = lambda shape: pl.BlockSpec(shape, lambda i: (0, 0, 0))
    acc2 = lambda shape: pl.BlockSpec(shape, lambda i: (0, 0))
    tile = pltpu.VMEM((SUBLANES, N_STATES), F32)
    return _pcall_riding(
        body, (dy, u, dres, h_re, h_im, h_re, h_im, bbd_re, bbd_im, cbd_re, cbd_imn, a_re, a_im, dskip), ride,
        lambda: pl.program_id(0) == 0, lambda: pl.program_id(0) == n_chunks - 1,
        name="s5_bwd", grid=(n_chunks,),
        in_specs=[pl.BlockSpec((t_rows, D_MODEL), rev), pl.BlockSpec((t_rows, D_MODEL), rev),
                  pl.BlockSpec((t_rows, D_MODEL), rev),
                  pl.BlockSpec((t_rows, N_STATES), rev), pl.BlockSpec((t_rows, N_STATES), rev),
                  pl.BlockSpec((SUBLANES, N_STATES), prev_tile), pl.BlockSpec((SUBLANES, N_STATES), prev_tile),
                  full3(bbd_re), full3(bbd_im), full3(cbd_re), full3(cbd_imn), full2(a_re), full2(a_im), full2(dskip)],
        out_specs=[pl.BlockSpec((t_rows, D_MODEL), rev), acc3(bbd_re.shape), acc3(bbd_im.shape), acc3(cbd_re.shape),
                   acc3(cbd_imn.shape), acc2((1, N_STATES)), acc2((1, N_STATES)), acc2((1, D_MODEL))],
        out_shape=[jax.ShapeDtypeStruct((seq, D_MODEL), F32), jax.ShapeDtypeStruct(bbd_re.shape, F32),
                   jax.ShapeDtypeStruct(bbd_im.shape, F32), jax.ShapeDtypeStruct(cbd_re.shape, F32),
                   jax.ShapeDtypeStruct(cbd_imn.shape, F32), jax.ShapeDtypeStruct((1, N_STATES), F32),
                   jax.ShapeDtypeStruct((1, N_STATES), F32), jax.ShapeDtypeStruct((1, D_MODEL), F32)],
        scratch_shapes=[pltpu.VMEM((t_rows, N_STATES), F32), pltpu.VMEM((t_rows, N_STATES), F32)] + [tile] * 12,
        compiler_params=_params(("arbitrary",)))


def _eye_groups():
    return jnp.eye(GROUPS_PER_TILE, dtype=F32)


def _blockdiag_in(bb):
    t = bb.transpose(0, 2, 1).reshape(N_UTILES, GROUPS_PER_TILE, SSM_GROUP, SSM_STATE)
    bd = jnp.einsum("jgcp,gh->jgchp", t, _eye_groups())
    return bd.reshape(N_UTILES, LANES, TILE_STATES)


def _blockdiag_in_t(d):
    t = jnp.einsum("jgchp,gh->jgcp", d.reshape(N_UTILES, GROUPS_PER_TILE, SSM_GROUP, GROUPS_PER_TILE, SSM_STATE),
                   _eye_groups())
    return t.reshape(N_GROUPS, SSM_GROUP, SSM_STATE).transpose(0, 2, 1)


def _blockdiag_out(c):
    t = c.transpose(0, 2, 1).reshape(N_UTILES, GROUPS_PER_TILE, SSM_STATE, SSM_GROUP)
    bd = jnp.einsum("jhpc,hg->jhpgc", t, _eye_groups())
    return bd.reshape(N_UTILES, TILE_STATES, LANES)


def _blockdiag_out_t(d):
    t = jnp.einsum("jhpgc,hg->jhpc", d.reshape(N_UTILES, GROUPS_PER_TILE, SSM_STATE, GROUPS_PER_TILE, SSM_GROUP),
                   _eye_groups())
    return t.reshape(N_GROUPS, SSM_STATE, SSM_GROUP).transpose(0, 2, 1)


ATT_TQ = 512
ATT_TK = 512
LOG2E = math.log2(math.e)
LN2 = math.log(2.0)
Q_PRESCALE = SM_SCALE * LOG2E


def _loop_in_pairs(n, step, carry, start=0):
    pairs = (n - start) // 2

    def two(t, c):
        return step(start + 2 * t + 1, step(start + 2 * t, c))

    carry = lax.fori_loop(0, pairs, two, carry)
    return lax.fori_loop(start + 2 * pairs, n, step, carry)


def _causal(s, transposed=False):
    r = lax.broadcasted_iota(jnp.int32, s.shape, 0)
    c = lax.broadcasted_iota(jnp.int32, s.shape, 1)
    return jnp.where((r <= c) if transposed else (c <= r), s, NEG_INF)


def _q_specs(rows, at):
    def nope(*ids):
        r, h = at(*ids)
        return r, 3 * (h // HEADS_PER_CHIP) + h % HEADS_PER_CHIP

    def rope(*ids):
        r, h = at(*ids)
        return r, 3 * (h // HEADS_PER_CHIP) + HEADS_PER_CHIP

    return [pl.BlockSpec((rows, LANES), nope), pl.BlockSpec((rows, LANES), rope)]


def _kv_specs(rows, at):
    def col(f):
        def index(*ids):
            r, h = at(*ids)
            return r, f(h)
        return index

    return [pl.BlockSpec((rows, LANES), col(lambda h: 2 * h)), pl.BlockSpec((rows, LANES), col(lambda h: h % HEADS_PER_CHIP)),
            pl.BlockSpec((rows, LANES), col(lambda h: 2 * h + 1))]


def _cat(a, b):
    return jnp.concatenate([a, b], axis=1)


def attn_fwd(q, kv, kr, ride=None, tq=ATT_TQ, tk=ATT_TK):
    seq = q.shape[0]
    n_heads = N_HEADS
    tq, tk = min(tq, seq), min(tk, seq)
    assert tq == tk

    def body(qn_ref, qr_ref, kn_ref, kr_ref, v_ref, o_ref, lse_ref):
        qi = pl.program_id(1)
        qv = _cat(qn_ref[...], qr_ref[...])
        jd = qi

        def block(j, carry, diag):
            m, l, acc = carry
            rows = pl.ds(pl.multiple_of(j * tk, tk), tk)
            s = lax.dot_general(qv, _cat(kn_ref[rows, :], kr_ref[rows, :]), _NT, preferred_element_type=F32)
            if diag:
                s = _causal(s)
            m_new = jnp.maximum(m, jnp.max(s, axis=-1, keepdims=True))
            p = jnp.exp2(s - m_new)
            corr = jnp.exp2(m - m_new)
            l = l * corr + jnp.sum(p, axis=-1, keepdims=True)
            acc = acc * corr + jnp.dot(p.astype(BF16), v_ref[rows, :], preferred_element_type=F32)
            return m_new, l, acc

        init = (jnp.full((tq, 1), NEG_INF, F32), jnp.zeros((tq, 1), F32), jnp.zeros((tq, V_HEAD), F32))
        carry = _loop_in_pairs(jd, lambda j, c: block(j, c, False), init)
        m, l, acc = block(jd, carry, True)
        o_ref[...] = acc / l
        lse_ref[...] = jnp.transpose(jnp.broadcast_to(m + jnp.log2(l), (tq, LANES)))[:1, :]

    n_q = seq // tq
    return _pcall_riding(
        body, (q, q, kv, kr, kv), ride,
        lambda: (pl.program_id(0) == 0) & (pl.program_id(1) == 0),
        lambda: (pl.program_id(0) == n_heads - 1) & (pl.program_id(1) == n_q - 1),
        middle=(lambda: (pl.program_id(0) == (5 * n_heads) // 8) & (pl.program_id(1) == 0)) if ride is not None else None,
        name="attn_fwd", grid=(n_heads, n_q),
        in_specs=_q_specs(tq, lambda h, i: (i, h)) + _kv_specs(seq, lambda h, i: (0, h)),
        out_specs=[pl.BlockSpec((tq, V_HEAD), lambda h, i: (i, h)),
                   pl.BlockSpec((None, None, 1, tq), lambda h, i: (h, i, 0, 0))],
        out_shape=[jax.ShapeDtypeStruct((seq, n_heads * V_HEAD), F32),
                   jax.ShapeDtypeStruct((n_heads, n_q, 1, tq), F32)],
        compiler_params=_params(("arbitrary", "arbitrary")))


def attn_bwd(q, kv, kr, do, lse_row, delta_row, tq=ATT_TK):
    seq = q.shape[0]
    tq = min(tq, seq)
    n_blk = seq // tq

    def body(qn_ref, qr_ref, kn_ref, kr_ref, v_ref, do_ref, lse_ref, delta_ref, dqn_ref, dqr_ref, dkv_ref, dkr_ref, dq_acc):
        head, kj = pl.program_id(0), pl.program_id(1)

        @pl.when(kj == 0)
        def _():
            dq_acc[...] = jnp.zeros_like(dq_acc)

        kc = _cat(kn_ref[...], kr_ref[...])
        vv = v_ref[...]

        def block(i, carry, diag):
            dk, dv = carry
            rows = pl.ds(pl.multiple_of(i * tq, tq), tq)
            qv = _cat(qn_ref[rows, :], qr_ref[rows, :])
            st = lax.dot_general(kc, qv, _NT, preferred_element_type=F32)
            if diag:
                st = _causal(st, transposed=True)
            pt = jnp.exp2(st - lse_ref[0, pl.ds(i, 1), :])
            dob = do_ref[rows, :].astype(BF16)
            dv = dv + jnp.dot(pt.astype(BF16), dob, preferred_element_type=F32)
            dpt = lax.dot_general(vv, dob, _NT, preferred_element_type=F32)
            dst = (pt * (dpt - delta_ref[0, pl.ds(i, 1), :])).astype(BF16)
            dk = dk + jnp.dot(dst, qv, preferred_element_type=F32)
            dq_acc[rows, :] += lax.dot_general(dst, kc, _TN, preferred_element_type=F32)
            return dk, dv

        carry = block(kj, (jnp.zeros((tq, 2 * LANES), F32), jnp.zeros((tq, V_HEAD), F32)), True)
        dk, dv = _loop_in_pairs(n_blk, lambda i, c: block(i, c, False), carry, start=kj + 1)
        dk = dk * LN2
        dkv_ref[...] = _cat(dk[:, :LANES], dv).astype(dkv_ref.dtype)
        lane = lax.broadcasted_iota(jnp.int32, (tq, LANES), 1)
        mine = (lane // HALF_ROPE) % HEADS_PER_CHIP == head % HEADS_PER_CHIP
        dkr_ref[0] = jnp.where(mine, dk[:, LANES:], 0.0)

        @pl.when(kj == n_blk - 1)
        def _():
            dqn_ref[...] = dq_acc[:, :LANES] * SM_SCALE

        @pl.when((kj == n_blk - 1) & (head % HEADS_PER_CHIP == 0))
        def _():
            dqr_ref[...] = dq_acc[:, LANES:] * SM_SCALE

        @pl.when((kj == n_blk - 1) & (head % HEADS_PER_CHIP > 0))
        def _():
            dqr_ref[...] += dq_acc[:, LANES:] * SM_SCALE

    return _pcall(
        body, name="attn_bwd", grid=(N_HEADS, n_blk),
        in_specs=_q_specs(seq, lambda h, j: (0, h)) + _kv_specs(tq, lambda h, j: (j, h))
        + [pl.BlockSpec((seq, V_HEAD), lambda h, j: (0, h)),
           pl.BlockSpec((1, n_blk, tq), lambda h, j: (h, 0, 0)),
           pl.BlockSpec((1, n_blk, tq), lambda h, j: (h, 0, 0))],
        out_specs=[pl.BlockSpec((seq, LANES), lambda h, j: (0, h)),
                   pl.BlockSpec((seq, LANES), lambda h, j: (0, h // HEADS_PER_CHIP)),
                   pl.BlockSpec((tq, QK_NOPE + V_HEAD), lambda h, j: (j, h)),
                   pl.BlockSpec((1, tq, LANES), lambda h, j: (h, j, 0))],
        out_shape=[jax.ShapeDtypeStruct((seq, N_HEADS * QK_NOPE), F32),
                   jax.ShapeDtypeStruct((seq, N_CHIPS * LANES), F32),
                   jax.ShapeDtypeStruct((seq, N_HEADS * (QK_NOPE + V_HEAD)), BF16),
                   jax.ShapeDtypeStruct((N_HEADS, seq, LANES), F32)],
        scratch_shapes=[pltpu.VMEM((seq, 2 * LANES), F32)],
        compiler_params=_params(("arbitrary", "arbitrary")),
    )(q, q, kv, kr, kv, do, lse_row, delta_row)


def head_sum(x, ts=512):
    n_heads, seq, w = x.shape
    ts = min(ts, seq)

    def body(x_ref, o_ref):
        o_ref[...] = jnp.sum(x_ref[...], axis=0)

    return _pcall(body, name="head_sum", grid=(seq // ts,),
                  in_specs=[pl.BlockSpec((n_heads, ts, w), lambda i: (0, i, 0))],
                  out_specs=pl.BlockSpec((ts, w), lambda i: (i, 0)),
                  out_shape=jax.ShapeDtypeStruct((seq, w), F32),
                  compiler_params=_params(("parallel",)))(x)


HEADS_PER_CHIP = N_HEADS // N_CHIPS
Q_CHIP = HEADS_PER_CHIP * QK_DIM
Q_CHIP_NOPE = HEADS_PER_CHIP * QK_NOPE


def _perm_q_cols(w):
    t = w.reshape(w.shape[0], HEADS_PER_CHIP, QK_DIM)
    return jnp.concatenate([t[:, :, :QK_NOPE].reshape(w.shape[0], -1),
                            t[:, :, QK_NOPE:QK_NOPE + HALF_ROPE].reshape(w.shape[0], -1),
                            t[:, :, QK_NOPE + HALF_ROPE:].reshape(w.shape[0], -1)], axis=1)


def _unperm_q_cols(w):
    r = w.shape[0]
    nope = w[:, :Q_CHIP_NOPE].reshape(r, HEADS_PER_CHIP, QK_NOPE)
    r1 = w[:, Q_CHIP_NOPE:Q_CHIP_NOPE + QK_ROPE].reshape(r, HEADS_PER_CHIP, HALF_ROPE)
    r2 = w[:, Q_CHIP_NOPE + QK_ROPE:].reshape(r, HEADS_PER_CHIP, HALF_ROPE)
    return jnp.concatenate([nope, r1, r2], axis=2).reshape(r, Q_CHIP)


def _pad_kva_cols(w):
    z = jnp.zeros((w.shape[0], HALF_ROPE), w.dtype)
    return jnp.concatenate([w[:, :KV_LORA], w[:, KV_LORA:KV_LORA + HALF_ROPE], z, w[:, KV_LORA + HALF_ROPE:], z], axis=1)


def _unpad_kva_cols(w):
    return jnp.concatenate([w[:, :KV_LORA], w[:, KV_LORA:KV_LORA + HALF_ROPE],
                            w[:, KV_LORA + QK_ROPE:KV_LORA + QK_ROPE + HALF_ROPE]], axis=1)


def _rope_tile(t, cs, sn):
    return t * cs + pltpu.roll(t, LANES // 2, 1) * sn


def _rope_tile_bwd(d, cs, sn):
    return d * cs + pltpu.roll(d * sn, LANES // 2, 1)


def _b_cols(tk, tn):
    return pl.BlockSpec((None, tk, tn), lambda i, j, k: (j, k, 0))


def _b_cols_t(tk, tn):
    return pl.BlockSpec((None, tn, tk), lambda i, j, k: (k, j, 0))


def _out_cols(shape):
    return shape, lambda tm, tn: pl.BlockSpec((None, tm, tn), lambda i, j, k: (j, i, 0))


def glu_proj(y, w_glu, tm=1024):
    seq, k_dim = y.shape
    tn = w_glu.shape[2]
    tm = min(tm, seq)
    half = N_CHIPS // 2

    def body(y_ref, wv_ref, wg_ref, val_ref, gate_ref, z_ref):
        yv = y_ref[...]
        v = jnp.dot(yv, wv_ref[...], preferred_element_type=F32)
        gt = jnp.dot(yv, wg_ref[...], preferred_element_type=F32)
        val_ref[...] = v
        gate_ref[...] = gt
        z_ref[...] = (v * _sigmoid(gt)).astype(z_ref.dtype)

    tile = pl.BlockSpec((tm, tn), lambda i, j: (i, j))
    return _pcall(
        body, name="glu_proj", grid=(seq // tm, half),
        in_specs=[pl.BlockSpec((tm, k_dim), lambda i, j: (i, 0)),
                  pl.BlockSpec((None, k_dim, tn), lambda i, j: (j, 0, 0)),
                  pl.BlockSpec((None, k_dim, tn), lambda i, j: (j + half, 0, 0))],
        out_specs=[tile, tile, tile],
        out_shape=[jax.ShapeDtypeStruct((seq, half * tn), F32), jax.ShapeDtypeStruct((seq, half * tn), F32),
                   jax.ShapeDtypeStruct((seq, half * tn), BF16)],
        compiler_params=_params(("parallel", "parallel")),
    )(y, w_glu, w_glu)


def _halves(a):
    return a.reshape(N_CHIPS, 2, a.shape[1] // 2, a.shape[2])


def device_step(x, positions, target, w, comm=None):
    seq = x.shape[0]
    w = dict(w)

    def gathered(names, outs):
        for n, a in zip(names, outs):
            if isinstance(n, tuple):
                w[n[0]] = [a.reshape(v.shape) if l == n[1] else v for l, v in enumerate(w[n[0]])]
            else:
                w[n] = a.reshape(w[n].shape)

    def ride_for(names):
        if comm is None:
            return None
        return GatherRide([_halves(w[n[0]][n[1]] if isinstance(n, tuple) else w[n]) for n in names])

    first_ride = ("ssm_w_glu", "ssm_w_out", ("w_ff1", 0), ("w_ff2", 0))
    mla_ride = ("kv_w_a", "kv_w_b", "q_w_a", "q_w_b", "attn_w_o")
    second_ride = (("w_ff1", 1), ("w_ff2", 1))

    inv_freq = ROPE_THETA ** (-jnp.arange(HALF_ROPE, dtype=F32) / HALF_ROPE)
    ang = positions.astype(F32)[:, None] * inv_freq
    cos, sin = jnp.cos(ang), jnp.sin(ang)
    zero = jnp.zeros_like(cos)
    cos_q, sin_q = jnp.concatenate([cos] * 4, 1), jnp.concatenate([-sin, -sin, sin, sin], 1)
    cos_k, sin_k = jnp.concatenate([cos, zero, cos, zero], 1), jnp.concatenate([-sin, zero, sin, zero], 1)
    ff_tile = D_FF // N_CHIPS
    pack_shape = (N_CHIPS, EARLY_ROWS, PACK_W)

    lr = w["ssm_lam_re"].reshape(N_STATES, 1)
    li = w["ssm_lam_im"].reshape(N_STATES, 1)
    ldt = jnp.repeat(w["ssm_log_dt"].reshape(N_GROUPS), SSM_STATE).reshape(N_STATES, 1)
    b_re = w["ssm_b_re"].reshape(N_STATES, SSM_GROUP)
    b_im = w["ssm_b_im"].reshape(N_STATES, SSM_GROUP)
    a_re, a_im, bb_re, bb_im = s5_prep(lr, li, ldt, b_re, b_im)
    a_re, a_im = a_re.reshape(1, N_STATES), a_im.reshape(1, N_STATES)
    bbd_re = _blockdiag_in(bb_re.reshape(N_GROUPS, SSM_STATE, SSM_GROUP)).astype(BF16)
    bbd_im = _blockdiag_in(bb_im.reshape(N_GROUPS, SSM_STATE, SSM_GROUP)).astype(BF16)
    cbd_re = _blockdiag_out(w["ssm_c_re"].reshape(N_GROUPS, SSM_GROUP, SSM_STATE)).astype(BF16)
    cbd_imn = _blockdiag_out(-w["ssm_c_im"].reshape(N_GROUPS, SSM_GROUP, SSM_STATE)).astype(BF16)
    dskip = w["ssm_d"].reshape(1, D_MODEL)
    (ypre, yg, h_re, h_im), landed = s5_fwd(x, bbd_re, bbd_im, cbd_re, cbd_imn, a_re, a_im, dskip, ride_for(first_ride))
    gathered(first_ride, landed)
    w_glu = w["ssm_w_glu"]
    glu_tile = w_glu.shape[2]
    val, gate, z = glu_proj(yg, w_glu)
    w_out = w["ssm_w_out"].reshape(D_MODEL, D_MODEL)
    ln = lambda name, l: w[name][l].reshape(1, D_MODEL)

    def then_ln(h, names, layer):
        def epi(r, hv, gl, bl):
            y = _layer_norm(hv, r, gl, bl)
            return r, y, y
        return dict(epi=epi, extras=(h, ln(names[0], layer), ln(names[1], layer)), out_dtypes=(F32, F32, BF16))

    mix0, h1, h1b = mm(z, w_out, name="ssm_out", **then_ln(x, ("ln_mix_g", "ln_mix_b"), 0))

    def mlp_fwd(h, hb, layer, riding=None, with_ln=True):
        pre = mm(hb, w["w_ff1"][layer], n_dim=D_FF, tiles=(None, ff_tile, None), b_view=_b_cols, name=f"ff1_{layer}",
                 out_dtypes=(BF16,), ride=ride_for(riding) if riding else None)
        if riding and comm is not None:
            pre, landed = pre
            gathered(riding, landed)
        post = then_ln(h, ("ln_ffn_g", "ln_ffn_b"), layer) if with_ln else {}
        return pre, mm(pre, w["w_ff2"][layer].reshape(D_FF, D_MODEL), pro_a=_relu2, name=f"ff2_{layer}", **post)

    f1pre, (f1, h2, h2b) = mlp_fwd(h1, h1b, 0, mla_ride)

    kv_w_a = w["kv_w_a"].reshape(D_MODEL, KVA_PAD)
    kv_w_b = w["kv_w_b"]
    q_w_a = w["q_w_a"].reshape(D_MODEL, Q_LORA)
    q_w_b = w["q_w_b"]
    w_o = w["attn_w_o"].reshape(D_MODEL, D_MODEL)
    kvb_tile = kv_w_b.shape[2]
    kvn_g = w["kv_norm_g"].reshape(1, KV_LORA)
    qn_g = w["q_norm_g"].reshape(1, Q_LORA)
    def kv_post(kva, g, cs, sn):
        tile = _rope_tile(kva[:, KV_LORA:], cs, sn)
        return kva, _rms(kva[:, :KV_LORA], g), _cat(tile, pltpu.roll(tile, HALF_ROPE, 1))
    kva, ckv, krope = mm(h2b, kv_w_a, epi=kv_post, extras=(kvn_g, cos_k, sin_k),
                         out_dtypes=(F32, (KV_LORA, BF16), (2 * LANES, BF16)), name="kv_a")
    kvb = mm(ckv, kv_w_b, n_dim=N_CHIPS * kvb_tile, tiles=(None, kvb_tile, KV_LORA), b_view=_b_cols, name="kv_b",
             out_dtypes=(BF16,))
    cq_raw, cq = mm(h2b, q_w_a, epi=lambda r, gq: (r, _rms(r, gq)), extras=(qn_g,), out_dtypes=(F32, BF16), name="q_a")

    def rope_and_scale(r, cs, sn):
        return (_cat(r[:, :Q_CHIP_NOPE], _rope_tile(r[:, Q_CHIP_NOPE:], cs, sn)) * Q_PRESCALE,)
    qro = mm(cq, q_w_b, n_dim=N_CHIPS * Q_CHIP, tiles=(None, Q_CHIP, Q_LORA), b_view=_b_cols, epi=rope_and_scale,
             extras=(cos_q, sin_q), out_dtypes=(BF16,), name="q_b")
    (o, lse), landed = attn_fwd(qro, kvb, krope, ride_for(second_ride))
    gathered(second_ride, landed)
    mix1, h3, h3b = mm(o, w_o, name="attn_out", **then_ln(h2, ("ln_mix_g", "ln_mix_b"), 1))
    f2pre, f2 = mlp_fwd(h3, h3b, 1, with_ln=False)
    def last_ln_loss_and_back(h, mix, gl, bl, t):
        e = _layer_norm(h, mix, gl, bl) - t
        dr, dg, db = _layer_norm_bwd(h, mix, gl, e * (1.0 / D_MODEL))
        return (dr, dr), (jnp.broadcast_to(jnp.sum(e * e), (1, LANES)), dg, db)
    dr4, dr4b, loss_acc, dg_f1, db_f1 = rowwise(
        last_ln_loss_and_back, (h3, f2, ln("ln_ffn_g", 1), ln("ln_ffn_b", 1), target),
        ((D_MODEL, F32), (D_MODEL, BF16)), accs=(LANES, D_MODEL, D_MODEL), name="ln_ffn_1_loss")
    loss = loss_acc[0, 0] * (0.5 / D_MODEL)

    g = {}

    def into_rows(off, rows_per_chip, shape=pack_shape):
        def view(tm, tn):
            if tm == N_CHIPS * rows_per_chip:
                return pl.BlockSpec((N_CHIPS, rows_per_chip, tn), lambda i, j, k: (0, off // rows_per_chip, 0))
            nb = rows_per_chip // tm
            return pl.BlockSpec((None, tm, tn), lambda i, j, k: (i // nb, off // tm + i % nb, 0))
        return shape, view

    def into_cols(off):
        return pack_shape, lambda tm, tn: pl.BlockSpec((None, tm, tn), lambda i, j, k: (j, off // tm + i, 0))

    def mlp_bwd(pack, dr, drb, hb, pre, layer, swap=False):
        w2_rows = (EARLY_OFF["w_ff2"] + layer * ff_tile, ff_tile)
        w1_rows = (EARLY_OFF["w_ff1"] + layer * D_MODEL, D_MODEL)
        ready = [(w1_rows[0] + w1_rows[1], w2_rows[0] - w1_rows[0] - w1_rows[1]), (w2_rows[0] + w2_rows[1], EARLY_ROWS - w2_rows[0] - w2_rows[1])]
        dpre = mm(drb, w["w_ff2"][layer].reshape(D_FF, D_MODEL), tb=True, epi=lambda r, p: (r * 2.0 * jnp.maximum(p, 0.0),),
                  extras=(pre,), out_dtypes=(BF16,), tiles=(None, ff_tile, None), name=f"ff2_dx_{layer}",
                  ride=SwapRide(pack, ready) if swap else None)
        if swap:
            dpre, (theirs,) = dpre
        pack = mm(pre, drb, ta=True, pro_a=_relu2, name=f"ff2_dw_{layer}", tiles=(ff_tile, PACK_W, None), into=pack,
                  out_view=into_rows(w2_rows[0], ff_tile))
        pack = mm(hb, dpre, ta=True, name=f"ff1_dw_{layer}", tiles=(None, PACK_W, None), into=pack,
                  out_view=into_cols(w1_rows[0]))
        dh = mm(dpre, w["w_ff1"][layer], tb=True, epi=lambda r, d: (r + DN_ALPHA * d,), extras=(dr,), n_dim=D_MODEL,
                tiles=(None, D_MODEL, ff_tile), b_view=_b_cols_t, name=f"ff1_dx_{layer}",
                ride=SwapRide(pack, [w1_rows, w2_rows], into=theirs) if swap else None)
        return (pack, *dh) if swap else (pack, dh)

    pack, dh3 = mlp_bwd(None, dr4, dr4b, h3b, f2pre, 1)
    dr3, dr3b, dg_m1, db_m1 = ln_bwd(h2, mix1, ln("ln_mix_g", 1), dh3, "ln_mix_bwd_1")
    shard_rows = D_MODEL // N_CHIPS
    pack = mm(o, dr3b, ta=True, name="attn_out_dw", tiles=(D_MODEL, PACK_W, None), into=pack,
              out_view=into_rows(EARLY_OFF["attn_w_o"], shard_rows))
    def head_dots(do, o):
        return do, jnp.concatenate([jnp.sum(do[:, V_HEAD * h:V_HEAD * (h + 1)] * o[:, V_HEAD * h:V_HEAD * (h + 1)], axis=1,
                                            keepdims=True) for h in range(N_HEADS)], axis=1)
    do, delta = mm(dr3b, w_o, tb=True, epi=head_dots, extras=(o,), out_dtypes=(F32, (N_HEADS, F32)), name="attn_out_dx")
    tb = min(ATT_TK, seq)
    lse_row = lse.reshape(N_HEADS, seq // tb, tb)
    delta_row = delta.T.reshape(N_HEADS, seq // tb, tb)
    dqn, dqr, dkvb, dkr = attn_bwd(qro, kvb, krope, do, lse_row, delta_row)

    def q_rope_bwd(dn, dr, cs, sn):
        parts = []
        for k in range(N_CHIPS):
            parts.append(dn[:, Q_CHIP_NOPE * k:Q_CHIP_NOPE * (k + 1)])
            parts.append(_rope_tile_bwd(dr[:, LANES * k:LANES * (k + 1)], cs, sn))
        return (jnp.concatenate(parts, axis=1),), ()
    (dqlin,) = rowwise(q_rope_bwd, (dqn, dqr, cos_q, sin_q), ((N_CHIPS * Q_CHIP, BF16),), name="q_rope_bwd")
    g["q_w_b"] = mm(cq, dqlin, ta=True, name="q_b_dw", tiles=(Q_LORA, Q_CHIP, None), out_view=_out_cols(q_w_b.shape))
    dcq = mm(dqlin, q_w_b, tb=True, n_dim=Q_LORA, tiles=(None, Q_LORA, Q_CHIP), b_view=_b_cols_t, name="q_b_dx")

    def q_norm_bwd(c, gq, d):
        dx, dgq = _rms_bwd(c, gq, d)
        return (dx,), (dgq,)
    dcq_raw, dqn_g = rowwise(q_norm_bwd, (cq_raw, qn_g, dcq), ((Q_LORA, BF16),), accs=(Q_LORA,), name="q_norm_bwd")
    g["q_w_a"] = mm(h2b, dcq_raw, ta=True, name="q_a_dw")
    g["kv_w_b"] = mm(ckv, dkvb, ta=True, name="kv_b_dw", tiles=(KV_LORA, kvb_tile, None), out_view=_out_cols(kv_w_b.shape))
    dckv = mm(dkvb, kv_w_b, tb=True, n_dim=KV_LORA, tiles=(None, KV_LORA, kvb_tile), b_view=_b_cols_t, name="kv_b_dx")
    dkr_sum = head_sum(dkr)

    def kv_post_bwd(kva, gk, dc, dk, cs, sn):
        dx, dgk = _rms_bwd(kva[:, :KV_LORA], gk, dc)
        dk = dk + pltpu.roll(dk, LANES - HALF_ROPE, 1)
        return (jnp.concatenate([dx, _rope_tile_bwd(dk, cs, sn)], axis=1),), (dgk,)
    dkva, dkvn_g = rowwise(kv_post_bwd, (kva, kvn_g, dckv, dkr_sum, cos_k, sin_k), ((KVA_PAD, BF16),),
                           accs=(KV_LORA,), name="kv_post_bwd")
    g["kv_w_a"] = mm(h2b, dkva, ta=True, name="kv_a_dw")
    dh2 = mm(dcq_raw, q_w_a, tb=True, epi=lambda r, d: (r + DN_ALPHA * d,), extras=(dr3,), name="q_a_dx")
    dh2 = mm(dkva, kv_w_a, tb=True, epi=lambda r, d: (r + d,), extras=(dh2,), name="kv_a_dx")

    dr2, dr2b, dg_f0, db_f0 = ln_bwd(h1, f1, ln("ln_ffn_g", 0), dh2, "ln_ffn_bwd_0")
    pack = put_rows(pack, packed_shards(g, MISC_EARLY, EARLY_ROWS - MISC_EARLY_OFF), MISC_EARLY_OFF)
    if comm is None:
        pack, dh1 = mlp_bwd(pack, dr2, dr2b, h1b, f1pre, 0)
    else:
        pack, dh1, (theirs,) = mlp_bwd(pack, dr2, dr2b, h1b, f1pre, 0, swap=True)
        early_sums = add_halves(pack, theirs, comm[1])
    dr1, dr1b, dg_m0, db_m0 = ln_bwd(x, mix0, ln("ln_mix_g", 0), dh1, "ln_mix_bwd_0")
    mid = mm(z, dr1b, ta=True, name="ssm_out_dw", tiles=(D_MODEL, PACK_W, None),
             out_view=into_rows(MID_OFF["ssm_w_out"], shard_rows, (N_CHIPS, MID_ROWS, PACK_W)))
    def glu_bwd(dz, vl, gt):
        sg = _sigmoid(gt)
        return (jnp.concatenate([dz * sg, dz * vl * sg * (1.0 - sg)], axis=1),)
    dvg = mm(dr1b, w_out, tb=True, epi=glu_bwd, extras=(val, gate), out_dtypes=((2 * D_MODEL, BF16),), name="ssm_out_dx")
    g["ssm_w_glu"] = mm(yg, dvg, ta=True, name="glu_proj_dw", tiles=(None, glu_tile, None), out_view=_out_cols(w_glu.shape))
    mid = put_rows(mid, packed_shards(g, MISC_MID, MID_ROWS - MISC_MID_OFF), MISC_MID_OFF)
    dypre = mm(dvg, w_glu, tb=True, epi=lambda r, y: (r * _gelu_grad(y),), extras=(ypre,), n_dim=D_MODEL,
               tiles=(None, D_MODEL, glu_tile), b_view=_b_cols_t, name="glu_proj_dx",
               ride=Together([SwapRide(mid), SendRide([(early_sums, (0, EARLY_HEAD), None)])]) if comm is not None else None)
    sends = None
    if comm is not None:
        dypre, (theirs, early_got) = dypre
        sends = SendRide([(early_sums, (EARLY_HEAD, EARLY_ROWS - EARLY_HEAD), early_got), add_halves(mid, theirs, comm[1])])
    (dx, dbbd_re, dbbd_im, dcbd_re, dcbd_imn, dar, dai, dd), got = s5_bwd(
        dypre, x, dr1, h_re, h_im, bbd_re, bbd_im, cbd_re, cbd_imn, a_re, a_im, dskip, sends)
    dbb_re = _blockdiag_in_t(dbbd_re).reshape(N_STATES, SSM_GROUP)
    dbb_im = _blockdiag_in_t(dbbd_im).reshape(N_STATES, SSM_GROUP)
    dlr, dli, dldt, db_re, db_im = s5_prep_bwd(lr, li, ldt, b_re, b_im, dar.reshape(N_STATES, 1),
                                               dai.reshape(N_STATES, 1), dbb_re, dbb_im)
    g["ssm_lam_re"] = dlr.reshape(1, N_GROUPS, SSM_STATE)
    g["ssm_lam_im"] = dli.reshape(1, N_GROUPS, SSM_STATE)
    g["ssm_log_dt"] = group_sum(dldt).reshape(1, N_GROUPS)
    g["ssm_b_re"] = db_re.reshape(1, N_GROUPS, SSM_STATE, SSM_GROUP)
    g["ssm_b_im"] = db_im.reshape(1, N_GROUPS, SSM_STATE, SSM_GROUP)
    g["ssm_c_re"] = _blockdiag_out_t(dcbd_re).reshape(1, N_GROUPS, SSM_GROUP, SSM_STATE)
    g["ssm_c_im"] = -_blockdiag_out_t(dcbd_imn).reshape(1, N_GROUPS, SSM_GROUP, SSM_STATE)
    g["ssm_d"] = dd
    g["ln_mix_g"] = jnp.concatenate([dg_m0, dg_m1], 0)
    g["ln_mix_b"] = jnp.concatenate([db_m0, db_m1], 0)
    g["ln_ffn_g"] = jnp.concatenate([dg_f0, dg_f1], 0)
    g["ln_ffn_b"] = jnp.concatenate([db_f0, db_f1], 0)
    g["kv_norm_g"] = dkvn_g.reshape(KV_LORA)
    g["q_norm_g"] = dqn_g
    return loss, dx, pack, mid, g, list(zip(sends.ins, got)) if comm is not None else None


def place(shard, me_idx, dtype, name, layer=None):
    rows, cols = shard.shape[-2:]
    tr = _tile(rows, (512, 256, 128))

    def body(m_ref, x_ref, o_ref):
        o_ref[...] = x_ref[...].astype(o_ref.dtype)

    in_spec = (pl.BlockSpec((tr, cols), lambda i, m: (i, 0)) if layer is None
               else pl.BlockSpec((None, tr, cols), lambda i, m: (layer, i, 0)))
    return _pcall(
        body, name=name,
        grid_spec=pltpu.PrefetchScalarGridSpec(
            num_scalar_prefetch=1, grid=(rows // tr,), in_specs=[in_spec],
            out_specs=pl.BlockSpec((None, tr, cols), lambda i, m: (m[0], i, 0))),
        out_shape=jax.ShapeDtypeStruct((N_CHIPS, rows, cols), dtype),
        compiler_params=_params(("parallel",)),
    )(me_idx, shard)


def place_many(shards, dtypes, me_idx, name):
    def body(m_ref, *refs):
        for x_ref, o_ref in zip(refs[:len(shards)], refs[len(shards):]):
            o_ref[...] = x_ref[...].astype(o_ref.dtype)

    return _pcall(
        body, name=name,
        grid_spec=pltpu.PrefetchScalarGridSpec(
            num_scalar_prefetch=1, grid=(1,),
            in_specs=[pl.BlockSpec(s.shape, lambda i, m: (0, 0)) for s in shards],
            out_specs=[pl.BlockSpec((None,) + s.shape, lambda i, m: (m[0], 0, 0)) for s in shards]),
        out_shape=[jax.ShapeDtypeStruct((N_CHIPS,) + s.shape, d) for s, d in zip(shards, dtypes)],
        compiler_params=_params(("arbitrary",)),
    )(me_idx, *shards)


def put_rows(pack, rows, off):
    _, n, cols = rows.shape

    def body(r_ref, p_ref, o_ref, sem):
        cp = pltpu.make_async_copy(r_ref.at[0], o_ref.at[pl.program_id(0), pl.ds(off, n), :], sem)
        cp.start()
        cp.wait()

    return _pcall(body, name="grad_put_rows", grid=(N_CHIPS,),
                  in_specs=[pl.BlockSpec((1, n, cols), lambda k: (k, 0, 0)), _ANY], out_specs=_ANY,
                  out_shape=jax.ShapeDtypeStruct(pack.shape, pack.dtype), input_output_aliases={1: 0},
                  scratch_shapes=[pltpu.SemaphoreType.DMA],
                  compiler_params=_params(("arbitrary",)))(rows, pack)


def _my_cols(c, mine=True):
    start = (c if mine else 1 - c) * HALF_W
    return pl.ds(pl.multiple_of(start, HALF_W), HALF_W)


def add_halves(gpack, got, c_idx):
    n, rows, _ = gpack.shape
    tr = min(G_BLOCK_ROWS, rows)
    blk = (None, tr, HALF_W)

    def body(c_ref, g_ref, r_ref, o_ref):
        o_ref[...] = (g_ref[...] + r_ref[...]).astype(o_ref.dtype)

    return _pcall(
        body, name="grad_add_halves",
        grid_spec=pltpu.PrefetchScalarGridSpec(
            num_scalar_prefetch=1, grid=(n, rows // tr),
            in_specs=[pl.BlockSpec(blk, lambda k, i, c: (k, i, c[0])), pl.BlockSpec(blk, lambda k, i, c: (k, i, 0))],
            out_specs=pl.BlockSpec(blk, lambda k, i, c: (k, i, 0))),
        out_shape=jax.ShapeDtypeStruct((n, rows, HALF_W), BF16),
        compiler_params=_params(("parallel", "parallel")),
    )(c_idx, gpack, got)


def sum_owner(part, got, idx, total_rows, row_off=0, into=None):
    _, rows, _ = part.shape
    tr = math.gcd(math.gcd(rows, row_off), G_BLOCK_ROWS)
    n_into = 0 if into is None else 1

    def body(m_ref, p_ref, g_ref, *rest):
        up = lambda v: v.astype(F32)
        rest[-1][...] = ((up(p_ref[...]) + up(g_ref[0])) + up(g_ref[1])) + up(g_ref[2])

    return _pcall(
        body, name="grad_sum_owner",
        grid_spec=pltpu.PrefetchScalarGridSpec(
            num_scalar_prefetch=1, grid=(rows // tr,),
            in_specs=[pl.BlockSpec((None, tr, HALF_W), lambda i, m: (m[0], i, 0)),
                      pl.BlockSpec((3, tr, HALF_W), lambda i, m: (0, i, 0))] + [_ANY] * n_into,
            out_specs=pl.BlockSpec((tr, HALF_W), lambda i, m: (row_off // tr + i, m[1]))),
        out_shape=jax.ShapeDtypeStruct((total_rows, PACK_W), F32),
        input_output_aliases={3: 0} if n_into else {},
        compiler_params=_params(("parallel",)),
    )(idx, part, got, *([into] if n_into else []))


def join_halves(red):
    def body(in_ref, out_ref, send_sem, recv_sem):
        x, y, c, _ = _place()
        sibling = (x, y, 1 - c)
        mine = out_ref.at[:, _my_cols(c)]
        cp = pltpu.make_async_remote_copy(src_ref=mine, dst_ref=mine, send_sem=send_sem, recv_sem=recv_sem,
                                          device_id=sibling, device_id_type=MESH)
        cp.start()
        cp.wait_send()
        other = out_ref.at[:, _my_cols(c, mine=False)]
        pltpu.make_async_remote_copy(src_ref=other, dst_ref=other, send_sem=send_sem, recv_sem=recv_sem,
                                     device_id=sibling, device_id_type=MESH).wait_recv()

    return _pcall(body, name="grad_join_halves", in_specs=[_ANY], out_specs=_ANY,
                  out_shape=jax.ShapeDtypeStruct(red.shape, red.dtype), input_output_aliases={0: 0},
                  scratch_shapes=[pltpu.SemaphoreType.DMA, pltpu.SemaphoreType.DMA])(red)


def adamw(gsrc, g_off, wt, m, v, name):
    n, cols = wt.shape
    tr = math.gcd(math.gcd(g_off, n), 256) if g_off else math.gcd(n, 256)
    off_blk = g_off // tr
    c1 = 1.0 / (1.0 - ADAM_B1 ** ADAM_STEP)
    c2 = 1.0 / (1.0 - ADAM_B2 ** ADAM_STEP)

    def body(g_ref, w_ref, m_ref, v_ref, go_ref, d_ref, mo_ref, vo_ref):
        gv = g_ref[...]
        mn = ADAM_B1 * m_ref[...] + (1.0 - ADAM_B1) * gv
        vn = ADAM_B2 * v_ref[...] + (1.0 - ADAM_B2) * gv * gv
        go_ref[...] = gv
        mo_ref[...] = mn
        vo_ref[...] = vn
        d_ref[...] = -ADAM_LR * ((mn * c1) / (jnp.sqrt(vn * c2) + ADAM_EPS) + ADAM_WD * w_ref[...])

    blk = pl.BlockSpec((tr, cols), lambda i: (i, 0))
    return _pcall(body, name=name, grid=(n // tr,),
                  in_specs=[pl.BlockSpec((tr, cols), lambda i: (off_blk + i, 0)), blk, blk, blk],
                  out_specs=[blk] * 4, out_shape=[jax.ShapeDtypeStruct((n, cols), F32)] * 4,
                  compiler_params=_params(("parallel",)))(gsrc, wt, m, v)


def _rows8(a):
    return -(-a.size // (8 * PACK_W)) * 8


def _as_rows(a, rows=None):
    flat = a.reshape(-1)
    n = _rows8(a) if rows is None else rows
    return jnp.pad(flat, (0, n * PACK_W - flat.shape[0])).reshape(n, PACK_W)


def local_shards_2d(wl):
    return {"w_ff1": [wl["w_ff1"][0], wl["w_ff1"][1]], "w_ff2": [wl["w_ff2"][0], wl["w_ff2"][1]],
            "ssm_w_glu": wl["ssm_w_glu"], "ssm_w_out": wl["ssm_w_out"], "kv_w_a": _pad_kva_cols(wl["kv_w_a"]),
            "kv_w_b": wl["kv_w_b"], "q_w_a": wl["q_w_a"], "q_w_b": _perm_q_cols(wl["q_w_b"]),
            "attn_w_o": wl["attn_w_o"], "ssm_d": wl["ssm_d"].reshape(2, -1)}


def misc_grad_shard(name, g, k):
    if name == "ssm_d":
        w = D_MODEL // N_CHIPS
        return g[:, w * k:w * (k + 1)]
    if name in ("ssm_w_glu", "kv_w_b"):
        return g[k]
    if name == "q_w_b":
        return _unperm_q_cols(g[k])
    rows = D_MODEL // N_CHIPS
    shard = g[rows * k:rows * (k + 1)]
    return _unpad_kva_cols(shard) if name == "kv_w_a" else shard


def packed_shards(g, names, rows, tail=None):
    blocks = []
    for k in range(N_CHIPS):
        parts = [_as_rows(misc_grad_shard(n, g[n], k), MISC_SHARD_ROWS[n]) for n in names]
        if tail is not None:
            parts.append(tail[k * (tail.shape[0] // N_CHIPS):(k + 1) * (tail.shape[0] // N_CHIPS)])
        blk = jnp.concatenate(parts, axis=0)
        blocks.append(jnp.pad(blk, ((0, rows - blk.shape[0]), (0, 0))))
    return jnp.stack(blocks)


def kernel(x, positions, ln_mix_g, ln_mix_b, ln_ffn_g, ln_ffn_b, w_ff1, w_ff2, ssm_lam_re, ssm_lam_im, ssm_log_dt, ssm_b_re, ssm_b_im, ssm_c_re, ssm_c_im, ssm_d, ssm_w_glu, ssm_w_out, kv_w_a, kv_norm_g, kv_w_b, q_w_a, q_norm_g, q_w_b, attn_w_o, loss_target, m_ln_mix_g, m_ln_mix_b, m_ln_ffn_g, m_ln_ffn_b, m_w_ff1, m_w_ff2, m_ssm_lam_re, m_ssm_lam_im, m_ssm_log_dt, m_ssm_b_re, m_ssm_b_im, m_ssm_c_re, m_ssm_c_im, m_ssm_d, m_ssm_w_glu, m_ssm_w_out, m_kv_w_a, m_kv_norm_g, m_kv_w_b, m_q_w_a, m_q_norm_g, m_q_w_b, m_attn_w_o, v_ln_mix_g, v_ln_mix_b, v_ln_ffn_g, v_ln_ffn_b, v_w_ff1, v_w_ff2, v_ssm_lam_re, v_ssm_lam_im, v_ssm_log_dt, v_ssm_b_re, v_ssm_b_im, v_ssm_c_re, v_ssm_c_im, v_ssm_d, v_ssm_w_glu, v_ssm_w_out, v_kv_w_a, v_kv_norm_g, v_kv_w_b, v_q_w_a, v_q_norm_g, v_q_w_b, v_attn_w_o):
    env = dict(locals())
    wl = {n: env[n] for n in WEIGHTS}
    ml = {n: env["m_" + n] for n in WEIGHTS}
    vl = {n: env["v_" + n] for n in WEIGHTS}
    for n in ("ssm_w_glu", "ssm_w_out", "q_w_a", "q_w_b", "attn_w_o"):
        wl[n], ml[n], vl[n] = wl[n][0], ml[n][0], vl[n][0]

    c_idx = lax.axis_index("c").astype(jnp.int32).reshape(1)
    me_idx = (2 * lax.axis_index("x") + lax.axis_index("y")).astype(jnp.int32).reshape(1)

    local = local_shards_2d(wl)
    stacked = {n: [place(wl[n], me_idx, BF16, f"place_{n}_{l}", layer=l) for l in range(DEPTH)] for n in ("w_ff1", "w_ff2")}
    others = [n for n in SHARDED if n not in stacked]
    stacked.update(zip(others, place_many([local[n] for n in others], [F32 if n == "ssm_d" else BF16 for n in others],
                                          me_idx, "place_others")))
    stacked["ssm_d"] = ride_alone(GatherRide([_halves(stacked["ssm_d"])]), "ssm_d_all_gather")[0].reshape(1, D_MODEL)
    for n in REPLICATED:
        stacked[n] = wl[n]

    loss_part, dx, early, mid, g, sent = device_step(x[0], positions[0], loss_target[0], stacked, comm=(me_idx, c_idx))
    loss = lax.psum(loss_part, ("x", "y", "c"))

    small = jnp.concatenate([_as_rows(g[n]) for n in REPLICATED], axis=0)
    small = jnp.pad(small, ((0, SMALL_ROWS - small.shape[0]), (0, 0)))
    late = packed_shards(g, MISC_LATE, LATE_ROWS, tail=small)
    late_sums = add_halves(late, ride_alone(SwapRide(late), "grad_swap_halves")[0], c_idx)
    sent.append((late_sums, ride_alone(SendRide([late_sums]), "grad_send_to_owners")[0]))
    where = jnp.concatenate([me_idx, c_idx])
    starts = (0, EARLY_ROWS, EARLY_ROWS + MID_ROWS)
    total_rows = EARLY_ROWS + MID_ROWS + LATE_ROWS
    reduced = None
    for (sums, got), off in zip(sent, starts):
        reduced = sum_owner(sums, got, where, total_rows, row_off=off, into=reduced)
    reduced = join_halves(reduced)
    quarter = reduced[starts[2] + SMALL_OFF:starts[2] + SMALL_OFF + SMALL_Q_ROWS]
    small_tot = ride_alone(GatherRide([_halves(place(quarter, me_idx, F32, "place_small_grads"))]),
                           "small_grad_all_gather")[0].reshape(SMALL_ROWS, PACK_W)

    out_g, out_d, out_m, out_v = {}, {}, {}, {}
    direct = {**EARLY_OFF, **{n: starts[1] + o for n, o in MID_OFF.items()}}
    for n, off in direct.items():
        res = adamw(reduced, off, wl[n].reshape(-1, PACK_W), ml[n].reshape(-1, PACK_W), vl[n].reshape(-1, PACK_W),
                    "adamw_" + n)
        out_g[n], out_d[n], out_m[n], out_v[n] = [a.reshape(env[n].shape) for a in res]
    for names, off in ((MISC_EARLY, MISC_EARLY_OFF), (MISC_MID, starts[1] + MISC_MID_OFF), (MISC_LATE, starts[2])):
        pack3 = lambda d: jnp.concatenate([_as_rows(d[n], MISC_SHARD_ROWS[n]) for n in names], axis=0)
        res = adamw(reduced, off, pack3(wl), pack3(ml), pack3(vl), "adamw_packed_" + names[0])
        r0 = 0
        for n in names:
            cnt = math.prod(env[n].shape)
            out_g[n], out_d[n], out_m[n], out_v[n] = [
                a[r0:r0 + MISC_SHARD_ROWS[n]].reshape(-1)[:cnt].reshape(env[n].shape) for a in res]
            r0 += MISC_SHARD_ROWS[n]
    ws = jnp.concatenate([_as_rows(wl[n]) for n in REPLICATED], axis=0)
    ms = jnp.concatenate([_as_rows(ml[n]) for n in REPLICATED], axis=0)
    vs = jnp.concatenate([_as_rows(vl[n]) for n in REPLICATED], axis=0)
    pad = ((0, SMALL_ROWS - ws.shape[0]), (0, 0))
    res = adamw(small_tot, 0, jnp.pad(ws, pad), jnp.pad(ms, pad), jnp.pad(vs, pad), "adamw_replicated")
    row = 0
    for n in REPLICATED:
        cnt = math.prod(env[n].shape)
        nrows = _rows8(env[n])
        out_g[n], out_d[n], out_m[n], out_v[n] = [a[row:row + nrows].reshape(-1)[:cnt].reshape(env[n].shape) for a in res]
        row += nrows

    return (loss, dx[None], *[out_g[n] for n in WEIGHTS], *[out_d[n] for n in WEIGHTS],
            *[out_m[n] for n in WEIGHTS], *[out_v[n] for n in WEIGHTS])
```

```python
import functools
import math

import jax
import jax.numpy as jnp
from jax import lax
from jax.experimental import pallas as pl
from jax.experimental.pallas import tpu as pltpu

F32 = jnp.float32
BF16 = jnp.bfloat16
MESH = pl.DeviceIdType.MESH

D_MODEL = 1024
DEPTH = 2
SSM_GROUP = 16
N_GROUPS = D_MODEL // SSM_GROUP
SSM_STATE = 64
N_STATES = N_GROUPS * SSM_STATE
N_HEADS = 8
QK_NOPE = 128
QK_ROPE = 64
HALF_ROPE = QK_ROPE // 2
V_HEAD = 128
QK_DIM = QK_NOPE + QK_ROPE
Q_LORA = 384
KV_LORA = 256
ROPE_THETA = 10000.0
SM_SCALE = QK_DIM ** -0.5
NEG_INF = -1e30
D_FF = 4 * D_MODEL
DN_ALPHA = (2 * DEPTH) ** 0.25
LN_EPS = 1e-5
RMS_EPS = 1e-6
ADAM_LR = 0.001
ADAM_B1 = 0.9
ADAM_B2 = 0.999
ADAM_EPS = 1e-08
ADAM_WD = 0.01
ADAM_STEP = 10

N_CHIPS = 4
LANES = 128
VMEM_LIMIT = 56 * 1024 * 1024
MM_VMEM_BUDGET = 40 * 1024 * 1024
PACK_W = 1024
KVA_PAD = 384
HALF_W = PACK_W // 2

SHARDED = ("w_ff1", "w_ff2", "ssm_w_glu", "ssm_w_out", "kv_w_a", "kv_w_b", "q_w_a", "q_w_b", "attn_w_o", "ssm_d")
G_BLOCK_ROWS = 960
EARLY_OFF = {"w_ff1": 0, "w_ff2": 2048, "attn_w_o": 4096}
MISC_EARLY = ("kv_w_b", "kv_w_a", "q_w_a", "q_w_b")
MISC_EARLY_OFF = 4352
EARLY_ROWS = 5 * G_BLOCK_ROWS
EARLY_HEAD = G_BLOCK_ROWS
MID_OFF = {"ssm_w_out": 0}
MISC_MID = ("ssm_w_glu",)
MISC_MID_OFF = 256
MID_ROWS = MISC_MID_OFF + 512
MISC_LATE = ("ssm_d",)
SMALL_Q_ROWS = 96
SMALL_ROWS = N_CHIPS * SMALL_Q_ROWS
SMALL_OFF = 16
LATE_ROWS = 192
MISC_SHARD_ROWS = {"ssm_d": 16, "ssm_w_glu": 512, "kv_w_b": 128, "kv_w_a": 80, "q_w_a": 96, "q_w_b": 144}
REPLICATED = ("ln_mix_g", "ln_mix_b", "ln_ffn_g", "ln_ffn_b", "ssm_lam_re", "ssm_lam_im", "ssm_log_dt",
              "ssm_b_re", "ssm_b_im", "ssm_c_re", "ssm_c_im", "kv_norm_g", "q_norm_g")
WEIGHTS = ("ln_mix_g", "ln_mix_b", "ln_ffn_g", "ln_ffn_b", "w_ff1", "w_ff2", "ssm_lam_re", "ssm_lam_im",
           "ssm_log_dt", "ssm_b_re", "ssm_b_im", "ssm_c_re", "ssm_c_im", "ssm_d", "ssm_w_glu", "ssm_w_out",
           "kv_w_a", "kv_norm_g", "kv_w_b", "q_w_a", "q_norm_g", "q_w_b", "attn_w_o")


def _pcall(body, **kw):
    return pl.pallas_call(body, **kw)


def _params(sem=None):
    return pltpu.CompilerParams(dimension_semantics=sem, vmem_limit_bytes=VMEM_LIMIT)


_ANY = pl.BlockSpec(memory_space=pl.ANY)


def _tile(dim, prefs):
    for p in prefs:
        if dim % p == 0:
            return p
    return dim


def _place():
    x, y, c = lax.axis_index("x"), lax.axis_index("y"), lax.axis_index("c")
    return x, y, c, [(1 - x, y), (x, 1 - y), (1 - x, 1 - y)]


def _remote(k, src, dst, to, send_sems, recv_sems):
    return pltpu.make_async_remote_copy(src_ref=src, dst_ref=dst, send_sem=send_sems.at[k], recv_sem=recv_sems.at[k],
                                        device_id=to, device_id_type=MESH)


class GatherRide:
    def __init__(self, arrs):
        self.ins = list(arrs)
        self.out_shapes = [jax.ShapeDtypeStruct(a.shape, a.dtype) for a in arrs]
        self.aliases = {i: i for i in range(len(arrs))}
        self.n_sems = 6 * len(arrs)

    def start(self, ins, outs, send_sems, recv_sems):
        x, y, c, chips = _place()
        me = 2 * x + y
        for a, o in enumerate(outs):
            for j, (px, py) in enumerate(chips):
                _remote(6 * a + j, o.at[me, c], o.at[me, c], (px, py, c), send_sems, recv_sems).start()

    def pass_on(self, ins, outs, send_sems, recv_sems):
        x, y, c, chips = _place()
        for a, o in enumerate(outs):
            for j, (px, py) in enumerate(chips):
                blk = o.at[2 * px + py, c]
                _remote(6 * a + j, blk, blk, (px, py, c), send_sems, recv_sems).wait_recv()
                _remote(6 * a + 3 + j, blk, blk, (x, y, 1 - c), send_sems, recv_sems).start()

    def finish(self, ins, outs, send_sems, recv_sems, passed_on=False):
        if not passed_on:
            self.pass_on(ins, outs, send_sems, recv_sems)
        x, y, c, chips = _place()
        me = 2 * x + y
        sibling = (x, y, 1 - c)
        for a, o in enumerate(outs):
            for j, (px, py) in enumerate(chips):
                blk = o.at[2 * px + py, 1 - c]
                _remote(6 * a + 3 + j, blk, blk, sibling, send_sems, recv_sems).wait_recv()
                _remote(6 * a + j, o.at[me, c], o.at[me, c], (px, py, c), send_sems, recv_sems).wait_send()
                mine = o.at[2 * px + py, c]
                _remote(6 * a + 3 + j, mine, mine, sibling, send_sems, recv_sems).wait_send()


class SendRide:
    base = 0

    def __init__(self, parts):
        parts = [p if isinstance(p, tuple) else (p, (0, p.shape[1]), None) for p in parts]
        self.rows = [rows for _, rows, _ in parts]
        self.n_parts = len(parts)
        self.ins = [p for p, _, _ in parts] + [into for _, _, into in parts if into is not None]
        self.out_shapes = [jax.ShapeDtypeStruct((3,) + p.shape[1:], p.dtype) for p, _, _ in parts]
        given = [a for a, (_, _, into) in enumerate(parts) if into is not None]
        self.aliases = {self.n_parts + i: a for i, a in enumerate(given)}
        self.n_sems = 3 * self.n_parts

    def _copies(self, ins, outs, send_sems, recv_sems):
        x, y, c, chips = _place()
        return [_remote(self.base + 3 * a + j, ins[a].at[2 * px + py, pl.ds(r0, n)], outs[a].at[j, pl.ds(r0, n)],
                        (px, py, c), send_sems, recv_sems)
                for a, (r0, n) in enumerate(self.rows) for j, (px, py) in enumerate(chips)]

    def start(self, ins, outs, send_sems, recv_sems):
        for cp in self._copies(ins, outs, send_sems, recv_sems):
            cp.start()

    def finish(self, ins, outs, send_sems, recv_sems):
        for cp in self._copies(ins, outs, send_sems, recv_sems):
            cp.wait()


class SwapRide:
    base = 0

    def __init__(self, pack, ranges=None, into=None):
        self.ins = [pack] if into is None else [pack, into]
        self.out_shapes = [jax.ShapeDtypeStruct(pack.shape[:2] + (HALF_W,), pack.dtype)]
        self.aliases = {} if into is None else {1: 0}
        self.ranges = ranges or [(0, pack.shape[1])]
        self.n_sems = len(self.ranges)

    def _copies(self, ins, outs, send_sems, recv_sems):
        x, y, c, _ = _place()
        return [_remote(self.base + k, ins[0].at[:, pl.ds(r0, n), _my_cols(c, mine=False)], outs[0].at[:, pl.ds(r0, n), :],
                        (x, y, 1 - c), send_sems, recv_sems) for k, (r0, n) in enumerate(self.ranges)]

    def start(self, ins, outs, send_sems, recv_sems):
        for cp in self._copies(ins, outs, send_sems, recv_sems):
            cp.start()

    def finish(self, ins, outs, send_sems, recv_sems):
        for cp in self._copies(ins, outs, send_sems, recv_sems):
            cp.wait()


class Together:
    def __init__(self, rides):
        self.rides = rides
        self.ins, self.out_shapes, self.aliases, self.n_sems = [], [], {}, 0
        for r in rides:
            r.base = self.n_sems
            self.aliases.update({len(self.ins) + i: len(self.out_shapes) + o for i, o in r.aliases.items()})
            self.ins += r.ins
            self.out_shapes += r.out_shapes
            self.n_sems += r.n_sems

    def _each(self, step, ins, outs, send_sems, recv_sems):
        i = o = 0
        for r in self.rides:
            getattr(r, step)(ins[i:i + len(r.ins)], outs[o:o + len(r.out_shapes)], send_sems, recv_sems)
            i, o = i + len(r.ins), o + len(r.out_shapes)

    def start(self, *refs):
        self._each("start", *refs)

    def finish(self, *refs):
        self._each("finish", *refs)


def _pcall_riding(body, args, ride, first, last, *, in_specs, out_specs, out_shape, scratch_shapes=(), middle=None,
                  **kw):
    n_in, n_out = len(args), len(out_shape)
    if ride is None:
        return _pcall(body, in_specs=in_specs, out_specs=out_specs, out_shape=out_shape,
                      scratch_shapes=list(scratch_shapes), **kw)(*args), []
    k_in, k_out = len(ride.ins), len(ride.out_shapes)

    def riding(*refs):
        ins, r_in = refs[:n_in], refs[n_in:n_in + k_in]
        outs = refs[n_in + k_in:n_in + k_in + n_out]
        r_out = refs[n_in + k_in + n_out:n_in + k_in + n_out + k_out]
        scratch, (send_sems, recv_sems) = refs[n_in + k_in + n_out + k_out:-2], refs[-2:]

        @pl.when(first())
        def _():
            ride.start(r_in, r_out, send_sems, recv_sems)

        if middle is not None:
            @pl.when(middle())
            def _():
                ride.pass_on(r_in, r_out, send_sems, recv_sems)

        body(*ins, *outs, *scratch)

        @pl.when(last())
        def _():
            if middle is not None:
                ride.finish(r_in, r_out, send_sems, recv_sems, passed_on=True)
            else:
                ride.finish(r_in, r_out, send_sems, recv_sems)

    res = _pcall(riding, in_specs=list(in_specs) + [_ANY] * k_in, out_specs=list(out_specs) + [_ANY] * k_out,
                 out_shape=list(out_shape) + ride.out_shapes,
                 input_output_aliases={n_in + i: n_out + o for i, o in ride.aliases.items()},
                 scratch_shapes=list(scratch_shapes) + [pltpu.SemaphoreType.DMA((ride.n_sems,))] * 2,
                 **kw)(*args, *ride.ins)
    return res[:n_out], res[n_out:]


def ride_alone(ride, name):
    def body(*refs):
        n = len(ride.ins)
        ins, outs, (send_sems, recv_sems) = refs[:n], refs[n:-2], refs[-2:]
        ride.start(ins, outs, send_sems, recv_sems)
        ride.finish(ins, outs, send_sems, recv_sems)

    return _pcall(body, name=name, in_specs=[_ANY] * len(ride.ins), out_specs=[_ANY] * len(ride.out_shapes),
                  out_shape=ride.out_shapes, input_output_aliases=dict(ride.aliases),
                  scratch_shapes=[pltpu.SemaphoreType.DMA((ride.n_sems,))] * 2)(*ride.ins)


def mm(a, b, *, name, ta=False, tb=False, pro_a=None, epi=None, extras=(), out_dtypes=(F32,), n_dim=None,
       tiles=(None, None, None), b_view=None, out_view=None, into=None, ride=None, accs=()):
    widths = [d[0] if isinstance(d, tuple) else None for d in out_dtypes]
    out_dtypes = [d[1] if isinstance(d, tuple) else d for d in out_dtypes]
    if ta:
        k_dim, m_dim = a.shape
    else:
        m_dim, k_dim = a.shape
    if n_dim is None:
        n_dim = b.shape[0] if tb else b.shape[1]
    tn = tiles[1] or (n_dim if n_dim <= 1024 else _tile(n_dim, (1024, 512, 256, 128)))
    tk = tiles[2] or (k_dim if k_dim <= 1024 else _tile(k_dim, (1024, 512, 256, 128)))
    nk = k_dim // tk

    def vmem_bytes(tm):
        blocks = tm * tk * a.dtype.itemsize + tk * tn * b.dtype.itemsize
        blocks += sum(tm * (tn if e.shape[1] == n_dim else e.shape[1]) * e.dtype.itemsize for e in extras if e.shape[0] > 1)
        blocks += tm * sum((w or tn) * jnp.dtype(d).itemsize for w, d in zip(widths, out_dtypes))
        return 2 * blocks + tm * tn * 4

    tm = tiles[0] or next((t for t in (4096, 2048, 1024, 512, 256) if m_dim % t == 0 and vmem_bytes(t) <= MM_VMEM_BUDGET),
                          _tile(m_dim, (128,)))
    assert m_dim % tm == 0 and n_dim % tn == 0 and k_dim % tk == 0, (name, m_dim, n_dim, k_dim, tm, tn, tk)
    assert tn == n_dim or not (any(widths) or accs), name
    n_ex, n_out = len(extras), len(out_dtypes)
    n_into = 0 if into is None else 1
    dims = (((0 if ta else 1,), (1 if tb else 0,)), ((), ()))

    def body(a_ref, b_ref, *rest):
        ex_refs, out_refs = rest[:n_ex], rest[n_ex + n_into:n_ex + n_into + n_out]
        sum_refs = rest[n_ex + n_into + n_out:n_ex + n_into + n_out + len(accs)]

        def partial():
            av = a_ref[...]
            if pro_a is not None:
                av = pro_a(av)
            return lax.dot_general(av.astype(BF16), b_ref[...].astype(BF16), dims, preferred_element_type=F32)

        def finish(r):
            res = epi(r, *[e[...] for e in ex_refs]) if epi is not None else (r,)
            for o_ref, v in zip(out_refs, res):
                o_ref[...] = v.reshape(o_ref.shape).astype(o_ref.dtype)
            if accs:
                @pl.when(pl.program_id(0) == 0)
                def _():
                    for s_ref in sum_refs:
                        s_ref[...] = jnp.zeros_like(s_ref)

                for s_ref, v in zip(sum_refs, res[n_out:]):
                    s_ref[...] += v

        if nk == 1:
            finish(partial())
            return
        acc = rest[-1]
        k = pl.program_id(2)

        @pl.when(k == 0)
        def _():
            acc[...] = partial()

        @pl.when(k > 0)
        def _():
            acc[...] += partial()

        @pl.when(k == nk - 1)
        def _():
            finish(acc[...])

    def ex_spec(e):
        if e.shape == (m_dim, n_dim):
            return o_spec
        if e.shape[0] == m_dim:
            return pl.BlockSpec((tm, e.shape[1]), lambda i, j, k: (i, 0))
        return pl.BlockSpec(e.shape, lambda i, j, k: (0, 0))

    a_spec = pl.BlockSpec((tk, tm), lambda i, j, k: (k, i)) if ta else pl.BlockSpec((tm, tk), lambda i, j, k: (i, k))
    if b_view is not None:
        b_spec = b_view(tk, tn)
    else:
        b_spec = pl.BlockSpec((tn, tk), lambda i, j, k: (j, k)) if tb else pl.BlockSpec((tk, tn), lambda i, j, k: (k, j))
    o_spec = pl.BlockSpec((tm, tn), lambda i, j, k: (i, j))
    if out_view is None:
        out_specs = [o_spec if w is None else pl.BlockSpec((tm, w), lambda i, j, k: (i, 0)) for w in widths]
        out_shape = [jax.ShapeDtypeStruct((m_dim, w or n_dim), dt) for w, dt in zip(widths, out_dtypes)]
    else:
        assert n_out == 1
        out_specs = [out_view[1](tm, tn)]
        out_shape = [jax.ShapeDtypeStruct(out_view[0], out_dtypes[0])]
    out_specs = out_specs + [pl.BlockSpec((1, w), lambda i, j, k: (0, 0)) for w in accs]
    out_shape = out_shape + [jax.ShapeDtypeStruct((1, w), F32) for w in accs]
    grid = (m_dim // tm, n_dim // tn, nk)
    scratch = [pltpu.VMEM((tm, tn), F32)] if nk > 1 else []
    if ride is not None:
        assert into is None
        at = lambda ids: functools.reduce(jnp.logical_and, [pl.program_id(d) == i for d, i in enumerate(ids)])
        outs, landed = _pcall_riding(
            body, (a, b, *extras), ride, lambda: at((0, 0, 0)), lambda: at([g - 1 for g in grid]),
            name=name, grid=grid, in_specs=[a_spec, b_spec] + [ex_spec(e) for e in extras], out_specs=out_specs,
            out_shape=out_shape, scratch_shapes=scratch, compiler_params=_params(("arbitrary",) * 3))
        return (outs[0] if len(outs) == 1 else outs), landed
    outs = _pcall(
        body, name=name, grid=grid,
        in_specs=[a_spec, b_spec] + [ex_spec(e) for e in extras] + [_ANY] * n_into,
        out_specs=out_specs, out_shape=out_shape,
        input_output_aliases={2 + n_ex: 0} if n_into else {},
        scratch_shapes=scratch,
        compiler_params=_params(("arbitrary",) * 3 if accs else ("parallel", "parallel", "arbitrary")),
    )(a, b, *extras, *([into] if n_into else []))
    return outs[0] if len(outs) == 1 else outs


def rowwise(fn, ins, outs, *, name, accs=(), tm=256):
    rows = ins[0].shape[0]
    tm = min(tm, rows)
    n_in, n_out, n_acc = len(ins), len(outs), len(accs)

    def body(*refs):
        in_refs, out_refs, acc_refs = refs[:n_in], refs[n_in:n_in + n_out], refs[n_in + n_out:]
        res, sums = fn(*[r[...] for r in in_refs])
        for o_ref, v in zip(out_refs, res):
            o_ref[...] = v.astype(o_ref.dtype)
        if n_acc:
            @pl.when(pl.program_id(0) == 0)
            def _():
                for a_ref in acc_refs:
                    a_ref[...] = jnp.zeros_like(a_ref)

            for a_ref, s in zip(acc_refs, sums):
                a_ref[...] += s

    def spec(arr):
        if arr.shape[0] == rows:
            return pl.BlockSpec((tm, arr.shape[1]), lambda i: (i, 0))
        return pl.BlockSpec(arr.shape, lambda i: (0, 0))

    res = _pcall(
        body, name=name, grid=(rows // tm,),
        in_specs=[spec(a) for a in ins],
        out_specs=[pl.BlockSpec((tm, w), lambda i: (i, 0)) for w, _ in outs]
        + [pl.BlockSpec((1, w), lambda i: (0, 0)) for w in accs],
        out_shape=[jax.ShapeDtypeStruct((rows, w), dt) for w, dt in outs]
        + [jax.ShapeDtypeStruct((1, w), F32) for w in accs],
        compiler_params=_params(("arbitrary",) if n_acc else ("parallel",)),
    )(*ins)
    return res


def _relu2(v):
    r = jnp.maximum(v, 0.0)
    return r * r


def _gelu(x):
    c = math.sqrt(2.0 / math.pi)
    return 0.5 * x * (1.0 + jnp.tanh(c * (x + 0.044715 * x * x * x)))


def _gelu_grad(x):
    c = math.sqrt(2.0 / math.pi)
    t = jnp.tanh(c * (x + 0.044715 * x * x * x))
    return 0.5 * (1.0 + t) + 0.5 * x * (1.0 - t * t) * c * (1.0 + 3 * 0.044715 * x * x)


def _sigmoid(x):
    return 1.0 / (1.0 + jnp.exp(-x))


def _layer_norm(h, mix, g, b):
    r = DN_ALPHA * h + mix
    mu = jnp.mean(r, axis=-1, keepdims=True)
    xc = r - mu
    var = jnp.mean(xc * xc, axis=-1, keepdims=True)
    return xc * lax.rsqrt(var + LN_EPS) * g + b


def _layer_norm_bwd(h, mix, g, dy):
    r = DN_ALPHA * h + mix
    mu = jnp.mean(r, axis=-1, keepdims=True)
    xc = r - mu
    var = jnp.mean(xc * xc, axis=-1, keepdims=True)
    rstd = lax.rsqrt(var + LN_EPS)
    xhat = xc * rstd
    dxh = dy * g
    m1 = jnp.mean(dxh, axis=-1, keepdims=True)
    m2 = jnp.mean(dxh * xhat, axis=-1, keepdims=True)
    dr = rstd * (dxh - m1 - xhat * m2)
    return dr, jnp.sum(dy * xhat, axis=0, keepdims=True), jnp.sum(dy, axis=0, keepdims=True)


def ln_bwd(h, mix, g, dy, name):
    def fn(h, mix, g, dy):
        dr, dg, db = _layer_norm_bwd(h, mix, g, dy)
        return (dr, dr), (dg, db)
    return rowwise(fn, (h, mix, g, dy), ((D_MODEL, F32), (D_MODEL, BF16)), accs=(D_MODEL, D_MODEL), name=name)


def _rms(x, g):
    r = lax.rsqrt(jnp.mean(x * x, axis=-1, keepdims=True) + RMS_EPS)
    return x * r * g


def _rms_bwd(x, g, dy):
    r = lax.rsqrt(jnp.mean(x * x, axis=-1, keepdims=True) + RMS_EPS)
    xn = x * r
    dyg = dy * g
    dx = r * (dyg - xn * jnp.mean(dyg * xn, axis=-1, keepdims=True))
    return dx, jnp.sum(dy * xn, axis=0, keepdims=True)


def _s5_disc(lr, li, ldt):
    dt = jnp.exp(ldt)
    mag = jnp.exp(lr * dt)
    cs, sn = jnp.cos(li * dt), jnp.sin(li * dt)
    ar, ai = mag * cs, mag * sn
    inv = 1.0 / (lr * lr + li * li)
    n_re = (ar - 1.0) * lr + ai * li
    n_im = ai * lr - (ar - 1.0) * li
    return dt, mag, cs, sn, ar, ai, inv, n_re, n_im


def s5_prep(lr, li, ldt, b_re, b_im):
    def fn(lr, li, ldt, b_re, b_im):
        _, _, _, _, ar, ai, inv, n_re, n_im = _s5_disc(lr, li, ldt)
        cr, ci = n_re * inv, n_im * inv
        return (ar, ai, cr * b_re - ci * b_im, cr * b_im + ci * b_re), ()
    return rowwise(fn, (lr, li, ldt, b_re, b_im), ((1, F32), (1, F32), (SSM_GROUP, F32), (SSM_GROUP, F32)),
                   name="s5_prep", tm=512)


def s5_prep_bwd(lr, li, ldt, b_re, b_im, dar, dai, dbb_re, dbb_im):
    def fn(lr, li, ldt, b_re, b_im, dar, dai, dbb_re, dbb_im):
        dt, mag, cs, sn, ar, ai, inv, n_re, n_im = _s5_disc(lr, li, ldt)
        cr, ci = n_re * inv, n_im * inv
        db_re = cr * dbb_re + ci * dbb_im
        db_im = cr * dbb_im - ci * dbb_re
        dcr = jnp.sum(dbb_re * b_re + dbb_im * b_im, axis=-1, keepdims=True)
        dci = jnp.sum(dbb_im * b_re - dbb_re * b_im, axis=-1, keepdims=True)
        dar = dar + (dcr * lr - dci * li) * inv
        dai = dai + (dcr * li + dci * lr) * inv
        dinv = dcr * n_re + dci * n_im
        dlr = (dcr * (ar - 1.0) + dci * ai) * inv - 2.0 * lr * inv * inv * dinv
        dli = (dcr * ai - dci * (ar - 1.0)) * inv - 2.0 * li * inv * inv * dinv
        dmag = dar * cs + dai * sn
        dth = dai * ar - dar * ai
        dlr = dlr + dmag * mag * dt
        dli = dli + dth * dt
        ddt = dmag * mag * lr + dth * li
        return (dlr, dli, ddt * dt, db_re, db_im), ()
    return rowwise(fn, (lr, li, ldt, b_re, b_im, dar, dai, dbb_re, dbb_im),
                   ((1, F32), (1, F32), (1, F32), (SSM_GROUP, F32), (SSM_GROUP, F32)), name="s5_prep_bwd", tm=512)


def group_sum(x):
    def body(x_ref, o_ref):
        o_ref[...] = jnp.sum(x_ref[...], axis=1)
    return _pcall(body, name="s5_group_sum", out_shape=jax.ShapeDtypeStruct((N_GROUPS, 1), F32))(
        x.reshape(N_GROUPS, SSM_STATE, 1))


GROUPS_PER_TILE = LANES // SSM_GROUP
TILE_STATES = GROUPS_PER_TILE * SSM_STATE
N_UTILES = D_MODEL // LANES


SUBLANES = 8
SCAN_STRIP = 1024
N_STRIPS = N_STATES // SCAN_STRIP
_NT = (((1,), (1,)), ((), ()))
_TN = (((0,), (0,)), ((), ()))


def _scan_coefs(are, aim, shifted, reverse):
    ar = are[...]
    ai = -aim[...] if reverse else aim[...]
    powers = {1: (ar, ai)}
    for d in (2, 4):
        r, i = powers[d // 2]
        powers[d] = (r * r - i * i, 2.0 * r * i)
    rid = lax.broadcasted_iota(jnp.int32, (SUBLANES, N_STATES), 0)
    first = (rid == SUBLANES - 1) if reverse else (rid == 0)
    masks = [(1, first)] + [(d, (rid <= SUBLANES - 1 - d) if reverse else (rid >= d)) for d in (1, 2, 4)]
    for n, (d, keep) in enumerate(masks):
        for part in (0, 1):
            shifted[2 * n + part][...] = jnp.where(keep, jnp.broadcast_to(powers[d][part], (SUBLANES, N_STATES)), 0.0)


def _tile_scan(xr, xi, shifted, nbr_re, nbr_im, reverse):
    for n, d in enumerate((1, 1, 2, 4)):
        by = SUBLANES - d if reverse else d
        fr, fi = (nbr_re, nbr_im) if n == 0 else (xr, xi)
        sr, si = pltpu.roll(fr, by, 0), pltpu.roll(fi, by, 0)
        kr, ki = shifted[2 * n], shifted[2 * n + 1]
        xr, xi = xr + kr * sr - ki * si, xi + kr * si + ki * sr
    return xr, xi


def _tile_rows(t):
    return pl.ds(pl.multiple_of(t * SUBLANES, SUBLANES), SUBLANES)


def s5_fwd(u, bbd_re, bbd_im, cbd_re, cbd_imn, a_re, a_im, dskip, ride=None, t_rows=256):
    seq = u.shape[0]
    t_rows = min(t_rows, seq)
    n_tiles = t_rows // SUBLANES

    def body(u_ref, bre, bim, cre, cimn, are, aim, d_ref, y_ref, gelu_ref, hre_ref, him_ref, car_re, car_im, *shifted):
        @pl.when(pl.program_id(0) == 0)
        def _():
            car_re[...] = jnp.zeros_like(car_re)
            car_im[...] = jnp.zeros_like(car_im)
            _scan_coefs(are, aim, shifted, reverse=False)

        uf = u_ref[...]
        ub = uf.astype(BF16)
        for j in range(N_UTILES):
            uj = ub[:, LANES * j:LANES * (j + 1)]
            sl = slice(TILE_STATES * j, TILE_STATES * (j + 1))
            hre_ref[:, sl] = jnp.dot(uj, bre[j], preferred_element_type=F32)
            him_ref[:, sl] = jnp.dot(uj, bim[j], preferred_element_type=F32)
        for s in range(N_STRIPS):
            cols = pl.ds(s * SCAN_STRIP, SCAN_STRIP)
            coefs = [c[:, cols] for c in shifted]

            def step(t, before):
                rows = _tile_rows(t)
                hr, hi = _tile_scan(hre_ref[rows, cols], him_ref[rows, cols], coefs, before[0], before[1], False)
                hre_ref[rows, cols] = hr
                him_ref[rows, cols] = hi
                return hr, hi

            cr, ci = lax.fori_loop(0, n_tiles, step, (car_re[:, cols], car_im[:, cols]))
            car_re[:, cols] = cr
            car_im[:, cols] = ci
        dv = d_ref[...]
        for j in range(N_UTILES):
            st = slice(TILE_STATES * j, TILE_STATES * (j + 1))
            yj = (jnp.dot(hre_ref[:, st].astype(BF16), cre[j], preferred_element_type=F32)
                  + jnp.dot(him_ref[:, st].astype(BF16), cimn[j], preferred_element_type=F32))
            sl = slice(LANES * j, LANES * (j + 1))
            yj = yj + dv[:, sl] * uf[:, sl]
            y_ref[:, sl] = yj
            gelu_ref[:, sl] = _gelu(yj).astype(gelu_ref.dtype)

    full3 = lambda a: pl.BlockSpec(a.shape, lambda i: (0, 0, 0))
    full2 = lambda a: pl.BlockSpec(a.shape, lambda i: (0, 0))
    tile = pltpu.VMEM((SUBLANES, N_STATES), F32)
    n_chunks = seq // t_rows
    return _pcall_riding(
        body, (u, bbd_re, bbd_im, cbd_re, cbd_imn, a_re, a_im, dskip), ride,
        lambda: pl.program_id(0) == 0, lambda: pl.program_id(0) == n_chunks - 1,
        middle=(lambda: pl.program_id(0) == (7 * n_chunks) // 8) if ride is not None else None,
        name="s5_fwd", grid=(n_chunks,),
        in_specs=[pl.BlockSpec((t_rows, D_MODEL), lambda i: (i, 0)), full3(bbd_re), full3(bbd_im), full3(cbd_re),
                  full3(cbd_imn), full2(a_re), full2(a_im), full2(dskip)],
        out_specs=[pl.BlockSpec((t_rows, D_MODEL), lambda i: (i, 0)),
                   pl.BlockSpec((t_rows, D_MODEL), lambda i: (i, 0)),
                   pl.BlockSpec((t_rows, N_STATES), lambda i: (i, 0)),
                   pl.BlockSpec((t_rows, N_STATES), lambda i: (i, 0))],
        out_shape=[jax.ShapeDtypeStruct((seq, D_MODEL), F32),
                   jax.ShapeDtypeStruct((seq, D_MODEL), BF16),
                   jax.ShapeDtypeStruct((seq, N_STATES), F32),
                   jax.ShapeDtypeStruct((seq, N_STATES), F32)],
        scratch_shapes=[tile] * 10,
        compiler_params=_params(("arbitrary",)))


def s5_bwd(dy, u, dres, h_re, h_im, bbd_re, bbd_im, cbd_re, cbd_imn, a_re, a_im, dskip, ride=None, t_rows=256):
    seq = u.shape[0]
    t_rows = min(t_rows, seq)
    n_chunks = seq // t_rows

    n_tiles = t_rows // SUBLANES

    def body(dy_ref, u_ref, dres_ref, hre_ref, him_ref, hpre_ref, hpim_ref, bre, bim, cre, cimn, are, aim, d_ref,
             dx_ref, dbre, dbim, dcre, dcimn, dar_ref, dai_ref, dd_ref, lre, lim, car_re, car_im, acc_re, acc_im,
             *shifted):
        i = pl.program_id(0)

        @pl.when(i == 0)
        def _():
            for r in (car_re, car_im, acc_re, acc_im, dbre, dbim, dcre, dcimn, dd_ref):
                r[...] = jnp.zeros_like(r)
            _scan_coefs(are, aim, shifted, reverse=True)

        dyf = dy_ref[...]
        dyb = dyf.astype(BF16)
        uf = u_ref[...]
        ub = uf.astype(BF16)
        for j in range(N_UTILES):
            dyj = dyb[:, LANES * j:LANES * (j + 1)]
            st = slice(TILE_STATES * j, TILE_STATES * (j + 1))
            lre[:, st] = lax.dot_general(dyj, cre[j], _NT, preferred_element_type=F32)
            lim[:, st] = lax.dot_general(dyj, cimn[j], _NT, preferred_element_type=F32)
        has_pred = (i < n_chunks - 1).astype(F32)
        last_row = lax.broadcasted_iota(jnp.int32, (SUBLANES, SCAN_STRIP), 0) == SUBLANES - 1
        for s in range(N_STRIPS):
            cols = pl.ds(s * SCAN_STRIP, SCAN_STRIP)
            coefs = [c[:, cols] for c in shifted]
            before_re, before_im = hpre_ref[:, cols] * has_pred, hpim_ref[:, cols] * has_pred

            def step(k, carry):
                after_re, after_im, dar, dai = carry
                t = n_tiles - 1 - k
                rows = _tile_rows(t)
                lr, li = _tile_scan(lre[rows, cols], lim[rows, cols], coefs, after_re, after_im, True)
                lre[rows, cols] = lr
                lim[rows, cols] = li
                prev = _tile_rows(jnp.maximum(t - 1, 0))
                pre_re = jnp.where(t == 0, before_re, hre_ref[prev, cols])
                pre_im = jnp.where(t == 0, before_im, him_ref[prev, cols])
                hpr = pltpu.roll(jnp.where(last_row, pre_re, hre_ref[rows, cols]), 1, 0)
                hpi = pltpu.roll(jnp.where(last_row, pre_im, him_ref[rows, cols]), 1, 0)
                return lr, li, dar + lr * hpr + li * hpi, dai + li * hpr - lr * hpi

            cr, ci, dar, dai = lax.fori_loop(0, n_tiles, step, (car_re[:, cols], car_im[:, cols],
                                                               acc_re[:, cols], acc_im[:, cols]))
            car_re[:, cols] = cr
            car_im[:, cols] = ci
            acc_re[:, cols] = dar
            acc_im[:, cols] = dai

        dv = d_ref[...]
        for j in range(N_UTILES):
            sl = slice(LANES * j, LANES * (j + 1))
            st = slice(TILE_STATES * j, TILE_STATES * (j + 1))
            lrj = lre[:, st].astype(BF16)
            lij = lim[:, st].astype(BF16)
            du = (lax.dot_general(lrj, bre[j], _NT, preferred_element_type=F32)
                  + lax.dot_general(lij, bim[j], _NT, preferred_element_type=F32))
            dx_ref[:, sl] = du + dv[:, sl] * dyf[:, sl] + DN_ALPHA * dres_ref[:, sl]
            uj = ub[:, sl]
            dbre[j] += lax.dot_general(uj, lrj, _TN, preferred_element_type=F32)
            dbim[j] += lax.dot_general(uj, lij, _TN, preferred_element_type=F32)
            dyj = dyb[:, sl]
            dcre[j] += lax.dot_general(hre_ref[:, st].astype(BF16), dyj, _TN, preferred_element_type=F32)
            dcimn[j] += lax.dot_general(him_ref[:, st].astype(BF16), dyj, _TN, preferred_element_type=F32)
        dd_ref[...] += jnp.sum(dyf * uf, axis=0, keepdims=True)

        @pl.when(i == n_chunks - 1)
        def _():
            dar_ref[...] = jnp.sum(acc_re[...], axis=0, keepdims=True)
            dai_ref[...] = jnp.sum(acc_im[...], axis=0, keepdims=True)

    rev = lambda i: (n_chunks - 1 - i, 0)
    prev_tile = lambda i: (jnp.maximum((n_chunks - 1 - i) * n_tiles - 1, 0), 0)
    once = pl.Buffered(1)
    full3 = lambda a: pl.BlockSpec(a.shape, lambda i: (0, 0, 0), pipeline_mode=once)
    full2 = lambda a: pl.BlockSpec(a.shape, lambda i: (0, 0), pipeline_mode=once)
    acc3 = lambda shape: pl.BlockSpec(shape, lambda i: (0, 0, 0))
    acc2 = lambda shape: pl.BlockSpec(shape, lambda i: (0, 0))
    tile = pltpu.VMEM((SUBLANES, N_STATES), F32)
    return _pcall_riding(
        body, (dy, u, dres, h_re, h_im, h_re, h_im, bbd_re, bbd_im, cbd_re, cbd_imn, a_re, a_im, dskip), ride,
        lambda: pl.program_id(0) == 0, lambda: pl.program_id(0) == n_chunks - 1,
        name="s5_bwd", grid=(n_chunks,),
        in_specs=[pl.BlockSpec((t_rows, D_MODEL), rev), pl.BlockSpec((t_rows, D_MODEL), rev),
                  pl.BlockSpec((t_rows, D_MODEL), rev),
                  pl.BlockSpec((t_rows, N_STATES), rev), pl.BlockSpec((t_rows, N_STATES), rev),
                  pl.BlockSpec((SUBLANES, N_STATES), prev_tile), pl.BlockSpec((SUBLANES, N_STATES), prev_tile),
                  full3(bbd_re), full3(bbd_im), full3(cbd_re), full3(cbd_imn), full2(a_re), full2(a_im), full2(dskip)],
        out_specs=[pl.BlockSpec((t_rows, D_MODEL), rev), acc3(bbd_re.shape), acc3(bbd_im.shape), acc3(cbd_re.shape),
                   acc3(cbd_imn.shape), acc2((1, N_STATES)), acc2((1, N_STATES)), acc2((1, D_MODEL))],
        out_shape=[jax.ShapeDtypeStruct((seq, D_MODEL), F32), jax.ShapeDtypeStruct(bbd_re.shape, F32),
                   jax.ShapeDtypeStruct(bbd_im.shape, F32), jax.ShapeDtypeStruct(cbd_re.shape, F32),
                   jax.ShapeDtypeStruct(cbd_imn.shape, F32), jax.ShapeDtypeStruct((1, N_STATES), F32),
                   jax.ShapeDtypeStruct((1, N_STATES), F32), jax.ShapeDtypeStruct((1, D_MODEL), F32)],
        scratch_shapes=[pltpu.VMEM((t_rows, N_STATES), F32), pltpu.VMEM((t_rows, N_STATES), F32)] + [tile] * 12,
        compiler_params=_params(("arbitrary",)))


def _eye_groups():
    return jnp.eye(GROUPS_PER_TILE, dtype=F32)


def _blockdiag_in(bb):
    t = bb.transpose(0, 2, 1).reshape(N_UTILES, GROUPS_PER_TILE, SSM_GROUP, SSM_STATE)
    bd = jnp.einsum("jgcp,gh->jgchp", t, _eye_groups())
    return bd.reshape(N_UTILES, LANES, TILE_STATES)


def _blockdiag_in_t(d):
    t = jnp.einsum("jgchp,gh->jgcp", d.reshape(N_UTILES, GROUPS_PER_TILE, SSM_GROUP, GROUPS_PER_TILE, SSM_STATE),
                   _eye_groups())
    return t.reshape(N_GROUPS, SSM_GROUP, SSM_STATE).transpose(0, 2, 1)


def _blockdiag_out(c):
    t = c.transpose(0, 2, 1).reshape(N_UTILES, GROUPS_PER_TILE, SSM_STATE, SSM_GROUP)
    bd = jnp.einsum("jhpc,hg->jhpgc", t, _eye_groups())
    return bd.reshape(N_UTILES, TILE_STATES, LANES)


def _blockdiag_out_t(d):
    t = jnp.einsum("jhpgc,hg->jhpc", d.reshape(N_UTILES, GROUPS_PER_TILE, SSM_STATE, GROUPS_PER_TILE, SSM_GROUP),
                   _eye_groups())
    return t.reshape(N_GROUPS, SSM_STATE, SSM_GROUP).transpose(0, 2, 1)


ATT_TQ = 512
ATT_TK = 512
LOG2E = math.log2(math.e)
LN2 = math.log(2.0)
Q_PRESCALE = SM_SCALE * LOG2E


def _loop_in_pairs(n, step, carry, start=0):
    pairs = (n - start) // 2

    def two(t, c):
        return step(start + 2 * t + 1, step(start + 2 * t, c))

    carry = lax.fori_loop(0, pairs, two, carry)
    return lax.fori_loop(start + 2 * pairs, n, step, carry)


def _causal(s, transposed=False):
    r = lax.broadcasted_iota(jnp.int32, s.shape, 0)
    c = lax.broadcasted_iota(jnp.int32, s.shape, 1)
    return jnp.where((r <= c) if transposed else (c <= r), s, NEG_INF)


def _q_specs(rows, at):
    def nope(*ids):
        r, h = at(*ids)
        return r, 3 * (h // HEADS_PER_CHIP) + h % HEADS_PER_CHIP

    def rope(*ids):
        r, h = at(*ids)
        return r, 3 * (h // HEADS_PER_CHIP) + HEADS_PER_CHIP

    return [pl.BlockSpec((rows, LANES), nope), pl.BlockSpec((rows, LANES), rope)]


def _kv_specs(rows, at):
    def col(f):
        def index(*ids):
            r, h = at(*ids)
            return r, f(h)
        return index

    return [pl.BlockSpec((rows, LANES), col(lambda h: 2 * h)), pl.BlockSpec((rows, LANES), col(lambda h: h % HEADS_PER_CHIP)),
            pl.BlockSpec((rows, LANES), col(lambda h: 2 * h + 1))]


def _cat(a, b):
    return jnp.concatenate([a, b], axis=1)


def attn_fwd(q, kv, kr, ride=None, tq=ATT_TQ, tk=ATT_TK):
    seq = q.shape[0]
    n_heads = N_HEADS
    tq, tk = min(tq, seq), min(tk, seq)
    assert tq == tk

    def body(qn_ref, qr_ref, kn_ref, kr_ref, v_ref, o_ref, lse_ref):
        qi = pl.program_id(1)
        qv = _cat(qn_ref[...], qr_ref[...])
        jd = qi

        def block(j, carry, diag):
            m, l, acc = carry
            rows = pl.ds(pl.multiple_of(j * tk, tk), tk)
            s = lax.dot_general(qv, _cat(kn_ref[rows, :], kr_ref[rows, :]), _NT, preferred_element_type=F32)
            if diag:
                s = _causal(s)
            m_new = jnp.maximum(m, jnp.max(s, axis=-1, keepdims=True))
            p = jnp.exp2(s - m_new)
            corr = jnp.exp2(m - m_new)
            l = l * corr + jnp.sum(p, axis=-1, keepdims=True)
            acc = acc * corr + jnp.dot(p.astype(BF16), v_ref[rows, :], preferred_element_type=F32)
            return m_new, l, acc

        init = (jnp.full((tq, 1), NEG_INF, F32), jnp.zeros((tq, 1), F32), jnp.zeros((tq, V_HEAD), F32))
        carry = _loop_in_pairs(jd, lambda j, c: block(j, c, False), init)
        m, l, acc = block(jd, carry, True)
        o_ref[...] = acc / l
        lse_ref[...] = jnp.transpose(jnp.broadcast_to(m + jnp.log2(l), (tq, LANES)))[:1, :]

    n_q = seq // tq
    return _pcall_riding(
        body, (q, q, kv, kr, kv), ride,
        lambda: (pl.program_id(0) == 0) & (pl.program_id(1) == 0),
        lambda: (pl.program_id(0) == n_heads - 1) & (pl.program_id(1) == n_q - 1),
        middle=(lambda: (pl.program_id(0) == (5 * n_heads) // 8) & (pl.program_id(1) == 0)) if ride is not None else None,
        name="attn_fwd", grid=(n_heads, n_q),
        in_specs=_q_specs(tq, lambda h, i: (i, h)) + _kv_specs(seq, lambda h, i: (0, h)),
        out_specs=[pl.BlockSpec((tq, V_HEAD), lambda h, i: (i, h)),
                   pl.BlockSpec((None, None, 1, tq), lambda h, i: (h, i, 0, 0))],
        out_shape=[jax.ShapeDtypeStruct((seq, n_heads * V_HEAD), F32),
                   jax.ShapeDtypeStruct((n_heads, n_q, 1, tq), F32)],
        compiler_params=_params(("arbitrary", "arbitrary")))


def attn_bwd(q, kv, kr, do, lse_row, delta_row, tq=ATT_TK):
    seq = q.shape[0]
    tq = min(tq, seq)
    n_blk = seq // tq

    def body(qn_ref, qr_ref, kn_ref, kr_ref, v_ref, do_ref, lse_ref, delta_ref, dqn_ref, dqr_ref, dkv_ref, dkr_ref, dq_acc):
        head, kj = pl.program_id(0), pl.program_id(1)

        @pl.when(kj == 0)
        def _():
            dq_acc[...] = jnp.zeros_like(dq_acc)

        kc = _cat(kn_ref[...], kr_ref[...])
        vv = v_ref[...]

        def block(i, carry, diag):
            dk, dv = carry
            rows = pl.ds(pl.multiple_of(i * tq, tq), tq)
            qv = _cat(qn_ref[rows, :], qr_ref[rows, :])
            st = lax.dot_general(kc, qv, _NT, preferred_element_type=F32)
            if diag:
                st = _causal(st, transposed=True)
            pt = jnp.exp2(st - lse_ref[0, pl.ds(i, 1), :])
            dob = do_ref[rows, :].astype(BF16)
            dv = dv + jnp.dot(pt.astype(BF16), dob, preferred_element_type=F32)
            dpt = lax.dot_general(vv, dob, _NT, preferred_element_type=F32)
            dst = (pt * (dpt - delta_ref[0, pl.ds(i, 1), :])).astype(BF16)
            dk = dk + jnp.dot(dst, qv, preferred_element_type=F32)
            dq_acc[rows, :] += lax.dot_general(dst, kc, _TN, preferred_element_type=F32)
            return dk, dv

        carry = block(kj, (jnp.zeros((tq, 2 * LANES), F32), jnp.zeros((tq, V_HEAD), F32)), True)
        dk, dv = _loop_in_pairs(n_blk, lambda i, c: block(i, c, False), carry, start=kj + 1)
        dk = dk * LN2
        dkv_ref[...] = _cat(dk[:, :LANES], dv).astype(dkv_ref.dtype)
        lane = lax.broadcasted_iota(jnp.int32, (tq, LANES), 1)
        mine = (lane // HALF_ROPE) % HEADS_PER_CHIP == head % HEADS_PER_CHIP
        dkr_ref[0] = jnp.where(mine, dk[:, LANES:], 0.0)

        @pl.when(kj == n_blk - 1)
        def _():
            dqn_ref[...] = dq_acc[:, :LANES] * SM_SCALE

        @pl.when((kj == n_blk - 1) & (head % HEADS_PER_CHIP == 0))
        def _():
            dqr_ref[...] = dq_acc[:, LANES:] * SM_SCALE

        @pl.when((kj == n_blk - 1) & (head % HEADS_PER_CHIP > 0))
        def _():
            dqr_ref[...] += dq_acc[:, LANES:] * SM_SCALE

    return _pcall(
        body, name="attn_bwd", grid=(N_HEADS, n_blk),
        in_specs=_q_specs(seq, lambda h, j: (0, h)) + _kv_specs(tq, lambda h, j: (j, h))
        + [pl.BlockSpec((seq, V_HEAD), lambda h, j: (0, h)),
           pl.BlockSpec((1, n_blk, tq), lambda h, j: (h, 0, 0)),
           pl.BlockSpec((1, n_blk, tq), lambda h, j: (h, 0, 0))],
        out_specs=[pl.BlockSpec((seq, LANES), lambda h, j: (0, h)),
                   pl.BlockSpec((seq, LANES), lambda h, j: (0, h // HEADS_PER_CHIP)),
                   pl.BlockSpec((tq, QK_NOPE + V_HEAD), lambda h, j: (j, h)),
                   pl.BlockSpec((1, tq, LANES), lambda h, j: (h, j, 0))],
        out_shape=[jax.ShapeDtypeStruct((seq, N_HEADS * QK_NOPE), F32),
                   jax.ShapeDtypeStruct((seq, N_CHIPS * LANES), F32),
                   jax.ShapeDtypeStruct((seq, N_HEADS * (QK_NOPE + V_HEAD)), BF16),
                   jax.ShapeDtypeStruct((N_HEADS, seq, LANES), F32)],
        scratch_shapes=[pltpu.VMEM((seq, 2 * LANES), F32)],
        compiler_params=_params(("arbitrary", "arbitrary")),
    )(q, q, kv, kr, kv, do, lse_row, delta_row)


def head_sum(x, ts=512):
    n_heads, seq, w = x.shape
    ts = min(ts, seq)

    def body(x_ref, o_ref):
        o_ref[...] = jnp.sum(x_ref[...], axis=0)

    return _pcall(body, name="head_sum", grid=(seq // ts,),
                  in_specs=[pl.BlockSpec((n_heads, ts, w), lambda i: (0, i, 0))],
                  out_specs=pl.BlockSpec((ts, w), lambda i: (i, 0)),
                  out_shape=jax.ShapeDtypeStruct((seq, w), F32),
                  compiler_params=_params(("parallel",)))(x)


HEADS_PER_CHIP = N_HEADS // N_CHIPS
Q_CHIP = HEADS_PER_CHIP * QK_DIM
Q_CHIP_NOPE = HEADS_PER_CHIP * QK_NOPE


def _perm_q_cols(w):
    t = w.reshape(w.shape[0], HEADS_PER_CHIP, QK_DIM)
    return jnp.concatenate([t[:, :, :QK_NOPE].reshape(w.shape[0], -1),
                            t[:, :, QK_NOPE:QK_NOPE + HALF_ROPE].reshape(w.shape[0], -1),
                            t[:, :, QK_NOPE + HALF_ROPE:].reshape(w.shape[0], -1)], axis=1)


def _unperm_q_cols(w):
    r = w.shape[0]
    nope = w[:, :Q_CHIP_NOPE].reshape(r, HEADS_PER_CHIP, QK_NOPE)
    r1 = w[:, Q_CHIP_NOPE:Q_CHIP_NOPE + QK_ROPE].reshape(r, HEADS_PER_CHIP, HALF_ROPE)
    r2 = w[:, Q_CHIP_NOPE + QK_ROPE:].reshape(r, HEADS_PER_CHIP, HALF_ROPE)
    return jnp.concatenate([nope, r1, r2], axis=2).reshape(r, Q_CHIP)


def _pad_kva_cols(w):
    z = jnp.zeros((w.shape[0], HALF_ROPE), w.dtype)
    return jnp.concatenate([w[:, :KV_LORA], w[:, KV_LORA:KV_LORA + HALF_ROPE], z, w[:, KV_LORA + HALF_ROPE:], z], axis=1)


def _unpad_kva_cols(w):
    return jnp.concatenate([w[:, :KV_LORA], w[:, KV_LORA:KV_LORA + HALF_ROPE],
                            w[:, KV_LORA + QK_ROPE:KV_LORA + QK_ROPE + HALF_ROPE]], axis=1)


def _rope_tile(t, cs, sn):
    return t * cs + pltpu.roll(t, LANES // 2, 1) * sn


def _rope_tile_bwd(d, cs, sn):
    return d * cs + pltpu.roll(d * sn, LANES // 2, 1)


def _b_cols(tk, tn):
    return pl.BlockSpec((None, tk, tn), lambda i, j, k: (j, k, 0))


def _b_cols_t(tk, tn):
    return pl.BlockSpec((None, tn, tk), lambda i, j, k: (k, j, 0))


def _out_cols(shape):
    return shape, lambda tm, tn: pl.BlockSpec((None, tm, tn), lambda i, j, k: (j, i, 0))


def glu_proj(y, w_glu, tm=1024):
    seq, k_dim = y.shape
    tn = w_glu.shape[2]
    tm = min(tm, seq)
    half = N_CHIPS // 2

    def body(y_ref, wv_ref, wg_ref, val_ref, gate_ref, z_ref):
        yv = y_ref[...]
        v = jnp.dot(yv, wv_ref[...], preferred_element_type=F32)
        gt = jnp.dot(yv, wg_ref[...], preferred_element_type=F32)
        val_ref[...] = v
        gate_ref[...] = gt
        z_ref[...] = (v * _sigmoid(gt)).astype(z_ref.dtype)

    tile = pl.BlockSpec((tm, tn), lambda i, j: (i, j))
    return _pcall(
        body, name="glu_proj", grid=(seq // tm, half),
        in_specs=[pl.BlockSpec((tm, k_dim), lambda i, j: (i, 0)),
                  pl.BlockSpec((None, k_dim, tn), lambda i, j: (j, 0, 0)),
                  pl.BlockSpec((None, k_dim, tn), lambda i, j: (j + half, 0, 0))],
        out_specs=[tile, tile, tile],
        out_shape=[jax.ShapeDtypeStruct((seq, half * tn), F32), jax.ShapeDtypeStruct((seq, half * tn), F32),
                   jax.ShapeDtypeStruct((seq, half * tn), BF16)],
        compiler_params=_params(("parallel", "parallel")),
    )(y, w_glu, w_glu)


def _halves(a):
    return a.reshape(N_CHIPS, 2, a.shape[1] // 2, a.shape[2])


def device_step(x, positions, target, w, comm=None):
    seq = x.shape[0]
    w = dict(w)

    def gathered(names, outs):
        for n, a in zip(names, outs):
            if isinstance(n, tuple):
                w[n[0]] = [a.reshape(v.shape) if l == n[1] else v for l, v in enumerate(w[n[0]])]
            else:
                w[n] = a.reshape(w[n].shape)

    def ride_for(names):
        if comm is None:
            return None
        return GatherRide([_halves(w[n[0]][n[1]] if isinstance(n, tuple) else w[n]) for n in names])

    first_ride = ("ssm_w_glu", "ssm_w_out", ("w_ff1", 0), ("w_ff2", 0))
    mla_ride = ("kv_w_a", "kv_w_b", "q_w_a", "q_w_b", "attn_w_o")
    second_ride = (("w_ff1", 1), ("w_ff2", 1))

    inv_freq = ROPE_THETA ** (-jnp.arange(HALF_ROPE, dtype=F32) / HALF_ROPE)
    ang = positions.astype(F32)[:, None] * inv_freq
    cos, sin = jnp.cos(ang), jnp.sin(ang)
    zero = jnp.zeros_like(cos)
    cos_q, sin_q = jnp.concatenate([cos] * 4, 1), jnp.concatenate([-sin, -sin, sin, sin], 1)
    cos_k, sin_k = jnp.concatenate([cos, zero, cos, zero], 1), jnp.concatenate([-sin, zero, sin, zero], 1)
    ff_tile = D_FF // N_CHIPS
    pack_shape = (N_CHIPS, EARLY_ROWS, PACK_W)

    lr = w["ssm_lam_re"].reshape(N_STATES, 1)
    li = w["ssm_lam_im"].reshape(N_STATES, 1)
    ldt = jnp.repeat(w["ssm_log_dt"].reshape(N_GROUPS), SSM_STATE).reshape(N_STATES, 1)
    b_re = w["ssm_b_re"].reshape(N_STATES, SSM_GROUP)
    b_im = w["ssm_b_im"].reshape(N_STATES, SSM_GROUP)
    a_re, a_im, bb_re, bb_im = s5_prep(lr, li, ldt, b_re, b_im)
    a_re, a_im = a_re.reshape(1, N_STATES), a_im.reshape(1, N_STATES)
    bbd_re = _blockdiag_in(bb_re.reshape(N_GROUPS, SSM_STATE, SSM_GROUP)).astype(BF16)
    bbd_im = _blockdiag_in(bb_im.reshape(N_GROUPS, SSM_STATE, SSM_GROUP)).astype(BF16)
    cbd_re = _blockdiag_out(w["ssm_c_re"].reshape(N_GROUPS, SSM_GROUP, SSM_STATE)).astype(BF16)
    cbd_imn = _blockdiag_out(-w["ssm_c_im"].reshape(N_GROUPS, SSM_GROUP, SSM_STATE)).astype(BF16)
    dskip = w["ssm_d"].reshape(1, D_MODEL)
    (ypre, yg, h_re, h_im), landed = s5_fwd(x, bbd_re, bbd_im, cbd_re, cbd_imn, a_re, a_im, dskip, ride_for(first_ride))
    gathered(first_ride, landed)
    w_glu = w["ssm_w_glu"]
    glu_tile = w_glu.shape[2]
    val, gate, z = glu_proj(yg, w_glu)
    w_out = w["ssm_w_out"].reshape(D_MODEL, D_MODEL)
    ln = lambda name, l: w[name][l].reshape(1, D_MODEL)

    def then_ln(h, names, layer):
        def epi(r, hv, gl, bl):
            y = _layer_norm(hv, r, gl, bl)
            return r, y, y
        return dict(epi=epi, extras=(h, ln(names[0], layer), ln(names[1], layer)), out_dtypes=(F32, F32, BF16))

    mix0, h1, h1b = mm(z, w_out, name="ssm_out", **then_ln(x, ("ln_mix_g", "ln_mix_b"), 0))

    def mlp_fwd(h, hb, layer, riding=None, with_ln=True):
        pre = mm(hb, w["w_ff1"][layer], n_dim=D_FF, tiles=(None, ff_tile, None), b_view=_b_cols, name=f"ff1_{layer}",
                 out_dtypes=(BF16,), ride=ride_for(riding) if riding else None)
        if riding and comm is not None:
            pre, landed = pre
            gathered(riding, landed)
        post = then_ln(h, ("ln_ffn_g", "ln_ffn_b"), layer) if with_ln else {}
        return pre, mm(pre, w["w_ff2"][layer].reshape(D_FF, D_MODEL), pro_a=_relu2, name=f"ff2_{layer}", **post)

    f1pre, (f1, h2, h2b) = mlp_fwd(h1, h1b, 0, mla_ride)

    kv_w_a = w["kv_w_a"].reshape(D_MODEL, KVA_PAD)
    kv_w_b = w["kv_w_b"]
    q_w_a = w["q_w_a"].reshape(D_MODEL, Q_LORA)
    q_w_b = w["q_w_b"]
    w_o = w["attn_w_o"].reshape(D_MODEL, D_MODEL)
    kvb_tile = kv_w_b.shape[2]
    kvn_g = w["kv_norm_g"].reshape(1, KV_LORA)
    qn_g = w["q_norm_g"].reshape(1, Q_LORA)
    def kv_post(kva, g, cs, sn):
        tile = _rope_tile(kva[:, KV_LORA:], cs, sn)
        return kva, _rms(kva[:, :KV_LORA], g), _cat(tile, pltpu.roll(tile, HALF_ROPE, 1))
    kva, ckv, krope = mm(h2b, kv_w_a, epi=kv_post, extras=(kvn_g, cos_k, sin_k),
                         out_dtypes=(F32, (KV_LORA, BF16), (2 * LANES, BF16)), name="kv_a")
    kvb = mm(ckv, kv_w_b, n_dim=N_CHIPS * kvb_tile, tiles=(None, kvb_tile, KV_LORA), b_view=_b_cols, name="kv_b",
             out_dtypes=(BF16,))
    cq_raw, cq = mm(h2b, q_w_a, epi=lambda r, gq: (r, _rms(r, gq)), extras=(qn_g,), out_dtypes=(F32, BF16), name="q_a")

    def rope_and_scale(r, cs, sn):
        return (_cat(r[:, :Q_CHIP_NOPE], _rope_tile(r[:, Q_CHIP_NOPE:], cs, sn)) * Q_PRESCALE,)
    qro = mm(cq, q_w_b, n_dim=N_CHIPS * Q_CHIP, tiles=(None, Q_CHIP, Q_LORA), b_view=_b_cols, epi=rope_and_scale,
             extras=(cos_q, sin_q), out_dtypes=(BF16,), name="q_b")
    (o, lse), landed = attn_fwd(qro, kvb, krope, ride_for(second_ride))
    gathered(second_ride, landed)
    mix1, h3, h3b = mm(o, w_o, name="attn_out", **then_ln(h2, ("ln_mix_g", "ln_mix_b"), 1))
    f2pre, f2 = mlp_fwd(h3, h3b, 1, with_ln=False)
    def last_ln_loss_and_back(h, mix, gl, bl, t):
        e = _layer_norm(h, mix, gl, bl) - t
        dr, dg, db = _layer_norm_bwd(h, mix, gl, e * (1.0 / D_MODEL))
        return (dr, dr), (jnp.broadcast_to(jnp.sum(e * e), (1, LANES)), dg, db)
    dr4, dr4b, loss_acc, dg_f1, db_f1 = rowwise(
        last_ln_loss_and_back, (h3, f2, ln("ln_ffn_g", 1), ln("ln_ffn_b", 1), target),
        ((D_MODEL, F32), (D_MODEL, BF16)), accs=(LANES, D_MODEL, D_MODEL), name="ln_ffn_1_loss")
    loss = loss_acc[0, 0] * (0.5 / D_MODEL)

    g = {}

    def into_rows(off, rows_per_chip, shape=pack_shape):
        def view(tm, tn):
            if tm == N_CHIPS * rows_per_chip:
                return pl.BlockSpec((N_CHIPS, rows_per_chip, tn), lambda i, j, k: (0, off // rows_per_chip, 0))
            nb = rows_per_chip // tm
            return pl.BlockSpec((None, tm, tn), lambda i, j, k: (i // nb, off // tm + i % nb, 0))
        return shape, view

    def into_cols(off):
        return pack_shape, lambda tm, tn: pl.BlockSpec((None, tm, tn), lambda i, j, k: (j, off // tm + i, 0))

    def mlp_bwd(pack, dr, drb, hb, pre, layer, swap=False):
        w2_rows = (EARLY_OFF["w_ff2"] + layer * ff_tile, ff_tile)
        w1_rows = (EARLY_OFF["w_ff1"] + layer * D_MODEL, D_MODEL)
        ready = [(w1_rows[0] + w1_rows[1], w2_rows[0] - w1_rows[0] - w1_rows[1]), (w2_rows[0] + w2_rows[1], EARLY_ROWS - w2_rows[0] - w2_rows[1])]
        dpre = mm(drb, w["w_ff2"][layer].reshape(D_FF, D_MODEL), tb=True, epi=lambda r, p: (r * 2.0 * jnp.maximum(p, 0.0),),
                  extras=(pre,), out_dtypes=(BF16,), tiles=(None, ff_tile, None), name=f"ff2_dx_{layer}",
                  ride=SwapRide(pack, ready) if swap else None)
        if swap:
            dpre, (theirs,) = dpre
        pack = mm(pre, drb, ta=True, pro_a=_relu2, name=f"ff2_dw_{layer}", tiles=(ff_tile, PACK_W, None), into=pack,
                  out_view=into_rows(w2_rows[0], ff_tile))
        pack = mm(hb, dpre, ta=True, name=f"ff1_dw_{layer}", tiles=(None, PACK_W, None), into=pack,
                  out_view=into_cols(w1_rows[0]))
        dh = mm(dpre, w["w_ff1"][layer], tb=True, epi=lambda r, d: (r + DN_ALPHA * d,), extras=(dr,), n_dim=D_MODEL,
                tiles=(None, D_MODEL, ff_tile), b_view=_b_cols_t, name=f"ff1_dx_{layer}",
                ride=SwapRide(pack, [w1_rows, w2_rows], into=theirs) if swap else None)
        return (pack, *dh) if swap else (pack, dh)

    pack, dh3 = mlp_bwd(None, dr4, dr4b, h3b, f2pre, 1)
    dr3, dr3b, dg_m1, db_m1 = ln_bwd(h2, mix1, ln("ln_mix_g", 1), dh3, "ln_mix_bwd_1")
    shard_rows = D_MODEL // N_CHIPS
    pack = mm(o, dr3b, ta=True, name="attn_out_dw", tiles=(D_MODEL, PACK_W, None), into=pack,
              out_view=into_rows(EARLY_OFF["attn_w_o"], shard_rows))
    def head_dots(do, o):
        return do, jnp.concatenate([jnp.sum(do[:, V_HEAD * h:V_HEAD * (h + 1)] * o[:, V_HEAD * h:V_HEAD * (h + 1)], axis=1,
                                            keepdims=True) for h in range(N_HEADS)], axis=1)
    do, delta = mm(dr3b, w_o, tb=True, epi=head_dots, extras=(o,), out_dtypes=(F32, (N_HEADS, F32)), name="attn_out_dx")
    tb = min(ATT_TK, seq)
    lse_row = lse.reshape(N_HEADS, seq // tb, tb)
    delta_row = delta.T.reshape(N_HEADS, seq // tb, tb)
    dqn, dqr, dkvb, dkr = attn_bwd(qro, kvb, krope, do, lse_row, delta_row)

    def q_rope_bwd(dn, dr, cs, sn):
        parts = []
        for k in range(N_CHIPS):
            parts.append(dn[:, Q_CHIP_NOPE * k:Q_CHIP_NOPE * (k + 1)])
            parts.append(_rope_tile_bwd(dr[:, LANES * k:LANES * (k + 1)], cs, sn))
        return (jnp.concatenate(parts, axis=1),), ()
    (dqlin,) = rowwise(q_rope_bwd, (dqn, dqr, cos_q, sin_q), ((N_CHIPS * Q_CHIP, BF16),), name="q_rope_bwd")
    g["q_w_b"] = mm(cq, dqlin, ta=True, name="q_b_dw", tiles=(Q_LORA, Q_CHIP, None), out_view=_out_cols(q_w_b.shape))
    dcq_raw, dqn_g = mm(dqlin, q_w_b, tb=True, n_dim=Q_LORA, tiles=(None, Q_LORA, Q_CHIP), b_view=_b_cols_t,
                        epi=lambda d, c, gq: _rms_bwd(c, gq, d), extras=(cq_raw, qn_g), out_dtypes=(BF16,),
                        accs=(Q_LORA,), name="q_b_dx")
    g["q_w_a"] = mm(h2b, dcq_raw, ta=True, name="q_a_dw")
    g["kv_w_b"] = mm(ckv, dkvb, ta=True, name="kv_b_dw", tiles=(KV_LORA, kvb_tile, None), out_view=_out_cols(kv_w_b.shape))
    dkr_sum = head_sum(dkr)

    def kv_post_bwd(dc, kva, gk, dk, cs, sn):
        dx, dgk = _rms_bwd(kva[:, :KV_LORA], gk, dc)
        dk = dk + pltpu.roll(dk, LANES - HALF_ROPE, 1)
        return jnp.concatenate([dx, _rope_tile_bwd(dk, cs, sn)], axis=1), dgk
    dkva, dkvn_g = mm(dkvb, kv_w_b, tb=True, n_dim=KV_LORA, tiles=(None, KV_LORA, kvb_tile), b_view=_b_cols_t,
                      epi=kv_post_bwd, extras=(kva, kvn_g, dkr_sum, cos_k, sin_k), out_dtypes=((KVA_PAD, BF16),),
                      accs=(KV_LORA,), name="kv_b_dx")
    g["kv_w_a"] = mm(h2b, dkva, ta=True, name="kv_a_dw")
    dh2 = mm(dcq_raw, q_w_a, tb=True, epi=lambda r, d: (r + DN_ALPHA * d,), extras=(dr3,), name="q_a_dx")

    def ln_ffn_bwd(r, d, h, f, gl):
        dr, dg, db = _layer_norm_bwd(h, f, gl, r + d)
        return dr, dr, dg, db
    dr2, dr2b, dg_f0, db_f0 = mm(dkva, kv_w_a, tb=True, epi=ln_ffn_bwd, extras=(dh2, h1, f1, ln("ln_ffn_g", 0)),
                                 out_dtypes=(F32, BF16), accs=(D_MODEL, D_MODEL), name="kv_a_dx")
    pack = put_rows(pack, packed_shards(g, MISC_EARLY, EARLY_ROWS - MISC_EARLY_OFF), MISC_EARLY_OFF)
    if comm is None:
        pack, dh1 = mlp_bwd(pack, dr2, dr2b, h1b, f1pre, 0)
    else:
        pack, dh1, (theirs,) = mlp_bwd(pack, dr2, dr2b, h1b, f1pre, 0, swap=True)
        early_sums = add_halves(pack, theirs, comm[1])
    dr1, dr1b, dg_m0, db_m0 = ln_bwd(x, mix0, ln("ln_mix_g", 0), dh1, "ln_mix_bwd_0")
    mid = mm(z, dr1b, ta=True, name="ssm_out_dw", tiles=(D_MODEL, PACK_W, None),
             out_view=into_rows(MID_OFF["ssm_w_out"], shard_rows, (N_CHIPS, MID_ROWS, PACK_W)))
    def glu_bwd(dz, vl, gt):
        sg = _sigmoid(gt)
        return (jnp.concatenate([dz * sg, dz * vl * sg * (1.0 - sg)], axis=1),)
    dvg = mm(dr1b, w_out, tb=True, epi=glu_bwd, extras=(val, gate), out_dtypes=((2 * D_MODEL, BF16),), name="ssm_out_dx")
    g["ssm_w_glu"] = mm(yg, dvg, ta=True, name="glu_proj_dw", tiles=(None, glu_tile, None), out_view=_out_cols(w_glu.shape))
    mid = put_rows(mid, packed_shards(g, MISC_MID, MID_ROWS - MISC_MID_OFF), MISC_MID_OFF)
    dypre = mm(dvg, w_glu, tb=True, epi=lambda r, y: (r * _gelu_grad(y),), extras=(ypre,), n_dim=D_MODEL,
               tiles=(None, D_MODEL, glu_tile), b_view=_b_cols_t, name="glu_proj_dx",
               ride=Together([SwapRide(mid), SendRide([(early_sums, (0, EARLY_HEAD), None)])]) if comm is not None else None)
    sends = None
    if comm is not None:
        dypre, (theirs, early_got) = dypre
        sends = SendRide([(early_sums, (EARLY_HEAD, EARLY_ROWS - EARLY_HEAD), early_got), add_halves(mid, theirs, comm[1])])
    (dx, dbbd_re, dbbd_im, dcbd_re, dcbd_imn, dar, dai, dd), got = s5_bwd(
        dypre, x, dr1, h_re, h_im, bbd_re, bbd_im, cbd_re, cbd_imn, a_re, a_im, dskip, sends)
    dbb_re = _blockdiag_in_t(dbbd_re).reshape(N_STATES, SSM_GROUP)
    dbb_im = _blockdiag_in_t(dbbd_im).reshape(N_STATES, SSM_GROUP)
    dlr, dli, dldt, db_re, db_im = s5_prep_bwd(lr, li, ldt, b_re, b_im, dar.reshape(N_STATES, 1),
                                               dai.reshape(N_STATES, 1), dbb_re, dbb_im)
    g["ssm_lam_re"] = dlr.reshape(1, N_GROUPS, SSM_STATE)
    g["ssm_lam_im"] = dli.reshape(1, N_GROUPS, SSM_STATE)
    g["ssm_log_dt"] = group_sum(dldt).reshape(1, N_GROUPS)
    g["ssm_b_re"] = db_re.reshape(1, N_GROUPS, SSM_STATE, SSM_GROUP)
    g["ssm_b_im"] = db_im.reshape(1, N_GROUPS, SSM_STATE, SSM_GROUP)
    g["ssm_c_re"] = _blockdiag_out_t(dcbd_re).reshape(1, N_GROUPS, SSM_GROUP, SSM_STATE)
    g["ssm_c_im"] = -_blockdiag_out_t(dcbd_imn).reshape(1, N_GROUPS, SSM_GROUP, SSM_STATE)
    g["ssm_d"] = dd
    g["ln_mix_g"] = jnp.concatenate([dg_m0, dg_m1], 0)
    g["ln_mix_b"] = jnp.concatenate([db_m0, db_m1], 0)
    g["ln_ffn_g"] = jnp.concatenate([dg_f0, dg_f1], 0)
    g["ln_ffn_b"] = jnp.concatenate([db_f0, db_f1], 0)
    g["kv_norm_g"] = dkvn_g.reshape(KV_LORA)
    g["q_norm_g"] = dqn_g
    return loss, dx, pack, mid, g, list(zip(sends.ins, got)) if comm is not None else None


def place(shard, me_idx, dtype, name, layer=None):
    rows, cols = shard.shape[-2:]
    tr = _tile(rows, (512, 256, 128))

    def body(m_ref, x_ref, o_ref):
        o_ref[...] = x_ref[...].astype(o_ref.dtype)

    in_spec = (pl.BlockSpec((tr, cols), lambda i, m: (i, 0)) if layer is None
               else pl.BlockSpec((None, tr, cols), lambda i, m: (layer, i, 0)))
    return _pcall(
        body, name=name,
        grid_spec=pltpu.PrefetchScalarGridSpec(
            num_scalar_prefetch=1, grid=(rows // tr,), in_specs=[in_spec],
            out_specs=pl.BlockSpec((None, tr, cols), lambda i, m: (m[0], i, 0))),
        out_shape=jax.ShapeDtypeStruct((N_CHIPS, rows, cols), dtype),
        compiler_params=_params(("parallel",)),
    )(me_idx, shard)


def place_many(shards, dtypes, me_idx, name):
    def body(m_ref, *refs):
        for x_ref, o_ref in zip(refs[:len(shards)], refs[len(shards):]):
            o_ref[...] = x_ref[...].astype(o_ref.dtype)

    return _pcall(
        body, name=name,
        grid_spec=pltpu.PrefetchScalarGridSpec(
            num_scalar_prefetch=1, grid=(1,),
            in_specs=[pl.BlockSpec(s.shape, lambda i, m: (0, 0)) for s in shards],
            out_specs=[pl.BlockSpec((None,) + s.shape, lambda i, m: (m[0], 0, 0)) for s in shards]),
        out_shape=[jax.ShapeDtypeStruct((N_CHIPS,) + s.shape, d) for s, d in zip(shards, dtypes)],
        compiler_params=_params(("arbitrary",)),
    )(me_idx, *shards)


def put_rows(pack, rows, off):
    _, n, cols = rows.shape

    def body(r_ref, p_ref, o_ref, sem):
        cp = pltpu.make_async_copy(r_ref.at[0], o_ref.at[pl.program_id(0), pl.ds(off, n), :], sem)
        cp.start()
        cp.wait()

    return _pcall(body, name="grad_put_rows", grid=(N_CHIPS,),
                  in_specs=[pl.BlockSpec((1, n, cols), lambda k: (k, 0, 0)), _ANY], out_specs=_ANY,
                  out_shape=jax.ShapeDtypeStruct(pack.shape, pack.dtype), input_output_aliases={1: 0},
                  scratch_shapes=[pltpu.SemaphoreType.DMA],
                  compiler_params=_params(("arbitrary",)))(rows, pack)


def _my_cols(c, mine=True):
    start = (c if mine else 1 - c) * HALF_W
    return pl.ds(pl.multiple_of(start, HALF_W), HALF_W)


def add_halves(gpack, got, c_idx):
    n, rows, _ = gpack.shape
    tr = min(G_BLOCK_ROWS, rows)
    blk = (None, tr, HALF_W)

    def body(c_ref, g_ref, r_ref, o_ref):
        o_ref[...] = (g_ref[...] + r_ref[...]).astype(o_ref.dtype)

    return _pcall(
        body, name="grad_add_halves",
        grid_spec=pltpu.PrefetchScalarGridSpec(
            num_scalar_prefetch=1, grid=(n, rows // tr),
            in_specs=[pl.BlockSpec(blk, lambda k, i, c: (k, i, c[0])), pl.BlockSpec(blk, lambda k, i, c: (k, i, 0))],
            out_specs=pl.BlockSpec(blk, lambda k, i, c: (k, i, 0))),
        out_shape=jax.ShapeDtypeStruct((n, rows, HALF_W), BF16),
        compiler_params=_params(("parallel", "parallel")),
    )(c_idx, gpack, got)


def sum_owner(part, got, idx, total_rows, row_off=0, into=None):
    _, rows, _ = part.shape
    tr = math.gcd(math.gcd(rows, row_off), G_BLOCK_ROWS)
    n_into = 0 if into is None else 1

    def body(m_ref, p_ref, g_ref, *rest):
        up = lambda v: v.astype(F32)
        rest[-1][...] = ((up(p_ref[...]) + up(g_ref[0])) + up(g_ref[1])) + up(g_ref[2])

    return _pcall(
        body, name="grad_sum_owner",
        grid_spec=pltpu.PrefetchScalarGridSpec(
            num_scalar_prefetch=1, grid=(rows // tr,),
            in_specs=[pl.BlockSpec((None, tr, HALF_W), lambda i, m: (m[0], i, 0)),
                      pl.BlockSpec((3, tr, HALF_W), lambda i, m: (0, i, 0))] + [_ANY] * n_into,
            out_specs=pl.BlockSpec((tr, HALF_W), lambda i, m: (row_off // tr + i, m[1]))),
        out_shape=jax.ShapeDtypeStruct((total_rows, PACK_W), F32),
        input_output_aliases={3: 0} if n_into else {},
        compiler_params=_params(("parallel",)),
    )(idx, part, got, *([into] if n_into else []))


def join_halves(red):
    def body(in_ref, out_ref, send_sem, recv_sem):
        x, y, c, _ = _place()
        sibling = (x, y, 1 - c)
        mine = out_ref.at[:, _my_cols(c)]
        cp = pltpu.make_async_remote_copy(src_ref=mine, dst_ref=mine, send_sem=send_sem, recv_sem=recv_sem,
                                          device_id=sibling, device_id_type=MESH)
        cp.start()
        cp.wait_send()
        other = out_ref.at[:, _my_cols(c, mine=False)]
        pltpu.make_async_remote_copy(src_ref=other, dst_ref=other, send_sem=send_sem, recv_sem=recv_sem,
                                     device_id=sibling, device_id_type=MESH).wait_recv()

    return _pcall(body, name="grad_join_halves", in_specs=[_ANY], out_specs=_ANY,
                  out_shape=jax.ShapeDtypeStruct(red.shape, red.dtype), input_output_aliases={0: 0},
                  scratch_shapes=[pltpu.SemaphoreType.DMA, pltpu.SemaphoreType.DMA])(red)


def adamw(gsrc, g_off, wt, m, v, name):
    n, cols = wt.shape
    tr = math.gcd(math.gcd(g_off, n), 256) if g_off else math.gcd(n, 256)
    off_blk = g_off // tr
    c1 = 1.0 / (1.0 - ADAM_B1 ** ADAM_STEP)
    c2 = 1.0 / (1.0 - ADAM_B2 ** ADAM_STEP)

    def body(g_ref, w_ref, m_ref, v_ref, go_ref, d_ref, mo_ref, vo_ref):
        gv = g_ref[...]
        mn = ADAM_B1 * m_ref[...] + (1.0 - ADAM_B1) * gv
        vn = ADAM_B2 * v_ref[...] + (1.0 - ADAM_B2) * gv * gv
        go_ref[...] = gv
        mo_ref[...] = mn
        vo_ref[...] = vn
        d_ref[...] = -ADAM_LR * ((mn * c1) / (jnp.sqrt(vn * c2) + ADAM_EPS) + ADAM_WD * w_ref[...])

    blk = pl.BlockSpec((tr, cols), lambda i: (i, 0))
    return _pcall(body, name=name, grid=(n // tr,),
                  in_specs=[pl.BlockSpec((tr, cols), lambda i: (off_blk + i, 0)), blk, blk, blk],
                  out_specs=[blk] * 4, out_shape=[jax.ShapeDtypeStruct((n, cols), F32)] * 4,
                  compiler_params=_params(("parallel",)))(gsrc, wt, m, v)


def _rows8(a):
    return -(-a.size // (8 * PACK_W)) * 8


def _as_rows(a, rows=None):
    flat = a.reshape(-1)
    n = _rows8(a) if rows is None else rows
    return jnp.pad(flat, (0, n * PACK_W - flat.shape[0])).reshape(n, PACK_W)


def local_shards_2d(wl):
    return {"w_ff1": [wl["w_ff1"][0], wl["w_ff1"][1]], "w_ff2": [wl["w_ff2"][0], wl["w_ff2"][1]],
            "ssm_w_glu": wl["ssm_w_glu"], "ssm_w_out": wl["ssm_w_out"], "kv_w_a": _pad_kva_cols(wl["kv_w_a"]),
            "kv_w_b": wl["kv_w_b"], "q_w_a": wl["q_w_a"], "q_w_b": _perm_q_cols(wl["q_w_b"]),
            "attn_w_o": wl["attn_w_o"], "ssm_d": wl["ssm_d"].reshape(2, -1)}


def misc_grad_shard(name, g, k):
    if name == "ssm_d":
        w = D_MODEL // N_CHIPS
        return g[:, w * k:w * (k + 1)]
    if name in ("ssm_w_glu", "kv_w_b"):
        return g[k]
    if name == "q_w_b":
        return _unperm_q_cols(g[k])
    rows = D_MODEL // N_CHIPS
    shard = g[rows * k:rows * (k + 1)]
    return _unpad_kva_cols(shard) if name == "kv_w_a" else shard


def packed_shards(g, names, rows, tail=None):
    blocks = []
    for k in range(N_CHIPS):
        parts = [_as_rows(misc_grad_shard(n, g[n], k), MISC_SHARD_ROWS[n]) for n in names]
        if tail is not None:
            parts.append(tail[k * (tail.shape[0] // N_CHIPS):(k + 1) * (tail.shape[0] // N_CHIPS)])
        blk = jnp.concatenate(parts, axis=0)
        blocks.append(jnp.pad(blk, ((0, rows - blk.shape[0]), (0, 0))))
    return jnp.stack(blocks)


def kernel(x, positions, ln_mix_g, ln_mix_b, ln_ffn_g, ln_ffn_b, w_ff1, w_ff2, ssm_lam_re, ssm_lam_im, ssm_log_dt, ssm_b_re, ssm_b_im, ssm_c_re, ssm_c_im, ssm_d, ssm_w_glu, ssm_w_out, kv_w_a, kv_norm_g, kv_w_b, q_w_a, q_norm_g, q_w_b, attn_w_o, loss_target, m_ln_mix_g, m_ln_mix_b, m_ln_ffn_g, m_ln_ffn_b, m_w_ff1, m_w_ff2, m_ssm_lam_re, m_ssm_lam_im, m_ssm_log_dt, m_ssm_b_re, m_ssm_b_im, m_ssm_c_re, m_ssm_c_im, m_ssm_d, m_ssm_w_glu, m_ssm_w_out, m_kv_w_a, m_kv_norm_g, m_kv_w_b, m_q_w_a, m_q_norm_g, m_q_w_b, m_attn_w_o, v_ln_mix_g, v_ln_mix_b, v_ln_ffn_g, v_ln_ffn_b, v_w_ff1, v_w_ff2, v_ssm_lam_re, v_ssm_lam_im, v_ssm_log_dt, v_ssm_b_re, v_ssm_b_im, v_ssm_c_re, v_ssm_c_im, v_ssm_d, v_ssm_w_glu, v_ssm_w_out, v_kv_w_a, v_kv_norm_g, v_kv_w_b, v_q_w_a, v_q_norm_g, v_q_w_b, v_attn_w_o):
    env = dict(locals())
    wl = {n: env[n] for n in WEIGHTS}
    ml = {n: env["m_" + n] for n in WEIGHTS}
    vl = {n: env["v_" + n] for n in WEIGHTS}
    for n in ("ssm_w_glu", "ssm_w_out", "q_w_a", "q_w_b", "attn_w_o"):
        wl[n], ml[n], vl[n] = wl[n][0], ml[n][0], vl[n][0]

    c_idx = lax.axis_index("c").astype(jnp.int32).reshape(1)
    me_idx = (2 * lax.axis_index("x") + lax.axis_index("y")).astype(jnp.int32).reshape(1)

    local = local_shards_2d(wl)
    stacked = {n: [place(wl[n], me_idx, BF16, f"place_{n}_{l}", layer=l) for l in range(DEPTH)] for n in ("w_ff1", "w_ff2")}
    others = [n for n in SHARDED if n not in stacked]
    stacked.update(zip(others, place_many([local[n] for n in others], [F32 if n == "ssm_d" else BF16 for n in others],
                                          me_idx, "place_others")))
    stacked["ssm_d"] = ride_alone(GatherRide([_halves(stacked["ssm_d"])]), "ssm_d_all_gather")[0].reshape(1, D_MODEL)
    for n in REPLICATED:
        stacked[n] = wl[n]

    loss_part, dx, early, mid, g, sent = device_step(x[0], positions[0], loss_target[0], stacked, comm=(me_idx, c_idx))
    loss = lax.psum(loss_part, ("x", "y", "c"))

    small = jnp.concatenate([_as_rows(g[n]) for n in REPLICATED], axis=0)
    small = jnp.pad(small, ((0, SMALL_ROWS - small.shape[0]), (0, 0)))
    late = packed_shards(g, MISC_LATE, LATE_ROWS, tail=small)
    late_sums = add_halves(late, ride_alone(SwapRide(late), "grad_swap_halves")[0], c_idx)
    sent.append((late_sums, ride_alone(SendRide([late_sums]), "grad_send_to_owners")[0]))
    where = jnp.concatenate([me_idx, c_idx])
    starts = (0, EARLY_ROWS, EARLY_ROWS + MID_ROWS)
    total_rows = EARLY_ROWS + MID_ROWS + LATE_ROWS
    reduced = None
    for (sums, got), off in zip(sent, starts):
        reduced = sum_owner(sums, got, where, total_rows, row_off=off, into=reduced)
    reduced = join_halves(reduced)
    quarter = reduced[starts[2] + SMALL_OFF:starts[2] + SMALL_OFF + SMALL_Q_ROWS]
    small_tot = ride_alone(GatherRide([_halves(place(quarter, me_idx, F32, "place_small_grads"))]),
                           "small_grad_all_gather")[0].reshape(SMALL_ROWS, PACK_W)

    out_g, out_d, out_m, out_v = {}, {}, {}, {}
    direct = {**EARLY_OFF, **{n: starts[1] + o for n, o in MID_OFF.items()}}
    for n, off in direct.items():
        res = adamw(reduced, off, wl[n].reshape(-1, PACK_W), ml[n].reshape(-1, PACK_W), vl[n].reshape(-1, PACK_W),
                    "adamw_" + n)
        out_g[n], out_d[n], out_m[n], out_v[n] = [a.reshape(env[n].shape) for a in res]
    for names, off in ((MISC_EARLY, MISC_EARLY_OFF), (MISC_MID, starts[1] + MISC_MID_OFF), (MISC_LATE, starts[2])):
        pack3 = lambda d: jnp.concatenate([_as_rows(d[n], MISC_SHARD_ROWS[n]) for n in names], axis=0)
        res = adamw(reduced, off, pack3(wl), pack3(ml), pack3(vl), "adamw_packed_" + names[0])
        r0 = 0
        for n in names:
            cnt = math.prod(env[n].shape)
            out_g[n], out_d[n], out_m[n], out_v[n] = [
                a[r0:r0 + MISC_SHARD_ROWS[n]].reshape(-1)[:cnt].reshape(env[n].shape) for a in res]
            r0 += MISC_SHARD_ROWS[n]
    ws = jnp.concatenate([_as_rows(wl[n]) for n in REPLICATED], axis=0)
    ms = jnp.concatenate([_as_rows(ml[n]) for n in REPLICATED], axis=0)
    vs = jnp.concatenate([_as_rows(vl[n]) for n in REPLICATED], axis=0)
    pad = ((0, SMALL_ROWS - ws.shape[0]), (0, 0))
    res = adamw(small_tot, 0, jnp.pad(ws, pad), jnp.pad(ms, pad), jnp.pad(vs, pad), "adamw_replicated")
    row = 0
    for n in REPLICATED:
        cnt = math.prod(env[n].shape)
        nrows = _rows8(env[n])
        out_g[n], out_d[n], out_m[n], out_v[n] = [a[row:row + nrows].reshape(-1)[:cnt].reshape(env[n].shape) for a in res]
        row += nrows

    return (loss, dx[None], *[out_g[n] for n in WEIGHTS], *[out_d[n] for n in WEIGHTS],
            *[out_m[n] for n in WEIGHTS], *[out_v[n] for n in WEIGHTS])
```

```python
import functools
import math

import jax
import jax.numpy as jnp
from jax import lax
from jax.experimental import pallas as pl
from jax.experimental.pallas import tpu as pltpu

F32 = jnp.float32
BF16 = jnp.bfloat16
MESH = pl.DeviceIdType.MESH

D_MODEL = 1024
DEPTH = 2
SSM_GROUP = 16
N_GROUPS = D_MODEL // SSM_GROUP
SSM_STATE = 64
N_STATES = N_GROUPS * SSM_STATE
N_HEADS = 8
QK_NOPE = 128
QK_ROPE = 64
HALF_ROPE = QK_ROPE // 2
V_HEAD = 128
QK_DIM = QK_NOPE + QK_ROPE
Q_LORA = 384
KV_LORA = 256
ROPE_THETA = 10000.0
SM_SCALE = QK_DIM ** -0.5
NEG_INF = -1e30
D_FF = 4 * D_MODEL
DN_ALPHA = (2 * DEPTH) ** 0.25
LN_EPS = 1e-5
RMS_EPS = 1e-6
ADAM_LR = 0.001
ADAM_B1 = 0.9
ADAM_B2 = 0.999
ADAM_EPS = 1e-08
ADAM_WD = 0.01
ADAM_STEP = 10

N_CHIPS = 4
LANES = 128
VMEM_LIMIT = 56 * 1024 * 1024
MM_VMEM_BUDGET = 40 * 1024 * 1024
PACK_W = 1024
KVA_PAD = 384
HALF_W = PACK_W // 2

SHARDED = ("w_ff1", "w_ff2", "ssm_w_glu", "ssm_w_out", "kv_w_a", "kv_w_b", "q_w_a", "q_w_b", "attn_w_o", "ssm_d")
G_BLOCK_ROWS = 960
EARLY_OFF = {"w_ff1": 0, "w_ff2": 2048, "attn_w_o": 4096}
MISC_EARLY = ("kv_w_b", "kv_w_a", "q_w_a", "q_w_b")
MISC_EARLY_OFF = 4352
EARLY_ROWS = 5 * G_BLOCK_ROWS
EARLY_HEAD = G_BLOCK_ROWS
MID_OFF = {"ssm_w_out": 0}
MISC_MID = ("ssm_w_glu",)
MISC_MID_OFF = 256
MID_ROWS = MISC_MID_OFF + 512
MISC_LATE = ("ssm_d",)
SMALL_Q_ROWS = 96
SMALL_ROWS = N_CHIPS * SMALL_Q_ROWS
SMALL_OFF = 16
LATE_ROWS = 192
MISC_SHARD_ROWS = {"ssm_d": 16, "ssm_w_glu": 512, "kv_w_b": 128, "kv_w_a": 80, "q_w_a": 96, "q_w_b": 144}
REPLICATED = ("ln_mix_g", "ln_mix_b", "ln_ffn_g", "ln_ffn_b", "ssm_lam_re", "ssm_lam_im", "ssm_log_dt",
              "ssm_b_re", "ssm_b_im", "ssm_c_re", "ssm_c_im", "kv_norm_g", "q_norm_g")
WEIGHTS = ("ln_mix_g", "ln_mix_b", "ln_ffn_g", "ln_ffn_b", "w_ff1", "w_ff2", "ssm_lam_re", "ssm_lam_im",
           "ssm_log_dt", "ssm_b_re", "ssm_b_im", "ssm_c_re", "ssm_c_im", "ssm_d", "ssm_w_glu", "ssm_w_out",
           "kv_w_a", "kv_norm_g", "kv_w_b", "q_w_a", "q_norm_g", "q_w_b", "attn_w_o")


def _pcall(body, **kw):
    return pl.pallas_call(body, **kw)


def _params(sem=None):
    return pltpu.CompilerParams(dimension_semantics=sem, vmem_limit_bytes=VMEM_LIMIT)


_ANY = pl.BlockSpec(memory_space=pl.ANY)


def _tile(dim, prefs):
    for p in prefs:
        if dim % p == 0:
            return p
    return dim


def _place():
    x, y, c = lax.axis_index("x"), lax.axis_index("y"), lax.axis_index("c")
    return x, y, c, [(1 - x, y), (x, 1 - y), (1 - x, 1 - y)]


def _remote(k, src, dst, to, send_sems, recv_sems):
    return pltpu.make_async_remote_copy(src_ref=src, dst_ref=dst, send_sem=send_sems.at[k], recv_sem=recv_sems.at[k],
                                        device_id=to, device_id_type=MESH)


class GatherRide:
    def __init__(self, arrs):
        self.ins = list(arrs)
        self.out_shapes = [jax.ShapeDtypeStruct(a.shape, a.dtype) for a in arrs]
        self.aliases = {i: i for i in range(len(arrs))}
        self.n_sems = 6 * len(arrs)

    def start(self, ins, outs, send_sems, recv_sems):
        x, y, c, chips = _place()
        me = 2 * x + y
        for a, o in enumerate(outs):
            for j, (px, py) in enumerate(chips):
                _remote(6 * a + j, o.at[me, c], o.at[me, c], (px, py, c), send_sems, recv_sems).start()

    def pass_on(self, ins, outs, send_sems, recv_sems):
        x, y, c, chips = _place()
        for a, o in enumerate(outs):
            for j, (px, py) in enumerate(chips):
                blk = o.at[2 * px + py, c]
                _remote(6 * a + j, blk, blk, (px, py, c), send_sems, recv_sems).wait_recv()
                _remote(6 * a + 3 + j, blk, blk, (x, y, 1 - c), send_sems, recv_sems).start()

    def finish(self, ins, outs, send_sems, recv_sems, passed_on=False):
        if not passed_on:
            self.pass_on(ins, outs, send_sems, recv_sems)
        x, y, c, chips = _place()
        me = 2 * x + y
        sibling = (x, y, 1 - c)
        for a, o in enumerate(outs):
            for j, (px, py) in enumerate(chips):
                blk = o.at[2 * px + py, 1 - c]
                _remote(6 * a + 3 + j, blk, blk, sibling, send_sems, recv_sems).wait_recv()
                _remote(6 * a + j, o.at[me, c], o.at[me, c], (px, py, c), send_sems, recv_sems).wait_send()
                mine = o.at[2 * px + py, c]
                _remote(6 * a + 3 + j, mine, mine, sibling, send_sems, recv_sems).wait_send()


class SendRide:
    base = 0

    def __init__(self, parts):
        parts = [p if isinstance(p, tuple) else (p, (0, p.shape[1]), None) for p in parts]
        self.rows = [rows for _, rows, _ in parts]
        self.n_parts = len(parts)
        self.ins = [p for p, _, _ in parts] + [into for _, _, into in parts if into is not None]
        self.out_shapes = [jax.ShapeDtypeStruct((3,) + p.shape[1:], p.dtype) for p, _, _ in parts]
        given = [a for a, (_, _, into) in enumerate(parts) if into is not None]
        self.aliases = {self.n_parts + i: a for i, a in enumerate(given)}
        self.n_sems = 3 * self.n_parts

    def _copies(self, ins, outs, send_sems, recv_sems):
        x, y, c, chips = _place()
        return [_remote(self.base + 3 * a + j, ins[a].at[2 * px + py, pl.ds(r0, n)], outs[a].at[j, pl.ds(r0, n)],
                        (px, py, c), send_sems, recv_sems)
                for a, (r0, n) in enumerate(self.rows) for j, (px, py) in enumerate(chips)]

    def start(self, ins, outs, send_sems, recv_sems):
        for cp in self._copies(ins, outs, send_sems, recv_sems):
            cp.start()

    def finish(self, ins, outs, send_sems, recv_sems):
        for cp in self._copies(ins, outs, send_sems, recv_sems):
            cp.wait()


class SwapRide:
    base = 0

    def __init__(self, pack, ranges=None, into=None):
        self.ins = [pack] if into is None else [pack, into]
        self.out_shapes = [jax.ShapeDtypeStruct(pack.shape[:2] + (HALF_W,), pack.dtype)]
        self.aliases = {} if into is None else {1: 0}
        self.ranges = ranges or [(0, pack.shape[1])]
        self.n_sems = len(self.ranges)

    def _copies(self, ins, outs, send_sems, recv_sems):
        x, y, c, _ = _place()
        return [_remote(self.base + k, ins[0].at[:, pl.ds(r0, n), _my_cols(c, mine=False)], outs[0].at[:, pl.ds(r0, n), :],
                        (x, y, 1 - c), send_sems, recv_sems) for k, (r0, n) in enumerate(self.ranges)]

    def start(self, ins, outs, send_sems, recv_sems):
        for cp in self._copies(ins, outs, send_sems, recv_sems):
            cp.start()

    def finish(self, ins, outs, send_sems, recv_sems):
        for cp in self._copies(ins, outs, send_sems, recv_sems):
            cp.wait()


class Together:
    def __init__(self, rides):
        self.rides = rides
        self.ins, self.out_shapes, self.aliases, self.n_sems = [], [], {}, 0
        for r in rides:
            r.base = self.n_sems
            self.aliases.update({len(self.ins) + i: len(self.out_shapes) + o for i, o in r.aliases.items()})
            self.ins += r.ins
            self.out_shapes += r.out_shapes
            self.n_sems += r.n_sems

    def _each(self, step, ins, outs, send_sems, recv_sems):
        i = o = 0
        for r in self.rides:
            getattr(r, step)(ins[i:i + len(r.ins)], outs[o:o + len(r.out_shapes)], send_sems, recv_sems)
            i, o = i + len(r.ins), o + len(r.out_shapes)

    def start(self, *refs):
        self._each("start", *refs)

    def finish(self, *refs):
        self._each("finish", *refs)


def _pcall_riding(body, args, ride, first, last, *, in_specs, out_specs, out_shape, scratch_shapes=(), middle=None,
                  in_place=None, **kw):
    n_in, n_out = len(args), len(out_shape)
    in_place = in_place or {}
    if ride is None:
        return _pcall(body, in_specs=in_specs, out_specs=out_specs, out_shape=out_shape,
                      input_output_aliases=in_place, scratch_shapes=list(scratch_shapes), **kw)(*args), []
    k_in, k_out = len(ride.ins), len(ride.out_shapes)

    def riding(*refs):
        ins, r_in = refs[:n_in], refs[n_in:n_in + k_in]
        outs = refs[n_in + k_in:n_in + k_in + n_out]
        r_out = refs[n_in + k_in + n_out:n_in + k_in + n_out + k_out]
        scratch, (send_sems, recv_sems) = refs[n_in + k_in + n_out + k_out:-2], refs[-2:]

        @pl.when(first())
        def _():
            ride.start(r_in, r_out, send_sems, recv_sems)

        if middle is not None:
            @pl.when(middle())
            def _():
                ride.pass_on(r_in, r_out, send_sems, recv_sems)

        body(*ins, *outs, *scratch)

        @pl.when(last())
        def _():
            if middle is not None:
                ride.finish(r_in, r_out, send_sems, recv_sems, passed_on=True)
            else:
                ride.finish(r_in, r_out, send_sems, recv_sems)

    res = _pcall(riding, in_specs=list(in_specs) + [_ANY] * k_in, out_specs=list(out_specs) + [_ANY] * k_out,
                 out_shape=list(out_shape) + ride.out_shapes,
                 input_output_aliases={**in_place, **{n_in + i: n_out + o for i, o in ride.aliases.items()}},
                 scratch_shapes=list(scratch_shapes) + [pltpu.SemaphoreType.DMA((ride.n_sems,))] * 2,
                 **kw)(*args, *ride.ins)
    return res[:n_out], res[n_out:]


def ride_alone(ride, name):
    def body(*refs):
        n = len(ride.ins)
        ins, outs, (send_sems, recv_sems) = refs[:n], refs[n:-2], refs[-2:]
        ride.start(ins, outs, send_sems, recv_sems)
        ride.finish(ins, outs, send_sems, recv_sems)

    return _pcall(body, name=name, in_specs=[_ANY] * len(ride.ins), out_specs=[_ANY] * len(ride.out_shapes),
                  out_shape=ride.out_shapes, input_output_aliases=dict(ride.aliases),
                  scratch_shapes=[pltpu.SemaphoreType.DMA((ride.n_sems,))] * 2)(*ride.ins)


def mm(a, b, *, name, ta=False, tb=False, pro_a=None, epi=None, extras=(), out_dtypes=(F32,), n_dim=None,
       tiles=(None, None, None), b_view=None, out_view=None, into=None, ride=None, accs=()):
    widths = [d[0] if isinstance(d, tuple) else None for d in out_dtypes]
    out_dtypes = [d[1] if isinstance(d, tuple) else d for d in out_dtypes]
    if ta:
        k_dim, m_dim = a.shape
    else:
        m_dim, k_dim = a.shape
    if n_dim is None:
        n_dim = b.shape[0] if tb else b.shape[1]
    tn = tiles[1] or (n_dim if n_dim <= 1024 else _tile(n_dim, (1024, 512, 256, 128)))
    tk = tiles[2] or (k_dim if k_dim <= 1024 else _tile(k_dim, (1024, 512, 256, 128)))
    nk = k_dim // tk

    def vmem_bytes(tm):
        blocks = tm * tk * a.dtype.itemsize + tk * tn * b.dtype.itemsize
        blocks += sum(tm * (tn if e.shape[1] == n_dim else e.shape[1]) * e.dtype.itemsize for e in extras if e.shape[0] > 1)
        blocks += tm * sum((w or tn) * jnp.dtype(d).itemsize for w, d in zip(widths, out_dtypes))
        return 2 * blocks + tm * tn * 4

    tm = tiles[0] or next((t for t in (4096, 2048, 1024, 512, 256) if m_dim % t == 0 and vmem_bytes(t) <= MM_VMEM_BUDGET),
                          _tile(m_dim, (128,)))
    assert m_dim % tm == 0 and n_dim % tn == 0 and k_dim % tk == 0, (name, m_dim, n_dim, k_dim, tm, tn, tk)
    assert tn == n_dim or not (any(widths) or accs), name
    n_ex, n_out = len(extras), len(out_dtypes)
    n_into = 0 if into is None else 1
    dims = (((0 if ta else 1,), (1 if tb else 0,)), ((), ()))

    def body(a_ref, b_ref, *rest):
        ex_refs, out_refs = rest[:n_ex], rest[n_ex + n_into:n_ex + n_into + n_out]
        sum_refs = rest[n_ex + n_into + n_out:n_ex + n_into + n_out + len(accs)]

        def partial():
            av = a_ref[...]
            if pro_a is not None:
                av = pro_a(av)
            return lax.dot_general(av.astype(BF16), b_ref[...].astype(BF16), dims, preferred_element_type=F32)

        def finish(r):
            res = epi(r, *[e[...] for e in ex_refs]) if epi is not None else (r,)
            for o_ref, v in zip(out_refs, res):
                o_ref[...] = v.reshape(o_ref.shape).astype(o_ref.dtype)
            if accs:
                @pl.when(pl.program_id(0) == 0)
                def _():
                    for s_ref in sum_refs:
                        s_ref[...] = jnp.zeros_like(s_ref)

                for s_ref, v in zip(sum_refs, res[n_out:]):
                    s_ref[...] += v

        if nk == 1:
            finish(partial())
            return
        acc = rest[-1]
        k = pl.program_id(2)

        @pl.when(k == 0)
        def _():
            acc[...] = partial()

        @pl.when(k > 0)
        def _():
            acc[...] += partial()

        @pl.when(k == nk - 1)
        def _():
            finish(acc[...])

    def ex_spec(e):
        if e.shape == (m_dim, n_dim):
            return o_spec
        if e.shape[0] == m_dim:
            return pl.BlockSpec((tm, e.shape[1]), lambda i, j, k: (i, 0))
        return pl.BlockSpec(e.shape, lambda i, j, k: (0, 0))

    a_spec = pl.BlockSpec((tk, tm), lambda i, j, k: (k, i)) if ta else pl.BlockSpec((tm, tk), lambda i, j, k: (i, k))
    if b_view is not None:
        b_spec = b_view(tk, tn)
    else:
        b_spec = pl.BlockSpec((tn, tk), lambda i, j, k: (j, k)) if tb else pl.BlockSpec((tk, tn), lambda i, j, k: (k, j))
    o_spec = pl.BlockSpec((tm, tn), lambda i, j, k: (i, j))
    if out_view is None:
        out_specs = [o_spec if w is None else pl.BlockSpec((tm, w), lambda i, j, k: (i, 0)) for w in widths]
        out_shape = [jax.ShapeDtypeStruct((m_dim, w or n_dim), dt) for w, dt in zip(widths, out_dtypes)]
    else:
        assert n_out == 1
        out_specs = [out_view[1](tm, tn)]
        out_shape = [jax.ShapeDtypeStruct(out_view[0], out_dtypes[0])]
    out_specs = out_specs + [pl.BlockSpec((1, w), lambda i, j, k: (0, 0)) for w in accs]
    out_shape = out_shape + [jax.ShapeDtypeStruct((1, w), F32) for w in accs]
    grid = (m_dim // tm, n_dim // tn, nk)
    scratch = [pltpu.VMEM((tm, tn), F32)] if nk > 1 else []
    if ride is not None:
        assert into is None
        at = lambda ids: functools.reduce(jnp.logical_and, [pl.program_id(d) == i for d, i in enumerate(ids)])
        outs, landed = _pcall_riding(
            body, (a, b, *extras), ride, lambda: at((0, 0, 0)), lambda: at([g - 1 for g in grid]),
            name=name, grid=grid, in_specs=[a_spec, b_spec] + [ex_spec(e) for e in extras], out_specs=out_specs,
            out_shape=out_shape, scratch_shapes=scratch, compiler_params=_params(("arbitrary",) * 3))
        return (outs[0] if len(outs) == 1 else outs), landed
    outs = _pcall(
        body, name=name, grid=grid,
        in_specs=[a_spec, b_spec] + [ex_spec(e) for e in extras] + [_ANY] * n_into,
        out_specs=out_specs, out_shape=out_shape,
        input_output_aliases={2 + n_ex: 0} if n_into else {},
        scratch_shapes=scratch,
        compiler_params=_params(("arbitrary",) * 3 if accs else ("parallel", "parallel", "arbitrary")),
    )(a, b, *extras, *([into] if n_into else []))
    return outs[0] if len(outs) == 1 else outs


def rowwise(fn, ins, outs, *, name, accs=(), tm=256):
    rows = ins[0].shape[0]
    tm = min(tm, rows)
    n_in, n_out, n_acc = len(ins), len(outs), len(accs)

    def body(*refs):
        in_refs, out_refs, acc_refs = refs[:n_in], refs[n_in:n_in + n_out], refs[n_in + n_out:]
        res, sums = fn(*[r[...] for r in in_refs])
        for o_ref, v in zip(out_refs, res):
            o_ref[...] = v.astype(o_ref.dtype)
        if n_acc:
            @pl.when(pl.program_id(0) == 0)
            def _():
                for a_ref in acc_refs:
                    a_ref[...] = jnp.zeros_like(a_ref)

            for a_ref, s in zip(acc_refs, sums):
                a_ref[...] += s

    def spec(arr):
        if arr.shape[0] == rows:
            return pl.BlockSpec((tm, arr.shape[1]), lambda i: (i, 0))
        return pl.BlockSpec(arr.shape, lambda i: (0, 0))

    res = _pcall(
        body, name=name, grid=(rows // tm,),
        in_specs=[spec(a) for a in ins],
        out_specs=[pl.BlockSpec((tm, w), lambda i: (i, 0)) for w, _ in outs]
        + [pl.BlockSpec((1, w), lambda i: (0, 0)) for w in accs],
        out_shape=[jax.ShapeDtypeStruct((rows, w), dt) for w, dt in outs]
        + [jax.ShapeDtypeStruct((1, w), F32) for w in accs],
        compiler_params=_params(("arbitrary",) if n_acc else ("parallel",)),
    )(*ins)
    return res


def _relu2(v):
    r = jnp.maximum(v, 0.0)
    return r * r


def _gelu(x):
    c = math.sqrt(2.0 / math.pi)
    return 0.5 * x * (1.0 + jnp.tanh(c * (x + 0.044715 * x * x * x)))


def _gelu_grad(x):
    c = math.sqrt(2.0 / math.pi)
    t = jnp.tanh(c * (x + 0.044715 * x * x * x))
    return 0.5 * (1.0 + t) + 0.5 * x * (1.0 - t * t) * c * (1.0 + 3 * 0.044715 * x * x)


def _sigmoid(x):
    return 1.0 / (1.0 + jnp.exp(-x))


def _layer_norm(h, mix, g, b):
    r = DN_ALPHA * h + mix
    mu = jnp.mean(r, axis=-1, keepdims=True)
    xc = r - mu
    var = jnp.mean(xc * xc, axis=-1, keepdims=True)
    return xc * lax.rsqrt(var + LN_EPS) * g + b


def _layer_norm_bwd(h, mix, g, dy):
    r = DN_ALPHA * h + mix
    mu = jnp.mean(r, axis=-1, keepdims=True)
    xc = r - mu
    var = jnp.mean(xc * xc, axis=-1, keepdims=True)
    rstd = lax.rsqrt(var + LN_EPS)
    xhat = xc * rstd
    dxh = dy * g
    m1 = jnp.mean(dxh, axis=-1, keepdims=True)
    m2 = jnp.mean(dxh * xhat, axis=-1, keepdims=True)
    dr = rstd * (dxh - m1 - xhat * m2)
    return dr, jnp.sum(dy * xhat, axis=0, keepdims=True), jnp.sum(dy, axis=0, keepdims=True)


def ln_bwd(h, mix, g, dy, name):
    def fn(h, mix, g, dy):
        dr, dg, db = _layer_norm_bwd(h, mix, g, dy)
        return (dr, dr), (dg, db)
    return rowwise(fn, (h, mix, g, dy), ((D_MODEL, F32), (D_MODEL, BF16)), accs=(D_MODEL, D_MODEL), name=name)


def _rms(x, g):
    r = lax.rsqrt(jnp.mean(x * x, axis=-1, keepdims=True) + RMS_EPS)
    return x * r * g


def _rms_bwd(x, g, dy):
    r = lax.rsqrt(jnp.mean(x * x, axis=-1, keepdims=True) + RMS_EPS)
    xn = x * r
    dyg = dy * g
    dx = r * (dyg - xn * jnp.mean(dyg * xn, axis=-1, keepdims=True))
    return dx, jnp.sum(dy * xn, axis=0, keepdims=True)


def _s5_disc(lr, li, ldt):
    dt = jnp.exp(ldt)
    mag = jnp.exp(lr * dt)
    cs, sn = jnp.cos(li * dt), jnp.sin(li * dt)
    ar, ai = mag * cs, mag * sn
    inv = 1.0 / (lr * lr + li * li)
    n_re = (ar - 1.0) * lr + ai * li
    n_im = ai * lr - (ar - 1.0) * li
    return dt, mag, cs, sn, ar, ai, inv, n_re, n_im


def s5_prep(lr, li, ldt, b_re, b_im):
    def fn(lr, li, ldt, b_re, b_im):
        _, _, _, _, ar, ai, inv, n_re, n_im = _s5_disc(lr, li, ldt)
        cr, ci = n_re * inv, n_im * inv
        return (ar, ai, cr * b_re - ci * b_im, cr * b_im + ci * b_re), ()
    return rowwise(fn, (lr, li, ldt, b_re, b_im), ((1, F32), (1, F32), (SSM_GROUP, F32), (SSM_GROUP, F32)),
                   name="s5_prep", tm=512)


def s5_prep_bwd(lr, li, ldt, b_re, b_im, dar, dai, dbb_re, dbb_im):
    def fn(lr, li, ldt, b_re, b_im, dar, dai, dbb_re, dbb_im):
        dt, mag, cs, sn, ar, ai, inv, n_re, n_im = _s5_disc(lr, li, ldt)
        cr, ci = n_re * inv, n_im * inv
        db_re = cr * dbb_re + ci * dbb_im
        db_im = cr * dbb_im - ci * dbb_re
        dcr = jnp.sum(dbb_re * b_re + dbb_im * b_im, axis=-1, keepdims=True)
        dci = jnp.sum(dbb_im * b_re - dbb_re * b_im, axis=-1, keepdims=True)
        dar = dar + (dcr * lr - dci * li) * inv
        dai = dai + (dcr * li + dci * lr) * inv
        dinv = dcr * n_re + dci * n_im
        dlr = (dcr * (ar - 1.0) + dci * ai) * inv - 2.0 * lr * inv * inv * dinv
        dli = (dcr * ai - dci * (ar - 1.0)) * inv - 2.0 * li * inv * inv * dinv
        dmag = dar * cs + dai * sn
        dth = dai * ar - dar * ai
        dlr = dlr + dmag * mag * dt
        dli = dli + dth * dt
        ddt = dmag * mag * lr + dth * li
        return (dlr, dli, ddt * dt, db_re, db_im), ()
    return rowwise(fn, (lr, li, ldt, b_re, b_im, dar, dai, dbb_re, dbb_im),
                   ((1, F32), (1, F32), (1, F32), (SSM_GROUP, F32), (SSM_GROUP, F32)), name="s5_prep_bwd", tm=512)


def group_sum(x):
    def body(x_ref, o_ref):
        o_ref[...] = jnp.sum(x_ref[...], axis=1)
    return _pcall(body, name="s5_group_sum", out_shape=jax.ShapeDtypeStruct((N_GROUPS, 1), F32))(
        x.reshape(N_GROUPS, SSM_STATE, 1))


GROUPS_PER_TILE = LANES // SSM_GROUP
TILE_STATES = GROUPS_PER_TILE * SSM_STATE
N_UTILES = D_MODEL // LANES


SUBLANES = 8
SCAN_STRIP = 1024
N_STRIPS = N_STATES // SCAN_STRIP
_NT = (((1,), (1,)), ((), ()))
_TN = (((0,), (0,)), ((), ()))


def _scan_coefs(are, aim, shifted, reverse):
    ar = are[...]
    ai = -aim[...] if reverse else aim[...]
    powers = {1: (ar, ai)}
    for d in (2, 4):
        r, i = powers[d // 2]
        powers[d] = (r * r - i * i, 2.0 * r * i)
    rid = lax.broadcasted_iota(jnp.int32, (SUBLANES, N_STATES), 0)
    first = (rid == SUBLANES - 1) if reverse else (rid == 0)
    masks = [(1, first)] + [(d, (rid <= SUBLANES - 1 - d) if reverse else (rid >= d)) for d in (1, 2, 4)]
    for n, (d, keep) in enumerate(masks):
        for part in (0, 1):
            shifted[2 * n + part][...] = jnp.where(keep, jnp.broadcast_to(powers[d][part], (SUBLANES, N_STATES)), 0.0)


def _tile_scan(xr, xi, shifted, nbr_re, nbr_im, reverse):
    for n, d in enumerate((1, 1, 2, 4)):
        by = SUBLANES - d if reverse else d
        fr, fi = (nbr_re, nbr_im) if n == 0 else (xr, xi)
        sr, si = pltpu.roll(fr, by, 0), pltpu.roll(fi, by, 0)
        kr, ki = shifted[2 * n], shifted[2 * n + 1]
        xr, xi = xr + kr * sr - ki * si, xi + kr * si + ki * sr
    return xr, xi


def _tile_rows(t):
    return pl.ds(pl.multiple_of(t * SUBLANES, SUBLANES), SUBLANES)


def s5_fwd(u, bbd_re, bbd_im, cbd_re, cbd_imn, a_re, a_im, dskip, ride=None, t_rows=256):
    seq = u.shape[0]
    t_rows = min(t_rows, seq)
    n_tiles = t_rows // SUBLANES

    def body(u_ref, bre, bim, cre, cimn, are, aim, d_ref, y_ref, gelu_ref, hre_ref, him_ref, car_re, car_im, *shifted):
        @pl.when(pl.program_id(0) == 0)
        def _():
            car_re[...] = jnp.zeros_like(car_re)
            car_im[...] = jnp.zeros_like(car_im)
            _scan_coefs(are, aim, shifted, reverse=False)

        uf = u_ref[...]
        ub = uf.astype(BF16)
        for j in range(N_UTILES):
            uj = ub[:, LANES * j:LANES * (j + 1)]
            sl = slice(TILE_STATES * j, TILE_STATES * (j + 1))
            hre_ref[:, sl] = jnp.dot(uj, bre[j], preferred_element_type=F32)
            him_ref[:, sl] = jnp.dot(uj, bim[j], preferred_element_type=F32)
        for s in range(N_STRIPS):
            cols = pl.ds(s * SCAN_STRIP, SCAN_STRIP)
            coefs = [c[:, cols] for c in shifted]

            def step(t, before):
                rows = _tile_rows(t)
                hr, hi = _tile_scan(hre_ref[rows, cols], him_ref[rows, cols], coefs, before[0], before[1], False)
                hre_ref[rows, cols] = hr
                him_ref[rows, cols] = hi
                return hr, hi

            cr, ci = lax.fori_loop(0, n_tiles, step, (car_re[:, cols], car_im[:, cols]))
            car_re[:, cols] = cr
            car_im[:, cols] = ci
        dv = d_ref[...]
        for j in range(N_UTILES):
            st = slice(TILE_STATES * j, TILE_STATES * (j + 1))
            yj = (jnp.dot(hre_ref[:, st].astype(BF16), cre[j], preferred_element_type=F32)
                  + jnp.dot(him_ref[:, st].astype(BF16), cimn[j], preferred_element_type=F32))
            sl = slice(LANES * j, LANES * (j + 1))
            yj = yj + dv[:, sl] * uf[:, sl]
            y_ref[:, sl] = yj
            gelu_ref[:, sl] = _gelu(yj).astype(gelu_ref.dtype)

    full3 = lambda a: pl.BlockSpec(a.shape, lambda i: (0, 0, 0))
    full2 = lambda a: pl.BlockSpec(a.shape, lambda i: (0, 0))
    tile = pltpu.VMEM((SUBLANES, N_STATES), F32)
    n_chunks = seq // t_rows
    return _pcall_riding(
        body, (u, bbd_re, bbd_im, cbd_re, cbd_imn, a_re, a_im, dskip), ride,
        lambda: pl.program_id(0) == 0, lambda: pl.program_id(0) == n_chunks - 1,
        middle=(lambda: pl.program_id(0) == (7 * n_chunks) // 8) if ride is not None else None,
        name="s5_fwd", grid=(n_chunks,),
        in_specs=[pl.BlockSpec((t_rows, D_MODEL), lambda i: (i, 0)), full3(bbd_re), full3(bbd_im), full3(cbd_re),
                  full3(cbd_imn), full2(a_re), full2(a_im), full2(dskip)],
        out_specs=[pl.BlockSpec((t_rows, D_MODEL), lambda i: (i, 0)),
                   pl.BlockSpec((t_rows, D_MODEL), lambda i: (i, 0)),
                   pl.BlockSpec((t_rows, N_STATES), lambda i: (i, 0)),
                   pl.BlockSpec((t_rows, N_STATES), lambda i: (i, 0))],
        out_shape=[jax.ShapeDtypeStruct((seq, D_MODEL), F32),
                   jax.ShapeDtypeStruct((seq, D_MODEL), BF16),
                   jax.ShapeDtypeStruct((seq, N_STATES), F32),
                   jax.ShapeDtypeStruct((seq, N_STATES), F32)],
        scratch_shapes=[tile] * 10,
        compiler_params=_params(("arbitrary",)))


def s5_bwd(dy, u, dres, h_re, h_im, bbd_re, bbd_im, cbd_re, cbd_imn, a_re, a_im, dskip, ride=None, t_rows=256):
    seq = u.shape[0]
    t_rows = min(t_rows, seq)
    n_chunks = seq // t_rows

    n_tiles = t_rows // SUBLANES

    def body(dy_ref, u_ref, dres_ref, hre_ref, him_ref, hpre_ref, hpim_ref, bre, bim, cre, cimn, are, aim, d_ref,
             dx_ref, dbre, dbim, dcre, dcimn, dar_ref, dai_ref, dd_ref, lre, lim, car_re, car_im, acc_re, acc_im,
             *shifted):
        i = pl.program_id(0)

        @pl.when(i == 0)
        def _():
            for r in (car_re, car_im, acc_re, acc_im, dbre, dbim, dcre, dcimn, dd_ref):
                r[...] = jnp.zeros_like(r)
            _scan_coefs(are, aim, shifted, reverse=True)

        dyf = dy_ref[...]
        dyb = dyf.astype(BF16)
        uf = u_ref[...]
        ub = uf.astype(BF16)
        for j in range(N_UTILES):
            dyj = dyb[:, LANES * j:LANES * (j + 1)]
            st = slice(TILE_STATES * j, TILE_STATES * (j + 1))
            lre[:, st] = lax.dot_general(dyj, cre[j], _NT, preferred_element_type=F32)
            lim[:, st] = lax.dot_general(dyj, cimn[j], _NT, preferred_element_type=F32)
        has_pred = (i < n_chunks - 1).astype(F32)
        last_row = lax.broadcasted_iota(jnp.int32, (SUBLANES, SCAN_STRIP), 0) == SUBLANES - 1
        for s in range(N_STRIPS):
            cols = pl.ds(s * SCAN_STRIP, SCAN_STRIP)
            coefs = [c[:, cols] for c in shifted]
            before_re, before_im = hpre_ref[:, cols] * has_pred, hpim_ref[:, cols] * has_pred

            def step(k, carry):
                after_re, after_im, dar, dai = carry
                t = n_tiles - 1 - k
                rows = _tile_rows(t)
                lr, li = _tile_scan(lre[rows, cols], lim[rows, cols], coefs, after_re, after_im, True)
                lre[rows, cols] = lr
                lim[rows, cols] = li
                prev = _tile_rows(jnp.maximum(t - 1, 0))
                pre_re = jnp.where(t == 0, before_re, hre_ref[prev, cols])
                pre_im = jnp.where(t == 0, before_im, him_ref[prev, cols])
                hpr = pltpu.roll(jnp.where(last_row, pre_re, hre_ref[rows, cols]), 1, 0)
                hpi = pltpu.roll(jnp.where(last_row, pre_im, him_ref[rows, cols]), 1, 0)
                return lr, li, dar + lr * hpr + li * hpi, dai + li * hpr - lr * hpi

            cr, ci, dar, dai = lax.fori_loop(0, n_tiles, step, (car_re[:, cols], car_im[:, cols],
                                                               acc_re[:, cols], acc_im[:, cols]))
            car_re[:, cols] = cr
            car_im[:, cols] = ci
            acc_re[:, cols] = dar
            acc_im[:, cols] = dai

        dv = d_ref[...]
        for j in range(N_UTILES):
            sl = slice(LANES * j, LANES * (j + 1))
            st = slice(TILE_STATES * j, TILE_STATES * (j + 1))
            lrj = lre[:, st].astype(BF16)
            lij = lim[:, st].astype(BF16)
            du = (lax.dot_general(lrj, bre[j], _NT, preferred_element_type=F32)
                  + lax.dot_general(lij, bim[j], _NT, preferred_element_type=F32))
            dx_ref[:, sl] = du + dv[:, sl] * dyf[:, sl] + DN_ALPHA * dres_ref[:, sl]
            uj = ub[:, sl]
            dbre[j] += lax.dot_general(uj, lrj, _TN, preferred_element_type=F32)
            dbim[j] += lax.dot_general(uj, lij, _TN, preferred_element_type=F32)
            dyj = dyb[:, sl]
            dcre[j] += lax.dot_general(hre_ref[:, st].astype(BF16), dyj, _TN, preferred_element_type=F32)
            dcimn[j] += lax.dot_general(him_ref[:, st].astype(BF16), dyj, _TN, preferred_element_type=F32)
        dd_ref[...] += jnp.sum(dyf * uf, axis=0, keepdims=True)

        @pl.when(i == n_chunks - 1)
        def _():
            dar_ref[...] = jnp.sum(acc_re[...], axis=0, keepdims=True)
            dai_ref[...] = jnp.sum(acc_im[...], axis=0, keepdims=True)

    rev = lambda i: (n_chunks - 1 - i, 0)
    prev_tile = lambda i: (jnp.maximum((n_chunks - 1 - i) * n_tiles - 1, 0), 0)
    once = pl.Buffered(1)
    full3 = lambda a: pl.BlockSpec(a.shape, lambda i: (0, 0, 0), pipeline_mode=once)
    full2 = lambda a: pl.BlockSpec(a.shape, lambda i: (0, 0), pipeline_mode=once)
    acc3 = lambda shape: pl.BlockSpec(shape, lambda i: (0, 0, 0))
    acc2 = lambda shape: pl.BlockSpec(shape, lambda i: (0, 0))
    tile = pltpu.VMEM((SUBLANES, N_STATES), F32)
    return _pcall_riding(
        body, (dy, u, dres, h_re, h_im, h_re, h_im, bbd_re, bbd_im, cbd_re, cbd_imn, a_re, a_im, dskip), ride,
        lambda: pl.program_id(0) == 0, lambda: pl.program_id(0) == n_chunks - 1,
        name="s5_bwd", grid=(n_chunks,),
        in_specs=[pl.BlockSpec((t_rows, D_MODEL), rev), pl.BlockSpec((t_rows, D_MODEL), rev),
                  pl.BlockSpec((t_rows, D_MODEL), rev),
                  pl.BlockSpec((t_rows, N_STATES), rev), pl.BlockSpec((t_rows, N_STATES), rev),
                  pl.BlockSpec((SUBLANES, N_STATES), prev_tile), pl.BlockSpec((SUBLANES, N_STATES), prev_tile),
                  full3(bbd_re), full3(bbd_im), full3(cbd_re), full3(cbd_imn), full2(a_re), full2(a_im), full2(dskip)],
        out_specs=[pl.BlockSpec((t_rows, D_MODEL), rev), acc3(bbd_re.shape), acc3(bbd_im.shape), acc3(cbd_re.shape),
                   acc3(cbd_imn.shape), acc2((1, N_STATES)), acc2((1, N_STATES)), acc2((1, D_MODEL))],
        out_shape=[jax.ShapeDtypeStruct((seq, D_MODEL), F32), jax.ShapeDtypeStruct(bbd_re.shape, F32),
                   jax.ShapeDtypeStruct(bbd_im.shape, F32), jax.ShapeDtypeStruct(cbd_re.shape, F32),
                   jax.ShapeDtypeStruct(cbd_imn.shape, F32), jax.ShapeDtypeStruct((1, N_STATES), F32),
                   jax.ShapeDtypeStruct((1, N_STATES), F32), jax.ShapeDtypeStruct((1, D_MODEL), F32)],
        scratch_shapes=[pltpu.VMEM((t_rows, N_STATES), F32), pltpu.VMEM((t_rows, N_STATES), F32)] + [tile] * 12,
        in_place={2: 0},
        compiler_params=_params(("arbitrary",)))


def _eye_groups():
    return jnp.eye(GROUPS_PER_TILE, dtype=F32)


def _blockdiag_in(bb):
    t = bb.transpose(0, 2, 1).reshape(N_UTILES, GROUPS_PER_TILE, SSM_GROUP, SSM_STATE)
    bd = jnp.einsum("jgcp,gh->jgchp", t, _eye_groups())
    return bd.reshape(N_UTILES, LANES, TILE_STATES)


def _blockdiag_in_t(d):
    t = jnp.einsum("jgchp,gh->jgcp", d.reshape(N_UTILES, GROUPS_PER_TILE, SSM_GROUP, GROUPS_PER_TILE, SSM_STATE),
                   _eye_groups())
    return t.reshape(N_GROUPS, SSM_GROUP, SSM_STATE).transpose(0, 2, 1)


def _blockdiag_out(c):
    t = c.transpose(0, 2, 1).reshape(N_UTILES, GROUPS_PER_TILE, SSM_STATE, SSM_GROUP)
    bd = jnp.einsum("jhpc,hg->jhpgc", t, _eye_groups())
    return bd.reshape(N_UTILES, TILE_STATES, LANES)


def _blockdiag_out_t(d):
    t = jnp.einsum("jhpgc,hg->jhpc", d.reshape(N_UTILES, GROUPS_PER_TILE, SSM_STATE, GROUPS_PER_TILE, SSM_GROUP),
                   _eye_groups())
    return t.reshape(N_GROUPS, SSM_STATE, SSM_GROUP).transpose(0, 2, 1)


ATT_TQ = 512
ATT_TK = 512
LOG2E = math.log2(math.e)
LN2 = math.log(2.0)
Q_PRESCALE = SM_SCALE * LOG2E


def _loop_in_pairs(n, step, carry, start=0):
    pairs = (n - start) // 2

    def two(t, c):
        return step(start + 2 * t + 1, step(start + 2 * t, c))

    carry = lax.fori_loop(0, pairs, two, carry)
    return lax.fori_loop(start + 2 * pairs, n, step, carry)


def _causal(s, transposed=False):
    r = lax.broadcasted_iota(jnp.int32, s.shape, 0)
    c = lax.broadcasted_iota(jnp.int32, s.shape, 1)
    return jnp.where((r <= c) if transposed else (c <= r), s, NEG_INF)


def _q_specs(rows, at):
    def nope(*ids):
        r, h = at(*ids)
        return r, 3 * (h // HEADS_PER_CHIP) + h % HEADS_PER_CHIP

    def rope(*ids):
        r, h = at(*ids)
        return r, 3 * (h // HEADS_PER_CHIP) + HEADS_PER_CHIP

    return [pl.BlockSpec((rows, LANES), nope), pl.BlockSpec((rows, LANES), rope)]


def _kv_specs(rows, at):
    def col(f):
        def index(*ids):
            r, h = at(*ids)
            return r, f(h)
        return index

    return [pl.BlockSpec((rows, LANES), col(lambda h: 2 * h)), pl.BlockSpec((rows, LANES), col(lambda h: h % HEADS_PER_CHIP)),
            pl.BlockSpec((rows, LANES), col(lambda h: 2 * h + 1))]


def _cat(a, b):
    return jnp.concatenate([a, b], axis=1)


def attn_fwd(q, kv, kr, ride=None, tq=ATT_TQ, tk=ATT_TK):
    seq = q.shape[0]
    n_heads = N_HEADS
    tq, tk = min(tq, seq), min(tk, seq)
    assert tq == tk

    def body(qn_ref, qr_ref, kn_ref, kr_ref, v_ref, o_ref, lse_ref):
        qi = pl.program_id(1)
        qv = _cat(qn_ref[...], qr_ref[...])
        jd = qi

        def block(j, carry, diag):
            m, l, acc = carry
            rows = pl.ds(pl.multiple_of(j * tk, tk), tk)
            s = lax.dot_general(qv, _cat(kn_ref[rows, :], kr_ref[rows, :]), _NT, preferred_element_type=F32)
            if diag:
                s = _causal(s)
            m_new = jnp.maximum(m, jnp.max(s, axis=-1, keepdims=True))
            p = jnp.exp2(s - m_new)
            corr = jnp.exp2(m - m_new)
            l = l * corr + jnp.sum(p, axis=-1, keepdims=True)
            acc = acc * corr + jnp.dot(p.astype(BF16), v_ref[rows, :], preferred_element_type=F32)
            return m_new, l, acc

        init = (jnp.full((tq, 1), NEG_INF, F32), jnp.zeros((tq, 1), F32), jnp.zeros((tq, V_HEAD), F32))
        carry = _loop_in_pairs(jd, lambda j, c: block(j, c, False), init)
        m, l, acc = block(jd, carry, True)
        o_ref[...] = acc / l
        lse_ref[...] = jnp.transpose(jnp.broadcast_to(m + jnp.log2(l), (tq, LANES)))[:1, :]

    n_q = seq // tq
    return _pcall_riding(
        body, (q, q, kv, kr, kv), ride,
        lambda: (pl.program_id(0) == 0) & (pl.program_id(1) == 0),
        lambda: (pl.program_id(0) == n_heads - 1) & (pl.program_id(1) == n_q - 1),
        middle=(lambda: (pl.program_id(0) == (5 * n_heads) // 8) & (pl.program_id(1) == 0)) if ride is not None else None,
        name="attn_fwd", grid=(n_heads, n_q),
        in_specs=_q_specs(tq, lambda h, i: (i, h)) + _kv_specs(seq, lambda h, i: (0, h)),
        out_specs=[pl.BlockSpec((tq, V_HEAD), lambda h, i: (i, h)),
                   pl.BlockSpec((None, None, 1, tq), lambda h, i: (h, i, 0, 0))],
        out_shape=[jax.ShapeDtypeStruct((seq, n_heads * V_HEAD), F32),
                   jax.ShapeDtypeStruct((n_heads, n_q, 1, tq), F32)],
        compiler_params=_params(("arbitrary", "arbitrary")))


def attn_bwd(q, kv, kr, do, lse_row, delta_row, tq=ATT_TK):
    seq = q.shape[0]
    tq = min(tq, seq)
    n_blk = seq // tq

    def body(qn_ref, qr_ref, kn_ref, kr_ref, v_ref, do_ref, lse_ref, delta_ref, dqn_ref, dqr_ref, dkv_ref, dkr_ref, dq_acc):
        head, kj = pl.program_id(0), pl.program_id(1)

        @pl.when(kj == 0)
        def _():
            dq_acc[...] = jnp.zeros_like(dq_acc)

        kc = _cat(kn_ref[...], kr_ref[...])
        vv = v_ref[...]

        def block(i, carry, diag):
            dk, dv = carry
            rows = pl.ds(pl.multiple_of(i * tq, tq), tq)
            qv = _cat(qn_ref[rows, :], qr_ref[rows, :])
            st = lax.dot_general(kc, qv, _NT, preferred_element_type=F32)
            if diag:
                st = _causal(st, transposed=True)
            pt = jnp.exp2(st - lse_ref[0, pl.ds(i, 1), :])
            dob = do_ref[rows, :].astype(BF16)
            dv = dv + jnp.dot(pt.astype(BF16), dob, preferred_element_type=F32)
            dpt = lax.dot_general(vv, dob, _NT, preferred_element_type=F32)
            dst = (pt * (dpt - delta_ref[0, pl.ds(i, 1), :])).astype(BF16)
            dk = dk + jnp.dot(dst, qv, preferred_element_type=F32)
            dq_acc[rows, :] += lax.dot_general(dst, kc, _TN, preferred_element_type=F32)
            return dk, dv

        carry = block(kj, (jnp.zeros((tq, 2 * LANES), F32), jnp.zeros((tq, V_HEAD), F32)), True)
        dk, dv = _loop_in_pairs(n_blk, lambda i, c: block(i, c, False), carry, start=kj + 1)
        dk = dk * LN2
        dkv_ref[...] = _cat(dk[:, :LANES], dv).astype(dkv_ref.dtype)
        lane = lax.broadcasted_iota(jnp.int32, (tq, LANES), 1)
        mine = (lane // HALF_ROPE) % HEADS_PER_CHIP == head % HEADS_PER_CHIP
        dkr_ref[0] = jnp.where(mine, dk[:, LANES:], 0.0)

        @pl.when(kj == n_blk - 1)
        def _():
            dqn_ref[...] = dq_acc[:, :LANES] * SM_SCALE

        @pl.when((kj == n_blk - 1) & (head % HEADS_PER_CHIP == 0))
        def _():
            dqr_ref[...] = dq_acc[:, LANES:] * SM_SCALE

        @pl.when((kj == n_blk - 1) & (head % HEADS_PER_CHIP > 0))
        def _():
            dqr_ref[...] += dq_acc[:, LANES:] * SM_SCALE

    return _pcall(
        body, name="attn_bwd", grid=(N_HEADS, n_blk),
        in_specs=_q_specs(seq, lambda h, j: (0, h)) + _kv_specs(tq, lambda h, j: (j, h))
        + [pl.BlockSpec((seq, V_HEAD), lambda h, j: (0, h)),
           pl.BlockSpec((1, n_blk, tq), lambda h, j: (h, 0, 0)),
           pl.BlockSpec((1, n_blk, tq), lambda h, j: (h, 0, 0))],
        out_specs=[pl.BlockSpec((seq, LANES), lambda h, j: (0, h)),
                   pl.BlockSpec((seq, LANES), lambda h, j: (0, h // HEADS_PER_CHIP)),
                   pl.BlockSpec((tq, QK_NOPE + V_HEAD), lambda h, j: (j, h)),
                   pl.BlockSpec((1, tq, LANES), lambda h, j: (h, j, 0))],
        out_shape=[jax.ShapeDtypeStruct((seq, N_HEADS * QK_NOPE), F32),
                   jax.ShapeDtypeStruct((seq, N_CHIPS * LANES), F32),
                   jax.ShapeDtypeStruct((seq, N_HEADS * (QK_NOPE + V_HEAD)), BF16),
                   jax.ShapeDtypeStruct((N_HEADS, seq, LANES), F32)],
        scratch_shapes=[pltpu.VMEM((seq, 2 * LANES), F32)],
        compiler_params=_params(("arbitrary", "arbitrary")),
    )(q, q, kv, kr, kv, do, lse_row, delta_row)


def head_sum(x, ts=512):
    n_heads, seq, w = x.shape
    ts = min(ts, seq)

    def body(x_ref, o_ref):
        o_ref[...] = jnp.sum(x_ref[...], axis=0)

    return _pcall(body, name="head_sum", grid=(seq // ts,),
                  in_specs=[pl.BlockSpec((n_heads, ts, w), lambda i: (0, i, 0))],
                  out_specs=pl.BlockSpec((ts, w), lambda i: (i, 0)),
                  out_shape=jax.ShapeDtypeStruct((seq, w), F32),
                  compiler_params=_params(("parallel",)))(x)


HEADS_PER_CHIP = N_HEADS // N_CHIPS
Q_CHIP = HEADS_PER_CHIP * QK_DIM
Q_CHIP_NOPE = HEADS_PER_CHIP * QK_NOPE


def _perm_q_cols(w):
    t = w.reshape(w.shape[0], HEADS_PER_CHIP, QK_DIM)
    return jnp.concatenate([t[:, :, :QK_NOPE].reshape(w.shape[0], -1),
                            t[:, :, QK_NOPE:QK_NOPE + HALF_ROPE].reshape(w.shape[0], -1),
                            t[:, :, QK_NOPE + HALF_ROPE:].reshape(w.shape[0], -1)], axis=1)


def _unperm_q_cols(w):
    r = w.shape[0]
    nope = w[:, :Q_CHIP_NOPE].reshape(r, HEADS_PER_CHIP, QK_NOPE)
    r1 = w[:, Q_CHIP_NOPE:Q_CHIP_NOPE + QK_ROPE].reshape(r, HEADS_PER_CHIP, HALF_ROPE)
    r2 = w[:, Q_CHIP_NOPE + QK_ROPE:].reshape(r, HEADS_PER_CHIP, HALF_ROPE)
    return jnp.concatenate([nope, r1, r2], axis=2).reshape(r, Q_CHIP)


def _pad_kva_cols(w):
    z = jnp.zeros((w.shape[0], HALF_ROPE), w.dtype)
    return jnp.concatenate([w[:, :KV_LORA], w[:, KV_LORA:KV_LORA + HALF_ROPE], z, w[:, KV_LORA + HALF_ROPE:], z], axis=1)


def _unpad_kva_cols(w):
    return jnp.concatenate([w[:, :KV_LORA], w[:, KV_LORA:KV_LORA + HALF_ROPE],
                            w[:, KV_LORA + QK_ROPE:KV_LORA + QK_ROPE + HALF_ROPE]], axis=1)


def _rope_tile(t, cs, sn):
    return t * cs + pltpu.roll(t, LANES // 2, 1) * sn


def _rope_tile_bwd(d, cs, sn):
    return d * cs + pltpu.roll(d * sn, LANES // 2, 1)


def _b_cols(tk, tn):
    return pl.BlockSpec((None, tk, tn), lambda i, j, k: (j, k, 0))


def _b_cols_t(tk, tn):
    return pl.BlockSpec((None, tn, tk), lambda i, j, k: (k, j, 0))


def _out_cols(shape):
    return shape, lambda tm, tn: pl.BlockSpec((None, tm, tn), lambda i, j, k: (j, i, 0))


def glu_proj(y, w_glu, tm=1024):
    seq, k_dim = y.shape
    tn = w_glu.shape[2]
    tm = min(tm, seq)
    half = N_CHIPS // 2

    def body(y_ref, wv_ref, wg_ref, val_ref, gate_ref, z_ref):
        yv = y_ref[...]
        v = jnp.dot(yv, wv_ref[...], preferred_element_type=F32)
        gt = jnp.dot(yv, wg_ref[...], preferred_element_type=F32)
        val_ref[...] = v
        gate_ref[...] = gt
        z_ref[...] = (v * _sigmoid(gt)).astype(z_ref.dtype)

    tile = pl.BlockSpec((tm, tn), lambda i, j: (i, j))
    return _pcall(
        body, name="glu_proj", grid=(seq // tm, half),
        in_specs=[pl.BlockSpec((tm, k_dim), lambda i, j: (i, 0)),
                  pl.BlockSpec((None, k_dim, tn), lambda i, j: (j, 0, 0)),
                  pl.BlockSpec((None, k_dim, tn), lambda i, j: (j + half, 0, 0))],
        out_specs=[tile, tile, tile],
        out_shape=[jax.ShapeDtypeStruct((seq, half * tn), F32), jax.ShapeDtypeStruct((seq, half * tn), F32),
                   jax.ShapeDtypeStruct((seq, half * tn), BF16)],
        compiler_params=_params(("parallel", "parallel")),
    )(y, w_glu, w_glu)


def _halves(a):
    return a.reshape(N_CHIPS, 2, a.shape[1] // 2, a.shape[2])


def device_step(x, positions, target, w, comm=None):
    seq = x.shape[0]
    w = dict(w)

    def gathered(names, outs):
        for n, a in zip(names, outs):
            if isinstance(n, tuple):
                w[n[0]] = [a.reshape(v.shape) if l == n[1] else v for l, v in enumerate(w[n[0]])]
            else:
                w[n] = a.reshape(w[n].shape)

    def ride_for(names):
        if comm is None:
            return None
        return GatherRide([_halves(w[n[0]][n[1]] if isinstance(n, tuple) else w[n]) for n in names])

    first_ride = ("ssm_w_glu", "ssm_w_out", ("w_ff1", 0), ("w_ff2", 0))
    mla_ride = ("kv_w_a", "kv_w_b", "q_w_a", "q_w_b", "attn_w_o")
    second_ride = (("w_ff1", 1), ("w_ff2", 1))

    inv_freq = ROPE_THETA ** (-jnp.arange(HALF_ROPE, dtype=F32) / HALF_ROPE)
    ang = positions.astype(F32)[:, None] * jnp.tile(inv_freq, LANES // HALF_ROPE)
    cos, sin = jnp.cos(ang), jnp.sin(ang)
    quarter = jnp.arange(LANES) // HALF_ROPE
    sign = jnp.where(quarter < 2, -1.0, 1.0).astype(F32)
    own = (quarter % 2 == 0).astype(F32)
    cos_q, sin_q = cos, sin * sign
    cos_k, sin_k = cos * own, sin * (sign * own)
    ff_tile = D_FF // N_CHIPS
    pack_shape = (N_CHIPS, EARLY_ROWS, PACK_W)

    lr = w["ssm_lam_re"].reshape(N_STATES, 1)
    li = w["ssm_lam_im"].reshape(N_STATES, 1)
    ldt = jnp.repeat(w["ssm_log_dt"].reshape(N_GROUPS), SSM_STATE).reshape(N_STATES, 1)
    b_re = w["ssm_b_re"].reshape(N_STATES, SSM_GROUP)
    b_im = w["ssm_b_im"].reshape(N_STATES, SSM_GROUP)
    a_re, a_im, bb_re, bb_im = s5_prep(lr, li, ldt, b_re, b_im)
    a_re, a_im = a_re.reshape(1, N_STATES), a_im.reshape(1, N_STATES)
    bbd_re = _blockdiag_in(bb_re.reshape(N_GROUPS, SSM_STATE, SSM_GROUP)).astype(BF16)
    bbd_im = _blockdiag_in(bb_im.reshape(N_GROUPS, SSM_STATE, SSM_GROUP)).astype(BF16)
    cbd_re = _blockdiag_out(w["ssm_c_re"].reshape(N_GROUPS, SSM_GROUP, SSM_STATE)).astype(BF16)
    cbd_imn = _blockdiag_out(-w["ssm_c_im"].reshape(N_GROUPS, SSM_GROUP, SSM_STATE)).astype(BF16)
    dskip = w["ssm_d"].reshape(1, D_MODEL)
    (ypre, yg, h_re, h_im), landed = s5_fwd(x, bbd_re, bbd_im, cbd_re, cbd_imn, a_re, a_im, dskip, ride_for(first_ride))
    gathered(first_ride, landed)
    w_glu = w["ssm_w_glu"]
    glu_tile = w_glu.shape[2]
    val, gate, z = glu_proj(yg, w_glu)
    w_out = w["ssm_w_out"].reshape(D_MODEL, D_MODEL)
    ln = lambda name, l: w[name][l].reshape(1, D_MODEL)

    def then_ln(h, names, layer):
        def epi(r, hv, gl, bl):
            y = _layer_norm(hv, r, gl, bl)
            return r, y, y
        return dict(epi=epi, extras=(h, ln(names[0], layer), ln(names[1], layer)), out_dtypes=(F32, F32, BF16))

    mix0, h1, h1b = mm(z, w_out, name="ssm_out", **then_ln(x, ("ln_mix_g", "ln_mix_b"), 0))

    def mlp_fwd(h, hb, layer, riding=None, with_ln=True):
        pre = mm(hb, w["w_ff1"][layer], n_dim=D_FF, tiles=(None, ff_tile, None), b_view=_b_cols, name=f"ff1_{layer}",
                 out_dtypes=(BF16,), ride=ride_for(riding) if riding else None)
        if riding and comm is not None:
            pre, landed = pre
            gathered(riding, landed)
        post = then_ln(h, ("ln_ffn_g", "ln_ffn_b"), layer) if with_ln else {}
        return pre, mm(pre, w["w_ff2"][layer].reshape(D_FF, D_MODEL), pro_a=_relu2, name=f"ff2_{layer}", **post)

    f1pre, (f1, h2, h2b) = mlp_fwd(h1, h1b, 0, mla_ride)

    kv_w_a = w["kv_w_a"].reshape(D_MODEL, KVA_PAD)
    kv_w_b = w["kv_w_b"]
    q_w_a = w["q_w_a"].reshape(D_MODEL, Q_LORA)
    q_w_b = w["q_w_b"]
    w_o = w["attn_w_o"].reshape(D_MODEL, D_MODEL)
    kvb_tile = kv_w_b.shape[2]
    kvn_g = w["kv_norm_g"].reshape(1, KV_LORA)
    qn_g = w["q_norm_g"].reshape(1, Q_LORA)
    def kv_post(kva, g, cs, sn):
        tile = _rope_tile(kva[:, KV_LORA:], cs, sn)
        return kva, _rms(kva[:, :KV_LORA], g), _cat(tile, pltpu.roll(tile, HALF_ROPE, 1))
    kva, ckv, krope = mm(h2b, kv_w_a, epi=kv_post, extras=(kvn_g, cos_k, sin_k),
                         out_dtypes=(F32, (KV_LORA, BF16), (2 * LANES, BF16)), name="kv_a")
    kvb = mm(ckv, kv_w_b, n_dim=N_CHIPS * kvb_tile, tiles=(None, kvb_tile, KV_LORA), b_view=_b_cols, name="kv_b",
             out_dtypes=(BF16,))
    cq_raw, cq = mm(h2b, q_w_a, epi=lambda r, gq: (r, _rms(r, gq)), extras=(qn_g,), out_dtypes=(F32, BF16), name="q_a")

    def rope_and_scale(r, cs, sn):
        return (_cat(r[:, :Q_CHIP_NOPE], _rope_tile(r[:, Q_CHIP_NOPE:], cs, sn)) * Q_PRESCALE,)
    qro = mm(cq, q_w_b, n_dim=N_CHIPS * Q_CHIP, tiles=(None, Q_CHIP, Q_LORA), b_view=_b_cols, epi=rope_and_scale,
             extras=(cos_q, sin_q), out_dtypes=(BF16,), name="q_b")
    (o, lse), landed = attn_fwd(qro, kvb, krope, ride_for(second_ride))
    gathered(second_ride, landed)
    mix1, h3, h3b = mm(o, w_o, name="attn_out", **then_ln(h2, ("ln_mix_g", "ln_mix_b"), 1))
    f2pre, f2 = mlp_fwd(h3, h3b, 1, with_ln=False)
    def last_ln_loss_and_back(h, mix, gl, bl, t):
        e = _layer_norm(h, mix, gl, bl) - t
        dr, dg, db = _layer_norm_bwd(h, mix, gl, e * (1.0 / D_MODEL))
        return (dr, dr), (jnp.broadcast_to(jnp.sum(e * e), (1, LANES)), dg, db)
    dr4, dr4b, loss_acc, dg_f1, db_f1 = rowwise(
        last_ln_loss_and_back, (h3, f2, ln("ln_ffn_g", 1), ln("ln_ffn_b", 1), target),
        ((D_MODEL, F32), (D_MODEL, BF16)), accs=(LANES, D_MODEL, D_MODEL), name="ln_ffn_1_loss")
    loss = loss_acc[0, 0] * (0.5 / D_MODEL)

    g = {}

    def into_rows(off, rows_per_chip, shape=pack_shape):
        def view(tm, tn):
            if tm == N_CHIPS * rows_per_chip:
                return pl.BlockSpec((N_CHIPS, rows_per_chip, tn), lambda i, j, k: (0, off // rows_per_chip, 0))
            nb = rows_per_chip // tm
            return pl.BlockSpec((None, tm, tn), lambda i, j, k: (i // nb, off // tm + i % nb, 0))
        return shape, view

    def into_cols(off):
        return pack_shape, lambda tm, tn: pl.BlockSpec((None, tm, tn), lambda i, j, k: (j, off // tm + i, 0))

    def mlp_bwd(pack, dr, drb, hb, pre, layer, swap=False):
        w2_rows = (EARLY_OFF["w_ff2"] + layer * ff_tile, ff_tile)
        w1_rows = (EARLY_OFF["w_ff1"] + layer * D_MODEL, D_MODEL)
        ready = [(w1_rows[0] + w1_rows[1], w2_rows[0] - w1_rows[0] - w1_rows[1]), (w2_rows[0] + w2_rows[1], EARLY_ROWS - w2_rows[0] - w2_rows[1])]
        dpre = mm(drb, w["w_ff2"][layer].reshape(D_FF, D_MODEL), tb=True, epi=lambda r, p: (r * 2.0 * jnp.maximum(p, 0.0),),
                  extras=(pre,), out_dtypes=(BF16,), tiles=(None, ff_tile, None), name=f"ff2_dx_{layer}",
                  ride=SwapRide(pack, ready) if swap else None)
        if swap:
            dpre, (theirs,) = dpre
        pack = mm(pre, drb, ta=True, pro_a=_relu2, name=f"ff2_dw_{layer}", tiles=(ff_tile, PACK_W, None), into=pack,
                  out_view=into_rows(w2_rows[0], ff_tile))
        pack = mm(hb, dpre, ta=True, name=f"ff1_dw_{layer}", tiles=(None, PACK_W, None), into=pack,
                  out_view=into_cols(w1_rows[0]))
        dh = mm(dpre, w["w_ff1"][layer], tb=True, epi=lambda r, d: (r + DN_ALPHA * d,), extras=(dr,), n_dim=D_MODEL,
                tiles=(None, D_MODEL, ff_tile), b_view=_b_cols_t, name=f"ff1_dx_{layer}",
                ride=SwapRide(pack, [w1_rows, w2_rows], into=theirs) if swap else None)
        return (pack, *dh) if swap else (pack, dh)

    pack, dh3 = mlp_bwd(None, dr4, dr4b, h3b, f2pre, 1)
    dr3, dr3b, dg_m1, db_m1 = ln_bwd(h2, mix1, ln("ln_mix_g", 1), dh3, "ln_mix_bwd_1")
    shard_rows = D_MODEL // N_CHIPS
    pack = mm(o, dr3b, ta=True, name="attn_out_dw", tiles=(D_MODEL, PACK_W, None), into=pack,
              out_view=into_rows(EARLY_OFF["attn_w_o"], shard_rows))
    def head_dots(do, o):
        return do, jnp.concatenate([jnp.sum(do[:, V_HEAD * h:V_HEAD * (h + 1)] * o[:, V_HEAD * h:V_HEAD * (h + 1)], axis=1,
                                            keepdims=True) for h in range(N_HEADS)], axis=1)
    do, delta = mm(dr3b, w_o, tb=True, epi=head_dots, extras=(o,), out_dtypes=(F32, (N_HEADS, F32)), name="attn_out_dx")
    tb = min(ATT_TK, seq)
    lse_row = lse.reshape(N_HEADS, seq // tb, tb)
    delta_row = delta.T.reshape(N_HEADS, seq // tb, tb)
    dqn, dqr, dkvb, dkr = attn_bwd(qro, kvb, krope, do, lse_row, delta_row)

    def q_rope_bwd(dn, dr, cs, sn):
        parts = []
        for k in range(N_CHIPS):
            parts.append(dn[:, Q_CHIP_NOPE * k:Q_CHIP_NOPE * (k + 1)])
            parts.append(_rope_tile_bwd(dr[:, LANES * k:LANES * (k + 1)], cs, sn))
        return (jnp.concatenate(parts, axis=1),), ()
    (dqlin,) = rowwise(q_rope_bwd, (dqn, dqr, cos_q, sin_q), ((N_CHIPS * Q_CHIP, BF16),), name="q_rope_bwd")
    g["q_w_b"] = mm(cq, dqlin, ta=True, name="q_b_dw", tiles=(Q_LORA, Q_CHIP, None), out_view=_out_cols(q_w_b.shape))
    dcq_raw, dqn_g = mm(dqlin, q_w_b, tb=True, n_dim=Q_LORA, tiles=(None, Q_LORA, Q_CHIP), b_view=_b_cols_t,
                        epi=lambda d, c, gq: _rms_bwd(c, gq, d), extras=(cq_raw, qn_g), out_dtypes=(BF16,),
                        accs=(Q_LORA,), name="q_b_dx")
    g["q_w_a"] = mm(h2b, dcq_raw, ta=True, name="q_a_dw")
    g["kv_w_b"] = mm(ckv, dkvb, ta=True, name="kv_b_dw", tiles=(KV_LORA, kvb_tile, None), out_view=_out_cols(kv_w_b.shape))
    dkr_sum = head_sum(dkr)

    def kv_post_bwd(dc, kva, gk, dk, cs, sn):
        dx, dgk = _rms_bwd(kva[:, :KV_LORA], gk, dc)
        dk = dk + pltpu.roll(dk, LANES - HALF_ROPE, 1)
        return jnp.concatenate([dx, _rope_tile_bwd(dk, cs, sn)], axis=1), dgk
    dkva, dkvn_g = mm(dkvb, kv_w_b, tb=True, n_dim=KV_LORA, tiles=(None, KV_LORA, kvb_tile), b_view=_b_cols_t,
                      epi=kv_post_bwd, extras=(kva, kvn_g, dkr_sum, cos_k, sin_k), out_dtypes=((KVA_PAD, BF16),),
                      accs=(KV_LORA,), name="kv_b_dx")
    g["kv_w_a"] = mm(h2b, dkva, ta=True, name="kv_a_dw")
    dh2 = mm(dcq_raw, q_w_a, tb=True, epi=lambda r, d: (r + DN_ALPHA * d,), extras=(dr3,), name="q_a_dx")

    def ln_ffn_bwd(r, d, h, f, gl):
        dr, dg, db = _layer_norm_bwd(h, f, gl, r + d)
        return dr, dr, dg, db
    dr2, dr2b, dg_f0, db_f0 = mm(dkva, kv_w_a, tb=True, epi=ln_ffn_bwd, extras=(dh2, h1, f1, ln("ln_ffn_g", 0)),
                                 out_dtypes=(F32, BF16), accs=(D_MODEL, D_MODEL), name="kv_a_dx")
    pack = put_rows(pack, packed_shards(g, MISC_EARLY, EARLY_ROWS - MISC_EARLY_OFF), MISC_EARLY_OFF)
    if comm is None:
        pack, dh1 = mlp_bwd(pack, dr2, dr2b, h1b, f1pre, 0)
    else:
        pack, dh1, (theirs,) = mlp_bwd(pack, dr2, dr2b, h1b, f1pre, 0, swap=True)
        early_sums = add_halves(pack, theirs, comm[1])
    dr1, dr1b, dg_m0, db_m0 = ln_bwd(x, mix0, ln("ln_mix_g", 0), dh1, "ln_mix_bwd_0")
    mid = mm(z, dr1b, ta=True, name="ssm_out_dw", tiles=(D_MODEL, PACK_W, None),
             out_view=into_rows(MID_OFF["ssm_w_out"], shard_rows, (N_CHIPS, MID_ROWS, PACK_W)))
    def glu_bwd(dz, vl, gt):
        sg = _sigmoid(gt)
        return (jnp.concatenate([dz * sg, dz * vl * sg * (1.0 - sg)], axis=1),)
    dvg = mm(dr1b, w_out, tb=True, epi=glu_bwd, extras=(val, gate), out_dtypes=((2 * D_MODEL, BF16),), name="ssm_out_dx")
    g["ssm_w_glu"] = mm(yg, dvg, ta=True, name="glu_proj_dw", tiles=(None, glu_tile, None), out_view=_out_cols(w_glu.shape))
    mid = put_rows(mid, packed_shards(g, MISC_MID, MID_ROWS - MISC_MID_OFF), MISC_MID_OFF)
    dypre = mm(dvg, w_glu, tb=True, epi=lambda r, y: (r * _gelu_grad(y),), extras=(ypre,), n_dim=D_MODEL,
               tiles=(None, D_MODEL, glu_tile), b_view=_b_cols_t, name="glu_proj_dx",
               ride=Together([SwapRide(mid), SendRide([(early_sums, (0, EARLY_HEAD), None)])]) if comm is not None else None)
    sends = None
    if comm is not None:
        dypre, (theirs, early_got) = dypre
        sends = SendRide([(early_sums, (EARLY_HEAD, EARLY_ROWS - EARLY_HEAD), early_got), add_halves(mid, theirs, comm[1])])
    (dx, dbbd_re, dbbd_im, dcbd_re, dcbd_imn, dar, dai, dd), got = s5_bwd(
        dypre, x, dr1, h_re, h_im, bbd_re, bbd_im, cbd_re, cbd_imn, a_re, a_im, dskip, sends)
    dbb_re = _blockdiag_in_t(dbbd_re).reshape(N_STATES, SSM_GROUP)
    dbb_im = _blockdiag_in_t(dbbd_im).reshape(N_STATES, SSM_GROUP)
    dlr, dli, dldt, db_re, db_im = s5_prep_bwd(lr, li, ldt, b_re, b_im, dar.reshape(N_STATES, 1),
                                               dai.reshape(N_STATES, 1), dbb_re, dbb_im)
    g["ssm_lam_re"] = dlr.reshape(1, N_GROUPS, SSM_STATE)
    g["ssm_lam_im"] = dli.reshape(1, N_GROUPS, SSM_STATE)
    g["ssm_log_dt"] = group_sum(dldt).reshape(1, N_GROUPS)
    g["ssm_b_re"] = db_re.reshape(1, N_GROUPS, SSM_STATE, SSM_GROUP)
    g["ssm_b_im"] = db_im.reshape(1, N_GROUPS, SSM_STATE, SSM_GROUP)
    g["ssm_c_re"] = _blockdiag_out_t(dcbd_re).reshape(1, N_GROUPS, SSM_GROUP, SSM_STATE)
    g["ssm_c_im"] = -_blockdiag_out_t(dcbd_imn).reshape(1, N_GROUPS, SSM_GROUP, SSM_STATE)
    g["ssm_d"] = dd
    g["ln_mix_g"] = jnp.concatenate([dg_m0, dg_m1], 0)
    g["ln_mix_b"] = jnp.concatenate([db_m0, db_m1], 0)
    g["ln_ffn_g"] = jnp.concatenate([dg_f0, dg_f1], 0)
    g["ln_ffn_b"] = jnp.concatenate([db_f0, db_f1], 0)
    g["kv_norm_g"] = dkvn_g.reshape(KV_LORA)
    g["q_norm_g"] = dqn_g
    return loss, dx, pack, mid, g, list(zip(sends.ins, got)) if comm is not None else None


def place(shard, me_idx, dtype, name, layer=None):
    rows, cols = shard.shape[-2:]
    tr = _tile(rows, (512, 256, 128))

    def body(m_ref, x_ref, o_ref):
        o_ref[...] = x_ref[...].astype(o_ref.dtype)

    in_spec = (pl.BlockSpec((tr, cols), lambda i, m: (i, 0)) if layer is None
               else pl.BlockSpec((None, tr, cols), lambda i, m: (layer, i, 0)))
    return _pcall(
        body, name=name,
        grid_spec=pltpu.PrefetchScalarGridSpec(
            num_scalar_prefetch=1, grid=(rows // tr,), in_specs=[in_spec],
            out_specs=pl.BlockSpec((None, tr, cols), lambda i, m: (m[0], i, 0))),
        out_shape=jax.ShapeDtypeStruct((N_CHIPS, rows, cols), dtype),
        compiler_params=_params(("parallel",)),
    )(me_idx, shard)


def place_many(shards, dtypes, me_idx, name):
    def body(m_ref, *refs):
        for x_ref, o_ref in zip(refs[:len(shards)], refs[len(shards):]):
            o_ref[...] = x_ref[...].astype(o_ref.dtype)

    return _pcall(
        body, name=name,
        grid_spec=pltpu.PrefetchScalarGridSpec(
            num_scalar_prefetch=1, grid=(1,),
            in_specs=[pl.BlockSpec(s.shape, lambda i, m: (0, 0)) for s in shards],
            out_specs=[pl.BlockSpec((None,) + s.shape, lambda i, m: (m[0], 0, 0)) for s in shards]),
        out_shape=[jax.ShapeDtypeStruct((N_CHIPS,) + s.shape, d) for s, d in zip(shards, dtypes)],
        compiler_params=_params(("arbitrary",)),
    )(me_idx, *shards)


def put_rows(pack, rows, off):
    _, n, cols = rows.shape

    def body(r_ref, p_ref, o_ref, sem):
        cp = pltpu.make_async_copy(r_ref.at[0], o_ref.at[pl.program_id(0), pl.ds(off, n), :], sem)
        cp.start()
        cp.wait()

    return _pcall(body, name="grad_put_rows", grid=(N_CHIPS,),
                  in_specs=[pl.BlockSpec((1, n, cols), lambda k: (k, 0, 0)), _ANY], out_specs=_ANY,
                  out_shape=jax.ShapeDtypeStruct(pack.shape, pack.dtype), input_output_aliases={1: 0},
                  scratch_shapes=[pltpu.SemaphoreType.DMA],
                  compiler_params=_params(("arbitrary",)))(rows, pack)


def _my_cols(c, mine=True):
    start = (c if mine else 1 - c) * HALF_W
    return pl.ds(pl.multiple_of(start, HALF_W), HALF_W)


def add_halves(gpack, got, c_idx):
    n, rows, _ = gpack.shape
    tr = min(G_BLOCK_ROWS, rows)
    blk = (None, tr, HALF_W)

    def body(c_ref, g_ref, r_ref, o_ref):
        o_ref[...] = (g_ref[...] + r_ref[...]).astype(o_ref.dtype)

    return _pcall(
        body, name="grad_add_halves",
        grid_spec=pltpu.PrefetchScalarGridSpec(
            num_scalar_prefetch=1, grid=(n, rows // tr),
            in_specs=[pl.BlockSpec(blk, lambda k, i, c: (k, i, c[0])), pl.BlockSpec(blk, lambda k, i, c: (k, i, 0))],
            out_specs=pl.BlockSpec(blk, lambda k, i, c: (k, i, 0))),
        out_shape=jax.ShapeDtypeStruct((n, rows, HALF_W), BF16),
        compiler_params=_params(("parallel", "parallel")),
    )(c_idx, gpack, got)


def sum_owner(part, got, idx, total_rows, row_off=0, into=None):
    _, rows, _ = part.shape
    tr = math.gcd(math.gcd(rows, row_off), G_BLOCK_ROWS)
    n_into = 0 if into is None else 1

    def body(m_ref, p_ref, g_ref, *rest):
        up = lambda v: v.astype(F32)
        rest[-1][...] = ((up(p_ref[...]) + up(g_ref[0])) + up(g_ref[1])) + up(g_ref[2])

    return _pcall(
        body, name="grad_sum_owner",
        grid_spec=pltpu.PrefetchScalarGridSpec(
            num_scalar_prefetch=1, grid=(rows // tr,),
            in_specs=[pl.BlockSpec((None, tr, HALF_W), lambda i, m: (m[0], i, 0)),
                      pl.BlockSpec((3, tr, HALF_W), lambda i, m: (0, i, 0))] + [_ANY] * n_into,
            out_specs=pl.BlockSpec((tr, HALF_W), lambda i, m: (row_off // tr + i, m[1]))),
        out_shape=jax.ShapeDtypeStruct((total_rows, PACK_W), F32),
        input_output_aliases={3: 0} if n_into else {},
        compiler_params=_params(("parallel",)),
    )(idx, part, got, *([into] if n_into else []))


def join_halves(red):
    def body(in_ref, out_ref, send_sem, recv_sem):
        x, y, c, _ = _place()
        sibling = (x, y, 1 - c)
        mine = out_ref.at[:, _my_cols(c)]
        cp = pltpu.make_async_remote_copy(src_ref=mine, dst_ref=mine, send_sem=send_sem, recv_sem=recv_sem,
                                          device_id=sibling, device_id_type=MESH)
        cp.start()
        cp.wait_send()
        other = out_ref.at[:, _my_cols(c, mine=False)]
        pltpu.make_async_remote_copy(src_ref=other, dst_ref=other, send_sem=send_sem, recv_sem=recv_sem,
                                     device_id=sibling, device_id_type=MESH).wait_recv()

    return _pcall(body, name="grad_join_halves", in_specs=[_ANY], out_specs=_ANY,
                  out_shape=jax.ShapeDtypeStruct(red.shape, red.dtype), input_output_aliases={0: 0},
                  scratch_shapes=[pltpu.SemaphoreType.DMA, pltpu.SemaphoreType.DMA])(red)


def adamw(gsrc, g_off, wt, m, v, name):
    n, cols = wt.shape
    tr = math.gcd(math.gcd(g_off, n), 256) if g_off else math.gcd(n, 256)
    off_blk = g_off // tr
    c1 = 1.0 / (1.0 - ADAM_B1 ** ADAM_STEP)
    c2 = 1.0 / (1.0 - ADAM_B2 ** ADAM_STEP)

    def body(g_ref, w_ref, m_ref, v_ref, go_ref, d_ref, mo_ref, vo_ref):
        gv = g_ref[...]
        mn = ADAM_B1 * m_ref[...] + (1.0 - ADAM_B1) * gv
        vn = ADAM_B2 * v_ref[...] + (1.0 - ADAM_B2) * gv * gv
        go_ref[...] = gv
        mo_ref[...] = mn
        vo_ref[...] = vn
        d_ref[...] = -ADAM_LR * ((mn * c1) / (jnp.sqrt(vn * c2) + ADAM_EPS) + ADAM_WD * w_ref[...])

    blk = pl.BlockSpec((tr, cols), lambda i: (i, 0))
    return _pcall(body, name=name, grid=(n // tr,),
                  in_specs=[pl.BlockSpec((tr, cols), lambda i: (off_blk + i, 0)), blk, blk, blk],
                  out_specs=[blk] * 4, out_shape=[jax.ShapeDtypeStruct((n, cols), F32)] * 4,
                  compiler_params=_params(("parallel",)))(gsrc, wt, m, v)


def _rows8(a):
    return -(-a.size // (8 * PACK_W)) * 8


def _as_rows(a, rows=None):
    flat = a.reshape(-1)
    n = _rows8(a) if rows is None else rows
    return jnp.pad(flat, (0, n * PACK_W - flat.shape[0])).reshape(n, PACK_W)


def local_shards_2d(wl):
    return {"w_ff1": [wl["w_ff1"][0], wl["w_ff1"][1]], "w_ff2": [wl["w_ff2"][0], wl["w_ff2"][1]],
            "ssm_w_glu": wl["ssm_w_glu"], "ssm_w_out": wl["ssm_w_out"], "kv_w_a": _pad_kva_cols(wl["kv_w_a"]),
            "kv_w_b": wl["kv_w_b"], "q_w_a": wl["q_w_a"], "q_w_b": _perm_q_cols(wl["q_w_b"]),
            "attn_w_o": wl["attn_w_o"], "ssm_d": wl["ssm_d"].reshape(2, -1)}


def misc_grad_shard(name, g, k):
    if name == "ssm_d":
        w = D_MODEL // N_CHIPS
        return g[:, w * k:w * (k + 1)]
    if name in ("ssm_w_glu", "kv_w_b"):
        return g[k]
    if name == "q_w_b":
        return _unperm_q_cols(g[k])
    rows = D_MODEL // N_CHIPS
    shard = g[rows * k:rows * (k + 1)]
    return _unpad_kva_cols(shard) if name == "kv_w_a" else shard


def packed_shards(g, names, rows, tail=None):
    blocks = []
    for k in range(N_CHIPS):
        parts = [_as_rows(misc_grad_shard(n, g[n], k), MISC_SHARD_ROWS[n]) for n in names]
        if tail is not None:
            parts.append(tail[k * (tail.shape[0] // N_CHIPS):(k + 1) * (tail.shape[0] // N_CHIPS)])
        blk = jnp.concatenate(parts, axis=0)
        blocks.append(jnp.pad(blk, ((0, rows - blk.shape[0]), (0, 0))))
    return jnp.stack(blocks)


def kernel(x, positions, ln_mix_g, ln_mix_b, ln_ffn_g, ln_ffn_b, w_ff1, w_ff2, ssm_lam_re, ssm_lam_im, ssm_log_dt, ssm_b_re, ssm_b_im, ssm_c_re, ssm_c_im, ssm_d, ssm_w_glu, ssm_w_out, kv_w_a, kv_norm_g, kv_w_b, q_w_a, q_norm_g, q_w_b, attn_w_o, loss_target, m_ln_mix_g, m_ln_mix_b, m_ln_ffn_g, m_ln_ffn_b, m_w_ff1, m_w_ff2, m_ssm_lam_re, m_ssm_lam_im, m_ssm_log_dt, m_ssm_b_re, m_ssm_b_im, m_ssm_c_re, m_ssm_c_im, m_ssm_d, m_ssm_w_glu, m_ssm_w_out, m_kv_w_a, m_kv_norm_g, m_kv_w_b, m_q_w_a, m_q_norm_g, m_q_w_b, m_attn_w_o, v_ln_mix_g, v_ln_mix_b, v_ln_ffn_g, v_ln_ffn_b, v_w_ff1, v_w_ff2, v_ssm_lam_re, v_ssm_lam_im, v_ssm_log_dt, v_ssm_b_re, v_ssm_b_im, v_ssm_c_re, v_ssm_c_im, v_ssm_d, v_ssm_w_glu, v_ssm_w_out, v_kv_w_a, v_kv_norm_g, v_kv_w_b, v_q_w_a, v_q_norm_g, v_q_w_b, v_attn_w_o):
    env = dict(locals())
    wl = {n: env[n] for n in WEIGHTS}
    ml = {n: env["m_" + n] for n in WEIGHTS}
    vl = {n: env["v_" + n] for n in WEIGHTS}
    for n in ("ssm_w_glu", "ssm_w_out", "q_w_a", "q_w_b", "attn_w_o"):
        wl[n], ml[n], vl[n] = wl[n][0], ml[n][0], vl[n][0]

    c_idx = lax.axis_index("c").astype(jnp.int32).reshape(1)
    me_idx = (2 * lax.axis_index("x") + lax.axis_index("y")).astype(jnp.int32).reshape(1)

    local = local_shards_2d(wl)
    stacked = {n: [place(wl[n], me_idx, BF16, f"place_{n}_{l}", layer=l) for l in range(DEPTH)] for n in ("w_ff1", "w_ff2")}
    others = [n for n in SHARDED if n not in stacked]
    stacked.update(zip(others, place_many([local[n] for n in others], [F32 if n == "ssm_d" else BF16 for n in others],
                                          me_idx, "place_others")))
    stacked["ssm_d"] = ride_alone(GatherRide([_halves(stacked["ssm_d"])]), "ssm_d_all_gather")[0].reshape(1, D_MODEL)
    for n in REPLICATED:
        stacked[n] = wl[n]

    loss_part, dx, early, mid, g, sent = device_step(x[0], positions[0], loss_target[0], stacked, comm=(me_idx, c_idx))
    loss = lax.psum(loss_part, ("x", "y", "c"))

    small = jnp.concatenate([_as_rows(g[n]) for n in REPLICATED], axis=0)
    small = jnp.pad(small, ((0, SMALL_ROWS - small.shape[0]), (0, 0)))
    late = packed_shards(g, MISC_LATE, LATE_ROWS, tail=small)
    late_sums = add_halves(late, ride_alone(SwapRide(late), "grad_swap_halves")[0], c_idx)
    sent.append((late_sums, ride_alone(SendRide([late_sums]), "grad_send_to_owners")[0]))
    where = jnp.concatenate([me_idx, c_idx])
    starts = (0, EARLY_ROWS, EARLY_ROWS + MID_ROWS)
    total_rows = EARLY_ROWS + MID_ROWS + LATE_ROWS
    reduced = None
    for (sums, got), off in zip(sent, starts):
        reduced = sum_owner(sums, got, where, total_rows, row_off=off, into=reduced)
    reduced = join_halves(reduced)
    quarter = reduced[starts[2] + SMALL_OFF:starts[2] + SMALL_OFF + SMALL_Q_ROWS]
    small_tot = ride_alone(GatherRide([_halves(place(quarter, me_idx, F32, "place_small_grads"))]),
                           "small_grad_all_gather")[0].reshape(SMALL_ROWS, PACK_W)

    out_g, out_d, out_m, out_v = {}, {}, {}, {}
    direct = {**EARLY_OFF, **{n: starts[1] + o for n, o in MID_OFF.items()}}
    for n, off in direct.items():
        res = adamw(reduced, off, wl[n].reshape(-1, PACK_W), ml[n].reshape(-1, PACK_W), vl[n].reshape(-1, PACK_W),
                    "adamw_" + n)
        out_g[n], out_d[n], out_m[n], out_v[n] = [a.reshape(env[n].shape) for a in res]
    for names, off in ((MISC_EARLY, MISC_EARLY_OFF), (MISC_MID, starts[1] + MISC_MID_OFF), (MISC_LATE, starts[2])):
        pack3 = lambda d: jnp.concatenate([_as_rows(d[n], MISC_SHARD_ROWS[n]) for n in names], axis=0)
        res = adamw(reduced, off, pack3(wl), pack3(ml), pack3(vl), "adamw_packed_" + names[0])
        r0 = 0
        for n in names:
            cnt = math.prod(env[n].shape)
            out_g[n], out_d[n], out_m[n], out_v[n] = [
                a[r0:r0 + MISC_SHARD_ROWS[n]].reshape(-1)[:cnt].reshape(env[n].shape) for a in res]
            r0 += MISC_SHARD_ROWS[n]
    ws = jnp.concatenate([_as_rows(wl[n]) for n in REPLICATED], axis=0)
    ms = jnp.concatenate([_as_rows(ml[n]) for n in REPLICATED], axis=0)
    vs = jnp.concatenate([_as_rows(vl[n]) for n in REPLICATED], axis=0)
    pad = ((0, SMALL_ROWS - ws.shape[0]), (0, 0))
    res = adamw(small_tot, 0, jnp.pad(ws, pad), jnp.pad(ms, pad), jnp.pad(vs, pad), "adamw_replicated")
    row = 0
    for n in REPLICATED:
        cnt = math.prod(env[n].shape)
        nrows = _rows8(env[n])
        out_g[n], out_d[n], out_m[n], out_v[n] = [a[row:row + nrows].reshape(-1)[:cnt].reshape(env[n].shape) for a in res]
        row += nrows

    return (loss, dx[None], *[out_g[n] for n in WEIGHTS], *[out_d[n] for n in WEIGHTS],
            *[out_m[n] for n in WEIGHTS], *[out_v[n] for n in WEIGHTS])
```

```python
import functools
import math

import jax
import jax.numpy as jnp
from jax import lax
from jax.experimental import pallas as pl
from jax.experimental.pallas import tpu as pltpu

F32 = jnp.float32
BF16 = jnp.bfloat16
MESH = pl.DeviceIdType.MESH

D_MODEL = 1024
DEPTH = 2
SSM_GROUP = 16
N_GROUPS = D_MODEL // SSM_GROUP
SSM_STATE = 64
N_STATES = N_GROUPS * SSM_STATE
N_HEADS = 8
QK_NOPE = 128
QK_ROPE = 64
HALF_ROPE = QK_ROPE // 2
V_HEAD = 128
QK_DIM = QK_NOPE + QK_ROPE
Q_LORA = 384
KV_LORA = 256
ROPE_THETA = 10000.0
SM_SCALE = QK_DIM ** -0.5
NEG_INF = -1e30
D_FF = 4 * D_MODEL
DN_ALPHA = (2 * DEPTH) ** 0.25
LN_EPS = 1e-5
RMS_EPS = 1e-6
ADAM_LR = 0.001
ADAM_B1 = 0.9
ADAM_B2 = 0.999
ADAM_EPS = 1e-08
ADAM_WD = 0.01
ADAM_STEP = 10

N_CHIPS = 4
LANES = 128
VMEM_LIMIT = 56 * 1024 * 1024
MM_VMEM_BUDGET = 40 * 1024 * 1024
PACK_W = 1024
KVA_PAD = 384
HALF_W = PACK_W // 2

SHARDED = ("w_ff1", "w_ff2", "ssm_w_glu", "ssm_w_out", "kv_w_a", "kv_w_b", "q_w_a", "q_w_b", "attn_w_o", "ssm_d")
G_BLOCK_ROWS = 960
EARLY_OFF = {"w_ff1": 0, "w_ff2": 2048, "attn_w_o": 4096}
MISC_EARLY = ("kv_w_b", "kv_w_a", "q_w_a", "q_w_b")
MISC_EARLY_OFF = 4352
EARLY_ROWS = 5 * G_BLOCK_ROWS
EARLY_HEAD = G_BLOCK_ROWS
MID_OFF = {"ssm_w_out": 0}
MISC_MID = ("ssm_w_glu",)
MISC_MID_OFF = 256
MID_ROWS = MISC_MID_OFF + 512
MISC_LATE = ("ssm_d",)
SMALL_Q_ROWS = 96
SMALL_ROWS = N_CHIPS * SMALL_Q_ROWS
SMALL_OFF = 16
LATE_ROWS = 192
MISC_SHARD_ROWS = {"ssm_d": 16, "ssm_w_glu": 512, "kv_w_b": 128, "kv_w_a": 80, "q_w_a": 96, "q_w_b": 144}
REPLICATED = ("ln_mix_g", "ln_mix_b", "ln_ffn_g", "ln_ffn_b", "ssm_lam_re", "ssm_lam_im", "ssm_log_dt",
              "ssm_b_re", "ssm_b_im", "ssm_c_re", "ssm_c_im", "kv_norm_g", "q_norm_g")
WEIGHTS = ("ln_mix_g", "ln_mix_b", "ln_ffn_g", "ln_ffn_b", "w_ff1", "w_ff2", "ssm_lam_re", "ssm_lam_im",
           "ssm_log_dt", "ssm_b_re", "ssm_b_im", "ssm_c_re", "ssm_c_im", "ssm_d", "ssm_w_glu", "ssm_w_out",
           "kv_w_a", "kv_norm_g", "kv_w_b", "q_w_a", "q_norm_g", "q_w_b", "attn_w_o")


def _pcall(body, **kw):
    return pl.pallas_call(body, **kw)


def _params(sem=None):
    return pltpu.CompilerParams(dimension_semantics=sem, vmem_limit_bytes=VMEM_LIMIT)


_ANY = pl.BlockSpec(memory_space=pl.ANY)


def _tile(dim, prefs):
    for p in prefs:
        if dim % p == 0:
            return p
    return dim


def _place():
    x, y, c = lax.axis_index("x"), lax.axis_index("y"), lax.axis_index("c")
    return x, y, c, [(1 - x, y), (x, 1 - y), (1 - x, 1 - y)]


def _remote(k, src, dst, to, send_sems, recv_sems):
    return pltpu.make_async_remote_copy(src_ref=src, dst_ref=dst, send_sem=send_sems.at[k], recv_sem=recv_sems.at[k],
                                        device_id=to, device_id_type=MESH)


class GatherRide:
    def __init__(self, arrs):
        self.ins = list(arrs)
        self.out_shapes = [jax.ShapeDtypeStruct(a.shape, a.dtype) for a in arrs]
        self.aliases = {i: i for i in range(len(arrs))}
        self.n_sems = 6 * len(arrs)

    def start(self, ins, outs, send_sems, recv_sems):
        x, y, c, chips = _place()
        me = 2 * x + y
        for a, o in enumerate(outs):
            for j, (px, py) in enumerate(chips):
                _remote(6 * a + j, o.at[me, c], o.at[me, c], (px, py, c), send_sems, recv_sems).start()

    def pass_on(self, ins, outs, send_sems, recv_sems):
        x, y, c, chips = _place()
        for a, o in enumerate(outs):
            for j, (px, py) in enumerate(chips):
                blk = o.at[2 * px + py, c]
                _remote(6 * a + j, blk, blk, (px, py, c), send_sems, recv_sems).wait_recv()
                _remote(6 * a + 3 + j, blk, blk, (x, y, 1 - c), send_sems, recv_sems).start()

    def finish(self, ins, outs, send_sems, recv_sems, passed_on=False):
        if not passed_on:
            self.pass_on(ins, outs, send_sems, recv_sems)
        x, y, c, chips = _place()
        me = 2 * x + y
        sibling = (x, y, 1 - c)
        for a, o in enumerate(outs):
            for j, (px, py) in enumerate(chips):
                blk = o.at[2 * px + py, 1 - c]
                _remote(6 * a + 3 + j, blk, blk, sibling, send_sems, recv_sems).wait_recv()
                _remote(6 * a + j, o.at[me, c], o.at[me, c], (px, py, c), send_sems, recv_sems).wait_send()
                mine = o.at[2 * px + py, c]
                _remote(6 * a + 3 + j, mine, mine, sibling, send_sems, recv_sems).wait_send()


class SendRide:
    base = 0

    def __init__(self, parts):
        parts = [p if isinstance(p, tuple) else (p, (0, p.shape[1]), None) for p in parts]
        self.rows = [rows for _, rows, _ in parts]
        self.n_parts = len(parts)
        self.ins = [p for p, _, _ in parts] + [into for _, _, into in parts if into is not None]
        self.out_shapes = [jax.ShapeDtypeStruct((3,) + p.shape[1:], p.dtype) for p, _, _ in parts]
        given = [a for a, (_, _, into) in enumerate(parts) if into is not None]
        self.aliases = {self.n_parts + i: a for i, a in enumerate(given)}
        self.n_sems = 3 * self.n_parts

    def _copies(self, ins, outs, send_sems, recv_sems):
        x, y, c, chips = _place()
        return [_remote(self.base + 3 * a + j, ins[a].at[2 * px + py, pl.ds(r0, n)], outs[a].at[j, pl.ds(r0, n)],
                        (px, py, c), send_sems, recv_sems)
                for a, (r0, n) in enumerate(self.rows) for j, (px, py) in enumerate(chips)]

    def start(self, ins, outs, send_sems, recv_sems):
        for cp in self._copies(ins, outs, send_sems, recv_sems):
            cp.start()

    def finish(self, ins, outs, send_sems, recv_sems):
        for cp in self._copies(ins, outs, send_sems, recv_sems):
            cp.wait()


class SwapRide:
    base = 0

    def __init__(self, pack, ranges=None, into=None):
        self.ins = [pack] if into is None else [pack, into]
        self.out_shapes = [jax.ShapeDtypeStruct(pack.shape[:2] + (HALF_W,), pack.dtype)]
        self.aliases = {} if into is None else {1: 0}
        self.ranges = ranges or [(0, pack.shape[1])]
        self.n_sems = len(self.ranges)

    def _copies(self, ins, outs, send_sems, recv_sems):
        x, y, c, _ = _place()
        return [_remote(self.base + k, ins[0].at[:, pl.ds(r0, n), _my_cols(c, mine=False)], outs[0].at[:, pl.ds(r0, n), :],
                        (x, y, 1 - c), send_sems, recv_sems) for k, (r0, n) in enumerate(self.ranges)]

    def start(self, ins, outs, send_sems, recv_sems):
        for cp in self._copies(ins, outs, send_sems, recv_sems):
            cp.start()

    def finish(self, ins, outs, send_sems, recv_sems):
        for cp in self._copies(ins, outs, send_sems, recv_sems):
            cp.wait()


class Together:
    def __init__(self, rides):
        self.rides = rides
        self.ins, self.out_shapes, self.aliases, self.n_sems = [], [], {}, 0
        for r in rides:
            r.base = self.n_sems
            self.aliases.update({len(self.ins) + i: len(self.out_shapes) + o for i, o in r.aliases.items()})
            self.ins += r.ins
            self.out_shapes += r.out_shapes
            self.n_sems += r.n_sems

    def _each(self, step, ins, outs, send_sems, recv_sems):
        i = o = 0
        for r in self.rides:
            getattr(r, step)(ins[i:i + len(r.ins)], outs[o:o + len(r.out_shapes)], send_sems, recv_sems)
            i, o = i + len(r.ins), o + len(r.out_shapes)

    def start(self, *refs):
        self._each("start", *refs)

    def finish(self, *refs):
        self._each("finish", *refs)


def _pcall_riding(body, args, ride, first, last, *, in_specs, out_specs, out_shape, scratch_shapes=(), middle=None,
                  in_place=None, **kw):
    n_in, n_out = len(args), len(out_shape)
    in_place = in_place or {}
    if ride is None:
        return _pcall(body, in_specs=in_specs, out_specs=out_specs, out_shape=out_shape,
                      input_output_aliases=in_place, scratch_shapes=list(scratch_shapes), **kw)(*args), []
    k_in, k_out = len(ride.ins), len(ride.out_shapes)

    def riding(*refs):
        ins, r_in = refs[:n_in], refs[n_in:n_in + k_in]
        outs = refs[n_in + k_in:n_in + k_in + n_out]
        r_out = refs[n_in + k_in + n_out:n_in + k_in + n_out + k_out]
        scratch, (send_sems, recv_sems) = refs[n_in + k_in + n_out + k_out:-2], refs[-2:]

        @pl.when(first())
        def _():
            ride.start(r_in, r_out, send_sems, recv_sems)

        if middle is not None:
            @pl.when(middle())
            def _():
                ride.pass_on(r_in, r_out, send_sems, recv_sems)

        body(*ins, *outs, *scratch)

        @pl.when(last())
        def _():
            if middle is not None:
                ride.finish(r_in, r_out, send_sems, recv_sems, passed_on=True)
            else:
                ride.finish(r_in, r_out, send_sems, recv_sems)

    res = _pcall(riding, in_specs=list(in_specs) + [_ANY] * k_in, out_specs=list(out_specs) + [_ANY] * k_out,
                 out_shape=list(out_shape) + ride.out_shapes,
                 input_output_aliases={**in_place, **{n_in + i: n_out + o for i, o in ride.aliases.items()}},
                 scratch_shapes=list(scratch_shapes) + [pltpu.SemaphoreType.DMA((ride.n_sems,))] * 2,
                 **kw)(*args, *ride.ins)
    return res[:n_out], res[n_out:]


def ride_alone(ride, name):
    def body(*refs):
        n = len(ride.ins)
        ins, outs, (send_sems, recv_sems) = refs[:n], refs[n:-2], refs[-2:]
        ride.start(ins, outs, send_sems, recv_sems)
        ride.finish(ins, outs, send_sems, recv_sems)

    return _pcall(body, name=name, in_specs=[_ANY] * len(ride.ins), out_specs=[_ANY] * len(ride.out_shapes),
                  out_shape=ride.out_shapes, input_output_aliases=dict(ride.aliases),
                  scratch_shapes=[pltpu.SemaphoreType.DMA((ride.n_sems,))] * 2)(*ride.ins)


def mm(a, b, *, name, ta=False, tb=False, pro_a=None, epi=None, extras=(), out_dtypes=(F32,), n_dim=None,
       tiles=(None, None, None), b_view=None, out_view=None, into=None, ride=None, accs=()):
    widths = [d[0] if isinstance(d, tuple) else None for d in out_dtypes]
    out_dtypes = [d[1] if isinstance(d, tuple) else d for d in out_dtypes]
    if ta:
        k_dim, m_dim = a.shape
    else:
        m_dim, k_dim = a.shape
    if n_dim is None:
        n_dim = b.shape[0] if tb else b.shape[1]
    tn = tiles[1] or (n_dim if n_dim <= 1024 else _tile(n_dim, (1024, 512, 256, 128)))
    tk = tiles[2] or (k_dim if k_dim <= 1024 else _tile(k_dim, (1024, 512, 256, 128)))
    nk = k_dim // tk

    def vmem_bytes(tm):
        blocks = tm * tk * a.dtype.itemsize + tk * tn * b.dtype.itemsize
        blocks += sum(tm * (tn if e.shape[1] == n_dim else e.shape[1]) * e.dtype.itemsize for e in extras if e.shape[0] > 1)
        blocks += tm * sum((w or tn) * jnp.dtype(d).itemsize for w, d in zip(widths, out_dtypes))
        return 2 * blocks + tm * tn * 4

    tm = tiles[0] or next((t for t in (4096, 2048, 1024, 512, 256) if m_dim % t == 0 and vmem_bytes(t) <= MM_VMEM_BUDGET),
                          _tile(m_dim, (128,)))
    assert m_dim % tm == 0 and n_dim % tn == 0 and k_dim % tk == 0, (name, m_dim, n_dim, k_dim, tm, tn, tk)
    assert tn == n_dim or not (any(widths) or accs), name
    n_ex, n_out = len(extras), len(out_dtypes)
    n_into = 0 if into is None else 1
    dims = (((0 if ta else 1,), (1 if tb else 0,)), ((), ()))

    def body(a_ref, b_ref, *rest):
        ex_refs, out_refs = rest[:n_ex], rest[n_ex + n_into:n_ex + n_into + n_out]
        sum_refs = rest[n_ex + n_into + n_out:n_ex + n_into + n_out + len(accs)]

        def partial():
            av = a_ref[...]
            if pro_a is not None:
                av = pro_a(av)
            return lax.dot_general(av.astype(BF16), b_ref[...].astype(BF16), dims, preferred_element_type=F32)

        def finish(r):
            res = epi(r, *[e[...] for e in ex_refs]) if epi is not None else (r,)
            for o_ref, v in zip(out_refs, res):
                o_ref[...] = v.reshape(o_ref.shape).astype(o_ref.dtype)
            if accs:
                @pl.when(pl.program_id(0) == 0)
                def _():
                    for s_ref in sum_refs:
                        s_ref[...] = jnp.zeros_like(s_ref)

                for s_ref, v in zip(sum_refs, res[n_out:]):
                    s_ref[...] += v

        if nk == 1:
            finish(partial())
            return
        acc = rest[-1]
        k = pl.program_id(2)

        @pl.when(k == 0)
        def _():
            acc[...] = partial()

        @pl.when(k > 0)
        def _():
            acc[...] += partial()

        @pl.when(k == nk - 1)
        def _():
            finish(acc[...])

    def ex_spec(e):
        if e.shape == (m_dim, n_dim):
            return o_spec
        if e.shape[0] == m_dim:
            return pl.BlockSpec((tm, e.shape[1]), lambda i, j, k: (i, 0))
        return pl.BlockSpec(e.shape, lambda i, j, k: (0, 0))

    a_spec = pl.BlockSpec((tk, tm), lambda i, j, k: (k, i)) if ta else pl.BlockSpec((tm, tk), lambda i, j, k: (i, k))
    if b_view is not None:
        b_spec = b_view(tk, tn)
    else:
        b_spec = pl.BlockSpec((tn, tk), lambda i, j, k: (j, k)) if tb else pl.BlockSpec((tk, tn), lambda i, j, k: (k, j))
    o_spec = pl.BlockSpec((tm, tn), lambda i, j, k: (i, j))
    if out_view is None:
        out_specs = [o_spec if w is None else pl.BlockSpec((tm, w), lambda i, j, k: (i, 0)) for w in widths]
        out_shape = [jax.ShapeDtypeStruct((m_dim, w or n_dim), dt) for w, dt in zip(widths, out_dtypes)]
    else:
        assert n_out == 1
        out_specs = [out_view[1](tm, tn)]
        out_shape = [jax.ShapeDtypeStruct(out_view[0], out_dtypes[0])]
    out_specs = out_specs + [pl.BlockSpec((1, w), lambda i, j, k: (0, 0)) for w in accs]
    out_shape = out_shape + [jax.ShapeDtypeStruct((1, w), F32) for w in accs]
    grid = (m_dim // tm, n_dim // tn, nk)
    scratch = [pltpu.VMEM((tm, tn), F32)] if nk > 1 else []
    if ride is not None:
        assert into is None
        at = lambda ids: functools.reduce(jnp.logical_and, [pl.program_id(d) == i for d, i in enumerate(ids)])
        outs, landed = _pcall_riding(
            body, (a, b, *extras), ride, lambda: at((0, 0, 0)), lambda: at([g - 1 for g in grid]),
            name=name, grid=grid, in_specs=[a_spec, b_spec] + [ex_spec(e) for e in extras], out_specs=out_specs,
            out_shape=out_shape, scratch_shapes=scratch, compiler_params=_params(("arbitrary",) * 3))
        return (outs[0] if len(outs) == 1 else outs), landed
    outs = _pcall(
        body, name=name, grid=grid,
        in_specs=[a_spec, b_spec] + [ex_spec(e) for e in extras] + [_ANY] * n_into,
        out_specs=out_specs, out_shape=out_shape,
        input_output_aliases={2 + n_ex: 0} if n_into else {},
        scratch_shapes=scratch,
        compiler_params=_params(("arbitrary",) * 3 if accs else ("parallel", "parallel", "arbitrary")),
    )(a, b, *extras, *([into] if n_into else []))
    return outs[0] if len(outs) == 1 else outs


def rowwise(fn, ins, outs, *, name, accs=(), tm=256):
    rows = ins[0].shape[0]
    tm = min(tm, rows)
    n_in, n_out, n_acc = len(ins), len(outs), len(accs)

    def body(*refs):
        in_refs, out_refs, acc_refs = refs[:n_in], refs[n_in:n_in + n_out], refs[n_in + n_out:]
        res, sums = fn(*[r[...] for r in in_refs])
        for o_ref, v in zip(out_refs, res):
            o_ref[...] = v.astype(o_ref.dtype)
        if n_acc:
            @pl.when(pl.program_id(0) == 0)
            def _():
                for a_ref in acc_refs:
                    a_ref[...] = jnp.zeros_like(a_ref)

            for a_ref, s in zip(acc_refs, sums):
                a_ref[...] += s

    def spec(arr):
        if arr.shape[0] == rows:
            return pl.BlockSpec((tm, arr.shape[1]), lambda i: (i, 0))
        return pl.BlockSpec(arr.shape, lambda i: (0, 0))

    res = _pcall(
        body, name=name, grid=(rows // tm,),
        in_specs=[spec(a) for a in ins],
        out_specs=[pl.BlockSpec((tm, w), lambda i: (i, 0)) for w, _ in outs]
        + [pl.BlockSpec((1, w), lambda i: (0, 0)) for w in accs],
        out_shape=[jax.ShapeDtypeStruct((rows, w), dt) for w, dt in outs]
        + [jax.ShapeDtypeStruct((1, w), F32) for w in accs],
        compiler_params=_params(("arbitrary",) if n_acc else ("parallel",)),
    )(*ins)
    return res


def _relu2(v):
    r = jnp.maximum(v, 0.0)
    return r * r


def _gelu(x):
    c = math.sqrt(2.0 / math.pi)
    return 0.5 * x * (1.0 + jnp.tanh(c * (x + 0.044715 * x * x * x)))


def _gelu_grad(x):
    c = math.sqrt(2.0 / math.pi)
    t = jnp.tanh(c * (x + 0.044715 * x * x * x))
    return 0.5 * (1.0 + t) + 0.5 * x * (1.0 - t * t) * c * (1.0 + 3 * 0.044715 * x * x)


def _sigmoid(x):
    return 1.0 / (1.0 + jnp.exp(-x))


def _layer_norm(h, mix, g, b):
    r = DN_ALPHA * h + mix
    mu = jnp.mean(r, axis=-1, keepdims=True)
    xc = r - mu
    var = jnp.mean(xc * xc, axis=-1, keepdims=True)
    return xc * lax.rsqrt(var + LN_EPS) * g + b


def _layer_norm_bwd(h, mix, g, dy):
    r = DN_ALPHA * h + mix
    mu = jnp.mean(r, axis=-1, keepdims=True)
    xc = r - mu
    var = jnp.mean(xc * xc, axis=-1, keepdims=True)
    rstd = lax.rsqrt(var + LN_EPS)
    xhat = xc * rstd
    dxh = dy * g
    m1 = jnp.mean(dxh, axis=-1, keepdims=True)
    m2 = jnp.mean(dxh * xhat, axis=-1, keepdims=True)
    dr = rstd * (dxh - m1 - xhat * m2)
    return dr, jnp.sum(dy * xhat, axis=0, keepdims=True), jnp.sum(dy, axis=0, keepdims=True)


def ln_bwd(h, mix, g, dy, name):
    def fn(h, mix, g, dy):
        dr, dg, db = _layer_norm_bwd(h, mix, g, dy)
        return (dr, dr), (dg, db)
    return rowwise(fn, (h, mix, g, dy), ((D_MODEL, F32), (D_MODEL, BF16)), accs=(D_MODEL, D_MODEL), name=name)


def _rms(x, g):
    r = lax.rsqrt(jnp.mean(x * x, axis=-1, keepdims=True) + RMS_EPS)
    return x * r * g


def _rms_bwd(x, g, dy):
    r = lax.rsqrt(jnp.mean(x * x, axis=-1, keepdims=True) + RMS_EPS)
    xn = x * r
    dyg = dy * g
    dx = r * (dyg - xn * jnp.mean(dyg * xn, axis=-1, keepdims=True))
    return dx, jnp.sum(dy * xn, axis=0, keepdims=True)


def _s5_disc(lr, li, ldt):
    dt = jnp.exp(ldt)
    mag = jnp.exp(lr * dt)
    cs, sn = jnp.cos(li * dt), jnp.sin(li * dt)
    ar, ai = mag * cs, mag * sn
    inv = 1.0 / (lr * lr + li * li)
    n_re = (ar - 1.0) * lr + ai * li
    n_im = ai * lr - (ar - 1.0) * li
    return dt, mag, cs, sn, ar, ai, inv, n_re, n_im


def s5_prep(lr, li, ldt, b_re, b_im):
    def fn(lr, li, ldt, b_re, b_im):
        _, _, _, _, ar, ai, inv, n_re, n_im = _s5_disc(lr, li, ldt)
        cr, ci = n_re * inv, n_im * inv
        return (ar, ai, cr * b_re - ci * b_im, cr * b_im + ci * b_re), ()
    return rowwise(fn, (lr, li, ldt, b_re, b_im), ((1, F32), (1, F32), (SSM_GROUP, F32), (SSM_GROUP, F32)),
                   name="s5_prep", tm=512)


def s5_prep_bwd(lr, li, ldt, b_re, b_im, dar, dai, dbb_re, dbb_im):
    def fn(lr, li, ldt, b_re, b_im, dar, dai, dbb_re, dbb_im):
        dt, mag, cs, sn, ar, ai, inv, n_re, n_im = _s5_disc(lr, li, ldt)
        cr, ci = n_re * inv, n_im * inv
        db_re = cr * dbb_re + ci * dbb_im
        db_im = cr * dbb_im - ci * dbb_re
        dcr = jnp.sum(dbb_re * b_re + dbb_im * b_im, axis=-1, keepdims=True)
        dci = jnp.sum(dbb_im * b_re - dbb_re * b_im, axis=-1, keepdims=True)
        dar = dar + (dcr * lr - dci * li) * inv
        dai = dai + (dcr * li + dci * lr) * inv
        dinv = dcr * n_re + dci * n_im
        dlr = (dcr * (ar - 1.0) + dci * ai) * inv - 2.0 * lr * inv * inv * dinv
        dli = (dcr * ai - dci * (ar - 1.0)) * inv - 2.0 * li * inv * inv * dinv
        dmag = dar * cs + dai * sn
        dth = dai * ar - dar * ai
        dlr = dlr + dmag * mag * dt
        dli = dli + dth * dt
        ddt = dmag * mag * lr + dth * li
        return (dlr, dli, ddt * dt, db_re, db_im), ()
    return rowwise(fn, (lr, li, ldt, b_re, b_im, dar, dai, dbb_re, dbb_im),
                   ((1, F32), (1, F32), (1, F32), (SSM_GROUP, F32), (SSM_GROUP, F32)), name="s5_prep_bwd", tm=512)


def group_sum(x):
    def body(x_ref, o_ref):
        o_ref[...] = jnp.sum(x_ref[...], axis=1)
    return _pcall(body, name="s5_group_sum", out_shape=jax.ShapeDtypeStruct((N_GROUPS, 1), F32))(
        x.reshape(N_GROUPS, SSM_STATE, 1))


GROUPS_PER_TILE = LANES // SSM_GROUP
TILE_STATES = GROUPS_PER_TILE * SSM_STATE
N_UTILES = D_MODEL // LANES


SUBLANES = 8
SCAN_STRIP = 1024
N_STRIPS = N_STATES // SCAN_STRIP
_NT = (((1,), (1,)), ((), ()))
_TN = (((0,), (0,)), ((), ()))


def _scan_coefs(are, aim, shifted, reverse):
    ar = are[...]
    ai = -aim[...] if reverse else aim[...]
    powers = {1: (ar, ai)}
    for d in (2, 4):
        r, i = powers[d // 2]
        powers[d] = (r * r - i * i, 2.0 * r * i)
    rid = lax.broadcasted_iota(jnp.int32, (SUBLANES, N_STATES), 0)
    first = (rid == SUBLANES - 1) if reverse else (rid == 0)
    masks = [(1, first)] + [(d, (rid <= SUBLANES - 1 - d) if reverse else (rid >= d)) for d in (1, 2, 4)]
    for n, (d, keep) in enumerate(masks):
        for part in (0, 1):
            shifted[2 * n + part][...] = jnp.where(keep, jnp.broadcast_to(powers[d][part], (SUBLANES, N_STATES)), 0.0)


def _tile_scan(xr, xi, shifted, nbr_re, nbr_im, reverse):
    for n, d in enumerate((1, 1, 2, 4)):
        by = SUBLANES - d if reverse else d
        fr, fi = (nbr_re, nbr_im) if n == 0 else (xr, xi)
        sr, si = pltpu.roll(fr, by, 0), pltpu.roll(fi, by, 0)
        kr, ki = shifted[2 * n], shifted[2 * n + 1]
        xr, xi = xr + kr * sr - ki * si, xi + kr * si + ki * sr
    return xr, xi


def _tile_rows(t):
    return pl.ds(pl.multiple_of(t * SUBLANES, SUBLANES), SUBLANES)


def s5_fwd(u, bbd_re, bbd_im, cbd_re, cbd_imn, a_re, a_im, dskip, ride=None, t_rows=256):
    seq = u.shape[0]
    t_rows = min(t_rows, seq)
    n_tiles = t_rows // SUBLANES

    def body(u_ref, bre, bim, cre, cimn, are, aim, d_ref, y_ref, gelu_ref, hre_ref, him_ref, car_re, car_im, *shifted):
        @pl.when(pl.program_id(0) == 0)
        def _():
            car_re[...] = jnp.zeros_like(car_re)
            car_im[...] = jnp.zeros_like(car_im)
            _scan_coefs(are, aim, shifted, reverse=False)

        uf = u_ref[...]
        ub = uf.astype(BF16)
        for j in range(N_UTILES):
            uj = ub[:, LANES * j:LANES * (j + 1)]
            sl = slice(TILE_STATES * j, TILE_STATES * (j + 1))
            hre_ref[:, sl] = jnp.dot(uj, bre[j], preferred_element_type=F32)
            him_ref[:, sl] = jnp.dot(uj, bim[j], preferred_element_type=F32)
        for s in range(N_STRIPS):
            cols = pl.ds(s * SCAN_STRIP, SCAN_STRIP)
            coefs = [c[:, cols] for c in shifted]

            def step(t, before):
                rows = _tile_rows(t)
                hr, hi = _tile_scan(hre_ref[rows, cols], him_ref[rows, cols], coefs, before[0], before[1], False)
                hre_ref[rows, cols] = hr
                him_ref[rows, cols] = hi
                return hr, hi

            cr, ci = lax.fori_loop(0, n_tiles, step, (car_re[:, cols], car_im[:, cols]))
            car_re[:, cols] = cr
            car_im[:, cols] = ci
        dv = d_ref[...]
        for j in range(N_UTILES):
            st = slice(TILE_STATES * j, TILE_STATES * (j + 1))
            yj = (jnp.dot(hre_ref[:, st].astype(BF16), cre[j], preferred_element_type=F32)
                  + jnp.dot(him_ref[:, st].astype(BF16), cimn[j], preferred_element_type=F32))
            sl = slice(LANES * j, LANES * (j + 1))
            yj = yj + dv[:, sl] * uf[:, sl]
            y_ref[:, sl] = yj
            gelu_ref[:, sl] = _gelu(yj).astype(gelu_ref.dtype)

    full3 = lambda a: pl.BlockSpec(a.shape, lambda i: (0, 0, 0))
    full2 = lambda a: pl.BlockSpec(a.shape, lambda i: (0, 0))
    tile = pltpu.VMEM((SUBLANES, N_STATES), F32)
    n_chunks = seq // t_rows
    return _pcall_riding(
        body, (u, bbd_re, bbd_im, cbd_re, cbd_imn, a_re, a_im, dskip), ride,
        lambda: pl.program_id(0) == 0, lambda: pl.program_id(0) == n_chunks - 1,
        middle=(lambda: pl.program_id(0) == (7 * n_chunks) // 8) if ride is not None else None,
        name="s5_fwd", grid=(n_chunks,),
        in_specs=[pl.BlockSpec((t_rows, D_MODEL), lambda i: (i, 0)), full3(bbd_re), full3(bbd_im), full3(cbd_re),
                  full3(cbd_imn), full2(a_re), full2(a_im), full2(dskip)],
        out_specs=[pl.BlockSpec((t_rows, D_MODEL), lambda i: (i, 0)),
                   pl.BlockSpec((t_rows, D_MODEL), lambda i: (i, 0)),
                   pl.BlockSpec((t_rows, N_STATES), lambda i: (i, 0)),
                   pl.BlockSpec((t_rows, N_STATES), lambda i: (i, 0))],
        out_shape=[jax.ShapeDtypeStruct((seq, D_MODEL), F32),
                   jax.ShapeDtypeStruct((seq, D_MODEL), BF16),
                   jax.ShapeDtypeStruct((seq, N_STATES), F32),
                   jax.ShapeDtypeStruct((seq, N_STATES), F32)],
        scratch_shapes=[tile] * 10,
        compiler_params=_params(("arbitrary",)))


def s5_bwd(dy, u, dres, h_re, h_im, bbd_re, bbd_im, cbd_re, cbd_imn, a_re, a_im, dskip, ride=None, t_rows=256):
    seq = u.shape[0]
    t_rows = min(t_rows, seq)
    n_chunks = seq // t_rows

    n_tiles = t_rows // SUBLANES

    def body(dy_ref, u_ref, dres_ref, hre_ref, him_ref, hpre_ref, hpim_ref, bre, bim, cre, cimn, are, aim, d_ref,
             dx_ref, dbre, dbim, dcre, dcimn, dar_ref, dai_ref, dd_ref, lre, lim, car_re, car_im, acc_re, acc_im,
             *shifted):
        i = pl.program_id(0)

        @pl.when(i == 0)
        def _():
            for r in (car_re, car_im, acc_re, acc_im, dbre, dbim, dcre, dcimn, dd_ref):
                r[...] = jnp.zeros_like(r)
            _scan_coefs(are, aim, shifted, reverse=True)

        dyf = dy_ref[...]
        dyb = dyf.astype(BF16)
        uf = u_ref[...]
        ub = uf.astype(BF16)
        for j in range(N_UTILES):
            dyj = dyb[:, LANES * j:LANES * (j + 1)]
            st = slice(TILE_STATES * j, TILE_STATES * (j + 1))
            lre[:, st] = lax.dot_general(dyj, cre[j], _NT, preferred_element_type=F32)
            lim[:, st] = lax.dot_general(dyj, cimn[j], _NT, preferred_element_type=F32)
        has_pred = (i < n_chunks - 1).astype(F32)
        last_row = lax.broadcasted_iota(jnp.int32, (SUBLANES, SCAN_STRIP), 0) == SUBLANES - 1
        for s in range(N_STRIPS):
            cols = pl.ds(s * SCAN_STRIP, SCAN_STRIP)
            coefs = [c[:, cols] for c in shifted]
            before_re, before_im = hpre_ref[:, cols] * has_pred, hpim_ref[:, cols] * has_pred

            def step(k, carry):
                after_re, after_im, dar, dai = carry
                t = n_tiles - 1 - k
                rows = _tile_rows(t)
                lr, li = _tile_scan(lre[rows, cols], lim[rows, cols], coefs, after_re, after_im, True)
                lre[rows, cols] = lr
                lim[rows, cols] = li
                prev = _tile_rows(jnp.maximum(t - 1, 0))
                pre_re = jnp.where(t == 0, before_re, hre_ref[prev, cols])
                pre_im = jnp.where(t == 0, before_im, him_ref[prev, cols])
                hpr = pltpu.roll(jnp.where(last_row, pre_re, hre_ref[rows, cols]), 1, 0)
                hpi = pltpu.roll(jnp.where(last_row, pre_im, him_ref[rows, cols]), 1, 0)
                return lr, li, dar + lr * hpr + li * hpi, dai + li * hpr - lr * hpi

            cr, ci, dar, dai = lax.fori_loop(0, n_tiles, step, (car_re[:, cols], car_im[:, cols],
                                                               acc_re[:, cols], acc_im[:, cols]))
            car_re[:, cols] = cr
            car_im[:, cols] = ci
            acc_re[:, cols] = dar
            acc_im[:, cols] = dai

        dv = d_ref[...]
        for j in range(N_UTILES):
            sl = slice(LANES * j, LANES * (j + 1))
            st = slice(TILE_STATES * j, TILE_STATES * (j + 1))
            lrj = lre[:, st].astype(BF16)
            lij = lim[:, st].astype(BF16)
            du = (lax.dot_general(lrj, bre[j], _NT, preferred_element_type=F32)
                  + lax.dot_general(lij, bim[j], _NT, preferred_element_type=F32))
            dx_ref[:, sl] = du + dv[:, sl] * dyf[:, sl] + DN_ALPHA * dres_ref[:, sl]
            uj = ub[:, sl]
            dbre[j] += lax.dot_general(uj, lrj, _TN, preferred_element_type=F32)
            dbim[j] += lax.dot_general(uj, lij, _TN, preferred_element_type=F32)
            dyj = dyb[:, sl]
            dcre[j] += lax.dot_general(hre_ref[:, st].astype(BF16), dyj, _TN, preferred_element_type=F32)
            dcimn[j] += lax.dot_general(him_ref[:, st].astype(BF16), dyj, _TN, preferred_element_type=F32)
        dd_ref[...] += jnp.sum(dyf * uf, axis=0, keepdims=True)

        @pl.when(i == n_chunks - 1)
        def _():
            dar_ref[...] = jnp.sum(acc_re[...], axis=0, keepdims=True)
            dai_ref[...] = jnp.sum(acc_im[...], axis=0, keepdims=True)

    rev = lambda i: (n_chunks - 1 - i, 0)
    prev_tile = lambda i: (jnp.maximum((n_chunks - 1 - i) * n_tiles - 1, 0), 0)
    once = pl.Buffered(1)
    full3 = lambda a: pl.BlockSpec(a.shape, lambda i: (0, 0, 0), pipeline_mode=once)
    full2 = lambda a: pl.BlockSpec(a.shape, lambda i: (0, 0), pipeline_mode=once)
    acc3 = lambda shape: pl.BlockSpec(shape, lambda i: (0, 0, 0))
    acc2 = lambda shape: pl.BlockSpec(shape, lambda i: (0, 0))
    tile = pltpu.VMEM((SUBLANES, N_STATES), F32)
    return _pcall_riding(
        body, (dy, u, dres, h_re, h_im, h_re, h_im, bbd_re, bbd_im, cbd_re, cbd_imn, a_re, a_im, dskip), ride,
        lambda: pl.program_id(0) == 0, lambda: pl.program_id(0) == n_chunks - 1,
        name="s5_bwd", grid=(n_chunks,),
        in_specs=[pl.BlockSpec((t_rows, D_MODEL), rev), pl.BlockSpec((t_rows, D_MODEL), rev),
                  pl.BlockSpec((t_rows, D_MODEL), rev),
                  pl.BlockSpec((t_rows, N_STATES), rev), pl.BlockSpec((t_rows, N_STATES), rev),
                  pl.BlockSpec((SUBLANES, N_STATES), prev_tile), pl.BlockSpec((SUBLANES, N_STATES), prev_tile),
                  full3(bbd_re), full3(bbd_im), full3(cbd_re), full3(cbd_imn), full2(a_re), full2(a_im), full2(dskip)],
        out_specs=[pl.BlockSpec((t_rows, D_MODEL), rev), acc3(bbd_re.shape), acc3(bbd_im.shape), acc3(cbd_re.shape),
                   acc3(cbd_imn.shape), acc2((1, N_STATES)), acc2((1, N_STATES)), acc2((1, D_MODEL))],
        out_shape=[jax.ShapeDtypeStruct((seq, D_MODEL), F32), jax.ShapeDtypeStruct(bbd_re.shape, F32),
                   jax.ShapeDtypeStruct(bbd_im.shape, F32), jax.ShapeDtypeStruct(cbd_re.shape, F32),
                   jax.ShapeDtypeStruct(cbd_imn.shape, F32), jax.ShapeDtypeStruct((1, N_STATES), F32),
                   jax.ShapeDtypeStruct((1, N_STATES), F32), jax.ShapeDtypeStruct((1, D_MODEL), F32)],
        scratch_shapes=[pltpu.VMEM((t_rows, N_STATES), F32), pltpu.VMEM((t_rows, N_STATES), F32)] + [tile] * 12,
        in_place={2: 0},
        compiler_params=_params(("arbitrary",)))


def _eye_groups():
    return jnp.eye(GROUPS_PER_TILE, dtype=F32)


def _blockdiag_in(bb):
    t = bb.transpose(0, 2, 1).reshape(N_UTILES, GROUPS_PER_TILE, SSM_GROUP, SSM_STATE)
    bd = jnp.einsum("jgcp,gh->jgchp", t, _eye_groups())
    return bd.reshape(N_UTILES, LANES, TILE_STATES)


def _blockdiag_in_t(d):
    t = jnp.einsum("jgchp,gh->jgcp", d.reshape(N_UTILES, GROUPS_PER_TILE, SSM_GROUP, GROUPS_PER_TILE, SSM_STATE),
                   _eye_groups())
    return t.reshape(N_GROUPS, SSM_GROUP, SSM_STATE).transpose(0, 2, 1)


def _blockdiag_out(c):
    t = c.transpose(0, 2, 1).reshape(N_UTILES, GROUPS_PER_TILE, SSM_STATE, SSM_GROUP)
    bd = jnp.einsum("jhpc,hg->jhpgc", t, _eye_groups())
    return bd.reshape(N_UTILES, TILE_STATES, LANES)


def _blockdiag_out_t(d):
    t = jnp.einsum("jhpgc,hg->jhpc", d.reshape(N_UTILES, GROUPS_PER_TILE, SSM_STATE, GROUPS_PER_TILE, SSM_GROUP),
                   _eye_groups())
    return t.reshape(N_GROUPS, SSM_STATE, SSM_GROUP).transpose(0, 2, 1)


ATT_TQ = 512
ATT_TK = 512
LOG2E = math.log2(math.e)
LN2 = math.log(2.0)
Q_PRESCALE = SM_SCALE * LOG2E


def _loop_in_pairs(n, step, carry, start=0):
    pairs = (n - start) // 2

    def two(t, c):
        return step(start + 2 * t + 1, step(start + 2 * t, c))

    carry = lax.fori_loop(0, pairs, two, carry)
    return lax.fori_loop(start + 2 * pairs, n, step, carry)


def _causal(s, transposed=False):
    r = lax.broadcasted_iota(jnp.int32, s.shape, 0)
    c = lax.broadcasted_iota(jnp.int32, s.shape, 1)
    return jnp.where((r <= c) if transposed else (c <= r), s, NEG_INF)


def _q_specs(rows, at):
    def nope(*ids):
        r, h = at(*ids)
        return r, 3 * (h // HEADS_PER_CHIP) + h % HEADS_PER_CHIP

    def rope(*ids):
        r, h = at(*ids)
        return r, 3 * (h // HEADS_PER_CHIP) + HEADS_PER_CHIP

    return [pl.BlockSpec((rows, LANES), nope), pl.BlockSpec((rows, LANES), rope)]


def _kv_specs(rows, at):
    def col(f):
        def index(*ids):
            r, h = at(*ids)
            return r, f(h)
        return index

    return [pl.BlockSpec((rows, LANES), col(lambda h: 2 * h)), pl.BlockSpec((rows, LANES), col(lambda h: h % HEADS_PER_CHIP)),
            pl.BlockSpec((rows, LANES), col(lambda h: 2 * h + 1))]


def _cat(a, b):
    return jnp.concatenate([a, b], axis=1)


def attn_fwd(q, kv, kr, ride=None, tq=ATT_TQ, tk=ATT_TK):
    seq = q.shape[0]
    n_heads = N_HEADS
    tq, tk = min(tq, seq), min(tk, seq)
    assert tq == tk

    def body(qn_ref, qr_ref, kn_ref, kr_ref, v_ref, o_ref, lse_ref):
        qi = pl.program_id(1)
        qv = _cat(qn_ref[...], qr_ref[...])
        jd = qi

        def block(j, carry, diag):
            m, l, acc = carry
            rows = pl.ds(pl.multiple_of(j * tk, tk), tk)
            s = lax.dot_general(qv, _cat(kn_ref[rows, :], kr_ref[rows, :]), _NT, preferred_element_type=F32)
            if diag:
                s = _causal(s)
            m_new = jnp.maximum(m, jnp.max(s, axis=-1, keepdims=True))
            p = jnp.exp2(s - m_new)
            corr = jnp.exp2(m - m_new)
            l = l * corr + jnp.sum(p, axis=-1, keepdims=True)
            acc = acc * corr + jnp.dot(p.astype(BF16), v_ref[rows, :], preferred_element_type=F32)
            return m_new, l, acc

        init = (jnp.full((tq, 1), NEG_INF, F32), jnp.zeros((tq, 1), F32), jnp.zeros((tq, V_HEAD), F32))
        carry = _loop_in_pairs(jd, lambda j, c: block(j, c, False), init)
        m, l, acc = block(jd, carry, True)
        o_ref[...] = acc / l
        lse_ref[...] = jnp.transpose(jnp.broadcast_to(m + jnp.log2(l), (tq, LANES)))[:1, :]

    n_q = seq // tq
    return _pcall_riding(
        body, (q, q, kv, kr, kv), ride,
        lambda: (pl.program_id(0) == 0) & (pl.program_id(1) == 0),
        lambda: (pl.program_id(0) == n_heads - 1) & (pl.program_id(1) == n_q - 1),
        middle=(lambda: (pl.program_id(0) == (5 * n_heads) // 8) & (pl.program_id(1) == 0)) if ride is not None else None,
        name="attn_fwd", grid=(n_heads, n_q),
        in_specs=_q_specs(tq, lambda h, i: (i, h)) + _kv_specs(seq, lambda h, i: (0, h)),
        out_specs=[pl.BlockSpec((tq, V_HEAD), lambda h, i: (i, h)),
                   pl.BlockSpec((None, None, 1, tq), lambda h, i: (h, i, 0, 0))],
        out_shape=[jax.ShapeDtypeStruct((seq, n_heads * V_HEAD), F32),
                   jax.ShapeDtypeStruct((n_heads, n_q, 1, tq), F32)],
        compiler_params=_params(("arbitrary", "arbitrary")))


def attn_bwd(q, kv, kr, do, lse_row, delta_row, tq=ATT_TK):
    seq = q.shape[0]
    tq = min(tq, seq)
    n_blk = seq // tq

    def body(qn_ref, qr_ref, kn_ref, kr_ref, v_ref, do_ref, lse_ref, delta_ref, dqn_ref, dqr_ref, dkv_ref, dkr_ref, dq_acc):
        head, kj = pl.program_id(0), pl.program_id(1)

        @pl.when(kj == 0)
        def _():
            dq_acc[...] = jnp.zeros_like(dq_acc)

        kc = _cat(kn_ref[...], kr_ref[...])
        vv = v_ref[...]

        def block(i, carry, diag):
            dk, dv = carry
            rows = pl.ds(pl.multiple_of(i * tq, tq), tq)
            qv = _cat(qn_ref[rows, :], qr_ref[rows, :])
            st = lax.dot_general(kc, qv, _NT, preferred_element_type=F32)
            if diag:
                st = _causal(st, transposed=True)
            pt = jnp.exp2(st - lse_ref[0, pl.ds(i, 1), :])
            dob = do_ref[rows, :].astype(BF16)
            dv = dv + jnp.dot(pt.astype(BF16), dob, preferred_element_type=F32)
            dpt = lax.dot_general(vv, dob, _NT, preferred_element_type=F32)
            dst = (pt * (dpt - delta_ref[0, pl.ds(i, 1), :])).astype(BF16)
            dk = dk + jnp.dot(dst, qv, preferred_element_type=F32)
            dq_acc[rows, :] += lax.dot_general(dst, kc, _TN, preferred_element_type=F32)
            return dk, dv

        carry = block(kj, (jnp.zeros((tq, 2 * LANES), F32), jnp.zeros((tq, V_HEAD), F32)), True)
        dk, dv = _loop_in_pairs(n_blk, lambda i, c: block(i, c, False), carry, start=kj + 1)
        dk = dk * LN2
        dkv_ref[...] = _cat(dk[:, :LANES], dv).astype(dkv_ref.dtype)
        lane = lax.broadcasted_iota(jnp.int32, (tq, LANES), 1)
        mine = (lane // HALF_ROPE) % HEADS_PER_CHIP == head % HEADS_PER_CHIP
        dkr_ref[0] = jnp.where(mine, dk[:, LANES:], 0.0)

        @pl.when(kj == n_blk - 1)
        def _():
            dqn_ref[...] = dq_acc[:, :LANES] * SM_SCALE

        @pl.when((kj == n_blk - 1) & (head % HEADS_PER_CHIP == 0))
        def _():
            dqr_ref[...] = dq_acc[:, LANES:] * SM_SCALE

        @pl.when((kj == n_blk - 1) & (head % HEADS_PER_CHIP > 0))
        def _():
            dqr_ref[...] += dq_acc[:, LANES:] * SM_SCALE

    return _pcall(
        body, name="attn_bwd", grid=(N_HEADS, n_blk),
        in_specs=_q_specs(seq, lambda h, j: (0, h)) + _kv_specs(tq, lambda h, j: (j, h))
        + [pl.BlockSpec((seq, V_HEAD), lambda h, j: (0, h)),
           pl.BlockSpec((1, n_blk, tq), lambda h, j: (h, 0, 0)),
           pl.BlockSpec((1, n_blk, tq), lambda h, j: (h, 0, 0))],
        out_specs=[pl.BlockSpec((seq, LANES), lambda h, j: (0, h)),
                   pl.BlockSpec((seq, LANES), lambda h, j: (0, h // HEADS_PER_CHIP)),
                   pl.BlockSpec((tq, QK_NOPE + V_HEAD), lambda h, j: (j, h)),
                   pl.BlockSpec((1, tq, LANES), lambda h, j: (h, j, 0))],
        out_shape=[jax.ShapeDtypeStruct((seq, N_HEADS * QK_NOPE), F32),
                   jax.ShapeDtypeStruct((seq, N_CHIPS * LANES), F32),
                   jax.ShapeDtypeStruct((seq, N_HEADS * (QK_NOPE + V_HEAD)), BF16),
                   jax.ShapeDtypeStruct((N_HEADS, seq, LANES), F32)],
        scratch_shapes=[pltpu.VMEM((seq, 2 * LANES), F32)],
        compiler_params=_params(("arbitrary", "arbitrary")),
    )(q, q, kv, kr, kv, do, lse_row, delta_row)


def head_sum(x, ts=512):
    n_heads, seq, w = x.shape
    ts = min(ts, seq)

    def body(x_ref, o_ref):
        o_ref[...] = jnp.sum(x_ref[...], axis=0)

    return _pcall(body, name="head_sum", grid=(seq // ts,),
                  in_specs=[pl.BlockSpec((n_heads, ts, w), lambda i: (0, i, 0))],
                  out_specs=pl.BlockSpec((ts, w), lambda i: (i, 0)),
                  out_shape=jax.ShapeDtypeStruct((seq, w), F32),
                  compiler_params=_params(("parallel",)))(x)


HEADS_PER_CHIP = N_HEADS // N_CHIPS
Q_CHIP = HEADS_PER_CHIP * QK_DIM
Q_CHIP_NOPE = HEADS_PER_CHIP * QK_NOPE


def _perm_q_cols(w):
    t = w.reshape(w.shape[0], HEADS_PER_CHIP, QK_DIM)
    return jnp.concatenate([t[:, :, :QK_NOPE].reshape(w.shape[0], -1),
                            t[:, :, QK_NOPE:QK_NOPE + HALF_ROPE].reshape(w.shape[0], -1),
                            t[:, :, QK_NOPE + HALF_ROPE:].reshape(w.shape[0], -1)], axis=1)


def _unperm_q_cols(w):
    r = w.shape[0]
    nope = w[:, :Q_CHIP_NOPE].reshape(r, HEADS_PER_CHIP, QK_NOPE)
    r1 = w[:, Q_CHIP_NOPE:Q_CHIP_NOPE + QK_ROPE].reshape(r, HEADS_PER_CHIP, HALF_ROPE)
    r2 = w[:, Q_CHIP_NOPE + QK_ROPE:].reshape(r, HEADS_PER_CHIP, HALF_ROPE)
    return jnp.concatenate([nope, r1, r2], axis=2).reshape(r, Q_CHIP)


def _pad_kva_cols(w):
    z = jnp.zeros((w.shape[0], HALF_ROPE), w.dtype)
    return jnp.concatenate([w[:, :KV_LORA], w[:, KV_LORA:KV_LORA + HALF_ROPE], z, w[:, KV_LORA + HALF_ROPE:], z], axis=1)


def _unpad_kva_cols(w):
    return jnp.concatenate([w[:, :KV_LORA], w[:, KV_LORA:KV_LORA + HALF_ROPE],
                            w[:, KV_LORA + QK_ROPE:KV_LORA + QK_ROPE + HALF_ROPE]], axis=1)


def _rope_tile(t, cs, sn):
    return t * cs + pltpu.roll(t, LANES // 2, 1) * sn


def _rope_tile_bwd(d, cs, sn):
    return d * cs + pltpu.roll(d * sn, LANES // 2, 1)


def _b_cols(tk, tn):
    return pl.BlockSpec((None, tk, tn), lambda i, j, k: (j, k, 0))


def _b_cols_t(tk, tn):
    return pl.BlockSpec((None, tn, tk), lambda i, j, k: (k, j, 0))


def _out_cols(shape):
    return shape, lambda tm, tn: pl.BlockSpec((None, tm, tn), lambda i, j, k: (j, i, 0))


def glu_proj(y, w_glu, tm=1024):
    seq, k_dim = y.shape
    tn = w_glu.shape[2]
    tm = min(tm, seq)
    half = N_CHIPS // 2

    def body(y_ref, wv_ref, wg_ref, val_ref, gate_ref, z_ref):
        yv = y_ref[...]
        v = jnp.dot(yv, wv_ref[...], preferred_element_type=F32)
        gt = jnp.dot(yv, wg_ref[...], preferred_element_type=F32)
        val_ref[...] = v
        gate_ref[...] = gt
        z_ref[...] = (v * _sigmoid(gt)).astype(z_ref.dtype)

    tile = pl.BlockSpec((tm, tn), lambda i, j: (i, j))
    return _pcall(
        body, name="glu_proj", grid=(seq // tm, half),
        in_specs=[pl.BlockSpec((tm, k_dim), lambda i, j: (i, 0)),
                  pl.BlockSpec((None, k_dim, tn), lambda i, j: (j, 0, 0)),
                  pl.BlockSpec((None, k_dim, tn), lambda i, j: (j + half, 0, 0))],
        out_specs=[tile, tile, tile],
        out_shape=[jax.ShapeDtypeStruct((seq, half * tn), F32), jax.ShapeDtypeStruct((seq, half * tn), F32),
                   jax.ShapeDtypeStruct((seq, half * tn), BF16)],
        compiler_params=_params(("parallel", "parallel")),
    )(y, w_glu, w_glu)


def _halves(a):
    return a.reshape(N_CHIPS, 2, a.shape[1] // 2, a.shape[2])


def device_step(x, positions, target, w, comm=None):
    seq = x.shape[0]
    w = dict(w)

    def gathered(names, outs):
        for n, a in zip(names, outs):
            if isinstance(n, tuple):
                w[n[0]] = [a.reshape(v.shape) if l == n[1] else v for l, v in enumerate(w[n[0]])]
            else:
                w[n] = a.reshape(w[n].shape)

    def ride_for(names):
        if comm is None:
            return None
        return GatherRide([_halves(w[n[0]][n[1]] if isinstance(n, tuple) else w[n]) for n in names])

    first_ride = ("ssm_w_glu", "ssm_w_out", ("w_ff1", 0), ("w_ff2", 0))
    mla_ride = ("kv_w_a", "kv_w_b", "q_w_a", "q_w_b", "attn_w_o")
    second_ride = (("w_ff1", 1), ("w_ff2", 1))

    inv_freq = ROPE_THETA ** (-jnp.arange(HALF_ROPE, dtype=F32) / HALF_ROPE)
    ang = positions.astype(F32)[:, None] * jnp.tile(inv_freq, LANES // HALF_ROPE)
    cos, sin = jnp.cos(ang), jnp.sin(ang)
    quarter = jnp.arange(LANES) // HALF_ROPE
    sign = jnp.where(quarter < 2, -1.0, 1.0).astype(F32)
    own = (quarter % 2 == 0).astype(F32)
    cos_q, sin_q = cos, sin * sign
    cos_k, sin_k = cos * own, sin * (sign * own)
    ff_tile = D_FF // N_CHIPS
    pack_shape = (N_CHIPS, EARLY_ROWS, PACK_W)

    lr = w["ssm_lam_re"].reshape(N_STATES, 1)
    li = w["ssm_lam_im"].reshape(N_STATES, 1)
    ldt = jnp.repeat(w["ssm_log_dt"].reshape(N_GROUPS), SSM_STATE).reshape(N_STATES, 1)
    b_re = w["ssm_b_re"].reshape(N_STATES, SSM_GROUP)
    b_im = w["ssm_b_im"].reshape(N_STATES, SSM_GROUP)
    a_re, a_im, bb_re, bb_im = s5_prep(lr, li, ldt, b_re, b_im)
    a_re, a_im = a_re.reshape(1, N_STATES), a_im.reshape(1, N_STATES)
    bbd_re = _blockdiag_in(bb_re.reshape(N_GROUPS, SSM_STATE, SSM_GROUP)).astype(BF16)
    bbd_im = _blockdiag_in(bb_im.reshape(N_GROUPS, SSM_STATE, SSM_GROUP)).astype(BF16)
    cbd_re = _blockdiag_out(w["ssm_c_re"].reshape(N_GROUPS, SSM_GROUP, SSM_STATE)).astype(BF16)
    cbd_imn = _blockdiag_out(-w["ssm_c_im"].reshape(N_GROUPS, SSM_GROUP, SSM_STATE)).astype(BF16)
    dskip = w["ssm_d"].reshape(1, D_MODEL)
    (ypre, yg, h_re, h_im), landed = s5_fwd(x, bbd_re, bbd_im, cbd_re, cbd_imn, a_re, a_im, dskip, ride_for(first_ride))
    gathered(first_ride, landed)
    w_glu = w["ssm_w_glu"]
    glu_tile = w_glu.shape[2]
    val, gate, z = glu_proj(yg, w_glu)
    w_out = w["ssm_w_out"].reshape(D_MODEL, D_MODEL)
    ln = lambda name, l: w[name][l].reshape(1, D_MODEL)

    def then_ln(h, names, layer):
        def epi(r, hv, gl, bl):
            y = _layer_norm(hv, r, gl, bl)
            return r, y, y
        return dict(epi=epi, extras=(h, ln(names[0], layer), ln(names[1], layer)), out_dtypes=(F32, F32, BF16))

    mix0, h1, h1b = mm(z, w_out, name="ssm_out", **then_ln(x, ("ln_mix_g", "ln_mix_b"), 0))

    def mlp_fwd(h, hb, layer, riding=None, with_ln=True):
        pre = mm(hb, w["w_ff1"][layer], n_dim=D_FF, tiles=(None, ff_tile, None), b_view=_b_cols, name=f"ff1_{layer}",
                 out_dtypes=(BF16,), ride=ride_for(riding) if riding else None)
        if riding and comm is not None:
            pre, landed = pre
            gathered(riding, landed)
        post = then_ln(h, ("ln_ffn_g", "ln_ffn_b"), layer) if with_ln else {}
        return pre, mm(pre, w["w_ff2"][layer].reshape(D_FF, D_MODEL), pro_a=_relu2, name=f"ff2_{layer}", **post)

    f1pre, (f1, h2, h2b) = mlp_fwd(h1, h1b, 0, mla_ride)

    kv_w_a = w["kv_w_a"].reshape(D_MODEL, KVA_PAD)
    kv_w_b = w["kv_w_b"]
    q_w_a = w["q_w_a"].reshape(D_MODEL, Q_LORA)
    q_w_b = w["q_w_b"]
    w_o = w["attn_w_o"].reshape(D_MODEL, D_MODEL)
    kvb_tile = kv_w_b.shape[2]
    kvn_g = w["kv_norm_g"].reshape(1, KV_LORA)
    qn_g = w["q_norm_g"].reshape(1, Q_LORA)
    def kv_post(kva, g, cs, sn):
        tile = _rope_tile(kva[:, KV_LORA:], cs, sn)
        return kva, _rms(kva[:, :KV_LORA], g), _cat(tile, pltpu.roll(tile, HALF_ROPE, 1))
    kva, ckv, krope = mm(h2b, kv_w_a, epi=kv_post, extras=(kvn_g, cos_k, sin_k),
                         out_dtypes=(F32, (KV_LORA, BF16), (2 * LANES, BF16)), name="kv_a")
    kvb = mm(ckv, kv_w_b, n_dim=N_CHIPS * kvb_tile, tiles=(None, kvb_tile, KV_LORA), b_view=_b_cols, name="kv_b",
             out_dtypes=(BF16,))
    cq_raw, cq = mm(h2b, q_w_a, epi=lambda r, gq: (r, _rms(r, gq)), extras=(qn_g,), out_dtypes=(F32, BF16), name="q_a")

    def rope_and_scale(r, cs, sn):
        return (_cat(r[:, :Q_CHIP_NOPE], _rope_tile(r[:, Q_CHIP_NOPE:], cs, sn)) * Q_PRESCALE,)
    qro = mm(cq, q_w_b, n_dim=N_CHIPS * Q_CHIP, tiles=(None, Q_CHIP, Q_LORA), b_view=_b_cols, epi=rope_and_scale,
             extras=(cos_q, sin_q), out_dtypes=(BF16,), name="q_b")
    (o, lse), landed = attn_fwd(qro, kvb, krope, ride_for(second_ride))
    gathered(second_ride, landed)
    mix1, h3, h3b = mm(o, w_o, name="attn_out", **then_ln(h2, ("ln_mix_g", "ln_mix_b"), 1))
    f2pre, f2 = mlp_fwd(h3, h3b, 1, with_ln=False)
    def last_ln_loss_and_back(h, mix, gl, bl, t):
        e = _layer_norm(h, mix, gl, bl) - t
        dr, dg, db = _layer_norm_bwd(h, mix, gl, e * (1.0 / D_MODEL))
        return (dr, dr), (jnp.broadcast_to(jnp.sum(e * e), (1, LANES)), dg, db)
    dr4, dr4b, loss_acc, dg_f1, db_f1 = rowwise(
        last_ln_loss_and_back, (h3, f2, ln("ln_ffn_g", 1), ln("ln_ffn_b", 1), target),
        ((D_MODEL, F32), (D_MODEL, BF16)), accs=(LANES, D_MODEL, D_MODEL), name="ln_ffn_1_loss")
    loss = loss_acc[0, 0] * (0.5 / D_MODEL)

    g = {}

    def into_rows(off, rows_per_chip, shape=pack_shape):
        def view(tm, tn):
            if tm == N_CHIPS * rows_per_chip:
                return pl.BlockSpec((N_CHIPS, rows_per_chip, tn), lambda i, j, k: (0, off // rows_per_chip, 0))
            nb = rows_per_chip // tm
            return pl.BlockSpec((None, tm, tn), lambda i, j, k: (i // nb, off // tm + i % nb, 0))
        return shape, view

    def into_cols(off):
        return pack_shape, lambda tm, tn: pl.BlockSpec((None, tm, tn), lambda i, j, k: (j, off // tm + i, 0))

    def mlp_bwd(pack, dr, drb, hb, pre, layer, swap=False):
        w2_rows = (EARLY_OFF["w_ff2"] + layer * ff_tile, ff_tile)
        w1_rows = (EARLY_OFF["w_ff1"] + layer * D_MODEL, D_MODEL)
        ready = [(w1_rows[0] + w1_rows[1], w2_rows[0] - w1_rows[0] - w1_rows[1]), (w2_rows[0] + w2_rows[1], EARLY_ROWS - w2_rows[0] - w2_rows[1])]
        dpre = mm(drb, w["w_ff2"][layer].reshape(D_FF, D_MODEL), tb=True, epi=lambda r, p: (r * 2.0 * jnp.maximum(p, 0.0),),
                  extras=(pre,), out_dtypes=(BF16,), tiles=(None, ff_tile, None), name=f"ff2_dx_{layer}",
                  ride=SwapRide(pack, ready) if swap else None)
        if swap:
            dpre, (theirs,) = dpre
        pack = mm(pre, drb, ta=True, pro_a=_relu2, name=f"ff2_dw_{layer}", tiles=(ff_tile, PACK_W, None), into=pack,
                  out_view=into_rows(w2_rows[0], ff_tile))
        pack = mm(hb, dpre, ta=True, name=f"ff1_dw_{layer}", tiles=(None, PACK_W, None), into=pack,
                  out_view=into_cols(w1_rows[0]))
        dh = mm(dpre, w["w_ff1"][layer], tb=True, epi=lambda r, d: (r + DN_ALPHA * d,), extras=(dr,), n_dim=D_MODEL,
                tiles=(None, D_MODEL, ff_tile), b_view=_b_cols_t, name=f"ff1_dx_{layer}",
                ride=SwapRide(pack, [w1_rows, w2_rows], into=theirs) if swap else None)
        return (pack, *dh) if swap else (pack, dh)

    pack, dh3 = mlp_bwd(None, dr4, dr4b, h3b, f2pre, 1)
    dr3, dr3b, dg_m1, db_m1 = ln_bwd(h2, mix1, ln("ln_mix_g", 1), dh3, "ln_mix_bwd_1")
    shard_rows = D_MODEL // N_CHIPS
    pack = mm(o, dr3b, ta=True, name="attn_out_dw", tiles=(D_MODEL, PACK_W, None), into=pack,
              out_view=into_rows(EARLY_OFF["attn_w_o"], shard_rows))
    def head_dots(do, o):
        return do, jnp.concatenate([jnp.sum(do[:, V_HEAD * h:V_HEAD * (h + 1)] * o[:, V_HEAD * h:V_HEAD * (h + 1)], axis=1,
                                            keepdims=True) for h in range(N_HEADS)], axis=1)
    do, delta = mm(dr3b, w_o, tb=True, epi=head_dots, extras=(o,), out_dtypes=(F32, (N_HEADS, F32)), name="attn_out_dx")
    tb = min(ATT_TK, seq)
    lse_row = lse.reshape(N_HEADS, seq // tb, tb)
    delta_row = delta.T.reshape(N_HEADS, seq // tb, tb)
    dqn, dqr, dkvb, dkr = attn_bwd(qro, kvb, krope, do, lse_row, delta_row)

    def q_rope_bwd(dn, dr, cs, sn):
        parts = []
        for k in range(N_CHIPS):
            parts.append(dn[:, Q_CHIP_NOPE * k:Q_CHIP_NOPE * (k + 1)])
            parts.append(_rope_tile_bwd(dr[:, LANES * k:LANES * (k + 1)], cs, sn))
        return (jnp.concatenate(parts, axis=1),), ()
    (dqlin,) = rowwise(q_rope_bwd, (dqn, dqr, cos_q, sin_q), ((N_CHIPS * Q_CHIP, BF16),), name="q_rope_bwd")
    g["q_w_b"] = mm(cq, dqlin, ta=True, name="q_b_dw", tiles=(Q_LORA, Q_CHIP, None), out_view=_out_cols(q_w_b.shape))
    dcq_raw, dqn_g = mm(dqlin, q_w_b, tb=True, n_dim=Q_LORA, tiles=(None, Q_LORA, Q_CHIP), b_view=_b_cols_t,
                        epi=lambda d, c, gq: _rms_bwd(c, gq, d), extras=(cq_raw, qn_g), out_dtypes=(BF16,),
                        accs=(Q_LORA,), name="q_b_dx")
    g["q_w_a"] = mm(h2b, dcq_raw, ta=True, name="q_a_dw")
    g["kv_w_b"] = mm(ckv, dkvb, ta=True, name="kv_b_dw", tiles=(KV_LORA, kvb_tile, None), out_view=_out_cols(kv_w_b.shape))
    dkr_sum = head_sum(dkr)

    def kv_post_bwd(dc, kva, gk, dk, cs, sn):
        dx, dgk = _rms_bwd(kva[:, :KV_LORA], gk, dc)
        dk = dk + pltpu.roll(dk, LANES - HALF_ROPE, 1)
        return jnp.concatenate([dx, _rope_tile_bwd(dk, cs, sn)], axis=1), dgk
    dkva, dkvn_g = mm(dkvb, kv_w_b, tb=True, n_dim=KV_LORA, tiles=(None, KV_LORA, kvb_tile), b_view=_b_cols_t,
                      epi=kv_post_bwd, extras=(kva, kvn_g, dkr_sum, cos_k, sin_k), out_dtypes=((KVA_PAD, BF16),),
                      accs=(KV_LORA,), name="kv_b_dx")
    g["kv_w_a"] = mm(h2b, dkva, ta=True, name="kv_a_dw")

    def ln_ffn_bwd(r, dc, wq, d, h, f, gl):
        via_q = lax.dot_general(dc, wq, (((1,), (1,)), ((), ())), preferred_element_type=F32)
        dr, dg, db = _layer_norm_bwd(h, f, gl, r + (via_q + DN_ALPHA * d))
        return dr, dr, dg, db
    dr2, dr2b, dg_f0, db_f0 = mm(dkva, kv_w_a, tb=True, epi=ln_ffn_bwd,
                                 extras=(dcq_raw, q_w_a.astype(BF16), dr3, h1, f1, ln("ln_ffn_g", 0)),
                                 out_dtypes=(F32, BF16), accs=(D_MODEL, D_MODEL), name="qkv_a_dx")
    pack = put_rows(pack, packed_shards(g, MISC_EARLY, EARLY_ROWS - MISC_EARLY_OFF), MISC_EARLY_OFF)
    if comm is None:
        pack, dh1 = mlp_bwd(pack, dr2, dr2b, h1b, f1pre, 0)
    else:
        pack, dh1, (theirs,) = mlp_bwd(pack, dr2, dr2b, h1b, f1pre, 0, swap=True)
        early_sums = add_halves(pack, theirs, comm[1])
    dr1, dr1b, dg_m0, db_m0 = ln_bwd(x, mix0, ln("ln_mix_g", 0), dh1, "ln_mix_bwd_0")
    mid = mm(z, dr1b, ta=True, name="ssm_out_dw", tiles=(D_MODEL, PACK_W, None),
             out_view=into_rows(MID_OFF["ssm_w_out"], shard_rows, (N_CHIPS, MID_ROWS, PACK_W)))
    def glu_bwd(dz, vl, gt):
        sg = _sigmoid(gt)
        return (jnp.concatenate([dz * sg, dz * vl * sg * (1.0 - sg)], axis=1),)
    dvg = mm(dr1b, w_out, tb=True, epi=glu_bwd, extras=(val, gate), out_dtypes=((2 * D_MODEL, BF16),), name="ssm_out_dx")
    g["ssm_w_glu"] = mm(yg, dvg, ta=True, name="glu_proj_dw", tiles=(None, glu_tile, None), out_view=_out_cols(w_glu.shape))
    mid = put_rows(mid, packed_shards(g, MISC_MID, MID_ROWS - MISC_MID_OFF), MISC_MID_OFF)
    dypre = mm(dvg, w_glu, tb=True, epi=lambda r, y: (r * _gelu_grad(y),), extras=(ypre,), n_dim=D_MODEL,
               tiles=(None, D_MODEL, glu_tile), b_view=_b_cols_t, name="glu_proj_dx",
               ride=Together([SwapRide(mid), SendRide([(early_sums, (0, EARLY_HEAD), None)])]) if comm is not None else None)
    sends = None
    if comm is not None:
        dypre, (theirs, early_got) = dypre
        sends = SendRide([(early_sums, (EARLY_HEAD, EARLY_ROWS - EARLY_HEAD), early_got), add_halves(mid, theirs, comm[1])])
    (dx, dbbd_re, dbbd_im, dcbd_re, dcbd_imn, dar, dai, dd), got = s5_bwd(
        dypre, x, dr1, h_re, h_im, bbd_re, bbd_im, cbd_re, cbd_imn, a_re, a_im, dskip, sends)
    dbb_re = _blockdiag_in_t(dbbd_re).reshape(N_STATES, SSM_GROUP)
    dbb_im = _blockdiag_in_t(dbbd_im).reshape(N_STATES, SSM_GROUP)
    dlr, dli, dldt, db_re, db_im = s5_prep_bwd(lr, li, ldt, b_re, b_im, dar.reshape(N_STATES, 1),
                                               dai.reshape(N_STATES, 1), dbb_re, dbb_im)
    g["ssm_lam_re"] = dlr.reshape(1, N_GROUPS, SSM_STATE)
    g["ssm_lam_im"] = dli.reshape(1, N_GROUPS, SSM_STATE)
    g["ssm_log_dt"] = group_sum(dldt).reshape(1, N_GROUPS)
    g["ssm_b_re"] = db_re.reshape(1, N_GROUPS, SSM_STATE, SSM_GROUP)
    g["ssm_b_im"] = db_im.reshape(1, N_GROUPS, SSM_STATE, SSM_GROUP)
    g["ssm_c_re"] = _blockdiag_out_t(dcbd_re).reshape(1, N_GROUPS, SSM_GROUP, SSM_STATE)
    g["ssm_c_im"] = -_blockdiag_out_t(dcbd_imn).reshape(1, N_GROUPS, SSM_GROUP, SSM_STATE)
    g["ssm_d"] = dd
    g["ln_mix_g"] = jnp.concatenate([dg_m0, dg_m1], 0)
    g["ln_mix_b"] = jnp.concatenate([db_m0, db_m1], 0)
    g["ln_ffn_g"] = jnp.concatenate([dg_f0, dg_f1], 0)
    g["ln_ffn_b"] = jnp.concatenate([db_f0, db_f1], 0)
    g["kv_norm_g"] = dkvn_g.reshape(KV_LORA)
    g["q_norm_g"] = dqn_g
    return loss, dx, pack, mid, g, list(zip(sends.ins, got)) if comm is not None else None


def place(shard, me_idx, dtype, name, layer=None):
    rows, cols = shard.shape[-2:]
    tr = _tile(rows, (512, 256, 128))

    def body(m_ref, x_ref, o_ref):
        o_ref[...] = x_ref[...].astype(o_ref.dtype)

    in_spec = (pl.BlockSpec((tr, cols), lambda i, m: (i, 0)) if layer is None
               else pl.BlockSpec((None, tr, cols), lambda i, m: (layer, i, 0)))
    return _pcall(
        body, name=name,
        grid_spec=pltpu.PrefetchScalarGridSpec(
            num_scalar_prefetch=1, grid=(rows // tr,), in_specs=[in_spec],
            out_specs=pl.BlockSpec((None, tr, cols), lambda i, m: (m[0], i, 0))),
        out_shape=jax.ShapeDtypeStruct((N_CHIPS, rows, cols), dtype),
        compiler_params=_params(("parallel",)),
    )(me_idx, shard)


def place_many(shards, dtypes, me_idx, name):
    def body(m_ref, *refs):
        for x_ref, o_ref in zip(refs[:len(shards)], refs[len(shards):]):
            o_ref[...] = x_ref[...].astype(o_ref.dtype)

    return _pcall(
        body, name=name,
        grid_spec=pltpu.PrefetchScalarGridSpec(
            num_scalar_prefetch=1, grid=(1,),
            in_specs=[pl.BlockSpec(s.shape, lambda i, m: (0, 0)) for s in shards],
            out_specs=[pl.BlockSpec((None,) + s.shape, lambda i, m: (m[0], 0, 0)) for s in shards]),
        out_shape=[jax.ShapeDtypeStruct((N_CHIPS,) + s.shape, d) for s, d in zip(shards, dtypes)],
        compiler_params=_params(("arbitrary",)),
    )(me_idx, *shards)


def put_rows(pack, rows, off):
    _, n, cols = rows.shape

    def body(r_ref, p_ref, o_ref, sem):
        cp = pltpu.make_async_copy(r_ref.at[0], o_ref.at[pl.program_id(0), pl.ds(off, n), :], sem)
        cp.start()
        cp.wait()

    return _pcall(body, name="grad_put_rows", grid=(N_CHIPS,),
                  in_specs=[pl.BlockSpec((1, n, cols), lambda k: (k, 0, 0)), _ANY], out_specs=_ANY,
                  out_shape=jax.ShapeDtypeStruct(pack.shape, pack.dtype), input_output_aliases={1: 0},
                  scratch_shapes=[pltpu.SemaphoreType.DMA],
                  compiler_params=_params(("arbitrary",)))(rows, pack)


def _my_cols(c, mine=True):
    start = (c if mine else 1 - c) * HALF_W
    return pl.ds(pl.multiple_of(start, HALF_W), HALF_W)


def add_halves(gpack, got, c_idx):
    n, rows, _ = gpack.shape
    tr = min(G_BLOCK_ROWS, rows)
    blk = (None, tr, HALF_W)

    def body(c_ref, g_ref, r_ref, o_ref):
        o_ref[...] = (g_ref[...] + r_ref[...]).astype(o_ref.dtype)

    return _pcall(
        body, name="grad_add_halves",
        grid_spec=pltpu.PrefetchScalarGridSpec(
            num_scalar_prefetch=1, grid=(n, rows // tr),
            in_specs=[pl.BlockSpec(blk, lambda k, i, c: (k, i, c[0])), pl.BlockSpec(blk, lambda k, i, c: (k, i, 0))],
            out_specs=pl.BlockSpec(blk, lambda k, i, c: (k, i, 0))),
        out_shape=jax.ShapeDtypeStruct((n, rows, HALF_W), BF16),
        compiler_params=_params(("parallel", "parallel")),
    )(c_idx, gpack, got)


def sum_owner(part, got, idx, total_rows, row_off=0, into=None):
    _, rows, _ = part.shape
    tr = math.gcd(math.gcd(rows, row_off), G_BLOCK_ROWS)
    n_into = 0 if into is None else 1

    def body(m_ref, p_ref, g_ref, *rest):
        up = lambda v: v.astype(F32)
        rest[-1][...] = ((up(p_ref[...]) + up(g_ref[0])) + up(g_ref[1])) + up(g_ref[2])

    return _pcall(
        body, name="grad_sum_owner",
        grid_spec=pltpu.PrefetchScalarGridSpec(
            num_scalar_prefetch=1, grid=(rows // tr,),
            in_specs=[pl.BlockSpec((None, tr, HALF_W), lambda i, m: (m[0], i, 0)),
                      pl.BlockSpec((3, tr, HALF_W), lambda i, m: (0, i, 0))] + [_ANY] * n_into,
            out_specs=pl.BlockSpec((tr, HALF_W), lambda i, m: (row_off // tr + i, m[1]))),
        out_shape=jax.ShapeDtypeStruct((total_rows, PACK_W), F32),
        input_output_aliases={3: 0} if n_into else {},
        compiler_params=_params(("parallel",)),
    )(idx, part, got, *([into] if n_into else []))


def join_halves(red):
    def body(in_ref, out_ref, send_sem, recv_sem):
        x, y, c, _ = _place()
        sibling = (x, y, 1 - c)
        mine = out_ref.at[:, _my_cols(c)]
        cp = pltpu.make_async_remote_copy(src_ref=mine, dst_ref=mine, send_sem=send_sem, recv_sem=recv_sem,
                                          device_id=sibling, device_id_type=MESH)
        cp.start()
        cp.wait_send()
        other = out_ref.at[:, _my_cols(c, mine=False)]
        pltpu.make_async_remote_copy(src_ref=other, dst_ref=other, send_sem=send_sem, recv_sem=recv_sem,
                                     device_id=sibling, device_id_type=MESH).wait_recv()

    return _pcall(body, name="grad_join_halves", in_specs=[_ANY], out_specs=_ANY,
                  out_shape=jax.ShapeDtypeStruct(red.shape, red.dtype), input_output_aliases={0: 0},
                  scratch_shapes=[pltpu.SemaphoreType.DMA, pltpu.SemaphoreType.DMA])(red)


def adamw(gsrc, g_off, wt, m, v, name):
    n, cols = wt.shape
    tr = math.gcd(math.gcd(g_off, n), 256) if g_off else math.gcd(n, 256)
    off_blk = g_off // tr
    c1 = 1.0 / (1.0 - ADAM_B1 ** ADAM_STEP)
    c2 = 1.0 / (1.0 - ADAM_B2 ** ADAM_STEP)

    def body(g_ref, w_ref, m_ref, v_ref, go_ref, d_ref, mo_ref, vo_ref):
        gv = g_ref[...]
        mn = ADAM_B1 * m_ref[...] + (1.0 - ADAM_B1) * gv
        vn = ADAM_B2 * v_ref[...] + (1.0 - ADAM_B2) * gv * gv
        go_ref[...] = gv
        mo_ref[...] = mn
        vo_ref[...] = vn
        d_ref[...] = -ADAM_LR * ((mn * c1) / (jnp.sqrt(vn * c2) + ADAM_EPS) + ADAM_WD * w_ref[...])

    blk = pl.BlockSpec((tr, cols), lambda i: (i, 0))
    return _pcall(body, name=name, grid=(n // tr,),
                  in_specs=[pl.BlockSpec((tr, cols), lambda i: (off_blk + i, 0)), blk, blk, blk],
                  out_specs=[blk] * 4, out_shape=[jax.ShapeDtypeStruct((n, cols), F32)] * 4,
                  compiler_params=_params(("parallel",)))(gsrc, wt, m, v)


def _rows8(a):
    return -(-a.size // (8 * PACK_W)) * 8


def _as_rows(a, rows=None):
    flat = a.reshape(-1)
    n = _rows8(a) if rows is None else rows
    return jnp.pad(flat, (0, n * PACK_W - flat.shape[0])).reshape(n, PACK_W)


def local_shards_2d(wl):
    return {"w_ff1": [wl["w_ff1"][0], wl["w_ff1"][1]], "w_ff2": [wl["w_ff2"][0], wl["w_ff2"][1]],
            "ssm_w_glu": wl["ssm_w_glu"], "ssm_w_out": wl["ssm_w_out"], "kv_w_a": _pad_kva_cols(wl["kv_w_a"]),
            "kv_w_b": wl["kv_w_b"], "q_w_a": wl["q_w_a"], "q_w_b": _perm_q_cols(wl["q_w_b"]),
            "attn_w_o": wl["attn_w_o"], "ssm_d": wl["ssm_d"].reshape(2, -1)}


def misc_grad_shard(name, g, k):
    if name == "ssm_d":
        w = D_MODEL // N_CHIPS
        return g[:, w * k:w * (k + 1)]
    if name in ("ssm_w_glu", "kv_w_b"):
        return g[k]
    if name == "q_w_b":
        return _unperm_q_cols(g[k])
    rows = D_MODEL // N_CHIPS
    shard = g[rows * k:rows * (k + 1)]
    return _unpad_kva_cols(shard) if name == "kv_w_a" else shard


def packed_shards(g, names, rows, tail=None):
    blocks = []
    for k in range(N_CHIPS):
        parts = [_as_rows(misc_grad_shard(n, g[n], k), MISC_SHARD_ROWS[n]) for n in names]
        if tail is not None:
            parts.append(tail[k * (tail.shape[0] // N_CHIPS):(k + 1) * (tail.shape[0] // N_CHIPS)])
        blk = jnp.concatenate(parts, axis=0)
        blocks.append(jnp.pad(blk, ((0, rows - blk.shape[0]), (0, 0))))
    return jnp.stack(blocks)


def kernel(x, positions, ln_mix_g, ln_mix_b, ln_ffn_g, ln_ffn_b, w_ff1, w_ff2, ssm_lam_re, ssm_lam_im, ssm_log_dt, ssm_b_re, ssm_b_im, ssm_c_re, ssm_c_im, ssm_d, ssm_w_glu, ssm_w_out, kv_w_a, kv_norm_g, kv_w_b, q_w_a, q_norm_g, q_w_b, attn_w_o, loss_target, m_ln_mix_g, m_ln_mix_b, m_ln_ffn_g, m_ln_ffn_b, m_w_ff1, m_w_ff2, m_ssm_lam_re, m_ssm_lam_im, m_ssm_log_dt, m_ssm_b_re, m_ssm_b_im, m_ssm_c_re, m_ssm_c_im, m_ssm_d, m_ssm_w_glu, m_ssm_w_out, m_kv_w_a, m_kv_norm_g, m_kv_w_b, m_q_w_a, m_q_norm_g, m_q_w_b, m_attn_w_o, v_ln_mix_g, v_ln_mix_b, v_ln_ffn_g, v_ln_ffn_b, v_w_ff1, v_w_ff2, v_ssm_lam_re, v_ssm_lam_im, v_ssm_log_dt, v_ssm_b_re, v_ssm_b_im, v_ssm_c_re, v_ssm_c_im, v_ssm_d, v_ssm_w_glu, v_ssm_w_out, v_kv_w_a, v_kv_norm_g, v_kv_w_b, v_q_w_a, v_q_norm_g, v_q_w_b, v_attn_w_o):
    env = dict(locals())
    wl = {n: env[n] for n in WEIGHTS}
    ml = {n: env["m_" + n] for n in WEIGHTS}
    vl = {n: env["v_" + n] for n in WEIGHTS}
    for n in ("ssm_w_glu", "ssm_w_out", "q_w_a", "q_w_b", "attn_w_o"):
        wl[n], ml[n], vl[n] = wl[n][0], ml[n][0], vl[n][0]

    c_idx = lax.axis_index("c").astype(jnp.int32).reshape(1)
    me_idx = (2 * lax.axis_index("x") + lax.axis_index("y")).astype(jnp.int32).reshape(1)

    local = local_shards_2d(wl)
    stacked = {n: [place(wl[n], me_idx, BF16, f"place_{n}_{l}", layer=l) for l in range(DEPTH)] for n in ("w_ff1", "w_ff2")}
    others = [n for n in SHARDED if n not in stacked]
    stacked.update(zip(others, place_many([local[n] for n in others], [F32 if n == "ssm_d" else BF16 for n in others],
                                          me_idx, "place_others")))
    stacked["ssm_d"] = ride_alone(GatherRide([_halves(stacked["ssm_d"])]), "ssm_d_all_gather")[0].reshape(1, D_MODEL)
    for n in REPLICATED:
        stacked[n] = wl[n]

    loss_part, dx, early, mid, g, sent = device_step(x[0], positions[0], loss_target[0], stacked, comm=(me_idx, c_idx))
    loss = lax.psum(loss_part, ("x", "y", "c"))

    small = jnp.concatenate([_as_rows(g[n]) for n in REPLICATED], axis=0)
    small = jnp.pad(small, ((0, SMALL_ROWS - small.shape[0]), (0, 0)))
    late = packed_shards(g, MISC_LATE, LATE_ROWS, tail=small)
    late_sums = add_halves(late, ride_alone(SwapRide(late), "grad_swap_halves")[0], c_idx)
    sent.append((late_sums, ride_alone(SendRide([late_sums]), "grad_send_to_owners")[0]))
    where = jnp.concatenate([me_idx, c_idx])
    starts = (0, EARLY_ROWS, EARLY_ROWS + MID_ROWS)
    total_rows = EARLY_ROWS + MID_ROWS + LATE_ROWS
    reduced = None
    for (sums, got), off in zip(sent, starts):
        reduced = sum_owner(sums, got, where, total_rows, row_off=off, into=reduced)
    reduced = join_halves(reduced)
    quarter = reduced[starts[2] + SMALL_OFF:starts[2] + SMALL_OFF + SMALL_Q_ROWS]
    small_tot = ride_alone(GatherRide([_halves(place(quarter, me_idx, F32, "place_small_grads"))]),
                           "small_grad_all_gather")[0].reshape(SMALL_ROWS, PACK_W)

    out_g, out_d, out_m, out_v = {}, {}, {}, {}
    direct = {**EARLY_OFF, **{n: starts[1] + o for n, o in MID_OFF.items()}}
    for n, off in direct.items():
        res = adamw(reduced, off, wl[n].reshape(-1, PACK_W), ml[n].reshape(-1, PACK_W), vl[n].reshape(-1, PACK_W),
                    "adamw_" + n)
        out_g[n], out_d[n], out_m[n], out_v[n] = [a.reshape(env[n].shape) for a in res]
    for names, off in ((MISC_EARLY, MISC_EARLY_OFF), (MISC_MID, starts[1] + MISC_MID_OFF), (MISC_LATE, starts[2])):
        pack3 = lambda d: jnp.concatenate([_as_rows(d[n], MISC_SHARD_ROWS[n]) for n in names], axis=0)
        res = adamw(reduced, off, pack3(wl), pack3(ml), pack3(vl), "adamw_packed_" + names[0])
        r0 = 0
        for n in names:
            cnt = math.prod(env[n].shape)
            out_g[n], out_d[n], out_m[n], out_v[n] = [
                a[r0:r0 + MISC_SHARD_ROWS[n]].reshape(-1)[:cnt].reshape(env[n].shape) for a in res]
            r0 += MISC_SHARD_ROWS[n]
    ws = jnp.concatenate([_as_rows(wl[n]) for n in REPLICATED], axis=0)
    ms = jnp.concatenate([_as_rows(ml[n]) for n in REPLICATED], axis=0)
    vs = jnp.concatenate([_as_rows(vl[n]) for n in REPLICATED], axis=0)
    pad = ((0, SMALL_ROWS - ws.shape[0]), (0, 0))
    res = adamw(small_tot, 0, jnp.pad(ws, pad), jnp.pad(ms, pad), jnp.pad(vs, pad), "adamw_replicated")
    row = 0
    for n in REPLICATED:
        cnt = math.prod(env[n].shape)
        nrows = _rows8(env[n])
        out_g[n], out_d[n], out_m[n], out_v[n] = [a[row:row + nrows].reshape(-1)[:cnt].reshape(env[n].shape) for a in res]
        row += nrows

    return (loss, dx[None], *[out_g[n] for n in WEIGHTS], *[out_d[n] for n in WEIGHTS],
            *[out_m[n] for n in WEIGHTS], *[out_v[n] for n in WEIGHTS])
```

```python
import functools
import math

import jax
import jax.numpy as jnp
from jax import lax
from jax.experimental import pallas as pl
from jax.experimental.pallas import tpu as pltpu

F32 = jnp.float32
BF16 = jnp.bfloat16
MESH = pl.DeviceIdType.MESH

D_MODEL = 1024
DEPTH = 2
SSM_GROUP = 16
N_GROUPS = D_MODEL // SSM_GROUP
SSM_STATE = 64
N_STATES = N_GROUPS * SSM_STATE
N_HEADS = 8
QK_NOPE = 128
QK_ROPE = 64
HALF_ROPE = QK_ROPE // 2
V_HEAD = 128
QK_DIM = QK_NOPE + QK_ROPE
Q_LORA = 384
KV_LORA = 256
ROPE_THETA = 10000.0
SM_SCALE = QK_DIM ** -0.5
NEG_INF = -1e30
D_FF = 4 * D_MODEL
DN_ALPHA = (2 * DEPTH) ** 0.25
LN_EPS = 1e-5
RMS_EPS = 1e-6
ADAM_LR = 0.001
ADAM_B1 = 0.9
ADAM_B2 = 0.999
ADAM_EPS = 1e-08
ADAM_WD = 0.01
ADAM_STEP = 10

N_CHIPS = 4
LANES = 128
VMEM_LIMIT = 56 * 1024 * 1024
MM_VMEM_BUDGET = 40 * 1024 * 1024
PACK_W = 1024
KVA_PAD = 384
HALF_W = PACK_W // 2

SHARDED = ("w_ff1", "w_ff2", "ssm_w_glu", "ssm_w_out", "kv_w_a", "kv_w_b", "q_w_a", "q_w_b", "attn_w_o", "ssm_d")
G_BLOCK_ROWS = 960
EARLY_OFF = {"w_ff1": 0, "w_ff2": 2048, "attn_w_o": 4096}
MISC_EARLY = ("kv_w_b", "kv_w_a", "q_w_a", "q_w_b")
MISC_EARLY_OFF = 4352
EARLY_ROWS = 5 * G_BLOCK_ROWS
EARLY_HEAD = G_BLOCK_ROWS
MID_OFF = {"ssm_w_out": 0}
MISC_MID = ("ssm_w_glu",)
MISC_MID_OFF = 256
MID_ROWS = MISC_MID_OFF + 512
MISC_LATE = ("ssm_d",)
SMALL_Q_ROWS = 96
SMALL_ROWS = N_CHIPS * SMALL_Q_ROWS
SMALL_OFF = 16
LATE_ROWS = 192
MISC_SHARD_ROWS = {"ssm_d": 16, "ssm_w_glu": 512, "kv_w_b": 128, "kv_w_a": 80, "q_w_a": 96, "q_w_b": 144}
REPLICATED = ("ln_mix_g", "ln_mix_b", "ln_ffn_g", "ln_ffn_b", "ssm_lam_re", "ssm_lam_im", "ssm_log_dt",
              "ssm_b_re", "ssm_b_im", "ssm_c_re", "ssm_c_im", "kv_norm_g", "q_norm_g")
WEIGHTS = ("ln_mix_g", "ln_mix_b", "ln_ffn_g", "ln_ffn_b", "w_ff1", "w_ff2", "ssm_lam_re", "ssm_lam_im",
           "ssm_log_dt", "ssm_b_re", "ssm_b_im", "ssm_c_re", "ssm_c_im", "ssm_d", "ssm_w_glu", "ssm_w_out",
           "kv_w_a", "kv_norm_g", "kv_w_b", "q_w_a", "q_norm_g", "q_w_b", "attn_w_o")


def _pcall(body, **kw):
    return pl.pallas_call(body, **kw)


def _params(sem=None):
    return pltpu.CompilerParams(dimension_semantics=sem, vmem_limit_bytes=VMEM_LIMIT)


_ANY = pl.BlockSpec(memory_space=pl.ANY)


def _tile(dim, prefs):
    for p in prefs:
        if dim % p == 0:
            return p
    return dim


def _place():
    x, y, c = lax.axis_index("x"), lax.axis_index("y"), lax.axis_index("c")
    return x, y, c, [(1 - x, y), (x, 1 - y), (1 - x, 1 - y)]


def _remote(k, src, dst, to, send_sems, recv_sems):
    return pltpu.make_async_remote_copy(src_ref=src, dst_ref=dst, send_sem=send_sems.at[k], recv_sem=recv_sems.at[k],
                                        device_id=to, device_id_type=MESH)


class GatherRide:
    def __init__(self, arrs):
        self.ins = list(arrs)
        self.out_shapes = [jax.ShapeDtypeStruct(a.shape, a.dtype) for a in arrs]
        self.aliases = {i: i for i in range(len(arrs))}
        self.n_sems = 6 * len(arrs)

    def start(self, ins, outs, send_sems, recv_sems):
        x, y, c, chips = _place()
        me = 2 * x + y
        for a, o in enumerate(outs):
            for j, (px, py) in enumerate(chips):
                _remote(6 * a + j, o.at[me, c], o.at[me, c], (px, py, c), send_sems, recv_sems).start()

    def pass_on(self, ins, outs, send_sems, recv_sems):
        x, y, c, chips = _place()
        for a, o in enumerate(outs):
            for j, (px, py) in enumerate(chips):
                blk = o.at[2 * px + py, c]
                _remote(6 * a + j, blk, blk, (px, py, c), send_sems, recv_sems).wait_recv()
                _remote(6 * a + 3 + j, blk, blk, (x, y, 1 - c), send_sems, recv_sems).start()

    def finish(self, ins, outs, send_sems, recv_sems, passed_on=False):
        if not passed_on:
            self.pass_on(ins, outs, send_sems, recv_sems)
        x, y, c, chips = _place()
        me = 2 * x + y
        sibling = (x, y, 1 - c)
        for a, o in enumerate(outs):
            for j, (px, py) in enumerate(chips):
                blk = o.at[2 * px + py, 1 - c]
                _remote(6 * a + 3 + j, blk, blk, sibling, send_sems, recv_sems).wait_recv()
                _remote(6 * a + j, o.at[me, c], o.at[me, c], (px, py, c), send_sems, recv_sems).wait_send()
                mine = o.at[2 * px + py, c]
                _remote(6 * a + 3 + j, mine, mine, sibling, send_sems, recv_sems).wait_send()


class SendRide:
    base = 0

    def __init__(self, parts):
        parts = [p if isinstance(p, tuple) else (p, (0, p.shape[1]), None) for p in parts]
        self.rows = [rows for _, rows, _ in parts]
        self.n_parts = len(parts)
        self.ins = [p for p, _, _ in parts] + [into for _, _, into in parts if into is not None]
        self.out_shapes = [jax.ShapeDtypeStruct((3,) + p.shape[1:], p.dtype) for p, _, _ in parts]
        given = [a for a, (_, _, into) in enumerate(parts) if into is not None]
        self.aliases = {self.n_parts + i: a for i, a in enumerate(given)}
        self.n_sems = 3 * self.n_parts

    def _copies(self, ins, outs, send_sems, recv_sems):
        x, y, c, chips = _place()
        return [_remote(self.base + 3 * a + j, ins[a].at[2 * px + py, pl.ds(r0, n)], outs[a].at[j, pl.ds(r0, n)],
                        (px, py, c), send_sems, recv_sems)
                for a, (r0, n) in enumerate(self.rows) for j, (px, py) in enumerate(chips)]

    def start(self, ins, outs, send_sems, recv_sems):
        for cp in self._copies(ins, outs, send_sems, recv_sems):
            cp.start()

    def finish(self, ins, outs, send_sems, recv_sems):
        for cp in self._copies(ins, outs, send_sems, recv_sems):
            cp.wait()


class SwapRide:
    base = 0

    def __init__(self, pack, ranges=None, into=None):
        self.ins = [pack] if into is None else [pack, into]
        self.out_shapes = [jax.ShapeDtypeStruct(pack.shape[:2] + (HALF_W,), pack.dtype)]
        self.aliases = {} if into is None else {1: 0}
        self.ranges = ranges or [(0, pack.shape[1])]
        self.n_sems = len(self.ranges)

    def _copies(self, ins, outs, send_sems, recv_sems):
        x, y, c, _ = _place()
        return [_remote(self.base + k, ins[0].at[:, pl.ds(r0, n), _my_cols(c, mine=False)], outs[0].at[:, pl.ds(r0, n), :],
                        (x, y, 1 - c), send_sems, recv_sems) for k, (r0, n) in enumerate(self.ranges)]

    def start(self, ins, outs, send_sems, recv_sems):
        for cp in self._copies(ins, outs, send_sems, recv_sems):
            cp.start()

    def finish(self, ins, outs, send_sems, recv_sems):
        for cp in self._copies(ins, outs, send_sems, recv_sems):
            cp.wait()


class Together:
    def __init__(self, rides):
        self.rides = rides
        self.ins, self.out_shapes, self.aliases, self.n_sems = [], [], {}, 0
        for r in rides:
            r.base = self.n_sems
            self.aliases.update({len(self.ins) + i: len(self.out_shapes) + o for i, o in r.aliases.items()})
            self.ins += r.ins
            self.out_shapes += r.out_shapes
            self.n_sems += r.n_sems

    def _each(self, step, ins, outs, send_sems, recv_sems):
        i = o = 0
        for r in self.rides:
            getattr(r, step)(ins[i:i + len(r.ins)], outs[o:o + len(r.out_shapes)], send_sems, recv_sems)
            i, o = i + len(r.ins), o + len(r.out_shapes)

    def start(self, *refs):
        self._each("start", *refs)

    def finish(self, *refs):
        self._each("finish", *refs)


def _pcall_riding(body, args, ride, first, last, *, in_specs, out_specs, out_shape, scratch_shapes=(), middle=None,
                  in_place=None, **kw):
    n_in, n_out = len(args), len(out_shape)
    in_place = in_place or {}
    if ride is None:
        return _pcall(body, in_specs=in_specs, out_specs=out_specs, out_shape=out_shape,
                      input_output_aliases=in_place, scratch_shapes=list(scratch_shapes), **kw)(*args), []
    k_in, k_out = len(ride.ins), len(ride.out_shapes)

    def riding(*refs):
        ins, r_in = refs[:n_in], refs[n_in:n_in + k_in]
        outs = refs[n_in + k_in:n_in + k_in + n_out]
        r_out = refs[n_in + k_in + n_out:n_in + k_in + n_out + k_out]
        scratch, (send_sems, recv_sems) = refs[n_in + k_in + n_out + k_out:-2], refs[-2:]

        @pl.when(first())
        def _():
            ride.start(r_in, r_out, send_sems, recv_sems)

        if middle is not None:
            @pl.when(middle())
            def _():
                ride.pass_on(r_in, r_out, send_sems, recv_sems)

        body(*ins, *outs, *scratch)

        @pl.when(last())
        def _():
            if middle is not None:
                ride.finish(r_in, r_out, send_sems, recv_sems, passed_on=True)
            else:
                ride.finish(r_in, r_out, send_sems, recv_sems)

    res = _pcall(riding, in_specs=list(in_specs) + [_ANY] * k_in, out_specs=list(out_specs) + [_ANY] * k_out,
                 out_shape=list(out_shape) + ride.out_shapes,
                 input_output_aliases={**in_place, **{n_in + i: n_out + o for i, o in ride.aliases.items()}},
                 scratch_shapes=list(scratch_shapes) + [pltpu.SemaphoreType.DMA((ride.n_sems,))] * 2,
                 **kw)(*args, *ride.ins)
    return res[:n_out], res[n_out:]


def ride_alone(ride, name):
    def body(*refs):
        n = len(ride.ins)
        ins, outs, (send_sems, recv_sems) = refs[:n], refs[n:-2], refs[-2:]
        ride.start(ins, outs, send_sems, recv_sems)
        ride.finish(ins, outs, send_sems, recv_sems)

    return _pcall(body, name=name, in_specs=[_ANY] * len(ride.ins), out_specs=[_ANY] * len(ride.out_shapes),
                  out_shape=ride.out_shapes, input_output_aliases=dict(ride.aliases),
                  scratch_shapes=[pltpu.SemaphoreType.DMA((ride.n_sems,))] * 2)(*ride.ins)


def mm(a, b, *, name, ta=False, tb=False, pro_a=None, epi=None, extras=(), out_dtypes=(F32,), n_dim=None,
       tiles=(None, None, None), b_view=None, out_view=None, into=None, ride=None, accs=()):
    widths = [d[0] if isinstance(d, tuple) else None for d in out_dtypes]
    out_dtypes = [d[1] if isinstance(d, tuple) else d for d in out_dtypes]
    if ta:
        k_dim, m_dim = a.shape
    else:
        m_dim, k_dim = a.shape
    if n_dim is None:
        n_dim = b.shape[0] if tb else b.shape[1]
    tn = tiles[1] or (n_dim if n_dim <= 1024 else _tile(n_dim, (1024, 512, 256, 128)))
    tk = tiles[2] or (k_dim if k_dim <= 1024 else _tile(k_dim, (1024, 512, 256, 128)))
    nk = k_dim // tk

    def vmem_bytes(tm):
        blocks = tm * tk * a.dtype.itemsize + tk * tn * b.dtype.itemsize
        blocks += sum(tm * (tn if e.shape[1] == n_dim else e.shape[1]) * e.dtype.itemsize for e in extras if e.shape[0] > 1)
        blocks += tm * sum((w or tn) * jnp.dtype(d).itemsize for w, d in zip(widths, out_dtypes))
        return 2 * blocks + tm * tn * 4

    tm = tiles[0] or next((t for t in (4096, 2048, 1024, 512, 256) if m_dim % t == 0 and vmem_bytes(t) <= MM_VMEM_BUDGET),
                          _tile(m_dim, (128,)))
    assert m_dim % tm == 0 and n_dim % tn == 0 and k_dim % tk == 0, (name, m_dim, n_dim, k_dim, tm, tn, tk)
    assert tn == n_dim or not (any(widths) or accs), name
    n_ex, n_out = len(extras), len(out_dtypes)
    n_into = 0 if into is None else 1
    dims = (((0 if ta else 1,), (1 if tb else 0,)), ((), ()))

    def body(a_ref, b_ref, *rest):
        ex_refs, out_refs = rest[:n_ex], rest[n_ex + n_into:n_ex + n_into + n_out]
        sum_refs = rest[n_ex + n_into + n_out:n_ex + n_into + n_out + len(accs)]

        def partial():
            av = a_ref[...]
            if pro_a is not None:
                av = pro_a(av)
            return lax.dot_general(av.astype(BF16), b_ref[...].astype(BF16), dims, preferred_element_type=F32)

        def finish(r):
            res = epi(r, *[e[...] for e in ex_refs]) if epi is not None else (r,)
            for o_ref, v in zip(out_refs, res):
                o_ref[...] = v.reshape(o_ref.shape).astype(o_ref.dtype)
            if accs:
                @pl.when(pl.program_id(0) == 0)
                def _():
                    for s_ref in sum_refs:
                        s_ref[...] = jnp.zeros_like(s_ref)

                for s_ref, v in zip(sum_refs, res[n_out:]):
                    s_ref[...] += v

        if nk == 1:
            finish(partial())
            return
        acc = rest[-1]
        k = pl.program_id(2)

        @pl.when(k == 0)
        def _():
            acc[...] = partial()

        @pl.when(k > 0)
        def _():
            acc[...] += partial()

        @pl.when(k == nk - 1)
        def _():
            finish(acc[...])

    def ex_spec(e):
        if e.shape == (m_dim, n_dim):
            return o_spec
        if e.shape[0] == m_dim:
            return pl.BlockSpec((tm, e.shape[1]), lambda i, j, k: (i, 0))
        return pl.BlockSpec(e.shape, lambda i, j, k: (0, 0))

    a_spec = pl.BlockSpec((tk, tm), lambda i, j, k: (k, i)) if ta else pl.BlockSpec((tm, tk), lambda i, j, k: (i, k))
    if b_view is not None:
        b_spec = b_view(tk, tn)
    else:
        b_spec = pl.BlockSpec((tn, tk), lambda i, j, k: (j, k)) if tb else pl.BlockSpec((tk, tn), lambda i, j, k: (k, j))
    o_spec = pl.BlockSpec((tm, tn), lambda i, j, k: (i, j))
    if out_view is None:
        out_specs = [o_spec if w is None else pl.BlockSpec((tm, w), lambda i, j, k: (i, 0)) for w in widths]
        out_shape = [jax.ShapeDtypeStruct((m_dim, w or n_dim), dt) for w, dt in zip(widths, out_dtypes)]
    else:
        assert n_out == 1
        out_specs = [out_view[1](tm, tn)]
        out_shape = [jax.ShapeDtypeStruct(out_view[0], out_dtypes[0])]
    out_specs = out_specs + [pl.BlockSpec((1, w), lambda i, j, k: (0, 0)) for w in accs]
    out_shape = out_shape + [jax.ShapeDtypeStruct((1, w), F32) for w in accs]
    grid = (m_dim // tm, n_dim // tn, nk)
    scratch = [pltpu.VMEM((tm, tn), F32)] if nk > 1 else []
    if ride is not None:
        assert into is None
        at = lambda ids: functools.reduce(jnp.logical_and, [pl.program_id(d) == i for d, i in enumerate(ids)])
        outs, landed = _pcall_riding(
            body, (a, b, *extras), ride, lambda: at((0, 0, 0)), lambda: at([g - 1 for g in grid]),
            name=name, grid=grid, in_specs=[a_spec, b_spec] + [ex_spec(e) for e in extras], out_specs=out_specs,
            out_shape=out_shape, scratch_shapes=scratch, compiler_params=_params(("arbitrary",) * 3))
        return (outs[0] if len(outs) == 1 else outs), landed
    outs = _pcall(
        body, name=name, grid=grid,
        in_specs=[a_spec, b_spec] + [ex_spec(e) for e in extras] + [_ANY] * n_into,
        out_specs=out_specs, out_shape=out_shape,
        input_output_aliases={2 + n_ex: 0} if n_into else {},
        scratch_shapes=scratch,
        compiler_params=_params(("arbitrary",) * 3 if accs else ("parallel", "parallel", "arbitrary")),
    )(a, b, *extras, *([into] if n_into else []))
    return outs[0] if len(outs) == 1 else outs


def rowwise(fn, ins, outs, *, name, accs=(), tm=256):
    rows = ins[0].shape[0]
    tm = min(tm, rows)
    n_in, n_out, n_acc = len(ins), len(outs), len(accs)

    def body(*refs):
        in_refs, out_refs, acc_refs = refs[:n_in], refs[n_in:n_in + n_out], refs[n_in + n_out:]
        res, sums = fn(*[r[...] for r in in_refs])
        for o_ref, v in zip(out_refs, res):
            o_ref[...] = v.astype(o_ref.dtype)
        if n_acc:
            @pl.when(pl.program_id(0) == 0)
            def _():
                for a_ref in acc_refs:
                    a_ref[...] = jnp.zeros_like(a_ref)

            for a_ref, s in zip(acc_refs, sums):
                a_ref[...] += s

    def spec(arr):
        if arr.shape[0] == rows:
            return pl.BlockSpec((tm, arr.shape[1]), lambda i: (i, 0))
        return pl.BlockSpec(arr.shape, lambda i: (0, 0))

    res = _pcall(
        body, name=name, grid=(rows // tm,),
        in_specs=[spec(a) for a in ins],
        out_specs=[pl.BlockSpec((tm, w), lambda i: (i, 0)) for w, _ in outs]
        + [pl.BlockSpec((1, w), lambda i: (0, 0)) for w in accs],
        out_shape=[jax.ShapeDtypeStruct((rows, w), dt) for w, dt in outs]
        + [jax.ShapeDtypeStruct((1, w), F32) for w in accs],
        compiler_params=_params(("arbitrary",) if n_acc else ("parallel",)),
    )(*ins)
    return res


def _relu2(v):
    r = jnp.maximum(v, 0.0)
    return r * r


def _gelu(x):
    c = math.sqrt(2.0 / math.pi)
    return 0.5 * x * (1.0 + jnp.tanh(c * (x + 0.044715 * x * x * x)))


def _gelu_grad(x):
    c = math.sqrt(2.0 / math.pi)
    t = jnp.tanh(c * (x + 0.044715 * x * x * x))
    return 0.5 * (1.0 + t) + 0.5 * x * (1.0 - t * t) * c * (1.0 + 3 * 0.044715 * x * x)


def _sigmoid(x):
    return 1.0 / (1.0 + jnp.exp(-x))


def _layer_norm(h, mix, g, b):
    r = DN_ALPHA * h + mix
    mu = jnp.mean(r, axis=-1, keepdims=True)
    xc = r - mu
    var = jnp.mean(xc * xc, axis=-1, keepdims=True)
    return xc * lax.rsqrt(var + LN_EPS) * g + b


def _layer_norm_bwd(h, mix, g, dy):
    r = DN_ALPHA * h + mix
    mu = jnp.mean(r, axis=-1, keepdims=True)
    xc = r - mu
    var = jnp.mean(xc * xc, axis=-1, keepdims=True)
    rstd = lax.rsqrt(var + LN_EPS)
    xhat = xc * rstd
    dxh = dy * g
    m1 = jnp.mean(dxh, axis=-1, keepdims=True)
    m2 = jnp.mean(dxh * xhat, axis=-1, keepdims=True)
    dr = rstd * (dxh - m1 - xhat * m2)
    return dr, jnp.sum(dy * xhat, axis=0, keepdims=True), jnp.sum(dy, axis=0, keepdims=True)


def ln_bwd(h, mix, g, dy, name):
    def fn(h, mix, g, dy):
        dr, dg, db = _layer_norm_bwd(h, mix, g, dy)
        return (dr, dr), (dg, db)
    return rowwise(fn, (h, mix, g, dy), ((D_MODEL, F32), (D_MODEL, BF16)), accs=(D_MODEL, D_MODEL), name=name, tm=512)


def _rms(x, g):
    r = lax.rsqrt(jnp.mean(x * x, axis=-1, keepdims=True) + RMS_EPS)
    return x * r * g


def _rms_bwd(x, g, dy):
    r = lax.rsqrt(jnp.mean(x * x, axis=-1, keepdims=True) + RMS_EPS)
    xn = x * r
    dyg = dy * g
    dx = r * (dyg - xn * jnp.mean(dyg * xn, axis=-1, keepdims=True))
    return dx, jnp.sum(dy * xn, axis=0, keepdims=True)


def _s5_disc(lr, li, ldt):
    dt = jnp.exp(ldt)
    mag = jnp.exp(lr * dt)
    cs, sn = jnp.cos(li * dt), jnp.sin(li * dt)
    ar, ai = mag * cs, mag * sn
    inv = 1.0 / (lr * lr + li * li)
    n_re = (ar - 1.0) * lr + ai * li
    n_im = ai * lr - (ar - 1.0) * li
    return dt, mag, cs, sn, ar, ai, inv, n_re, n_im


def s5_prep(lr, li, ldt, b_re, b_im):
    def fn(lr, li, ldt, b_re, b_im):
        _, _, _, _, ar, ai, inv, n_re, n_im = _s5_disc(lr, li, ldt)
        cr, ci = n_re * inv, n_im * inv
        return (ar, ai, cr * b_re - ci * b_im, cr * b_im + ci * b_re), ()
    return rowwise(fn, (lr, li, ldt, b_re, b_im), ((1, F32), (1, F32), (SSM_GROUP, F32), (SSM_GROUP, F32)),
                   name="s5_prep", tm=512)


def s5_prep_bwd(lr, li, ldt, b_re, b_im, dar, dai, dbb_re, dbb_im):
    def fn(lr, li, ldt, b_re, b_im, dar, dai, dbb_re, dbb_im):
        dt, mag, cs, sn, ar, ai, inv, n_re, n_im = _s5_disc(lr, li, ldt)
        cr, ci = n_re * inv, n_im * inv
        db_re = cr * dbb_re + ci * dbb_im
        db_im = cr * dbb_im - ci * dbb_re
        dcr = jnp.sum(dbb_re * b_re + dbb_im * b_im, axis=-1, keepdims=True)
        dci = jnp.sum(dbb_im * b_re - dbb_re * b_im, axis=-1, keepdims=True)
        dar = dar + (dcr * lr - dci * li) * inv
        dai = dai + (dcr * li + dci * lr) * inv
        dinv = dcr * n_re + dci * n_im
        dlr = (dcr * (ar - 1.0) + dci * ai) * inv - 2.0 * lr * inv * inv * dinv
        dli = (dcr * ai - dci * (ar - 1.0)) * inv - 2.0 * li * inv * inv * dinv
        dmag = dar * cs + dai * sn
        dth = dai * ar - dar * ai
        dlr = dlr + dmag * mag * dt
        dli = dli + dth * dt
        ddt = dmag * mag * lr + dth * li
        return (dlr, dli, ddt * dt, db_re, db_im), ()
    return rowwise(fn, (lr, li, ldt, b_re, b_im, dar, dai, dbb_re, dbb_im),
                   ((1, F32), (1, F32), (1, F32), (SSM_GROUP, F32), (SSM_GROUP, F32)), name="s5_prep_bwd", tm=512)


def group_sum(x):
    def body(x_ref, o_ref):
        o_ref[...] = jnp.sum(x_ref[...], axis=1)
    return _pcall(body, name="s5_group_sum", out_shape=jax.ShapeDtypeStruct((N_GROUPS, 1), F32))(
        x.reshape(N_GROUPS, SSM_STATE, 1))


GROUPS_PER_TILE = LANES // SSM_GROUP
TILE_STATES = GROUPS_PER_TILE * SSM_STATE
N_UTILES = D_MODEL // LANES


SUBLANES = 8
SCAN_STRIP = 1024
N_STRIPS = N_STATES // SCAN_STRIP
_NT = (((1,), (1,)), ((), ()))
_TN = (((0,), (0,)), ((), ()))


def _scan_coefs(are, aim, shifted, reverse):
    ar = are[...]
    ai = -aim[...] if reverse else aim[...]
    powers = {1: (ar, ai)}
    for d in (2, 4):
        r, i = powers[d // 2]
        powers[d] = (r * r - i * i, 2.0 * r * i)
    rid = lax.broadcasted_iota(jnp.int32, (SUBLANES, N_STATES), 0)
    first = (rid == SUBLANES - 1) if reverse else (rid == 0)
    masks = [(1, first)] + [(d, (rid <= SUBLANES - 1 - d) if reverse else (rid >= d)) for d in (1, 2, 4)]
    for n, (d, keep) in enumerate(masks):
        for part in (0, 1):
            shifted[2 * n + part][...] = jnp.where(keep, jnp.broadcast_to(powers[d][part], (SUBLANES, N_STATES)), 0.0)


def _tile_scan(xr, xi, shifted, nbr_re, nbr_im, reverse):
    for n, d in enumerate((1, 1, 2, 4)):
        by = SUBLANES - d if reverse else d
        fr, fi = (nbr_re, nbr_im) if n == 0 else (xr, xi)
        sr, si = pltpu.roll(fr, by, 0), pltpu.roll(fi, by, 0)
        kr, ki = shifted[2 * n], shifted[2 * n + 1]
        xr, xi = xr + kr * sr - ki * si, xi + kr * si + ki * sr
    return xr, xi


def _tile_rows(t):
    return pl.ds(pl.multiple_of(t * SUBLANES, SUBLANES), SUBLANES)


def s5_fwd(u, bbd_re, bbd_im, cbd_re, cbd_imn, a_re, a_im, dskip, ride=None, t_rows=256):
    seq = u.shape[0]
    t_rows = min(t_rows, seq)
    n_tiles = t_rows // SUBLANES

    def body(u_ref, bre, bim, cre, cimn, are, aim, d_ref, y_ref, gelu_ref, hre_ref, him_ref, car_re, car_im, *shifted):
        @pl.when(pl.program_id(0) == 0)
        def _():
            car_re[...] = jnp.zeros_like(car_re)
            car_im[...] = jnp.zeros_like(car_im)
            _scan_coefs(are, aim, shifted, reverse=False)

        uf = u_ref[...]
        ub = uf.astype(BF16)
        for j in range(N_UTILES):
            uj = ub[:, LANES * j:LANES * (j + 1)]
            sl = slice(TILE_STATES * j, TILE_STATES * (j + 1))
            hre_ref[:, sl] = jnp.dot(uj, bre[j], preferred_element_type=F32)
            him_ref[:, sl] = jnp.dot(uj, bim[j], preferred_element_type=F32)
        for s in range(N_STRIPS):
            cols = pl.ds(s * SCAN_STRIP, SCAN_STRIP)
            coefs = [c[:, cols] for c in shifted]

            def step(t, before):
                rows = _tile_rows(t)
                hr, hi = _tile_scan(hre_ref[rows, cols], him_ref[rows, cols], coefs, before[0], before[1], False)
                hre_ref[rows, cols] = hr
                him_ref[rows, cols] = hi
                return hr, hi

            cr, ci = lax.fori_loop(0, n_tiles, step, (car_re[:, cols], car_im[:, cols]))
            car_re[:, cols] = cr
            car_im[:, cols] = ci
        dv = d_ref[...]
        for j in range(N_UTILES):
            st = slice(TILE_STATES * j, TILE_STATES * (j + 1))
            yj = (jnp.dot(hre_ref[:, st].astype(BF16), cre[j], preferred_element_type=F32)
                  + jnp.dot(him_ref[:, st].astype(BF16), cimn[j], preferred_element_type=F32))
            sl = slice(LANES * j, LANES * (j + 1))
            yj = yj + dv[:, sl] * uf[:, sl]
            y_ref[:, sl] = yj
            gelu_ref[:, sl] = _gelu(yj).astype(gelu_ref.dtype)

    full3 = lambda a: pl.BlockSpec(a.shape, lambda i: (0, 0, 0))
    full2 = lambda a: pl.BlockSpec(a.shape, lambda i: (0, 0))
    tile = pltpu.VMEM((SUBLANES, N_STATES), F32)
    n_chunks = seq // t_rows
    return _pcall_riding(
        body, (u, bbd_re, bbd_im, cbd_re, cbd_imn, a_re, a_im, dskip), ride,
        lambda: pl.program_id(0) == 0, lambda: pl.program_id(0) == n_chunks - 1,
        middle=(lambda: pl.program_id(0) == (7 * n_chunks) // 8) if ride is not None else None,
        name="s5_fwd", grid=(n_chunks,),
        in_specs=[pl.BlockSpec((t_rows, D_MODEL), lambda i: (i, 0)), full3(bbd_re), full3(bbd_im), full3(cbd_re),
                  full3(cbd_imn), full2(a_re), full2(a_im), full2(dskip)],
        out_specs=[pl.BlockSpec((t_rows, D_MODEL), lambda i: (i, 0)),
                   pl.BlockSpec((t_rows, D_MODEL), lambda i: (i, 0)),
                   pl.BlockSpec((t_rows, N_STATES), lambda i: (i, 0)),
                   pl.BlockSpec((t_rows, N_STATES), lambda i: (i, 0))],
        out_shape=[jax.ShapeDtypeStruct((seq, D_MODEL), F32),
                   jax.ShapeDtypeStruct((seq, D_MODEL), BF16),
                   jax.ShapeDtypeStruct((seq, N_STATES), F32),
                   jax.ShapeDtypeStruct((seq, N_STATES), F32)],
        scratch_shapes=[tile] * 10,
        compiler_params=_params(("arbitrary",)))


def s5_bwd(dy, u, dres, h_re, h_im, bbd_re, bbd_im, cbd_re, cbd_imn, a_re, a_im, dskip, ride=None, t_rows=256):
    seq = u.shape[0]
    t_rows = min(t_rows, seq)
    n_chunks = seq // t_rows

    n_tiles = t_rows // SUBLANES

    def body(dy_ref, u_ref, dres_ref, hre_ref, him_ref, hpre_ref, hpim_ref, bre, bim, cre, cimn, are, aim, d_ref,
             dx_ref, dbre, dbim, dcre, dcimn, dar_ref, dai_ref, dd_ref, lre, lim, car_re, car_im, acc_re, acc_im,
             *shifted):
        i = pl.program_id(0)

        @pl.when(i == 0)
        def _():
            for r in (car_re, car_im, acc_re, acc_im, dbre, dbim, dcre, dcimn, dd_ref):
                r[...] = jnp.zeros_like(r)
            _scan_coefs(are, aim, shifted, reverse=True)

        dyf = dy_ref[...]
        dyb = dyf.astype(BF16)
        uf = u_ref[...]
        ub = uf.astype(BF16)
        for j in range(N_UTILES):
            dyj = dyb[:, LANES * j:LANES * (j + 1)]
            st = slice(TILE_STATES * j, TILE_STATES * (j + 1))
            lre[:, st] = lax.dot_general(dyj, cre[j], _NT, preferred_element_type=F32)
            lim[:, st] = lax.dot_general(dyj, cimn[j], _NT, preferred_element_type=F32)
        has_pred = (i < n_chunks - 1).astype(F32)
        last_row = lax.broadcasted_iota(jnp.int32, (SUBLANES, SCAN_STRIP), 0) == SUBLANES - 1
        for s in range(N_STRIPS):
            cols = pl.ds(s * SCAN_STRIP, SCAN_STRIP)
            coefs = [c[:, cols] for c in shifted]
            before_re, before_im = hpre_ref[:, cols] * has_pred, hpim_ref[:, cols] * has_pred

            def step(k, carry):
                after_re, after_im, dar, dai = carry
                t = n_tiles - 1 - k
                rows = _tile_rows(t)
                lr, li = _tile_scan(lre[rows, cols], lim[rows, cols], coefs, after_re, after_im, True)
                lre[rows, cols] = lr
                lim[rows, cols] = li
                prev = _tile_rows(jnp.maximum(t - 1, 0))
                pre_re = jnp.where(t == 0, before_re, hre_ref[prev, cols])
                pre_im = jnp.where(t == 0, before_im, him_ref[prev, cols])
                hpr = pltpu.roll(jnp.where(last_row, pre_re, hre_ref[rows, cols]), 1, 0)
                hpi = pltpu.roll(jnp.where(last_row, pre_im, him_ref[rows, cols]), 1, 0)
                return lr, li, dar + lr * hpr + li * hpi, dai + li * hpr - lr * hpi

            cr, ci, dar, dai = lax.fori_loop(0, n_tiles, step, (car_re[:, cols], car_im[:, cols],
                                                               acc_re[:, cols], acc_im[:, cols]))
            car_re[:, cols] = cr
            car_im[:, cols] = ci
            acc_re[:, cols] = dar
            acc_im[:, cols] = dai

        dv = d_ref[...]
        for j in range(N_UTILES):
            sl = slice(LANES * j, LANES * (j + 1))
            st = slice(TILE_STATES * j, TILE_STATES * (j + 1))
            lrj = lre[:, st].astype(BF16)
            lij = lim[:, st].astype(BF16)
            du = (lax.dot_general(lrj, bre[j], _NT, preferred_element_type=F32)
                  + lax.dot_general(lij, bim[j], _NT, preferred_element_type=F32))
            dx_ref[:, sl] = du + dv[:, sl] * dyf[:, sl] + DN_ALPHA * dres_ref[:, sl]
            uj = ub[:, sl]
            dbre[j] += lax.dot_general(uj, lrj, _TN, preferred_element_type=F32)
            dbim[j] += lax.dot_general(uj, lij, _TN, preferred_element_type=F32)
            dyj = dyb[:, sl]
            dcre[j] += lax.dot_general(hre_ref[:, st].astype(BF16), dyj, _TN, preferred_element_type=F32)
            dcimn[j] += lax.dot_general(him_ref[:, st].astype(BF16), dyj, _TN, preferred_element_type=F32)
        dd_ref[...] += jnp.sum(dyf * uf, axis=0, keepdims=True)

        @pl.when(i == n_chunks - 1)
        def _():
            dar_ref[...] = jnp.sum(acc_re[...], axis=0, keepdims=True)
            dai_ref[...] = jnp.sum(acc_im[...], axis=0, keepdims=True)

    rev = lambda i: (n_chunks - 1 - i, 0)
    prev_tile = lambda i: (jnp.maximum((n_chunks - 1 - i) * n_tiles - 1, 0), 0)
    once = pl.Buffered(1)
    full3 = lambda a: pl.BlockSpec(a.shape, lambda i: (0, 0, 0), pipeline_mode=once)
    full2 = lambda a: pl.BlockSpec(a.shape, lambda i: (0, 0), pipeline_mode=once)
    acc3 = lambda shape: pl.BlockSpec(shape, lambda i: (0, 0, 0))
    acc2 = lambda shape: pl.BlockSpec(shape, lambda i: (0, 0))
    tile = pltpu.VMEM((SUBLANES, N_STATES), F32)
    return _pcall_riding(
        body, (dy, u, dres, h_re, h_im, h_re, h_im, bbd_re, bbd_im, cbd_re, cbd_imn, a_re, a_im, dskip), ride,
        lambda: pl.program_id(0) == 0, lambda: pl.program_id(0) == n_chunks - 1,
        name="s5_bwd", grid=(n_chunks,),
        in_specs=[pl.BlockSpec((t_rows, D_MODEL), rev), pl.BlockSpec((t_rows, D_MODEL), rev),
                  pl.BlockSpec((t_rows, D_MODEL), rev),
                  pl.BlockSpec((t_rows, N_STATES), rev), pl.BlockSpec((t_rows, N_STATES), rev),
                  pl.BlockSpec((SUBLANES, N_STATES), prev_tile), pl.BlockSpec((SUBLANES, N_STATES), prev_tile),
                  full3(bbd_re), full3(bbd_im), full3(cbd_re), full3(cbd_imn), full2(a_re), full2(a_im), full2(dskip)],
        out_specs=[pl.BlockSpec((t_rows, D_MODEL), rev), acc3(bbd_re.shape), acc3(bbd_im.shape), acc3(cbd_re.shape),
                   acc3(cbd_imn.shape), acc2((1, N_STATES)), acc2((1, N_STATES)), acc2((1, D_MODEL))],
        out_shape=[jax.ShapeDtypeStruct((seq, D_MODEL), F32), jax.ShapeDtypeStruct(bbd_re.shape, F32),
                   jax.ShapeDtypeStruct(bbd_im.shape, F32), jax.ShapeDtypeStruct(cbd_re.shape, F32),
                   jax.ShapeDtypeStruct(cbd_imn.shape, F32), jax.ShapeDtypeStruct((1, N_STATES), F32),
                   jax.ShapeDtypeStruct((1, N_STATES), F32), jax.ShapeDtypeStruct((1, D_MODEL), F32)],
        scratch_shapes=[pltpu.VMEM((t_rows, N_STATES), F32), pltpu.VMEM((t_rows, N_STATES), F32)] + [tile] * 12,
        in_place={2: 0},
        compiler_params=_params(("arbitrary",)))


def _eye_groups():
    return jnp.eye(GROUPS_PER_TILE, dtype=F32)


def _blockdiag_in(bb):
    t = bb.transpose(0, 2, 1).reshape(N_UTILES, GROUPS_PER_TILE, SSM_GROUP, SSM_STATE)
    bd = jnp.einsum("jgcp,gh->jgchp", t, _eye_groups())
    return bd.reshape(N_UTILES, LANES, TILE_STATES)


def _blockdiag_in_t(d):
    t = jnp.einsum("jgchp,gh->jgcp", d.reshape(N_UTILES, GROUPS_PER_TILE, SSM_GROUP, GROUPS_PER_TILE, SSM_STATE),
                   _eye_groups())
    return t.reshape(N_GROUPS, SSM_GROUP, SSM_STATE).transpose(0, 2, 1)


def _blockdiag_out(c):
    t = c.transpose(0, 2, 1).reshape(N_UTILES, GROUPS_PER_TILE, SSM_STATE, SSM_GROUP)
    bd = jnp.einsum("jhpc,hg->jhpgc", t, _eye_groups())
    return bd.reshape(N_UTILES, TILE_STATES, LANES)


def _blockdiag_out_t(d):
    t = jnp.einsum("jhpgc,hg->jhpc", d.reshape(N_UTILES, GROUPS_PER_TILE, SSM_STATE, GROUPS_PER_TILE, SSM_GROUP),
                   _eye_groups())
    return t.reshape(N_GROUPS, SSM_STATE, SSM_GROUP).transpose(0, 2, 1)


ATT_TQ = 512
ATT_TK = 512
LOG2E = math.log2(math.e)
LN2 = math.log(2.0)
Q_PRESCALE = SM_SCALE * LOG2E


def _loop_in_pairs(n, step, carry, start=0):
    pairs = (n - start) // 2

    def two(t, c):
        return step(start + 2 * t + 1, step(start + 2 * t, c))

    carry = lax.fori_loop(0, pairs, two, carry)
    return lax.fori_loop(start + 2 * pairs, n, step, carry)


def _causal(s, transposed=False):
    r = lax.broadcasted_iota(jnp.int32, s.shape, 0)
    c = lax.broadcasted_iota(jnp.int32, s.shape, 1)
    return jnp.where((r <= c) if transposed else (c <= r), s, NEG_INF)


def _q_specs(rows, at):
    def nope(*ids):
        r, h = at(*ids)
        return r, 3 * (h // HEADS_PER_CHIP) + h % HEADS_PER_CHIP

    def rope(*ids):
        r, h = at(*ids)
        return r, 3 * (h // HEADS_PER_CHIP) + HEADS_PER_CHIP

    return [pl.BlockSpec((rows, LANES), nope), pl.BlockSpec((rows, LANES), rope)]


def _kv_specs(rows, at):
    def col(f):
        def index(*ids):
            r, h = at(*ids)
            return r, f(h)
        return index

    return [pl.BlockSpec((rows, LANES), col(lambda h: 2 * h)), pl.BlockSpec((rows, LANES), col(lambda h: h % HEADS_PER_CHIP)),
            pl.BlockSpec((rows, LANES), col(lambda h: 2 * h + 1))]


def _cat(a, b):
    return jnp.concatenate([a, b], axis=1)


def attn_fwd(q, kv, kr, ride=None, tq=ATT_TQ, tk=ATT_TK):
    seq = q.shape[0]
    n_heads = N_HEADS
    tq, tk = min(tq, seq), min(tk, seq)
    assert tq == tk

    def body(qn_ref, qr_ref, kn_ref, kr_ref, v_ref, o_ref, lse_ref):
        qi = pl.program_id(1)
        qv = _cat(qn_ref[...], qr_ref[...])
        jd = qi

        def block(j, carry, diag):
            m, l, acc = carry
            rows = pl.ds(pl.multiple_of(j * tk, tk), tk)
            s = lax.dot_general(qv, _cat(kn_ref[rows, :], kr_ref[rows, :]), _NT, preferred_element_type=F32)
            if diag:
                s = _causal(s)
            m_new = jnp.maximum(m, jnp.max(s, axis=-1, keepdims=True))
            p = jnp.exp2(s - m_new)
            corr = jnp.exp2(m - m_new)
            l = l * corr + jnp.sum(p, axis=-1, keepdims=True)
            acc = acc * corr + jnp.dot(p.astype(BF16), v_ref[rows, :], preferred_element_type=F32)
            return m_new, l, acc

        init = (jnp.full((tq, 1), NEG_INF, F32), jnp.zeros((tq, 1), F32), jnp.zeros((tq, V_HEAD), F32))
        carry = _loop_in_pairs(jd, lambda j, c: block(j, c, False), init)
        m, l, acc = block(jd, carry, True)
        o_ref[...] = acc / l
        lse_ref[...] = jnp.transpose(jnp.broadcast_to(m + jnp.log2(l), (tq, LANES)))[:1, :]

    n_q = seq // tq
    return _pcall_riding(
        body, (q, q, kv, kr, kv), ride,
        lambda: (pl.program_id(0) == 0) & (pl.program_id(1) == 0),
        lambda: (pl.program_id(0) == n_heads - 1) & (pl.program_id(1) == n_q - 1),
        middle=(lambda: (pl.program_id(0) == (5 * n_heads) // 8) & (pl.program_id(1) == 0)) if ride is not None else None,
        name="attn_fwd", grid=(n_heads, n_q),
        in_specs=_q_specs(tq, lambda h, i: (i, h)) + _kv_specs(seq, lambda h, i: (0, h)),
        out_specs=[pl.BlockSpec((tq, V_HEAD), lambda h, i: (i, h)),
                   pl.BlockSpec((None, None, 1, tq), lambda h, i: (h, i, 0, 0))],
        out_shape=[jax.ShapeDtypeStruct((seq, n_heads * V_HEAD), F32),
                   jax.ShapeDtypeStruct((n_heads, n_q, 1, tq), F32)],
        compiler_params=_params(("arbitrary", "arbitrary")))


def attn_bwd(q, kv, kr, do, lse_row, delta_row, tq=ATT_TK):
    seq = q.shape[0]
    tq = min(tq, seq)
    n_blk = seq // tq

    def body(qn_ref, qr_ref, kn_ref, kr_ref, v_ref, do_ref, lse_ref, delta_ref, dqn_ref, dqr_ref, dkv_ref, dkr_ref, dq_acc):
        head, kj = pl.program_id(0), pl.program_id(1)

        @pl.when(kj == 0)
        def _():
            dq_acc[...] = jnp.zeros_like(dq_acc)

        kc = _cat(kn_ref[...], kr_ref[...])
        vv = v_ref[...]

        def block(i, carry, diag):
            dk, dv = carry
            rows = pl.ds(pl.multiple_of(i * tq, tq), tq)
            qv = _cat(qn_ref[rows, :], qr_ref[rows, :])
            st = lax.dot_general(kc, qv, _NT, preferred_element_type=F32)
            if diag:
                st = _causal(st, transposed=True)
            pt = jnp.exp2(st - lse_ref[0, pl.ds(i, 1), :])
            dob = do_ref[rows, :].astype(BF16)
            dv = dv + jnp.dot(pt.astype(BF16), dob, preferred_element_type=F32)
            dpt = lax.dot_general(vv, dob, _NT, preferred_element_type=F32)
            dst = (pt * (dpt - delta_ref[0, pl.ds(i, 1), :])).astype(BF16)
            dk = dk + jnp.dot(dst, qv, preferred_element_type=F32)
            dq_acc[rows, :] += lax.dot_general(dst, kc, _TN, preferred_element_type=F32)
            return dk, dv

        carry = block(kj, (jnp.zeros((tq, 2 * LANES), F32), jnp.zeros((tq, V_HEAD), F32)), True)
        dk, dv = _loop_in_pairs(n_blk, lambda i, c: block(i, c, False), carry, start=kj + 1)
        dk = dk * LN2
        dkv_ref[...] = _cat(dk[:, :LANES], dv).astype(dkv_ref.dtype)
        lane = lax.broadcasted_iota(jnp.int32, (tq, LANES), 1)
        mine = (lane // HALF_ROPE) % HEADS_PER_CHIP == head % HEADS_PER_CHIP
        dkr_ref[0] = jnp.where(mine, dk[:, LANES:], 0.0)

        @pl.when(kj == n_blk - 1)
        def _():
            dqn_ref[...] = dq_acc[:, :LANES] * SM_SCALE

        @pl.when((kj == n_blk - 1) & (head % HEADS_PER_CHIP == 0))
        def _():
            dqr_ref[...] = dq_acc[:, LANES:] * SM_SCALE

        @pl.when((kj == n_blk - 1) & (head % HEADS_PER_CHIP > 0))
        def _():
            dqr_ref[...] += dq_acc[:, LANES:] * SM_SCALE

    return _pcall(
        body, name="attn_bwd", grid=(N_HEADS, n_blk),
        in_specs=_q_specs(seq, lambda h, j: (0, h)) + _kv_specs(tq, lambda h, j: (j, h))
        + [pl.BlockSpec((seq, V_HEAD), lambda h, j: (0, h)),
           pl.BlockSpec((1, n_blk, tq), lambda h, j: (h, 0, 0)),
           pl.BlockSpec((1, n_blk, tq), lambda h, j: (h, 0, 0))],
        out_specs=[pl.BlockSpec((seq, LANES), lambda h, j: (0, h)),
                   pl.BlockSpec((seq, LANES), lambda h, j: (0, h // HEADS_PER_CHIP)),
                   pl.BlockSpec((tq, QK_NOPE + V_HEAD), lambda h, j: (j, h)),
                   pl.BlockSpec((1, tq, LANES), lambda h, j: (h, j, 0))],
        out_shape=[jax.ShapeDtypeStruct((seq, N_HEADS * QK_NOPE), F32),
                   jax.ShapeDtypeStruct((seq, N_CHIPS * LANES), F32),
                   jax.ShapeDtypeStruct((seq, N_HEADS * (QK_NOPE + V_HEAD)), BF16),
                   jax.ShapeDtypeStruct((N_HEADS, seq, LANES), F32)],
        scratch_shapes=[pltpu.VMEM((seq, 2 * LANES), F32)],
        compiler_params=_params(("arbitrary", "arbitrary")),
    )(q, q, kv, kr, kv, do, lse_row, delta_row)


def head_sum(x, ts=512):
    n_heads, seq, w = x.shape
    ts = min(ts, seq)

    def body(x_ref, o_ref):
        o_ref[...] = jnp.sum(x_ref[...], axis=0)

    return _pcall(body, name="head_sum", grid=(seq // ts,),
                  in_specs=[pl.BlockSpec((n_heads, ts, w), lambda i: (0, i, 0))],
                  out_specs=pl.BlockSpec((ts, w), lambda i: (i, 0)),
                  out_shape=jax.ShapeDtypeStruct((seq, w), F32),
                  compiler_params=_params(("parallel",)))(x)


HEADS_PER_CHIP = N_HEADS // N_CHIPS
Q_CHIP = HEADS_PER_CHIP * QK_DIM
Q_CHIP_NOPE = HEADS_PER_CHIP * QK_NOPE


def _perm_q_cols(w):
    t = w.reshape(w.shape[0], HEADS_PER_CHIP, QK_DIM)
    return jnp.concatenate([t[:, :, :QK_NOPE].reshape(w.shape[0], -1),
                            t[:, :, QK_NOPE:QK_NOPE + HALF_ROPE].reshape(w.shape[0], -1),
                            t[:, :, QK_NOPE + HALF_ROPE:].reshape(w.shape[0], -1)], axis=1)


def _unperm_q_cols(w):
    r = w.shape[0]
    nope = w[:, :Q_CHIP_NOPE].reshape(r, HEADS_PER_CHIP, QK_NOPE)
    r1 = w[:, Q_CHIP_NOPE:Q_CHIP_NOPE + QK_ROPE].reshape(r, HEADS_PER_CHIP, HALF_ROPE)
    r2 = w[:, Q_CHIP_NOPE + QK_ROPE:].reshape(r, HEADS_PER_CHIP, HALF_ROPE)
    return jnp.concatenate([nope, r1, r2], axis=2).reshape(r, Q_CHIP)


def _pad_kva_cols(w):
    z = jnp.zeros((w.shape[0], HALF_ROPE), w.dtype)
    return jnp.concatenate([w[:, :KV_LORA], w[:, KV_LORA:KV_LORA + HALF_ROPE], z, w[:, KV_LORA + HALF_ROPE:], z], axis=1)


def _unpad_kva_cols(w):
    return jnp.concatenate([w[:, :KV_LORA], w[:, KV_LORA:KV_LORA + HALF_ROPE],
                            w[:, KV_LORA + QK_ROPE:KV_LORA + QK_ROPE + HALF_ROPE]], axis=1)


def _rope_tile(t, cs, sn):
    return t * cs + pltpu.roll(t, LANES // 2, 1) * sn


def _rope_tile_bwd(d, cs, sn):
    return d * cs + pltpu.roll(d * sn, LANES // 2, 1)


def _b_cols(tk, tn):
    return pl.BlockSpec((None, tk, tn), lambda i, j, k: (j, k, 0))


def _b_cols_t(tk, tn):
    return pl.BlockSpec((None, tn, tk), lambda i, j, k: (k, j, 0))


def _out_cols(shape):
    return shape, lambda tm, tn: pl.BlockSpec((None, tm, tn), lambda i, j, k: (j, i, 0))


def glu_proj(y, w_glu, tm=1024):
    seq, k_dim = y.shape
    tn = w_glu.shape[2]
    tm = min(tm, seq)
    half = N_CHIPS // 2

    def body(y_ref, wv_ref, wg_ref, val_ref, gate_ref, z_ref):
        yv = y_ref[...]
        v = jnp.dot(yv, wv_ref[...], preferred_element_type=F32)
        gt = jnp.dot(yv, wg_ref[...], preferred_element_type=F32)
        val_ref[...] = v
        gate_ref[...] = gt
        z_ref[...] = (v * _sigmoid(gt)).astype(z_ref.dtype)

    tile = pl.BlockSpec((tm, tn), lambda i, j: (i, j))
    return _pcall(
        body, name="glu_proj", grid=(seq // tm, half),
        in_specs=[pl.BlockSpec((tm, k_dim), lambda i, j: (i, 0)),
                  pl.BlockSpec((None, k_dim, tn), lambda i, j: (j, 0, 0)),
                  pl.BlockSpec((None, k_dim, tn), lambda i, j: (j + half, 0, 0))],
        out_specs=[tile, tile, tile],
        out_shape=[jax.ShapeDtypeStruct((seq, half * tn), F32), jax.ShapeDtypeStruct((seq, half * tn), F32),
                   jax.ShapeDtypeStruct((seq, half * tn), BF16)],
        compiler_params=_params(("parallel", "parallel")),
    )(y, w_glu, w_glu)


def _halves(a):
    return a.reshape(N_CHIPS, 2, a.shape[1] // 2, a.shape[2])


def device_step(x, positions, target, w, comm=None):
    seq = x.shape[0]
    w = dict(w)

    def gathered(names, outs):
        for n, a in zip(names, outs):
            if isinstance(n, tuple):
                w[n[0]] = [a.reshape(v.shape) if l == n[1] else v for l, v in enumerate(w[n[0]])]
            else:
                w[n] = a.reshape(w[n].shape)

    def ride_for(names):
        if comm is None:
            return None
        return GatherRide([_halves(w[n[0]][n[1]] if isinstance(n, tuple) else w[n]) for n in names])

    first_ride = ("ssm_w_glu", "ssm_w_out", ("w_ff1", 0), ("w_ff2", 0))
    mla_ride = ("kv_w_a", "kv_w_b", "q_w_a", "q_w_b", "attn_w_o")
    second_ride = (("w_ff1", 1), ("w_ff2", 1))

    inv_freq = ROPE_THETA ** (-jnp.arange(HALF_ROPE, dtype=F32) / HALF_ROPE)
    ang = positions.astype(F32)[:, None] * jnp.tile(inv_freq, LANES // HALF_ROPE)
    cos, sin = jnp.cos(ang), jnp.sin(ang)
    quarter = jnp.arange(LANES) // HALF_ROPE
    sign = jnp.where(quarter < 2, -1.0, 1.0).astype(F32)
    own = (quarter % 2 == 0).astype(F32)
    cos_q, sin_q = cos, sin * sign
    cos_k, sin_k = cos * own, sin * (sign * own)
    ff_tile = D_FF // N_CHIPS
    pack_shape = (N_CHIPS, EARLY_ROWS, PACK_W)

    lr = w["ssm_lam_re"].reshape(N_STATES, 1)
    li = w["ssm_lam_im"].reshape(N_STATES, 1)
    ldt = jnp.repeat(w["ssm_log_dt"].reshape(N_GROUPS), SSM_STATE).reshape(N_STATES, 1)
    b_re = w["ssm_b_re"].reshape(N_STATES, SSM_GROUP)
    b_im = w["ssm_b_im"].reshape(N_STATES, SSM_GROUP)
    a_re, a_im, bb_re, bb_im = s5_prep(lr, li, ldt, b_re, b_im)
    a_re, a_im = a_re.reshape(1, N_STATES), a_im.reshape(1, N_STATES)
    bbd_re = _blockdiag_in(bb_re.reshape(N_GROUPS, SSM_STATE, SSM_GROUP)).astype(BF16)
    bbd_im = _blockdiag_in(bb_im.reshape(N_GROUPS, SSM_STATE, SSM_GROUP)).astype(BF16)
    cbd_re = _blockdiag_out(w["ssm_c_re"].reshape(N_GROUPS, SSM_GROUP, SSM_STATE)).astype(BF16)
    cbd_imn = _blockdiag_out(-w["ssm_c_im"].reshape(N_GROUPS, SSM_GROUP, SSM_STATE)).astype(BF16)
    dskip = w["ssm_d"].reshape(1, D_MODEL)
    (ypre, yg, h_re, h_im), landed = s5_fwd(x, bbd_re, bbd_im, cbd_re, cbd_imn, a_re, a_im, dskip, ride_for(first_ride))
    gathered(first_ride, landed)
    w_glu = w["ssm_w_glu"]
    glu_tile = w_glu.shape[2]
    val, gate, z = glu_proj(yg, w_glu)
    w_out = w["ssm_w_out"].reshape(D_MODEL, D_MODEL)
    ln = lambda name, l: w[name][l].reshape(1, D_MODEL)

    def then_ln(h, names, layer):
        def epi(r, hv, gl, bl):
            y = _layer_norm(hv, r, gl, bl)
            return r, y, y
        return dict(epi=epi, extras=(h, ln(names[0], layer), ln(names[1], layer)), out_dtypes=(F32, F32, BF16))

    mix0, h1, h1b = mm(z, w_out, name="ssm_out", **then_ln(x, ("ln_mix_g", "ln_mix_b"), 0))

    def mlp_fwd(h, hb, layer, riding=None, with_ln=True):
        pre = mm(hb, w["w_ff1"][layer], n_dim=D_FF, tiles=(None, ff_tile, None), b_view=_b_cols, name=f"ff1_{layer}",
                 out_dtypes=(BF16,), ride=ride_for(riding) if riding else None)
        if riding and comm is not None:
            pre, landed = pre
            gathered(riding, landed)
        post = then_ln(h, ("ln_ffn_g", "ln_ffn_b"), layer) if with_ln else {}
        return pre, mm(pre, w["w_ff2"][layer].reshape(D_FF, D_MODEL), pro_a=_relu2, name=f"ff2_{layer}", **post)

    f1pre, (f1, h2, h2b) = mlp_fwd(h1, h1b, 0, mla_ride)

    kv_w_a = w["kv_w_a"].reshape(D_MODEL, KVA_PAD)
    kv_w_b = w["kv_w_b"]
    q_w_a = w["q_w_a"].reshape(D_MODEL, Q_LORA)
    q_w_b = w["q_w_b"]
    w_o = w["attn_w_o"].reshape(D_MODEL, D_MODEL)
    kvb_tile = kv_w_b.shape[2]
    kvn_g = w["kv_norm_g"].reshape(1, KV_LORA)
    qn_g = w["q_norm_g"].reshape(1, Q_LORA)
    def kv_post(kva, g, cs, sn):
        tile = _rope_tile(kva[:, KV_LORA:], cs, sn)
        return kva, _rms(kva[:, :KV_LORA], g), _cat(tile, pltpu.roll(tile, HALF_ROPE, 1))
    kva, ckv, krope = mm(h2b, kv_w_a, epi=kv_post, extras=(kvn_g, cos_k, sin_k),
                         out_dtypes=(F32, (KV_LORA, BF16), (2 * LANES, BF16)), name="kv_a")
    kvb = mm(ckv, kv_w_b, n_dim=N_CHIPS * kvb_tile, tiles=(None, kvb_tile, KV_LORA), b_view=_b_cols, name="kv_b",
             out_dtypes=(BF16,))
    cq_raw, cq = mm(h2b, q_w_a, epi=lambda r, gq: (r, _rms(r, gq)), extras=(qn_g,), out_dtypes=(F32, BF16), name="q_a")

    def rope_and_scale(r, cs, sn):
        return (_cat(r[:, :Q_CHIP_NOPE], _rope_tile(r[:, Q_CHIP_NOPE:], cs, sn)) * Q_PRESCALE,)
    qro = mm(cq, q_w_b, n_dim=N_CHIPS * Q_CHIP, tiles=(None, Q_CHIP, Q_LORA), b_view=_b_cols, epi=rope_and_scale,
             extras=(cos_q, sin_q), out_dtypes=(BF16,), name="q_b")
    (o, lse), landed = attn_fwd(qro, kvb, krope, ride_for(second_ride))
    gathered(second_ride, landed)
    mix1, h3, h3b = mm(o, w_o, name="attn_out", **then_ln(h2, ("ln_mix_g", "ln_mix_b"), 1))
    f2pre, f2 = mlp_fwd(h3, h3b, 1, with_ln=False)
    def last_ln_loss_and_back(h, mix, gl, bl, t):
        e = _layer_norm(h, mix, gl, bl) - t
        dr, dg, db = _layer_norm_bwd(h, mix, gl, e * (1.0 / D_MODEL))
        return (dr, dr), (jnp.broadcast_to(jnp.sum(e * e), (1, LANES)), dg, db)
    dr4, dr4b, loss_acc, dg_f1, db_f1 = rowwise(
        last_ln_loss_and_back, (h3, f2, ln("ln_ffn_g", 1), ln("ln_ffn_b", 1), target),
        ((D_MODEL, F32), (D_MODEL, BF16)), accs=(LANES, D_MODEL, D_MODEL), name="ln_ffn_1_loss")
    loss = loss_acc[0, 0] * (0.5 / D_MODEL)

    g = {}

    def into_rows(off, rows_per_chip, shape=pack_shape):
        def view(tm, tn):
            if tm == N_CHIPS * rows_per_chip:
                return pl.BlockSpec((N_CHIPS, rows_per_chip, tn), lambda i, j, k: (0, off // rows_per_chip, 0))
            nb = rows_per_chip // tm
            return pl.BlockSpec((None, tm, tn), lambda i, j, k: (i // nb, off // tm + i % nb, 0))
        return shape, view

    def into_cols(off):
        return pack_shape, lambda tm, tn: pl.BlockSpec((None, tm, tn), lambda i, j, k: (j, off // tm + i, 0))

    def mlp_bwd(pack, dr, drb, hb, pre, layer, swap=False):
        w2_rows = (EARLY_OFF["w_ff2"] + layer * ff_tile, ff_tile)
        w1_rows = (EARLY_OFF["w_ff1"] + layer * D_MODEL, D_MODEL)
        ready = [(w1_rows[0] + w1_rows[1], w2_rows[0] - w1_rows[0] - w1_rows[1]), (w2_rows[0] + w2_rows[1], EARLY_ROWS - w2_rows[0] - w2_rows[1])]
        dpre = mm(drb, w["w_ff2"][layer].reshape(D_FF, D_MODEL), tb=True, epi=lambda r, p: (r * 2.0 * jnp.maximum(p, 0.0),),
                  extras=(pre,), out_dtypes=(BF16,), tiles=(None, ff_tile, None), name=f"ff2_dx_{layer}",
                  ride=SwapRide(pack, ready) if swap else None)
        if swap:
            dpre, (theirs,) = dpre
        pack = mm(pre, drb, ta=True, pro_a=_relu2, name=f"ff2_dw_{layer}", tiles=(ff_tile, PACK_W, None), into=pack,
                  out_view=into_rows(w2_rows[0], ff_tile))
        pack = mm(hb, dpre, ta=True, name=f"ff1_dw_{layer}", tiles=(None, PACK_W, None), into=pack,
                  out_view=into_cols(w1_rows[0]))
        dh = mm(dpre, w["w_ff1"][layer], tb=True, epi=lambda r, d: (r + DN_ALPHA * d,), extras=(dr,), n_dim=D_MODEL,
                tiles=(None, D_MODEL, ff_tile), b_view=_b_cols_t, name=f"ff1_dx_{layer}",
                ride=SwapRide(pack, [w1_rows, w2_rows], into=theirs) if swap else None)
        return (pack, *dh) if swap else (pack, dh)

    pack, dh3 = mlp_bwd(None, dr4, dr4b, h3b, f2pre, 1)
    dr3, dr3b, dg_m1, db_m1 = ln_bwd(h2, mix1, ln("ln_mix_g", 1), dh3, "ln_mix_bwd_1")
    shard_rows = D_MODEL // N_CHIPS
    pack = mm(o, dr3b, ta=True, name="attn_out_dw", tiles=(D_MODEL, PACK_W, None), into=pack,
              out_view=into_rows(EARLY_OFF["attn_w_o"], shard_rows))
    def head_dots(do, o):
        return do, jnp.concatenate([jnp.sum(do[:, V_HEAD * h:V_HEAD * (h + 1)] * o[:, V_HEAD * h:V_HEAD * (h + 1)], axis=1,
                                            keepdims=True) for h in range(N_HEADS)], axis=1)
    do, delta = mm(dr3b, w_o, tb=True, epi=head_dots, extras=(o,), out_dtypes=(F32, (N_HEADS, F32)), name="attn_out_dx")
    tb = min(ATT_TK, seq)
    lse_row = lse.reshape(N_HEADS, seq // tb, tb)
    delta_row = delta.T.reshape(N_HEADS, seq // tb, tb)
    dqn, dqr, dkvb, dkr = attn_bwd(qro, kvb, krope, do, lse_row, delta_row)

    def q_rope_bwd(dn, dr, cs, sn):
        parts = []
        for k in range(N_CHIPS):
            parts.append(dn[:, Q_CHIP_NOPE * k:Q_CHIP_NOPE * (k + 1)])
            parts.append(_rope_tile_bwd(dr[:, LANES * k:LANES * (k + 1)], cs, sn))
        return (jnp.concatenate(parts, axis=1),), ()
    (dqlin,) = rowwise(q_rope_bwd, (dqn, dqr, cos_q, sin_q), ((N_CHIPS * Q_CHIP, BF16),), name="q_rope_bwd")
    g["q_w_b"] = mm(cq, dqlin, ta=True, name="q_b_dw", tiles=(Q_LORA, Q_CHIP, None), out_view=_out_cols(q_w_b.shape))
    dcq_raw, dqn_g = mm(dqlin, q_w_b, tb=True, n_dim=Q_LORA, tiles=(None, Q_LORA, Q_CHIP), b_view=_b_cols_t,
                        epi=lambda d, c, gq: _rms_bwd(c, gq, d), extras=(cq_raw, qn_g), out_dtypes=(BF16,),
                        accs=(Q_LORA,), name="q_b_dx")
    g["q_w_a"] = mm(h2b, dcq_raw, ta=True, name="q_a_dw")
    g["kv_w_b"] = mm(ckv, dkvb, ta=True, name="kv_b_dw", tiles=(KV_LORA, kvb_tile, None), out_view=_out_cols(kv_w_b.shape))
    dkr_sum = head_sum(dkr)

    def kv_post_bwd(dc, kva, gk, dk, cs, sn):
        dx, dgk = _rms_bwd(kva[:, :KV_LORA], gk, dc)
        dk = dk + pltpu.roll(dk, LANES - HALF_ROPE, 1)
        return jnp.concatenate([dx, _rope_tile_bwd(dk, cs, sn)], axis=1), dgk
    dkva, dkvn_g = mm(dkvb, kv_w_b, tb=True, n_dim=KV_LORA, tiles=(None, KV_LORA, kvb_tile), b_view=_b_cols_t,
                      epi=kv_post_bwd, extras=(kva, kvn_g, dkr_sum, cos_k, sin_k), out_dtypes=((KVA_PAD, BF16),),
                      accs=(KV_LORA,), name="kv_b_dx")
    g["kv_w_a"] = mm(h2b, dkva, ta=True, name="kv_a_dw")

    def ln_ffn_bwd(r, dc, wq, d, h, f, gl):
        via_q = lax.dot_general(dc, wq, (((1,), (1,)), ((), ())), preferred_element_type=F32)
        dr, dg, db = _layer_norm_bwd(h, f, gl, r + (via_q + DN_ALPHA * d))
        return dr, dr, dg, db
    dr2, dr2b, dg_f0, db_f0 = mm(dkva, kv_w_a, tb=True, epi=ln_ffn_bwd,
                                 extras=(dcq_raw, q_w_a.astype(BF16), dr3, h1, f1, ln("ln_ffn_g", 0)),
                                 out_dtypes=(F32, BF16), accs=(D_MODEL, D_MODEL), name="qkv_a_dx")
    pack = put_rows(pack, packed_shards(g, MISC_EARLY, EARLY_ROWS - MISC_EARLY_OFF), MISC_EARLY_OFF)
    if comm is None:
        pack, dh1 = mlp_bwd(pack, dr2, dr2b, h1b, f1pre, 0)
    else:
        pack, dh1, (theirs,) = mlp_bwd(pack, dr2, dr2b, h1b, f1pre, 0, swap=True)
        early_sums = add_halves(pack, theirs, comm[1])
    dr1, dr1b, dg_m0, db_m0 = ln_bwd(x, mix0, ln("ln_mix_g", 0), dh1, "ln_mix_bwd_0")
    mid = mm(z, dr1b, ta=True, name="ssm_out_dw", tiles=(D_MODEL, PACK_W, None),
             out_view=into_rows(MID_OFF["ssm_w_out"], shard_rows, (N_CHIPS, MID_ROWS, PACK_W)))
    def glu_bwd(dz, vl, gt):
        sg = _sigmoid(gt)
        return (jnp.concatenate([dz * sg, dz * vl * sg * (1.0 - sg)], axis=1),)
    dvg = mm(dr1b, w_out, tb=True, epi=glu_bwd, extras=(val, gate), out_dtypes=((2 * D_MODEL, BF16),), name="ssm_out_dx")
    g["ssm_w_glu"] = mm(yg, dvg, ta=True, name="glu_proj_dw", tiles=(None, glu_tile, None), out_view=_out_cols(w_glu.shape))
    mid = put_rows(mid, packed_shards(g, MISC_MID, MID_ROWS - MISC_MID_OFF), MISC_MID_OFF)
    dypre = mm(dvg, w_glu, tb=True, epi=lambda r, y: (r * _gelu_grad(y),), extras=(ypre,), n_dim=D_MODEL,
               tiles=(None, D_MODEL, glu_tile), b_view=_b_cols_t, name="glu_proj_dx",
               ride=Together([SwapRide(mid), SendRide([(early_sums, (0, EARLY_HEAD), None)])]) if comm is not None else None)
    sends = None
    if comm is not None:
        dypre, (theirs, early_got) = dypre
        sends = SendRide([(early_sums, (EARLY_HEAD, EARLY_ROWS - EARLY_HEAD), early_got), add_halves(mid, theirs, comm[1])])
    (dx, dbbd_re, dbbd_im, dcbd_re, dcbd_imn, dar, dai, dd), got = s5_bwd(
        dypre, x, dr1, h_re, h_im, bbd_re, bbd_im, cbd_re, cbd_imn, a_re, a_im, dskip, sends)
    dbb_re = _blockdiag_in_t(dbbd_re).reshape(N_STATES, SSM_GROUP)
    dbb_im = _blockdiag_in_t(dbbd_im).reshape(N_STATES, SSM_GROUP)
    dlr, dli, dldt, db_re, db_im = s5_prep_bwd(lr, li, ldt, b_re, b_im, dar.reshape(N_STATES, 1),
                                               dai.reshape(N_STATES, 1), dbb_re, dbb_im)
    g["ssm_lam_re"] = dlr.reshape(1, N_GROUPS, SSM_STATE)
    g["ssm_lam_im"] = dli.reshape(1, N_GROUPS, SSM_STATE)
    g["ssm_log_dt"] = group_sum(dldt).reshape(1, N_GROUPS)
    g["ssm_b_re"] = db_re.reshape(1, N_GROUPS, SSM_STATE, SSM_GROUP)
    g["ssm_b_im"] = db_im.reshape(1, N_GROUPS, SSM_STATE, SSM_GROUP)
    g["ssm_c_re"] = _blockdiag_out_t(dcbd_re).reshape(1, N_GROUPS, SSM_GROUP, SSM_STATE)
    g["ssm_c_im"] = -_blockdiag_out_t(dcbd_imn).reshape(1, N_GROUPS, SSM_GROUP, SSM_STATE)
    g["ssm_d"] = dd
    g["ln_mix_g"] = jnp.concatenate([dg_m0, dg_m1], 0)
    g["ln_mix_b"] = jnp.concatenate([db_m0, db_m1], 0)
    g["ln_ffn_g"] = jnp.concatenate([dg_f0, dg_f1], 0)
    g["ln_ffn_b"] = jnp.concatenate([db_f0, db_f1], 0)
    g["kv_norm_g"] = dkvn_g.reshape(KV_LORA)
    g["q_norm_g"] = dqn_g
    return loss, dx, pack, mid, g, list(zip(sends.ins, got)) if comm is not None else None


def place(shard, me_idx, dtype, name, layer=None):
    rows, cols = shard.shape[-2:]
    tr = _tile(rows, (512, 256, 128))

    def body(m_ref, x_ref, o_ref):
        o_ref[...] = x_ref[...].astype(o_ref.dtype)

    in_spec = (pl.BlockSpec((tr, cols), lambda i, m: (i, 0)) if layer is None
               else pl.BlockSpec((None, tr, cols), lambda i, m: (layer, i, 0)))
    return _pcall(
        body, name=name,
        grid_spec=pltpu.PrefetchScalarGridSpec(
            num_scalar_prefetch=1, grid=(rows // tr,), in_specs=[in_spec],
            out_specs=pl.BlockSpec((None, tr, cols), lambda i, m: (m[0], i, 0))),
        out_shape=jax.ShapeDtypeStruct((N_CHIPS, rows, cols), dtype),
        compiler_params=_params(("parallel",)),
    )(me_idx, shard)


def place_many(shards, dtypes, me_idx, name):
    def body(m_ref, *refs):
        for x_ref, o_ref in zip(refs[:len(shards)], refs[len(shards):]):
            o_ref[...] = x_ref[...].astype(o_ref.dtype)

    return _pcall(
        body, name=name,
        grid_spec=pltpu.PrefetchScalarGridSpec(
            num_scalar_prefetch=1, grid=(1,),
            in_specs=[pl.BlockSpec(s.shape, lambda i, m: (0, 0)) for s in shards],
            out_specs=[pl.BlockSpec((None,) + s.shape, lambda i, m: (m[0], 0, 0)) for s in shards]),
        out_shape=[jax.ShapeDtypeStruct((N_CHIPS,) + s.shape, d) for s, d in zip(shards, dtypes)],
        compiler_params=_params(("arbitrary",)),
    )(me_idx, *shards)


def put_rows(pack, rows, off):
    _, n, cols = rows.shape

    def body(r_ref, p_ref, o_ref, sem):
        cp = pltpu.make_async_copy(r_ref.at[0], o_ref.at[pl.program_id(0), pl.ds(off, n), :], sem)
        cp.start()
        cp.wait()

    return _pcall(body, name="grad_put_rows", grid=(N_CHIPS,),
                  in_specs=[pl.BlockSpec((1, n, cols), lambda k: (k, 0, 0)), _ANY], out_specs=_ANY,
                  out_shape=jax.ShapeDtypeStruct(pack.shape, pack.dtype), input_output_aliases={1: 0},
                  scratch_shapes=[pltpu.SemaphoreType.DMA],
                  compiler_params=_params(("arbitrary",)))(rows, pack)


def _my_cols(c, mine=True):
    start = (c if mine else 1 - c) * HALF_W
    return pl.ds(pl.multiple_of(start, HALF_W), HALF_W)


def add_halves(gpack, got, c_idx):
    n, rows, _ = gpack.shape
    tr = min(G_BLOCK_ROWS, rows)
    blk = (None, tr, HALF_W)

    def body(c_ref, g_ref, r_ref, o_ref):
        o_ref[...] = (g_ref[...] + r_ref[...]).astype(o_ref.dtype)

    return _pcall(
        body, name="grad_add_halves",
        grid_spec=pltpu.PrefetchScalarGridSpec(
            num_scalar_prefetch=1, grid=(n, rows // tr),
            in_specs=[pl.BlockSpec(blk, lambda k, i, c: (k, i, c[0])), pl.BlockSpec(blk, lambda k, i, c: (k, i, 0))],
            out_specs=pl.BlockSpec(blk, lambda k, i, c: (k, i, 0))),
        out_shape=jax.ShapeDtypeStruct((n, rows, HALF_W), BF16),
        compiler_params=_params(("parallel", "parallel")),
    )(c_idx, gpack, got)


def sum_owner(part, got, idx, total_rows, row_off=0, into=None):
    _, rows, _ = part.shape
    tr = math.gcd(math.gcd(rows, row_off), G_BLOCK_ROWS)
    n_into = 0 if into is None else 1

    def body(m_ref, p_ref, g_ref, *rest):
        up = lambda v: v.astype(F32)
        rest[-1][...] = ((up(p_ref[...]) + up(g_ref[0])) + up(g_ref[1])) + up(g_ref[2])

    return _pcall(
        body, name="grad_sum_owner",
        grid_spec=pltpu.PrefetchScalarGridSpec(
            num_scalar_prefetch=1, grid=(rows // tr,),
            in_specs=[pl.BlockSpec((None, tr, HALF_W), lambda i, m: (m[0], i, 0)),
                      pl.BlockSpec((3, tr, HALF_W), lambda i, m: (0, i, 0))] + [_ANY] * n_into,
            out_specs=pl.BlockSpec((tr, HALF_W), lambda i, m: (row_off // tr + i, m[1]))),
        out_shape=jax.ShapeDtypeStruct((total_rows, PACK_W), F32),
        input_output_aliases={3: 0} if n_into else {},
        compiler_params=_params(("parallel",)),
    )(idx, part, got, *([into] if n_into else []))


def join_halves(red):
    def body(in_ref, out_ref, send_sem, recv_sem):
        x, y, c, _ = _place()
        sibling = (x, y, 1 - c)
        mine = out_ref.at[:, _my_cols(c)]
        cp = pltpu.make_async_remote_copy(src_ref=mine, dst_ref=mine, send_sem=send_sem, recv_sem=recv_sem,
                                          device_id=sibling, device_id_type=MESH)
        cp.start()
        cp.wait_send()
        other = out_ref.at[:, _my_cols(c, mine=False)]
        pltpu.make_async_remote_copy(src_ref=other, dst_ref=other, send_sem=send_sem, recv_sem=recv_sem,
                                     device_id=sibling, device_id_type=MESH).wait_recv()

    return _pcall(body, name="grad_join_halves", in_specs=[_ANY], out_specs=_ANY,
                  out_shape=jax.ShapeDtypeStruct(red.shape, red.dtype), input_output_aliases={0: 0},
                  scratch_shapes=[pltpu.SemaphoreType.DMA, pltpu.SemaphoreType.DMA])(red)


def adamw(gsrc, g_off, wt, m, v, name):
    n, cols = wt.shape
    tr = math.gcd(math.gcd(g_off, n), 256) if g_off else math.gcd(n, 256)
    off_blk = g_off // tr
    c1 = 1.0 / (1.0 - ADAM_B1 ** ADAM_STEP)
    c2 = 1.0 / (1.0 - ADAM_B2 ** ADAM_STEP)

    def body(g_ref, w_ref, m_ref, v_ref, go_ref, d_ref, mo_ref, vo_ref):
        gv = g_ref[...]
        mn = ADAM_B1 * m_ref[...] + (1.0 - ADAM_B1) * gv
        vn = ADAM_B2 * v_ref[...] + (1.0 - ADAM_B2) * gv * gv
        go_ref[...] = gv
        mo_ref[...] = mn
        vo_ref[...] = vn
        d_ref[...] = -ADAM_LR * ((mn * c1) / (jnp.sqrt(vn * c2) + ADAM_EPS) + ADAM_WD * w_ref[...])

    blk = pl.BlockSpec((tr, cols), lambda i: (i, 0))
    return _pcall(body, name=name, grid=(n // tr,),
                  in_specs=[pl.BlockSpec((tr, cols), lambda i: (off_blk + i, 0)), blk, blk, blk],
                  out_specs=[blk] * 4, out_shape=[jax.ShapeDtypeStruct((n, cols), F32)] * 4,
                  compiler_params=_params(("parallel",)))(gsrc, wt, m, v)


def _rows8(a):
    return -(-a.size // (8 * PACK_W)) * 8


def _as_rows(a, rows=None):
    flat = a.reshape(-1)
    n = _rows8(a) if rows is None else rows
    return jnp.pad(flat, (0, n * PACK_W - flat.shape[0])).reshape(n, PACK_W)


def local_shards_2d(wl):
    return {"w_ff1": [wl["w_ff1"][0], wl["w_ff1"][1]], "w_ff2": [wl["w_ff2"][0], wl["w_ff2"][1]],
            "ssm_w_glu": wl["ssm_w_glu"], "ssm_w_out": wl["ssm_w_out"], "kv_w_a": _pad_kva_cols(wl["kv_w_a"]),
            "kv_w_b": wl["kv_w_b"], "q_w_a": wl["q_w_a"], "q_w_b": _perm_q_cols(wl["q_w_b"]),
            "attn_w_o": wl["attn_w_o"], "ssm_d": wl["ssm_d"].reshape(2, -1)}


def misc_grad_shard(name, g, k):
    if name == "ssm_d":
        w = D_MODEL // N_CHIPS
        return g[:, w * k:w * (k + 1)]
    if name in ("ssm_w_glu", "kv_w_b"):
        return g[k]
    if name == "q_w_b":
        return _unperm_q_cols(g[k])
    rows = D_MODEL // N_CHIPS
    shard = g[rows * k:rows * (k + 1)]
    return _unpad_kva_cols(shard) if name == "kv_w_a" else shard


def packed_shards(g, names, rows, tail=None):
    blocks = []
    for k in range(N_CHIPS):
        parts = [_as_rows(misc_grad_shard(n, g[n], k), MISC_SHARD_ROWS[n]) for n in names]
        if tail is not None:
            parts.append(tail[k * (tail.shape[0] // N_CHIPS):(k + 1) * (tail.shape[0] // N_CHIPS)])
        blk = jnp.concatenate(parts, axis=0)
        blocks.append(jnp.pad(blk, ((0, rows - blk.shape[0]), (0, 0))))
    return jnp.stack(blocks)


def kernel(x, positions, ln_mix_g, ln_mix_b, ln_ffn_g, ln_ffn_b, w_ff1, w_ff2, ssm_lam_re, ssm_lam_im, ssm_log_dt, ssm_b_re, ssm_b_im, ssm_c_re, ssm_c_im, ssm_d, ssm_w_glu, ssm_w_out, kv_w_a, kv_norm_g, kv_w_b, q_w_a, q_norm_g, q_w_b, attn_w_o, loss_target, m_ln_mix_g, m_ln_mix_b, m_ln_ffn_g, m_ln_ffn_b, m_w_ff1, m_w_ff2, m_ssm_lam_re, m_ssm_lam_im, m_ssm_log_dt, m_ssm_b_re, m_ssm_b_im, m_ssm_c_re, m_ssm_c_im, m_ssm_d, m_ssm_w_glu, m_ssm_w_out, m_kv_w_a, m_kv_norm_g, m_kv_w_b, m_q_w_a, m_q_norm_g, m_q_w_b, m_attn_w_o, v_ln_mix_g, v_ln_mix_b, v_ln_ffn_g, v_ln_ffn_b, v_w_ff1, v_w_ff2, v_ssm_lam_re, v_ssm_lam_im, v_ssm_log_dt, v_ssm_b_re, v_ssm_b_im, v_ssm_c_re, v_ssm_c_im, v_ssm_d, v_ssm_w_glu, v_ssm_w_out, v_kv_w_a, v_kv_norm_g, v_kv_w_b, v_q_w_a, v_q_norm_g, v_q_w_b, v_attn_w_o):
    env = dict(locals())
    wl = {n: env[n] for n in WEIGHTS}
    ml = {n: env["m_" + n] for n in WEIGHTS}
    vl = {n: env["v_" + n] for n in WEIGHTS}
    for n in ("ssm_w_glu", "ssm_w_out", "q_w_a", "q_w_b", "attn_w_o"):
        wl[n], ml[n], vl[n] = wl[n][0], ml[n][0], vl[n][0]

    c_idx = lax.axis_index("c").astype(jnp.int32).reshape(1)
    me_idx = (2 * lax.axis_index("x") + lax.axis_index("y")).astype(jnp.int32).reshape(1)

    local = local_shards_2d(wl)
    stacked = {n: [place(wl[n], me_idx, BF16, f"place_{n}_{l}", layer=l) for l in range(DEPTH)] for n in ("w_ff1", "w_ff2")}
    others = [n for n in SHARDED if n not in stacked]
    stacked.update(zip(others, place_many([local[n] for n in others], [F32 if n == "ssm_d" else BF16 for n in others],
                                          me_idx, "place_others")))
    stacked["ssm_d"] = ride_alone(GatherRide([_halves(stacked["ssm_d"])]), "ssm_d_all_gather")[0].reshape(1, D_MODEL)
    for n in REPLICATED:
        stacked[n] = wl[n]

    loss_part, dx, early, mid, g, sent = device_step(x[0], positions[0], loss_target[0], stacked, comm=(me_idx, c_idx))
    loss = lax.psum(loss_part, ("x", "y", "c"))

    small = jnp.concatenate([_as_rows(g[n]) for n in REPLICATED], axis=0)
    small = jnp.pad(small, ((0, SMALL_ROWS - small.shape[0]), (0, 0)))
    late = packed_shards(g, MISC_LATE, LATE_ROWS, tail=small)
    late_sums = add_halves(late, ride_alone(SwapRide(late), "grad_swap_halves")[0], c_idx)
    sent.append((late_sums, ride_alone(SendRide([late_sums]), "grad_send_to_owners")[0]))
    where = jnp.concatenate([me_idx, c_idx])
    starts = (0, EARLY_ROWS, EARLY_ROWS + MID_ROWS)
    total_rows = EARLY_ROWS + MID_ROWS + LATE_ROWS
    reduced = None
    for (sums, got), off in zip(sent, starts):
        reduced = sum_owner(sums, got, where, total_rows, row_off=off, into=reduced)
    reduced = join_halves(reduced)
    quarter = reduced[starts[2] + SMALL_OFF:starts[2] + SMALL_OFF + SMALL_Q_ROWS]
    small_tot = ride_alone(GatherRide([_halves(place(quarter, me_idx, F32, "place_small_grads"))]),
                           "small_grad_all_gather")[0].reshape(SMALL_ROWS, PACK_W)

    out_g, out_d, out_m, out_v = {}, {}, {}, {}
    direct = {**EARLY_OFF, **{n: starts[1] + o for n, o in MID_OFF.items()}}
    for n, off in direct.items():
        res = adamw(reduced, off, wl[n].reshape(-1, PACK_W), ml[n].reshape(-1, PACK_W), vl[n].reshape(-1, PACK_W),
                    "adamw_" + n)
        out_g[n], out_d[n], out_m[n], out_v[n] = [a.reshape(env[n].shape) for a in res]
    for names, off in ((MISC_EARLY, MISC_EARLY_OFF), (MISC_MID, starts[1] + MISC_MID_OFF), (MISC_LATE, starts[2])):
        pack3 = lambda d: jnp.concatenate([_as_rows(d[n], MISC_SHARD_ROWS[n]) for n in names], axis=0)
        res = adamw(reduced, off, pack3(wl), pack3(ml), pack3(vl), "adamw_packed_" + names[0])
        r0 = 0
        for n in names:
            cnt = math.prod(env[n].shape)
            out_g[n], out_d[n], out_m[n], out_v[n] = [
                a[r0:r0 + MISC_SHARD_ROWS[n]].reshape(-1)[:cnt].reshape(env[n].shape) for a in res]
            r0 += MISC_SHARD_ROWS[n]
    ws = jnp.concatenate([_as_rows(wl[n]) for n in REPLICATED], axis=0)
    ms = jnp.concatenate([_as_rows(ml[n]) for n in REPLICATED], axis=0)
    vs = jnp.concatenate([_as_rows(vl[n]) for n in REPLICATED], axis=0)
    pad = ((0, SMALL_ROWS - ws.shape[0]), (0, 0))
    res = adamw(small_tot, 0, jnp.pad(ws, pad), jnp.pad(ms, pad), jnp.pad(vs, pad), "adamw_replicated")
    row = 0
    for n in REPLICATED:
        cnt = math.prod(env[n].shape)
        nrows = _rows8(env[n])
        out_g[n], out_d[n], out_m[n], out_v[n] = [a[row:row + nrows].reshape(-1)[:cnt].reshape(env[n].shape) for a in res]
        row += nrows

    return (loss, dx[None], *[out_g[n] for n in WEIGHTS], *[out_d[n] for n in WEIGHTS],
            *[out_m[n] for n in WEIGHTS], *[out_v[n] for n in WEIGHTS])
```

```python
import functools
import math

import jax
import jax.numpy as jnp
from jax import lax
from jax.experimental import pallas as pl
from jax.experimental.pallas import tpu as pltpu

F32 = jnp.float32
BF16 = jnp.bfloat16
MESH = pl.DeviceIdType.MESH

D_MODEL = 1024
DEPTH = 2
SSM_GROUP = 16
N_GROUPS = D_MODEL // SSM_GROUP
SSM_STATE = 64
N_STATES = N_GROUPS * SSM_STATE
N_HEADS = 8
QK_NOPE = 128
QK_ROPE = 64
HALF_ROPE = QK_ROPE // 2
V_HEAD = 128
QK_DIM = QK_NOPE + QK_ROPE
Q_LORA = 384
KV_LORA = 256
ROPE_THETA = 10000.0
SM_SCALE = QK_DIM ** -0.5
NEG_INF = -1e30
D_FF = 4 * D_MODEL
DN_ALPHA = (2 * DEPTH) ** 0.25
LN_EPS = 1e-5
RMS_EPS = 1e-6
ADAM_LR = 0.001
ADAM_B1 = 0.9
ADAM_B2 = 0.999
ADAM_EPS = 1e-08
ADAM_WD = 0.01
ADAM_STEP = 10

N_CHIPS = 4
LANES = 128
VMEM_LIMIT = 56 * 1024 * 1024
MM_VMEM_BUDGET = 40 * 1024 * 1024
PACK_W = 1024
KVA_PAD = 384
HALF_W = PACK_W // 2

SHARDED = ("w_ff1", "w_ff2", "ssm_w_glu", "ssm_w_out", "kv_w_a", "kv_w_b", "q_w_a", "q_w_b", "attn_w_o", "ssm_d")
G_BLOCK_ROWS = 960
EARLY_OFF = {"w_ff1": 0, "w_ff2": 2048, "attn_w_o": 4096}
MISC_EARLY = ("kv_w_b", "kv_w_a", "q_w_a", "q_w_b")
MISC_EARLY_OFF = 4352
EARLY_ROWS = 5 * G_BLOCK_ROWS
EARLY_HEAD = G_BLOCK_ROWS
MID_OFF = {"ssm_w_out": 0}
MISC_MID = ("ssm_w_glu",)
MISC_MID_OFF = 256
MID_ROWS = MISC_MID_OFF + 512
MISC_LATE = ("ssm_d",)
SMALL_Q_ROWS = 96
SMALL_ROWS = N_CHIPS * SMALL_Q_ROWS
SMALL_OFF = 16
LATE_ROWS = 192
MISC_SHARD_ROWS = {"ssm_d": 16, "ssm_w_glu": 512, "kv_w_b": 128, "kv_w_a": 80, "q_w_a": 96, "q_w_b": 144}
REPLICATED = ("ln_mix_g", "ln_mix_b", "ln_ffn_g", "ln_ffn_b", "ssm_lam_re", "ssm_lam_im", "ssm_log_dt",
              "ssm_b_re", "ssm_b_im", "ssm_c_re", "ssm_c_im", "kv_norm_g", "q_norm_g")
WEIGHTS = ("ln_mix_g", "ln_mix_b", "ln_ffn_g", "ln_ffn_b", "w_ff1", "w_ff2", "ssm_lam_re", "ssm_lam_im",
           "ssm_log_dt", "ssm_b_re", "ssm_b_im", "ssm_c_re", "ssm_c_im", "ssm_d", "ssm_w_glu", "ssm_w_out",
           "kv_w_a", "kv_norm_g", "kv_w_b", "q_w_a", "q_norm_g", "q_w_b", "attn_w_o")


def _pcall(body, **kw):
    return pl.pallas_call(body, **kw)


def _params(sem=None):
    return pltpu.CompilerParams(dimension_semantics=sem, vmem_limit_bytes=VMEM_LIMIT)


_ANY = pl.BlockSpec(memory_space=pl.ANY)


def _tile(dim, prefs):
    for p in prefs:
        if dim % p == 0:
            return p
    return dim


def _place():
    x, y, c = lax.axis_index("x"), lax.axis_index("y"), lax.axis_index("c")
    return x, y, c, [(1 - x, y), (x, 1 - y), (1 - x, 1 - y)]


def _remote(k, src, dst, to, send_sems, recv_sems):
    return pltpu.make_async_remote_copy(src_ref=src, dst_ref=dst, send_sem=send_sems.at[k], recv_sem=recv_sems.at[k],
                                        device_id=to, device_id_type=MESH)


class GatherRide:
    def __init__(self, arrs):
        self.ins = list(arrs)
        self.out_shapes = [jax.ShapeDtypeStruct(a.shape, a.dtype) for a in arrs]
        self.aliases = {i: i for i in range(len(arrs))}
        self.n_sems = 6 * len(arrs)

    def start(self, ins, outs, send_sems, recv_sems):
        x, y, c, chips = _place()
        me = 2 * x + y
        for a, o in enumerate(outs):
            for j, (px, py) in enumerate(chips):
                _remote(6 * a + j, o.at[me, c], o.at[me, c], (px, py, c), send_sems, recv_sems).start()

    def pass_on(self, ins, outs, send_sems, recv_sems):
        x, y, c, chips = _place()
        for a, o in enumerate(outs):
            for j, (px, py) in enumerate(chips):
                blk = o.at[2 * px + py, c]
                _remote(6 * a + j, blk, blk, (px, py, c), send_sems, recv_sems).wait_recv()
                _remote(6 * a + 3 + j, blk, blk, (x, y, 1 - c), send_sems, recv_sems).start()

    def finish(self, ins, outs, send_sems, recv_sems, passed_on=False):
        if not passed_on:
            self.pass_on(ins, outs, send_sems, recv_sems)
        x, y, c, chips = _place()
        me = 2 * x + y
        sibling = (x, y, 1 - c)
        for a, o in enumerate(outs):
            for j, (px, py) in enumerate(chips):
                blk = o.at[2 * px + py, 1 - c]
                _remote(6 * a + 3 + j, blk, blk, sibling, send_sems, recv_sems).wait_recv()
                _remote(6 * a + j, o.at[me, c], o.at[me, c], (px, py, c), send_sems, recv_sems).wait_send()
                mine = o.at[2 * px + py, c]
                _remote(6 * a + 3 + j, mine, mine, sibling, send_sems, recv_sems).wait_send()


class SendRide:
    base = 0

    def __init__(self, parts):
        parts = [p if isinstance(p, tuple) else (p, (0, p.shape[1]), None) for p in parts]
        self.rows = [rows for _, rows, _ in parts]
        self.n_parts = len(parts)
        self.ins = [p for p, _, _ in parts] + [into for _, _, into in parts if into is not None]
        self.out_shapes = [jax.ShapeDtypeStruct((3,) + p.shape[1:], p.dtype) for p, _, _ in parts]
        given = [a for a, (_, _, into) in enumerate(parts) if into is not None]
        self.aliases = {self.n_parts + i: a for i, a in enumerate(given)}
        self.n_sems = 3 * self.n_parts

    def _copies(self, ins, outs, send_sems, recv_sems):
        x, y, c, chips = _place()
        return [_remote(self.base + 3 * a + j, ins[a].at[2 * px + py, pl.ds(r0, n)], outs[a].at[j, pl.ds(r0, n)],
                        (px, py, c), send_sems, recv_sems)
                for a, (r0, n) in enumerate(self.rows) for j, (px, py) in enumerate(chips)]

    def start(self, ins, outs, send_sems, recv_sems):
        for cp in self._copies(ins, outs, send_sems, recv_sems):
            cp.start()

    def finish(self, ins, outs, send_sems, recv_sems):
        for cp in self._copies(ins, outs, send_sems, recv_sems):
            cp.wait()


class SwapRide:
    base = 0

    def __init__(self, pack, ranges=None, into=None):
        self.ins = [pack] if into is None else [pack, into]
        self.out_shapes = [jax.ShapeDtypeStruct(pack.shape[:2] + (HALF_W,), pack.dtype)]
        self.aliases = {} if into is None else {1: 0}
        self.ranges = ranges or [(0, pack.shape[1])]
        self.n_sems = len(self.ranges)

    def _copies(self, ins, outs, send_sems, recv_sems):
        x, y, c, _ = _place()
        return [_remote(self.base + k, ins[0].at[:, pl.ds(r0, n), _my_cols(c, mine=False)], outs[0].at[:, pl.ds(r0, n), :],
                        (x, y, 1 - c), send_sems, recv_sems) for k, (r0, n) in enumerate(self.ranges)]

    def start(self, ins, outs, send_sems, recv_sems):
        for cp in self._copies(ins, outs, send_sems, recv_sems):
            cp.start()

    def finish(self, ins, outs, send_sems, recv_sems):
        for cp in self._copies(ins, outs, send_sems, recv_sems):
            cp.wait()


class Together:
    def __init__(self, rides):
        self.rides = rides
        self.ins, self.out_shapes, self.aliases, self.n_sems = [], [], {}, 0
        for r in rides:
            r.base = self.n_sems
            self.aliases.update({len(self.ins) + i: len(self.out_shapes) + o for i, o in r.aliases.items()})
            self.ins += r.ins
            self.out_shapes += r.out_shapes
            self.n_sems += r.n_sems

    def _each(self, step, ins, outs, send_sems, recv_sems):
        i = o = 0
        for r in self.rides:
            getattr(r, step)(ins[i:i + len(r.ins)], outs[o:o + len(r.out_shapes)], send_sems, recv_sems)
            i, o = i + len(r.ins), o + len(r.out_shapes)

    def start(self, *refs):
        self._each("start", *refs)

    def finish(self, *refs):
        self._each("finish", *refs)


def _pcall_riding(body, args, ride, first, last, *, in_specs, out_specs, out_shape, scratch_shapes=(), middle=None,
                  in_place=None, **kw):
    n_in, n_out = len(args), len(out_shape)
    in_place = in_place or {}
    if ride is None:
        return _pcall(body, in_specs=in_specs, out_specs=out_specs, out_shape=out_shape,
                      input_output_aliases=in_place, scratch_shapes=list(scratch_shapes), **kw)(*args), []
    k_in, k_out = len(ride.ins), len(ride.out_shapes)

    def riding(*refs):
        ins, r_in = refs[:n_in], refs[n_in:n_in + k_in]
        outs = refs[n_in + k_in:n_in + k_in + n_out]
        r_out = refs[n_in + k_in + n_out:n_in + k_in + n_out + k_out]
        scratch, (send_sems, recv_sems) = refs[n_in + k_in + n_out + k_out:-2], refs[-2:]

        @pl.when(first())
        def _():
            ride.start(r_in, r_out, send_sems, recv_sems)

        if middle is not None:
            @pl.when(middle())
            def _():
                ride.pass_on(r_in, r_out, send_sems, recv_sems)

        body(*ins, *outs, *scratch)

        @pl.when(last())
        def _():
            if middle is not None:
                ride.finish(r_in, r_out, send_sems, recv_sems, passed_on=True)
            else:
                ride.finish(r_in, r_out, send_sems, recv_sems)

    res = _pcall(riding, in_specs=list(in_specs) + [_ANY] * k_in, out_specs=list(out_specs) + [_ANY] * k_out,
                 out_shape=list(out_shape) + ride.out_shapes,
                 input_output_aliases={**in_place, **{n_in + i: n_out + o for i, o in ride.aliases.items()}},
                 scratch_shapes=list(scratch_shapes) + [pltpu.SemaphoreType.DMA((ride.n_sems,))] * 2,
                 **kw)(*args, *ride.ins)
    return res[:n_out], res[n_out:]


def ride_alone(ride, name):
    def body(*refs):
        n = len(ride.ins)
        ins, outs, (send_sems, recv_sems) = refs[:n], refs[n:-2], refs[-2:]
        ride.start(ins, outs, send_sems, recv_sems)
        ride.finish(ins, outs, send_sems, recv_sems)

    return _pcall(body, name=name, in_specs=[_ANY] * len(ride.ins), out_specs=[_ANY] * len(ride.out_shapes),
                  out_shape=ride.out_shapes, input_output_aliases=dict(ride.aliases),
                  scratch_shapes=[pltpu.SemaphoreType.DMA((ride.n_sems,))] * 2)(*ride.ins)


def mm(a, b, *, name, ta=False, tb=False, pro_a=None, epi=None, extras=(), out_dtypes=(F32,), n_dim=None,
       tiles=(None, None, None), b_view=None, out_view=None, into=None, ride=None, accs=()):
    widths = [d[0] if isinstance(d, tuple) else None for d in out_dtypes]
    out_dtypes = [d[1] if isinstance(d, tuple) else d for d in out_dtypes]
    if ta:
        k_dim, m_dim = a.shape
    else:
        m_dim, k_dim = a.shape
    if n_dim is None:
        n_dim = b.shape[0] if tb else b.shape[1]
    tn = tiles[1] or (n_dim if n_dim <= 1024 else _tile(n_dim, (1024, 512, 256, 128)))
    tk = tiles[2] or (k_dim if k_dim <= 1024 else _tile(k_dim, (1024, 512, 256, 128)))
    nk = k_dim // tk

    def vmem_bytes(tm):
        blocks = tm * tk * a.dtype.itemsize + tk * tn * b.dtype.itemsize
        blocks += sum(tm * (tn if e.shape[1] == n_dim else e.shape[1]) * e.dtype.itemsize for e in extras if e.shape[0] > 1)
        blocks += tm * sum((w or tn) * jnp.dtype(d).itemsize for w, d in zip(widths, out_dtypes))
        return 2 * blocks + tm * tn * 4

    tm = tiles[0] or next((t for t in (4096, 2048, 1024, 512, 256) if m_dim % t == 0 and vmem_bytes(t) <= MM_VMEM_BUDGET),
                          _tile(m_dim, (128,)))
    assert m_dim % tm == 0 and n_dim % tn == 0 and k_dim % tk == 0, (name, m_dim, n_dim, k_dim, tm, tn, tk)
    assert tn == n_dim or not (any(widths) or accs), name
    n_ex, n_out = len(extras), len(out_dtypes)
    n_into = 0 if into is None else 1
    dims = (((0 if ta else 1,), (1 if tb else 0,)), ((), ()))

    def body(a_ref, b_ref, *rest):
        ex_refs, out_refs = rest[:n_ex], rest[n_ex + n_into:n_ex + n_into + n_out]
        sum_refs = rest[n_ex + n_into + n_out:n_ex + n_into + n_out + len(accs)]

        def partial():
            av = a_ref[...]
            if pro_a is not None:
                av = pro_a(av)
            return lax.dot_general(av.astype(BF16), b_ref[...].astype(BF16), dims, preferred_element_type=F32)

        def finish(r):
            res = epi(r, *[e[...] for e in ex_refs]) if epi is not None else (r,)
            for o_ref, v in zip(out_refs, res):
                o_ref[...] = v.reshape(o_ref.shape).astype(o_ref.dtype)
            if accs:
                @pl.when(pl.program_id(0) == 0)
                def _():
                    for s_ref in sum_refs:
                        s_ref[...] = jnp.zeros_like(s_ref)

                for s_ref, v in zip(sum_refs, res[n_out:]):
                    s_ref[...] += v

        if nk == 1:
            finish(partial())
            return
        acc = rest[-1]
        k = pl.program_id(2)

        @pl.when(k == 0)
        def _():
            acc[...] = partial()

        @pl.when(k > 0)
        def _():
            acc[...] += partial()

        @pl.when(k == nk - 1)
        def _():
            finish(acc[...])

    def ex_spec(e):
        if e.shape == (m_dim, n_dim):
            return o_spec
        if e.shape[0] == m_dim:
            return pl.BlockSpec((tm, e.shape[1]), lambda i, j, k: (i, 0))
        return pl.BlockSpec(e.shape, lambda i, j, k: (0, 0))

    a_spec = pl.BlockSpec((tk, tm), lambda i, j, k: (k, i)) if ta else pl.BlockSpec((tm, tk), lambda i, j, k: (i, k))
    if b_view is not None:
        b_spec = b_view(tk, tn)
    else:
        b_spec = pl.BlockSpec((tn, tk), lambda i, j, k: (j, k)) if tb else pl.BlockSpec((tk, tn), lambda i, j, k: (k, j))
    o_spec = pl.BlockSpec((tm, tn), lambda i, j, k: (i, j))
    if out_view is None:
        out_specs = [o_spec if w is None else pl.BlockSpec((tm, w), lambda i, j, k: (i, 0)) for w in widths]
        out_shape = [jax.ShapeDtypeStruct((m_dim, w or n_dim), dt) for w, dt in zip(widths, out_dtypes)]
    else:
        assert n_out == 1
        out_specs = [out_view[1](tm, tn)]
        out_shape = [jax.ShapeDtypeStruct(out_view[0], out_dtypes[0])]
    out_specs = out_specs + [pl.BlockSpec((1, w), lambda i, j, k: (0, 0)) for w in accs]
    out_shape = out_shape + [jax.ShapeDtypeStruct((1, w), F32) for w in accs]
    grid = (m_dim // tm, n_dim // tn, nk)
    scratch = [pltpu.VMEM((tm, tn), F32)] if nk > 1 else []
    if ride is not None:
        assert into is None
        at = lambda ids: functools.reduce(jnp.logical_and, [pl.program_id(d) == i for d, i in enumerate(ids)])
        outs, landed = _pcall_riding(
            body, (a, b, *extras), ride, lambda: at((0, 0, 0)), lambda: at([g - 1 for g in grid]),
            name=name, grid=grid, in_specs=[a_spec, b_spec] + [ex_spec(e) for e in extras], out_specs=out_specs,
            out_shape=out_shape, scratch_shapes=scratch, compiler_params=_params(("arbitrary",) * 3))
        return (outs[0] if len(outs) == 1 else outs), landed
    outs = _pcall(
        body, name=name, grid=grid,
        in_specs=[a_spec, b_spec] + [ex_spec(e) for e in extras] + [_ANY] * n_into,
        out_specs=out_specs, out_shape=out_shape,
        input_output_aliases={2 + n_ex: 0} if n_into else {},
        scratch_shapes=scratch,
        compiler_params=_params(("arbitrary",) * 3 if accs else ("parallel", "parallel", "arbitrary")),
    )(a, b, *extras, *([into] if n_into else []))
    return outs[0] if len(outs) == 1 else outs


def rowwise(fn, ins, outs, *, name, accs=(), tm=256):
    rows = ins[0].shape[0]
    tm = min(tm, rows)
    n_in, n_out, n_acc = len(ins), len(outs), len(accs)

    def body(*refs):
        in_refs, out_refs, acc_refs = refs[:n_in], refs[n_in:n_in + n_out], refs[n_in + n_out:]
        res, sums = fn(*[r[...] for r in in_refs])
        for o_ref, v in zip(out_refs, res):
            o_ref[...] = v.astype(o_ref.dtype)
        if n_acc:
            @pl.when(pl.program_id(0) == 0)
            def _():
                for a_ref in acc_refs:
                    a_ref[...] = jnp.zeros_like(a_ref)

            for a_ref, s in zip(acc_refs, sums):
                a_ref[...] += s

    def spec(arr):
        if arr.shape[0] == rows:
            return pl.BlockSpec((tm, arr.shape[1]), lambda i: (i, 0))
        return pl.BlockSpec(arr.shape, lambda i: (0, 0))

    res = _pcall(
        body, name=name, grid=(rows // tm,),
        in_specs=[spec(a) for a in ins],
        out_specs=[pl.BlockSpec((tm, w), lambda i: (i, 0)) for w, _ in outs]
        + [pl.BlockSpec((1, w), lambda i: (0, 0)) for w in accs],
        out_shape=[jax.ShapeDtypeStruct((rows, w), dt) for w, dt in outs]
        + [jax.ShapeDtypeStruct((1, w), F32) for w in accs],
        compiler_params=_params(("arbitrary",) if n_acc else ("parallel",)),
    )(*ins)
    return res


def _relu2(v):
    r = jnp.maximum(v, 0.0)
    return r * r


def _gelu(x):
    c = math.sqrt(2.0 / math.pi)
    return 0.5 * x * (1.0 + jnp.tanh(c * (x + 0.044715 * x * x * x)))


def _gelu_grad(x):
    c = math.sqrt(2.0 / math.pi)
    t = jnp.tanh(c * (x + 0.044715 * x * x * x))
    return 0.5 * (1.0 + t) + 0.5 * x * (1.0 - t * t) * c * (1.0 + 3 * 0.044715 * x * x)


def _sigmoid(x):
    return 1.0 / (1.0 + jnp.exp(-x))


def _layer_norm(h, mix, g, b):
    r = DN_ALPHA * h + mix
    mu = jnp.mean(r, axis=-1, keepdims=True)
    xc = r - mu
    var = jnp.mean(xc * xc, axis=-1, keepdims=True)
    return xc * lax.rsqrt(var + LN_EPS) * g + b


def _layer_norm_bwd(h, mix, g, dy):
    r = DN_ALPHA * h + mix
    mu = jnp.mean(r, axis=-1, keepdims=True)
    xc = r - mu
    var = jnp.mean(xc * xc, axis=-1, keepdims=True)
    rstd = lax.rsqrt(var + LN_EPS)
    xhat = xc * rstd
    dxh = dy * g
    m1 = jnp.mean(dxh, axis=-1, keepdims=True)
    m2 = jnp.mean(dxh * xhat, axis=-1, keepdims=True)
    dr = rstd * (dxh - m1 - xhat * m2)
    return dr, jnp.sum(dy * xhat, axis=0, keepdims=True), jnp.sum(dy, axis=0, keepdims=True)


def ln_bwd(h, mix, g, dy, name):
    def fn(h, mix, g, dy):
        dr, dg, db = _layer_norm_bwd(h, mix, g, dy)
        return (dr, dr), (dg, db)
    return rowwise(fn, (h, mix, g, dy), ((D_MODEL, F32), (D_MODEL, BF16)), accs=(D_MODEL, D_MODEL), name=name, tm=512)


def _rms(x, g):
    r = lax.rsqrt(jnp.mean(x * x, axis=-1, keepdims=True) + RMS_EPS)
    return x * r * g


def _rms_bwd(x, g, dy):
    r = lax.rsqrt(jnp.mean(x * x, axis=-1, keepdims=True) + RMS_EPS)
    xn = x * r
    dyg = dy * g
    dx = r * (dyg - xn * jnp.mean(dyg * xn, axis=-1, keepdims=True))
    return dx, jnp.sum(dy * xn, axis=0, keepdims=True)


def _s5_disc(lr, li, ldt):
    dt = jnp.exp(ldt)
    mag = jnp.exp(lr * dt)
    cs, sn = jnp.cos(li * dt), jnp.sin(li * dt)
    ar, ai = mag * cs, mag * sn
    inv = 1.0 / (lr * lr + li * li)
    n_re = (ar - 1.0) * lr + ai * li
    n_im = ai * lr - (ar - 1.0) * li
    return dt, mag, cs, sn, ar, ai, inv, n_re, n_im


def s5_prep(lr, li, ldt, b_re, b_im):
    def fn(lr, li, ldt, b_re, b_im):
        _, _, _, _, ar, ai, inv, n_re, n_im = _s5_disc(lr, li, ldt)
        cr, ci = n_re * inv, n_im * inv
        return (ar, ai, cr * b_re - ci * b_im, cr * b_im + ci * b_re), ()
    return rowwise(fn, (lr, li, ldt, b_re, b_im), ((1, F32), (1, F32), (SSM_GROUP, F32), (SSM_GROUP, F32)),
                   name="s5_prep", tm=512)


def s5_prep_bwd(lr, li, ldt, b_re, b_im, dar, dai, dbb_re, dbb_im):
    def fn(lr, li, ldt, b_re, b_im, dar, dai, dbb_re, dbb_im):
        dt, mag, cs, sn, ar, ai, inv, n_re, n_im = _s5_disc(lr, li, ldt)
        cr, ci = n_re * inv, n_im * inv
        db_re = cr * dbb_re + ci * dbb_im
        db_im = cr * dbb_im - ci * dbb_re
        dcr = jnp.sum(dbb_re * b_re + dbb_im * b_im, axis=-1, keepdims=True)
        dci = jnp.sum(dbb_im * b_re - dbb_re * b_im, axis=-1, keepdims=True)
        dar = dar + (dcr * lr - dci * li) * inv
        dai = dai + (dcr * li + dci * lr) * inv
        dinv = dcr * n_re + dci * n_im
        dlr = (dcr * (ar - 1.0) + dci * ai) * inv - 2.0 * lr * inv * inv * dinv
        dli = (dcr * ai - dci * (ar - 1.0)) * inv - 2.0 * li * inv * inv * dinv
        dmag = dar * cs + dai * sn
        dth = dai * ar - dar * ai
        dlr = dlr + dmag * mag * dt
        dli = dli + dth * dt
        ddt = dmag * mag * lr + dth * li
        return (dlr, dli, ddt * dt, db_re, db_im), ()
    return rowwise(fn, (lr, li, ldt, b_re, b_im, dar, dai, dbb_re, dbb_im),
                   ((1, F32), (1, F32), (1, F32), (SSM_GROUP, F32), (SSM_GROUP, F32)), name="s5_prep_bwd", tm=512)


def group_sum(x):
    def body(x_ref, o_ref):
        o_ref[...] = jnp.sum(x_ref[...], axis=1)
    return _pcall(body, name="s5_group_sum", out_shape=jax.ShapeDtypeStruct((N_GROUPS, 1), F32))(
        x.reshape(N_GROUPS, SSM_STATE, 1))


GROUPS_PER_TILE = LANES // SSM_GROUP
TILE_STATES = GROUPS_PER_TILE * SSM_STATE
N_UTILES = D_MODEL // LANES


SUBLANES = 8
SCAN_STRIP = 1024
N_STRIPS = N_STATES // SCAN_STRIP
_NT = (((1,), (1,)), ((), ()))
_TN = (((0,), (0,)), ((), ()))


def _scan_coefs(are, aim, shifted, reverse):
    ar = are[...]
    ai = -aim[...] if reverse else aim[...]
    powers = {1: (ar, ai)}
    for d in (2, 4):
        r, i = powers[d // 2]
        powers[d] = (r * r - i * i, 2.0 * r * i)
    rid = lax.broadcasted_iota(jnp.int32, (SUBLANES, N_STATES), 0)
    first = (rid == SUBLANES - 1) if reverse else (rid == 0)
    masks = [(1, first)] + [(d, (rid <= SUBLANES - 1 - d) if reverse else (rid >= d)) for d in (1, 2, 4)]
    for n, (d, keep) in enumerate(masks):
        for part in (0, 1):
            shifted[2 * n + part][...] = jnp.where(keep, jnp.broadcast_to(powers[d][part], (SUBLANES, N_STATES)), 0.0)


def _tile_scan(xr, xi, shifted, nbr_re, nbr_im, reverse):
    for n, d in enumerate((1, 1, 2, 4)):
        by = SUBLANES - d if reverse else d
        fr, fi = (nbr_re, nbr_im) if n == 0 else (xr, xi)
        sr, si = pltpu.roll(fr, by, 0), pltpu.roll(fi, by, 0)
        kr, ki = shifted[2 * n], shifted[2 * n + 1]
        xr, xi = xr + kr * sr - ki * si, xi + kr * si + ki * sr
    return xr, xi


def _tile_rows(t):
    return pl.ds(pl.multiple_of(t * SUBLANES, SUBLANES), SUBLANES)


def s5_fwd(u, bbd_re, bbd_im, cbd_re, cbd_imn, a_re, a_im, dskip, ride=None, t_rows=256):
    seq = u.shape[0]
    t_rows = min(t_rows, seq)
    n_tiles = t_rows // SUBLANES

    def body(u_ref, bre, bim, cre, cimn, are, aim, d_ref, y_ref, gelu_ref, hre_ref, him_ref, car_re, car_im, *shifted):
        @pl.when(pl.program_id(0) == 0)
        def _():
            car_re[...] = jnp.zeros_like(car_re)
            car_im[...] = jnp.zeros_like(car_im)
            _scan_coefs(are, aim, shifted, reverse=False)

        uf = u_ref[...]
        ub = uf.astype(BF16)
        for j in range(N_UTILES):
            uj = ub[:, LANES * j:LANES * (j + 1)]
            sl = slice(TILE_STATES * j, TILE_STATES * (j + 1))
            hre_ref[:, sl] = jnp.dot(uj, bre[j], preferred_element_type=F32)
            him_ref[:, sl] = jnp.dot(uj, bim[j], preferred_element_type=F32)
        for s in range(N_STRIPS):
            cols = pl.ds(s * SCAN_STRIP, SCAN_STRIP)
            coefs = [c[:, cols] for c in shifted]

            def step(t, before):
                rows = _tile_rows(t)
                hr, hi = _tile_scan(hre_ref[rows, cols], him_ref[rows, cols], coefs, before[0], before[1], False)
                hre_ref[rows, cols] = hr
                him_ref[rows, cols] = hi
                return hr, hi

            cr, ci = lax.fori_loop(0, n_tiles, step, (car_re[:, cols], car_im[:, cols]))
            car_re[:, cols] = cr
            car_im[:, cols] = ci
        dv = d_ref[...]
        for j in range(N_UTILES):
            st = slice(TILE_STATES * j, TILE_STATES * (j + 1))
            yj = (jnp.dot(hre_ref[:, st].astype(BF16), cre[j], preferred_element_type=F32)
                  + jnp.dot(him_ref[:, st].astype(BF16), cimn[j], preferred_element_type=F32))
            sl = slice(LANES * j, LANES * (j + 1))
            yj = yj + dv[:, sl] * uf[:, sl]
            y_ref[:, sl] = yj
            gelu_ref[:, sl] = _gelu(yj).astype(gelu_ref.dtype)

    full3 = lambda a: pl.BlockSpec(a.shape, lambda i: (0, 0, 0))
    full2 = lambda a: pl.BlockSpec(a.shape, lambda i: (0, 0))
    tile = pltpu.VMEM((SUBLANES, N_STATES), F32)
    n_chunks = seq // t_rows
    return _pcall_riding(
        body, (u, bbd_re, bbd_im, cbd_re, cbd_imn, a_re, a_im, dskip), ride,
        lambda: pl.program_id(0) == 0, lambda: pl.program_id(0) == n_chunks - 1,
        middle=(lambda: pl.program_id(0) == (7 * n_chunks) // 8) if ride is not None else None,
        name="s5_fwd", grid=(n_chunks,),
        in_specs=[pl.BlockSpec((t_rows, D_MODEL), lambda i: (i, 0)), full3(bbd_re), full3(bbd_im), full3(cbd_re),
                  full3(cbd_imn), full2(a_re), full2(a_im), full2(dskip)],
        out_specs=[pl.BlockSpec((t_rows, D_MODEL), lambda i: (i, 0)),
                   pl.BlockSpec((t_rows, D_MODEL), lambda i: (i, 0)),
                   pl.BlockSpec((t_rows, N_STATES), lambda i: (i, 0)),
                   pl.BlockSpec((t_rows, N_STATES), lambda i: (i, 0))],
        out_shape=[jax.ShapeDtypeStruct((seq, D_MODEL), F32),
                   jax.ShapeDtypeStruct((seq, D_MODEL), BF16),
                   jax.ShapeDtypeStruct((seq, N_STATES), F32),
                   jax.ShapeDtypeStruct((seq, N_STATES), F32)],
        scratch_shapes=[tile] * 10,
        compiler_params=_params(("arbitrary",)))


def s5_bwd(dy, u, dres, h_re, h_im, bbd_re, bbd_im, cbd_re, cbd_imn, a_re, a_im, dskip, ride=None, t_rows=256):
    seq = u.shape[0]
    t_rows = min(t_rows, seq)
    n_chunks = seq // t_rows

    n_tiles = t_rows // SUBLANES

    def body(dy_ref, u_ref, dres_ref, hre_ref, him_ref, hpre_ref, hpim_ref, bre, bim, cre, cimn, are, aim, d_ref,
             dx_ref, dbre, dbim, dcre, dcimn, dar_ref, dai_ref, dd_ref, lre, lim, car_re, car_im, acc_re, acc_im,
             *shifted):
        i = pl.program_id(0)

        @pl.when(i == 0)
        def _():
            for r in (car_re, car_im, acc_re, acc_im, dbre, dbim, dcre, dcimn, dd_ref):
                r[...] = jnp.zeros_like(r)
            _scan_coefs(are, aim, shifted, reverse=True)

        dyf = dy_ref[...]
        dyb = dyf.astype(BF16)
        uf = u_ref[...]
        ub = uf.astype(BF16)
        for j in range(N_UTILES):
            dyj = dyb[:, LANES * j:LANES * (j + 1)]
            st = slice(TILE_STATES * j, TILE_STATES * (j + 1))
            lre[:, st] = lax.dot_general(dyj, cre[j], _NT, preferred_element_type=F32)
            lim[:, st] = lax.dot_general(dyj, cimn[j], _NT, preferred_element_type=F32)
        has_pred = (i < n_chunks - 1).astype(F32)
        last_row = lax.broadcasted_iota(jnp.int32, (SUBLANES, SCAN_STRIP), 0) == SUBLANES - 1
        for s in range(N_STRIPS):
            cols = pl.ds(s * SCAN_STRIP, SCAN_STRIP)
            coefs = [c[:, cols] for c in shifted]
            before_re, before_im = hpre_ref[:, cols] * has_pred, hpim_ref[:, cols] * has_pred

            def step(k, carry):
                after_re, after_im, dar, dai = carry
                t = n_tiles - 1 - k
                rows = _tile_rows(t)
                lr, li = _tile_scan(lre[rows, cols], lim[rows, cols], coefs, after_re, after_im, True)
                lre[rows, cols] = lr
                lim[rows, cols] = li
                prev = _tile_rows(jnp.maximum(t - 1, 0))
                pre_re = jnp.where(t == 0, before_re, hre_ref[prev, cols])
                pre_im = jnp.where(t == 0, before_im, him_ref[prev, cols])
                hpr = pltpu.roll(jnp.where(last_row, pre_re, hre_ref[rows, cols]), 1, 0)
                hpi = pltpu.roll(jnp.where(last_row, pre_im, him_ref[rows, cols]), 1, 0)
                return lr, li, dar + lr * hpr + li * hpi, dai + li * hpr - lr * hpi

            cr, ci, dar, dai = lax.fori_loop(0, n_tiles, step, (car_re[:, cols], car_im[:, cols],
                                                               acc_re[:, cols], acc_im[:, cols]))
            car_re[:, cols] = cr
            car_im[:, cols] = ci
            acc_re[:, cols] = dar
            acc_im[:, cols] = dai

        dv = d_ref[...]
        for j in range(N_UTILES):
            sl = slice(LANES * j, LANES * (j + 1))
            st = slice(TILE_STATES * j, TILE_STATES * (j + 1))
            lrj = lre[:, st].astype(BF16)
            lij = lim[:, st].astype(BF16)
            du = (lax.dot_general(lrj, bre[j], _NT, preferred_element_type=F32)
                  + lax.dot_general(lij, bim[j], _NT, preferred_element_type=F32))
            dx_ref[:, sl] = du + dv[:, sl] * dyf[:, sl] + DN_ALPHA * dres_ref[:, sl]
            uj = ub[:, sl]
            dbre[j] += lax.dot_general(uj, lrj, _TN, preferred_element_type=F32)
            dbim[j] += lax.dot_general(uj, lij, _TN, preferred_element_type=F32)
            dyj = dyb[:, sl]
            dcre[j] += lax.dot_general(hre_ref[:, st].astype(BF16), dyj, _TN, preferred_element_type=F32)
            dcimn[j] += lax.dot_general(him_ref[:, st].astype(BF16), dyj, _TN, preferred_element_type=F32)
        dd_ref[...] += jnp.sum(dyf * uf, axis=0, keepdims=True)

        @pl.when(i == n_chunks - 1)
        def _():
            dar_ref[...] = jnp.sum(acc_re[...], axis=0, keepdims=True)
            dai_ref[...] = jnp.sum(acc_im[...], axis=0, keepdims=True)

    rev = lambda i: (n_chunks - 1 - i, 0)
    prev_tile = lambda i: (jnp.maximum((n_chunks - 1 - i) * n_tiles - 1, 0), 0)
    once = pl.Buffered(1)
    full3 = lambda a: pl.BlockSpec(a.shape, lambda i: (0, 0, 0), pipeline_mode=once)
    full2 = lambda a: pl.BlockSpec(a.shape, lambda i: (0, 0), pipeline_mode=once)
    acc3 = lambda shape: pl.BlockSpec(shape, lambda i: (0, 0, 0))
    acc2 = lambda shape: pl.BlockSpec(shape, lambda i: (0, 0))
    tile = pltpu.VMEM((SUBLANES, N_STATES), F32)
    return _pcall_riding(
        body, (dy, u, dres, h_re, h_im, h_re, h_im, bbd_re, bbd_im, cbd_re, cbd_imn, a_re, a_im, dskip), ride,
        lambda: pl.program_id(0) == 0, lambda: pl.program_id(0) == n_chunks - 1,
        name="s5_bwd", grid=(n_chunks,),
        in_specs=[pl.BlockSpec((t_rows, D_MODEL), rev), pl.BlockSpec((t_rows, D_MODEL), rev),
                  pl.BlockSpec((t_rows, D_MODEL), rev),
                  pl.BlockSpec((t_rows, N_STATES), rev), pl.BlockSpec((t_rows, N_STATES), rev),
                  pl.BlockSpec((SUBLANES, N_STATES), prev_tile), pl.BlockSpec((SUBLANES, N_STATES), prev_tile),
                  full3(bbd_re), full3(bbd_im), full3(cbd_re), full3(cbd_imn), full2(a_re), full2(a_im), full2(dskip)],
        out_specs=[pl.BlockSpec((t_rows, D_MODEL), rev), acc3(bbd_re.shape), acc3(bbd_im.shape), acc3(cbd_re.shape),
                   acc3(cbd_imn.shape), acc2((1, N_STATES)), acc2((1, N_STATES)), acc2((1, D_MODEL))],
        out_shape=[jax.ShapeDtypeStruct((seq, D_MODEL), F32), jax.ShapeDtypeStruct(bbd_re.shape, F32),
                   jax.ShapeDtypeStruct(bbd_im.shape, F32), jax.ShapeDtypeStruct(cbd_re.shape, F32),
                   jax.ShapeDtypeStruct(cbd_imn.shape, F32), jax.ShapeDtypeStruct((1, N_STATES), F32),
                   jax.ShapeDtypeStruct((1, N_STATES), F32), jax.ShapeDtypeStruct((1, D_MODEL), F32)],
        scratch_shapes=[pltpu.VMEM((t_rows, N_STATES), F32), pltpu.VMEM((t_rows, N_STATES), F32)] + [tile] * 12,
        in_place={2: 0},
        compiler_params=_params(("arbitrary",)))


def _eye_groups():
    return jnp.eye(GROUPS_PER_TILE, dtype=F32)


def _blockdiag_in(bb):
    t = bb.transpose(0, 2, 1).reshape(N_UTILES, GROUPS_PER_TILE, SSM_GROUP, SSM_STATE)
    bd = jnp.einsum("jgcp,gh->jgchp", t, _eye_groups())
    return bd.reshape(N_UTILES, LANES, TILE_STATES)


def _blockdiag_in_t(d):
    t = jnp.einsum("jgchp,gh->jgcp", d.reshape(N_UTILES, GROUPS_PER_TILE, SSM_GROUP, GROUPS_PER_TILE, SSM_STATE),
                   _eye_groups())
    return t.reshape(N_GROUPS, SSM_GROUP, SSM_STATE).transpose(0, 2, 1)


def _blockdiag_out(c):
    t = c.transpose(0, 2, 1).reshape(N_UTILES, GROUPS_PER_TILE, SSM_STATE, SSM_GROUP)
    bd = jnp.einsum("jhpc,hg->jhpgc", t, _eye_groups())
    return bd.reshape(N_UTILES, TILE_STATES, LANES)


def _blockdiag_out_t(d):
    t = jnp.einsum("jhpgc,hg->jhpc", d.reshape(N_UTILES, GROUPS_PER_TILE, SSM_STATE, GROUPS_PER_TILE, SSM_GROUP),
                   _eye_groups())
    return t.reshape(N_GROUPS, SSM_STATE, SSM_GROUP).transpose(0, 2, 1)


ATT_TQ = 512
ATT_TK = 512
LOG2E = math.log2(math.e)
LN2 = math.log(2.0)
Q_PRESCALE = SM_SCALE * LOG2E


def _loop_in_pairs(n, step, carry, start=0):
    pairs = (n - start) // 2

    def two(t, c):
        return step(start + 2 * t + 1, step(start + 2 * t, c))

    carry = lax.fori_loop(0, pairs, two, carry)
    return lax.fori_loop(start + 2 * pairs, n, step, carry)


def _causal(s, transposed=False):
    r = lax.broadcasted_iota(jnp.int32, s.shape, 0)
    c = lax.broadcasted_iota(jnp.int32, s.shape, 1)
    return jnp.where((r <= c) if transposed else (c <= r), s, NEG_INF)


def _q_specs(rows, at):
    def nope(*ids):
        r, h = at(*ids)
        return r, 3 * (h // HEADS_PER_CHIP) + h % HEADS_PER_CHIP

    def rope(*ids):
        r, h = at(*ids)
        return r, 3 * (h // HEADS_PER_CHIP) + HEADS_PER_CHIP

    return [pl.BlockSpec((rows, LANES), nope), pl.BlockSpec((rows, LANES), rope)]


def _kv_specs(rows, at):
    def col(f):
        def index(*ids):
            r, h = at(*ids)
            return r, f(h)
        return index

    return [pl.BlockSpec((rows, LANES), col(lambda h: 2 * h)), pl.BlockSpec((rows, LANES), col(lambda h: h % HEADS_PER_CHIP)),
            pl.BlockSpec((rows, LANES), col(lambda h: 2 * h + 1))]


def _cat(a, b):
    return jnp.concatenate([a, b], axis=1)


def attn_fwd(q, kv, kr, ride=None, tq=ATT_TQ, tk=ATT_TK):
    seq = q.shape[0]
    n_heads = N_HEADS
    tq, tk = min(tq, seq), min(tk, seq)
    assert tq == tk

    def body(qn_ref, qr_ref, kn_ref, kr_ref, v_ref, o_ref, lse_ref):
        qi = pl.program_id(1)
        qv = _cat(qn_ref[...], qr_ref[...])
        jd = qi

        def block(j, carry, diag):
            m, l, acc = carry
            rows = pl.ds(pl.multiple_of(j * tk, tk), tk)
            s = lax.dot_general(qv, _cat(kn_ref[rows, :], kr_ref[rows, :]), _NT, preferred_element_type=F32)
            if diag:
                s = _causal(s)
            m_new = jnp.maximum(m, jnp.max(s, axis=-1, keepdims=True))
            p = jnp.exp2(s - m_new)
            corr = jnp.exp2(m - m_new)
            l = l * corr + jnp.sum(p, axis=-1, keepdims=True)
            acc = acc * corr + jnp.dot(p.astype(BF16), v_ref[rows, :], preferred_element_type=F32)
            return m_new, l, acc

        init = (jnp.full((tq, 1), NEG_INF, F32), jnp.zeros((tq, 1), F32), jnp.zeros((tq, V_HEAD), F32))
        carry = _loop_in_pairs(jd, lambda j, c: block(j, c, False), init)
        m, l, acc = block(jd, carry, True)
        o_ref[...] = acc / l
        lse_ref[...] = jnp.transpose(jnp.broadcast_to(m + jnp.log2(l), (tq, LANES)))[:1, :]

    n_q = seq // tq
    return _pcall_riding(
        body, (q, q, kv, kr, kv), ride,
        lambda: (pl.program_id(0) == 0) & (pl.program_id(1) == 0),
        lambda: (pl.program_id(0) == n_heads - 1) & (pl.program_id(1) == n_q - 1),
        middle=(lambda: (pl.program_id(0) == (5 * n_heads) // 8) & (pl.program_id(1) == 0)) if ride is not None else None,
        name="attn_fwd", grid=(n_heads, n_q),
        in_specs=_q_specs(tq, lambda h, i: (i, h)) + _kv_specs(seq, lambda h, i: (0, h)),
        out_specs=[pl.BlockSpec((tq, V_HEAD), lambda h, i: (i, h)),
                   pl.BlockSpec((None, None, 1, tq), lambda h, i: (h, i, 0, 0))],
        out_shape=[jax.ShapeDtypeStruct((seq, n_heads * V_HEAD), F32),
                   jax.ShapeDtypeStruct((n_heads, n_q, 1, tq), F32)],
        compiler_params=_params(("arbitrary", "arbitrary")))


def attn_bwd(q, kv, kr, do, lse_row, delta_row, tq=ATT_TK):
    seq = q.shape[0]
    tq = min(tq, seq)
    n_blk = seq // tq

    def body(qn_ref, qr_ref, kn_ref, kr_ref, v_ref, do_ref, lse_ref, delta_ref, dqn_ref, dqr_ref, dkv_ref, dkr_ref, dq_acc):
        head, kj = pl.program_id(0), pl.program_id(1)

        @pl.when(kj == 0)
        def _():
            dq_acc[...] = jnp.zeros_like(dq_acc)

        kc = _cat(kn_ref[...], kr_ref[...])
        vv = v_ref[...]

        def block(i, carry, diag):
            dk, dv = carry
            rows = pl.ds(pl.multiple_of(i * tq, tq), tq)
            qv = _cat(qn_ref[rows, :], qr_ref[rows, :])
            st = lax.dot_general(kc, qv, _NT, preferred_element_type=F32)
            if diag:
                st = _causal(st, transposed=True)
            pt = jnp.exp2(st - lse_ref[0, pl.ds(i, 1), :])
            dob = do_ref[rows, :].astype(BF16)
            dv = dv + jnp.dot(pt.astype(BF16), dob, preferred_element_type=F32)
            dpt = lax.dot_general(vv, dob, _NT, preferred_element_type=F32)
            dst = (pt * (dpt - delta_ref[0, pl.ds(i, 1), :])).astype(BF16)
            dk = dk + jnp.dot(dst, qv, preferred_element_type=F32)
            dq_acc[rows, :] += lax.dot_general(dst, kc, _TN, preferred_element_type=F32)
            return dk, dv

        carry = block(kj, (jnp.zeros((tq, 2 * LANES), F32), jnp.zeros((tq, V_HEAD), F32)), True)
        dk, dv = _loop_in_pairs(n_blk, lambda i, c: block(i, c, False), carry, start=kj + 1)
        dk = dk * LN2
        dkv_ref[...] = _cat(dk[:, :LANES], dv).astype(dkv_ref.dtype)
        lane = lax.broadcasted_iota(jnp.int32, (tq, LANES), 1)
        mine = (lane // HALF_ROPE) % HEADS_PER_CHIP == head % HEADS_PER_CHIP
        dkr_ref[0] = jnp.where(mine, dk[:, LANES:], 0.0)

        @pl.when(kj == n_blk - 1)
        def _():
            dqn_ref[...] = dq_acc[:, :LANES] * SM_SCALE

        @pl.when((kj == n_blk - 1) & (head % HEADS_PER_CHIP == 0))
        def _():
            dqr_ref[...] = dq_acc[:, LANES:] * SM_SCALE

        @pl.when((kj == n_blk - 1) & (head % HEADS_PER_CHIP > 0))
        def _():
            dqr_ref[...] += dq_acc[:, LANES:] * SM_SCALE

    return _pcall(
        body, name="attn_bwd", grid=(N_HEADS, n_blk),
        in_specs=_q_specs(seq, lambda h, j: (0, h)) + _kv_specs(tq, lambda h, j: (j, h))
        + [pl.BlockSpec((seq, V_HEAD), lambda h, j: (0, h)),
           pl.BlockSpec((1, n_blk, tq), lambda h, j: (h, 0, 0)),
           pl.BlockSpec((1, n_blk, tq), lambda h, j: (h, 0, 0))],
        out_specs=[pl.BlockSpec((seq, LANES), lambda h, j: (0, h)),
                   pl.BlockSpec((seq, LANES), lambda h, j: (0, h // HEADS_PER_CHIP)),
                   pl.BlockSpec((tq, QK_NOPE + V_HEAD), lambda h, j: (j, h)),
                   pl.BlockSpec((1, tq, LANES), lambda h, j: (h, j, 0))],
        out_shape=[jax.ShapeDtypeStruct((seq, N_HEADS * QK_NOPE), F32),
                   jax.ShapeDtypeStruct((seq, N_CHIPS * LANES), F32),
                   jax.ShapeDtypeStruct((seq, N_HEADS * (QK_NOPE + V_HEAD)), BF16),
                   jax.ShapeDtypeStruct((N_HEADS, seq, LANES), F32)],
        scratch_shapes=[pltpu.VMEM((seq, 2 * LANES), F32)],
        compiler_params=_params(("arbitrary", "arbitrary")),
    )(q, q, kv, kr, kv, do, lse_row, delta_row)


def head_sum(x, ts=512):
    n_heads, seq, w = x.shape
    ts = min(ts, seq)

    def body(x_ref, o_ref):
        o_ref[...] = jnp.sum(x_ref[...], axis=0)

    return _pcall(body, name="head_sum", grid=(seq // ts,),
                  in_specs=[pl.BlockSpec((n_heads, ts, w), lambda i: (0, i, 0))],
                  out_specs=pl.BlockSpec((ts, w), lambda i: (i, 0)),
                  out_shape=jax.ShapeDtypeStruct((seq, w), F32),
                  compiler_params=_params(("parallel",)))(x)


HEADS_PER_CHIP = N_HEADS // N_CHIPS
Q_CHIP = HEADS_PER_CHIP * QK_DIM
Q_CHIP_NOPE = HEADS_PER_CHIP * QK_NOPE


def _perm_q_cols(w):
    t = w.reshape(w.shape[0], HEADS_PER_CHIP, QK_DIM)
    return jnp.concatenate([t[:, :, :QK_NOPE].reshape(w.shape[0], -1),
                            t[:, :, QK_NOPE:QK_NOPE + HALF_ROPE].reshape(w.shape[0], -1),
                            t[:, :, QK_NOPE + HALF_ROPE:].reshape(w.shape[0], -1)], axis=1)


def _unperm_q_cols(w):
    r = w.shape[0]
    nope = w[:, :Q_CHIP_NOPE].reshape(r, HEADS_PER_CHIP, QK_NOPE)
    r1 = w[:, Q_CHIP_NOPE:Q_CHIP_NOPE + QK_ROPE].reshape(r, HEADS_PER_CHIP, HALF_ROPE)
    r2 = w[:, Q_CHIP_NOPE + QK_ROPE:].reshape(r, HEADS_PER_CHIP, HALF_ROPE)
    return jnp.concatenate([nope, r1, r2], axis=2).reshape(r, Q_CHIP)


def _pad_kva_cols(w):
    z = jnp.zeros((w.shape[0], HALF_ROPE), w.dtype)
    return jnp.concatenate([w[:, :KV_LORA], w[:, KV_LORA:KV_LORA + HALF_ROPE], z, w[:, KV_LORA + HALF_ROPE:], z], axis=1)


def _unpad_kva_cols(w):
    return jnp.concatenate([w[:, :KV_LORA], w[:, KV_LORA:KV_LORA + HALF_ROPE],
                            w[:, KV_LORA + QK_ROPE:KV_LORA + QK_ROPE + HALF_ROPE]], axis=1)


def _rope_tile(t, cs, sn):
    return t * cs + pltpu.roll(t, LANES // 2, 1) * sn


def _rope_tile_bwd(d, cs, sn):
    return d * cs + pltpu.roll(d * sn, LANES // 2, 1)


def _b_cols(tk, tn):
    return pl.BlockSpec((None, tk, tn), lambda i, j, k: (j, k, 0))


def _b_cols_t(tk, tn):
    return pl.BlockSpec((None, tn, tk), lambda i, j, k: (k, j, 0))


def _out_cols(shape):
    return shape, lambda tm, tn: pl.BlockSpec((None, tm, tn), lambda i, j, k: (j, i, 0))


def glu_proj(y, w_glu, tm=1024):
    seq, k_dim = y.shape
    tn = w_glu.shape[2]
    tm = min(tm, seq)
    half = N_CHIPS // 2

    def body(y_ref, wv_ref, wg_ref, val_ref, gate_ref, z_ref):
        yv = y_ref[...]
        v = jnp.dot(yv, wv_ref[...], preferred_element_type=F32)
        gt = jnp.dot(yv, wg_ref[...], preferred_element_type=F32)
        val_ref[...] = v
        gate_ref[...] = gt
        z_ref[...] = (v * _sigmoid(gt)).astype(z_ref.dtype)

    tile = pl.BlockSpec((tm, tn), lambda i, j: (i, j))
    return _pcall(
        body, name="glu_proj", grid=(seq // tm, half),
        in_specs=[pl.BlockSpec((tm, k_dim), lambda i, j: (i, 0)),
                  pl.BlockSpec((None, k_dim, tn), lambda i, j: (j, 0, 0)),
                  pl.BlockSpec((None, k_dim, tn), lambda i, j: (j + half, 0, 0))],
        out_specs=[tile, tile, tile],
        out_shape=[jax.ShapeDtypeStruct((seq, half * tn), F32), jax.ShapeDtypeStruct((seq, half * tn), F32),
                   jax.ShapeDtypeStruct((seq, half * tn), BF16)],
        compiler_params=_params(("parallel", "parallel")),
    )(y, w_glu, w_glu)


def _halves(a):
    return a.reshape(N_CHIPS, 2, a.shape[1] // 2, a.shape[2])


def device_step(x, positions, target, w, comm=None):
    seq = x.shape[0]
    w = dict(w)

    def gathered(names, outs):
        for n, a in zip(names, outs):
            if isinstance(n, tuple):
                w[n[0]] = [a.reshape(v.shape) if l == n[1] else v for l, v in enumerate(w[n[0]])]
            else:
                w[n] = a.reshape(w[n].shape)

    def ride_for(names):
        if comm is None:
            return None
        return GatherRide([_halves(w[n[0]][n[1]] if isinstance(n, tuple) else w[n]) for n in names])

    first_ride = ("ssm_w_glu", "ssm_w_out", ("w_ff1", 0), ("w_ff2", 0))
    mla_ride = ("kv_w_a", "kv_w_b", "q_w_a", "q_w_b", "attn_w_o")
    second_ride = (("w_ff1", 1), ("w_ff2", 1))

    inv_freq = ROPE_THETA ** (-jnp.arange(HALF_ROPE, dtype=F32) / HALF_ROPE)
    ang = positions.astype(F32)[:, None] * jnp.tile(inv_freq, LANES // HALF_ROPE)
    cos, sin = jnp.cos(ang), jnp.sin(ang)
    quarter = jnp.arange(LANES) // HALF_ROPE
    sign = jnp.where(quarter < 2, -1.0, 1.0).astype(F32)
    own = (quarter % 2 == 0).astype(F32)
    cos_q, sin_q = cos, sin * sign
    cos_k, sin_k = cos * own, sin * (sign * own)
    ff_tile = D_FF // N_CHIPS
    pack_shape = (N_CHIPS, EARLY_ROWS, PACK_W)

    lr = w["ssm_lam_re"].reshape(N_STATES, 1)
    li = w["ssm_lam_im"].reshape(N_STATES, 1)
    ldt = jnp.repeat(w["ssm_log_dt"].reshape(N_GROUPS), SSM_STATE).reshape(N_STATES, 1)
    b_re = w["ssm_b_re"].reshape(N_STATES, SSM_GROUP)
    b_im = w["ssm_b_im"].reshape(N_STATES, SSM_GROUP)
    a_re, a_im, bb_re, bb_im = s5_prep(lr, li, ldt, b_re, b_im)
    a_re, a_im = a_re.reshape(1, N_STATES), a_im.reshape(1, N_STATES)
    bbd_re = _blockdiag_in(bb_re.reshape(N_GROUPS, SSM_STATE, SSM_GROUP)).astype(BF16)
    bbd_im = _blockdiag_in(bb_im.reshape(N_GROUPS, SSM_STATE, SSM_GROUP)).astype(BF16)
    cbd_re = _blockdiag_out(w["ssm_c_re"].reshape(N_GROUPS, SSM_GROUP, SSM_STATE)).astype(BF16)
    cbd_imn = _blockdiag_out(-w["ssm_c_im"].reshape(N_GROUPS, SSM_GROUP, SSM_STATE)).astype(BF16)
    dskip = w["ssm_d"].reshape(1, D_MODEL)
    (ypre, yg, h_re, h_im), landed = s5_fwd(x, bbd_re, bbd_im, cbd_re, cbd_imn, a_re, a_im, dskip, ride_for(first_ride))
    gathered(first_ride, landed)
    w_glu = w["ssm_w_glu"]
    glu_tile = w_glu.shape[2]
    val, gate, z = glu_proj(yg, w_glu)
    w_out = w["ssm_w_out"].reshape(D_MODEL, D_MODEL)
    ln = lambda name, l: w[name][l].reshape(1, D_MODEL)

    def then_ln(h, names, layer):
        def epi(r, hv, gl, bl):
            y = _layer_norm(hv, r, gl, bl)
            return r, y, y
        return dict(epi=epi, extras=(h, ln(names[0], layer), ln(names[1], layer)), out_dtypes=(F32, F32, BF16))

    mix0, h1, h1b = mm(z, w_out, name="ssm_out", **then_ln(x, ("ln_mix_g", "ln_mix_b"), 0))

    def mlp_fwd(h, hb, layer, riding=None, with_ln=True):
        pre = mm(hb, w["w_ff1"][layer], n_dim=D_FF, tiles=(None, ff_tile, None), b_view=_b_cols, name=f"ff1_{layer}",
                 out_dtypes=(BF16,), ride=ride_for(riding) if riding else None)
        if riding and comm is not None:
            pre, landed = pre
            gathered(riding, landed)
        post = then_ln(h, ("ln_ffn_g", "ln_ffn_b"), layer) if with_ln else {}
        return pre, mm(pre, w["w_ff2"][layer].reshape(D_FF, D_MODEL), pro_a=_relu2, name=f"ff2_{layer}", **post)

    f1pre, (f1, h2, h2b) = mlp_fwd(h1, h1b, 0, mla_ride)

    kv_w_a = w["kv_w_a"].reshape(D_MODEL, KVA_PAD)
    kv_w_b = w["kv_w_b"]
    q_w_a = w["q_w_a"].reshape(D_MODEL, Q_LORA)
    q_w_b = w["q_w_b"]
    w_o = w["attn_w_o"].reshape(D_MODEL, D_MODEL)
    kvb_tile = kv_w_b.shape[2]
    kvn_g = w["kv_norm_g"].reshape(1, KV_LORA)
    qn_g = w["q_norm_g"].reshape(1, Q_LORA)
    def kv_post(kva, g, cs, sn):
        tile = _rope_tile(kva[:, KV_LORA:], cs, sn)
        return kva, _rms(kva[:, :KV_LORA], g), _cat(tile, pltpu.roll(tile, HALF_ROPE, 1))
    kva, ckv, krope = mm(h2b, kv_w_a, epi=kv_post, extras=(kvn_g, cos_k, sin_k),
                         out_dtypes=(F32, (KV_LORA, BF16), (2 * LANES, BF16)), name="kv_a")
    kvb = mm(ckv, kv_w_b, n_dim=N_CHIPS * kvb_tile, tiles=(None, kvb_tile, KV_LORA), b_view=_b_cols, name="kv_b",
             out_dtypes=(BF16,))
    cq_raw, cq = mm(h2b, q_w_a, epi=lambda r, gq: (r, _rms(r, gq)), extras=(qn_g,), out_dtypes=(F32, BF16), name="q_a")

    def rope_and_scale(r, cs, sn):
        return (_cat(r[:, :Q_CHIP_NOPE], _rope_tile(r[:, Q_CHIP_NOPE:], cs, sn)) * Q_PRESCALE,)
    qro = mm(cq, q_w_b, n_dim=N_CHIPS * Q_CHIP, tiles=(None, Q_CHIP, Q_LORA), b_view=_b_cols, epi=rope_and_scale,
             extras=(cos_q, sin_q), out_dtypes=(BF16,), name="q_b")
    (o, lse), landed = attn_fwd(qro, kvb, krope, ride_for(second_ride))
    gathered(second_ride, landed)
    mix1, h3, h3b = mm(o, w_o, name="attn_out", **then_ln(h2, ("ln_mix_g", "ln_mix_b"), 1))
    f2pre, f2 = mlp_fwd(h3, h3b, 1, with_ln=False)
    def last_ln_loss_and_back(h, mix, gl, bl, t):
        e = _layer_norm(h, mix, gl, bl) - t
        dr, dg, db = _layer_norm_bwd(h, mix, gl, e * (1.0 / D_MODEL))
        return (dr, dr), (jnp.broadcast_to(jnp.sum(e * e), (1, LANES)), dg, db)
    dr4, dr4b, loss_acc, dg_f1, db_f1 = rowwise(
        last_ln_loss_and_back, (h3, f2, ln("ln_ffn_g", 1), ln("ln_ffn_b", 1), target),
        ((D_MODEL, F32), (D_MODEL, BF16)), accs=(LANES, D_MODEL, D_MODEL), name="ln_ffn_1_loss", tm=512)
    loss = loss_acc[0, 0] * (0.5 / D_MODEL)

    g = {}

    def into_rows(off, rows_per_chip, shape=pack_shape):
        def view(tm, tn):
            if tm == N_CHIPS * rows_per_chip:
                return pl.BlockSpec((N_CHIPS, rows_per_chip, tn), lambda i, j, k: (0, off // rows_per_chip, 0))
            nb = rows_per_chip // tm
            return pl.BlockSpec((None, tm, tn), lambda i, j, k: (i // nb, off // tm + i % nb, 0))
        return shape, view

    def into_cols(off):
        return pack_shape, lambda tm, tn: pl.BlockSpec((None, tm, tn), lambda i, j, k: (j, off // tm + i, 0))

    def mlp_bwd(pack, dr, drb, hb, pre, layer, swap=False):
        w2_rows = (EARLY_OFF["w_ff2"] + layer * ff_tile, ff_tile)
        w1_rows = (EARLY_OFF["w_ff1"] + layer * D_MODEL, D_MODEL)
        ready = [(w1_rows[0] + w1_rows[1], w2_rows[0] - w1_rows[0] - w1_rows[1]), (w2_rows[0] + w2_rows[1], EARLY_ROWS - w2_rows[0] - w2_rows[1])]
        dpre = mm(drb, w["w_ff2"][layer].reshape(D_FF, D_MODEL), tb=True, epi=lambda r, p: (r * 2.0 * jnp.maximum(p, 0.0),),
                  extras=(pre,), out_dtypes=(BF16,), tiles=(None, ff_tile, None), name=f"ff2_dx_{layer}",
                  ride=SwapRide(pack, ready) if swap else None)
        if swap:
            dpre, (theirs,) = dpre
        pack = mm(pre, drb, ta=True, pro_a=_relu2, name=f"ff2_dw_{layer}", tiles=(ff_tile, PACK_W, None), into=pack,
                  out_view=into_rows(w2_rows[0], ff_tile))
        pack = mm(hb, dpre, ta=True, name=f"ff1_dw_{layer}", tiles=(None, PACK_W, None), into=pack,
                  out_view=into_cols(w1_rows[0]))
        dh = mm(dpre, w["w_ff1"][layer], tb=True, epi=lambda r, d: (r + DN_ALPHA * d,), extras=(dr,), n_dim=D_MODEL,
                tiles=(None, D_MODEL, ff_tile), b_view=_b_cols_t, name=f"ff1_dx_{layer}",
                ride=SwapRide(pack, [w1_rows, w2_rows], into=theirs) if swap else None)
        return (pack, *dh) if swap else (pack, dh)

    pack, dh3 = mlp_bwd(None, dr4, dr4b, h3b, f2pre, 1)
    dr3, dr3b, dg_m1, db_m1 = ln_bwd(h2, mix1, ln("ln_mix_g", 1), dh3, "ln_mix_bwd_1")
    shard_rows = D_MODEL // N_CHIPS
    pack = mm(o, dr3b, ta=True, name="attn_out_dw", tiles=(D_MODEL, PACK_W, None), into=pack,
              out_view=into_rows(EARLY_OFF["attn_w_o"], shard_rows))
    def head_dots(do, o):
        return do, jnp.concatenate([jnp.sum(do[:, V_HEAD * h:V_HEAD * (h + 1)] * o[:, V_HEAD * h:V_HEAD * (h + 1)], axis=1,
                                            keepdims=True) for h in range(N_HEADS)], axis=1)
    do, delta = mm(dr3b, w_o, tb=True, epi=head_dots, extras=(o,), out_dtypes=(F32, (N_HEADS, F32)), name="attn_out_dx")
    tb = min(ATT_TK, seq)
    lse_row = lse.reshape(N_HEADS, seq // tb, tb)
    delta_row = delta.T.reshape(N_HEADS, seq // tb, tb)
    dqn, dqr, dkvb, dkr = attn_bwd(qro, kvb, krope, do, lse_row, delta_row)

    def q_rope_bwd(dn, dr, cs, sn):
        parts = []
        for k in range(N_CHIPS):
            parts.append(dn[:, Q_CHIP_NOPE * k:Q_CHIP_NOPE * (k + 1)])
            parts.append(_rope_tile_bwd(dr[:, LANES * k:LANES * (k + 1)], cs, sn))
        return (jnp.concatenate(parts, axis=1),), ()
    (dqlin,) = rowwise(q_rope_bwd, (dqn, dqr, cos_q, sin_q), ((N_CHIPS * Q_CHIP, BF16),), name="q_rope_bwd", tm=512)
    g["q_w_b"] = mm(cq, dqlin, ta=True, name="q_b_dw", tiles=(Q_LORA, Q_CHIP, None), out_view=_out_cols(q_w_b.shape))
    dcq_raw, dqn_g = mm(dqlin, q_w_b, tb=True, n_dim=Q_LORA, tiles=(None, Q_LORA, Q_CHIP), b_view=_b_cols_t,
                        epi=lambda d, c, gq: _rms_bwd(c, gq, d), extras=(cq_raw, qn_g), out_dtypes=(BF16,),
                        accs=(Q_LORA,), name="q_b_dx")
    g["q_w_a"] = mm(h2b, dcq_raw, ta=True, name="q_a_dw")
    g["kv_w_b"] = mm(ckv, dkvb, ta=True, name="kv_b_dw", tiles=(KV_LORA, kvb_tile, None), out_view=_out_cols(kv_w_b.shape))
    dkr_sum = head_sum(dkr)

    def kv_post_bwd(dc, kva, gk, dk, cs, sn):
        dx, dgk = _rms_bwd(kva[:, :KV_LORA], gk, dc)
        dk = dk + pltpu.roll(dk, LANES - HALF_ROPE, 1)
        return jnp.concatenate([dx, _rope_tile_bwd(dk, cs, sn)], axis=1), dgk
    dkva, dkvn_g = mm(dkvb, kv_w_b, tb=True, n_dim=KV_LORA, tiles=(None, KV_LORA, kvb_tile), b_view=_b_cols_t,
                      epi=kv_post_bwd, extras=(kva, kvn_g, dkr_sum, cos_k, sin_k), out_dtypes=((KVA_PAD, BF16),),
                      accs=(KV_LORA,), name="kv_b_dx")
    g["kv_w_a"] = mm(h2b, dkva, ta=True, name="kv_a_dw")

    def ln_ffn_bwd(r, dc, wq, d, h, f, gl):
        via_q = lax.dot_general(dc, wq, (((1,), (1,)), ((), ())), preferred_element_type=F32)
        dr, dg, db = _layer_norm_bwd(h, f, gl, r + (via_q + DN_ALPHA * d))
        return dr, dr, dg, db
    dr2, dr2b, dg_f0, db_f0 = mm(dkva, kv_w_a, tb=True, epi=ln_ffn_bwd,
                                 extras=(dcq_raw, q_w_a.astype(BF16), dr3, h1, f1, ln("ln_ffn_g", 0)),
                                 out_dtypes=(F32, BF16), accs=(D_MODEL, D_MODEL), name="qkv_a_dx")
    pack = put_rows(pack, packed_shards(g, MISC_EARLY, EARLY_ROWS - MISC_EARLY_OFF), MISC_EARLY_OFF)
    if comm is None:
        pack, dh1 = mlp_bwd(pack, dr2, dr2b, h1b, f1pre, 0)
    else:
        pack, dh1, (theirs,) = mlp_bwd(pack, dr2, dr2b, h1b, f1pre, 0, swap=True)
        early_sums = add_halves(pack, theirs, comm[1])
    dr1, dr1b, dg_m0, db_m0 = ln_bwd(x, mix0, ln("ln_mix_g", 0), dh1, "ln_mix_bwd_0")
    mid = mm(z, dr1b, ta=True, name="ssm_out_dw", tiles=(D_MODEL, PACK_W, None),
             out_view=into_rows(MID_OFF["ssm_w_out"], shard_rows, (N_CHIPS, MID_ROWS, PACK_W)))
    def glu_bwd(dz, vl, gt):
        sg = _sigmoid(gt)
        return (jnp.concatenate([dz * sg, dz * vl * sg * (1.0 - sg)], axis=1),)
    dvg = mm(dr1b, w_out, tb=True, epi=glu_bwd, extras=(val, gate), out_dtypes=((2 * D_MODEL, BF16),), name="ssm_out_dx")
    g["ssm_w_glu"] = mm(yg, dvg, ta=True, name="glu_proj_dw", tiles=(None, glu_tile, None), out_view=_out_cols(w_glu.shape))
    mid = put_rows(mid, packed_shards(g, MISC_MID, MID_ROWS - MISC_MID_OFF), MISC_MID_OFF)
    dypre = mm(dvg, w_glu, tb=True, epi=lambda r, y: (r * _gelu_grad(y),), extras=(ypre,), n_dim=D_MODEL,
               tiles=(None, D_MODEL, glu_tile), b_view=_b_cols_t, name="glu_proj_dx",
               ride=Together([SwapRide(mid), SendRide([(early_sums, (0, EARLY_HEAD), None)])]) if comm is not None else None)
    sends = None
    if comm is not None:
        dypre, (theirs, early_got) = dypre
        sends = SendRide([(early_sums, (EARLY_HEAD, EARLY_ROWS - EARLY_HEAD), early_got), add_halves(mid, theirs, comm[1])])
    (dx, dbbd_re, dbbd_im, dcbd_re, dcbd_imn, dar, dai, dd), got = s5_bwd(
        dypre, x, dr1, h_re, h_im, bbd_re, bbd_im, cbd_re, cbd_imn, a_re, a_im, dskip, sends)
    dbb_re = _blockdiag_in_t(dbbd_re).reshape(N_STATES, SSM_GROUP)
    dbb_im = _blockdiag_in_t(dbbd_im).reshape(N_STATES, SSM_GROUP)
    dlr, dli, dldt, db_re, db_im = s5_prep_bwd(lr, li, ldt, b_re, b_im, dar.reshape(N_STATES, 1),
                                               dai.reshape(N_STATES, 1), dbb_re, dbb_im)
    g["ssm_lam_re"] = dlr.reshape(1, N_GROUPS, SSM_STATE)
    g["ssm_lam_im"] = dli.reshape(1, N_GROUPS, SSM_STATE)
    g["ssm_log_dt"] = group_sum(dldt).reshape(1, N_GROUPS)
    g["ssm_b_re"] = db_re.reshape(1, N_GROUPS, SSM_STATE, SSM_GROUP)
    g["ssm_b_im"] = db_im.reshape(1, N_GROUPS, SSM_STATE, SSM_GROUP)
    g["ssm_c_re"] = _blockdiag_out_t(dcbd_re).reshape(1, N_GROUPS, SSM_GROUP, SSM_STATE)
    g["ssm_c_im"] = -_blockdiag_out_t(dcbd_imn).reshape(1, N_GROUPS, SSM_GROUP, SSM_STATE)
    g["ssm_d"] = dd
    g["ln_mix_g"] = jnp.concatenate([dg_m0, dg_m1], 0)
    g["ln_mix_b"] = jnp.concatenate([db_m0, db_m1], 0)
    g["ln_ffn_g"] = jnp.concatenate([dg_f0, dg_f1], 0)
    g["ln_ffn_b"] = jnp.concatenate([db_f0, db_f1], 0)
    g["kv_norm_g"] = dkvn_g.reshape(KV_LORA)
    g["q_norm_g"] = dqn_g
    return loss, dx, pack, mid, g, list(zip(sends.ins, got)) if comm is not None else None


def place(shard, me_idx, dtype, name, layer=None):
    rows, cols = shard.shape[-2:]
    tr = _tile(rows, (512, 256, 128))

    def body(m_ref, x_ref, o_ref):
        o_ref[...] = x_ref[...].astype(o_ref.dtype)

    in_spec = (pl.BlockSpec((tr, cols), lambda i, m: (i, 0)) if layer is None
               else pl.BlockSpec((None, tr, cols), lambda i, m: (layer, i, 0)))
    return _pcall(
        body, name=name,
        grid_spec=pltpu.PrefetchScalarGridSpec(
            num_scalar_prefetch=1, grid=(rows // tr,), in_specs=[in_spec],
            out_specs=pl.BlockSpec((None, tr, cols), lambda i, m: (m[0], i, 0))),
        out_shape=jax.ShapeDtypeStruct((N_CHIPS, rows, cols), dtype),
        compiler_params=_params(("parallel",)),
    )(me_idx, shard)


def place_many(shards, dtypes, me_idx, name):
    def body(m_ref, *refs):
        for x_ref, o_ref in zip(refs[:len(shards)], refs[len(shards):]):
            o_ref[...] = x_ref[...].astype(o_ref.dtype)

    return _pcall(
        body, name=name,
        grid_spec=pltpu.PrefetchScalarGridSpec(
            num_scalar_prefetch=1, grid=(1,),
            in_specs=[pl.BlockSpec(s.shape, lambda i, m: (0, 0)) for s in shards],
            out_specs=[pl.BlockSpec((None,) + s.shape, lambda i, m: (m[0], 0, 0)) for s in shards]),
        out_shape=[jax.ShapeDtypeStruct((N_CHIPS,) + s.shape, d) for s, d in zip(shards, dtypes)],
        compiler_params=_params(("arbitrary",)),
    )(me_idx, *shards)


def put_rows(pack, rows, off):
    _, n, cols = rows.shape

    def body(r_ref, p_ref, o_ref, sem):
        cp = pltpu.make_async_copy(r_ref.at[0], o_ref.at[pl.program_id(0), pl.ds(off, n), :], sem)
        cp.start()
        cp.wait()

    return _pcall(body, name="grad_put_rows", grid=(N_CHIPS,),
                  in_specs=[pl.BlockSpec((1, n, cols), lambda k: (k, 0, 0)), _ANY], out_specs=_ANY,
                  out_shape=jax.ShapeDtypeStruct(pack.shape, pack.dtype), input_output_aliases={1: 0},
                  scratch_shapes=[pltpu.SemaphoreType.DMA],
                  compiler_params=_params(("arbitrary",)))(rows, pack)


def _my_cols(c, mine=True):
    start = (c if mine else 1 - c) * HALF_W
    return pl.ds(pl.multiple_of(start, HALF_W), HALF_W)


def add_halves(gpack, got, c_idx):
    n, rows, _ = gpack.shape
    tr = min(G_BLOCK_ROWS, rows)
    blk = (None, tr, HALF_W)

    def body(c_ref, g_ref, r_ref, o_ref):
        o_ref[...] = (g_ref[...] + r_ref[...]).astype(o_ref.dtype)

    return _pcall(
        body, name="grad_add_halves",
        grid_spec=pltpu.PrefetchScalarGridSpec(
            num_scalar_prefetch=1, grid=(n, rows // tr),
            in_specs=[pl.BlockSpec(blk, lambda k, i, c: (k, i, c[0])), pl.BlockSpec(blk, lambda k, i, c: (k, i, 0))],
            out_specs=pl.BlockSpec(blk, lambda k, i, c: (k, i, 0))),
        out_shape=jax.ShapeDtypeStruct((n, rows, HALF_W), BF16),
        compiler_params=_params(("parallel", "parallel")),
    )(c_idx, gpack, got)


def sum_owner(part, got, idx, total_rows, row_off=0, into=None):
    _, rows, _ = part.shape
    tr = math.gcd(math.gcd(rows, row_off), G_BLOCK_ROWS)
    n_into = 0 if into is None else 1

    def body(m_ref, p_ref, g_ref, *rest):
        up = lambda v: v.astype(F32)
        rest[-1][...] = ((up(p_ref[...]) + up(g_ref[0])) + up(g_ref[1])) + up(g_ref[2])

    return _pcall(
        body, name="grad_sum_owner",
        grid_spec=pltpu.PrefetchScalarGridSpec(
            num_scalar_prefetch=1, grid=(rows // tr,),
            in_specs=[pl.BlockSpec((None, tr, HALF_W), lambda i, m: (m[0], i, 0)),
                      pl.BlockSpec((3, tr, HALF_W), lambda i, m: (0, i, 0))] + [_ANY] * n_into,
            out_specs=pl.BlockSpec((tr, HALF_W), lambda i, m: (row_off // tr + i, m[1]))),
        out_shape=jax.ShapeDtypeStruct((total_rows, PACK_W), F32),
        input_output_aliases={3: 0} if n_into else {},
        compiler_params=_params(("parallel",)),
    )(idx, part, got, *([into] if n_into else []))


def join_halves(red):
    def body(in_ref, out_ref, send_sem, recv_sem):
        x, y, c, _ = _place()
        sibling = (x, y, 1 - c)
        mine = out_ref.at[:, _my_cols(c)]
        cp = pltpu.make_async_remote_copy(src_ref=mine, dst_ref=mine, send_sem=send_sem, recv_sem=recv_sem,
                                          device_id=sibling, device_id_type=MESH)
        cp.start()
        cp.wait_send()
        other = out_ref.at[:, _my_cols(c, mine=False)]
        pltpu.make_async_remote_copy(src_ref=other, dst_ref=other, send_sem=send_sem, recv_sem=recv_sem,
                                     device_id=sibling, device_id_type=MESH).wait_recv()

    return _pcall(body, name="grad_join_halves", in_specs=[_ANY], out_specs=_ANY,
                  out_shape=jax.ShapeDtypeStruct(red.shape, red.dtype), input_output_aliases={0: 0},
                  scratch_shapes=[pltpu.SemaphoreType.DMA, pltpu.SemaphoreType.DMA])(red)


def adamw(gsrc, g_off, wt, m, v, name):
    n, cols = wt.shape
    tr = math.gcd(math.gcd(g_off, n), 256) if g_off else math.gcd(n, 256)
    off_blk = g_off // tr
    c1 = 1.0 / (1.0 - ADAM_B1 ** ADAM_STEP)
    c2 = 1.0 / (1.0 - ADAM_B2 ** ADAM_STEP)

    def body(g_ref, w_ref, m_ref, v_ref, go_ref, d_ref, mo_ref, vo_ref):
        gv = g_ref[...]
        mn = ADAM_B1 * m_ref[...] + (1.0 - ADAM_B1) * gv
        vn = ADAM_B2 * v_ref[...] + (1.0 - ADAM_B2) * gv * gv
        go_ref[...] = gv
        mo_ref[...] = mn
        vo_ref[...] = vn
        d_ref[...] = -ADAM_LR * ((mn * c1) / (jnp.sqrt(vn * c2) + ADAM_EPS) + ADAM_WD * w_ref[...])

    blk = pl.BlockSpec((tr, cols), lambda i: (i, 0))
    return _pcall(body, name=name, grid=(n // tr,),
                  in_specs=[pl.BlockSpec((tr, cols), lambda i: (off_blk + i, 0)), blk, blk, blk],
                  out_specs=[blk] * 4, out_shape=[jax.ShapeDtypeStruct((n, cols), F32)] * 4,
                  compiler_params=_params(("parallel",)))(gsrc, wt, m, v)


def _rows8(a):
    return -(-a.size // (8 * PACK_W)) * 8


def _as_rows(a, rows=None):
    flat = a.reshape(-1)
    n = _rows8(a) if rows is None else rows
    return jnp.pad(flat, (0, n * PACK_W - flat.shape[0])).reshape(n, PACK_W)


def local_shards_2d(wl):
    return {"w_ff1": [wl["w_ff1"][0], wl["w_ff1"][1]], "w_ff2": [wl["w_ff2"][0], wl["w_ff2"][1]],
            "ssm_w_glu": wl["ssm_w_glu"], "ssm_w_out": wl["ssm_w_out"], "kv_w_a": _pad_kva_cols(wl["kv_w_a"]),
            "kv_w_b": wl["kv_w_b"], "q_w_a": wl["q_w_a"], "q_w_b": _perm_q_cols(wl["q_w_b"]),
            "attn_w_o": wl["attn_w_o"], "ssm_d": wl["ssm_d"].reshape(2, -1)}


def misc_grad_shard(name, g, k):
    if name == "ssm_d":
        w = D_MODEL // N_CHIPS
        return g[:, w * k:w * (k + 1)]
    if name in ("ssm_w_glu", "kv_w_b"):
        return g[k]
    if name == "q_w_b":
        return _unperm_q_cols(g[k])
    rows = D_MODEL // N_CHIPS
    shard = g[rows * k:rows * (k + 1)]
    return _unpad_kva_cols(shard) if name == "kv_w_a" else shard


def packed_shards(g, names, rows, tail=None):
    blocks = []
    for k in range(N_CHIPS):
        parts = [_as_rows(misc_grad_shard(n, g[n], k), MISC_SHARD_ROWS[n]) for n in names]
        if tail is not None:
            parts.append(tail[k * (tail.shape[0] // N_CHIPS):(k + 1) * (tail.shape[0] // N_CHIPS)])
        blk = jnp.concatenate(parts, axis=0)
        blocks.append(jnp.pad(blk, ((0, rows - blk.shape[0]), (0, 0))))
    return jnp.stack(blocks)


def kernel(x, positions, ln_mix_g, ln_mix_b, ln_ffn_g, ln_ffn_b, w_ff1, w_ff2, ssm_lam_re, ssm_lam_im, ssm_log_dt, ssm_b_re, ssm_b_im, ssm_c_re, ssm_c_im, ssm_d, ssm_w_glu, ssm_w_out, kv_w_a, kv_norm_g, kv_w_b, q_w_a, q_norm_g, q_w_b, attn_w_o, loss_target, m_ln_mix_g, m_ln_mix_b, m_ln_ffn_g, m_ln_ffn_b, m_w_ff1, m_w_ff2, m_ssm_lam_re, m_ssm_lam_im, m_ssm_log_dt, m_ssm_b_re, m_ssm_b_im, m_ssm_c_re, m_ssm_c_im, m_ssm_d, m_ssm_w_glu, m_ssm_w_out, m_kv_w_a, m_kv_norm_g, m_kv_w_b, m_q_w_a, m_q_norm_g, m_q_w_b, m_attn_w_o, v_ln_mix_g, v_ln_mix_b, v_ln_ffn_g, v_ln_ffn_b, v_w_ff1, v_w_ff2, v_ssm_lam_re, v_ssm_lam_im, v_ssm_log_dt, v_ssm_b_re, v_ssm_b_im, v_ssm_c_re, v_ssm_c_im, v_ssm_d, v_ssm_w_glu, v_ssm_w_out, v_kv_w_a, v_kv_norm_g, v_kv_w_b, v_q_w_a, v_q_norm_g, v_q_w_b, v_attn_w_o):
    env = dict(locals())
    wl = {n: env[n] for n in WEIGHTS}
    ml = {n: env["m_" + n] for n in WEIGHTS}
    vl = {n: env["v_" + n] for n in WEIGHTS}
    for n in ("ssm_w_glu", "ssm_w_out", "q_w_a", "q_w_b", "attn_w_o"):
        wl[n], ml[n], vl[n] = wl[n][0], ml[n][0], vl[n][0]

    c_idx = lax.axis_index("c").astype(jnp.int32).reshape(1)
    me_idx = (2 * lax.axis_index("x") + lax.axis_index("y")).astype(jnp.int32).reshape(1)

    local = local_shards_2d(wl)
    stacked = {n: [place(wl[n], me_idx, BF16, f"place_{n}_{l}", layer=l) for l in range(DEPTH)] for n in ("w_ff1", "w_ff2")}
    others = [n for n in SHARDED if n not in stacked]
    stacked.update(zip(others, place_many([local[n] for n in others], [F32 if n == "ssm_d" else BF16 for n in others],
                                          me_idx, "place_others")))
    stacked["ssm_d"] = ride_alone(GatherRide([_halves(stacked["ssm_d"])]), "ssm_d_all_gather")[0].reshape(1, D_MODEL)
    for n in REPLICATED:
        stacked[n] = wl[n]

    loss_part, dx, early, mid, g, sent = device_step(x[0], positions[0], loss_target[0], stacked, comm=(me_idx, c_idx))
    loss = lax.psum(loss_part, ("x", "y", "c"))

    small = jnp.concatenate([_as_rows(g[n]) for n in REPLICATED], axis=0)
    small = jnp.pad(small, ((0, SMALL_ROWS - small.shape[0]), (0, 0)))
    late = packed_shards(g, MISC_LATE, LATE_ROWS, tail=small)
    late_sums = add_halves(late, ride_alone(SwapRide(late), "grad_swap_halves")[0], c_idx)
    sent.append((late_sums, ride_alone(SendRide([late_sums]), "grad_send_to_owners")[0]))
    where = jnp.concatenate([me_idx, c_idx])
    starts = (0, EARLY_ROWS, EARLY_ROWS + MID_ROWS)
    total_rows = EARLY_ROWS + MID_ROWS + LATE_ROWS
    reduced = None
    for (sums, got), off in zip(sent, starts):
        reduced = sum_owner(sums, got, where, total_rows, row_off=off, into=reduced)
    reduced = join_halves(reduced)
    quarter = reduced[starts[2] + SMALL_OFF:starts[2] + SMALL_OFF + SMALL_Q_ROWS]
    small_tot = ride_alone(GatherRide([_halves(place(quarter, me_idx, F32, "place_small_grads"))]),
                           "small_grad_all_gather")[0].reshape(SMALL_ROWS, PACK_W)

    out_g, out_d, out_m, out_v = {}, {}, {}, {}
    direct = {**EARLY_OFF, **{n: starts[1] + o for n, o in MID_OFF.items()}}
    for n, off in direct.items():
        res = adamw(reduced, off, wl[n].reshape(-1, PACK_W), ml[n].reshape(-1, PACK_W), vl[n].reshape(-1, PACK_W),
                    "adamw_" + n)
        out_g[n], out_d[n], out_m[n], out_v[n] = [a.reshape(env[n].shape) for a in res]
    for names, off in ((MISC_EARLY, MISC_EARLY_OFF), (MISC_MID, starts[1] + MISC_MID_OFF), (MISC_LATE, starts[2])):
        pack3 = lambda d: jnp.concatenate([_as_rows(d[n], MISC_SHARD_ROWS[n]) for n in names], axis=0)
        res = adamw(reduced, off, pack3(wl), pack3(ml), pack3(vl), "adamw_packed_" + names[0])
        r0 = 0
        for n in names:
            cnt = math.prod(env[n].shape)
            out_g[n], out_d[n], out_m[n], out_v[n] = [
                a[r0:r0 + MISC_SHARD_ROWS[n]].reshape(-1)[:cnt].reshape(env[n].shape) for a in res]
            r0 += MISC_SHARD_ROWS[n]
    ws = jnp.concatenate([_as_rows(wl[n]) for n in REPLICATED], axis=0)
    ms = jnp.concatenate([_as_rows(ml[n]) for n in REPLICATED], axis=0)
    vs = jnp.concatenate([_as_rows(vl[n]) for n in REPLICATED], axis=0)
    pad = ((0, SMALL_ROWS - ws.shape[0]), (0, 0))
    res = adamw(small_tot, 0, jnp.pad(ws, pad), jnp.pad(ms, pad), jnp.pad(vs, pad), "adamw_replicated")
    row = 0
    for n in REPLICATED:
        cnt = math.prod(env[n].shape)
        nrows = _rows8(env[n])
        out_g[n], out_d[n], out_m[n], out_v[n] = [a[row:row + nrows].reshape(-1)[:cnt].reshape(env[n].shape) for a in res]
        row += nrows

    return (loss, dx[None], *[out_g[n] for n in WEIGHTS], *[out_d[n] for n in WEIGHTS],
            *[out_m[n] for n in WEIGHTS], *[out_v[n] for n in WEIGHTS])
```

```python
import functools
import math

import jax
import jax.numpy as jnp
from jax import lax
from jax.experimental import pallas as pl
from jax.experimental.pallas import tpu as pltpu

F32 = jnp.float32
BF16 = jnp.bfloat16
MESH = pl.DeviceIdType.MESH

D_MODEL = 1024
DEPTH = 2
SSM_GROUP = 16
N_GROUPS = D_MODEL // SSM_GROUP
SSM_STATE = 64
N_STATES = N_GROUPS * SSM_STATE
N_HEADS = 8
QK_NOPE = 128
QK_ROPE = 64
HALF_ROPE = QK_ROPE // 2
V_HEAD = 128
QK_DIM = QK_NOPE + QK_ROPE
Q_LORA = 384
KV_LORA = 256
ROPE_THETA = 10000.0
SM_SCALE = QK_DIM ** -0.5
NEG_INF = -1e30
D_FF = 4 * D_MODEL
DN_ALPHA = (2 * DEPTH) ** 0.25
LN_EPS = 1e-5
RMS_EPS = 1e-6
ADAM_LR = 0.001
ADAM_B1 = 0.9
ADAM_B2 = 0.999
ADAM_EPS = 1e-08
ADAM_WD = 0.01
ADAM_STEP = 10

N_CHIPS = 4
LANES = 128
VMEM_LIMIT = 56 * 1024 * 1024
MM_VMEM_BUDGET = 40 * 1024 * 1024
PACK_W = 1024
KVA_PAD = 384
HALF_W = PACK_W // 2

SHARDED = ("w_ff1", "w_ff2", "ssm_w_glu", "ssm_w_out", "kv_w_a", "kv_w_b", "q_w_a", "q_w_b", "attn_w_o", "ssm_d")
G_BLOCK_ROWS = 960
EARLY_OFF = {"w_ff1": 0, "w_ff2": 2048, "attn_w_o": 4096}
MISC_EARLY = ("kv_w_b", "kv_w_a", "q_w_a", "q_w_b")
MISC_EARLY_OFF = 4352
EARLY_ROWS = 5 * G_BLOCK_ROWS
EARLY_HEAD = G_BLOCK_ROWS
MID_OFF = {"ssm_w_out": 0}
MISC_MID = ("ssm_w_glu",)
MISC_MID_OFF = 256
MID_ROWS = MISC_MID_OFF + 512
MISC_LATE = ("ssm_d",)
SMALL_Q_ROWS = 96
SMALL_ROWS = N_CHIPS * SMALL_Q_ROWS
SMALL_OFF = 16
LATE_ROWS = 192
MISC_SHARD_ROWS = {"ssm_d": 16, "ssm_w_glu": 512, "kv_w_b": 128, "kv_w_a": 80, "q_w_a": 96, "q_w_b": 144}
REPLICATED = ("ln_mix_g", "ln_mix_b", "ln_ffn_g", "ln_ffn_b", "ssm_lam_re", "ssm_lam_im", "ssm_log_dt",
              "ssm_b_re", "ssm_b_im", "ssm_c_re", "ssm_c_im", "kv_norm_g", "q_norm_g")
WEIGHTS = ("ln_mix_g", "ln_mix_b", "ln_ffn_g", "ln_ffn_b", "w_ff1", "w_ff2", "ssm_lam_re", "ssm_lam_im",
           "ssm_log_dt", "ssm_b_re", "ssm_b_im", "ssm_c_re", "ssm_c_im", "ssm_d", "ssm_w_glu", "ssm_w_out",
           "kv_w_a", "kv_norm_g", "kv_w_b", "q_w_a", "q_norm_g", "q_w_b", "attn_w_o")


def _pcall(body, **kw):
    return pl.pallas_call(body, **kw)


def _params(sem=None):
    return pltpu.CompilerParams(dimension_semantics=sem, vmem_limit_bytes=VMEM_LIMIT)


_ANY = pl.BlockSpec(memory_space=pl.ANY)


def _tile(dim, prefs):
    for p in prefs:
        if dim % p == 0:
            return p
    return dim


def _place():
    x, y, c = lax.axis_index("x"), lax.axis_index("y"), lax.axis_index("c")
    return x, y, c, [(1 - x, y), (x, 1 - y), (1 - x, 1 - y)]


def _remote(k, src, dst, to, send_sems, recv_sems):
    return pltpu.make_async_remote_copy(src_ref=src, dst_ref=dst, send_sem=send_sems.at[k], recv_sem=recv_sems.at[k],
                                        device_id=to, device_id_type=MESH)


class GatherRide:
    def __init__(self, arrs):
        self.ins = list(arrs)
        self.out_shapes = [jax.ShapeDtypeStruct(a.shape, a.dtype) for a in arrs]
        self.aliases = {i: i for i in range(len(arrs))}
        self.n_sems = 6 * len(arrs)

    def start(self, ins, outs, send_sems, recv_sems):
        x, y, c, chips = _place()
        me = 2 * x + y
        for a, o in enumerate(outs):
            for j, (px, py) in enumerate(chips):
                _remote(6 * a + j, o.at[me, c], o.at[me, c], (px, py, c), send_sems, recv_sems).start()

    def pass_on(self, ins, outs, send_sems, recv_sems):
        x, y, c, chips = _place()
        for a, o in enumerate(outs):
            for j, (px, py) in enumerate(chips):
                blk = o.at[2 * px + py, c]
                _remote(6 * a + j, blk, blk, (px, py, c), send_sems, recv_sems).wait_recv()
                _remote(6 * a + 3 + j, blk, blk, (x, y, 1 - c), send_sems, recv_sems).start()

    def finish(self, ins, outs, send_sems, recv_sems, passed_on=False):
        if not passed_on:
            self.pass_on(ins, outs, send_sems, recv_sems)
        x, y, c, chips = _place()
        me = 2 * x + y
        sibling = (x, y, 1 - c)
        for a, o in enumerate(outs):
            for j, (px, py) in enumerate(chips):
                blk = o.at[2 * px + py, 1 - c]
                _remote(6 * a + 3 + j, blk, blk, sibling, send_sems, recv_sems).wait_recv()
                _remote(6 * a + j, o.at[me, c], o.at[me, c], (px, py, c), send_sems, recv_sems).wait_send()
                mine = o.at[2 * px + py, c]
                _remote(6 * a + 3 + j, mine, mine, sibling, send_sems, recv_sems).wait_send()


class SendRide:
    base = 0

    def __init__(self, parts):
        parts = [p if isinstance(p, tuple) else (p, (0, p.shape[1]), None) for p in parts]
        self.rows = [rows for _, rows, _ in parts]
        self.n_parts = len(parts)
        self.ins = [p for p, _, _ in parts] + [into for _, _, into in parts if into is not None]
        self.out_shapes = [jax.ShapeDtypeStruct((3,) + p.shape[1:], p.dtype) for p, _, _ in parts]
        given = [a for a, (_, _, into) in enumerate(parts) if into is not None]
        self.aliases = {self.n_parts + i: a for i, a in enumerate(given)}
        self.n_sems = 3 * self.n_parts

    def _copies(self, ins, outs, send_sems, recv_sems):
        x, y, c, chips = _place()
        return [_remote(self.base + 3 * a + j, ins[a].at[2 * px + py, pl.ds(r0, n)], outs[a].at[j, pl.ds(r0, n)],
                        (px, py, c), send_sems, recv_sems)
                for a, (r0, n) in enumerate(self.rows) for j, (px, py) in enumerate(chips)]

    def start(self, ins, outs, send_sems, recv_sems):
        for cp in self._copies(ins, outs, send_sems, recv_sems):
            cp.start()

    def finish(self, ins, outs, send_sems, recv_sems):
        for cp in self._copies(ins, outs, send_sems, recv_sems):
            cp.wait()


class SwapRide:
    base = 0

    def __init__(self, pack, ranges=None, into=None):
        self.ins = [pack] if into is None else [pack, into]
        self.out_shapes = [jax.ShapeDtypeStruct(pack.shape[:2] + (HALF_W,), pack.dtype)]
        self.aliases = {} if into is None else {1: 0}
        self.ranges = ranges or [(0, pack.shape[1])]
        self.n_sems = len(self.ranges)

    def _copies(self, ins, outs, send_sems, recv_sems):
        x, y, c, _ = _place()
        return [_remote(self.base + k, ins[0].at[:, pl.ds(r0, n), _my_cols(c, mine=False)], outs[0].at[:, pl.ds(r0, n), :],
                        (x, y, 1 - c), send_sems, recv_sems) for k, (r0, n) in enumerate(self.ranges)]

    def start(self, ins, outs, send_sems, recv_sems):
        for cp in self._copies(ins, outs, send_sems, recv_sems):
            cp.start()

    def finish(self, ins, outs, send_sems, recv_sems):
        for cp in self._copies(ins, outs, send_sems, recv_sems):
            cp.wait()


class Together:
    def __init__(self, rides):
        self.rides = rides
        self.ins, self.out_shapes, self.aliases, self.n_sems = [], [], {}, 0
        for r in rides:
            r.base = self.n_sems
            self.aliases.update({len(self.ins) + i: len(self.out_shapes) + o for i, o in r.aliases.items()})
            self.ins += r.ins
            self.out_shapes += r.out_shapes
            self.n_sems += r.n_sems

    def _each(self, step, ins, outs, send_sems, recv_sems):
        i = o = 0
        for r in self.rides:
            getattr(r, step)(ins[i:i + len(r.ins)], outs[o:o + len(r.out_shapes)], send_sems, recv_sems)
            i, o = i + len(r.ins), o + len(r.out_shapes)

    def start(self, *refs):
        self._each("start", *refs)

    def finish(self, *refs):
        self._each("finish", *refs)


def _pcall_riding(body, args, ride, first, last, *, in_specs, out_specs, out_shape, scratch_shapes=(), middle=None,
                  in_place=None, **kw):
    n_in, n_out = len(args), len(out_shape)
    in_place = in_place or {}
    if ride is None:
        return _pcall(body, in_specs=in_specs, out_specs=out_specs, out_shape=out_shape,
                      input_output_aliases=in_place, scratch_shapes=list(scratch_shapes), **kw)(*args), []
    k_in, k_out = len(ride.ins), len(ride.out_shapes)

    def riding(*refs):
        ins, r_in = refs[:n_in], refs[n_in:n_in + k_in]
        outs = refs[n_in + k_in:n_in + k_in + n_out]
        r_out = refs[n_in + k_in + n_out:n_in + k_in + n_out + k_out]
        scratch, (send_sems, recv_sems) = refs[n_in + k_in + n_out + k_out:-2], refs[-2:]

        @pl.when(first())
        def _():
            ride.start(r_in, r_out, send_sems, recv_sems)

        if middle is not None:
            @pl.when(middle())
            def _():
                ride.pass_on(r_in, r_out, send_sems, recv_sems)

        body(*ins, *outs, *scratch)

        @pl.when(last())
        def _():
            if middle is not None:
                ride.finish(r_in, r_out, send_sems, recv_sems, passed_on=True)
            else:
                ride.finish(r_in, r_out, send_sems, recv_sems)

    res = _pcall(riding, in_specs=list(in_specs) + [_ANY] * k_in, out_specs=list(out_specs) + [_ANY] * k_out,
                 out_shape=list(out_shape) + ride.out_shapes,
                 input_output_aliases={**in_place, **{n_in + i: n_out + o for i, o in ride.aliases.items()}},
                 scratch_shapes=list(scratch_shapes) + [pltpu.SemaphoreType.DMA((ride.n_sems,))] * 2,
                 **kw)(*args, *ride.ins)
    return res[:n_out], res[n_out:]


def ride_alone(ride, name):
    def body(*refs):
        n = len(ride.ins)
        ins, outs, (send_sems, recv_sems) = refs[:n], refs[n:-2], refs[-2:]
        ride.start(ins, outs, send_sems, recv_sems)
        ride.finish(ins, outs, send_sems, recv_sems)

    return _pcall(body, name=name, in_specs=[_ANY] * len(ride.ins), out_specs=[_ANY] * len(ride.out_shapes),
                  out_shape=ride.out_shapes, input_output_aliases=dict(ride.aliases),
                  scratch_shapes=[pltpu.SemaphoreType.DMA((ride.n_sems,))] * 2)(*ride.ins)


def mm(a, b, *, name, ta=False, tb=False, pro_a=None, epi=None, extras=(), out_dtypes=(F32,), n_dim=None,
       tiles=(None, None, None), b_view=None, out_view=None, into=None, ride=None, accs=()):
    widths = [d[0] if isinstance(d, tuple) else None for d in out_dtypes]
    out_dtypes = [d[1] if isinstance(d, tuple) else d for d in out_dtypes]
    if ta:
        k_dim, m_dim = a.shape
    else:
        m_dim, k_dim = a.shape
    if n_dim is None:
        n_dim = b.shape[0] if tb else b.shape[1]
    tn = tiles[1] or (n_dim if n_dim <= 1024 else _tile(n_dim, (1024, 512, 256, 128)))
    tk = tiles[2] or (k_dim if k_dim <= 1024 else _tile(k_dim, (1024, 512, 256, 128)))
    nk = k_dim // tk

    def vmem_bytes(tm):
        blocks = tm * tk * a.dtype.itemsize + tk * tn * b.dtype.itemsize
        blocks += sum(tm * (tn if e.shape[1] == n_dim else e.shape[1]) * e.dtype.itemsize
                      for e in extras if e.ndim == 2 and e.shape[0] > 1)
        blocks += sum(e.shape[0] * tm * e.shape[2] * e.dtype.itemsize for e in extras if e.ndim == 3)
        blocks += tm * sum((w or tn) * jnp.dtype(d).itemsize for w, d in zip(widths, out_dtypes))
        return 2 * blocks + tm * tn * 4

    tm = tiles[0] or next((t for t in (4096, 2048, 1024, 512, 256) if m_dim % t == 0 and vmem_bytes(t) <= MM_VMEM_BUDGET),
                          _tile(m_dim, (128,)))
    assert m_dim % tm == 0 and n_dim % tn == 0 and k_dim % tk == 0, (name, m_dim, n_dim, k_dim, tm, tn, tk)
    assert tn == n_dim or not (any(widths) or accs), name
    n_ex, n_out = len(extras), len(out_dtypes)
    n_into = 0 if into is None else 1
    dims = (((0 if ta else 1,), (1 if tb else 0,)), ((), ()))

    def body(a_ref, b_ref, *rest):
        ex_refs, out_refs = rest[:n_ex], rest[n_ex + n_into:n_ex + n_into + n_out]
        sum_refs = rest[n_ex + n_into + n_out:n_ex + n_into + n_out + len(accs)]

        def partial():
            av = a_ref[...]
            if pro_a is not None:
                av = pro_a(av)
            return lax.dot_general(av.astype(BF16), b_ref[...].astype(BF16), dims, preferred_element_type=F32)

        def finish(r):
            res = epi(r, *[e[...] for e in ex_refs]) if epi is not None else (r,)
            for o_ref, v in zip(out_refs, res):
                o_ref[...] = v.reshape(o_ref.shape).astype(o_ref.dtype)
            if accs:
                @pl.when(pl.program_id(0) == 0)
                def _():
                    for s_ref in sum_refs:
                        s_ref[...] = jnp.zeros_like(s_ref)

                for s_ref, v in zip(sum_refs, res[n_out:]):
                    s_ref[...] += v

        if nk == 1:
            finish(partial())
            return
        acc = rest[-1]
        k = pl.program_id(2)

        @pl.when(k == 0)
        def _():
            acc[...] = partial()

        @pl.when(k > 0)
        def _():
            acc[...] += partial()

        @pl.when(k == nk - 1)
        def _():
            finish(acc[...])

    def ex_spec(e):
        if e.ndim == 3:
            return pl.BlockSpec((e.shape[0], tm, e.shape[2]), lambda i, j, k: (0, i, 0))
        if e.shape == (m_dim, n_dim):
            return o_spec
        if e.shape[0] == m_dim:
            return pl.BlockSpec((tm, e.shape[1]), lambda i, j, k: (i, 0))
        return pl.BlockSpec(e.shape, lambda i, j, k: (0, 0))

    a_spec = pl.BlockSpec((tk, tm), lambda i, j, k: (k, i)) if ta else pl.BlockSpec((tm, tk), lambda i, j, k: (i, k))
    if b_view is not None:
        b_spec = b_view(tk, tn)
    else:
        b_spec = pl.BlockSpec((tn, tk), lambda i, j, k: (j, k)) if tb else pl.BlockSpec((tk, tn), lambda i, j, k: (k, j))
    o_spec = pl.BlockSpec((tm, tn), lambda i, j, k: (i, j))
    if out_view is None:
        out_specs = [o_spec if w is None else pl.BlockSpec((tm, w), lambda i, j, k: (i, 0)) for w in widths]
        out_shape = [jax.ShapeDtypeStruct((m_dim, w or n_dim), dt) for w, dt in zip(widths, out_dtypes)]
    else:
        assert n_out == 1
        out_specs = [out_view[1](tm, tn)]
        out_shape = [jax.ShapeDtypeStruct(out_view[0], out_dtypes[0])]
    out_specs = out_specs + [pl.BlockSpec((1, w), lambda i, j, k: (0, 0)) for w in accs]
    out_shape = out_shape + [jax.ShapeDtypeStruct((1, w), F32) for w in accs]
    grid = (m_dim // tm, n_dim // tn, nk)
    scratch = [pltpu.VMEM((tm, tn), F32)] if nk > 1 else []
    if ride is not None:
        assert into is None
        at = lambda ids: functools.reduce(jnp.logical_and, [pl.program_id(d) == i for d, i in enumerate(ids)])
        outs, landed = _pcall_riding(
            body, (a, b, *extras), ride, lambda: at((0, 0, 0)), lambda: at([g - 1 for g in grid]),
            name=name, grid=grid, in_specs=[a_spec, b_spec] + [ex_spec(e) for e in extras], out_specs=out_specs,
            out_shape=out_shape, scratch_shapes=scratch, compiler_params=_params(("arbitrary",) * 3))
        return (outs[0] if len(outs) == 1 else outs), landed
    outs = _pcall(
        body, name=name, grid=grid,
        in_specs=[a_spec, b_spec] + [ex_spec(e) for e in extras] + [_ANY] * n_into,
        out_specs=out_specs, out_shape=out_shape,
        input_output_aliases={2 + n_ex: 0} if n_into else {},
        scratch_shapes=scratch,
        compiler_params=_params(("arbitrary",) * 3 if accs else ("parallel", "parallel", "arbitrary")),
    )(a, b, *extras, *([into] if n_into else []))
    return outs[0] if len(outs) == 1 else outs


def rowwise(fn, ins, outs, *, name, accs=(), tm=256):
    rows = ins[0].shape[0]
    tm = min(tm, rows)
    n_in, n_out, n_acc = len(ins), len(outs), len(accs)

    def body(*refs):
        in_refs, out_refs, acc_refs = refs[:n_in], refs[n_in:n_in + n_out], refs[n_in + n_out:]
        res, sums = fn(*[r[...] for r in in_refs])
        for o_ref, v in zip(out_refs, res):
            o_ref[...] = v.astype(o_ref.dtype)
        if n_acc:
            @pl.when(pl.program_id(0) == 0)
            def _():
                for a_ref in acc_refs:
                    a_ref[...] = jnp.zeros_like(a_ref)

            for a_ref, s in zip(acc_refs, sums):
                a_ref[...] += s

    def spec(arr):
        if arr.shape[0] == rows:
            return pl.BlockSpec((tm, arr.shape[1]), lambda i: (i, 0))
        return pl.BlockSpec(arr.shape, lambda i: (0, 0))

    res = _pcall(
        body, name=name, grid=(rows // tm,),
        in_specs=[spec(a) for a in ins],
        out_specs=[pl.BlockSpec((tm, w), lambda i: (i, 0)) for w, _ in outs]
        + [pl.BlockSpec((1, w), lambda i: (0, 0)) for w in accs],
        out_shape=[jax.ShapeDtypeStruct((rows, w), dt) for w, dt in outs]
        + [jax.ShapeDtypeStruct((1, w), F32) for w in accs],
        compiler_params=_params(("arbitrary",) if n_acc else ("parallel",)),
    )(*ins)
    return res


def _relu2(v):
    r = jnp.maximum(v, 0.0)
    return r * r


def _gelu(x):
    c = math.sqrt(2.0 / math.pi)
    return 0.5 * x * (1.0 + jnp.tanh(c * (x + 0.044715 * x * x * x)))


def _gelu_grad(x):
    c = math.sqrt(2.0 / math.pi)
    t = jnp.tanh(c * (x + 0.044715 * x * x * x))
    return 0.5 * (1.0 + t) + 0.5 * x * (1.0 - t * t) * c * (1.0 + 3 * 0.044715 * x * x)


def _sigmoid(x):
    return 1.0 / (1.0 + jnp.exp(-x))


def _layer_norm(h, mix, g, b):
    r = DN_ALPHA * h + mix
    mu = jnp.mean(r, axis=-1, keepdims=True)
    xc = r - mu
    var = jnp.mean(xc * xc, axis=-1, keepdims=True)
    return xc * lax.rsqrt(var + LN_EPS) * g + b


def _layer_norm_bwd(h, mix, g, dy):
    r = DN_ALPHA * h + mix
    mu = jnp.mean(r, axis=-1, keepdims=True)
    xc = r - mu
    var = jnp.mean(xc * xc, axis=-1, keepdims=True)
    rstd = lax.rsqrt(var + LN_EPS)
    xhat = xc * rstd
    dxh = dy * g
    m1 = jnp.mean(dxh, axis=-1, keepdims=True)
    m2 = jnp.mean(dxh * xhat, axis=-1, keepdims=True)
    dr = rstd * (dxh - m1 - xhat * m2)
    return dr, jnp.sum(dy * xhat, axis=0, keepdims=True), jnp.sum(dy, axis=0, keepdims=True)


def ln_bwd(h, mix, g, dy, name):
    def fn(h, mix, g, dy):
        dr, dg, db = _layer_norm_bwd(h, mix, g, dy)
        return (dr, dr), (dg, db)
    return rowwise(fn, (h, mix, g, dy), ((D_MODEL, F32), (D_MODEL, BF16)), accs=(D_MODEL, D_MODEL), name=name, tm=512)


def _rms(x, g):
    r = lax.rsqrt(jnp.mean(x * x, axis=-1, keepdims=True) + RMS_EPS)
    return x * r * g


def _rms_bwd(x, g, dy):
    r = lax.rsqrt(jnp.mean(x * x, axis=-1, keepdims=True) + RMS_EPS)
    xn = x * r
    dyg = dy * g
    dx = r * (dyg - xn * jnp.mean(dyg * xn, axis=-1, keepdims=True))
    return dx, jnp.sum(dy * xn, axis=0, keepdims=True)


def _s5_disc(lr, li, ldt):
    dt = jnp.exp(ldt)
    mag = jnp.exp(lr * dt)
    cs, sn = jnp.cos(li * dt), jnp.sin(li * dt)
    ar, ai = mag * cs, mag * sn
    inv = 1.0 / (lr * lr + li * li)
    n_re = (ar - 1.0) * lr + ai * li
    n_im = ai * lr - (ar - 1.0) * li
    return dt, mag, cs, sn, ar, ai, inv, n_re, n_im


def s5_prep(lr, li, ldt, b_re, b_im):
    def fn(lr, li, ldt, b_re, b_im):
        _, _, _, _, ar, ai, inv, n_re, n_im = _s5_disc(lr, li, ldt)
        cr, ci = n_re * inv, n_im * inv
        return (ar, ai, cr * b_re - ci * b_im, cr * b_im + ci * b_re), ()
    return rowwise(fn, (lr, li, ldt, b_re, b_im), ((1, F32), (1, F32), (SSM_GROUP, F32), (SSM_GROUP, F32)),
                   name="s5_prep", tm=512)


def s5_prep_bwd(lr, li, ldt, b_re, b_im, dar, dai, dbb_re, dbb_im):
    def fn(lr, li, ldt, b_re, b_im, dar, dai, dbb_re, dbb_im):
        dt, mag, cs, sn, ar, ai, inv, n_re, n_im = _s5_disc(lr, li, ldt)
        cr, ci = n_re * inv, n_im * inv
        db_re = cr * dbb_re + ci * dbb_im
        db_im = cr * dbb_im - ci * dbb_re
        dcr = jnp.sum(dbb_re * b_re + dbb_im * b_im, axis=-1, keepdims=True)
        dci = jnp.sum(dbb_im * b_re - dbb_re * b_im, axis=-1, keepdims=True)
        dar = dar + (dcr * lr - dci * li) * inv
        dai = dai + (dcr * li + dci * lr) * inv
        dinv = dcr * n_re + dci * n_im
        dlr = (dcr * (ar - 1.0) + dci * ai) * inv - 2.0 * lr * inv * inv * dinv
        dli = (dcr * ai - dci * (ar - 1.0)) * inv - 2.0 * li * inv * inv * dinv
        dmag = dar * cs + dai * sn
        dth = dai * ar - dar * ai
        dlr = dlr + dmag * mag * dt
        dli = dli + dth * dt
        ddt = dmag * mag * lr + dth * li
        return (dlr, dli, ddt * dt, db_re, db_im), ()
    return rowwise(fn, (lr, li, ldt, b_re, b_im, dar, dai, dbb_re, dbb_im),
                   ((1, F32), (1, F32), (1, F32), (SSM_GROUP, F32), (SSM_GROUP, F32)), name="s5_prep_bwd", tm=512)


def group_sum(x):
    def body(x_ref, o_ref):
        o_ref[...] = jnp.sum(x_ref[...], axis=1)
    return _pcall(body, name="s5_group_sum", out_shape=jax.ShapeDtypeStruct((N_GROUPS, 1), F32))(
        x.reshape(N_GROUPS, SSM_STATE, 1))


GROUPS_PER_TILE = LANES // SSM_GROUP
TILE_STATES = GROUPS_PER_TILE * SSM_STATE
N_UTILES = D_MODEL // LANES


SUBLANES = 8
SCAN_STRIP = 1024
N_STRIPS = N_STATES // SCAN_STRIP
_NT = (((1,), (1,)), ((), ()))
_TN = (((0,), (0,)), ((), ()))


def _scan_coefs(are, aim, shifted, reverse):
    ar = are[...]
    ai = -aim[...] if reverse else aim[...]
    powers = {1: (ar, ai)}
    for d in (2, 4):
        r, i = powers[d // 2]
        powers[d] = (r * r - i * i, 2.0 * r * i)
    rid = lax.broadcasted_iota(jnp.int32, (SUBLANES, N_STATES), 0)
    first = (rid == SUBLANES - 1) if reverse else (rid == 0)
    masks = [(1, first)] + [(d, (rid <= SUBLANES - 1 - d) if reverse else (rid >= d)) for d in (1, 2, 4)]
    for n, (d, keep) in enumerate(masks):
        for part in (0, 1):
            shifted[2 * n + part][...] = jnp.where(keep, jnp.broadcast_to(powers[d][part], (SUBLANES, N_STATES)), 0.0)


def _tile_scan(xr, xi, shifted, nbr_re, nbr_im, reverse):
    for n, d in enumerate((1, 1, 2, 4)):
        by = SUBLANES - d if reverse else d
        fr, fi = (nbr_re, nbr_im) if n == 0 else (xr, xi)
        sr, si = pltpu.roll(fr, by, 0), pltpu.roll(fi, by, 0)
        kr, ki = shifted[2 * n], shifted[2 * n + 1]
        xr, xi = xr + kr * sr - ki * si, xi + kr * si + ki * sr
    return xr, xi


def _tile_rows(t):
    return pl.ds(pl.multiple_of(t * SUBLANES, SUBLANES), SUBLANES)


def s5_fwd(u, bbd_re, bbd_im, cbd_re, cbd_imn, a_re, a_im, dskip, ride=None, t_rows=256):
    seq = u.shape[0]
    t_rows = min(t_rows, seq)
    n_tiles = t_rows // SUBLANES

    def body(u_ref, bre, bim, cre, cimn, are, aim, d_ref, y_ref, gelu_ref, hre_ref, him_ref, car_re, car_im, *shifted):
        @pl.when(pl.program_id(0) == 0)
        def _():
            car_re[...] = jnp.zeros_like(car_re)
            car_im[...] = jnp.zeros_like(car_im)
            _scan_coefs(are, aim, shifted, reverse=False)

        uf = u_ref[...]
        ub = uf.astype(BF16)
        for j in range(N_UTILES):
            uj = ub[:, LANES * j:LANES * (j + 1)]
            sl = slice(TILE_STATES * j, TILE_STATES * (j + 1))
            hre_ref[:, sl] = jnp.dot(uj, bre[j], preferred_element_type=F32)
            him_ref[:, sl] = jnp.dot(uj, bim[j], preferred_element_type=F32)
        for s in range(N_STRIPS):
            cols = pl.ds(s * SCAN_STRIP, SCAN_STRIP)
            coefs = [c[:, cols] for c in shifted]

            def step(t, before):
                rows = _tile_rows(t)
                hr, hi = _tile_scan(hre_ref[rows, cols], him_ref[rows, cols], coefs, before[0], before[1], False)
                hre_ref[rows, cols] = hr
                him_ref[rows, cols] = hi
                return hr, hi

            cr, ci = lax.fori_loop(0, n_tiles, step, (car_re[:, cols], car_im[:, cols]))
            car_re[:, cols] = cr
            car_im[:, cols] = ci
        dv = d_ref[...]
        for j in range(N_UTILES):
            st = slice(TILE_STATES * j, TILE_STATES * (j + 1))
            yj = (jnp.dot(hre_ref[:, st].astype(BF16), cre[j], preferred_element_type=F32)
                  + jnp.dot(him_ref[:, st].astype(BF16), cimn[j], preferred_element_type=F32))
            sl = slice(LANES * j, LANES * (j + 1))
            yj = yj + dv[:, sl] * uf[:, sl]
            y_ref[:, sl] = yj
            gelu_ref[:, sl] = _gelu(yj).astype(gelu_ref.dtype)

    full3 = lambda a: pl.BlockSpec(a.shape, lambda i: (0, 0, 0))
    full2 = lambda a: pl.BlockSpec(a.shape, lambda i: (0, 0))
    tile = pltpu.VMEM((SUBLANES, N_STATES), F32)
    n_chunks = seq // t_rows
    return _pcall_riding(
        body, (u, bbd_re, bbd_im, cbd_re, cbd_imn, a_re, a_im, dskip), ride,
        lambda: pl.program_id(0) == 0, lambda: pl.program_id(0) == n_chunks - 1,
        middle=(lambda: pl.program_id(0) == (7 * n_chunks) // 8) if ride is not None else None,
        name="s5_fwd", grid=(n_chunks,),
        in_specs=[pl.BlockSpec((t_rows, D_MODEL), lambda i: (i, 0)), full3(bbd_re), full3(bbd_im), full3(cbd_re),
                  full3(cbd_imn), full2(a_re), full2(a_im), full2(dskip)],
        out_specs=[pl.BlockSpec((t_rows, D_MODEL), lambda i: (i, 0)),
                   pl.BlockSpec((t_rows, D_MODEL), lambda i: (i, 0)),
                   pl.BlockSpec((t_rows, N_STATES), lambda i: (i, 0)),
                   pl.BlockSpec((t_rows, N_STATES), lambda i: (i, 0))],
        out_shape=[jax.ShapeDtypeStruct((seq, D_MODEL), F32),
                   jax.ShapeDtypeStruct((seq, D_MODEL), BF16),
                   jax.ShapeDtypeStruct((seq, N_STATES), F32),
                   jax.ShapeDtypeStruct((seq, N_STATES), F32)],
        scratch_shapes=[tile] * 10,
        compiler_params=_params(("arbitrary",)))


def s5_bwd(dy, u, dres, h_re, h_im, bbd_re, bbd_im, cbd_re, cbd_imn, a_re, a_im, dskip, ride=None, t_rows=256):
    seq = u.shape[0]
    t_rows = min(t_rows, seq)
    n_chunks = seq // t_rows

    n_tiles = t_rows // SUBLANES

    def body(dy_ref, u_ref, dres_ref, hre_ref, him_ref, hpre_ref, hpim_ref, bre, bim, cre, cimn, are, aim, d_ref,
             dx_ref, dbre, dbim, dcre, dcimn, dar_ref, dai_ref, dd_ref, lre, lim, car_re, car_im, acc_re, acc_im,
             *shifted):
        i = pl.program_id(0)

        @pl.when(i == 0)
        def _():
            for r in (car_re, car_im, acc_re, acc_im, dbre, dbim, dcre, dcimn, dd_ref):
                r[...] = jnp.zeros_like(r)
            _scan_coefs(are, aim, shifted, reverse=True)

        dyf = dy_ref[...]
        dyb = dyf.astype(BF16)
        uf = u_ref[...]
        ub = uf.astype(BF16)
        for j in range(N_UTILES):
            dyj = dyb[:, LANES * j:LANES * (j + 1)]
            st = slice(TILE_STATES * j, TILE_STATES * (j + 1))
            lre[:, st] = lax.dot_general(dyj, cre[j], _NT, preferred_element_type=F32)
            lim[:, st] = lax.dot_general(dyj, cimn[j], _NT, preferred_element_type=F32)
        has_pred = (i < n_chunks - 1).astype(F32)
        last_row = lax.broadcasted_iota(jnp.int32, (SUBLANES, SCAN_STRIP), 0) == SUBLANES - 1
        for s in range(N_STRIPS):
            cols = pl.ds(s * SCAN_STRIP, SCAN_STRIP)
            coefs = [c[:, cols] for c in shifted]
            before_re, before_im = hpre_ref[:, cols] * has_pred, hpim_ref[:, cols] * has_pred

            def step(k, carry):
                after_re, after_im, dar, dai = carry
                t = n_tiles - 1 - k
                rows = _tile_rows(t)
                lr, li = _tile_scan(lre[rows, cols], lim[rows, cols], coefs, after_re, after_im, True)
                lre[rows, cols] = lr
                lim[rows, cols] = li
                prev = _tile_rows(jnp.maximum(t - 1, 0))
                pre_re = jnp.where(t == 0, before_re, hre_ref[prev, cols])
                pre_im = jnp.where(t == 0, before_im, him_ref[prev, cols])
                hpr = pltpu.roll(jnp.where(last_row, pre_re, hre_ref[rows, cols]), 1, 0)
                hpi = pltpu.roll(jnp.where(last_row, pre_im, him_ref[rows, cols]), 1, 0)
                return lr, li, dar + lr * hpr + li * hpi, dai + li * hpr - lr * hpi

            cr, ci, dar, dai = lax.fori_loop(0, n_tiles, step, (car_re[:, cols], car_im[:, cols],
                                                               acc_re[:, cols], acc_im[:, cols]))
            car_re[:, cols] = cr
            car_im[:, cols] = ci
            acc_re[:, cols] = dar
            acc_im[:, cols] = dai

        dv = d_ref[...]
        for j in range(N_UTILES):
            sl = slice(LANES * j, LANES * (j + 1))
            st = slice(TILE_STATES * j, TILE_STATES * (j + 1))
            lrj = lre[:, st].astype(BF16)
            lij = lim[:, st].astype(BF16)
            du = (lax.dot_general(lrj, bre[j], _NT, preferred_element_type=F32)
                  + lax.dot_general(lij, bim[j], _NT, preferred_element_type=F32))
            dx_ref[:, sl] = du + dv[:, sl] * dyf[:, sl] + DN_ALPHA * dres_ref[:, sl]
            uj = ub[:, sl]
            dbre[j] += lax.dot_general(uj, lrj, _TN, preferred_element_type=F32)
            dbim[j] += lax.dot_general(uj, lij, _TN, preferred_element_type=F32)
            dyj = dyb[:, sl]
            dcre[j] += lax.dot_general(hre_ref[:, st].astype(BF16), dyj, _TN, preferred_element_type=F32)
            dcimn[j] += lax.dot_general(him_ref[:, st].astype(BF16), dyj, _TN, preferred_element_type=F32)
        dd_ref[...] += jnp.sum(dyf * uf, axis=0, keepdims=True)

        @pl.when(i == n_chunks - 1)
        def _():
            dar_ref[...] = jnp.sum(acc_re[...], axis=0, keepdims=True)
            dai_ref[...] = jnp.sum(acc_im[...], axis=0, keepdims=True)

    rev = lambda i: (n_chunks - 1 - i, 0)
    prev_tile = lambda i: (jnp.maximum((n_chunks - 1 - i) * n_tiles - 1, 0), 0)
    once = pl.Buffered(1)
    full3 = lambda a: pl.BlockSpec(a.shape, lambda i: (0, 0, 0), pipeline_mode=once)
    full2 = lambda a: pl.BlockSpec(a.shape, lambda i: (0, 0), pipeline_mode=once)
    acc3 = lambda shape: pl.BlockSpec(shape, lambda i: (0, 0, 0))
    acc2 = lambda shape: pl.BlockSpec(shape, lambda i: (0, 0))
    tile = pltpu.VMEM((SUBLANES, N_STATES), F32)
    return _pcall_riding(
        body, (dy, u, dres, h_re, h_im, h_re, h_im, bbd_re, bbd_im, cbd_re, cbd_imn, a_re, a_im, dskip), ride,
        lambda: pl.program_id(0) == 0, lambda: pl.program_id(0) == n_chunks - 1,
        name="s5_bwd", grid=(n_chunks,),
        in_specs=[pl.BlockSpec((t_rows, D_MODEL), rev), pl.BlockSpec((t_rows, D_MODEL), rev),
                  pl.BlockSpec((t_rows, D_MODEL), rev),
                  pl.BlockSpec((t_rows, N_STATES), rev), pl.BlockSpec((t_rows, N_STATES), rev),
                  pl.BlockSpec((SUBLANES, N_STATES), prev_tile), pl.BlockSpec((SUBLANES, N_STATES), prev_tile),
                  full3(bbd_re), full3(bbd_im), full3(cbd_re), full3(cbd_imn), full2(a_re), full2(a_im), full2(dskip)],
        out_specs=[pl.BlockSpec((t_rows, D_MODEL), rev), acc3(bbd_re.shape), acc3(bbd_im.shape), acc3(cbd_re.shape),
                   acc3(cbd_imn.shape), acc2((1, N_STATES)), acc2((1, N_STATES)), acc2((1, D_MODEL))],
        out_shape=[jax.ShapeDtypeStruct((seq, D_MODEL), F32), jax.ShapeDtypeStruct(bbd_re.shape, F32),
                   jax.ShapeDtypeStruct(bbd_im.shape, F32), jax.ShapeDtypeStruct(cbd_re.shape, F32),
                   jax.ShapeDtypeStruct(cbd_imn.shape, F32), jax.ShapeDtypeStruct((1, N_STATES), F32),
                   jax.ShapeDtypeStruct((1, N_STATES), F32), jax.ShapeDtypeStruct((1, D_MODEL), F32)],
        scratch_shapes=[pltpu.VMEM((t_rows, N_STATES), F32), pltpu.VMEM((t_rows, N_STATES), F32)] + [tile] * 12,
        in_place={2: 0},
        compiler_params=_params(("arbitrary",)))


def _eye_groups():
    return jnp.eye(GROUPS_PER_TILE, dtype=F32)


def _blockdiag_in(bb):
    t = bb.transpose(0, 2, 1).reshape(N_UTILES, GROUPS_PER_TILE, SSM_GROUP, SSM_STATE)
    bd = jnp.einsum("jgcp,gh->jgchp", t, _eye_groups())
    return bd.reshape(N_UTILES, LANES, TILE_STATES)


def _blockdiag_in_t(d):
    t = jnp.einsum("jgchp,gh->jgcp", d.reshape(N_UTILES, GROUPS_PER_TILE, SSM_GROUP, GROUPS_PER_TILE, SSM_STATE),
                   _eye_groups())
    return t.reshape(N_GROUPS, SSM_GROUP, SSM_STATE).transpose(0, 2, 1)


def _blockdiag_out(c):
    t = c.transpose(0, 2, 1).reshape(N_UTILES, GROUPS_PER_TILE, SSM_STATE, SSM_GROUP)
    bd = jnp.einsum("jhpc,hg->jhpgc", t, _eye_groups())
    return bd.reshape(N_UTILES, TILE_STATES, LANES)


def _blockdiag_out_t(d):
    t = jnp.einsum("jhpgc,hg->jhpc", d.reshape(N_UTILES, GROUPS_PER_TILE, SSM_STATE, GROUPS_PER_TILE, SSM_GROUP),
                   _eye_groups())
    return t.reshape(N_GROUPS, SSM_STATE, SSM_GROUP).transpose(0, 2, 1)


ATT_TQ = 512
ATT_TK = 512
LOG2E = math.log2(math.e)
LN2 = math.log(2.0)
Q_PRESCALE = SM_SCALE * LOG2E


def _loop_in_pairs(n, step, carry, start=0):
    pairs = (n - start) // 2

    def two(t, c):
        return step(start + 2 * t + 1, step(start + 2 * t, c))

    carry = lax.fori_loop(0, pairs, two, carry)
    return lax.fori_loop(start + 2 * pairs, n, step, carry)


def _causal(s, transposed=False):
    r = lax.broadcasted_iota(jnp.int32, s.shape, 0)
    c = lax.broadcasted_iota(jnp.int32, s.shape, 1)
    return jnp.where((r <= c) if transposed else (c <= r), s, NEG_INF)


def _q_specs(rows, at):
    def nope(*ids):
        r, h = at(*ids)
        return r, 3 * (h // HEADS_PER_CHIP) + h % HEADS_PER_CHIP

    def rope(*ids):
        r, h = at(*ids)
        return r, 3 * (h // HEADS_PER_CHIP) + HEADS_PER_CHIP

    return [pl.BlockSpec((rows, LANES), nope), pl.BlockSpec((rows, LANES), rope)]


def _kv_specs(rows, at):
    def col(f):
        def index(*ids):
            r, h = at(*ids)
            return r, f(h)
        return index

    return [pl.BlockSpec((rows, LANES), col(lambda h: 2 * h)), pl.BlockSpec((rows, LANES), col(lambda h: h % HEADS_PER_CHIP)),
            pl.BlockSpec((rows, LANES), col(lambda h: 2 * h + 1))]


def _cat(a, b):
    return jnp.concatenate([a, b], axis=1)


def attn_fwd(q, kv, kr, ride=None, tq=ATT_TQ, tk=ATT_TK):
    seq = q.shape[0]
    n_heads = N_HEADS
    tq, tk = min(tq, seq), min(tk, seq)
    assert tq == tk

    def body(qn_ref, qr_ref, kn_ref, kr_ref, v_ref, o_ref, lse_ref):
        qi = pl.program_id(1)
        qv = _cat(qn_ref[...], qr_ref[...])
        jd = qi

        def block(j, carry, diag):
            m, l, acc = carry
            rows = pl.ds(pl.multiple_of(j * tk, tk), tk)
            s = lax.dot_general(qv, _cat(kn_ref[rows, :], kr_ref[rows, :]), _NT, preferred_element_type=F32)
            if diag:
                s = _causal(s)
            m_new = jnp.maximum(m, jnp.max(s, axis=-1, keepdims=True))
            p = jnp.exp2(s - m_new)
            corr = jnp.exp2(m - m_new)
            l = l * corr + jnp.sum(p, axis=-1, keepdims=True)
            acc = acc * corr + jnp.dot(p.astype(BF16), v_ref[rows, :], preferred_element_type=F32)
            return m_new, l, acc

        init = (jnp.full((tq, 1), NEG_INF, F32), jnp.zeros((tq, 1), F32), jnp.zeros((tq, V_HEAD), F32))
        carry = _loop_in_pairs(jd, lambda j, c: block(j, c, False), init)
        m, l, acc = block(jd, carry, True)
        o_ref[...] = acc / l
        lse_ref[...] = jnp.transpose(jnp.broadcast_to(m + jnp.log2(l), (tq, LANES)))[:1, :]

    n_q = seq // tq
    return _pcall_riding(
        body, (q, q, kv, kr, kv), ride,
        lambda: (pl.program_id(0) == 0) & (pl.program_id(1) == 0),
        lambda: (pl.program_id(0) == n_heads - 1) & (pl.program_id(1) == n_q - 1),
        middle=(lambda: (pl.program_id(0) == (5 * n_heads) // 8) & (pl.program_id(1) == 0)) if ride is not None else None,
        name="attn_fwd", grid=(n_heads, n_q),
        in_specs=_q_specs(tq, lambda h, i: (i, h)) + _kv_specs(seq, lambda h, i: (0, h)),
        out_specs=[pl.BlockSpec((tq, V_HEAD), lambda h, i: (i, h)),
                   pl.BlockSpec((None, None, 1, tq), lambda h, i: (h, i, 0, 0))],
        out_shape=[jax.ShapeDtypeStruct((seq, n_heads * V_HEAD), F32),
                   jax.ShapeDtypeStruct((n_heads, n_q, 1, tq), F32)],
        compiler_params=_params(("arbitrary", "arbitrary")))


def attn_bwd(q, kv, kr, do, lse_row, delta_row, tq=ATT_TK):
    seq = q.shape[0]
    tq = min(tq, seq)
    n_blk = seq // tq

    def body(qn_ref, qr_ref, kn_ref, kr_ref, v_ref, do_ref, lse_ref, delta_ref, dqn_ref, dqr_ref, dkv_ref, dkr_ref, dq_acc):
        head, kj = pl.program_id(0), pl.program_id(1)

        @pl.when(kj == 0)
        def _():
            dq_acc[...] = jnp.zeros_like(dq_acc)

        kc = _cat(kn_ref[...], kr_ref[...])
        vv = v_ref[...]

        def block(i, carry, diag):
            dk, dv = carry
            rows = pl.ds(pl.multiple_of(i * tq, tq), tq)
            qv = _cat(qn_ref[rows, :], qr_ref[rows, :])
            st = lax.dot_general(kc, qv, _NT, preferred_element_type=F32)
            if diag:
                st = _causal(st, transposed=True)
            pt = jnp.exp2(st - lse_ref[0, pl.ds(i, 1), :])
            dob = do_ref[rows, :].astype(BF16)
            dv = dv + jnp.dot(pt.astype(BF16), dob, preferred_element_type=F32)
            dpt = lax.dot_general(vv, dob, _NT, preferred_element_type=F32)
            dst = (pt * (dpt - delta_ref[0, pl.ds(i, 1), :])).astype(BF16)
            dk = dk + jnp.dot(dst, qv, preferred_element_type=F32)
            dq_acc[rows, :] += lax.dot_general(dst, kc, _TN, preferred_element_type=F32)
            return dk, dv

        carry = block(kj, (jnp.zeros((tq, 2 * LANES), F32), jnp.zeros((tq, V_HEAD), F32)), True)
        dk, dv = _loop_in_pairs(n_blk, lambda i, c: block(i, c, False), carry, start=kj + 1)
        dk = dk * LN2
        dkv_ref[...] = _cat(dk[:, :LANES], dv).astype(dkv_ref.dtype)
        lane = lax.broadcasted_iota(jnp.int32, (tq, LANES), 1)
        mine = (lane // HALF_ROPE) % HEADS_PER_CHIP == head % HEADS_PER_CHIP
        dkr_ref[0] = jnp.where(mine, dk[:, LANES:], 0.0)

        @pl.when(kj == n_blk - 1)
        def _():
            dqn_ref[...] = dq_acc[:, :LANES] * SM_SCALE

        @pl.when((kj == n_blk - 1) & (head % HEADS_PER_CHIP == 0))
        def _():
            dqr_ref[...] = dq_acc[:, LANES:] * SM_SCALE

        @pl.when((kj == n_blk - 1) & (head % HEADS_PER_CHIP > 0))
        def _():
            dqr_ref[...] += dq_acc[:, LANES:] * SM_SCALE

    return _pcall(
        body, name="attn_bwd", grid=(N_HEADS, n_blk),
        in_specs=_q_specs(seq, lambda h, j: (0, h)) + _kv_specs(tq, lambda h, j: (j, h))
        + [pl.BlockSpec((seq, V_HEAD), lambda h, j: (0, h)),
           pl.BlockSpec((1, n_blk, tq), lambda h, j: (h, 0, 0)),
           pl.BlockSpec((1, n_blk, tq), lambda h, j: (h, 0, 0))],
        out_specs=[pl.BlockSpec((seq, LANES), lambda h, j: (0, h)),
                   pl.BlockSpec((seq, LANES), lambda h, j: (0, h // HEADS_PER_CHIP)),
                   pl.BlockSpec((tq, QK_NOPE + V_HEAD), lambda h, j: (j, h)),
                   pl.BlockSpec((1, tq, LANES), lambda h, j: (h, j, 0))],
        out_shape=[jax.ShapeDtypeStruct((seq, N_HEADS * QK_NOPE), F32),
                   jax.ShapeDtypeStruct((seq, N_CHIPS * LANES), F32),
                   jax.ShapeDtypeStruct((seq, N_HEADS * (QK_NOPE + V_HEAD)), BF16),
                   jax.ShapeDtypeStruct((N_HEADS, seq, LANES), F32)],
        scratch_shapes=[pltpu.VMEM((seq, 2 * LANES), F32)],
        compiler_params=_params(("arbitrary", "arbitrary")),
    )(q, q, kv, kr, kv, do, lse_row, delta_row)


HEADS_PER_CHIP = N_HEADS // N_CHIPS
Q_CHIP = HEADS_PER_CHIP * QK_DIM
Q_CHIP_NOPE = HEADS_PER_CHIP * QK_NOPE


def _perm_q_cols(w):
    t = w.reshape(w.shape[0], HEADS_PER_CHIP, QK_DIM)
    return jnp.concatenate([t[:, :, :QK_NOPE].reshape(w.shape[0], -1),
                            t[:, :, QK_NOPE:QK_NOPE + HALF_ROPE].reshape(w.shape[0], -1),
                            t[:, :, QK_NOPE + HALF_ROPE:].reshape(w.shape[0], -1)], axis=1)


def _unperm_q_cols(w):
    r = w.shape[0]
    nope = w[:, :Q_CHIP_NOPE].reshape(r, HEADS_PER_CHIP, QK_NOPE)
    r1 = w[:, Q_CHIP_NOPE:Q_CHIP_NOPE + QK_ROPE].reshape(r, HEADS_PER_CHIP, HALF_ROPE)
    r2 = w[:, Q_CHIP_NOPE + QK_ROPE:].reshape(r, HEADS_PER_CHIP, HALF_ROPE)
    return jnp.concatenate([nope, r1, r2], axis=2).reshape(r, Q_CHIP)


def _pad_kva_cols(w):
    z = jnp.zeros((w.shape[0], HALF_ROPE), w.dtype)
    return jnp.concatenate([w[:, :KV_LORA], w[:, KV_LORA:KV_LORA + HALF_ROPE], z, w[:, KV_LORA + HALF_ROPE:], z], axis=1)


def _unpad_kva_cols(w):
    return jnp.concatenate([w[:, :KV_LORA], w[:, KV_LORA:KV_LORA + HALF_ROPE],
                            w[:, KV_LORA + QK_ROPE:KV_LORA + QK_ROPE + HALF_ROPE]], axis=1)


def _rope_tile(t, cs, sn):
    return t * cs + pltpu.roll(t, LANES // 2, 1) * sn


def _rope_tile_bwd(d, cs, sn):
    return d * cs + pltpu.roll(d * sn, LANES // 2, 1)


def _b_cols(tk, tn):
    return pl.BlockSpec((None, tk, tn), lambda i, j, k: (j, k, 0))


def _b_cols_t(tk, tn):
    return pl.BlockSpec((None, tn, tk), lambda i, j, k: (k, j, 0))


def _out_cols(shape):
    return shape, lambda tm, tn: pl.BlockSpec((None, tm, tn), lambda i, j, k: (j, i, 0))


def glu_proj(y, w_glu, tm=1024):
    seq, k_dim = y.shape
    tn = w_glu.shape[2]
    tm = min(tm, seq)
    half = N_CHIPS // 2

    def body(y_ref, wv_ref, wg_ref, val_ref, gate_ref, z_ref):
        yv = y_ref[...]
        v = jnp.dot(yv, wv_ref[...], preferred_element_type=F32)
        gt = jnp.dot(yv, wg_ref[...], preferred_element_type=F32)
        val_ref[...] = v
        gate_ref[...] = gt
        z_ref[...] = (v * _sigmoid(gt)).astype(z_ref.dtype)

    tile = pl.BlockSpec((tm, tn), lambda i, j: (i, j))
    return _pcall(
        body, name="glu_proj", grid=(seq // tm, half),
        in_specs=[pl.BlockSpec((tm, k_dim), lambda i, j: (i, 0)),
                  pl.BlockSpec((None, k_dim, tn), lambda i, j: (j, 0, 0)),
                  pl.BlockSpec((None, k_dim, tn), lambda i, j: (j + half, 0, 0))],
        out_specs=[tile, tile, tile],
        out_shape=[jax.ShapeDtypeStruct((seq, half * tn), F32), jax.ShapeDtypeStruct((seq, half * tn), F32),
                   jax.ShapeDtypeStruct((seq, half * tn), BF16)],
        compiler_params=_params(("parallel", "parallel")),
    )(y, w_glu, w_glu)


def _halves(a):
    return a.reshape(N_CHIPS, 2, a.shape[1] // 2, a.shape[2])


def device_step(x, positions, target, w, comm=None):
    seq = x.shape[0]
    w = dict(w)

    def gathered(names, outs):
        for n, a in zip(names, outs):
            if isinstance(n, tuple):
                w[n[0]] = [a.reshape(v.shape) if l == n[1] else v for l, v in enumerate(w[n[0]])]
            else:
                w[n] = a.reshape(w[n].shape)

    def ride_for(names):
        if comm is None:
            return None
        return GatherRide([_halves(w[n[0]][n[1]] if isinstance(n, tuple) else w[n]) for n in names])

    first_ride = ("ssm_w_glu", "ssm_w_out", ("w_ff1", 0), ("w_ff2", 0))
    mla_ride = ("kv_w_a", "kv_w_b", "q_w_a", "q_w_b", "attn_w_o")
    second_ride = (("w_ff1", 1), ("w_ff2", 1))

    inv_freq = ROPE_THETA ** (-jnp.arange(HALF_ROPE, dtype=F32) / HALF_ROPE)
    ang = positions.astype(F32)[:, None] * jnp.tile(inv_freq, LANES // HALF_ROPE)
    cos, sin = jnp.cos(ang), jnp.sin(ang)
    quarter = jnp.arange(LANES) // HALF_ROPE
    sign = jnp.where(quarter < 2, -1.0, 1.0).astype(F32)
    own = (quarter % 2 == 0).astype(F32)
    cos_q, sin_q = cos, sin * sign
    cos_k, sin_k = cos * own, sin * (sign * own)
    ff_tile = D_FF // N_CHIPS
    pack_shape = (N_CHIPS, EARLY_ROWS, PACK_W)

    lr = w["ssm_lam_re"].reshape(N_STATES, 1)
    li = w["ssm_lam_im"].reshape(N_STATES, 1)
    ldt = jnp.repeat(w["ssm_log_dt"].reshape(N_GROUPS), SSM_STATE).reshape(N_STATES, 1)
    b_re = w["ssm_b_re"].reshape(N_STATES, SSM_GROUP)
    b_im = w["ssm_b_im"].reshape(N_STATES, SSM_GROUP)
    a_re, a_im, bb_re, bb_im = s5_prep(lr, li, ldt, b_re, b_im)
    a_re, a_im = a_re.reshape(1, N_STATES), a_im.reshape(1, N_STATES)
    bbd_re = _blockdiag_in(bb_re.reshape(N_GROUPS, SSM_STATE, SSM_GROUP)).astype(BF16)
    bbd_im = _blockdiag_in(bb_im.reshape(N_GROUPS, SSM_STATE, SSM_GROUP)).astype(BF16)
    cbd_re = _blockdiag_out(w["ssm_c_re"].reshape(N_GROUPS, SSM_GROUP, SSM_STATE)).astype(BF16)
    cbd_imn = _blockdiag_out(-w["ssm_c_im"].reshape(N_GROUPS, SSM_GROUP, SSM_STATE)).astype(BF16)
    dskip = w["ssm_d"].reshape(1, D_MODEL)
    (ypre, yg, h_re, h_im), landed = s5_fwd(x, bbd_re, bbd_im, cbd_re, cbd_imn, a_re, a_im, dskip, ride_for(first_ride))
    gathered(first_ride, landed)
    w_glu = w["ssm_w_glu"]
    glu_tile = w_glu.shape[2]
    val, gate, z = glu_proj(yg, w_glu)
    w_out = w["ssm_w_out"].reshape(D_MODEL, D_MODEL)
    ln = lambda name, l: w[name][l].reshape(1, D_MODEL)

    def then_ln(h, names, layer):
        def epi(r, hv, gl, bl):
            y = _layer_norm(hv, r, gl, bl)
            return r, y, y
        return dict(epi=epi, extras=(h, ln(names[0], layer), ln(names[1], layer)), out_dtypes=(F32, F32, BF16))

    mix0, h1, h1b = mm(z, w_out, name="ssm_out", **then_ln(x, ("ln_mix_g", "ln_mix_b"), 0))

    def mlp_fwd(h, hb, layer, riding=None, with_ln=True):
        pre = mm(hb, w["w_ff1"][layer], n_dim=D_FF, tiles=(None, ff_tile, None), b_view=_b_cols, name=f"ff1_{layer}",
                 out_dtypes=(BF16,), ride=ride_for(riding) if riding else None)
        if riding and comm is not None:
            pre, landed = pre
            gathered(riding, landed)
        post = then_ln(h, ("ln_ffn_g", "ln_ffn_b"), layer) if with_ln else {}
        return pre, mm(pre, w["w_ff2"][layer].reshape(D_FF, D_MODEL), pro_a=_relu2, name=f"ff2_{layer}", **post)

    f1pre, (f1, h2, h2b) = mlp_fwd(h1, h1b, 0, mla_ride)

    kv_w_a = w["kv_w_a"].reshape(D_MODEL, KVA_PAD)
    kv_w_b = w["kv_w_b"]
    q_w_a = w["q_w_a"].reshape(D_MODEL, Q_LORA)
    q_w_b = w["q_w_b"]
    w_o = w["attn_w_o"].reshape(D_MODEL, D_MODEL)
    kvb_tile = kv_w_b.shape[2]
    kvn_g = w["kv_norm_g"].reshape(1, KV_LORA)
    qn_g = w["q_norm_g"].reshape(1, Q_LORA)
    def kv_post(kva, g, cs, sn):
        tile = _rope_tile(kva[:, KV_LORA:], cs, sn)
        return kva, _rms(kva[:, :KV_LORA], g), _cat(tile, pltpu.roll(tile, HALF_ROPE, 1))
    kva, ckv, krope = mm(h2b, kv_w_a, epi=kv_post, extras=(kvn_g, cos_k, sin_k),
                         out_dtypes=(F32, (KV_LORA, BF16), (2 * LANES, BF16)), name="kv_a")
    kvb = mm(ckv, kv_w_b, n_dim=N_CHIPS * kvb_tile, tiles=(None, kvb_tile, KV_LORA), b_view=_b_cols, name="kv_b",
             out_dtypes=(BF16,))
    cq_raw, cq = mm(h2b, q_w_a, epi=lambda r, gq: (r, _rms(r, gq)), extras=(qn_g,), out_dtypes=(F32, BF16), name="q_a")

    def rope_and_scale(r, cs, sn):
        return (_cat(r[:, :Q_CHIP_NOPE], _rope_tile(r[:, Q_CHIP_NOPE:], cs, sn)) * Q_PRESCALE,)
    qro = mm(cq, q_w_b, n_dim=N_CHIPS * Q_CHIP, tiles=(None, Q_CHIP, Q_LORA), b_view=_b_cols, epi=rope_and_scale,
             extras=(cos_q, sin_q), out_dtypes=(BF16,), name="q_b")
    (o, lse), landed = attn_fwd(qro, kvb, krope, ride_for(second_ride))
    gathered(second_ride, landed)
    mix1, h3, h3b = mm(o, w_o, name="attn_out", **then_ln(h2, ("ln_mix_g", "ln_mix_b"), 1))
    f2pre, f2 = mlp_fwd(h3, h3b, 1, with_ln=False)
    def last_ln_loss_and_back(h, mix, gl, bl, t):
        e = _layer_norm(h, mix, gl, bl) - t
        dr, dg, db = _layer_norm_bwd(h, mix, gl, e * (1.0 / D_MODEL))
        return (dr, dr), (jnp.broadcast_to(jnp.sum(e * e), (1, LANES)), dg, db)
    dr4, dr4b, loss_acc, dg_f1, db_f1 = rowwise(
        last_ln_loss_and_back, (h3, f2, ln("ln_ffn_g", 1), ln("ln_ffn_b", 1), target),
        ((D_MODEL, F32), (D_MODEL, BF16)), accs=(LANES, D_MODEL, D_MODEL), name="ln_ffn_1_loss", tm=512)
    loss = loss_acc[0, 0] * (0.5 / D_MODEL)

    g = {}

    def into_rows(off, rows_per_chip, shape=pack_shape):
        def view(tm, tn):
            if tm == N_CHIPS * rows_per_chip:
                return pl.BlockSpec((N_CHIPS, rows_per_chip, tn), lambda i, j, k: (0, off // rows_per_chip, 0))
            nb = rows_per_chip // tm
            return pl.BlockSpec((None, tm, tn), lambda i, j, k: (i // nb, off // tm + i % nb, 0))
        return shape, view

    def into_cols(off):
        return pack_shape, lambda tm, tn: pl.BlockSpec((None, tm, tn), lambda i, j, k: (j, off // tm + i, 0))

    def mlp_bwd(pack, dr, drb, hb, pre, layer, swap=False):
        w2_rows = (EARLY_OFF["w_ff2"] + layer * ff_tile, ff_tile)
        w1_rows = (EARLY_OFF["w_ff1"] + layer * D_MODEL, D_MODEL)
        ready = [(w1_rows[0] + w1_rows[1], w2_rows[0] - w1_rows[0] - w1_rows[1]), (w2_rows[0] + w2_rows[1], EARLY_ROWS - w2_rows[0] - w2_rows[1])]
        dpre = mm(drb, w["w_ff2"][layer].reshape(D_FF, D_MODEL), tb=True, epi=lambda r, p: (r * 2.0 * jnp.maximum(p, 0.0),),
                  extras=(pre,), out_dtypes=(BF16,), tiles=(None, ff_tile, None), name=f"ff2_dx_{layer}",
                  ride=SwapRide(pack, ready) if swap else None)
        if swap:
            dpre, (theirs,) = dpre
        pack = mm(pre, drb, ta=True, pro_a=_relu2, name=f"ff2_dw_{layer}", tiles=(ff_tile, PACK_W, None), into=pack,
                  out_view=into_rows(w2_rows[0], ff_tile))
        pack = mm(hb, dpre, ta=True, name=f"ff1_dw_{layer}", tiles=(None, PACK_W, None), into=pack,
                  out_view=into_cols(w1_rows[0]))
        dh = mm(dpre, w["w_ff1"][layer], tb=True, epi=lambda r, d: (r + DN_ALPHA * d,), extras=(dr,), n_dim=D_MODEL,
                tiles=(None, D_MODEL, ff_tile), b_view=_b_cols_t, name=f"ff1_dx_{layer}",
                ride=SwapRide(pack, [w1_rows, w2_rows], into=theirs) if swap else None)
        return (pack, *dh) if swap else (pack, dh)

    pack, dh3 = mlp_bwd(None, dr4, dr4b, h3b, f2pre, 1)
    dr3, dr3b, dg_m1, db_m1 = ln_bwd(h2, mix1, ln("ln_mix_g", 1), dh3, "ln_mix_bwd_1")
    shard_rows = D_MODEL // N_CHIPS
    pack = mm(o, dr3b, ta=True, name="attn_out_dw", tiles=(D_MODEL, PACK_W, None), into=pack,
              out_view=into_rows(EARLY_OFF["attn_w_o"], shard_rows))
    def head_dots(do, o):
        return do, jnp.concatenate([jnp.sum(do[:, V_HEAD * h:V_HEAD * (h + 1)] * o[:, V_HEAD * h:V_HEAD * (h + 1)], axis=1,
                                            keepdims=True) for h in range(N_HEADS)], axis=1)
    do, delta = mm(dr3b, w_o, tb=True, epi=head_dots, extras=(o,), out_dtypes=(F32, (N_HEADS, F32)), name="attn_out_dx")
    tb = min(ATT_TK, seq)
    lse_row = lse.reshape(N_HEADS, seq // tb, tb)
    delta_row = delta.T.reshape(N_HEADS, seq // tb, tb)
    dqn, dqr, dkvb, dkr = attn_bwd(qro, kvb, krope, do, lse_row, delta_row)

    def q_rope_bwd(dn, dr, cs, sn):
        parts = []
        for k in range(N_CHIPS):
            parts.append(dn[:, Q_CHIP_NOPE * k:Q_CHIP_NOPE * (k + 1)])
            parts.append(_rope_tile_bwd(dr[:, LANES * k:LANES * (k + 1)], cs, sn))
        return (jnp.concatenate(parts, axis=1),), ()
    (dqlin,) = rowwise(q_rope_bwd, (dqn, dqr, cos_q, sin_q), ((N_CHIPS * Q_CHIP, BF16),), name="q_rope_bwd", tm=512)
    g["q_w_b"] = mm(cq, dqlin, ta=True, name="q_b_dw", tiles=(Q_LORA, Q_CHIP, None), out_view=_out_cols(q_w_b.shape))
    dcq_raw, dqn_g = mm(dqlin, q_w_b, tb=True, n_dim=Q_LORA, tiles=(None, Q_LORA, Q_CHIP), b_view=_b_cols_t,
                        epi=lambda d, c, gq: _rms_bwd(c, gq, d), extras=(cq_raw, qn_g), out_dtypes=(BF16,),
                        accs=(Q_LORA,), name="q_b_dx")
    g["q_w_a"] = mm(h2b, dcq_raw, ta=True, name="q_a_dw")
    g["kv_w_b"] = mm(ckv, dkvb, ta=True, name="kv_b_dw", tiles=(KV_LORA, kvb_tile, None), out_view=_out_cols(kv_w_b.shape))

    def kv_post_bwd(dc, kva, gk, dk, cs, sn):
        dx, dgk = _rms_bwd(kva[:, :KV_LORA], gk, dc)
        dk = jnp.sum(dk, axis=0)
        dk = dk + pltpu.roll(dk, LANES - HALF_ROPE, 1)
        return jnp.concatenate([dx, _rope_tile_bwd(dk, cs, sn)], axis=1), dgk
    dkva, dkvn_g = mm(dkvb, kv_w_b, tb=True, n_dim=KV_LORA, tiles=(None, KV_LORA, kvb_tile), b_view=_b_cols_t,
                      epi=kv_post_bwd, extras=(kva, kvn_g, dkr, cos_k, sin_k), out_dtypes=((KVA_PAD, BF16),),
                      accs=(KV_LORA,), name="kv_b_dx")
    g["kv_w_a"] = mm(h2b, dkva, ta=True, name="kv_a_dw")

    def ln_ffn_bwd(r, dc, wq, d, h, f, gl):
        via_q = lax.dot_general(dc, wq, (((1,), (1,)), ((), ())), preferred_element_type=F32)
        dr, dg, db = _layer_norm_bwd(h, f, gl, r + (via_q + DN_ALPHA * d))
        return dr, dr, dg, db
    dr2, dr2b, dg_f0, db_f0 = mm(dkva, kv_w_a, tb=True, epi=ln_ffn_bwd,
                                 extras=(dcq_raw, q_w_a.astype(BF16), dr3, h1, f1, ln("ln_ffn_g", 0)),
                                 out_dtypes=(F32, BF16), accs=(D_MODEL, D_MODEL), name="qkv_a_dx")
    pack = put_rows(pack, packed_shards(g, MISC_EARLY, EARLY_ROWS - MISC_EARLY_OFF), MISC_EARLY_OFF)
    if comm is None:
        pack, dh1 = mlp_bwd(pack, dr2, dr2b, h1b, f1pre, 0)
    else:
        pack, dh1, (theirs,) = mlp_bwd(pack, dr2, dr2b, h1b, f1pre, 0, swap=True)
        early_sums = add_halves(pack, theirs, comm[1])
    dr1, dr1b, dg_m0, db_m0 = ln_bwd(x, mix0, ln("ln_mix_g", 0), dh1, "ln_mix_bwd_0")
    mid = mm(z, dr1b, ta=True, name="ssm_out_dw", tiles=(D_MODEL, PACK_W, None),
             out_view=into_rows(MID_OFF["ssm_w_out"], shard_rows, (N_CHIPS, MID_ROWS, PACK_W)))
    def glu_bwd(dz, vl, gt):
        sg = _sigmoid(gt)
        return (jnp.concatenate([dz * sg, dz * vl * sg * (1.0 - sg)], axis=1),)
    dvg = mm(dr1b, w_out, tb=True, epi=glu_bwd, extras=(val, gate), out_dtypes=((2 * D_MODEL, BF16),), name="ssm_out_dx")
    g["ssm_w_glu"] = mm(yg, dvg, ta=True, name="glu_proj_dw", tiles=(None, glu_tile, None), out_view=_out_cols(w_glu.shape))
    mid = put_rows(mid, packed_shards(g, MISC_MID, MID_ROWS - MISC_MID_OFF), MISC_MID_OFF)
    dypre = mm(dvg, w_glu, tb=True, epi=lambda r, y: (r * _gelu_grad(y),), extras=(ypre,), n_dim=D_MODEL,
               tiles=(None, D_MODEL, glu_tile), b_view=_b_cols_t, name="glu_proj_dx",
               ride=Together([SwapRide(mid), SendRide([(early_sums, (0, EARLY_HEAD), None)])]) if comm is not None else None)
    sends = None
    if comm is not None:
        dypre, (theirs, early_got) = dypre
        sends = SendRide([(early_sums, (EARLY_HEAD, EARLY_ROWS - EARLY_HEAD), early_got), add_halves(mid, theirs, comm[1])])
    (dx, dbbd_re, dbbd_im, dcbd_re, dcbd_imn, dar, dai, dd), got = s5_bwd(
        dypre, x, dr1, h_re, h_im, bbd_re, bbd_im, cbd_re, cbd_imn, a_re, a_im, dskip, sends)
    dbb_re = _blockdiag_in_t(dbbd_re).reshape(N_STATES, SSM_GROUP)
    dbb_im = _blockdiag_in_t(dbbd_im).reshape(N_STATES, SSM_GROUP)
    dlr, dli, dldt, db_re, db_im = s5_prep_bwd(lr, li, ldt, b_re, b_im, dar.reshape(N_STATES, 1),
                                               dai.reshape(N_STATES, 1), dbb_re, dbb_im)
    g["ssm_lam_re"] = dlr.reshape(1, N_GROUPS, SSM_STATE)
    g["ssm_lam_im"] = dli.reshape(1, N_GROUPS, SSM_STATE)
    g["ssm_log_dt"] = group_sum(dldt).reshape(1, N_GROUPS)
    g["ssm_b_re"] = db_re.reshape(1, N_GROUPS, SSM_STATE, SSM_GROUP)
    g["ssm_b_im"] = db_im.reshape(1, N_GROUPS, SSM_STATE, SSM_GROUP)
    g["ssm_c_re"] = _blockdiag_out_t(dcbd_re).reshape(1, N_GROUPS, SSM_GROUP, SSM_STATE)
    g["ssm_c_im"] = -_blockdiag_out_t(dcbd_imn).reshape(1, N_GROUPS, SSM_GROUP, SSM_STATE)
    g["ssm_d"] = dd
    g["ln_mix_g"] = jnp.concatenate([dg_m0, dg_m1], 0)
    g["ln_mix_b"] = jnp.concatenate([db_m0, db_m1], 0)
    g["ln_ffn_g"] = jnp.concatenate([dg_f0, dg_f1], 0)
    g["ln_ffn_b"] = jnp.concatenate([db_f0, db_f1], 0)
    g["kv_norm_g"] = dkvn_g.reshape(KV_LORA)
    g["q_norm_g"] = dqn_g
    return loss, dx, pack, mid, g, list(zip(sends.ins, got)) if comm is not None else None


def place(shard, me_idx, dtype, name, layer=None):
    rows, cols = shard.shape[-2:]
    tr = _tile(rows, (512, 256, 128))

    def body(m_ref, x_ref, o_ref):
        o_ref[...] = x_ref[...].astype(o_ref.dtype)

    in_spec = (pl.BlockSpec((tr, cols), lambda i, m: (i, 0)) if layer is None
               else pl.BlockSpec((None, tr, cols), lambda i, m: (layer, i, 0)))
    return _pcall(
        body, name=name,
        grid_spec=pltpu.PrefetchScalarGridSpec(
            num_scalar_prefetch=1, grid=(rows // tr,), in_specs=[in_spec],
            out_specs=pl.BlockSpec((None, tr, cols), lambda i, m: (m[0], i, 0))),
        out_shape=jax.ShapeDtypeStruct((N_CHIPS, rows, cols), dtype),
        compiler_params=_params(("parallel",)),
    )(me_idx, shard)


def place_many(shards, dtypes, me_idx, name):
    def body(m_ref, *refs):
        for x_ref, o_ref in zip(refs[:len(shards)], refs[len(shards):]):
            o_ref[...] = x_ref[...].astype(o_ref.dtype)

    return _pcall(
        body, name=name,
        grid_spec=pltpu.PrefetchScalarGridSpec(
            num_scalar_prefetch=1, grid=(1,),
            in_specs=[pl.BlockSpec(s.shape, lambda i, m: (0, 0)) for s in shards],
            out_specs=[pl.BlockSpec((None,) + s.shape, lambda i, m: (m[0], 0, 0)) for s in shards]),
        out_shape=[jax.ShapeDtypeStruct((N_CHIPS,) + s.shape, d) for s, d in zip(shards, dtypes)],
        compiler_params=_params(("arbitrary",)),
    )(me_idx, *shards)


def put_rows(pack, rows, off):
    _, n, cols = rows.shape

    def body(r_ref, p_ref, o_ref, sem):
        cp = pltpu.make_async_copy(r_ref.at[0], o_ref.at[pl.program_id(0), pl.ds(off, n), :], sem)
        cp.start()
        cp.wait()

    return _pcall(body, name="grad_put_rows", grid=(N_CHIPS,),
                  in_specs=[pl.BlockSpec((1, n, cols), lambda k: (k, 0, 0)), _ANY], out_specs=_ANY,
                  out_shape=jax.ShapeDtypeStruct(pack.shape, pack.dtype), input_output_aliases={1: 0},
                  scratch_shapes=[pltpu.SemaphoreType.DMA],
                  compiler_params=_params(("arbitrary",)))(rows, pack)


def _my_cols(c, mine=True):
    start = (c if mine else 1 - c) * HALF_W
    return pl.ds(pl.multiple_of(start, HALF_W), HALF_W)


def add_halves(gpack, got, c_idx):
    n, rows, _ = gpack.shape
    tr = min(G_BLOCK_ROWS, rows)
    blk = (None, tr, HALF_W)

    def body(c_ref, g_ref, r_ref, o_ref):
        o_ref[...] = (g_ref[...] + r_ref[...]).astype(o_ref.dtype)

    return _pcall(
        body, name="grad_add_halves",
        grid_spec=pltpu.PrefetchScalarGridSpec(
            num_scalar_prefetch=1, grid=(n, rows // tr),
            in_specs=[pl.BlockSpec(blk, lambda k, i, c: (k, i, c[0])), pl.BlockSpec(blk, lambda k, i, c: (k, i, 0))],
            out_specs=pl.BlockSpec(blk, lambda k, i, c: (k, i, 0))),
        out_shape=jax.ShapeDtypeStruct((n, rows, HALF_W), BF16),
        compiler_params=_params(("parallel", "parallel")),
    )(c_idx, gpack, got)


def sum_owner(part, got, idx, total_rows, row_off=0, into=None):
    _, rows, _ = part.shape
    tr = math.gcd(math.gcd(rows, row_off), G_BLOCK_ROWS)
    n_into = 0 if into is None else 1

    def body(m_ref, p_ref, g_ref, *rest):
        up = lambda v: v.astype(F32)
        rest[-1][...] = ((up(p_ref[...]) + up(g_ref[0])) + up(g_ref[1])) + up(g_ref[2])

    return _pcall(
        body, name="grad_sum_owner",
        grid_spec=pltpu.PrefetchScalarGridSpec(
            num_scalar_prefetch=1, grid=(rows // tr,),
            in_specs=[pl.BlockSpec((None, tr, HALF_W), lambda i, m: (m[0], i, 0)),
                      pl.BlockSpec((3, tr, HALF_W), lambda i, m: (0, i, 0))] + [_ANY] * n_into,
            out_specs=pl.BlockSpec((tr, HALF_W), lambda i, m: (row_off // tr + i, m[1]))),
        out_shape=jax.ShapeDtypeStruct((total_rows, PACK_W), F32),
        input_output_aliases={3: 0} if n_into else {},
        compiler_params=_params(("parallel",)),
    )(idx, part, got, *([into] if n_into else []))


def join_halves(red):
    def body(in_ref, out_ref, send_sem, recv_sem):
        x, y, c, _ = _place()
        sibling = (x, y, 1 - c)
        mine = out_ref.at[:, _my_cols(c)]
        cp = pltpu.make_async_remote_copy(src_ref=mine, dst_ref=mine, send_sem=send_sem, recv_sem=recv_sem,
                                          device_id=sibling, device_id_type=MESH)
        cp.start()
        cp.wait_send()
        other = out_ref.at[:, _my_cols(c, mine=False)]
        pltpu.make_async_remote_copy(src_ref=other, dst_ref=other, send_sem=send_sem, recv_sem=recv_sem,
                                     device_id=sibling, device_id_type=MESH).wait_recv()

    return _pcall(body, name="grad_join_halves", in_specs=[_ANY], out_specs=_ANY,
                  out_shape=jax.ShapeDtypeStruct(red.shape, red.dtype), input_output_aliases={0: 0},
                  scratch_shapes=[pltpu.SemaphoreType.DMA, pltpu.SemaphoreType.DMA])(red)


def adamw(gsrc, g_off, wt, m, v, name):
    n, cols = wt.shape
    tr = math.gcd(math.gcd(g_off, n), 256) if g_off else math.gcd(n, 256)
    off_blk = g_off // tr
    c1 = 1.0 / (1.0 - ADAM_B1 ** ADAM_STEP)
    c2 = 1.0 / (1.0 - ADAM_B2 ** ADAM_STEP)

    def body(g_ref, w_ref, m_ref, v_ref, go_ref, d_ref, mo_ref, vo_ref):
        gv = g_ref[...]
        mn = ADAM_B1 * m_ref[...] + (1.0 - ADAM_B1) * gv
        vn = ADAM_B2 * v_ref[...] + (1.0 - ADAM_B2) * gv * gv
        go_ref[...] = gv
        mo_ref[...] = mn
        vo_ref[...] = vn
        d_ref[...] = -ADAM_LR * ((mn * c1) / (jnp.sqrt(vn * c2) + ADAM_EPS) + ADAM_WD * w_ref[...])

    blk = pl.BlockSpec((tr, cols), lambda i: (i, 0))
    return _pcall(body, name=name, grid=(n // tr,),
                  in_specs=[pl.BlockSpec((tr, cols), lambda i: (off_blk + i, 0)), blk, blk, blk],
                  out_specs=[blk] * 4, out_shape=[jax.ShapeDtypeStruct((n, cols), F32)] * 4,
                  compiler_params=_params(("parallel",)))(gsrc, wt, m, v)


def _rows8(a):
    return -(-a.size // (8 * PACK_W)) * 8


def _as_rows(a, rows=None):
    flat = a.reshape(-1)
    n = _rows8(a) if rows is None else rows
    return jnp.pad(flat, (0, n * PACK_W - flat.shape[0])).reshape(n, PACK_W)


def local_shards_2d(wl):
    return {"w_ff1": [wl["w_ff1"][0], wl["w_ff1"][1]], "w_ff2": [wl["w_ff2"][0], wl["w_ff2"][1]],
            "ssm_w_glu": wl["ssm_w_glu"], "ssm_w_out": wl["ssm_w_out"], "kv_w_a": _pad_kva_cols(wl["kv_w_a"]),
            "kv_w_b": wl["kv_w_b"], "q_w_a": wl["q_w_a"], "q_w_b": _perm_q_cols(wl["q_w_b"]),
            "attn_w_o": wl["attn_w_o"], "ssm_d": wl["ssm_d"].reshape(2, -1)}


def misc_grad_shard(name, g, k):
    if name == "ssm_d":
        w = D_MODEL // N_CHIPS
        return g[:, w * k:w * (k + 1)]
    if name in ("ssm_w_glu", "kv_w_b"):
        return g[k]
    if name == "q_w_b":
        return _unperm_q_cols(g[k])
    rows = D_MODEL // N_CHIPS
    shard = g[rows * k:rows * (k + 1)]
    return _unpad_kva_cols(shard) if name == "kv_w_a" else shard


def packed_shards(g, names, rows, tail=None):
    blocks = []
    for k in range(N_CHIPS):
        parts = [_as_rows(misc_grad_shard(n, g[n], k), MISC_SHARD_ROWS[n]) for n in names]
        if tail is not None:
            parts.append(tail[k * (tail.shape[0] // N_CHIPS):(k + 1) * (tail.shape[0] // N_CHIPS)])
        blk = jnp.concatenate(parts, axis=0)
        blocks.append(jnp.pad(blk, ((0, rows - blk.shape[0]), (0, 0))))
    return jnp.stack(blocks)


def kernel(x, positions, ln_mix_g, ln_mix_b, ln_ffn_g, ln_ffn_b, w_ff1, w_ff2, ssm_lam_re, ssm_lam_im, ssm_log_dt, ssm_b_re, ssm_b_im, ssm_c_re, ssm_c_im, ssm_d, ssm_w_glu, ssm_w_out, kv_w_a, kv_norm_g, kv_w_b, q_w_a, q_norm_g, q_w_b, attn_w_o, loss_target, m_ln_mix_g, m_ln_mix_b, m_ln_ffn_g, m_ln_ffn_b, m_w_ff1, m_w_ff2, m_ssm_lam_re, m_ssm_lam_im, m_ssm_log_dt, m_ssm_b_re, m_ssm_b_im, m_ssm_c_re, m_ssm_c_im, m_ssm_d, m_ssm_w_glu, m_ssm_w_out, m_kv_w_a, m_kv_norm_g, m_kv_w_b, m_q_w_a, m_q_norm_g, m_q_w_b, m_attn_w_o, v_ln_mix_g, v_ln_mix_b, v_ln_ffn_g, v_ln_ffn_b, v_w_ff1, v_w_ff2, v_ssm_lam_re, v_ssm_lam_im, v_ssm_log_dt, v_ssm_b_re, v_ssm_b_im, v_ssm_c_re, v_ssm_c_im, v_ssm_d, v_ssm_w_glu, v_ssm_w_out, v_kv_w_a, v_kv_norm_g, v_kv_w_b, v_q_w_a, v_q_norm_g, v_q_w_b, v_attn_w_o):
    env = dict(locals())
    wl = {n: env[n] for n in WEIGHTS}
    ml = {n: env["m_" + n] for n in WEIGHTS}
    vl = {n: env["v_" + n] for n in WEIGHTS}
    for n in ("ssm_w_glu", "ssm_w_out", "q_w_a", "q_w_b", "attn_w_o"):
        wl[n], ml[n], vl[n] = wl[n][0], ml[n][0], vl[n][0]

    c_idx = lax.axis_index("c").astype(jnp.int32).reshape(1)
    me_idx = (2 * lax.axis_index("x") + lax.axis_index("y")).astype(jnp.int32).reshape(1)

    local = local_shards_2d(wl)
    stacked = {n: [place(wl[n], me_idx, BF16, f"place_{n}_{l}", layer=l) for l in range(DEPTH)] for n in ("w_ff1", "w_ff2")}
    others = [n for n in SHARDED if n not in stacked]
    stacked.update(zip(others, place_many([local[n] for n in others], [F32 if n == "ssm_d" else BF16 for n in others],
                                          me_idx, "place_others")))
    stacked["ssm_d"] = ride_alone(GatherRide([_halves(stacked["ssm_d"])]), "ssm_d_all_gather")[0].reshape(1, D_MODEL)
    for n in REPLICATED:
        stacked[n] = wl[n]

    loss_part, dx, early, mid, g, sent = device_step(x[0], positions[0], loss_target[0], stacked, comm=(me_idx, c_idx))
    loss = lax.psum(loss_part, ("x", "y", "c"))

    small = jnp.concatenate([_as_rows(g[n]) for n in REPLICATED], axis=0)
    small = jnp.pad(small, ((0, SMALL_ROWS - small.shape[0]), (0, 0)))
    late = packed_shards(g, MISC_LATE, LATE_ROWS, tail=small)
    late_sums = add_halves(late, ride_alone(SwapRide(late), "grad_swap_halves")[0], c_idx)
    sent.append((late_sums, ride_alone(SendRide([late_sums]), "grad_send_to_owners")[0]))
    where = jnp.concatenate([me_idx, c_idx])
    starts = (0, EARLY_ROWS, EARLY_ROWS + MID_ROWS)
    total_rows = EARLY_ROWS + MID_ROWS + LATE_ROWS
    reduced = None
    for (sums, got), off in zip(sent, starts):
        reduced = sum_owner(sums, got, where, total_rows, row_off=off, into=reduced)
    reduced = join_halves(reduced)
    quarter = reduced[starts[2] + SMALL_OFF:starts[2] + SMALL_OFF + SMALL_Q_ROWS]
    small_tot = ride_alone(GatherRide([_halves(place(quarter, me_idx, F32, "place_small_grads"))]),
                           "small_grad_all_gather")[0].reshape(SMALL_ROWS, PACK_W)

    out_g, out_d, out_m, out_v = {}, {}, {}, {}
    direct = {**EARLY_OFF, **{n: starts[1] + o for n, o in MID_OFF.items()}}
    for n, off in direct.items():
        res = adamw(reduced, off, wl[n].reshape(-1, PACK_W), ml[n].reshape(-1, PACK_W), vl[n].reshape(-1, PACK_W),
                    "adamw_" + n)
        out_g[n], out_d[n], out_m[n], out_v[n] = [a.reshape(env[n].shape) for a in res]
    for names, off in ((MISC_EARLY, MISC_EARLY_OFF), (MISC_MID, starts[1] + MISC_MID_OFF), (MISC_LATE, starts[2])):
        pack3 = lambda d: jnp.concatenate([_as_rows(d[n], MISC_SHARD_ROWS[n]) for n in names], axis=0)
        res = adamw(reduced, off, pack3(wl), pack3(ml), pack3(vl), "adamw_packed_" + names[0])
        r0 = 0
        for n in names:
            cnt = math.prod(env[n].shape)
            out_g[n], out_d[n], out_m[n], out_v[n] = [
                a[r0:r0 + MISC_SHARD_ROWS[n]].reshape(-1)[:cnt].reshape(env[n].shape) for a in res]
            r0 += MISC_SHARD_ROWS[n]
    ws = jnp.concatenate([_as_rows(wl[n]) for n in REPLICATED], axis=0)
    ms = jnp.concatenate([_as_rows(ml[n]) for n in REPLICATED], axis=0)
    vs = jnp.concatenate([_as_rows(vl[n]) for n in REPLICATED], axis=0)
    pad = ((0, SMALL_ROWS - ws.shape[0]), (0, 0))
    res = adamw(small_tot, 0, jnp.pad(ws, pad), jnp.pad(ms, pad), jnp.pad(vs, pad), "adamw_replicated")
    row = 0
    for n in REPLICATED:
        cnt = math.prod(env[n].shape)
        nrows = _rows8(env[n])
        out_g[n], out_d[n], out_m[n], out_v[n] = [a[row:row + nrows].reshape(-1)[:cnt].reshape(env[n].shape) for a in res]
        row += nrows

    return (loss, dx[None], *[out_g[n] for n in WEIGHTS], *[out_d[n] for n in WEIGHTS],
            *[out_m[n] for n in WEIGHTS], *[out_v[n] for n in WEIGHTS])
```

```python
import functools
import math

import jax
import jax.numpy as jnp
from jax import lax
from jax.experimental import pallas as pl
from jax.experimental.pallas import tpu as pltpu

F32 = jnp.float32
BF16 = jnp.bfloat16
MESH = pl.DeviceIdType.MESH

D_MODEL = 1024
DEPTH = 2
SSM_GROUP = 16
N_GROUPS = D_MODEL // SSM_GROUP
SSM_STATE = 64
N_STATES = N_GROUPS * SSM_STATE
N_HEADS = 8
QK_NOPE = 128
QK_ROPE = 64
HALF_ROPE = QK_ROPE // 2
V_HEAD = 128
QK_DIM = QK_NOPE + QK_ROPE
Q_LORA = 384
KV_LORA = 256
ROPE_THETA = 10000.0
SM_SCALE = QK_DIM ** -0.5
NEG_INF = -1e30
D_FF = 4 * D_MODEL
DN_ALPHA = (2 * DEPTH) ** 0.25
LN_EPS = 1e-5
RMS_EPS = 1e-6
ADAM_LR = 0.001
ADAM_B1 = 0.9
ADAM_B2 = 0.999
ADAM_EPS = 1e-08
ADAM_WD = 0.01
ADAM_STEP = 10

N_CHIPS = 4
LANES = 128
VMEM_LIMIT = 56 * 1024 * 1024
MM_VMEM_BUDGET = 40 * 1024 * 1024
PACK_W = 1024
KVA_PAD = 384
HALF_W = PACK_W // 2

SHARDED = ("w_ff1", "w_ff2", "ssm_w_glu", "ssm_w_out", "kv_w_a", "kv_w_b", "q_w_a", "q_w_b", "attn_w_o", "ssm_d")
G_BLOCK_ROWS = 960
EARLY_OFF = {"w_ff1": 0, "w_ff2": 2048, "attn_w_o": 4096}
MISC_EARLY = ("kv_w_b", "kv_w_a", "q_w_a", "q_w_b")
MISC_EARLY_OFF = 4352
EARLY_ROWS = 5 * G_BLOCK_ROWS
EARLY_HEAD = G_BLOCK_ROWS
MID_OFF = {"ssm_w_out": 0}
MISC_MID = ("ssm_w_glu",)
MISC_MID_OFF = 256
MID_ROWS = MISC_MID_OFF + 512
MISC_LATE = ("ssm_d",)
SMALL_Q_ROWS = 96
SMALL_ROWS = N_CHIPS * SMALL_Q_ROWS
SMALL_OFF = 16
LATE_ROWS = 192
MISC_SHARD_ROWS = {"ssm_d": 16, "ssm_w_glu": 512, "kv_w_b": 128, "kv_w_a": 80, "q_w_a": 96, "q_w_b": 144}
REPLICATED = ("ln_mix_g", "ln_mix_b", "ln_ffn_g", "ln_ffn_b", "ssm_lam_re", "ssm_lam_im", "ssm_log_dt",
              "ssm_b_re", "ssm_b_im", "ssm_c_re", "ssm_c_im", "kv_norm_g", "q_norm_g")
WEIGHTS = ("ln_mix_g", "ln_mix_b", "ln_ffn_g", "ln_ffn_b", "w_ff1", "w_ff2", "ssm_lam_re", "ssm_lam_im",
           "ssm_log_dt", "ssm_b_re", "ssm_b_im", "ssm_c_re", "ssm_c_im", "ssm_d", "ssm_w_glu", "ssm_w_out",
           "kv_w_a", "kv_norm_g", "kv_w_b", "q_w_a", "q_norm_g", "q_w_b", "attn_w_o")


def _pcall(body, **kw):
    return pl.pallas_call(body, **kw)


def _params(sem=None):
    return pltpu.CompilerParams(dimension_semantics=sem, vmem_limit_bytes=VMEM_LIMIT)


_ANY = pl.BlockSpec(memory_space=pl.ANY)


def _tile(dim, prefs):
    for p in prefs:
        if dim % p == 0:
            return p
    return dim


def _place():
    x, y, c = lax.axis_index("x"), lax.axis_index("y"), lax.axis_index("c")
    return x, y, c, [(1 - x, y), (x, 1 - y), (1 - x, 1 - y)]


def _remote(k, src, dst, to, send_sems, recv_sems):
    return pltpu.make_async_remote_copy(src_ref=src, dst_ref=dst, send_sem=send_sems.at[k], recv_sem=recv_sems.at[k],
                                        device_id=to, device_id_type=MESH)


class GatherRide:
    def __init__(self, arrs):
        self.ins = list(arrs)
        self.out_shapes = [jax.ShapeDtypeStruct(a.shape, a.dtype) for a in arrs]
        self.aliases = {i: i for i in range(len(arrs))}
        self.n_sems = 6 * len(arrs)

    def start(self, ins, outs, send_sems, recv_sems):
        x, y, c, chips = _place()
        me = 2 * x + y
        for a, o in enumerate(outs):
            for j, (px, py) in enumerate(chips):
                _remote(6 * a + j, o.at[me, c], o.at[me, c], (px, py, c), send_sems, recv_sems).start()

    def pass_on(self, ins, outs, send_sems, recv_sems):
        x, y, c, chips = _place()
        for a, o in enumerate(outs):
            for j, (px, py) in enumerate(chips):
                blk = o.at[2 * px + py, c]
                _remote(6 * a + j, blk, blk, (px, py, c), send_sems, recv_sems).wait_recv()
                _remote(6 * a + 3 + j, blk, blk, (x, y, 1 - c), send_sems, recv_sems).start()

    def finish(self, ins, outs, send_sems, recv_sems, passed_on=False):
        if not passed_on:
            self.pass_on(ins, outs, send_sems, recv_sems)
        x, y, c, chips = _place()
        me = 2 * x + y
        sibling = (x, y, 1 - c)
        for a, o in enumerate(outs):
            for j, (px, py) in enumerate(chips):
                blk = o.at[2 * px + py, 1 - c]
                _remote(6 * a + 3 + j, blk, blk, sibling, send_sems, recv_sems).wait_recv()
                _remote(6 * a + j, o.at[me, c], o.at[me, c], (px, py, c), send_sems, recv_sems).wait_send()
                mine = o.at[2 * px + py, c]
                _remote(6 * a + 3 + j, mine, mine, sibling, send_sems, recv_sems).wait_send()


class SendRide:
    base = 0

    def __init__(self, parts):
        parts = [p if isinstance(p, tuple) else (p, (0, p.shape[1]), None) for p in parts]
        self.rows = [rows for _, rows, _ in parts]
        self.n_parts = len(parts)
        self.ins = [p for p, _, _ in parts] + [into for _, _, into in parts if into is not None]
        self.out_shapes = [jax.ShapeDtypeStruct((3,) + p.shape[1:], p.dtype) for p, _, _ in parts]
        given = [a for a, (_, _, into) in enumerate(parts) if into is not None]
        self.aliases = {self.n_parts + i: a for i, a in enumerate(given)}
        self.n_sems = 3 * self.n_parts

    def _copies(self, ins, outs, send_sems, recv_sems):
        x, y, c, chips = _place()
        return [_remote(self.base + 3 * a + j, ins[a].at[2 * px + py, pl.ds(r0, n)], outs[a].at[j, pl.ds(r0, n)],
                        (px, py, c), send_sems, recv_sems)
                for a, (r0, n) in enumerate(self.rows) for j, (px, py) in enumerate(chips)]

    def start(self, ins, outs, send_sems, recv_sems):
        for cp in self._copies(ins, outs, send_sems, recv_sems):
            cp.start()

    def finish(self, ins, outs, send_sems, recv_sems):
        for cp in self._copies(ins, outs, send_sems, recv_sems):
            cp.wait()


class SwapRide:
    base = 0

    def __init__(self, pack, ranges=None, into=None):
        self.ins = [pack] if into is None else [pack, into]
        self.out_shapes = [jax.ShapeDtypeStruct(pack.shape[:2] + (HALF_W,), pack.dtype)]
        self.aliases = {} if into is None else {1: 0}
        self.ranges = ranges or [(0, pack.shape[1])]
        self.n_sems = len(self.ranges)

    def _copies(self, ins, outs, send_sems, recv_sems):
        x, y, c, _ = _place()
        return [_remote(self.base + k, ins[0].at[:, pl.ds(r0, n), _my_cols(c, mine=False)], outs[0].at[:, pl.ds(r0, n), :],
                        (x, y, 1 - c), send_sems, recv_sems) for k, (r0, n) in enumerate(self.ranges)]

    def start(self, ins, outs, send_sems, recv_sems):
        for cp in self._copies(ins, outs, send_sems, recv_sems):
            cp.start()

    def finish(self, ins, outs, send_sems, recv_sems):
        for cp in self._copies(ins, outs, send_sems, recv_sems):
            cp.wait()


class Together:
    def __init__(self, rides):
        self.rides = rides
        self.ins, self.out_shapes, self.aliases, self.n_sems = [], [], {}, 0
        for r in rides:
            r.base = self.n_sems
            self.aliases.update({len(self.ins) + i: len(self.out_shapes) + o for i, o in r.aliases.items()})
            self.ins += r.ins
            self.out_shapes += r.out_shapes
            self.n_sems += r.n_sems

    def _each(self, step, ins, outs, send_sems, recv_sems):
        i = o = 0
        for r in self.rides:
            getattr(r, step)(ins[i:i + len(r.ins)], outs[o:o + len(r.out_shapes)], send_sems, recv_sems)
            i, o = i + len(r.ins), o + len(r.out_shapes)

    def start(self, *refs):
        self._each("start", *refs)

    def finish(self, *refs):
        self._each("finish", *refs)


def _pcall_riding(body, args, ride, first, last, *, in_specs, out_specs, out_shape, scratch_shapes=(), middle=None,
                  in_place=None, **kw):
    n_in, n_out = len(args), len(out_shape)
    in_place = in_place or {}
    if ride is None:
        return _pcall(body, in_specs=in_specs, out_specs=out_specs, out_shape=out_shape,
                      input_output_aliases=in_place, scratch_shapes=list(scratch_shapes), **kw)(*args), []
    k_in, k_out = len(ride.ins), len(ride.out_shapes)

    def riding(*refs):
        ins, r_in = refs[:n_in], refs[n_in:n_in + k_in]
        outs = refs[n_in + k_in:n_in + k_in + n_out]
        r_out = refs[n_in + k_in + n_out:n_in + k_in + n_out + k_out]
        scratch, (send_sems, recv_sems) = refs[n_in + k_in + n_out + k_out:-2], refs[-2:]

        @pl.when(first())
        def _():
            ride.start(r_in, r_out, send_sems, recv_sems)

        if middle is not None:
            @pl.when(middle())
            def _():
                ride.pass_on(r_in, r_out, send_sems, recv_sems)

        body(*ins, *outs, *scratch)

        @pl.when(last())
        def _():
            if middle is not None:
                ride.finish(r_in, r_out, send_sems, recv_sems, passed_on=True)
            else:
                ride.finish(r_in, r_out, send_sems, recv_sems)

    res = _pcall(riding, in_specs=list(in_specs) + [_ANY] * k_in, out_specs=list(out_specs) + [_ANY] * k_out,
                 out_shape=list(out_shape) + ride.out_shapes,
                 input_output_aliases={**in_place, **{n_in + i: n_out + o for i, o in ride.aliases.items()}},
                 scratch_shapes=list(scratch_shapes) + [pltpu.SemaphoreType.DMA((ride.n_sems,))] * 2,
                 **kw)(*args, *ride.ins)
    return res[:n_out], res[n_out:]


def ride_alone(ride, name):
    def body(*refs):
        n = len(ride.ins)
        ins, outs, (send_sems, recv_sems) = refs[:n], refs[n:-2], refs[-2:]
        ride.start(ins, outs, send_sems, recv_sems)
        ride.finish(ins, outs, send_sems, recv_sems)

    return _pcall(body, name=name, in_specs=[_ANY] * len(ride.ins), out_specs=[_ANY] * len(ride.out_shapes),
                  out_shape=ride.out_shapes, input_output_aliases=dict(ride.aliases),
                  scratch_shapes=[pltpu.SemaphoreType.DMA((ride.n_sems,))] * 2)(*ride.ins)


def mm(a, b, *, name, ta=False, tb=False, pro_a=None, epi=None, extras=(), out_dtypes=(F32,), n_dim=None,
       tiles=(None, None, None), b_view=None, out_view=None, into=None, ride=None, accs=()):
    widths = [d[0] if isinstance(d, tuple) else None for d in out_dtypes]
    out_dtypes = [d[1] if isinstance(d, tuple) else d for d in out_dtypes]
    if ta:
        k_dim, m_dim = a.shape
    else:
        m_dim, k_dim = a.shape
    if n_dim is None:
        n_dim = b.shape[0] if tb else b.shape[1]
    tn = tiles[1] or (n_dim if n_dim <= 1024 else _tile(n_dim, (1024, 512, 256, 128)))
    tk = tiles[2] or (k_dim if k_dim <= 1024 else _tile(k_dim, (1024, 512, 256, 128)))
    nk = k_dim // tk

    def vmem_bytes(tm):
        blocks = tm * tk * a.dtype.itemsize + tk * tn * b.dtype.itemsize
        blocks += sum(tm * (tn if e.shape[1] == n_dim else e.shape[1]) * e.dtype.itemsize
                      for e in extras if e.ndim == 2 and e.shape[0] > 1)
        blocks += sum(e.shape[0] * tm * e.shape[2] * e.dtype.itemsize for e in extras if e.ndim == 3)
        blocks += tm * sum((w or tn) * jnp.dtype(d).itemsize for w, d in zip(widths, out_dtypes))
        return 2 * blocks + tm * tn * 4

    tm = tiles[0] or next((t for t in (4096, 2048, 1024, 512, 256) if m_dim % t == 0 and vmem_bytes(t) <= MM_VMEM_BUDGET),
                          _tile(m_dim, (128,)))
    assert m_dim % tm == 0 and n_dim % tn == 0 and k_dim % tk == 0, (name, m_dim, n_dim, k_dim, tm, tn, tk)
    assert tn == n_dim or not (any(widths) or accs), name
    n_ex, n_out = len(extras), len(out_dtypes)
    n_into = 0 if into is None else 1
    dims = (((0 if ta else 1,), (1 if tb else 0,)), ((), ()))

    def body(a_ref, b_ref, *rest):
        ex_refs, out_refs = rest[:n_ex], rest[n_ex + n_into:n_ex + n_into + n_out]
        sum_refs = rest[n_ex + n_into + n_out:n_ex + n_into + n_out + len(accs)]

        def partial():
            av = a_ref[...]
            if pro_a is not None:
                av = pro_a(av)
            return lax.dot_general(av.astype(BF16), b_ref[...].astype(BF16), dims, preferred_element_type=F32)

        def finish(r):
            res = epi(r, *[e[...] for e in ex_refs]) if epi is not None else (r,)
            for o_ref, v in zip(out_refs, res):
                o_ref[...] = v.reshape(o_ref.shape).astype(o_ref.dtype)
            if accs:
                @pl.when(pl.program_id(0) == 0)
                def _():
                    for s_ref in sum_refs:
                        s_ref[...] = jnp.zeros_like(s_ref)

                for s_ref, v in zip(sum_refs, res[n_out:]):
                    s_ref[...] += v

        if nk == 1:
            finish(partial())
            return
        acc = rest[-1]
        k = pl.program_id(2)

        @pl.when(k == 0)
        def _():
            acc[...] = partial()

        @pl.when(k > 0)
        def _():
            acc[...] += partial()

        @pl.when(k == nk - 1)
        def _():
            finish(acc[...])

    def ex_spec(e):
        if e.ndim == 3:
            return pl.BlockSpec((e.shape[0], tm, e.shape[2]), lambda i, j, k: (0, i, 0))
        if e.shape == (m_dim, n_dim):
            return o_spec
        if e.shape[0] == m_dim:
            return pl.BlockSpec((tm, e.shape[1]), lambda i, j, k: (i, 0))
        return pl.BlockSpec(e.shape, lambda i, j, k: (0, 0))

    a_spec = pl.BlockSpec((tk, tm), lambda i, j, k: (k, i)) if ta else pl.BlockSpec((tm, tk), lambda i, j, k: (i, k))
    if b_view is not None:
        b_spec = b_view(tk, tn)
    else:
        b_spec = pl.BlockSpec((tn, tk), lambda i, j, k: (j, k)) if tb else pl.BlockSpec((tk, tn), lambda i, j, k: (k, j))
    o_spec = pl.BlockSpec((tm, tn), lambda i, j, k: (i, j))
    if out_view is None:
        out_specs = [o_spec if w is None else pl.BlockSpec((tm, w), lambda i, j, k: (i, 0)) for w in widths]
        out_shape = [jax.ShapeDtypeStruct((m_dim, w or n_dim), dt) for w, dt in zip(widths, out_dtypes)]
    else:
        assert n_out == 1
        out_specs = [out_view[1](tm, tn)]
        out_shape = [jax.ShapeDtypeStruct(out_view[0], out_dtypes[0])]
    out_specs = out_specs + [pl.BlockSpec((1, w), lambda i, j, k: (0, 0)) for w in accs]
    out_shape = out_shape + [jax.ShapeDtypeStruct((1, w), F32) for w in accs]
    grid = (m_dim // tm, n_dim // tn, nk)
    scratch = [pltpu.VMEM((tm, tn), F32)] if nk > 1 else []
    if ride is not None:
        assert into is None
        at = lambda ids: functools.reduce(jnp.logical_and, [pl.program_id(d) == i for d, i in enumerate(ids)])
        outs, landed = _pcall_riding(
            body, (a, b, *extras), ride, lambda: at((0, 0, 0)), lambda: at([g - 1 for g in grid]),
            name=name, grid=grid, in_specs=[a_spec, b_spec] + [ex_spec(e) for e in extras], out_specs=out_specs,
            out_shape=out_shape, scratch_shapes=scratch, compiler_params=_params(("arbitrary",) * 3))
        return (outs[0] if len(outs) == 1 else outs), landed
    outs = _pcall(
        body, name=name, grid=grid,
        in_specs=[a_spec, b_spec] + [ex_spec(e) for e in extras] + [_ANY] * n_into,
        out_specs=out_specs, out_shape=out_shape,
        input_output_aliases={2 + n_ex: 0} if n_into else {},
        scratch_shapes=scratch,
        compiler_params=_params(("arbitrary",) * 3 if accs else ("parallel", "parallel", "arbitrary")),
    )(a, b, *extras, *([into] if n_into else []))
    return outs[0] if len(outs) == 1 else outs


def rowwise(fn, ins, outs, *, name, accs=(), tm=256):
    rows = ins[0].shape[0]
    tm = min(tm, rows)
    n_in, n_out, n_acc = len(ins), len(outs), len(accs)

    def body(*refs):
        in_refs, out_refs, acc_refs = refs[:n_in], refs[n_in:n_in + n_out], refs[n_in + n_out:]
        res, sums = fn(*[r[...] for r in in_refs])
        for o_ref, v in zip(out_refs, res):
            o_ref[...] = v.astype(o_ref.dtype)
        if n_acc:
            @pl.when(pl.program_id(0) == 0)
            def _():
                for a_ref in acc_refs:
                    a_ref[...] = jnp.zeros_like(a_ref)

            for a_ref, s in zip(acc_refs, sums):
                a_ref[...] += s

    def spec(arr):
        if arr.shape[0] == rows:
            return pl.BlockSpec((tm, arr.shape[1]), lambda i: (i, 0))
        return pl.BlockSpec(arr.shape, lambda i: (0, 0))

    res = _pcall(
        body, name=name, grid=(rows // tm,),
        in_specs=[spec(a) for a in ins],
        out_specs=[pl.BlockSpec((tm, w), lambda i: (i, 0)) for w, _ in outs]
        + [pl.BlockSpec((1, w), lambda i: (0, 0)) for w in accs],
        out_shape=[jax.ShapeDtypeStruct((rows, w), dt) for w, dt in outs]
        + [jax.ShapeDtypeStruct((1, w), F32) for w in accs],
        compiler_params=_params(("arbitrary",) if n_acc else ("parallel",)),
    )(*ins)
    return res


def _relu2(v):
    r = jnp.maximum(v, 0.0)
    return r * r


def _gelu(x):
    c = math.sqrt(2.0 / math.pi)
    return 0.5 * x * (1.0 + jnp.tanh(c * (x + 0.044715 * x * x * x)))


def _gelu_grad(x):
    c = math.sqrt(2.0 / math.pi)
    t = jnp.tanh(c * (x + 0.044715 * x * x * x))
    return 0.5 * (1.0 + t) + 0.5 * x * (1.0 - t * t) * c * (1.0 + 3 * 0.044715 * x * x)


def _sigmoid(x):
    return 1.0 / (1.0 + jnp.exp(-x))


def _layer_norm(h, mix, g, b):
    r = DN_ALPHA * h + mix
    mu = jnp.mean(r, axis=-1, keepdims=True)
    xc = r - mu
    var = jnp.mean(xc * xc, axis=-1, keepdims=True)
    return xc * lax.rsqrt(var + LN_EPS) * g + b


def _layer_norm_bwd(h, mix, g, dy):
    r = DN_ALPHA * h + mix
    mu = jnp.mean(r, axis=-1, keepdims=True)
    xc = r - mu
    var = jnp.mean(xc * xc, axis=-1, keepdims=True)
    rstd = lax.rsqrt(var + LN_EPS)
    xhat = xc * rstd
    dxh = dy * g
    m1 = jnp.mean(dxh, axis=-1, keepdims=True)
    m2 = jnp.mean(dxh * xhat, axis=-1, keepdims=True)
    dr = rstd * (dxh - m1 - xhat * m2)
    return dr, jnp.sum(dy * xhat, axis=0, keepdims=True), jnp.sum(dy, axis=0, keepdims=True)


def ln_bwd(h, mix, g, dy, name):
    def fn(h, mix, g, dy):
        dr, dg, db = _layer_norm_bwd(h, mix, g, dy)
        return (dr, dr), (dg, db)
    return rowwise(fn, (h, mix, g, dy), ((D_MODEL, F32), (D_MODEL, BF16)), accs=(D_MODEL, D_MODEL), name=name, tm=512)


def _rms(x, g):
    r = lax.rsqrt(jnp.mean(x * x, axis=-1, keepdims=True) + RMS_EPS)
    return x * r * g


def _rms_bwd(x, g, dy):
    r = lax.rsqrt(jnp.mean(x * x, axis=-1, keepdims=True) + RMS_EPS)
    xn = x * r
    dyg = dy * g
    dx = r * (dyg - xn * jnp.mean(dyg * xn, axis=-1, keepdims=True))
    return dx, jnp.sum(dy * xn, axis=0, keepdims=True)


def _s5_disc(lr, li, ldt):
    dt = jnp.exp(ldt)
    mag = jnp.exp(lr * dt)
    cs, sn = jnp.cos(li * dt), jnp.sin(li * dt)
    ar, ai = mag * cs, mag * sn
    inv = 1.0 / (lr * lr + li * li)
    n_re = (ar - 1.0) * lr + ai * li
    n_im = ai * lr - (ar - 1.0) * li
    return dt, mag, cs, sn, ar, ai, inv, n_re, n_im


def s5_prep(lr, li, ldt, b_re, b_im):
    def fn(lr, li, ldt, b_re, b_im):
        _, _, _, _, ar, ai, inv, n_re, n_im = _s5_disc(lr, li, ldt)
        cr, ci = n_re * inv, n_im * inv
        return (ar, ai, cr * b_re - ci * b_im, cr * b_im + ci * b_re), ()
    return rowwise(fn, (lr, li, ldt, b_re, b_im), ((1, F32), (1, F32), (SSM_GROUP, F32), (SSM_GROUP, F32)),
                   name="s5_prep", tm=512)


def s5_prep_bwd(lr, li, ldt, b_re, b_im, dar, dai, dbb_re, dbb_im):
    def fn(lr, li, ldt, b_re, b_im, dar, dai, dbb_re, dbb_im):
        dt, mag, cs, sn, ar, ai, inv, n_re, n_im = _s5_disc(lr, li, ldt)
        cr, ci = n_re * inv, n_im * inv
        db_re = cr * dbb_re + ci * dbb_im
        db_im = cr * dbb_im - ci * dbb_re
        dcr = jnp.sum(dbb_re * b_re + dbb_im * b_im, axis=-1, keepdims=True)
        dci = jnp.sum(dbb_im * b_re - dbb_re * b_im, axis=-1, keepdims=True)
        dar = dar + (dcr * lr - dci * li) * inv
        dai = dai + (dcr * li + dci * lr) * inv
        dinv = dcr * n_re + dci * n_im
        dlr = (dcr * (ar - 1.0) + dci * ai) * inv - 2.0 * lr * inv * inv * dinv
        dli = (dcr * ai - dci * (ar - 1.0)) * inv - 2.0 * li * inv * inv * dinv
        dmag = dar * cs + dai * sn
        dth = dai * ar - dar * ai
        dlr = dlr + dmag * mag * dt
        dli = dli + dth * dt
        ddt = dmag * mag * lr + dth * li
        return (dlr, dli, ddt * dt, db_re, db_im), ()
    return rowwise(fn, (lr, li, ldt, b_re, b_im, dar, dai, dbb_re, dbb_im),
                   ((1, F32), (1, F32), (1, F32), (SSM_GROUP, F32), (SSM_GROUP, F32)), name="s5_prep_bwd", tm=512)


def group_sum(x):
    def body(x_ref, o_ref):
        o_ref[...] = jnp.sum(x_ref[...], axis=1)
    return _pcall(body, name="s5_group_sum", out_shape=jax.ShapeDtypeStruct((N_GROUPS, 1), F32))(
        x.reshape(N_GROUPS, SSM_STATE, 1))


GROUPS_PER_TILE = LANES // SSM_GROUP
TILE_STATES = GROUPS_PER_TILE * SSM_STATE
N_UTILES = D_MODEL // LANES


SUBLANES = 8
SCAN_STRIP = 1024
N_STRIPS = N_STATES // SCAN_STRIP
_NT = (((1,), (1,)), ((), ()))
_TN = (((0,), (0,)), ((), ()))


def _scan_coefs(are, aim, shifted, reverse):
    ar = are[...]
    ai = -aim[...] if reverse else aim[...]
    powers = {1: (ar, ai)}
    for d in (2, 4):
        r, i = powers[d // 2]
        powers[d] = (r * r - i * i, 2.0 * r * i)
    rid = lax.broadcasted_iota(jnp.int32, (SUBLANES, N_STATES), 0)
    first = (rid == SUBLANES - 1) if reverse else (rid == 0)
    masks = [(1, first)] + [(d, (rid <= SUBLANES - 1 - d) if reverse else (rid >= d)) for d in (1, 2, 4)]
    for n, (d, keep) in enumerate(masks):
        for part in (0, 1):
            shifted[2 * n + part][...] = jnp.where(keep, jnp.broadcast_to(powers[d][part], (SUBLANES, N_STATES)), 0.0)


def _tile_scan(xr, xi, shifted, nbr_re, nbr_im, reverse):
    for n, d in enumerate((1, 1, 2, 4)):
        by = SUBLANES - d if reverse else d
        fr, fi = (nbr_re, nbr_im) if n == 0 else (xr, xi)
        sr, si = pltpu.roll(fr, by, 0), pltpu.roll(fi, by, 0)
        kr, ki = shifted[2 * n], shifted[2 * n + 1]
        xr, xi = xr + kr * sr - ki * si, xi + kr * si + ki * sr
    return xr, xi


def _tile_rows(t):
    return pl.ds(pl.multiple_of(t * SUBLANES, SUBLANES), SUBLANES)


def s5_fwd(u, bbd_re, bbd_im, cbd_re, cbd_imn, a_re, a_im, dskip, ride=None, t_rows=512):
    seq = u.shape[0]
    t_rows = min(t_rows, seq)
    n_tiles = t_rows // SUBLANES

    def body(u_ref, bre, bim, cre, cimn, are, aim, d_ref, y_ref, gelu_ref, hre_ref, him_ref, car_re, car_im, *shifted):
        @pl.when(pl.program_id(0) == 0)
        def _():
            car_re[...] = jnp.zeros_like(car_re)
            car_im[...] = jnp.zeros_like(car_im)
            _scan_coefs(are, aim, shifted, reverse=False)

        uf = u_ref[...]
        ub = uf.astype(BF16)
        for j in range(N_UTILES):
            uj = ub[:, LANES * j:LANES * (j + 1)]
            sl = slice(TILE_STATES * j, TILE_STATES * (j + 1))
            hre_ref[:, sl] = jnp.dot(uj, bre[j], preferred_element_type=F32)
            him_ref[:, sl] = jnp.dot(uj, bim[j], preferred_element_type=F32)
        for s in range(N_STRIPS):
            cols = pl.ds(s * SCAN_STRIP, SCAN_STRIP)
            coefs = [c[:, cols] for c in shifted]

            def step(t, before):
                rows = _tile_rows(t)
                hr, hi = _tile_scan(hre_ref[rows, cols], him_ref[rows, cols], coefs, before[0], before[1], False)
                hre_ref[rows, cols] = hr
                him_ref[rows, cols] = hi
                return hr, hi

            cr, ci = lax.fori_loop(0, n_tiles, step, (car_re[:, cols], car_im[:, cols]))
            car_re[:, cols] = cr
            car_im[:, cols] = ci
        dv = d_ref[...]
        for j in range(N_UTILES):
            st = slice(TILE_STATES * j, TILE_STATES * (j + 1))
            yj = (jnp.dot(hre_ref[:, st].astype(BF16), cre[j], preferred_element_type=F32)
                  + jnp.dot(him_ref[:, st].astype(BF16), cimn[j], preferred_element_type=F32))
            sl = slice(LANES * j, LANES * (j + 1))
            yj = yj + dv[:, sl] * uf[:, sl]
            y_ref[:, sl] = yj
            gelu_ref[:, sl] = _gelu(yj).astype(gelu_ref.dtype)

    full3 = lambda a: pl.BlockSpec(a.shape, lambda i: (0, 0, 0))
    full2 = lambda a: pl.BlockSpec(a.shape, lambda i: (0, 0))
    tile = pltpu.VMEM((SUBLANES, N_STATES), F32)
    n_chunks = seq // t_rows
    return _pcall_riding(
        body, (u, bbd_re, bbd_im, cbd_re, cbd_imn, a_re, a_im, dskip), ride,
        lambda: pl.program_id(0) == 0, lambda: pl.program_id(0) == n_chunks - 1,
        middle=(lambda: pl.program_id(0) == (7 * n_chunks) // 8) if ride is not None else None,
        name="s5_fwd", grid=(n_chunks,),
        in_specs=[pl.BlockSpec((t_rows, D_MODEL), lambda i: (i, 0)), full3(bbd_re), full3(bbd_im), full3(cbd_re),
                  full3(cbd_imn), full2(a_re), full2(a_im), full2(dskip)],
        out_specs=[pl.BlockSpec((t_rows, D_MODEL), lambda i: (i, 0)),
                   pl.BlockSpec((t_rows, D_MODEL), lambda i: (i, 0)),
                   pl.BlockSpec((t_rows, N_STATES), lambda i: (i, 0)),
                   pl.BlockSpec((t_rows, N_STATES), lambda i: (i, 0))],
        out_shape=[jax.ShapeDtypeStruct((seq, D_MODEL), F32),
                   jax.ShapeDtypeStruct((seq, D_MODEL), BF16),
                   jax.ShapeDtypeStruct((seq, N_STATES), F32),
                   jax.ShapeDtypeStruct((seq, N_STATES), F32)],
        scratch_shapes=[tile] * 10,
        compiler_params=_params(("arbitrary",)))


def s5_bwd(dy, u, dres, h_re, h_im, bbd_re, bbd_im, cbd_re, cbd_imn, a_re, a_im, dskip, ride=None, t_rows=256):
    seq = u.shape[0]
    t_rows = min(t_rows, seq)
    n_chunks = seq // t_rows

    n_tiles = t_rows // SUBLANES

    def body(dy_ref, u_ref, dres_ref, hre_ref, him_ref, hpre_ref, hpim_ref, bre, bim, cre, cimn, are, aim, d_ref,
             dx_ref, dbre, dbim, dcre, dcimn, dar_ref, dai_ref, dd_ref, lre, lim, car_re, car_im, acc_re, acc_im,
             *shifted):
        i = pl.program_id(0)

        @pl.when(i == 0)
        def _():
            for r in (car_re, car_im, acc_re, acc_im, dbre, dbim, dcre, dcimn, dd_ref):
                r[...] = jnp.zeros_like(r)
            _scan_coefs(are, aim, shifted, reverse=True)

        dyf = dy_ref[...]
        dyb = dyf.astype(BF16)
        uf = u_ref[...]
        ub = uf.astype(BF16)
        for j in range(N_UTILES):
            dyj = dyb[:, LANES * j:LANES * (j + 1)]
            st = slice(TILE_STATES * j, TILE_STATES * (j + 1))
            lre[:, st] = lax.dot_general(dyj, cre[j], _NT, preferred_element_type=F32)
            lim[:, st] = lax.dot_general(dyj, cimn[j], _NT, preferred_element_type=F32)
        has_pred = (i < n_chunks - 1).astype(F32)
        last_row = lax.broadcasted_iota(jnp.int32, (SUBLANES, SCAN_STRIP), 0) == SUBLANES - 1
        for s in range(N_STRIPS):
            cols = pl.ds(s * SCAN_STRIP, SCAN_STRIP)
            coefs = [c[:, cols] for c in shifted]
            before_re, before_im = hpre_ref[:, cols] * has_pred, hpim_ref[:, cols] * has_pred

            def step(k, carry):
                after_re, after_im, dar, dai = carry
                t = n_tiles - 1 - k
                rows = _tile_rows(t)
                lr, li = _tile_scan(lre[rows, cols], lim[rows, cols], coefs, after_re, after_im, True)
                lre[rows, cols] = lr
                lim[rows, cols] = li
                prev = _tile_rows(jnp.maximum(t - 1, 0))
                pre_re = jnp.where(t == 0, before_re, hre_ref[prev, cols])
                pre_im = jnp.where(t == 0, before_im, him_ref[prev, cols])
                hpr = pltpu.roll(jnp.where(last_row, pre_re, hre_ref[rows, cols]), 1, 0)
                hpi = pltpu.roll(jnp.where(last_row, pre_im, him_ref[rows, cols]), 1, 0)
                return lr, li, dar + lr * hpr + li * hpi, dai + li * hpr - lr * hpi

            cr, ci, dar, dai = lax.fori_loop(0, n_tiles, step, (car_re[:, cols], car_im[:, cols],
                                                               acc_re[:, cols], acc_im[:, cols]))
            car_re[:, cols] = cr
            car_im[:, cols] = ci
            acc_re[:, cols] = dar
            acc_im[:, cols] = dai

        dv = d_ref[...]
        for j in range(N_UTILES):
            sl = slice(LANES * j, LANES * (j + 1))
            st = slice(TILE_STATES * j, TILE_STATES * (j + 1))
            lrj = lre[:, st].astype(BF16)
            lij = lim[:, st].astype(BF16)
            du = (lax.dot_general(lrj, bre[j], _NT, preferred_element_type=F32)
                  + lax.dot_general(lij, bim[j], _NT, preferred_element_type=F32))
            dx_ref[:, sl] = du + dv[:, sl] * dyf[:, sl] + DN_ALPHA * dres_ref[:, sl]
            uj = ub[:, sl]
            dbre[j] += lax.dot_general(uj, lrj, _TN, preferred_element_type=F32)
            dbim[j] += lax.dot_general(uj, lij, _TN, preferred_element_type=F32)
            dyj = dyb[:, sl]
            dcre[j] += lax.dot_general(hre_ref[:, st].astype(BF16), dyj, _TN, preferred_element_type=F32)
            dcimn[j] += lax.dot_general(him_ref[:, st].astype(BF16), dyj, _TN, preferred_element_type=F32)
        dd_ref[...] += jnp.sum(dyf * uf, axis=0, keepdims=True)

        @pl.when(i == n_chunks - 1)
        def _():
            dar_ref[...] = jnp.sum(acc_re[...], axis=0, keepdims=True)
            dai_ref[...] = jnp.sum(acc_im[...], axis=0, keepdims=True)

    rev = lambda i: (n_chunks - 1 - i, 0)
    prev_tile = lambda i: (jnp.maximum((n_chunks - 1 - i) * n_tiles - 1, 0), 0)
    once = pl.Buffered(1)
    full3 = lambda a: pl.BlockSpec(a.shape, lambda i: (0, 0, 0), pipeline_mode=once)
    full2 = lambda a: pl.BlockSpec(a.shape, lambda i: (0, 0), pipeline_mode=once)
    acc3 = lambda shape: pl.BlockSpec(shape, lambda i: (0, 0, 0))
    acc2 = lambda shape: pl.BlockSpec(shape, lambda i: (0, 0))
    tile = pltpu.VMEM((SUBLANES, N_STATES), F32)
    return _pcall_riding(
        body, (dy, u, dres, h_re, h_im, h_re, h_im, bbd_re, bbd_im, cbd_re, cbd_imn, a_re, a_im, dskip), ride,
        lambda: pl.program_id(0) == 0, lambda: pl.program_id(0) == n_chunks - 1,
        name="s5_bwd", grid=(n_chunks,),
        in_specs=[pl.BlockSpec((t_rows, D_MODEL), rev), pl.BlockSpec((t_rows, D_MODEL), rev),
                  pl.BlockSpec((t_rows, D_MODEL), rev),
                  pl.BlockSpec((t_rows, N_STATES), rev), pl.BlockSpec((t_rows, N_STATES), rev),
                  pl.BlockSpec((SUBLANES, N_STATES), prev_tile), pl.BlockSpec((SUBLANES, N_STATES), prev_tile),
                  full3(bbd_re), full3(bbd_im), full3(cbd_re), full3(cbd_imn), full2(a_re), full2(a_im), full2(dskip)],
        out_specs=[pl.BlockSpec((t_rows, D_MODEL), rev), acc3(bbd_re.shape), acc3(bbd_im.shape), acc3(cbd_re.shape),
                   acc3(cbd_imn.shape), acc2((1, N_STATES)), acc2((1, N_STATES)), acc2((1, D_MODEL))],
        out_shape=[jax.ShapeDtypeStruct((seq, D_MODEL), F32), jax.ShapeDtypeStruct(bbd_re.shape, F32),
                   jax.ShapeDtypeStruct(bbd_im.shape, F32), jax.ShapeDtypeStruct(cbd_re.shape, F32),
                   jax.ShapeDtypeStruct(cbd_imn.shape, F32), jax.ShapeDtypeStruct((1, N_STATES), F32),
                   jax.ShapeDtypeStruct((1, N_STATES), F32), jax.ShapeDtypeStruct((1, D_MODEL), F32)],
        scratch_shapes=[pltpu.VMEM((t_rows, N_STATES), F32), pltpu.VMEM((t_rows, N_STATES), F32)] + [tile] * 12,
        in_place={2: 0},
        compiler_params=_params(("arbitrary",)))


def _eye_groups():
    return jnp.eye(GROUPS_PER_TILE, dtype=F32)


def _blockdiag_in(bb):
    t = bb.transpose(0, 2, 1).reshape(N_UTILES, GROUPS_PER_TILE, SSM_GROUP, SSM_STATE)
    bd = jnp.einsum("jgcp,gh->jgchp", t, _eye_groups())
    return bd.reshape(N_UTILES, LANES, TILE_STATES)


def _blockdiag_in_t(d):
    t = jnp.einsum("jgchp,gh->jgcp", d.reshape(N_UTILES, GROUPS_PER_TILE, SSM_GROUP, GROUPS_PER_TILE, SSM_STATE),
                   _eye_groups())
    return t.reshape(N_GROUPS, SSM_GROUP, SSM_STATE).transpose(0, 2, 1)


def _blockdiag_out(c):
    t = c.transpose(0, 2, 1).reshape(N_UTILES, GROUPS_PER_TILE, SSM_STATE, SSM_GROUP)
    bd = jnp.einsum("jhpc,hg->jhpgc", t, _eye_groups())
    return bd.reshape(N_UTILES, TILE_STATES, LANES)


def _blockdiag_out_t(d):
    t = jnp.einsum("jhpgc,hg->jhpc", d.reshape(N_UTILES, GROUPS_PER_TILE, SSM_STATE, GROUPS_PER_TILE, SSM_GROUP),
                   _eye_groups())
    return t.reshape(N_GROUPS, SSM_STATE, SSM_GROUP).transpose(0, 2, 1)


ATT_TQ = 512
ATT_TK = 512
LOG2E = math.log2(math.e)
LN2 = math.log(2.0)
Q_PRESCALE = SM_SCALE * LOG2E


def _loop_in_pairs(n, step, carry, start=0):
    pairs = (n - start) // 2

    def two(t, c):
        return step(start + 2 * t + 1, step(start + 2 * t, c))

    carry = lax.fori_loop(0, pairs, two, carry)
    return lax.fori_loop(start + 2 * pairs, n, step, carry)


def _causal(s, transposed=False):
    r = lax.broadcasted_iota(jnp.int32, s.shape, 0)
    c = lax.broadcasted_iota(jnp.int32, s.shape, 1)
    return jnp.where((r <= c) if transposed else (c <= r), s, NEG_INF)


def _q_specs(rows, at):
    def nope(*ids):
        r, h = at(*ids)
        return r, 3 * (h // HEADS_PER_CHIP) + h % HEADS_PER_CHIP

    def rope(*ids):
        r, h = at(*ids)
        return r, 3 * (h // HEADS_PER_CHIP) + HEADS_PER_CHIP

    return [pl.BlockSpec((rows, LANES), nope), pl.BlockSpec((rows, LANES), rope)]


def _kv_specs(rows, at):
    def col(f):
        def index(*ids):
            r, h = at(*ids)
            return r, f(h)
        return index

    return [pl.BlockSpec((rows, LANES), col(lambda h: 2 * h)), pl.BlockSpec((rows, LANES), col(lambda h: h % HEADS_PER_CHIP)),
            pl.BlockSpec((rows, LANES), col(lambda h: 2 * h + 1))]


def _cat(a, b):
    return jnp.concatenate([a, b], axis=1)


def attn_fwd(q, kv, kr, ride=None, tq=ATT_TQ, tk=ATT_TK):
    seq = q.shape[0]
    n_heads = N_HEADS
    tq, tk = min(tq, seq), min(tk, seq)
    assert tq == tk

    def body(qn_ref, qr_ref, kn_ref, kr_ref, v_ref, o_ref, lse_ref):
        qi = pl.program_id(1)
        qv = _cat(qn_ref[...], qr_ref[...])
        jd = qi

        def block(j, carry, diag):
            m, l, acc = carry
            rows = pl.ds(pl.multiple_of(j * tk, tk), tk)
            s = lax.dot_general(qv, _cat(kn_ref[rows, :], kr_ref[rows, :]), _NT, preferred_element_type=F32)
            if diag:
                s = _causal(s)
            m_new = jnp.maximum(m, jnp.max(s, axis=-1, keepdims=True))
            p = jnp.exp2(s - m_new)
            corr = jnp.exp2(m - m_new)
            l = l * corr + jnp.sum(p, axis=-1, keepdims=True)
            acc = acc * corr + jnp.dot(p.astype(BF16), v_ref[rows, :], preferred_element_type=F32)
            return m_new, l, acc

        init = (jnp.full((tq, 1), NEG_INF, F32), jnp.zeros((tq, 1), F32), jnp.zeros((tq, V_HEAD), F32))
        carry = _loop_in_pairs(jd, lambda j, c: block(j, c, False), init)
        m, l, acc = block(jd, carry, True)
        o_ref[...] = acc / l
        lse_ref[...] = jnp.transpose(jnp.broadcast_to(m + jnp.log2(l), (tq, LANES)))[:1, :]

    n_q = seq // tq
    return _pcall_riding(
        body, (q, q, kv, kr, kv), ride,
        lambda: (pl.program_id(0) == 0) & (pl.program_id(1) == 0),
        lambda: (pl.program_id(0) == n_heads - 1) & (pl.program_id(1) == n_q - 1),
        middle=(lambda: (pl.program_id(0) == (5 * n_heads) // 8) & (pl.program_id(1) == 0)) if ride is not None else None,
        name="attn_fwd", grid=(n_heads, n_q),
        in_specs=_q_specs(tq, lambda h, i: (i, h)) + _kv_specs(seq, lambda h, i: (0, h)),
        out_specs=[pl.BlockSpec((tq, V_HEAD), lambda h, i: (i, h)),
                   pl.BlockSpec((None, None, 1, tq), lambda h, i: (h, i, 0, 0))],
        out_shape=[jax.ShapeDtypeStruct((seq, n_heads * V_HEAD), F32),
                   jax.ShapeDtypeStruct((n_heads, n_q, 1, tq), F32)],
        compiler_params=_params(("arbitrary", "arbitrary")))


def attn_bwd(q, kv, kr, do, lse_row, delta_row, tq=ATT_TK):
    seq = q.shape[0]
    tq = min(tq, seq)
    n_blk = seq // tq

    def body(qn_ref, qr_ref, kn_ref, kr_ref, v_ref, do_ref, lse_ref, delta_ref, dqn_ref, dqr_ref, dkv_ref, dkr_ref, dq_acc):
        head, kj = pl.program_id(0), pl.program_id(1)

        @pl.when(kj == 0)
        def _():
            dq_acc[...] = jnp.zeros_like(dq_acc)

        kc = _cat(kn_ref[...], kr_ref[...])
        vv = v_ref[...]

        def block(i, carry, diag):
            dk, dv = carry
            rows = pl.ds(pl.multiple_of(i * tq, tq), tq)
            qv = _cat(qn_ref[rows, :], qr_ref[rows, :])
            st = lax.dot_general(kc, qv, _NT, preferred_element_type=F32)
            if diag:
                st = _causal(st, transposed=True)
            pt = jnp.exp2(st - lse_ref[0, pl.ds(i, 1), :])
            dob = do_ref[rows, :].astype(BF16)
            dv = dv + jnp.dot(pt.astype(BF16), dob, preferred_element_type=F32)
            dpt = lax.dot_general(vv, dob, _NT, preferred_element_type=F32)
            dst = (pt * (dpt - delta_ref[0, pl.ds(i, 1), :])).astype(BF16)
            dk = dk + jnp.dot(dst, qv, preferred_element_type=F32)
            dq_acc[rows, :] += lax.dot_general(dst, kc, _TN, preferred_element_type=F32)
            return dk, dv

        carry = block(kj, (jnp.zeros((tq, 2 * LANES), F32), jnp.zeros((tq, V_HEAD), F32)), True)
        dk, dv = _loop_in_pairs(n_blk, lambda i, c: block(i, c, False), carry, start=kj + 1)
        dk = dk * LN2
        dkv_ref[...] = _cat(dk[:, :LANES], dv).astype(dkv_ref.dtype)
        lane = lax.broadcasted_iota(jnp.int32, (tq, LANES), 1)
        mine = (lane // HALF_ROPE) % HEADS_PER_CHIP == head % HEADS_PER_CHIP
        dkr_ref[0] = jnp.where(mine, dk[:, LANES:], 0.0)

        @pl.when(kj == n_blk - 1)
        def _():
            dqn_ref[...] = dq_acc[:, :LANES] * SM_SCALE

        @pl.when((kj == n_blk - 1) & (head % HEADS_PER_CHIP == 0))
        def _():
            dqr_ref[...] = dq_acc[:, LANES:] * SM_SCALE

        @pl.when((kj == n_blk - 1) & (head % HEADS_PER_CHIP > 0))
        def _():
            dqr_ref[...] += dq_acc[:, LANES:] * SM_SCALE

    return _pcall(
        body, name="attn_bwd", grid=(N_HEADS, n_blk),
        in_specs=_q_specs(seq, lambda h, j: (0, h)) + _kv_specs(tq, lambda h, j: (j, h))
        + [pl.BlockSpec((seq, V_HEAD), lambda h, j: (0, h)),
           pl.BlockSpec((1, n_blk, tq), lambda h, j: (h, 0, 0)),
           pl.BlockSpec((1, n_blk, tq), lambda h, j: (h, 0, 0))],
        out_specs=[pl.BlockSpec((seq, LANES), lambda h, j: (0, h)),
                   pl.BlockSpec((seq, LANES), lambda h, j: (0, h // HEADS_PER_CHIP)),
                   pl.BlockSpec((tq, QK_NOPE + V_HEAD), lambda h, j: (j, h)),
                   pl.BlockSpec((1, tq, LANES), lambda h, j: (h, j, 0))],
        out_shape=[jax.ShapeDtypeStruct((seq, N_HEADS * QK_NOPE), F32),
                   jax.ShapeDtypeStruct((seq, N_CHIPS * LANES), F32),
                   jax.ShapeDtypeStruct((seq, N_HEADS * (QK_NOPE + V_HEAD)), BF16),
                   jax.ShapeDtypeStruct((N_HEADS, seq, LANES), F32)],
        scratch_shapes=[pltpu.VMEM((seq, 2 * LANES), F32)],
        compiler_params=_params(("arbitrary", "arbitrary")),
    )(q, q, kv, kr, kv, do, lse_row, delta_row)


HEADS_PER_CHIP = N_HEADS // N_CHIPS
Q_CHIP = HEADS_PER_CHIP * QK_DIM
Q_CHIP_NOPE = HEADS_PER_CHIP * QK_NOPE


def _perm_q_cols(w):
    t = w.reshape(w.shape[0], HEADS_PER_CHIP, QK_DIM)
    return jnp.concatenate([t[:, :, :QK_NOPE].reshape(w.shape[0], -1),
                            t[:, :, QK_NOPE:QK_NOPE + HALF_ROPE].reshape(w.shape[0], -1),
                            t[:, :, QK_NOPE + HALF_ROPE:].reshape(w.shape[0], -1)], axis=1)


def _unperm_q_cols(w):
    r = w.shape[0]
    nope = w[:, :Q_CHIP_NOPE].reshape(r, HEADS_PER_CHIP, QK_NOPE)
    r1 = w[:, Q_CHIP_NOPE:Q_CHIP_NOPE + QK_ROPE].reshape(r, HEADS_PER_CHIP, HALF_ROPE)
    r2 = w[:, Q_CHIP_NOPE + QK_ROPE:].reshape(r, HEADS_PER_CHIP, HALF_ROPE)
    return jnp.concatenate([nope, r1, r2], axis=2).reshape(r, Q_CHIP)


def _pad_kva_cols(w):
    z = jnp.zeros((w.shape[0], HALF_ROPE), w.dtype)
    return jnp.concatenate([w[:, :KV_LORA], w[:, KV_LORA:KV_LORA + HALF_ROPE], z, w[:, KV_LORA + HALF_ROPE:], z], axis=1)


def _unpad_kva_cols(w):
    return jnp.concatenate([w[:, :KV_LORA], w[:, KV_LORA:KV_LORA + HALF_ROPE],
                            w[:, KV_LORA + QK_ROPE:KV_LORA + QK_ROPE + HALF_ROPE]], axis=1)


def _rope_tile(t, cs, sn):
    return t * cs + pltpu.roll(t, LANES // 2, 1) * sn


def _rope_tile_bwd(d, cs, sn):
    return d * cs + pltpu.roll(d * sn, LANES // 2, 1)


def _b_cols(tk, tn):
    return pl.BlockSpec((None, tk, tn), lambda i, j, k: (j, k, 0))


def _b_cols_t(tk, tn):
    return pl.BlockSpec((None, tn, tk), lambda i, j, k: (k, j, 0))


def _out_cols(shape):
    return shape, lambda tm, tn: pl.BlockSpec((None, tm, tn), lambda i, j, k: (j, i, 0))


def glu_proj(y, w_glu, tm=1024):
    seq, k_dim = y.shape
    tn = w_glu.shape[2]
    tm = min(tm, seq)
    half = N_CHIPS // 2

    def body(y_ref, wv_ref, wg_ref, val_ref, gate_ref, z_ref):
        yv = y_ref[...]
        v = jnp.dot(yv, wv_ref[...], preferred_element_type=F32)
        gt = jnp.dot(yv, wg_ref[...], preferred_element_type=F32)
        val_ref[...] = v
        gate_ref[...] = gt
        z_ref[...] = (v * _sigmoid(gt)).astype(z_ref.dtype)

    tile = pl.BlockSpec((tm, tn), lambda i, j: (i, j))
    return _pcall(
        body, name="glu_proj", grid=(seq // tm, half),
        in_specs=[pl.BlockSpec((tm, k_dim), lambda i, j: (i, 0)),
                  pl.BlockSpec((None, k_dim, tn), lambda i, j: (j, 0, 0)),
                  pl.BlockSpec((None, k_dim, tn), lambda i, j: (j + half, 0, 0))],
        out_specs=[tile, tile, tile],
        out_shape=[jax.ShapeDtypeStruct((seq, half * tn), F32), jax.ShapeDtypeStruct((seq, half * tn), F32),
                   jax.ShapeDtypeStruct((seq, half * tn), BF16)],
        compiler_params=_params(("parallel", "parallel")),
    )(y, w_glu, w_glu)


def _halves(a):
    return a.reshape(N_CHIPS, 2, a.shape[1] // 2, a.shape[2])


def device_step(x, positions, target, w, comm=None):
    seq = x.shape[0]
    w = dict(w)

    def gathered(names, outs):
        for n, a in zip(names, outs):
            if isinstance(n, tuple):
                w[n[0]] = [a.reshape(v.shape) if l == n[1] else v for l, v in enumerate(w[n[0]])]
            else:
                w[n] = a.reshape(w[n].shape)

    def ride_for(names):
        if comm is None:
            return None
        return GatherRide([_halves(w[n[0]][n[1]] if isinstance(n, tuple) else w[n]) for n in names])

    first_ride = ("ssm_w_glu", "ssm_w_out", ("w_ff1", 0), ("w_ff2", 0))
    mla_ride = ("kv_w_a", "kv_w_b", "q_w_a", "q_w_b", "attn_w_o")
    second_ride = (("w_ff1", 1), ("w_ff2", 1))

    inv_freq = ROPE_THETA ** (-jnp.arange(HALF_ROPE, dtype=F32) / HALF_ROPE)
    ang = positions.astype(F32)[:, None] * jnp.tile(inv_freq, LANES // HALF_ROPE)
    cos, sin = jnp.cos(ang), jnp.sin(ang)
    quarter = jnp.arange(LANES) // HALF_ROPE
    sign = jnp.where(quarter < 2, -1.0, 1.0).astype(F32)
    own = (quarter % 2 == 0).astype(F32)
    cos_q, sin_q = cos, sin * sign
    cos_k, sin_k = cos * own, sin * (sign * own)
    ff_tile = D_FF // N_CHIPS
    pack_shape = (N_CHIPS, EARLY_ROWS, PACK_W)

    lr = w["ssm_lam_re"].reshape(N_STATES, 1)
    li = w["ssm_lam_im"].reshape(N_STATES, 1)
    ldt = jnp.repeat(w["ssm_log_dt"].reshape(N_GROUPS), SSM_STATE).reshape(N_STATES, 1)
    b_re = w["ssm_b_re"].reshape(N_STATES, SSM_GROUP)
    b_im = w["ssm_b_im"].reshape(N_STATES, SSM_GROUP)
    a_re, a_im, bb_re, bb_im = s5_prep(lr, li, ldt, b_re, b_im)
    a_re, a_im = a_re.reshape(1, N_STATES), a_im.reshape(1, N_STATES)
    bbd_re = _blockdiag_in(bb_re.reshape(N_GROUPS, SSM_STATE, SSM_GROUP)).astype(BF16)
    bbd_im = _blockdiag_in(bb_im.reshape(N_GROUPS, SSM_STATE, SSM_GROUP)).astype(BF16)
    cbd_re = _blockdiag_out(w["ssm_c_re"].reshape(N_GROUPS, SSM_GROUP, SSM_STATE)).astype(BF16)
    cbd_imn = _blockdiag_out(-w["ssm_c_im"].reshape(N_GROUPS, SSM_GROUP, SSM_STATE)).astype(BF16)
    dskip = w["ssm_d"].reshape(1, D_MODEL)
    (ypre, yg, h_re, h_im), landed = s5_fwd(x, bbd_re, bbd_im, cbd_re, cbd_imn, a_re, a_im, dskip, ride_for(first_ride))
    gathered(first_ride, landed)
    w_glu = w["ssm_w_glu"]
    glu_tile = w_glu.shape[2]
    val, gate, z = glu_proj(yg, w_glu)
    w_out = w["ssm_w_out"].reshape(D_MODEL, D_MODEL)
    ln = lambda name, l: w[name][l].reshape(1, D_MODEL)

    def then_ln(h, names, layer):
        def epi(r, hv, gl, bl):
            y = _layer_norm(hv, r, gl, bl)
            return r, y, y
        return dict(epi=epi, extras=(h, ln(names[0], layer), ln(names[1], layer)), out_dtypes=(F32, F32, BF16))

    mix0, h1, h1b = mm(z, w_out, name="ssm_out", **then_ln(x, ("ln_mix_g", "ln_mix_b"), 0))

    def mlp_fwd(h, hb, layer, riding=None, with_ln=True):
        pre = mm(hb, w["w_ff1"][layer], n_dim=D_FF, tiles=(None, ff_tile, None), b_view=_b_cols, name=f"ff1_{layer}",
                 out_dtypes=(BF16,), ride=ride_for(riding) if riding else None)
        if riding and comm is not None:
            pre, landed = pre
            gathered(riding, landed)
        post = then_ln(h, ("ln_ffn_g", "ln_ffn_b"), layer) if with_ln else {}
        return pre, mm(pre, w["w_ff2"][layer].reshape(D_FF, D_MODEL), pro_a=_relu2, name=f"ff2_{layer}", **post)

    f1pre, (f1, h2, h2b) = mlp_fwd(h1, h1b, 0, mla_ride)

    kv_w_a = w["kv_w_a"].reshape(D_MODEL, KVA_PAD)
    kv_w_b = w["kv_w_b"]
    q_w_a = w["q_w_a"].reshape(D_MODEL, Q_LORA)
    q_w_b = w["q_w_b"]
    w_o = w["attn_w_o"].reshape(D_MODEL, D_MODEL)
    kvb_tile = kv_w_b.shape[2]
    kvn_g = w["kv_norm_g"].reshape(1, KV_LORA)
    qn_g = w["q_norm_g"].reshape(1, Q_LORA)
    def kv_post(kva, g, cs, sn):
        tile = _rope_tile(kva[:, KV_LORA:], cs, sn)
        return kva, _rms(kva[:, :KV_LORA], g), _cat(tile, pltpu.roll(tile, HALF_ROPE, 1))
    kva, ckv, krope = mm(h2b, kv_w_a, epi=kv_post, extras=(kvn_g, cos_k, sin_k),
                         out_dtypes=(F32, (KV_LORA, BF16), (2 * LANES, BF16)), name="kv_a")
    kvb = mm(ckv, kv_w_b, n_dim=N_CHIPS * kvb_tile, tiles=(None, kvb_tile, KV_LORA), b_view=_b_cols, name="kv_b",
             out_dtypes=(BF16,))
    cq_raw, cq = mm(h2b, q_w_a, epi=lambda r, gq: (r, _rms(r, gq)), extras=(qn_g,), out_dtypes=(F32, BF16), name="q_a")

    def rope_and_scale(r, cs, sn):
        return (_cat(r[:, :Q_CHIP_NOPE], _rope_tile(r[:, Q_CHIP_NOPE:], cs, sn)) * Q_PRESCALE,)
    qro = mm(cq, q_w_b, n_dim=N_CHIPS * Q_CHIP, tiles=(None, Q_CHIP, Q_LORA), b_view=_b_cols, epi=rope_and_scale,
             extras=(cos_q, sin_q), out_dtypes=(BF16,), name="q_b")
    (o, lse), landed = attn_fwd(qro, kvb, krope, ride_for(second_ride))
    gathered(second_ride, landed)
    mix1, h3, h3b = mm(o, w_o, name="attn_out", **then_ln(h2, ("ln_mix_g", "ln_mix_b"), 1))
    f2pre, f2 = mlp_fwd(h3, h3b, 1, with_ln=False)
    def last_ln_loss_and_back(h, mix, gl, bl, t):
        e = _layer_norm(h, mix, gl, bl) - t
        dr, dg, db = _layer_norm_bwd(h, mix, gl, e * (1.0 / D_MODEL))
        return (dr, dr), (jnp.broadcast_to(jnp.sum(e * e), (1, LANES)), dg, db)
    dr4, dr4b, loss_acc, dg_f1, db_f1 = rowwise(
        last_ln_loss_and_back, (h3, f2, ln("ln_ffn_g", 1), ln("ln_ffn_b", 1), target),
        ((D_MODEL, F32), (D_MODEL, BF16)), accs=(LANES, D_MODEL, D_MODEL), name="ln_ffn_1_loss", tm=512)
    loss = loss_acc[0, 0] * (0.5 / D_MODEL)

    g = {}

    def into_rows(off, rows_per_chip, shape=pack_shape):
        def view(tm, tn):
            if tm == N_CHIPS * rows_per_chip:
                return pl.BlockSpec((N_CHIPS, rows_per_chip, tn), lambda i, j, k: (0, off // rows_per_chip, 0))
            nb = rows_per_chip // tm
            return pl.BlockSpec((None, tm, tn), lambda i, j, k: (i // nb, off // tm + i % nb, 0))
        return shape, view

    def into_cols(off):
        return pack_shape, lambda tm, tn: pl.BlockSpec((None, tm, tn), lambda i, j, k: (j, off // tm + i, 0))

    def mlp_bwd(pack, dr, drb, hb, pre, layer, swap=False):
        w2_rows = (EARLY_OFF["w_ff2"] + layer * ff_tile, ff_tile)
        w1_rows = (EARLY_OFF["w_ff1"] + layer * D_MODEL, D_MODEL)
        ready = [(w1_rows[0] + w1_rows[1], w2_rows[0] - w1_rows[0] - w1_rows[1]), (w2_rows[0] + w2_rows[1], EARLY_ROWS - w2_rows[0] - w2_rows[1])]
        dpre = mm(drb, w["w_ff2"][layer].reshape(D_FF, D_MODEL), tb=True, epi=lambda r, p: (r * 2.0 * jnp.maximum(p, 0.0),),
                  extras=(pre,), out_dtypes=(BF16,), tiles=(None, ff_tile, None), name=f"ff2_dx_{layer}",
                  ride=SwapRide(pack, ready) if swap else None)
        if swap:
            dpre, (theirs,) = dpre
        pack = mm(pre, drb, ta=True, pro_a=_relu2, name=f"ff2_dw_{layer}", tiles=(ff_tile, PACK_W, None), into=pack,
                  out_view=into_rows(w2_rows[0], ff_tile))
        pack = mm(hb, dpre, ta=True, name=f"ff1_dw_{layer}", tiles=(None, PACK_W, None), into=pack,
                  out_view=into_cols(w1_rows[0]))
        dh = mm(dpre, w["w_ff1"][layer], tb=True, epi=lambda r, d: (r + DN_ALPHA * d,), extras=(dr,), n_dim=D_MODEL,
                tiles=(None, D_MODEL, ff_tile), b_view=_b_cols_t, name=f"ff1_dx_{layer}",
                ride=SwapRide(pack, [w1_rows, w2_rows], into=theirs) if swap else None)
        return (pack, *dh) if swap else (pack, dh)

    pack, dh3 = mlp_bwd(None, dr4, dr4b, h3b, f2pre, 1)
    dr3, dr3b, dg_m1, db_m1 = ln_bwd(h2, mix1, ln("ln_mix_g", 1), dh3, "ln_mix_bwd_1")
    shard_rows = D_MODEL // N_CHIPS
    pack = mm(o, dr3b, ta=True, name="attn_out_dw", tiles=(D_MODEL, PACK_W, None), into=pack,
              out_view=into_rows(EARLY_OFF["attn_w_o"], shard_rows))
    def head_dots(do, o):
        return do, jnp.concatenate([jnp.sum(do[:, V_HEAD * h:V_HEAD * (h + 1)] * o[:, V_HEAD * h:V_HEAD * (h + 1)], axis=1,
                                            keepdims=True) for h in range(N_HEADS)], axis=1)
    do, delta = mm(dr3b, w_o, tb=True, epi=head_dots, extras=(o,), out_dtypes=(F32, (N_HEADS, F32)), name="attn_out_dx")
    tb = min(ATT_TK, seq)
    lse_row = lse.reshape(N_HEADS, seq // tb, tb)
    delta_row = delta.T.reshape(N_HEADS, seq // tb, tb)
    dqn, dqr, dkvb, dkr = attn_bwd(qro, kvb, krope, do, lse_row, delta_row)

    def q_rope_bwd(dn, dr, cs, sn):
        parts = []
        for k in range(N_CHIPS):
            parts.append(dn[:, Q_CHIP_NOPE * k:Q_CHIP_NOPE * (k + 1)])
            parts.append(_rope_tile_bwd(dr[:, LANES * k:LANES * (k + 1)], cs, sn))
        return (jnp.concatenate(parts, axis=1),), ()
    (dqlin,) = rowwise(q_rope_bwd, (dqn, dqr, cos_q, sin_q), ((N_CHIPS * Q_CHIP, BF16),), name="q_rope_bwd", tm=512)
    g["q_w_b"] = mm(cq, dqlin, ta=True, name="q_b_dw", tiles=(Q_LORA, Q_CHIP, None), out_view=_out_cols(q_w_b.shape))
    dcq_raw, dqn_g = mm(dqlin, q_w_b, tb=True, n_dim=Q_LORA, tiles=(None, Q_LORA, Q_CHIP), b_view=_b_cols_t,
                        epi=lambda d, c, gq: _rms_bwd(c, gq, d), extras=(cq_raw, qn_g), out_dtypes=(BF16,),
                        accs=(Q_LORA,), name="q_b_dx")
    g["q_w_a"] = mm(h2b, dcq_raw, ta=True, name="q_a_dw")
    g["kv_w_b"] = mm(ckv, dkvb, ta=True, name="kv_b_dw", tiles=(KV_LORA, kvb_tile, None), out_view=_out_cols(kv_w_b.shape))

    def kv_post_bwd(dc, kva, gk, dk, cs, sn):
        dx, dgk = _rms_bwd(kva[:, :KV_LORA], gk, dc)
        dk = jnp.sum(dk, axis=0)
        dk = dk + pltpu.roll(dk, LANES - HALF_ROPE, 1)
        return jnp.concatenate([dx, _rope_tile_bwd(dk, cs, sn)], axis=1), dgk
    dkva, dkvn_g = mm(dkvb, kv_w_b, tb=True, n_dim=KV_LORA, tiles=(None, KV_LORA, kvb_tile), b_view=_b_cols_t,
                      epi=kv_post_bwd, extras=(kva, kvn_g, dkr, cos_k, sin_k), out_dtypes=((KVA_PAD, BF16),),
                      accs=(KV_LORA,), name="kv_b_dx")
    g["kv_w_a"] = mm(h2b, dkva, ta=True, name="kv_a_dw")

    def ln_ffn_bwd(r, dc, wq, d, h, f, gl):
        via_q = lax.dot_general(dc, wq, (((1,), (1,)), ((), ())), preferred_element_type=F32)
        dr, dg, db = _layer_norm_bwd(h, f, gl, r + (via_q + DN_ALPHA * d))
        return dr, dr, dg, db
    dr2, dr2b, dg_f0, db_f0 = mm(dkva, kv_w_a, tb=True, epi=ln_ffn_bwd,
                                 extras=(dcq_raw, q_w_a.astype(BF16), dr3, h1, f1, ln("ln_ffn_g", 0)),
                                 out_dtypes=(F32, BF16), accs=(D_MODEL, D_MODEL), name="qkv_a_dx")
    pack = put_rows(pack, packed_shards(g, MISC_EARLY, EARLY_ROWS - MISC_EARLY_OFF), MISC_EARLY_OFF)
    if comm is None:
        pack, dh1 = mlp_bwd(pack, dr2, dr2b, h1b, f1pre, 0)
    else:
        pack, dh1, (theirs,) = mlp_bwd(pack, dr2, dr2b, h1b, f1pre, 0, swap=True)
        early_sums = add_halves(pack, theirs, comm[1])
    dr1, dr1b, dg_m0, db_m0 = ln_bwd(x, mix0, ln("ln_mix_g", 0), dh1, "ln_mix_bwd_0")
    mid = mm(z, dr1b, ta=True, name="ssm_out_dw", tiles=(D_MODEL, PACK_W, None),
             out_view=into_rows(MID_OFF["ssm_w_out"], shard_rows, (N_CHIPS, MID_ROWS, PACK_W)))
    def glu_bwd(dz, vl, gt):
        sg = _sigmoid(gt)
        return (jnp.concatenate([dz * sg, dz * vl * sg * (1.0 - sg)], axis=1),)
    dvg = mm(dr1b, w_out, tb=True, epi=glu_bwd, extras=(val, gate), out_dtypes=((2 * D_MODEL, BF16),), name="ssm_out_dx")
    g["ssm_w_glu"] = mm(yg, dvg, ta=True, name="glu_proj_dw", tiles=(None, glu_tile, None), out_view=_out_cols(w_glu.shape))
    mid = put_rows(mid, packed_shards(g, MISC_MID, MID_ROWS - MISC_MID_OFF), MISC_MID_OFF)
    dypre = mm(dvg, w_glu, tb=True, epi=lambda r, y: (r * _gelu_grad(y),), extras=(ypre,), n_dim=D_MODEL,
               tiles=(None, D_MODEL, glu_tile), b_view=_b_cols_t, name="glu_proj_dx",
               ride=Together([SwapRide(mid), SendRide([(early_sums, (0, EARLY_HEAD), None)])]) if comm is not None else None)
    sends = None
    if comm is not None:
        dypre, (theirs, early_got) = dypre
        sends = SendRide([(early_sums, (EARLY_HEAD, EARLY_ROWS - EARLY_HEAD), early_got), add_halves(mid, theirs, comm[1])])
    (dx, dbbd_re, dbbd_im, dcbd_re, dcbd_imn, dar, dai, dd), got = s5_bwd(
        dypre, x, dr1, h_re, h_im, bbd_re, bbd_im, cbd_re, cbd_imn, a_re, a_im, dskip, sends)
    dbb_re = _blockdiag_in_t(dbbd_re).reshape(N_STATES, SSM_GROUP)
    dbb_im = _blockdiag_in_t(dbbd_im).reshape(N_STATES, SSM_GROUP)
    dlr, dli, dldt, db_re, db_im = s5_prep_bwd(lr, li, ldt, b_re, b_im, dar.reshape(N_STATES, 1),
                                               dai.reshape(N_STATES, 1), dbb_re, dbb_im)
    g["ssm_lam_re"] = dlr.reshape(1, N_GROUPS, SSM_STATE)
    g["ssm_lam_im"] = dli.reshape(1, N_GROUPS, SSM_STATE)
    g["ssm_log_dt"] = group_sum(dldt).reshape(1, N_GROUPS)
    g["ssm_b_re"] = db_re.reshape(1, N_GROUPS, SSM_STATE, SSM_GROUP)
    g["ssm_b_im"] = db_im.reshape(1, N_GROUPS, SSM_STATE, SSM_GROUP)
    g["ssm_c_re"] = _blockdiag_out_t(dcbd_re).reshape(1, N_GROUPS, SSM_GROUP, SSM_STATE)
    g["ssm_c_im"] = -_blockdiag_out_t(dcbd_imn).reshape(1, N_GROUPS, SSM_GROUP, SSM_STATE)
    g["ssm_d"] = dd
    g["ln_mix_g"] = jnp.concatenate([dg_m0, dg_m1], 0)
    g["ln_mix_b"] = jnp.concatenate([db_m0, db_m1], 0)
    g["ln_ffn_g"] = jnp.concatenate([dg_f0, dg_f1], 0)
    g["ln_ffn_b"] = jnp.concatenate([db_f0, db_f1], 0)
    g["kv_norm_g"] = dkvn_g.reshape(KV_LORA)
    g["q_norm_g"] = dqn_g
    return loss, dx, pack, mid, g, list(zip(sends.ins, got)) if comm is not None else None


def place(shard, me_idx, dtype, name, layer=None):
    rows, cols = shard.shape[-2:]
    tr = _tile(rows, (512, 256, 128))

    def body(m_ref, x_ref, o_ref):
        o_ref[...] = x_ref[...].astype(o_ref.dtype)

    in_spec = (pl.BlockSpec((tr, cols), lambda i, m: (i, 0)) if layer is None
               else pl.BlockSpec((None, tr, cols), lambda i, m: (layer, i, 0)))
    return _pcall(
        body, name=name,
        grid_spec=pltpu.PrefetchScalarGridSpec(
            num_scalar_prefetch=1, grid=(rows // tr,), in_specs=[in_spec],
            out_specs=pl.BlockSpec((None, tr, cols), lambda i, m: (m[0], i, 0))),
        out_shape=jax.ShapeDtypeStruct((N_CHIPS, rows, cols), dtype),
        compiler_params=_params(("parallel",)),
    )(me_idx, shard)


def place_many(shards, dtypes, me_idx, name):
    def body(m_ref, *refs):
        for x_ref, o_ref in zip(refs[:len(shards)], refs[len(shards):]):
            o_ref[...] = x_ref[...].astype(o_ref.dtype)

    return _pcall(
        body, name=name,
        grid_spec=pltpu.PrefetchScalarGridSpec(
            num_scalar_prefetch=1, grid=(1,),
            in_specs=[pl.BlockSpec(s.shape, lambda i, m: (0, 0)) for s in shards],
            out_specs=[pl.BlockSpec((None,) + s.shape, lambda i, m: (m[0], 0, 0)) for s in shards]),
        out_shape=[jax.ShapeDtypeStruct((N_CHIPS,) + s.shape, d) for s, d in zip(shards, dtypes)],
        compiler_params=_params(("arbitrary",)),
    )(me_idx, *shards)


def put_rows(pack, rows, off):
    _, n, cols = rows.shape

    def body(r_ref, p_ref, o_ref, sem):
        cp = pltpu.make_async_copy(r_ref.at[0], o_ref.at[pl.program_id(0), pl.ds(off, n), :], sem)
        cp.start()
        cp.wait()

    return _pcall(body, name="grad_put_rows", grid=(N_CHIPS,),
                  in_specs=[pl.BlockSpec((1, n, cols), lambda k: (k, 0, 0)), _ANY], out_specs=_ANY,
                  out_shape=jax.ShapeDtypeStruct(pack.shape, pack.dtype), input_output_aliases={1: 0},
                  scratch_shapes=[pltpu.SemaphoreType.DMA],
                  compiler_params=_params(("arbitrary",)))(rows, pack)


def _my_cols(c, mine=True):
    start = (c if mine else 1 - c) * HALF_W
    return pl.ds(pl.multiple_of(start, HALF_W), HALF_W)


def add_halves(gpack, got, c_idx):
    n, rows, _ = gpack.shape
    tr = min(G_BLOCK_ROWS, rows)
    blk = (None, tr, HALF_W)

    def body(c_ref, g_ref, r_ref, o_ref):
        o_ref[...] = (g_ref[...] + r_ref[...]).astype(o_ref.dtype)

    return _pcall(
        body, name="grad_add_halves",
        grid_spec=pltpu.PrefetchScalarGridSpec(
            num_scalar_prefetch=1, grid=(n, rows // tr),
            in_specs=[pl.BlockSpec(blk, lambda k, i, c: (k, i, c[0])), pl.BlockSpec(blk, lambda k, i, c: (k, i, 0))],
            out_specs=pl.BlockSpec(blk, lambda k, i, c: (k, i, 0))),
        out_shape=jax.ShapeDtypeStruct((n, rows, HALF_W), BF16),
        compiler_params=_params(("parallel", "parallel")),
    )(c_idx, gpack, got)


def sum_owner(part, got, idx, total_rows, row_off=0, into=None):
    _, rows, _ = part.shape
    tr = math.gcd(math.gcd(rows, row_off), G_BLOCK_ROWS)
    n_into = 0 if into is None else 1

    def body(m_ref, p_ref, g_ref, *rest):
        up = lambda v: v.astype(F32)
        rest[-1][...] = ((up(p_ref[...]) + up(g_ref[0])) + up(g_ref[1])) + up(g_ref[2])

    return _pcall(
        body, name="grad_sum_owner",
        grid_spec=pltpu.PrefetchScalarGridSpec(
            num_scalar_prefetch=1, grid=(rows // tr,),
            in_specs=[pl.BlockSpec((None, tr, HALF_W), lambda i, m: (m[0], i, 0)),
                      pl.BlockSpec((3, tr, HALF_W), lambda i, m: (0, i, 0))] + [_ANY] * n_into,
            out_specs=pl.BlockSpec((tr, HALF_W), lambda i, m: (row_off // tr + i, m[1]))),
        out_shape=jax.ShapeDtypeStruct((total_rows, PACK_W), F32),
        input_output_aliases={3: 0} if n_into else {},
        compiler_params=_params(("parallel",)),
    )(idx, part, got, *([into] if n_into else []))


def join_halves(red):
    def body(in_ref, out_ref, send_sem, recv_sem):
        x, y, c, _ = _place()
        sibling = (x, y, 1 - c)
        mine = out_ref.at[:, _my_cols(c)]
        cp = pltpu.make_async_remote_copy(src_ref=mine, dst_ref=mine, send_sem=send_sem, recv_sem=recv_sem,
                                          device_id=sibling, device_id_type=MESH)
        cp.start()
        cp.wait_send()
        other = out_ref.at[:, _my_cols(c, mine=False)]
        pltpu.make_async_remote_copy(src_ref=other, dst_ref=other, send_sem=send_sem, recv_sem=recv_sem,
                                     device_id=sibling, device_id_type=MESH).wait_recv()

    return _pcall(body, name="grad_join_halves", in_specs=[_ANY], out_specs=_ANY,
                  out_shape=jax.ShapeDtypeStruct(red.shape, red.dtype), input_output_aliases={0: 0},
                  scratch_shapes=[pltpu.SemaphoreType.DMA, pltpu.SemaphoreType.DMA])(red)


def adamw(gsrc, g_off, wt, m, v, name):
    n, cols = wt.shape
    tr = math.gcd(math.gcd(g_off, n), 256) if g_off else math.gcd(n, 256)
    off_blk = g_off // tr
    c1 = 1.0 / (1.0 - ADAM_B1 ** ADAM_STEP)
    c2 = 1.0 / (1.0 - ADAM_B2 ** ADAM_STEP)

    def body(g_ref, w_ref, m_ref, v_ref, go_ref, d_ref, mo_ref, vo_ref):
        gv = g_ref[...]
        mn = ADAM_B1 * m_ref[...] + (1.0 - ADAM_B1) * gv
        vn = ADAM_B2 * v_ref[...] + (1.0 - ADAM_B2) * gv * gv
        go_ref[...] = gv
        mo_ref[...] = mn
        vo_ref[...] = vn
        d_ref[...] = -ADAM_LR * ((mn * c1) / (jnp.sqrt(vn * c2) + ADAM_EPS) + ADAM_WD * w_ref[...])

    blk = pl.BlockSpec((tr, cols), lambda i: (i, 0))
    return _pcall(body, name=name, grid=(n // tr,),
                  in_specs=[pl.BlockSpec((tr, cols), lambda i: (off_blk + i, 0)), blk, blk, blk],
                  out_specs=[blk] * 4, out_shape=[jax.ShapeDtypeStruct((n, cols), F32)] * 4,
                  compiler_params=_params(("parallel",)))(gsrc, wt, m, v)


def _rows8(a):
    return -(-a.size // (8 * PACK_W)) * 8


def _as_rows(a, rows=None):
    flat = a.reshape(-1)
    n = _rows8(a) if rows is None else rows
    return jnp.pad(flat, (0, n * PACK_W - flat.shape[0])).reshape(n, PACK_W)


def local_shards_2d(wl):
    return {"w_ff1": [wl["w_ff1"][0], wl["w_ff1"][1]], "w_ff2": [wl["w_ff2"][0], wl["w_ff2"][1]],
            "ssm_w_glu": wl["ssm_w_glu"], "ssm_w_out": wl["ssm_w_out"], "kv_w_a": _pad_kva_cols(wl["kv_w_a"]),
            "kv_w_b": wl["kv_w_b"], "q_w_a": wl["q_w_a"], "q_w_b": _perm_q_cols(wl["q_w_b"]),
            "attn_w_o": wl["attn_w_o"], "ssm_d": wl["ssm_d"].reshape(2, -1)}


def misc_grad_shard(name, g, k):
    if name == "ssm_d":
        w = D_MODEL // N_CHIPS
        return g[:, w * k:w * (k + 1)]
    if name in ("ssm_w_glu", "kv_w_b"):
        return g[k]
    if name == "q_w_b":
        return _unperm_q_cols(g[k])
    rows = D_MODEL // N_CHIPS
    shard = g[rows * k:rows * (k + 1)]
    return _unpad_kva_cols(shard) if name == "kv_w_a" else shard


def packed_shards(g, names, rows, tail=None):
    blocks = []
    for k in range(N_CHIPS):
        parts = [_as_rows(misc_grad_shard(n, g[n], k), MISC_SHARD_ROWS[n]) for n in names]
        if tail is not None:
            parts.append(tail[k * (tail.shape[0] // N_CHIPS):(k + 1) * (tail.shape[0] // N_CHIPS)])
        blk = jnp.concatenate(parts, axis=0)
        blocks.append(jnp.pad(blk, ((0, rows - blk.shape[0]), (0, 0))))
    return jnp.stack(blocks)


def kernel(x, positions, ln_mix_g, ln_mix_b, ln_ffn_g, ln_ffn_b, w_ff1, w_ff2, ssm_lam_re, ssm_lam_im, ssm_log_dt, ssm_b_re, ssm_b_im, ssm_c_re, ssm_c_im, ssm_d, ssm_w_glu, ssm_w_out, kv_w_a, kv_norm_g, kv_w_b, q_w_a, q_norm_g, q_w_b, attn_w_o, loss_target, m_ln_mix_g, m_ln_mix_b, m_ln_ffn_g, m_ln_ffn_b, m_w_ff1, m_w_ff2, m_ssm_lam_re, m_ssm_lam_im, m_ssm_log_dt, m_ssm_b_re, m_ssm_b_im, m_ssm_c_re, m_ssm_c_im, m_ssm_d, m_ssm_w_glu, m_ssm_w_out, m_kv_w_a, m_kv_norm_g, m_kv_w_b, m_q_w_a, m_q_norm_g, m_q_w_b, m_attn_w_o, v_ln_mix_g, v_ln_mix_b, v_ln_ffn_g, v_ln_ffn_b, v_w_ff1, v_w_ff2, v_ssm_lam_re, v_ssm_lam_im, v_ssm_log_dt, v_ssm_b_re, v_ssm_b_im, v_ssm_c_re, v_ssm_c_im, v_ssm_d, v_ssm_w_glu, v_ssm_w_out, v_kv_w_a, v_kv_norm_g, v_kv_w_b, v_q_w_a, v_q_norm_g, v_q_w_b, v_attn_w_o):
    env = dict(locals())
    wl = {n: env[n] for n in WEIGHTS}
    ml = {n: env["m_" + n] for n in WEIGHTS}
    vl = {n: env["v_" + n] for n in WEIGHTS}
    for n in ("ssm_w_glu", "ssm_w_out", "q_w_a", "q_w_b", "attn_w_o"):
        wl[n], ml[n], vl[n] = wl[n][0], ml[n][0], vl[n][0]

    c_idx = lax.axis_index("c").astype(jnp.int32).reshape(1)
    me_idx = (2 * lax.axis_index("x") + lax.axis_index("y")).astype(jnp.int32).reshape(1)

    local = local_shards_2d(wl)
    stacked = {n: [place(wl[n], me_idx, BF16, f"place_{n}_{l}", layer=l) for l in range(DEPTH)] for n in ("w_ff1", "w_ff2")}
    others = [n for n in SHARDED if n not in stacked]
    stacked.update(zip(others, place_many([local[n] for n in others], [F32 if n == "ssm_d" else BF16 for n in others],
                                          me_idx, "place_others")))
    stacked["ssm_d"] = ride_alone(GatherRide([_halves(stacked["ssm_d"])]), "ssm_d_all_gather")[0].reshape(1, D_MODEL)
    for n in REPLICATED:
        stacked[n] = wl[n]

    loss_part, dx, early, mid, g, sent = device_step(x[0], positions[0], loss_target[0], stacked, comm=(me_idx, c_idx))
    loss = lax.psum(loss_part, ("x", "y", "c"))

    small = jnp.concatenate([_as_rows(g[n]) for n in REPLICATED], axis=0)
    small = jnp.pad(small, ((0, SMALL_ROWS - small.shape[0]), (0, 0)))
    late = packed_shards(g, MISC_LATE, LATE_ROWS, tail=small)
    late_sums = add_halves(late, ride_alone(SwapRide(late), "grad_swap_halves")[0], c_idx)
    sent.append((late_sums, ride_alone(SendRide([late_sums]), "grad_send_to_owners")[0]))
    where = jnp.concatenate([me_idx, c_idx])
    starts = (0, EARLY_ROWS, EARLY_ROWS + MID_ROWS)
    total_rows = EARLY_ROWS + MID_ROWS + LATE_ROWS
    reduced = None
    for (sums, got), off in zip(sent, starts):
        reduced = sum_owner(sums, got, where, total_rows, row_off=off, into=reduced)
    reduced = join_halves(reduced)
    quarter = reduced[starts[2] + SMALL_OFF:starts[2] + SMALL_OFF + SMALL_Q_ROWS]
    small_tot = ride_alone(GatherRide([_halves(place(quarter, me_idx, F32, "place_small_grads"))]),
                           "small_grad_all_gather")[0].reshape(SMALL_ROWS, PACK_W)

    out_g, out_d, out_m, out_v = {}, {}, {}, {}
    direct = {**EARLY_OFF, **{n: starts[1] + o for n, o in MID_OFF.items()}}
    for n, off in direct.items():
        res = adamw(reduced, off, wl[n].reshape(-1, PACK_W), ml[n].reshape(-1, PACK_W), vl[n].reshape(-1, PACK_W),
                    "adamw_" + n)
        out_g[n], out_d[n], out_m[n], out_v[n] = [a.reshape(env[n].shape) for a in res]
    for names, off in ((MISC_EARLY, MISC_EARLY_OFF), (MISC_MID, starts[1] + MISC_MID_OFF), (MISC_LATE, starts[2])):
        pack3 = lambda d: jnp.concatenate([_as_rows(d[n], MISC_SHARD_ROWS[n]) for n in names], axis=0)
        res = adamw(reduced, off, pack3(wl), pack3(ml), pack3(vl), "adamw_packed_" + names[0])
        r0 = 0
        for n in names:
            cnt = math.prod(env[n].shape)
            out_g[n], out_d[n], out_m[n], out_v[n] = [
                a[r0:r0 + MISC_SHARD_ROWS[n]].reshape(-1)[:cnt].reshape(env[n].shape) for a in res]
            r0 += MISC_SHARD_ROWS[n]
    ws = jnp.concatenate([_as_rows(wl[n]) for n in REPLICATED], axis=0)
    ms = jnp.concatenate([_as_rows(ml[n]) for n in REPLICATED], axis=0)
    vs = jnp.concatenate([_as_rows(vl[n]) for n in REPLICATED], axis=0)
    pad = ((0, SMALL_ROWS - ws.shape[0]), (0, 0))
    res = adamw(small_tot, 0, jnp.pad(ws, pad), jnp.pad(ms, pad), jnp.pad(vs, pad), "adamw_replicated")
    row = 0
    for n in REPLICATED:
        cnt = math.prod(env[n].shape)
        nrows = _rows8(env[n])
        out_g[n], out_d[n], out_m[n], out_v[n] = [a[row:row + nrows].reshape(-1)[:cnt].reshape(env[n].shape) for a in res]
        row += nrows

    return (loss, dx[None], *[out_g[n] for n in WEIGHTS], *[out_d[n] for n in WEIGHTS],
            *[out_m[n] for n in WEIGHTS], *[out_v[n] for n in WEIGHTS])
```
